```python
import jax, jax.numpy as jnp
from jax import lax
import numpy as np

D_MODEL = 1024
BATCH = 8
SEQ = 4096
DEPTH = 1

POOL_WINDOWS = (2, 4, 8, 16)
N_POOL_GROUPS = len(POOL_WINDOWS)
D_POOL = D_MODEL // 2
POOL_GROUP = D_POOL // N_POOL_GROUPS
D_RNN = D_MODEL
N_RNN_HEADS = 8
RNN_HEAD = D_RNN // N_RNN_HEADS
CONV_WIDTH = 4
LRU_C = 8.0
N_BRANCHES = 2
D_IN = D_POOL + 2 * D_RNN + N_BRANCHES * D_MODEL
D_FF = -(-8 * D_MODEL // (3 * 256)) * 256
NORM_EPS = 1e-6

kernel_name = "hybrid_pool_rglru_gated_block"


def rmsnorm(x, g):
    xf = x.astype(jnp.float32)
    y = xf * lax.rsqrt(jnp.mean(xf * xf, axis=-1, keepdims=True) + NORM_EPS)
    return (y * g.astype(jnp.float32)).astype(x.dtype)


def pool_mixer(u, w_grp, scale):
    B, S, _ = u.shape
    uf = u.astype(jnp.float32).reshape(B, S, N_POOL_GROUPS, POOL_GROUP)
    c = jnp.cumsum(uf, axis=1)
    pos = jnp.arange(S)
    outs = []
    for g, w in enumerate(POOL_WINDOWS):
        cg = c[:, :, g]
        c_lo = jnp.pad(cg[:, : S - w], ((0, 0), (w, 0), (0, 0)))
        count = jnp.minimum(pos + 1, w).astype(jnp.float32)[None, :, None]
        outs.append((cg - c_lo) / count - uf[:, :, g])
    pooled = jnp.stack(outs, axis=2).astype(u.dtype)
    mixed = jnp.einsum("bsgc,gcd->bsgd", pooled, w_grp)
    return mixed.reshape(B, S, D_POOL) * scale


def causal_depthwise_conv(u, w, b):
    S = u.shape[1]
    up = jnp.pad(u, ((0, 0), (CONV_WIDTH - 1, 0), (0, 0)))
    y = b
    for k in range(CONV_WIDTH):
        y = y + up[:, k : k + S] * w[k]
    return y


def rg_lru(v, w_a, b_a, w_x, b_x, lam):
    B, S, _ = v.shape
    vh = v.reshape(B, S, N_RNN_HEADS, RNN_HEAD)
    r = jax.nn.sigmoid((jnp.einsum("bshi,hij->bshj", vh, w_a) + b_a).astype(jnp.float32)).reshape(B, S, D_RNN)
    i = jax.nn.sigmoid((jnp.einsum("bshi,hij->bshj", vh, w_x) + b_x).astype(jnp.float32)).reshape(B, S, D_RNN)
    log_a = -LRU_C * r * jax.nn.softplus(-lam.astype(jnp.float32))
    a = jnp.exp(log_a)
    b = jnp.sqrt(-jnp.expm1(2.0 * log_a)) * i * v.astype(jnp.float32)

    def combine(left, right):
        a1, b1 = left
        a2, b2 = right
        return a1 * a2, a2 * b1 + b2

    _, h = lax.associative_scan(combine, (a, b), axis=1)
    return h.astype(v.dtype)


def _fwd_setup_inputs(seed: int = 0) -> dict:
    key = jax.random.key(seed)
    ks = jax.random.split(key, 22)
    f32 = jnp.float32
    L = DEPTH

    def nrm(k, shape, fan_in):
        return jax.random.normal(k, shape, f32) * fan_in ** -0.5

    def small(k, shape, s=0.02):
        return jax.random.normal(k, shape, f32) * s

    x = jax.random.normal(ks[0], (BATCH, SEQ, D_MODEL), f32)
    norm_mix = 1.0 + small(ks[1], (L, D_MODEL))
    w_in = nrm(ks[2], (L, D_MODEL, D_IN), D_MODEL)
    w_pool_grp = nrm(ks[3], (L, N_POOL_GROUPS, POOL_GROUP, POOL_GROUP), POOL_GROUP)
    pool_scale = 1.0 + small(ks[4], (L, D_POOL))
    w_pool_out = nrm(ks[5], (L, D_POOL, D_MODEL), D_POOL)
    conv_w = nrm(ks[6], (L, CONV_WIDTH, D_RNN), CONV_WIDTH)
    conv_b = small(ks[7], (L, D_RNN))
    w_rg_a = nrm(ks[8], (L, N_RNN_HEADS, RNN_HEAD, RNN_HEAD), RNN_HEAD)
    b_rg_a = small(ks[9], (L, N_RNN_HEADS, RNN_HEAD))
    w_rg_x = nrm(ks[10], (L, N_RNN_HEADS, RNN_HEAD, RNN_HEAD), RNN_HEAD)
    b_rg_x = small(ks[11], (L, N_RNN_HEADS, RNN_HEAD))
    a_c = jax.random.uniform(ks[12], (L, D_RNN), f32, minval=0.9, maxval=0.999)
    a0 = a_c ** (1.0 / LRU_C)
    lru_lambda = jnp.log(a0) - jnp.log1p(-a0)
    w_rnn_out = nrm(ks[13], (L, D_RNN, D_MODEL), D_RNN)
    w_o = nrm(ks[14], (L, D_MODEL, D_MODEL), D_MODEL)
    norm_ffn = 1.0 + small(ks[15], (L, D_MODEL))
    w_ffn_in = nrm(ks[16], (L, D_MODEL, 2 * D_FF), D_MODEL)
    w_ffn_out = nrm(ks[17], (L, D_FF, D_MODEL), D_FF)
    norm_final = 1.0 + small(ks[18], (D_MODEL,))
    return {"x": x, "norm_mix": norm_mix, "w_in": w_in, "w_pool_grp": w_pool_grp,
            "pool_scale": pool_scale, "w_pool_out": w_pool_out, "conv_w": conv_w, "conv_b": conv_b,
            "w_rg_a": w_rg_a, "b_rg_a": b_rg_a, "w_rg_x": w_rg_x, "b_rg_x": b_rg_x,
            "lru_lambda": lru_lambda, "w_rnn_out": w_rnn_out, "w_o": w_o, "norm_ffn": norm_ffn,
            "w_ffn_in": w_ffn_in, "w_ffn_out": w_ffn_out, "norm_final": norm_final}


def _fwd_reference(x, norm_mix, w_in, w_pool_grp, pool_scale, w_pool_out, conv_w, conv_b,
              w_rg_a, b_rg_a, w_rg_x, b_rg_x, lru_lambda, w_rnn_out, w_o, norm_ffn,
              w_ffn_in, w_ffn_out, norm_final):
    B, S, _ = x.shape
    for l in range(DEPTH):
        h = rmsnorm(x, norm_mix[l])
        proj = h @ w_in[l]
        o1 = D_POOL
        o2 = o1 + D_RNN
        o3 = o2 + D_RNN
        u_pool = proj[..., :o1]
        u_rnn = proj[..., o1:o2]
        u_gate = proj[..., o2:o3]
        g_merge = jax.nn.sigmoid(proj[..., o3:].reshape(B, S, N_BRANCHES, D_MODEL))

        y_pool = pool_mixer(u_pool, w_pool_grp[l], pool_scale[l]) @ w_pool_out[l]

        v = causal_depthwise_conv(u_rnn, conv_w[l], conv_b[l])
        hr = rg_lru(v, w_rg_a[l], b_rg_a[l], w_rg_x[l], b_rg_x[l], lru_lambda[l])
        y_rnn = (hr * jax.nn.gelu(u_gate)) @ w_rnn_out[l]

        mix = g_merge[:, :, 0] * y_pool + g_merge[:, :, 1] * y_rnn
        x = x + mix @ w_o[l]

        h = rmsnorm(x, norm_ffn[l])
        gu = h @ w_ffn_in[l]
        gate, up = gu[..., :D_FF], gu[..., D_FF:]
        x = x + (jax.nn.silu(gate) * up) @ w_ffn_out[l]
    return rmsnorm(x, norm_final)


import jax as _jax
import jax.numpy as _jnp

TWIN_FORMAT = 'train_step'
FWD_PARAMS = ['x', 'norm_mix', 'w_in', 'w_pool_grp', 'pool_scale', 'w_pool_out', 'conv_w', 'conv_b', 'w_rg_a', 'b_rg_a', 'w_rg_x', 'b_rg_x', 'lru_lambda', 'w_rnn_out', 'w_o', 'norm_ffn', 'w_ffn_in', 'w_ffn_out', 'norm_final']
TWIN_WEIGHTS = ['norm_mix', 'w_in', 'w_pool_grp', 'pool_scale', 'w_pool_out', 'conv_w', 'conv_b', 'w_rg_a', 'b_rg_a', 'w_rg_x', 'b_rg_x', 'lru_lambda', 'w_rnn_out', 'w_o', 'norm_ffn', 'w_ffn_in', 'w_ffn_out', 'norm_final']
TWIN_DIFF_INPUT = 'x'
TWIN_INPUTS = ['x', 'norm_mix', 'w_in', 'w_pool_grp', 'pool_scale', 'w_pool_out', 'conv_w', 'conv_b', 'w_rg_a', 'b_rg_a', 'w_rg_x', 'b_rg_x', 'lru_lambda', 'w_rnn_out', 'w_o', 'norm_ffn', 'w_ffn_in', 'w_ffn_out', 'norm_final', 'loss_target', 'm_norm_mix', 'm_w_in', 'm_w_pool_grp', 'm_pool_scale', 'm_w_pool_out', 'm_conv_w', 'm_conv_b', 'm_w_rg_a', 'm_b_rg_a', 'm_w_rg_x', 'm_b_rg_x', 'm_lru_lambda', 'm_w_rnn_out', 'm_w_o', 'm_norm_ffn', 'm_w_ffn_in', 'm_w_ffn_out', 'm_norm_final', 'v_norm_mix', 'v_w_in', 'v_w_pool_grp', 'v_pool_scale', 'v_w_pool_out', 'v_conv_w', 'v_conv_b', 'v_w_rg_a', 'v_b_rg_a', 'v_w_rg_x', 'v_b_rg_x', 'v_lru_lambda', 'v_w_rnn_out', 'v_w_o', 'v_norm_ffn', 'v_w_ffn_in', 'v_w_ffn_out', 'v_norm_final']
TWIN_OUTPUTS = ['loss', 'grad_x', 'grad_norm_mix', 'grad_w_in', 'grad_w_pool_grp', 'grad_pool_scale', 'grad_w_pool_out', 'grad_conv_w', 'grad_conv_b', 'grad_w_rg_a', 'grad_b_rg_a', 'grad_w_rg_x', 'grad_b_rg_x', 'grad_lru_lambda', 'grad_w_rnn_out', 'grad_w_o', 'grad_norm_ffn', 'grad_w_ffn_in', 'grad_w_ffn_out', 'grad_norm_final', 'delta_norm_mix', 'delta_w_in', 'delta_w_pool_grp', 'delta_pool_scale', 'delta_w_pool_out', 'delta_conv_w', 'delta_conv_b', 'delta_w_rg_a', 'delta_b_rg_a', 'delta_w_rg_x', 'delta_b_rg_x', 'delta_lru_lambda', 'delta_w_rnn_out', 'delta_w_o', 'delta_norm_ffn', 'delta_w_ffn_in', 'delta_w_ffn_out', 'delta_norm_final', 'new_m_norm_mix', 'new_m_w_in', 'new_m_w_pool_grp', 'new_m_pool_scale', 'new_m_w_pool_out', 'new_m_conv_w', 'new_m_conv_b', 'new_m_w_rg_a', 'new_m_b_rg_a', 'new_m_w_rg_x', 'new_m_b_rg_x', 'new_m_lru_lambda', 'new_m_w_rnn_out', 'new_m_w_o', 'new_m_norm_ffn', 'new_m_w_ffn_in', 'new_m_w_ffn_out', 'new_m_norm_final', 'new_v_norm_mix', 'new_v_w_in', 'new_v_w_pool_grp', 'new_v_pool_scale', 'new_v_w_pool_out', 'new_v_conv_w', 'new_v_conv_b', 'new_v_w_rg_a', 'new_v_b_rg_a', 'new_v_w_rg_x', 'new_v_b_rg_x', 'new_v_lru_lambda', 'new_v_w_rnn_out', 'new_v_w_o', 'new_v_norm_ffn', 'new_v_w_ffn_in', 'new_v_w_ffn_out', 'new_v_norm_final']
TWIN_LEAF_KINDS = {'loss': 'loss', 'grad_x': 'grad_x', 'grad_norm_mix': 'grad_w', 'grad_w_in': 'grad_w', 'grad_w_pool_grp': 'grad_w', 'grad_pool_scale': 'grad_w', 'grad_w_pool_out': 'grad_w', 'grad_conv_w': 'grad_w', 'grad_conv_b': 'grad_w', 'grad_w_rg_a': 'grad_w', 'grad_b_rg_a': 'grad_w', 'grad_w_rg_x': 'grad_w', 'grad_b_rg_x': 'grad_w', 'grad_lru_lambda': 'grad_w', 'grad_w_rnn_out': 'grad_w', 'grad_w_o': 'grad_w', 'grad_norm_ffn': 'grad_w', 'grad_w_ffn_in': 'grad_w', 'grad_w_ffn_out': 'grad_w', 'grad_norm_final': 'grad_w', 'delta_norm_mix': 'delta_w', 'delta_w_in': 'delta_w', 'delta_w_pool_grp': 'delta_w', 'delta_pool_scale': 'delta_w', 'delta_w_pool_out': 'delta_w', 'delta_conv_w': 'delta_w', 'delta_conv_b': 'delta_w', 'delta_w_rg_a': 'delta_w', 'delta_b_rg_a': 'delta_w', 'delta_w_rg_x': 'delta_w', 'delta_b_rg_x': 'delta_w', 'delta_lru_lambda': 'delta_w', 'delta_w_rnn_out': 'delta_w', 'delta_w_o': 'delta_w', 'delta_norm_ffn': 'delta_w', 'delta_w_ffn_in': 'delta_w', 'delta_w_ffn_out': 'delta_w', 'delta_norm_final': 'delta_w', 'new_m_norm_mix': 'new_m', 'new_m_w_in': 'new_m', 'new_m_w_pool_grp': 'new_m', 'new_m_pool_scale': 'new_m', 'new_m_w_pool_out': 'new_m', 'new_m_conv_w': 'new_m', 'new_m_conv_b': 'new_m', 'new_m_w_rg_a': 'new_m', 'new_m_b_rg_a': 'new_m', 'new_m_w_rg_x': 'new_m', 'new_m_b_rg_x': 'new_m', 'new_m_lru_lambda': 'new_m', 'new_m_w_rnn_out': 'new_m', 'new_m_w_o': 'new_m', 'new_m_norm_ffn': 'new_m', 'new_m_w_ffn_in': 'new_m', 'new_m_w_ffn_out': 'new_m', 'new_m_norm_final': 'new_m', 'new_v_norm_mix': 'new_v', 'new_v_w_in': 'new_v', 'new_v_w_pool_grp': 'new_v', 'new_v_pool_scale': 'new_v', 'new_v_w_pool_out': 'new_v', 'new_v_conv_w': 'new_v', 'new_v_conv_b': 'new_v', 'new_v_w_rg_a': 'new_v', 'new_v_b_rg_a': 'new_v', 'new_v_w_rg_x': 'new_v', 'new_v_b_rg_x': 'new_v', 'new_v_lru_lambda': 'new_v', 'new_v_w_rnn_out': 'new_v', 'new_v_w_o': 'new_v', 'new_v_norm_ffn': 'new_v', 'new_v_w_ffn_in': 'new_v', 'new_v_w_ffn_out': 'new_v', 'new_v_norm_final': 'new_v'}


def _forward(args):
    return _fwd_reference(*[args[k] for k in FWD_PARAMS])


def _output_shape():
    def fwd():
        inp = _fwd_setup_inputs(0)
        return _fwd_reference(*[inp[k] for k in FWD_PARAMS])
    out = _jax.eval_shape(fwd)
    return out.shape, out.dtype

N_MICROBATCH = 1
ADAM_LR = 0.001
ADAM_B1 = 0.9
ADAM_B2 = 0.999
ADAM_EPS = 1e-08
ADAM_WD = 0.01
ADAM_STEP = 10
PER_EXAMPLE_BATCH_AXIS = {'x': 0, 'loss_target': 0}
SHARED_INPUTS = []
_WEIGHT_DTYPES = {'norm_mix': _jnp.float32, 'w_in': _jnp.float32, 'w_pool_grp': _jnp.float32, 'pool_scale': _jnp.float32, 'w_pool_out': _jnp.float32, 'conv_w': _jnp.float32, 'conv_b': _jnp.float32, 'w_rg_a': _jnp.float32, 'b_rg_a': _jnp.float32, 'w_rg_x': _jnp.float32, 'b_rg_x': _jnp.float32, 'lru_lambda': _jnp.float32, 'w_rnn_out': _jnp.float32, 'w_o': _jnp.float32, 'norm_ffn': _jnp.float32, 'w_ffn_in': _jnp.float32, 'w_ffn_out': _jnp.float32, 'norm_final': _jnp.float32}
MOMENT_SCALE = {'norm_mix': 1.083263e-01, 'w_in': 5.018704e-02, 'w_pool_grp': 1.192621e-01, 'pool_scale': 1.311352e-01, 'w_pool_out': 8.291022e-02, 'conv_w': 4.258505e-02, 'conv_b': 4.783169e-01, 'w_rg_a': 1.351920e-02, 'b_rg_a': 9.868131e-03, 'w_rg_x': 2.418329e-02, 'b_rg_x': 1.343083e-02, 'lru_lambda': 1.881815e-02, 'w_rnn_out': 4.008137e-02, 'w_o': 9.202318e-02, 'norm_ffn': 1.236777e-01, 'w_ffn_in': 5.095568e-02, 'w_ffn_out': 8.306365e-02, 'norm_final': 3.200607e+01}


def _to_microbatches(a, axis):
    t = _jnp.moveaxis(a, axis, 0)
    t = t.reshape((N_MICROBATCH, t.shape[0] // N_MICROBATCH) + t.shape[1:])
    return _jnp.moveaxis(t, 1, axis + 1)


def setup_inputs(seed: int = 0) -> dict:
    inp = _fwd_setup_inputs(seed)
    key = _jax.random.fold_in(_jax.random.key(seed), 7919)
    shape, _ = _output_shape()
    out = dict(inp)
    out["loss_target"] = _jax.random.normal(_jax.random.fold_in(key, 0), shape, _jnp.float32)
    for i, name in enumerate(TWIN_WEIGHTS):
        w = inp[name].astype(_jnp.float32)
        if MOMENT_SCALE is None:
            s = _jnp.sqrt(_jnp.mean(_jnp.square(w)) + 1e-30)
        else:
            s = MOMENT_SCALE[name]
        km, kv = _jax.random.split(_jax.random.fold_in(key, i + 1))
        out[name] = w
        out["m_" + name] = s * _jax.random.normal(km, w.shape, _jnp.float32)
        out["v_" + name] = (s * s) * _jax.random.uniform(kv, w.shape, _jnp.float32, 0.5, 1.5)
    if N_MICROBATCH > 1:
        for name, axis in PER_EXAMPLE_BATCH_AXIS.items():
            out[name] = _to_microbatches(out[name], axis)
    return {'x': out['x'], 'norm_mix': out['norm_mix'], 'w_in': out['w_in'], 'w_pool_grp': out['w_pool_grp'], 'pool_scale': out['pool_scale'], 'w_pool_out': out['w_pool_out'], 'conv_w': out['conv_w'], 'conv_b': out['conv_b'], 'w_rg_a': out['w_rg_a'], 'b_rg_a': out['b_rg_a'], 'w_rg_x': out['w_rg_x'], 'b_rg_x': out['b_rg_x'], 'lru_lambda': out['lru_lambda'], 'w_rnn_out': out['w_rnn_out'], 'w_o': out['w_o'], 'norm_ffn': out['norm_ffn'], 'w_ffn_in': out['w_ffn_in'], 'w_ffn_out': out['w_ffn_out'], 'norm_final': out['norm_final'], 'loss_target': out['loss_target'], 'm_norm_mix': out['m_norm_mix'], 'm_w_in': out['m_w_in'], 'm_w_pool_grp': out['m_w_pool_grp'], 'm_pool_scale': out['m_pool_scale'], 'm_w_pool_out': out['m_w_pool_out'], 'm_conv_w': out['m_conv_w'], 'm_conv_b': out['m_conv_b'], 'm_w_rg_a': out['m_w_rg_a'], 'm_b_rg_a': out['m_b_rg_a'], 'm_w_rg_x': out['m_w_rg_x'], 'm_b_rg_x': out['m_b_rg_x'], 'm_lru_lambda': out['m_lru_lambda'], 'm_w_rnn_out': out['m_w_rnn_out'], 'm_w_o': out['m_w_o'], 'm_norm_ffn': out['m_norm_ffn'], 'm_w_ffn_in': out['m_w_ffn_in'], 'm_w_ffn_out': out['m_w_ffn_out'], 'm_norm_final': out['m_norm_final'], 'v_norm_mix': out['v_norm_mix'], 'v_w_in': out['v_w_in'], 'v_w_pool_grp': out['v_w_pool_grp'], 'v_pool_scale': out['v_pool_scale'], 'v_w_pool_out': out['v_w_pool_out'], 'v_conv_w': out['v_conv_w'], 'v_conv_b': out['v_conv_b'], 'v_w_rg_a': out['v_w_rg_a'], 'v_b_rg_a': out['v_b_rg_a'], 'v_w_rg_x': out['v_w_rg_x'], 'v_b_rg_x': out['v_b_rg_x'], 'v_lru_lambda': out['v_lru_lambda'], 'v_w_rnn_out': out['v_w_rnn_out'], 'v_w_o': out['v_w_o'], 'v_norm_ffn': out['v_norm_ffn'], 'v_w_ffn_in': out['v_w_ffn_in'], 'v_w_ffn_out': out['v_w_ffn_out'], 'v_norm_final': out['v_norm_final']}


def _loss(weights, diff, rest, loss_target):
    with _jax.named_scope("forward"):
        args = {**rest, TWIN_DIFF_INPUT: diff, **{k: w.astype(_WEIGHT_DTYPES[k]) for k, w in weights.items()}}
        y = _forward(args)
    with _jax.named_scope("loss_head"):
        err = _jnp.square(y.astype(_jnp.float32) - loss_target)
        return 0.5 * _jnp.sum(_jnp.mean(err, axis=-1)) if err.ndim else 0.5 * err


def _adamw(w, g, m, v):
    m = ADAM_B1 * m + (1.0 - ADAM_B1) * g
    v = ADAM_B2 * v + (1.0 - ADAM_B2) * _jnp.square(g)
    m_hat = m / (1.0 - ADAM_B1 ** ADAM_STEP)
    v_hat = v / (1.0 - ADAM_B2 ** ADAM_STEP)
    delta = -ADAM_LR * (m_hat / (_jnp.sqrt(v_hat) + ADAM_EPS) + ADAM_WD * w)
    return delta, m, v


def reference(x, norm_mix, w_in, w_pool_grp, pool_scale, w_pool_out, conv_w, conv_b, w_rg_a, b_rg_a, w_rg_x, b_rg_x, lru_lambda, w_rnn_out, w_o, norm_ffn, w_ffn_in, w_ffn_out, norm_final, loss_target, m_norm_mix, m_w_in, m_w_pool_grp, m_pool_scale, m_w_pool_out, m_conv_w, m_conv_b, m_w_rg_a, m_b_rg_a, m_w_rg_x, m_b_rg_x, m_lru_lambda, m_w_rnn_out, m_w_o, m_norm_ffn, m_w_ffn_in, m_w_ffn_out, m_norm_final, v_norm_mix, v_w_in, v_w_pool_grp, v_pool_scale, v_w_pool_out, v_conv_w, v_conv_b, v_w_rg_a, v_b_rg_a, v_w_rg_x, v_b_rg_x, v_lru_lambda, v_w_rnn_out, v_w_o, v_norm_ffn, v_w_ffn_in, v_w_ffn_out, v_norm_final):
    given = dict(x=x, norm_mix=norm_mix, w_in=w_in, w_pool_grp=w_pool_grp, pool_scale=pool_scale, w_pool_out=w_pool_out, conv_w=conv_w, conv_b=conv_b, w_rg_a=w_rg_a, b_rg_a=b_rg_a, w_rg_x=w_rg_x, b_rg_x=b_rg_x, lru_lambda=lru_lambda, w_rnn_out=w_rnn_out, w_o=w_o, norm_ffn=norm_ffn, w_ffn_in=w_ffn_in, w_ffn_out=w_ffn_out, norm_final=norm_final, loss_target=loss_target, m_norm_mix=m_norm_mix, m_w_in=m_w_in, m_w_pool_grp=m_w_pool_grp, m_pool_scale=m_pool_scale, m_w_pool_out=m_w_pool_out, m_conv_w=m_conv_w, m_conv_b=m_conv_b, m_w_rg_a=m_w_rg_a, m_b_rg_a=m_b_rg_a, m_w_rg_x=m_w_rg_x, m_b_rg_x=m_b_rg_x, m_lru_lambda=m_lru_lambda, m_w_rnn_out=m_w_rnn_out, m_w_o=m_w_o, m_norm_ffn=m_norm_ffn, m_w_ffn_in=m_w_ffn_in, m_w_ffn_out=m_w_ffn_out, m_norm_final=m_norm_final, v_norm_mix=v_norm_mix, v_w_in=v_w_in, v_w_pool_grp=v_w_pool_grp, v_pool_scale=v_pool_scale, v_w_pool_out=v_w_pool_out, v_conv_w=v_conv_w, v_conv_b=v_conv_b, v_w_rg_a=v_w_rg_a, v_b_rg_a=v_b_rg_a, v_w_rg_x=v_w_rg_x, v_b_rg_x=v_b_rg_x, v_lru_lambda=v_lru_lambda, v_w_rnn_out=v_w_rnn_out, v_w_o=v_w_o, v_norm_ffn=v_norm_ffn, v_w_ffn_in=v_w_ffn_in, v_w_ffn_out=v_w_ffn_out, v_norm_final=v_norm_final)
    weights = {n: given[n] for n in TWIN_WEIGHTS}
    shared = {n: given[n] for n in SHARED_INPUTS}
    per_example = {n: given[n] for n in ['x']}
    grad_fn = _jax.value_and_grad(_loss, argnums=(0, 1))

    def one_microbatch(ex, loss_target):
        ex = dict(ex)
        diff = ex.pop(TWIN_DIFF_INPUT)
        return grad_fn(weights, diff, {**shared, **ex}, loss_target)

    if N_MICROBATCH == 1:
        loss, (grad_w, grad_x) = one_microbatch(per_example, given["loss_target"])
    else:
        def body(carry, xs):
            loss_sum, grad_sum = carry
            l_k, (gw_k, gx_k) = one_microbatch(xs[0], xs[1])
            with _jax.named_scope("update"):
                return (loss_sum + l_k, _jax.tree.map(_jnp.add, grad_sum, gw_k)), gx_k

        init = (_jnp.zeros((), _jnp.float32), _jax.tree.map(_jnp.zeros_like, weights))
        (loss, grad_w), grad_x = _jax.lax.scan(body, init, (per_example, given["loss_target"]))
    with _jax.named_scope("update"):
        delta_w, new_m, new_v = {}, {}, {}
        for n in TWIN_WEIGHTS:
            delta_w[n], new_m[n], new_v[n] = _adamw(weights[n], grad_w[n], given["m_" + n], given["v_" + n])
    return (loss, grad_x, *[grad_w[n] for n in TWIN_WEIGHTS], *[delta_w[n] for n in TWIN_WEIGHTS],
            *[new_m[n] for n in TWIN_WEIGHTS], *[new_v[n] for n in TWIN_WEIGHTS])
```

```python
import functools
import math

import jax
import jax.numpy as jnp
from jax import lax
from jax.experimental import pallas as pl
from jax.experimental.pallas import tpu as pltpu

F32 = jnp.float32
BF16 = jnp.bfloat16

D = 1024
DP = 512
PG = 128
WINDOWS = (2, 4, 8, 16)
DR = 1024
NH = 8
HD = 128
DIN = 4608
DFF = 2816
EPS = 1e-6
LRU_C = 8.0
POOL_HALO = 16
CONV_HALO = 8

ADAM_LR = 0.001
ADAM_B1 = 0.9
ADAM_B2 = 0.999
ADAM_EPS = 1e-08
ADAM_WD = 0.01
ADAM_STEP = 10

VMEM_LIMIT = 56 * 1024 * 1024
MESH_AXES = ("x", "y", "c")
MESH = pl.DeviceIdType.MESH


def _dot(a, b):
    return jnp.dot(a, b, preferred_element_type=F32)


def _dot_nt(a, b):
    return lax.dot_general(a, b, (((1,), (1,)), ((), ())), preferred_element_type=F32)


def _dot_tn(a, b):
    return lax.dot_general(a, b, (((0,), (0,)), ((), ())), preferred_element_type=F32)


def _params(*sem):
    return pltpu.CompilerParams(dimension_semantics=sem, vmem_limit_bytes=VMEM_LIMIT)


def _resident(shape):
    nd = len(shape)
    return pl.BlockSpec(shape, lambda i: (0,) * nd, pipeline_mode=pl.Buffered(1))


def _rows(shape_cols, tm):
    return pl.BlockSpec((tm, shape_cols), lambda i: (i, 0))


def _gelu(x):
    c = math.sqrt(2.0 / math.pi)
    t = jnp.tanh(c * (x + 0.044715 * x * x * x))
    return 0.5 * x * (1.0 + t), t


def _gelu_grad(x, t):
    c = math.sqrt(2.0 / math.pi)
    return 0.5 * (1.0 + t) + 0.5 * x * (1.0 - t * t) * (c * (1.0 + 3.0 * 0.044715 * x * x))


def _softplus_neg(lam):
    z = jnp.exp(-jnp.abs(lam))
    u = 1.0 + z
    dlt = u - 1.0
    log1p = jnp.where(dlt == 0.0, z, jnp.log(u) * (z / jnp.where(dlt == 0.0, 1.0, dlt)))
    return jnp.maximum(-lam, 0.0) + log1p


def _pool_windows(ext, shift_sign):
    n = ext.shape[0]
    s = ext
    outs = []
    for w in WINDOWS:
        d = w // 2
        s = s + pltpu.roll(s, d if shift_sign > 0 else n - d, axis=0)
        outs.append(s[:, :PG])
        s = s[:, PG:]
    return outs


def _conv_taps(uext):
    taps = []
    for k in range(4):
        sh = 3 - k
        v = uext if sh == 0 else pltpu.roll(uext, sh, axis=0)
        taps.append(v[CONV_HALO:, :])
    return taps


def _gates(v, wa_ref, ba_ref, wx_ref, bx_ref, sp):
    vb = v.astype(BF16)
    ra, rx = [], []
    for h in range(NH):
        vh = vb[:, h * HD:(h + 1) * HD]
        ra.append(_dot(vh, wa_ref[h]))
        rx.append(_dot(vh, wx_ref[h]))
    r = jax.nn.sigmoid(jnp.concatenate(ra, axis=1) + ba_ref[...])
    i = jax.nn.sigmoid(jnp.concatenate(rx, axis=1) + bx_ref[...])
    log_a = (-LRU_C) * r * sp
    a = jnp.exp(log_a)
    mult = jnp.sqrt(-jnp.tanh(log_a) * (1.0 + a * a))
    return r, i, a, mult


def _in_proj(x, norm_mix, w_in, tm=256):
    S = x.shape[0]

    def body(x_ref, g_ref, w_ref, proj_ref, h_ref):
        xv = x_ref[...]
        r = lax.rsqrt(jnp.mean(xv * xv, axis=-1, keepdims=True) + EPS)
        h = (xv * r * g_ref[...]).astype(BF16)
        h_ref[...] = h
        for n0 in range(0, DIN, 512):
            proj_ref[:, n0:n0 + 512] = _dot(h, w_ref[:, n0:n0 + 512])

    return pl.pallas_call(
        body, name="in_proj", grid=(S // tm,),
        in_specs=[_rows(D, tm), _resident((1, D)), _resident((D, DIN))],
        out_specs=[_rows(DIN, tm), _rows(D, tm)],
        out_shape=[jax.ShapeDtypeStruct((S, DIN), F32), jax.ShapeDtypeStruct((S, D), BF16)],
        compiler_params=_params("parallel"),
    )(x, norm_mix, w_in)


def _mixer_fwd(proj, wg, scale, w_pool_out, conv_w, conv_b, wa, ba, wx, bx, lam, w_rnn_out, tm=256):
    S = proj.shape[0]
    UW = DP + 2 * DR

    def body(proj_ref, wg_ref, scale_ref, wpo_ref, cw_ref, cb_ref, wa_ref, ba_ref, wx_ref, bx_ref, lam_ref, wro_ref,
             pooled_ref, pm_ref, ypool_ref, hr_ref, z_ref, yrnn_ref, pool_carry, conv_carry, h_carry):
        i = pl.program_id(0)

        @pl.when(i == 0)
        def _():
            pool_carry[...] = jnp.zeros_like(pool_carry)
            conv_carry[...] = jnp.zeros_like(conv_carry)
            h_carry[...] = jnp.zeros_like(h_carry)

        rows = lax.broadcasted_iota(jnp.int32, (tm, 1), 0)
        t_glob = i * tm + rows

        u_pool = proj_ref[:, 0:DP]
        ext = jnp.concatenate([pool_carry[...], u_pool], axis=0)
        pool_carry[...] = u_pool[tm - POOL_HALO:, :]
        sums = _pool_windows(ext, +1)
        mixed = []
        for g, w in enumerate(WINDOWS):
            cnt = jnp.minimum(t_glob + 1, w).astype(F32)
            pooled_g = sums[g][POOL_HALO:, :] / cnt - u_pool[:, g * PG:(g + 1) * PG]
            pooled_b = pooled_g.astype(BF16)
            pooled_ref[:, g * PG:(g + 1) * PG] = pooled_b
            mixed.append(_dot(pooled_b, wg_ref[g]))
        pm = (jnp.concatenate(mixed, axis=1) * scale_ref[...]).astype(BF16)
        pm_ref[...] = pm
        ypool_ref[...] = _dot(pm, wpo_ref[...])

        u_rnn = proj_ref[:, DP:DP + DR]
        uext = jnp.concatenate([conv_carry[...], u_rnn], axis=0)
        conv_carry[...] = u_rnn[tm - CONV_HALO:, :]
        taps = _conv_taps(uext)
        v = cb_ref[...]
        for k in range(4):
            v = v + taps[k] * cw_ref[k:k + 1, :]
        sp = _softplus_neg(lam_ref[...])
        _, gi, a, mult = _gates(v, wa_ref, ba_ref, wx_ref, bx_ref, sp)
        b = mult * gi * v
        A, B = a, b
        d = 1
        while d < tm:
            keep = rows >= d
            As = pltpu.roll(A, d, axis=0)
            Bs = pltpu.roll(B, d, axis=0)
            B = jnp.where(keep, A * Bs + B, B)
            A = jnp.where(keep, A * As, A)
            d *= 2
        hr = A * h_carry[0:1, :] + B
        h_carry[0:1, :] = hr[tm - 1:tm, :]
        hr_ref[...] = hr
        gg, _ = _gelu(proj_ref[:, DP + DR:UW])
        z = (hr * gg).astype(BF16)
        z_ref[...] = z
        yrnn_ref[...] = _dot(z, wro_ref[...])

    return pl.pallas_call(
        body, name="mixer_fwd", grid=(S // tm,),
        in_specs=[_rows(UW, tm), _resident((4, PG, PG)), _resident((1, DP)), _resident((DP, D)), _resident((4, DR)),
                  _resident((1, DR)), _resident((NH, HD, HD)), _resident((1, DR)), _resident((NH, HD, HD)),
                  _resident((1, DR)), _resident((1, DR)), _resident((DR, D))],
        out_specs=[_rows(DP, tm), _rows(DP, tm), _rows(D, tm), _rows(DR, tm), _rows(DR, tm), _rows(D, tm)],
        out_shape=[jax.ShapeDtypeStruct((S, DP), BF16), jax.ShapeDtypeStruct((S, DP), BF16),
                   jax.ShapeDtypeStruct((S, D), F32), jax.ShapeDtypeStruct((S, DR), F32),
                   jax.ShapeDtypeStruct((S, DR), BF16), jax.ShapeDtypeStruct((S, D), F32)],
        scratch_shapes=[pltpu.VMEM((POOL_HALO, DP), F32), pltpu.VMEM((CONV_HALO, DR), F32), pltpu.VMEM((8, DR), F32)],
        compiler_params=_params("arbitrary"),
    )(proj, wg, scale, w_pool_out, conv_w, conv_b, wa, ba, wx, bx, lam, w_rnn_out)


def _mid(x, proj, y_pool, y_rnn, target, w_o, norm_ffn, w_ffn_in, w_ffn_out, norm_final, tm=128):
    S = x.shape[0]
    GL0 = (DP + 2 * DR) // 512

    def gl_spec(k):
        return pl.BlockSpec((tm, 512), lambda i: (i, GL0 + k))

    def body(x_ref, gl0, gl1, gl2, gl3, yp_ref, yr_ref, t_ref, wo_ref, gf_ref, wfi_ref, wfo_ref, gn_ref,
             mix_ref, h2_ref, act_ref, dx3b_ref, dgu_ref, dx2_ref, dx2b_ref, dmixo_ref, loss_ref, dvec_ref):
        i = pl.program_id(0)

        @pl.when(i == 0)
        def _():
            loss_ref[...] = jnp.zeros_like(loss_ref)
            dvec_ref[...] = jnp.zeros_like(dvec_ref)

        s_p = jax.nn.sigmoid(jnp.concatenate([gl0[...], gl1[...]], axis=1))
        s_r = jax.nn.sigmoid(jnp.concatenate([gl2[...], gl3[...]], axis=1))
        mix = (s_p * yp_ref[...] + s_r * yr_ref[...]).astype(BF16)
        mix_ref[...] = mix
        x2 = x_ref[...] + _dot(mix, wo_ref[...])
        r2 = lax.rsqrt(jnp.mean(x2 * x2, axis=-1, keepdims=True) + EPS)
        xh2 = x2 * r2
        g_ffn = gf_ref[...]
        h2 = (xh2 * g_ffn).astype(BF16)
        h2_ref[...] = h2
        gate = _dot(h2, wfi_ref[:, 0:DFF])
        up = _dot(h2, wfi_ref[:, DFF:2 * DFF])
        sg = jax.nn.sigmoid(gate)
        sl = gate * sg
        act = (sl * up).astype(BF16)
        act_ref[...] = act
        x3 = x2 + _dot(act, wfo_ref[...])
        r3 = lax.rsqrt(jnp.mean(x3 * x3, axis=-1, keepdims=True) + EPS)
        xh3 = x3 * r3
        g_fin = gn_ref[...]
        e = xh3 * g_fin - t_ref[...]
        loss_ref[...] += jnp.sum(e * e, axis=(0, 1), keepdims=True) * (0.5 / D)
        dy = e * (1.0 / D)
        dvec_ref[0:1, :] += jnp.sum(dy * xh3, axis=0, keepdims=True)
        dxh3 = dy * g_fin
        dx3 = r3 * (dxh3 - xh3 * jnp.mean(dxh3 * xh3, axis=-1, keepdims=True))
        dx3b = dx3.astype(BF16)
        dx3b_ref[...] = dx3b
        dact = _dot_nt(dx3b, wfo_ref[...])
        dgate = (dact * up * (sg * (1.0 + gate * (1.0 - sg)))).astype(BF16)
        dup = (dact * sl).astype(BF16)
        dgu_ref[:, 0:DFF] = dgate
        dgu_ref[:, DFF:2 * DFF] = dup
        dh2 = _dot_nt(dgate, wfi_ref[:, 0:DFF]) + _dot_nt(dup, wfi_ref[:, DFF:2 * DFF])
        dvec_ref[1:2, :] += jnp.sum(dh2 * xh2, axis=0, keepdims=True)
        dxh2 = dh2 * g_ffn
        dx2 = dx3 + r2 * (dxh2 - xh2 * jnp.mean(dxh2 * xh2, axis=-1, keepdims=True))
        dx2_ref[...] = dx2
        dx2b = dx2.astype(BF16)
        dx2b_ref[...] = dx2b
        dmixo_ref[...] = _dot_nt(dx2b, wo_ref[...])

    return pl.pallas_call(
        body, name="mid", grid=(S // tm,),
        in_specs=[_rows(D, tm), gl_spec(0), gl_spec(1), gl_spec(2), gl_spec(3), _rows(D, tm), _rows(D, tm), _rows(D, tm),
                  _resident((D, D)), _resident((1, D)), _resident((D, 2 * DFF)), _resident((DFF, D)), _resident((1, D))],
        out_specs=[_rows(D, tm), _rows(D, tm), _rows(DFF, tm), _rows(D, tm), _rows(2 * DFF, tm), _rows(D, tm),
                   _rows(D, tm), _rows(D, tm), _resident((1, 1)), _resident((8, D))],
        out_shape=[jax.ShapeDtypeStruct((S, D), BF16), jax.ShapeDtypeStruct((S, D), BF16),
                   jax.ShapeDtypeStruct((S, DFF), BF16), jax.ShapeDtypeStruct((S, D), BF16),
                   jax.ShapeDtypeStruct((S, 2 * DFF), BF16), jax.ShapeDtypeStruct((S, D), F32),
                   jax.ShapeDtypeStruct((S, D), BF16), jax.ShapeDtypeStruct((S, D), F32),
                   jax.ShapeDtypeStruct((1, 1), F32), jax.ShapeDtypeStruct((8, D), F32)],
        compiler_params=_params("arbitrary"),
    )(x, proj, proj, proj, proj, y_pool, y_rnn, target, w_o, norm_ffn, w_ffn_in, w_ffn_out, norm_final)


VEC_ROWS = 16
MAT_WA = 4 * PG
MAT_WX = MAT_WA + NH * HD
MAT_ROWS = MAT_WX + NH * HD


def _mixer_bwd(proj, dmixo, y_pool, y_rnn, hr, wg, scale, w_pool_out, conv_w, conv_b, wa, ba, wx, bx, lam, w_rnn_out,
               tm=256):
    S = proj.shape[0]
    nt = S // tm

    def rev(cols):
        return pl.BlockSpec((tm, cols), lambda i: (nt - 1 - i, 0))

    def halo(rows_, cols):
        per = tm // rows_
        return pl.BlockSpec((rows_, cols), lambda i: (jnp.maximum((nt - 1 - i) * per - 1, 0), 0))

    def body(proj_ref, projh_ref, dmixo_ref, yp_ref, yr_ref, hr_ref, hrh_ref, wg_ref, scale_ref, wpo_ref, cw_ref, cb_ref,
             wa_ref, ba_ref, wx_ref, bx_ref, lam_ref, wro_ref,
             dproj_ref, dypb_ref, dyrb_ref, dmat_ref, dvec_ref,
             q_carry, dv_carry, a_carry, g_carry):
        i = pl.program_id(0)
        ti = nt - 1 - i

        @pl.when(i == 0)
        def _():
            q_carry[...] = jnp.zeros_like(q_carry)
            dv_carry[...] = jnp.zeros_like(dv_carry)
            a_carry[...] = jnp.zeros_like(a_carry)
            g_carry[...] = jnp.zeros_like(g_carry)
            dmat_ref[...] = jnp.zeros_like(dmat_ref)
            dvec_ref[...] = jnp.zeros_like(dvec_ref)

        rows = lax.broadcasted_iota(jnp.int32, (tm, 1), 0)
        t_glob = ti * tm + rows
        has_prev = (ti > 0).astype(F32)
        dmixo = dmixo_ref[...]

        s_p = jax.nn.sigmoid(proj_ref[:, DP + 2 * DR:DP + 2 * DR + D])
        s_r = jax.nn.sigmoid(proj_ref[:, DP + 2 * DR + D:DIN])
        dproj_ref[:, DP + 2 * DR:DP + 2 * DR + D] = (dmixo * yp_ref[...] * s_p * (1.0 - s_p)).astype(BF16)
        dproj_ref[:, DP + 2 * DR + D:DIN] = (dmixo * yr_ref[...] * s_r * (1.0 - s_r)).astype(BF16)
        dyp = (dmixo * s_p).astype(BF16)
        dyr = (dmixo * s_r).astype(BF16)
        dypb_ref[...] = dyp
        dyrb_ref[...] = dyr

        dz = _dot_nt(dyr, wro_ref[...])
        u_gate = proj_ref[:, DP + DR:DP + 2 * DR]
        gg, tg = _gelu(u_gate)
        hr_t = hr_ref[...]
        dproj_ref[:, DP + DR:DP + 2 * DR] = (dz * hr_t * _gelu_grad(u_gate, tg)).astype(BF16)
        dhr = dz * gg

        u_rnn = proj_ref[:, DP:DP + DR]
        uext = jnp.concatenate([projh_ref[POOL_HALO - CONV_HALO:, DP:DP + DR] * has_prev, u_rnn], axis=0)
        taps = _conv_taps(uext)
        v = cb_ref[...]
        for k in range(4):
            v = v + taps[k] * cw_ref[k:k + 1, :]
        sp = _softplus_neg(lam_ref[...])
        r, gi, a, mult = _gates(v, wa_ref, ba_ref, wx_ref, bx_ref, sp)

        C = jnp.where(rows == tm - 1, a_carry[0:1, :], pltpu.roll(a, tm - 1, axis=0))
        G = dhr
        d = 1
        while d < tm:
            keep = rows < tm - d
            Gs = pltpu.roll(G, tm - d, axis=0)
            Cs = pltpu.roll(C, tm - d, axis=0)
            G = jnp.where(keep, G + C * Gs, G)
            C = jnp.where(keep, C * Cs, C)
            d *= 2
        g = G + C * g_carry[0:1, :]
        a_carry[0:1, :] = a[0:1, :]
        g_carry[0:1, :] = g[0:1, :]

        h_prev = jnp.where(rows == 0, hrh_ref[7:8, :] * has_prev, pltpu.roll(hr_t, 1, axis=0))
        da = g * h_prev
        dmult = g * gi * v
        di = g * mult * v
        dv = g * mult * gi
        dlog_a = da * a - dmult * (a * a / mult)
        dvec_ref[4:5, :] += jnp.sum(dlog_a * r, axis=0, keepdims=True)
        dra = (dlog_a * ((-LRU_C) * sp) * r * (1.0 - r))
        drx = di * gi * (1.0 - gi)
        dvec_ref[2:3, :] += jnp.sum(dra, axis=0, keepdims=True)
        dvec_ref[3:4, :] += jnp.sum(drx, axis=0, keepdims=True)
        drab = dra.astype(BF16)
        drxb = drx.astype(BF16)
        vb = v.astype(BF16)
        dvg = []
        for h in range(NH):
            sl = slice(h * HD, (h + 1) * HD)
            dvg.append(_dot_nt(drab[:, sl], wa_ref[h]) + _dot_nt(drxb[:, sl], wx_ref[h]))
            dmat_ref[MAT_WA + h * HD:MAT_WA + (h + 1) * HD, :] += _dot_tn(vb[:, sl], drab[:, sl])
            dmat_ref[MAT_WX + h * HD:MAT_WX + (h + 1) * HD, :] += _dot_tn(vb[:, sl], drxb[:, sl])
        dv = dv + jnp.concatenate(dvg, axis=1)
        dvec_ref[1:2, :] += jnp.sum(dv, axis=0, keepdims=True)
        for k in range(4):
            dvec_ref[5 + k:6 + k, :] += jnp.sum(dv * taps[k], axis=0, keepdims=True)
        dvext = jnp.concatenate([dv, dv_carry[...]], axis=0)
        dv_carry[...] = dv[0:CONV_HALO, :]
        n = tm + CONV_HALO
        du_rnn = dv * cw_ref[3:4, :]
        for k in range(3):
            du_rnn = du_rnn + pltpu.roll(dvext, n - (3 - k), axis=0)[0:tm, :] * cw_ref[k:k + 1, :]
        dproj_ref[:, DP:DP + DR] = du_rnn.astype(BF16)

        dpm = _dot_nt(dyp, wpo_ref[...])
        u_pool = proj_ref[:, 0:DP]
        ext = jnp.concatenate([projh_ref[:, 0:DP] * has_prev, u_pool], axis=0)
        sums = _pool_windows(ext, +1)
        scale_v = scale_ref[...]
        qs = []
        dpooled = []
        dscale = []
        for gi_, w in enumerate(WINDOWS):
            sl = slice(gi_ * PG, (gi_ + 1) * PG)
            cnt = jnp.minimum(t_glob + 1, w).astype(F32)
            pooled_b = (sums[gi_][POOL_HALO:, :] / cnt - u_pool[:, sl]).astype(BF16)
            mixed_g = _dot(pooled_b, wg_ref[gi_])
            dscale.append(jnp.sum(dpm[:, sl] * mixed_g, axis=0, keepdims=True))
            dmixed_b = (dpm[:, sl] * scale_v[:, sl]).astype(BF16)
            dmat_ref[gi_ * PG:(gi_ + 1) * PG, :] += _dot_tn(pooled_b, dmixed_b)
            dp_g = _dot_nt(dmixed_b, wg_ref[gi_])
            dpooled.append(dp_g)
            qs.append(dp_g / cnt)
        dvec_ref[0:1, 0:DP] += jnp.concatenate(dscale, axis=1)
        q = jnp.concatenate(qs, axis=1)
        qext = jnp.concatenate([q, q_carry[...]], axis=0)
        q_carry[...] = q[0:POOL_HALO, :]
        tsum = _pool_windows(qext, -1)
        for gi_ in range(4):
            dproj_ref[:, gi_ * PG:(gi_ + 1) * PG] = (tsum[gi_][0:tm, :] - dpooled[gi_]).astype(BF16)

        @pl.when(i == nt - 1)
        def _():
            dvec_ref[4:5, :] = dvec_ref[4:5, :] * (LRU_C * jax.nn.sigmoid(-lam_ref[...]))

    return pl.pallas_call(
        body, name="mixer_bwd", grid=(nt,),
        in_specs=[rev(DIN), halo(POOL_HALO, DIN), rev(D), rev(D), rev(D), rev(DR), halo(8, DR),
                  _resident((4, PG, PG)), _resident((1, DP)), _resident((DP, D)), _resident((4, DR)), _resident((1, DR)),
                  _resident((NH, HD, HD)), _resident((1, DR)), _resident((NH, HD, HD)), _resident((1, DR)),
                  _resident((1, DR)), _resident((DR, D))],
        out_specs=[rev(DIN), rev(D), rev(D), _resident((MAT_ROWS, HD)), _resident((VEC_ROWS, DR))],
        out_shape=[jax.ShapeDtypeStruct((S, DIN), BF16), jax.ShapeDtypeStruct((S, D), BF16),
                   jax.ShapeDtypeStruct((S, D), BF16), jax.ShapeDtypeStruct((MAT_ROWS, HD), F32),
                   jax.ShapeDtypeStruct((VEC_ROWS, DR), F32)],
        scratch_shapes=[pltpu.VMEM((POOL_HALO, DP), F32), pltpu.VMEM((CONV_HALO, DR), F32), pltpu.VMEM((8, DR), F32),
                        pltpu.VMEM((8, DR), F32)],
        compiler_params=_params("arbitrary"),
    )(proj, proj, dmixo, y_pool, y_rnn, hr, hr, wg, scale, w_pool_out, conv_w, conv_b, wa, ba, wx, bx, lam, w_rnn_out)


def _in_bwd(dproj, x, dx2, norm_mix, w_in, tm=256):
    S = x.shape[0]

    def body(dp_ref, x_ref, dx2_ref, g_ref, w_ref, dx_ref, dg_ref):
        i = pl.program_id(0)

        @pl.when(i == 0)
        def _():
            dg_ref[...] = jnp.zeros_like(dg_ref)

        dh = _dot_nt(dp_ref[:, 0:1536], w_ref[:, 0:1536])
        dh = dh + _dot_nt(dp_ref[:, 1536:3072], w_ref[:, 1536:3072])
        dh = dh + _dot_nt(dp_ref[:, 3072:DIN], w_ref[:, 3072:DIN])
        xv = x_ref[...]
        r = lax.rsqrt(jnp.mean(xv * xv, axis=-1, keepdims=True) + EPS)
        xh = xv * r
        dg_ref[0:1, :] += jnp.sum(dh * xh, axis=0, keepdims=True)
        dxh = dh * g_ref[...]
        dx_ref[...] = dx2_ref[...] + r * (dxh - xh * jnp.mean(dxh * xh, axis=-1, keepdims=True))

    return pl.pallas_call(
        body, name="in_bwd", grid=(S // tm,),
        in_specs=[_rows(DIN, tm), _rows(D, tm), _rows(D, tm), _resident((1, D)), _resident((D, DIN))],
        out_specs=[_rows(D, tm), _resident((8, D))],
        out_shape=[jax.ShapeDtypeStruct((S, D), F32), jax.ShapeDtypeStruct((8, D), F32)],
        compiler_params=_params("arbitrary"),
    )(dproj, x, dx2, norm_mix, w_in)


def _wgrad(a, b, name, tk, tn, ts=512):
    S, K = a.shape
    N = b.shape[1]
    ns = S // ts

    def body(a_ref, b_ref, o_ref, acc_ref):
        s = pl.program_id(2)

        @pl.when(s == 0)
        def _():
            acc_ref[...] = jnp.zeros_like(acc_ref)

        acc_ref[...] += _dot_tn(a_ref[...], b_ref[...])

        @pl.when(s == ns - 1)
        def _():
            o_ref[...] = acc_ref[...].astype(BF16)

    return pl.pallas_call(
        body, name=name, grid=(K // tk, N // tn, ns),
        in_specs=[pl.BlockSpec((ts, tk), lambda k, n, s: (s, k)), pl.BlockSpec((ts, tn), lambda k, n, s: (s, n))],
        out_specs=pl.BlockSpec((tk, tn), lambda k, n, s: (k, n)),
        out_shape=jax.ShapeDtypeStruct((K, N), BF16),
        scratch_shapes=[pltpu.VMEM((tk, tn), F32)],
        compiler_params=_params("parallel", "parallel", "arbitrary"),
    )(a, b)


VEC_SCALE, VEC_CONV_B, VEC_BA, VEC_BX, VEC_LAM, VEC_CONV_W, VEC_NORM_FINAL, VEC_NORM_FFN, VEC_NORM_MIX = 0, 1, 2, 3, 4, 5, 9, 10, 11


def _local_step(x, target, wts, small):
    w_in, w_pool_out, conv_w, w_rnn_out, w_o, w_ffn_in, w_ffn_out = wts
    norm_mix, wg, scale, conv_b, wa, ba, wx, bx, lam, norm_ffn, norm_final = small
    wg_b, wa_b, wx_b = wg.astype(BF16), wa.astype(BF16), wx.astype(BF16)
    ba2, bx2 = ba.reshape(1, DR), bx.reshape(1, DR)
    proj, h1 = _in_proj(x, norm_mix, w_in)
    _, pm, y_pool, hr, z, y_rnn = _mixer_fwd(proj, wg_b, scale, w_pool_out, conv_w, conv_b, wa_b, ba2, wx_b, bx2, lam,
                                             w_rnn_out)
    mix, h2, act, dx3b, dgu, dx2, dx2b, dmixo, loss, dvec_mid = _mid(x, proj, y_pool, y_rnn, target, w_o, norm_ffn,
                                                                    w_ffn_in, w_ffn_out, norm_final)
    dproj, dypb, dyrb, dmat, dvec_mix = _mixer_bwd(proj, dmixo, y_pool, y_rnn, hr, wg_b, scale, w_pool_out, conv_w,
                                                   conv_b, wa_b, ba2, wx_b, bx2, lam, w_rnn_out)
    grad_x, dvec_in = _in_bwd(dproj, x, dx2, norm_mix, w_in)
    big = (
        _wgrad(h1, dproj, "wgrad_in", 1024, 1152),
        _wgrad(pm, dypb, "wgrad_pool_out", 512, 1024),
        _wgrad(z, dyrb, "wgrad_rnn_out", 1024, 1024),
        _wgrad(mix, dx2b, "wgrad_o", 1024, 1024),
        _wgrad(h2, dgu, "wgrad_ffn_in", 1024, 1408),
        _wgrad(act, dx3b, "wgrad_ffn_out", 1408, 1024),
    )
    dvec = jnp.concatenate([dvec_mix[0:9], dvec_mid[0:2], dvec_in[0:1], jnp.zeros((VEC_ROWS - 12, DR), F32)], axis=0)
    return loss, grad_x, big, dmat, dvec


class _Big:
    def __init__(self, name, rows, cols, axis, n, dtype=BF16):
        self.name, self.rows, self.cols, self.axis, self.n, self.dtype = name, rows, cols, axis, n, dtype
        self.align = 16 if dtype == BF16 else 8
        self.rowhalf = axis == 1 and n % 128 != 0
        self.cb_shape = (rows, 2 * n) if axis == 1 else (2 * n, cols)

    def ag_block(self, ref, j):
        if self.axis == 1:
            return ref.at[:, pl.ds(pl.multiple_of(j * 2 * self.n, 128), 2 * self.n)]
        return ref.at[pl.ds(pl.multiple_of(j * 2 * self.n, self.align), 2 * self.n), :]

    def ag_half(self, ref, j, h):
        if self.axis == 1:
            hr = self.rows // 2
            return ref.at[pl.ds(pl.multiple_of(h * hr, self.align), hr),
                          pl.ds(pl.multiple_of(j * 2 * self.n, 128), 2 * self.n)]
        return ref.at[pl.ds(pl.multiple_of(j * 2 * self.n + h * self.n, self.align), self.n), :]

    def cb_half(self, ref, h):
        hr = self.cb_shape[0] // 2
        return ref.at[pl.ds(pl.multiple_of(h * hr, self.align), hr), :]

    @property
    def piece_shape(self):
        if self.rowhalf:
            return (self.rows // 2, 2 * self.n)
        return (self.rows, self.n) if self.axis == 1 else (self.n, self.cols)

    @property
    def sub_blocks(self):
        return 4 if self.rowhalf else 1

    @property
    def block(self):
        r, c = self.piece_shape
        return (r // self.sub_blocks, c)

    def block_index(self, j, h, r):
        if self.rowhalf:
            return (h * self.sub_blocks + r, j)
        return (0, 2 * j + h) if self.axis == 1 else (2 * j + h, 0)

    def piece(self, ref, j, h):
        if self.rowhalf:
            return self.ag_half(ref, j, h)
        if self.axis == 1:
            return ref.at[:, pl.ds(pl.multiple_of((2 * j + h) * self.n, 128), self.n)]
        return ref.at[pl.ds(pl.multiple_of((2 * j + h) * self.n, self.align), self.n), :]


BIG = (_Big("w_in", D, DIN, 1, DIN // 8), _Big("w_pool_out", DP, D, 1, D // 8), _Big("w_rnn_out", DR, D, 0, DR // 8),
       _Big("w_o", D, D, 0, D // 8), _Big("w_ffn_in", D, 2 * DFF, 1, 2 * DFF // 8), _Big("w_ffn_out", DFF, D, 0, DFF // 8))
CONV_W = _Big("conv_w", 16, DR, 1, DR // 8, F32)
GATHERED = BIG + (CONV_W,)

HBM_SPEC = pl.BlockSpec(memory_space=pl.ANY)
VMEM_SPEC = pl.BlockSpec(memory_space=pltpu.VMEM)


def _place():
    x, y, c = (lax.axis_index(a) for a in MESH_AXES)
    other_chips = [(1 - x, y), (x, 1 - y), (1 - x, 1 - y)]
    return x, y, c, other_chips


def _remote(src, dst, send_sems, recv_sems, idx, to):
    return pltpu.make_async_remote_copy(src_ref=src, dst_ref=dst, send_sem=send_sems.at[idx], recv_sem=recv_sems.at[idx],
                                        device_id=to, device_id_type=MESH)


def _pad_to_chip_block(t, w, c):
    w = w.astype(t.dtype)
    start = (0, t.n * c) if t.axis == 1 else (t.n * c, 0)
    return lax.dynamic_update_slice(jnp.zeros(t.cb_shape, t.dtype), w, start)


def _add_into(dst_ref, src_ref):
    rows = dst_ref.shape[0]
    ch = 64 if rows % 64 == 0 else rows

    def step(i, carry):
        sl = pl.ds(pl.multiple_of(i * ch, ch), ch)
        dst_ref[sl, :] = (dst_ref[sl, :].astype(F32) + src_ref[sl, :].astype(F32)).astype(dst_ref.dtype)
        return carry

    lax.fori_loop(0, rows // ch, step, 0)


def _all_gather(padded):
    nt = len(GATHERED)

    def body(*refs):
        ins, outs, mine, sib = (refs[k * nt:(k + 1) * nt] for k in range(4))
        send_sems, recv_sems, loc_sems = refs[4 * nt:]
        x, y, c, chips = _place()
        sibling = (x, y, 1 - c)
        m = 2 * x + y
        loads, swaps = [], []
        for t in range(nt):
            loads.append(pltpu.make_async_copy(ins[t], mine[t], loc_sems.at[t, 0]))
            swaps.append(_remote(ins[t], sib[t], send_sems, recv_sems, (t, 0), sibling))
            loads[t].start()
            swaps[t].start()
        stores, sends = [], []
        for t, T in enumerate(GATHERED):
            loads[t].wait()
            swaps[t].wait_recv()
            _add_into(mine[t], sib[t])
            stores.append(pltpu.make_async_copy(mine[t], T.ag_block(outs[t], m), loc_sems.at[t, 1]))
            stores[t].start()
            for k, chip in enumerate(chips):
                cp = _remote(T.cb_half(mine[t], c), T.ag_half(outs[t], m, c), send_sems, recv_sems, (t, 1 + k), (*chip, c))
                cp.start()
                sends.append(cp)
        passed = []
        for t, T in enumerate(GATHERED):
            for k, chip in enumerate(chips):
                land = T.ag_half(outs[t], 2 * chip[0] + chip[1], c)
                _remote(land, land, send_sems, recv_sems, (t, 1 + k), sibling).wait_recv()
                cp = _remote(land, land, send_sems, recv_sems, (t, 4 + k), sibling)
                cp.start()
                passed.append(cp)
        for t, T in enumerate(GATHERED):
            for k, chip in enumerate(chips):
                land = T.ag_half(outs[t], 2 * chip[0] + chip[1], 1 - c)
                _remote(land, land, send_sems, recv_sems, (t, 4 + k), sibling).wait_recv()
        for cp in swaps + sends + passed:
            cp.wait_send()
        for st in stores:
            st.wait()

    return pl.pallas_call(
        body, name="all_gather_weights",
        in_specs=[HBM_SPEC] * nt, out_specs=[HBM_SPEC] * nt,
        out_shape=[jax.ShapeDtypeStruct((T.rows, T.cols), T.dtype) for T in GATHERED],
        scratch_shapes=[pltpu.VMEM(T.cb_shape, T.dtype) for T in GATHERED] * 2
        + [pltpu.SemaphoreType.DMA((nt, 7)), pltpu.SemaphoreType.DMA((nt, 7)), pltpu.SemaphoreType.DMA((nt, 2))],
        compiler_params=pltpu.CompilerParams(vmem_limit_bytes=VMEM_LIMIT),
    )(*padded)


def _pair_exchange(grads):
    nt = len(BIG)

    def body(*refs):
        ins, outs = refs[:nt], refs[nt:2 * nt]
        send_sems, recv_sems = refs[2 * nt:]
        x, y, c, _ = _place()
        cps = []
        for t, T in enumerate(BIG):
            for j in range(4):
                cp = _remote(T.piece(ins[t], j, 1 - c), outs[t].at[j], send_sems, recv_sems, (t, j), (x, y, 1 - c))
                cp.start()
                cps.append(cp)
        for cp in cps:
            cp.wait()

    return pl.pallas_call(
        body, name="grad_pair_exchange",
        in_specs=[HBM_SPEC] * nt, out_specs=[HBM_SPEC] * nt,
        out_shape=[jax.ShapeDtypeStruct((4,) + T.piece_shape, BF16) for T in BIG],
        scratch_shapes=[pltpu.SemaphoreType.DMA((nt, 4)), pltpu.SemaphoreType.DMA((nt, 4))],
    )(*grads)


def _pair_sum(T, g, lz1, where):
    blk = T.block

    def body(where_ref, g_ref, l_ref, o_ref):
        o_ref[...] = (g_ref[...].astype(F32) + l_ref[...].astype(F32)).astype(BF16)

    return pl.pallas_call(
        body, name="grad_pair_sum_" + T.name,
        grid_spec=pltpu.PrefetchScalarGridSpec(
            num_scalar_prefetch=1, grid=(3, T.sub_blocks),
            in_specs=[pl.BlockSpec(blk, lambda k, r, w: T.block_index(w[k], w[3], r)),
                      pl.BlockSpec((None,) + blk, lambda k, r, w: (w[k], r, 0))],
            out_specs=pl.BlockSpec((None,) + blk, lambda k, r, w: (k, r, 0))),
        out_shape=jax.ShapeDtypeStruct((3,) + T.piece_shape, BF16),
        compiler_params=_params("arbitrary", "arbitrary"),
    )(where, g, lz1)


def _chip_scatter(sums):
    nt = len(BIG)

    def body(*refs):
        ins, outs = refs[:nt], refs[nt:2 * nt]
        send_sems, recv_sems = refs[2 * nt:]
        x, y, c, chips = _place()
        cps = []
        for t in range(nt):
            for k, chip in enumerate(chips):
                cp = _remote(ins[t].at[k], outs[t].at[k], send_sems, recv_sems, (t, k), (*chip, c))
                cp.start()
                cps.append(cp)
        for cp in cps:
            cp.wait()

    return pl.pallas_call(
        body, name="grad_chip_scatter",
        in_specs=[HBM_SPEC] * nt, out_specs=[HBM_SPEC] * nt,
        out_shape=[jax.ShapeDtypeStruct((3,) + T.piece_shape, BF16) for T in BIG],
        scratch_shapes=[pltpu.SemaphoreType.DMA((nt, 3)), pltpu.SemaphoreType.DMA((nt, 3))],
    )(*sums)


def _adamw(w, g, m, v):
    m = ADAM_B1 * m + (1.0 - ADAM_B1) * g
    v = ADAM_B2 * v + (1.0 - ADAM_B2) * (g * g)
    m_hat = m / (1.0 - ADAM_B1 ** ADAM_STEP)
    v_hat = v / (1.0 - ADAM_B2 ** ADAM_STEP)
    delta = -ADAM_LR * (m_hat / (jnp.sqrt(v_hat) + ADAM_EPS) + ADAM_WD * w)
    return delta, m, v


def _final_sum(T, g, lz1, lz2, where, wmv=None):
    blk = T.block
    n_out = 1 if wmv is None else 4

    def body(where_ref, g_ref, l1_ref, l2_ref, *rest):
        tot = g_ref[...].astype(F32) + l1_ref[...].astype(F32)
        for k in range(3):
            tot = tot + l2_ref[k].astype(F32)
        if wmv is None:
            rest[0][...] = tot
        else:
            w_ref, m_ref, v_ref, g_out, d_out, m_out, v_out = rest
            g_out[...] = tot
            d_out[...], m_out[...], v_out[...] = _adamw(w_ref[...], tot, m_ref[...], v_ref[...])

    own = pl.BlockSpec(blk, lambda r, w: (r, 0))
    return pl.pallas_call(
        body, name="grad_final_" + T.name,
        grid_spec=pltpu.PrefetchScalarGridSpec(
            num_scalar_prefetch=1, grid=(T.sub_blocks,),
            in_specs=[pl.BlockSpec(blk, lambda r, w: T.block_index(w[0], w[1], r)),
                      pl.BlockSpec((None,) + blk, lambda r, w: (w[0], r, 0)),
                      pl.BlockSpec((3,) + blk, lambda r, w: (0, r, 0))] + ([] if wmv is None else [own] * 3),
            out_specs=[own] * n_out),
        out_shape=[jax.ShapeDtypeStruct(T.piece_shape, F32)] * n_out,
        compiler_params=_params("arbitrary"),
    )(where, g, lz1, lz2, *(() if wmv is None else wmv))


def _pair_join(halves):
    nt = len(halves)

    def body(*refs):
        ins, outs = refs[:nt], refs[nt:2 * nt]
        send_sems, recv_sems, loc_sems = refs[2 * nt:]
        x, y, c, _ = _place()
        cps, loc = [], []
        for t in range(nt):
            hr = ins[t].shape[0]
            mine = outs[t].at[pl.ds(pl.multiple_of(c * hr, 8), hr), :]
            loc.append(pltpu.make_async_copy(ins[t], mine, loc_sems.at[t]))
            cps.append(_remote(ins[t], mine, send_sems, recv_sems, (t,), (x, y, 1 - c)))
            loc[t].start()
            cps[t].start()
        for t in range(nt):
            hr = ins[t].shape[0]
            theirs = outs[t].at[pl.ds(pl.multiple_of((1 - c) * hr, 8), hr), :]
            _remote(theirs, theirs, send_sems, recv_sems, (t,), (x, y, 1 - c)).wait_recv()
            cps[t].wait_send()
            loc[t].wait()

    return pl.pallas_call(
        body, name="grad_pair_join",
        in_specs=[HBM_SPEC] * nt, out_specs=[HBM_SPEC] * nt,
        out_shape=[jax.ShapeDtypeStruct((2 * h.shape[0], h.shape[1]), F32) for h in halves],
        scratch_shapes=[pltpu.SemaphoreType.DMA((nt,)), pltpu.SemaphoreType.DMA((nt,)), pltpu.SemaphoreType.DMA((nt,))],
    )(*halves)


def _adam_rows(name, w, g, m, v, tm=256):
    rows, cols = w.shape

    def body(w_ref, g_ref, m_ref, v_ref, d_out, m_out, v_out):
        d_out[...], m_out[...], v_out[...] = _adamw(w_ref[...], g_ref[...], m_ref[...], v_ref[...])

    spec = pl.BlockSpec((tm, cols), lambda i: (i, 0))
    return pl.pallas_call(
        body, name="adam_" + name, grid=(rows // tm,), in_specs=[spec] * 4, out_specs=[spec] * 3,
        out_shape=[jax.ShapeDtypeStruct((rows, cols), F32)] * 3, compiler_params=_params("parallel"),
    )(w, g, m, v)


MAT_PIECE = MAT_ROWS // 8
VEC_PIECE = DR // 8


def _small_all_reduce(dmat, dvec):
    def body(mat_ref, vec_ref, mat_out, vec_out, mat_in, vec_in, mat_sum, vec_sum, send_sems, recv_sems, loc_sems):
        x, y, c, _ = _place()
        me = 4 * x + 2 * y + c

        def peer(r):
            return (1 - x if r & 4 else x, 1 - y if r & 2 else y, 1 - c if r & 1 else c)

        def index(p):
            return 4 * p[0] + 2 * p[1] + p[2]

        def mat_rows(ref, p):
            return ref.at[pl.ds(pl.multiple_of(p * MAT_PIECE, 8), MAT_PIECE), :]

        def vec_cols(ref, p):
            return ref.at[:, pl.ds(pl.multiple_of(p * VEC_PIECE, 128), VEC_PIECE)]

        own = [pltpu.make_async_copy(mat_rows(mat_ref, me), mat_in.at[0], loc_sems.at[0]),
               pltpu.make_async_copy(vec_cols(vec_ref, me), vec_in.at[0], loc_sems.at[1])]
        for cp in own:
            cp.start()
        scatter = []
        for r in range(1, 8):
            p = index(peer(r))
            scatter.append(_remote(mat_rows(mat_ref, p), mat_in.at[r], send_sems, recv_sems, (0, r), peer(r)))
            scatter.append(_remote(vec_cols(vec_ref, p), vec_in.at[r], send_sems, recv_sems, (1, r), peer(r)))
        for cp in scatter:
            cp.start()
        for cp in own:
            cp.wait()
        for cp in scatter:
            cp.wait_recv()
        mat_tot, vec_tot = mat_in[0], vec_in[0]
        for r in range(1, 8):
            mat_tot = mat_tot + mat_in[r]
            vec_tot = vec_tot + vec_in[r]
        mat_sum[...] = mat_tot
        vec_sum[...] = vec_tot
        keep = [pltpu.make_async_copy(mat_sum, mat_rows(mat_out, me), loc_sems.at[2]),
                pltpu.make_async_copy(vec_sum, vec_cols(vec_out, me), loc_sems.at[3])]
        for cp in keep:
            cp.start()
        spread = []
        for r in range(1, 8):
            spread.append(_remote(mat_sum, mat_rows(mat_out, me), send_sems, recv_sems, (2, r), peer(r)))
            spread.append(_remote(vec_sum, vec_cols(vec_out, me), send_sems, recv_sems, (3, r), peer(r)))
        for cp in spread:
            cp.start()
        for r in range(1, 8):
            p = index(peer(r))
            _remote(mat_sum, mat_rows(mat_out, p), send_sems, recv_sems, (2, r), peer(r)).wait_recv()
            _remote(vec_sum, vec_cols(vec_out, p), send_sems, recv_sems, (3, r), peer(r)).wait_recv()
        for cp in scatter + spread:
            cp.wait_send()
        for cp in keep:
            cp.wait()

    return pl.pallas_call(
        body, name="small_all_reduce",
        in_specs=[VMEM_SPEC, VMEM_SPEC], out_specs=[VMEM_SPEC, VMEM_SPEC],
        out_shape=[jax.ShapeDtypeStruct((MAT_ROWS, HD), F32), jax.ShapeDtypeStruct((VEC_ROWS, DR), F32)],
        scratch_shapes=[pltpu.VMEM((8, MAT_PIECE, HD), F32), pltpu.VMEM((8, VEC_ROWS, VEC_PIECE), F32),
                        pltpu.VMEM((MAT_PIECE, HD), F32), pltpu.VMEM((VEC_ROWS, VEC_PIECE), F32),
                        pltpu.SemaphoreType.DMA((4, 8)), pltpu.SemaphoreType.DMA((4, 8)), pltpu.SemaphoreType.DMA((4,))],
    )(dmat, dvec)


def _adam_small(grads, wmv):
    n = len(grads)

    def body(*refs):
        g_refs, rest = refs[:n], refs[n:]
        ins, outs = rest[:3 * n], rest[3 * n:]
        for i in range(n):
            d, m, v = _adamw(ins[3 * i][...], g_refs[i][...], ins[3 * i + 1][...], ins[3 * i + 2][...])
            outs[3 * i][...], outs[3 * i + 1][...], outs[3 * i + 2][...] = d, m, v

    flat = [a for t in wmv for a in t]
    return pl.pallas_call(
        body, name="adam_small",
        in_specs=[VMEM_SPEC] * (4 * n), out_specs=[VMEM_SPEC] * (3 * n),
        out_shape=[jax.ShapeDtypeStruct(a.shape, F32) for a in flat],
    )(*grads, *flat)


WEIGHT_NAMES = ("norm_mix", "w_in", "w_pool_grp", "pool_scale", "w_pool_out", "conv_w", "conv_b", "w_rg_a", "b_rg_a", "w_rg_x",
                "b_rg_x", "lru_lambda", "w_rnn_out", "w_o", "norm_ffn", "w_ffn_in", "w_ffn_out", "norm_final")


def kernel(x, norm_mix, w_in, w_pool_grp, pool_scale, w_pool_out, conv_w, conv_b, w_rg_a, b_rg_a, w_rg_x, b_rg_x, lru_lambda, w_rnn_out, w_o, norm_ffn, w_ffn_in, w_ffn_out, norm_final, loss_target, m_norm_mix, m_w_in, m_w_pool_grp, m_pool_scale, m_w_pool_out, m_conv_w, m_conv_b, m_w_rg_a, m_b_rg_a, m_w_rg_x, m_b_rg_x, m_lru_lambda, m_w_rnn_out, m_w_o, m_norm_ffn, m_w_ffn_in, m_w_ffn_out, m_norm_final, v_norm_mix, v_w_in, v_w_pool_grp, v_pool_scale, v_w_pool_out, v_conv_w, v_conv_b, v_w_rg_a, v_b_rg_a, v_w_rg_x, v_b_rg_x, v_lru_lambda, v_w_rnn_out, v_w_o, v_norm_ffn, v_w_ffn_in, v_w_ffn_out, v_norm_final):
    w = dict(norm_mix=norm_mix, w_in=w_in, w_pool_grp=w_pool_grp, pool_scale=pool_scale, w_pool_out=w_pool_out, conv_w=conv_w,
             conv_b=conv_b, w_rg_a=w_rg_a, b_rg_a=b_rg_a, w_rg_x=w_rg_x, b_rg_x=b_rg_x, lru_lambda=lru_lambda,
             w_rnn_out=w_rnn_out, w_o=w_o, norm_ffn=norm_ffn, w_ffn_in=w_ffn_in, w_ffn_out=w_ffn_out, norm_final=norm_final)
    m = dict(norm_mix=m_norm_mix, w_in=m_w_in, w_pool_grp=m_w_pool_grp, pool_scale=m_pool_scale, w_pool_out=m_w_pool_out,
             conv_w=m_conv_w, conv_b=m_conv_b, w_rg_a=m_w_rg_a, b_rg_a=m_b_rg_a, w_rg_x=m_w_rg_x, b_rg_x=m_b_rg_x,
             lru_lambda=m_lru_lambda, w_rnn_out=m_w_rnn_out, w_o=m_w_o, norm_ffn=m_norm_ffn, w_ffn_in=m_w_ffn_in,
             w_ffn_out=m_w_ffn_out, norm_final=m_norm_final)
    v = dict(norm_mix=v_norm_mix, w_in=v_w_in, w_pool_grp=v_w_pool_grp, pool_scale=v_pool_scale, w_pool_out=v_w_pool_out,
             conv_w=v_conv_w, conv_b=v_conv_b, w_rg_a=v_w_rg_a, b_rg_a=v_b_rg_a, w_rg_x=v_w_rg_x, b_rg_x=v_b_rg_x,
             lru_lambda=v_lru_lambda, w_rnn_out=v_w_rnn_out, w_o=v_w_o, norm_ffn=v_norm_ffn, w_ffn_in=v_w_ffn_in,
             w_ffn_out=v_w_ffn_out, norm_final=v_norm_final)
    xi, yi, ci = (lax.axis_index(a) for a in MESH_AXES)
    chip = 2 * xi + yi
    other = [2 * (1 - xi) + yi, 2 * xi + (1 - yi), 2 * (1 - xi) + (1 - yi)]

    gathered = _all_gather([_pad_to_chip_block(T, w[T.name][0], ci) for T in GATHERED])
    full = {T.name: a for T, a in zip(GATHERED, gathered)}
    wts = (full["w_in"], full["w_pool_out"], full["conv_w"][0:4], full["w_rnn_out"], full["w_o"], full["w_ffn_in"],
           full["w_ffn_out"])
    small = (norm_mix, w_pool_grp[0], pool_scale, conv_b, w_rg_a[0], b_rg_a[0], w_rg_x[0], b_rg_x[0], lru_lambda, norm_ffn,
             norm_final.reshape(1, D))

    loss_part, grad_x, big, dmat, dvec = _local_step(x[0], loss_target[0], wts, small)
    loss = lax.psum(loss_part[0, 0], MESH_AXES)

    grads, delta, new_m, new_v = {}, {}, {}, {}
    where4 = jnp.stack(other + [ci]).astype(jnp.int32)
    where2 = jnp.stack([chip, ci]).astype(jnp.int32)
    lz1 = _pair_exchange(big)
    sums = [_pair_sum(T, g, l, where4) for T, g, l in zip(BIG, big, lz1)]
    lz2 = _chip_scatter(sums)
    halves = []
    for T, g, l1, l2 in zip(BIG, big, lz1, lz2):
        if T.rowhalf:
            halves.append(_final_sum(T, g, l1, l2, where2)[0])
        else:
            n = T.name
            grads[n], delta[n], new_m[n], new_v[n] = _final_sum(T, g, l1, l2, where2, (w[n][0], m[n][0], v[n][0]))
    joined = _pair_join(halves)
    for T, j in zip([T for T in BIG if T.rowhalf], joined):
        n = T.name
        grads[n] = lax.dynamic_slice(j, (0, T.n * ci), (T.rows, T.n))
        delta[n], new_m[n], new_v[n] = _adam_rows(n, w[n][0], grads[n], m[n][0], v[n][0])

    mat, vec = _small_all_reduce(dmat, dvec)
    me = 4 * xi + 2 * yi + ci
    small_grads = dict(
        w_pool_grp=mat[0:MAT_WA], w_rg_a=mat[MAT_WA:MAT_WX], w_rg_x=mat[MAT_WX:MAT_ROWS],
        pool_scale=vec[VEC_SCALE:VEC_SCALE + 1, 0:DP], conv_b=vec[VEC_CONV_B:VEC_CONV_B + 1],
        b_rg_a=vec[VEC_BA:VEC_BA + 1], b_rg_x=vec[VEC_BX:VEC_BX + 1], lru_lambda=vec[VEC_LAM:VEC_LAM + 1],
        conv_w=lax.dynamic_slice(vec, (VEC_CONV_W, VEC_PIECE * me), (4, VEC_PIECE)),
        norm_final=vec[VEC_NORM_FINAL:VEC_NORM_FINAL + 1], norm_ffn=vec[VEC_NORM_FFN:VEC_NORM_FFN + 1],
        norm_mix=vec[VEC_NORM_MIX:VEC_NORM_MIX + 1])
    names = list(small_grads)
    as2d = lambda a, g: a.reshape(g.shape)
    upd = _adam_small([small_grads[n] for n in names],
                      [(as2d(w[n], small_grads[n]), as2d(m[n], small_grads[n]), as2d(v[n], small_grads[n])) for n in names])
    for i, n in enumerate(names):
        grads[n] = small_grads[n]
        delta[n], new_m[n], new_v[n] = upd[3 * i:3 * i + 3]

    shaped = lambda d: [d[n].reshape(w[n].shape) for n in WEIGHT_NAMES]
    return (loss, grad_x[None], *shaped(grads), *shaped(delta), *shaped(new_m), *shaped(new_v))
```

```python
import functools
import math

import jax
import jax.numpy as jnp
from jax import lax
from jax.experimental import pallas as pl
from jax.experimental.pallas import tpu as pltpu

F32 = jnp.float32
BF16 = jnp.bfloat16

D = 1024
DP = 512
PG = 128
WINDOWS = (2, 4, 8, 16)
DR = 1024
NH = 8
HD = 128
DIN = 4608
DFF = 2816
EPS = 1e-6
LRU_C = 8.0
POOL_HALO = 16
CONV_HALO = 8

ADAM_LR = 0.001
ADAM_B1 = 0.9
ADAM_B2 = 0.999
ADAM_EPS = 1e-08
ADAM_WD = 0.01
ADAM_STEP = 10

VMEM_LIMIT = 56 * 1024 * 1024
MESH_AXES = ("x", "y", "c")
MESH = pl.DeviceIdType.MESH


def _dot(a, b):
    return jnp.dot(a, b, preferred_element_type=F32)


def _dot_nt(a, b):
    return lax.dot_general(a, b, (((1,), (1,)), ((), ())), preferred_element_type=F32)


def _dot_tn(a, b):
    return lax.dot_general(a, b, (((0,), (0,)), ((), ())), preferred_element_type=F32)


def _params(*sem):
    return pltpu.CompilerParams(dimension_semantics=sem, vmem_limit_bytes=VMEM_LIMIT)


def _resident(shape):
    nd = len(shape)
    return pl.BlockSpec(shape, lambda i: (0,) * nd, pipeline_mode=pl.Buffered(1))


def _rows(shape_cols, tm):
    return pl.BlockSpec((tm, shape_cols), lambda i: (i, 0))


def _gelu(x):
    c = math.sqrt(2.0 / math.pi)
    t = jnp.tanh(c * (x + 0.044715 * x * x * x))
    return 0.5 * x * (1.0 + t), t


def _gelu_grad(x, t):
    c = math.sqrt(2.0 / math.pi)
    return 0.5 * (1.0 + t) + 0.5 * x * (1.0 - t * t) * (c * (1.0 + 3.0 * 0.044715 * x * x))


def _softplus_neg(lam):
    z = jnp.exp(-jnp.abs(lam))
    u = 1.0 + z
    dlt = u - 1.0
    log1p = jnp.where(dlt == 0.0, z, jnp.log(u) * (z / jnp.where(dlt == 0.0, 1.0, dlt)))
    return jnp.maximum(-lam, 0.0) + log1p


def _sigmoid(x):
    return 0.5 * jnp.tanh(0.5 * x) + 0.5


def _linear_scan(A, B, reverse):
    n, cols = A.shape
    rows = lax.broadcasted_iota(jnp.int32, (n, 1), 0)
    d = 1
    while d < n:
        if d < 8:
            keep = (rows < n - d) if reverse else (rows >= d)
            shift = n - d if reverse else d
            B = jnp.where(keep, A * pltpu.roll(B, shift, axis=0) + B, B)
            A = jnp.where(keep, A * pltpu.roll(A, shift, axis=0), A)
        else:
            def shifted(v, fill):
                pad = jnp.full((d, cols), fill, v.dtype)
                return jnp.concatenate([v[d:], pad] if reverse else [pad, v[:n - d]], axis=0)
            B = A * shifted(B, 0.0) + B
            A = A * shifted(A, 1.0)
        d *= 2
    return A, B


def _pool_windows(ext, shift_sign):
    n = ext.shape[0]
    s = ext
    outs = []
    for w in WINDOWS:
        d = w // 2
        s = s + pltpu.roll(s, d if shift_sign > 0 else n - d, axis=0)
        outs.append(s[:, :PG])
        s = s[:, PG:]
    return outs


def _conv_taps(uext):
    taps = []
    for k in range(4):
        sh = 3 - k
        v = uext if sh == 0 else pltpu.roll(uext, sh, axis=0)
        taps.append(v[CONV_HALO:, :])
    return taps


def _gates(v, wa_ref, ba_ref, wx_ref, bx_ref, sp):
    vb = v.astype(BF16)
    ra, rx = [], []
    for h in range(NH):
        vh = vb[:, h * HD:(h + 1) * HD]
        ra.append(_dot(vh, wa_ref[h]))
        rx.append(_dot(vh, wx_ref[h]))
    r = _sigmoid(jnp.concatenate(ra, axis=1) + ba_ref[...])
    i = _sigmoid(jnp.concatenate(rx, axis=1) + bx_ref[...])
    log_a = (-LRU_C) * r * sp
    a = jnp.exp(log_a)
    mult = jnp.sqrt(-jnp.tanh(log_a) * (1.0 + a * a))
    return r, i, a, mult


def _in_proj(x, norm_mix, w_in, tm=512):
    S = x.shape[0]

    def body(x_ref, g_ref, w_ref, proj_ref, h_ref):
        xv = x_ref[...]
        r = lax.rsqrt(jnp.mean(xv * xv, axis=-1, keepdims=True) + EPS)
        h = (xv * r * g_ref[...]).astype(BF16)
        h_ref[...] = h
        for n0 in range(0, DIN, 512):
            proj_ref[:, n0:n0 + 512] = _dot(h, w_ref[:, n0:n0 + 512])

    return pl.pallas_call(
        body, name="in_proj", grid=(S // tm,),
        in_specs=[_rows(D, tm), _resident((1, D)), _resident((D, DIN))],
        out_specs=[_rows(DIN, tm), _rows(D, tm)],
        out_shape=[jax.ShapeDtypeStruct((S, DIN), F32), jax.ShapeDtypeStruct((S, D), BF16)],
        compiler_params=_params("parallel"),
    )(x, norm_mix, w_in)


def _mixer_fwd(proj, wg, scale, w_pool_out, conv_w, conv_b, wa, ba, wx, bx, lam, w_rnn_out, tm=256):
    S = proj.shape[0]
    UW = DP + 2 * DR

    def body(proj_ref, wg_ref, scale_ref, wpo_ref, cw_ref, cb_ref, wa_ref, ba_ref, wx_ref, bx_ref, lam_ref, wro_ref,
             pooled_ref, pm_ref, ypool_ref, hr_ref, z_ref, yrnn_ref, pool_carry, conv_carry, h_carry):
        i = pl.program_id(0)

        @pl.when(i == 0)
        def _():
            pool_carry[...] = jnp.zeros_like(pool_carry)
            conv_carry[...] = jnp.zeros_like(conv_carry)
            h_carry[...] = jnp.zeros_like(h_carry)

        rows = lax.broadcasted_iota(jnp.int32, (tm, 1), 0)
        t_glob = i * tm + rows

        u_pool = proj_ref[:, 0:DP]
        ext = jnp.concatenate([pool_carry[...], u_pool], axis=0)
        pool_carry[...] = u_pool[tm - POOL_HALO:, :]
        sums = _pool_windows(ext, +1)
        mixed = []
        for g, w in enumerate(WINDOWS):
            cnt = jnp.minimum(t_glob + 1, w).astype(F32)
            pooled_g = sums[g][POOL_HALO:, :] / cnt - u_pool[:, g * PG:(g + 1) * PG]
            pooled_b = pooled_g.astype(BF16)
            pooled_ref[:, g * PG:(g + 1) * PG] = pooled_b
            mixed.append(_dot(pooled_b, wg_ref[g]))
        pm = (jnp.concatenate(mixed, axis=1) * scale_ref[...]).astype(BF16)
        pm_ref[...] = pm
        ypool_ref[...] = _dot(pm, wpo_ref[...])

        u_rnn = proj_ref[:, DP:DP + DR]
        uext = jnp.concatenate([conv_carry[...], u_rnn], axis=0)
        conv_carry[...] = u_rnn[tm - CONV_HALO:, :]
        taps = _conv_taps(uext)
        v = cb_ref[...]
        for k in range(4):
            v = v + taps[k] * cw_ref[k:k + 1, :]
        sp = _softplus_neg(lam_ref[...])
        _, gi, a, mult = _gates(v, wa_ref, ba_ref, wx_ref, bx_ref, sp)
        b = mult * gi * v
        A, B = _linear_scan(a, b, reverse=False)
        hr = A * h_carry[0:1, :] + B
        h_carry[0:1, :] = hr[tm - 1:tm, :]
        hr_ref[...] = hr
        gg, _ = _gelu(proj_ref[:, DP + DR:UW])
        z = (hr * gg).astype(BF16)
        z_ref[...] = z
        yrnn_ref[...] = _dot(z, wro_ref[...])

    return pl.pallas_call(
        body, name="mixer_fwd", grid=(S // tm,),
        in_specs=[_rows(UW, tm), _resident((4, PG, PG)), _resident((1, DP)), _resident((DP, D)), _resident((4, DR)),
                  _resident((1, DR)), _resident((NH, HD, HD)), _resident((1, DR)), _resident((NH, HD, HD)),
                  _resident((1, DR)), _resident((1, DR)), _resident((DR, D))],
        out_specs=[_rows(DP, tm), _rows(DP, tm), _rows(D, tm), _rows(DR, tm), _rows(DR, tm), _rows(D, tm)],
        out_shape=[jax.ShapeDtypeStruct((S, DP), BF16), jax.ShapeDtypeStruct((S, DP), BF16),
                   jax.ShapeDtypeStruct((S, D), F32), jax.ShapeDtypeStruct((S, DR), F32),
                   jax.ShapeDtypeStruct((S, DR), BF16), jax.ShapeDtypeStruct((S, D), F32)],
        scratch_shapes=[pltpu.VMEM((POOL_HALO, DP), F32), pltpu.VMEM((CONV_HALO, DR), F32), pltpu.VMEM((8, DR), F32)],
        compiler_params=_params("arbitrary"),
    )(proj, wg, scale, w_pool_out, conv_w, conv_b, wa, ba, wx, bx, lam, w_rnn_out)


FF_CHUNK = DFF // 4


def _rms(x):
    r = lax.rsqrt(jnp.mean(x * x, axis=-1, keepdims=True) + EPS)
    return r, x * r


def _rms_bwd(dh, g, r, xh):
    dxh = dh * g
    return r * (dxh - xh * jnp.mean(dxh * xh, axis=-1, keepdims=True))


def _merge_out(x, proj, y_pool, y_rnn, w_o, norm_ffn, tm=512):
    S = x.shape[0]
    GL0 = (DP + 2 * DR) // 512

    def gl_spec(k):
        return pl.BlockSpec((tm, 512), lambda i: (i, GL0 + k))

    def body(x_ref, gl0, gl1, gl2, gl3, yp_ref, yr_ref, wo_ref, gf_ref, mix_ref, x2_ref, h2_ref):
        s_p = _sigmoid(jnp.concatenate([gl0[...], gl1[...]], axis=1))
        s_r = _sigmoid(jnp.concatenate([gl2[...], gl3[...]], axis=1))
        mix = (s_p * yp_ref[...] + s_r * yr_ref[...]).astype(BF16)
        mix_ref[...] = mix
        x2 = x_ref[...] + _dot(mix, wo_ref[...])
        x2_ref[...] = x2
        _, xh2 = _rms(x2)
        h2_ref[...] = (xh2 * gf_ref[...]).astype(BF16)

    return pl.pallas_call(
        body, name="merge_out", grid=(S // tm,),
        in_specs=[_rows(D, tm), gl_spec(0), gl_spec(1), gl_spec(2), gl_spec(3), _rows(D, tm), _rows(D, tm),
                  _resident((D, D)), _resident((1, D))],
        out_specs=[_rows(D, tm), _rows(D, tm), _rows(D, tm)],
        out_shape=[jax.ShapeDtypeStruct((S, D), BF16), jax.ShapeDtypeStruct((S, D), F32), jax.ShapeDtypeStruct((S, D), BF16)],
        compiler_params=_params("parallel"),
    )(x, proj, proj, proj, proj, y_pool, y_rnn, w_o, norm_ffn)


def _ffn_up(h2, w_ffn_in, tm=512):
    S = h2.shape[0]

    def body(h_ref, w_ref, gu_ref, act_ref):
        h = h_ref[...]
        for c0 in range(0, DFF, FF_CHUNK):
            gate = _dot(h, w_ref[:, c0:c0 + FF_CHUNK])
            up = _dot(h, w_ref[:, DFF + c0:DFF + c0 + FF_CHUNK])
            gu_ref[:, c0:c0 + FF_CHUNK] = gate.astype(BF16)
            gu_ref[:, DFF + c0:DFF + c0 + FF_CHUNK] = up.astype(BF16)
            act_ref[:, c0:c0 + FF_CHUNK] = (gate * _sigmoid(gate) * up).astype(BF16)

    return pl.pallas_call(
        body, name="ffn_up", grid=(S // tm,),
        in_specs=[_rows(D, tm), _resident((D, 2 * DFF))],
        out_specs=[_rows(2 * DFF, tm), _rows(DFF, tm)],
        out_shape=[jax.ShapeDtypeStruct((S, 2 * DFF), BF16), jax.ShapeDtypeStruct((S, DFF), BF16)],
        compiler_params=_params("parallel"),
    )(h2, w_ffn_in)


def _ffn_down_loss(act, x2, target, w_ffn_out, norm_final, tm=512):
    S = act.shape[0]

    def body(act_ref, x2_ref, t_ref, w_ref, gn_ref, dx3_ref, dx3b_ref, loss_ref, dvec_ref):
        i = pl.program_id(0)

        @pl.when(i == 0)
        def _():
            loss_ref[...] = jnp.zeros_like(loss_ref)
            dvec_ref[...] = jnp.zeros_like(dvec_ref)

        x3 = x2_ref[...] + _dot(act_ref[...], w_ref[...])
        r3, xh3 = _rms(x3)
        g_fin = gn_ref[...]
        e = xh3 * g_fin - t_ref[...]
        loss_ref[...] += jnp.sum(e * e, axis=(0, 1), keepdims=True) * (0.5 / D)
        dy = e * (1.0 / D)
        dvec_ref[0:1, :] += jnp.sum(dy * xh3, axis=0, keepdims=True)
        dx3 = _rms_bwd(dy, g_fin, r3, xh3)
        dx3_ref[...] = dx3
        dx3b_ref[...] = dx3.astype(BF16)

    return pl.pallas_call(
        body, name="ffn_down_loss", grid=(S // tm,),
        in_specs=[_rows(DFF, tm), _rows(D, tm), _rows(D, tm), _resident((DFF, D)), _resident((1, D))],
        out_specs=[_rows(D, tm), _rows(D, tm), _resident((1, 1)), _resident((8, D))],
        out_shape=[jax.ShapeDtypeStruct((S, D), F32), jax.ShapeDtypeStruct((S, D), BF16),
                   jax.ShapeDtypeStruct((1, 1), F32), jax.ShapeDtypeStruct((8, D), F32)],
        compiler_params=_params("arbitrary"),
    )(act, x2, target, w_ffn_out, norm_final)


def _ffn_bwd_down(dx3b, gu, w_ffn_out, tm=512):
    S = dx3b.shape[0]

    def body(d_ref, gu_ref, w_ref, dgu_ref):
        d = d_ref[...]
        for c0 in range(0, DFF, FF_CHUNK):
            dact = _dot_nt(d, w_ref[c0:c0 + FF_CHUNK, :])
            gate = gu_ref[:, c0:c0 + FF_CHUNK].astype(F32)
            up = gu_ref[:, DFF + c0:DFF + c0 + FF_CHUNK].astype(F32)
            sg = _sigmoid(gate)
            dgu_ref[:, c0:c0 + FF_CHUNK] = (dact * up * (sg * (1.0 + gate * (1.0 - sg)))).astype(BF16)
            dgu_ref[:, DFF + c0:DFF + c0 + FF_CHUNK] = (dact * (gate * sg)).astype(BF16)

    return pl.pallas_call(
        body, name="ffn_bwd_down", grid=(S // tm,),
        in_specs=[_rows(D, tm), _rows(2 * DFF, tm), _resident((DFF, D))],
        out_specs=_rows(2 * DFF, tm),
        out_shape=jax.ShapeDtypeStruct((S, 2 * DFF), BF16),
        compiler_params=_params("parallel"),
    )(dx3b, gu, w_ffn_out)


def _ffn_bwd_up(dgu, x2, dx3, w_ffn_in, norm_ffn, w_o, tm=512):
    S = dgu.shape[0]

    def body(dgu_ref, x2_ref, dx3_ref, wfi_ref, gf_ref, wo_ref, dx2_ref, dx2b_ref, dmixo_ref, dvec_ref):
        i = pl.program_id(0)

        @pl.when(i == 0)
        def _():
            dvec_ref[...] = jnp.zeros_like(dvec_ref)

        dh2 = _dot_nt(dgu_ref[:, 0:DFF], wfi_ref[:, 0:DFF]) + _dot_nt(dgu_ref[:, DFF:2 * DFF], wfi_ref[:, DFF:2 * DFF])
        r2, xh2 = _rms(x2_ref[...])
        dvec_ref[0:1, :] += jnp.sum(dh2 * xh2, axis=0, keepdims=True)
        dx2 = dx3_ref[...] + _rms_bwd(dh2, gf_ref[...], r2, xh2)
        dx2_ref[...] = dx2
        dx2b = dx2.astype(BF16)
        dx2b_ref[...] = dx2b
        dmixo_ref[...] = _dot_nt(dx2b, wo_ref[...])

    return pl.pallas_call(
        body, name="ffn_bwd_up", grid=(S // tm,),
        in_specs=[_rows(2 * DFF, tm), _rows(D, tm), _rows(D, tm), _resident((D, 2 * DFF)), _resident((1, D)),
                  _resident((D, D))],
        out_specs=[_rows(D, tm), _rows(D, tm), _rows(D, tm), _resident((8, D))],
        out_shape=[jax.ShapeDtypeStruct((S, D), F32), jax.ShapeDtypeStruct((S, D), BF16), jax.ShapeDtypeStruct((S, D), F32),
                   jax.ShapeDtypeStruct((8, D), F32)],
        compiler_params=_params("arbitrary"),
    )(dgu, x2, dx3, w_ffn_in, norm_ffn, w_o)


VEC_ROWS = 16
MAT_WA = 4 * PG
MAT_WX = MAT_WA + NH * HD
MAT_ROWS = MAT_WX + NH * HD


def _mixer_bwd(proj, dmixo, y_pool, y_rnn, hr, wg, scale, w_pool_out, conv_w, conv_b, wa, ba, wx, bx, lam, w_rnn_out,
               tm=256):
    S = proj.shape[0]
    nt = S // tm

    def rev(cols):
        return pl.BlockSpec((tm, cols), lambda i: (nt - 1 - i, 0))

    def halo(rows_, cols):
        per = tm // rows_
        return pl.BlockSpec((rows_, cols), lambda i: (jnp.maximum((nt - 1 - i) * per - 1, 0), 0))

    def body(proj_ref, projh_ref, dmixo_ref, yp_ref, yr_ref, hr_ref, hrh_ref, wg_ref, scale_ref, wpo_ref, cw_ref, cb_ref,
             wa_ref, ba_ref, wx_ref, bx_ref, lam_ref, wro_ref,
             dproj_ref, dypb_ref, dyrb_ref, dmat_ref, dvec_ref,
             q_carry, dv_carry, a_carry, g_carry):
        i = pl.program_id(0)
        ti = nt - 1 - i

        @pl.when(i == 0)
        def _():
            q_carry[...] = jnp.zeros_like(q_carry)
            dv_carry[...] = jnp.zeros_like(dv_carry)
            a_carry[...] = jnp.zeros_like(a_carry)
            g_carry[...] = jnp.zeros_like(g_carry)
            dmat_ref[...] = jnp.zeros_like(dmat_ref)
            dvec_ref[...] = jnp.zeros_like(dvec_ref)

        rows = lax.broadcasted_iota(jnp.int32, (tm, 1), 0)
        t_glob = ti * tm + rows
        has_prev = (ti > 0).astype(F32)
        dmixo = dmixo_ref[...]

        s_p = _sigmoid(proj_ref[:, DP + 2 * DR:DP + 2 * DR + D])
        s_r = _sigmoid(proj_ref[:, DP + 2 * DR + D:DIN])
        dproj_ref[:, DP + 2 * DR:DP + 2 * DR + D] = (dmixo * yp_ref[...] * s_p * (1.0 - s_p)).astype(BF16)
        dproj_ref[:, DP + 2 * DR + D:DIN] = (dmixo * yr_ref[...] * s_r * (1.0 - s_r)).astype(BF16)
        dyp = (dmixo * s_p).astype(BF16)
        dyr = (dmixo * s_r).astype(BF16)
        dypb_ref[...] = dyp
        dyrb_ref[...] = dyr

        dz = _dot_nt(dyr, wro_ref[...])
        u_gate = proj_ref[:, DP + DR:DP + 2 * DR]
        gg, tg = _gelu(u_gate)
        hr_t = hr_ref[...]
        dproj_ref[:, DP + DR:DP + 2 * DR] = (dz * hr_t * _gelu_grad(u_gate, tg)).astype(BF16)
        dhr = dz * gg

        u_rnn = proj_ref[:, DP:DP + DR]
        uext = jnp.concatenate([projh_ref[POOL_HALO - CONV_HALO:, DP:DP + DR] * has_prev, u_rnn], axis=0)
        taps = _conv_taps(uext)
        v = cb_ref[...]
        for k in range(4):
            v = v + taps[k] * cw_ref[k:k + 1, :]
        sp = _softplus_neg(lam_ref[...])
        r, gi, a, mult = _gates(v, wa_ref, ba_ref, wx_ref, bx_ref, sp)

        C = jnp.where(rows == tm - 1, a_carry[0:1, :], pltpu.roll(a, tm - 1, axis=0))
        C, G = _linear_scan(C, dhr, reverse=True)
        g = G + C * g_carry[0:1, :]
        a_carry[0:1, :] = a[0:1, :]
        g_carry[0:1, :] = g[0:1, :]

        h_prev = jnp.where(rows == 0, hrh_ref[7:8, :] * has_prev, pltpu.roll(hr_t, 1, axis=0))
        da = g * h_prev
        dmult = g * gi * v
        di = g * mult * v
        dv = g * mult * gi
        dlog_a = da * a - dmult * (a * a / mult)
        dvec_ref[4:5, :] += jnp.sum(dlog_a * r, axis=0, keepdims=True)
        dra = (dlog_a * ((-LRU_C) * sp) * r * (1.0 - r))
        drx = di * gi * (1.0 - gi)
        dvec_ref[2:3, :] += jnp.sum(dra, axis=0, keepdims=True)
        dvec_ref[3:4, :] += jnp.sum(drx, axis=0, keepdims=True)
        drab = dra.astype(BF16)
        drxb = drx.astype(BF16)
        vb = v.astype(BF16)
        dvg = []
        for h in range(NH):
            sl = slice(h * HD, (h + 1) * HD)
            dvg.append(_dot_nt(drab[:, sl], wa_ref[h]) + _dot_nt(drxb[:, sl], wx_ref[h]))
            dmat_ref[MAT_WA + h * HD:MAT_WA + (h + 1) * HD, :] += _dot_tn(vb[:, sl], drab[:, sl])
            dmat_ref[MAT_WX + h * HD:MAT_WX + (h + 1) * HD, :] += _dot_tn(vb[:, sl], drxb[:, sl])
        dv = dv + jnp.concatenate(dvg, axis=1)
        dvec_ref[1:2, :] += jnp.sum(dv, axis=0, keepdims=True)
        for k in range(4):
            dvec_ref[5 + k:6 + k, :] += jnp.sum(dv * taps[k], axis=0, keepdims=True)
        dvext = jnp.concatenate([dv, dv_carry[...]], axis=0)
        dv_carry[...] = dv[0:CONV_HALO, :]
        n = tm + CONV_HALO
        du_rnn = dv * cw_ref[3:4, :]
        for k in range(3):
            du_rnn = du_rnn + pltpu.roll(dvext, n - (3 - k), axis=0)[0:tm, :] * cw_ref[k:k + 1, :]
        dproj_ref[:, DP:DP + DR] = du_rnn.astype(BF16)

        dpm = _dot_nt(dyp, wpo_ref[...])
        u_pool = proj_ref[:, 0:DP]
        ext = jnp.concatenate([projh_ref[:, 0:DP] * has_prev, u_pool], axis=0)
        sums = _pool_windows(ext, +1)
        scale_v = scale_ref[...]
        qs = []
        dpooled = []
        dscale = []
        for gi_, w in enumerate(WINDOWS):
            sl = slice(gi_ * PG, (gi_ + 1) * PG)
            cnt = jnp.minimum(t_glob + 1, w).astype(F32)
            pooled_b = (sums[gi_][POOL_HALO:, :] / cnt - u_pool[:, sl]).astype(BF16)
            mixed_g = _dot(pooled_b, wg_ref[gi_])
            dscale.append(jnp.sum(dpm[:, sl] * mixed_g, axis=0, keepdims=True))
            dmixed_b = (dpm[:, sl] * scale_v[:, sl]).astype(BF16)
            dmat_ref[gi_ * PG:(gi_ + 1) * PG, :] += _dot_tn(pooled_b, dmixed_b)
            dp_g = _dot_nt(dmixed_b, wg_ref[gi_])
            dpooled.append(dp_g)
            qs.append(dp_g / cnt)
        dvec_ref[0:1, 0:DP] += jnp.concatenate(dscale, axis=1)
        q = jnp.concatenate(qs, axis=1)
        qext = jnp.concatenate([q, q_carry[...]], axis=0)
        q_carry[...] = q[0:POOL_HALO, :]
        tsum = _pool_windows(qext, -1)
        for gi_ in range(4):
            dproj_ref[:, gi_ * PG:(gi_ + 1) * PG] = (tsum[gi_][0:tm, :] - dpooled[gi_]).astype(BF16)

        @pl.when(i == nt - 1)
        def _():
            dvec_ref[4:5, :] = dvec_ref[4:5, :] * (LRU_C * _sigmoid(-lam_ref[...]))

    return pl.pallas_call(
        body, name="mixer_bwd", grid=(nt,),
        in_specs=[rev(DIN), halo(POOL_HALO, DIN), rev(D), rev(D), rev(D), rev(DR), halo(8, DR),
                  _resident((4, PG, PG)), _resident((1, DP)), _resident((DP, D)), _resident((4, DR)), _resident((1, DR)),
                  _resident((NH, HD, HD)), _resident((1, DR)), _resident((NH, HD, HD)), _resident((1, DR)),
                  _resident((1, DR)), _resident((DR, D))],
        out_specs=[rev(DIN), rev(D), rev(D), _resident((MAT_ROWS, HD)), _resident((VEC_ROWS, DR))],
        out_shape=[jax.ShapeDtypeStruct((S, DIN), BF16), jax.ShapeDtypeStruct((S, D), BF16),
                   jax.ShapeDtypeStruct((S, D), BF16), jax.ShapeDtypeStruct((MAT_ROWS, HD), F32),
                   jax.ShapeDtypeStruct((VEC_ROWS, DR), F32)],
        scratch_shapes=[pltpu.VMEM((POOL_HALO, DP), F32), pltpu.VMEM((CONV_HALO, DR), F32), pltpu.VMEM((8, DR), F32),
                        pltpu.VMEM((8, DR), F32)],
        compiler_params=_params("arbitrary"),
    )(proj, proj, dmixo, y_pool, y_rnn, hr, hr, wg, scale, w_pool_out, conv_w, conv_b, wa, ba, wx, bx, lam, w_rnn_out)


def _in_bwd(dproj, x, dx2, norm_mix, w_in, tm=512):
    S = x.shape[0]

    def body(dp_ref, x_ref, dx2_ref, g_ref, w_ref, dx_ref, dg_ref):
        i = pl.program_id(0)

        @pl.when(i == 0)
        def _():
            dg_ref[...] = jnp.zeros_like(dg_ref)

        dh = _dot_nt(dp_ref[:, 0:1536], w_ref[:, 0:1536])
        dh = dh + _dot_nt(dp_ref[:, 1536:3072], w_ref[:, 1536:3072])
        dh = dh + _dot_nt(dp_ref[:, 3072:DIN], w_ref[:, 3072:DIN])
        xv = x_ref[...]
        r = lax.rsqrt(jnp.mean(xv * xv, axis=-1, keepdims=True) + EPS)
        xh = xv * r
        dg_ref[0:1, :] += jnp.sum(dh * xh, axis=0, keepdims=True)
        dxh = dh * g_ref[...]
        dx_ref[...] = dx2_ref[...] + r * (dxh - xh * jnp.mean(dxh * xh, axis=-1, keepdims=True))

    return pl.pallas_call(
        body, name="in_bwd", grid=(S // tm,),
        in_specs=[_rows(DIN, tm), _rows(D, tm), _rows(D, tm), _resident((1, D)), _resident((D, DIN))],
        out_specs=[_rows(D, tm), _resident((8, D))],
        out_shape=[jax.ShapeDtypeStruct((S, D), F32), jax.ShapeDtypeStruct((8, D), F32)],
        compiler_params=_params("arbitrary"),
    )(dproj, x, dx2, norm_mix, w_in)


def _wgrad(a, b, name, tk, tn):
    S, K = a.shape
    N = b.shape[1]

    def body(a_ref, b_ref, o_ref):
        o_ref[...] = _dot_tn(a_ref[...], b_ref[...]).astype(BF16)

    return pl.pallas_call(
        body, name=name, grid=(K // tk, N // tn),
        in_specs=[pl.BlockSpec((S, tk), lambda k, n: (0, k)), pl.BlockSpec((S, tn), lambda k, n: (0, n))],
        out_specs=pl.BlockSpec((tk, tn), lambda k, n: (k, n)),
        out_shape=jax.ShapeDtypeStruct((K, N), BF16),
        compiler_params=_params("parallel", "parallel"),
    )(a, b)


VEC_SCALE, VEC_CONV_B, VEC_BA, VEC_BX, VEC_LAM, VEC_CONV_W, VEC_NORM_FINAL, VEC_NORM_FFN, VEC_NORM_MIX = 0, 1, 2, 3, 4, 5, 9, 10, 11


def _local_step(x, target, wts, small):
    w_in, w_pool_out, conv_w, w_rnn_out, w_o, w_ffn_in, w_ffn_out = wts
    norm_mix, wg, scale, conv_b, wa, ba, wx, bx, lam, norm_ffn, norm_final = small
    wg_b, wa_b, wx_b = wg.astype(BF16), wa.astype(BF16), wx.astype(BF16)
    ba2, bx2 = ba.reshape(1, DR), bx.reshape(1, DR)
    proj, h1 = _in_proj(x, norm_mix, w_in)
    _, pm, y_pool, hr, z, y_rnn = _mixer_fwd(proj, wg_b, scale, w_pool_out, conv_w, conv_b, wa_b, ba2, wx_b, bx2, lam,
                                             w_rnn_out)
    mix, x2, h2 = _merge_out(x, proj, y_pool, y_rnn, w_o, norm_ffn)
    gu, act = _ffn_up(h2, w_ffn_in)
    dx3, dx3b, loss, dvec_fin = _ffn_down_loss(act, x2, target, w_ffn_out, norm_final)
    dgu = _ffn_bwd_down(dx3b, gu, w_ffn_out)
    dx2, dx2b, dmixo, dvec_ffn = _ffn_bwd_up(dgu, x2, dx3, w_ffn_in, norm_ffn, w_o)
    dproj, dypb, dyrb, dmat, dvec_mix = _mixer_bwd(proj, dmixo, y_pool, y_rnn, hr, wg_b, scale, w_pool_out, conv_w,
                                                   conv_b, wa_b, ba2, wx_b, bx2, lam, w_rnn_out)
    grad_x, dvec_in = _in_bwd(dproj, x, dx2, norm_mix, w_in)
    big = (
        _wgrad(h1, dproj, "wgrad_in", 1024, 1152),
        _wgrad(pm, dypb, "wgrad_pool_out", 512, 1024),
        _wgrad(z, dyrb, "wgrad_rnn_out", 1024, 1024),
        _wgrad(mix, dx2b, "wgrad_o", 1024, 1024),
        _wgrad(h2, dgu, "wgrad_ffn_in", 512, 1408),
        _wgrad(act, dx3b, "wgrad_ffn_out", 1408, 512),
    )
    dvec = jnp.concatenate([dvec_mix[0:9], dvec_fin[0:1], dvec_ffn[0:1], dvec_in[0:1], jnp.zeros((VEC_ROWS - 12, DR), F32)],
                           axis=0)
    return loss, grad_x, big, dmat, dvec


class _Big:
    def __init__(self, name, rows, cols, axis, n, dtype=BF16):
        self.name, self.rows, self.cols, self.axis, self.n, self.dtype = name, rows, cols, axis, n, dtype
        self.align = 16 if dtype == BF16 else 8
        self.rowhalf = axis == 1 and n % 128 != 0
        self.cb_shape = (rows, 2 * n) if axis == 1 else (2 * n, cols)

    def ag_block(self, ref, j):
        if self.axis == 1:
            return ref.at[:, pl.ds(pl.multiple_of(j * 2 * self.n, 128), 2 * self.n)]
        return ref.at[pl.ds(pl.multiple_of(j * 2 * self.n, self.align), 2 * self.n), :]

    def ag_half(self, ref, j, h):
        if self.axis == 1:
            hr = self.rows // 2
            return ref.at[pl.ds(pl.multiple_of(h * hr, self.align), hr),
                          pl.ds(pl.multiple_of(j * 2 * self.n, 128), 2 * self.n)]
        return ref.at[pl.ds(pl.multiple_of(j * 2 * self.n + h * self.n, self.align), self.n), :]

    def cb_half(self, ref, h):
        hr = self.cb_shape[0] // 2
        return ref.at[pl.ds(pl.multiple_of(h * hr, self.align), hr), :]

    @property
    def piece_shape(self):
        if self.rowhalf:
            return (self.rows // 2, 2 * self.n)
        return (self.rows, self.n) if self.axis == 1 else (self.n, self.cols)

    @property
    def sub_blocks(self):
        return 4 if self.rowhalf else 1

    @property
    def block(self):
        r, c = self.piece_shape
        return (r // self.sub_blocks, c)

    def block_index(self, j, h, r):
        if self.rowhalf:
            return (h * self.sub_blocks + r, j)
        return (0, 2 * j + h) if self.axis == 1 else (2 * j + h, 0)

    def piece(self, ref, j, h):
        if self.rowhalf:
            return self.ag_half(ref, j, h)
        if self.axis == 1:
            return ref.at[:, pl.ds(pl.multiple_of((2 * j + h) * self.n, 128), self.n)]
        return ref.at[pl.ds(pl.multiple_of((2 * j + h) * self.n, self.align), self.n), :]


BIG = (_Big("w_in", D, DIN, 1, DIN // 8), _Big("w_pool_out", DP, D, 1, D // 8), _Big("w_rnn_out", DR, D, 0, DR // 8),
       _Big("w_o", D, D, 0, D // 8), _Big("w_ffn_in", D, 2 * DFF, 1, 2 * DFF // 8), _Big("w_ffn_out", DFF, D, 0, DFF // 8))
CONV_W = _Big("conv_w", 16, DR, 1, DR // 8, F32)
GATHERED = BIG + (CONV_W,)

HBM_SPEC = pl.BlockSpec(memory_space=pl.ANY)
VMEM_SPEC = pl.BlockSpec(memory_space=pltpu.VMEM)


def _place():
    x, y, c = (lax.axis_index(a) for a in MESH_AXES)
    other_chips = [(1 - x, y), (x, 1 - y), (1 - x, 1 - y)]
    return x, y, c, other_chips


def _remote(src, dst, send_sems, recv_sems, idx, to):
    return pltpu.make_async_remote_copy(src_ref=src, dst_ref=dst, send_sem=send_sems.at[idx], recv_sem=recv_sems.at[idx],
                                        device_id=to, device_id_type=MESH)


def _pad_to_chip_block(t, w, c):
    w = w.astype(t.dtype)
    start = (0, t.n * c) if t.axis == 1 else (t.n * c, 0)
    return lax.dynamic_update_slice(jnp.zeros(t.cb_shape, t.dtype), w, start)


def _add_into(dst_ref, src_ref):
    rows = dst_ref.shape[0]
    ch = 64 if rows % 64 == 0 else rows

    def step(i, carry):
        sl = pl.ds(pl.multiple_of(i * ch, ch), ch)
        dst_ref[sl, :] = (dst_ref[sl, :].astype(F32) + src_ref[sl, :].astype(F32)).astype(dst_ref.dtype)
        return carry

    lax.fori_loop(0, rows // ch, step, 0)


def _all_gather(padded):
    nt = len(GATHERED)

    def body(*refs):
        ins, outs, mine, sib = (refs[k * nt:(k + 1) * nt] for k in range(4))
        send_sems, recv_sems, loc_sems = refs[4 * nt:]
        x, y, c, chips = _place()
        sibling = (x, y, 1 - c)
        m = 2 * x + y
        loads, swaps = [], []
        for t in range(nt):
            loads.append(pltpu.make_async_copy(ins[t], mine[t], loc_sems.at[t, 0]))
            swaps.append(_remote(ins[t], sib[t], send_sems, recv_sems, (t, 0), sibling))
            loads[t].start()
            swaps[t].start()
        stores, sends = [], []
        for t, T in enumerate(GATHERED):
            loads[t].wait()
            swaps[t].wait_recv()
            _add_into(mine[t], sib[t])
            stores.append(pltpu.make_async_copy(mine[t], T.ag_block(outs[t], m), loc_sems.at[t, 1]))
            stores[t].start()
            for k, chip in enumerate(chips):
                cp = _remote(T.cb_half(mine[t], c), T.ag_half(outs[t], m, c), send_sems, recv_sems, (t, 1 + k), (*chip, c))
                cp.start()
                sends.append(cp)
        passed = []
        for t, T in enumerate(GATHERED):
            for k, chip in enumerate(chips):
                land = T.ag_half(outs[t], 2 * chip[0] + chip[1], c)
                _remote(land, land, send_sems, recv_sems, (t, 1 + k), sibling).wait_recv()
                cp = _remote(land, land, send_sems, recv_sems, (t, 4 + k), sibling)
                cp.start()
                passed.append(cp)
        for t, T in enumerate(GATHERED):
            for k, chip in enumerate(chips):
                land = T.ag_half(outs[t], 2 * chip[0] + chip[1], 1 - c)
                _remote(land, land, send_sems, recv_sems, (t, 4 + k), sibling).wait_recv()
        for cp in swaps + sends + passed:
            cp.wait_send()
        for st in stores:
            st.wait()

    return pl.pallas_call(
        body, name="all_gather_weights",
        in_specs=[HBM_SPEC] * nt, out_specs=[HBM_SPEC] * nt,
        out_shape=[jax.ShapeDtypeStruct((T.rows, T.cols), T.dtype) for T in GATHERED],
        scratch_shapes=[pltpu.VMEM(T.cb_shape, T.dtype) for T in GATHERED] * 2
        + [pltpu.SemaphoreType.DMA((nt, 7)), pltpu.SemaphoreType.DMA((nt, 7)), pltpu.SemaphoreType.DMA((nt, 2))],
        compiler_params=pltpu.CompilerParams(vmem_limit_bytes=VMEM_LIMIT),
    )(*padded)


def _pair_exchange(grads):
    nt = len(BIG)

    def body(*refs):
        ins, outs = refs[:nt], refs[nt:2 * nt]
        send_sems, recv_sems = refs[2 * nt:]
        x, y, c, _ = _place()
        cps = []
        for t, T in enumerate(BIG):
            for j in range(4):
                cp = _remote(T.piece(ins[t], j, 1 - c), outs[t].at[j], send_sems, recv_sems, (t, j), (x, y, 1 - c))
                cp.start()
                cps.append(cp)
        for cp in cps:
            cp.wait()

    return pl.pallas_call(
        body, name="grad_pair_exchange",
        in_specs=[HBM_SPEC] * nt, out_specs=[HBM_SPEC] * nt,
        out_shape=[jax.ShapeDtypeStruct((4,) + T.piece_shape, BF16) for T in BIG],
        scratch_shapes=[pltpu.SemaphoreType.DMA((nt, 4)), pltpu.SemaphoreType.DMA((nt, 4))],
    )(*grads)


def _pair_sum(T, g, lz1, where):
    blk = T.block

    def body(where_ref, g_ref, l_ref, o_ref):
        o_ref[...] = (g_ref[...].astype(F32) + l_ref[...].astype(F32)).astype(BF16)

    return pl.pallas_call(
        body, name="grad_pair_sum_" + T.name,
        grid_spec=pltpu.PrefetchScalarGridSpec(
            num_scalar_prefetch=1, grid=(3, T.sub_blocks),
            in_specs=[pl.BlockSpec(blk, lambda k, r, w: T.block_index(w[k], w[3], r)),
                      pl.BlockSpec((None,) + blk, lambda k, r, w: (w[k], r, 0))],
            out_specs=pl.BlockSpec((None,) + blk, lambda k, r, w: (k, r, 0))),
        out_shape=jax.ShapeDtypeStruct((3,) + T.piece_shape, BF16),
        compiler_params=_params("arbitrary", "arbitrary"),
    )(where, g, lz1)


def _chip_scatter(sums):
    nt = len(BIG)

    def body(*refs):
        ins, outs = refs[:nt], refs[nt:2 * nt]
        send_sems, recv_sems = refs[2 * nt:]
        x, y, c, chips = _place()
        cps = []
        for t in range(nt):
            for k, chip in enumerate(chips):
                cp = _remote(ins[t].at[k], outs[t].at[k], send_sems, recv_sems, (t, k), (*chip, c))
                cp.start()
                cps.append(cp)
        for cp in cps:
            cp.wait()

    return pl.pallas_call(
        body, name="grad_chip_scatter",
        in_specs=[HBM_SPEC] * nt, out_specs=[HBM_SPEC] * nt,
        out_shape=[jax.ShapeDtypeStruct((3,) + T.piece_shape, BF16) for T in BIG],
        scratch_shapes=[pltpu.SemaphoreType.DMA((nt, 3)), pltpu.SemaphoreType.DMA((nt, 3))],
    )(*sums)


def _adamw(w, g, m, v):
    m = ADAM_B1 * m + (1.0 - ADAM_B1) * g
    v = ADAM_B2 * v + (1.0 - ADAM_B2) * (g * g)
    m_hat = m / (1.0 - ADAM_B1 ** ADAM_STEP)
    v_hat = v / (1.0 - ADAM_B2 ** ADAM_STEP)
    delta = -ADAM_LR * (m_hat / (jnp.sqrt(v_hat) + ADAM_EPS) + ADAM_WD * w)
    return delta, m, v


def _final_sum(T, g, lz1, lz2, where, wmv=None):
    blk = T.block
    n_out = 1 if wmv is None else 4

    def body(where_ref, g_ref, l1_ref, l2_ref, *rest):
        tot = g_ref[...].astype(F32) + l1_ref[...].astype(F32)
        for k in range(3):
            tot = tot + l2_ref[k].astype(F32)
        if wmv is None:
            rest[0][...] = tot
        else:
            w_ref, m_ref, v_ref, g_out, d_out, m_out, v_out = rest
            g_out[...] = tot
            d_out[...], m_out[...], v_out[...] = _adamw(w_ref[...], tot, m_ref[...], v_ref[...])

    own = pl.BlockSpec(blk, lambda r, w: (r, 0))
    return pl.pallas_call(
        body, name="grad_final_" + T.name,
        grid_spec=pltpu.PrefetchScalarGridSpec(
            num_scalar_prefetch=1, grid=(T.sub_blocks,),
            in_specs=[pl.BlockSpec(blk, lambda r, w: T.block_index(w[0], w[1], r)),
                      pl.BlockSpec((None,) + blk, lambda r, w: (w[0], r, 0)),
                      pl.BlockSpec((3,) + blk, lambda r, w: (0, r, 0))] + ([] if wmv is None else [own] * 3),
            out_specs=[own] * n_out),
        out_shape=[jax.ShapeDtypeStruct(T.piece_shape, F32)] * n_out,
        compiler_params=_params("arbitrary"),
    )(where, g, lz1, lz2, *(() if wmv is None else wmv))


def _pair_join(halves):
    nt = len(halves)

    def body(*refs):
        ins, outs = refs[:nt], refs[nt:2 * nt]
        send_sems, recv_sems, loc_sems = refs[2 * nt:]
        x, y, c, _ = _place()
        cps, loc = [], []
        for t in range(nt):
            hr = ins[t].shape[0]
            mine = outs[t].at[pl.ds(pl.multiple_of(c * hr, 8), hr), :]
            loc.append(pltpu.make_async_copy(ins[t], mine, loc_sems.at[t]))
            cps.append(_remote(ins[t], mine, send_sems, recv_sems, (t,), (x, y, 1 - c)))
            loc[t].start()
            cps[t].start()
        for t in range(nt):
            hr = ins[t].shape[0]
            theirs = outs[t].at[pl.ds(pl.multiple_of((1 - c) * hr, 8), hr), :]
            _remote(theirs, theirs, send_sems, recv_sems, (t,), (x, y, 1 - c)).wait_recv()
            cps[t].wait_send()
            loc[t].wait()

    return pl.pallas_call(
        body, name="grad_pair_join",
        in_specs=[VMEM_SPEC] * nt, out_specs=[VMEM_SPEC] * nt,
        out_shape=[jax.ShapeDtypeStruct((2 * h.shape[0], h.shape[1]), F32) for h in halves],
        scratch_shapes=[pltpu.SemaphoreType.DMA((nt,)), pltpu.SemaphoreType.DMA((nt,)), pltpu.SemaphoreType.DMA((nt,))],
        compiler_params=pltpu.CompilerParams(vmem_limit_bytes=VMEM_LIMIT),
    )(*halves)


def _adam_rows(name, w, g, m, v, tm=256):
    rows, cols = w.shape

    def body(w_ref, g_ref, m_ref, v_ref, d_out, m_out, v_out):
        d_out[...], m_out[...], v_out[...] = _adamw(w_ref[...], g_ref[...], m_ref[...], v_ref[...])

    spec = pl.BlockSpec((tm, cols), lambda i: (i, 0))
    return pl.pallas_call(
        body, name="adam_" + name, grid=(rows // tm,), in_specs=[spec] * 4, out_specs=[spec] * 3,
        out_shape=[jax.ShapeDtypeStruct((rows, cols), F32)] * 3, compiler_params=_params("parallel"),
    )(w, g, m, v)


MAT_PIECE = MAT_ROWS // 8
VEC_PIECE = DR // 8


def _small_all_reduce(dmat, dvec):
    def body(mat_ref, vec_ref, mat_out, vec_out, mat_in, vec_in, mat_sum, vec_sum, send_sems, recv_sems, loc_sems):
        x, y, c, _ = _place()
        me = 4 * x + 2 * y + c

        def peer(r):
            return (1 - x if r & 4 else x, 1 - y if r & 2 else y, 1 - c if r & 1 else c)

        def index(p):
            return 4 * p[0] + 2 * p[1] + p[2]

        def mat_rows(ref, p):
            return ref.at[pl.ds(pl.multiple_of(p * MAT_PIECE, 8), MAT_PIECE), :]

        def vec_cols(ref, p):
            return ref.at[:, pl.ds(pl.multiple_of(p * VEC_PIECE, 128), VEC_PIECE)]

        own = [pltpu.make_async_copy(mat_rows(mat_ref, me), mat_in.at[0], loc_sems.at[0]),
               pltpu.make_async_copy(vec_cols(vec_ref, me), vec_in.at[0], loc_sems.at[1])]
        for cp in own:
            cp.start()
        scatter = []
        for r in range(1, 8):
            p = index(peer(r))
            scatter.append(_remote(mat_rows(mat_ref, p), mat_in.at[r], send_sems, recv_sems, (0, r), peer(r)))
            scatter.append(_remote(vec_cols(vec_ref, p), vec_in.at[r], send_sems, recv_sems, (1, r), peer(r)))
        for cp in scatter:
            cp.start()
        for cp in own:
            cp.wait()
        for cp in scatter:
            cp.wait_recv()
        mat_tot, vec_tot = mat_in[0], vec_in[0]
        for r in range(1, 8):
            mat_tot = mat_tot + mat_in[r]
            vec_tot = vec_tot + vec_in[r]
        mat_sum[...] = mat_tot
        vec_sum[...] = vec_tot
        keep = [pltpu.make_async_copy(mat_sum, mat_rows(mat_out, me), loc_sems.at[2]),
                pltpu.make_async_copy(vec_sum, vec_cols(vec_out, me), loc_sems.at[3])]
        for cp in keep:
            cp.start()
        spread = []
        for r in range(1, 8):
            spread.append(_remote(mat_sum, mat_rows(mat_out, me), send_sems, recv_sems, (2, r), peer(r)))
            spread.append(_remote(vec_sum, vec_cols(vec_out, me), send_sems, recv_sems, (3, r), peer(r)))
        for cp in spread:
            cp.start()
        for r in range(1, 8):
            p = index(peer(r))
            _remote(mat_sum, mat_rows(mat_out, p), send_sems, recv_sems, (2, r), peer(r)).wait_recv()
            _remote(vec_sum, vec_cols(vec_out, p), send_sems, recv_sems, (3, r), peer(r)).wait_recv()
        for cp in scatter + spread:
            cp.wait_send()
        for cp in keep:
            cp.wait()

    return pl.pallas_call(
        body, name="small_all_reduce",
        in_specs=[VMEM_SPEC, VMEM_SPEC], out_specs=[VMEM_SPEC, VMEM_SPEC],
        out_shape=[jax.ShapeDtypeStruct((MAT_ROWS, HD), F32), jax.ShapeDtypeStruct((VEC_ROWS, DR), F32)],
        scratch_shapes=[pltpu.VMEM((8, MAT_PIECE, HD), F32), pltpu.VMEM((8, VEC_ROWS, VEC_PIECE), F32),
                        pltpu.VMEM((MAT_PIECE, HD), F32), pltpu.VMEM((VEC_ROWS, VEC_PIECE), F32),
                        pltpu.SemaphoreType.DMA((4, 8)), pltpu.SemaphoreType.DMA((4, 8)), pltpu.SemaphoreType.DMA((4,))],
    )(dmat, dvec)


def _adam_small(grads, wmv):
    n = len(grads)

    def body(*refs):
        g_refs, rest = refs[:n], refs[n:]
        ins, outs = rest[:3 * n], rest[3 * n:]
        for i in range(n):
            d, m, v = _adamw(ins[3 * i][...], g_refs[i][...], ins[3 * i + 1][...], ins[3 * i + 2][...])
            outs[3 * i][...], outs[3 * i + 1][...], outs[3 * i + 2][...] = d, m, v

    flat = [a for t in wmv for a in t]
    return pl.pallas_call(
        body, name="adam_small",
        in_specs=[VMEM_SPEC] * (4 * n), out_specs=[VMEM_SPEC] * (3 * n),
        out_shape=[jax.ShapeDtypeStruct(a.shape, F32) for a in flat],
    )(*grads, *flat)


WEIGHT_NAMES = ("norm_mix", "w_in", "w_pool_grp", "pool_scale", "w_pool_out", "conv_w", "conv_b", "w_rg_a", "b_rg_a", "w_rg_x",
                "b_rg_x", "lru_lambda", "w_rnn_out", "w_o", "norm_ffn", "w_ffn_in", "w_ffn_out", "norm_final")


def kernel(x, norm_mix, w_in, w_pool_grp, pool_scale, w_pool_out, conv_w, conv_b, w_rg_a, b_rg_a, w_rg_x, b_rg_x, lru_lambda, w_rnn_out, w_o, norm_ffn, w_ffn_in, w_ffn_out, norm_final, loss_target, m_norm_mix, m_w_in, m_w_pool_grp, m_pool_scale, m_w_pool_out, m_conv_w, m_conv_b, m_w_rg_a, m_b_rg_a, m_w_rg_x, m_b_rg_x, m_lru_lambda, m_w_rnn_out, m_w_o, m_norm_ffn, m_w_ffn_in, m_w_ffn_out, m_norm_final, v_norm_mix, v_w_in, v_w_pool_grp, v_pool_scale, v_w_pool_out, v_conv_w, v_conv_b, v_w_rg_a, v_b_rg_a, v_w_rg_x, v_b_rg_x, v_lru_lambda, v_w_rnn_out, v_w_o, v_norm_ffn, v_w_ffn_in, v_w_ffn_out, v_norm_final):
    w = dict(norm_mix=norm_mix, w_in=w_in, w_pool_grp=w_pool_grp, pool_scale=pool_scale, w_pool_out=w_pool_out, conv_w=conv_w,
             conv_b=conv_b, w_rg_a=w_rg_a, b_rg_a=b_rg_a, w_rg_x=w_rg_x, b_rg_x=b_rg_x, lru_lambda=lru_lambda,
             w_rnn_out=w_rnn_out, w_o=w_o, norm_ffn=norm_ffn, w_ffn_in=w_ffn_in, w_ffn_out=w_ffn_out, norm_final=norm_final)
    m = dict(norm_mix=m_norm_mix, w_in=m_w_in, w_pool_grp=m_w_pool_grp, pool_scale=m_pool_scale, w_pool_out=m_w_pool_out,
             conv_w=m_conv_w, conv_b=m_conv_b, w_rg_a=m_w_rg_a, b_rg_a=m_b_rg_a, w_rg_x=m_w_rg_x, b_rg_x=m_b_rg_x,
             lru_lambda=m_lru_lambda, w_rnn_out=m_w_rnn_out, w_o=m_w_o, norm_ffn=m_norm_ffn, w_ffn_in=m_w_ffn_in,
             w_ffn_out=m_w_ffn_out, norm_final=m_norm_final)
    v = dict(norm_mix=v_norm_mix, w_in=v_w_in, w_pool_grp=v_w_pool_grp, pool_scale=v_pool_scale, w_pool_out=v_w_pool_out,
             conv_w=v_conv_w, conv_b=v_conv_b, w_rg_a=v_w_rg_a, b_rg_a=v_b_rg_a, w_rg_x=v_w_rg_x, b_rg_x=v_b_rg_x,
             lru_lambda=v_lru_lambda, w_rnn_out=v_w_rnn_out, w_o=v_w_o, norm_ffn=v_norm_ffn, w_ffn_in=v_w_ffn_in,
             w_ffn_out=v_w_ffn_out, norm_final=v_norm_final)
    xi, yi, ci = (lax.axis_index(a) for a in MESH_AXES)
    chip = 2 * xi + yi
    other = [2 * (1 - xi) + yi, 2 * xi + (1 - yi), 2 * (1 - xi) + (1 - yi)]

    gathered = _all_gather([_pad_to_chip_block(T, w[T.name][0], ci) for T in GATHERED])
    full = {T.name: a for T, a in zip(GATHERED, gathered)}
    wts = (full["w_in"], full["w_pool_out"], full["conv_w"][0:4], full["w_rnn_out"], full["w_o"], full["w_ffn_in"],
           full["w_ffn_out"])
    small = (norm_mix, w_pool_grp[0], pool_scale, conv_b, w_rg_a[0], b_rg_a[0], w_rg_x[0], b_rg_x[0], lru_lambda, norm_ffn,
             norm_final.reshape(1, D))

    loss_part, grad_x, big, dmat, dvec = _local_step(x[0], loss_target[0], wts, small)
    loss = lax.psum(loss_part[0, 0], MESH_AXES)

    grads, delta, new_m, new_v = {}, {}, {}, {}
    where4 = jnp.stack(other + [ci]).astype(jnp.int32)
    where2 = jnp.stack([chip, ci]).astype(jnp.int32)
    lz1 = _pair_exchange(big)
    sums = [_pair_sum(T, g, l, where4) for T, g, l in zip(BIG, big, lz1)]
    lz2 = _chip_scatter(sums)
    halves = []
    for T, g, l1, l2 in zip(BIG, big, lz1, lz2):
        if T.rowhalf:
            halves.append(_final_sum(T, g, l1, l2, where2)[0])
        else:
            n = T.name
            grads[n], delta[n], new_m[n], new_v[n] = _final_sum(T, g, l1, l2, where2, (w[n][0], m[n][0], v[n][0]))
    joined = _pair_join(halves)
    for T, j in zip([T for T in BIG if T.rowhalf], joined):
        n = T.name
        grads[n] = lax.dynamic_slice(j, (0, T.n * ci), (T.rows, T.n))
        delta[n], new_m[n], new_v[n] = _adam_rows(n, w[n][0], grads[n], m[n][0], v[n][0])

    mat, vec = _small_all_reduce(dmat, dvec)
    me = 4 * xi + 2 * yi + ci
    small_grads = dict(
        w_pool_grp=mat[0:MAT_WA], w_rg_a=mat[MAT_WA:MAT_WX], w_rg_x=mat[MAT_WX:MAT_ROWS],
        pool_scale=vec[VEC_SCALE:VEC_SCALE + 1, 0:DP], conv_b=vec[VEC_CONV_B:VEC_CONV_B + 1],
        b_rg_a=vec[VEC_BA:VEC_BA + 1], b_rg_x=vec[VEC_BX:VEC_BX + 1], lru_lambda=vec[VEC_LAM:VEC_LAM + 1],
        conv_w=lax.dynamic_slice(vec, (VEC_CONV_W, VEC_PIECE * me), (4, VEC_PIECE)),
        norm_final=vec[VEC_NORM_FINAL:VEC_NORM_FINAL + 1], norm_ffn=vec[VEC_NORM_FFN:VEC_NORM_FFN + 1],
        norm_mix=vec[VEC_NORM_MIX:VEC_NORM_MIX + 1])
    names = list(small_grads)
    as2d = lambda a, g: a.reshape(g.shape)
    upd = _adam_small([small_grads[n] for n in names],
                      [(as2d(w[n], small_grads[n]), as2d(m[n], small_grads[n]), as2d(v[n], small_grads[n])) for n in names])
    for i, n in enumerate(names):
        grads[n] = small_grads[n]
        delta[n], new_m[n], new_v[n] = upd[3 * i:3 * i + 3]

    shaped = lambda d: [d[n].reshape(w[n].shape) for n in WEIGHT_NAMES]
    return (loss, grad_x[None], *shaped(grads), *shaped(delta), *shaped(new_m), *shaped(new_v))
```

```python
import functools
import math

import jax
import jax.numpy as jnp
from jax import lax
from jax.experimental import pallas as pl
from jax.experimental.pallas import tpu as pltpu

F32 = jnp.float32
BF16 = jnp.bfloat16

D = 1024
DP = 512
PG = 128
WINDOWS = (2, 4, 8, 16)
DR = 1024
NH = 8
HD = 128
DIN = 4608
DFF = 2816
EPS = 1e-6
LRU_C = 8.0
POOL_HALO = 16
CONV_HALO = 8

ADAM_LR = 0.001
ADAM_B1 = 0.9
ADAM_B2 = 0.999
ADAM_EPS = 1e-08
ADAM_WD = 0.01
ADAM_STEP = 10

VMEM_LIMIT = 56 * 1024 * 1024
MESH_AXES = ("x", "y", "c")
MESH = pl.DeviceIdType.MESH


def _dot(a, b):
    return jnp.dot(a, b, preferred_element_type=F32)


def _dot_nt(a, b):
    return lax.dot_general(a, b, (((1,), (1,)), ((), ())), preferred_element_type=F32)


def _dot_tn(a, b):
    return lax.dot_general(a, b, (((0,), (0,)), ((), ())), preferred_element_type=F32)


def _params(*sem):
    return pltpu.CompilerParams(dimension_semantics=sem, vmem_limit_bytes=VMEM_LIMIT)


def _resident(shape):
    nd = len(shape)
    return pl.BlockSpec(shape, lambda i: (0,) * nd, pipeline_mode=pl.Buffered(1))


def _rows(shape_cols, tm):
    return pl.BlockSpec((tm, shape_cols), lambda i: (i, 0))


def _gelu(x):
    c = math.sqrt(2.0 / math.pi)
    t = jnp.tanh(c * (x + 0.044715 * x * x * x))
    return 0.5 * x * (1.0 + t), t


def _gelu_grad(x, t):
    c = math.sqrt(2.0 / math.pi)
    return 0.5 * (1.0 + t) + 0.5 * x * (1.0 - t * t) * (c * (1.0 + 3.0 * 0.044715 * x * x))


def _softplus_neg(lam):
    z = jnp.exp(-jnp.abs(lam))
    u = 1.0 + z
    dlt = u - 1.0
    log1p = jnp.where(dlt == 0.0, z, jnp.log(u) * (z / jnp.where(dlt == 0.0, 1.0, dlt)))
    return jnp.maximum(-lam, 0.0) + log1p


def _sigmoid(x):
    return 0.5 * jnp.tanh(0.5 * x) + 0.5


def _linear_scan(A, B, reverse):
    n, cols = A.shape
    rows = lax.broadcasted_iota(jnp.int32, (n, 1), 0)
    d = 1
    while d < n:
        if d < 8:
            keep = (rows < n - d) if reverse else (rows >= d)
            shift = n - d if reverse else d
            B = jnp.where(keep, A * pltpu.roll(B, shift, axis=0) + B, B)
            A = jnp.where(keep, A * pltpu.roll(A, shift, axis=0), A)
        else:
            def shifted(v, fill):
                pad = jnp.full((d, cols), fill, v.dtype)
                return jnp.concatenate([v[d:], pad] if reverse else [pad, v[:n - d]], axis=0)
            B = A * shifted(B, 0.0) + B
            A = A * shifted(A, 1.0)
        d *= 2
    return A, B


def _pool_windows(ext, shift_sign):
    n = ext.shape[0]
    s = ext
    outs = []
    for w in WINDOWS:
        d = w // 2
        s = s + pltpu.roll(s, d if shift_sign > 0 else n - d, axis=0)
        outs.append(s[:, :PG])
        s = s[:, PG:]
    return outs


def _conv_taps(uext):
    taps = []
    for k in range(4):
        sh = 3 - k
        v = uext if sh == 0 else pltpu.roll(uext, sh, axis=0)
        taps.append(v[CONV_HALO:, :])
    return taps


def _gates(v, wa_ref, ba_ref, wx_ref, bx_ref, sp):
    vb = v.astype(BF16)
    ra, rx = [], []
    for h in range(NH):
        vh = vb[:, h * HD:(h + 1) * HD]
        ra.append(_dot(vh, wa_ref[h]))
        rx.append(_dot(vh, wx_ref[h]))
    r = _sigmoid(jnp.concatenate(ra, axis=1) + ba_ref[...])
    i = _sigmoid(jnp.concatenate(rx, axis=1) + bx_ref[...])
    log_a = (-LRU_C) * r * sp
    a = jnp.exp(log_a)
    mult = jnp.sqrt(-jnp.tanh(log_a) * (1.0 + a * a))
    return r, i, a, mult


def _in_proj(x, norm_mix, w_in, tm=512):
    S = x.shape[0]

    def body(x_ref, g_ref, w_ref, proj_ref, h_ref):
        xv = x_ref[...]
        r = lax.rsqrt(jnp.mean(xv * xv, axis=-1, keepdims=True) + EPS)
        h = (xv * r * g_ref[...]).astype(BF16)
        h_ref[...] = h
        for n0 in range(0, DIN, 512):
            proj_ref[:, n0:n0 + 512] = _dot_nt(h, w_ref[n0:n0 + 512, :])

    return pl.pallas_call(
        body, name="in_proj", grid=(S // tm,),
        in_specs=[_rows(D, tm), _resident((1, D)), _resident((DIN, D))],
        out_specs=[_rows(DIN, tm), _rows(D, tm)],
        out_shape=[jax.ShapeDtypeStruct((S, DIN), F32), jax.ShapeDtypeStruct((S, D), BF16)],
        compiler_params=_params("parallel"),
    )(x, norm_mix, w_in)


def _mixer_fwd(proj, wg, scale, w_pool_out, conv_w, conv_b, wa, ba, wx, bx, lam, w_rnn_out, tm=256):
    S = proj.shape[0]
    UW = DP + 2 * DR

    def body(proj_ref, wg_ref, scale_ref, wpo_ref, cw_ref, cb_ref, wa_ref, ba_ref, wx_ref, bx_ref, lam_ref, wro_ref,
             pooled_ref, pm_ref, ypool_ref, hr_ref, z_ref, yrnn_ref, pool_carry, conv_carry, h_carry):
        i = pl.program_id(0)

        @pl.when(i == 0)
        def _():
            pool_carry[...] = jnp.zeros_like(pool_carry)
            conv_carry[...] = jnp.zeros_like(conv_carry)
            h_carry[...] = jnp.zeros_like(h_carry)

        rows = lax.broadcasted_iota(jnp.int32, (tm, 1), 0)
        t_glob = i * tm + rows

        u_pool = proj_ref[:, 0:DP]
        ext = jnp.concatenate([pool_carry[...], u_pool], axis=0)
        pool_carry[...] = u_pool[tm - POOL_HALO:, :]
        sums = _pool_windows(ext, +1)
        mixed = []
        for g, w in enumerate(WINDOWS):
            cnt = jnp.minimum(t_glob + 1, w).astype(F32)
            pooled_g = sums[g][POOL_HALO:, :] / cnt - u_pool[:, g * PG:(g + 1) * PG]
            pooled_b = pooled_g.astype(BF16)
            pooled_ref[:, g * PG:(g + 1) * PG] = pooled_b
            mixed.append(_dot(pooled_b, wg_ref[g]))
        pm = (jnp.concatenate(mixed, axis=1) * scale_ref[...]).astype(BF16)
        pm_ref[...] = pm
        ypool_ref[...] = _dot(pm, wpo_ref[...])

        u_rnn = proj_ref[:, DP:DP + DR]
        uext = jnp.concatenate([conv_carry[...], u_rnn], axis=0)
        conv_carry[...] = u_rnn[tm - CONV_HALO:, :]
        taps = _conv_taps(uext)
        v = cb_ref[...]
        for k in range(4):
            v = v + taps[k] * cw_ref[k:k + 1, :]
        sp = _softplus_neg(lam_ref[...])
        _, gi, a, mult = _gates(v, wa_ref, ba_ref, wx_ref, bx_ref, sp)
        b = mult * gi * v
        A, B = _linear_scan(a, b, reverse=False)
        hr = A * h_carry[0:1, :] + B
        h_carry[0:1, :] = hr[tm - 1:tm, :]
        hr_ref[...] = hr
        gg, _ = _gelu(proj_ref[:, DP + DR:UW])
        z = (hr * gg).astype(BF16)
        z_ref[...] = z
        yrnn_ref[...] = _dot(z, wro_ref[...])

    return pl.pallas_call(
        body, name="mixer_fwd", grid=(S // tm,),
        in_specs=[_rows(UW, tm), _resident((4, PG, PG)), _resident((1, DP)), _resident((DP, D)), _resident((4, DR)),
                  _resident((1, DR)), _resident((NH, HD, HD)), _resident((1, DR)), _resident((NH, HD, HD)),
                  _resident((1, DR)), _resident((1, DR)), _resident((DR, D))],
        out_specs=[_rows(DP, tm), _rows(DP, tm), _rows(D, tm), _rows(DR, tm), _rows(DR, tm), _rows(D, tm)],
        out_shape=[jax.ShapeDtypeStruct((S, DP), BF16), jax.ShapeDtypeStruct((S, DP), BF16),
                   jax.ShapeDtypeStruct((S, D), F32), jax.ShapeDtypeStruct((S, DR), F32),
                   jax.ShapeDtypeStruct((S, DR), BF16), jax.ShapeDtypeStruct((S, D), F32)],
        scratch_shapes=[pltpu.VMEM((POOL_HALO, DP), F32), pltpu.VMEM((CONV_HALO, DR), F32), pltpu.VMEM((8, DR), F32)],
        compiler_params=_params("arbitrary"),
    )(proj, wg, scale, w_pool_out, conv_w, conv_b, wa, ba, wx, bx, lam, w_rnn_out)


FF_CHUNK = DFF // 4


def _rms(x):
    r = lax.rsqrt(jnp.mean(x * x, axis=-1, keepdims=True) + EPS)
    return r, x * r


def _rms_bwd(dh, g, r, xh):
    dxh = dh * g
    return r * (dxh - xh * jnp.mean(dxh * xh, axis=-1, keepdims=True))


def _merge_out(x, proj, y_pool, y_rnn, w_o, norm_ffn, tm=512):
    S = x.shape[0]
    GL0 = (DP + 2 * DR) // 512

    def gl_spec(k):
        return pl.BlockSpec((tm, 512), lambda i: (i, GL0 + k))

    def body(x_ref, gl0, gl1, gl2, gl3, yp_ref, yr_ref, wo_ref, gf_ref, mix_ref, x2_ref, h2_ref):
        s_p = _sigmoid(jnp.concatenate([gl0[...], gl1[...]], axis=1))
        s_r = _sigmoid(jnp.concatenate([gl2[...], gl3[...]], axis=1))
        mix = (s_p * yp_ref[...] + s_r * yr_ref[...]).astype(BF16)
        mix_ref[...] = mix
        x2 = x_ref[...] + _dot(mix, wo_ref[...])
        x2_ref[...] = x2
        _, xh2 = _rms(x2)
        h2_ref[...] = (xh2 * gf_ref[...]).astype(BF16)

    return pl.pallas_call(
        body, name="merge_out", grid=(S // tm,),
        in_specs=[_rows(D, tm), gl_spec(0), gl_spec(1), gl_spec(2), gl_spec(3), _rows(D, tm), _rows(D, tm),
                  _resident((D, D)), _resident((1, D))],
        out_specs=[_rows(D, tm), _rows(D, tm), _rows(D, tm)],
        out_shape=[jax.ShapeDtypeStruct((S, D), BF16), jax.ShapeDtypeStruct((S, D), F32), jax.ShapeDtypeStruct((S, D), BF16)],
        compiler_params=_params("parallel"),
    )(x, proj, proj, proj, proj, y_pool, y_rnn, w_o, norm_ffn)


def _ffn_up(h2, w_ffn_in, tm=512):
    S = h2.shape[0]

    def body(h_ref, w_ref, gu_ref, act_ref):
        h = h_ref[...]
        for c0 in range(0, DFF, FF_CHUNK):
            gate = _dot_nt(h, w_ref[c0:c0 + FF_CHUNK, :])
            up = _dot_nt(h, w_ref[DFF + c0:DFF + c0 + FF_CHUNK, :])
            gu_ref[:, c0:c0 + FF_CHUNK] = gate.astype(BF16)
            gu_ref[:, DFF + c0:DFF + c0 + FF_CHUNK] = up.astype(BF16)
            act_ref[:, c0:c0 + FF_CHUNK] = (gate * _sigmoid(gate) * up).astype(BF16)

    return pl.pallas_call(
        body, name="ffn_up", grid=(S // tm,),
        in_specs=[_rows(D, tm), _resident((2 * DFF, D))],
        out_specs=[_rows(2 * DFF, tm), _rows(DFF, tm)],
        out_shape=[jax.ShapeDtypeStruct((S, 2 * DFF), BF16), jax.ShapeDtypeStruct((S, DFF), BF16)],
        compiler_params=_params("parallel"),
    )(h2, w_ffn_in)


def _ffn_down_loss(act, x2, target, w_ffn_out, norm_final, tm=512):
    S = act.shape[0]

    def body(act_ref, x2_ref, t_ref, w_ref, gn_ref, dx3_ref, dx3b_ref, loss_ref, dvec_ref):
        i = pl.program_id(0)

        @pl.when(i == 0)
        def _():
            loss_ref[...] = jnp.zeros_like(loss_ref)
            dvec_ref[...] = jnp.zeros_like(dvec_ref)

        x3 = x2_ref[...] + _dot(act_ref[...], w_ref[...])
        r3, xh3 = _rms(x3)
        g_fin = gn_ref[...]
        e = xh3 * g_fin - t_ref[...]
        loss_ref[...] += jnp.sum(e * e, axis=(0, 1), keepdims=True) * (0.5 / D)
        dy = e * (1.0 / D)
        dvec_ref[0:1, :] += jnp.sum(dy * xh3, axis=0, keepdims=True)
        dx3 = _rms_bwd(dy, g_fin, r3, xh3)
        dx3_ref[...] = dx3
        dx3b_ref[...] = dx3.astype(BF16)

    return pl.pallas_call(
        body, name="ffn_down_loss", grid=(S // tm,),
        in_specs=[_rows(DFF, tm), _rows(D, tm), _rows(D, tm), _resident((DFF, D)), _resident((1, D))],
        out_specs=[_rows(D, tm), _rows(D, tm), _resident((1, 1)), _resident((8, D))],
        out_shape=[jax.ShapeDtypeStruct((S, D), F32), jax.ShapeDtypeStruct((S, D), BF16),
                   jax.ShapeDtypeStruct((1, 1), F32), jax.ShapeDtypeStruct((8, D), F32)],
        compiler_params=_params("arbitrary"),
    )(act, x2, target, w_ffn_out, norm_final)


def _ffn_bwd_down(dx3b, gu, w_ffn_out, tm=512):
    S = dx3b.shape[0]

    def body(d_ref, gu_ref, w_ref, dgu_ref):
        d = d_ref[...]
        for c0 in range(0, DFF, FF_CHUNK):
            dact = _dot_nt(d, w_ref[c0:c0 + FF_CHUNK, :])
            gate = gu_ref[:, c0:c0 + FF_CHUNK].astype(F32)
            up = gu_ref[:, DFF + c0:DFF + c0 + FF_CHUNK].astype(F32)
            sg = _sigmoid(gate)
            dgu_ref[:, c0:c0 + FF_CHUNK] = (dact * up * (sg * (1.0 + gate * (1.0 - sg)))).astype(BF16)
            dgu_ref[:, DFF + c0:DFF + c0 + FF_CHUNK] = (dact * (gate * sg)).astype(BF16)

    return pl.pallas_call(
        body, name="ffn_bwd_down", grid=(S // tm,),
        in_specs=[_rows(D, tm), _rows(2 * DFF, tm), _resident((DFF, D))],
        out_specs=_rows(2 * DFF, tm),
        out_shape=jax.ShapeDtypeStruct((S, 2 * DFF), BF16),
        compiler_params=_params("parallel"),
    )(dx3b, gu, w_ffn_out)


def _ffn_bwd_up(dgu, x2, dx3, w_ffn_in, norm_ffn, w_o, tm=512):
    S = dgu.shape[0]

    def body(dgu_ref, x2_ref, dx3_ref, wfi_ref, gf_ref, wo_ref, dx2_ref, dx2b_ref, dmixo_ref, dvec_ref):
        i = pl.program_id(0)

        @pl.when(i == 0)
        def _():
            dvec_ref[...] = jnp.zeros_like(dvec_ref)

        dh2 = _dot(dgu_ref[:, 0:DFF], wfi_ref[0:DFF, :]) + _dot(dgu_ref[:, DFF:2 * DFF], wfi_ref[DFF:2 * DFF, :])
        r2, xh2 = _rms(x2_ref[...])
        dvec_ref[0:1, :] += jnp.sum(dh2 * xh2, axis=0, keepdims=True)
        dx2 = dx3_ref[...] + _rms_bwd(dh2, gf_ref[...], r2, xh2)
        dx2_ref[...] = dx2
        dx2b = dx2.astype(BF16)
        dx2b_ref[...] = dx2b
        dmixo_ref[...] = _dot_nt(dx2b, wo_ref[...])

    return pl.pallas_call(
        body, name="ffn_bwd_up", grid=(S // tm,),
        in_specs=[_rows(2 * DFF, tm), _rows(D, tm), _rows(D, tm), _resident((2 * DFF, D)), _resident((1, D)),
                  _resident((D, D))],
        out_specs=[_rows(D, tm), _rows(D, tm), _rows(D, tm), _resident((8, D))],
        out_shape=[jax.ShapeDtypeStruct((S, D), F32), jax.ShapeDtypeStruct((S, D), BF16), jax.ShapeDtypeStruct((S, D), F32),
                   jax.ShapeDtypeStruct((8, D), F32)],
        compiler_params=_params("arbitrary"),
    )(dgu, x2, dx3, w_ffn_in, norm_ffn, w_o)


VEC_ROWS = 16
MAT_WA = 4 * PG
MAT_WX = MAT_WA + NH * HD
MAT_ROWS = MAT_WX + NH * HD


def _mixer_bwd(proj, dmixo, y_pool, y_rnn, hr, wg, scale, w_pool_out, conv_w, conv_b, wa, ba, wx, bx, lam, w_rnn_out,
               tm=256):
    S = proj.shape[0]
    nt = S // tm

    def rev(cols):
        return pl.BlockSpec((tm, cols), lambda i: (nt - 1 - i, 0))

    def halo(rows_, cols):
        per = tm // rows_
        return pl.BlockSpec((rows_, cols), lambda i: (jnp.maximum((nt - 1 - i) * per - 1, 0), 0))

    def body(proj_ref, projh_ref, dmixo_ref, yp_ref, yr_ref, hr_ref, hrh_ref, wg_ref, scale_ref, wpo_ref, cw_ref, cb_ref,
             wa_ref, ba_ref, wx_ref, bx_ref, lam_ref, wro_ref,
             dproj_ref, dypb_ref, dyrb_ref, dmat_ref, dvec_ref,
             q_carry, dv_carry, a_carry, g_carry):
        i = pl.program_id(0)
        ti = nt - 1 - i

        @pl.when(i == 0)
        def _():
            q_carry[...] = jnp.zeros_like(q_carry)
            dv_carry[...] = jnp.zeros_like(dv_carry)
            a_carry[...] = jnp.zeros_like(a_carry)
            g_carry[...] = jnp.zeros_like(g_carry)
            dmat_ref[...] = jnp.zeros_like(dmat_ref)
            dvec_ref[...] = jnp.zeros_like(dvec_ref)

        rows = lax.broadcasted_iota(jnp.int32, (tm, 1), 0)
        t_glob = ti * tm + rows
        has_prev = (ti > 0).astype(F32)
        dmixo = dmixo_ref[...]

        s_p = _sigmoid(proj_ref[:, DP + 2 * DR:DP + 2 * DR + D])
        s_r = _sigmoid(proj_ref[:, DP + 2 * DR + D:DIN])
        dproj_ref[:, DP + 2 * DR:DP + 2 * DR + D] = (dmixo * yp_ref[...] * s_p * (1.0 - s_p)).astype(BF16)
        dproj_ref[:, DP + 2 * DR + D:DIN] = (dmixo * yr_ref[...] * s_r * (1.0 - s_r)).astype(BF16)
        dyp = (dmixo * s_p).astype(BF16)
        dyr = (dmixo * s_r).astype(BF16)
        dypb_ref[...] = dyp
        dyrb_ref[...] = dyr

        dz = _dot_nt(dyr, wro_ref[...])
        u_gate = proj_ref[:, DP + DR:DP + 2 * DR]
        gg, tg = _gelu(u_gate)
        hr_t = hr_ref[...]
        dproj_ref[:, DP + DR:DP + 2 * DR] = (dz * hr_t * _gelu_grad(u_gate, tg)).astype(BF16)
        dhr = dz * gg

        u_rnn = proj_ref[:, DP:DP + DR]
        uext = jnp.concatenate([projh_ref[POOL_HALO - CONV_HALO:, DP:DP + DR] * has_prev, u_rnn], axis=0)
        taps = _conv_taps(uext)
        v = cb_ref[...]
        for k in range(4):
            v = v + taps[k] * cw_ref[k:k + 1, :]
        sp = _softplus_neg(lam_ref[...])
        r, gi, a, mult = _gates(v, wa_ref, ba_ref, wx_ref, bx_ref, sp)

        C = jnp.where(rows == tm - 1, a_carry[0:1, :], pltpu.roll(a, tm - 1, axis=0))
        C, G = _linear_scan(C, dhr, reverse=True)
        g = G + C * g_carry[0:1, :]
        a_carry[0:1, :] = a[0:1, :]
        g_carry[0:1, :] = g[0:1, :]

        h_prev = jnp.where(rows == 0, hrh_ref[7:8, :] * has_prev, pltpu.roll(hr_t, 1, axis=0))
        da = g * h_prev
        dmult = g * gi * v
        di = g * mult * v
        dv = g * mult * gi
        dlog_a = da * a - dmult * (a * a / mult)
        dvec_ref[4:5, :] += jnp.sum(dlog_a * r, axis=0, keepdims=True)
        dra = (dlog_a * ((-LRU_C) * sp) * r * (1.0 - r))
        drx = di * gi * (1.0 - gi)
        dvec_ref[2:3, :] += jnp.sum(dra, axis=0, keepdims=True)
        dvec_ref[3:4, :] += jnp.sum(drx, axis=0, keepdims=True)
        drab = dra.astype(BF16)
        drxb = drx.astype(BF16)
        vb = v.astype(BF16)
        dvg = []
        for h in range(NH):
            sl = slice(h * HD, (h + 1) * HD)
            dvg.append(_dot_nt(drab[:, sl], wa_ref[h]) + _dot_nt(drxb[:, sl], wx_ref[h]))
            dmat_ref[MAT_WA + h * HD:MAT_WA + (h + 1) * HD, :] += _dot_tn(vb[:, sl], drab[:, sl])
            dmat_ref[MAT_WX + h * HD:MAT_WX + (h + 1) * HD, :] += _dot_tn(vb[:, sl], drxb[:, sl])
        dv = dv + jnp.concatenate(dvg, axis=1)
        dvec_ref[1:2, :] += jnp.sum(dv, axis=0, keepdims=True)
        for k in range(4):
            dvec_ref[5 + k:6 + k, :] += jnp.sum(dv * taps[k], axis=0, keepdims=True)
        dvext = jnp.concatenate([dv, dv_carry[...]], axis=0)
        dv_carry[...] = dv[0:CONV_HALO, :]
        n = tm + CONV_HALO
        du_rnn = dv * cw_ref[3:4, :]
        for k in range(3):
            du_rnn = du_rnn + pltpu.roll(dvext, n - (3 - k), axis=0)[0:tm, :] * cw_ref[k:k + 1, :]
        dproj_ref[:, DP:DP + DR] = du_rnn.astype(BF16)

        dpm = _dot_nt(dyp, wpo_ref[...])
        u_pool = proj_ref[:, 0:DP]
        ext = jnp.concatenate([projh_ref[:, 0:DP] * has_prev, u_pool], axis=0)
        sums = _pool_windows(ext, +1)
        scale_v = scale_ref[...]
        qs = []
        dpooled = []
        dscale = []
        for gi_, w in enumerate(WINDOWS):
            sl = slice(gi_ * PG, (gi_ + 1) * PG)
            cnt = jnp.minimum(t_glob + 1, w).astype(F32)
            pooled_b = (sums[gi_][POOL_HALO:, :] / cnt - u_pool[:, sl]).astype(BF16)
            mixed_g = _dot(pooled_b, wg_ref[gi_])
            dscale.append(jnp.sum(dpm[:, sl] * mixed_g, axis=0, keepdims=True))
            dmixed_b = (dpm[:, sl] * scale_v[:, sl]).astype(BF16)
            dmat_ref[gi_ * PG:(gi_ + 1) * PG, :] += _dot_tn(pooled_b, dmixed_b)
            dp_g = _dot_nt(dmixed_b, wg_ref[gi_])
            dpooled.append(dp_g)
            qs.append(dp_g / cnt)
        dvec_ref[0:1, 0:DP] += jnp.concatenate(dscale, axis=1)
        q = jnp.concatenate(qs, axis=1)
        qext = jnp.concatenate([q, q_carry[...]], axis=0)
        q_carry[...] = q[0:POOL_HALO, :]
        tsum = _pool_windows(qext, -1)
        for gi_ in range(4):
            dproj_ref[:, gi_ * PG:(gi_ + 1) * PG] = (tsum[gi_][0:tm, :] - dpooled[gi_]).astype(BF16)

        @pl.when(i == nt - 1)
        def _():
            dvec_ref[4:5, :] = dvec_ref[4:5, :] * (LRU_C * _sigmoid(-lam_ref[...]))

    return pl.pallas_call(
        body, name="mixer_bwd", grid=(nt,),
        in_specs=[rev(DIN), halo(POOL_HALO, DIN), rev(D), rev(D), rev(D), rev(DR), halo(8, DR),
                  _resident((4, PG, PG)), _resident((1, DP)), _resident((DP, D)), _resident((4, DR)), _resident((1, DR)),
                  _resident((NH, HD, HD)), _resident((1, DR)), _resident((NH, HD, HD)), _resident((1, DR)),
                  _resident((1, DR)), _resident((DR, D))],
        out_specs=[rev(DIN), rev(D), rev(D), _resident((MAT_ROWS, HD)), _resident((VEC_ROWS, DR))],
        out_shape=[jax.ShapeDtypeStruct((S, DIN), BF16), jax.ShapeDtypeStruct((S, D), BF16),
                   jax.ShapeDtypeStruct((S, D), BF16), jax.ShapeDtypeStruct((MAT_ROWS, HD), F32),
                   jax.ShapeDtypeStruct((VEC_ROWS, DR), F32)],
        scratch_shapes=[pltpu.VMEM((POOL_HALO, DP), F32), pltpu.VMEM((CONV_HALO, DR), F32), pltpu.VMEM((8, DR), F32),
                        pltpu.VMEM((8, DR), F32)],
        compiler_params=_params("arbitrary"),
    )(proj, proj, dmixo, y_pool, y_rnn, hr, hr, wg, scale, w_pool_out, conv_w, conv_b, wa, ba, wx, bx, lam, w_rnn_out)


def _in_bwd(dproj, x, dx2, norm_mix, w_in, tm=512):
    S = x.shape[0]

    def body(dp_ref, x_ref, dx2_ref, g_ref, w_ref, dx_ref, dg_ref):
        i = pl.program_id(0)

        @pl.when(i == 0)
        def _():
            dg_ref[...] = jnp.zeros_like(dg_ref)

        dh = _dot(dp_ref[:, 0:1536], w_ref[0:1536, :])
        dh = dh + _dot(dp_ref[:, 1536:3072], w_ref[1536:3072, :])
        dh = dh + _dot(dp_ref[:, 3072:DIN], w_ref[3072:DIN, :])
        xv = x_ref[...]
        r = lax.rsqrt(jnp.mean(xv * xv, axis=-1, keepdims=True) + EPS)
        xh = xv * r
        dg_ref[0:1, :] += jnp.sum(dh * xh, axis=0, keepdims=True)
        dxh = dh * g_ref[...]
        dx_ref[...] = dx2_ref[...] + r * (dxh - xh * jnp.mean(dxh * xh, axis=-1, keepdims=True))

    return pl.pallas_call(
        body, name="in_bwd", grid=(S // tm,),
        in_specs=[_rows(DIN, tm), _rows(D, tm), _rows(D, tm), _resident((1, D)), _resident((DIN, D))],
        out_specs=[_rows(D, tm), _resident((8, D))],
        out_shape=[jax.ShapeDtypeStruct((S, D), F32), jax.ShapeDtypeStruct((8, D), F32)],
        compiler_params=_params("arbitrary"),
    )(dproj, x, dx2, norm_mix, w_in)


def _wgrad(a, b, name, tk, tn):
    S, K = a.shape
    N = b.shape[1]

    def body(a_ref, b_ref, o_ref):
        o_ref[...] = _dot_tn(a_ref[...], b_ref[...]).astype(BF16)

    return pl.pallas_call(
        body, name=name, grid=(K // tk, N // tn),
        in_specs=[pl.BlockSpec((S, tk), lambda k, n: (0, k)), pl.BlockSpec((S, tn), lambda k, n: (0, n))],
        out_specs=pl.BlockSpec((tk, tn), lambda k, n: (k, n)),
        out_shape=jax.ShapeDtypeStruct((K, N), BF16),
        compiler_params=_params("parallel", "parallel"),
    )(a, b)


VEC_SCALE, VEC_CONV_B, VEC_BA, VEC_BX, VEC_LAM, VEC_CONV_W, VEC_NORM_FINAL, VEC_NORM_FFN, VEC_NORM_MIX = 0, 1, 2, 3, 4, 5, 9, 10, 11


def _local_step(x, target, wts, small):
    w_in, w_pool_out, conv_w, w_rnn_out, w_o, w_ffn_in, w_ffn_out = wts
    norm_mix, wg, scale, conv_b, wa, ba, wx, bx, lam, norm_ffn, norm_final = small
    wg_b, wa_b, wx_b = wg.astype(BF16), wa.astype(BF16), wx.astype(BF16)
    ba2, bx2 = ba.reshape(1, DR), bx.reshape(1, DR)
    proj, h1 = _in_proj(x, norm_mix, w_in)
    _, pm, y_pool, hr, z, y_rnn = _mixer_fwd(proj, wg_b, scale, w_pool_out, conv_w, conv_b, wa_b, ba2, wx_b, bx2, lam,
                                             w_rnn_out)
    mix, x2, h2 = _merge_out(x, proj, y_pool, y_rnn, w_o, norm_ffn)
    gu, act = _ffn_up(h2, w_ffn_in)
    dx3, dx3b, loss, dvec_fin = _ffn_down_loss(act, x2, target, w_ffn_out, norm_final)
    dgu = _ffn_bwd_down(dx3b, gu, w_ffn_out)
    dx2, dx2b, dmixo, dvec_ffn = _ffn_bwd_up(dgu, x2, dx3, w_ffn_in, norm_ffn, w_o)
    dproj, dypb, dyrb, dmat, dvec_mix = _mixer_bwd(proj, dmixo, y_pool, y_rnn, hr, wg_b, scale, w_pool_out, conv_w,
                                                   conv_b, wa_b, ba2, wx_b, bx2, lam, w_rnn_out)
    grad_x, dvec_in = _in_bwd(dproj, x, dx2, norm_mix, w_in)
    big = (
        _wgrad(dproj, h1, "wgrad_in", 1152, 1024),
        _wgrad(pm, dypb, "wgrad_pool_out", 512, 1024),
        _wgrad(z, dyrb, "wgrad_rnn_out", 1024, 1024),
        _wgrad(mix, dx2b, "wgrad_o", 1024, 1024),
        _wgrad(dgu, h2, "wgrad_ffn_in", 1408, 512),
        _wgrad(act, dx3b, "wgrad_ffn_out", 1408, 512),
    )
    dvec = jnp.concatenate([dvec_mix[0:9], dvec_fin[0:1], dvec_ffn[0:1], dvec_in[0:1], jnp.zeros((VEC_ROWS - 12, DR), F32)],
                           axis=0)
    return loss, grad_x, big, dmat, dvec


class _Big:
    def __init__(self, name, rows, cols, axis, n, dtype=BF16, transposed=False):
        self.name, self.rows, self.cols, self.axis, self.n, self.dtype = name, rows, cols, axis, n, dtype
        self.transposed = transposed
        self.block_shape = (rows, n) if axis == 1 else (n, cols)

    def block(self, ref, p):
        if self.axis == 1:
            return ref.at[:, pl.ds(pl.multiple_of(p * self.n, 128), self.n)]
        return ref.at[pl.ds(pl.multiple_of(p * self.n, 16 if self.dtype == BF16 else 8), self.n), :]

    def block_index(self, p):
        return (0, p) if self.axis == 1 else (p, 0)


BIG = (_Big("w_in", DIN, D, 0, DIN // 8, transposed=True), _Big("w_pool_out", DP, D, 1, D // 8),
       _Big("w_rnn_out", DR, D, 0, DR // 8), _Big("w_o", D, D, 0, D // 8),
       _Big("w_ffn_in", 2 * DFF, D, 0, 2 * DFF // 8, transposed=True), _Big("w_ffn_out", DFF, D, 0, DFF // 8))
CONV_W = _Big("conv_w", 8, DR, 1, DR // 8, F32)
GATHERED = BIG + (CONV_W,)

HBM_SPEC = pl.BlockSpec(memory_space=pl.ANY)
VMEM_SPEC = pl.BlockSpec(memory_space=pltpu.VMEM)


def _place():
    x, y, c = (lax.axis_index(a) for a in MESH_AXES)
    other_chips = [(1 - x, y), (x, 1 - y), (1 - x, 1 - y)]
    return x, y, c, other_chips


def _remote(src, dst, send_sems, recv_sems, idx, to):
    return pltpu.make_async_remote_copy(src_ref=src, dst_ref=dst, send_sem=send_sems.at[idx], recv_sem=recv_sems.at[idx],
                                        device_id=to, device_id_type=MESH)


def _device_index(chip, core):
    return 4 * chip[0] + 2 * chip[1] + core


def _all_gather(blocks):
    nt = len(GATHERED)

    def body(*refs):
        ins, outs, mine = (refs[k * nt:(k + 1) * nt] for k in range(3))
        send_sems, recv_sems, loc_sems = refs[3 * nt:]
        x, y, c, chips = _place()
        sibling = (x, y, 1 - c)
        me = _device_index((x, y), c)
        loads = [pltpu.make_async_copy(ins[t], mine[t], loc_sems.at[t, 0]) for t in range(nt)]
        for ld in loads:
            ld.start()
        stores, first = [], []
        for t, T in enumerate(GATHERED):
            loads[t].wait()
            place = T.block(outs[t], me)
            stores.append(pltpu.make_async_copy(mine[t], place, loc_sems.at[t, 1]))
            first.append(_remote(mine[t], place, send_sems, recv_sems, (t, 0), sibling))
            first += [_remote(mine[t], place, send_sems, recv_sems, (t, 1 + k), (*chip, c)) for k, chip in enumerate(chips)]
        for cp in stores + first:
            cp.start()
        passed = []
        for t, T in enumerate(GATHERED):
            for k, chip in enumerate(chips):
                land = T.block(outs[t], _device_index(chip, c))
                _remote(land, land, send_sems, recv_sems, (t, 1 + k), sibling).wait_recv()
                passed.append(_remote(land, land, send_sems, recv_sems, (t, 4 + k), sibling))
                passed[-1].start()
        for t, T in enumerate(GATHERED):
            land = T.block(outs[t], _device_index((x, y), 1 - c))
            _remote(land, land, send_sems, recv_sems, (t, 0), sibling).wait_recv()
            for k, chip in enumerate(chips):
                land = T.block(outs[t], _device_index(chip, 1 - c))
                _remote(land, land, send_sems, recv_sems, (t, 4 + k), sibling).wait_recv()
        for cp in first + passed:
            cp.wait_send()
        for st in stores:
            st.wait()

    return pl.pallas_call(
        body, name="all_gather_weights",
        in_specs=[HBM_SPEC] * nt, out_specs=[HBM_SPEC] * nt,
        out_shape=[jax.ShapeDtypeStruct((T.rows, T.cols), T.dtype) for T in GATHERED],
        scratch_shapes=[pltpu.VMEM(T.block_shape, T.dtype) for T in GATHERED]
        + [pltpu.SemaphoreType.DMA((nt, 7)), pltpu.SemaphoreType.DMA((nt, 7)), pltpu.SemaphoreType.DMA((nt, 2))],
        compiler_params=pltpu.CompilerParams(vmem_limit_bytes=VMEM_LIMIT),
    )(*blocks)


def _pair_exchange(grads):
    nt = len(BIG)

    def body(*refs):
        ins, outs = refs[:nt], refs[nt:2 * nt]
        send_sems, recv_sems = refs[2 * nt:]
        x, y, c, _ = _place()
        cps = []
        for t, T in enumerate(BIG):
            for j in range(4):
                cp = _remote(T.block(ins[t], 2 * j + 1 - c), outs[t].at[j], send_sems, recv_sems, (t, j), (x, y, 1 - c))
                cp.start()
                cps.append(cp)
        for cp in cps:
            cp.wait()

    return pl.pallas_call(
        body, name="grad_pair_exchange",
        in_specs=[HBM_SPEC] * nt, out_specs=[HBM_SPEC] * nt,
        out_shape=[jax.ShapeDtypeStruct((4,) + T.block_shape, BF16) for T in BIG],
        scratch_shapes=[pltpu.SemaphoreType.DMA((nt, 4)), pltpu.SemaphoreType.DMA((nt, 4))],
    )(*grads)


def _pair_sum(T, g, lz1, where):
    blk = T.block_shape

    def body(where_ref, g_ref, l_ref, o_ref):
        o_ref[...] = (g_ref[...].astype(F32) + l_ref[...].astype(F32)).astype(BF16)

    return pl.pallas_call(
        body, name="grad_pair_sum_" + T.name,
        grid_spec=pltpu.PrefetchScalarGridSpec(
            num_scalar_prefetch=1, grid=(3,),
            in_specs=[pl.BlockSpec(blk, lambda k, w: T.block_index(2 * w[k] + w[3])),
                      pl.BlockSpec((None,) + blk, lambda k, w: (w[k], 0, 0))],
            out_specs=pl.BlockSpec((None,) + blk, lambda k, w: (k, 0, 0))),
        out_shape=jax.ShapeDtypeStruct((3,) + T.block_shape, BF16),
        compiler_params=_params("arbitrary"),
    )(where, g, lz1)


def _chip_scatter(sums):
    nt = len(BIG)

    def body(*refs):
        ins, outs = refs[:nt], refs[nt:2 * nt]
        send_sems, recv_sems = refs[2 * nt:]
        x, y, c, chips = _place()
        cps = []
        for t in range(nt):
            for k, chip in enumerate(chips):
                cp = _remote(ins[t].at[k], outs[t].at[k], send_sems, recv_sems, (t, k), (*chip, c))
                cp.start()
                cps.append(cp)
        for cp in cps:
            cp.wait()

    return pl.pallas_call(
        body, name="grad_chip_scatter",
        in_specs=[HBM_SPEC] * nt, out_specs=[HBM_SPEC] * nt,
        out_shape=[jax.ShapeDtypeStruct((3,) + T.block_shape, BF16) for T in BIG],
        scratch_shapes=[pltpu.SemaphoreType.DMA((nt, 3)), pltpu.SemaphoreType.DMA((nt, 3))],
    )(*sums)


def _adamw(w, g, m, v):
    m = ADAM_B1 * m + (1.0 - ADAM_B1) * g
    v = ADAM_B2 * v + (1.0 - ADAM_B2) * (g * g)
    m_hat = m / (1.0 - ADAM_B1 ** ADAM_STEP)
    v_hat = v / (1.0 - ADAM_B2 ** ADAM_STEP)
    delta = -ADAM_LR * (m_hat / (jnp.sqrt(v_hat) + ADAM_EPS) + ADAM_WD * w)
    return delta, m, v


def _final_sum(T, g, lz1, lz2, where, w, m, v):
    rows, cols = T.block_shape
    sub = 4 if T.axis == 0 and rows % 64 == 0 and rows > 256 else 1
    blk = (rows // sub, cols)

    def body(where_ref, g_ref, l1_ref, l2_ref, w_ref, m_ref, v_ref, g_out, d_out, m_out, v_out):
        tot = g_ref[...].astype(F32) + l1_ref[...].astype(F32)
        for k in range(3):
            tot = tot + l2_ref[k].astype(F32)
        g_out[...] = tot
        d_out[...], m_out[...], v_out[...] = _adamw(w_ref[...], tot, m_ref[...], v_ref[...])

    def in_whole(r, wh):
        p = wh[0]
        return (0, p) if T.axis == 1 else (p * sub + r, 0)

    own = pl.BlockSpec(blk, lambda r, wh: (r, 0))
    return pl.pallas_call(
        body, name="grad_final_" + T.name,
        grid_spec=pltpu.PrefetchScalarGridSpec(
            num_scalar_prefetch=1, grid=(sub,),
            in_specs=[pl.BlockSpec(blk, in_whole),
                      pl.BlockSpec((None,) + blk, lambda r, wh: (wh[1], r, 0)),
                      pl.BlockSpec((3,) + blk, lambda r, wh: (0, r, 0)), own, own, own],
            out_specs=[own] * 4),
        out_shape=[jax.ShapeDtypeStruct(T.block_shape, F32)] * 4,
        compiler_params=_params("arbitrary"),
    )(where, g, lz1, lz2, w, m, v)


MAT_PIECE = MAT_ROWS // 8
VEC_PIECE = DR // 8


def _small_all_reduce(dmat, dvec):
    def body(mat_ref, vec_ref, mat_out, vec_out, mat_in, vec_in, mat_sum, vec_sum, send_sems, recv_sems, loc_sems):
        x, y, c, _ = _place()
        me = 4 * x + 2 * y + c

        def peer(r):
            return (1 - x if r & 4 else x, 1 - y if r & 2 else y, 1 - c if r & 1 else c)

        def index(p):
            return 4 * p[0] + 2 * p[1] + p[2]

        def mat_rows(ref, p):
            return ref.at[pl.ds(pl.multiple_of(p * MAT_PIECE, 8), MAT_PIECE), :]

        def vec_cols(ref, p):
            return ref.at[:, pl.ds(pl.multiple_of(p * VEC_PIECE, 128), VEC_PIECE)]

        own = [pltpu.make_async_copy(mat_rows(mat_ref, me), mat_in.at[0], loc_sems.at[0]),
               pltpu.make_async_copy(vec_cols(vec_ref, me), vec_in.at[0], loc_sems.at[1])]
        for cp in own:
            cp.start()
        scatter = []
        for r in range(1, 8):
            p = index(peer(r))
            scatter.append(_remote(mat_rows(mat_ref, p), mat_in.at[r], send_sems, recv_sems, (0, r), peer(r)))
            scatter.append(_remote(vec_cols(vec_ref, p), vec_in.at[r], send_sems, recv_sems, (1, r), peer(r)))
        for cp in scatter:
            cp.start()
        for cp in own:
            cp.wait()
        for cp in scatter:
            cp.wait_recv()
        mat_tot, vec_tot = mat_in[0], vec_in[0]
        for r in range(1, 8):
            mat_tot = mat_tot + mat_in[r]
            vec_tot = vec_tot + vec_in[r]
        mat_sum[...] = mat_tot
        vec_sum[...] = vec_tot
        keep = [pltpu.make_async_copy(mat_sum, mat_rows(mat_out, me), loc_sems.at[2]),
                pltpu.make_async_copy(vec_sum, vec_cols(vec_out, me), loc_sems.at[3])]
        for cp in keep:
            cp.start()
        spread = []
        for r in range(1, 8):
            spread.append(_remote(mat_sum, mat_rows(mat_out, me), send_sems, recv_sems, (2, r), peer(r)))
            spread.append(_remote(vec_sum, vec_cols(vec_out, me), send_sems, recv_sems, (3, r), peer(r)))
        for cp in spread:
            cp.start()
        for r in range(1, 8):
            p = index(peer(r))
            _remote(mat_sum, mat_rows(mat_out, p), send_sems, recv_sems, (2, r), peer(r)).wait_recv()
            _remote(vec_sum, vec_cols(vec_out, p), send_sems, recv_sems, (3, r), peer(r)).wait_recv()
        for cp in scatter + spread:
            cp.wait_send()
        for cp in keep:
            cp.wait()

    return pl.pallas_call(
        body, name="small_all_reduce",
        in_specs=[VMEM_SPEC, VMEM_SPEC], out_specs=[VMEM_SPEC, VMEM_SPEC],
        out_shape=[jax.ShapeDtypeStruct((MAT_ROWS, HD), F32), jax.ShapeDtypeStruct((VEC_ROWS, DR), F32)],
        scratch_shapes=[pltpu.VMEM((8, MAT_PIECE, HD), F32), pltpu.VMEM((8, VEC_ROWS, VEC_PIECE), F32),
                        pltpu.VMEM((MAT_PIECE, HD), F32), pltpu.VMEM((VEC_ROWS, VEC_PIECE), F32),
                        pltpu.SemaphoreType.DMA((4, 8)), pltpu.SemaphoreType.DMA((4, 8)), pltpu.SemaphoreType.DMA((4,))],
    )(dmat, dvec)


def _adam_small(grads, wmv):
    n = len(grads)

    def body(*refs):
        g_refs, rest = refs[:n], refs[n:]
        ins, outs = rest[:3 * n], rest[3 * n:]
        for i in range(n):
            d, m, v = _adamw(ins[3 * i][...], g_refs[i][...], ins[3 * i + 1][...], ins[3 * i + 2][...])
            outs[3 * i][...], outs[3 * i + 1][...], outs[3 * i + 2][...] = d, m, v

    flat = [a for t in wmv for a in t]
    return pl.pallas_call(
        body, name="adam_small",
        in_specs=[VMEM_SPEC] * (4 * n), out_specs=[VMEM_SPEC] * (3 * n),
        out_shape=[jax.ShapeDtypeStruct(a.shape, F32) for a in flat],
    )(*grads, *flat)


WEIGHT_NAMES = ("norm_mix", "w_in", "w_pool_grp", "pool_scale", "w_pool_out", "conv_w", "conv_b", "w_rg_a", "b_rg_a", "w_rg_x",
                "b_rg_x", "lru_lambda", "w_rnn_out", "w_o", "norm_ffn", "w_ffn_in", "w_ffn_out", "norm_final")


def kernel(x, norm_mix, w_in, w_pool_grp, pool_scale, w_pool_out, conv_w, conv_b, w_rg_a, b_rg_a, w_rg_x, b_rg_x, lru_lambda, w_rnn_out, w_o, norm_ffn, w_ffn_in, w_ffn_out, norm_final, loss_target, m_norm_mix, m_w_in, m_w_pool_grp, m_pool_scale, m_w_pool_out, m_conv_w, m_conv_b, m_w_rg_a, m_b_rg_a, m_w_rg_x, m_b_rg_x, m_lru_lambda, m_w_rnn_out, m_w_o, m_norm_ffn, m_w_ffn_in, m_w_ffn_out, m_norm_final, v_norm_mix, v_w_in, v_w_pool_grp, v_pool_scale, v_w_pool_out, v_conv_w, v_conv_b, v_w_rg_a, v_b_rg_a, v_w_rg_x, v_b_rg_x, v_lru_lambda, v_w_rnn_out, v_w_o, v_norm_ffn, v_w_ffn_in, v_w_ffn_out, v_norm_final):
    w = dict(norm_mix=norm_mix, w_in=w_in, w_pool_grp=w_pool_grp, pool_scale=pool_scale, w_pool_out=w_pool_out, conv_w=conv_w,
             conv_b=conv_b, w_rg_a=w_rg_a, b_rg_a=b_rg_a, w_rg_x=w_rg_x, b_rg_x=b_rg_x, lru_lambda=lru_lambda,
             w_rnn_out=w_rnn_out, w_o=w_o, norm_ffn=norm_ffn, w_ffn_in=w_ffn_in, w_ffn_out=w_ffn_out, norm_final=norm_final)
    m = dict(norm_mix=m_norm_mix, w_in=m_w_in, w_pool_grp=m_w_pool_grp, pool_scale=m_pool_scale, w_pool_out=m_w_pool_out,
             conv_w=m_conv_w, conv_b=m_conv_b, w_rg_a=m_w_rg_a, b_rg_a=m_b_rg_a, w_rg_x=m_w_rg_x, b_rg_x=m_b_rg_x,
             lru_lambda=m_lru_lambda, w_rnn_out=m_w_rnn_out, w_o=m_w_o, norm_ffn=m_norm_ffn, w_ffn_in=m_w_ffn_in,
             w_ffn_out=m_w_ffn_out, norm_final=m_norm_final)
    v = dict(norm_mix=v_norm_mix, w_in=v_w_in, w_pool_grp=v_w_pool_grp, pool_scale=v_pool_scale, w_pool_out=v_w_pool_out,
             conv_w=v_conv_w, conv_b=v_conv_b, w_rg_a=v_w_rg_a, b_rg_a=v_b_rg_a, w_rg_x=v_w_rg_x, b_rg_x=v_b_rg_x,
             lru_lambda=v_lru_lambda, w_rnn_out=v_w_rnn_out, w_o=v_w_o, norm_ffn=v_norm_ffn, w_ffn_in=v_w_ffn_in,
             w_ffn_out=v_w_ffn_out, norm_final=v_norm_final)
    xi, yi, ci = (lax.axis_index(a) for a in MESH_AXES)
    chip = 2 * xi + yi
    other = [2 * (1 - xi) + yi, 2 * xi + (1 - yi), 2 * (1 - xi) + (1 - yi)]

    def held(T, a):
        return jnp.swapaxes(a, 0, 1) if T.transposed else a

    conv_block = jnp.pad(conv_w[0], ((0, CONV_W.rows - 4), (0, 0)))
    gathered = _all_gather([held(T, w[T.name][0]).astype(T.dtype) for T in BIG] + [conv_block])
    full = {T.name: a for T, a in zip(GATHERED, gathered)}
    wts = (full["w_in"], full["w_pool_out"], full["conv_w"][0:4], full["w_rnn_out"], full["w_o"], full["w_ffn_in"],
           full["w_ffn_out"])
    small = (norm_mix, w_pool_grp[0], pool_scale, conv_b, w_rg_a[0], b_rg_a[0], w_rg_x[0], b_rg_x[0], lru_lambda, norm_ffn,
             norm_final.reshape(1, D))

    loss_part, grad_x, big, dmat, dvec = _local_step(x[0], loss_target[0], wts, small)
    loss = lax.psum(loss_part[0, 0], MESH_AXES)

    grads, delta, new_m, new_v = {}, {}, {}, {}
    where4 = jnp.stack(other + [ci]).astype(jnp.int32)
    where2 = jnp.stack([2 * chip + ci, chip]).astype(jnp.int32)
    lz1 = _pair_exchange(big)
    sums = [_pair_sum(T, g, l, where4) for T, g, l in zip(BIG, big, lz1)]
    lz2 = _chip_scatter(sums)
    for T, g, l1, l2 in zip(BIG, big, lz1, lz2):
        n = T.name
        out = _final_sum(T, g, l1, l2, where2, held(T, w[n][0]), held(T, m[n][0]), held(T, v[n][0]))
        grads[n], delta[n], new_m[n], new_v[n] = (held(T, a) for a in out)

    mat, vec = _small_all_reduce(dmat, dvec)
    me = 4 * xi + 2 * yi + ci
    small_grads = dict(
        w_pool_grp=mat[0:MAT_WA], w_rg_a=mat[MAT_WA:MAT_WX], w_rg_x=mat[MAT_WX:MAT_ROWS],
        pool_scale=vec[VEC_SCALE:VEC_SCALE + 1, 0:DP], conv_b=vec[VEC_CONV_B:VEC_CONV_B + 1],
        b_rg_a=vec[VEC_BA:VEC_BA + 1], b_rg_x=vec[VEC_BX:VEC_BX + 1], lru_lambda=vec[VEC_LAM:VEC_LAM + 1],
        conv_w=lax.dynamic_slice(vec, (VEC_CONV_W, VEC_PIECE * me), (4, VEC_PIECE)),
        norm_final=vec[VEC_NORM_FINAL:VEC_NORM_FINAL + 1], norm_ffn=vec[VEC_NORM_FFN:VEC_NORM_FFN + 1],
        norm_mix=vec[VEC_NORM_MIX:VEC_NORM_MIX + 1])
    names = list(small_grads)
    as2d = lambda a, g: a.reshape(g.shape)
    upd = _adam_small([small_grads[n] for n in names],
                      [(as2d(w[n], small_grads[n]), as2d(m[n], small_grads[n]), as2d(v[n], small_grads[n])) for n in names])
    for i, n in enumerate(names):
        grads[n] = small_grads[n]
        delta[n], new_m[n], new_v[n] = upd[3 * i:3 * i + 3]

    shaped = lambda d: [d[n].reshape(w[n].shape) for n in WEIGHT_NAMES]
    return (loss, grad_x[None], *shaped(grads), *shaped(delta), *shaped(new_m), *shaped(new_v))
```

```python
import functools
import math

import jax
import jax.numpy as jnp
from jax import lax
from jax.experimental import pallas as pl
from jax.experimental.pallas import tpu as pltpu

F32 = jnp.float32
BF16 = jnp.bfloat16

D = 1024
DP = 512
PG = 128
WINDOWS = (2, 4, 8, 16)
DR = 1024
NH = 8
HD = 128
DIN = 4608
DFF = 2816
EPS = 1e-6
LRU_C = 8.0
POOL_HALO = 16
CONV_HALO = 8

ADAM_LR = 0.001
ADAM_B1 = 0.9
ADAM_B2 = 0.999
ADAM_EPS = 1e-08
ADAM_WD = 0.01
ADAM_STEP = 10

VMEM_LIMIT = 56 * 1024 * 1024
MESH_AXES = ("x", "y", "c")
MESH = pl.DeviceIdType.MESH


def _dot(a, b):
    return jnp.dot(a, b, preferred_element_type=F32)


def _dot_nt(a, b):
    return lax.dot_general(a, b, (((1,), (1,)), ((), ())), preferred_element_type=F32)


def _dot_tn(a, b):
    return lax.dot_general(a, b, (((0,), (0,)), ((), ())), preferred_element_type=F32)


def _params(*sem):
    return pltpu.CompilerParams(dimension_semantics=sem, vmem_limit_bytes=VMEM_LIMIT)


def _resident(shape):
    nd = len(shape)
    return pl.BlockSpec(shape, lambda i: (0,) * nd, pipeline_mode=pl.Buffered(1))


def _rows(shape_cols, tm):
    return pl.BlockSpec((tm, shape_cols), lambda i: (i, 0))


def _call(body, name, grid, in_specs, out_specs, out_shape, operands, scratch_shapes=(), exchange=None, exchange_operands=()):
    n_in, n_out, n_scr = len(in_specs), len(out_specs), len(scratch_shapes)
    steps = math.prod(grid)
    if exchange is None:
        outs = pl.pallas_call(body, name=name, grid=grid, in_specs=in_specs, out_specs=out_specs, out_shape=out_shape,
                              scratch_shapes=list(scratch_shapes), compiler_params=_params(*["arbitrary"] * len(grid)))(*operands)
        return outs, []
    e_in, e_out = len(exchange.in_specs), len(exchange.out_specs)

    def hosted(*refs):
        ins, refs = refs[:n_in], refs[n_in:]
        e_ins, refs = refs[:e_in], refs[e_in:]
        outs, refs = refs[:n_out], refs[n_out:]
        e_outs, refs = refs[:e_out], refs[e_out:]
        scr, e_scr = refs[:n_scr], refs[n_scr:]
        step = pl.program_id(0)
        for axis in range(1, len(grid)):
            step = step * grid[axis] + pl.program_id(axis)
        pl.when(step == 0)(lambda: exchange.start(e_ins, e_outs, e_scr))
        if exchange.middle is not None:
            pl.when(step == exchange.middle_at(steps))(lambda: exchange.middle(e_ins, e_outs, e_scr))
        body(*ins, *outs, *scr)
        pl.when(step == steps - 1)(lambda: exchange.finish(e_ins, e_outs, e_scr))

    outs = pl.pallas_call(
        hosted, name=name, grid=grid, in_specs=list(in_specs) + exchange.in_specs,
        out_specs=list(out_specs) + exchange.out_specs, out_shape=list(out_shape) + exchange.out_shape,
        scratch_shapes=list(scratch_shapes) + exchange.scratch_shapes,
        compiler_params=_params(*["arbitrary"] * len(grid)))(*operands, *exchange_operands)
    return outs[:n_out], outs[n_out:]


def _gelu(x):
    c = math.sqrt(2.0 / math.pi)
    t = jnp.tanh(c * (x + 0.044715 * x * x * x))
    return 0.5 * x * (1.0 + t), t


def _gelu_grad(x, t):
    c = math.sqrt(2.0 / math.pi)
    return 0.5 * (1.0 + t) + 0.5 * x * (1.0 - t * t) * (c * (1.0 + 3.0 * 0.044715 * x * x))


def _softplus_neg(lam):
    z = jnp.exp(-jnp.abs(lam))
    u = 1.0 + z
    dlt = u - 1.0
    log1p = jnp.where(dlt == 0.0, z, jnp.log(u) * (z / jnp.where(dlt == 0.0, 1.0, dlt)))
    return jnp.maximum(-lam, 0.0) + log1p


def _sigmoid(x):
    return 0.5 * jnp.tanh(0.5 * x) + 0.5


def _linear_scan(A, B, reverse):
    n, cols = A.shape
    rows = lax.broadcasted_iota(jnp.int32, (n, 1), 0)
    d = 1
    while d < n:
        if d < 8:
            keep = (rows < n - d) if reverse else (rows >= d)
            shift = n - d if reverse else d
            B = jnp.where(keep, A * pltpu.roll(B, shift, axis=0) + B, B)
            A = jnp.where(keep, A * pltpu.roll(A, shift, axis=0), A)
        else:
            def shifted(v, fill):
                pad = jnp.full((d, cols), fill, v.dtype)
                return jnp.concatenate([v[d:], pad] if reverse else [pad, v[:n - d]], axis=0)
            B = A * shifted(B, 0.0) + B
            A = A * shifted(A, 1.0)
        d *= 2
    return A, B


def _pool_windows(ext, shift_sign):
    n = ext.shape[0]
    s = ext
    outs = []
    for w in WINDOWS:
        d = w // 2
        s = s + pltpu.roll(s, d if shift_sign > 0 else n - d, axis=0)
        outs.append(s[:, :PG])
        s = s[:, PG:]
    return outs


def _conv_taps(uext):
    taps = []
    for k in range(4):
        sh = 3 - k
        v = uext if sh == 0 else pltpu.roll(uext, sh, axis=0)
        taps.append(v[CONV_HALO:, :])
    return taps


def _gates(v, wa_ref, ba_ref, wx_ref, bx_ref, sp):
    vb = v.astype(BF16)
    ra, rx = [], []
    for h in range(NH):
        vh = vb[:, h * HD:(h + 1) * HD]
        ra.append(_dot(vh, wa_ref[h]))
        rx.append(_dot(vh, wx_ref[h]))
    r = _sigmoid(jnp.concatenate(ra, axis=1) + ba_ref[...])
    i = _sigmoid(jnp.concatenate(rx, axis=1) + bx_ref[...])
    log_a = (-LRU_C) * r * sp
    a = jnp.exp(log_a)
    mult = jnp.sqrt(-jnp.tanh(log_a) * (1.0 + a * a))
    return r, i, a, mult


def _in_proj(x, norm_mix, w_in, exchange=None, exchange_operands=(), tm=512):
    S = x.shape[0]

    def body(x_ref, g_ref, w_ref, proj_ref, h_ref):
        xv = x_ref[...]
        r = lax.rsqrt(jnp.mean(xv * xv, axis=-1, keepdims=True) + EPS)
        h = (xv * r * g_ref[...]).astype(BF16)
        h_ref[...] = h
        for n0 in range(0, DIN, 512):
            proj_ref[:, n0:n0 + 512] = _dot_nt(h, w_ref[n0:n0 + 512, :])

    return _call(
        body, "in_proj", (S // tm,),
        in_specs=[_rows(D, tm), _resident((1, D)), _resident((DIN, D))],
        out_specs=[_rows(DIN, tm), _rows(D, tm)],
        out_shape=[jax.ShapeDtypeStruct((S, DIN), F32), jax.ShapeDtypeStruct((S, D), BF16)],
        operands=(x, norm_mix, w_in), exchange=exchange, exchange_operands=exchange_operands)


def _mixer_fwd(proj, wg, scale, w_pool_out, conv_w, conv_b, wa, ba, wx, bx, lam, w_rnn_out, exchange=None,
               exchange_operands=(), tm=256):
    S = proj.shape[0]
    UW = DP + 2 * DR

    def body(proj_ref, wg_ref, scale_ref, wpo_ref, cw_ref, cb_ref, wa_ref, ba_ref, wx_ref, bx_ref, lam_ref, wro_ref,
             pooled_ref, pm_ref, ypool_ref, hr_ref, z_ref, yrnn_ref, pool_carry, conv_carry, h_carry):
        i = pl.program_id(0)

        @pl.when(i == 0)
        def _():
            pool_carry[...] = jnp.zeros_like(pool_carry)
            conv_carry[...] = jnp.zeros_like(conv_carry)
            h_carry[...] = jnp.zeros_like(h_carry)

        rows = lax.broadcasted_iota(jnp.int32, (tm, 1), 0)
        t_glob = i * tm + rows

        u_pool = proj_ref[:, 0:DP]
        ext = jnp.concatenate([pool_carry[...], u_pool], axis=0)
        pool_carry[...] = u_pool[tm - POOL_HALO:, :]
        sums = _pool_windows(ext, +1)
        mixed = []
        for g, w in enumerate(WINDOWS):
            cnt = jnp.minimum(t_glob + 1, w).astype(F32)
            pooled_g = sums[g][POOL_HALO:, :] / cnt - u_pool[:, g * PG:(g + 1) * PG]
            pooled_b = pooled_g.astype(BF16)
            pooled_ref[:, g * PG:(g + 1) * PG] = pooled_b
            mixed.append(_dot(pooled_b, wg_ref[g]))
        pm = (jnp.concatenate(mixed, axis=1) * scale_ref[...]).astype(BF16)
        pm_ref[...] = pm
        ypool_ref[...] = _dot(pm, wpo_ref[...])

        u_rnn = proj_ref[:, DP:DP + DR]
        uext = jnp.concatenate([conv_carry[...], u_rnn], axis=0)
        conv_carry[...] = u_rnn[tm - CONV_HALO:, :]
        taps = _conv_taps(uext)
        v = cb_ref[...]
        for k in range(4):
            v = v + taps[k] * cw_ref[k:k + 1, :]
        sp = _softplus_neg(lam_ref[...])
        _, gi, a, mult = _gates(v, wa_ref, ba_ref, wx_ref, bx_ref, sp)
        b = mult * gi * v
        A, B = _linear_scan(a, b, reverse=False)
        hr = A * h_carry[0:1, :] + B
        h_carry[0:1, :] = hr[tm - 1:tm, :]
        hr_ref[...] = hr
        gg, _ = _gelu(proj_ref[:, DP + DR:UW])
        z = (hr * gg).astype(BF16)
        z_ref[...] = z
        yrnn_ref[...] = _dot(z, wro_ref[...])

    return _call(
        body, "mixer_fwd", (S // tm,),
        in_specs=[_rows(UW, tm), _resident((4, PG, PG)), _resident((1, DP)), _resident((DP, D)), _resident((4, DR)),
                  _resident((1, DR)), _resident((NH, HD, HD)), _resident((1, DR)), _resident((NH, HD, HD)),
                  _resident((1, DR)), _resident((1, DR)), _resident((DR, D))],
        out_specs=[_rows(DP, tm), _rows(DP, tm), _rows(D, tm), _rows(DR, tm), _rows(DR, tm), _rows(D, tm)],
        out_shape=[jax.ShapeDtypeStruct((S, DP), BF16), jax.ShapeDtypeStruct((S, DP), BF16),
                   jax.ShapeDtypeStruct((S, D), F32), jax.ShapeDtypeStruct((S, DR), F32),
                   jax.ShapeDtypeStruct((S, DR), BF16), jax.ShapeDtypeStruct((S, D), F32)],
        scratch_shapes=[pltpu.VMEM((POOL_HALO, DP), F32), pltpu.VMEM((CONV_HALO, DR), F32), pltpu.VMEM((8, DR), F32)],
        operands=(proj, wg, scale, w_pool_out, conv_w, conv_b, wa, ba, wx, bx, lam, w_rnn_out),
        exchange=exchange, exchange_operands=exchange_operands)


FF_CHUNK = DFF // 4


def _rms(x):
    r = lax.rsqrt(jnp.mean(x * x, axis=-1, keepdims=True) + EPS)
    return r, x * r


def _rms_bwd(dh, g, r, xh):
    dxh = dh * g
    return r * (dxh - xh * jnp.mean(dxh * xh, axis=-1, keepdims=True))


def _merge_out(x, proj, y_pool, y_rnn, w_o, norm_ffn, exchange=None, exchange_operands=(), tm=512):
    S = x.shape[0]
    GL0 = (DP + 2 * DR) // 512

    def gl_spec(k):
        return pl.BlockSpec((tm, 512), lambda i: (i, GL0 + k))

    def body(x_ref, gl0, gl1, gl2, gl3, yp_ref, yr_ref, wo_ref, gf_ref, mix_ref, x2_ref, h2_ref):
        s_p = _sigmoid(jnp.concatenate([gl0[...], gl1[...]], axis=1))
        s_r = _sigmoid(jnp.concatenate([gl2[...], gl3[...]], axis=1))
        mix = (s_p * yp_ref[...] + s_r * yr_ref[...]).astype(BF16)
        mix_ref[...] = mix
        x2 = x_ref[...] + _dot(mix, wo_ref[...])
        x2_ref[...] = x2
        _, xh2 = _rms(x2)
        h2_ref[...] = (xh2 * gf_ref[...]).astype(BF16)

    return _call(
        body, "merge_out", (S // tm,),
        in_specs=[_rows(D, tm), gl_spec(0), gl_spec(1), gl_spec(2), gl_spec(3), _rows(D, tm), _rows(D, tm),
                  _resident((D, D)), _resident((1, D))],
        out_specs=[_rows(D, tm), _rows(D, tm), _rows(D, tm)],
        out_shape=[jax.ShapeDtypeStruct((S, D), BF16), jax.ShapeDtypeStruct((S, D), F32), jax.ShapeDtypeStruct((S, D), BF16)],
        operands=(x, proj, proj, proj, proj, y_pool, y_rnn, w_o, norm_ffn),
        exchange=exchange, exchange_operands=exchange_operands)


def _ffn_up(h2, w_ffn_in, tm=512):
    S = h2.shape[0]

    def body(h_ref, w_ref, gu_ref, act_ref):
        h = h_ref[...]
        for c0 in range(0, DFF, FF_CHUNK):
            gate = _dot_nt(h, w_ref[c0:c0 + FF_CHUNK, :])
            up = _dot_nt(h, w_ref[DFF + c0:DFF + c0 + FF_CHUNK, :])
            gu_ref[:, c0:c0 + FF_CHUNK] = gate.astype(BF16)
            gu_ref[:, DFF + c0:DFF + c0 + FF_CHUNK] = up.astype(BF16)
            act_ref[:, c0:c0 + FF_CHUNK] = (gate * _sigmoid(gate) * up).astype(BF16)

    return pl.pallas_call(
        body, name="ffn_up", grid=(S // tm,),
        in_specs=[_rows(D, tm), _resident((2 * DFF, D))],
        out_specs=[_rows(2 * DFF, tm), _rows(DFF, tm)],
        out_shape=[jax.ShapeDtypeStruct((S, 2 * DFF), BF16), jax.ShapeDtypeStruct((S, DFF), BF16)],
        compiler_params=_params("parallel"),
    )(h2, w_ffn_in)


def _ffn_down_loss(act, x2, target, w_ffn_out, norm_final, tm=512):
    S = act.shape[0]

    def body(act_ref, x2_ref, t_ref, w_ref, gn_ref, dx3_ref, dx3b_ref, loss_ref, dvec_ref):
        i = pl.program_id(0)

        @pl.when(i == 0)
        def _():
            loss_ref[...] = jnp.zeros_like(loss_ref)
            dvec_ref[...] = jnp.zeros_like(dvec_ref)

        x3 = x2_ref[...] + _dot(act_ref[...], w_ref[...])
        r3, xh3 = _rms(x3)
        g_fin = gn_ref[...]
        e = xh3 * g_fin - t_ref[...]
        loss_ref[...] += jnp.sum(e * e, axis=(0, 1), keepdims=True) * (0.5 / D)
        dy = e * (1.0 / D)
        dvec_ref[0:1, :] += jnp.sum(dy * xh3, axis=0, keepdims=True)
        dx3 = _rms_bwd(dy, g_fin, r3, xh3)
        dx3_ref[...] = dx3
        dx3b_ref[...] = dx3.astype(BF16)

    return pl.pallas_call(
        body, name="ffn_down_loss", grid=(S // tm,),
        in_specs=[_rows(DFF, tm), _rows(D, tm), _rows(D, tm), _resident((DFF, D)), _resident((1, D))],
        out_specs=[_rows(D, tm), _rows(D, tm), _resident((1, 1)), _resident((8, D))],
        out_shape=[jax.ShapeDtypeStruct((S, D), F32), jax.ShapeDtypeStruct((S, D), BF16),
                   jax.ShapeDtypeStruct((1, 1), F32), jax.ShapeDtypeStruct((8, D), F32)],
        compiler_params=_params("arbitrary"),
    )(act, x2, target, w_ffn_out, norm_final)


def _ffn_bwd_down(dx3b, gu, w_ffn_out, tm=512):
    S = dx3b.shape[0]

    def body(d_ref, gu_ref, w_ref, dgu_ref):
        d = d_ref[...]
        for c0 in range(0, DFF, FF_CHUNK):
            dact = _dot_nt(d, w_ref[c0:c0 + FF_CHUNK, :])
            gate = gu_ref[:, c0:c0 + FF_CHUNK].astype(F32)
            up = gu_ref[:, DFF + c0:DFF + c0 + FF_CHUNK].astype(F32)
            sg = _sigmoid(gate)
            dgu_ref[:, c0:c0 + FF_CHUNK] = (dact * up * (sg * (1.0 + gate * (1.0 - sg)))).astype(BF16)
            dgu_ref[:, DFF + c0:DFF + c0 + FF_CHUNK] = (dact * (gate * sg)).astype(BF16)

    return pl.pallas_call(
        body, name="ffn_bwd_down", grid=(S // tm,),
        in_specs=[_rows(D, tm), _rows(2 * DFF, tm), _resident((DFF, D))],
        out_specs=_rows(2 * DFF, tm),
        out_shape=jax.ShapeDtypeStruct((S, 2 * DFF), BF16),
        compiler_params=_params("parallel"),
    )(dx3b, gu, w_ffn_out)


def _ffn_bwd_up(dgu, x2, dx3, w_ffn_in, norm_ffn, w_o, tm=512):
    S = dgu.shape[0]

    def body(dgu_ref, x2_ref, dx3_ref, wfi_ref, gf_ref, wo_ref, dx2_ref, dx2b_ref, dmixo_ref, dvec_ref):
        i = pl.program_id(0)

        @pl.when(i == 0)
        def _():
            dvec_ref[...] = jnp.zeros_like(dvec_ref)

        dh2 = _dot(dgu_ref[:, 0:DFF], wfi_ref[0:DFF, :]) + _dot(dgu_ref[:, DFF:2 * DFF], wfi_ref[DFF:2 * DFF, :])
        r2, xh2 = _rms(x2_ref[...])
        dvec_ref[0:1, :] += jnp.sum(dh2 * xh2, axis=0, keepdims=True)
        dx2 = dx3_ref[...] + _rms_bwd(dh2, gf_ref[...], r2, xh2)
        dx2_ref[...] = dx2
        dx2b = dx2.astype(BF16)
        dx2b_ref[...] = dx2b
        dmixo_ref[...] = _dot_nt(dx2b, wo_ref[...])

    return pl.pallas_call(
        body, name="ffn_bwd_up", grid=(S // tm,),
        in_specs=[_rows(2 * DFF, tm), _rows(D, tm), _rows(D, tm), _resident((2 * DFF, D)), _resident((1, D)),
                  _resident((D, D))],
        out_specs=[_rows(D, tm), _rows(D, tm), _rows(D, tm), _resident((8, D))],
        out_shape=[jax.ShapeDtypeStruct((S, D), F32), jax.ShapeDtypeStruct((S, D), BF16), jax.ShapeDtypeStruct((S, D), F32),
                   jax.ShapeDtypeStruct((8, D), F32)],
        compiler_params=_params("arbitrary"),
    )(dgu, x2, dx3, w_ffn_in, norm_ffn, w_o)


VEC_ROWS = 16
MAT_WA = 4 * PG
MAT_WX = MAT_WA + NH * HD
MAT_ROWS = MAT_WX + NH * HD


def _mixer_bwd(proj, dmixo, y_pool, y_rnn, hr, wg, scale, w_pool_out, conv_w, conv_b, wa, ba, wx, bx, lam, w_rnn_out,
               exchange=None, exchange_operands=(), tm=256):
    S = proj.shape[0]
    nt = S // tm

    def rev(cols):
        return pl.BlockSpec((tm, cols), lambda i: (nt - 1 - i, 0))

    def halo(rows_, cols):
        per = tm // rows_
        return pl.BlockSpec((rows_, cols), lambda i: (jnp.maximum((nt - 1 - i) * per - 1, 0), 0))

    def body(proj_ref, projh_ref, dmixo_ref, yp_ref, yr_ref, hr_ref, hrh_ref, wg_ref, scale_ref, wpo_ref, cw_ref, cb_ref,
             wa_ref, ba_ref, wx_ref, bx_ref, lam_ref, wro_ref,
             dproj_ref, dypb_ref, dyrb_ref, dmat_ref, dvec_ref,
             q_carry, dv_carry, a_carry, g_carry):
        i = pl.program_id(0)
        ti = nt - 1 - i

        @pl.when(i == 0)
        def _():
            q_carry[...] = jnp.zeros_like(q_carry)
            dv_carry[...] = jnp.zeros_like(dv_carry)
            a_carry[...] = jnp.zeros_like(a_carry)
            g_carry[...] = jnp.zeros_like(g_carry)
            dmat_ref[...] = jnp.zeros_like(dmat_ref)
            dvec_ref[...] = jnp.zeros_like(dvec_ref)

        rows = lax.broadcasted_iota(jnp.int32, (tm, 1), 0)
        t_glob = ti * tm + rows
        has_prev = (ti > 0).astype(F32)
        dmixo = dmixo_ref[...]

        s_p = _sigmoid(proj_ref[:, DP + 2 * DR:DP + 2 * DR + D])
        s_r = _sigmoid(proj_ref[:, DP + 2 * DR + D:DIN])
        dproj_ref[:, DP + 2 * DR:DP + 2 * DR + D] = (dmixo * yp_ref[...] * s_p * (1.0 - s_p)).astype(BF16)
        dproj_ref[:, DP + 2 * DR + D:DIN] = (dmixo * yr_ref[...] * s_r * (1.0 - s_r)).astype(BF16)
        dyp = (dmixo * s_p).astype(BF16)
        dyr = (dmixo * s_r).astype(BF16)
        dypb_ref[...] = dyp
        dyrb_ref[...] = dyr

        dz = _dot_nt(dyr, wro_ref[...])
        u_gate = proj_ref[:, DP + DR:DP + 2 * DR]
        gg, tg = _gelu(u_gate)
        hr_t = hr_ref[...]
        dproj_ref[:, DP + DR:DP + 2 * DR] = (dz * hr_t * _gelu_grad(u_gate, tg)).astype(BF16)
        dhr = dz * gg

        u_rnn = proj_ref[:, DP:DP + DR]
        uext = jnp.concatenate([projh_ref[POOL_HALO - CONV_HALO:, DP:DP + DR] * has_prev, u_rnn], axis=0)
        taps = _conv_taps(uext)
        v = cb_ref[...]
        for k in range(4):
            v = v + taps[k] * cw_ref[k:k + 1, :]
        sp = _softplus_neg(lam_ref[...])
        r, gi, a, mult = _gates(v, wa_ref, ba_ref, wx_ref, bx_ref, sp)

        C = jnp.where(rows == tm - 1, a_carry[0:1, :], pltpu.roll(a, tm - 1, axis=0))
        C, G = _linear_scan(C, dhr, reverse=True)
        g = G + C * g_carry[0:1, :]
        a_carry[0:1, :] = a[0:1, :]
        g_carry[0:1, :] = g[0:1, :]

        h_prev = jnp.where(rows == 0, hrh_ref[7:8, :] * has_prev, pltpu.roll(hr_t, 1, axis=0))
        da = g * h_prev
        dmult = g * gi * v
        di = g * mult * v
        dv = g * mult * gi
        dlog_a = da * a - dmult * (a * a / mult)
        dvec_ref[4:5, :] += jnp.sum(dlog_a * r, axis=0, keepdims=True)
        dra = (dlog_a * ((-LRU_C) * sp) * r * (1.0 - r))
        drx = di * gi * (1.0 - gi)
        dvec_ref[2:3, :] += jnp.sum(dra, axis=0, keepdims=True)
        dvec_ref[3:4, :] += jnp.sum(drx, axis=0, keepdims=True)
        drab = dra.astype(BF16)
        drxb = drx.astype(BF16)
        vb = v.astype(BF16)
        dvg = []
        for h in range(NH):
            sl = slice(h * HD, (h + 1) * HD)
            dvg.append(_dot_nt(drab[:, sl], wa_ref[h]) + _dot_nt(drxb[:, sl], wx_ref[h]))
            dmat_ref[MAT_WA + h * HD:MAT_WA + (h + 1) * HD, :] += _dot_tn(vb[:, sl], drab[:, sl])
            dmat_ref[MAT_WX + h * HD:MAT_WX + (h + 1) * HD, :] += _dot_tn(vb[:, sl], drxb[:, sl])
        dv = dv + jnp.concatenate(dvg, axis=1)
        dvec_ref[1:2, :] += jnp.sum(dv, axis=0, keepdims=True)
        for k in range(4):
            dvec_ref[5 + k:6 + k, :] += jnp.sum(dv * taps[k], axis=0, keepdims=True)
        dvext = jnp.concatenate([dv, dv_carry[...]], axis=0)
        dv_carry[...] = dv[0:CONV_HALO, :]
        n = tm + CONV_HALO
        du_rnn = dv * cw_ref[3:4, :]
        for k in range(3):
            du_rnn = du_rnn + pltpu.roll(dvext, n - (3 - k), axis=0)[0:tm, :] * cw_ref[k:k + 1, :]
        dproj_ref[:, DP:DP + DR] = du_rnn.astype(BF16)

        dpm = _dot_nt(dyp, wpo_ref[...])
        u_pool = proj_ref[:, 0:DP]
        ext = jnp.concatenate([projh_ref[:, 0:DP] * has_prev, u_pool], axis=0)
        sums = _pool_windows(ext, +1)
        scale_v = scale_ref[...]
        qs = []
        dpooled = []
        dscale = []
        for gi_, w in enumerate(WINDOWS):
            sl = slice(gi_ * PG, (gi_ + 1) * PG)
            cnt = jnp.minimum(t_glob + 1, w).astype(F32)
            pooled_b = (sums[gi_][POOL_HALO:, :] / cnt - u_pool[:, sl]).astype(BF16)
            mixed_g = _dot(pooled_b, wg_ref[gi_])
            dscale.append(jnp.sum(dpm[:, sl] * mixed_g, axis=0, keepdims=True))
            dmixed_b = (dpm[:, sl] * scale_v[:, sl]).astype(BF16)
            dmat_ref[gi_ * PG:(gi_ + 1) * PG, :] += _dot_tn(pooled_b, dmixed_b)
            dp_g = _dot_nt(dmixed_b, wg_ref[gi_])
            dpooled.append(dp_g)
            qs.append(dp_g / cnt)
        dvec_ref[0:1, 0:DP] += jnp.concatenate(dscale, axis=1)
        q = jnp.concatenate(qs, axis=1)
        qext = jnp.concatenate([q, q_carry[...]], axis=0)
        q_carry[...] = q[0:POOL_HALO, :]
        tsum = _pool_windows(qext, -1)
        for gi_ in range(4):
            dproj_ref[:, gi_ * PG:(gi_ + 1) * PG] = (tsum[gi_][0:tm, :] - dpooled[gi_]).astype(BF16)

        @pl.when(i == nt - 1)
        def _():
            dvec_ref[4:5, :] = dvec_ref[4:5, :] * (LRU_C * _sigmoid(-lam_ref[...]))

    return _call(
        body, "mixer_bwd", (nt,),
        in_specs=[rev(DIN), halo(POOL_HALO, DIN), rev(D), rev(D), rev(D), rev(DR), halo(8, DR),
                  _resident((4, PG, PG)), _resident((1, DP)), _resident((DP, D)), _resident((4, DR)), _resident((1, DR)),
                  _resident((NH, HD, HD)), _resident((1, DR)), _resident((NH, HD, HD)), _resident((1, DR)),
                  _resident((1, DR)), _resident((DR, D))],
        out_specs=[rev(DIN), rev(D), rev(D), _resident((MAT_ROWS, HD)), _resident((VEC_ROWS, DR))],
        out_shape=[jax.ShapeDtypeStruct((S, DIN), BF16), jax.ShapeDtypeStruct((S, D), BF16),
                   jax.ShapeDtypeStruct((S, D), BF16), jax.ShapeDtypeStruct((MAT_ROWS, HD), F32),
                   jax.ShapeDtypeStruct((VEC_ROWS, DR), F32)],
        scratch_shapes=[pltpu.VMEM((POOL_HALO, DP), F32), pltpu.VMEM((CONV_HALO, DR), F32), pltpu.VMEM((8, DR), F32),
                        pltpu.VMEM((8, DR), F32)],
        operands=(proj, proj, dmixo, y_pool, y_rnn, hr, hr, wg, scale, w_pool_out, conv_w, conv_b, wa, ba, wx, bx, lam,
                  w_rnn_out),
        exchange=exchange, exchange_operands=exchange_operands)


def _in_bwd(dproj, x, dx2, norm_mix, w_in, tm=512):
    S = x.shape[0]

    def body(dp_ref, x_ref, dx2_ref, g_ref, w_ref, dx_ref, dg_ref):
        i = pl.program_id(0)

        @pl.when(i == 0)
        def _():
            dg_ref[...] = jnp.zeros_like(dg_ref)

        dh = _dot(dp_ref[:, 0:1536], w_ref[0:1536, :])
        dh = dh + _dot(dp_ref[:, 1536:3072], w_ref[1536:3072, :])
        dh = dh + _dot(dp_ref[:, 3072:DIN], w_ref[3072:DIN, :])
        xv = x_ref[...]
        r = lax.rsqrt(jnp.mean(xv * xv, axis=-1, keepdims=True) + EPS)
        xh = xv * r
        dg_ref[0:1, :] += jnp.sum(dh * xh, axis=0, keepdims=True)
        dxh = dh * g_ref[...]
        dx_ref[...] = dx2_ref[...] + r * (dxh - xh * jnp.mean(dxh * xh, axis=-1, keepdims=True))

    return pl.pallas_call(
        body, name="in_bwd", grid=(S // tm,),
        in_specs=[_rows(DIN, tm), _rows(D, tm), _rows(D, tm), _resident((1, D)), _resident((DIN, D))],
        out_specs=[_rows(D, tm), _resident((8, D))],
        out_shape=[jax.ShapeDtypeStruct((S, D), F32), jax.ShapeDtypeStruct((8, D), F32)],
        compiler_params=_params("arbitrary"),
    )(dproj, x, dx2, norm_mix, w_in)


def _wgrad(a, b, name, tk, tn, exchange=None, exchange_operands=()):
    S, K = a.shape
    N = b.shape[1]

    def body(a_ref, b_ref, o_ref):
        o_ref[...] = _dot_tn(a_ref[...], b_ref[...]).astype(BF16)

    (out,), exchanged = _call(
        body, name, (K // tk, N // tn),
        in_specs=[pl.BlockSpec((S, tk), lambda k, n: (0, k)), pl.BlockSpec((S, tn), lambda k, n: (0, n))],
        out_specs=[pl.BlockSpec((tk, tn), lambda k, n: (k, n))],
        out_shape=[jax.ShapeDtypeStruct((K, N), BF16)],
        operands=(a, b), exchange=exchange, exchange_operands=exchange_operands)
    return (out, exchanged) if exchange is not None else out


VEC_SCALE, VEC_CONV_B, VEC_BA, VEC_BX, VEC_LAM, VEC_CONV_W, VEC_NORM_FINAL, VEC_NORM_FFN, VEC_NORM_MIX = 0, 1, 2, 3, 4, 5, 9, 10, 11


class _Big:
    def __init__(self, name, rows, cols, axis, n, dtype=BF16, transposed=False):
        self.name, self.rows, self.cols, self.axis, self.n, self.dtype = name, rows, cols, axis, n, dtype
        self.transposed = transposed
        self.block_shape = (rows, n) if axis == 1 else (n, cols)

    def block(self, ref, p):
        if self.axis == 1:
            return ref.at[:, pl.ds(pl.multiple_of(p * self.n, 128), self.n)]
        return ref.at[pl.ds(pl.multiple_of(p * self.n, 16 if self.dtype == BF16 else 8), self.n), :]

    def block_index(self, p):
        return (0, p) if self.axis == 1 else (p, 0)


BIG = (_Big("w_in", DIN, D, 0, DIN // 8, transposed=True), _Big("w_pool_out", DP, D, 1, D // 8),
       _Big("w_rnn_out", DR, D, 0, DR // 8), _Big("w_o", D, D, 0, D // 8),
       _Big("w_ffn_in", 2 * DFF, D, 0, 2 * DFF // 8, transposed=True), _Big("w_ffn_out", DFF, D, 0, DFF // 8))
CONV_W = _Big("conv_w", 8, DR, 1, DR // 8, F32)
GATHERED = BIG + (CONV_W,)

HBM_SPEC = pl.BlockSpec(memory_space=pl.ANY)
VMEM_SPEC = pl.BlockSpec(memory_space=pltpu.VMEM)


def _place():
    x, y, c = (lax.axis_index(a) for a in MESH_AXES)
    other_chips = [(1 - x, y), (x, 1 - y), (1 - x, 1 - y)]
    return x, y, c, other_chips


def _remote(src, dst, send_sems, recv_sems, idx, to):
    return pltpu.make_async_remote_copy(src_ref=src, dst_ref=dst, send_sem=send_sems.at[idx], recv_sem=recv_sems.at[idx],
                                        device_id=to, device_id_type=MESH)


def _device_index(chip, core):
    return 4 * chip[0] + 2 * chip[1] + core


class _Gather:
    def __init__(self, tensors):
        self.tensors = tuple(tensors)
        n = len(self.tensors)
        self.in_specs = [HBM_SPEC] * n
        self.out_specs = [HBM_SPEC] * n
        self.out_shape = [jax.ShapeDtypeStruct((T.rows, T.cols), T.dtype) for T in self.tensors]
        self.scratch_shapes = [pltpu.VMEM(T.block_shape, T.dtype) for T in self.tensors] + [
            pltpu.SemaphoreType.DMA((n, 7)), pltpu.SemaphoreType.DMA((n, 7)), pltpu.SemaphoreType.DMA((n, 2))]

    def middle_at(self, steps):
        return (3 * steps) // 4

    def _copies(self, ins, outs, scratch):
        n = len(self.tensors)
        mine, (send_sems, recv_sems, loc_sems) = scratch[:n], scratch[n:]
        x, y, c, chips = _place()
        sibling = (x, y, 1 - c)
        me = _device_index((x, y), c)
        loads, stores, first, passed, arrivals, late = [], [], [], [], [], []
        for t, T in enumerate(self.tensors):
            place = T.block(outs[t], me)
            loads.append(pltpu.make_async_copy(ins[t], mine[t], loc_sems.at[t, 0]))
            stores.append(pltpu.make_async_copy(mine[t], place, loc_sems.at[t, 1]))
            first.append(_remote(mine[t], place, send_sems, recv_sems, (t, 0), sibling))
            theirs = T.block(outs[t], _device_index((x, y), 1 - c))
            late.append(_remote(theirs, theirs, send_sems, recv_sems, (t, 0), sibling))
            for k, chip in enumerate(chips):
                first.append(_remote(mine[t], place, send_sems, recv_sems, (t, 1 + k), (*chip, c)))
                land = T.block(outs[t], _device_index(chip, c))
                arrivals.append(_remote(land, land, send_sems, recv_sems, (t, 1 + k), sibling))
                passed.append(_remote(land, land, send_sems, recv_sems, (t, 4 + k), sibling))
                theirs = T.block(outs[t], _device_index(chip, 1 - c))
                late.append(_remote(theirs, theirs, send_sems, recv_sems, (t, 4 + k), sibling))
        return loads, stores, first, passed, arrivals, late

    def start(self, ins, outs, scratch):
        loads, stores, first, _, _, _ = self._copies(ins, outs, scratch)
        for cp in loads:
            cp.start()
        for cp in loads:
            cp.wait()
        for cp in stores + first:
            cp.start()

    def middle(self, ins, outs, scratch):
        _, _, _, passed, arrivals, _ = self._copies(ins, outs, scratch)
        for arrived, cp in zip(arrivals, passed):
            arrived.wait_recv()
            cp.start()

    def finish(self, ins, outs, scratch):
        _, stores, first, passed, _, late = self._copies(ins, outs, scratch)
        for cp in late:
            cp.wait_recv()
        for cp in first + passed:
            cp.wait_send()
        for cp in stores:
            cp.wait()


def _all_gather(blocks, tensors, name):
    gather = _Gather(tensors)

    def body(*refs):
        n = len(gather.tensors)
        ins, outs, scratch = refs[:n], refs[n:2 * n], refs[2 * n:]
        gather.start(ins, outs, scratch)
        gather.middle(ins, outs, scratch)
        gather.finish(ins, outs, scratch)

    return pl.pallas_call(
        body, name=name, in_specs=gather.in_specs, out_specs=gather.out_specs, out_shape=gather.out_shape,
        scratch_shapes=gather.scratch_shapes, compiler_params=pltpu.CompilerParams(vmem_limit_bytes=VMEM_LIMIT),
    )(*blocks)


def _pair_exchange(grads, tensors, name):
    nt = len(tensors)

    def body(*refs):
        ins, outs = refs[:nt], refs[nt:2 * nt]
        send_sems, recv_sems = refs[2 * nt:]
        x, y, c, _ = _place()
        cps = []
        for t, T in enumerate(tensors):
            for j in range(4):
                cp = _remote(T.block(ins[t], 2 * j + 1 - c), outs[t].at[j], send_sems, recv_sems, (t, j), (x, y, 1 - c))
                cp.start()
                cps.append(cp)
        for cp in cps:
            cp.wait()

    return pl.pallas_call(
        body, name=name,
        in_specs=[HBM_SPEC] * nt, out_specs=[HBM_SPEC] * nt,
        out_shape=[jax.ShapeDtypeStruct((4,) + T.block_shape, BF16) for T in tensors],
        scratch_shapes=[pltpu.SemaphoreType.DMA((nt, 4)), pltpu.SemaphoreType.DMA((nt, 4))],
    )(*grads)


def _pair_sum(T, g, lz1, where):
    blk = T.block_shape

    def body(where_ref, g_ref, l_ref, o_ref):
        o_ref[...] = (g_ref[...].astype(F32) + l_ref[...].astype(F32)).astype(BF16)

    return pl.pallas_call(
        body, name="grad_pair_sum_" + T.name,
        grid_spec=pltpu.PrefetchScalarGridSpec(
            num_scalar_prefetch=1, grid=(3,),
            in_specs=[pl.BlockSpec(blk, lambda k, w: T.block_index(2 * w[k] + w[3])),
                      pl.BlockSpec((None,) + blk, lambda k, w: (w[k], 0, 0))],
            out_specs=pl.BlockSpec((None,) + blk, lambda k, w: (k, 0, 0))),
        out_shape=jax.ShapeDtypeStruct((3,) + T.block_shape, BF16),
        compiler_params=_params("arbitrary"),
    )(where, g, lz1)


class _Scatter:
    middle = None

    def __init__(self, tensors):
        n = len(tensors)
        self.in_specs = [HBM_SPEC] * n
        self.out_specs = [HBM_SPEC] * n
        self.out_shape = [jax.ShapeDtypeStruct((3,) + T.block_shape, BF16) for T in tensors]
        self.scratch_shapes = [pltpu.SemaphoreType.DMA((n, 3)), pltpu.SemaphoreType.DMA((n, 3))]

    def _copies(self, ins, outs, scratch):
        send_sems, recv_sems = scratch
        x, y, c, chips = _place()
        return [_remote(ins[t].at[k], outs[t].at[k], send_sems, recv_sems, (t, k), (*chip, c))
                for t in range(len(ins)) for k, chip in enumerate(chips)]

    def start(self, ins, outs, scratch):
        for cp in self._copies(ins, outs, scratch):
            cp.start()

    def finish(self, ins, outs, scratch):
        for cp in self._copies(ins, outs, scratch):
            cp.wait()


def _chip_scatter(sums, tensors, name):
    scatter = _Scatter(tensors)
    n = len(tensors)

    def body(*refs):
        ins, outs, scratch = refs[:n], refs[n:2 * n], refs[2 * n:]
        scatter.start(ins, outs, scratch)
        scatter.finish(ins, outs, scratch)

    return pl.pallas_call(
        body, name=name, in_specs=scatter.in_specs, out_specs=scatter.out_specs, out_shape=scatter.out_shape,
        scratch_shapes=scatter.scratch_shapes,
    )(*sums)


def _adamw(w, g, m, v):
    m = ADAM_B1 * m + (1.0 - ADAM_B1) * g
    v = ADAM_B2 * v + (1.0 - ADAM_B2) * (g * g)
    m_hat = m / (1.0 - ADAM_B1 ** ADAM_STEP)
    v_hat = v / (1.0 - ADAM_B2 ** ADAM_STEP)
    delta = -ADAM_LR * (m_hat / (jnp.sqrt(v_hat) + ADAM_EPS) + ADAM_WD * w)
    return delta, m, v


def _final_sum(T, g, lz1, lz2, where, w, m, v):
    rows, cols = T.block_shape
    sub = 4 if T.axis == 0 and rows % 64 == 0 and rows > 256 else 1
    blk = (rows // sub, cols)

    def body(where_ref, g_ref, l1_ref, l2_ref, w_ref, m_ref, v_ref, g_out, d_out, m_out, v_out):
        tot = g_ref[...].astype(F32) + l1_ref[...].astype(F32)
        for k in range(3):
            tot = tot + l2_ref[k].astype(F32)
        g_out[...] = tot
        d_out[...], m_out[...], v_out[...] = _adamw(w_ref[...], tot, m_ref[...], v_ref[...])

    def in_whole(r, wh):
        p = wh[0]
        return (0, p) if T.axis == 1 else (p * sub + r, 0)

    own = pl.BlockSpec(blk, lambda r, wh: (r, 0))
    return pl.pallas_call(
        body, name="grad_final_" + T.name,
        grid_spec=pltpu.PrefetchScalarGridSpec(
            num_scalar_prefetch=1, grid=(sub,),
            in_specs=[pl.BlockSpec(blk, in_whole),
                      pl.BlockSpec((None,) + blk, lambda r, wh: (wh[1], r, 0)),
                      pl.BlockSpec((3,) + blk, lambda r, wh: (0, r, 0)), own, own, own],
            out_specs=[own] * 4),
        out_shape=[jax.ShapeDtypeStruct(T.block_shape, F32)] * 4,
        compiler_params=_params("arbitrary"),
    )(where, g, lz1, lz2, w, m, v)


MAT_PIECE = MAT_ROWS // 8
VEC_PIECE = DR // 8


def _small_all_reduce(dmat, dvec):
    def body(mat_ref, vec_ref, mat_out, vec_out, mat_in, vec_in, mat_sum, vec_sum, send_sems, recv_sems, loc_sems):
        x, y, c, _ = _place()
        me = 4 * x + 2 * y + c

        def peer(r):
            return (1 - x if r & 4 else x, 1 - y if r & 2 else y, 1 - c if r & 1 else c)

        def index(p):
            return 4 * p[0] + 2 * p[1] + p[2]

        def mat_rows(ref, p):
            return ref.at[pl.ds(pl.multiple_of(p * MAT_PIECE, 8), MAT_PIECE), :]

        def vec_cols(ref, p):
            return ref.at[:, pl.ds(pl.multiple_of(p * VEC_PIECE, 128), VEC_PIECE)]

        own = [pltpu.make_async_copy(mat_rows(mat_ref, me), mat_in.at[0], loc_sems.at[0]),
               pltpu.make_async_copy(vec_cols(vec_ref, me), vec_in.at[0], loc_sems.at[1])]
        for cp in own:
            cp.start()
        scatter = []
        for r in range(1, 8):
            p = index(peer(r))
            scatter.append(_remote(mat_rows(mat_ref, p), mat_in.at[r], send_sems, recv_sems, (0, r), peer(r)))
            scatter.append(_remote(vec_cols(vec_ref, p), vec_in.at[r], send_sems, recv_sems, (1, r), peer(r)))
        for cp in scatter:
            cp.start()
        for cp in own:
            cp.wait()
        for cp in scatter:
            cp.wait_recv()
        mat_tot, vec_tot = mat_in[0], vec_in[0]
        for r in range(1, 8):
            mat_tot = mat_tot + mat_in[r]
            vec_tot = vec_tot + vec_in[r]
        mat_sum[...] = mat_tot
        vec_sum[...] = vec_tot
        keep = [pltpu.make_async_copy(mat_sum, mat_rows(mat_out, me), loc_sems.at[2]),
                pltpu.make_async_copy(vec_sum, vec_cols(vec_out, me), loc_sems.at[3])]
        for cp in keep:
            cp.start()
        spread = []
        for r in range(1, 8):
            spread.append(_remote(mat_sum, mat_rows(mat_out, me), send_sems, recv_sems, (2, r), peer(r)))
            spread.append(_remote(vec_sum, vec_cols(vec_out, me), send_sems, recv_sems, (3, r), peer(r)))
        for cp in spread:
            cp.start()
        for r in range(1, 8):
            p = index(peer(r))
            _remote(mat_sum, mat_rows(mat_out, p), send_sems, recv_sems, (2, r), peer(r)).wait_recv()
            _remote(vec_sum, vec_cols(vec_out, p), send_sems, recv_sems, (3, r), peer(r)).wait_recv()
        for cp in scatter + spread:
            cp.wait_send()
        for cp in keep:
            cp.wait()

    return pl.pallas_call(
        body, name="small_all_reduce",
        in_specs=[VMEM_SPEC, VMEM_SPEC], out_specs=[VMEM_SPEC, VMEM_SPEC],
        out_shape=[jax.ShapeDtypeStruct((MAT_ROWS, HD), F32), jax.ShapeDtypeStruct((VEC_ROWS, DR), F32)],
        scratch_shapes=[pltpu.VMEM((8, MAT_PIECE, HD), F32), pltpu.VMEM((8, VEC_ROWS, VEC_PIECE), F32),
                        pltpu.VMEM((MAT_PIECE, HD), F32), pltpu.VMEM((VEC_ROWS, VEC_PIECE), F32),
                        pltpu.SemaphoreType.DMA((4, 8)), pltpu.SemaphoreType.DMA((4, 8)), pltpu.SemaphoreType.DMA((4,))],
    )(dmat, dvec)


def _adam_small(grads, wmv):
    n = len(grads)

    def body(*refs):
        g_refs, rest = refs[:n], refs[n:]
        ins, outs = rest[:3 * n], rest[3 * n:]
        for i in range(n):
            d, m, v = _adamw(ins[3 * i][...], g_refs[i][...], ins[3 * i + 1][...], ins[3 * i + 2][...])
            outs[3 * i][...], outs[3 * i + 1][...], outs[3 * i + 2][...] = d, m, v

    flat = [a for t in wmv for a in t]
    return pl.pallas_call(
        body, name="adam_small",
        in_specs=[VMEM_SPEC] * (4 * n), out_specs=[VMEM_SPEC] * (3 * n),
        out_shape=[jax.ShapeDtypeStruct(a.shape, F32) for a in flat],
    )(*grads, *flat)


WEIGHT_NAMES = ("norm_mix", "w_in", "w_pool_grp", "pool_scale", "w_pool_out", "conv_w", "conv_b", "w_rg_a", "b_rg_a", "w_rg_x",
                "b_rg_x", "lru_lambda", "w_rnn_out", "w_o", "norm_ffn", "w_ffn_in", "w_ffn_out", "norm_final")


def kernel(x, norm_mix, w_in, w_pool_grp, pool_scale, w_pool_out, conv_w, conv_b, w_rg_a, b_rg_a, w_rg_x, b_rg_x, lru_lambda, w_rnn_out, w_o, norm_ffn, w_ffn_in, w_ffn_out, norm_final, loss_target, m_norm_mix, m_w_in, m_w_pool_grp, m_pool_scale, m_w_pool_out, m_conv_w, m_conv_b, m_w_rg_a, m_b_rg_a, m_w_rg_x, m_b_rg_x, m_lru_lambda, m_w_rnn_out, m_w_o, m_norm_ffn, m_w_ffn_in, m_w_ffn_out, m_norm_final, v_norm_mix, v_w_in, v_w_pool_grp, v_pool_scale, v_w_pool_out, v_conv_w, v_conv_b, v_w_rg_a, v_b_rg_a, v_w_rg_x, v_b_rg_x, v_lru_lambda, v_w_rnn_out, v_w_o, v_norm_ffn, v_w_ffn_in, v_w_ffn_out, v_norm_final):
    w = dict(norm_mix=norm_mix, w_in=w_in, w_pool_grp=w_pool_grp, pool_scale=pool_scale, w_pool_out=w_pool_out, conv_w=conv_w,
             conv_b=conv_b, w_rg_a=w_rg_a, b_rg_a=b_rg_a, w_rg_x=w_rg_x, b_rg_x=b_rg_x, lru_lambda=lru_lambda,
             w_rnn_out=w_rnn_out, w_o=w_o, norm_ffn=norm_ffn, w_ffn_in=w_ffn_in, w_ffn_out=w_ffn_out, norm_final=norm_final)
    m = dict(norm_mix=m_norm_mix, w_in=m_w_in, w_pool_grp=m_w_pool_grp, pool_scale=m_pool_scale, w_pool_out=m_w_pool_out,
             conv_w=m_conv_w, conv_b=m_conv_b, w_rg_a=m_w_rg_a, b_rg_a=m_b_rg_a, w_rg_x=m_w_rg_x, b_rg_x=m_b_rg_x,
             lru_lambda=m_lru_lambda, w_rnn_out=m_w_rnn_out, w_o=m_w_o, norm_ffn=m_norm_ffn, w_ffn_in=m_w_ffn_in,
             w_ffn_out=m_w_ffn_out, norm_final=m_norm_final)
    v = dict(norm_mix=v_norm_mix, w_in=v_w_in, w_pool_grp=v_w_pool_grp, pool_scale=v_pool_scale, w_pool_out=v_w_pool_out,
             conv_w=v_conv_w, conv_b=v_conv_b, w_rg_a=v_w_rg_a, b_rg_a=v_b_rg_a, w_rg_x=v_w_rg_x, b_rg_x=v_b_rg_x,
             lru_lambda=v_lru_lambda, w_rnn_out=v_w_rnn_out, w_o=v_w_o, norm_ffn=v_norm_ffn, w_ffn_in=v_w_ffn_in,
             w_ffn_out=v_w_ffn_out, norm_final=v_norm_final)
    xi, yi, ci = (lax.axis_index(a) for a in MESH_AXES)
    chip = 2 * xi + yi
    other = [2 * (1 - xi) + yi, 2 * xi + (1 - yi), 2 * (1 - xi) + (1 - yi)]

    def held(T, a):
        return jnp.swapaxes(a, 0, 1) if T.transposed else a

    where4 = jnp.stack(other + [ci]).astype(jnp.int32)
    where2 = jnp.stack([2 * chip + ci, chip]).astype(jnp.int32)
    by_name = {T.name: T for T in GATHERED}
    block = {T.name: held(T, w[T.name][0]).astype(T.dtype) for T in BIG}
    block["conv_w"] = jnp.pad(conv_w[0], ((0, CONV_W.rows - 4), (0, 0)))

    def gather_of(*names):
        return dict(exchange=_Gather([by_name[n] for n in names]), exchange_operands=[block[n] for n in names])

    def pair_sums(names, partials, tag):
        ts = [by_name[n] for n in names]
        landed = _pair_exchange(partials, ts, "grad_pair_exchange_" + tag)
        return landed, [_pair_sum(T, g, l, where4) for T, g, l in zip(ts, partials, landed)]

    xs, target = x[0], loss_target[0]
    wg_b, wa_b, wx_b = (a[0].astype(BF16) for a in (w_pool_grp, w_rg_a, w_rg_x))
    ba2, bx2 = b_rg_a.reshape(1, DR), b_rg_x.reshape(1, DR)
    (w_in_g,) = _all_gather([block["w_in"]], [by_name["w_in"]], "all_gather_w_in")
    (proj, h1), (w_pool_out_g, w_rnn_out_g, w_o_g, conv_g) = _in_proj(
        xs, norm_mix, w_in_g, **gather_of("w_pool_out", "w_rnn_out", "w_o", "conv_w"))
    mixer_weights = (wg_b, pool_scale, w_pool_out_g, conv_g[0:4], conv_b, wa_b, ba2, wx_b, bx2, lru_lambda, w_rnn_out_g)
    (_, pm, y_pool, hr, z, y_rnn), (w_ffn_in_g,) = _mixer_fwd(proj, *mixer_weights, **gather_of("w_ffn_in"))
    (mix, x2, h2), (w_ffn_out_g,) = _merge_out(xs, proj, y_pool, y_rnn, w_o_g, norm_ffn, **gather_of("w_ffn_out"))
    gu, act = _ffn_up(h2, w_ffn_in_g)
    dx3, dx3b, loss_part, dvec_fin = _ffn_down_loss(act, x2, target, w_ffn_out_g, norm_final.reshape(1, D))
    loss = lax.psum(loss_part[0, 0], MESH_AXES)

    dgu = _ffn_bwd_down(dx3b, gu, w_ffn_out_g)
    dx2, dx2b, dmixo, dvec_ffn = _ffn_bwd_up(dgu, x2, dx3, w_ffn_in_g, norm_ffn, w_o_g)
    names_a = ("w_ffn_in", "w_ffn_out")
    part_a = [_wgrad(dgu, h2, "wgrad_ffn_in", 1408, 512), _wgrad(act, dx3b, "wgrad_ffn_out", 1408, 512)]
    lz1_a, sums_a = pair_sums(names_a, part_a, "ffn")
    (dproj, dypb, dyrb, dmat, dvec_mix), lz2_a = _mixer_bwd(
        proj, dmixo, y_pool, y_rnn, hr, *mixer_weights,
        exchange=_Scatter([by_name[n] for n in names_a]), exchange_operands=sums_a)
    names_b = ("w_pool_out", "w_rnn_out", "w_o")
    part_b = [_wgrad(pm, dypb, "wgrad_pool_out", 512, 1024), _wgrad(z, dyrb, "wgrad_rnn_out", 1024, 1024),
              _wgrad(mix, dx2b, "wgrad_o", 1024, 1024)]
    lz1_b, sums_b = pair_sums(names_b, part_b, "mix")
    grad_x, dvec_in = _in_bwd(dproj, xs, dx2, norm_mix, w_in_g)
    g_in, lz2_b = _wgrad(dproj, h1, "wgrad_in", 1152, 1024,
                         exchange=_Scatter([by_name[n] for n in names_b]), exchange_operands=sums_b)
    lz1_c, sums_c = pair_sums(("w_in",), [g_in], "in")
    lz2_c = _chip_scatter(sums_c, [by_name["w_in"]], "grad_chip_scatter_w_in")

    grads, delta, new_m, new_v = {}, {}, {}, {}
    for n, g, l1, l2 in zip(names_a + names_b + ("w_in",), part_a + part_b + [g_in], lz1_a + lz1_b + lz1_c,
                            lz2_a + lz2_b + lz2_c):
        T = by_name[n]
        out = _final_sum(T, g, l1, l2, where2, held(T, w[n][0]), held(T, m[n][0]), held(T, v[n][0]))
        grads[n], delta[n], new_m[n], new_v[n] = (held(T, a) for a in out)
    dvec = jnp.concatenate([dvec_mix[0:9], dvec_fin[0:1], dvec_ffn[0:1], dvec_in[0:1], jnp.zeros((VEC_ROWS - 12, DR), F32)],
                           axis=0)

    mat, vec = _small_all_reduce(dmat, dvec)
    me = 4 * xi + 2 * yi + ci
    small_grads = dict(
        w_pool_grp=mat[0:MAT_WA], w_rg_a=mat[MAT_WA:MAT_WX], w_rg_x=mat[MAT_WX:MAT_ROWS],
        pool_scale=vec[VEC_SCALE:VEC_SCALE + 1, 0:DP], conv_b=vec[VEC_CONV_B:VEC_CONV_B + 1],
        b_rg_a=vec[VEC_BA:VEC_BA + 1], b_rg_x=vec[VEC_BX:VEC_BX + 1], lru_lambda=vec[VEC_LAM:VEC_LAM + 1],
        conv_w=lax.dynamic_slice(vec, (VEC_CONV_W, VEC_PIECE * me), (4, VEC_PIECE)),
        norm_final=vec[VEC_NORM_FINAL:VEC_NORM_FINAL + 1], norm_ffn=vec[VEC_NORM_FFN:VEC_NORM_FFN + 1],
        norm_mix=vec[VEC_NORM_MIX:VEC_NORM_MIX + 1])
    names = list(small_grads)
    as2d = lambda a, g: a.reshape(g.shape)
    upd = _adam_small([small_grads[n] for n in names],
                      [(as2d(w[n], small_grads[n]), as2d(m[n], small_grads[n]), as2d(v[n], small_grads[n])) for n in names])
    for i, n in enumerate(names):
        grads[n] = small_grads[n]
        delta[n], new_m[n], new_v[n] = upd[3 * i:3 * i + 3]

    shaped = lambda d: [d[n].reshape(w[n].shape) for n in WEIGHT_NAMES]
    return (loss, grad_x[None], *shaped(grads), *shaped(delta), *shaped(new_m), *shaped(new_v))
```

```python
import functools
import math

import jax
import jax.numpy as jnp
from jax import lax
from jax.experimental import pallas as pl
from jax.experimental.pallas import tpu as pltpu

F32 = jnp.float32
BF16 = jnp.bfloat16

D = 1024
DP = 512
PG = 128
WINDOWS = (2, 4, 8, 16)
DR = 1024
NH = 8
HD = 128
DIN = 4608
DFF = 2816
EPS = 1e-6
LRU_C = 8.0
POOL_HALO = 16
CONV_HALO = 8

ADAM_LR = 0.001
ADAM_B1 = 0.9
ADAM_B2 = 0.999
ADAM_EPS = 1e-08
ADAM_WD = 0.01
ADAM_STEP = 10

VMEM_LIMIT = 56 * 1024 * 1024
MESH_AXES = ("x", "y", "c")
MESH = pl.DeviceIdType.MESH


def _dot(a, b):
    return jnp.dot(a, b, preferred_element_type=F32)


def _dot_nt(a, b):
    return lax.dot_general(a, b, (((1,), (1,)), ((), ())), preferred_element_type=F32)


def _dot_tn(a, b):
    return lax.dot_general(a, b, (((0,), (0,)), ((), ())), preferred_element_type=F32)


def _params(*sem):
    return pltpu.CompilerParams(dimension_semantics=sem, vmem_limit_bytes=VMEM_LIMIT)


def _resident(shape):
    nd = len(shape)
    return pl.BlockSpec(shape, lambda i: (0,) * nd, pipeline_mode=pl.Buffered(1))


def _rows(shape_cols, tm):
    return pl.BlockSpec((tm, shape_cols), lambda i: (i, 0))


def _call(body, name, grid, in_specs, out_specs, out_shape, operands, scratch_shapes=(), exchange=None, exchange_operands=()):
    n_in, n_out, n_scr = len(in_specs), len(out_specs), len(scratch_shapes)
    steps = math.prod(grid)
    if exchange is None:
        outs = pl.pallas_call(body, name=name, grid=grid, in_specs=in_specs, out_specs=out_specs, out_shape=out_shape,
                              scratch_shapes=list(scratch_shapes), compiler_params=_params(*["arbitrary"] * len(grid)))(*operands)
        return outs, []
    e_in, e_out = len(exchange.in_specs), len(exchange.out_specs)

    def hosted(*refs):
        ins, refs = refs[:n_in], refs[n_in:]
        e_ins, refs = refs[:e_in], refs[e_in:]
        outs, refs = refs[:n_out], refs[n_out:]
        e_outs, refs = refs[:e_out], refs[e_out:]
        scr, e_scr = refs[:n_scr], refs[n_scr:]
        step = pl.program_id(0)
        for axis in range(1, len(grid)):
            step = step * grid[axis] + pl.program_id(axis)
        pl.when(step == 0)(lambda: exchange.start(e_ins, e_outs, e_scr))
        if exchange.middle is not None:
            pl.when(step == exchange.middle_at(steps))(lambda: exchange.middle(e_ins, e_outs, e_scr))
        body(*ins, *outs, *scr)
        pl.when(step == steps - 1)(lambda: exchange.finish(e_ins, e_outs, e_scr))

    outs = pl.pallas_call(
        hosted, name=name, grid=grid, in_specs=list(in_specs) + exchange.in_specs,
        out_specs=list(out_specs) + exchange.out_specs, out_shape=list(out_shape) + exchange.out_shape,
        scratch_shapes=list(scratch_shapes) + exchange.scratch_shapes,
        compiler_params=_params(*["arbitrary"] * len(grid)))(*operands, *exchange_operands)
    return outs[:n_out], outs[n_out:]


def _gelu(x):
    c = math.sqrt(2.0 / math.pi)
    t = jnp.tanh(c * (x + 0.044715 * x * x * x))
    return 0.5 * x * (1.0 + t), t


def _gelu_grad(x, t):
    c = math.sqrt(2.0 / math.pi)
    return 0.5 * (1.0 + t) + 0.5 * x * (1.0 - t * t) * (c * (1.0 + 3.0 * 0.044715 * x * x))


def _softplus_neg(lam):
    z = jnp.exp(-jnp.abs(lam))
    u = 1.0 + z
    dlt = u - 1.0
    log1p = jnp.where(dlt == 0.0, z, jnp.log(u) * (z / jnp.where(dlt == 0.0, 1.0, dlt)))
    return jnp.maximum(-lam, 0.0) + log1p


def _sigmoid(x):
    return 0.5 * jnp.tanh(0.5 * x) + 0.5


def _linear_scan(A, B, reverse):
    n, cols = A.shape
    rows = lax.broadcasted_iota(jnp.int32, (n, 1), 0)
    d = 1
    while d < n:
        if d < 8:
            keep = (rows < n - d) if reverse else (rows >= d)
            shift = n - d if reverse else d
            B = jnp.where(keep, A * pltpu.roll(B, shift, axis=0) + B, B)
            A = jnp.where(keep, A * pltpu.roll(A, shift, axis=0), A)
        else:
            def shifted(v, fill):
                pad = jnp.full((d, cols), fill, v.dtype)
                return jnp.concatenate([v[d:], pad] if reverse else [pad, v[:n - d]], axis=0)
            B = A * shifted(B, 0.0) + B
            A = A * shifted(A, 1.0)
        d *= 2
    return A, B


def _pool_windows(ext, shift_sign):
    n = ext.shape[0]
    s = ext
    outs = []
    for w in WINDOWS:
        d = w // 2
        s = s + pltpu.roll(s, d if shift_sign > 0 else n - d, axis=0)
        outs.append(s[:, :PG])
        s = s[:, PG:]
    return outs


def _conv_taps(uext):
    taps = []
    for k in range(4):
        sh = 3 - k
        v = uext if sh == 0 else pltpu.roll(uext, sh, axis=0)
        taps.append(v[CONV_HALO:, :])
    return taps


def _gates(v, wa_ref, ba_ref, wx_ref, bx_ref, sp):
    vb = v.astype(BF16)
    ra, rx = [], []
    for h in range(NH):
        vh = vb[:, h * HD:(h + 1) * HD]
        ra.append(_dot(vh, wa_ref[h]))
        rx.append(_dot(vh, wx_ref[h]))
    r = _sigmoid(jnp.concatenate(ra, axis=1) + ba_ref[...])
    i = _sigmoid(jnp.concatenate(rx, axis=1) + bx_ref[...])
    log_a = (-LRU_C) * r * sp
    a = jnp.exp(log_a)
    mult = jnp.sqrt(-jnp.tanh(log_a) * (1.0 + a * a))
    return r, i, a, mult


def _in_proj(x, norm_mix, w_in, exchange=None, exchange_operands=(), tm=512):
    S = x.shape[0]

    def body(x_ref, g_ref, w_ref, proj_ref, h_ref):
        xv = x_ref[...]
        r = lax.rsqrt(jnp.mean(xv * xv, axis=-1, keepdims=True) + EPS)
        h = (xv * r * g_ref[...]).astype(BF16)
        h_ref[...] = h
        for n0 in range(0, DIN, 512):
            proj_ref[:, n0:n0 + 512] = _dot_nt(h, w_ref[n0:n0 + 512, :])

    return _call(
        body, "in_proj", (S // tm,),
        in_specs=[_rows(D, tm), _resident((1, D)), _resident((DIN, D))],
        out_specs=[_rows(DIN, tm), _rows(D, tm)],
        out_shape=[jax.ShapeDtypeStruct((S, DIN), F32), jax.ShapeDtypeStruct((S, D), BF16)],
        operands=(x, norm_mix, w_in), exchange=exchange, exchange_operands=exchange_operands)


def _mixer_fwd(proj, wg, scale, w_pool_out, conv_w, conv_b, wa, ba, wx, bx, lam, w_rnn_out, exchange=None,
               exchange_operands=(), tm=256):
    S = proj.shape[0]
    UW = DP + 2 * DR

    def body(proj_ref, wg_ref, scale_ref, wpo_ref, cw_ref, cb_ref, wa_ref, ba_ref, wx_ref, bx_ref, lam_ref, wro_ref,
             pooled_ref, pm_ref, ypool_ref, hr_ref, z_ref, yrnn_ref, pool_carry, conv_carry, h_carry):
        i = pl.program_id(0)

        @pl.when(i == 0)
        def _():
            pool_carry[...] = jnp.zeros_like(pool_carry)
            conv_carry[...] = jnp.zeros_like(conv_carry)
            h_carry[...] = jnp.zeros_like(h_carry)

        rows = lax.broadcasted_iota(jnp.int32, (tm, 1), 0)
        t_glob = i * tm + rows

        u_pool = proj_ref[:, 0:DP]
        ext = jnp.concatenate([pool_carry[...], u_pool], axis=0)
        pool_carry[...] = u_pool[tm - POOL_HALO:, :]
        sums = _pool_windows(ext, +1)
        mixed = []
        for g, w in enumerate(WINDOWS):
            cnt = jnp.minimum(t_glob + 1, w).astype(F32)
            pooled_g = sums[g][POOL_HALO:, :] / cnt - u_pool[:, g * PG:(g + 1) * PG]
            pooled_b = pooled_g.astype(BF16)
            pooled_ref[:, g * PG:(g + 1) * PG] = pooled_b
            mixed.append(_dot(pooled_b, wg_ref[g]))
        pm = (jnp.concatenate(mixed, axis=1) * scale_ref[...]).astype(BF16)
        pm_ref[...] = pm
        ypool_ref[...] = _dot(pm, wpo_ref[...])

        u_rnn = proj_ref[:, DP:DP + DR]
        uext = jnp.concatenate([conv_carry[...], u_rnn], axis=0)
        conv_carry[...] = u_rnn[tm - CONV_HALO:, :]
        taps = _conv_taps(uext)
        v = cb_ref[...]
        for k in range(4):
            v = v + taps[k] * cw_ref[k:k + 1, :]
        sp = _softplus_neg(lam_ref[...])
        _, gi, a, mult = _gates(v, wa_ref, ba_ref, wx_ref, bx_ref, sp)
        b = mult * gi * v
        A, B = _linear_scan(a, b, reverse=False)
        hr = A * h_carry[0:1, :] + B
        h_carry[0:1, :] = hr[tm - 1:tm, :]
        hr_ref[...] = hr
        gg, _ = _gelu(proj_ref[:, DP + DR:UW])
        z = (hr * gg).astype(BF16)
        z_ref[...] = z
        yrnn_ref[...] = _dot(z, wro_ref[...])

    return _call(
        body, "mixer_fwd", (S // tm,),
        in_specs=[_rows(UW, tm), _resident((4, PG, PG)), _resident((1, DP)), _resident((DP, D)), _resident((4, DR)),
                  _resident((1, DR)), _resident((NH, HD, HD)), _resident((1, DR)), _resident((NH, HD, HD)),
                  _resident((1, DR)), _resident((1, DR)), _resident((DR, D))],
        out_specs=[_rows(DP, tm), _rows(DP, tm), _rows(D, tm), _rows(DR, tm), _rows(DR, tm), _rows(D, tm)],
        out_shape=[jax.ShapeDtypeStruct((S, DP), BF16), jax.ShapeDtypeStruct((S, DP), BF16),
                   jax.ShapeDtypeStruct((S, D), F32), jax.ShapeDtypeStruct((S, DR), F32),
                   jax.ShapeDtypeStruct((S, DR), BF16), jax.ShapeDtypeStruct((S, D), F32)],
        scratch_shapes=[pltpu.VMEM((POOL_HALO, DP), F32), pltpu.VMEM((CONV_HALO, DR), F32), pltpu.VMEM((8, DR), F32)],
        operands=(proj, wg, scale, w_pool_out, conv_w, conv_b, wa, ba, wx, bx, lam, w_rnn_out),
        exchange=exchange, exchange_operands=exchange_operands)


FF_CHUNK = DFF // 4


def _rms(x):
    r = lax.rsqrt(jnp.mean(x * x, axis=-1, keepdims=True) + EPS)
    return r, x * r


def _rms_bwd(dh, g, r, xh):
    dxh = dh * g
    return r * (dxh - xh * jnp.mean(dxh * xh, axis=-1, keepdims=True))


def _merge_out(x, proj, y_pool, y_rnn, w_o, norm_ffn, exchange=None, exchange_operands=(), tm=512):
    S = x.shape[0]
    GL0 = (DP + 2 * DR) // 512

    def gl_spec(k):
        return pl.BlockSpec((tm, 512), lambda i: (i, GL0 + k))

    def body(x_ref, gl0, gl1, gl2, gl3, yp_ref, yr_ref, wo_ref, gf_ref, mix_ref, x2_ref, h2_ref):
        s_p = _sigmoid(jnp.concatenate([gl0[...], gl1[...]], axis=1))
        s_r = _sigmoid(jnp.concatenate([gl2[...], gl3[...]], axis=1))
        mix = (s_p * yp_ref[...] + s_r * yr_ref[...]).astype(BF16)
        mix_ref[...] = mix
        x2 = x_ref[...] + _dot(mix, wo_ref[...])
        x2_ref[...] = x2
        _, xh2 = _rms(x2)
        h2_ref[...] = (xh2 * gf_ref[...]).astype(BF16)

    return _call(
        body, "merge_out", (S // tm,),
        in_specs=[_rows(D, tm), gl_spec(0), gl_spec(1), gl_spec(2), gl_spec(3), _rows(D, tm), _rows(D, tm),
                  _resident((D, D)), _resident((1, D))],
        out_specs=[_rows(D, tm), _rows(D, tm), _rows(D, tm)],
        out_shape=[jax.ShapeDtypeStruct((S, D), BF16), jax.ShapeDtypeStruct((S, D), F32), jax.ShapeDtypeStruct((S, D), BF16)],
        operands=(x, proj, proj, proj, proj, y_pool, y_rnn, w_o, norm_ffn),
        exchange=exchange, exchange_operands=exchange_operands)


def _ffn_up(h2, w_ffn_in, exchange=None, exchange_operands=(), tm=512):
    S = h2.shape[0]

    def body(h_ref, w_ref, gu_ref, act_ref):
        h = h_ref[...]
        for c0 in range(0, DFF, FF_CHUNK):
            gate = _dot_nt(h, w_ref[c0:c0 + FF_CHUNK, :])
            up = _dot_nt(h, w_ref[DFF + c0:DFF + c0 + FF_CHUNK, :])
            gu_ref[:, c0:c0 + FF_CHUNK] = gate.astype(BF16)
            gu_ref[:, DFF + c0:DFF + c0 + FF_CHUNK] = up.astype(BF16)
            act_ref[:, c0:c0 + FF_CHUNK] = (gate * _sigmoid(gate) * up).astype(BF16)

    return _call(
        body, "ffn_up", (S // tm,),
        in_specs=[_rows(D, tm), _resident((2 * DFF, D))],
        out_specs=[_rows(2 * DFF, tm), _rows(DFF, tm)],
        out_shape=[jax.ShapeDtypeStruct((S, 2 * DFF), BF16), jax.ShapeDtypeStruct((S, DFF), BF16)],
        operands=(h2, w_ffn_in), exchange=exchange, exchange_operands=exchange_operands)


def _ffn_down_loss(act, x2, target, w_ffn_out, norm_final, tm=512):
    S = act.shape[0]

    def body(act_ref, x2_ref, t_ref, w_ref, gn_ref, dx3_ref, dx3b_ref, loss_ref, dvec_ref):
        i = pl.program_id(0)

        @pl.when(i == 0)
        def _():
            loss_ref[...] = jnp.zeros_like(loss_ref)
            dvec_ref[...] = jnp.zeros_like(dvec_ref)

        x3 = x2_ref[...] + _dot(act_ref[...], w_ref[...])
        r3, xh3 = _rms(x3)
        g_fin = gn_ref[...]
        e = xh3 * g_fin - t_ref[...]
        loss_ref[...] += jnp.sum(e * e, axis=(0, 1), keepdims=True) * (0.5 / D)
        dy = e * (1.0 / D)
        dvec_ref[0:1, :] += jnp.sum(dy * xh3, axis=0, keepdims=True)
        dx3 = _rms_bwd(dy, g_fin, r3, xh3)
        dx3_ref[...] = dx3
        dx3b_ref[...] = dx3.astype(BF16)

    return pl.pallas_call(
        body, name="ffn_down_loss", grid=(S // tm,),
        in_specs=[_rows(DFF, tm), _rows(D, tm), _rows(D, tm), _resident((DFF, D)), _resident((1, D))],
        out_specs=[_rows(D, tm), _rows(D, tm), _resident((1, 1)), _resident((8, D))],
        out_shape=[jax.ShapeDtypeStruct((S, D), F32), jax.ShapeDtypeStruct((S, D), BF16),
                   jax.ShapeDtypeStruct((1, 1), F32), jax.ShapeDtypeStruct((8, D), F32)],
        compiler_params=_params("arbitrary"),
    )(act, x2, target, w_ffn_out, norm_final)


def _ffn_bwd_down(dx3b, gu, w_ffn_out, tm=512):
    S = dx3b.shape[0]

    def body(d_ref, gu_ref, w_ref, dgu_ref):
        d = d_ref[...]
        for c0 in range(0, DFF, FF_CHUNK):
            dact = _dot_nt(d, w_ref[c0:c0 + FF_CHUNK, :])
            gate = gu_ref[:, c0:c0 + FF_CHUNK].astype(F32)
            up = gu_ref[:, DFF + c0:DFF + c0 + FF_CHUNK].astype(F32)
            sg = _sigmoid(gate)
            dgu_ref[:, c0:c0 + FF_CHUNK] = (dact * up * (sg * (1.0 + gate * (1.0 - sg)))).astype(BF16)
            dgu_ref[:, DFF + c0:DFF + c0 + FF_CHUNK] = (dact * (gate * sg)).astype(BF16)

    return pl.pallas_call(
        body, name="ffn_bwd_down", grid=(S // tm,),
        in_specs=[_rows(D, tm), _rows(2 * DFF, tm), _resident((DFF, D))],
        out_specs=_rows(2 * DFF, tm),
        out_shape=jax.ShapeDtypeStruct((S, 2 * DFF), BF16),
        compiler_params=_params("parallel"),
    )(dx3b, gu, w_ffn_out)


def _ffn_bwd_up(dgu, x2, dx3, w_ffn_in, norm_ffn, w_o, tm=512):
    S = dgu.shape[0]

    def body(dgu_ref, x2_ref, dx3_ref, wfi_ref, gf_ref, wo_ref, dx2_ref, dx2b_ref, dmixo_ref, dvec_ref):
        i = pl.program_id(0)

        @pl.when(i == 0)
        def _():
            dvec_ref[...] = jnp.zeros_like(dvec_ref)

        dh2 = _dot(dgu_ref[:, 0:DFF], wfi_ref[0:DFF, :]) + _dot(dgu_ref[:, DFF:2 * DFF], wfi_ref[DFF:2 * DFF, :])
        r2, xh2 = _rms(x2_ref[...])
        dvec_ref[0:1, :] += jnp.sum(dh2 * xh2, axis=0, keepdims=True)
        dx2 = dx3_ref[...] + _rms_bwd(dh2, gf_ref[...], r2, xh2)
        dx2_ref[...] = dx2
        dx2b = dx2.astype(BF16)
        dx2b_ref[...] = dx2b
        dmixo_ref[...] = _dot_nt(dx2b, wo_ref[...])

    return pl.pallas_call(
        body, name="ffn_bwd_up", grid=(S // tm,),
        in_specs=[_rows(2 * DFF, tm), _rows(D, tm), _rows(D, tm), _resident((2 * DFF, D)), _resident((1, D)),
                  _resident((D, D))],
        out_specs=[_rows(D, tm), _rows(D, tm), _rows(D, tm), _resident((8, D))],
        out_shape=[jax.ShapeDtypeStruct((S, D), F32), jax.ShapeDtypeStruct((S, D), BF16), jax.ShapeDtypeStruct((S, D), F32),
                   jax.ShapeDtypeStruct((8, D), F32)],
        compiler_params=_params("arbitrary"),
    )(dgu, x2, dx3, w_ffn_in, norm_ffn, w_o)


VEC_ROWS = 16
MAT_WA = 4 * PG
MAT_WX = MAT_WA + NH * HD
MAT_ROWS = MAT_WX + NH * HD


def _mixer_bwd(proj, dmixo, y_pool, y_rnn, hr, wg, scale, w_pool_out, conv_w, conv_b, wa, ba, wx, bx, lam, w_rnn_out,
               exchange=None, exchange_operands=(), tm=256):
    S = proj.shape[0]
    nt = S // tm

    def rev(cols):
        return pl.BlockSpec((tm, cols), lambda i: (nt - 1 - i, 0))

    def halo(rows_, cols):
        per = tm // rows_
        return pl.BlockSpec((rows_, cols), lambda i: (jnp.maximum((nt - 1 - i) * per - 1, 0), 0))

    def body(proj_ref, projh_ref, dmixo_ref, yp_ref, yr_ref, hr_ref, hrh_ref, wg_ref, scale_ref, wpo_ref, cw_ref, cb_ref,
             wa_ref, ba_ref, wx_ref, bx_ref, lam_ref, wro_ref,
             dproj_ref, dypb_ref, dyrb_ref, dmat_ref, dvec_ref,
             q_carry, dv_carry, a_carry, g_carry):
        i = pl.program_id(0)
        ti = nt - 1 - i

        @pl.when(i == 0)
        def _():
            q_carry[...] = jnp.zeros_like(q_carry)
            dv_carry[...] = jnp.zeros_like(dv_carry)
            a_carry[...] = jnp.zeros_like(a_carry)
            g_carry[...] = jnp.zeros_like(g_carry)
            dmat_ref[...] = jnp.zeros_like(dmat_ref)
            dvec_ref[...] = jnp.zeros_like(dvec_ref)

        rows = lax.broadcasted_iota(jnp.int32, (tm, 1), 0)
        t_glob = ti * tm + rows
        has_prev = (ti > 0).astype(F32)
        dmixo = dmixo_ref[...]

        s_p = _sigmoid(proj_ref[:, DP + 2 * DR:DP + 2 * DR + D])
        s_r = _sigmoid(proj_ref[:, DP + 2 * DR + D:DIN])
        dproj_ref[:, DP + 2 * DR:DP + 2 * DR + D] = (dmixo * yp_ref[...] * s_p * (1.0 - s_p)).astype(BF16)
        dproj_ref[:, DP + 2 * DR + D:DIN] = (dmixo * yr_ref[...] * s_r * (1.0 - s_r)).astype(BF16)
        dyp = (dmixo * s_p).astype(BF16)
        dyr = (dmixo * s_r).astype(BF16)
        dypb_ref[...] = dyp
        dyrb_ref[...] = dyr

        dz = _dot_nt(dyr, wro_ref[...])
        u_gate = proj_ref[:, DP + DR:DP + 2 * DR]
        gg, tg = _gelu(u_gate)
        hr_t = hr_ref[...]
        dproj_ref[:, DP + DR:DP + 2 * DR] = (dz * hr_t * _gelu_grad(u_gate, tg)).astype(BF16)
        dhr = dz * gg

        u_rnn = proj_ref[:, DP:DP + DR]
        uext = jnp.concatenate([projh_ref[POOL_HALO - CONV_HALO:, DP:DP + DR] * has_prev, u_rnn], axis=0)
        taps = _conv_taps(uext)
        v = cb_ref[...]
        for k in range(4):
            v = v + taps[k] * cw_ref[k:k + 1, :]
        sp = _softplus_neg(lam_ref[...])
        r, gi, a, mult = _gates(v, wa_ref, ba_ref, wx_ref, bx_ref, sp)

        C = jnp.where(rows == tm - 1, a_carry[0:1, :], pltpu.roll(a, tm - 1, axis=0))
        C, G = _linear_scan(C, dhr, reverse=True)
        g = G + C * g_carry[0:1, :]
        a_carry[0:1, :] = a[0:1, :]
        g_carry[0:1, :] = g[0:1, :]

        h_prev = jnp.where(rows == 0, hrh_ref[7:8, :] * has_prev, pltpu.roll(hr_t, 1, axis=0))
        da = g * h_prev
        dmult = g * gi * v
        di = g * mult * v
        dv = g * mult * gi
        dlog_a = da * a - dmult * (a * a / mult)
        dvec_ref[4:5, :] += jnp.sum(dlog_a * r, axis=0, keepdims=True)
        dra = (dlog_a * ((-LRU_C) * sp) * r * (1.0 - r))
        drx = di * gi * (1.0 - gi)
        dvec_ref[2:3, :] += jnp.sum(dra, axis=0, keepdims=True)
        dvec_ref[3:4, :] += jnp.sum(drx, axis=0, keepdims=True)
        drab = dra.astype(BF16)
        drxb = drx.astype(BF16)
        vb = v.astype(BF16)
        dvg = []
        for h in range(NH):
            sl = slice(h * HD, (h + 1) * HD)
            dvg.append(_dot_nt(drab[:, sl], wa_ref[h]) + _dot_nt(drxb[:, sl], wx_ref[h]))
            dmat_ref[MAT_WA + h * HD:MAT_WA + (h + 1) * HD, :] += _dot_tn(vb[:, sl], drab[:, sl])
            dmat_ref[MAT_WX + h * HD:MAT_WX + (h + 1) * HD, :] += _dot_tn(vb[:, sl], drxb[:, sl])
        dv = dv + jnp.concatenate(dvg, axis=1)
        dvec_ref[1:2, :] += jnp.sum(dv, axis=0, keepdims=True)
        for k in range(4):
            dvec_ref[5 + k:6 + k, :] += jnp.sum(dv * taps[k], axis=0, keepdims=True)
        dvext = jnp.concatenate([dv, dv_carry[...]], axis=0)
        dv_carry[...] = dv[0:CONV_HALO, :]
        n = tm + CONV_HALO
        du_rnn = dv * cw_ref[3:4, :]
        for k in range(3):
            du_rnn = du_rnn + pltpu.roll(dvext, n - (3 - k), axis=0)[0:tm, :] * cw_ref[k:k + 1, :]
        dproj_ref[:, DP:DP + DR] = du_rnn.astype(BF16)

        dpm = _dot_nt(dyp, wpo_ref[...])
        u_pool = proj_ref[:, 0:DP]
        ext = jnp.concatenate([projh_ref[:, 0:DP] * has_prev, u_pool], axis=0)
        sums = _pool_windows(ext, +1)
        scale_v = scale_ref[...]
        qs = []
        dpooled = []
        dscale = []
        for gi_, w in enumerate(WINDOWS):
            sl = slice(gi_ * PG, (gi_ + 1) * PG)
            cnt = jnp.minimum(t_glob + 1, w).astype(F32)
            pooled_b = (sums[gi_][POOL_HALO:, :] / cnt - u_pool[:, sl]).astype(BF16)
            mixed_g = _dot(pooled_b, wg_ref[gi_])
            dscale.append(jnp.sum(dpm[:, sl] * mixed_g, axis=0, keepdims=True))
            dmixed_b = (dpm[:, sl] * scale_v[:, sl]).astype(BF16)
            dmat_ref[gi_ * PG:(gi_ + 1) * PG, :] += _dot_tn(pooled_b, dmixed_b)
            dp_g = _dot_nt(dmixed_b, wg_ref[gi_])
            dpooled.append(dp_g)
            qs.append(dp_g / cnt)
        dvec_ref[0:1, 0:DP] += jnp.concatenate(dscale, axis=1)
        q = jnp.concatenate(qs, axis=1)
        qext = jnp.concatenate([q, q_carry[...]], axis=0)
        q_carry[...] = q[0:POOL_HALO, :]
        tsum = _pool_windows(qext, -1)
        for gi_ in range(4):
            dproj_ref[:, gi_ * PG:(gi_ + 1) * PG] = (tsum[gi_][0:tm, :] - dpooled[gi_]).astype(BF16)

        @pl.when(i == nt - 1)
        def _():
            dvec_ref[4:5, :] = dvec_ref[4:5, :] * (LRU_C * _sigmoid(-lam_ref[...]))

    return _call(
        body, "mixer_bwd", (nt,),
        in_specs=[rev(DIN), halo(POOL_HALO, DIN), rev(D), rev(D), rev(D), rev(DR), halo(8, DR),
                  _resident((4, PG, PG)), _resident((1, DP)), _resident((DP, D)), _resident((4, DR)), _resident((1, DR)),
                  _resident((NH, HD, HD)), _resident((1, DR)), _resident((NH, HD, HD)), _resident((1, DR)),
                  _resident((1, DR)), _resident((DR, D))],
        out_specs=[rev(DIN), rev(D), rev(D), _resident((MAT_ROWS, HD)), _resident((VEC_ROWS, DR))],
        out_shape=[jax.ShapeDtypeStruct((S, DIN), BF16), jax.ShapeDtypeStruct((S, D), BF16),
                   jax.ShapeDtypeStruct((S, D), BF16), jax.ShapeDtypeStruct((MAT_ROWS, HD), F32),
                   jax.ShapeDtypeStruct((VEC_ROWS, DR), F32)],
        scratch_shapes=[pltpu.VMEM((POOL_HALO, DP), F32), pltpu.VMEM((CONV_HALO, DR), F32), pltpu.VMEM((8, DR), F32),
                        pltpu.VMEM((8, DR), F32)],
        operands=(proj, proj, dmixo, y_pool, y_rnn, hr, hr, wg, scale, w_pool_out, conv_w, conv_b, wa, ba, wx, bx, lam,
                  w_rnn_out),
        exchange=exchange, exchange_operands=exchange_operands)


def _in_bwd(dproj, x, dx2, norm_mix, w_in, exchange=None, exchange_operands=(), tm=512):
    S = x.shape[0]

    def body(dp_ref, x_ref, dx2_ref, g_ref, w_ref, dx_ref, dg_ref):
        i = pl.program_id(0)

        @pl.when(i == 0)
        def _():
            dg_ref[...] = jnp.zeros_like(dg_ref)

        dh = _dot(dp_ref[:, 0:1536], w_ref[0:1536, :])
        dh = dh + _dot(dp_ref[:, 1536:3072], w_ref[1536:3072, :])
        dh = dh + _dot(dp_ref[:, 3072:DIN], w_ref[3072:DIN, :])
        xv = x_ref[...]
        r = lax.rsqrt(jnp.mean(xv * xv, axis=-1, keepdims=True) + EPS)
        xh = xv * r
        dg_ref[0:1, :] += jnp.sum(dh * xh, axis=0, keepdims=True)
        dxh = dh * g_ref[...]
        dx_ref[...] = dx2_ref[...] + r * (dxh - xh * jnp.mean(dxh * xh, axis=-1, keepdims=True))

    return _call(
        body, "in_bwd", (S // tm,),
        in_specs=[_rows(DIN, tm), _rows(D, tm), _rows(D, tm), _resident((1, D)), _resident((DIN, D))],
        out_specs=[_rows(D, tm), _resident((8, D))],
        out_shape=[jax.ShapeDtypeStruct((S, D), F32), jax.ShapeDtypeStruct((8, D), F32)],
        operands=(dproj, x, dx2, norm_mix, w_in), exchange=exchange, exchange_operands=exchange_operands)


def _wgrad(a, b, name, tk, tn, exchange=None, exchange_operands=()):
    S, K = a.shape
    N = b.shape[1]

    def body(a_ref, b_ref, o_ref):
        o_ref[...] = _dot_tn(a_ref[...], b_ref[...]).astype(BF16)

    (out,), exchanged = _call(
        body, name, (K // tk, N // tn),
        in_specs=[pl.BlockSpec((S, tk), lambda k, n: (0, k)), pl.BlockSpec((S, tn), lambda k, n: (0, n))],
        out_specs=[pl.BlockSpec((tk, tn), lambda k, n: (k, n))],
        out_shape=[jax.ShapeDtypeStruct((K, N), BF16)],
        operands=(a, b), exchange=exchange, exchange_operands=exchange_operands)
    return (out, exchanged) if exchange is not None else out


VEC_SCALE, VEC_CONV_B, VEC_BA, VEC_BX, VEC_LAM, VEC_CONV_W, VEC_NORM_FINAL, VEC_NORM_FFN = 0, 1, 2, 3, 4, 5, 9, 10


class _Big:
    def __init__(self, name, rows, cols, axis, n, dtype=BF16, transposed=False):
        self.name, self.rows, self.cols, self.axis, self.n, self.dtype = name, rows, cols, axis, n, dtype
        self.transposed = transposed
        self.block_shape = (rows, n) if axis == 1 else (n, cols)

    def block(self, ref, p):
        if self.axis == 1:
            return ref.at[:, pl.ds(pl.multiple_of(p * self.n, 128), self.n)]
        return ref.at[pl.ds(pl.multiple_of(p * self.n, 16 if self.dtype == BF16 else 8), self.n), :]

    def block_index(self, p):
        return (0, p) if self.axis == 1 else (p, 0)


BIG = (_Big("w_in", DIN, D, 0, DIN // 8, transposed=True), _Big("w_pool_out", DP, D, 1, D // 8),
       _Big("w_rnn_out", DR, D, 0, DR // 8), _Big("w_o", D, D, 0, D // 8),
       _Big("w_ffn_in", 2 * DFF, D, 0, 2 * DFF // 8, transposed=True), _Big("w_ffn_out", DFF, D, 0, DFF // 8))
CONV_W = _Big("conv_w", 8, DR, 1, DR // 8, F32)
GATHERED = BIG + (CONV_W,)

HBM_SPEC = pl.BlockSpec(memory_space=pl.ANY)
VMEM_SPEC = pl.BlockSpec(memory_space=pltpu.VMEM)


def _place():
    x, y, c = (lax.axis_index(a) for a in MESH_AXES)
    other_chips = [(1 - x, y), (x, 1 - y), (1 - x, 1 - y)]
    return x, y, c, other_chips


def _remote(src, dst, send_sems, recv_sems, idx, to):
    return pltpu.make_async_remote_copy(src_ref=src, dst_ref=dst, send_sem=send_sems.at[idx], recv_sem=recv_sems.at[idx],
                                        device_id=to, device_id_type=MESH)


def _device_index(chip, core):
    return 4 * chip[0] + 2 * chip[1] + core


class _Gather:
    def __init__(self, tensors):
        self.tensors = tuple(tensors)
        n = len(self.tensors)
        self.in_specs = [HBM_SPEC] * n
        self.out_specs = [HBM_SPEC] * n
        self.out_shape = [jax.ShapeDtypeStruct((T.rows, T.cols), T.dtype) for T in self.tensors]
        self.scratch_shapes = [pltpu.VMEM(T.block_shape, T.dtype) for T in self.tensors] + [
            pltpu.SemaphoreType.DMA((n, 7)), pltpu.SemaphoreType.DMA((n, 7)), pltpu.SemaphoreType.DMA((n, 2))]

    def middle_at(self, steps):
        return (3 * steps) // 4

    def _copies(self, ins, outs, scratch):
        n = len(self.tensors)
        mine, (send_sems, recv_sems, loc_sems) = scratch[:n], scratch[n:]
        x, y, c, chips = _place()
        sibling = (x, y, 1 - c)
        me = _device_index((x, y), c)
        loads, stores, first, passed, arrivals, late = [], [], [], [], [], []
        for t, T in enumerate(self.tensors):
            place = T.block(outs[t], me)
            loads.append(pltpu.make_async_copy(ins[t], mine[t], loc_sems.at[t, 0]))
            stores.append(pltpu.make_async_copy(mine[t], place, loc_sems.at[t, 1]))
            first.append(_remote(mine[t], place, send_sems, recv_sems, (t, 0), sibling))
            theirs = T.block(outs[t], _device_index((x, y), 1 - c))
            late.append(_remote(theirs, theirs, send_sems, recv_sems, (t, 0), sibling))
            for k, chip in enumerate(chips):
                first.append(_remote(mine[t], place, send_sems, recv_sems, (t, 1 + k), (*chip, c)))
                land = T.block(outs[t], _device_index(chip, c))
                arrivals.append(_remote(land, land, send_sems, recv_sems, (t, 1 + k), sibling))
                passed.append(_remote(land, land, send_sems, recv_sems, (t, 4 + k), sibling))
                theirs = T.block(outs[t], _device_index(chip, 1 - c))
                late.append(_remote(theirs, theirs, send_sems, recv_sems, (t, 4 + k), sibling))
        return loads, stores, first, passed, arrivals, late

    def start(self, ins, outs, scratch):
        loads, stores, first, _, _, _ = self._copies(ins, outs, scratch)
        for cp in loads:
            cp.start()
        for cp in loads:
            cp.wait()
        for cp in stores + first:
            cp.start()

    def middle(self, ins, outs, scratch):
        _, _, _, passed, arrivals, _ = self._copies(ins, outs, scratch)
        for arrived, cp in zip(arrivals, passed):
            arrived.wait_recv()
            cp.start()

    def finish(self, ins, outs, scratch):
        _, stores, first, passed, _, late = self._copies(ins, outs, scratch)
        for cp in late:
            cp.wait_recv()
        for cp in first + passed:
            cp.wait_send()
        for cp in stores:
            cp.wait()


def _all_gather(blocks, tensors, name):
    gather = _Gather(tensors)

    def body(*refs):
        n = len(gather.tensors)
        ins, outs, scratch = refs[:n], refs[n:2 * n], refs[2 * n:]
        gather.start(ins, outs, scratch)
        gather.middle(ins, outs, scratch)
        gather.finish(ins, outs, scratch)

    return pl.pallas_call(
        body, name=name, in_specs=gather.in_specs, out_specs=gather.out_specs, out_shape=gather.out_shape,
        scratch_shapes=gather.scratch_shapes, compiler_params=pltpu.CompilerParams(vmem_limit_bytes=VMEM_LIMIT),
    )(*blocks)


def _pair_exchange(grads, tensors, name):
    nt = len(tensors)

    def body(*refs):
        ins, outs = refs[:nt], refs[nt:2 * nt]
        send_sems, recv_sems = refs[2 * nt:]
        x, y, c, _ = _place()
        cps = []
        for t, T in enumerate(tensors):
            for j in range(4):
                cp = _remote(T.block(ins[t], 2 * j + 1 - c), outs[t].at[j], send_sems, recv_sems, (t, j), (x, y, 1 - c))
                cp.start()
                cps.append(cp)
        for cp in cps:
            cp.wait()

    return pl.pallas_call(
        body, name=name,
        in_specs=[HBM_SPEC] * nt, out_specs=[HBM_SPEC] * nt,
        out_shape=[jax.ShapeDtypeStruct((4,) + T.block_shape, BF16) for T in tensors],
        scratch_shapes=[pltpu.SemaphoreType.DMA((nt, 4)), pltpu.SemaphoreType.DMA((nt, 4))],
    )(*grads)


def _pair_sum(T, g, lz1, where):
    blk = T.block_shape

    def body(where_ref, g_ref, l_ref, o_ref):
        o_ref[...] = (g_ref[...].astype(F32) + l_ref[...].astype(F32)).astype(BF16)

    return pl.pallas_call(
        body, name="grad_pair_sum_" + T.name,
        grid_spec=pltpu.PrefetchScalarGridSpec(
            num_scalar_prefetch=1, grid=(3,),
            in_specs=[pl.BlockSpec(blk, lambda k, w: T.block_index(2 * w[k] + w[3])),
                      pl.BlockSpec((None,) + blk, lambda k, w: (w[k], 0, 0))],
            out_specs=pl.BlockSpec((None,) + blk, lambda k, w: (k, 0, 0))),
        out_shape=jax.ShapeDtypeStruct((3,) + T.block_shape, BF16),
        compiler_params=_params("arbitrary"),
    )(where, g, lz1)


class _Scatter:
    middle = None

    def __init__(self, tensors):
        n = len(tensors)
        self.in_specs = [HBM_SPEC] * n
        self.out_specs = [HBM_SPEC] * n
        self.out_shape = [jax.ShapeDtypeStruct((3,) + T.block_shape, BF16) for T in tensors]
        self.scratch_shapes = [pltpu.SemaphoreType.DMA((n, 3)), pltpu.SemaphoreType.DMA((n, 3))]

    def _copies(self, ins, outs, scratch):
        send_sems, recv_sems = scratch
        x, y, c, chips = _place()
        return [_remote(ins[t].at[k], outs[t].at[k], send_sems, recv_sems, (t, k), (*chip, c))
                for t in range(len(ins)) for k, chip in enumerate(chips)]

    def start(self, ins, outs, scratch):
        for cp in self._copies(ins, outs, scratch):
            cp.start()

    def finish(self, ins, outs, scratch):
        for cp in self._copies(ins, outs, scratch):
            cp.wait()


def _chip_scatter(sums, tensors, name):
    scatter = _Scatter(tensors)
    n = len(tensors)

    def body(*refs):
        ins, outs, scratch = refs[:n], refs[n:2 * n], refs[2 * n:]
        scatter.start(ins, outs, scratch)
        scatter.finish(ins, outs, scratch)

    return pl.pallas_call(
        body, name=name, in_specs=scatter.in_specs, out_specs=scatter.out_specs, out_shape=scatter.out_shape,
        scratch_shapes=scatter.scratch_shapes,
    )(*sums)


def _adamw(w, g, m, v):
    m = ADAM_B1 * m + (1.0 - ADAM_B1) * g
    v = ADAM_B2 * v + (1.0 - ADAM_B2) * (g * g)
    m_hat = m / (1.0 - ADAM_B1 ** ADAM_STEP)
    v_hat = v / (1.0 - ADAM_B2 ** ADAM_STEP)
    delta = -ADAM_LR * (m_hat / (jnp.sqrt(v_hat) + ADAM_EPS) + ADAM_WD * w)
    return delta, m, v


def _final_sum(T, g, lz1, lz2, where, w, m, v):
    rows, cols = T.block_shape
    sub = 4 if T.axis == 0 and rows % 64 == 0 and rows > 256 else 1
    blk = (rows // sub, cols)

    def body(where_ref, g_ref, l1_ref, l2_ref, w_ref, m_ref, v_ref, g_out, d_out, m_out, v_out):
        tot = g_ref[...].astype(F32) + l1_ref[...].astype(F32)
        for k in range(3):
            tot = tot + l2_ref[k].astype(F32)
        g_out[...] = tot
        d_out[...], m_out[...], v_out[...] = _adamw(w_ref[...], tot, m_ref[...], v_ref[...])

    def in_whole(r, wh):
        p = wh[0]
        return (0, p) if T.axis == 1 else (p * sub + r, 0)

    own = pl.BlockSpec(blk, lambda r, wh: (r, 0))
    return pl.pallas_call(
        body, name="grad_final_" + T.name,
        grid_spec=pltpu.PrefetchScalarGridSpec(
            num_scalar_prefetch=1, grid=(sub,),
            in_specs=[pl.BlockSpec(blk, in_whole),
                      pl.BlockSpec((None,) + blk, lambda r, wh: (wh[1], r, 0)),
                      pl.BlockSpec((3,) + blk, lambda r, wh: (0, r, 0)), own, own, own],
            out_specs=[own] * 4),
        out_shape=[jax.ShapeDtypeStruct(T.block_shape, F32)] * 4,
        compiler_params=_params("arbitrary"),
    )(where, g, lz1, lz2, w, m, v)


MAT_PIECE = MAT_ROWS // 8
VEC_PIECE = DR // 8


class _AllReduce:
    def __init__(self, items):
        self.items = tuple(items)
        n = len(self.items)
        self.in_specs = [HBM_SPEC] * n
        self.out_specs = [HBM_SPEC] * n
        self.out_shape = [jax.ShapeDtypeStruct(shape, F32) for shape, _ in self.items]
        pieces = [(shape[0] // 8, shape[1]) if axis == 0 else (shape[0], shape[1] // 8) for shape, axis in self.items]
        self.scratch_shapes = ([pltpu.VMEM((8,) + p, F32) for p in pieces] + [pltpu.VMEM(p, F32) for p in pieces] + [
            pltpu.SemaphoreType.DMA((2 * n, 8)), pltpu.SemaphoreType.DMA((2 * n, 8)), pltpu.SemaphoreType.DMA((2 * n,))])

    def middle_at(self, steps):
        return steps // 2

    def _copies(self, ins, outs, scratch):
        n = len(self.items)
        landed, sums, (send_sems, recv_sems, loc_sems) = scratch[:n], scratch[n:2 * n], scratch[2 * n:]
        x, y, c, _ = _place()
        me = _device_index((x, y), c)

        def peer(r):
            return (1 - x if r & 4 else x, 1 - y if r & 2 else y, 1 - c if r & 1 else c)

        def piece(i, ref, p):
            shape, axis = self.items[i]
            if axis == 0:
                rows = shape[0] // 8
                return ref.at[pl.ds(pl.multiple_of(p * rows, 8), rows), :]
            cols = shape[1] // 8
            return ref.at[:, pl.ds(pl.multiple_of(p * cols, 128), cols)]

        own, scatter, arrivals, keep, spread, late = [], [], [], [], [], []
        for i in range(n):
            own.append(pltpu.make_async_copy(piece(i, ins[i], me), landed[i].at[0], loc_sems.at[2 * i]))
            keep.append(pltpu.make_async_copy(sums[i], piece(i, outs[i], me), loc_sems.at[2 * i + 1]))
            for r in range(1, 8):
                to = peer(r)
                p = _device_index(to[:2], to[2])
                scatter.append(_remote(piece(i, ins[i], p), landed[i].at[r], send_sems, recv_sems, (2 * i, r), to))
                spread.append(_remote(sums[i], piece(i, outs[i], me), send_sems, recv_sems, (2 * i + 1, r), to))
                late.append(_remote(sums[i], piece(i, outs[i], p), send_sems, recv_sems, (2 * i + 1, r), to))
        return own, scatter, keep, spread, late, landed, sums

    def start(self, ins, outs, scratch):
        own, scatter, _, _, _, _, _ = self._copies(ins, outs, scratch)
        for cp in own + scatter:
            cp.start()

    def middle(self, ins, outs, scratch):
        own, scatter, keep, spread, _, landed, sums = self._copies(ins, outs, scratch)
        for cp in own:
            cp.wait()
        for cp in scatter:
            cp.wait_recv()
        for i in range(len(self.items)):
            total = landed[i][0]
            for r in range(1, 8):
                total = total + landed[i][r]
            sums[i][...] = total
        for cp in keep + spread:
            cp.start()

    def finish(self, ins, outs, scratch):
        _, scatter, keep, spread, late, _, _ = self._copies(ins, outs, scratch)
        for cp in late:
            cp.wait_recv()
        for cp in scatter + spread:
            cp.wait_send()
        for cp in keep:
            cp.wait()


class _Both:
    def __init__(self, a, b):
        self.a, self.b = a, b
        self.in_specs, self.out_specs = a.in_specs + b.in_specs, a.out_specs + b.out_specs
        self.out_shape, self.scratch_shapes = a.out_shape + b.out_shape, a.scratch_shapes + b.scratch_shapes

    def middle_at(self, steps):
        return steps // 2

    def _each(self, ins, outs, scratch):
        a = self.a
        i, o, s = len(a.in_specs), len(a.out_specs), len(a.scratch_shapes)
        return (a, ins[:i], outs[:o], scratch[:s]), (self.b, ins[i:], outs[o:], scratch[s:])

    def start(self, ins, outs, scratch):
        for e, i, o, s in self._each(ins, outs, scratch):
            e.start(i, o, s)

    def middle(self, ins, outs, scratch):
        for e, i, o, s in self._each(ins, outs, scratch):
            if e.middle is not None:
                e.middle(i, o, s)

    def finish(self, ins, outs, scratch):
        for e, i, o, s in self._each(ins, outs, scratch):
            e.finish(i, o, s)


def _all_reduce(arrays, items, name):
    reduce = _AllReduce(items)
    n = len(items)

    def body(*refs):
        ins, outs, scratch = refs[:n], refs[n:2 * n], refs[2 * n:]
        reduce.start(ins, outs, scratch)
        reduce.middle(ins, outs, scratch)
        reduce.finish(ins, outs, scratch)

    return pl.pallas_call(
        body, name=name, in_specs=reduce.in_specs, out_specs=reduce.out_specs, out_shape=reduce.out_shape,
        scratch_shapes=reduce.scratch_shapes,
    )(*arrays)


def _adam_small(grads, wmv):
    n = len(grads)

    def body(*refs):
        g_refs, rest = refs[:n], refs[n:]
        ins, outs = rest[:3 * n], rest[3 * n:]
        for i in range(n):
            d, m, v = _adamw(ins[3 * i][...], g_refs[i][...], ins[3 * i + 1][...], ins[3 * i + 2][...])
            outs[3 * i][...], outs[3 * i + 1][...], outs[3 * i + 2][...] = d, m, v

    flat = [a for t in wmv for a in t]
    return pl.pallas_call(
        body, name="adam_small",
        in_specs=[VMEM_SPEC] * (4 * n), out_specs=[VMEM_SPEC] * (3 * n),
        out_shape=[jax.ShapeDtypeStruct(a.shape, F32) for a in flat],
    )(*grads, *flat)


WEIGHT_NAMES = ("norm_mix", "w_in", "w_pool_grp", "pool_scale", "w_pool_out", "conv_w", "conv_b", "w_rg_a", "b_rg_a", "w_rg_x",
                "b_rg_x", "lru_lambda", "w_rnn_out", "w_o", "norm_ffn", "w_ffn_in", "w_ffn_out", "norm_final")


def kernel(x, norm_mix, w_in, w_pool_grp, pool_scale, w_pool_out, conv_w, conv_b, w_rg_a, b_rg_a, w_rg_x, b_rg_x, lru_lambda, w_rnn_out, w_o, norm_ffn, w_ffn_in, w_ffn_out, norm_final, loss_target, m_norm_mix, m_w_in, m_w_pool_grp, m_pool_scale, m_w_pool_out, m_conv_w, m_conv_b, m_w_rg_a, m_b_rg_a, m_w_rg_x, m_b_rg_x, m_lru_lambda, m_w_rnn_out, m_w_o, m_norm_ffn, m_w_ffn_in, m_w_ffn_out, m_norm_final, v_norm_mix, v_w_in, v_w_pool_grp, v_pool_scale, v_w_pool_out, v_conv_w, v_conv_b, v_w_rg_a, v_b_rg_a, v_w_rg_x, v_b_rg_x, v_lru_lambda, v_w_rnn_out, v_w_o, v_norm_ffn, v_w_ffn_in, v_w_ffn_out, v_norm_final):
    w = dict(norm_mix=norm_mix, w_in=w_in, w_pool_grp=w_pool_grp, pool_scale=pool_scale, w_pool_out=w_pool_out, conv_w=conv_w,
             conv_b=conv_b, w_rg_a=w_rg_a, b_rg_a=b_rg_a, w_rg_x=w_rg_x, b_rg_x=b_rg_x, lru_lambda=lru_lambda,
             w_rnn_out=w_rnn_out, w_o=w_o, norm_ffn=norm_ffn, w_ffn_in=w_ffn_in, w_ffn_out=w_ffn_out, norm_final=norm_final)
    m = dict(norm_mix=m_norm_mix, w_in=m_w_in, w_pool_grp=m_w_pool_grp, pool_scale=m_pool_scale, w_pool_out=m_w_pool_out,
             conv_w=m_conv_w, conv_b=m_conv_b, w_rg_a=m_w_rg_a, b_rg_a=m_b_rg_a, w_rg_x=m_w_rg_x, b_rg_x=m_b_rg_x,
             lru_lambda=m_lru_lambda, w_rnn_out=m_w_rnn_out, w_o=m_w_o, norm_ffn=m_norm_ffn, w_ffn_in=m_w_ffn_in,
             w_ffn_out=m_w_ffn_out, norm_final=m_norm_final)
    v = dict(norm_mix=v_norm_mix, w_in=v_w_in, w_pool_grp=v_w_pool_grp, pool_scale=v_pool_scale, w_pool_out=v_w_pool_out,
             conv_w=v_conv_w, conv_b=v_conv_b, w_rg_a=v_w_rg_a, b_rg_a=v_b_rg_a, w_rg_x=v_w_rg_x, b_rg_x=v_b_rg_x,
             lru_lambda=v_lru_lambda, w_rnn_out=v_w_rnn_out, w_o=v_w_o, norm_ffn=v_norm_ffn, w_ffn_in=v_w_ffn_in,
             w_ffn_out=v_w_ffn_out, norm_final=v_norm_final)
    xi, yi, ci = (lax.axis_index(a) for a in MESH_AXES)
    chip = 2 * xi + yi
    other = [2 * (1 - xi) + yi, 2 * xi + (1 - yi), 2 * (1 - xi) + (1 - yi)]

    def held(T, a):
        return jnp.swapaxes(a, 0, 1) if T.transposed else a

    where4 = jnp.stack(other + [ci]).astype(jnp.int32)
    where2 = jnp.stack([2 * chip + ci, chip]).astype(jnp.int32)
    by_name = {T.name: T for T in GATHERED}
    block = {T.name: held(T, w[T.name][0]).astype(T.dtype) for T in BIG}
    block["conv_w"] = jnp.pad(conv_w[0], ((0, CONV_W.rows - 4), (0, 0)))

    def gather_of(*names):
        return dict(exchange=_Gather([by_name[n] for n in names]), exchange_operands=[block[n] for n in names])

    def pair_sums(names, partials, tag):
        ts = [by_name[n] for n in names]
        landed = _pair_exchange(partials, ts, "grad_pair_exchange_" + tag)
        return landed, [_pair_sum(T, g, l, where4) for T, g, l in zip(ts, partials, landed)]

    xs, target = x[0], loss_target[0]
    wg_b, wa_b, wx_b = (a[0].astype(BF16) for a in (w_pool_grp, w_rg_a, w_rg_x))
    ba2, bx2 = b_rg_a.reshape(1, DR), b_rg_x.reshape(1, DR)
    (w_in_g,) = _all_gather([block["w_in"]], [by_name["w_in"]], "all_gather_w_in")
    (proj, h1), (w_pool_out_g, w_rnn_out_g, w_o_g, conv_g) = _in_proj(
        xs, norm_mix, w_in_g, **gather_of("w_pool_out", "w_rnn_out", "w_o", "conv_w"))
    mixer_weights = (wg_b, pool_scale, w_pool_out_g, conv_g[0:4], conv_b, wa_b, ba2, wx_b, bx2, lru_lambda, w_rnn_out_g)
    (_, pm, y_pool, hr, z, y_rnn), (w_ffn_in_g,) = _mixer_fwd(proj, *mixer_weights, **gather_of("w_ffn_in"))
    (mix, x2, h2), _ = _merge_out(xs, proj, y_pool, y_rnn, w_o_g, norm_ffn)
    (gu, act), (w_ffn_out_g,) = _ffn_up(h2, w_ffn_in_g, **gather_of("w_ffn_out"))
    dx3, dx3b, loss_part, dvec_fin = _ffn_down_loss(act, x2, target, w_ffn_out_g, norm_final.reshape(1, D))
    loss = lax.psum(loss_part[0, 0], MESH_AXES)

    dgu = _ffn_bwd_down(dx3b, gu, w_ffn_out_g)
    dx2, dx2b, dmixo, dvec_ffn = _ffn_bwd_up(dgu, x2, dx3, w_ffn_in_g, norm_ffn, w_o_g)
    names_a = ("w_ffn_in", "w_ffn_out", "w_o")
    part_a = [_wgrad(dgu, h2, "wgrad_ffn_in", 1408, 512), _wgrad(act, dx3b, "wgrad_ffn_out", 1408, 512),
              _wgrad(mix, dx2b, "wgrad_o", 1024, 1024)]
    lz1_a, sums_a = pair_sums(names_a, part_a, "ffn")
    (dproj, dypb, dyrb, dmat, dvec_mix), lz2_a = _mixer_bwd(
        proj, dmixo, y_pool, y_rnn, hr, *mixer_weights,
        exchange=_Scatter([by_name[n] for n in names_a]), exchange_operands=sums_a)
    names_b = ("w_pool_out", "w_rnn_out")
    part_b = [_wgrad(pm, dypb, "wgrad_pool_out", 512, 1024), _wgrad(z, dyrb, "wgrad_rnn_out", 1024, 1024)]
    lz1_b, sums_b = pair_sums(names_b, part_b, "mix")
    dvec = jnp.concatenate([dvec_mix[0:9], dvec_fin[0:1], dvec_ffn[0:1], jnp.zeros((VEC_ROWS - 11, DR), F32)], axis=0)
    g_in, exchanged = _wgrad(
        dproj, h1, "wgrad_in", 1152, 1024,
        exchange=_Both(_Scatter([by_name[n] for n in names_b]), _AllReduce([((MAT_ROWS, HD), 0), ((VEC_ROWS, DR), 1)])),
        exchange_operands=sums_b + [dmat, dvec])
    lz2_b, (mat, vec) = exchanged[:2], exchanged[2:]
    lz1_c, sums_c = pair_sums(("w_in",), [g_in], "in")
    (grad_x, dvec_in), lz2_c = _in_bwd(dproj, xs, dx2, norm_mix, w_in_g,
                                       exchange=_Scatter([by_name["w_in"]]), exchange_operands=sums_c)
    (vec_in,) = _all_reduce([dvec_in], [((8, D), 1)], "all_reduce_norm_mix")

    grads, delta, new_m, new_v = {}, {}, {}, {}
    for n, g, l1, l2 in zip(names_a + names_b + ("w_in",), part_a + part_b + [g_in], lz1_a + lz1_b + lz1_c,
                            lz2_a + lz2_b + lz2_c):
        T = by_name[n]
        out = _final_sum(T, g, l1, l2, where2, held(T, w[n][0]), held(T, m[n][0]), held(T, v[n][0]))
        grads[n], delta[n], new_m[n], new_v[n] = (held(T, a) for a in out)
    me = 4 * xi + 2 * yi + ci
    small_grads = dict(
        w_pool_grp=mat[0:MAT_WA], w_rg_a=mat[MAT_WA:MAT_WX], w_rg_x=mat[MAT_WX:MAT_ROWS],
        pool_scale=vec[VEC_SCALE:VEC_SCALE + 1, 0:DP], conv_b=vec[VEC_CONV_B:VEC_CONV_B + 1],
        b_rg_a=vec[VEC_BA:VEC_BA + 1], b_rg_x=vec[VEC_BX:VEC_BX + 1], lru_lambda=vec[VEC_LAM:VEC_LAM + 1],
        conv_w=lax.dynamic_slice(vec, (VEC_CONV_W, VEC_PIECE * me), (4, VEC_PIECE)),
        norm_final=vec[VEC_NORM_FINAL:VEC_NORM_FINAL + 1], norm_ffn=vec[VEC_NORM_FFN:VEC_NORM_FFN + 1],
        norm_mix=vec_in[0:1])
    names = list(small_grads)
    as2d = lambda a, g: a.reshape(g.shape)
    upd = _adam_small([small_grads[n] for n in names],
                      [(as2d(w[n], small_grads[n]), as2d(m[n], small_grads[n]), as2d(v[n], small_grads[n])) for n in names])
    for i, n in enumerate(names):
        grads[n] = small_grads[n]
        delta[n], new_m[n], new_v[n] = upd[3 * i:3 * i + 3]

    shaped = lambda d: [d[n].reshape(w[n].shape) for n in WEIGHT_NAMES]
    return (loss, grad_x[None], *shaped(grads), *shaped(delta), *shaped(new_m), *shaped(new_v))
```

```python
import functools
import math

import jax
import jax.numpy as jnp
from jax import lax
from jax.experimental import pallas as pl
from jax.experimental.pallas import tpu as pltpu

F32 = jnp.float32
BF16 = jnp.bfloat16

D = 1024
DP = 512
PG = 128
WINDOWS = (2, 4, 8, 16)
DR = 1024
NH = 8
HD = 128
DIN = 4608
DFF = 2816
EPS = 1e-6
LRU_C = 8.0
POOL_HALO = 16
CONV_HALO = 8

ADAM_LR = 0.001
ADAM_B1 = 0.9
ADAM_B2 = 0.999
ADAM_EPS = 1e-08
ADAM_WD = 0.01
ADAM_STEP = 10

VMEM_LIMIT = 56 * 1024 * 1024
MESH_AXES = ("x", "y", "c")
MESH = pl.DeviceIdType.MESH


def _dot(a, b):
    return jnp.dot(a, b, preferred_element_type=F32)


def _dot_nt(a, b):
    return lax.dot_general(a, b, (((1,), (1,)), ((), ())), preferred_element_type=F32)


def _dot_tn(a, b):
    return lax.dot_general(a, b, (((0,), (0,)), ((), ())), preferred_element_type=F32)


def _params(*sem):
    return pltpu.CompilerParams(dimension_semantics=sem, vmem_limit_bytes=VMEM_LIMIT)


def _resident(shape):
    nd = len(shape)
    return pl.BlockSpec(shape, lambda i: (0,) * nd, pipeline_mode=pl.Buffered(1))


def _rows(shape_cols, tm):
    return pl.BlockSpec((tm, shape_cols), lambda i: (i, 0))


def _call(body, name, grid, in_specs, out_specs, out_shape, operands, scratch_shapes=(), exchange=None, exchange_operands=()):
    n_in, n_out, n_scr = len(in_specs), len(out_specs), len(scratch_shapes)
    steps = math.prod(grid)
    if exchange is None:
        outs = pl.pallas_call(body, name=name, grid=grid, in_specs=in_specs, out_specs=out_specs, out_shape=out_shape,
                              scratch_shapes=list(scratch_shapes), compiler_params=_params(*["arbitrary"] * len(grid)))(*operands)
        return outs, []
    e_in, e_out = len(exchange.in_specs), len(exchange.out_specs)

    def hosted(*refs):
        ins, refs = refs[:n_in], refs[n_in:]
        e_ins, refs = refs[:e_in], refs[e_in:]
        outs, refs = refs[:n_out], refs[n_out:]
        e_outs, refs = refs[:e_out], refs[e_out:]
        scr, e_scr = refs[:n_scr], refs[n_scr:]
        step = pl.program_id(0)
        for axis in range(1, len(grid)):
            step = step * grid[axis] + pl.program_id(axis)
        pl.when(step == 0)(lambda: exchange.start(e_ins, e_outs, e_scr))
        if exchange.middle is not None:
            pl.when(step == exchange.middle_at(steps))(lambda: exchange.middle(e_ins, e_outs, e_scr))
        body(*ins, *outs, *scr)
        pl.when(step == steps - 1)(lambda: exchange.finish(e_ins, e_outs, e_scr))

    outs = pl.pallas_call(
        hosted, name=name, grid=grid, in_specs=list(in_specs) + exchange.in_specs,
        out_specs=list(out_specs) + exchange.out_specs, out_shape=list(out_shape) + exchange.out_shape,
        scratch_shapes=list(scratch_shapes) + exchange.scratch_shapes,
        compiler_params=_params(*["arbitrary"] * len(grid)))(*operands, *exchange_operands)
    return outs[:n_out], outs[n_out:]


GELU_C = math.sqrt(2.0 / math.pi)
GELU_K = 0.044715 * GELU_C


def _gelu(x, with_grad=False):
    x2 = x * x
    t = jnp.tanh(x * (GELU_C + GELU_K * x2))
    hx = 0.5 * x
    y = hx + hx * t
    if not with_grad:
        return y
    return y, 0.5 + 0.5 * t + hx * (1.0 - t * t) * (GELU_C + (3.0 * GELU_K) * x2)


def _softplus_neg(lam):
    z = jnp.exp(-jnp.abs(lam))
    u = 1.0 + z
    dlt = u - 1.0
    log1p = jnp.where(dlt == 0.0, z, jnp.log(u) * (z / jnp.where(dlt == 0.0, 1.0, dlt)))
    return jnp.maximum(-lam, 0.0) + log1p


def _sigmoid(x):
    return 0.5 * jnp.tanh(0.5 * x) + 0.5


def _linear_scan(out_ref, A, B, h0, reverse):
    n = A.shape[0]
    sub = lax.broadcasted_iota(jnp.int32, (8, 1), 0)
    tiles = range(n // 8 - 1, -1, -1) if reverse else range(n // 8)
    carry = h0
    for j in tiles:
        a, b = A[8 * j:8 * j + 8, :], B[8 * j:8 * j + 8, :]
        for d in (1, 2, 4):
            keep = (sub < 8 - d) if reverse else (sub >= d)
            shift = 8 - d if reverse else d
            b = jnp.where(keep, a * pltpu.roll(b, shift, axis=0) + b, b)
            a = jnp.where(keep, a * pltpu.roll(a, shift, axis=0), a)
        h = a * carry + b
        out_ref[8 * j:8 * j + 8, :] = h
        carry = h[0:1, :] if reverse else h[7:8, :]
    return carry


def _pool_windows(ext, shift_sign):
    n = ext.shape[0]
    s = ext
    outs = []
    for w in WINDOWS:
        d = w // 2
        s = s + pltpu.roll(s, d if shift_sign > 0 else n - d, axis=0)
        outs.append(s[:, :PG])
        s = s[:, PG:]
    return outs


def _conv_taps(uext):
    taps = []
    for k in range(4):
        sh = 3 - k
        v = uext if sh == 0 else pltpu.roll(uext, sh, axis=0)
        taps.append(v[CONV_HALO:, :])
    return taps


def _gates(v, wa_ref, ba_ref, wx_ref, bx_ref, sp):
    vb = v.astype(BF16)
    ra, rx = [], []
    for h in range(NH):
        vh = vb[:, h * HD:(h + 1) * HD]
        ra.append(_dot(vh, wa_ref[h]))
        rx.append(_dot(vh, wx_ref[h]))
    r = _sigmoid(jnp.concatenate(ra, axis=1) + ba_ref[...])
    i = _sigmoid(jnp.concatenate(rx, axis=1) + bx_ref[...])
    log_a = r * ((-LRU_C) * sp)
    a = jnp.exp(log_a)
    one_minus = -jnp.tanh(log_a) * (1.0 + a * a)
    return r, i, a, jnp.sqrt(one_minus), lax.rsqrt(one_minus)


def _in_proj(x, norm_mix, w_in, exchange=None, exchange_operands=(), tm=512):
    S = x.shape[0]

    def body(x_ref, g_ref, w_ref, proj_ref, h_ref):
        xv = x_ref[...]
        r = lax.rsqrt(jnp.mean(xv * xv, axis=-1, keepdims=True) + EPS)
        h = (xv * r * g_ref[...]).astype(BF16)
        h_ref[...] = h
        for n0 in range(0, DIN, 512):
            proj_ref[:, n0:n0 + 512] = _dot_nt(h, w_ref[n0:n0 + 512, :])

    return _call(
        body, "in_proj", (S // tm,),
        in_specs=[_rows(D, tm), _resident((1, D)), _resident((DIN, D))],
        out_specs=[_rows(DIN, tm), _rows(D, tm)],
        out_shape=[jax.ShapeDtypeStruct((S, DIN), F32), jax.ShapeDtypeStruct((S, D), BF16)],
        operands=(x, norm_mix, w_in), exchange=exchange, exchange_operands=exchange_operands)


def _mixer_fwd(proj, wg, scale, w_pool_out, conv_w, conv_b, wa, ba, wx, bx, lam, w_rnn_out, exchange=None,
               exchange_operands=(), tm=256):
    S = proj.shape[0]
    UW = DP + 2 * DR

    def body(proj_ref, wg_ref, scale_ref, wpo_ref, cw_ref, cb_ref, wa_ref, ba_ref, wx_ref, bx_ref, lam_ref, wro_ref,
             pooled_ref, pm_ref, ypool_ref, hr_ref, z_ref, yrnn_ref, pool_carry, conv_carry, h_carry):
        i = pl.program_id(0)

        @pl.when(i == 0)
        def _():
            pool_carry[...] = jnp.zeros_like(pool_carry)
            conv_carry[...] = jnp.zeros_like(conv_carry)
            h_carry[...] = jnp.zeros_like(h_carry)

        rows = lax.broadcasted_iota(jnp.int32, (tm, 1), 0)
        t_glob = i * tm + rows

        u_pool = proj_ref[:, 0:DP]
        ext = jnp.concatenate([pool_carry[...], u_pool], axis=0)
        pool_carry[...] = u_pool[tm - POOL_HALO:, :]
        sums = _pool_windows(ext, +1)
        mixed = []
        for g, w in enumerate(WINDOWS):
            inv_cnt = 1.0 / jnp.minimum(t_glob + 1, w).astype(F32)
            pooled_g = sums[g][POOL_HALO:, :] * inv_cnt - u_pool[:, g * PG:(g + 1) * PG]
            pooled_b = pooled_g.astype(BF16)
            pooled_ref[:, g * PG:(g + 1) * PG] = pooled_b
            mixed.append(_dot(pooled_b, wg_ref[g]))
        pm = (jnp.concatenate(mixed, axis=1) * scale_ref[...]).astype(BF16)
        pm_ref[...] = pm
        ypool_ref[...] = _dot(pm, wpo_ref[...])

        u_rnn = proj_ref[:, DP:DP + DR]
        uext = jnp.concatenate([conv_carry[...], u_rnn], axis=0)
        conv_carry[...] = u_rnn[tm - CONV_HALO:, :]
        taps = _conv_taps(uext)
        v = cb_ref[...]
        for k in range(4):
            v = v + taps[k] * cw_ref[k:k + 1, :]
        sp = _softplus_neg(lam_ref[...])
        _, gi, a, mult, _ = _gates(v, wa_ref, ba_ref, wx_ref, bx_ref, sp)
        h_carry[0:1, :] = _linear_scan(hr_ref, a, mult * gi * v, h_carry[0:1, :], reverse=False)
        z = (hr_ref[...] * _gelu(proj_ref[:, DP + DR:UW])).astype(BF16)
        z_ref[...] = z
        yrnn_ref[...] = _dot(z, wro_ref[...])

    return _call(
        body, "mixer_fwd", (S // tm,),
        in_specs=[_rows(UW, tm), _resident((4, PG, PG)), _resident((1, DP)), _resident((DP, D)), _resident((4, DR)),
                  _resident((1, DR)), _resident((NH, HD, HD)), _resident((1, DR)), _resident((NH, HD, HD)),
                  _resident((1, DR)), _resident((1, DR)), _resident((DR, D))],
        out_specs=[_rows(DP, tm), _rows(DP, tm), _rows(D, tm), _rows(DR, tm), _rows(DR, tm), _rows(D, tm)],
        out_shape=[jax.ShapeDtypeStruct((S, DP), BF16), jax.ShapeDtypeStruct((S, DP), BF16),
                   jax.ShapeDtypeStruct((S, D), F32), jax.ShapeDtypeStruct((S, DR), F32),
                   jax.ShapeDtypeStruct((S, DR), BF16), jax.ShapeDtypeStruct((S, D), F32)],
        scratch_shapes=[pltpu.VMEM((POOL_HALO, DP), F32), pltpu.VMEM((CONV_HALO, DR), F32), pltpu.VMEM((8, DR), F32)],
        operands=(proj, wg, scale, w_pool_out, conv_w, conv_b, wa, ba, wx, bx, lam, w_rnn_out),
        exchange=exchange, exchange_operands=exchange_operands)


FF_CHUNK = DFF // 4


def _rms(x):
    r = lax.rsqrt(jnp.mean(x * x, axis=-1, keepdims=True) + EPS)
    return r, x * r


def _rms_bwd(dh, g, r, xh):
    dxh = dh * g
    return r * (dxh - xh * jnp.mean(dxh * xh, axis=-1, keepdims=True))


def _merge_out(x, proj, y_pool, y_rnn, w_o, norm_ffn, exchange=None, exchange_operands=(), tm=512):
    S = x.shape[0]
    GL0 = (DP + 2 * DR) // 512

    def gl_spec(k):
        return pl.BlockSpec((tm, 512), lambda i: (i, GL0 + k))

    def body(x_ref, gl0, gl1, gl2, gl3, yp_ref, yr_ref, wo_ref, gf_ref, mix_ref, x2_ref, h2_ref):
        s_p = _sigmoid(jnp.concatenate([gl0[...], gl1[...]], axis=1))
        s_r = _sigmoid(jnp.concatenate([gl2[...], gl3[...]], axis=1))
        mix = (s_p * yp_ref[...] + s_r * yr_ref[...]).astype(BF16)
        mix_ref[...] = mix
        x2 = x_ref[...] + _dot(mix, wo_ref[...])
        x2_ref[...] = x2
        _, xh2 = _rms(x2)
        h2_ref[...] = (xh2 * gf_ref[...]).astype(BF16)

    return _call(
        body, "merge_out", (S // tm,),
        in_specs=[_rows(D, tm), gl_spec(0), gl_spec(1), gl_spec(2), gl_spec(3), _rows(D, tm), _rows(D, tm),
                  _resident((D, D)), _resident((1, D))],
        out_specs=[_rows(D, tm), _rows(D, tm), _rows(D, tm)],
        out_shape=[jax.ShapeDtypeStruct((S, D), BF16), jax.ShapeDtypeStruct((S, D), F32), jax.ShapeDtypeStruct((S, D), BF16)],
        operands=(x, proj, proj, proj, proj, y_pool, y_rnn, w_o, norm_ffn),
        exchange=exchange, exchange_operands=exchange_operands)


def _ffn_up(h2, w_ffn_in, exchange=None, exchange_operands=(), tm=512):
    S = h2.shape[0]

    def body(h_ref, w_ref, gu_ref, act_ref):
        h = h_ref[...]
        for c0 in range(0, DFF, FF_CHUNK):
            gate = _dot_nt(h, w_ref[c0:c0 + FF_CHUNK, :])
            up = _dot_nt(h, w_ref[DFF + c0:DFF + c0 + FF_CHUNK, :])
            gu_ref[:, c0:c0 + FF_CHUNK] = gate.astype(BF16)
            gu_ref[:, DFF + c0:DFF + c0 + FF_CHUNK] = up.astype(BF16)
            act_ref[:, c0:c0 + FF_CHUNK] = (gate * _sigmoid(gate) * up).astype(BF16)

    return _call(
        body, "ffn_up", (S // tm,),
        in_specs=[_rows(D, tm), _resident((2 * DFF, D))],
        out_specs=[_rows(2 * DFF, tm), _rows(DFF, tm)],
        out_shape=[jax.ShapeDtypeStruct((S, 2 * DFF), BF16), jax.ShapeDtypeStruct((S, DFF), BF16)],
        operands=(h2, w_ffn_in), exchange=exchange, exchange_operands=exchange_operands)


def _ffn_down_loss(act, x2, target, w_ffn_out, norm_final, tm=512):
    S = act.shape[0]

    def body(act_ref, x2_ref, t_ref, w_ref, gn_ref, dx3_ref, dx3b_ref, loss_ref, dvec_ref):
        i = pl.program_id(0)

        @pl.when(i == 0)
        def _():
            loss_ref[...] = jnp.zeros_like(loss_ref)
            dvec_ref[...] = jnp.zeros_like(dvec_ref)

        x3 = x2_ref[...] + _dot(act_ref[...], w_ref[...])
        r3, xh3 = _rms(x3)
        g_fin = gn_ref[...]
        e = xh3 * g_fin - t_ref[...]
        loss_ref[...] += jnp.sum(e * e, axis=(0, 1), keepdims=True) * (0.5 / D)
        dy = e * (1.0 / D)
        dvec_ref[0:1, :] += jnp.sum(dy * xh3, axis=0, keepdims=True)
        dx3 = _rms_bwd(dy, g_fin, r3, xh3)
        dx3_ref[...] = dx3
        dx3b_ref[...] = dx3.astype(BF16)

    return pl.pallas_call(
        body, name="ffn_down_loss", grid=(S // tm,),
        in_specs=[_rows(DFF, tm), _rows(D, tm), _rows(D, tm), _resident((DFF, D)), _resident((1, D))],
        out_specs=[_rows(D, tm), _rows(D, tm), _resident((1, 1)), _resident((8, D))],
        out_shape=[jax.ShapeDtypeStruct((S, D), F32), jax.ShapeDtypeStruct((S, D), BF16),
                   jax.ShapeDtypeStruct((1, 1), F32), jax.ShapeDtypeStruct((8, D), F32)],
        compiler_params=_params("arbitrary"),
    )(act, x2, target, w_ffn_out, norm_final)


def _ffn_bwd_down(dx3b, gu, w_ffn_out, tm=512):
    S = dx3b.shape[0]

    def body(d_ref, gu_ref, w_ref, dgu_ref):
        d = d_ref[...]
        for c0 in range(0, DFF, FF_CHUNK):
            dact = _dot_nt(d, w_ref[c0:c0 + FF_CHUNK, :])
            gate = gu_ref[:, c0:c0 + FF_CHUNK].astype(F32)
            up = gu_ref[:, DFF + c0:DFF + c0 + FF_CHUNK].astype(F32)
            sg = _sigmoid(gate)
            dgu_ref[:, c0:c0 + FF_CHUNK] = (dact * up * (sg * (1.0 + gate * (1.0 - sg)))).astype(BF16)
            dgu_ref[:, DFF + c0:DFF + c0 + FF_CHUNK] = (dact * (gate * sg)).astype(BF16)

    return pl.pallas_call(
        body, name="ffn_bwd_down", grid=(S // tm,),
        in_specs=[_rows(D, tm), _rows(2 * DFF, tm), _resident((DFF, D))],
        out_specs=_rows(2 * DFF, tm),
        out_shape=jax.ShapeDtypeStruct((S, 2 * DFF), BF16),
        compiler_params=_params("parallel"),
    )(dx3b, gu, w_ffn_out)


def _ffn_bwd_up(dgu, x2, dx3, w_ffn_in, norm_ffn, w_o, tm=512):
    S = dgu.shape[0]

    def body(dgu_ref, x2_ref, dx3_ref, wfi_ref, gf_ref, wo_ref, dx2_ref, dx2b_ref, dmixo_ref, dvec_ref):
        i = pl.program_id(0)

        @pl.when(i == 0)
        def _():
            dvec_ref[...] = jnp.zeros_like(dvec_ref)

        dh2 = _dot(dgu_ref[:, 0:DFF], wfi_ref[0:DFF, :]) + _dot(dgu_ref[:, DFF:2 * DFF], wfi_ref[DFF:2 * DFF, :])
        r2, xh2 = _rms(x2_ref[...])
        dvec_ref[0:1, :] += jnp.sum(dh2 * xh2, axis=0, keepdims=True)
        dx2 = dx3_ref[...] + _rms_bwd(dh2, gf_ref[...], r2, xh2)
        dx2_ref[...] = dx2
        dx2b = dx2.astype(BF16)
        dx2b_ref[...] = dx2b
        dmixo_ref[...] = _dot_nt(dx2b, wo_ref[...])

    return pl.pallas_call(
        body, name="ffn_bwd_up", grid=(S // tm,),
        in_specs=[_rows(2 * DFF, tm), _rows(D, tm), _rows(D, tm), _resident((2 * DFF, D)), _resident((1, D)),
                  _resident((D, D))],
        out_specs=[_rows(D, tm), _rows(D, tm), _rows(D, tm), _resident((8, D))],
        out_shape=[jax.ShapeDtypeStruct((S, D), F32), jax.ShapeDtypeStruct((S, D), BF16), jax.ShapeDtypeStruct((S, D), F32),
                   jax.ShapeDtypeStruct((8, D), F32)],
        compiler_params=_params("arbitrary"),
    )(dgu, x2, dx3, w_ffn_in, norm_ffn, w_o)


VEC_ROWS = 16
MAT_WA = 4 * PG
MAT_WX = MAT_WA + NH * HD
MAT_ROWS = MAT_WX + NH * HD


def _mixer_bwd(proj, dmixo, y_pool, y_rnn, hr, wg, scale, w_pool_out, conv_w, conv_b, wa, ba, wx, bx, lam, w_rnn_out,
               exchange=None, exchange_operands=(), tm=256):
    S = proj.shape[0]
    nt = S // tm

    def rev(cols):
        return pl.BlockSpec((tm, cols), lambda i: (nt - 1 - i, 0))

    def halo(rows_, cols):
        per = tm // rows_
        return pl.BlockSpec((rows_, cols), lambda i: (jnp.maximum((nt - 1 - i) * per - 1, 0), 0))

    def body(proj_ref, projh_ref, dmixo_ref, yp_ref, yr_ref, hr_ref, hrh_ref, wg_ref, scale_ref, wpo_ref, cw_ref, cb_ref,
             wa_ref, ba_ref, wx_ref, bx_ref, lam_ref, wro_ref,
             dproj_ref, dypb_ref, dyrb_ref, dmat_ref, dvec_ref,
             q_carry, dv_carry, a_carry, g_carry, g_scr):
        i = pl.program_id(0)
        ti = nt - 1 - i

        @pl.when(i == 0)
        def _():
            q_carry[...] = jnp.zeros_like(q_carry)
            dv_carry[...] = jnp.zeros_like(dv_carry)
            a_carry[...] = jnp.zeros_like(a_carry)
            g_carry[...] = jnp.zeros_like(g_carry)
            dmat_ref[...] = jnp.zeros_like(dmat_ref)
            dvec_ref[...] = jnp.zeros_like(dvec_ref)

        rows = lax.broadcasted_iota(jnp.int32, (tm, 1), 0)
        t_glob = ti * tm + rows
        has_prev = (ti > 0).astype(F32)
        dmixo = dmixo_ref[...]

        s_p = _sigmoid(proj_ref[:, DP + 2 * DR:DP + 2 * DR + D])
        s_r = _sigmoid(proj_ref[:, DP + 2 * DR + D:DIN])
        dproj_ref[:, DP + 2 * DR:DP + 2 * DR + D] = (dmixo * yp_ref[...] * s_p * (1.0 - s_p)).astype(BF16)
        dproj_ref[:, DP + 2 * DR + D:DIN] = (dmixo * yr_ref[...] * s_r * (1.0 - s_r)).astype(BF16)
        dyp = (dmixo * s_p).astype(BF16)
        dyr = (dmixo * s_r).astype(BF16)
        dypb_ref[...] = dyp
        dyrb_ref[...] = dyr

        dz = _dot_nt(dyr, wro_ref[...])
        u_gate = proj_ref[:, DP + DR:DP + 2 * DR]
        gg, dgelu = _gelu(u_gate, with_grad=True)
        hr_t = hr_ref[...]
        dproj_ref[:, DP + DR:DP + 2 * DR] = (dz * hr_t * dgelu).astype(BF16)
        dhr = dz * gg

        u_rnn = proj_ref[:, DP:DP + DR]
        uext = jnp.concatenate([projh_ref[POOL_HALO - CONV_HALO:, DP:DP + DR] * has_prev, u_rnn], axis=0)
        taps = _conv_taps(uext)
        v = cb_ref[...]
        for k in range(4):
            v = v + taps[k] * cw_ref[k:k + 1, :]
        sp = _softplus_neg(lam_ref[...])
        r, gi, a, mult, inv_mult = _gates(v, wa_ref, ba_ref, wx_ref, bx_ref, sp)

        C = jnp.where(rows == tm - 1, a_carry[0:1, :], pltpu.roll(a, tm - 1, axis=0))
        g_carry[0:1, :] = _linear_scan(g_scr, C, dhr, g_carry[0:1, :], reverse=True)
        a_carry[0:1, :] = a[0:1, :]
        g = g_scr[...]

        h_prev = jnp.where(rows == 0, hrh_ref[7:8, :] * has_prev, pltpu.roll(hr_t, 1, axis=0))
        da = g * h_prev
        gm = g * mult
        dmult = g * gi * v
        di = gm * v
        dv = gm * gi
        dlog_a = da * a - dmult * (a * a * inv_mult)
        dvec_ref[4:5, :] += jnp.sum(dlog_a * r, axis=0, keepdims=True)
        dra = (dlog_a * ((-LRU_C) * sp) * r * (1.0 - r))
        drx = di * gi * (1.0 - gi)
        dvec_ref[2:3, :] += jnp.sum(dra, axis=0, keepdims=True)
        dvec_ref[3:4, :] += jnp.sum(drx, axis=0, keepdims=True)
        drab = dra.astype(BF16)
        drxb = drx.astype(BF16)
        vb = v.astype(BF16)
        dvg = []
        for h in range(NH):
            sl = slice(h * HD, (h + 1) * HD)
            dvg.append(_dot_nt(drab[:, sl], wa_ref[h]) + _dot_nt(drxb[:, sl], wx_ref[h]))
            dmat_ref[MAT_WA + h * HD:MAT_WA + (h + 1) * HD, :] += _dot_tn(vb[:, sl], drab[:, sl])
            dmat_ref[MAT_WX + h * HD:MAT_WX + (h + 1) * HD, :] += _dot_tn(vb[:, sl], drxb[:, sl])
        dv = dv + jnp.concatenate(dvg, axis=1)
        dvec_ref[1:2, :] += jnp.sum(dv, axis=0, keepdims=True)
        for k in range(4):
            dvec_ref[5 + k:6 + k, :] += jnp.sum(dv * taps[k], axis=0, keepdims=True)
        dvext = jnp.concatenate([dv, dv_carry[...]], axis=0)
        dv_carry[...] = dv[0:CONV_HALO, :]
        n = tm + CONV_HALO
        du_rnn = dv * cw_ref[3:4, :]
        for k in range(3):
            du_rnn = du_rnn + pltpu.roll(dvext, n - (3 - k), axis=0)[0:tm, :] * cw_ref[k:k + 1, :]
        dproj_ref[:, DP:DP + DR] = du_rnn.astype(BF16)

        dpm = _dot_nt(dyp, wpo_ref[...])
        u_pool = proj_ref[:, 0:DP]
        ext = jnp.concatenate([projh_ref[:, 0:DP] * has_prev, u_pool], axis=0)
        sums = _pool_windows(ext, +1)
        scale_v = scale_ref[...]
        qs = []
        dpooled = []
        dscale = []
        for gi_, w in enumerate(WINDOWS):
            sl = slice(gi_ * PG, (gi_ + 1) * PG)
            inv_cnt = 1.0 / jnp.minimum(t_glob + 1, w).astype(F32)
            pooled_b = (sums[gi_][POOL_HALO:, :] * inv_cnt - u_pool[:, sl]).astype(BF16)
            mixed_g = _dot(pooled_b, wg_ref[gi_])
            dscale.append(jnp.sum(dpm[:, sl] * mixed_g, axis=0, keepdims=True))
            dmixed_b = (dpm[:, sl] * scale_v[:, sl]).astype(BF16)
            dmat_ref[gi_ * PG:(gi_ + 1) * PG, :] += _dot_tn(pooled_b, dmixed_b)
            dp_g = _dot_nt(dmixed_b, wg_ref[gi_])
            dpooled.append(dp_g)
            qs.append(dp_g * inv_cnt)
        dvec_ref[0:1, 0:DP] += jnp.concatenate(dscale, axis=1)
        q = jnp.concatenate(qs, axis=1)
        qext = jnp.concatenate([q, q_carry[...]], axis=0)
        q_carry[...] = q[0:POOL_HALO, :]
        tsum = _pool_windows(qext, -1)
        for gi_ in range(4):
            dproj_ref[:, gi_ * PG:(gi_ + 1) * PG] = (tsum[gi_][0:tm, :] - dpooled[gi_]).astype(BF16)

        @pl.when(i == nt - 1)
        def _():
            dvec_ref[4:5, :] = dvec_ref[4:5, :] * (LRU_C * _sigmoid(-lam_ref[...]))

    return _call(
        body, "mixer_bwd", (nt,),
        in_specs=[rev(DIN), halo(POOL_HALO, DIN), rev(D), rev(D), rev(D), rev(DR), halo(8, DR),
                  _resident((4, PG, PG)), _resident((1, DP)), _resident((DP, D)), _resident((4, DR)), _resident((1, DR)),
                  _resident((NH, HD, HD)), _resident((1, DR)), _resident((NH, HD, HD)), _resident((1, DR)),
                  _resident((1, DR)), _resident((DR, D))],
        out_specs=[rev(DIN), rev(D), rev(D), _resident((MAT_ROWS, HD)), _resident((VEC_ROWS, DR))],
        out_shape=[jax.ShapeDtypeStruct((S, DIN), BF16), jax.ShapeDtypeStruct((S, D), BF16),
                   jax.ShapeDtypeStruct((S, D), BF16), jax.ShapeDtypeStruct((MAT_ROWS, HD), F32),
                   jax.ShapeDtypeStruct((VEC_ROWS, DR), F32)],
        scratch_shapes=[pltpu.VMEM((POOL_HALO, DP), F32), pltpu.VMEM((CONV_HALO, DR), F32), pltpu.VMEM((8, DR), F32),
                        pltpu.VMEM((8, DR), F32), pltpu.VMEM((tm, DR), F32)],
        operands=(proj, proj, dmixo, y_pool, y_rnn, hr, hr, wg, scale, w_pool_out, conv_w, conv_b, wa, ba, wx, bx, lam,
                  w_rnn_out),
        exchange=exchange, exchange_operands=exchange_operands)


def _in_bwd(dproj, x, dx2, norm_mix, w_in, exchange=None, exchange_operands=(), tm=512):
    S = x.shape[0]

    def body(dp_ref, x_ref, dx2_ref, g_ref, w_ref, dx_ref, dg_ref):
        i = pl.program_id(0)

        @pl.when(i == 0)
        def _():
            dg_ref[...] = jnp.zeros_like(dg_ref)

        dh = _dot(dp_ref[:, 0:1536], w_ref[0:1536, :])
        dh = dh + _dot(dp_ref[:, 1536:3072], w_ref[1536:3072, :])
        dh = dh + _dot(dp_ref[:, 3072:DIN], w_ref[3072:DIN, :])
        xv = x_ref[...]
        r = lax.rsqrt(jnp.mean(xv * xv, axis=-1, keepdims=True) + EPS)
        xh = xv * r
        dg_ref[0:1, :] += jnp.sum(dh * xh, axis=0, keepdims=True)
        dxh = dh * g_ref[...]
        dx_ref[...] = dx2_ref[...] + r * (dxh - xh * jnp.mean(dxh * xh, axis=-1, keepdims=True))

    return _call(
        body, "in_bwd", (S // tm,),
        in_specs=[_rows(DIN, tm), _rows(D, tm), _rows(D, tm), _resident((1, D)), _resident((DIN, D))],
        out_specs=[_rows(D, tm), _resident((8, D))],
        out_shape=[jax.ShapeDtypeStruct((S, D), F32), jax.ShapeDtypeStruct((8, D), F32)],
        operands=(dproj, x, dx2, norm_mix, w_in), exchange=exchange, exchange_operands=exchange_operands)


def _wgrad(a, b, name, tk, tn, exchange=None, exchange_operands=()):
    S, K = a.shape
    N = b.shape[1]

    def body(a_ref, b_ref, o_ref):
        o_ref[...] = _dot_tn(a_ref[...], b_ref[...]).astype(BF16)

    (out,), exchanged = _call(
        body, name, (K // tk, N // tn),
        in_specs=[pl.BlockSpec((S, tk), lambda k, n: (0, k)), pl.BlockSpec((S, tn), lambda k, n: (0, n))],
        out_specs=[pl.BlockSpec((tk, tn), lambda k, n: (k, n))],
        out_shape=[jax.ShapeDtypeStruct((K, N), BF16)],
        operands=(a, b), exchange=exchange, exchange_operands=exchange_operands)
    return (out, exchanged) if exchange is not None else out


VEC_SCALE, VEC_CONV_B, VEC_BA, VEC_BX, VEC_LAM, VEC_CONV_W, VEC_NORM_FINAL, VEC_NORM_FFN = 0, 1, 2, 3, 4, 5, 9, 10
VEC_LOSS = 11


class _Big:
    def __init__(self, name, rows, cols, axis, n, dtype=BF16, transposed=False):
        self.name, self.rows, self.cols, self.axis, self.n, self.dtype = name, rows, cols, axis, n, dtype
        self.transposed = transposed
        self.block_shape = (rows, n) if axis == 1 else (n, cols)

    def block(self, ref, p):
        if self.axis == 1:
            return ref.at[:, pl.ds(pl.multiple_of(p * self.n, 128), self.n)]
        return ref.at[pl.ds(pl.multiple_of(p * self.n, 16 if self.dtype == BF16 else 8), self.n), :]

    def block_index(self, p):
        return (0, p) if self.axis == 1 else (p, 0)


BIG = (_Big("w_in", DIN, D, 0, DIN // 8, transposed=True), _Big("w_pool_out", DP, D, 1, D // 8),
       _Big("w_rnn_out", DR, D, 0, DR // 8), _Big("w_o", D, D, 0, D // 8),
       _Big("w_ffn_in", 2 * DFF, D, 0, 2 * DFF // 8, transposed=True), _Big("w_ffn_out", DFF, D, 0, DFF // 8))
CONV_W = _Big("conv_w", 8, DR, 1, DR // 8, F32)
GATHERED = BIG + (CONV_W,)

HBM_SPEC = pl.BlockSpec(memory_space=pl.ANY)
VMEM_SPEC = pl.BlockSpec(memory_space=pltpu.VMEM)


def _place():
    x, y, c = (lax.axis_index(a) for a in MESH_AXES)
    other_chips = [(1 - x, y), (x, 1 - y), (1 - x, 1 - y)]
    return x, y, c, other_chips


def _remote(src, dst, send_sems, recv_sems, idx, to):
    return pltpu.make_async_remote_copy(src_ref=src, dst_ref=dst, send_sem=send_sems.at[idx], recv_sem=recv_sems.at[idx],
                                        device_id=to, device_id_type=MESH)


def _device_index(chip, core):
    return 4 * chip[0] + 2 * chip[1] + core


class _Gather:
    def __init__(self, tensors):
        self.tensors = tuple(tensors)
        n = len(self.tensors)
        self.in_specs = [HBM_SPEC] * n
        self.out_specs = [HBM_SPEC] * n
        self.out_shape = [jax.ShapeDtypeStruct((T.rows, T.cols), T.dtype) for T in self.tensors]
        self.scratch_shapes = [pltpu.VMEM(T.block_shape, T.dtype) for T in self.tensors] + [
            pltpu.SemaphoreType.DMA((n, 7)), pltpu.SemaphoreType.DMA((n, 7)), pltpu.SemaphoreType.DMA((n, 2))]

    def middle_at(self, steps):
        return (3 * steps) // 4

    def _copies(self, ins, outs, scratch):
        n = len(self.tensors)
        mine, (send_sems, recv_sems, loc_sems) = scratch[:n], scratch[n:]
        x, y, c, chips = _place()
        sibling = (x, y, 1 - c)
        me = _device_index((x, y), c)
        loads, stores, first, passed, arrivals, late = [], [], [], [], [], []
        for t, T in enumerate(self.tensors):
            place = T.block(outs[t], me)
            loads.append(pltpu.make_async_copy(ins[t], mine[t], loc_sems.at[t, 0]))
            stores.append(pltpu.make_async_copy(mine[t], place, loc_sems.at[t, 1]))
            first.append(_remote(mine[t], place, send_sems, recv_sems, (t, 0), sibling))
            theirs = T.block(outs[t], _device_index((x, y), 1 - c))
            late.append(_remote(theirs, theirs, send_sems, recv_sems, (t, 0), sibling))
            for k, chip in enumerate(chips):
                first.append(_remote(mine[t], place, send_sems, recv_sems, (t, 1 + k), (*chip, c)))
                land = T.block(outs[t], _device_index(chip, c))
                arrivals.append(_remote(land, land, send_sems, recv_sems, (t, 1 + k), sibling))
                passed.append(_remote(land, land, send_sems, recv_sems, (t, 4 + k), sibling))
                theirs = T.block(outs[t], _device_index(chip, 1 - c))
                late.append(_remote(theirs, theirs, send_sems, recv_sems, (t, 4 + k), sibling))
        return loads, stores, first, passed, arrivals, late

    def start(self, ins, outs, scratch):
        loads, stores, first, _, _, _ = self._copies(ins, outs, scratch)
        for cp in loads:
            cp.start()
        for cp in loads:
            cp.wait()
        for cp in stores + first:
            cp.start()

    def middle(self, ins, outs, scratch):
        _, _, _, passed, arrivals, _ = self._copies(ins, outs, scratch)
        for arrived, cp in zip(arrivals, passed):
            arrived.wait_recv()
            cp.start()

    def finish(self, ins, outs, scratch):
        _, stores, first, passed, _, late = self._copies(ins, outs, scratch)
        for cp in late:
            cp.wait_recv()
        for cp in first + passed:
            cp.wait_send()
        for cp in stores:
            cp.wait()


def _all_gather(blocks, tensors, name):
    gather = _Gather(tensors)

    def body(*refs):
        n = len(gather.tensors)
        ins, outs, scratch = refs[:n], refs[n:2 * n], refs[2 * n:]
        gather.start(ins, outs, scratch)
        gather.middle(ins, outs, scratch)
        gather.finish(ins, outs, scratch)

    return pl.pallas_call(
        body, name=name, in_specs=gather.in_specs, out_specs=gather.out_specs, out_shape=gather.out_shape,
        scratch_shapes=gather.scratch_shapes, compiler_params=pltpu.CompilerParams(vmem_limit_bytes=VMEM_LIMIT),
    )(*blocks)


def _pair_exchange(grads, tensors, name):
    nt = len(tensors)

    def body(*refs):
        ins, outs = refs[:nt], refs[nt:2 * nt]
        send_sems, recv_sems = refs[2 * nt:]
        x, y, c, _ = _place()
        cps = []
        for t, T in enumerate(tensors):
            for j in range(4):
                cp = _remote(T.block(ins[t], 2 * j + 1 - c), outs[t].at[j], send_sems, recv_sems, (t, j), (x, y, 1 - c))
                cp.start()
                cps.append(cp)
        for cp in cps:
            cp.wait()

    return pl.pallas_call(
        body, name=name,
        in_specs=[HBM_SPEC] * nt, out_specs=[HBM_SPEC] * nt,
        out_shape=[jax.ShapeDtypeStruct((4,) + T.block_shape, BF16) for T in tensors],
        scratch_shapes=[pltpu.SemaphoreType.DMA((nt, 4)), pltpu.SemaphoreType.DMA((nt, 4))],
    )(*grads)


def _pair_sum(T, g, lz1, where):
    blk = T.block_shape

    def body(where_ref, g_ref, l_ref, o_ref):
        o_ref[...] = (g_ref[...].astype(F32) + l_ref[...].astype(F32)).astype(BF16)

    return pl.pallas_call(
        body, name="grad_pair_sum_" + T.name,
        grid_spec=pltpu.PrefetchScalarGridSpec(
            num_scalar_prefetch=1, grid=(3,),
            in_specs=[pl.BlockSpec(blk, lambda k, w: T.block_index(2 * w[k] + w[3])),
                      pl.BlockSpec((None,) + blk, lambda k, w: (w[k], 0, 0))],
            out_specs=pl.BlockSpec((None,) + blk, lambda k, w: (k, 0, 0))),
        out_shape=jax.ShapeDtypeStruct((3,) + T.block_shape, BF16),
        compiler_params=_params("arbitrary"),
    )(where, g, lz1)


class _Scatter:
    middle = None

    def __init__(self, tensors):
        n = len(tensors)
        self.in_specs = [HBM_SPEC] * n
        self.out_specs = [HBM_SPEC] * n
        self.out_shape = [jax.ShapeDtypeStruct((3,) + T.block_shape, BF16) for T in tensors]
        self.scratch_shapes = [pltpu.SemaphoreType.DMA((n, 3)), pltpu.SemaphoreType.DMA((n, 3))]

    def _copies(self, ins, outs, scratch):
        send_sems, recv_sems = scratch
        x, y, c, chips = _place()
        return [_remote(ins[t].at[k], outs[t].at[k], send_sems, recv_sems, (t, k), (*chip, c))
                for t in range(len(ins)) for k, chip in enumerate(chips)]

    def start(self, ins, outs, scratch):
        for cp in self._copies(ins, outs, scratch):
            cp.start()

    def finish(self, ins, outs, scratch):
        for cp in self._copies(ins, outs, scratch):
            cp.wait()


def _chip_scatter(sums, tensors, name):
    scatter = _Scatter(tensors)
    n = len(tensors)

    def body(*refs):
        ins, outs, scratch = refs[:n], refs[n:2 * n], refs[2 * n:]
        scatter.start(ins, outs, scratch)
        scatter.finish(ins, outs, scratch)

    return pl.pallas_call(
        body, name=name, in_specs=scatter.in_specs, out_specs=scatter.out_specs, out_shape=scatter.out_shape,
        scratch_shapes=scatter.scratch_shapes,
    )(*sums)


def _adamw(w, g, m, v):
    m = ADAM_B1 * m + (1.0 - ADAM_B1) * g
    v = ADAM_B2 * v + (1.0 - ADAM_B2) * (g * g)
    m_hat = m / (1.0 - ADAM_B1 ** ADAM_STEP)
    v_hat = v / (1.0 - ADAM_B2 ** ADAM_STEP)
    delta = -ADAM_LR * (m_hat / (jnp.sqrt(v_hat) + ADAM_EPS) + ADAM_WD * w)
    return delta, m, v


def _final_sum(T, g, lz1, lz2, where, w, m, v):
    rows, cols = T.block_shape
    sub = 4 if T.axis == 0 and rows % 64 == 0 and rows > 256 else 1
    blk = (rows // sub, cols)

    def body(where_ref, g_ref, l1_ref, l2_ref, w_ref, m_ref, v_ref, g_out, d_out, m_out, v_out):
        tot = g_ref[...].astype(F32) + l1_ref[...].astype(F32)
        for k in range(3):
            tot = tot + l2_ref[k].astype(F32)
        g_out[...] = tot
        d_out[...], m_out[...], v_out[...] = _adamw(w_ref[...], tot, m_ref[...], v_ref[...])

    def in_whole(r, wh):
        p = wh[0]
        return (0, p) if T.axis == 1 else (p * sub + r, 0)

    own = pl.BlockSpec(blk, lambda r, wh: (r, 0))
    return pl.pallas_call(
        body, name="grad_final_" + T.name,
        grid_spec=pltpu.PrefetchScalarGridSpec(
            num_scalar_prefetch=1, grid=(sub,),
            in_specs=[pl.BlockSpec(blk, in_whole),
                      pl.BlockSpec((None,) + blk, lambda r, wh: (wh[1], r, 0)),
                      pl.BlockSpec((3,) + blk, lambda r, wh: (0, r, 0)), own, own, own],
            out_specs=[own] * 4),
        out_shape=[jax.ShapeDtypeStruct(T.block_shape, F32)] * 4,
        compiler_params=_params("arbitrary"),
    )(where, g, lz1, lz2, w, m, v)


MAT_PIECE = MAT_ROWS // 8
VEC_PIECE = DR // 8


class _AllReduce:
    def __init__(self, items):
        self.items = tuple(items)
        n = len(self.items)
        self.in_specs = [HBM_SPEC] * n
        self.out_specs = [HBM_SPEC] * n
        self.out_shape = [jax.ShapeDtypeStruct(shape, F32) for shape, _ in self.items]
        pieces = [(shape[0] // 8, shape[1]) if axis == 0 else (shape[0], shape[1] // 8) for shape, axis in self.items]
        self.scratch_shapes = ([pltpu.VMEM((8,) + p, F32) for p in pieces] + [pltpu.VMEM(p, F32) for p in pieces] + [
            pltpu.SemaphoreType.DMA((2 * n, 8)), pltpu.SemaphoreType.DMA((2 * n, 8)), pltpu.SemaphoreType.DMA((2 * n,))])

    def middle_at(self, steps):
        return steps // 2

    def _copies(self, ins, outs, scratch):
        n = len(self.items)
        landed, sums, (send_sems, recv_sems, loc_sems) = scratch[:n], scratch[n:2 * n], scratch[2 * n:]
        x, y, c, _ = _place()
        me = _device_index((x, y), c)

        def peer(r):
            return (1 - x if r & 4 else x, 1 - y if r & 2 else y, 1 - c if r & 1 else c)

        def piece(i, ref, p):
            shape, axis = self.items[i]
            if axis == 0:
                rows = shape[0] // 8
                return ref.at[pl.ds(pl.multiple_of(p * rows, 8), rows), :]
            cols = shape[1] // 8
            return ref.at[:, pl.ds(pl.multiple_of(p * cols, 128), cols)]

        own, scatter, arrivals, keep, spread, late = [], [], [], [], [], []
        for i in range(n):
            own.append(pltpu.make_async_copy(piece(i, ins[i], me), landed[i].at[0], loc_sems.at[2 * i]))
            keep.append(pltpu.make_async_copy(sums[i], piece(i, outs[i], me), loc_sems.at[2 * i + 1]))
            for r in range(1, 8):
                to = peer(r)
                p = _device_index(to[:2], to[2])
                scatter.append(_remote(piece(i, ins[i], p), landed[i].at[r], send_sems, recv_sems, (2 * i, r), to))
                spread.append(_remote(sums[i], piece(i, outs[i], me), send_sems, recv_sems, (2 * i + 1, r), to))
                late.append(_remote(sums[i], piece(i, outs[i], p), send_sems, recv_sems, (2 * i + 1, r), to))
        return own, scatter, keep, spread, late, landed, sums

    def start(self, ins, outs, scratch):
        own, scatter, _, _, _, _, _ = self._copies(ins, outs, scratch)
        for cp in own + scatter:
            cp.start()

    def middle(self, ins, outs, scratch):
        own, scatter, keep, spread, _, landed, sums = self._copies(ins, outs, scratch)
        for cp in own:
            cp.wait()
        for cp in scatter:
            cp.wait_recv()
        for i in range(len(self.items)):
            total = landed[i][0]
            for r in range(1, 8):
                total = total + landed[i][r]
            sums[i][...] = total
        for cp in keep + spread:
            cp.start()

    def finish(self, ins, outs, scratch):
        _, scatter, keep, spread, late, _, _ = self._copies(ins, outs, scratch)
        for cp in late:
            cp.wait_recv()
        for cp in scatter + spread:
            cp.wait_send()
        for cp in keep:
            cp.wait()


class _Both:
    def __init__(self, a, b):
        self.a, self.b = a, b
        self.in_specs, self.out_specs = a.in_specs + b.in_specs, a.out_specs + b.out_specs
        self.out_shape, self.scratch_shapes = a.out_shape + b.out_shape, a.scratch_shapes + b.scratch_shapes

    def middle_at(self, steps):
        return steps // 2

    def _each(self, ins, outs, scratch):
        a = self.a
        i, o, s = len(a.in_specs), len(a.out_specs), len(a.scratch_shapes)
        return (a, ins[:i], outs[:o], scratch[:s]), (self.b, ins[i:], outs[o:], scratch[s:])

    def start(self, ins, outs, scratch):
        for e, i, o, s in self._each(ins, outs, scratch):
            e.start(i, o, s)

    def middle(self, ins, outs, scratch):
        for e, i, o, s in self._each(ins, outs, scratch):
            if e.middle is not None:
                e.middle(i, o, s)

    def finish(self, ins, outs, scratch):
        for e, i, o, s in self._each(ins, outs, scratch):
            e.finish(i, o, s)


def _all_reduce(arrays, items, name):
    reduce = _AllReduce(items)
    n = len(items)

    def body(*refs):
        ins, outs, scratch = refs[:n], refs[n:2 * n], refs[2 * n:]
        reduce.start(ins, outs, scratch)
        reduce.middle(ins, outs, scratch)
        reduce.finish(ins, outs, scratch)

    return pl.pallas_call(
        body, name=name, in_specs=reduce.in_specs, out_specs=reduce.out_specs, out_shape=reduce.out_shape,
        scratch_shapes=reduce.scratch_shapes,
    )(*arrays)


def _adam_small(grads, wmv):
    n = len(grads)

    def body(*refs):
        g_refs, rest = refs[:n], refs[n:]
        ins, outs = rest[:3 * n], rest[3 * n:]
        for i in range(n):
            d, m, v = _adamw(ins[3 * i][...], g_refs[i][...], ins[3 * i + 1][...], ins[3 * i + 2][...])
            outs[3 * i][...], outs[3 * i + 1][...], outs[3 * i + 2][...] = d, m, v

    flat = [a for t in wmv for a in t]
    return pl.pallas_call(
        body, name="adam_small",
        in_specs=[VMEM_SPEC] * (4 * n), out_specs=[VMEM_SPEC] * (3 * n),
        out_shape=[jax.ShapeDtypeStruct(a.shape, F32) for a in flat],
    )(*grads, *flat)


WEIGHT_NAMES = ("norm_mix", "w_in", "w_pool_grp", "pool_scale", "w_pool_out", "conv_w", "conv_b", "w_rg_a", "b_rg_a", "w_rg_x",
                "b_rg_x", "lru_lambda", "w_rnn_out", "w_o", "norm_ffn", "w_ffn_in", "w_ffn_out", "norm_final")


def kernel(x, norm_mix, w_in, w_pool_grp, pool_scale, w_pool_out, conv_w, conv_b, w_rg_a, b_rg_a, w_rg_x, b_rg_x, lru_lambda, w_rnn_out, w_o, norm_ffn, w_ffn_in, w_ffn_out, norm_final, loss_target, m_norm_mix, m_w_in, m_w_pool_grp, m_pool_scale, m_w_pool_out, m_conv_w, m_conv_b, m_w_rg_a, m_b_rg_a, m_w_rg_x, m_b_rg_x, m_lru_lambda, m_w_rnn_out, m_w_o, m_norm_ffn, m_w_ffn_in, m_w_ffn_out, m_norm_final, v_norm_mix, v_w_in, v_w_pool_grp, v_pool_scale, v_w_pool_out, v_conv_w, v_conv_b, v_w_rg_a, v_b_rg_a, v_w_rg_x, v_b_rg_x, v_lru_lambda, v_w_rnn_out, v_w_o, v_norm_ffn, v_w_ffn_in, v_w_ffn_out, v_norm_final):
    w = dict(norm_mix=norm_mix, w_in=w_in, w_pool_grp=w_pool_grp, pool_scale=pool_scale, w_pool_out=w_pool_out, conv_w=conv_w,
             conv_b=conv_b, w_rg_a=w_rg_a, b_rg_a=b_rg_a, w_rg_x=w_rg_x, b_rg_x=b_rg_x, lru_lambda=lru_lambda,
             w_rnn_out=w_rnn_out, w_o=w_o, norm_ffn=norm_ffn, w_ffn_in=w_ffn_in, w_ffn_out=w_ffn_out, norm_final=norm_final)
    m = dict(norm_mix=m_norm_mix, w_in=m_w_in, w_pool_grp=m_w_pool_grp, pool_scale=m_pool_scale, w_pool_out=m_w_pool_out,
             conv_w=m_conv_w, conv_b=m_conv_b, w_rg_a=m_w_rg_a, b_rg_a=m_b_rg_a, w_rg_x=m_w_rg_x, b_rg_x=m_b_rg_x,
             lru_lambda=m_lru_lambda, w_rnn_out=m_w_rnn_out, w_o=m_w_o, norm_ffn=m_norm_ffn, w_ffn_in=m_w_ffn_in,
             w_ffn_out=m_w_ffn_out, norm_final=m_norm_final)
    v = dict(norm_mix=v_norm_mix, w_in=v_w_in, w_pool_grp=v_w_pool_grp, pool_scale=v_pool_scale, w_pool_out=v_w_pool_out,
             conv_w=v_conv_w, conv_b=v_conv_b, w_rg_a=v_w_rg_a, b_rg_a=v_b_rg_a, w_rg_x=v_w_rg_x, b_rg_x=v_b_rg_x,
             lru_lambda=v_lru_lambda, w_rnn_out=v_w_rnn_out, w_o=v_w_o, norm_ffn=v_norm_ffn, w_ffn_in=v_w_ffn_in,
             w_ffn_out=v_w_ffn_out, norm_final=v_norm_final)
    xi, yi, ci = (lax.axis_index(a) for a in MESH_AXES)
    chip = 2 * xi + yi
    other = [2 * (1 - xi) + yi, 2 * xi + (1 - yi), 2 * (1 - xi) + (1 - yi)]

    def held(T, a):
        return jnp.swapaxes(a, 0, 1) if T.transposed else a

    where4 = jnp.stack(other + [ci]).astype(jnp.int32)
    where2 = jnp.stack([2 * chip + ci, chip]).astype(jnp.int32)
    by_name = {T.name: T for T in GATHERED}
    block = {T.name: held(T, w[T.name][0]).astype(T.dtype) for T in BIG}
    block["conv_w"] = jnp.pad(conv_w[0], ((0, CONV_W.rows - 4), (0, 0)))

    def gather_of(*names):
        return dict(exchange=_Gather([by_name[n] for n in names]), exchange_operands=[block[n] for n in names])

    def pair_sums(names, partials, tag):
        ts = [by_name[n] for n in names]
        landed = _pair_exchange(partials, ts, "grad_pair_exchange_" + tag)
        return landed, [_pair_sum(T, g, l, where4) for T, g, l in zip(ts, partials, landed)]

    xs, target = x[0], loss_target[0]
    wg_b, wa_b, wx_b = (a[0].astype(BF16) for a in (w_pool_grp, w_rg_a, w_rg_x))
    ba2, bx2 = b_rg_a.reshape(1, DR), b_rg_x.reshape(1, DR)
    (w_in_g,) = _all_gather([block["w_in"]], [by_name["w_in"]], "all_gather_w_in")
    (proj, h1), (w_pool_out_g, w_rnn_out_g, w_o_g, conv_g) = _in_proj(
        xs, norm_mix, w_in_g, **gather_of("w_pool_out", "w_rnn_out", "w_o", "conv_w"))
    mixer_weights = (wg_b, pool_scale, w_pool_out_g, conv_g[0:4], conv_b, wa_b, ba2, wx_b, bx2, lru_lambda, w_rnn_out_g)
    (_, pm, y_pool, hr, z, y_rnn), (w_ffn_in_g,) = _mixer_fwd(proj, *mixer_weights, **gather_of("w_ffn_in"))
    (mix, x2, h2), _ = _merge_out(xs, proj, y_pool, y_rnn, w_o_g, norm_ffn)
    (gu, act), (w_ffn_out_g,) = _ffn_up(h2, w_ffn_in_g, **gather_of("w_ffn_out"))
    dx3, dx3b, loss_part, dvec_fin = _ffn_down_loss(act, x2, target, w_ffn_out_g, norm_final.reshape(1, D))

    dgu = _ffn_bwd_down(dx3b, gu, w_ffn_out_g)
    dx2, dx2b, dmixo, dvec_ffn = _ffn_bwd_up(dgu, x2, dx3, w_ffn_in_g, norm_ffn, w_o_g)
    names_a = ("w_ffn_in", "w_ffn_out", "w_o")
    part_a = [_wgrad(dgu, h2, "wgrad_ffn_in", 1408, 512), _wgrad(act, dx3b, "wgrad_ffn_out", 1408, 512),
              _wgrad(mix, dx2b, "wgrad_o", 1024, 1024)]
    lz1_a, sums_a = pair_sums(names_a, part_a, "ffn")
    (dproj, dypb, dyrb, dmat, dvec_mix), lz2_a = _mixer_bwd(
        proj, dmixo, y_pool, y_rnn, hr, *mixer_weights,
        exchange=_Scatter([by_name[n] for n in names_a]), exchange_operands=sums_a)
    names_b = ("w_pool_out", "w_rnn_out")
    part_b = [_wgrad(pm, dypb, "wgrad_pool_out", 512, 1024), _wgrad(z, dyrb, "wgrad_rnn_out", 1024, 1024)]
    lz1_b, sums_b = pair_sums(names_b, part_b, "mix")
    dvec = jnp.concatenate([dvec_mix[0:9], dvec_fin[0:1], dvec_ffn[0:1], jnp.pad(loss_part, ((0, 0), (0, DR - 1))),
                            jnp.zeros((VEC_ROWS - 12, DR), F32)], axis=0)
    g_in, exchanged = _wgrad(
        dproj, h1, "wgrad_in", 1152, 1024,
        exchange=_Both(_Scatter([by_name[n] for n in names_b]), _AllReduce([((MAT_ROWS, HD), 0), ((VEC_ROWS, DR), 1)])),
        exchange_operands=sums_b + [dmat, dvec])
    lz2_b, (mat, vec) = exchanged[:2], exchanged[2:]
    loss = vec[VEC_LOSS, 0]
    lz1_c, sums_c = pair_sums(("w_in",), [g_in], "in")
    (grad_x, dvec_in), lz2_c = _in_bwd(dproj, xs, dx2, norm_mix, w_in_g,
                                       exchange=_Scatter([by_name["w_in"]]), exchange_operands=sums_c)
    (vec_in,) = _all_reduce([dvec_in], [((8, D), 1)], "all_reduce_norm_mix")

    grads, delta, new_m, new_v = {}, {}, {}, {}
    for n, g, l1, l2 in zip(names_a + names_b + ("w_in",), part_a + part_b + [g_in], lz1_a + lz1_b + lz1_c,
                            lz2_a + lz2_b + lz2_c):
        T = by_name[n]
        out = _final_sum(T, g, l1, l2, where2, held(T, w[n][0]), held(T, m[n][0]), held(T, v[n][0]))
        grads[n], delta[n], new_m[n], new_v[n] = (held(T, a) for a in out)
    me = 4 * xi + 2 * yi + ci
    small_grads = dict(
        w_pool_grp=mat[0:MAT_WA], w_rg_a=mat[MAT_WA:MAT_WX], w_rg_x=mat[MAT_WX:MAT_ROWS],
        pool_scale=vec[VEC_SCALE:VEC_SCALE + 1, 0:DP], conv_b=vec[VEC_CONV_B:VEC_CONV_B + 1],
        b_rg_a=vec[VEC_BA:VEC_BA + 1], b_rg_x=vec[VEC_BX:VEC_BX + 1], lru_lambda=vec[VEC_LAM:VEC_LAM + 1],
        conv_w=lax.dynamic_slice(vec, (VEC_CONV_W, VEC_PIECE * me), (4, VEC_PIECE)),
        norm_final=vec[VEC_NORM_FINAL:VEC_NORM_FINAL + 1], norm_ffn=vec[VEC_NORM_FFN:VEC_NORM_FFN + 1],
        norm_mix=vec_in[0:1])
    names = list(small_grads)
    as2d = lambda a, g: a.reshape(g.shape)
    upd = _adam_small([small_grads[n] for n in names],
                      [(as2d(w[n], small_grads[n]), as2d(m[n], small_grads[n]), as2d(v[n], small_grads[n])) for n in names])
    for i, n in enumerate(names):
        grads[n] = small_grads[n]
        delta[n], new_m[n], new_v[n] = upd[3 * i:3 * i + 3]

    shaped = lambda d: [d[n].reshape(w[n].shape) for n in WEIGHT_NAMES]
    return (loss, grad_x[None], *shaped(grads), *shaped(delta), *shaped(new_m), *shaped(new_v))
```

```python
import functools
import math

import jax
import jax.numpy as jnp
from jax import lax
from jax.experimental import pallas as pl
from jax.experimental.pallas import tpu as pltpu

F32 = jnp.float32
BF16 = jnp.bfloat16

D = 1024
DP = 512
PG = 128
WINDOWS = (2, 4, 8, 16)
DR = 1024
NH = 8
HD = 128
DIN = 4608
DFF = 2816
EPS = 1e-6
LRU_C = 8.0
POOL_HALO = 16
CONV_HALO = 8

ADAM_LR = 0.001
ADAM_B1 = 0.9
ADAM_B2 = 0.999
ADAM_EPS = 1e-08
ADAM_WD = 0.01
ADAM_STEP = 10

VMEM_LIMIT = 56 * 1024 * 1024
MESH_AXES = ("x", "y", "c")
MESH = pl.DeviceIdType.MESH


def _dot(a, b):
    return jnp.dot(a, b, preferred_element_type=F32)


def _dot_nt(a, b):
    return lax.dot_general(a, b, (((1,), (1,)), ((), ())), preferred_element_type=F32)


def _dot_tn(a, b):
    return lax.dot_general(a, b, (((0,), (0,)), ((), ())), preferred_element_type=F32)


def _params(*sem):
    return pltpu.CompilerParams(dimension_semantics=sem, vmem_limit_bytes=VMEM_LIMIT)


def _resident(shape):
    nd = len(shape)
    return pl.BlockSpec(shape, lambda i: (0,) * nd, pipeline_mode=pl.Buffered(1))


def _rows(shape_cols, tm):
    return pl.BlockSpec((tm, shape_cols), lambda i: (i, 0))


def _call(body, name, grid, in_specs, out_specs, out_shape, operands, scratch_shapes=(), exchange=None, exchange_operands=()):
    n_in, n_out, n_scr = len(in_specs), len(out_specs), len(scratch_shapes)
    steps = math.prod(grid)
    if exchange is None:
        outs = pl.pallas_call(body, name=name, grid=grid, in_specs=in_specs, out_specs=out_specs, out_shape=out_shape,
                              scratch_shapes=list(scratch_shapes), compiler_params=_params(*["arbitrary"] * len(grid)))(*operands)
        return outs, []
    e_in, e_out = len(exchange.in_specs), len(exchange.out_specs)

    def hosted(*refs):
        ins, refs = refs[:n_in], refs[n_in:]
        e_ins, refs = refs[:e_in], refs[e_in:]
        outs, refs = refs[:n_out], refs[n_out:]
        e_outs, refs = refs[:e_out], refs[e_out:]
        scr, e_scr = refs[:n_scr], refs[n_scr:]
        step = pl.program_id(0)
        for axis in range(1, len(grid)):
            step = step * grid[axis] + pl.program_id(axis)
        pl.when(step == 0)(lambda: exchange.start(e_ins, e_outs, e_scr))
        if exchange.middle is not None:
            pl.when(step == exchange.middle_at(steps))(lambda: exchange.middle(e_ins, e_outs, e_scr))
        body(*ins, *outs, *scr)
        pl.when(step == steps - 1)(lambda: exchange.finish(e_ins, e_outs, e_scr))

    outs = pl.pallas_call(
        hosted, name=name, grid=grid, in_specs=list(in_specs) + exchange.in_specs,
        out_specs=list(out_specs) + exchange.out_specs, out_shape=list(out_shape) + exchange.out_shape,
        scratch_shapes=list(scratch_shapes) + exchange.scratch_shapes,
        compiler_params=_params(*["arbitrary"] * len(grid)))(*operands, *exchange_operands)
    return outs[:n_out], outs[n_out:]


GELU_C = math.sqrt(2.0 / math.pi)
GELU_K = 0.044715 * GELU_C


def _gelu(x, with_grad=False):
    x2 = x * x
    t = jnp.tanh(x * (GELU_C + GELU_K * x2))
    hx = 0.5 * x
    y = hx + hx * t
    if not with_grad:
        return y
    return y, 0.5 + 0.5 * t + hx * (1.0 - t * t) * (GELU_C + (3.0 * GELU_K) * x2)


def _softplus_neg(lam):
    z = jnp.exp(-jnp.abs(lam))
    u = 1.0 + z
    dlt = u - 1.0
    log1p = jnp.where(dlt == 0.0, z, jnp.log(u) * (z / jnp.where(dlt == 0.0, 1.0, dlt)))
    return jnp.maximum(-lam, 0.0) + log1p


def _sigmoid(x):
    return 0.5 * jnp.tanh(0.5 * x) + 0.5


def _linear_scan(out_ref, A, B, h0, reverse):
    n = A.shape[0]
    sub = lax.broadcasted_iota(jnp.int32, (8, 1), 0)
    tiles = range(n // 8 - 1, -1, -1) if reverse else range(n // 8)
    carry = h0
    for j in tiles:
        a, b = A[8 * j:8 * j + 8, :], B[8 * j:8 * j + 8, :]
        for d in (1, 2, 4):
            keep = (sub < 8 - d) if reverse else (sub >= d)
            shift = 8 - d if reverse else d
            b = jnp.where(keep, a * pltpu.roll(b, shift, axis=0) + b, b)
            a = jnp.where(keep, a * pltpu.roll(a, shift, axis=0), a)
        h = a * carry + b
        out_ref[8 * j:8 * j + 8, :] = h
        carry = h[0:1, :] if reverse else h[7:8, :]
    return carry


def _pool_windows(ext, shift_sign):
    n = ext.shape[0]
    s = ext
    outs = []
    for w in WINDOWS:
        d = w // 2
        s = s + pltpu.roll(s, d if shift_sign > 0 else n - d, axis=0)
        outs.append(s[:, :PG])
        s = s[:, PG:]
    return outs


def _conv_taps(uext):
    taps = []
    for k in range(4):
        sh = 3 - k
        v = uext if sh == 0 else pltpu.roll(uext, sh, axis=0)
        taps.append(v[CONV_HALO:, :])
    return taps


def _gates(v, wa_ref, ba_ref, wx_ref, bx_ref, sp):
    vb = v.astype(BF16)
    ra, rx = [], []
    for h in range(NH):
        vh = vb[:, h * HD:(h + 1) * HD]
        ra.append(_dot(vh, wa_ref[h]))
        rx.append(_dot(vh, wx_ref[h]))
    r = _sigmoid(jnp.concatenate(ra, axis=1) + ba_ref[...])
    i = _sigmoid(jnp.concatenate(rx, axis=1) + bx_ref[...])
    log_a = r * ((-LRU_C) * sp)
    a = jnp.exp(log_a)
    one_minus = -jnp.tanh(log_a) * (1.0 + a * a)
    return r, i, a, jnp.sqrt(one_minus), lax.rsqrt(one_minus)


def _in_proj(x, norm_mix, w_in, exchange=None, exchange_operands=(), tm=512):
    S = x.shape[0]

    def body(x_ref, g_ref, w_ref, proj_ref, h_ref):
        xv = x_ref[...]
        r = lax.rsqrt(jnp.mean(xv * xv, axis=-1, keepdims=True) + EPS)
        h = (xv * r * g_ref[...]).astype(BF16)
        h_ref[...] = h
        for n0 in range(0, DIN, 512):
            proj_ref[:, n0:n0 + 512] = _dot_nt(h, w_ref[n0:n0 + 512, :])

    return _call(
        body, "in_proj", (S // tm,),
        in_specs=[_rows(D, tm), _resident((1, D)), _resident((DIN, D))],
        out_specs=[_rows(DIN, tm), _rows(D, tm)],
        out_shape=[jax.ShapeDtypeStruct((S, DIN), F32), jax.ShapeDtypeStruct((S, D), BF16)],
        operands=(x, norm_mix, w_in), exchange=exchange, exchange_operands=exchange_operands)


def _mixer_fwd(proj, wg, scale, w_pool_out, conv_w, conv_b, wa, ba, wx, bx, lam, w_rnn_out, exchange=None,
               exchange_operands=(), tm=256):
    S = proj.shape[0]
    UW = DP + 2 * DR

    def body(proj_ref, wg_ref, scale_ref, wpo_ref, cw_ref, cb_ref, wa_ref, ba_ref, wx_ref, bx_ref, lam_ref, wro_ref,
             pooled_ref, pm_ref, ypool_ref, hr_ref, z_ref, yrnn_ref, pool_carry, conv_carry, h_carry):
        i = pl.program_id(0)

        @pl.when(i == 0)
        def _():
            pool_carry[...] = jnp.zeros_like(pool_carry)
            conv_carry[...] = jnp.zeros_like(conv_carry)
            h_carry[...] = jnp.zeros_like(h_carry)

        rows = lax.broadcasted_iota(jnp.int32, (tm, 1), 0)
        t_glob = i * tm + rows

        u_pool = proj_ref[:, 0:DP]
        ext = jnp.concatenate([pool_carry[...], u_pool], axis=0)
        pool_carry[...] = u_pool[tm - POOL_HALO:, :]
        sums = _pool_windows(ext, +1)
        mixed = []
        for g, w in enumerate(WINDOWS):
            inv_cnt = 1.0 / jnp.minimum(t_glob + 1, w).astype(F32)
            pooled_g = sums[g][POOL_HALO:, :] * inv_cnt - u_pool[:, g * PG:(g + 1) * PG]
            pooled_b = pooled_g.astype(BF16)
            pooled_ref[:, g * PG:(g + 1) * PG] = pooled_b
            mixed.append(_dot(pooled_b, wg_ref[g]))
        pm = (jnp.concatenate(mixed, axis=1) * scale_ref[...]).astype(BF16)
        pm_ref[...] = pm
        ypool_ref[...] = _dot(pm, wpo_ref[...])

        u_rnn = proj_ref[:, DP:DP + DR]
        uext = jnp.concatenate([conv_carry[...], u_rnn], axis=0)
        conv_carry[...] = u_rnn[tm - CONV_HALO:, :]
        taps = _conv_taps(uext)
        v = cb_ref[...]
        for k in range(4):
            v = v + taps[k] * cw_ref[k:k + 1, :]
        sp = _softplus_neg(lam_ref[...])
        _, gi, a, mult, _ = _gates(v, wa_ref, ba_ref, wx_ref, bx_ref, sp)
        h_carry[0:1, :] = _linear_scan(hr_ref, a, mult * gi * v, h_carry[0:1, :], reverse=False)
        z = (hr_ref[...] * _gelu(proj_ref[:, DP + DR:UW])).astype(BF16)
        z_ref[...] = z
        yrnn_ref[...] = _dot(z, wro_ref[...])

    return _call(
        body, "mixer_fwd", (S // tm,),
        in_specs=[_rows(UW, tm), _resident((4, PG, PG)), _resident((1, DP)), _resident((DP, D)), _resident((4, DR)),
                  _resident((1, DR)), _resident((NH, HD, HD)), _resident((1, DR)), _resident((NH, HD, HD)),
                  _resident((1, DR)), _resident((1, DR)), _resident((DR, D))],
        out_specs=[_rows(DP, tm), _rows(DP, tm), _rows(D, tm), _rows(DR, tm), _rows(DR, tm), _rows(D, tm)],
        out_shape=[jax.ShapeDtypeStruct((S, DP), BF16), jax.ShapeDtypeStruct((S, DP), BF16),
                   jax.ShapeDtypeStruct((S, D), F32), jax.ShapeDtypeStruct((S, DR), F32),
                   jax.ShapeDtypeStruct((S, DR), BF16), jax.ShapeDtypeStruct((S, D), F32)],
        scratch_shapes=[pltpu.VMEM((POOL_HALO, DP), F32), pltpu.VMEM((CONV_HALO, DR), F32), pltpu.VMEM((8, DR), F32)],
        operands=(proj, wg, scale, w_pool_out, conv_w, conv_b, wa, ba, wx, bx, lam, w_rnn_out),
        exchange=exchange, exchange_operands=exchange_operands)


FF_CHUNK = DFF // 4


def _rms(x):
    r = lax.rsqrt(jnp.mean(x * x, axis=-1, keepdims=True) + EPS)
    return r, x * r


def _rms_bwd(dh, g, r, xh):
    dxh = dh * g
    return r * (dxh - xh * jnp.mean(dxh * xh, axis=-1, keepdims=True))


def _merge_out(x, proj, y_pool, y_rnn, w_o, norm_ffn, exchange=None, exchange_operands=(), tm=512):
    S = x.shape[0]
    GL0 = (DP + 2 * DR) // 512

    def gl_spec(k):
        return pl.BlockSpec((tm, 512), lambda i: (i, GL0 + k))

    def body(x_ref, gl0, gl1, gl2, gl3, yp_ref, yr_ref, wo_ref, gf_ref, mix_ref, x2_ref, h2_ref):
        s_p = _sigmoid(jnp.concatenate([gl0[...], gl1[...]], axis=1))
        s_r = _sigmoid(jnp.concatenate([gl2[...], gl3[...]], axis=1))
        mix = (s_p * yp_ref[...] + s_r * yr_ref[...]).astype(BF16)
        mix_ref[...] = mix
        x2 = x_ref[...] + _dot(mix, wo_ref[...])
        x2_ref[...] = x2
        _, xh2 = _rms(x2)
        h2_ref[...] = (xh2 * gf_ref[...]).astype(BF16)

    return _call(
        body, "merge_out", (S // tm,),
        in_specs=[_rows(D, tm), gl_spec(0), gl_spec(1), gl_spec(2), gl_spec(3), _rows(D, tm), _rows(D, tm),
                  _resident((D, D)), _resident((1, D))],
        out_specs=[_rows(D, tm), _rows(D, tm), _rows(D, tm)],
        out_shape=[jax.ShapeDtypeStruct((S, D), BF16), jax.ShapeDtypeStruct((S, D), F32), jax.ShapeDtypeStruct((S, D), BF16)],
        operands=(x, proj, proj, proj, proj, y_pool, y_rnn, w_o, norm_ffn),
        exchange=exchange, exchange_operands=exchange_operands)


def _ffn_up(h2, w_lo, w_hi, exchange=None, exchange_operands=(), tm=512):
    S = h2.shape[0]
    HALF = D // 2

    def body(h_ref, lo_ref, hi_ref, gu_ref, act_ref):
        h_lo, h_hi = h_ref[:, 0:HALF], h_ref[:, HALF:D]
        for c0 in range(0, DFF, FF_CHUNK):
            gate = _dot_nt(h_lo, lo_ref[c0:c0 + FF_CHUNK, :]) + _dot_nt(h_hi, hi_ref[c0:c0 + FF_CHUNK, :])
            up = (_dot_nt(h_lo, lo_ref[DFF + c0:DFF + c0 + FF_CHUNK, :])
                  + _dot_nt(h_hi, hi_ref[DFF + c0:DFF + c0 + FF_CHUNK, :]))
            gu_ref[:, c0:c0 + FF_CHUNK] = gate.astype(BF16)
            gu_ref[:, DFF + c0:DFF + c0 + FF_CHUNK] = up.astype(BF16)
            act_ref[:, c0:c0 + FF_CHUNK] = (gate * _sigmoid(gate) * up).astype(BF16)

    return _call(
        body, "ffn_up", (S // tm,),
        in_specs=[_rows(D, tm), _resident((2 * DFF, HALF)), _resident((2 * DFF, HALF))],
        out_specs=[_rows(2 * DFF, tm), _rows(DFF, tm)],
        out_shape=[jax.ShapeDtypeStruct((S, 2 * DFF), BF16), jax.ShapeDtypeStruct((S, DFF), BF16)],
        operands=(h2, w_lo, w_hi), exchange=exchange, exchange_operands=exchange_operands)


def _ffn_down_loss(act, x2, target, w_ffn_out, norm_final, tm=512):
    S = act.shape[0]

    def body(act_ref, x2_ref, t_ref, w_ref, gn_ref, dx3_ref, dx3b_ref, loss_ref, dvec_ref):
        i = pl.program_id(0)

        @pl.when(i == 0)
        def _():
            loss_ref[...] = jnp.zeros_like(loss_ref)
            dvec_ref[...] = jnp.zeros_like(dvec_ref)

        x3 = x2_ref[...] + _dot(act_ref[...], w_ref[...])
        r3, xh3 = _rms(x3)
        g_fin = gn_ref[...]
        e = xh3 * g_fin - t_ref[...]
        loss_ref[...] += jnp.sum(e * e, axis=(0, 1), keepdims=True) * (0.5 / D)
        dy = e * (1.0 / D)
        dvec_ref[0:1, :] += jnp.sum(dy * xh3, axis=0, keepdims=True)
        dx3 = _rms_bwd(dy, g_fin, r3, xh3)
        dx3_ref[...] = dx3
        dx3b_ref[...] = dx3.astype(BF16)

    return pl.pallas_call(
        body, name="ffn_down_loss", grid=(S // tm,),
        in_specs=[_rows(DFF, tm), _rows(D, tm), _rows(D, tm), _resident((DFF, D)), _resident((1, D))],
        out_specs=[_rows(D, tm), _rows(D, tm), _resident((1, 1)), _resident((8, D))],
        out_shape=[jax.ShapeDtypeStruct((S, D), F32), jax.ShapeDtypeStruct((S, D), BF16),
                   jax.ShapeDtypeStruct((1, 1), F32), jax.ShapeDtypeStruct((8, D), F32)],
        compiler_params=_params("arbitrary"),
    )(act, x2, target, w_ffn_out, norm_final)


def _ffn_bwd_down(dx3b, gu, w_ffn_out, tm=512):
    S = dx3b.shape[0]

    def body(d_ref, gu_ref, w_ref, dgu_ref):
        d = d_ref[...]
        for c0 in range(0, DFF, FF_CHUNK):
            dact = _dot_nt(d, w_ref[c0:c0 + FF_CHUNK, :])
            gate = gu_ref[:, c0:c0 + FF_CHUNK].astype(F32)
            up = gu_ref[:, DFF + c0:DFF + c0 + FF_CHUNK].astype(F32)
            sg = _sigmoid(gate)
            dgu_ref[:, c0:c0 + FF_CHUNK] = (dact * up * (sg * (1.0 + gate * (1.0 - sg)))).astype(BF16)
            dgu_ref[:, DFF + c0:DFF + c0 + FF_CHUNK] = (dact * (gate * sg)).astype(BF16)

    return pl.pallas_call(
        body, name="ffn_bwd_down", grid=(S // tm,),
        in_specs=[_rows(D, tm), _rows(2 * DFF, tm), _resident((DFF, D))],
        out_specs=_rows(2 * DFF, tm),
        out_shape=jax.ShapeDtypeStruct((S, 2 * DFF), BF16),
        compiler_params=_params("parallel"),
    )(dx3b, gu, w_ffn_out)


def _ffn_bwd_up(dgu, x2, dx3, w_lo, w_hi, norm_ffn, w_o, tm=512):
    S = dgu.shape[0]
    HALF = D // 2

    def body(dgu_ref, x2_ref, dx3_ref, lo_ref, hi_ref, gf_ref, wo_ref, dx2_ref, dx2b_ref, dmixo_ref, dvec_ref):
        i = pl.program_id(0)

        @pl.when(i == 0)
        def _():
            dvec_ref[...] = jnp.zeros_like(dvec_ref)

        dgate, dup = dgu_ref[:, 0:DFF], dgu_ref[:, DFF:2 * DFF]
        dh2 = jnp.concatenate([_dot(dgate, w[0:DFF, :]) + _dot(dup, w[DFF:2 * DFF, :]) for w in (lo_ref, hi_ref)], axis=1)
        r2, xh2 = _rms(x2_ref[...])
        dvec_ref[0:1, :] += jnp.sum(dh2 * xh2, axis=0, keepdims=True)
        dx2 = dx3_ref[...] + _rms_bwd(dh2, gf_ref[...], r2, xh2)
        dx2_ref[...] = dx2
        dx2b = dx2.astype(BF16)
        dx2b_ref[...] = dx2b
        dmixo_ref[...] = _dot_nt(dx2b, wo_ref[...])

    return pl.pallas_call(
        body, name="ffn_bwd_up", grid=(S // tm,),
        in_specs=[_rows(2 * DFF, tm), _rows(D, tm), _rows(D, tm), _resident((2 * DFF, HALF)), _resident((2 * DFF, HALF)),
                  _resident((1, D)), _resident((D, D))],
        out_specs=[_rows(D, tm), _rows(D, tm), _rows(D, tm), _resident((8, D))],
        out_shape=[jax.ShapeDtypeStruct((S, D), F32), jax.ShapeDtypeStruct((S, D), BF16), jax.ShapeDtypeStruct((S, D), F32),
                   jax.ShapeDtypeStruct((8, D), F32)],
        compiler_params=_params("arbitrary"),
    )(dgu, x2, dx3, w_lo, w_hi, norm_ffn, w_o)


VEC_ROWS = 16
MAT_WA = 4 * PG
MAT_WX = MAT_WA + NH * HD
MAT_ROWS = MAT_WX + NH * HD


def _mixer_bwd(proj, dmixo, y_pool, y_rnn, hr, wg, scale, w_pool_out, conv_w, conv_b, wa, ba, wx, bx, lam, w_rnn_out,
               exchange=None, exchange_operands=(), tm=256):
    S = proj.shape[0]
    nt = S // tm

    def rev(cols):
        return pl.BlockSpec((tm, cols), lambda i: (nt - 1 - i, 0))

    def halo(rows_, cols):
        per = tm // rows_
        return pl.BlockSpec((rows_, cols), lambda i: (jnp.maximum((nt - 1 - i) * per - 1, 0), 0))

    def body(proj_ref, projh_ref, dmixo_ref, yp_ref, yr_ref, hr_ref, hrh_ref, wg_ref, scale_ref, wpo_ref, cw_ref, cb_ref,
             wa_ref, ba_ref, wx_ref, bx_ref, lam_ref, wro_ref,
             dproj_ref, dypb_ref, dyrb_ref, dmat_ref, dvec_ref,
             q_carry, dv_carry, a_carry, g_carry, g_scr):
        i = pl.program_id(0)
        ti = nt - 1 - i

        @pl.when(i == 0)
        def _():
            q_carry[...] = jnp.zeros_like(q_carry)
            dv_carry[...] = jnp.zeros_like(dv_carry)
            a_carry[...] = jnp.zeros_like(a_carry)
            g_carry[...] = jnp.zeros_like(g_carry)
            dmat_ref[...] = jnp.zeros_like(dmat_ref)
            dvec_ref[...] = jnp.zeros_like(dvec_ref)

        rows = lax.broadcasted_iota(jnp.int32, (tm, 1), 0)
        t_glob = ti * tm + rows
        has_prev = (ti > 0).astype(F32)
        dmixo = dmixo_ref[...]

        s_p = _sigmoid(proj_ref[:, DP + 2 * DR:DP + 2 * DR + D])
        s_r = _sigmoid(proj_ref[:, DP + 2 * DR + D:DIN])
        dproj_ref[:, DP + 2 * DR:DP + 2 * DR + D] = (dmixo * yp_ref[...] * s_p * (1.0 - s_p)).astype(BF16)
        dproj_ref[:, DP + 2 * DR + D:DIN] = (dmixo * yr_ref[...] * s_r * (1.0 - s_r)).astype(BF16)
        dyp = (dmixo * s_p).astype(BF16)
        dyr = (dmixo * s_r).astype(BF16)
        dypb_ref[...] = dyp
        dyrb_ref[...] = dyr

        dz = _dot_nt(dyr, wro_ref[...])
        u_gate = proj_ref[:, DP + DR:DP + 2 * DR]
        gg, dgelu = _gelu(u_gate, with_grad=True)
        hr_t = hr_ref[...]
        dproj_ref[:, DP + DR:DP + 2 * DR] = (dz * hr_t * dgelu).astype(BF16)
        dhr = dz * gg

        u_rnn = proj_ref[:, DP:DP + DR]
        uext = jnp.concatenate([projh_ref[POOL_HALO - CONV_HALO:, DP:DP + DR] * has_prev, u_rnn], axis=0)
        taps = _conv_taps(uext)
        v = cb_ref[...]
        for k in range(4):
            v = v + taps[k] * cw_ref[k:k + 1, :]
        sp = _softplus_neg(lam_ref[...])
        r, gi, a, mult, inv_mult = _gates(v, wa_ref, ba_ref, wx_ref, bx_ref, sp)

        C = jnp.where(rows == tm - 1, a_carry[0:1, :], pltpu.roll(a, tm - 1, axis=0))
        g_carry[0:1, :] = _linear_scan(g_scr, C, dhr, g_carry[0:1, :], reverse=True)
        a_carry[0:1, :] = a[0:1, :]
        g = g_scr[...]

        h_prev = jnp.where(rows == 0, hrh_ref[7:8, :] * has_prev, pltpu.roll(hr_t, 1, axis=0))
        da = g * h_prev
        gm = g * mult
        dmult = g * gi * v
        di = gm * v
        dv = gm * gi
        dlog_a = da * a - dmult * (a * a * inv_mult)
        dvec_ref[4:5, :] += jnp.sum(dlog_a * r, axis=0, keepdims=True)
        dra = (dlog_a * ((-LRU_C) * sp) * r * (1.0 - r))
        drx = di * gi * (1.0 - gi)
        dvec_ref[2:3, :] += jnp.sum(dra, axis=0, keepdims=True)
        dvec_ref[3:4, :] += jnp.sum(drx, axis=0, keepdims=True)
        drab = dra.astype(BF16)
        drxb = drx.astype(BF16)
        vb = v.astype(BF16)
        dvg = []
        for h in range(NH):
            sl = slice(h * HD, (h + 1) * HD)
            dvg.append(_dot_nt(drab[:, sl], wa_ref[h]) + _dot_nt(drxb[:, sl], wx_ref[h]))
            dmat_ref[MAT_WA + h * HD:MAT_WA + (h + 1) * HD, :] += _dot_tn(vb[:, sl], drab[:, sl])
            dmat_ref[MAT_WX + h * HD:MAT_WX + (h + 1) * HD, :] += _dot_tn(vb[:, sl], drxb[:, sl])
        dv = dv + jnp.concatenate(dvg, axis=1)
        dvec_ref[1:2, :] += jnp.sum(dv, axis=0, keepdims=True)
        for k in range(4):
            dvec_ref[5 + k:6 + k, :] += jnp.sum(dv * taps[k], axis=0, keepdims=True)
        dvext = jnp.concatenate([dv, dv_carry[...]], axis=0)
        dv_carry[...] = dv[0:CONV_HALO, :]
        n = tm + CONV_HALO
        du_rnn = dv * cw_ref[3:4, :]
        for k in range(3):
            du_rnn = du_rnn + pltpu.roll(dvext, n - (3 - k), axis=0)[0:tm, :] * cw_ref[k:k + 1, :]
        dproj_ref[:, DP:DP + DR] = du_rnn.astype(BF16)

        dpm = _dot_nt(dyp, wpo_ref[...])
        u_pool = proj_ref[:, 0:DP]
        ext = jnp.concatenate([projh_ref[:, 0:DP] * has_prev, u_pool], axis=0)
        sums = _pool_windows(ext, +1)
        scale_v = scale_ref[...]
        qs = []
        dpooled = []
        dscale = []
        for gi_, w in enumerate(WINDOWS):
            sl = slice(gi_ * PG, (gi_ + 1) * PG)
            inv_cnt = 1.0 / jnp.minimum(t_glob + 1, w).astype(F32)
            pooled_b = (sums[gi_][POOL_HALO:, :] * inv_cnt - u_pool[:, sl]).astype(BF16)
            mixed_g = _dot(pooled_b, wg_ref[gi_])
            dscale.append(jnp.sum(dpm[:, sl] * mixed_g, axis=0, keepdims=True))
            dmixed_b = (dpm[:, sl] * scale_v[:, sl]).astype(BF16)
            dmat_ref[gi_ * PG:(gi_ + 1) * PG, :] += _dot_tn(pooled_b, dmixed_b)
            dp_g = _dot_nt(dmixed_b, wg_ref[gi_])
            dpooled.append(dp_g)
            qs.append(dp_g * inv_cnt)
        dvec_ref[0:1, 0:DP] += jnp.concatenate(dscale, axis=1)
        q = jnp.concatenate(qs, axis=1)
        qext = jnp.concatenate([q, q_carry[...]], axis=0)
        q_carry[...] = q[0:POOL_HALO, :]
        tsum = _pool_windows(qext, -1)
        for gi_ in range(4):
            dproj_ref[:, gi_ * PG:(gi_ + 1) * PG] = (tsum[gi_][0:tm, :] - dpooled[gi_]).astype(BF16)

        @pl.when(i == nt - 1)
        def _():
            dvec_ref[4:5, :] = dvec_ref[4:5, :] * (LRU_C * _sigmoid(-lam_ref[...]))

    return _call(
        body, "mixer_bwd", (nt,),
        in_specs=[rev(DIN), halo(POOL_HALO, DIN), rev(D), rev(D), rev(D), rev(DR), halo(8, DR),
                  _resident((4, PG, PG)), _resident((1, DP)), _resident((DP, D)), _resident((4, DR)), _resident((1, DR)),
                  _resident((NH, HD, HD)), _resident((1, DR)), _resident((NH, HD, HD)), _resident((1, DR)),
                  _resident((1, DR)), _resident((DR, D))],
        out_specs=[rev(DIN), rev(D), rev(D), _resident((MAT_ROWS, HD)), _resident((VEC_ROWS, DR))],
        out_shape=[jax.ShapeDtypeStruct((S, DIN), BF16), jax.ShapeDtypeStruct((S, D), BF16),
                   jax.ShapeDtypeStruct((S, D), BF16), jax.ShapeDtypeStruct((MAT_ROWS, HD), F32),
                   jax.ShapeDtypeStruct((VEC_ROWS, DR), F32)],
        scratch_shapes=[pltpu.VMEM((POOL_HALO, DP), F32), pltpu.VMEM((CONV_HALO, DR), F32), pltpu.VMEM((8, DR), F32),
                        pltpu.VMEM((8, DR), F32), pltpu.VMEM((tm, DR), F32)],
        operands=(proj, proj, dmixo, y_pool, y_rnn, hr, hr, wg, scale, w_pool_out, conv_w, conv_b, wa, ba, wx, bx, lam,
                  w_rnn_out),
        exchange=exchange, exchange_operands=exchange_operands)


def _in_bwd(dproj, x, dx2, norm_mix, w_in, exchange=None, exchange_operands=(), tm=512):
    S = x.shape[0]

    def body(dp_ref, x_ref, dx2_ref, g_ref, w_ref, dx_ref, dg_ref):
        i = pl.program_id(0)

        @pl.when(i == 0)
        def _():
            dg_ref[...] = jnp.zeros_like(dg_ref)

        dh = _dot(dp_ref[:, 0:1536], w_ref[0:1536, :])
        dh = dh + _dot(dp_ref[:, 1536:3072], w_ref[1536:3072, :])
        dh = dh + _dot(dp_ref[:, 3072:DIN], w_ref[3072:DIN, :])
        xv = x_ref[...]
        r = lax.rsqrt(jnp.mean(xv * xv, axis=-1, keepdims=True) + EPS)
        xh = xv * r
        dg_ref[0:1, :] += jnp.sum(dh * xh, axis=0, keepdims=True)
        dxh = dh * g_ref[...]
        dx_ref[...] = dx2_ref[...] + r * (dxh - xh * jnp.mean(dxh * xh, axis=-1, keepdims=True))

    return _call(
        body, "in_bwd", (S // tm,),
        in_specs=[_rows(DIN, tm), _rows(D, tm), _rows(D, tm), _resident((1, D)), _resident((DIN, D))],
        out_specs=[_rows(D, tm), _resident((8, D))],
        out_shape=[jax.ShapeDtypeStruct((S, D), F32), jax.ShapeDtypeStruct((8, D), F32)],
        operands=(dproj, x, dx2, norm_mix, w_in), exchange=exchange, exchange_operands=exchange_operands)


def _wgrad(a, b, name, tk, tn, exchange=None, exchange_operands=()):
    S, K = a.shape
    N = b.shape[1]

    def body(a_ref, b_ref, o_ref):
        o_ref[...] = _dot_tn(a_ref[...], b_ref[...]).astype(BF16)

    (out,), exchanged = _call(
        body, name, (K // tk, N // tn),
        in_specs=[pl.BlockSpec((S, tk), lambda k, n: (0, k)), pl.BlockSpec((S, tn), lambda k, n: (0, n))],
        out_specs=[pl.BlockSpec((tk, tn), lambda k, n: (k, n))],
        out_shape=[jax.ShapeDtypeStruct((K, N), BF16)],
        operands=(a, b), exchange=exchange, exchange_operands=exchange_operands)
    return (out, exchanged) if exchange is not None else out


VEC_SCALE, VEC_CONV_B, VEC_BA, VEC_BX, VEC_LAM, VEC_CONV_W, VEC_NORM_FINAL, VEC_NORM_FFN = 0, 1, 2, 3, 4, 5, 9, 10
VEC_LOSS = 11


class _Big:
    def __init__(self, name, rows, cols, axis, n, dtype=BF16, transposed=False):
        self.name, self.rows, self.cols, self.axis, self.n, self.dtype = name, rows, cols, axis, n, dtype
        self.transposed = transposed
        self.block_shape = (rows, n) if axis == 1 else (n, cols)

    def block(self, ref, p):
        if self.axis == 1:
            return ref.at[:, pl.ds(pl.multiple_of(p * self.n, 128), self.n)]
        return ref.at[pl.ds(pl.multiple_of(p * self.n, 16 if self.dtype == BF16 else 8), self.n), :]

    def block_index(self, p):
        return (0, p) if self.axis == 1 else (p, 0)


BIG = (_Big("w_in", DIN, D, 0, DIN // 8, transposed=True), _Big("w_pool_out", DP, D, 1, D // 8),
       _Big("w_rnn_out", DR, D, 0, DR // 8), _Big("w_o", D, D, 0, D // 8),
       _Big("w_ffn_in", 2 * DFF, D, 0, 2 * DFF // 8, transposed=True), _Big("w_ffn_out", DFF, D, 0, DFF // 8))
CONV_W = _Big("conv_w", 8, DR, 1, DR // 8, F32)
W_FFN_IN_HALVES = (_Big("w_ffn_in_lo", 2 * DFF, D // 2, 0, 2 * DFF // 8), _Big("w_ffn_in_hi", 2 * DFF, D // 2, 0, 2 * DFF // 8))
GATHERED = BIG + (CONV_W,) + W_FFN_IN_HALVES

HBM_SPEC = pl.BlockSpec(memory_space=pl.ANY)
VMEM_SPEC = pl.BlockSpec(memory_space=pltpu.VMEM)


def _place():
    x, y, c = (lax.axis_index(a) for a in MESH_AXES)
    other_chips = [(1 - x, y), (x, 1 - y), (1 - x, 1 - y)]
    return x, y, c, other_chips


def _remote(src, dst, send_sems, recv_sems, idx, to):
    return pltpu.make_async_remote_copy(src_ref=src, dst_ref=dst, send_sem=send_sems.at[idx], recv_sem=recv_sems.at[idx],
                                        device_id=to, device_id_type=MESH)


def _device_index(chip, core):
    return 4 * chip[0] + 2 * chip[1] + core


class _Gather:
    def __init__(self, tensors):
        self.tensors = tuple(tensors)
        n = len(self.tensors)
        self.in_specs = [HBM_SPEC] * n
        self.out_specs = [HBM_SPEC] * n
        self.out_shape = [jax.ShapeDtypeStruct((T.rows, T.cols), T.dtype) for T in self.tensors]
        self.scratch_shapes = [pltpu.VMEM(T.block_shape, T.dtype) for T in self.tensors] + [
            pltpu.SemaphoreType.DMA((n, 7)), pltpu.SemaphoreType.DMA((n, 7)), pltpu.SemaphoreType.DMA((n, 2))]

    def middle_at(self, steps):
        return (3 * steps) // 4

    def _copies(self, ins, outs, scratch):
        n = len(self.tensors)
        mine, (send_sems, recv_sems, loc_sems) = scratch[:n], scratch[n:]
        x, y, c, chips = _place()
        sibling = (x, y, 1 - c)
        me = _device_index((x, y), c)
        loads, stores, first, passed, arrivals, late = [], [], [], [], [], []
        for t, T in enumerate(self.tensors):
            place = T.block(outs[t], me)
            loads.append(pltpu.make_async_copy(ins[t], mine[t], loc_sems.at[t, 0]))
            stores.append(pltpu.make_async_copy(mine[t], place, loc_sems.at[t, 1]))
            first.append(_remote(mine[t], place, send_sems, recv_sems, (t, 0), sibling))
            theirs = T.block(outs[t], _device_index((x, y), 1 - c))
            late.append(_remote(theirs, theirs, send_sems, recv_sems, (t, 0), sibling))
            for k, chip in enumerate(chips):
                first.append(_remote(mine[t], place, send_sems, recv_sems, (t, 1 + k), (*chip, c)))
                land = T.block(outs[t], _device_index(chip, c))
                arrivals.append(_remote(land, land, send_sems, recv_sems, (t, 1 + k), sibling))
                passed.append(_remote(land, land, send_sems, recv_sems, (t, 4 + k), sibling))
                theirs = T.block(outs[t], _device_index(chip, 1 - c))
                late.append(_remote(theirs, theirs, send_sems, recv_sems, (t, 4 + k), sibling))
        return loads, stores, first, passed, arrivals, late

    def start(self, ins, outs, scratch):
        loads, stores, first, _, _, _ = self._copies(ins, outs, scratch)
        for cp in loads:
            cp.start()
        for cp in loads:
            cp.wait()
        for cp in stores + first:
            cp.start()

    def middle(self, ins, outs, scratch):
        _, _, _, passed, arrivals, _ = self._copies(ins, outs, scratch)
        for arrived, cp in zip(arrivals, passed):
            arrived.wait_recv()
            cp.start()

    def finish(self, ins, outs, scratch):
        _, stores, first, passed, _, late = self._copies(ins, outs, scratch)
        for cp in late:
            cp.wait_recv()
        for cp in first + passed:
            cp.wait_send()
        for cp in stores:
            cp.wait()


def _all_gather(blocks, tensors, name):
    gather = _Gather(tensors)

    def body(*refs):
        n = len(gather.tensors)
        ins, outs, scratch = refs[:n], refs[n:2 * n], refs[2 * n:]
        gather.start(ins, outs, scratch)
        gather.middle(ins, outs, scratch)
        gather.finish(ins, outs, scratch)

    return pl.pallas_call(
        body, name=name, in_specs=gather.in_specs, out_specs=gather.out_specs, out_shape=gather.out_shape,
        scratch_shapes=gather.scratch_shapes, compiler_params=pltpu.CompilerParams(vmem_limit_bytes=VMEM_LIMIT),
    )(*blocks)


PAIR_ROWS = 32


def _pair_reduce(grads, tensors, name):
    nt = len(tensors)

    def body(*refs):
        ins, own_out, sums_out, landed, mine = (refs[k * nt:(k + 1) * nt] for k in range(5))
        send_sems, recv_sems, loc_sems = refs[5 * nt:]
        x, y, c, chips = _place()
        chip_of = [2 * chip[0] + chip[1] for chip in chips]
        swaps, loads = [], []
        for t, T in enumerate(tensors):
            for j in range(4):
                swaps.append(_remote(T.block(ins[t], 2 * j + 1 - c), landed[t].at[j], send_sems, recv_sems, (t, j),
                                     (x, y, 1 - c)))
            for k in range(3):
                loads.append(pltpu.make_async_copy(T.block(ins[t], 2 * chip_of[k] + c), mine[t].at[k], loc_sems.at[t, k]))
        for cp in swaps + loads:
            cp.start()
        for cp in loads:
            cp.wait()
        for cp in swaps:
            cp.wait_recv()
        stores = []
        for t, T in enumerate(tensors):
            for k in range(3):
                acc, got = mine[t].at[k], landed[t].at[chip_of[k]]

                def add(i, carry, acc=acc, got=got):
                    rows = pl.ds(pl.multiple_of(i * PAIR_ROWS, PAIR_ROWS), PAIR_ROWS)
                    acc[rows, :] = (acc[rows, :].astype(F32) + got[rows, :].astype(F32)).astype(BF16)
                    return carry

                lax.fori_loop(0, T.block_shape[0] // PAIR_ROWS, add, 0)
            stores.append(pltpu.make_async_copy(mine[t], sums_out[t], loc_sems.at[t, 3]))
            stores.append(pltpu.make_async_copy(landed[t].at[2 * x + y], own_out[t], loc_sems.at[t, 4]))
        for cp in stores:
            cp.start()
        for cp in swaps:
            cp.wait_send()
        for cp in stores:
            cp.wait()

    blocks = [T.block_shape for T in tensors]
    return pl.pallas_call(
        body, name=name,
        in_specs=[HBM_SPEC] * nt, out_specs=[HBM_SPEC] * (2 * nt),
        out_shape=[jax.ShapeDtypeStruct(b, BF16) for b in blocks] + [jax.ShapeDtypeStruct((3,) + b, BF16) for b in blocks],
        scratch_shapes=[pltpu.VMEM((4,) + b, BF16) for b in blocks] + [pltpu.VMEM((3,) + b, BF16) for b in blocks]
        + [pltpu.SemaphoreType.DMA((nt, 4)), pltpu.SemaphoreType.DMA((nt, 4)), pltpu.SemaphoreType.DMA((nt, 5))],
        compiler_params=pltpu.CompilerParams(vmem_limit_bytes=VMEM_LIMIT),
    )(*grads)


class _Scatter:
    middle = None

    def __init__(self, tensors):
        n = len(tensors)
        self.in_specs = [HBM_SPEC] * n
        self.out_specs = [HBM_SPEC] * n
        self.out_shape = [jax.ShapeDtypeStruct((3,) + T.block_shape, BF16) for T in tensors]
        self.scratch_shapes = [pltpu.SemaphoreType.DMA((n, 3)), pltpu.SemaphoreType.DMA((n, 3))]

    def _copies(self, ins, outs, scratch):
        send_sems, recv_sems = scratch
        x, y, c, chips = _place()
        return [_remote(ins[t].at[k], outs[t].at[k], send_sems, recv_sems, (t, k), (*chip, c))
                for t in range(len(ins)) for k, chip in enumerate(chips)]

    def start(self, ins, outs, scratch):
        for cp in self._copies(ins, outs, scratch):
            cp.start()

    def finish(self, ins, outs, scratch):
        for cp in self._copies(ins, outs, scratch):
            cp.wait()


def _chip_scatter(sums, tensors, name):
    scatter = _Scatter(tensors)
    n = len(tensors)

    def body(*refs):
        ins, outs, scratch = refs[:n], refs[n:2 * n], refs[2 * n:]
        scatter.start(ins, outs, scratch)
        scatter.finish(ins, outs, scratch)

    return pl.pallas_call(
        body, name=name, in_specs=scatter.in_specs, out_specs=scatter.out_specs, out_shape=scatter.out_shape,
        scratch_shapes=scatter.scratch_shapes,
    )(*sums)


def _adamw(w, g, m, v):
    m = ADAM_B1 * m + (1.0 - ADAM_B1) * g
    v = ADAM_B2 * v + (1.0 - ADAM_B2) * (g * g)
    m_hat = m / (1.0 - ADAM_B1 ** ADAM_STEP)
    v_hat = v / (1.0 - ADAM_B2 ** ADAM_STEP)
    delta = -ADAM_LR * (m_hat / (jnp.sqrt(v_hat) + ADAM_EPS) + ADAM_WD * w)
    return delta, m, v


def _final_sum(T, g, lz1, lz2, where, w, m, v):
    rows, cols = T.block_shape
    sub = 4 if T.axis == 0 and rows % 64 == 0 and rows > 256 else 1
    blk = (rows // sub, cols)

    def body(where_ref, g_ref, l1_ref, l2_ref, w_ref, m_ref, v_ref, g_out, d_out, m_out, v_out):
        tot = g_ref[...].astype(F32) + l1_ref[...].astype(F32)
        for k in range(3):
            tot = tot + l2_ref[k].astype(F32)
        g_out[...] = tot
        d_out[...], m_out[...], v_out[...] = _adamw(w_ref[...], tot, m_ref[...], v_ref[...])

    def in_whole(r, wh):
        p = wh[0]
        return (0, p) if T.axis == 1 else (p * sub + r, 0)

    own = pl.BlockSpec(blk, lambda r, wh: (r, 0))
    return pl.pallas_call(
        body, name="grad_final_" + T.name,
        grid_spec=pltpu.PrefetchScalarGridSpec(
            num_scalar_prefetch=1, grid=(sub,),
            in_specs=[pl.BlockSpec(blk, in_whole),
                      own,
                      pl.BlockSpec((3,) + blk, lambda r, wh: (0, r, 0)), own, own, own],
            out_specs=[own] * 4),
        out_shape=[jax.ShapeDtypeStruct(T.block_shape, F32)] * 4,
        compiler_params=_params("arbitrary"),
    )(where, g, lz1, lz2, w, m, v)


MAT_PIECE = MAT_ROWS // 8
VEC_PIECE = DR // 8


class _AllReduce:
    def __init__(self, items):
        self.items = tuple(items)
        n = len(self.items)
        self.in_specs = [HBM_SPEC] * n
        self.out_specs = [HBM_SPEC] * n
        self.out_shape = [jax.ShapeDtypeStruct(shape, F32) for shape, _ in self.items]
        pieces = [(shape[0] // 8, shape[1]) if axis == 0 else (shape[0], shape[1] // 8) for shape, axis in self.items]
        self.scratch_shapes = ([pltpu.VMEM((8,) + p, F32) for p in pieces] + [pltpu.VMEM(p, F32) for p in pieces] + [
            pltpu.SemaphoreType.DMA((2 * n, 8)), pltpu.SemaphoreType.DMA((2 * n, 8)), pltpu.SemaphoreType.DMA((2 * n,))])

    def middle_at(self, steps):
        return steps // 4

    def _copies(self, ins, outs, scratch):
        n = len(self.items)
        landed, sums, (send_sems, recv_sems, loc_sems) = scratch[:n], scratch[n:2 * n], scratch[2 * n:]
        x, y, c, _ = _place()
        me = _device_index((x, y), c)

        def peer(r):
            return (1 - x if r & 4 else x, 1 - y if r & 2 else y, 1 - c if r & 1 else c)

        def piece(i, ref, p):
            shape, axis = self.items[i]
            if axis == 0:
                rows = shape[0] // 8
                return ref.at[pl.ds(pl.multiple_of(p * rows, 8), rows), :]
            cols = shape[1] // 8
            return ref.at[:, pl.ds(pl.multiple_of(p * cols, 128), cols)]

        own, scatter, arrivals, keep, spread, late = [], [], [], [], [], []
        for i in range(n):
            own.append(pltpu.make_async_copy(piece(i, ins[i], me), landed[i].at[0], loc_sems.at[2 * i]))
            keep.append(pltpu.make_async_copy(sums[i], piece(i, outs[i], me), loc_sems.at[2 * i + 1]))
            for r in range(1, 8):
                to = peer(r)
                p = _device_index(to[:2], to[2])
                scatter.append(_remote(piece(i, ins[i], p), landed[i].at[r], send_sems, recv_sems, (2 * i, r), to))
                spread.append(_remote(sums[i], piece(i, outs[i], me), send_sems, recv_sems, (2 * i + 1, r), to))
                late.append(_remote(sums[i], piece(i, outs[i], p), send_sems, recv_sems, (2 * i + 1, r), to))
        return own, scatter, keep, spread, late, landed, sums

    def start(self, ins, outs, scratch):
        own, scatter, _, _, _, _, _ = self._copies(ins, outs, scratch)
        for cp in own + scatter:
            cp.start()

    def middle(self, ins, outs, scratch):
        own, scatter, keep, spread, _, landed, sums = self._copies(ins, outs, scratch)
        for cp in own:
            cp.wait()
        for cp in scatter:
            cp.wait_recv()
        for i in range(len(self.items)):
            total = landed[i][0]
            for r in range(1, 8):
                total = total + landed[i][r]
            sums[i][...] = total
        for cp in keep + spread:
            cp.start()

    def finish(self, ins, outs, scratch):
        _, scatter, keep, spread, late, _, _ = self._copies(ins, outs, scratch)
        for cp in late:
            cp.wait_recv()
        for cp in scatter + spread:
            cp.wait_send()
        for cp in keep:
            cp.wait()


class _Both:
    def __init__(self, a, b):
        self.a, self.b = a, b
        self.in_specs, self.out_specs = a.in_specs + b.in_specs, a.out_specs + b.out_specs
        self.out_shape, self.scratch_shapes = a.out_shape + b.out_shape, a.scratch_shapes + b.scratch_shapes

    def middle_at(self, steps):
        return min(e.middle_at(steps) for e in (self.a, self.b) if e.middle is not None)

    def _each(self, ins, outs, scratch):
        a = self.a
        i, o, s = len(a.in_specs), len(a.out_specs), len(a.scratch_shapes)
        return (a, ins[:i], outs[:o], scratch[:s]), (self.b, ins[i:], outs[o:], scratch[s:])

    def start(self, ins, outs, scratch):
        for e, i, o, s in self._each(ins, outs, scratch):
            e.start(i, o, s)

    def middle(self, ins, outs, scratch):
        for e, i, o, s in self._each(ins, outs, scratch):
            if e.middle is not None:
                e.middle(i, o, s)

    def finish(self, ins, outs, scratch):
        for e, i, o, s in self._each(ins, outs, scratch):
            e.finish(i, o, s)


def _all_reduce(arrays, items, name):
    reduce = _AllReduce(items)
    n = len(items)

    def body(*refs):
        ins, outs, scratch = refs[:n], refs[n:2 * n], refs[2 * n:]
        reduce.start(ins, outs, scratch)
        reduce.middle(ins, outs, scratch)
        reduce.finish(ins, outs, scratch)

    return pl.pallas_call(
        body, name=name, in_specs=reduce.in_specs, out_specs=reduce.out_specs, out_shape=reduce.out_shape,
        scratch_shapes=reduce.scratch_shapes,
    )(*arrays)


def _adam_small(grads, wmv):
    n = len(grads)

    def body(*refs):
        g_refs, rest = refs[:n], refs[n:]
        ins, outs = rest[:3 * n], rest[3 * n:]
        for i in range(n):
            d, m, v = _adamw(ins[3 * i][...], g_refs[i][...], ins[3 * i + 1][...], ins[3 * i + 2][...])
            outs[3 * i][...], outs[3 * i + 1][...], outs[3 * i + 2][...] = d, m, v

    flat = [a for t in wmv for a in t]
    return pl.pallas_call(
        body, name="adam_small",
        in_specs=[VMEM_SPEC] * (4 * n), out_specs=[VMEM_SPEC] * (3 * n),
        out_shape=[jax.ShapeDtypeStruct(a.shape, F32) for a in flat],
    )(*grads, *flat)


WEIGHT_NAMES = ("norm_mix", "w_in", "w_pool_grp", "pool_scale", "w_pool_out", "conv_w", "conv_b", "w_rg_a", "b_rg_a", "w_rg_x",
                "b_rg_x", "lru_lambda", "w_rnn_out", "w_o", "norm_ffn", "w_ffn_in", "w_ffn_out", "norm_final")


def kernel(x, norm_mix, w_in, w_pool_grp, pool_scale, w_pool_out, conv_w, conv_b, w_rg_a, b_rg_a, w_rg_x, b_rg_x, lru_lambda, w_rnn_out, w_o, norm_ffn, w_ffn_in, w_ffn_out, norm_final, loss_target, m_norm_mix, m_w_in, m_w_pool_grp, m_pool_scale, m_w_pool_out, m_conv_w, m_conv_b, m_w_rg_a, m_b_rg_a, m_w_rg_x, m_b_rg_x, m_lru_lambda, m_w_rnn_out, m_w_o, m_norm_ffn, m_w_ffn_in, m_w_ffn_out, m_norm_final, v_norm_mix, v_w_in, v_w_pool_grp, v_pool_scale, v_w_pool_out, v_conv_w, v_conv_b, v_w_rg_a, v_b_rg_a, v_w_rg_x, v_b_rg_x, v_lru_lambda, v_w_rnn_out, v_w_o, v_norm_ffn, v_w_ffn_in, v_w_ffn_out, v_norm_final):
    w = dict(norm_mix=norm_mix, w_in=w_in, w_pool_grp=w_pool_grp, pool_scale=pool_scale, w_pool_out=w_pool_out, conv_w=conv_w,
             conv_b=conv_b, w_rg_a=w_rg_a, b_rg_a=b_rg_a, w_rg_x=w_rg_x, b_rg_x=b_rg_x, lru_lambda=lru_lambda,
             w_rnn_out=w_rnn_out, w_o=w_o, norm_ffn=norm_ffn, w_ffn_in=w_ffn_in, w_ffn_out=w_ffn_out, norm_final=norm_final)
    m = dict(norm_mix=m_norm_mix, w_in=m_w_in, w_pool_grp=m_w_pool_grp, pool_scale=m_pool_scale, w_pool_out=m_w_pool_out,
             conv_w=m_conv_w, conv_b=m_conv_b, w_rg_a=m_w_rg_a, b_rg_a=m_b_rg_a, w_rg_x=m_w_rg_x, b_rg_x=m_b_rg_x,
             lru_lambda=m_lru_lambda, w_rnn_out=m_w_rnn_out, w_o=m_w_o, norm_ffn=m_norm_ffn, w_ffn_in=m_w_ffn_in,
             w_ffn_out=m_w_ffn_out, norm_final=m_norm_final)
    v = dict(norm_mix=v_norm_mix, w_in=v_w_in, w_pool_grp=v_w_pool_grp, pool_scale=v_pool_scale, w_pool_out=v_w_pool_out,
             conv_w=v_conv_w, conv_b=v_conv_b, w_rg_a=v_w_rg_a, b_rg_a=v_b_rg_a, w_rg_x=v_w_rg_x, b_rg_x=v_b_rg_x,
             lru_lambda=v_lru_lambda, w_rnn_out=v_w_rnn_out, w_o=v_w_o, norm_ffn=v_norm_ffn, w_ffn_in=v_w_ffn_in,
             w_ffn_out=v_w_ffn_out, norm_final=v_norm_final)
    xi, yi, ci = (lax.axis_index(a) for a in MESH_AXES)
    chip = 2 * xi + yi

    def held(T, a):
        return jnp.swapaxes(a, 0, 1) if T.transposed else a

    where = jnp.stack([2 * chip + ci]).astype(jnp.int32)
    by_name = {T.name: T for T in GATHERED}
    block = {T.name: held(T, w[T.name][0]).astype(T.dtype) for T in BIG}
    block["conv_w"] = jnp.pad(conv_w[0], ((0, CONV_W.rows - 4), (0, 0)))
    block["w_ffn_in_lo"], block["w_ffn_in_hi"] = block["w_ffn_in"][:, :D // 2], block["w_ffn_in"][:, D // 2:]

    def gather_of(*names):
        return dict(exchange=_Gather([by_name[n] for n in names]), exchange_operands=[block[n] for n in names])

    def pair_sums(names, partials, tag):
        out = _pair_reduce(partials, [by_name[n] for n in names], "grad_pair_reduce_" + tag)
        return list(out[:len(names)]), list(out[len(names):])

    xs, target = x[0], loss_target[0]
    wg_b, wa_b, wx_b = (a[0].astype(BF16) for a in (w_pool_grp, w_rg_a, w_rg_x))
    ba2, bx2 = b_rg_a.reshape(1, DR), b_rg_x.reshape(1, DR)
    (w_in_g,) = _all_gather([block["w_in"]], [by_name["w_in"]], "all_gather_w_in")
    (proj, h1), (w_pool_out_g, w_rnn_out_g, w_o_g, conv_g, w_ffn_lo_g) = _in_proj(
        xs, norm_mix, w_in_g, **gather_of("w_pool_out", "w_rnn_out", "w_o", "conv_w", "w_ffn_in_lo"))
    mixer_weights = (wg_b, pool_scale, w_pool_out_g, conv_g[0:4], conv_b, wa_b, ba2, wx_b, bx2, lru_lambda, w_rnn_out_g)
    (_, pm, y_pool, hr, z, y_rnn), (w_ffn_hi_g,) = _mixer_fwd(proj, *mixer_weights, **gather_of("w_ffn_in_hi"))
    (mix, x2, h2), _ = _merge_out(xs, proj, y_pool, y_rnn, w_o_g, norm_ffn)
    (gu, act), (w_ffn_out_g,) = _ffn_up(h2, w_ffn_lo_g, w_ffn_hi_g, **gather_of("w_ffn_out"))
    dx3, dx3b, loss_part, dvec_fin = _ffn_down_loss(act, x2, target, w_ffn_out_g, norm_final.reshape(1, D))

    dgu = _ffn_bwd_down(dx3b, gu, w_ffn_out_g)
    dx2, dx2b, dmixo, dvec_ffn = _ffn_bwd_up(dgu, x2, dx3, w_ffn_lo_g, w_ffn_hi_g, norm_ffn, w_o_g)
    names_a = ("w_ffn_in", "w_ffn_out", "w_o")
    part_a = [_wgrad(dgu, h2, "wgrad_ffn_in", 1408, 512), _wgrad(act, dx3b, "wgrad_ffn_out", 1408, 512),
              _wgrad(mix, dx2b, "wgrad_o", 1024, 1024)]
    lz1_a, sums_a = pair_sums(names_a, part_a, "ffn")
    (dproj, dypb, dyrb, dmat, dvec_mix), lz2_a = _mixer_bwd(
        proj, dmixo, y_pool, y_rnn, hr, *mixer_weights,
        exchange=_Scatter([by_name[n] for n in names_a]), exchange_operands=sums_a)
    names_b = ("w_pool_out", "w_rnn_out")
    part_b = [_wgrad(pm, dypb, "wgrad_pool_out", 512, 1024), _wgrad(z, dyrb, "wgrad_rnn_out", 1024, 1024)]
    lz1_b, sums_b = pair_sums(names_b, part_b, "mix")
    dvec = jnp.concatenate([dvec_mix[0:9], dvec_fin[0:1], dvec_ffn[0:1], jnp.pad(loss_part, ((0, 0), (0, DR - 1))),
                            jnp.zeros((VEC_ROWS - 12, DR), F32)], axis=0)
    g_in, exchanged = _wgrad(
        dproj, h1, "wgrad_in", 1152, 1024,
        exchange=_Both(_Scatter([by_name[n] for n in names_b]), _AllReduce([((MAT_ROWS, HD), 0), ((VEC_ROWS, DR), 1)])),
        exchange_operands=sums_b + [dmat, dvec])
    lz2_b, (mat, vec) = exchanged[:2], exchanged[2:]
    loss = vec[VEC_LOSS, 0]
    lz1_c, sums_c = pair_sums(("w_in",), [g_in], "in")
    (grad_x, dvec_in), lz2_c = _in_bwd(dproj, xs, dx2, norm_mix, w_in_g,
                                       exchange=_Scatter([by_name["w_in"]]), exchange_operands=sums_c)
    (vec_in,) = _all_reduce([dvec_in], [((8, D), 1)], "all_reduce_norm_mix")

    grads, delta, new_m, new_v = {}, {}, {}, {}
    for n, g, l1, l2 in zip(names_a + names_b + ("w_in",), part_a + part_b + [g_in], lz1_a + lz1_b + lz1_c,
                            lz2_a + lz2_b + lz2_c):
        T = by_name[n]
        out = _final_sum(T, g, l1, l2, where, held(T, w[n][0]), held(T, m[n][0]), held(T, v[n][0]))
        grads[n], delta[n], new_m[n], new_v[n] = (held(T, a) for a in out)
    me = 4 * xi + 2 * yi + ci
    small_grads = dict(
        w_pool_grp=mat[0:MAT_WA], w_rg_a=mat[MAT_WA:MAT_WX], w_rg_x=mat[MAT_WX:MAT_ROWS],
        pool_scale=vec[VEC_SCALE:VEC_SCALE + 1, 0:DP], conv_b=vec[VEC_CONV_B:VEC_CONV_B + 1],
        b_rg_a=vec[VEC_BA:VEC_BA + 1], b_rg_x=vec[VEC_BX:VEC_BX + 1], lru_lambda=vec[VEC_LAM:VEC_LAM + 1],
        conv_w=lax.dynamic_slice(vec, (VEC_CONV_W, VEC_PIECE * me), (4, VEC_PIECE)),
        norm_final=vec[VEC_NORM_FINAL:VEC_NORM_FINAL + 1], norm_ffn=vec[VEC_NORM_FFN:VEC_NORM_FFN + 1],
        norm_mix=vec_in[0:1])
    names = list(small_grads)
    as2d = lambda a, g: a.reshape(g.shape)
    upd = _adam_small([small_grads[n] for n in names],
                      [(as2d(w[n], small_grads[n]), as2d(m[n], small_grads[n]), as2d(v[n], small_grads[n])) for n in names])
    for i, n in enumerate(names):
        grads[n] = small_grads[n]
        delta[n], new_m[n], new_v[n] = upd[3 * i:3 * i + 3]

    shaped = lambda d: [d[n].reshape(w[n].shape) for n in WEIGHT_NAMES]
    return (loss, grad_x[None], *shaped(grads), *shaped(delta), *shaped(new_m), *shaped(new_v))
```

```python
import functools
import math

import jax
import jax.numpy as jnp
from jax import lax
from jax.experimental import pallas as pl
from jax.experimental.pallas import tpu as pltpu

F32 = jnp.float32
BF16 = jnp.bfloat16

D = 1024
DP = 512
PG = 128
WINDOWS = (2, 4, 8, 16)
DR = 1024
NH = 8
HD = 128
DIN = 4608
DFF = 2816
EPS = 1e-6
LRU_C = 8.0
POOL_HALO = 16
CONV_HALO = 8

ADAM_LR = 0.001
ADAM_B1 = 0.9
ADAM_B2 = 0.999
ADAM_EPS = 1e-08
ADAM_WD = 0.01
ADAM_STEP = 10

VMEM_LIMIT = 56 * 1024 * 1024
MESH_AXES = ("x", "y", "c")
MESH = pl.DeviceIdType.MESH


def _dot(a, b):
    return jnp.dot(a, b, preferred_element_type=F32)


def _dot_nt(a, b):
    return lax.dot_general(a, b, (((1,), (1,)), ((), ())), preferred_element_type=F32)


def _dot_tn(a, b):
    return lax.dot_general(a, b, (((0,), (0,)), ((), ())), preferred_element_type=F32)


def _params(*sem):
    return pltpu.CompilerParams(dimension_semantics=sem, vmem_limit_bytes=VMEM_LIMIT)


def _resident(shape):
    nd = len(shape)
    return pl.BlockSpec(shape, lambda i: (0,) * nd, pipeline_mode=pl.Buffered(1))


def _rows(shape_cols, tm):
    return pl.BlockSpec((tm, shape_cols), lambda i: (i, 0))


def _call(body, name, grid, in_specs, out_specs, out_shape, operands, scratch_shapes=(), exchange=None, exchange_operands=()):
    n_in, n_out, n_scr = len(in_specs), len(out_specs), len(scratch_shapes)
    steps = math.prod(grid)
    if exchange is None:
        outs = pl.pallas_call(body, name=name, grid=grid, in_specs=in_specs, out_specs=out_specs, out_shape=out_shape,
                              scratch_shapes=list(scratch_shapes), compiler_params=_params(*["arbitrary"] * len(grid)))(*operands)
        return outs, []
    e_in, e_out = len(exchange.in_specs), len(exchange.out_specs)

    def hosted(*refs):
        ins, refs = refs[:n_in], refs[n_in:]
        e_ins, refs = refs[:e_in], refs[e_in:]
        outs, refs = refs[:n_out], refs[n_out:]
        e_outs, refs = refs[:e_out], refs[e_out:]
        scr, e_scr = refs[:n_scr], refs[n_scr:]
        step = pl.program_id(0)
        for axis in range(1, len(grid)):
            step = step * grid[axis] + pl.program_id(axis)
        pl.when(step == 0)(lambda: exchange.start(e_ins, e_outs, e_scr))
        if exchange.middle is not None:
            pl.when(step == exchange.middle_at(steps))(lambda: exchange.middle(e_ins, e_outs, e_scr))
        body(*ins, *outs, *scr)
        pl.when(step == steps - 1)(lambda: exchange.finish(e_ins, e_outs, e_scr))

    outs = pl.pallas_call(
        hosted, name=name, grid=grid, in_specs=list(in_specs) + exchange.in_specs,
        out_specs=list(out_specs) + exchange.out_specs, out_shape=list(out_shape) + exchange.out_shape,
        scratch_shapes=list(scratch_shapes) + exchange.scratch_shapes,
        compiler_params=_params(*["arbitrary"] * len(grid)))(*operands, *exchange_operands)
    return outs[:n_out], outs[n_out:]


GELU_C = math.sqrt(2.0 / math.pi)
GELU_K = 0.044715 * GELU_C


def _gelu(x, with_grad=False):
    x2 = x * x
    t = jnp.tanh(x * (GELU_C + GELU_K * x2))
    hx = 0.5 * x
    y = hx + hx * t
    if not with_grad:
        return y
    return y, 0.5 + 0.5 * t + hx * (1.0 - t * t) * (GELU_C + (3.0 * GELU_K) * x2)


def _softplus_neg(lam):
    z = jnp.exp(-jnp.abs(lam))
    u = 1.0 + z
    dlt = u - 1.0
    log1p = jnp.where(dlt == 0.0, z, jnp.log(u) * (z / jnp.where(dlt == 0.0, 1.0, dlt)))
    return jnp.maximum(-lam, 0.0) + log1p


def _sigmoid(x):
    return 0.5 * jnp.tanh(0.5 * x) + 0.5


def _linear_scan(out_ref, A, B, h0, reverse):
    n = A.shape[0]
    sub = lax.broadcasted_iota(jnp.int32, (8, 1), 0)
    tiles = range(n // 8 - 1, -1, -1) if reverse else range(n // 8)
    carry = h0
    for j in tiles:
        a, b = A[8 * j:8 * j + 8, :], B[8 * j:8 * j + 8, :]
        for d in (1, 2, 4):
            keep = (sub < 8 - d) if reverse else (sub >= d)
            shift = 8 - d if reverse else d
            b = jnp.where(keep, a * pltpu.roll(b, shift, axis=0) + b, b)
            a = jnp.where(keep, a * pltpu.roll(a, shift, axis=0), a)
        h = a * carry + b
        out_ref[8 * j:8 * j + 8, :] = h
        carry = h[0:1, :] if reverse else h[7:8, :]
    return carry


def _pool_windows(ext, shift_sign):
    n = ext.shape[0]
    s = ext
    outs = []
    for w in WINDOWS:
        d = w // 2
        s = s + pltpu.roll(s, d if shift_sign > 0 else n - d, axis=0)
        outs.append(s[:, :PG])
        s = s[:, PG:]
    return outs


def _conv_taps(uext):
    taps = []
    for k in range(4):
        sh = 3 - k
        v = uext if sh == 0 else pltpu.roll(uext, sh, axis=0)
        taps.append(v[CONV_HALO:, :])
    return taps


def _gates(v, wa_ref, ba_ref, wx_ref, bx_ref, sp):
    vb = v.astype(BF16)
    ra, rx = [], []
    for h in range(NH):
        vh = vb[:, h * HD:(h + 1) * HD]
        ra.append(_dot(vh, wa_ref[h]))
        rx.append(_dot(vh, wx_ref[h]))
    r = _sigmoid(jnp.concatenate(ra, axis=1) + ba_ref[...])
    i = _sigmoid(jnp.concatenate(rx, axis=1) + bx_ref[...])
    log_a = r * ((-LRU_C) * sp)
    a = jnp.exp(log_a)
    one_minus = -jnp.tanh(log_a) * (1.0 + a * a)
    return r, i, a, jnp.sqrt(one_minus), lax.rsqrt(one_minus)


def _in_proj(x, norm_mix, w_in, exchange=None, exchange_operands=(), tm=512):
    S = x.shape[0]

    def body(x_ref, g_ref, w_ref, proj_ref, h_ref):
        xv = x_ref[...]
        r = lax.rsqrt(jnp.mean(xv * xv, axis=-1, keepdims=True) + EPS)
        h = (xv * r * g_ref[...]).astype(BF16)
        h_ref[...] = h
        for n0 in range(0, DIN, 512):
            proj_ref[:, n0:n0 + 512] = _dot_nt(h, w_ref[n0:n0 + 512, :])

    return _call(
        body, "in_proj", (S // tm,),
        in_specs=[_rows(D, tm), _resident((1, D)), _resident((DIN, D))],
        out_specs=[_rows(DIN, tm), _rows(D, tm)],
        out_shape=[jax.ShapeDtypeStruct((S, DIN), F32), jax.ShapeDtypeStruct((S, D), BF16)],
        operands=(x, norm_mix, w_in), exchange=exchange, exchange_operands=exchange_operands)


def _mixer_fwd(proj, wg, scale, w_pool_out, conv_w, conv_b, wa, ba, wx, bx, lam, w_rnn_out, exchange=None,
               exchange_operands=(), tm=256):
    S = proj.shape[0]
    UW = DP + 2 * DR

    def body(proj_ref, wg_ref, scale_ref, wpo_ref, cw_ref, cb_ref, wa_ref, ba_ref, wx_ref, bx_ref, lam_ref, wro_ref,
             pooled_ref, pm_ref, ypool_ref, hr_ref, z_ref, yrnn_ref, pool_carry, conv_carry, h_carry):
        i = pl.program_id(0)

        @pl.when(i == 0)
        def _():
            pool_carry[...] = jnp.zeros_like(pool_carry)
            conv_carry[...] = jnp.zeros_like(conv_carry)
            h_carry[...] = jnp.zeros_like(h_carry)

        rows = lax.broadcasted_iota(jnp.int32, (tm, 1), 0)
        t_glob = i * tm + rows

        u_pool = proj_ref[:, 0:DP]
        ext = jnp.concatenate([pool_carry[...], u_pool], axis=0)
        pool_carry[...] = u_pool[tm - POOL_HALO:, :]
        sums = _pool_windows(ext, +1)
        mixed = []
        for g, w in enumerate(WINDOWS):
            inv_cnt = 1.0 / jnp.minimum(t_glob + 1, w).astype(F32)
            pooled_g = sums[g][POOL_HALO:, :] * inv_cnt - u_pool[:, g * PG:(g + 1) * PG]
            pooled_b = pooled_g.astype(BF16)
            pooled_ref[:, g * PG:(g + 1) * PG] = pooled_b
            mixed.append(_dot(pooled_b, wg_ref[g]))
        pm = (jnp.concatenate(mixed, axis=1) * scale_ref[...]).astype(BF16)
        pm_ref[...] = pm
        ypool_ref[...] = _dot(pm, wpo_ref[...])

        u_rnn = proj_ref[:, DP:DP + DR]
        uext = jnp.concatenate([conv_carry[...], u_rnn], axis=0)
        conv_carry[...] = u_rnn[tm - CONV_HALO:, :]
        taps = _conv_taps(uext)
        v = cb_ref[...]
        for k in range(4):
            v = v + taps[k] * cw_ref[k:k + 1, :]
        sp = _softplus_neg(lam_ref[...])
        _, gi, a, mult, _ = _gates(v, wa_ref, ba_ref, wx_ref, bx_ref, sp)
        h_carry[0:1, :] = _linear_scan(hr_ref, a, mult * gi * v, h_carry[0:1, :], reverse=False)
        z = (hr_ref[...] * _gelu(proj_ref[:, DP + DR:UW])).astype(BF16)
        z_ref[...] = z
        yrnn_ref[...] = _dot(z, wro_ref[...])

    return _call(
        body, "mixer_fwd", (S // tm,),
        in_specs=[_rows(UW, tm), _resident((4, PG, PG)), _resident((1, DP)), _resident((DP, D)), _resident((4, DR)),
                  _resident((1, DR)), _resident((NH, HD, HD)), _resident((1, DR)), _resident((NH, HD, HD)),
                  _resident((1, DR)), _resident((1, DR)), _resident((DR, D))],
        out_specs=[_rows(DP, tm), _rows(DP, tm), _rows(D, tm), _rows(DR, tm), _rows(DR, tm), _rows(D, tm)],
        out_shape=[jax.ShapeDtypeStruct((S, DP), BF16), jax.ShapeDtypeStruct((S, DP), BF16),
                   jax.ShapeDtypeStruct((S, D), F32), jax.ShapeDtypeStruct((S, DR), F32),
                   jax.ShapeDtypeStruct((S, DR), BF16), jax.ShapeDtypeStruct((S, D), F32)],
        scratch_shapes=[pltpu.VMEM((POOL_HALO, DP), F32), pltpu.VMEM((CONV_HALO, DR), F32), pltpu.VMEM((8, DR), F32)],
        operands=(proj, wg, scale, w_pool_out, conv_w, conv_b, wa, ba, wx, bx, lam, w_rnn_out),
        exchange=exchange, exchange_operands=exchange_operands)


FF_CHUNK = DFF // 4


def _rms(x):
    r = lax.rsqrt(jnp.mean(x * x, axis=-1, keepdims=True) + EPS)
    return r, x * r


def _rms_bwd(dh, g, r, xh):
    dxh = dh * g
    return r * (dxh - xh * jnp.mean(dxh * xh, axis=-1, keepdims=True))


def _merge_out(x, proj, y_pool, y_rnn, w_o, norm_ffn, exchange=None, exchange_operands=(), tm=512):
    S = x.shape[0]
    GL0 = (DP + 2 * DR) // 512

    def gl_spec(k):
        return pl.BlockSpec((tm, 512), lambda i: (i, GL0 + k))

    def body(x_ref, gl0, gl1, gl2, gl3, yp_ref, yr_ref, wo_ref, gf_ref, mix_ref, x2_ref, h2_ref):
        s_p = _sigmoid(jnp.concatenate([gl0[...], gl1[...]], axis=1))
        s_r = _sigmoid(jnp.concatenate([gl2[...], gl3[...]], axis=1))
        mix = (s_p * yp_ref[...] + s_r * yr_ref[...]).astype(BF16)
        mix_ref[...] = mix
        x2 = x_ref[...] + _dot(mix, wo_ref[...])
        x2_ref[...] = x2
        _, xh2 = _rms(x2)
        h2_ref[...] = (xh2 * gf_ref[...]).astype(BF16)

    return _call(
        body, "merge_out", (S // tm,),
        in_specs=[_rows(D, tm), gl_spec(0), gl_spec(1), gl_spec(2), gl_spec(3), _rows(D, tm), _rows(D, tm),
                  _resident((D, D)), _resident((1, D))],
        out_specs=[_rows(D, tm), _rows(D, tm), _rows(D, tm)],
        out_shape=[jax.ShapeDtypeStruct((S, D), BF16), jax.ShapeDtypeStruct((S, D), F32), jax.ShapeDtypeStruct((S, D), BF16)],
        operands=(x, proj, proj, proj, proj, y_pool, y_rnn, w_o, norm_ffn),
        exchange=exchange, exchange_operands=exchange_operands)


def _ffn_up(h2, w_lo, w_hi, exchange=None, exchange_operands=(), tm=512):
    S = h2.shape[0]
    HALF = D // 2

    def body(h_ref, lo_ref, hi_ref, gu_ref, act_ref):
        h_lo, h_hi = h_ref[:, 0:HALF], h_ref[:, HALF:D]
        for c0 in range(0, DFF, FF_CHUNK):
            gate = _dot_nt(h_lo, lo_ref[c0:c0 + FF_CHUNK, :]) + _dot_nt(h_hi, hi_ref[c0:c0 + FF_CHUNK, :])
            up = (_dot_nt(h_lo, lo_ref[DFF + c0:DFF + c0 + FF_CHUNK, :])
                  + _dot_nt(h_hi, hi_ref[DFF + c0:DFF + c0 + FF_CHUNK, :]))
            gu_ref[:, c0:c0 + FF_CHUNK] = gate.astype(BF16)
            gu_ref[:, DFF + c0:DFF + c0 + FF_CHUNK] = up.astype(BF16)
            act_ref[:, c0:c0 + FF_CHUNK] = (gate * _sigmoid(gate) * up).astype(BF16)

    return _call(
        body, "ffn_up", (S // tm,),
        in_specs=[_rows(D, tm), _resident((2 * DFF, HALF)), _resident((2 * DFF, HALF))],
        out_specs=[_rows(2 * DFF, tm), _rows(DFF, tm)],
        out_shape=[jax.ShapeDtypeStruct((S, 2 * DFF), BF16), jax.ShapeDtypeStruct((S, DFF), BF16)],
        operands=(h2, w_lo, w_hi), exchange=exchange, exchange_operands=exchange_operands)


def _ffn_down_loss(act, x2, target, w_ffn_out, norm_final, tm=512):
    S = act.shape[0]

    def body(act_ref, x2_ref, t_ref, w_ref, gn_ref, dx3_ref, dx3b_ref, loss_ref, dvec_ref):
        i = pl.program_id(0)

        @pl.when(i == 0)
        def _():
            loss_ref[...] = jnp.zeros_like(loss_ref)
            dvec_ref[...] = jnp.zeros_like(dvec_ref)

        x3 = x2_ref[...] + _dot(act_ref[...], w_ref[...])
        r3, xh3 = _rms(x3)
        g_fin = gn_ref[...]
        e = xh3 * g_fin - t_ref[...]
        loss_ref[...] += jnp.sum(e * e, axis=(0, 1), keepdims=True) * (0.5 / D)
        dy = e * (1.0 / D)
        dvec_ref[0:1, :] += jnp.sum(dy * xh3, axis=0, keepdims=True)
        dx3 = _rms_bwd(dy, g_fin, r3, xh3)
        dx3_ref[...] = dx3
        dx3b_ref[...] = dx3.astype(BF16)

    return pl.pallas_call(
        body, name="ffn_down_loss", grid=(S // tm,),
        in_specs=[_rows(DFF, tm), _rows(D, tm), _rows(D, tm), _resident((DFF, D)), _resident((1, D))],
        out_specs=[_rows(D, tm), _rows(D, tm), _resident((1, 1)), _resident((8, D))],
        out_shape=[jax.ShapeDtypeStruct((S, D), F32), jax.ShapeDtypeStruct((S, D), BF16),
                   jax.ShapeDtypeStruct((1, 1), F32), jax.ShapeDtypeStruct((8, D), F32)],
        compiler_params=_params("arbitrary"),
    )(act, x2, target, w_ffn_out, norm_final)


def _ffn_bwd_down(dx3b, gu, w_ffn_out, tm=512):
    S = dx3b.shape[0]

    def body(d_ref, gu_ref, w_ref, dgu_ref):
        d = d_ref[...]
        for c0 in range(0, DFF, FF_CHUNK):
            dact = _dot_nt(d, w_ref[c0:c0 + FF_CHUNK, :])
            gate = gu_ref[:, c0:c0 + FF_CHUNK].astype(F32)
            up = gu_ref[:, DFF + c0:DFF + c0 + FF_CHUNK].astype(F32)
            sg = _sigmoid(gate)
            dgu_ref[:, c0:c0 + FF_CHUNK] = (dact * up * (sg * (1.0 + gate * (1.0 - sg)))).astype(BF16)
            dgu_ref[:, DFF + c0:DFF + c0 + FF_CHUNK] = (dact * (gate * sg)).astype(BF16)

    return pl.pallas_call(
        body, name="ffn_bwd_down", grid=(S // tm,),
        in_specs=[_rows(D, tm), _rows(2 * DFF, tm), _resident((DFF, D))],
        out_specs=_rows(2 * DFF, tm),
        out_shape=jax.ShapeDtypeStruct((S, 2 * DFF), BF16),
        compiler_params=_params("parallel"),
    )(dx3b, gu, w_ffn_out)


def _ffn_bwd_up(dgu, x2, dx3, w_lo, w_hi, norm_ffn, w_o, tm=512):
    S = dgu.shape[0]
    HALF = D // 2

    def body(dgu_ref, x2_ref, dx3_ref, lo_ref, hi_ref, gf_ref, wo_ref, dx2_ref, dx2b_ref, dmixo_ref, dvec_ref):
        i = pl.program_id(0)

        @pl.when(i == 0)
        def _():
            dvec_ref[...] = jnp.zeros_like(dvec_ref)

        dgate, dup = dgu_ref[:, 0:DFF], dgu_ref[:, DFF:2 * DFF]
        dh2 = jnp.concatenate([_dot(dgate, w[0:DFF, :]) + _dot(dup, w[DFF:2 * DFF, :]) for w in (lo_ref, hi_ref)], axis=1)
        r2, xh2 = _rms(x2_ref[...])
        dvec_ref[0:1, :] += jnp.sum(dh2 * xh2, axis=0, keepdims=True)
        dx2 = dx3_ref[...] + _rms_bwd(dh2, gf_ref[...], r2, xh2)
        dx2_ref[...] = dx2
        dx2b = dx2.astype(BF16)
        dx2b_ref[...] = dx2b
        dmixo_ref[...] = _dot_nt(dx2b, wo_ref[...])

    return pl.pallas_call(
        body, name="ffn_bwd_up", grid=(S // tm,),
        in_specs=[_rows(2 * DFF, tm), _rows(D, tm), _rows(D, tm), _resident((2 * DFF, HALF)), _resident((2 * DFF, HALF)),
                  _resident((1, D)), _resident((D, D))],
        out_specs=[_rows(D, tm), _rows(D, tm), _rows(D, tm), _resident((8, D))],
        out_shape=[jax.ShapeDtypeStruct((S, D), F32), jax.ShapeDtypeStruct((S, D), BF16), jax.ShapeDtypeStruct((S, D), F32),
                   jax.ShapeDtypeStruct((8, D), F32)],
        compiler_params=_params("arbitrary"),
    )(dgu, x2, dx3, w_lo, w_hi, norm_ffn, w_o)


VEC_ROWS = 16
MAT_WA = 4 * PG
MAT_WX = MAT_WA + NH * HD
MAT_ROWS = MAT_WX + NH * HD


def _mixer_bwd(proj, dmixo, y_pool, y_rnn, hr, wg, scale, w_pool_out, conv_w, conv_b, wa, ba, wx, bx, lam, w_rnn_out,
               exchange=None, exchange_operands=(), tm=256):
    S = proj.shape[0]
    nt = S // tm

    def rev(cols):
        return pl.BlockSpec((tm, cols), lambda i: (nt - 1 - i, 0))

    def halo(rows_, cols):
        per = tm // rows_
        return pl.BlockSpec((rows_, cols), lambda i: (jnp.maximum((nt - 1 - i) * per - 1, 0), 0))

    def body(proj_ref, projh_ref, dmixo_ref, yp_ref, yr_ref, hr_ref, hrh_ref, wg_ref, scale_ref, wpo_ref, cw_ref, cb_ref,
             wa_ref, ba_ref, wx_ref, bx_ref, lam_ref, wro_ref,
             dproj_ref, dypb_ref, dyrb_ref, dmat_ref, dvec_ref,
             q_carry, dv_carry, a_carry, g_carry, g_scr):
        i = pl.program_id(0)
        ti = nt - 1 - i

        @pl.when(i == 0)
        def _():
            q_carry[...] = jnp.zeros_like(q_carry)
            dv_carry[...] = jnp.zeros_like(dv_carry)
            a_carry[...] = jnp.zeros_like(a_carry)
            g_carry[...] = jnp.zeros_like(g_carry)
            dmat_ref[...] = jnp.zeros_like(dmat_ref)
            dvec_ref[...] = jnp.zeros_like(dvec_ref)

        rows = lax.broadcasted_iota(jnp.int32, (tm, 1), 0)
        t_glob = ti * tm + rows
        has_prev = (ti > 0).astype(F32)
        dmixo = dmixo_ref[...]

        s_p = _sigmoid(proj_ref[:, DP + 2 * DR:DP + 2 * DR + D])
        s_r = _sigmoid(proj_ref[:, DP + 2 * DR + D:DIN])
        dproj_ref[:, DP + 2 * DR:DP + 2 * DR + D] = (dmixo * yp_ref[...] * s_p * (1.0 - s_p)).astype(BF16)
        dproj_ref[:, DP + 2 * DR + D:DIN] = (dmixo * yr_ref[...] * s_r * (1.0 - s_r)).astype(BF16)
        dyp = (dmixo * s_p).astype(BF16)
        dyr = (dmixo * s_r).astype(BF16)
        dypb_ref[...] = dyp
        dyrb_ref[...] = dyr

        dz = _dot_nt(dyr, wro_ref[...])
        u_gate = proj_ref[:, DP + DR:DP + 2 * DR]
        gg, dgelu = _gelu(u_gate, with_grad=True)
        hr_t = hr_ref[...]
        dproj_ref[:, DP + DR:DP + 2 * DR] = (dz * hr_t * dgelu).astype(BF16)
        dhr = dz * gg

        u_rnn = proj_ref[:, DP:DP + DR]
        uext = jnp.concatenate([projh_ref[POOL_HALO - CONV_HALO:, DP:DP + DR] * has_prev, u_rnn], axis=0)
        taps = _conv_taps(uext)
        v = cb_ref[...]
        for k in range(4):
            v = v + taps[k] * cw_ref[k:k + 1, :]
        sp = _softplus_neg(lam_ref[...])
        r, gi, a, mult, inv_mult = _gates(v, wa_ref, ba_ref, wx_ref, bx_ref, sp)

        C = jnp.where(rows == tm - 1, a_carry[0:1, :], pltpu.roll(a, tm - 1, axis=0))
        g_carry[0:1, :] = _linear_scan(g_scr, C, dhr, g_carry[0:1, :], reverse=True)
        a_carry[0:1, :] = a[0:1, :]
        g = g_scr[...]

        h_prev = jnp.where(rows == 0, hrh_ref[7:8, :] * has_prev, pltpu.roll(hr_t, 1, axis=0))
        da = g * h_prev
        gm = g * mult
        dmult = g * gi * v
        di = gm * v
        dv = gm * gi
        dlog_a = da * a - dmult * (a * a * inv_mult)
        dvec_ref[4:5, :] += jnp.sum(dlog_a * r, axis=0, keepdims=True)
        dra = (dlog_a * ((-LRU_C) * sp) * r * (1.0 - r))
        drx = di * gi * (1.0 - gi)
        dvec_ref[2:3, :] += jnp.sum(dra, axis=0, keepdims=True)
        dvec_ref[3:4, :] += jnp.sum(drx, axis=0, keepdims=True)
        drab = dra.astype(BF16)
        drxb = drx.astype(BF16)
        vb = v.astype(BF16)
        dvg = []
        for h in range(NH):
            sl = slice(h * HD, (h + 1) * HD)
            dvg.append(_dot_nt(drab[:, sl], wa_ref[h]) + _dot_nt(drxb[:, sl], wx_ref[h]))
            dmat_ref[MAT_WA + h * HD:MAT_WA + (h + 1) * HD, :] += _dot_tn(vb[:, sl], drab[:, sl])
            dmat_ref[MAT_WX + h * HD:MAT_WX + (h + 1) * HD, :] += _dot_tn(vb[:, sl], drxb[:, sl])
        dv = dv + jnp.concatenate(dvg, axis=1)
        dvec_ref[1:2, :] += jnp.sum(dv, axis=0, keepdims=True)
        for k in range(4):
            dvec_ref[5 + k:6 + k, :] += jnp.sum(dv * taps[k], axis=0, keepdims=True)
        dvext = jnp.concatenate([dv, dv_carry[...]], axis=0)
        dv_carry[...] = dv[0:CONV_HALO, :]
        n = tm + CONV_HALO
        du_rnn = dv * cw_ref[3:4, :]
        for k in range(3):
            du_rnn = du_rnn + pltpu.roll(dvext, n - (3 - k), axis=0)[0:tm, :] * cw_ref[k:k + 1, :]
        dproj_ref[:, DP:DP + DR] = du_rnn.astype(BF16)

        dpm = _dot_nt(dyp, wpo_ref[...])
        u_pool = proj_ref[:, 0:DP]
        ext = jnp.concatenate([projh_ref[:, 0:DP] * has_prev, u_pool], axis=0)
        sums = _pool_windows(ext, +1)
        scale_v = scale_ref[...]
        qs = []
        dpooled = []
        dscale = []
        for gi_, w in enumerate(WINDOWS):
            sl = slice(gi_ * PG, (gi_ + 1) * PG)
            inv_cnt = 1.0 / jnp.minimum(t_glob + 1, w).astype(F32)
            pooled_b = (sums[gi_][POOL_HALO:, :] * inv_cnt - u_pool[:, sl]).astype(BF16)
            mixed_g = _dot(pooled_b, wg_ref[gi_])
            dscale.append(jnp.sum(dpm[:, sl] * mixed_g, axis=0, keepdims=True))
            dmixed_b = (dpm[:, sl] * scale_v[:, sl]).astype(BF16)
            dmat_ref[gi_ * PG:(gi_ + 1) * PG, :] += _dot_tn(pooled_b, dmixed_b)
            dp_g = _dot_nt(dmixed_b, wg_ref[gi_])
            dpooled.append(dp_g)
            qs.append(dp_g * inv_cnt)
        dvec_ref[0:1, 0:DP] += jnp.concatenate(dscale, axis=1)
        q = jnp.concatenate(qs, axis=1)
        qext = jnp.concatenate([q, q_carry[...]], axis=0)
        q_carry[...] = q[0:POOL_HALO, :]
        tsum = _pool_windows(qext, -1)
        for gi_ in range(4):
            dproj_ref[:, gi_ * PG:(gi_ + 1) * PG] = (tsum[gi_][0:tm, :] - dpooled[gi_]).astype(BF16)

        @pl.when(i == nt - 1)
        def _():
            dvec_ref[4:5, :] = dvec_ref[4:5, :] * (LRU_C * _sigmoid(-lam_ref[...]))

    return _call(
        body, "mixer_bwd", (nt,),
        in_specs=[rev(DIN), halo(POOL_HALO, DIN), rev(D), rev(D), rev(D), rev(DR), halo(8, DR),
                  _resident((4, PG, PG)), _resident((1, DP)), _resident((DP, D)), _resident((4, DR)), _resident((1, DR)),
                  _resident((NH, HD, HD)), _resident((1, DR)), _resident((NH, HD, HD)), _resident((1, DR)),
                  _resident((1, DR)), _resident((DR, D))],
        out_specs=[rev(DIN), rev(D), rev(D), _resident((MAT_ROWS, HD)), _resident((VEC_ROWS, DR))],
        out_shape=[jax.ShapeDtypeStruct((S, DIN), BF16), jax.ShapeDtypeStruct((S, D), BF16),
                   jax.ShapeDtypeStruct((S, D), BF16), jax.ShapeDtypeStruct((MAT_ROWS, HD), F32),
                   jax.ShapeDtypeStruct((VEC_ROWS, DR), F32)],
        scratch_shapes=[pltpu.VMEM((POOL_HALO, DP), F32), pltpu.VMEM((CONV_HALO, DR), F32), pltpu.VMEM((8, DR), F32),
                        pltpu.VMEM((8, DR), F32), pltpu.VMEM((tm, DR), F32)],
        operands=(proj, proj, dmixo, y_pool, y_rnn, hr, hr, wg, scale, w_pool_out, conv_w, conv_b, wa, ba, wx, bx, lam,
                  w_rnn_out),
        exchange=exchange, exchange_operands=exchange_operands)


def _in_bwd(dproj, x, dx2, norm_mix, w_in, exchange=None, exchange_operands=(), tm=512):
    S = x.shape[0]

    def body(dp_ref, x_ref, dx2_ref, g_ref, w_ref, dx_ref, dg_ref):
        i = pl.program_id(0)

        @pl.when(i == 0)
        def _():
            dg_ref[...] = jnp.zeros_like(dg_ref)

        dh = _dot(dp_ref[:, 0:1536], w_ref[0:1536, :])
        dh = dh + _dot(dp_ref[:, 1536:3072], w_ref[1536:3072, :])
        dh = dh + _dot(dp_ref[:, 3072:DIN], w_ref[3072:DIN, :])
        xv = x_ref[...]
        r = lax.rsqrt(jnp.mean(xv * xv, axis=-1, keepdims=True) + EPS)
        xh = xv * r
        dg_ref[0:1, :] += jnp.sum(dh * xh, axis=0, keepdims=True)
        dxh = dh * g_ref[...]
        dx_ref[...] = dx2_ref[...] + r * (dxh - xh * jnp.mean(dxh * xh, axis=-1, keepdims=True))

    return _call(
        body, "in_bwd", (S // tm,),
        in_specs=[_rows(DIN, tm), _rows(D, tm), _rows(D, tm), _resident((1, D)), _resident((DIN, D))],
        out_specs=[_rows(D, tm), _resident((8, D))],
        out_shape=[jax.ShapeDtypeStruct((S, D), F32), jax.ShapeDtypeStruct((8, D), F32)],
        operands=(dproj, x, dx2, norm_mix, w_in), exchange=exchange, exchange_operands=exchange_operands)


def _wgrad(a, b, name, tk, tn, exchange=None, exchange_operands=()):
    S, K = a.shape
    N = b.shape[1]

    def body(a_ref, b_ref, o_ref):
        o_ref[...] = _dot_tn(a_ref[...], b_ref[...]).astype(BF16)

    (out,), exchanged = _call(
        body, name, (K // tk, N // tn),
        in_specs=[pl.BlockSpec((S, tk), lambda k, n: (0, k)), pl.BlockSpec((S, tn), lambda k, n: (0, n))],
        out_specs=[pl.BlockSpec((tk, tn), lambda k, n: (k, n))],
        out_shape=[jax.ShapeDtypeStruct((K, N), BF16)],
        operands=(a, b), exchange=exchange, exchange_operands=exchange_operands)
    return (out, exchanged) if exchange is not None else out


VEC_SCALE, VEC_CONV_B, VEC_BA, VEC_BX, VEC_LAM, VEC_CONV_W, VEC_NORM_FINAL, VEC_NORM_FFN = 0, 1, 2, 3, 4, 5, 9, 10
VEC_LOSS = 11


class _Big:
    def __init__(self, name, rows, cols, axis, n, dtype=BF16, transposed=False):
        self.name, self.rows, self.cols, self.axis, self.n, self.dtype = name, rows, cols, axis, n, dtype
        self.transposed = transposed
        self.block_shape = (rows, n) if axis == 1 else (n, cols)

    def block(self, ref, p):
        if self.axis == 1:
            return ref.at[:, pl.ds(pl.multiple_of(p * self.n, 128), self.n)]
        return ref.at[pl.ds(pl.multiple_of(p * self.n, 16 if self.dtype == BF16 else 8), self.n), :]

    def block_index(self, p):
        return (0, p) if self.axis == 1 else (p, 0)


BIG = (_Big("w_in", DIN, D, 0, DIN // 8, transposed=True), _Big("w_pool_out", DP, D, 1, D // 8),
       _Big("w_rnn_out", DR, D, 0, DR // 8), _Big("w_o", D, D, 0, D // 8),
       _Big("w_ffn_in", 2 * DFF, D, 0, 2 * DFF // 8, transposed=True), _Big("w_ffn_out", DFF, D, 0, DFF // 8))
CONV_W = _Big("conv_w", 8, DR, 1, DR // 8, F32)
W_FFN_IN_HALVES = (_Big("w_ffn_in_lo", 2 * DFF, D // 2, 0, 2 * DFF // 8), _Big("w_ffn_in_hi", 2 * DFF, D // 2, 0, 2 * DFF // 8))
GATHERED = BIG + (CONV_W,) + W_FFN_IN_HALVES

HBM_SPEC = pl.BlockSpec(memory_space=pl.ANY)
VMEM_SPEC = pl.BlockSpec(memory_space=pltpu.VMEM)


def _place():
    x, y, c = (lax.axis_index(a) for a in MESH_AXES)
    other_chips = [(1 - x, y), (x, 1 - y), (1 - x, 1 - y)]
    return x, y, c, other_chips


def _remote(src, dst, send_sems, recv_sems, idx, to):
    return pltpu.make_async_remote_copy(src_ref=src, dst_ref=dst, send_sem=send_sems.at[idx], recv_sem=recv_sems.at[idx],
                                        device_id=to, device_id_type=MESH)


def _device_index(chip, core):
    return 4 * chip[0] + 2 * chip[1] + core


class _Gather:
    def __init__(self, tensors):
        self.tensors = tuple(tensors)
        n = len(self.tensors)
        self.in_specs = [HBM_SPEC] * n
        self.out_specs = [HBM_SPEC] * n
        self.out_shape = [jax.ShapeDtypeStruct((T.rows, T.cols), T.dtype) for T in self.tensors]
        self.scratch_shapes = [pltpu.VMEM(T.block_shape, T.dtype) for T in self.tensors] + [
            pltpu.SemaphoreType.DMA((n, 7)), pltpu.SemaphoreType.DMA((n, 7)), pltpu.SemaphoreType.DMA((n, 2))]

    def middle_at(self, steps):
        return (3 * steps) // 4

    def _copies(self, ins, outs, scratch):
        n = len(self.tensors)
        mine, (send_sems, recv_sems, loc_sems) = scratch[:n], scratch[n:]
        x, y, c, chips = _place()
        sibling = (x, y, 1 - c)
        me = _device_index((x, y), c)
        loads, stores, first, passed, arrivals, late = [], [], [], [], [], []
        for t, T in enumerate(self.tensors):
            place = T.block(outs[t], me)
            loads.append(pltpu.make_async_copy(ins[t], mine[t], loc_sems.at[t, 0]))
            stores.append(pltpu.make_async_copy(mine[t], place, loc_sems.at[t, 1]))
            first.append(_remote(mine[t], place, send_sems, recv_sems, (t, 0), sibling))
            theirs = T.block(outs[t], _device_index((x, y), 1 - c))
            late.append(_remote(theirs, theirs, send_sems, recv_sems, (t, 0), sibling))
            for k, chip in enumerate(chips):
                first.append(_remote(mine[t], place, send_sems, recv_sems, (t, 1 + k), (*chip, c)))
                land = T.block(outs[t], _device_index(chip, c))
                arrivals.append(_remote(land, land, send_sems, recv_sems, (t, 1 + k), sibling))
                passed.append(_remote(land, land, send_sems, recv_sems, (t, 4 + k), sibling))
                theirs = T.block(outs[t], _device_index(chip, 1 - c))
                late.append(_remote(theirs, theirs, send_sems, recv_sems, (t, 4 + k), sibling))
        return loads, stores, first, passed, arrivals, late

    def start(self, ins, outs, scratch):
        loads, stores, first, _, _, _ = self._copies(ins, outs, scratch)
        for cp in loads:
            cp.start()
        for cp in loads:
            cp.wait()
        for cp in stores + first:
            cp.start()

    def middle(self, ins, outs, scratch):
        _, _, _, passed, arrivals, _ = self._copies(ins, outs, scratch)
        for arrived, cp in zip(arrivals, passed):
            arrived.wait_recv()
            cp.start()

    def finish(self, ins, outs, scratch):
        _, stores, first, passed, _, late = self._copies(ins, outs, scratch)
        for cp in late:
            cp.wait_recv()
        for cp in first + passed:
            cp.wait_send()
        for cp in stores:
            cp.wait()


def _all_gather(blocks, tensors, name):
    gather = _Gather(tensors)

    def body(*refs):
        n = len(gather.tensors)
        ins, outs, scratch = refs[:n], refs[n:2 * n], refs[2 * n:]
        gather.start(ins, outs, scratch)
        gather.middle(ins, outs, scratch)
        gather.finish(ins, outs, scratch)

    return pl.pallas_call(
        body, name=name, in_specs=gather.in_specs, out_specs=gather.out_specs, out_shape=gather.out_shape,
        scratch_shapes=gather.scratch_shapes, compiler_params=pltpu.CompilerParams(vmem_limit_bytes=VMEM_LIMIT),
    )(*blocks)


PAIR_ROWS = 32


def _pair_reduce(grads, tensors, name):
    nt = len(tensors)

    def body(*refs):
        ins, own_out, sums_out, landed, mine = (refs[k * nt:(k + 1) * nt] for k in range(5))
        send_sems, recv_sems, loc_sems = refs[5 * nt:]
        x, y, c, chips = _place()
        chip_of = [2 * chip[0] + chip[1] for chip in chips]
        swaps, loads = [], []
        for t, T in enumerate(tensors):
            for j in range(4):
                swaps.append(_remote(T.block(ins[t], 2 * j + 1 - c), landed[t].at[j], send_sems, recv_sems, (t, j),
                                     (x, y, 1 - c)))
            for k in range(3):
                loads.append(pltpu.make_async_copy(T.block(ins[t], 2 * chip_of[k] + c), mine[t].at[k], loc_sems.at[t, k]))
        for cp in swaps + loads:
            cp.start()
        for cp in loads:
            cp.wait()
        for cp in swaps:
            cp.wait_recv()
        stores = []
        for t, T in enumerate(tensors):
            for k in range(3):
                acc, got = mine[t].at[k], landed[t].at[chip_of[k]]

                def add(i, carry, acc=acc, got=got):
                    rows = pl.ds(pl.multiple_of(i * PAIR_ROWS, PAIR_ROWS), PAIR_ROWS)
                    acc[rows, :] = (acc[rows, :].astype(F32) + got[rows, :].astype(F32)).astype(BF16)
                    return carry

                lax.fori_loop(0, T.block_shape[0] // PAIR_ROWS, add, 0)
            stores.append(pltpu.make_async_copy(mine[t], sums_out[t], loc_sems.at[t, 3]))
            stores.append(pltpu.make_async_copy(landed[t].at[2 * x + y], own_out[t], loc_sems.at[t, 4]))
        for cp in stores:
            cp.start()
        for cp in swaps:
            cp.wait_send()
        for cp in stores:
            cp.wait()

    blocks = [T.block_shape for T in tensors]
    return pl.pallas_call(
        body, name=name,
        in_specs=[HBM_SPEC] * nt, out_specs=[HBM_SPEC] * (2 * nt),
        out_shape=[jax.ShapeDtypeStruct(b, BF16) for b in blocks] + [jax.ShapeDtypeStruct((3,) + b, BF16) for b in blocks],
        scratch_shapes=[pltpu.VMEM((4,) + b, BF16) for b in blocks] + [pltpu.VMEM((3,) + b, BF16) for b in blocks]
        + [pltpu.SemaphoreType.DMA((nt, 4)), pltpu.SemaphoreType.DMA((nt, 4)), pltpu.SemaphoreType.DMA((nt, 5))],
        compiler_params=pltpu.CompilerParams(vmem_limit_bytes=VMEM_LIMIT),
    )(*grads)


class _Scatter:
    middle = None

    def __init__(self, tensors):
        n = len(tensors)
        self.in_specs = [HBM_SPEC] * n
        self.out_specs = [HBM_SPEC] * n
        self.out_shape = [jax.ShapeDtypeStruct((3,) + T.block_shape, BF16) for T in tensors]
        self.scratch_shapes = [pltpu.SemaphoreType.DMA((n, 3)), pltpu.SemaphoreType.DMA((n, 3))]

    def _copies(self, ins, outs, scratch):
        send_sems, recv_sems = scratch
        x, y, c, chips = _place()
        return [_remote(ins[t].at[k], outs[t].at[k], send_sems, recv_sems, (t, k), (*chip, c))
                for t in range(len(ins)) for k, chip in enumerate(chips)]

    def start(self, ins, outs, scratch):
        for cp in self._copies(ins, outs, scratch):
            cp.start()

    def finish(self, ins, outs, scratch):
        for cp in self._copies(ins, outs, scratch):
            cp.wait()


def _chip_scatter(sums, tensors, name):
    scatter = _Scatter(tensors)
    n = len(tensors)

    def body(*refs):
        ins, outs, scratch = refs[:n], refs[n:2 * n], refs[2 * n:]
        scatter.start(ins, outs, scratch)
        scatter.finish(ins, outs, scratch)

    return pl.pallas_call(
        body, name=name, in_specs=scatter.in_specs, out_specs=scatter.out_specs, out_shape=scatter.out_shape,
        scratch_shapes=scatter.scratch_shapes,
    )(*sums)


def _adamw(w, g, m, v):
    m = ADAM_B1 * m + (1.0 - ADAM_B1) * g
    v = ADAM_B2 * v + (1.0 - ADAM_B2) * (g * g)
    m_hat = m / (1.0 - ADAM_B1 ** ADAM_STEP)
    v_hat = v / (1.0 - ADAM_B2 ** ADAM_STEP)
    delta = -ADAM_LR * (m_hat / (jnp.sqrt(v_hat) + ADAM_EPS) + ADAM_WD * w)
    return delta, m, v


def _final_sum(T, g, lz1, lz2, where, w, m, v):
    rows, cols = T.block_shape
    sub = 4 if T.axis == 0 and rows % 64 == 0 and rows > 256 else 1
    blk = (rows // sub, cols)

    def body(where_ref, g_ref, l1_ref, l2_ref, w_ref, m_ref, v_ref, g_out, d_out, m_out, v_out):
        tot = g_ref[...].astype(F32) + l1_ref[...].astype(F32)
        for k in range(3):
            tot = tot + l2_ref[k].astype(F32)
        g_out[...] = tot
        d_out[...], m_out[...], v_out[...] = _adamw(w_ref[...], tot, m_ref[...], v_ref[...])

    def in_whole(r, wh):
        p = wh[0]
        return (0, p) if T.axis == 1 else (p * sub + r, 0)

    own = pl.BlockSpec(blk, lambda r, wh: (r, 0))
    return pl.pallas_call(
        body, name="grad_final_" + T.name,
        grid_spec=pltpu.PrefetchScalarGridSpec(
            num_scalar_prefetch=1, grid=(sub,),
            in_specs=[pl.BlockSpec(blk, in_whole),
                      own,
                      pl.BlockSpec((3,) + blk, lambda r, wh: (0, r, 0)), own, own, own],
            out_specs=[own] * 4),
        out_shape=[jax.ShapeDtypeStruct(T.block_shape, F32)] * 4,
        compiler_params=_params("arbitrary"),
    )(where, g, lz1, lz2, w, m, v)


MAT_PIECE = MAT_ROWS // 8
VEC_PIECE = DR // 8


class _AllReduce:
    def __init__(self, items):
        self.items = tuple(items)
        n = len(self.items)
        self.in_specs = [HBM_SPEC] * n
        self.out_specs = [HBM_SPEC] * n
        self.out_shape = [jax.ShapeDtypeStruct(shape, F32) for shape, _ in self.items]
        pieces = [(shape[0] // 8, shape[1]) if axis == 0 else (shape[0], shape[1] // 8) for shape, axis in self.items]
        self.scratch_shapes = ([pltpu.VMEM((8,) + p, F32) for p in pieces] + [pltpu.VMEM(p, F32) for p in pieces] + [
            pltpu.SemaphoreType.DMA((2 * n, 8)), pltpu.SemaphoreType.DMA((2 * n, 8)), pltpu.SemaphoreType.DMA((2 * n,))])

    def middle_at(self, steps):
        return steps // 2

    def _copies(self, ins, outs, scratch):
        n = len(self.items)
        landed, sums, (send_sems, recv_sems, loc_sems) = scratch[:n], scratch[n:2 * n], scratch[2 * n:]
        x, y, c, _ = _place()
        me = _device_index((x, y), c)

        def peer(r):
            return (1 - x if r & 4 else x, 1 - y if r & 2 else y, 1 - c if r & 1 else c)

        def piece(i, ref, p):
            shape, axis = self.items[i]
            if axis == 0:
                rows = shape[0] // 8
                return ref.at[pl.ds(pl.multiple_of(p * rows, 8), rows), :]
            cols = shape[1] // 8
            return ref.at[:, pl.ds(pl.multiple_of(p * cols, 128), cols)]

        own, scatter, arrivals, keep, spread, late = [], [], [], [], [], []
        for i in range(n):
            own.append(pltpu.make_async_copy(piece(i, ins[i], me), landed[i].at[0], loc_sems.at[2 * i]))
            keep.append(pltpu.make_async_copy(sums[i], piece(i, outs[i], me), loc_sems.at[2 * i + 1]))
            for r in range(1, 8):
                to = peer(r)
                p = _device_index(to[:2], to[2])
                scatter.append(_remote(piece(i, ins[i], p), landed[i].at[r], send_sems, recv_sems, (2 * i, r), to))
                spread.append(_remote(sums[i], piece(i, outs[i], me), send_sems, recv_sems, (2 * i + 1, r), to))
                late.append(_remote(sums[i], piece(i, outs[i], p), send_sems, recv_sems, (2 * i + 1, r), to))
        return own, scatter, keep, spread, late, landed, sums

    def start(self, ins, outs, scratch):
        own, scatter, _, _, _, _, _ = self._copies(ins, outs, scratch)
        for cp in own + scatter:
            cp.start()

    def middle(self, ins, outs, scratch):
        own, scatter, keep, spread, _, landed, sums = self._copies(ins, outs, scratch)
        for cp in own:
            cp.wait()
        for cp in scatter:
            cp.wait_recv()
        for i in range(len(self.items)):
            total = landed[i][0]
            for r in range(1, 8):
                total = total + landed[i][r]
            sums[i][...] = total
        for cp in keep + spread:
            cp.start()

    def finish(self, ins, outs, scratch):
        _, scatter, keep, spread, late, _, _ = self._copies(ins, outs, scratch)
        for cp in late:
            cp.wait_recv()
        for cp in scatter + spread:
            cp.wait_send()
        for cp in keep:
            cp.wait()


class _Both:
    def __init__(self, a, b):
        self.a, self.b = a, b
        self.in_specs, self.out_specs = a.in_specs + b.in_specs, a.out_specs + b.out_specs
        self.out_shape, self.scratch_shapes = a.out_shape + b.out_shape, a.scratch_shapes + b.scratch_shapes

    def middle_at(self, steps):
        return min(e.middle_at(steps) for e in (self.a, self.b) if e.middle is not None)

    def _each(self, ins, outs, scratch):
        a = self.a
        i, o, s = len(a.in_specs), len(a.out_specs), len(a.scratch_shapes)
        return (a, ins[:i], outs[:o], scratch[:s]), (self.b, ins[i:], outs[o:], scratch[s:])

    def start(self, ins, outs, scratch):
        for e, i, o, s in self._each(ins, outs, scratch):
            e.start(i, o, s)

    def middle(self, ins, outs, scratch):
        for e, i, o, s in self._each(ins, outs, scratch):
            if e.middle is not None:
                e.middle(i, o, s)

    def finish(self, ins, outs, scratch):
        for e, i, o, s in self._each(ins, outs, scratch):
            e.finish(i, o, s)


def _all_reduce(arrays, items, name):
    reduce = _AllReduce(items)
    n = len(items)

    def body(*refs):
        ins, outs, scratch = refs[:n], refs[n:2 * n], refs[2 * n:]
        reduce.start(ins, outs, scratch)
        reduce.middle(ins, outs, scratch)
        reduce.finish(ins, outs, scratch)

    return pl.pallas_call(
        body, name=name, in_specs=reduce.in_specs, out_specs=reduce.out_specs, out_shape=reduce.out_shape,
        scratch_shapes=reduce.scratch_shapes,
    )(*arrays)


def _adam_small(grads, wmv):
    n = len(grads)

    def body(*refs):
        g_refs, rest = refs[:n], refs[n:]
        ins, outs = rest[:3 * n], rest[3 * n:]
        for i in range(n):
            d, m, v = _adamw(ins[3 * i][...], g_refs[i][...], ins[3 * i + 1][...], ins[3 * i + 2][...])
            outs[3 * i][...], outs[3 * i + 1][...], outs[3 * i + 2][...] = d, m, v

    flat = [a for t in wmv for a in t]
    return pl.pallas_call(
        body, name="adam_small",
        in_specs=[VMEM_SPEC] * (4 * n), out_specs=[VMEM_SPEC] * (3 * n),
        out_shape=[jax.ShapeDtypeStruct(a.shape, F32) for a in flat],
    )(*grads, *flat)


WEIGHT_NAMES = ("norm_mix", "w_in", "w_pool_grp", "pool_scale", "w_pool_out", "conv_w", "conv_b", "w_rg_a", "b_rg_a", "w_rg_x",
                "b_rg_x", "lru_lambda", "w_rnn_out", "w_o", "norm_ffn", "w_ffn_in", "w_ffn_out", "norm_final")


def kernel(x, norm_mix, w_in, w_pool_grp, pool_scale, w_pool_out, conv_w, conv_b, w_rg_a, b_rg_a, w_rg_x, b_rg_x, lru_lambda, w_rnn_out, w_o, norm_ffn, w_ffn_in, w_ffn_out, norm_final, loss_target, m_norm_mix, m_w_in, m_w_pool_grp, m_pool_scale, m_w_pool_out, m_conv_w, m_conv_b, m_w_rg_a, m_b_rg_a, m_w_rg_x, m_b_rg_x, m_lru_lambda, m_w_rnn_out, m_w_o, m_norm_ffn, m_w_ffn_in, m_w_ffn_out, m_norm_final, v_norm_mix, v_w_in, v_w_pool_grp, v_pool_scale, v_w_pool_out, v_conv_w, v_conv_b, v_w_rg_a, v_b_rg_a, v_w_rg_x, v_b_rg_x, v_lru_lambda, v_w_rnn_out, v_w_o, v_norm_ffn, v_w_ffn_in, v_w_ffn_out, v_norm_final):
    w = dict(norm_mix=norm_mix, w_in=w_in, w_pool_grp=w_pool_grp, pool_scale=pool_scale, w_pool_out=w_pool_out, conv_w=conv_w,
             conv_b=conv_b, w_rg_a=w_rg_a, b_rg_a=b_rg_a, w_rg_x=w_rg_x, b_rg_x=b_rg_x, lru_lambda=lru_lambda,
             w_rnn_out=w_rnn_out, w_o=w_o, norm_ffn=norm_ffn, w_ffn_in=w_ffn_in, w_ffn_out=w_ffn_out, norm_final=norm_final)
    m = dict(norm_mix=m_norm_mix, w_in=m_w_in, w_pool_grp=m_w_pool_grp, pool_scale=m_pool_scale, w_pool_out=m_w_pool_out,
             conv_w=m_conv_w, conv_b=m_conv_b, w_rg_a=m_w_rg_a, b_rg_a=m_b_rg_a, w_rg_x=m_w_rg_x, b_rg_x=m_b_rg_x,
             lru_lambda=m_lru_lambda, w_rnn_out=m_w_rnn_out, w_o=m_w_o, norm_ffn=m_norm_ffn, w_ffn_in=m_w_ffn_in,
             w_ffn_out=m_w_ffn_out, norm_final=m_norm_final)
    v = dict(norm_mix=v_norm_mix, w_in=v_w_in, w_pool_grp=v_w_pool_grp, pool_scale=v_pool_scale, w_pool_out=v_w_pool_out,
             conv_w=v_conv_w, conv_b=v_conv_b, w_rg_a=v_w_rg_a, b_rg_a=v_b_rg_a, w_rg_x=v_w_rg_x, b_rg_x=v_b_rg_x,
             lru_lambda=v_lru_lambda, w_rnn_out=v_w_rnn_out, w_o=v_w_o, norm_ffn=v_norm_ffn, w_ffn_in=v_w_ffn_in,
             w_ffn_out=v_w_ffn_out, norm_final=v_norm_final)
    xi, yi, ci = (lax.axis_index(a) for a in MESH_AXES)
    chip = 2 * xi + yi

    def held(T, a):
        return jnp.swapaxes(a, 0, 1) if T.transposed else a

    where = jnp.stack([2 * chip + ci]).astype(jnp.int32)
    by_name = {T.name: T for T in GATHERED}
    block = {T.name: held(T, w[T.name][0]).astype(T.dtype) for T in BIG}
    block["conv_w"] = jnp.pad(conv_w[0], ((0, CONV_W.rows - 4), (0, 0)))
    block["w_ffn_in_lo"], block["w_ffn_in_hi"] = block["w_ffn_in"][:, :D // 2], block["w_ffn_in"][:, D // 2:]

    def gather_of(*names):
        return dict(exchange=_Gather([by_name[n] for n in names]), exchange_operands=[block[n] for n in names])

    def pair_sums(names, partials, tag):
        out = _pair_reduce(partials, [by_name[n] for n in names], "grad_pair_reduce_" + tag)
        return list(out[:len(names)]), list(out[len(names):])

    xs, target = x[0], loss_target[0]
    wg_b, wa_b, wx_b = (a[0].astype(BF16) for a in (w_pool_grp, w_rg_a, w_rg_x))
    ba2, bx2 = b_rg_a.reshape(1, DR), b_rg_x.reshape(1, DR)
    (w_in_g,) = _all_gather([block["w_in"]], [by_name["w_in"]], "all_gather_w_in")
    (proj, h1), (w_pool_out_g, w_rnn_out_g, w_o_g, conv_g) = _in_proj(
        xs, norm_mix, w_in_g, **gather_of("w_pool_out", "w_rnn_out", "w_o", "conv_w"))
    mixer_weights = (wg_b, pool_scale, w_pool_out_g, conv_g[0:4], conv_b, wa_b, ba2, wx_b, bx2, lru_lambda, w_rnn_out_g)
    (_, pm, y_pool, hr, z, y_rnn), (w_ffn_hi_g,) = _mixer_fwd(proj, *mixer_weights, **gather_of("w_ffn_in_hi"))
    (mix, x2, h2), (w_ffn_lo_g,) = _merge_out(xs, proj, y_pool, y_rnn, w_o_g, norm_ffn, **gather_of("w_ffn_in_lo"))
    (gu, act), (w_ffn_out_g,) = _ffn_up(h2, w_ffn_lo_g, w_ffn_hi_g, **gather_of("w_ffn_out"))
    dx3, dx3b, loss_part, dvec_fin = _ffn_down_loss(act, x2, target, w_ffn_out_g, norm_final.reshape(1, D))

    dgu = _ffn_bwd_down(dx3b, gu, w_ffn_out_g)
    dx2, dx2b, dmixo, dvec_ffn = _ffn_bwd_up(dgu, x2, dx3, w_ffn_lo_g, w_ffn_hi_g, norm_ffn, w_o_g)
    names_a = ("w_ffn_in", "w_ffn_out", "w_o")
    part_a = [_wgrad(dgu, h2, "wgrad_ffn_in", 1408, 512), _wgrad(act, dx3b, "wgrad_ffn_out", 1408, 512),
              _wgrad(mix, dx2b, "wgrad_o", 1024, 1024)]
    lz1_a, sums_a = pair_sums(names_a, part_a, "ffn")
    (dproj, dypb, dyrb, dmat, dvec_mix), lz2_a = _mixer_bwd(
        proj, dmixo, y_pool, y_rnn, hr, *mixer_weights,
        exchange=_Scatter([by_name[n] for n in names_a]), exchange_operands=sums_a)
    names_b = ("w_pool_out", "w_rnn_out")
    part_b = [_wgrad(pm, dypb, "wgrad_pool_out", 512, 1024), _wgrad(z, dyrb, "wgrad_rnn_out", 1024, 1024)]
    lz1_b, sums_b = pair_sums(names_b, part_b, "mix")
    dvec = jnp.concatenate([dvec_mix[0:9], dvec_fin[0:1], dvec_ffn[0:1], jnp.pad(loss_part, ((0, 0), (0, DR - 1))),
                            jnp.zeros((VEC_ROWS - 12, DR), F32)], axis=0)
    g_in, exchanged = _wgrad(
        dproj, h1, "wgrad_in", 1152, 1024,
        exchange=_Both(_Scatter([by_name[n] for n in names_b]), _AllReduce([((MAT_ROWS, HD), 0), ((VEC_ROWS, DR), 1)])),
        exchange_operands=sums_b + [dmat, dvec])
    lz2_b, (mat, vec) = exchanged[:2], exchanged[2:]
    loss = vec[VEC_LOSS, 0]
    lz1_c, sums_c = pair_sums(("w_in",), [g_in], "in")
    (grad_x, dvec_in), lz2_c = _in_bwd(dproj, xs, dx2, norm_mix, w_in_g,
                                       exchange=_Scatter([by_name["w_in"]]), exchange_operands=sums_c)
    (vec_in,) = _all_reduce([dvec_in], [((8, D), 1)], "all_reduce_norm_mix")

    grads, delta, new_m, new_v = {}, {}, {}, {}
    for n, g, l1, l2 in zip(names_a + names_b + ("w_in",), part_a + part_b + [g_in], lz1_a + lz1_b + lz1_c,
                            lz2_a + lz2_b + lz2_c):
        T = by_name[n]
        out = _final_sum(T, g, l1, l2, where, held(T, w[n][0]), held(T, m[n][0]), held(T, v[n][0]))
        grads[n], delta[n], new_m[n], new_v[n] = (held(T, a) for a in out)
    me = 4 * xi + 2 * yi + ci
    small_grads = dict(
        w_pool_grp=mat[0:MAT_WA], w_rg_a=mat[MAT_WA:MAT_WX], w_rg_x=mat[MAT_WX:MAT_ROWS],
        pool_scale=vec[VEC_SCALE:VEC_SCALE + 1, 0:DP], conv_b=vec[VEC_CONV_B:VEC_CONV_B + 1],
        b_rg_a=vec[VEC_BA:VEC_BA + 1], b_rg_x=vec[VEC_BX:VEC_BX + 1], lru_lambda=vec[VEC_LAM:VEC_LAM + 1],
        conv_w=lax.dynamic_slice(vec, (VEC_CONV_W, VEC_PIECE * me), (4, VEC_PIECE)),
        norm_final=vec[VEC_NORM_FINAL:VEC_NORM_FINAL + 1], norm_ffn=vec[VEC_NORM_FFN:VEC_NORM_FFN + 1],
        norm_mix=vec_in[0:1])
    names = list(small_grads)
    as2d = lambda a, g: a.reshape(g.shape)
    upd = _adam_small([small_grads[n] for n in names],
                      [(as2d(w[n], small_grads[n]), as2d(m[n], small_grads[n]), as2d(v[n], small_grads[n])) for n in names])
    for i, n in enumerate(names):
        grads[n] = small_grads[n]
        delta[n], new_m[n], new_v[n] = upd[3 * i:3 * i + 3]

    shaped = lambda d: [d[n].reshape(w[n].shape) for n in WEIGHT_NAMES]
    return (loss, grad_x[None], *shaped(grads), *shaped(delta), *shaped(new_m), *shaped(new_v))
```

```python
import functools
import math

import jax
import jax.numpy as jnp
from jax import lax
from jax.experimental import pallas as pl
from jax.experimental.pallas import tpu as pltpu

F32 = jnp.float32
BF16 = jnp.bfloat16

D = 1024
DP = 512
PG = 128
WINDOWS = (2, 4, 8, 16)
DR = 1024
NH = 8
HD = 128
DIN = 4608
DFF = 2816
EPS = 1e-6
LRU_C = 8.0
POOL_HALO = 16
CONV_HALO = 8
KEPT = 5

ADAM_LR = 0.001
ADAM_B1 = 0.9
ADAM_B2 = 0.999
ADAM_EPS = 1e-08
ADAM_WD = 0.01
ADAM_STEP = 10

VMEM_LIMIT = 56 * 1024 * 1024
MESH_AXES = ("x", "y", "c")
MESH = pl.DeviceIdType.MESH


def _dot(a, b):
    return jnp.dot(a, b, preferred_element_type=F32)


def _dot_nt(a, b):
    return lax.dot_general(a, b, (((1,), (1,)), ((), ())), preferred_element_type=F32)


def _dot_tn(a, b):
    return lax.dot_general(a, b, (((0,), (0,)), ((), ())), preferred_element_type=F32)


def _params(*sem):
    return pltpu.CompilerParams(dimension_semantics=sem, vmem_limit_bytes=VMEM_LIMIT)


def _resident(shape):
    nd = len(shape)
    return pl.BlockSpec(shape, lambda i: (0,) * nd, pipeline_mode=pl.Buffered(1))


def _rows(shape_cols, tm):
    return pl.BlockSpec((tm, shape_cols), lambda i: (i, 0))


def _call(body, name, grid, in_specs, out_specs, out_shape, operands, scratch_shapes=(), exchange=None, exchange_operands=()):
    n_in, n_out, n_scr = len(in_specs), len(out_specs), len(scratch_shapes)
    steps = math.prod(grid)
    if exchange is None:
        outs = pl.pallas_call(body, name=name, grid=grid, in_specs=in_specs, out_specs=out_specs, out_shape=out_shape,
                              scratch_shapes=list(scratch_shapes), compiler_params=_params(*["arbitrary"] * len(grid)))(*operands)
        return outs, []
    e_in, e_out = len(exchange.in_specs), len(exchange.out_specs)

    def hosted(*refs):
        ins, refs = refs[:n_in], refs[n_in:]
        e_ins, refs = refs[:e_in], refs[e_in:]
        outs, refs = refs[:n_out], refs[n_out:]
        e_outs, refs = refs[:e_out], refs[e_out:]
        scr, e_scr = refs[:n_scr], refs[n_scr:]
        step = pl.program_id(0)
        for axis in range(1, len(grid)):
            step = step * grid[axis] + pl.program_id(axis)
        pl.when(step == 0)(lambda: exchange.start(e_ins, e_outs, e_scr))
        if exchange.middle is not None:
            pl.when(step == exchange.middle_at(steps))(lambda: exchange.middle(e_ins, e_outs, e_scr))
        body(*ins, *outs, *scr)
        pl.when(step == steps - 1)(lambda: exchange.finish(e_ins, e_outs, e_scr))

    outs = pl.pallas_call(
        hosted, name=name, grid=grid, in_specs=list(in_specs) + exchange.in_specs,
        out_specs=list(out_specs) + exchange.out_specs, out_shape=list(out_shape) + exchange.out_shape,
        scratch_shapes=list(scratch_shapes) + exchange.scratch_shapes,
        compiler_params=_params(*["arbitrary"] * len(grid)))(*operands, *exchange_operands)
    return outs[:n_out], outs[n_out:]


GELU_C = math.sqrt(2.0 / math.pi)
GELU_K = 0.044715 * GELU_C


def _gelu(x, with_grad=False):
    x2 = x * x
    t = jnp.tanh(x * (GELU_C + GELU_K * x2))
    hx = 0.5 * x
    y = hx + hx * t
    if not with_grad:
        return y
    return y, 0.5 + 0.5 * t + hx * (1.0 - t * t) * (GELU_C + (3.0 * GELU_K) * x2)


def _softplus_neg(lam):
    z = jnp.exp(-jnp.abs(lam))
    u = 1.0 + z
    dlt = u - 1.0
    log1p = jnp.where(dlt == 0.0, z, jnp.log(u) * (z / jnp.where(dlt == 0.0, 1.0, dlt)))
    return jnp.maximum(-lam, 0.0) + log1p


def _sigmoid(x):
    return 0.5 * jnp.tanh(0.5 * x) + 0.5


def _linear_scan(out_ref, A, B, h0, reverse):
    n = A.shape[0]
    sub = lax.broadcasted_iota(jnp.int32, (8, 1), 0)
    tiles = range(n // 8 - 1, -1, -1) if reverse else range(n // 8)
    carry = h0
    for j in tiles:
        a, b = A[8 * j:8 * j + 8, :], B[8 * j:8 * j + 8, :]
        for d in (1, 2, 4):
            keep = (sub < 8 - d) if reverse else (sub >= d)
            shift = 8 - d if reverse else d
            b = jnp.where(keep, a * pltpu.roll(b, shift, axis=0) + b, b)
            a = jnp.where(keep, a * pltpu.roll(a, shift, axis=0), a)
        h = a * carry + b
        out_ref[8 * j:8 * j + 8, :] = h
        carry = h[0:1, :] if reverse else h[7:8, :]
    return carry


def _pool_windows(ext, shift_sign):
    n = ext.shape[0]
    s = ext
    outs = []
    for w in WINDOWS:
        d = w // 2
        s = s + pltpu.roll(s, d if shift_sign > 0 else n - d, axis=0)
        outs.append(s[:, :PG])
        s = s[:, PG:]
    return outs


def _conv_taps(uext):
    taps = []
    for k in range(4):
        sh = 3 - k
        v = uext if sh == 0 else pltpu.roll(uext, sh, axis=0)
        taps.append(v[CONV_HALO:, :])
    return taps


def _gates(v, wa_ref, ba_ref, wx_ref, bx_ref, sp):
    vb = v.astype(BF16)
    ra, rx = [], []
    for h in range(NH):
        vh = vb[:, h * HD:(h + 1) * HD]
        ra.append(_dot(vh, wa_ref[h]))
        rx.append(_dot(vh, wx_ref[h]))
    r = _sigmoid(jnp.concatenate(ra, axis=1) + ba_ref[...])
    i = _sigmoid(jnp.concatenate(rx, axis=1) + bx_ref[...])
    log_a = r * ((-LRU_C) * sp)
    a = jnp.exp(log_a)
    one_minus = -jnp.tanh(log_a) * (1.0 + a * a)
    return r, i, a, jnp.sqrt(one_minus), lax.rsqrt(one_minus)


def _in_proj(x, norm_mix, w_in, exchange=None, exchange_operands=(), tm=512):
    S = x.shape[0]

    def body(x_ref, g_ref, w_ref, proj_ref, h_ref):
        xv = x_ref[...]
        r = lax.rsqrt(jnp.mean(xv * xv, axis=-1, keepdims=True) + EPS)
        h = (xv * r * g_ref[...]).astype(BF16)
        h_ref[...] = h
        for n0 in range(0, DIN, 512):
            proj_ref[:, n0:n0 + 512] = _dot_nt(h, w_ref[n0:n0 + 512, :])

    return _call(
        body, "in_proj", (S // tm,),
        in_specs=[_rows(D, tm), _resident((1, D)), _resident((DIN, D))],
        out_specs=[_rows(DIN, tm), _rows(D, tm)],
        out_shape=[jax.ShapeDtypeStruct((S, DIN), F32), jax.ShapeDtypeStruct((S, D), BF16)],
        operands=(x, norm_mix, w_in), exchange=exchange, exchange_operands=exchange_operands)


def _mixer_fwd(proj, wg, scale, w_pool_out, conv_w, conv_b, wa, ba, wx, bx, lam, w_rnn_out, exchange=None,
               exchange_operands=(), tm=256):
    S = proj.shape[0]
    UW = DP + 2 * DR

    def body(proj_ref, wg_ref, scale_ref, wpo_ref, cw_ref, cb_ref, wa_ref, ba_ref, wx_ref, bx_ref, lam_ref, wro_ref,
             pooled_ref, pm_ref, ypool_ref, hr_ref, z_ref, yrnn_ref, kept_ref, pool_carry, conv_carry, h_carry):
        i = pl.program_id(0)

        @pl.when(i == 0)
        def _():
            pool_carry[...] = jnp.zeros_like(pool_carry)
            conv_carry[...] = jnp.zeros_like(conv_carry)
            h_carry[...] = jnp.zeros_like(h_carry)

        rows = lax.broadcasted_iota(jnp.int32, (tm, 1), 0)
        t_glob = i * tm + rows

        u_pool = proj_ref[:, 0:DP]
        ext = jnp.concatenate([pool_carry[...], u_pool], axis=0)
        pool_carry[...] = u_pool[tm - POOL_HALO:, :]
        sums = _pool_windows(ext, +1)
        mixed = []
        for g, w in enumerate(WINDOWS):
            inv_cnt = 1.0 / jnp.minimum(t_glob + 1, w).astype(F32)
            pooled_g = sums[g][POOL_HALO:, :] * inv_cnt - u_pool[:, g * PG:(g + 1) * PG]
            pooled_b = pooled_g.astype(BF16)
            pooled_ref[:, g * PG:(g + 1) * PG] = pooled_b
            mixed.append(_dot(pooled_b, wg_ref[g]))
        pm = (jnp.concatenate(mixed, axis=1) * scale_ref[...]).astype(BF16)
        pm_ref[...] = pm
        ypool_ref[...] = _dot(pm, wpo_ref[...])

        u_rnn = proj_ref[:, DP:DP + DR]
        uext = jnp.concatenate([conv_carry[...], u_rnn], axis=0)
        conv_carry[...] = u_rnn[tm - CONV_HALO:, :]
        taps = _conv_taps(uext)
        v = cb_ref[...]
        for k in range(4):
            v = v + taps[k] * cw_ref[k:k + 1, :]
        sp = _softplus_neg(lam_ref[...])
        r, gi, a, mult, _ = _gates(v, wa_ref, ba_ref, wx_ref, bx_ref, sp)
        for k, kept in enumerate((v, r, gi, a, mult)):
            kept_ref[k] = kept
        h_carry[0:1, :] = _linear_scan(hr_ref, a, mult * gi * v, h_carry[0:1, :], reverse=False)
        z = (hr_ref[...] * _gelu(proj_ref[:, DP + DR:UW])).astype(BF16)
        z_ref[...] = z
        yrnn_ref[...] = _dot(z, wro_ref[...])

    return _call(
        body, "mixer_fwd", (S // tm,),
        in_specs=[_rows(UW, tm), _resident((4, PG, PG)), _resident((1, DP)), _resident((DP, D)), _resident((4, DR)),
                  _resident((1, DR)), _resident((NH, HD, HD)), _resident((1, DR)), _resident((NH, HD, HD)),
                  _resident((1, DR)), _resident((1, DR)), _resident((DR, D))],
        out_specs=[_rows(DP, tm), _rows(DP, tm), _rows(D, tm), _rows(DR, tm), _rows(DR, tm), _rows(D, tm),
                   pl.BlockSpec((KEPT, tm, DR), lambda i: (0, i, 0))],
        out_shape=[jax.ShapeDtypeStruct((S, DP), BF16), jax.ShapeDtypeStruct((S, DP), BF16),
                   jax.ShapeDtypeStruct((S, D), F32), jax.ShapeDtypeStruct((S, DR), F32),
                   jax.ShapeDtypeStruct((S, DR), BF16), jax.ShapeDtypeStruct((S, D), F32),
                   jax.ShapeDtypeStruct((KEPT, S, DR), F32)],
        scratch_shapes=[pltpu.VMEM((POOL_HALO, DP), F32), pltpu.VMEM((CONV_HALO, DR), F32), pltpu.VMEM((8, DR), F32)],
        operands=(proj, wg, scale, w_pool_out, conv_w, conv_b, wa, ba, wx, bx, lam, w_rnn_out),
        exchange=exchange, exchange_operands=exchange_operands)


FF_CHUNKS = ((0, 768), (768, 1536), (1536, 2304), (2304, DFF))


def _rms(x):
    r = lax.rsqrt(jnp.mean(x * x, axis=-1, keepdims=True) + EPS)
    return r, x * r


def _rms_bwd(dh, g, r, xh):
    dxh = dh * g
    return r * (dxh - xh * jnp.mean(dxh * xh, axis=-1, keepdims=True))


def _merge_out(x, proj, y_pool, y_rnn, w_o, norm_ffn, exchange=None, exchange_operands=(), tm=512):
    S = x.shape[0]
    GL0 = (DP + 2 * DR) // 512

    def gl_spec(k):
        return pl.BlockSpec((tm, 512), lambda i: (i, GL0 + k))

    def body(x_ref, gl0, gl1, gl2, gl3, yp_ref, yr_ref, wo_ref, gf_ref, mix_ref, x2_ref, h2_ref):
        s_p = _sigmoid(jnp.concatenate([gl0[...], gl1[...]], axis=1))
        s_r = _sigmoid(jnp.concatenate([gl2[...], gl3[...]], axis=1))
        mix = (s_p * yp_ref[...] + s_r * yr_ref[...]).astype(BF16)
        mix_ref[...] = mix
        x2 = x_ref[...] + _dot(mix, wo_ref[...])
        x2_ref[...] = x2
        _, xh2 = _rms(x2)
        h2_ref[...] = (xh2 * gf_ref[...]).astype(BF16)

    return _call(
        body, "merge_out", (S // tm,),
        in_specs=[_rows(D, tm), gl_spec(0), gl_spec(1), gl_spec(2), gl_spec(3), _rows(D, tm), _rows(D, tm),
                  _resident((D, D)), _resident((1, D))],
        out_specs=[_rows(D, tm), _rows(D, tm), _rows(D, tm)],
        out_shape=[jax.ShapeDtypeStruct((S, D), BF16), jax.ShapeDtypeStruct((S, D), F32), jax.ShapeDtypeStruct((S, D), BF16)],
        operands=(x, proj, proj, proj, proj, y_pool, y_rnn, w_o, norm_ffn),
        exchange=exchange, exchange_operands=exchange_operands)


def _ffn_up(h2, w_lo, w_hi, exchange=None, exchange_operands=(), tm=512):
    S = h2.shape[0]
    HALF = D // 2

    def body(h_ref, lo_ref, hi_ref, gu_ref, act_ref):
        h_lo, h_hi = h_ref[:, 0:HALF], h_ref[:, HALF:D]
        for c0, c1 in FF_CHUNKS:
            gate = _dot_nt(h_lo, lo_ref[c0:c1, :]) + _dot_nt(h_hi, hi_ref[c0:c1, :])
            up = _dot_nt(h_lo, lo_ref[DFF + c0:DFF + c1, :]) + _dot_nt(h_hi, hi_ref[DFF + c0:DFF + c1, :])
            gu_ref[:, c0:c1] = gate.astype(BF16)
            gu_ref[:, DFF + c0:DFF + c1] = up.astype(BF16)
            act_ref[:, c0:c1] = (gate * _sigmoid(gate) * up).astype(BF16)

    return _call(
        body, "ffn_up", (S // tm,),
        in_specs=[_rows(D, tm), _resident((2 * DFF, HALF)), _resident((2 * DFF, HALF))],
        out_specs=[_rows(2 * DFF, tm), _rows(DFF, tm)],
        out_shape=[jax.ShapeDtypeStruct((S, 2 * DFF), BF16), jax.ShapeDtypeStruct((S, DFF), BF16)],
        operands=(h2, w_lo, w_hi), exchange=exchange, exchange_operands=exchange_operands)


def _ffn_down_loss(act, x2, target, w_ffn_out, norm_final, tm=512):
    S = act.shape[0]

    def body(act_ref, x2_ref, t_ref, w_ref, gn_ref, dx3_ref, dx3b_ref, loss_ref, dvec_ref):
        i = pl.program_id(0)

        @pl.when(i == 0)
        def _():
            loss_ref[...] = jnp.zeros_like(loss_ref)
            dvec_ref[...] = jnp.zeros_like(dvec_ref)

        x3 = x2_ref[...] + _dot(act_ref[...], w_ref[...])
        r3, xh3 = _rms(x3)
        g_fin = gn_ref[...]
        e = xh3 * g_fin - t_ref[...]
        loss_ref[...] += jnp.sum(e * e, axis=(0, 1), keepdims=True) * (0.5 / D)
        dy = e * (1.0 / D)
        dvec_ref[0:1, :] += jnp.sum(dy * xh3, axis=0, keepdims=True)
        dx3 = _rms_bwd(dy, g_fin, r3, xh3)
        dx3_ref[...] = dx3
        dx3b_ref[...] = dx3.astype(BF16)

    return pl.pallas_call(
        body, name="ffn_down_loss", grid=(S // tm,),
        in_specs=[_rows(DFF, tm), _rows(D, tm), _rows(D, tm), _resident((DFF, D)), _resident((1, D))],
        out_specs=[_rows(D, tm), _rows(D, tm), _resident((1, 1)), _resident((8, D))],
        out_shape=[jax.ShapeDtypeStruct((S, D), F32), jax.ShapeDtypeStruct((S, D), BF16),
                   jax.ShapeDtypeStruct((1, 1), F32), jax.ShapeDtypeStruct((8, D), F32)],
        compiler_params=_params("arbitrary"),
    )(act, x2, target, w_ffn_out, norm_final)


def _ffn_bwd_down(dx3b, gu, w_ffn_out, tm=512):
    S = dx3b.shape[0]

    def body(d_ref, gu_ref, w_ref, dgu_ref):
        d = d_ref[...]
        for c0, c1 in FF_CHUNKS:
            dact = _dot_nt(d, w_ref[c0:c1, :])
            gate = gu_ref[:, c0:c1].astype(F32)
            up = gu_ref[:, DFF + c0:DFF + c1].astype(F32)
            sg = _sigmoid(gate)
            dgu_ref[:, c0:c1] = (dact * up * (sg * (1.0 + gate * (1.0 - sg)))).astype(BF16)
            dgu_ref[:, DFF + c0:DFF + c1] = (dact * (gate * sg)).astype(BF16)

    return pl.pallas_call(
        body, name="ffn_bwd_down", grid=(S // tm,),
        in_specs=[_rows(D, tm), _rows(2 * DFF, tm), _resident((DFF, D))],
        out_specs=_rows(2 * DFF, tm),
        out_shape=jax.ShapeDtypeStruct((S, 2 * DFF), BF16),
        compiler_params=_params("parallel"),
    )(dx3b, gu, w_ffn_out)


def _ffn_bwd_up(dgu, x2, dx3, w_lo, w_hi, norm_ffn, w_o, tm=512):
    S = dgu.shape[0]
    HALF = D // 2

    def body(dgu_ref, x2_ref, dx3_ref, lo_ref, hi_ref, gf_ref, wo_ref, dx2_ref, dx2b_ref, dmixo_ref, dvec_ref):
        i = pl.program_id(0)

        @pl.when(i == 0)
        def _():
            dvec_ref[...] = jnp.zeros_like(dvec_ref)

        dgate, dup = dgu_ref[:, 0:DFF], dgu_ref[:, DFF:2 * DFF]
        dh2 = jnp.concatenate([_dot(dgate, w[0:DFF, :]) + _dot(dup, w[DFF:2 * DFF, :]) for w in (lo_ref, hi_ref)], axis=1)
        r2, xh2 = _rms(x2_ref[...])
        dvec_ref[0:1, :] += jnp.sum(dh2 * xh2, axis=0, keepdims=True)
        dx2 = dx3_ref[...] + _rms_bwd(dh2, gf_ref[...], r2, xh2)
        dx2_ref[...] = dx2
        dx2b = dx2.astype(BF16)
        dx2b_ref[...] = dx2b
        dmixo_ref[...] = _dot_nt(dx2b, wo_ref[...])

    return pl.pallas_call(
        body, name="ffn_bwd_up", grid=(S // tm,),
        in_specs=[_rows(2 * DFF, tm), _rows(D, tm), _rows(D, tm), _resident((2 * DFF, HALF)), _resident((2 * DFF, HALF)),
                  _resident((1, D)), _resident((D, D))],
        out_specs=[_rows(D, tm), _rows(D, tm), _rows(D, tm), _resident((8, D))],
        out_shape=[jax.ShapeDtypeStruct((S, D), F32), jax.ShapeDtypeStruct((S, D), BF16), jax.ShapeDtypeStruct((S, D), F32),
                   jax.ShapeDtypeStruct((8, D), F32)],
        compiler_params=_params("arbitrary"),
    )(dgu, x2, dx3, w_lo, w_hi, norm_ffn, w_o)


VEC_ROWS = 16
MAT_WA = 4 * PG
MAT_WX = MAT_WA + NH * HD
MAT_ROWS = MAT_WX + NH * HD


def _mixer_bwd(proj, dmixo, y_pool, y_rnn, hr, kept, wg, scale, w_pool_out, conv_w, conv_b, wa, ba, wx, bx, lam, w_rnn_out,
               exchange=None, exchange_operands=(), tm=256):
    S = proj.shape[0]
    nt = S // tm

    def rev(cols):
        return pl.BlockSpec((tm, cols), lambda i: (nt - 1 - i, 0))

    def halo(rows_, cols):
        per = tm // rows_
        return pl.BlockSpec((rows_, cols), lambda i: (jnp.maximum((nt - 1 - i) * per - 1, 0), 0))

    def body(proj_ref, projh_ref, dmixo_ref, yp_ref, yr_ref, hr_ref, hrh_ref, kept_ref, wg_ref, scale_ref, wpo_ref, cw_ref, cb_ref,
             wa_ref, ba_ref, wx_ref, bx_ref, lam_ref, wro_ref,
             dproj_ref, dypb_ref, dyrb_ref, dmat_ref, dvec_ref,
             q_carry, dv_carry, a_carry, g_carry, g_scr):
        i = pl.program_id(0)
        ti = nt - 1 - i

        @pl.when(i == 0)
        def _():
            q_carry[...] = jnp.zeros_like(q_carry)
            dv_carry[...] = jnp.zeros_like(dv_carry)
            a_carry[...] = jnp.zeros_like(a_carry)
            g_carry[...] = jnp.zeros_like(g_carry)
            dmat_ref[...] = jnp.zeros_like(dmat_ref)
            dvec_ref[...] = jnp.zeros_like(dvec_ref)

        rows = lax.broadcasted_iota(jnp.int32, (tm, 1), 0)
        t_glob = ti * tm + rows
        has_prev = (ti > 0).astype(F32)
        dmixo = dmixo_ref[...]

        s_p = _sigmoid(proj_ref[:, DP + 2 * DR:DP + 2 * DR + D])
        s_r = _sigmoid(proj_ref[:, DP + 2 * DR + D:DIN])
        dproj_ref[:, DP + 2 * DR:DP + 2 * DR + D] = (dmixo * yp_ref[...] * s_p * (1.0 - s_p)).astype(BF16)
        dproj_ref[:, DP + 2 * DR + D:DIN] = (dmixo * yr_ref[...] * s_r * (1.0 - s_r)).astype(BF16)
        dyp = (dmixo * s_p).astype(BF16)
        dyr = (dmixo * s_r).astype(BF16)
        dypb_ref[...] = dyp
        dyrb_ref[...] = dyr

        dz = _dot_nt(dyr, wro_ref[...])
        u_gate = proj_ref[:, DP + DR:DP + 2 * DR]
        gg, dgelu = _gelu(u_gate, with_grad=True)
        hr_t = hr_ref[...]
        dproj_ref[:, DP + DR:DP + 2 * DR] = (dz * hr_t * dgelu).astype(BF16)
        dhr = dz * gg

        sp = _softplus_neg(lam_ref[...])
        v, r, gi, a, mult = (kept_ref[k] for k in range(KEPT))
        inv_mult = 1.0 / mult

        C = jnp.where(rows == tm - 1, a_carry[0:1, :], pltpu.roll(a, tm - 1, axis=0))
        g_carry[0:1, :] = _linear_scan(g_scr, C, dhr, g_carry[0:1, :], reverse=True)
        a_carry[0:1, :] = a[0:1, :]
        g = g_scr[...]

        h_prev = jnp.where(rows == 0, hrh_ref[7:8, :] * has_prev, pltpu.roll(hr_t, 1, axis=0))
        da = g * h_prev
        gm = g * mult
        dmult = g * gi * v
        di = gm * v
        dv = gm * gi
        dlog_a = da * a - dmult * (a * a * inv_mult)
        dvec_ref[4:5, :] += jnp.sum(dlog_a * r, axis=0, keepdims=True)
        dra = (dlog_a * ((-LRU_C) * sp) * r * (1.0 - r))
        drx = di * gi * (1.0 - gi)
        dvec_ref[2:3, :] += jnp.sum(dra, axis=0, keepdims=True)
        dvec_ref[3:4, :] += jnp.sum(drx, axis=0, keepdims=True)
        drab = dra.astype(BF16)
        drxb = drx.astype(BF16)
        vb = v.astype(BF16)
        dvg = []
        for h in range(NH):
            sl = slice(h * HD, (h + 1) * HD)
            dvg.append(_dot_nt(drab[:, sl], wa_ref[h]) + _dot_nt(drxb[:, sl], wx_ref[h]))
            dmat_ref[MAT_WA + h * HD:MAT_WA + (h + 1) * HD, :] += _dot_tn(vb[:, sl], drab[:, sl])
            dmat_ref[MAT_WX + h * HD:MAT_WX + (h + 1) * HD, :] += _dot_tn(vb[:, sl], drxb[:, sl])
        dv = dv + jnp.concatenate(dvg, axis=1)
        dvec_ref[1:2, :] += jnp.sum(dv, axis=0, keepdims=True)
        dvext = jnp.concatenate([dv, dv_carry[...]], axis=0)
        dv_carry[...] = dv[0:CONV_HALO, :]
        n = tm + CONV_HALO
        u_rnn = proj_ref[:, DP:DP + DR]
        du_rnn = dv * cw_ref[3:4, :]
        dvec_ref[8:9, :] += jnp.sum(dv * u_rnn, axis=0, keepdims=True)
        for k in range(3):
            dv_k = pltpu.roll(dvext, n - (3 - k), axis=0)[0:tm, :]
            du_rnn = du_rnn + dv_k * cw_ref[k:k + 1, :]
            dvec_ref[5 + k:6 + k, :] += jnp.sum(dv_k * u_rnn, axis=0, keepdims=True)
        dproj_ref[:, DP:DP + DR] = du_rnn.astype(BF16)

        dpm = _dot_nt(dyp, wpo_ref[...])
        u_pool = proj_ref[:, 0:DP]
        ext = jnp.concatenate([projh_ref[:, 0:DP] * has_prev, u_pool], axis=0)
        sums = _pool_windows(ext, +1)
        scale_v = scale_ref[...]
        qs = []
        dpooled = []
        dscale = []
        for gi_, w in enumerate(WINDOWS):
            sl = slice(gi_ * PG, (gi_ + 1) * PG)
            inv_cnt = 1.0 / jnp.minimum(t_glob + 1, w).astype(F32)
            pooled_b = (sums[gi_][POOL_HALO:, :] * inv_cnt - u_pool[:, sl]).astype(BF16)
            mixed_g = _dot(pooled_b, wg_ref[gi_])
            dscale.append(jnp.sum(dpm[:, sl] * mixed_g, axis=0, keepdims=True))
            dmixed_b = (dpm[:, sl] * scale_v[:, sl]).astype(BF16)
            dmat_ref[gi_ * PG:(gi_ + 1) * PG, :] += _dot_tn(pooled_b, dmixed_b)
            dp_g = _dot_nt(dmixed_b, wg_ref[gi_])
            dpooled.append(dp_g)
            qs.append(dp_g * inv_cnt)
        dvec_ref[0:1, 0:DP] += jnp.concatenate(dscale, axis=1)
        q = jnp.concatenate(qs, axis=1)
        qext = jnp.concatenate([q, q_carry[...]], axis=0)
        q_carry[...] = q[0:POOL_HALO, :]
        tsum = _pool_windows(qext, -1)
        for gi_ in range(4):
            dproj_ref[:, gi_ * PG:(gi_ + 1) * PG] = (tsum[gi_][0:tm, :] - dpooled[gi_]).astype(BF16)

        @pl.when(i == nt - 1)
        def _():
            dvec_ref[4:5, :] = dvec_ref[4:5, :] * (LRU_C * _sigmoid(-lam_ref[...]))

    return _call(
        body, "mixer_bwd", (nt,),
        in_specs=[rev(DIN), halo(POOL_HALO, DIN), rev(D), rev(D), rev(D), rev(DR), halo(8, DR),
                  pl.BlockSpec((KEPT, tm, DR), lambda i: (0, nt - 1 - i, 0)), _resident((4, PG, PG)), _resident((1, DP)), _resident((DP, D)), _resident((4, DR)), _resident((1, DR)),
                  _resident((NH, HD, HD)), _resident((1, DR)), _resident((NH, HD, HD)), _resident((1, DR)),
                  _resident((1, DR)), _resident((DR, D))],
        out_specs=[rev(DIN), rev(D), rev(D), _resident((MAT_ROWS, HD)), _resident((VEC_ROWS, DR))],
        out_shape=[jax.ShapeDtypeStruct((S, DIN), BF16), jax.ShapeDtypeStruct((S, D), BF16),
                   jax.ShapeDtypeStruct((S, D), BF16), jax.ShapeDtypeStruct((MAT_ROWS, HD), F32),
                   jax.ShapeDtypeStruct((VEC_ROWS, DR), F32)],
        scratch_shapes=[pltpu.VMEM((POOL_HALO, DP), F32), pltpu.VMEM((CONV_HALO, DR), F32), pltpu.VMEM((8, DR), F32),
                        pltpu.VMEM((8, DR), F32), pltpu.VMEM((tm, DR), F32)],
        operands=(proj, proj, dmixo, y_pool, y_rnn, hr, hr, kept, wg, scale, w_pool_out, conv_w, conv_b, wa, ba, wx, bx, lam,
                  w_rnn_out),
        exchange=exchange, exchange_operands=exchange_operands)


def _in_bwd(dproj, x, dx2, norm_mix, w_in, exchange=None, exchange_operands=(), tm=512):
    S = x.shape[0]

    def body(dp_ref, x_ref, dx2_ref, g_ref, w_ref, dx_ref, dg_ref):
        i = pl.program_id(0)

        @pl.when(i == 0)
        def _():
            dg_ref[...] = jnp.zeros_like(dg_ref)

        dh = _dot(dp_ref[:, 0:1536], w_ref[0:1536, :])
        dh = dh + _dot(dp_ref[:, 1536:3072], w_ref[1536:3072, :])
        dh = dh + _dot(dp_ref[:, 3072:DIN], w_ref[3072:DIN, :])
        xv = x_ref[...]
        r = lax.rsqrt(jnp.mean(xv * xv, axis=-1, keepdims=True) + EPS)
        xh = xv * r
        dg_ref[0:1, :] += jnp.sum(dh * xh, axis=0, keepdims=True)
        dxh = dh * g_ref[...]
        dx_ref[...] = dx2_ref[...] + r * (dxh - xh * jnp.mean(dxh * xh, axis=-1, keepdims=True))

    return _call(
        body, "in_bwd", (S // tm,),
        in_specs=[_rows(DIN, tm), _rows(D, tm), _rows(D, tm), _resident((1, D)), _resident((DIN, D))],
        out_specs=[_rows(D, tm), _resident((8, D))],
        out_shape=[jax.ShapeDtypeStruct((S, D), F32), jax.ShapeDtypeStruct((8, D), F32)],
        operands=(dproj, x, dx2, norm_mix, w_in), exchange=exchange, exchange_operands=exchange_operands)


def _wgrad(a, b, name, tk, tn, exchange=None, exchange_operands=()):
    S, K = a.shape
    N = b.shape[1]

    def body(a_ref, b_ref, o_ref):
        o_ref[...] = _dot_tn(a_ref[...], b_ref[...]).astype(BF16)

    (out,), exchanged = _call(
        body, name, (K // tk, N // tn),
        in_specs=[pl.BlockSpec((S, tk), lambda k, n: (0, k)), pl.BlockSpec((S, tn), lambda k, n: (0, n))],
        out_specs=[pl.BlockSpec((tk, tn), lambda k, n: (k, n))],
        out_shape=[jax.ShapeDtypeStruct((K, N), BF16)],
        operands=(a, b), exchange=exchange, exchange_operands=exchange_operands)
    return (out, exchanged) if exchange is not None else out


VEC_SCALE, VEC_CONV_B, VEC_BA, VEC_BX, VEC_LAM, VEC_CONV_W, VEC_NORM_FINAL, VEC_NORM_FFN = 0, 1, 2, 3, 4, 5, 9, 10
VEC_LOSS = 11


class _Big:
    def __init__(self, name, rows, cols, axis, n, dtype=BF16, transposed=False):
        self.name, self.rows, self.cols, self.axis, self.n, self.dtype = name, rows, cols, axis, n, dtype
        self.transposed = transposed
        self.block_shape = (rows, n) if axis == 1 else (n, cols)

    def block(self, ref, p):
        if self.axis == 1:
            return ref.at[:, pl.ds(pl.multiple_of(p * self.n, 128), self.n)]
        return ref.at[pl.ds(pl.multiple_of(p * self.n, 16 if self.dtype == BF16 else 8), self.n), :]

    def block_index(self, p):
        return (0, p) if self.axis == 1 else (p, 0)


BIG = (_Big("w_in", DIN, D, 0, DIN // 8, transposed=True), _Big("w_pool_out", DP, D, 1, D // 8),
       _Big("w_rnn_out", DR, D, 0, DR // 8), _Big("w_o", D, D, 0, D // 8),
       _Big("w_ffn_in", 2 * DFF, D, 0, 2 * DFF // 8, transposed=True), _Big("w_ffn_out", DFF, D, 0, DFF // 8))
CONV_W = _Big("conv_w", 8, DR, 1, DR // 8, F32)
W_FFN_IN_HALVES = (_Big("w_ffn_in_lo", 2 * DFF, D // 2, 0, 2 * DFF // 8), _Big("w_ffn_in_hi", 2 * DFF, D // 2, 0, 2 * DFF // 8))
GATHERED = BIG + (CONV_W,) + W_FFN_IN_HALVES

HBM_SPEC = pl.BlockSpec(memory_space=pl.ANY)
VMEM_SPEC = pl.BlockSpec(memory_space=pltpu.VMEM)


def _place():
    x, y, c = (lax.axis_index(a) for a in MESH_AXES)
    other_chips = [(1 - x, y), (x, 1 - y), (1 - x, 1 - y)]
    return x, y, c, other_chips


def _remote(src, dst, send_sems, recv_sems, idx, to):
    return pltpu.make_async_remote_copy(src_ref=src, dst_ref=dst, send_sem=send_sems.at[idx], recv_sem=recv_sems.at[idx],
                                        device_id=to, device_id_type=MESH)


def _device_index(chip, core):
    return 4 * chip[0] + 2 * chip[1] + core


class _Gather:
    def __init__(self, tensors):
        self.tensors = tuple(tensors)
        n = len(self.tensors)
        self.in_specs = [HBM_SPEC] * n
        self.out_specs = [HBM_SPEC] * n
        self.out_shape = [jax.ShapeDtypeStruct((T.rows, T.cols), T.dtype) for T in self.tensors]
        self.scratch_shapes = [pltpu.VMEM(T.block_shape, T.dtype) for T in self.tensors] + [
            pltpu.SemaphoreType.DMA((n, 7)), pltpu.SemaphoreType.DMA((n, 7)), pltpu.SemaphoreType.DMA((n, 2))]

    def middle_at(self, steps):
        return steps - 1

    def _copies(self, ins, outs, scratch):
        n = len(self.tensors)
        mine, (send_sems, recv_sems, loc_sems) = scratch[:n], scratch[n:]
        x, y, c, chips = _place()
        sibling = (x, y, 1 - c)
        me = _device_index((x, y), c)
        loads, stores, first, passed, arrivals, late = [], [], [], [], [], []
        for t, T in enumerate(self.tensors):
            place = T.block(outs[t], me)
            loads.append(pltpu.make_async_copy(ins[t], mine[t], loc_sems.at[t, 0]))
            stores.append(pltpu.make_async_copy(mine[t], place, loc_sems.at[t, 1]))
            first.append(_remote(mine[t], place, send_sems, recv_sems, (t, 0), sibling))
            theirs = T.block(outs[t], _device_index((x, y), 1 - c))
            late.append(_remote(theirs, theirs, send_sems, recv_sems, (t, 0), sibling))
            for k, chip in enumerate(chips):
                first.append(_remote(mine[t], place, send_sems, recv_sems, (t, 1 + k), (*chip, c)))
                land = T.block(outs[t], _device_index(chip, c))
                arrivals.append(_remote(land, land, send_sems, recv_sems, (t, 1 + k), sibling))
                passed.append(_remote(land, land, send_sems, recv_sems, (t, 4 + k), sibling))
                theirs = T.block(outs[t], _device_index(chip, 1 - c))
                late.append(_remote(theirs, theirs, send_sems, recv_sems, (t, 4 + k), sibling))
        return loads, stores, first, passed, arrivals, late

    def start(self, ins, outs, scratch):
        loads, stores, first, _, _, _ = self._copies(ins, outs, scratch)
        for cp in loads:
            cp.start()
        for cp in loads:
            cp.wait()
        for cp in stores + first:
            cp.start()

    def middle(self, ins, outs, scratch):
        _, _, _, passed, arrivals, _ = self._copies(ins, outs, scratch)
        for arrived, cp in zip(arrivals, passed):
            arrived.wait_recv()
            cp.start()

    def finish(self, ins, outs, scratch):
        _, stores, first, passed, _, late = self._copies(ins, outs, scratch)
        for cp in late:
            cp.wait_recv()
        for cp in first + passed:
            cp.wait_send()
        for cp in stores:
            cp.wait()


def _all_gather(blocks, tensors, name):
    gather = _Gather(tensors)

    def body(*refs):
        n = len(gather.tensors)
        ins, outs, scratch = refs[:n], refs[n:2 * n], refs[2 * n:]
        gather.start(ins, outs, scratch)
        gather.middle(ins, outs, scratch)
        gather.finish(ins, outs, scratch)

    return pl.pallas_call(
        body, name=name, in_specs=gather.in_specs, out_specs=gather.out_specs, out_shape=gather.out_shape,
        scratch_shapes=gather.scratch_shapes, compiler_params=pltpu.CompilerParams(vmem_limit_bytes=VMEM_LIMIT),
    )(*blocks)


PAIR_ROWS = 32


def _pair_reduce(grads, tensors, name):
    nt = len(tensors)

    def body(*refs):
        ins, own_out, sums_out, landed, mine = (refs[k * nt:(k + 1) * nt] for k in range(5))
        send_sems, recv_sems, loc_sems = refs[5 * nt:]
        x, y, c, chips = _place()
        chip_of = [2 * chip[0] + chip[1] for chip in chips]
        swaps, loads = [], []
        for t, T in enumerate(tensors):
            for j in range(4):
                swaps.append(_remote(T.block(ins[t], 2 * j + 1 - c), landed[t].at[j], send_sems, recv_sems, (t, j),
                                     (x, y, 1 - c)))
            for k in range(3):
                loads.append(pltpu.make_async_copy(T.block(ins[t], 2 * chip_of[k] + c), mine[t].at[k], loc_sems.at[t, k]))
        for cp in swaps + loads:
            cp.start()
        for cp in loads:
            cp.wait()
        for cp in swaps:
            cp.wait_recv()
        stores = []
        for t, T in enumerate(tensors):
            for k in range(3):
                acc, got = mine[t].at[k], landed[t].at[chip_of[k]]

                def add(i, carry, acc=acc, got=got):
                    rows = pl.ds(pl.multiple_of(i * PAIR_ROWS, PAIR_ROWS), PAIR_ROWS)
                    acc[rows, :] = (acc[rows, :].astype(F32) + got[rows, :].astype(F32)).astype(BF16)
                    return carry

                lax.fori_loop(0, T.block_shape[0] // PAIR_ROWS, add, 0)
            stores.append(pltpu.make_async_copy(mine[t], sums_out[t], loc_sems.at[t, 3]))
            stores.append(pltpu.make_async_copy(landed[t].at[2 * x + y], own_out[t], loc_sems.at[t, 4]))
        for cp in stores:
            cp.start()
        for cp in swaps:
            cp.wait_send()
        for cp in stores:
            cp.wait()

    blocks = [T.block_shape for T in tensors]
    return pl.pallas_call(
        body, name=name,
        in_specs=[HBM_SPEC] * nt, out_specs=[HBM_SPEC] * (2 * nt),
        out_shape=[jax.ShapeDtypeStruct(b, BF16) for b in blocks] + [jax.ShapeDtypeStruct((3,) + b, BF16) for b in blocks],
        scratch_shapes=[pltpu.VMEM((4,) + b, BF16) for b in blocks] + [pltpu.VMEM((3,) + b, BF16) for b in blocks]
        + [pltpu.SemaphoreType.DMA((nt, 4)), pltpu.SemaphoreType.DMA((nt, 4)), pltpu.SemaphoreType.DMA((nt, 5))],
        compiler_params=pltpu.CompilerParams(vmem_limit_bytes=VMEM_LIMIT),
    )(*grads)


class _Scatter:
    middle = None

    def __init__(self, tensors):
        n = len(tensors)
        self.in_specs = [HBM_SPEC] * n
        self.out_specs = [HBM_SPEC] * n
        self.out_shape = [jax.ShapeDtypeStruct((3,) + T.block_shape, BF16) for T in tensors]
        self.scratch_shapes = [pltpu.SemaphoreType.DMA((n, 3)), pltpu.SemaphoreType.DMA((n, 3))]

    def _copies(self, ins, outs, scratch):
        send_sems, recv_sems = scratch
        x, y, c, chips = _place()
        return [_remote(ins[t].at[k], outs[t].at[k], send_sems, recv_sems, (t, k), (*chip, c))
                for t in range(len(ins)) for k, chip in enumerate(chips)]

    def start(self, ins, outs, scratch):
        for cp in self._copies(ins, outs, scratch):
            cp.start()

    def finish(self, ins, outs, scratch):
        for cp in self._copies(ins, outs, scratch):
            cp.wait()


def _chip_scatter(sums, tensors, name):
    scatter = _Scatter(tensors)
    n = len(tensors)

    def body(*refs):
        ins, outs, scratch = refs[:n], refs[n:2 * n], refs[2 * n:]
        scatter.start(ins, outs, scratch)
        scatter.finish(ins, outs, scratch)

    return pl.pallas_call(
        body, name=name, in_specs=scatter.in_specs, out_specs=scatter.out_specs, out_shape=scatter.out_shape,
        scratch_shapes=scatter.scratch_shapes,
    )(*sums)


def _adamw(w, g, m, v):
    m = ADAM_B1 * m + (1.0 - ADAM_B1) * g
    v = ADAM_B2 * v + (1.0 - ADAM_B2) * (g * g)
    m_hat = m / (1.0 - ADAM_B1 ** ADAM_STEP)
    v_hat = v / (1.0 - ADAM_B2 ** ADAM_STEP)
    delta = -ADAM_LR * (m_hat / (jnp.sqrt(v_hat) + ADAM_EPS) + ADAM_WD * w)
    return delta, m, v


def _final_sum(T, g, lz1, lz2, where, w, m, v):
    rows, cols = T.block_shape
    sub = 4 if T.axis == 0 and rows % 64 == 0 and rows > 256 else 1
    blk = (rows // sub, cols)

    def body(where_ref, g_ref, l1_ref, l2_ref, w_ref, m_ref, v_ref, g_out, d_out, m_out, v_out):
        tot = g_ref[...].astype(F32) + l1_ref[...].astype(F32)
        for k in range(3):
            tot = tot + l2_ref[k].astype(F32)
        g_out[...] = tot
        d_out[...], m_out[...], v_out[...] = _adamw(w_ref[...], tot, m_ref[...], v_ref[...])

    def in_whole(r, wh):
        p = wh[0]
        return (0, p) if T.axis == 1 else (p * sub + r, 0)

    own = pl.BlockSpec(blk, lambda r, wh: (r, 0))
    return pl.pallas_call(
        body, name="grad_final_" + T.name,
        grid_spec=pltpu.PrefetchScalarGridSpec(
            num_scalar_prefetch=1, grid=(sub,),
            in_specs=[pl.BlockSpec(blk, in_whole),
                      own,
                      pl.BlockSpec((3,) + blk, lambda r, wh: (0, r, 0)), own, own, own],
            out_specs=[own] * 4),
        out_shape=[jax.ShapeDtypeStruct(T.block_shape, F32)] * 4,
        compiler_params=_params("arbitrary"),
    )(where, g, lz1, lz2, w, m, v)


MAT_PIECE = MAT_ROWS // 8
VEC_PIECE = DR // 8


class _AllReduce:
    def __init__(self, items):
        self.items = tuple(items)
        n = len(self.items)
        self.in_specs = [HBM_SPEC] * n
        self.out_specs = [HBM_SPEC] * n
        self.out_shape = [jax.ShapeDtypeStruct(shape, F32) for shape, _ in self.items]
        pieces = [(shape[0] // 8, shape[1]) if axis == 0 else (shape[0], shape[1] // 8) for shape, axis in self.items]
        self.scratch_shapes = ([pltpu.VMEM((8,) + p, F32) for p in pieces] + [pltpu.VMEM(p, F32) for p in pieces] + [
            pltpu.SemaphoreType.DMA((2 * n, 8)), pltpu.SemaphoreType.DMA((2 * n, 8)), pltpu.SemaphoreType.DMA((2 * n,))])

    def middle_at(self, steps):
        return steps // 2

    def _copies(self, ins, outs, scratch):
        n = len(self.items)
        landed, sums, (send_sems, recv_sems, loc_sems) = scratch[:n], scratch[n:2 * n], scratch[2 * n:]
        x, y, c, _ = _place()
        me = _device_index((x, y), c)

        def peer(r):
            return (1 - x if r & 4 else x, 1 - y if r & 2 else y, 1 - c if r & 1 else c)

        def piece(i, ref, p):
            shape, axis = self.items[i]
            if axis == 0:
                rows = shape[0] // 8
                return ref.at[pl.ds(pl.multiple_of(p * rows, 8), rows), :]
            cols = shape[1] // 8
            return ref.at[:, pl.ds(pl.multiple_of(p * cols, 128), cols)]

        own, scatter, arrivals, keep, spread, late = [], [], [], [], [], []
        for i in range(n):
            own.append(pltpu.make_async_copy(piece(i, ins[i], me), landed[i].at[0], loc_sems.at[2 * i]))
            keep.append(pltpu.make_async_copy(sums[i], piece(i, outs[i], me), loc_sems.at[2 * i + 1]))
            for r in range(1, 8):
                to = peer(r)
                p = _device_index(to[:2], to[2])
                scatter.append(_remote(piece(i, ins[i], p), landed[i].at[r], send_sems, recv_sems, (2 * i, r), to))
                spread.append(_remote(sums[i], piece(i, outs[i], me), send_sems, recv_sems, (2 * i + 1, r), to))
                late.append(_remote(sums[i], piece(i, outs[i], p), send_sems, recv_sems, (2 * i + 1, r), to))
        return own, scatter, keep, spread, late, landed, sums

    def start(self, ins, outs, scratch):
        own, scatter, _, _, _, _, _ = self._copies(ins, outs, scratch)
        for cp in own + scatter:
            cp.start()

    def middle(self, ins, outs, scratch):
        own, scatter, keep, spread, _, landed, sums = self._copies(ins, outs, scratch)
        for cp in own:
            cp.wait()
        for cp in scatter:
            cp.wait_recv()
        for i in range(len(self.items)):
            total = landed[i][0]
            for r in range(1, 8):
                total = total + landed[i][r]
            sums[i][...] = total
        for cp in keep + spread:
            cp.start()

    def finish(self, ins, outs, scratch):
        _, scatter, keep, spread, late, _, _ = self._copies(ins, outs, scratch)
        for cp in late:
            cp.wait_recv()
        for cp in scatter + spread:
            cp.wait_send()
        for cp in keep:
            cp.wait()


class _Both:
    def __init__(self, a, b):
        self.a, self.b = a, b
        self.in_specs, self.out_specs = a.in_specs + b.in_specs, a.out_specs + b.out_specs
        self.out_shape, self.scratch_shapes = a.out_shape + b.out_shape, a.scratch_shapes + b.scratch_shapes

    def middle_at(self, steps):
        return min(e.middle_at(steps) for e in (self.a, self.b) if e.middle is not None)

    def _each(self, ins, outs, scratch):
        a = self.a
        i, o, s = len(a.in_specs), len(a.out_specs), len(a.scratch_shapes)
        return (a, ins[:i], outs[:o], scratch[:s]), (self.b, ins[i:], outs[o:], scratch[s:])

    def start(self, ins, outs, scratch):
        for e, i, o, s in self._each(ins, outs, scratch):
            e.start(i, o, s)

    def middle(self, ins, outs, scratch):
        for e, i, o, s in self._each(ins, outs, scratch):
            if e.middle is not None:
                e.middle(i, o, s)

    def finish(self, ins, outs, scratch):
        for e, i, o, s in self._each(ins, outs, scratch):
            e.finish(i, o, s)


def _all_reduce(arrays, items, name):
    reduce = _AllReduce(items)
    n = len(items)

    def body(*refs):
        ins, outs, scratch = refs[:n], refs[n:2 * n], refs[2 * n:]
        reduce.start(ins, outs, scratch)
        reduce.middle(ins, outs, scratch)
        reduce.finish(ins, outs, scratch)

    return pl.pallas_call(
        body, name=name, in_specs=reduce.in_specs, out_specs=reduce.out_specs, out_shape=reduce.out_shape,
        scratch_shapes=reduce.scratch_shapes,
    )(*arrays)


def _adam_small(grads, wmv):
    n = len(grads)

    def body(*refs):
        g_refs, rest = refs[:n], refs[n:]
        ins, outs = rest[:3 * n], rest[3 * n:]
        for i in range(n):
            d, m, v = _adamw(ins[3 * i][...], g_refs[i][...], ins[3 * i + 1][...], ins[3 * i + 2][...])
            outs[3 * i][...], outs[3 * i + 1][...], outs[3 * i + 2][...] = d, m, v

    flat = [a for t in wmv for a in t]
    return pl.pallas_call(
        body, name="adam_small",
        in_specs=[VMEM_SPEC] * (4 * n), out_specs=[VMEM_SPEC] * (3 * n),
        out_shape=[jax.ShapeDtypeStruct(a.shape, F32) for a in flat],
    )(*grads, *flat)


WEIGHT_NAMES = ("norm_mix", "w_in", "w_pool_grp", "pool_scale", "w_pool_out", "conv_w", "conv_b", "w_rg_a", "b_rg_a", "w_rg_x",
                "b_rg_x", "lru_lambda", "w_rnn_out", "w_o", "norm_ffn", "w_ffn_in", "w_ffn_out", "norm_final")


def kernel(x, norm_mix, w_in, w_pool_grp, pool_scale, w_pool_out, conv_w, conv_b, w_rg_a, b_rg_a, w_rg_x, b_rg_x, lru_lambda, w_rnn_out, w_o, norm_ffn, w_ffn_in, w_ffn_out, norm_final, loss_target, m_norm_mix, m_w_in, m_w_pool_grp, m_pool_scale, m_w_pool_out, m_conv_w, m_conv_b, m_w_rg_a, m_b_rg_a, m_w_rg_x, m_b_rg_x, m_lru_lambda, m_w_rnn_out, m_w_o, m_norm_ffn, m_w_ffn_in, m_w_ffn_out, m_norm_final, v_norm_mix, v_w_in, v_w_pool_grp, v_pool_scale, v_w_pool_out, v_conv_w, v_conv_b, v_w_rg_a, v_b_rg_a, v_w_rg_x, v_b_rg_x, v_lru_lambda, v_w_rnn_out, v_w_o, v_norm_ffn, v_w_ffn_in, v_w_ffn_out, v_norm_final):
    w = dict(norm_mix=norm_mix, w_in=w_in, w_pool_grp=w_pool_grp, pool_scale=pool_scale, w_pool_out=w_pool_out, conv_w=conv_w,
             conv_b=conv_b, w_rg_a=w_rg_a, b_rg_a=b_rg_a, w_rg_x=w_rg_x, b_rg_x=b_rg_x, lru_lambda=lru_lambda,
             w_rnn_out=w_rnn_out, w_o=w_o, norm_ffn=norm_ffn, w_ffn_in=w_ffn_in, w_ffn_out=w_ffn_out, norm_final=norm_final)
    m = dict(norm_mix=m_norm_mix, w_in=m_w_in, w_pool_grp=m_w_pool_grp, pool_scale=m_pool_scale, w_pool_out=m_w_pool_out,
             conv_w=m_conv_w, conv_b=m_conv_b, w_rg_a=m_w_rg_a, b_rg_a=m_b_rg_a, w_rg_x=m_w_rg_x, b_rg_x=m_b_rg_x,
             lru_lambda=m_lru_lambda, w_rnn_out=m_w_rnn_out, w_o=m_w_o, norm_ffn=m_norm_ffn, w_ffn_in=m_w_ffn_in,
             w_ffn_out=m_w_ffn_out, norm_final=m_norm_final)
    v = dict(norm_mix=v_norm_mix, w_in=v_w_in, w_pool_grp=v_w_pool_grp, pool_scale=v_pool_scale, w_pool_out=v_w_pool_out,
             conv_w=v_conv_w, conv_b=v_conv_b, w_rg_a=v_w_rg_a, b_rg_a=v_b_rg_a, w_rg_x=v_w_rg_x, b_rg_x=v_b_rg_x,
             lru_lambda=v_lru_lambda, w_rnn_out=v_w_rnn_out, w_o=v_w_o, norm_ffn=v_norm_ffn, w_ffn_in=v_w_ffn_in,
             w_ffn_out=v_w_ffn_out, norm_final=v_norm_final)
    xi, yi, ci = (lax.axis_index(a) for a in MESH_AXES)
    chip = 2 * xi + yi

    def held(T, a):
        return jnp.swapaxes(a, 0, 1) if T.transposed else a

    where = jnp.stack([2 * chip + ci]).astype(jnp.int32)
    by_name = {T.name: T for T in GATHERED}
    block = {T.name: held(T, w[T.name][0]).astype(T.dtype) for T in BIG}
    block["conv_w"] = jnp.pad(conv_w[0], ((0, CONV_W.rows - 4), (0, 0)))
    block["w_ffn_in_lo"], block["w_ffn_in_hi"] = block["w_ffn_in"][:, :D // 2], block["w_ffn_in"][:, D // 2:]

    def gather_of(*names):
        return dict(exchange=_Gather([by_name[n] for n in names]), exchange_operands=[block[n] for n in names])

    def pair_sums(names, partials, tag):
        out = _pair_reduce(partials, [by_name[n] for n in names], "grad_pair_reduce_" + tag)
        return list(out[:len(names)]), list(out[len(names):])

    xs, target = x[0], loss_target[0]
    wg_b, wa_b, wx_b = (a[0].astype(BF16) for a in (w_pool_grp, w_rg_a, w_rg_x))
    ba2, bx2 = b_rg_a.reshape(1, DR), b_rg_x.reshape(1, DR)
    (w_in_g,) = _all_gather([block["w_in"]], [by_name["w_in"]], "all_gather_w_in")
    (proj, h1), (w_pool_out_g, w_rnn_out_g, w_o_g, conv_g) = _in_proj(
        xs, norm_mix, w_in_g, **gather_of("w_pool_out", "w_rnn_out", "w_o", "conv_w"))
    mixer_weights = (wg_b, pool_scale, w_pool_out_g, conv_g[0:4], conv_b, wa_b, ba2, wx_b, bx2, lru_lambda, w_rnn_out_g)
    (_, pm, y_pool, hr, z, y_rnn, kept), (w_ffn_hi_g,) = _mixer_fwd(proj, *mixer_weights, **gather_of("w_ffn_in_hi"))
    (mix, x2, h2), (w_ffn_lo_g,) = _merge_out(xs, proj, y_pool, y_rnn, w_o_g, norm_ffn, **gather_of("w_ffn_in_lo"))
    (gu, act), (w_ffn_out_g,) = _ffn_up(h2, w_ffn_lo_g, w_ffn_hi_g, **gather_of("w_ffn_out"))
    dx3, dx3b, loss_part, dvec_fin = _ffn_down_loss(act, x2, target, w_ffn_out_g, norm_final.reshape(1, D))

    dgu = _ffn_bwd_down(dx3b, gu, w_ffn_out_g)
    dx2, dx2b, dmixo, dvec_ffn = _ffn_bwd_up(dgu, x2, dx3, w_ffn_lo_g, w_ffn_hi_g, norm_ffn, w_o_g)
    names_a = ("w_ffn_in", "w_ffn_out", "w_o")
    part_a = [_wgrad(dgu, h2, "wgrad_ffn_in", 1408, 512), _wgrad(act, dx3b, "wgrad_ffn_out", 1408, 512),
              _wgrad(mix, dx2b, "wgrad_o", 1024, 1024)]
    lz1_a, sums_a = pair_sums(names_a, part_a, "ffn")
    (dproj, dypb, dyrb, dmat, dvec_mix), lz2_a = _mixer_bwd(
        proj, dmixo, y_pool, y_rnn, hr, kept, *mixer_weights,
        exchange=_Scatter([by_name[n] for n in names_a]), exchange_operands=sums_a)
    names_b = ("w_pool_out", "w_rnn_out")
    part_b = [_wgrad(pm, dypb, "wgrad_pool_out", 512, 1024), _wgrad(z, dyrb, "wgrad_rnn_out", 1024, 1024)]
    lz1_b, sums_b = pair_sums(names_b, part_b, "mix")
    dvec = jnp.concatenate([dvec_mix[0:9], dvec_fin[0:1], dvec_ffn[0:1], jnp.pad(loss_part, ((0, 0), (0, DR - 1))),
                            jnp.zeros((VEC_ROWS - 12, DR), F32)], axis=0)
    g_in, exchanged = _wgrad(
        dproj, h1, "wgrad_in", 1152, 1024,
        exchange=_Both(_Scatter([by_name[n] for n in names_b]), _AllReduce([((MAT_ROWS, HD), 0), ((VEC_ROWS, DR), 1)])),
        exchange_operands=sums_b + [dmat, dvec])
    lz2_b, (mat, vec) = exchanged[:2], exchanged[2:]
    loss = vec[VEC_LOSS, 0]
    lz1_c, sums_c = pair_sums(("w_in",), [g_in], "in")
    (grad_x, dvec_in), lz2_c = _in_bwd(dproj, xs, dx2, norm_mix, w_in_g,
                                       exchange=_Scatter([by_name["w_in"]]), exchange_operands=sums_c)
    (vec_in,) = _all_reduce([dvec_in], [((8, D), 1)], "all_reduce_norm_mix")

    grads, delta, new_m, new_v = {}, {}, {}, {}
    for n, g, l1, l2 in zip(names_a + names_b + ("w_in",), part_a + part_b + [g_in], lz1_a + lz1_b + lz1_c,
                            lz2_a + lz2_b + lz2_c):
        T = by_name[n]
        out = _final_sum(T, g, l1, l2, where, held(T, w[n][0]), held(T, m[n][0]), held(T, v[n][0]))
        grads[n], delta[n], new_m[n], new_v[n] = (held(T, a) for a in out)
    me = 4 * xi + 2 * yi + ci
    small_grads = dict(
        w_pool_grp=mat[0:MAT_WA], w_rg_a=mat[MAT_WA:MAT_WX], w_rg_x=mat[MAT_WX:MAT_ROWS],
        pool_scale=vec[VEC_SCALE:VEC_SCALE + 1, 0:DP], conv_b=vec[VEC_CONV_B:VEC_CONV_B + 1],
        b_rg_a=vec[VEC_BA:VEC_BA + 1], b_rg_x=vec[VEC_BX:VEC_BX + 1], lru_lambda=vec[VEC_LAM:VEC_LAM + 1],
        conv_w=lax.dynamic_slice(vec, (VEC_CONV_W, VEC_PIECE * me), (4, VEC_PIECE)),
        norm_final=vec[VEC_NORM_FINAL:VEC_NORM_FINAL + 1], norm_ffn=vec[VEC_NORM_FFN:VEC_NORM_FFN + 1],
        norm_mix=vec_in[0:1])
    names = list(small_grads)
    as2d = lambda a, g: a.reshape(g.shape)
    upd = _adam_small([small_grads[n] for n in names],
                      [(as2d(w[n], small_grads[n]), as2d(m[n], small_grads[n]), as2d(v[n], small_grads[n])) for n in names])
    for i, n in enumerate(names):
        grads[n] = small_grads[n]
        delta[n], new_m[n], new_v[n] = upd[3 * i:3 * i + 3]

    shaped = lambda d: [d[n].reshape(w[n].shape) for n in WEIGHT_NAMES]
    return (loss, grad_x[None], *shaped(grads), *shaped(delta), *shaped(new_m), *shaped(new_v))
```

```python
import functools
import math

import jax
import jax.numpy as jnp
from jax import lax
from jax.experimental import pallas as pl
from jax.experimental.pallas import tpu as pltpu

F32 = jnp.float32
BF16 = jnp.bfloat16

D = 1024
DP = 512
PG = 128
WINDOWS = (2, 4, 8, 16)
DR = 1024
NH = 8
HD = 128
DIN = 4608
DFF = 2816
EPS = 1e-6
LRU_C = 8.0
POOL_HALO = 16
CONV_HALO = 8
KEPT = 5

ADAM_LR = 0.001
ADAM_B1 = 0.9
ADAM_B2 = 0.999
ADAM_EPS = 1e-08
ADAM_WD = 0.01
ADAM_STEP = 10

VMEM_LIMIT = 56 * 1024 * 1024
MESH_AXES = ("x", "y", "c")
MESH = pl.DeviceIdType.MESH


def _dot(a, b):
    return jnp.dot(a, b, preferred_element_type=F32)


def _dot_nt(a, b):
    return lax.dot_general(a, b, (((1,), (1,)), ((), ())), preferred_element_type=F32)


def _dot_tn(a, b):
    return lax.dot_general(a, b, (((0,), (0,)), ((), ())), preferred_element_type=F32)


def _params(*sem):
    return pltpu.CompilerParams(dimension_semantics=sem, vmem_limit_bytes=VMEM_LIMIT)


def _resident(shape):
    nd = len(shape)
    return pl.BlockSpec(shape, lambda i: (0,) * nd, pipeline_mode=pl.Buffered(1))


def _rows(shape_cols, tm):
    return pl.BlockSpec((tm, shape_cols), lambda i: (i, 0))


def _call(body, name, grid, in_specs, out_specs, out_shape, operands, scratch_shapes=(), exchange=None, exchange_operands=()):
    n_in, n_out, n_scr = len(in_specs), len(out_specs), len(scratch_shapes)
    steps = math.prod(grid)
    if exchange is None:
        outs = pl.pallas_call(body, name=name, grid=grid, in_specs=in_specs, out_specs=out_specs, out_shape=out_shape,
                              scratch_shapes=list(scratch_shapes), compiler_params=_params(*["arbitrary"] * len(grid)))(*operands)
        return outs, []
    e_in, e_out = len(exchange.in_specs), len(exchange.out_specs)

    def hosted(*refs):
        ins, refs = refs[:n_in], refs[n_in:]
        e_ins, refs = refs[:e_in], refs[e_in:]
        outs, refs = refs[:n_out], refs[n_out:]
        e_outs, refs = refs[:e_out], refs[e_out:]
        scr, e_scr = refs[:n_scr], refs[n_scr:]
        step = pl.program_id(0)
        for axis in range(1, len(grid)):
            step = step * grid[axis] + pl.program_id(axis)
        pl.when(step == 0)(lambda: exchange.start(e_ins, e_outs, e_scr))
        if exchange.middle is not None:
            pl.when(step == exchange.middle_at(steps))(lambda: exchange.middle(e_ins, e_outs, e_scr))
        body(*ins, *outs, *scr)
        pl.when(step == steps - 1)(lambda: exchange.finish(e_ins, e_outs, e_scr))

    outs = pl.pallas_call(
        hosted, name=name, grid=grid, in_specs=list(in_specs) + exchange.in_specs,
        out_specs=list(out_specs) + exchange.out_specs, out_shape=list(out_shape) + exchange.out_shape,
        scratch_shapes=list(scratch_shapes) + exchange.scratch_shapes,
        compiler_params=_params(*["arbitrary"] * len(grid)))(*operands, *exchange_operands)
    return outs[:n_out], outs[n_out:]


GELU_C = math.sqrt(2.0 / math.pi)
GELU_K = 0.044715 * GELU_C


def _gelu(x, with_grad=False):
    x2 = x * x
    t = jnp.tanh(x * (GELU_C + GELU_K * x2))
    hx = 0.5 * x
    y = hx + hx * t
    if not with_grad:
        return y
    return y, 0.5 + 0.5 * t + hx * (1.0 - t * t) * (GELU_C + (3.0 * GELU_K) * x2)


def _softplus_neg(lam):
    z = jnp.exp(-jnp.abs(lam))
    u = 1.0 + z
    dlt = u - 1.0
    log1p = jnp.where(dlt == 0.0, z, jnp.log(u) * (z / jnp.where(dlt == 0.0, 1.0, dlt)))
    return jnp.maximum(-lam, 0.0) + log1p


def _sigmoid(x):
    return 0.5 * jnp.tanh(0.5 * x) + 0.5


def _linear_scan(out_ref, A, B, h0, reverse):
    n = A.shape[0]
    sub = lax.broadcasted_iota(jnp.int32, (8, 1), 0)
    tiles = range(n // 8 - 1, -1, -1) if reverse else range(n // 8)
    carry = h0
    for j in tiles:
        a, b = A[8 * j:8 * j + 8, :], B[8 * j:8 * j + 8, :]
        for d in (1, 2, 4):
            keep = (sub < 8 - d) if reverse else (sub >= d)
            shift = 8 - d if reverse else d
            b = jnp.where(keep, a * pltpu.roll(b, shift, axis=0) + b, b)
            a = jnp.where(keep, a * pltpu.roll(a, shift, axis=0), a)
        h = a * carry + b
        out_ref[8 * j:8 * j + 8, :] = h
        carry = h[0:1, :] if reverse else h[7:8, :]
    return carry


def _pool_windows(ext, shift_sign):
    n = ext.shape[0]
    s = ext
    outs = []
    for w in WINDOWS:
        d = w // 2
        s = s + pltpu.roll(s, d if shift_sign > 0 else n - d, axis=0)
        outs.append(s[:, :PG])
        s = s[:, PG:]
    return outs


def _conv_taps(uext):
    taps = []
    for k in range(4):
        sh = 3 - k
        v = uext if sh == 0 else pltpu.roll(uext, sh, axis=0)
        taps.append(v[CONV_HALO:, :])
    return taps


def _gates(v, wa_ref, ba_ref, wx_ref, bx_ref, sp):
    vb = v.astype(BF16)
    ra, rx = [], []
    for h in range(NH):
        vh = vb[:, h * HD:(h + 1) * HD]
        ra.append(_dot(vh, wa_ref[h]))
        rx.append(_dot(vh, wx_ref[h]))
    r = _sigmoid(jnp.concatenate(ra, axis=1) + ba_ref[...])
    i = _sigmoid(jnp.concatenate(rx, axis=1) + bx_ref[...])
    log_a = r * ((-LRU_C) * sp)
    a = jnp.exp(log_a)
    one_minus = -jnp.tanh(log_a) * (1.0 + a * a)
    return r, i, a, jnp.sqrt(one_minus), lax.rsqrt(one_minus)


def _in_proj(x, norm_mix, w_in, exchange=None, exchange_operands=(), tm=512):
    S = x.shape[0]

    def body(x_ref, g_ref, w_ref, proj_ref, h_ref):
        xv = x_ref[...]
        r = lax.rsqrt(jnp.mean(xv * xv, axis=-1, keepdims=True) + EPS)
        h = (xv * r * g_ref[...]).astype(BF16)
        h_ref[...] = h
        for n0 in range(0, DIN, 512):
            proj_ref[:, n0:n0 + 512] = _dot_nt(h, w_ref[n0:n0 + 512, :])

    return _call(
        body, "in_proj", (S // tm,),
        in_specs=[_rows(D, tm), _resident((1, D)), _resident((DIN, D))],
        out_specs=[_rows(DIN, tm), _rows(D, tm)],
        out_shape=[jax.ShapeDtypeStruct((S, DIN), F32), jax.ShapeDtypeStruct((S, D), BF16)],
        operands=(x, norm_mix, w_in), exchange=exchange, exchange_operands=exchange_operands)


def _mixer_fwd(proj, wg, scale, w_pool_out, conv_w, conv_b, wa, ba, wx, bx, lam, w_rnn_out, exchange=None,
               exchange_operands=(), tm=256):
    S = proj.shape[0]
    UW = DP + 2 * DR

    def body(proj_ref, wg_ref, scale_ref, wpo_ref, cw_ref, cb_ref, wa_ref, ba_ref, wx_ref, bx_ref, lam_ref, wro_ref,
             pooled_ref, pm_ref, ypool_ref, hr_ref, z_ref, yrnn_ref, kept_ref, pool_carry, conv_carry, h_carry):
        i = pl.program_id(0)

        @pl.when(i == 0)
        def _():
            pool_carry[...] = jnp.zeros_like(pool_carry)
            conv_carry[...] = jnp.zeros_like(conv_carry)
            h_carry[...] = jnp.zeros_like(h_carry)

        rows = lax.broadcasted_iota(jnp.int32, (tm, 1), 0)
        t_glob = i * tm + rows

        u_pool = proj_ref[:, 0:DP]
        ext = jnp.concatenate([pool_carry[...], u_pool], axis=0)
        pool_carry[...] = u_pool[tm - POOL_HALO:, :]
        sums = _pool_windows(ext, +1)
        mixed = []
        for g, w in enumerate(WINDOWS):
            inv_cnt = 1.0 / jnp.minimum(t_glob + 1, w).astype(F32)
            pooled_g = sums[g][POOL_HALO:, :] * inv_cnt - u_pool[:, g * PG:(g + 1) * PG]
            pooled_b = pooled_g.astype(BF16)
            pooled_ref[:, g * PG:(g + 1) * PG] = pooled_b
            mixed.append(_dot(pooled_b, wg_ref[g]))
        pm = (jnp.concatenate(mixed, axis=1) * scale_ref[...]).astype(BF16)
        pm_ref[...] = pm
        ypool_ref[...] = _dot(pm, wpo_ref[...])

        u_rnn = proj_ref[:, DP:DP + DR]
        uext = jnp.concatenate([conv_carry[...], u_rnn], axis=0)
        conv_carry[...] = u_rnn[tm - CONV_HALO:, :]
        taps = _conv_taps(uext)
        v = cb_ref[...]
        for k in range(4):
            v = v + taps[k] * cw_ref[k:k + 1, :]
        sp = _softplus_neg(lam_ref[...])
        r, gi, a, mult, _ = _gates(v, wa_ref, ba_ref, wx_ref, bx_ref, sp)
        for k, kept in enumerate((v, r, gi, a, mult)):
            kept_ref[k] = kept
        h_carry[0:1, :] = _linear_scan(hr_ref, a, mult * gi * v, h_carry[0:1, :], reverse=False)
        z = (hr_ref[...] * _gelu(proj_ref[:, DP + DR:UW])).astype(BF16)
        z_ref[...] = z
        yrnn_ref[...] = _dot(z, wro_ref[...])

    return _call(
        body, "mixer_fwd", (S // tm,),
        in_specs=[_rows(UW, tm), _resident((4, PG, PG)), _resident((1, DP)), _resident((DP, D)), _resident((4, DR)),
                  _resident((1, DR)), _resident((NH, HD, HD)), _resident((1, DR)), _resident((NH, HD, HD)),
                  _resident((1, DR)), _resident((1, DR)), _resident((DR, D))],
        out_specs=[_rows(DP, tm), _rows(DP, tm), _rows(D, tm), _rows(DR, tm), _rows(DR, tm), _rows(D, tm),
                   pl.BlockSpec((KEPT, tm, DR), lambda i: (0, i, 0))],
        out_shape=[jax.ShapeDtypeStruct((S, DP), BF16), jax.ShapeDtypeStruct((S, DP), BF16),
                   jax.ShapeDtypeStruct((S, D), F32), jax.ShapeDtypeStruct((S, DR), F32),
                   jax.ShapeDtypeStruct((S, DR), BF16), jax.ShapeDtypeStruct((S, D), F32),
                   jax.ShapeDtypeStruct((KEPT, S, DR), F32)],
        scratch_shapes=[pltpu.VMEM((POOL_HALO, DP), F32), pltpu.VMEM((CONV_HALO, DR), F32), pltpu.VMEM((8, DR), F32)],
        operands=(proj, wg, scale, w_pool_out, conv_w, conv_b, wa, ba, wx, bx, lam, w_rnn_out),
        exchange=exchange, exchange_operands=exchange_operands)


FF_CHUNKS = ((0, 768), (768, 1536), (1536, 2304), (2304, DFF))


def _rms(x):
    r = lax.rsqrt(jnp.mean(x * x, axis=-1, keepdims=True) + EPS)
    return r, x * r


def _rms_bwd(dh, g, r, xh):
    dxh = dh * g
    return r * (dxh - xh * jnp.mean(dxh * xh, axis=-1, keepdims=True))


def _merge_out(x, proj, y_pool, y_rnn, w_o, norm_ffn, exchange=None, exchange_operands=(), tm=512):
    S = x.shape[0]
    GL0 = (DP + 2 * DR) // 512

    def gl_spec(k):
        return pl.BlockSpec((tm, 512), lambda i: (i, GL0 + k))

    def body(x_ref, gl0, gl1, gl2, gl3, yp_ref, yr_ref, wo_ref, gf_ref, mix_ref, x2_ref, h2_ref):
        s_p = _sigmoid(jnp.concatenate([gl0[...], gl1[...]], axis=1))
        s_r = _sigmoid(jnp.concatenate([gl2[...], gl3[...]], axis=1))
        mix = (s_p * yp_ref[...] + s_r * yr_ref[...]).astype(BF16)
        mix_ref[...] = mix
        x2 = x_ref[...] + _dot(mix, wo_ref[...])
        x2_ref[...] = x2
        _, xh2 = _rms(x2)
        h2_ref[...] = (xh2 * gf_ref[...]).astype(BF16)

    return _call(
        body, "merge_out", (S // tm,),
        in_specs=[_rows(D, tm), gl_spec(0), gl_spec(1), gl_spec(2), gl_spec(3), _rows(D, tm), _rows(D, tm),
                  _resident((D, D)), _resident((1, D))],
        out_specs=[_rows(D, tm), _rows(D, tm), _rows(D, tm)],
        out_shape=[jax.ShapeDtypeStruct((S, D), BF16), jax.ShapeDtypeStruct((S, D), F32), jax.ShapeDtypeStruct((S, D), BF16)],
        operands=(x, proj, proj, proj, proj, y_pool, y_rnn, w_o, norm_ffn),
        exchange=exchange, exchange_operands=exchange_operands)


def _ffn_up(h2, w_lo, w_hi, exchange=None, exchange_operands=(), tm=512):
    S = h2.shape[0]
    HALF = D // 2

    def body(h_ref, lo_ref, hi_ref, gu_ref, act_ref):
        h_lo, h_hi = h_ref[:, 0:HALF], h_ref[:, HALF:D]
        for c0, c1 in FF_CHUNKS:
            gate = _dot_nt(h_lo, lo_ref[c0:c1, :]) + _dot_nt(h_hi, hi_ref[c0:c1, :])
            up = _dot_nt(h_lo, lo_ref[DFF + c0:DFF + c1, :]) + _dot_nt(h_hi, hi_ref[DFF + c0:DFF + c1, :])
            gu_ref[:, c0:c1] = gate.astype(BF16)
            gu_ref[:, DFF + c0:DFF + c1] = up.astype(BF16)
            act_ref[:, c0:c1] = (gate * _sigmoid(gate) * up).astype(BF16)

    return _call(
        body, "ffn_up", (S // tm,),
        in_specs=[_rows(D, tm), _resident((2 * DFF, HALF)), _resident((2 * DFF, HALF))],
        out_specs=[_rows(2 * DFF, tm), _rows(DFF, tm)],
        out_shape=[jax.ShapeDtypeStruct((S, 2 * DFF), BF16), jax.ShapeDtypeStruct((S, DFF), BF16)],
        operands=(h2, w_lo, w_hi), exchange=exchange, exchange_operands=exchange_operands)


def _ffn_down_loss(act, x2, target, w_ffn_out, norm_final, tm=512):
    S = act.shape[0]

    def body(act_ref, x2_ref, t_ref, w_ref, gn_ref, dx3_ref, dx3b_ref, loss_ref, dvec_ref):
        i = pl.program_id(0)

        @pl.when(i == 0)
        def _():
            loss_ref[...] = jnp.zeros_like(loss_ref)
            dvec_ref[...] = jnp.zeros_like(dvec_ref)

        x3 = x2_ref[...] + _dot(act_ref[...], w_ref[...])
        r3, xh3 = _rms(x3)
        g_fin = gn_ref[...]
        e = xh3 * g_fin - t_ref[...]
        loss_ref[...] += jnp.sum(e * e, axis=(0, 1), keepdims=True) * (0.5 / D)
        dy = e * (1.0 / D)
        dvec_ref[0:1, :] += jnp.sum(dy * xh3, axis=0, keepdims=True)
        dx3 = _rms_bwd(dy, g_fin, r3, xh3)
        dx3_ref[...] = dx3
        dx3b_ref[...] = dx3.astype(BF16)

    return pl.pallas_call(
        body, name="ffn_down_loss", grid=(S // tm,),
        in_specs=[_rows(DFF, tm), _rows(D, tm), _rows(D, tm), _resident((DFF, D)), _resident((1, D))],
        out_specs=[_rows(D, tm), _rows(D, tm), _resident((1, 1)), _resident((8, D))],
        out_shape=[jax.ShapeDtypeStruct((S, D), F32), jax.ShapeDtypeStruct((S, D), BF16),
                   jax.ShapeDtypeStruct((1, 1), F32), jax.ShapeDtypeStruct((8, D), F32)],
        compiler_params=_params("arbitrary"),
    )(act, x2, target, w_ffn_out, norm_final)


def _ffn_bwd_down(dx3b, gu, w_ffn_out, tm=512):
    S = dx3b.shape[0]

    def body(d_ref, gu_ref, w_ref, dgu_ref):
        d = d_ref[...]
        for c0, c1 in FF_CHUNKS:
            dact = _dot_nt(d, w_ref[c0:c1, :])
            gate = gu_ref[:, c0:c1].astype(F32)
            up = gu_ref[:, DFF + c0:DFF + c1].astype(F32)
            sg = _sigmoid(gate)
            dgu_ref[:, c0:c1] = (dact * up * (sg * (1.0 + gate * (1.0 - sg)))).astype(BF16)
            dgu_ref[:, DFF + c0:DFF + c1] = (dact * (gate * sg)).astype(BF16)

    return pl.pallas_call(
        body, name="ffn_bwd_down", grid=(S // tm,),
        in_specs=[_rows(D, tm), _rows(2 * DFF, tm), _resident((DFF, D))],
        out_specs=_rows(2 * DFF, tm),
        out_shape=jax.ShapeDtypeStruct((S, 2 * DFF), BF16),
        compiler_params=_params("parallel"),
    )(dx3b, gu, w_ffn_out)


def _ffn_bwd_up(dgu, x2, dx3, w_lo, w_hi, norm_ffn, w_o, tm=512):
    S = dgu.shape[0]
    HALF = D // 2

    def body(dgu_ref, x2_ref, dx3_ref, lo_ref, hi_ref, gf_ref, wo_ref, dx2_ref, dx2b_ref, dmixo_ref, dvec_ref):
        i = pl.program_id(0)

        @pl.when(i == 0)
        def _():
            dvec_ref[...] = jnp.zeros_like(dvec_ref)

        dgate, dup = dgu_ref[:, 0:DFF], dgu_ref[:, DFF:2 * DFF]
        dh2 = jnp.concatenate([_dot(dgate, w[0:DFF, :]) + _dot(dup, w[DFF:2 * DFF, :]) for w in (lo_ref, hi_ref)], axis=1)
        r2, xh2 = _rms(x2_ref[...])
        dvec_ref[0:1, :] += jnp.sum(dh2 * xh2, axis=0, keepdims=True)
        dx2 = dx3_ref[...] + _rms_bwd(dh2, gf_ref[...], r2, xh2)
        dx2_ref[...] = dx2
        dx2b = dx2.astype(BF16)
        dx2b_ref[...] = dx2b
        dmixo_ref[...] = _dot_nt(dx2b, wo_ref[...])

    return pl.pallas_call(
        body, name="ffn_bwd_up", grid=(S // tm,),
        in_specs=[_rows(2 * DFF, tm), _rows(D, tm), _rows(D, tm), _resident((2 * DFF, HALF)), _resident((2 * DFF, HALF)),
                  _resident((1, D)), _resident((D, D))],
        out_specs=[_rows(D, tm), _rows(D, tm), _rows(D, tm), _resident((8, D))],
        out_shape=[jax.ShapeDtypeStruct((S, D), F32), jax.ShapeDtypeStruct((S, D), BF16), jax.ShapeDtypeStruct((S, D), F32),
                   jax.ShapeDtypeStruct((8, D), F32)],
        compiler_params=_params("arbitrary"),
    )(dgu, x2, dx3, w_lo, w_hi, norm_ffn, w_o)


VEC_ROWS = 16
MAT_WA = 4 * PG
MAT_WX = MAT_WA + NH * HD
MAT_ROWS = MAT_WX + NH * HD


def _mixer_bwd(proj, dmixo, y_pool, y_rnn, hr, kept, wg, scale, w_pool_out, conv_w, conv_b, wa, ba, wx, bx, lam, w_rnn_out,
               exchange=None, exchange_operands=(), tm=256):
    S = proj.shape[0]
    nt = S // tm

    def rev(cols):
        return pl.BlockSpec((tm, cols), lambda i: (nt - 1 - i, 0))

    def halo(rows_, cols):
        per = tm // rows_
        return pl.BlockSpec((rows_, cols), lambda i: (jnp.maximum((nt - 1 - i) * per - 1, 0), 0))

    def body(proj_ref, projh_ref, dmixo_ref, yp_ref, yr_ref, hr_ref, hrh_ref, kept_ref, wg_ref, scale_ref, wpo_ref, cw_ref, cb_ref,
             wa_ref, ba_ref, wx_ref, bx_ref, lam_ref, wro_ref,
             dproj_ref, dypb_ref, dyrb_ref, dmat_ref, dvec_ref,
             q_carry, dv_carry, a_carry, g_carry, g_scr):
        i = pl.program_id(0)
        ti = nt - 1 - i

        @pl.when(i == 0)
        def _():
            q_carry[...] = jnp.zeros_like(q_carry)
            dv_carry[...] = jnp.zeros_like(dv_carry)
            a_carry[...] = jnp.zeros_like(a_carry)
            g_carry[...] = jnp.zeros_like(g_carry)
            dmat_ref[...] = jnp.zeros_like(dmat_ref)
            dvec_ref[...] = jnp.zeros_like(dvec_ref)

        rows = lax.broadcasted_iota(jnp.int32, (tm, 1), 0)
        t_glob = ti * tm + rows
        has_prev = (ti > 0).astype(F32)
        dmixo = dmixo_ref[...]

        s_p = _sigmoid(proj_ref[:, DP + 2 * DR:DP + 2 * DR + D])
        s_r = _sigmoid(proj_ref[:, DP + 2 * DR + D:DIN])
        dproj_ref[:, DP + 2 * DR:DP + 2 * DR + D] = (dmixo * yp_ref[...] * s_p * (1.0 - s_p)).astype(BF16)
        dproj_ref[:, DP + 2 * DR + D:DIN] = (dmixo * yr_ref[...] * s_r * (1.0 - s_r)).astype(BF16)
        dyp = (dmixo * s_p).astype(BF16)
        dyr = (dmixo * s_r).astype(BF16)
        dypb_ref[...] = dyp
        dyrb_ref[...] = dyr

        dz = _dot_nt(dyr, wro_ref[...])
        u_gate = proj_ref[:, DP + DR:DP + 2 * DR]
        gg, dgelu = _gelu(u_gate, with_grad=True)
        hr_t = hr_ref[...]
        dproj_ref[:, DP + DR:DP + 2 * DR] = (dz * hr_t * dgelu).astype(BF16)
        dhr = dz * gg

        sp = _softplus_neg(lam_ref[...])
        v, r, gi, a, mult = (kept_ref[k] for k in range(KEPT))
        inv_mult = 1.0 / mult

        C = jnp.where(rows == tm - 1, a_carry[0:1, :], pltpu.roll(a, tm - 1, axis=0))
        g_carry[0:1, :] = _linear_scan(g_scr, C, dhr, g_carry[0:1, :], reverse=True)
        a_carry[0:1, :] = a[0:1, :]
        g = g_scr[...]

        h_prev = jnp.where(rows == 0, hrh_ref[7:8, :] * has_prev, pltpu.roll(hr_t, 1, axis=0))
        da = g * h_prev
        gm = g * mult
        dmult = g * gi * v
        di = gm * v
        dv = gm * gi
        dlog_a = da * a - dmult * (a * a * inv_mult)
        dvec_ref[4:5, :] += jnp.sum(dlog_a * r, axis=0, keepdims=True)
        dra = (dlog_a * ((-LRU_C) * sp) * r * (1.0 - r))
        drx = di * gi * (1.0 - gi)
        dvec_ref[2:3, :] += jnp.sum(dra, axis=0, keepdims=True)
        dvec_ref[3:4, :] += jnp.sum(drx, axis=0, keepdims=True)
        drab = dra.astype(BF16)
        drxb = drx.astype(BF16)
        vb = v.astype(BF16)
        dvg = []
        for h in range(NH):
            sl = slice(h * HD, (h + 1) * HD)
            dvg.append(_dot_nt(drab[:, sl], wa_ref[h]) + _dot_nt(drxb[:, sl], wx_ref[h]))
            dmat_ref[MAT_WA + h * HD:MAT_WA + (h + 1) * HD, :] += _dot_tn(vb[:, sl], drab[:, sl])
            dmat_ref[MAT_WX + h * HD:MAT_WX + (h + 1) * HD, :] += _dot_tn(vb[:, sl], drxb[:, sl])
        dv = dv + jnp.concatenate(dvg, axis=1)
        dvec_ref[1:2, :] += jnp.sum(dv, axis=0, keepdims=True)
        dvext = jnp.concatenate([dv, dv_carry[...]], axis=0)
        dv_carry[...] = dv[0:CONV_HALO, :]
        n = tm + CONV_HALO
        u_rnn = proj_ref[:, DP:DP + DR]
        du_rnn = dv * cw_ref[3:4, :]
        dvec_ref[8:9, :] += jnp.sum(dv * u_rnn, axis=0, keepdims=True)
        for k in range(3):
            dv_k = pltpu.roll(dvext, n - (3 - k), axis=0)[0:tm, :]
            du_rnn = du_rnn + dv_k * cw_ref[k:k + 1, :]
            dvec_ref[5 + k:6 + k, :] += jnp.sum(dv_k * u_rnn, axis=0, keepdims=True)
        dproj_ref[:, DP:DP + DR] = du_rnn.astype(BF16)

        dpm = _dot_nt(dyp, wpo_ref[...])
        u_pool = proj_ref[:, 0:DP]
        ext = jnp.concatenate([projh_ref[:, 0:DP] * has_prev, u_pool], axis=0)
        sums = _pool_windows(ext, +1)
        scale_v = scale_ref[...]
        qs = []
        dpooled = []
        dscale = []
        for gi_, w in enumerate(WINDOWS):
            sl = slice(gi_ * PG, (gi_ + 1) * PG)
            inv_cnt = 1.0 / jnp.minimum(t_glob + 1, w).astype(F32)
            pooled_b = (sums[gi_][POOL_HALO:, :] * inv_cnt - u_pool[:, sl]).astype(BF16)
            mixed_g = _dot(pooled_b, wg_ref[gi_])
            dscale.append(jnp.sum(dpm[:, sl] * mixed_g, axis=0, keepdims=True))
            dmixed_b = (dpm[:, sl] * scale_v[:, sl]).astype(BF16)
            dmat_ref[gi_ * PG:(gi_ + 1) * PG, :] += _dot_tn(pooled_b, dmixed_b)
            dp_g = _dot_nt(dmixed_b, wg_ref[gi_])
            dpooled.append(dp_g)
            qs.append(dp_g * inv_cnt)
        dvec_ref[0:1, 0:DP] += jnp.concatenate(dscale, axis=1)
        q = jnp.concatenate(qs, axis=1)
        qext = jnp.concatenate([q, q_carry[...]], axis=0)
        q_carry[...] = q[0:POOL_HALO, :]
        tsum = _pool_windows(qext, -1)
        for gi_ in range(4):
            dproj_ref[:, gi_ * PG:(gi_ + 1) * PG] = (tsum[gi_][0:tm, :] - dpooled[gi_]).astype(BF16)

        @pl.when(i == nt - 1)
        def _():
            dvec_ref[4:5, :] = dvec_ref[4:5, :] * (LRU_C * _sigmoid(-lam_ref[...]))

    return _call(
        body, "mixer_bwd", (nt,),
        in_specs=[rev(DIN), halo(POOL_HALO, DIN), rev(D), rev(D), rev(D), rev(DR), halo(8, DR),
                  pl.BlockSpec((KEPT, tm, DR), lambda i: (0, nt - 1 - i, 0)), _resident((4, PG, PG)), _resident((1, DP)), _resident((DP, D)), _resident((4, DR)), _resident((1, DR)),
                  _resident((NH, HD, HD)), _resident((1, DR)), _resident((NH, HD, HD)), _resident((1, DR)),
                  _resident((1, DR)), _resident((DR, D))],
        out_specs=[rev(DIN), rev(D), rev(D), _resident((MAT_ROWS, HD)), _resident((VEC_ROWS, DR))],
        out_shape=[jax.ShapeDtypeStruct((S, DIN), BF16), jax.ShapeDtypeStruct((S, D), BF16),
                   jax.ShapeDtypeStruct((S, D), BF16), jax.ShapeDtypeStruct((MAT_ROWS, HD), F32),
                   jax.ShapeDtypeStruct((VEC_ROWS, DR), F32)],
        scratch_shapes=[pltpu.VMEM((POOL_HALO, DP), F32), pltpu.VMEM((CONV_HALO, DR), F32), pltpu.VMEM((8, DR), F32),
                        pltpu.VMEM((8, DR), F32), pltpu.VMEM((tm, DR), F32)],
        operands=(proj, proj, dmixo, y_pool, y_rnn, hr, hr, kept, wg, scale, w_pool_out, conv_w, conv_b, wa, ba, wx, bx, lam,
                  w_rnn_out),
        exchange=exchange, exchange_operands=exchange_operands)


def _in_bwd(dproj, x, dx2, norm_mix, w_in, exchange=None, exchange_operands=(), tm=512):
    S = x.shape[0]

    def body(dp_ref, x_ref, dx2_ref, g_ref, w_ref, dx_ref, dg_ref):
        i = pl.program_id(0)

        @pl.when(i == 0)
        def _():
            dg_ref[...] = jnp.zeros_like(dg_ref)

        dh = _dot(dp_ref[:, 0:1536], w_ref[0:1536, :])
        dh = dh + _dot(dp_ref[:, 1536:3072], w_ref[1536:3072, :])
        dh = dh + _dot(dp_ref[:, 3072:DIN], w_ref[3072:DIN, :])
        xv = x_ref[...]
        r = lax.rsqrt(jnp.mean(xv * xv, axis=-1, keepdims=True) + EPS)
        xh = xv * r
        dg_ref[0:1, :] += jnp.sum(dh * xh, axis=0, keepdims=True)
        dxh = dh * g_ref[...]
        dx_ref[...] = dx2_ref[...] + r * (dxh - xh * jnp.mean(dxh * xh, axis=-1, keepdims=True))

    return _call(
        body, "in_bwd", (S // tm,),
        in_specs=[_rows(DIN, tm), _rows(D, tm), _rows(D, tm), _resident((1, D)), _resident((DIN, D))],
        out_specs=[_rows(D, tm), _resident((8, D))],
        out_shape=[jax.ShapeDtypeStruct((S, D), F32), jax.ShapeDtypeStruct((8, D), F32)],
        operands=(dproj, x, dx2, norm_mix, w_in), exchange=exchange, exchange_operands=exchange_operands)


def _wgrad(a, b, name, tk, tn, exchange=None, exchange_operands=()):
    S, K = a.shape
    N = b.shape[1]

    def body(a_ref, b_ref, o_ref):
        o_ref[...] = _dot_tn(a_ref[...], b_ref[...]).astype(BF16)

    (out,), exchanged = _call(
        body, name, (K // tk, N // tn),
        in_specs=[pl.BlockSpec((S, tk), lambda k, n: (0, k)), pl.BlockSpec((S, tn), lambda k, n: (0, n))],
        out_specs=[pl.BlockSpec((tk, tn), lambda k, n: (k, n))],
        out_shape=[jax.ShapeDtypeStruct((K, N), BF16)],
        operands=(a, b), exchange=exchange, exchange_operands=exchange_operands)
    return (out, exchanged) if exchange is not None else out


VEC_SCALE, VEC_CONV_B, VEC_BA, VEC_BX, VEC_LAM, VEC_CONV_W, VEC_NORM_FINAL, VEC_NORM_FFN = 0, 1, 2, 3, 4, 5, 9, 10
VEC_LOSS = 11


class _Big:
    def __init__(self, name, rows, cols, axis, n, dtype=BF16, transposed=False):
        self.name, self.rows, self.cols, self.axis, self.n, self.dtype = name, rows, cols, axis, n, dtype
        self.transposed = transposed
        self.block_shape = (rows, n) if axis == 1 else (n, cols)

    def block(self, ref, p):
        if self.axis == 1:
            return ref.at[:, pl.ds(pl.multiple_of(p * self.n, 128), self.n)]
        return ref.at[pl.ds(pl.multiple_of(p * self.n, 16 if self.dtype == BF16 else 8), self.n), :]

    def block_index(self, p):
        return (0, p) if self.axis == 1 else (p, 0)


BIG = (_Big("w_in", DIN, D, 0, DIN // 8, transposed=True), _Big("w_pool_out", DP, D, 1, D // 8),
       _Big("w_rnn_out", DR, D, 0, DR // 8), _Big("w_o", D, D, 0, D // 8),
       _Big("w_ffn_in", 2 * DFF, D, 0, 2 * DFF // 8, transposed=True), _Big("w_ffn_out", DFF, D, 0, DFF // 8))
CONV_W = _Big("conv_w", 8, DR, 1, DR // 8, F32)
W_FFN_IN_HALVES = (_Big("w_ffn_in_lo", 2 * DFF, D // 2, 0, 2 * DFF // 8), _Big("w_ffn_in_hi", 2 * DFF, D // 2, 0, 2 * DFF // 8))
GATHERED = BIG + (CONV_W,) + W_FFN_IN_HALVES

HBM_SPEC = pl.BlockSpec(memory_space=pl.ANY)
VMEM_SPEC = pl.BlockSpec(memory_space=pltpu.VMEM)


def _place():
    x, y, c = (lax.axis_index(a) for a in MESH_AXES)
    other_chips = [(1 - x, y), (x, 1 - y), (1 - x, 1 - y)]
    return x, y, c, other_chips


def _remote(src, dst, send_sems, recv_sems, idx, to):
    return pltpu.make_async_remote_copy(src_ref=src, dst_ref=dst, send_sem=send_sems.at[idx], recv_sem=recv_sems.at[idx],
                                        device_id=to, device_id_type=MESH)


def _device_index(chip, core):
    return 4 * chip[0] + 2 * chip[1] + core


class _Gather:
    def __init__(self, tensors):
        self.tensors = tuple(tensors)
        n = len(self.tensors)
        self.in_specs = [HBM_SPEC] * n
        self.out_specs = [HBM_SPEC] * n
        self.out_shape = [jax.ShapeDtypeStruct((T.rows, T.cols), T.dtype) for T in self.tensors]
        self.scratch_shapes = [pltpu.VMEM(T.block_shape, T.dtype) for T in self.tensors] + [
            pltpu.SemaphoreType.DMA((n, 7)), pltpu.SemaphoreType.DMA((n, 7)), pltpu.SemaphoreType.DMA((n, 2))]

    def middle_at(self, steps):
        return steps - 1

    def _copies(self, ins, outs, scratch):
        n = len(self.tensors)
        mine, (send_sems, recv_sems, loc_sems) = scratch[:n], scratch[n:]
        x, y, c, chips = _place()
        sibling = (x, y, 1 - c)
        me = _device_index((x, y), c)
        loads, stores, first, passed, arrivals, late = [], [], [], [], [], []
        for t, T in enumerate(self.tensors):
            place = T.block(outs[t], me)
            loads.append(pltpu.make_async_copy(ins[t], mine[t], loc_sems.at[t, 0]))
            stores.append(pltpu.make_async_copy(mine[t], place, loc_sems.at[t, 1]))
            first.append(_remote(mine[t], place, send_sems, recv_sems, (t, 0), sibling))
            theirs = T.block(outs[t], _device_index((x, y), 1 - c))
            late.append(_remote(theirs, theirs, send_sems, recv_sems, (t, 0), sibling))
            for k, chip in enumerate(chips):
                first.append(_remote(mine[t], place, send_sems, recv_sems, (t, 1 + k), (*chip, c)))
                land = T.block(outs[t], _device_index(chip, c))
                arrivals.append(_remote(land, land, send_sems, recv_sems, (t, 1 + k), sibling))
                passed.append(_remote(land, land, send_sems, recv_sems, (t, 4 + k), sibling))
                theirs = T.block(outs[t], _device_index(chip, 1 - c))
                late.append(_remote(theirs, theirs, send_sems, recv_sems, (t, 4 + k), sibling))
        return loads, stores, first, passed, arrivals, late

    def start(self, ins, outs, scratch):
        loads, stores, first, _, _, _ = self._copies(ins, outs, scratch)
        for cp in loads:
            cp.start()
        for cp in loads:
            cp.wait()
        for cp in stores + first:
            cp.start()

    def middle(self, ins, outs, scratch, skip=0):
        _, _, _, passed, arrivals, _ = self._copies(ins, outs, scratch)
        for arrived, cp in zip(arrivals[3 * skip:], passed[3 * skip:]):
            arrived.wait_recv()
            cp.start()

    def finish(self, ins, outs, scratch, skip=0):
        _, stores, first, passed, _, late = self._copies(ins, outs, scratch)
        for cp in late[4 * skip:]:
            cp.wait_recv()
        for cp in first + passed:
            cp.wait_send()
        for cp in stores[skip:]:
            cp.wait()


def _all_gather(blocks, tensors, name):
    gather = _Gather(tensors)

    def body(*refs):
        n = len(gather.tensors)
        ins, outs, scratch = refs[:n], refs[n:2 * n], refs[2 * n:]
        gather.start(ins, outs, scratch)
        gather.middle(ins, outs, scratch)
        gather.finish(ins, outs, scratch)

    return pl.pallas_call(
        body, name=name, in_specs=gather.in_specs, out_specs=gather.out_specs, out_shape=gather.out_shape,
        scratch_shapes=gather.scratch_shapes, compiler_params=pltpu.CompilerParams(vmem_limit_bytes=VMEM_LIMIT),
    )(*blocks)


def _in_proj_gather(x, norm_mix, blocks, tensors, order, tm=512):
    S = x.shape[0]
    nt = S // tm
    n = len(tensors)
    gather = _Gather(tensors)
    CB = 2 * tensors[0].n

    def body(order_ref, x_ref, g_ref, *refs):
        ins, (proj_ref, h_ref), outs = refs[:n], refs[n:n + 2], refs[n + 2:2 * n + 2]
        (h_all, w_chip, w_sem), scratch = refs[2 * n + 2:2 * n + 5], refs[2 * n + 5:]
        q, i = pl.program_id(0), pl.program_id(1)
        _, stores, first, passed, arrivals, late = gather._copies(ins, outs, scratch)

        def fetch(chip):
            rows = outs[0].at[pl.ds(pl.multiple_of(chip * CB, 16), CB), :]
            cp = pltpu.make_async_copy(rows, w_chip, w_sem)
            cp.start()
            cp.wait()

        @pl.when((q == 0) & (i == 0))
        def _():
            gather.start(ins, outs, scratch)
            late[0].wait_recv()
            stores[0].wait()
            fetch(order_ref[0])

        for k in range(3):
            @pl.when((q == k + 1) & (i == 0))
            def _(k=k):
                arrivals[k].wait_recv()
                passed[k].start()
                late[1 + k].wait_recv()
                fetch(order_ref[k + 1])

        rows = pl.ds(pl.multiple_of(i * tm, tm), tm)

        @pl.when(q == 0)
        def _():
            xv = x_ref[...]
            r = lax.rsqrt(jnp.mean(xv * xv, axis=-1, keepdims=True) + EPS)
            h = (xv * r * g_ref[...]).astype(BF16)
            h_all[rows, :] = h
            h_ref[...] = h

        proj_ref[...] = _dot_nt(h_all[rows, :], w_chip[...])

        @pl.when((q == 3) & (i == nt - 1))
        def _():
            gather.middle(ins, outs, scratch, skip=1)
            gather.finish(ins, outs, scratch, skip=1)

    row_tile = lambda q, i, order: (jnp.where(q == 0, i, nt - 1), 0)
    whole = lambda shape: pl.BlockSpec(shape, lambda q, i, order: (0,) * len(shape), pipeline_mode=pl.Buffered(1))
    outs = pl.pallas_call(
        body, name="in_proj_gather",
        grid_spec=pltpu.PrefetchScalarGridSpec(
            num_scalar_prefetch=1, grid=(4, nt),
            in_specs=[pl.BlockSpec((tm, D), row_tile), whole((1, D))] + gather.in_specs,
            out_specs=[pl.BlockSpec((tm, CB), lambda q, i, order: (i, order[q])), pl.BlockSpec((tm, D), row_tile)]
            + gather.out_specs,
            scratch_shapes=[pltpu.VMEM((S, D), BF16), pltpu.VMEM((CB, D), BF16), pltpu.SemaphoreType.DMA]
            + gather.scratch_shapes),
        out_shape=[jax.ShapeDtypeStruct((S, DIN), F32), jax.ShapeDtypeStruct((S, D), BF16)] + gather.out_shape,
        compiler_params=_params("arbitrary", "arbitrary"),
    )(order, x, norm_mix, *blocks)
    return outs[:2], outs[2:]


PAIR_ROWS = 32


def _pair_reduce(grads, tensors, name):
    nt = len(tensors)

    def body(*refs):
        ins, own_out, sums_out, landed, mine = (refs[k * nt:(k + 1) * nt] for k in range(5))
        send_sems, recv_sems, loc_sems = refs[5 * nt:]
        x, y, c, chips = _place()
        chip_of = [2 * chip[0] + chip[1] for chip in chips]
        swaps, loads = [], []
        for t, T in enumerate(tensors):
            for j in range(4):
                swaps.append(_remote(T.block(ins[t], 2 * j + 1 - c), landed[t].at[j], send_sems, recv_sems, (t, j),
                                     (x, y, 1 - c)))
            for k in range(3):
                loads.append(pltpu.make_async_copy(T.block(ins[t], 2 * chip_of[k] + c), mine[t].at[k], loc_sems.at[t, k]))
        for cp in swaps + loads:
            cp.start()
        for cp in loads:
            cp.wait()
        for cp in swaps:
            cp.wait_recv()
        stores = []
        for t, T in enumerate(tensors):
            for k in range(3):
                acc, got = mine[t].at[k], landed[t].at[chip_of[k]]

                def add(i, carry, acc=acc, got=got):
                    rows = pl.ds(pl.multiple_of(i * PAIR_ROWS, PAIR_ROWS), PAIR_ROWS)
                    acc[rows, :] = (acc[rows, :].astype(F32) + got[rows, :].astype(F32)).astype(BF16)
                    return carry

                lax.fori_loop(0, T.block_shape[0] // PAIR_ROWS, add, 0)
            stores.append(pltpu.make_async_copy(mine[t], sums_out[t], loc_sems.at[t, 3]))
            stores.append(pltpu.make_async_copy(landed[t].at[2 * x + y], own_out[t], loc_sems.at[t, 4]))
        for cp in stores:
            cp.start()
        for cp in swaps:
            cp.wait_send()
        for cp in stores:
            cp.wait()

    blocks = [T.block_shape for T in tensors]
    return pl.pallas_call(
        body, name=name,
        in_specs=[HBM_SPEC] * nt, out_specs=[HBM_SPEC] * (2 * nt),
        out_shape=[jax.ShapeDtypeStruct(b, BF16) for b in blocks] + [jax.ShapeDtypeStruct((3,) + b, BF16) for b in blocks],
        scratch_shapes=[pltpu.VMEM((4,) + b, BF16) for b in blocks] + [pltpu.VMEM((3,) + b, BF16) for b in blocks]
        + [pltpu.SemaphoreType.DMA((nt, 4)), pltpu.SemaphoreType.DMA((nt, 4)), pltpu.SemaphoreType.DMA((nt, 5))],
        compiler_params=pltpu.CompilerParams(vmem_limit_bytes=VMEM_LIMIT),
    )(*grads)


class _Scatter:
    middle = None

    def __init__(self, tensors):
        n = len(tensors)
        self.in_specs = [HBM_SPEC] * n
        self.out_specs = [HBM_SPEC] * n
        self.out_shape = [jax.ShapeDtypeStruct((3,) + T.block_shape, BF16) for T in tensors]
        self.scratch_shapes = [pltpu.SemaphoreType.DMA((n, 3)), pltpu.SemaphoreType.DMA((n, 3))]

    def _copies(self, ins, outs, scratch):
        send_sems, recv_sems = scratch
        x, y, c, chips = _place()
        return [_remote(ins[t].at[k], outs[t].at[k], send_sems, recv_sems, (t, k), (*chip, c))
                for t in range(len(ins)) for k, chip in enumerate(chips)]

    def start(self, ins, outs, scratch):
        for cp in self._copies(ins, outs, scratch):
            cp.start()

    def finish(self, ins, outs, scratch):
        for cp in self._copies(ins, outs, scratch):
            cp.wait()


def _chip_scatter(sums, tensors, name):
    scatter = _Scatter(tensors)
    n = len(tensors)

    def body(*refs):
        ins, outs, scratch = refs[:n], refs[n:2 * n], refs[2 * n:]
        scatter.start(ins, outs, scratch)
        scatter.finish(ins, outs, scratch)

    return pl.pallas_call(
        body, name=name, in_specs=scatter.in_specs, out_specs=scatter.out_specs, out_shape=scatter.out_shape,
        scratch_shapes=scatter.scratch_shapes,
    )(*sums)


def _adamw(w, g, m, v):
    m = ADAM_B1 * m + (1.0 - ADAM_B1) * g
    v = ADAM_B2 * v + (1.0 - ADAM_B2) * (g * g)
    m_hat = m / (1.0 - ADAM_B1 ** ADAM_STEP)
    v_hat = v / (1.0 - ADAM_B2 ** ADAM_STEP)
    delta = -ADAM_LR * (m_hat / (jnp.sqrt(v_hat) + ADAM_EPS) + ADAM_WD * w)
    return delta, m, v


def _final_sum(T, g, lz1, lz2, where, w, m, v):
    rows, cols = T.block_shape
    sub = 4 if T.axis == 0 and rows % 64 == 0 and rows > 256 else 1
    blk = (rows // sub, cols)

    def body(where_ref, g_ref, l1_ref, l2_ref, w_ref, m_ref, v_ref, g_out, d_out, m_out, v_out):
        tot = g_ref[...].astype(F32) + l1_ref[...].astype(F32)
        for k in range(3):
            tot = tot + l2_ref[k].astype(F32)
        g_out[...] = tot
        d_out[...], m_out[...], v_out[...] = _adamw(w_ref[...], tot, m_ref[...], v_ref[...])

    def in_whole(r, wh):
        p = wh[0]
        return (0, p) if T.axis == 1 else (p * sub + r, 0)

    own = pl.BlockSpec(blk, lambda r, wh: (r, 0))
    return pl.pallas_call(
        body, name="grad_final_" + T.name,
        grid_spec=pltpu.PrefetchScalarGridSpec(
            num_scalar_prefetch=1, grid=(sub,),
            in_specs=[pl.BlockSpec(blk, in_whole),
                      own,
                      pl.BlockSpec((3,) + blk, lambda r, wh: (0, r, 0)), own, own, own],
            out_specs=[own] * 4),
        out_shape=[jax.ShapeDtypeStruct(T.block_shape, F32)] * 4,
        compiler_params=_params("arbitrary"),
    )(where, g, lz1, lz2, w, m, v)


MAT_PIECE = MAT_ROWS // 8
VEC_PIECE = DR // 8


class _AllReduce:
    def __init__(self, items):
        self.items = tuple(items)
        n = len(self.items)
        self.in_specs = [HBM_SPEC] * n
        self.out_specs = [HBM_SPEC] * n
        self.out_shape = [jax.ShapeDtypeStruct(shape, F32) for shape, _ in self.items]
        pieces = [(shape[0] // 8, shape[1]) if axis == 0 else (shape[0], shape[1] // 8) for shape, axis in self.items]
        self.scratch_shapes = ([pltpu.VMEM((8,) + p, F32) for p in pieces] + [pltpu.VMEM(p, F32) for p in pieces] + [
            pltpu.SemaphoreType.DMA((2 * n, 8)), pltpu.SemaphoreType.DMA((2 * n, 8)), pltpu.SemaphoreType.DMA((2 * n,))])

    def middle_at(self, steps):
        return steps // 2

    def _copies(self, ins, outs, scratch):
        n = len(self.items)
        landed, sums, (send_sems, recv_sems, loc_sems) = scratch[:n], scratch[n:2 * n], scratch[2 * n:]
        x, y, c, _ = _place()
        me = _device_index((x, y), c)

        def peer(r):
            return (1 - x if r & 4 else x, 1 - y if r & 2 else y, 1 - c if r & 1 else c)

        def piece(i, ref, p):
            shape, axis = self.items[i]
            if axis == 0:
                rows = shape[0] // 8
                return ref.at[pl.ds(pl.multiple_of(p * rows, 8), rows), :]
            cols = shape[1] // 8
            return ref.at[:, pl.ds(pl.multiple_of(p * cols, 128), cols)]

        own, scatter, arrivals, keep, spread, late = [], [], [], [], [], []
        for i in range(n):
            own.append(pltpu.make_async_copy(piece(i, ins[i], me), landed[i].at[0], loc_sems.at[2 * i]))
            keep.append(pltpu.make_async_copy(sums[i], piece(i, outs[i], me), loc_sems.at[2 * i + 1]))
            for r in range(1, 8):
                to = peer(r)
                p = _device_index(to[:2], to[2])
                scatter.append(_remote(piece(i, ins[i], p), landed[i].at[r], send_sems, recv_sems, (2 * i, r), to))
                spread.append(_remote(sums[i], piece(i, outs[i], me), send_sems, recv_sems, (2 * i + 1, r), to))
                late.append(_remote(sums[i], piece(i, outs[i], p), send_sems, recv_sems, (2 * i + 1, r), to))
        return own, scatter, keep, spread, late, landed, sums

    def start(self, ins, outs, scratch):
        own, scatter, _, _, _, _, _ = self._copies(ins, outs, scratch)
        for cp in own + scatter:
            cp.start()

    def middle(self, ins, outs, scratch):
        own, scatter, keep, spread, _, landed, sums = self._copies(ins, outs, scratch)
        for cp in own:
            cp.wait()
        for cp in scatter:
            cp.wait_recv()
        for i in range(len(self.items)):
            total = landed[i][0]
            for r in range(1, 8):
                total = total + landed[i][r]
            sums[i][...] = total
        for cp in keep + spread:
            cp.start()

    def finish(self, ins, outs, scratch):
        _, scatter, keep, spread, late, _, _ = self._copies(ins, outs, scratch)
        for cp in late:
            cp.wait_recv()
        for cp in scatter + spread:
            cp.wait_send()
        for cp in keep:
            cp.wait()


class _Both:
    def __init__(self, a, b):
        self.a, self.b = a, b
        self.in_specs, self.out_specs = a.in_specs + b.in_specs, a.out_specs + b.out_specs
        self.out_shape, self.scratch_shapes = a.out_shape + b.out_shape, a.scratch_shapes + b.scratch_shapes

    def middle_at(self, steps):
        return min(e.middle_at(steps) for e in (self.a, self.b) if e.middle is not None)

    def _each(self, ins, outs, scratch):
        a = self.a
        i, o, s = len(a.in_specs), len(a.out_specs), len(a.scratch_shapes)
        return (a, ins[:i], outs[:o], scratch[:s]), (self.b, ins[i:], outs[o:], scratch[s:])

    def start(self, ins, outs, scratch):
        for e, i, o, s in self._each(ins, outs, scratch):
            e.start(i, o, s)

    def middle(self, ins, outs, scratch):
        for e, i, o, s in self._each(ins, outs, scratch):
            if e.middle is not None:
                e.middle(i, o, s)

    def finish(self, ins, outs, scratch):
        for e, i, o, s in self._each(ins, outs, scratch):
            e.finish(i, o, s)


def _all_reduce(arrays, items, name):
    reduce = _AllReduce(items)
    n = len(items)

    def body(*refs):
        ins, outs, scratch = refs[:n], refs[n:2 * n], refs[2 * n:]
        reduce.start(ins, outs, scratch)
        reduce.middle(ins, outs, scratch)
        reduce.finish(ins, outs, scratch)

    return pl.pallas_call(
        body, name=name, in_specs=reduce.in_specs, out_specs=reduce.out_specs, out_shape=reduce.out_shape,
        scratch_shapes=reduce.scratch_shapes,
    )(*arrays)


def _adam_small(grads, wmv):
    n = len(grads)

    def body(*refs):
        g_refs, rest = refs[:n], refs[n:]
        ins, outs = rest[:3 * n], rest[3 * n:]
        for i in range(n):
            d, m, v = _adamw(ins[3 * i][...], g_refs[i][...], ins[3 * i + 1][...], ins[3 * i + 2][...])
            outs[3 * i][...], outs[3 * i + 1][...], outs[3 * i + 2][...] = d, m, v

    flat = [a for t in wmv for a in t]
    return pl.pallas_call(
        body, name="adam_small",
        in_specs=[VMEM_SPEC] * (4 * n), out_specs=[VMEM_SPEC] * (3 * n),
        out_shape=[jax.ShapeDtypeStruct(a.shape, F32) for a in flat],
    )(*grads, *flat)


WEIGHT_NAMES = ("norm_mix", "w_in", "w_pool_grp", "pool_scale", "w_pool_out", "conv_w", "conv_b", "w_rg_a", "b_rg_a", "w_rg_x",
                "b_rg_x", "lru_lambda", "w_rnn_out", "w_o", "norm_ffn", "w_ffn_in", "w_ffn_out", "norm_final")


def kernel(x, norm_mix, w_in, w_pool_grp, pool_scale, w_pool_out, conv_w, conv_b, w_rg_a, b_rg_a, w_rg_x, b_rg_x, lru_lambda, w_rnn_out, w_o, norm_ffn, w_ffn_in, w_ffn_out, norm_final, loss_target, m_norm_mix, m_w_in, m_w_pool_grp, m_pool_scale, m_w_pool_out, m_conv_w, m_conv_b, m_w_rg_a, m_b_rg_a, m_w_rg_x, m_b_rg_x, m_lru_lambda, m_w_rnn_out, m_w_o, m_norm_ffn, m_w_ffn_in, m_w_ffn_out, m_norm_final, v_norm_mix, v_w_in, v_w_pool_grp, v_pool_scale, v_w_pool_out, v_conv_w, v_conv_b, v_w_rg_a, v_b_rg_a, v_w_rg_x, v_b_rg_x, v_lru_lambda, v_w_rnn_out, v_w_o, v_norm_ffn, v_w_ffn_in, v_w_ffn_out, v_norm_final):
    w = dict(norm_mix=norm_mix, w_in=w_in, w_pool_grp=w_pool_grp, pool_scale=pool_scale, w_pool_out=w_pool_out, conv_w=conv_w,
             conv_b=conv_b, w_rg_a=w_rg_a, b_rg_a=b_rg_a, w_rg_x=w_rg_x, b_rg_x=b_rg_x, lru_lambda=lru_lambda,
             w_rnn_out=w_rnn_out, w_o=w_o, norm_ffn=norm_ffn, w_ffn_in=w_ffn_in, w_ffn_out=w_ffn_out, norm_final=norm_final)
    m = dict(norm_mix=m_norm_mix, w_in=m_w_in, w_pool_grp=m_w_pool_grp, pool_scale=m_pool_scale, w_pool_out=m_w_pool_out,
             conv_w=m_conv_w, conv_b=m_conv_b, w_rg_a=m_w_rg_a, b_rg_a=m_b_rg_a, w_rg_x=m_w_rg_x, b_rg_x=m_b_rg_x,
             lru_lambda=m_lru_lambda, w_rnn_out=m_w_rnn_out, w_o=m_w_o, norm_ffn=m_norm_ffn, w_ffn_in=m_w_ffn_in,
             w_ffn_out=m_w_ffn_out, norm_final=m_norm_final)
    v = dict(norm_mix=v_norm_mix, w_in=v_w_in, w_pool_grp=v_w_pool_grp, pool_scale=v_pool_scale, w_pool_out=v_w_pool_out,
             conv_w=v_conv_w, conv_b=v_conv_b, w_rg_a=v_w_rg_a, b_rg_a=v_b_rg_a, w_rg_x=v_w_rg_x, b_rg_x=v_b_rg_x,
             lru_lambda=v_lru_lambda, w_rnn_out=v_w_rnn_out, w_o=v_w_o, norm_ffn=v_norm_ffn, w_ffn_in=v_w_ffn_in,
             w_ffn_out=v_w_ffn_out, norm_final=v_norm_final)
    xi, yi, ci = (lax.axis_index(a) for a in MESH_AXES)
    chip = 2 * xi + yi

    def held(T, a):
        return jnp.swapaxes(a, 0, 1) if T.transposed else a

    where = jnp.stack([2 * chip + ci]).astype(jnp.int32)
    by_name = {T.name: T for T in GATHERED}
    block = {T.name: held(T, w[T.name][0]).astype(T.dtype) for T in BIG}
    block["conv_w"] = jnp.pad(conv_w[0], ((0, CONV_W.rows - 4), (0, 0)))
    block["w_ffn_in_lo"], block["w_ffn_in_hi"] = block["w_ffn_in"][:, :D // 2], block["w_ffn_in"][:, D // 2:]

    def gather_of(*names):
        return dict(exchange=_Gather([by_name[n] for n in names]), exchange_operands=[block[n] for n in names])

    def pair_sums(names, partials, tag):
        out = _pair_reduce(partials, [by_name[n] for n in names], "grad_pair_reduce_" + tag)
        return list(out[:len(names)]), list(out[len(names):])

    xs, target = x[0], loss_target[0]
    wg_b, wa_b, wx_b = (a[0].astype(BF16) for a in (w_pool_grp, w_rg_a, w_rg_x))
    ba2, bx2 = b_rg_a.reshape(1, DR), b_rg_x.reshape(1, DR)
    first = ("w_in", "w_pool_out", "w_rnn_out", "w_o", "conv_w")
    order = jnp.stack([chip, 2 * (1 - xi) + yi, 2 * xi + (1 - yi), 2 * (1 - xi) + (1 - yi)]).astype(jnp.int32)
    (proj, h1), (w_in_g, w_pool_out_g, w_rnn_out_g, w_o_g, conv_g) = _in_proj_gather(
        xs, norm_mix, [block[n] for n in first], [by_name[n] for n in first], order)
    mixer_weights = (wg_b, pool_scale, w_pool_out_g, conv_g[0:4], conv_b, wa_b, ba2, wx_b, bx2, lru_lambda, w_rnn_out_g)
    (_, pm, y_pool, hr, z, y_rnn, kept), (w_ffn_hi_g,) = _mixer_fwd(proj, *mixer_weights, **gather_of("w_ffn_in_hi"))
    (mix, x2, h2), (w_ffn_lo_g,) = _merge_out(xs, proj, y_pool, y_rnn, w_o_g, norm_ffn, **gather_of("w_ffn_in_lo"))
    (gu, act), (w_ffn_out_g,) = _ffn_up(h2, w_ffn_lo_g, w_ffn_hi_g, **gather_of("w_ffn_out"))
    dx3, dx3b, loss_part, dvec_fin = _ffn_down_loss(act, x2, target, w_ffn_out_g, norm_final.reshape(1, D))

    dgu = _ffn_bwd_down(dx3b, gu, w_ffn_out_g)
    dx2, dx2b, dmixo, dvec_ffn = _ffn_bwd_up(dgu, x2, dx3, w_ffn_lo_g, w_ffn_hi_g, norm_ffn, w_o_g)
    names_a = ("w_ffn_in", "w_ffn_out", "w_o")
    part_a = [_wgrad(dgu, h2, "wgrad_ffn_in", 1408, 512), _wgrad(act, dx3b, "wgrad_ffn_out", 1408, 512),
              _wgrad(mix, dx2b, "wgrad_o", 1024, 1024)]
    lz1_a, sums_a = pair_sums(names_a, part_a, "ffn")
    (dproj, dypb, dyrb, dmat, dvec_mix), lz2_a = _mixer_bwd(
        proj, dmixo, y_pool, y_rnn, hr, kept, *mixer_weights,
        exchange=_Scatter([by_name[n] for n in names_a]), exchange_operands=sums_a)
    names_b = ("w_pool_out", "w_rnn_out")
    part_b = [_wgrad(pm, dypb, "wgrad_pool_out", 512, 1024), _wgrad(z, dyrb, "wgrad_rnn_out", 1024, 1024)]
    lz1_b, sums_b = pair_sums(names_b, part_b, "mix")
    dvec = jnp.concatenate([dvec_mix[0:9], dvec_fin[0:1], dvec_ffn[0:1], jnp.pad(loss_part, ((0, 0), (0, DR - 1))),
                            jnp.zeros((VEC_ROWS - 12, DR), F32)], axis=0)
    g_in, exchanged = _wgrad(
        dproj, h1, "wgrad_in", 1152, 1024,
        exchange=_Both(_Scatter([by_name[n] for n in names_b]), _AllReduce([((MAT_ROWS, HD), 0), ((VEC_ROWS, DR), 1)])),
        exchange_operands=sums_b + [dmat, dvec])
    lz2_b, (mat, vec) = exchanged[:2], exchanged[2:]
    loss = vec[VEC_LOSS, 0]
    lz1_c, sums_c = pair_sums(("w_in",), [g_in], "in")
    (grad_x, dvec_in), lz2_c = _in_bwd(dproj, xs, dx2, norm_mix, w_in_g,
                                       exchange=_Scatter([by_name["w_in"]]), exchange_operands=sums_c)
    (vec_in,) = _all_reduce([dvec_in], [((8, D), 1)], "all_reduce_norm_mix")

    grads, delta, new_m, new_v = {}, {}, {}, {}
    for n, g, l1, l2 in zip(names_a + names_b + ("w_in",), part_a + part_b + [g_in], lz1_a + lz1_b + lz1_c,
                            lz2_a + lz2_b + lz2_c):
        T = by_name[n]
        out = _final_sum(T, g, l1, l2, where, held(T, w[n][0]), held(T, m[n][0]), held(T, v[n][0]))
        grads[n], delta[n], new_m[n], new_v[n] = (held(T, a) for a in out)
    me = 4 * xi + 2 * yi + ci
    small_grads = dict(
        w_pool_grp=mat[0:MAT_WA], w_rg_a=mat[MAT_WA:MAT_WX], w_rg_x=mat[MAT_WX:MAT_ROWS],
        pool_scale=vec[VEC_SCALE:VEC_SCALE + 1, 0:DP], conv_b=vec[VEC_CONV_B:VEC_CONV_B + 1],
        b_rg_a=vec[VEC_BA:VEC_BA + 1], b_rg_x=vec[VEC_BX:VEC_BX + 1], lru_lambda=vec[VEC_LAM:VEC_LAM + 1],
        conv_w=lax.dynamic_slice(vec, (VEC_CONV_W, VEC_PIECE * me), (4, VEC_PIECE)),
        norm_final=vec[VEC_NORM_FINAL:VEC_NORM_FINAL + 1], norm_ffn=vec[VEC_NORM_FFN:VEC_NORM_FFN + 1],
        norm_mix=vec_in[0:1])
    names = list(small_grads)
    as2d = lambda a, g: a.reshape(g.shape)
    upd = _adam_small([small_grads[n] for n in names],
                      [(as2d(w[n], small_grads[n]), as2d(m[n], small_grads[n]), as2d(v[n], small_grads[n])) for n in names])
    for i, n in enumerate(names):
        grads[n] = small_grads[n]
        delta[n], new_m[n], new_v[n] = upd[3 * i:3 * i + 3]

    shaped = lambda d: [d[n].reshape(w[n].shape) for n in WEIGHT_NAMES]
    return (loss, grad_x[None], *shaped(grads), *shaped(delta), *shaped(new_m), *shaped(new_v))
```

```python
import functools
import math

import jax
import jax.numpy as jnp
from jax import lax
from jax.experimental import pallas as pl
from jax.experimental.pallas import tpu as pltpu

F32 = jnp.float32
BF16 = jnp.bfloat16

D = 1024
DP = 512
PG = 128
WINDOWS = (2, 4, 8, 16)
DR = 1024
NH = 8
HD = 128
DIN = 4608
DFF = 2816
EPS = 1e-6
LRU_C = 8.0
POOL_HALO = 16
CONV_HALO = 8
KEPT = 5

ADAM_LR = 0.001
ADAM_B1 = 0.9
ADAM_B2 = 0.999
ADAM_EPS = 1e-08
ADAM_WD = 0.01
ADAM_STEP = 10

VMEM_LIMIT = 56 * 1024 * 1024
MESH_AXES = ("x", "y", "c")
MESH = pl.DeviceIdType.MESH


def _dot(a, b):
    return jnp.dot(a, b, preferred_element_type=F32)


def _dot_nt(a, b):
    return lax.dot_general(a, b, (((1,), (1,)), ((), ())), preferred_element_type=F32)


def _dot_tn(a, b):
    return lax.dot_general(a, b, (((0,), (0,)), ((), ())), preferred_element_type=F32)


def _params(*sem):
    return pltpu.CompilerParams(dimension_semantics=sem, vmem_limit_bytes=VMEM_LIMIT)


def _resident(shape):
    nd = len(shape)
    return pl.BlockSpec(shape, lambda i: (0,) * nd, pipeline_mode=pl.Buffered(1))


def _rows(shape_cols, tm):
    return pl.BlockSpec((tm, shape_cols), lambda i: (i, 0))


def _call(body, name, grid, in_specs, out_specs, out_shape, operands, scratch_shapes=(), exchange=None, exchange_operands=()):
    n_in, n_out, n_scr = len(in_specs), len(out_specs), len(scratch_shapes)
    steps = math.prod(grid)
    if exchange is None:
        outs = pl.pallas_call(body, name=name, grid=grid, in_specs=in_specs, out_specs=out_specs, out_shape=out_shape,
                              scratch_shapes=list(scratch_shapes), compiler_params=_params(*["arbitrary"] * len(grid)))(*operands)
        return outs, []
    e_in, e_out = len(exchange.in_specs), len(exchange.out_specs)

    def hosted(*refs):
        ins, refs = refs[:n_in], refs[n_in:]
        e_ins, refs = refs[:e_in], refs[e_in:]
        outs, refs = refs[:n_out], refs[n_out:]
        e_outs, refs = refs[:e_out], refs[e_out:]
        scr, e_scr = refs[:n_scr], refs[n_scr:]
        step = pl.program_id(0)
        for axis in range(1, len(grid)):
            step = step * grid[axis] + pl.program_id(axis)
        pl.when(step == 0)(lambda: exchange.start(e_ins, e_outs, e_scr))
        for at, middle in exchange.middles(steps):
            pl.when(step == at)(lambda middle=middle: middle(e_ins, e_outs, e_scr))
        body(*ins, *outs, *scr)
        pl.when(step == steps - 1)(lambda: exchange.finish(e_ins, e_outs, e_scr))

    outs = pl.pallas_call(
        hosted, name=name, grid=grid, in_specs=list(in_specs) + exchange.in_specs,
        out_specs=list(out_specs) + exchange.out_specs, out_shape=list(out_shape) + exchange.out_shape,
        scratch_shapes=list(scratch_shapes) + exchange.scratch_shapes,
        compiler_params=_params(*["arbitrary"] * len(grid)))(*operands, *exchange_operands)
    return outs[:n_out], outs[n_out:]


GELU_C = math.sqrt(2.0 / math.pi)
GELU_K = 0.044715 * GELU_C


def _gelu(x, with_grad=False):
    x2 = x * x
    t = jnp.tanh(x * (GELU_C + GELU_K * x2))
    hx = 0.5 * x
    y = hx + hx * t
    if not with_grad:
        return y
    return y, 0.5 + 0.5 * t + hx * (1.0 - t * t) * (GELU_C + (3.0 * GELU_K) * x2)


def _softplus_neg(lam):
    z = jnp.exp(-jnp.abs(lam))
    u = 1.0 + z
    dlt = u - 1.0
    log1p = jnp.where(dlt == 0.0, z, jnp.log(u) * (z / jnp.where(dlt == 0.0, 1.0, dlt)))
    return jnp.maximum(-lam, 0.0) + log1p


def _sigmoid(x):
    return 0.5 * jnp.tanh(0.5 * x) + 0.5


def _linear_scan(out_ref, A, B, h0, reverse):
    n = A.shape[0]
    sub = lax.broadcasted_iota(jnp.int32, (8, 1), 0)
    tiles = range(n // 8 - 1, -1, -1) if reverse else range(n // 8)
    carry = h0
    for j in tiles:
        a, b = A[8 * j:8 * j + 8, :], B[8 * j:8 * j + 8, :]
        for d in (1, 2, 4):
            keep = (sub < 8 - d) if reverse else (sub >= d)
            shift = 8 - d if reverse else d
            b = jnp.where(keep, a * pltpu.roll(b, shift, axis=0) + b, b)
            a = jnp.where(keep, a * pltpu.roll(a, shift, axis=0), a)
        h = a * carry + b
        out_ref[8 * j:8 * j + 8, :] = h
        carry = h[0:1, :] if reverse else h[7:8, :]
    return carry


def _pool_windows(ext, shift_sign):
    n = ext.shape[0]
    s = ext
    outs = []
    for w in WINDOWS:
        d = w // 2
        s = s + pltpu.roll(s, d if shift_sign > 0 else n - d, axis=0)
        outs.append(s[:, :PG])
        s = s[:, PG:]
    return outs


def _conv_taps(uext):
    taps = []
    for k in range(4):
        sh = 3 - k
        v = uext if sh == 0 else pltpu.roll(uext, sh, axis=0)
        taps.append(v[CONV_HALO:, :])
    return taps


def _gates(v, wa_ref, ba_ref, wx_ref, bx_ref, sp):
    vb = v.astype(BF16)
    ra, rx = [], []
    for h in range(NH):
        vh = vb[:, h * HD:(h + 1) * HD]
        ra.append(_dot(vh, wa_ref[h]))
        rx.append(_dot(vh, wx_ref[h]))
    r = _sigmoid(jnp.concatenate(ra, axis=1) + ba_ref[...])
    i = _sigmoid(jnp.concatenate(rx, axis=1) + bx_ref[...])
    log_a = r * ((-LRU_C) * sp)
    a = jnp.exp(log_a)
    one_minus = -jnp.tanh(log_a) * (1.0 + a * a)
    return r, i, a, jnp.sqrt(one_minus), lax.rsqrt(one_minus)


def _mixer_fwd(proj, wg, scale, w_pool_out, conv_w, conv_b, wa, ba, wx, bx, lam, w_rnn_out, exchange=None,
               exchange_operands=(), tm=256):
    S = proj.shape[0]
    UW = DP + 2 * DR

    def body(proj_ref, wg_ref, scale_ref, wpo_ref, cw_ref, cb_ref, wa_ref, ba_ref, wx_ref, bx_ref, lam_ref, wro_ref,
             pooled_ref, pm_ref, ypool_ref, hr_ref, z_ref, yrnn_ref, kept_ref, pool_carry, conv_carry, h_carry):
        i = pl.program_id(0)

        @pl.when(i == 0)
        def _():
            pool_carry[...] = jnp.zeros_like(pool_carry)
            conv_carry[...] = jnp.zeros_like(conv_carry)
            h_carry[...] = jnp.zeros_like(h_carry)

        rows = lax.broadcasted_iota(jnp.int32, (tm, 1), 0)
        t_glob = i * tm + rows

        u_pool = proj_ref[:, 0:DP]
        ext = jnp.concatenate([pool_carry[...], u_pool], axis=0)
        pool_carry[...] = u_pool[tm - POOL_HALO:, :]
        sums = _pool_windows(ext, +1)
        mixed = []
        for g, w in enumerate(WINDOWS):
            inv_cnt = 1.0 / jnp.minimum(t_glob + 1, w).astype(F32)
            pooled_g = sums[g][POOL_HALO:, :] * inv_cnt - u_pool[:, g * PG:(g + 1) * PG]
            pooled_b = pooled_g.astype(BF16)
            pooled_ref[:, g * PG:(g + 1) * PG] = pooled_b
            mixed.append(_dot(pooled_b, wg_ref[g]))
        pm = (jnp.concatenate(mixed, axis=1) * scale_ref[...]).astype(BF16)
        pm_ref[...] = pm
        ypool_ref[...] = _dot(pm, wpo_ref[...])

        u_rnn = proj_ref[:, DP:DP + DR]
        uext = jnp.concatenate([conv_carry[...], u_rnn], axis=0)
        conv_carry[...] = u_rnn[tm - CONV_HALO:, :]
        taps = _conv_taps(uext)
        v = cb_ref[...]
        for k in range(4):
            v = v + taps[k] * cw_ref[k:k + 1, :]
        sp = _softplus_neg(lam_ref[...])
        r, gi, a, mult, _ = _gates(v, wa_ref, ba_ref, wx_ref, bx_ref, sp)
        for k, kept in enumerate((v, r, gi, a, mult)):
            kept_ref[k] = kept
        h_carry[0:1, :] = _linear_scan(hr_ref, a, mult * gi * v, h_carry[0:1, :], reverse=False)
        z = (hr_ref[...] * _gelu(proj_ref[:, DP + DR:UW])).astype(BF16)
        z_ref[...] = z
        yrnn_ref[...] = _dot(z, wro_ref[...])

    return _call(
        body, "mixer_fwd", (S // tm,),
        in_specs=[_rows(UW, tm), _resident((4, PG, PG)), _resident((1, DP)), _resident((DP, D)), _resident((4, DR)),
                  _resident((1, DR)), _resident((NH, HD, HD)), _resident((1, DR)), _resident((NH, HD, HD)),
                  _resident((1, DR)), _resident((1, DR)), _resident((DR, D))],
        out_specs=[_rows(DP, tm), _rows(DP, tm), _rows(D, tm), _rows(DR, tm), _rows(DR, tm), _rows(D, tm),
                   pl.BlockSpec((KEPT, tm, DR), lambda i: (0, i, 0))],
        out_shape=[jax.ShapeDtypeStruct((S, DP), BF16), jax.ShapeDtypeStruct((S, DP), BF16),
                   jax.ShapeDtypeStruct((S, D), F32), jax.ShapeDtypeStruct((S, DR), F32),
                   jax.ShapeDtypeStruct((S, DR), BF16), jax.ShapeDtypeStruct((S, D), F32),
                   jax.ShapeDtypeStruct((KEPT, S, DR), F32)],
        scratch_shapes=[pltpu.VMEM((POOL_HALO, DP), F32), pltpu.VMEM((CONV_HALO, DR), F32), pltpu.VMEM((8, DR), F32)],
        operands=(proj, wg, scale, w_pool_out, conv_w, conv_b, wa, ba, wx, bx, lam, w_rnn_out),
        exchange=exchange, exchange_operands=exchange_operands)


FF_CHUNKS = ((0, 768), (768, 1536), (1536, 2304), (2304, DFF))


def _rms(x):
    r = lax.rsqrt(jnp.mean(x * x, axis=-1, keepdims=True) + EPS)
    return r, x * r


def _rms_bwd(dh, g, r, xh):
    dxh = dh * g
    return r * (dxh - xh * jnp.mean(dxh * xh, axis=-1, keepdims=True))


def _merge_out(x, proj, y_pool, y_rnn, w_o, norm_ffn, exchange=None, exchange_operands=(), tm=512):
    S = x.shape[0]
    GL0 = (DP + 2 * DR) // 512

    def gl_spec(k):
        return pl.BlockSpec((tm, 512), lambda i: (i, GL0 + k))

    def body(x_ref, gl0, gl1, gl2, gl3, yp_ref, yr_ref, wo_ref, gf_ref, mix_ref, x2_ref, h2_ref):
        s_p = _sigmoid(jnp.concatenate([gl0[...], gl1[...]], axis=1))
        s_r = _sigmoid(jnp.concatenate([gl2[...], gl3[...]], axis=1))
        mix = (s_p * yp_ref[...] + s_r * yr_ref[...]).astype(BF16)
        mix_ref[...] = mix
        x2 = x_ref[...] + _dot(mix, wo_ref[...])
        x2_ref[...] = x2
        _, xh2 = _rms(x2)
        h2_ref[...] = (xh2 * gf_ref[...]).astype(BF16)

    return _call(
        body, "merge_out", (S // tm,),
        in_specs=[_rows(D, tm), gl_spec(0), gl_spec(1), gl_spec(2), gl_spec(3), _rows(D, tm), _rows(D, tm),
                  _resident((D, D)), _resident((1, D))],
        out_specs=[_rows(D, tm), _rows(D, tm), _rows(D, tm)],
        out_shape=[jax.ShapeDtypeStruct((S, D), BF16), jax.ShapeDtypeStruct((S, D), F32), jax.ShapeDtypeStruct((S, D), BF16)],
        operands=(x, proj, proj, proj, proj, y_pool, y_rnn, w_o, norm_ffn),
        exchange=exchange, exchange_operands=exchange_operands)


def _ffn_up(h2, w_lo, w_hi, exchange=None, exchange_operands=(), tm=512):
    S = h2.shape[0]
    HALF = D // 2

    def body(h_ref, lo_ref, hi_ref, gu_ref, act_ref):
        h_lo, h_hi = h_ref[:, 0:HALF], h_ref[:, HALF:D]
        for c0, c1 in FF_CHUNKS:
            gate = _dot_nt(h_lo, lo_ref[c0:c1, :]) + _dot_nt(h_hi, hi_ref[c0:c1, :])
            up = _dot_nt(h_lo, lo_ref[DFF + c0:DFF + c1, :]) + _dot_nt(h_hi, hi_ref[DFF + c0:DFF + c1, :])
            gu_ref[:, c0:c1] = gate.astype(BF16)
            gu_ref[:, DFF + c0:DFF + c1] = up.astype(BF16)
            act_ref[:, c0:c1] = (gate * _sigmoid(gate) * up).astype(BF16)

    return _call(
        body, "ffn_up", (S // tm,),
        in_specs=[_rows(D, tm), _resident((2 * DFF, HALF)), _resident((2 * DFF, HALF))],
        out_specs=[_rows(2 * DFF, tm), _rows(DFF, tm)],
        out_shape=[jax.ShapeDtypeStruct((S, 2 * DFF), BF16), jax.ShapeDtypeStruct((S, DFF), BF16)],
        operands=(h2, w_lo, w_hi), exchange=exchange, exchange_operands=exchange_operands)


def _ffn_down_loss(act, x2, target, w_ffn_out, norm_final, tm=512):
    S = act.shape[0]

    def body(act_ref, x2_ref, t_ref, w_ref, gn_ref, dx3_ref, dx3b_ref, loss_ref, dvec_ref):
        i = pl.program_id(0)

        @pl.when(i == 0)
        def _():
            loss_ref[...] = jnp.zeros_like(loss_ref)
            dvec_ref[...] = jnp.zeros_like(dvec_ref)

        x3 = x2_ref[...] + _dot(act_ref[...], w_ref[...])
        r3, xh3 = _rms(x3)
        g_fin = gn_ref[...]
        e = xh3 * g_fin - t_ref[...]
        loss_ref[...] += jnp.sum(e * e, axis=(0, 1), keepdims=True) * (0.5 / D)
        dy = e * (1.0 / D)
        dvec_ref[0:1, :] += jnp.sum(dy * xh3, axis=0, keepdims=True)
        dx3 = _rms_bwd(dy, g_fin, r3, xh3)
        dx3_ref[...] = dx3
        dx3b_ref[...] = dx3.astype(BF16)

    return pl.pallas_call(
        body, name="ffn_down_loss", grid=(S // tm,),
        in_specs=[_rows(DFF, tm), _rows(D, tm), _rows(D, tm), _resident((DFF, D)), _resident((1, D))],
        out_specs=[_rows(D, tm), _rows(D, tm), _resident((1, 1)), _resident((8, D))],
        out_shape=[jax.ShapeDtypeStruct((S, D), F32), jax.ShapeDtypeStruct((S, D), BF16),
                   jax.ShapeDtypeStruct((1, 1), F32), jax.ShapeDtypeStruct((8, D), F32)],
        compiler_params=_params("arbitrary"),
    )(act, x2, target, w_ffn_out, norm_final)


def _ffn_bwd_down(dx3b, gu, w_ffn_out, tm=512):
    S = dx3b.shape[0]

    def body(d_ref, gu_ref, w_ref, dgu_ref):
        d = d_ref[...]
        for c0, c1 in FF_CHUNKS:
            dact = _dot_nt(d, w_ref[c0:c1, :])
            gate = gu_ref[:, c0:c1].astype(F32)
            up = gu_ref[:, DFF + c0:DFF + c1].astype(F32)
            sg = _sigmoid(gate)
            dgu_ref[:, c0:c1] = (dact * up * (sg * (1.0 + gate * (1.0 - sg)))).astype(BF16)
            dgu_ref[:, DFF + c0:DFF + c1] = (dact * (gate * sg)).astype(BF16)

    return pl.pallas_call(
        body, name="ffn_bwd_down", grid=(S // tm,),
        in_specs=[_rows(D, tm), _rows(2 * DFF, tm), _resident((DFF, D))],
        out_specs=_rows(2 * DFF, tm),
        out_shape=jax.ShapeDtypeStruct((S, 2 * DFF), BF16),
        compiler_params=_params("parallel"),
    )(dx3b, gu, w_ffn_out)


def _ffn_bwd_up(dgu, x2, dx3, w_lo, w_hi, norm_ffn, w_o, tm=512):
    S = dgu.shape[0]
    HALF = D // 2

    def body(dgu_ref, x2_ref, dx3_ref, lo_ref, hi_ref, gf_ref, wo_ref, dx2_ref, dx2b_ref, dmixo_ref, dvec_ref):
        i = pl.program_id(0)

        @pl.when(i == 0)
        def _():
            dvec_ref[...] = jnp.zeros_like(dvec_ref)

        dgate, dup = dgu_ref[:, 0:DFF], dgu_ref[:, DFF:2 * DFF]
        dh2 = jnp.concatenate([_dot(dgate, w[0:DFF, :]) + _dot(dup, w[DFF:2 * DFF, :]) for w in (lo_ref, hi_ref)], axis=1)
        r2, xh2 = _rms(x2_ref[...])
        dvec_ref[0:1, :] += jnp.sum(dh2 * xh2, axis=0, keepdims=True)
        dx2 = dx3_ref[...] + _rms_bwd(dh2, gf_ref[...], r2, xh2)
        dx2_ref[...] = dx2
        dx2b = dx2.astype(BF16)
        dx2b_ref[...] = dx2b
        dmixo_ref[...] = _dot_nt(dx2b, wo_ref[...])

    return pl.pallas_call(
        body, name="ffn_bwd_up", grid=(S // tm,),
        in_specs=[_rows(2 * DFF, tm), _rows(D, tm), _rows(D, tm), _resident((2 * DFF, HALF)), _resident((2 * DFF, HALF)),
                  _resident((1, D)), _resident((D, D))],
        out_specs=[_rows(D, tm), _rows(D, tm), _rows(D, tm), _resident((8, D))],
        out_shape=[jax.ShapeDtypeStruct((S, D), F32), jax.ShapeDtypeStruct((S, D), BF16), jax.ShapeDtypeStruct((S, D), F32),
                   jax.ShapeDtypeStruct((8, D), F32)],
        compiler_params=_params("arbitrary"),
    )(dgu, x2, dx3, w_lo, w_hi, norm_ffn, w_o)


VEC_ROWS = 16
MAT_WA = 4 * PG
MAT_WX = MAT_WA + NH * HD
MAT_ROWS = MAT_WX + NH * HD


def _mixer_bwd(proj, dmixo, y_pool, y_rnn, hr, kept, wg, scale, w_pool_out, conv_w, conv_b, wa, ba, wx, bx, lam, w_rnn_out,
               exchange=None, exchange_operands=(), tm=256):
    S = proj.shape[0]
    nt = S // tm

    def rev(cols):
        return pl.BlockSpec((tm, cols), lambda i: (nt - 1 - i, 0))

    def halo(rows_, cols):
        per = tm // rows_
        return pl.BlockSpec((rows_, cols), lambda i: (jnp.maximum((nt - 1 - i) * per - 1, 0), 0))

    def body(proj_ref, projh_ref, dmixo_ref, yp_ref, yr_ref, hr_ref, hrh_ref, kept_ref, wg_ref, scale_ref, wpo_ref, cw_ref, cb_ref,
             wa_ref, ba_ref, wx_ref, bx_ref, lam_ref, wro_ref,
             dproj_ref, dypb_ref, dyrb_ref, dmat_ref, dvec_ref,
             q_carry, dv_carry, a_carry, g_carry, g_scr):
        i = pl.program_id(0)
        ti = nt - 1 - i

        @pl.when(i == 0)
        def _():
            q_carry[...] = jnp.zeros_like(q_carry)
            dv_carry[...] = jnp.zeros_like(dv_carry)
            a_carry[...] = jnp.zeros_like(a_carry)
            g_carry[...] = jnp.zeros_like(g_carry)
            dmat_ref[...] = jnp.zeros_like(dmat_ref)
            dvec_ref[...] = jnp.zeros_like(dvec_ref)

        rows = lax.broadcasted_iota(jnp.int32, (tm, 1), 0)
        t_glob = ti * tm + rows
        has_prev = (ti > 0).astype(F32)
        dmixo = dmixo_ref[...]

        s_p = _sigmoid(proj_ref[:, DP + 2 * DR:DP + 2 * DR + D])
        s_r = _sigmoid(proj_ref[:, DP + 2 * DR + D:DIN])
        dproj_ref[:, DP + 2 * DR:DP + 2 * DR + D] = (dmixo * yp_ref[...] * s_p * (1.0 - s_p)).astype(BF16)
        dproj_ref[:, DP + 2 * DR + D:DIN] = (dmixo * yr_ref[...] * s_r * (1.0 - s_r)).astype(BF16)
        dyp = (dmixo * s_p).astype(BF16)
        dyr = (dmixo * s_r).astype(BF16)
        dypb_ref[...] = dyp
        dyrb_ref[...] = dyr

        dz = _dot_nt(dyr, wro_ref[...])
        u_gate = proj_ref[:, DP + DR:DP + 2 * DR]
        gg, dgelu = _gelu(u_gate, with_grad=True)
        hr_t = hr_ref[...]
        dproj_ref[:, DP + DR:DP + 2 * DR] = (dz * hr_t * dgelu).astype(BF16)
        dhr = dz * gg

        sp = _softplus_neg(lam_ref[...])
        v, r, gi, a, mult = (kept_ref[k] for k in range(KEPT))
        inv_mult = 1.0 / mult

        C = jnp.where(rows == tm - 1, a_carry[0:1, :], pltpu.roll(a, tm - 1, axis=0))
        g_carry[0:1, :] = _linear_scan(g_scr, C, dhr, g_carry[0:1, :], reverse=True)
        a_carry[0:1, :] = a[0:1, :]
        g = g_scr[...]

        h_prev = jnp.where(rows == 0, hrh_ref[7:8, :] * has_prev, pltpu.roll(hr_t, 1, axis=0))
        da = g * h_prev
        gm = g * mult
        dmult = g * gi * v
        di = gm * v
        dv = gm * gi
        dlog_a = da * a - dmult * (a * a * inv_mult)
        dvec_ref[4:5, :] += jnp.sum(dlog_a * r, axis=0, keepdims=True)
        dra = (dlog_a * ((-LRU_C) * sp) * r * (1.0 - r))
        drx = di * gi * (1.0 - gi)
        dvec_ref[2:3, :] += jnp.sum(dra, axis=0, keepdims=True)
        dvec_ref[3:4, :] += jnp.sum(drx, axis=0, keepdims=True)
        drab = dra.astype(BF16)
        drxb = drx.astype(BF16)
        vb = v.astype(BF16)
        dvg = []
        for h in range(NH):
            sl = slice(h * HD, (h + 1) * HD)
            dvg.append(_dot_nt(drab[:, sl], wa_ref[h]) + _dot_nt(drxb[:, sl], wx_ref[h]))
            dmat_ref[MAT_WA + h * HD:MAT_WA + (h + 1) * HD, :] += _dot_tn(vb[:, sl], drab[:, sl])
            dmat_ref[MAT_WX + h * HD:MAT_WX + (h + 1) * HD, :] += _dot_tn(vb[:, sl], drxb[:, sl])
        dv = dv + jnp.concatenate(dvg, axis=1)
        dvec_ref[1:2, :] += jnp.sum(dv, axis=0, keepdims=True)
        dvext = jnp.concatenate([dv, dv_carry[...]], axis=0)
        dv_carry[...] = dv[0:CONV_HALO, :]
        n = tm + CONV_HALO
        u_rnn = proj_ref[:, DP:DP + DR]
        du_rnn = dv * cw_ref[3:4, :]
        dvec_ref[8:9, :] += jnp.sum(dv * u_rnn, axis=0, keepdims=True)
        for k in range(3):
            dv_k = pltpu.roll(dvext, n - (3 - k), axis=0)[0:tm, :]
            du_rnn = du_rnn + dv_k * cw_ref[k:k + 1, :]
            dvec_ref[5 + k:6 + k, :] += jnp.sum(dv_k * u_rnn, axis=0, keepdims=True)
        dproj_ref[:, DP:DP + DR] = du_rnn.astype(BF16)

        dpm = _dot_nt(dyp, wpo_ref[...])
        u_pool = proj_ref[:, 0:DP]
        ext = jnp.concatenate([projh_ref[:, 0:DP] * has_prev, u_pool], axis=0)
        sums = _pool_windows(ext, +1)
        scale_v = scale_ref[...]
        qs = []
        dpooled = []
        dscale = []
        for gi_, w in enumerate(WINDOWS):
            sl = slice(gi_ * PG, (gi_ + 1) * PG)
            inv_cnt = 1.0 / jnp.minimum(t_glob + 1, w).astype(F32)
            pooled_b = (sums[gi_][POOL_HALO:, :] * inv_cnt - u_pool[:, sl]).astype(BF16)
            mixed_g = _dot(pooled_b, wg_ref[gi_])
            dscale.append(jnp.sum(dpm[:, sl] * mixed_g, axis=0, keepdims=True))
            dmixed_b = (dpm[:, sl] * scale_v[:, sl]).astype(BF16)
            dmat_ref[gi_ * PG:(gi_ + 1) * PG, :] += _dot_tn(pooled_b, dmixed_b)
            dp_g = _dot_nt(dmixed_b, wg_ref[gi_])
            dpooled.append(dp_g)
            qs.append(dp_g * inv_cnt)
        dvec_ref[0:1, 0:DP] += jnp.concatenate(dscale, axis=1)
        q = jnp.concatenate(qs, axis=1)
        qext = jnp.concatenate([q, q_carry[...]], axis=0)
        q_carry[...] = q[0:POOL_HALO, :]
        tsum = _pool_windows(qext, -1)
        for gi_ in range(4):
            dproj_ref[:, gi_ * PG:(gi_ + 1) * PG] = (tsum[gi_][0:tm, :] - dpooled[gi_]).astype(BF16)

        @pl.when(i == nt - 1)
        def _():
            dvec_ref[4:5, :] = dvec_ref[4:5, :] * (LRU_C * _sigmoid(-lam_ref[...]))

    return _call(
        body, "mixer_bwd", (nt,),
        in_specs=[rev(DIN), halo(POOL_HALO, DIN), rev(D), rev(D), rev(D), rev(DR), halo(8, DR),
                  pl.BlockSpec((KEPT, tm, DR), lambda i: (0, nt - 1 - i, 0)), _resident((4, PG, PG)), _resident((1, DP)), _resident((DP, D)), _resident((4, DR)), _resident((1, DR)),
                  _resident((NH, HD, HD)), _resident((1, DR)), _resident((NH, HD, HD)), _resident((1, DR)),
                  _resident((1, DR)), _resident((DR, D))],
        out_specs=[rev(DIN), rev(D), rev(D), _resident((MAT_ROWS, HD)), _resident((VEC_ROWS, DR))],
        out_shape=[jax.ShapeDtypeStruct((S, DIN), BF16), jax.ShapeDtypeStruct((S, D), BF16),
                   jax.ShapeDtypeStruct((S, D), BF16), jax.ShapeDtypeStruct((MAT_ROWS, HD), F32),
                   jax.ShapeDtypeStruct((VEC_ROWS, DR), F32)],
        scratch_shapes=[pltpu.VMEM((POOL_HALO, DP), F32), pltpu.VMEM((CONV_HALO, DR), F32), pltpu.VMEM((8, DR), F32),
                        pltpu.VMEM((8, DR), F32), pltpu.VMEM((tm, DR), F32)],
        operands=(proj, proj, dmixo, y_pool, y_rnn, hr, hr, kept, wg, scale, w_pool_out, conv_w, conv_b, wa, ba, wx, bx, lam,
                  w_rnn_out),
        exchange=exchange, exchange_operands=exchange_operands)


def _in_bwd(dproj, x, dx2, norm_mix, w_in, exchange=None, exchange_operands=(), tm=512):
    S = x.shape[0]

    def body(dp_ref, x_ref, dx2_ref, g_ref, w_ref, dx_ref, dg_ref):
        i = pl.program_id(0)

        @pl.when(i == 0)
        def _():
            dg_ref[...] = jnp.zeros_like(dg_ref)

        dh = _dot(dp_ref[:, 0:1536], w_ref[0:1536, :])
        dh = dh + _dot(dp_ref[:, 1536:3072], w_ref[1536:3072, :])
        dh = dh + _dot(dp_ref[:, 3072:DIN], w_ref[3072:DIN, :])
        xv = x_ref[...]
        r = lax.rsqrt(jnp.mean(xv * xv, axis=-1, keepdims=True) + EPS)
        xh = xv * r
        dg_ref[0:1, :] += jnp.sum(dh * xh, axis=0, keepdims=True)
        dxh = dh * g_ref[...]
        dx_ref[...] = dx2_ref[...] + r * (dxh - xh * jnp.mean(dxh * xh, axis=-1, keepdims=True))

    return _call(
        body, "in_bwd", (S // tm,),
        in_specs=[_rows(DIN, tm), _rows(D, tm), _rows(D, tm), _resident((1, D)), _resident((DIN, D))],
        out_specs=[_rows(D, tm), _resident((8, D))],
        out_shape=[jax.ShapeDtypeStruct((S, D), F32), jax.ShapeDtypeStruct((8, D), F32)],
        operands=(dproj, x, dx2, norm_mix, w_in), exchange=exchange, exchange_operands=exchange_operands)


def _wgrad(a, b, name, tk, tn, exchange=None, exchange_operands=()):
    S, K = a.shape
    N = b.shape[1]

    def body(a_ref, b_ref, o_ref):
        o_ref[...] = _dot_tn(a_ref[...], b_ref[...]).astype(BF16)

    (out,), exchanged = _call(
        body, name, (K // tk, N // tn),
        in_specs=[pl.BlockSpec((S, tk), lambda k, n: (0, k)), pl.BlockSpec((S, tn), lambda k, n: (0, n))],
        out_specs=[pl.BlockSpec((tk, tn), lambda k, n: (k, n))],
        out_shape=[jax.ShapeDtypeStruct((K, N), BF16)],
        operands=(a, b), exchange=exchange, exchange_operands=exchange_operands)
    return (out, exchanged) if exchange is not None else out


VEC_SCALE, VEC_CONV_B, VEC_BA, VEC_BX, VEC_LAM, VEC_CONV_W, VEC_NORM_FINAL, VEC_NORM_FFN = 0, 1, 2, 3, 4, 5, 9, 10
VEC_LOSS = 11


class _Big:
    def __init__(self, name, rows, cols, axis, n, dtype=BF16, transposed=False):
        self.name, self.rows, self.cols, self.axis, self.n, self.dtype = name, rows, cols, axis, n, dtype
        self.transposed = transposed
        self.block_shape = (rows, n) if axis == 1 else (n, cols)

    def block(self, ref, p):
        if self.axis == 1:
            return ref.at[:, pl.ds(pl.multiple_of(p * self.n, 128), self.n)]
        return ref.at[pl.ds(pl.multiple_of(p * self.n, 16 if self.dtype == BF16 else 8), self.n), :]

    def block_index(self, p):
        return (0, p) if self.axis == 1 else (p, 0)


BIG = (_Big("w_in", DIN, D, 0, DIN // 8, transposed=True), _Big("w_pool_out", DP, D, 1, D // 8),
       _Big("w_rnn_out", DR, D, 0, DR // 8), _Big("w_o", D, D, 0, D // 8),
       _Big("w_ffn_in", 2 * DFF, D, 0, 2 * DFF // 8, transposed=True), _Big("w_ffn_out", DFF, D, 0, DFF // 8))
CONV_W = _Big("conv_w", 8, DR, 1, DR // 8, F32)
W_FFN_IN_HALVES = (_Big("w_ffn_in_lo", 2 * DFF, D // 2, 0, 2 * DFF // 8), _Big("w_ffn_in_hi", 2 * DFF, D // 2, 0, 2 * DFF // 8))
GATHERED = BIG + (CONV_W,) + W_FFN_IN_HALVES

HBM_SPEC = pl.BlockSpec(memory_space=pl.ANY)
VMEM_SPEC = pl.BlockSpec(memory_space=pltpu.VMEM)


def _place():
    x, y, c = (lax.axis_index(a) for a in MESH_AXES)
    other_chips = [(1 - x, y), (x, 1 - y), (1 - x, 1 - y)]
    return x, y, c, other_chips


def _remote(src, dst, send_sems, recv_sems, idx, to):
    return pltpu.make_async_remote_copy(src_ref=src, dst_ref=dst, send_sem=send_sems.at[idx], recv_sem=recv_sems.at[idx],
                                        device_id=to, device_id_type=MESH)


def _device_index(chip, core):
    return 4 * chip[0] + 2 * chip[1] + core


class _Gather:
    def __init__(self, tensors):
        self.tensors = tuple(tensors)
        n = len(self.tensors)
        self.in_specs = [HBM_SPEC] * n
        self.out_specs = [HBM_SPEC] * n
        self.out_shape = [jax.ShapeDtypeStruct((T.rows, T.cols), T.dtype) for T in self.tensors]
        self.scratch_shapes = [pltpu.VMEM(T.block_shape, T.dtype) for T in self.tensors] + [
            pltpu.SemaphoreType.DMA((n, 7)), pltpu.SemaphoreType.DMA((n, 7)), pltpu.SemaphoreType.DMA((n, 2))]

    def middles(self, steps):
        near = lambda ins, outs, scratch: self.middle(ins, outs, scratch, which=(0, 1))
        far = lambda ins, outs, scratch: self.middle(ins, outs, scratch, which=(2,))
        return [((5 * steps) // 8, near), (steps - 1, far)]

    def _copies(self, ins, outs, scratch):
        n = len(self.tensors)
        mine, (send_sems, recv_sems, loc_sems) = scratch[:n], scratch[n:]
        x, y, c, chips = _place()
        sibling = (x, y, 1 - c)
        me = _device_index((x, y), c)
        loads, stores, first, passed, arrivals, late = [], [], [], [], [], []
        for t, T in enumerate(self.tensors):
            place = T.block(outs[t], me)
            loads.append(pltpu.make_async_copy(ins[t], mine[t], loc_sems.at[t, 0]))
            stores.append(pltpu.make_async_copy(mine[t], place, loc_sems.at[t, 1]))
            first.append(_remote(mine[t], place, send_sems, recv_sems, (t, 0), sibling))
            theirs = T.block(outs[t], _device_index((x, y), 1 - c))
            late.append(_remote(theirs, theirs, send_sems, recv_sems, (t, 0), sibling))
            for k, chip in enumerate(chips):
                first.append(_remote(mine[t], place, send_sems, recv_sems, (t, 1 + k), (*chip, c)))
                land = T.block(outs[t], _device_index(chip, c))
                arrivals.append(_remote(land, land, send_sems, recv_sems, (t, 1 + k), sibling))
                passed.append(_remote(land, land, send_sems, recv_sems, (t, 4 + k), sibling))
                theirs = T.block(outs[t], _device_index(chip, 1 - c))
                late.append(_remote(theirs, theirs, send_sems, recv_sems, (t, 4 + k), sibling))
        return loads, stores, first, passed, arrivals, late

    def start(self, ins, outs, scratch):
        loads, stores, first, _, _, _ = self._copies(ins, outs, scratch)
        for cp in loads:
            cp.start()
        for cp in loads:
            cp.wait()
        for cp in stores + first:
            cp.start()

    def middle(self, ins, outs, scratch, skip=0, which=(0, 1, 2)):
        _, _, _, passed, arrivals, _ = self._copies(ins, outs, scratch)
        for t in range(skip, len(self.tensors)):
            for k in which:
                arrivals[3 * t + k].wait_recv()
                passed[3 * t + k].start()

    def finish(self, ins, outs, scratch, skip=0):
        _, stores, first, passed, _, late = self._copies(ins, outs, scratch)
        for cp in late[4 * skip:]:
            cp.wait_recv()
        for cp in first + passed:
            cp.wait_send()
        for cp in stores[skip:]:
            cp.wait()


def _in_proj_gather(x, norm_mix, blocks, tensors, order, tm=512):
    S = x.shape[0]
    nt = S // tm
    n = len(tensors)
    gather = _Gather(tensors)
    CB = 2 * tensors[0].n

    def body(order_ref, x_ref, g_ref, *refs):
        ins, (proj_ref, h_ref), outs = refs[:n], refs[n:n + 2], refs[n + 2:2 * n + 2]
        (h_all, w_chip, w_sem), scratch = refs[2 * n + 2:2 * n + 5], refs[2 * n + 5:]
        q, i = pl.program_id(0), pl.program_id(1)
        _, stores, first, passed, arrivals, late = gather._copies(ins, outs, scratch)

        def fetch(chip):
            rows = outs[0].at[pl.ds(pl.multiple_of(chip * CB, 16), CB), :]
            cp = pltpu.make_async_copy(rows, w_chip, w_sem)
            cp.start()
            cp.wait()

        @pl.when((q == 0) & (i == 0))
        def _():
            gather.start(ins, outs, scratch)
            late[0].wait_recv()
            stores[0].wait()
            fetch(order_ref[0])

        for k in range(3):
            @pl.when((q == k + 1) & (i == 0))
            def _(k=k):
                arrivals[k].wait_recv()
                passed[k].start()
                late[1 + k].wait_recv()
                fetch(order_ref[k + 1])

        rows = pl.ds(pl.multiple_of(i * tm, tm), tm)

        @pl.when(q == 0)
        def _():
            xv = x_ref[...]
            r = lax.rsqrt(jnp.mean(xv * xv, axis=-1, keepdims=True) + EPS)
            h = (xv * r * g_ref[...]).astype(BF16)
            h_all[rows, :] = h
            h_ref[...] = h

        proj_ref[...] = _dot_nt(h_all[rows, :], w_chip[...])

        @pl.when((q == 3) & (i == nt - 1))
        def _():
            gather.middle(ins, outs, scratch, skip=1)
            gather.finish(ins, outs, scratch, skip=1)

    row_tile = lambda q, i, order: (jnp.where(q == 0, i, nt - 1), 0)
    whole = lambda shape: pl.BlockSpec(shape, lambda q, i, order: (0,) * len(shape), pipeline_mode=pl.Buffered(1))
    outs = pl.pallas_call(
        body, name="in_proj_gather",
        grid_spec=pltpu.PrefetchScalarGridSpec(
            num_scalar_prefetch=1, grid=(4, nt),
            in_specs=[pl.BlockSpec((tm, D), row_tile), whole((1, D))] + gather.in_specs,
            out_specs=[pl.BlockSpec((tm, CB), lambda q, i, order: (i, order[q])), pl.BlockSpec((tm, D), row_tile)]
            + gather.out_specs,
            scratch_shapes=[pltpu.VMEM((S, D), BF16), pltpu.VMEM((CB, D), BF16), pltpu.SemaphoreType.DMA]
            + gather.scratch_shapes),
        out_shape=[jax.ShapeDtypeStruct((S, DIN), F32), jax.ShapeDtypeStruct((S, D), BF16)] + gather.out_shape,
        compiler_params=_params("arbitrary", "arbitrary"),
    )(order, x, norm_mix, *blocks)
    return outs[:2], outs[2:]


PAIR_ROWS = 32


def _pair_reduce(grads, tensors, name):
    nt = len(tensors)

    def body(*refs):
        ins, own_out, sums_out, landed, mine = (refs[k * nt:(k + 1) * nt] for k in range(5))
        send_sems, recv_sems, loc_sems = refs[5 * nt:]
        x, y, c, chips = _place()
        chip_of = [2 * chip[0] + chip[1] for chip in chips]
        swaps, loads = [], []
        for t, T in enumerate(tensors):
            for j in range(4):
                swaps.append(_remote(T.block(ins[t], 2 * j + 1 - c), landed[t].at[j], send_sems, recv_sems, (t, j),
                                     (x, y, 1 - c)))
            for k in range(3):
                loads.append(pltpu.make_async_copy(T.block(ins[t], 2 * chip_of[k] + c), mine[t].at[k], loc_sems.at[t, k]))
        for cp in swaps + loads:
            cp.start()
        for cp in loads:
            cp.wait()
        for cp in swaps:
            cp.wait_recv()
        stores = []
        for t, T in enumerate(tensors):
            for k in range(3):
                acc, got = mine[t].at[k], landed[t].at[chip_of[k]]

                def add(i, carry, acc=acc, got=got):
                    rows = pl.ds(pl.multiple_of(i * PAIR_ROWS, PAIR_ROWS), PAIR_ROWS)
                    acc[rows, :] = (acc[rows, :].astype(F32) + got[rows, :].astype(F32)).astype(BF16)
                    return carry

                lax.fori_loop(0, T.block_shape[0] // PAIR_ROWS, add, 0)
            stores.append(pltpu.make_async_copy(mine[t], sums_out[t], loc_sems.at[t, 3]))
            stores.append(pltpu.make_async_copy(landed[t].at[2 * x + y], own_out[t], loc_sems.at[t, 4]))
        for cp in stores:
            cp.start()
        for cp in swaps:
            cp.wait_send()
        for cp in stores:
            cp.wait()

    blocks = [T.block_shape for T in tensors]
    return pl.pallas_call(
        body, name=name,
        in_specs=[HBM_SPEC] * nt, out_specs=[HBM_SPEC] * (2 * nt),
        out_shape=[jax.ShapeDtypeStruct(b, BF16) for b in blocks] + [jax.ShapeDtypeStruct((3,) + b, BF16) for b in blocks],
        scratch_shapes=[pltpu.VMEM((4,) + b, BF16) for b in blocks] + [pltpu.VMEM((3,) + b, BF16) for b in blocks]
        + [pltpu.SemaphoreType.DMA((nt, 4)), pltpu.SemaphoreType.DMA((nt, 4)), pltpu.SemaphoreType.DMA((nt, 5))],
        compiler_params=pltpu.CompilerParams(vmem_limit_bytes=VMEM_LIMIT),
    )(*grads)


class _Scatter:
    def middles(self, steps):
        return []

    def __init__(self, tensors):
        n = len(tensors)
        self.in_specs = [HBM_SPEC] * n
        self.out_specs = [HBM_SPEC] * n
        self.out_shape = [jax.ShapeDtypeStruct((3,) + T.block_shape, BF16) for T in tensors]
        self.scratch_shapes = [pltpu.SemaphoreType.DMA((n, 3)), pltpu.SemaphoreType.DMA((n, 3))]

    def _copies(self, ins, outs, scratch):
        send_sems, recv_sems = scratch
        x, y, c, chips = _place()
        return [_remote(ins[t].at[k], outs[t].at[k], send_sems, recv_sems, (t, k), (*chip, c))
                for t in range(len(ins)) for k, chip in enumerate(chips)]

    def start(self, ins, outs, scratch):
        for cp in self._copies(ins, outs, scratch):
            cp.start()

    def finish(self, ins, outs, scratch):
        for cp in self._copies(ins, outs, scratch):
            cp.wait()


def _chip_scatter(sums, tensors, name):
    scatter = _Scatter(tensors)
    n = len(tensors)

    def body(*refs):
        ins, outs, scratch = refs[:n], refs[n:2 * n], refs[2 * n:]
        scatter.start(ins, outs, scratch)
        scatter.finish(ins, outs, scratch)

    return pl.pallas_call(
        body, name=name, in_specs=scatter.in_specs, out_specs=scatter.out_specs, out_shape=scatter.out_shape,
        scratch_shapes=scatter.scratch_shapes,
    )(*sums)


def _adamw(w, g, m, v):
    m = ADAM_B1 * m + (1.0 - ADAM_B1) * g
    v = ADAM_B2 * v + (1.0 - ADAM_B2) * (g * g)
    m_hat = m / (1.0 - ADAM_B1 ** ADAM_STEP)
    v_hat = v / (1.0 - ADAM_B2 ** ADAM_STEP)
    delta = -ADAM_LR * (m_hat / (jnp.sqrt(v_hat) + ADAM_EPS) + ADAM_WD * w)
    return delta, m, v


def _final_sum(T, g, lz1, lz2, where, w, m, v):
    rows, cols = T.block_shape
    sub = 4 if T.axis == 0 and rows % 64 == 0 and rows > 256 else 1
    blk = (rows // sub, cols)

    def body(where_ref, g_ref, l1_ref, l2_ref, w_ref, m_ref, v_ref, g_out, d_out, m_out, v_out):
        tot = g_ref[...].astype(F32) + l1_ref[...].astype(F32)
        for k in range(3):
            tot = tot + l2_ref[k].astype(F32)
        g_out[...] = tot
        d_out[...], m_out[...], v_out[...] = _adamw(w_ref[...], tot, m_ref[...], v_ref[...])

    def in_whole(r, wh):
        p = wh[0]
        return (0, p) if T.axis == 1 else (p * sub + r, 0)

    own = pl.BlockSpec(blk, lambda r, wh: (r, 0))
    return pl.pallas_call(
        body, name="grad_final_" + T.name,
        grid_spec=pltpu.PrefetchScalarGridSpec(
            num_scalar_prefetch=1, grid=(sub,),
            in_specs=[pl.BlockSpec(blk, in_whole),
                      own,
                      pl.BlockSpec((3,) + blk, lambda r, wh: (0, r, 0)), own, own, own],
            out_specs=[own] * 4),
        out_shape=[jax.ShapeDtypeStruct(T.block_shape, F32)] * 4,
        compiler_params=_params("arbitrary"),
    )(where, g, lz1, lz2, w, m, v)


MAT_PIECE = MAT_ROWS // 8
VEC_PIECE = DR // 8


class _AllReduce:
    def __init__(self, items):
        self.items = tuple(items)
        n = len(self.items)
        self.in_specs = [HBM_SPEC] * n
        self.out_specs = [HBM_SPEC] * n
        self.out_shape = [jax.ShapeDtypeStruct(shape, F32) for shape, _ in self.items]
        pieces = [(shape[0] // 8, shape[1]) if axis == 0 else (shape[0], shape[1] // 8) for shape, axis in self.items]
        self.scratch_shapes = ([pltpu.VMEM((8,) + p, F32) for p in pieces] + [pltpu.VMEM(p, F32) for p in pieces] + [
            pltpu.SemaphoreType.DMA((2 * n, 8)), pltpu.SemaphoreType.DMA((2 * n, 8)), pltpu.SemaphoreType.DMA((2 * n,))])

    def middles(self, steps):
        return [(steps // 2, self.middle)]

    def _copies(self, ins, outs, scratch):
        n = len(self.items)
        landed, sums, (send_sems, recv_sems, loc_sems) = scratch[:n], scratch[n:2 * n], scratch[2 * n:]
        x, y, c, _ = _place()
        me = _device_index((x, y), c)

        def peer(r):
            return (1 - x if r & 4 else x, 1 - y if r & 2 else y, 1 - c if r & 1 else c)

        def piece(i, ref, p):
            shape, axis = self.items[i]
            if axis == 0:
                rows = shape[0] // 8
                return ref.at[pl.ds(pl.multiple_of(p * rows, 8), rows), :]
            cols = shape[1] // 8
            return ref.at[:, pl.ds(pl.multiple_of(p * cols, 128), cols)]

        own, scatter, arrivals, keep, spread, late = [], [], [], [], [], []
        for i in range(n):
            own.append(pltpu.make_async_copy(piece(i, ins[i], me), landed[i].at[0], loc_sems.at[2 * i]))
            keep.append(pltpu.make_async_copy(sums[i], piece(i, outs[i], me), loc_sems.at[2 * i + 1]))
            for r in range(1, 8):
                to = peer(r)
                p = _device_index(to[:2], to[2])
                scatter.append(_remote(piece(i, ins[i], p), landed[i].at[r], send_sems, recv_sems, (2 * i, r), to))
                spread.append(_remote(sums[i], piece(i, outs[i], me), send_sems, recv_sems, (2 * i + 1, r), to))
                late.append(_remote(sums[i], piece(i, outs[i], p), send_sems, recv_sems, (2 * i + 1, r), to))
        return own, scatter, keep, spread, late, landed, sums

    def start(self, ins, outs, scratch):
        own, scatter, _, _, _, _, _ = self._copies(ins, outs, scratch)
        for cp in own + scatter:
            cp.start()

    def middle(self, ins, outs, scratch):
        own, scatter, keep, spread, _, landed, sums = self._copies(ins, outs, scratch)
        for cp in own:
            cp.wait()
        for cp in scatter:
            cp.wait_recv()
        for i in range(len(self.items)):
            total = landed[i][0]
            for r in range(1, 8):
                total = total + landed[i][r]
            sums[i][...] = total
        for cp in keep + spread:
            cp.start()

    def finish(self, ins, outs, scratch):
        _, scatter, keep, spread, late, _, _ = self._copies(ins, outs, scratch)
        for cp in late:
            cp.wait_recv()
        for cp in scatter + spread:
            cp.wait_send()
        for cp in keep:
            cp.wait()


class _Both:
    def __init__(self, a, b):
        self.a, self.b = a, b
        self.in_specs, self.out_specs = a.in_specs + b.in_specs, a.out_specs + b.out_specs
        self.out_shape, self.scratch_shapes = a.out_shape + b.out_shape, a.scratch_shapes + b.scratch_shapes

    def _each(self, ins, outs, scratch):
        a = self.a
        i, o, s = len(a.in_specs), len(a.out_specs), len(a.scratch_shapes)
        return (a, ins[:i], outs[:o], scratch[:s]), (self.b, ins[i:], outs[o:], scratch[s:])

    def middles(self, steps):
        def of(which, middle):
            return lambda ins, outs, scratch: middle(*self._each(ins, outs, scratch)[which][1:])
        return [(at, of(which, middle)) for which, e in enumerate((self.a, self.b)) for at, middle in e.middles(steps)]

    def start(self, ins, outs, scratch):
        for e, i, o, s in self._each(ins, outs, scratch):
            e.start(i, o, s)

    def finish(self, ins, outs, scratch):
        for e, i, o, s in self._each(ins, outs, scratch):
            e.finish(i, o, s)


def _all_reduce(arrays, items, name):
    reduce = _AllReduce(items)
    n = len(items)

    def body(*refs):
        ins, outs, scratch = refs[:n], refs[n:2 * n], refs[2 * n:]
        reduce.start(ins, outs, scratch)
        reduce.middle(ins, outs, scratch)
        reduce.finish(ins, outs, scratch)

    return pl.pallas_call(
        body, name=name, in_specs=reduce.in_specs, out_specs=reduce.out_specs, out_shape=reduce.out_shape,
        scratch_shapes=reduce.scratch_shapes,
    )(*arrays)


def _adam_small(grads, wmv):
    n = len(grads)

    def body(*refs):
        g_refs, rest = refs[:n], refs[n:]
        ins, outs = rest[:3 * n], rest[3 * n:]
        for i in range(n):
            d, m, v = _adamw(ins[3 * i][...], g_refs[i][...], ins[3 * i + 1][...], ins[3 * i + 2][...])
            outs[3 * i][...], outs[3 * i + 1][...], outs[3 * i + 2][...] = d, m, v

    flat = [a for t in wmv for a in t]
    return pl.pallas_call(
        body, name="adam_small",
        in_specs=[VMEM_SPEC] * (4 * n), out_specs=[VMEM_SPEC] * (3 * n),
        out_shape=[jax.ShapeDtypeStruct(a.shape, F32) for a in flat],
    )(*grads, *flat)


WEIGHT_NAMES = ("norm_mix", "w_in", "w_pool_grp", "pool_scale", "w_pool_out", "conv_w", "conv_b", "w_rg_a", "b_rg_a", "w_rg_x",
                "b_rg_x", "lru_lambda", "w_rnn_out", "w_o", "norm_ffn", "w_ffn_in", "w_ffn_out", "norm_final")


def kernel(x, norm_mix, w_in, w_pool_grp, pool_scale, w_pool_out, conv_w, conv_b, w_rg_a, b_rg_a, w_rg_x, b_rg_x, lru_lambda, w_rnn_out, w_o, norm_ffn, w_ffn_in, w_ffn_out, norm_final, loss_target, m_norm_mix, m_w_in, m_w_pool_grp, m_pool_scale, m_w_pool_out, m_conv_w, m_conv_b, m_w_rg_a, m_b_rg_a, m_w_rg_x, m_b_rg_x, m_lru_lambda, m_w_rnn_out, m_w_o, m_norm_ffn, m_w_ffn_in, m_w_ffn_out, m_norm_final, v_norm_mix, v_w_in, v_w_pool_grp, v_pool_scale, v_w_pool_out, v_conv_w, v_conv_b, v_w_rg_a, v_b_rg_a, v_w_rg_x, v_b_rg_x, v_lru_lambda, v_w_rnn_out, v_w_o, v_norm_ffn, v_w_ffn_in, v_w_ffn_out, v_norm_final):
    w = dict(norm_mix=norm_mix, w_in=w_in, w_pool_grp=w_pool_grp, pool_scale=pool_scale, w_pool_out=w_pool_out, conv_w=conv_w,
             conv_b=conv_b, w_rg_a=w_rg_a, b_rg_a=b_rg_a, w_rg_x=w_rg_x, b_rg_x=b_rg_x, lru_lambda=lru_lambda,
             w_rnn_out=w_rnn_out, w_o=w_o, norm_ffn=norm_ffn, w_ffn_in=w_ffn_in, w_ffn_out=w_ffn_out, norm_final=norm_final)
    m = dict(norm_mix=m_norm_mix, w_in=m_w_in, w_pool_grp=m_w_pool_grp, pool_scale=m_pool_scale, w_pool_out=m_w_pool_out,
             conv_w=m_conv_w, conv_b=m_conv_b, w_rg_a=m_w_rg_a, b_rg_a=m_b_rg_a, w_rg_x=m_w_rg_x, b_rg_x=m_b_rg_x,
             lru_lambda=m_lru_lambda, w_rnn_out=m_w_rnn_out, w_o=m_w_o, norm_ffn=m_norm_ffn, w_ffn_in=m_w_ffn_in,
             w_ffn_out=m_w_ffn_out, norm_final=m_norm_final)
    v = dict(norm_mix=v_norm_mix, w_in=v_w_in, w_pool_grp=v_w_pool_grp, pool_scale=v_pool_scale, w_pool_out=v_w_pool_out,
             conv_w=v_conv_w, conv_b=v_conv_b, w_rg_a=v_w_rg_a, b_rg_a=v_b_rg_a, w_rg_x=v_w_rg_x, b_rg_x=v_b_rg_x,
             lru_lambda=v_lru_lambda, w_rnn_out=v_w_rnn_out, w_o=v_w_o, norm_ffn=v_norm_ffn, w_ffn_in=v_w_ffn_in,
             w_ffn_out=v_w_ffn_out, norm_final=v_norm_final)
    xi, yi, ci = (lax.axis_index(a) for a in MESH_AXES)
    chip = 2 * xi + yi

    def held(T, a):
        return jnp.swapaxes(a, 0, 1) if T.transposed else a

    where = jnp.stack([2 * chip + ci]).astype(jnp.int32)
    by_name = {T.name: T for T in GATHERED}
    block = {T.name: held(T, w[T.name][0]).astype(T.dtype) for T in BIG}
    block["conv_w"] = jnp.pad(conv_w[0], ((0, CONV_W.rows - 4), (0, 0)))
    block["w_ffn_in_lo"], block["w_ffn_in_hi"] = block["w_ffn_in"][:, :D // 2], block["w_ffn_in"][:, D // 2:]

    def gather_of(*names):
        return dict(exchange=_Gather([by_name[n] for n in names]), exchange_operands=[block[n] for n in names])

    def pair_sums(names, partials, tag):
        out = _pair_reduce(partials, [by_name[n] for n in names], "grad_pair_reduce_" + tag)
        return list(out[:len(names)]), list(out[len(names):])

    xs, target = x[0], loss_target[0]
    wg_b, wa_b, wx_b = (a[0].astype(BF16) for a in (w_pool_grp, w_rg_a, w_rg_x))
    ba2, bx2 = b_rg_a.reshape(1, DR), b_rg_x.reshape(1, DR)
    first = ("w_in", "w_pool_out", "w_rnn_out", "conv_w")
    order = jnp.stack([chip, 2 * (1 - xi) + yi, 2 * xi + (1 - yi), 2 * (1 - xi) + (1 - yi)]).astype(jnp.int32)
    (proj, h1), (w_in_g, w_pool_out_g, w_rnn_out_g, conv_g) = _in_proj_gather(
        xs, norm_mix, [block[n] for n in first], [by_name[n] for n in first], order)
    mixer_weights = (wg_b, pool_scale, w_pool_out_g, conv_g[0:4], conv_b, wa_b, ba2, wx_b, bx2, lru_lambda, w_rnn_out_g)
    (_, pm, y_pool, hr, z, y_rnn, kept), (w_o_g, w_ffn_hi_g) = _mixer_fwd(
        proj, *mixer_weights, **gather_of("w_o", "w_ffn_in_hi"))
    (mix, x2, h2), (w_ffn_lo_g,) = _merge_out(xs, proj, y_pool, y_rnn, w_o_g, norm_ffn, **gather_of("w_ffn_in_lo"))
    (gu, act), (w_ffn_out_g,) = _ffn_up(h2, w_ffn_lo_g, w_ffn_hi_g, **gather_of("w_ffn_out"))
    dx3, dx3b, loss_part, dvec_fin = _ffn_down_loss(act, x2, target, w_ffn_out_g, norm_final.reshape(1, D))

    dgu = _ffn_bwd_down(dx3b, gu, w_ffn_out_g)
    dx2, dx2b, dmixo, dvec_ffn = _ffn_bwd_up(dgu, x2, dx3, w_ffn_lo_g, w_ffn_hi_g, norm_ffn, w_o_g)
    names_a = ("w_ffn_in", "w_ffn_out", "w_o")
    part_a = [_wgrad(dgu, h2, "wgrad_ffn_in", 1408, 512), _wgrad(act, dx3b, "wgrad_ffn_out", 1408, 512),
              _wgrad(mix, dx2b, "wgrad_o", 1024, 1024)]
    lz1_a, sums_a = pair_sums(names_a, part_a, "ffn")
    (dproj, dypb, dyrb, dmat, dvec_mix), lz2_a = _mixer_bwd(
        proj, dmixo, y_pool, y_rnn, hr, kept, *mixer_weights,
        exchange=_Scatter([by_name[n] for n in names_a]), exchange_operands=sums_a)
    names_b = ("w_pool_out", "w_rnn_out")
    part_b = [_wgrad(pm, dypb, "wgrad_pool_out", 512, 1024), _wgrad(z, dyrb, "wgrad_rnn_out", 1024, 1024)]
    lz1_b, sums_b = pair_sums(names_b, part_b, "mix")
    dvec = jnp.concatenate([dvec_mix[0:9], dvec_fin[0:1], dvec_ffn[0:1], jnp.pad(loss_part, ((0, 0), (0, DR - 1))),
                            jnp.zeros((VEC_ROWS - 12, DR), F32)], axis=0)
    g_in, exchanged = _wgrad(
        dproj, h1, "wgrad_in", 1152, 1024,
        exchange=_Both(_Scatter([by_name[n] for n in names_b]), _AllReduce([((MAT_ROWS, HD), 0), ((VEC_ROWS, DR), 1)])),
        exchange_operands=sums_b + [dmat, dvec])
    lz2_b, (mat, vec) = exchanged[:2], exchanged[2:]
    loss = vec[VEC_LOSS, 0]
    lz1_c, sums_c = pair_sums(("w_in",), [g_in], "in")
    (grad_x, dvec_in), lz2_c = _in_bwd(dproj, xs, dx2, norm_mix, w_in_g,
                                       exchange=_Scatter([by_name["w_in"]]), exchange_operands=sums_c)
    (vec_in,) = _all_reduce([dvec_in], [((8, D), 1)], "all_reduce_norm_mix")

    grads, delta, new_m, new_v = {}, {}, {}, {}
    for n, g, l1, l2 in zip(names_a + names_b + ("w_in",), part_a + part_b + [g_in], lz1_a + lz1_b + lz1_c,
                            lz2_a + lz2_b + lz2_c):
        T = by_name[n]
        out = _final_sum(T, g, l1, l2, where, held(T, w[n][0]), held(T, m[n][0]), held(T, v[n][0]))
        grads[n], delta[n], new_m[n], new_v[n] = (held(T, a) for a in out)
    me = 4 * xi + 2 * yi + ci
    small_grads = dict(
        w_pool_grp=mat[0:MAT_WA], w_rg_a=mat[MAT_WA:MAT_WX], w_rg_x=mat[MAT_WX:MAT_ROWS],
        pool_scale=vec[VEC_SCALE:VEC_SCALE + 1, 0:DP], conv_b=vec[VEC_CONV_B:VEC_CONV_B + 1],
        b_rg_a=vec[VEC_BA:VEC_BA + 1], b_rg_x=vec[VEC_BX:VEC_BX + 1], lru_lambda=vec[VEC_LAM:VEC_LAM + 1],
        conv_w=lax.dynamic_slice(vec, (VEC_CONV_W, VEC_PIECE * me), (4, VEC_PIECE)),
        norm_final=vec[VEC_NORM_FINAL:VEC_NORM_FINAL + 1], norm_ffn=vec[VEC_NORM_FFN:VEC_NORM_FFN + 1],
        norm_mix=vec_in[0:1])
    names = list(small_grads)
    as2d = lambda a, g: a.reshape(g.shape)
    upd = _adam_small([small_grads[n] for n in names],
                      [(as2d(w[n], small_grads[n]), as2d(m[n], small_grads[n]), as2d(v[n], small_grads[n])) for n in names])
    for i, n in enumerate(names):
        grads[n] = small_grads[n]
        delta[n], new_m[n], new_v[n] = upd[3 * i:3 * i + 3]

    shaped = lambda d: [d[n].reshape(w[n].shape) for n in WEIGHT_NAMES]
    return (loss, grad_x[None], *shaped(grads), *shaped(delta), *shaped(new_m), *shaped(new_v))
```

```python
import functools
import math

import jax
import jax.numpy as jnp
from jax import lax
from jax.experimental import pallas as pl
from jax.experimental.pallas import tpu as pltpu

F32 = jnp.float32
BF16 = jnp.bfloat16

D = 1024
DP = 512
PG = 128
WINDOWS = (2, 4, 8, 16)
DR = 1024
NH = 8
HD = 128
DIN = 4608
DFF = 2816
EPS = 1e-6
LRU_C = 8.0
POOL_HALO = 16
CONV_HALO = 8
KEPT = 5

ADAM_LR = 0.001
ADAM_B1 = 0.9
ADAM_B2 = 0.999
ADAM_EPS = 1e-08
ADAM_WD = 0.01
ADAM_STEP = 10

VMEM_LIMIT = 56 * 1024 * 1024
MESH_AXES = ("x", "y", "c")
MESH = pl.DeviceIdType.MESH


def _dot(a, b):
    return jnp.dot(a, b, preferred_element_type=F32)


def _dot_nt(a, b):
    return lax.dot_general(a, b, (((1,), (1,)), ((), ())), preferred_element_type=F32)


def _dot_tn(a, b):
    return lax.dot_general(a, b, (((0,), (0,)), ((), ())), preferred_element_type=F32)


def _params(*sem):
    return pltpu.CompilerParams(dimension_semantics=sem, vmem_limit_bytes=VMEM_LIMIT)


def _resident(shape):
    nd = len(shape)
    return pl.BlockSpec(shape, lambda i: (0,) * nd, pipeline_mode=pl.Buffered(1))


def _rows(shape_cols, tm):
    return pl.BlockSpec((tm, shape_cols), lambda i: (i, 0))


def _call(body, name, grid, in_specs, out_specs, out_shape, operands, scratch_shapes=(), exchange=None, exchange_operands=()):
    n_in, n_out, n_scr = len(in_specs), len(out_specs), len(scratch_shapes)
    steps = math.prod(grid)
    if exchange is None:
        outs = pl.pallas_call(body, name=name, grid=grid, in_specs=in_specs, out_specs=out_specs, out_shape=out_shape,
                              scratch_shapes=list(scratch_shapes), compiler_params=_params(*["arbitrary"] * len(grid)))(*operands)
        return outs, []
    e_in, e_out = len(exchange.in_specs), len(exchange.out_specs)

    def hosted(*refs):
        ins, refs = refs[:n_in], refs[n_in:]
        e_ins, refs = refs[:e_in], refs[e_in:]
        outs, refs = refs[:n_out], refs[n_out:]
        e_outs, refs = refs[:e_out], refs[e_out:]
        scr, e_scr = refs[:n_scr], refs[n_scr:]
        step = pl.program_id(0)
        for axis in range(1, len(grid)):
            step = step * grid[axis] + pl.program_id(axis)
        pl.when(step == 0)(lambda: exchange.start(e_ins, e_outs, e_scr))
        for at, middle in exchange.middles(steps):
            pl.when(step == at)(lambda middle=middle: middle(e_ins, e_outs, e_scr))
        body(*ins, *outs, *scr)
        pl.when(step == steps - 1)(lambda: exchange.finish(e_ins, e_outs, e_scr))

    outs = pl.pallas_call(
        hosted, name=name, grid=grid, in_specs=list(in_specs) + exchange.in_specs,
        out_specs=list(out_specs) + exchange.out_specs, out_shape=list(out_shape) + exchange.out_shape,
        scratch_shapes=list(scratch_shapes) + exchange.scratch_shapes,
        compiler_params=_params(*["arbitrary"] * len(grid)))(*operands, *exchange_operands)
    return outs[:n_out], outs[n_out:]


GELU_C = math.sqrt(2.0 / math.pi)
GELU_K = 0.044715 * GELU_C


def _gelu(x, with_grad=False):
    x2 = x * x
    t = jnp.tanh(x * (GELU_C + GELU_K * x2))
    hx = 0.5 * x
    y = hx + hx * t
    if not with_grad:
        return y
    return y, 0.5 + 0.5 * t + hx * (1.0 - t * t) * (GELU_C + (3.0 * GELU_K) * x2)


def _softplus_neg(lam):
    z = jnp.exp(-jnp.abs(lam))
    u = 1.0 + z
    dlt = u - 1.0
    log1p = jnp.where(dlt == 0.0, z, jnp.log(u) * (z / jnp.where(dlt == 0.0, 1.0, dlt)))
    return jnp.maximum(-lam, 0.0) + log1p


def _sigmoid(x):
    return 0.5 * jnp.tanh(0.5 * x) + 0.5


def _linear_scan(out_ref, A, B, h0, reverse):
    n = A.shape[0]
    sub = lax.broadcasted_iota(jnp.int32, (8, 1), 0)
    tiles = range(n // 8 - 1, -1, -1) if reverse else range(n // 8)
    carry = h0
    for j in tiles:
        a, b = A[8 * j:8 * j + 8, :], B[8 * j:8 * j + 8, :]
        for d in (1, 2, 4):
            keep = (sub < 8 - d) if reverse else (sub >= d)
            shift = 8 - d if reverse else d
            b = jnp.where(keep, a * pltpu.roll(b, shift, axis=0) + b, b)
            a = jnp.where(keep, a * pltpu.roll(a, shift, axis=0), a)
        h = a * carry + b
        out_ref[8 * j:8 * j + 8, :] = h
        carry = h[0:1, :] if reverse else h[7:8, :]
    return carry


def _pool_windows(ext, shift_sign):
    n = ext.shape[0]
    s = ext
    outs = []
    for w in WINDOWS:
        d = w // 2
        s = s + pltpu.roll(s, d if shift_sign > 0 else n - d, axis=0)
        outs.append(s[:, :PG])
        s = s[:, PG:]
    return outs


def _conv_taps(uext):
    taps = []
    for k in range(4):
        sh = 3 - k
        v = uext if sh == 0 else pltpu.roll(uext, sh, axis=0)
        taps.append(v[CONV_HALO:, :])
    return taps


def _gates(v, wa_ref, ba_ref, wx_ref, bx_ref, sp):
    vb = v.astype(BF16)
    ra, rx = [], []
    for h in range(NH):
        vh = vb[:, h * HD:(h + 1) * HD]
        ra.append(_dot(vh, wa_ref[h]))
        rx.append(_dot(vh, wx_ref[h]))
    r = _sigmoid(jnp.concatenate(ra, axis=1) + ba_ref[...])
    i = _sigmoid(jnp.concatenate(rx, axis=1) + bx_ref[...])
    log_a = r * ((-LRU_C) * sp)
    a = jnp.exp(log_a)
    one_minus = -jnp.tanh(log_a) * (1.0 + a * a)
    return r, i, a, jnp.sqrt(one_minus), lax.rsqrt(one_minus)


def _mixer_fwd(proj, wg, scale, w_pool_out, conv_w, conv_b, wa, ba, wx, bx, lam, w_rnn_out, exchange=None,
               exchange_operands=(), tm=256):
    S = proj.shape[0]
    UW = DP + 2 * DR

    def body(proj_ref, wg_ref, scale_ref, wpo_ref, cw_ref, cb_ref, wa_ref, ba_ref, wx_ref, bx_ref, lam_ref, wro_ref,
             pooled_ref, pm_ref, ypool_ref, hr_ref, z_ref, yrnn_ref, kept_ref, pool_carry, conv_carry, h_carry):
        i = pl.program_id(0)

        @pl.when(i == 0)
        def _():
            pool_carry[...] = jnp.zeros_like(pool_carry)
            conv_carry[...] = jnp.zeros_like(conv_carry)
            h_carry[...] = jnp.zeros_like(h_carry)

        rows = lax.broadcasted_iota(jnp.int32, (tm, 1), 0)
        t_glob = i * tm + rows

        u_pool = proj_ref[:, 0:DP]
        ext = jnp.concatenate([pool_carry[...], u_pool], axis=0)
        pool_carry[...] = u_pool[tm - POOL_HALO:, :]
        sums = _pool_windows(ext, +1)
        mixed = []
        for g, w in enumerate(WINDOWS):
            inv_cnt = 1.0 / jnp.minimum(t_glob + 1, w).astype(F32)
            pooled_g = sums[g][POOL_HALO:, :] * inv_cnt - u_pool[:, g * PG:(g + 1) * PG]
            pooled_b = pooled_g.astype(BF16)
            pooled_ref[:, g * PG:(g + 1) * PG] = pooled_b
            mixed.append(_dot(pooled_b, wg_ref[g]))
        pm = (jnp.concatenate(mixed, axis=1) * scale_ref[...]).astype(BF16)
        pm_ref[...] = pm
        ypool_ref[...] = _dot(pm, wpo_ref[...])

        u_rnn = proj_ref[:, DP:DP + DR]
        uext = jnp.concatenate([conv_carry[...], u_rnn], axis=0)
        conv_carry[...] = u_rnn[tm - CONV_HALO:, :]
        taps = _conv_taps(uext)
        v = cb_ref[...]
        for k in range(4):
            v = v + taps[k] * cw_ref[k:k + 1, :]
        sp = _softplus_neg(lam_ref[...])
        r, gi, a, mult, _ = _gates(v, wa_ref, ba_ref, wx_ref, bx_ref, sp)
        for k, kept in enumerate((v, r, gi, a, mult)):
            kept_ref[k] = kept
        h_carry[0:1, :] = _linear_scan(hr_ref, a, mult * gi * v, h_carry[0:1, :], reverse=False)
        z = (hr_ref[...] * _gelu(proj_ref[:, DP + DR:UW])).astype(BF16)
        z_ref[...] = z
        yrnn_ref[...] = _dot(z, wro_ref[...])

    return _call(
        body, "mixer_fwd", (S // tm,),
        in_specs=[_rows(UW, tm), _resident((4, PG, PG)), _resident((1, DP)), _resident((DP, D)), _resident((4, DR)),
                  _resident((1, DR)), _resident((NH, HD, HD)), _resident((1, DR)), _resident((NH, HD, HD)),
                  _resident((1, DR)), _resident((1, DR)), _resident((DR, D))],
        out_specs=[_rows(DP, tm), _rows(DP, tm), _rows(D, tm), _rows(DR, tm), _rows(DR, tm), _rows(D, tm),
                   pl.BlockSpec((KEPT, tm, DR), lambda i: (0, i, 0))],
        out_shape=[jax.ShapeDtypeStruct((S, DP), BF16), jax.ShapeDtypeStruct((S, DP), BF16),
                   jax.ShapeDtypeStruct((S, D), F32), jax.ShapeDtypeStruct((S, DR), F32),
                   jax.ShapeDtypeStruct((S, DR), BF16), jax.ShapeDtypeStruct((S, D), F32),
                   jax.ShapeDtypeStruct((KEPT, S, DR), F32)],
        scratch_shapes=[pltpu.VMEM((POOL_HALO, DP), F32), pltpu.VMEM((CONV_HALO, DR), F32), pltpu.VMEM((8, DR), F32)],
        operands=(proj, wg, scale, w_pool_out, conv_w, conv_b, wa, ba, wx, bx, lam, w_rnn_out),
        exchange=exchange, exchange_operands=exchange_operands)


FF_CHUNKS = ((0, 768), (768, 1536), (1536, 2304), (2304, DFF))


def _rms(x):
    r = lax.rsqrt(jnp.mean(x * x, axis=-1, keepdims=True) + EPS)
    return r, x * r


def _rms_bwd(dh, g, r, xh):
    dxh = dh * g
    return r * (dxh - xh * jnp.mean(dxh * xh, axis=-1, keepdims=True))


def _merge_out(x, proj, y_pool, y_rnn, w_o, norm_ffn, exchange=None, exchange_operands=(), tm=512):
    S = x.shape[0]
    GL0 = (DP + 2 * DR) // 512

    def gl_spec(k):
        return pl.BlockSpec((tm, 512), lambda i: (i, GL0 + k))

    def body(x_ref, gl0, gl1, gl2, gl3, yp_ref, yr_ref, wo_ref, gf_ref, mix_ref, x2_ref, h2_ref):
        s_p = _sigmoid(jnp.concatenate([gl0[...], gl1[...]], axis=1))
        s_r = _sigmoid(jnp.concatenate([gl2[...], gl3[...]], axis=1))
        mix = (s_p * yp_ref[...] + s_r * yr_ref[...]).astype(BF16)
        mix_ref[...] = mix
        x2 = x_ref[...] + _dot(mix, wo_ref[...])
        x2_ref[...] = x2
        _, xh2 = _rms(x2)
        h2_ref[...] = (xh2 * gf_ref[...]).astype(BF16)

    return _call(
        body, "merge_out", (S // tm,),
        in_specs=[_rows(D, tm), gl_spec(0), gl_spec(1), gl_spec(2), gl_spec(3), _rows(D, tm), _rows(D, tm),
                  _resident((D, D)), _resident((1, D))],
        out_specs=[_rows(D, tm), _rows(D, tm), _rows(D, tm)],
        out_shape=[jax.ShapeDtypeStruct((S, D), BF16), jax.ShapeDtypeStruct((S, D), F32), jax.ShapeDtypeStruct((S, D), BF16)],
        operands=(x, proj, proj, proj, proj, y_pool, y_rnn, w_o, norm_ffn),
        exchange=exchange, exchange_operands=exchange_operands)


def _ffn_up(h2, w_lo, w_hi, exchange=None, exchange_operands=(), tm=512):
    S = h2.shape[0]
    HALF = D // 2

    def body(h_ref, lo_ref, hi_ref, gu_ref, act_ref):
        h_lo, h_hi = h_ref[:, 0:HALF], h_ref[:, HALF:D]
        for c0, c1 in FF_CHUNKS:
            gate = _dot_nt(h_lo, lo_ref[c0:c1, :]) + _dot_nt(h_hi, hi_ref[c0:c1, :])
            up = _dot_nt(h_lo, lo_ref[DFF + c0:DFF + c1, :]) + _dot_nt(h_hi, hi_ref[DFF + c0:DFF + c1, :])
            gu_ref[:, c0:c1] = gate.astype(BF16)
            gu_ref[:, DFF + c0:DFF + c1] = up.astype(BF16)
            act_ref[:, c0:c1] = (gate * _sigmoid(gate) * up).astype(BF16)

    return _call(
        body, "ffn_up", (S // tm,),
        in_specs=[_rows(D, tm), _resident((2 * DFF, HALF)), _resident((2 * DFF, HALF))],
        out_specs=[_rows(2 * DFF, tm), _rows(DFF, tm)],
        out_shape=[jax.ShapeDtypeStruct((S, 2 * DFF), BF16), jax.ShapeDtypeStruct((S, DFF), BF16)],
        operands=(h2, w_lo, w_hi), exchange=exchange, exchange_operands=exchange_operands)


def _ffn_down_loss(act, x2, target, w_ffn_out, norm_final, tm=512):
    S = act.shape[0]

    def body(act_ref, x2_ref, t_ref, w_ref, gn_ref, dx3_ref, dx3b_ref, loss_ref, dvec_ref):
        i = pl.program_id(0)

        @pl.when(i == 0)
        def _():
            loss_ref[...] = jnp.zeros_like(loss_ref)
            dvec_ref[...] = jnp.zeros_like(dvec_ref)

        x3 = x2_ref[...] + _dot(act_ref[...], w_ref[...])
        r3, xh3 = _rms(x3)
        g_fin = gn_ref[...]
        e = xh3 * g_fin - t_ref[...]
        loss_ref[...] += jnp.sum(e * e, axis=(0, 1), keepdims=True) * (0.5 / D)
        dy = e * (1.0 / D)
        dvec_ref[0:1, :] += jnp.sum(dy * xh3, axis=0, keepdims=True)
        dx3 = _rms_bwd(dy, g_fin, r3, xh3)
        dx3_ref[...] = dx3
        dx3b_ref[...] = dx3.astype(BF16)

    return pl.pallas_call(
        body, name="ffn_down_loss", grid=(S // tm,),
        in_specs=[_rows(DFF, tm), _rows(D, tm), _rows(D, tm), _resident((DFF, D)), _resident((1, D))],
        out_specs=[_rows(D, tm), _rows(D, tm), _resident((1, 1)), _resident((8, D))],
        out_shape=[jax.ShapeDtypeStruct((S, D), F32), jax.ShapeDtypeStruct((S, D), BF16),
                   jax.ShapeDtypeStruct((1, 1), F32), jax.ShapeDtypeStruct((8, D), F32)],
        compiler_params=_params("arbitrary"),
    )(act, x2, target, w_ffn_out, norm_final)


def _ffn_bwd_down(dx3b, gu, w_ffn_out, tm=512):
    S = dx3b.shape[0]

    def body(d_ref, gu_ref, w_ref, dgu_ref):
        d = d_ref[...]
        for c0, c1 in FF_CHUNKS:
            dact = _dot_nt(d, w_ref[c0:c1, :])
            gate = gu_ref[:, c0:c1].astype(F32)
            up = gu_ref[:, DFF + c0:DFF + c1].astype(F32)
            sg = _sigmoid(gate)
            dgu_ref[:, c0:c1] = (dact * up * (sg * (1.0 + gate * (1.0 - sg)))).astype(BF16)
            dgu_ref[:, DFF + c0:DFF + c1] = (dact * (gate * sg)).astype(BF16)

    return pl.pallas_call(
        body, name="ffn_bwd_down", grid=(S // tm,),
        in_specs=[_rows(D, tm), _rows(2 * DFF, tm), _resident((DFF, D))],
        out_specs=_rows(2 * DFF, tm),
        out_shape=jax.ShapeDtypeStruct((S, 2 * DFF), BF16),
        compiler_params=_params("parallel"),
    )(dx3b, gu, w_ffn_out)


def _ffn_bwd_up(dgu, x2, dx3, w_lo, w_hi, norm_ffn, w_o, tm=512):
    S = dgu.shape[0]
    HALF = D // 2

    def body(dgu_ref, x2_ref, dx3_ref, lo_ref, hi_ref, gf_ref, wo_ref, dx2_ref, dx2b_ref, dmixo_ref, dvec_ref):
        i = pl.program_id(0)

        @pl.when(i == 0)
        def _():
            dvec_ref[...] = jnp.zeros_like(dvec_ref)

        dgate, dup = dgu_ref[:, 0:DFF], dgu_ref[:, DFF:2 * DFF]
        dh2 = jnp.concatenate([_dot(dgate, w[0:DFF, :]) + _dot(dup, w[DFF:2 * DFF, :]) for w in (lo_ref, hi_ref)], axis=1)
        r2, xh2 = _rms(x2_ref[...])
        dvec_ref[0:1, :] += jnp.sum(dh2 * xh2, axis=0, keepdims=True)
        dx2 = dx3_ref[...] + _rms_bwd(dh2, gf_ref[...], r2, xh2)
        dx2_ref[...] = dx2
        dx2b = dx2.astype(BF16)
        dx2b_ref[...] = dx2b
        dmixo_ref[...] = _dot_nt(dx2b, wo_ref[...])

    return pl.pallas_call(
        body, name="ffn_bwd_up", grid=(S // tm,),
        in_specs=[_rows(2 * DFF, tm), _rows(D, tm), _rows(D, tm), _resident((2 * DFF, HALF)), _resident((2 * DFF, HALF)),
                  _resident((1, D)), _resident((D, D))],
        out_specs=[_rows(D, tm), _rows(D, tm), _rows(D, tm), _resident((8, D))],
        out_shape=[jax.ShapeDtypeStruct((S, D), F32), jax.ShapeDtypeStruct((S, D), BF16), jax.ShapeDtypeStruct((S, D), F32),
                   jax.ShapeDtypeStruct((8, D), F32)],
        compiler_params=_params("arbitrary"),
    )(dgu, x2, dx3, w_lo, w_hi, norm_ffn, w_o)


VEC_ROWS = 16
MAT_WA = 4 * PG
MAT_WX = MAT_WA + NH * HD
MAT_ROWS = MAT_WX + NH * HD


def _mixer_bwd(proj, dmixo, y_pool, y_rnn, hr, kept, wg, scale, w_pool_out, conv_w, conv_b, wa, ba, wx, bx, lam, w_rnn_out,
               exchange=None, exchange_operands=(), tm=256):
    S = proj.shape[0]
    nt = S // tm

    def rev(cols):
        return pl.BlockSpec((tm, cols), lambda i: (nt - 1 - i, 0))

    def halo(rows_, cols):
        per = tm // rows_
        return pl.BlockSpec((rows_, cols), lambda i: (jnp.maximum((nt - 1 - i) * per - 1, 0), 0))

    def body(proj_ref, projh_ref, dmixo_ref, yp_ref, yr_ref, hr_ref, hrh_ref, kept_ref, wg_ref, scale_ref, wpo_ref, cw_ref, cb_ref,
             wa_ref, ba_ref, wx_ref, bx_ref, lam_ref, wro_ref,
             dproj_ref, dypb_ref, dyrb_ref, dmat_ref, dvec_ref,
             q_carry, dv_carry, a_carry, g_carry, g_scr):
        i = pl.program_id(0)
        ti = nt - 1 - i

        @pl.when(i == 0)
        def _():
            q_carry[...] = jnp.zeros_like(q_carry)
            dv_carry[...] = jnp.zeros_like(dv_carry)
            a_carry[...] = jnp.zeros_like(a_carry)
            g_carry[...] = jnp.zeros_like(g_carry)
            dmat_ref[...] = jnp.zeros_like(dmat_ref)
            dvec_ref[...] = jnp.zeros_like(dvec_ref)

        rows = lax.broadcasted_iota(jnp.int32, (tm, 1), 0)
        t_glob = ti * tm + rows
        has_prev = (ti > 0).astype(F32)
        dmixo = dmixo_ref[...]

        s_p = _sigmoid(proj_ref[:, DP + 2 * DR:DP + 2 * DR + D])
        s_r = _sigmoid(proj_ref[:, DP + 2 * DR + D:DIN])
        dproj_ref[:, DP + 2 * DR:DP + 2 * DR + D] = (dmixo * yp_ref[...] * s_p * (1.0 - s_p)).astype(BF16)
        dproj_ref[:, DP + 2 * DR + D:DIN] = (dmixo * yr_ref[...] * s_r * (1.0 - s_r)).astype(BF16)
        dyp = (dmixo * s_p).astype(BF16)
        dyr = (dmixo * s_r).astype(BF16)
        dypb_ref[...] = dyp
        dyrb_ref[...] = dyr

        dz = _dot_nt(dyr, wro_ref[...])
        u_gate = proj_ref[:, DP + DR:DP + 2 * DR]
        gg, dgelu = _gelu(u_gate, with_grad=True)
        hr_t = hr_ref[...]
        dproj_ref[:, DP + DR:DP + 2 * DR] = (dz * hr_t * dgelu).astype(BF16)
        dhr = dz * gg

        sp = _softplus_neg(lam_ref[...])
        v, r, gi, a, mult = (kept_ref[k] for k in range(KEPT))
        inv_mult = 1.0 / mult

        C = jnp.where(rows == tm - 1, a_carry[0:1, :], pltpu.roll(a, tm - 1, axis=0))
        g_carry[0:1, :] = _linear_scan(g_scr, C, dhr, g_carry[0:1, :], reverse=True)
        a_carry[0:1, :] = a[0:1, :]
        g = g_scr[...]

        h_prev = jnp.where(rows == 0, hrh_ref[7:8, :] * has_prev, pltpu.roll(hr_t, 1, axis=0))
        da = g * h_prev
        gm = g * mult
        dmult = g * gi * v
        di = gm * v
        dv = gm * gi
        dlog_a = da * a - dmult * (a * a * inv_mult)
        dvec_ref[4:5, :] += jnp.sum(dlog_a * r, axis=0, keepdims=True)
        dra = (dlog_a * ((-LRU_C) * sp) * r * (1.0 - r))
        drx = di * gi * (1.0 - gi)
        dvec_ref[2:3, :] += jnp.sum(dra, axis=0, keepdims=True)
        dvec_ref[3:4, :] += jnp.sum(drx, axis=0, keepdims=True)
        drab = dra.astype(BF16)
        drxb = drx.astype(BF16)
        vb = v.astype(BF16)
        dvg = []
        for h in range(NH):
            sl = slice(h * HD, (h + 1) * HD)
            dvg.append(_dot_nt(drab[:, sl], wa_ref[h]) + _dot_nt(drxb[:, sl], wx_ref[h]))
            dmat_ref[MAT_WA + h * HD:MAT_WA + (h + 1) * HD, :] += _dot_tn(vb[:, sl], drab[:, sl])
            dmat_ref[MAT_WX + h * HD:MAT_WX + (h + 1) * HD, :] += _dot_tn(vb[:, sl], drxb[:, sl])
        dv = dv + jnp.concatenate(dvg, axis=1)
        dvec_ref[1:2, :] += jnp.sum(dv, axis=0, keepdims=True)
        dvext = jnp.concatenate([dv, dv_carry[...]], axis=0)
        dv_carry[...] = dv[0:CONV_HALO, :]
        n = tm + CONV_HALO
        u_rnn = proj_ref[:, DP:DP + DR]
        du_rnn = dv * cw_ref[3:4, :]
        dvec_ref[8:9, :] += jnp.sum(dv * u_rnn, axis=0, keepdims=True)
        for k in range(3):
            dv_k = pltpu.roll(dvext, n - (3 - k), axis=0)[0:tm, :]
            du_rnn = du_rnn + dv_k * cw_ref[k:k + 1, :]
            dvec_ref[5 + k:6 + k, :] += jnp.sum(dv_k * u_rnn, axis=0, keepdims=True)
        dproj_ref[:, DP:DP + DR] = du_rnn.astype(BF16)

        dpm = _dot_nt(dyp, wpo_ref[...])
        u_pool = proj_ref[:, 0:DP]
        ext = jnp.concatenate([projh_ref[:, 0:DP] * has_prev, u_pool], axis=0)
        sums = _pool_windows(ext, +1)
        scale_v = scale_ref[...]
        qs = []
        dpooled = []
        dscale = []
        for gi_, w in enumerate(WINDOWS):
            sl = slice(gi_ * PG, (gi_ + 1) * PG)
            inv_cnt = 1.0 / jnp.minimum(t_glob + 1, w).astype(F32)
            pooled_b = (sums[gi_][POOL_HALO:, :] * inv_cnt - u_pool[:, sl]).astype(BF16)
            mixed_g = _dot(pooled_b, wg_ref[gi_])
            dscale.append(jnp.sum(dpm[:, sl] * mixed_g, axis=0, keepdims=True))
            dmixed_b = (dpm[:, sl] * scale_v[:, sl]).astype(BF16)
            dmat_ref[gi_ * PG:(gi_ + 1) * PG, :] += _dot_tn(pooled_b, dmixed_b)
            dp_g = _dot_nt(dmixed_b, wg_ref[gi_])
            dpooled.append(dp_g)
            qs.append(dp_g * inv_cnt)
        dvec_ref[0:1, 0:DP] += jnp.concatenate(dscale, axis=1)
        q = jnp.concatenate(qs, axis=1)
        qext = jnp.concatenate([q, q_carry[...]], axis=0)
        q_carry[...] = q[0:POOL_HALO, :]
        tsum = _pool_windows(qext, -1)
        for gi_ in range(4):
            dproj_ref[:, gi_ * PG:(gi_ + 1) * PG] = (tsum[gi_][0:tm, :] - dpooled[gi_]).astype(BF16)

        @pl.when(i == nt - 1)
        def _():
            dvec_ref[4:5, :] = dvec_ref[4:5, :] * (LRU_C * _sigmoid(-lam_ref[...]))

    return _call(
        body, "mixer_bwd", (nt,),
        in_specs=[rev(DIN), halo(POOL_HALO, DIN), rev(D), rev(D), rev(D), rev(DR), halo(8, DR),
                  pl.BlockSpec((KEPT, tm, DR), lambda i: (0, nt - 1 - i, 0)), _resident((4, PG, PG)), _resident((1, DP)), _resident((DP, D)), _resident((4, DR)), _resident((1, DR)),
                  _resident((NH, HD, HD)), _resident((1, DR)), _resident((NH, HD, HD)), _resident((1, DR)),
                  _resident((1, DR)), _resident((DR, D))],
        out_specs=[rev(DIN), rev(D), rev(D), _resident((MAT_ROWS, HD)), _resident((VEC_ROWS, DR))],
        out_shape=[jax.ShapeDtypeStruct((S, DIN), BF16), jax.ShapeDtypeStruct((S, D), BF16),
                   jax.ShapeDtypeStruct((S, D), BF16), jax.ShapeDtypeStruct((MAT_ROWS, HD), F32),
                   jax.ShapeDtypeStruct((VEC_ROWS, DR), F32)],
        scratch_shapes=[pltpu.VMEM((POOL_HALO, DP), F32), pltpu.VMEM((CONV_HALO, DR), F32), pltpu.VMEM((8, DR), F32),
                        pltpu.VMEM((8, DR), F32), pltpu.VMEM((tm, DR), F32)],
        operands=(proj, proj, dmixo, y_pool, y_rnn, hr, hr, kept, wg, scale, w_pool_out, conv_w, conv_b, wa, ba, wx, bx, lam,
                  w_rnn_out),
        exchange=exchange, exchange_operands=exchange_operands)


def _in_bwd(dproj, x, dx2, norm_mix, w_in, exchange=None, exchange_operands=(), tm=512):
    S = x.shape[0]

    def body(dp_ref, x_ref, dx2_ref, g_ref, w_ref, dx_ref, dg_ref):
        i = pl.program_id(0)

        @pl.when(i == 0)
        def _():
            dg_ref[...] = jnp.zeros_like(dg_ref)

        dh = _dot(dp_ref[:, 0:1536], w_ref[0:1536, :])
        dh = dh + _dot(dp_ref[:, 1536:3072], w_ref[1536:3072, :])
        dh = dh + _dot(dp_ref[:, 3072:DIN], w_ref[3072:DIN, :])
        xv = x_ref[...]
        r = lax.rsqrt(jnp.mean(xv * xv, axis=-1, keepdims=True) + EPS)
        xh = xv * r
        dg_ref[0:1, :] += jnp.sum(dh * xh, axis=0, keepdims=True)
        dxh = dh * g_ref[...]
        dx_ref[...] = dx2_ref[...] + r * (dxh - xh * jnp.mean(dxh * xh, axis=-1, keepdims=True))

    return _call(
        body, "in_bwd", (S // tm,),
        in_specs=[_rows(DIN, tm), _rows(D, tm), _rows(D, tm), _resident((1, D)), _resident((DIN, D))],
        out_specs=[_rows(D, tm), _resident((8, D))],
        out_shape=[jax.ShapeDtypeStruct((S, D), F32), jax.ShapeDtypeStruct((8, D), F32)],
        operands=(dproj, x, dx2, norm_mix, w_in), exchange=exchange, exchange_operands=exchange_operands)


def _wgrad(a, b, name, tk, tn, exchange=None, exchange_operands=()):
    S, K = a.shape
    N = b.shape[1]

    def body(a_ref, b_ref, o_ref):
        o_ref[...] = _dot_tn(a_ref[...], b_ref[...]).astype(BF16)

    (out,), exchanged = _call(
        body, name, (K // tk, N // tn),
        in_specs=[pl.BlockSpec((S, tk), lambda k, n: (0, k)), pl.BlockSpec((S, tn), lambda k, n: (0, n))],
        out_specs=[pl.BlockSpec((tk, tn), lambda k, n: (k, n))],
        out_shape=[jax.ShapeDtypeStruct((K, N), BF16)],
        operands=(a, b), exchange=exchange, exchange_operands=exchange_operands)
    return (out, exchanged) if exchange is not None else out


VEC_SCALE, VEC_CONV_B, VEC_BA, VEC_BX, VEC_LAM, VEC_CONV_W, VEC_NORM_FINAL, VEC_NORM_FFN = 0, 1, 2, 3, 4, 5, 9, 10
VEC_LOSS = 11


class _Big:
    def __init__(self, name, rows, cols, axis, n, dtype=BF16, transposed=False, src_cols=None):
        self.name, self.rows, self.cols, self.axis, self.n, self.dtype = name, rows, cols, axis, n, dtype
        self.transposed = transposed
        self.src_cols = src_cols
        self.block_shape = (rows, n) if axis == 1 else (n, cols)

    def block(self, ref, p):
        if self.axis == 1:
            return ref.at[:, pl.ds(pl.multiple_of(p * self.n, 128), self.n)]
        return ref.at[pl.ds(pl.multiple_of(p * self.n, 16 if self.dtype == BF16 else 8), self.n), :]

    def block_index(self, p):
        return (0, p) if self.axis == 1 else (p, 0)


BIG = (_Big("w_in", DIN, D, 0, DIN // 8, transposed=True), _Big("w_pool_out", DP, D, 1, D // 8),
       _Big("w_rnn_out", DR, D, 0, DR // 8), _Big("w_o", D, D, 0, D // 8),
       _Big("w_ffn_in", 2 * DFF, D, 0, 2 * DFF // 8, transposed=True), _Big("w_ffn_out", DFF, D, 0, DFF // 8))
CONV_W = _Big("conv_w", 8, DR, 1, DR // 8, F32)
W_FFN_IN_HALVES = (_Big("w_ffn_in_lo", 2 * DFF, D // 2, 0, 2 * DFF // 8, src_cols=(0, D // 2)),
                   _Big("w_ffn_in_hi", 2 * DFF, D // 2, 0, 2 * DFF // 8, src_cols=(D // 2, D)))
GATHERED = BIG + (CONV_W,) + W_FFN_IN_HALVES

HBM_SPEC = pl.BlockSpec(memory_space=pl.ANY)
VMEM_SPEC = pl.BlockSpec(memory_space=pltpu.VMEM)


def _place():
    x, y, c = (lax.axis_index(a) for a in MESH_AXES)
    other_chips = [(1 - x, y), (x, 1 - y), (1 - x, 1 - y)]
    return x, y, c, other_chips


def _remote(src, dst, send_sems, recv_sems, idx, to):
    return pltpu.make_async_remote_copy(src_ref=src, dst_ref=dst, send_sem=send_sems.at[idx], recv_sem=recv_sems.at[idx],
                                        device_id=to, device_id_type=MESH)


def _device_index(chip, core):
    return 4 * chip[0] + 2 * chip[1] + core


class _Gather:
    def __init__(self, tensors):
        self.tensors = tuple(tensors)
        n = len(self.tensors)
        self.in_specs = [HBM_SPEC] * n
        self.out_specs = [HBM_SPEC] * n
        self.out_shape = [jax.ShapeDtypeStruct((T.rows, T.cols), T.dtype) for T in self.tensors]
        self.scratch_shapes = [pltpu.VMEM(T.block_shape, T.dtype) for T in self.tensors] + [
            pltpu.VMEM(T.block_shape, F32) for T in self.tensors] + [
            pltpu.SemaphoreType.DMA((n, 7)), pltpu.SemaphoreType.DMA((n, 7)), pltpu.SemaphoreType.DMA((n, 2))]

    def middles(self, steps):
        return [(steps - 1, self.middle)]

    def _copies(self, ins, outs, scratch):
        n = len(self.tensors)
        mine, raw, (send_sems, recv_sems, loc_sems) = scratch[:n], scratch[n:2 * n], scratch[2 * n:]
        x, y, c, chips = _place()
        sibling = (x, y, 1 - c)
        me = _device_index((x, y), c)
        loads, stores, first, passed, arrivals, late = [], [], [], [], [], []
        for t, T in enumerate(self.tensors):
            place = T.block(outs[t], me)
            src = ins[t] if T.src_cols is None else ins[t].at[:, T.src_cols[0]:T.src_cols[1]]
            loads.append(pltpu.make_async_copy(src, raw[t], loc_sems.at[t, 0]))
            stores.append(pltpu.make_async_copy(mine[t], place, loc_sems.at[t, 1]))
            first.append(_remote(mine[t], place, send_sems, recv_sems, (t, 0), sibling))
            theirs = T.block(outs[t], _device_index((x, y), 1 - c))
            late.append(_remote(theirs, theirs, send_sems, recv_sems, (t, 0), sibling))
            for k, chip in enumerate(chips):
                first.append(_remote(mine[t], place, send_sems, recv_sems, (t, 1 + k), (*chip, c)))
                land = T.block(outs[t], _device_index(chip, c))
                arrivals.append(_remote(land, land, send_sems, recv_sems, (t, 1 + k), sibling))
                passed.append(_remote(land, land, send_sems, recv_sems, (t, 4 + k), sibling))
                theirs = T.block(outs[t], _device_index(chip, 1 - c))
                late.append(_remote(theirs, theirs, send_sems, recv_sems, (t, 4 + k), sibling))
        return loads, stores, first, passed, arrivals, late

    def start(self, ins, outs, scratch):
        loads, stores, first, _, _, _ = self._copies(ins, outs, scratch)
        n = len(self.tensors)
        for cp in loads:
            cp.start()
        for t, cp in enumerate(loads):
            cp.wait()
            scratch[t][...] = scratch[n + t][...].astype(self.tensors[t].dtype)
        for cp in stores + first:
            cp.start()

    def middle(self, ins, outs, scratch, skip=0):
        _, _, _, passed, arrivals, _ = self._copies(ins, outs, scratch)
        for arrived, cp in zip(arrivals[3 * skip:], passed[3 * skip:]):
            arrived.wait_recv()
            cp.start()

    def finish(self, ins, outs, scratch, skip=0):
        _, stores, first, passed, _, late = self._copies(ins, outs, scratch)
        for cp in late[4 * skip:]:
            cp.wait_recv()
        for cp in first + passed:
            cp.wait_send()
        for cp in stores[skip:]:
            cp.wait()


def _in_proj_gather(x, norm_mix, blocks, tensors, order, tm=512):
    S = x.shape[0]
    nt = S // tm
    n = len(tensors)
    gather = _Gather(tensors)
    CB = 2 * tensors[0].n

    def body(order_ref, x_ref, g_ref, *refs):
        ins, (proj_ref, h_ref), outs = refs[:n], refs[n:n + 2], refs[n + 2:2 * n + 2]
        (h_all, w_chip, w_sem), scratch = refs[2 * n + 2:2 * n + 5], refs[2 * n + 5:]
        q, i = pl.program_id(0), pl.program_id(1)
        _, stores, first, passed, arrivals, late = gather._copies(ins, outs, scratch)

        def fetch(chip):
            rows = outs[0].at[pl.ds(pl.multiple_of(chip * CB, 16), CB), :]
            cp = pltpu.make_async_copy(rows, w_chip, w_sem)
            cp.start()
            cp.wait()

        @pl.when((q == 0) & (i == 0))
        def _():
            gather.start(ins, outs, scratch)
            late[0].wait_recv()
            stores[0].wait()
            fetch(order_ref[0])

        for k in range(3):
            @pl.when((q == k + 1) & (i == 0))
            def _(k=k):
                arrivals[k].wait_recv()
                passed[k].start()
                late[1 + k].wait_recv()
                fetch(order_ref[k + 1])

        rows = pl.ds(pl.multiple_of(i * tm, tm), tm)

        @pl.when(q == 0)
        def _():
            xv = x_ref[...]
            r = lax.rsqrt(jnp.mean(xv * xv, axis=-1, keepdims=True) + EPS)
            h = (xv * r * g_ref[...]).astype(BF16)
            h_all[rows, :] = h
            h_ref[...] = h

        proj_ref[...] = _dot_nt(h_all[rows, :], w_chip[...])

        @pl.when((q == 3) & (i == nt - 1))
        def _():
            gather.middle(ins, outs, scratch, skip=1)
            gather.finish(ins, outs, scratch, skip=1)

    row_tile = lambda q, i, order: (jnp.where(q == 0, i, nt - 1), 0)
    whole = lambda shape: pl.BlockSpec(shape, lambda q, i, order: (0,) * len(shape), pipeline_mode=pl.Buffered(1))
    outs = pl.pallas_call(
        body, name="in_proj_gather",
        grid_spec=pltpu.PrefetchScalarGridSpec(
            num_scalar_prefetch=1, grid=(4, nt),
            in_specs=[pl.BlockSpec((tm, D), row_tile), whole((1, D))] + gather.in_specs,
            out_specs=[pl.BlockSpec((tm, CB), lambda q, i, order: (i, order[q])), pl.BlockSpec((tm, D), row_tile)]
            + gather.out_specs,
            scratch_shapes=[pltpu.VMEM((S, D), BF16), pltpu.VMEM((CB, D), BF16), pltpu.SemaphoreType.DMA]
            + gather.scratch_shapes),
        out_shape=[jax.ShapeDtypeStruct((S, DIN), F32), jax.ShapeDtypeStruct((S, D), BF16)] + gather.out_shape,
        compiler_params=_params("arbitrary", "arbitrary"),
    )(order, x, norm_mix, *blocks)
    return outs[:2], outs[2:]


PAIR_ROWS = 32


def _pair_reduce(grads, tensors, name):
    nt = len(tensors)

    def body(*refs):
        ins, own_out, sums_out, landed, mine = (refs[k * nt:(k + 1) * nt] for k in range(5))
        send_sems, recv_sems, loc_sems = refs[5 * nt:]
        x, y, c, chips = _place()
        chip_of = [2 * chip[0] + chip[1] for chip in chips]
        swaps, loads = [], []
        for t, T in enumerate(tensors):
            for j in range(4):
                swaps.append(_remote(T.block(ins[t], 2 * j + 1 - c), landed[t].at[j], send_sems, recv_sems, (t, j),
                                     (x, y, 1 - c)))
            for k in range(3):
                loads.append(pltpu.make_async_copy(T.block(ins[t], 2 * chip_of[k] + c), mine[t].at[k], loc_sems.at[t, k]))
        for cp in swaps + loads:
            cp.start()
        for cp in loads:
            cp.wait()
        for cp in swaps:
            cp.wait_recv()
        stores = []
        for t, T in enumerate(tensors):
            for k in range(3):
                acc, got = mine[t].at[k], landed[t].at[chip_of[k]]

                def add(i, carry, acc=acc, got=got):
                    rows = pl.ds(pl.multiple_of(i * PAIR_ROWS, PAIR_ROWS), PAIR_ROWS)
                    acc[rows, :] = (acc[rows, :].astype(F32) + got[rows, :].astype(F32)).astype(BF16)
                    return carry

                lax.fori_loop(0, T.block_shape[0] // PAIR_ROWS, add, 0)
            stores.append(pltpu.make_async_copy(mine[t], sums_out[t], loc_sems.at[t, 3]))
            stores.append(pltpu.make_async_copy(landed[t].at[2 * x + y], own_out[t], loc_sems.at[t, 4]))
        for cp in stores:
            cp.start()
        for cp in swaps:
            cp.wait_send()
        for cp in stores:
            cp.wait()

    blocks = [T.block_shape for T in tensors]
    return pl.pallas_call(
        body, name=name,
        in_specs=[HBM_SPEC] * nt, out_specs=[HBM_SPEC] * (2 * nt),
        out_shape=[jax.ShapeDtypeStruct(b, BF16) for b in blocks] + [jax.ShapeDtypeStruct((3,) + b, BF16) for b in blocks],
        scratch_shapes=[pltpu.VMEM((4,) + b, BF16) for b in blocks] + [pltpu.VMEM((3,) + b, BF16) for b in blocks]
        + [pltpu.SemaphoreType.DMA((nt, 4)), pltpu.SemaphoreType.DMA((nt, 4)), pltpu.SemaphoreType.DMA((nt, 5))],
        compiler_params=pltpu.CompilerParams(vmem_limit_bytes=VMEM_LIMIT),
    )(*grads)


class _Scatter:
    def middles(self, steps):
        return []

    def __init__(self, tensors):
        n = len(tensors)
        self.in_specs = [HBM_SPEC] * n
        self.out_specs = [HBM_SPEC] * n
        self.out_shape = [jax.ShapeDtypeStruct((3,) + T.block_shape, BF16) for T in tensors]
        self.scratch_shapes = [pltpu.SemaphoreType.DMA((n, 3)), pltpu.SemaphoreType.DMA((n, 3))]

    def _copies(self, ins, outs, scratch):
        send_sems, recv_sems = scratch
        x, y, c, chips = _place()
        return [_remote(ins[t].at[k], outs[t].at[k], send_sems, recv_sems, (t, k), (*chip, c))
                for t in range(len(ins)) for k, chip in enumerate(chips)]

    def start(self, ins, outs, scratch):
        for cp in self._copies(ins, outs, scratch):
            cp.start()

    def finish(self, ins, outs, scratch):
        for cp in self._copies(ins, outs, scratch):
            cp.wait()


def _chip_scatter(sums, tensors, name):
    scatter = _Scatter(tensors)
    n = len(tensors)

    def body(*refs):
        ins, outs, scratch = refs[:n], refs[n:2 * n], refs[2 * n:]
        scatter.start(ins, outs, scratch)
        scatter.finish(ins, outs, scratch)

    return pl.pallas_call(
        body, name=name, in_specs=scatter.in_specs, out_specs=scatter.out_specs, out_shape=scatter.out_shape,
        scratch_shapes=scatter.scratch_shapes,
    )(*sums)


def _adamw(w, g, m, v):
    m = ADAM_B1 * m + (1.0 - ADAM_B1) * g
    v = ADAM_B2 * v + (1.0 - ADAM_B2) * (g * g)
    m_hat = m / (1.0 - ADAM_B1 ** ADAM_STEP)
    v_hat = v / (1.0 - ADAM_B2 ** ADAM_STEP)
    delta = -ADAM_LR * (m_hat / (jnp.sqrt(v_hat) + ADAM_EPS) + ADAM_WD * w)
    return delta, m, v


def _final_sum(T, g, lz1, lz2, where, w, m, v):
    rows, cols = T.block_shape
    sub = 4 if T.axis == 0 and rows % 64 == 0 and rows > 256 else 1
    blk = (rows // sub, cols)

    def body(where_ref, g_ref, l1_ref, l2_ref, w_ref, m_ref, v_ref, g_out, d_out, m_out, v_out):
        tot = g_ref[...].astype(F32) + l1_ref[...].astype(F32)
        for k in range(3):
            tot = tot + l2_ref[k].astype(F32)
        g_out[...] = tot
        d_out[...], m_out[...], v_out[...] = _adamw(w_ref[...], tot, m_ref[...], v_ref[...])

    def in_whole(r, wh):
        p = wh[0]
        return (0, p) if T.axis == 1 else (p * sub + r, 0)

    own = pl.BlockSpec(blk, lambda r, wh: (r, 0))
    return pl.pallas_call(
        body, name="grad_final_" + T.name,
        grid_spec=pltpu.PrefetchScalarGridSpec(
            num_scalar_prefetch=1, grid=(sub,),
            in_specs=[pl.BlockSpec(blk, in_whole),
                      own,
                      pl.BlockSpec((3,) + blk, lambda r, wh: (0, r, 0)), own, own, own],
            out_specs=[own] * 4),
        out_shape=[jax.ShapeDtypeStruct(T.block_shape, F32)] * 4,
        compiler_params=_params("arbitrary"),
    )(where, g, lz1, lz2, w, m, v)


MAT_PIECE = MAT_ROWS // 8
VEC_PIECE = DR // 8


class _AllReduce:
    def __init__(self, items):
        self.items = tuple(items)
        n = len(self.items)
        self.in_specs = [HBM_SPEC] * n
        self.out_specs = [HBM_SPEC] * n
        self.out_shape = [jax.ShapeDtypeStruct(shape, F32) for shape, _ in self.items]
        pieces = [(shape[0] // 8, shape[1]) if axis == 0 else (shape[0], shape[1] // 8) for shape, axis in self.items]
        self.scratch_shapes = ([pltpu.VMEM((8,) + p, F32) for p in pieces] + [pltpu.VMEM(p, F32) for p in pieces] + [
            pltpu.SemaphoreType.DMA((2 * n, 8)), pltpu.SemaphoreType.DMA((2 * n, 8)), pltpu.SemaphoreType.DMA((2 * n,))])

    def middles(self, steps):
        return [(steps // 2, self.middle)]

    def _copies(self, ins, outs, scratch):
        n = len(self.items)
        landed, sums, (send_sems, recv_sems, loc_sems) = scratch[:n], scratch[n:2 * n], scratch[2 * n:]
        x, y, c, _ = _place()
        me = _device_index((x, y), c)

        def peer(r):
            return (1 - x if r & 4 else x, 1 - y if r & 2 else y, 1 - c if r & 1 else c)

        def piece(i, ref, p):
            shape, axis = self.items[i]
            if axis == 0:
                rows = shape[0] // 8
                return ref.at[pl.ds(pl.multiple_of(p * rows, 8), rows), :]
            cols = shape[1] // 8
            return ref.at[:, pl.ds(pl.multiple_of(p * cols, 128), cols)]

        own, scatter, arrivals, keep, spread, late = [], [], [], [], [], []
        for i in range(n):
            own.append(pltpu.make_async_copy(piece(i, ins[i], me), landed[i].at[0], loc_sems.at[2 * i]))
            keep.append(pltpu.make_async_copy(sums[i], piece(i, outs[i], me), loc_sems.at[2 * i + 1]))
            for r in range(1, 8):
                to = peer(r)
                p = _device_index(to[:2], to[2])
                scatter.append(_remote(piece(i, ins[i], p), landed[i].at[r], send_sems, recv_sems, (2 * i, r), to))
                spread.append(_remote(sums[i], piece(i, outs[i], me), send_sems, recv_sems, (2 * i + 1, r), to))
                late.append(_remote(sums[i], piece(i, outs[i], p), send_sems, recv_sems, (2 * i + 1, r), to))
        return own, scatter, keep, spread, late, landed, sums

    def start(self, ins, outs, scratch):
        own, scatter, _, _, _, _, _ = self._copies(ins, outs, scratch)
        for cp in own + scatter:
            cp.start()

    def middle(self, ins, outs, scratch):
        own, scatter, keep, spread, _, landed, sums = self._copies(ins, outs, scratch)
        for cp in own:
            cp.wait()
        for cp in scatter:
            cp.wait_recv()
        for i in range(len(self.items)):
            total = landed[i][0]
            for r in range(1, 8):
                total = total + landed[i][r]
            sums[i][...] = total
        for cp in keep + spread:
            cp.start()

    def finish(self, ins, outs, scratch):
        _, scatter, keep, spread, late, _, _ = self._copies(ins, outs, scratch)
        for cp in late:
            cp.wait_recv()
        for cp in scatter + spread:
            cp.wait_send()
        for cp in keep:
            cp.wait()


class _Both:
    def __init__(self, a, b):
        self.a, self.b = a, b
        self.in_specs, self.out_specs = a.in_specs + b.in_specs, a.out_specs + b.out_specs
        self.out_shape, self.scratch_shapes = a.out_shape + b.out_shape, a.scratch_shapes + b.scratch_shapes

    def _each(self, ins, outs, scratch):
        a = self.a
        i, o, s = len(a.in_specs), len(a.out_specs), len(a.scratch_shapes)
        return (a, ins[:i], outs[:o], scratch[:s]), (self.b, ins[i:], outs[o:], scratch[s:])

    def middles(self, steps):
        def of(which, middle):
            return lambda ins, outs, scratch: middle(*self._each(ins, outs, scratch)[which][1:])
        return [(at, of(which, middle)) for which, e in enumerate((self.a, self.b)) for at, middle in e.middles(steps)]

    def start(self, ins, outs, scratch):
        for e, i, o, s in self._each(ins, outs, scratch):
            e.start(i, o, s)

    def finish(self, ins, outs, scratch):
        for e, i, o, s in self._each(ins, outs, scratch):
            e.finish(i, o, s)


def _all_reduce(arrays, items, name):
    reduce = _AllReduce(items)
    n = len(items)

    def body(*refs):
        ins, outs, scratch = refs[:n], refs[n:2 * n], refs[2 * n:]
        reduce.start(ins, outs, scratch)
        reduce.middle(ins, outs, scratch)
        reduce.finish(ins, outs, scratch)

    return pl.pallas_call(
        body, name=name, in_specs=reduce.in_specs, out_specs=reduce.out_specs, out_shape=reduce.out_shape,
        scratch_shapes=reduce.scratch_shapes,
    )(*arrays)


def _adam_small(grads, wmv):
    n = len(grads)

    def body(*refs):
        g_refs, rest = refs[:n], refs[n:]
        ins, outs = rest[:3 * n], rest[3 * n:]
        for i in range(n):
            d, m, v = _adamw(ins[3 * i][...], g_refs[i][...], ins[3 * i + 1][...], ins[3 * i + 2][...])
            outs[3 * i][...], outs[3 * i + 1][...], outs[3 * i + 2][...] = d, m, v

    flat = [a for t in wmv for a in t]
    return pl.pallas_call(
        body, name="adam_small",
        in_specs=[VMEM_SPEC] * (4 * n), out_specs=[VMEM_SPEC] * (3 * n),
        out_shape=[jax.ShapeDtypeStruct(a.shape, F32) for a in flat],
    )(*grads, *flat)


WEIGHT_NAMES = ("norm_mix", "w_in", "w_pool_grp", "pool_scale", "w_pool_out", "conv_w", "conv_b", "w_rg_a", "b_rg_a", "w_rg_x",
                "b_rg_x", "lru_lambda", "w_rnn_out", "w_o", "norm_ffn", "w_ffn_in", "w_ffn_out", "norm_final")


def kernel(x, norm_mix, w_in, w_pool_grp, pool_scale, w_pool_out, conv_w, conv_b, w_rg_a, b_rg_a, w_rg_x, b_rg_x, lru_lambda, w_rnn_out, w_o, norm_ffn, w_ffn_in, w_ffn_out, norm_final, loss_target, m_norm_mix, m_w_in, m_w_pool_grp, m_pool_scale, m_w_pool_out, m_conv_w, m_conv_b, m_w_rg_a, m_b_rg_a, m_w_rg_x, m_b_rg_x, m_lru_lambda, m_w_rnn_out, m_w_o, m_norm_ffn, m_w_ffn_in, m_w_ffn_out, m_norm_final, v_norm_mix, v_w_in, v_w_pool_grp, v_pool_scale, v_w_pool_out, v_conv_w, v_conv_b, v_w_rg_a, v_b_rg_a, v_w_rg_x, v_b_rg_x, v_lru_lambda, v_w_rnn_out, v_w_o, v_norm_ffn, v_w_ffn_in, v_w_ffn_out, v_norm_final):
    w = dict(norm_mix=norm_mix, w_in=w_in, w_pool_grp=w_pool_grp, pool_scale=pool_scale, w_pool_out=w_pool_out, conv_w=conv_w,
             conv_b=conv_b, w_rg_a=w_rg_a, b_rg_a=b_rg_a, w_rg_x=w_rg_x, b_rg_x=b_rg_x, lru_lambda=lru_lambda,
             w_rnn_out=w_rnn_out, w_o=w_o, norm_ffn=norm_ffn, w_ffn_in=w_ffn_in, w_ffn_out=w_ffn_out, norm_final=norm_final)
    m = dict(norm_mix=m_norm_mix, w_in=m_w_in, w_pool_grp=m_w_pool_grp, pool_scale=m_pool_scale, w_pool_out=m_w_pool_out,
             conv_w=m_conv_w, conv_b=m_conv_b, w_rg_a=m_w_rg_a, b_rg_a=m_b_rg_a, w_rg_x=m_w_rg_x, b_rg_x=m_b_rg_x,
             lru_lambda=m_lru_lambda, w_rnn_out=m_w_rnn_out, w_o=m_w_o, norm_ffn=m_norm_ffn, w_ffn_in=m_w_ffn_in,
             w_ffn_out=m_w_ffn_out, norm_final=m_norm_final)
    v = dict(norm_mix=v_norm_mix, w_in=v_w_in, w_pool_grp=v_w_pool_grp, pool_scale=v_pool_scale, w_pool_out=v_w_pool_out,
             conv_w=v_conv_w, conv_b=v_conv_b, w_rg_a=v_w_rg_a, b_rg_a=v_b_rg_a, w_rg_x=v_w_rg_x, b_rg_x=v_b_rg_x,
             lru_lambda=v_lru_lambda, w_rnn_out=v_w_rnn_out, w_o=v_w_o, norm_ffn=v_norm_ffn, w_ffn_in=v_w_ffn_in,
             w_ffn_out=v_w_ffn_out, norm_final=v_norm_final)
    xi, yi, ci = (lax.axis_index(a) for a in MESH_AXES)
    chip = 2 * xi + yi

    def held(T, a):
        return jnp.swapaxes(a, 0, 1) if T.transposed else a

    where = jnp.stack([2 * chip + ci]).astype(jnp.int32)
    by_name = {T.name: T for T in GATHERED}
    block = {T.name: held(T, w[T.name][0]) for T in BIG}
    block["conv_w"] = jnp.pad(conv_w[0], ((0, CONV_W.rows - 4), (0, 0)))
    block["w_ffn_in_lo"] = block["w_ffn_in_hi"] = block["w_ffn_in"]

    def gather_of(*names):
        return dict(exchange=_Gather([by_name[n] for n in names]), exchange_operands=[block[n] for n in names])

    def pair_sums(names, partials, tag):
        out = _pair_reduce(partials, [by_name[n] for n in names], "grad_pair_reduce_" + tag)
        return list(out[:len(names)]), list(out[len(names):])

    xs, target = x[0], loss_target[0]
    wg_b, wa_b, wx_b = (a[0].astype(BF16) for a in (w_pool_grp, w_rg_a, w_rg_x))
    ba2, bx2 = b_rg_a.reshape(1, DR), b_rg_x.reshape(1, DR)
    first = ("w_in", "w_pool_out", "w_rnn_out", "conv_w")
    order = jnp.stack([chip, 2 * (1 - xi) + yi, 2 * xi + (1 - yi), 2 * (1 - xi) + (1 - yi)]).astype(jnp.int32)
    (proj, h1), (w_in_g, w_pool_out_g, w_rnn_out_g, conv_g) = _in_proj_gather(
        xs, norm_mix, [block[n] for n in first], [by_name[n] for n in first], order)
    mixer_weights = (wg_b, pool_scale, w_pool_out_g, conv_g[0:4], conv_b, wa_b, ba2, wx_b, bx2, lru_lambda, w_rnn_out_g)
    (_, pm, y_pool, hr, z, y_rnn, kept), (w_o_g, w_ffn_hi_g) = _mixer_fwd(
        proj, *mixer_weights, **gather_of("w_o", "w_ffn_in_hi"))
    (mix, x2, h2), (w_ffn_lo_g,) = _merge_out(xs, proj, y_pool, y_rnn, w_o_g, norm_ffn, **gather_of("w_ffn_in_lo"))
    (gu, act), (w_ffn_out_g,) = _ffn_up(h2, w_ffn_lo_g, w_ffn_hi_g, **gather_of("w_ffn_out"))
    dx3, dx3b, loss_part, dvec_fin = _ffn_down_loss(act, x2, target, w_ffn_out_g, norm_final.reshape(1, D))

    dgu = _ffn_bwd_down(dx3b, gu, w_ffn_out_g)
    dx2, dx2b, dmixo, dvec_ffn = _ffn_bwd_up(dgu, x2, dx3, w_ffn_lo_g, w_ffn_hi_g, norm_ffn, w_o_g)
    names_a = ("w_ffn_in", "w_ffn_out", "w_o")
    part_a = [_wgrad(dgu, h2, "wgrad_ffn_in", 1408, 512), _wgrad(act, dx3b, "wgrad_ffn_out", 1408, 512),
              _wgrad(mix, dx2b, "wgrad_o", 1024, 1024)]
    lz1_a, sums_a = pair_sums(names_a, part_a, "ffn")
    (dproj, dypb, dyrb, dmat, dvec_mix), lz2_a = _mixer_bwd(
        proj, dmixo, y_pool, y_rnn, hr, kept, *mixer_weights,
        exchange=_Scatter([by_name[n] for n in names_a]), exchange_operands=sums_a)
    names_b = ("w_pool_out", "w_rnn_out")
    part_b = [_wgrad(pm, dypb, "wgrad_pool_out", 512, 1024), _wgrad(z, dyrb, "wgrad_rnn_out", 1024, 1024)]
    lz1_b, sums_b = pair_sums(names_b, part_b, "mix")
    dvec = jnp.concatenate([dvec_mix[0:9], dvec_fin[0:1], dvec_ffn[0:1], jnp.pad(loss_part, ((0, 0), (0, DR - 1))),
                            jnp.zeros((VEC_ROWS - 12, DR), F32)], axis=0)
    g_in, exchanged = _wgrad(
        dproj, h1, "wgrad_in", 1152, 1024,
        exchange=_Both(_Scatter([by_name[n] for n in names_b]), _AllReduce([((MAT_ROWS, HD), 0), ((VEC_ROWS, DR), 1)])),
        exchange_operands=sums_b + [dmat, dvec])
    lz2_b, (mat, vec) = exchanged[:2], exchanged[2:]
    loss = vec[VEC_LOSS, 0]
    lz1_c, sums_c = pair_sums(("w_in",), [g_in], "in")
    (grad_x, dvec_in), lz2_c = _in_bwd(dproj, xs, dx2, norm_mix, w_in_g,
                                       exchange=_Scatter([by_name["w_in"]]), exchange_operands=sums_c)
    (vec_in,) = _all_reduce([dvec_in], [((8, D), 1)], "all_reduce_norm_mix")

    grads, delta, new_m, new_v = {}, {}, {}, {}
    for n, g, l1, l2 in zip(names_a + names_b + ("w_in",), part_a + part_b + [g_in], lz1_a + lz1_b + lz1_c,
                            lz2_a + lz2_b + lz2_c):
        T = by_name[n]
        out = _final_sum(T, g, l1, l2, where, held(T, w[n][0]), held(T, m[n][0]), held(T, v[n][0]))
        grads[n], delta[n], new_m[n], new_v[n] = (held(T, a) for a in out)
    me = 4 * xi + 2 * yi + ci
    small_grads = dict(
        w_pool_grp=mat[0:MAT_WA], w_rg_a=mat[MAT_WA:MAT_WX], w_rg_x=mat[MAT_WX:MAT_ROWS],
        pool_scale=vec[VEC_SCALE:VEC_SCALE + 1, 0:DP], conv_b=vec[VEC_CONV_B:VEC_CONV_B + 1],
        b_rg_a=vec[VEC_BA:VEC_BA + 1], b_rg_x=vec[VEC_BX:VEC_BX + 1], lru_lambda=vec[VEC_LAM:VEC_LAM + 1],
        conv_w=lax.dynamic_slice(vec, (VEC_CONV_W, VEC_PIECE * me), (4, VEC_PIECE)),
        norm_final=vec[VEC_NORM_FINAL:VEC_NORM_FINAL + 1], norm_ffn=vec[VEC_NORM_FFN:VEC_NORM_FFN + 1],
        norm_mix=vec_in[0:1])
    names = list(small_grads)
    as2d = lambda a, g: a.reshape(g.shape)
    upd = _adam_small([small_grads[n] for n in names],
                      [(as2d(w[n], small_grads[n]), as2d(m[n], small_grads[n]), as2d(v[n], small_grads[n])) for n in names])
    for i, n in enumerate(names):
        grads[n] = small_grads[n]
        delta[n], new_m[n], new_v[n] = upd[3 * i:3 * i + 3]

    shaped = lambda d: [d[n].reshape(w[n].shape) for n in WEIGHT_NAMES]
    return (loss, grad_x[None], *shaped(grads), *shaped(delta), *shaped(new_m), *shaped(new_v))
```

```python
import functools
import math

import jax
import jax.numpy as jnp
from jax import lax
from jax.experimental import pallas as pl
from jax.experimental.pallas import tpu as pltpu

F32 = jnp.float32
BF16 = jnp.bfloat16

D = 1024
DP = 512
PG = 128
WINDOWS = (2, 4, 8, 16)
DR = 1024
NH = 8
HD = 128
DIN = 4608
DFF = 2816
EPS = 1e-6
LRU_C = 8.0
POOL_HALO = 16
CONV_HALO = 8
KEPT = 5

ADAM_LR = 0.001
ADAM_B1 = 0.9
ADAM_B2 = 0.999
ADAM_EPS = 1e-08
ADAM_WD = 0.01
ADAM_STEP = 10

VMEM_LIMIT = 56 * 1024 * 1024
MESH_AXES = ("x", "y", "c")
MESH = pl.DeviceIdType.MESH


def _dot(a, b):
    return jnp.dot(a, b, preferred_element_type=F32)


def _dot_nt(a, b):
    return lax.dot_general(a, b, (((1,), (1,)), ((), ())), preferred_element_type=F32)


def _dot_tn(a, b):
    return lax.dot_general(a, b, (((0,), (0,)), ((), ())), preferred_element_type=F32)


def _params(*sem):
    return pltpu.CompilerParams(dimension_semantics=sem, vmem_limit_bytes=VMEM_LIMIT)


def _resident(shape):
    nd = len(shape)
    return pl.BlockSpec(shape, lambda i: (0,) * nd, pipeline_mode=pl.Buffered(1))


def _rows(shape_cols, tm):
    return pl.BlockSpec((tm, shape_cols), lambda i: (i, 0))


def _call(body, name, grid, in_specs, out_specs, out_shape, operands, scratch_shapes=(), exchange=None, exchange_operands=()):
    n_in, n_out, n_scr = len(in_specs), len(out_specs), len(scratch_shapes)
    steps = math.prod(grid)
    if exchange is None:
        outs = pl.pallas_call(body, name=name, grid=grid, in_specs=in_specs, out_specs=out_specs, out_shape=out_shape,
                              scratch_shapes=list(scratch_shapes), compiler_params=_params(*["arbitrary"] * len(grid)))(*operands)
        return outs, []
    e_in, e_out = len(exchange.in_specs), len(exchange.out_specs)

    def hosted(*refs):
        ins, refs = refs[:n_in], refs[n_in:]
        e_ins, refs = refs[:e_in], refs[e_in:]
        outs, refs = refs[:n_out], refs[n_out:]
        e_outs, refs = refs[:e_out], refs[e_out:]
        scr, e_scr = refs[:n_scr], refs[n_scr:]
        step = pl.program_id(0)
        for axis in range(1, len(grid)):
            step = step * grid[axis] + pl.program_id(axis)
        pl.when(step == 0)(lambda: exchange.start(e_ins, e_outs, e_scr))
        for at, middle in exchange.middles(steps):
            pl.when(step == at)(lambda middle=middle: middle(e_ins, e_outs, e_scr))
        body(*ins, *outs, *scr)
        pl.when(step == steps - 1)(lambda: exchange.finish(e_ins, e_outs, e_scr))

    outs = pl.pallas_call(
        hosted, name=name, grid=grid, in_specs=list(in_specs) + exchange.in_specs,
        out_specs=list(out_specs) + exchange.out_specs, out_shape=list(out_shape) + exchange.out_shape,
        scratch_shapes=list(scratch_shapes) + exchange.scratch_shapes,
        compiler_params=_params(*["arbitrary"] * len(grid)))(*operands, *exchange_operands)
    return outs[:n_out], outs[n_out:]


GELU_C = math.sqrt(2.0 / math.pi)
GELU_K = 0.044715 * GELU_C


def _gelu(x, with_grad=False):
    x2 = x * x
    t = jnp.tanh(x * (GELU_C + GELU_K * x2))
    hx = 0.5 * x
    y = hx + hx * t
    if not with_grad:
        return y
    return y, 0.5 + 0.5 * t + hx * (1.0 - t * t) * (GELU_C + (3.0 * GELU_K) * x2)


def _softplus_neg(lam):
    z = jnp.exp(-jnp.abs(lam))
    u = 1.0 + z
    dlt = u - 1.0
    log1p = jnp.where(dlt == 0.0, z, jnp.log(u) * (z / jnp.where(dlt == 0.0, 1.0, dlt)))
    return jnp.maximum(-lam, 0.0) + log1p


def _sigmoid(x):
    return 0.5 * jnp.tanh(0.5 * x) + 0.5


def _linear_scan(out_ref, A, B, h0, reverse):
    n = A.shape[0]
    sub = lax.broadcasted_iota(jnp.int32, (8, 1), 0)
    tiles = range(n // 8 - 1, -1, -1) if reverse else range(n // 8)
    carry = h0
    for j in tiles:
        a, b = A[8 * j:8 * j + 8, :], B[8 * j:8 * j + 8, :]
        for d in (1, 2, 4):
            keep = (sub < 8 - d) if reverse else (sub >= d)
            shift = 8 - d if reverse else d
            b = jnp.where(keep, a * pltpu.roll(b, shift, axis=0) + b, b)
            a = jnp.where(keep, a * pltpu.roll(a, shift, axis=0), a)
        h = a * carry + b
        out_ref[8 * j:8 * j + 8, :] = h
        carry = h[0:1, :] if reverse else h[7:8, :]
    return carry


def _pool_windows(ext, shift_sign):
    n = ext.shape[0]
    s = ext
    outs = []
    for w in WINDOWS:
        d = w // 2
        s = s + pltpu.roll(s, d if shift_sign > 0 else n - d, axis=0)
        outs.append(s[:, :PG])
        s = s[:, PG:]
    return outs


def _conv_taps(uext):
    taps = []
    for k in range(4):
        sh = 3 - k
        v = uext if sh == 0 else pltpu.roll(uext, sh, axis=0)
        taps.append(v[CONV_HALO:, :])
    return taps


def _gates(v, wa_ref, ba_ref, wx_ref, bx_ref, sp):
    vb = v.astype(BF16)
    ra, rx = [], []
    for h in range(NH):
        vh = vb[:, h * HD:(h + 1) * HD]
        ra.append(_dot(vh, wa_ref[h]))
        rx.append(_dot(vh, wx_ref[h]))
    r = _sigmoid(jnp.concatenate(ra, axis=1) + ba_ref[...])
    i = _sigmoid(jnp.concatenate(rx, axis=1) + bx_ref[...])
    log_a = r * ((-LRU_C) * sp)
    a = jnp.exp(log_a)
    one_minus = -jnp.tanh(log_a) * (1.0 + a * a)
    return r, i, a, jnp.sqrt(one_minus), lax.rsqrt(one_minus)


def _mixer_fwd(proj, wg, scale, w_pool_out, conv_w, conv_b, wa, ba, wx, bx, lam, w_rnn_out, exchange=None,
               exchange_operands=(), tm=256):
    S = proj.shape[0]
    UW = DP + 2 * DR

    def body(proj_ref, wg_ref, scale_ref, wpo_ref, cw_ref, cb_ref, wa_ref, ba_ref, wx_ref, bx_ref, lam_ref, wro_ref,
             pooled_ref, pm_ref, ypool_ref, hr_ref, z_ref, yrnn_ref, kept_ref, pool_carry, conv_carry, h_carry):
        i = pl.program_id(0)

        @pl.when(i == 0)
        def _():
            pool_carry[...] = jnp.zeros_like(pool_carry)
            conv_carry[...] = jnp.zeros_like(conv_carry)
            h_carry[...] = jnp.zeros_like(h_carry)

        rows = lax.broadcasted_iota(jnp.int32, (tm, 1), 0)
        t_glob = i * tm + rows

        u_pool = proj_ref[:, 0:DP]
        ext = jnp.concatenate([pool_carry[...], u_pool], axis=0)
        pool_carry[...] = u_pool[tm - POOL_HALO:, :]
        sums = _pool_windows(ext, +1)
        mixed = []
        for g, w in enumerate(WINDOWS):
            inv_cnt = 1.0 / jnp.minimum(t_glob + 1, w).astype(F32)
            pooled_g = sums[g][POOL_HALO:, :] * inv_cnt - u_pool[:, g * PG:(g + 1) * PG]
            pooled_b = pooled_g.astype(BF16)
            pooled_ref[:, g * PG:(g + 1) * PG] = pooled_b
            mixed.append(_dot(pooled_b, wg_ref[g]))
        pm = (jnp.concatenate(mixed, axis=1) * scale_ref[...]).astype(BF16)
        pm_ref[...] = pm
        ypool_ref[...] = _dot(pm, wpo_ref[...])

        u_rnn = proj_ref[:, DP:DP + DR]
        uext = jnp.concatenate([conv_carry[...], u_rnn], axis=0)
        conv_carry[...] = u_rnn[tm - CONV_HALO:, :]
        taps = _conv_taps(uext)
        v = cb_ref[...]
        for k in range(4):
            v = v + taps[k] * cw_ref[k:k + 1, :]
        sp = _softplus_neg(lam_ref[...])
        r, gi, a, mult, _ = _gates(v, wa_ref, ba_ref, wx_ref, bx_ref, sp)
        for k, kept in enumerate((v, r, gi, a, mult)):
            kept_ref[k] = kept
        h_carry[0:1, :] = _linear_scan(hr_ref, a, mult * gi * v, h_carry[0:1, :], reverse=False)
        z = (hr_ref[...] * _gelu(proj_ref[:, DP + DR:UW])).astype(BF16)
        z_ref[...] = z
        yrnn_ref[...] = _dot(z, wro_ref[...])

    return _call(
        body, "mixer_fwd", (S // tm,),
        in_specs=[_rows(UW, tm), _resident((4, PG, PG)), _resident((1, DP)), _resident((DP, D)), _resident((4, DR)),
                  _resident((1, DR)), _resident((NH, HD, HD)), _resident((1, DR)), _resident((NH, HD, HD)),
                  _resident((1, DR)), _resident((1, DR)), _resident((DR, D))],
        out_specs=[_rows(DP, tm), _rows(DP, tm), _rows(D, tm), _rows(DR, tm), _rows(DR, tm), _rows(D, tm),
                   pl.BlockSpec((KEPT, tm, DR), lambda i: (0, i, 0))],
        out_shape=[jax.ShapeDtypeStruct((S, DP), BF16), jax.ShapeDtypeStruct((S, DP), BF16),
                   jax.ShapeDtypeStruct((S, D), F32), jax.ShapeDtypeStruct((S, DR), F32),
                   jax.ShapeDtypeStruct((S, DR), BF16), jax.ShapeDtypeStruct((S, D), F32),
                   jax.ShapeDtypeStruct((KEPT, S, DR), F32)],
        scratch_shapes=[pltpu.VMEM((POOL_HALO, DP), F32), pltpu.VMEM((CONV_HALO, DR), F32), pltpu.VMEM((8, DR), F32)],
        operands=(proj, wg, scale, w_pool_out, conv_w, conv_b, wa, ba, wx, bx, lam, w_rnn_out),
        exchange=exchange, exchange_operands=exchange_operands)


FF_CHUNKS = ((0, 768), (768, 1536), (1536, 2304), (2304, DFF))


def _rms(x):
    r = lax.rsqrt(jnp.mean(x * x, axis=-1, keepdims=True) + EPS)
    return r, x * r


def _rms_bwd(dh, g, r, xh):
    dxh = dh * g
    return r * (dxh - xh * jnp.mean(dxh * xh, axis=-1, keepdims=True))


def _merge_out(x, proj, y_pool, y_rnn, w_o, norm_ffn, exchange=None, exchange_operands=(), tm=512):
    S = x.shape[0]
    GL0 = (DP + 2 * DR) // 512

    def gl_spec(k):
        return pl.BlockSpec((tm, 512), lambda i: (i, GL0 + k))

    def body(x_ref, gl0, gl1, gl2, gl3, yp_ref, yr_ref, wo_ref, gf_ref, mix_ref, x2_ref, h2_ref):
        s_p = _sigmoid(jnp.concatenate([gl0[...], gl1[...]], axis=1))
        s_r = _sigmoid(jnp.concatenate([gl2[...], gl3[...]], axis=1))
        mix = (s_p * yp_ref[...] + s_r * yr_ref[...]).astype(BF16)
        mix_ref[...] = mix
        x2 = x_ref[...] + _dot(mix, wo_ref[...])
        x2_ref[...] = x2
        _, xh2 = _rms(x2)
        h2_ref[...] = (xh2 * gf_ref[...]).astype(BF16)

    return _call(
        body, "merge_out", (S // tm,),
        in_specs=[_rows(D, tm), gl_spec(0), gl_spec(1), gl_spec(2), gl_spec(3), _rows(D, tm), _rows(D, tm),
                  _resident((D, D)), _resident((1, D))],
        out_specs=[_rows(D, tm), _rows(D, tm), _rows(D, tm)],
        out_shape=[jax.ShapeDtypeStruct((S, D), BF16), jax.ShapeDtypeStruct((S, D), F32), jax.ShapeDtypeStruct((S, D), BF16)],
        operands=(x, proj, proj, proj, proj, y_pool, y_rnn, w_o, norm_ffn),
        exchange=exchange, exchange_operands=exchange_operands)


def _ffn_up(h2, w_lo, w_hi, exchange=None, exchange_operands=(), tm=512):
    S = h2.shape[0]
    HALF = D // 2

    def body(h_ref, lo_ref, hi_ref, gu_ref, act_ref):
        h_lo, h_hi = h_ref[:, 0:HALF], h_ref[:, HALF:D]
        for c0, c1 in FF_CHUNKS:
            gate = _dot_nt(h_lo, lo_ref[c0:c1, :]) + _dot_nt(h_hi, hi_ref[c0:c1, :])
            up = _dot_nt(h_lo, lo_ref[DFF + c0:DFF + c1, :]) + _dot_nt(h_hi, hi_ref[DFF + c0:DFF + c1, :])
            gu_ref[:, c0:c1] = gate.astype(BF16)
            gu_ref[:, DFF + c0:DFF + c1] = up.astype(BF16)
            act_ref[:, c0:c1] = (gate * _sigmoid(gate) * up).astype(BF16)

    return _call(
        body, "ffn_up", (S // tm,),
        in_specs=[_rows(D, tm), _resident((2 * DFF, HALF)), _resident((2 * DFF, HALF))],
        out_specs=[_rows(2 * DFF, tm), _rows(DFF, tm)],
        out_shape=[jax.ShapeDtypeStruct((S, 2 * DFF), BF16), jax.ShapeDtypeStruct((S, DFF), BF16)],
        operands=(h2, w_lo, w_hi), exchange=exchange, exchange_operands=exchange_operands)


def _ffn_down_loss(act, x2, target, w_ffn_out, norm_final, tm=512):
    S = act.shape[0]

    def body(act_ref, x2_ref, t_ref, w_ref, gn_ref, dx3_ref, dx3b_ref, loss_ref, dvec_ref):
        i = pl.program_id(0)

        @pl.when(i == 0)
        def _():
            loss_ref[...] = jnp.zeros_like(loss_ref)
            dvec_ref[...] = jnp.zeros_like(dvec_ref)

        x3 = x2_ref[...] + _dot(act_ref[...], w_ref[...])
        r3, xh3 = _rms(x3)
        g_fin = gn_ref[...]
        e = xh3 * g_fin - t_ref[...]
        loss_ref[...] += jnp.sum(e * e, axis=(0, 1), keepdims=True) * (0.5 / D)
        dy = e * (1.0 / D)
        dvec_ref[0:1, :] += jnp.sum(dy * xh3, axis=0, keepdims=True)
        dx3 = _rms_bwd(dy, g_fin, r3, xh3)
        dx3_ref[...] = dx3
        dx3b_ref[...] = dx3.astype(BF16)

    return pl.pallas_call(
        body, name="ffn_down_loss", grid=(S // tm,),
        in_specs=[_rows(DFF, tm), _rows(D, tm), _rows(D, tm), _resident((DFF, D)), _resident((1, D))],
        out_specs=[_rows(D, tm), _rows(D, tm), _resident((1, 1)), _resident((8, D))],
        out_shape=[jax.ShapeDtypeStruct((S, D), F32), jax.ShapeDtypeStruct((S, D), BF16),
                   jax.ShapeDtypeStruct((1, 1), F32), jax.ShapeDtypeStruct((8, D), F32)],
        compiler_params=_params("arbitrary"),
    )(act, x2, target, w_ffn_out, norm_final)


def _ffn_bwd_down(dx3b, gu, w_ffn_out, tm=512):
    S = dx3b.shape[0]

    def body(d_ref, gu_ref, w_ref, dgu_ref):
        d = d_ref[...]
        for c0, c1 in FF_CHUNKS:
            dact = _dot_nt(d, w_ref[c0:c1, :])
            gate = gu_ref[:, c0:c1].astype(F32)
            up = gu_ref[:, DFF + c0:DFF + c1].astype(F32)
            sg = _sigmoid(gate)
            dgu_ref[:, c0:c1] = (dact * up * (sg * (1.0 + gate * (1.0 - sg)))).astype(BF16)
            dgu_ref[:, DFF + c0:DFF + c1] = (dact * (gate * sg)).astype(BF16)

    return pl.pallas_call(
        body, name="ffn_bwd_down", grid=(S // tm,),
        in_specs=[_rows(D, tm), _rows(2 * DFF, tm), _resident((DFF, D))],
        out_specs=_rows(2 * DFF, tm),
        out_shape=jax.ShapeDtypeStruct((S, 2 * DFF), BF16),
        compiler_params=_params("parallel"),
    )(dx3b, gu, w_ffn_out)


def _ffn_bwd_up(dgu, x2, dx3, w_lo, w_hi, norm_ffn, w_o, tm=512):
    S = dgu.shape[0]
    HALF = D // 2

    def body(dgu_ref, x2_ref, dx3_ref, lo_ref, hi_ref, gf_ref, wo_ref, dx2_ref, dx2b_ref, dmixo_ref, dvec_ref):
        i = pl.program_id(0)

        @pl.when(i == 0)
        def _():
            dvec_ref[...] = jnp.zeros_like(dvec_ref)

        dgate, dup = dgu_ref[:, 0:DFF], dgu_ref[:, DFF:2 * DFF]
        dh2 = jnp.concatenate([_dot(dgate, w[0:DFF, :]) + _dot(dup, w[DFF:2 * DFF, :]) for w in (lo_ref, hi_ref)], axis=1)
        r2, xh2 = _rms(x2_ref[...])
        dvec_ref[0:1, :] += jnp.sum(dh2 * xh2, axis=0, keepdims=True)
        dx2 = dx3_ref[...] + _rms_bwd(dh2, gf_ref[...], r2, xh2)
        dx2_ref[...] = dx2
        dx2b = dx2.astype(BF16)
        dx2b_ref[...] = dx2b
        dmixo_ref[...] = _dot_nt(dx2b, wo_ref[...])

    return pl.pallas_call(
        body, name="ffn_bwd_up", grid=(S // tm,),
        in_specs=[_rows(2 * DFF, tm), _rows(D, tm), _rows(D, tm), _resident((2 * DFF, HALF)), _resident((2 * DFF, HALF)),
                  _resident((1, D)), _resident((D, D))],
        out_specs=[_rows(D, tm), _rows(D, tm), _rows(D, tm), _resident((8, D))],
        out_shape=[jax.ShapeDtypeStruct((S, D), F32), jax.ShapeDtypeStruct((S, D), BF16), jax.ShapeDtypeStruct((S, D), F32),
                   jax.ShapeDtypeStruct((8, D), F32)],
        compiler_params=_params("arbitrary"),
    )(dgu, x2, dx3, w_lo, w_hi, norm_ffn, w_o)


VEC_ROWS = 16
MAT_WA = 4 * PG
MAT_WX = MAT_WA + NH * HD
MAT_ROWS = MAT_WX + NH * HD


def _mixer_bwd(proj, dmixo, y_pool, y_rnn, hr, kept, wg, scale, w_pool_out, conv_w, conv_b, wa, ba, wx, bx, lam, w_rnn_out,
               exchange=None, exchange_operands=(), tm=256):
    S = proj.shape[0]
    nt = S // tm

    def rev(cols):
        return pl.BlockSpec((tm, cols), lambda i: (nt - 1 - i, 0))

    def halo(rows_, cols):
        per = tm // rows_
        return pl.BlockSpec((rows_, cols), lambda i: (jnp.maximum((nt - 1 - i) * per - 1, 0), 0))

    def body(proj_ref, projh_ref, dmixo_ref, yp_ref, yr_ref, hr_ref, hrh_ref, kept_ref, wg_ref, scale_ref, wpo_ref, cw_ref, cb_ref,
             wa_ref, ba_ref, wx_ref, bx_ref, lam_ref, wro_ref,
             dproj_ref, dypb_ref, dyrb_ref, dmat_ref, dvec_ref,
             q_carry, dv_carry, a_carry, g_carry, g_scr):
        i = pl.program_id(0)
        ti = nt - 1 - i

        @pl.when(i == 0)
        def _():
            q_carry[...] = jnp.zeros_like(q_carry)
            dv_carry[...] = jnp.zeros_like(dv_carry)
            a_carry[...] = jnp.zeros_like(a_carry)
            g_carry[...] = jnp.zeros_like(g_carry)
            dmat_ref[...] = jnp.zeros_like(dmat_ref)
            dvec_ref[...] = jnp.zeros_like(dvec_ref)

        rows = lax.broadcasted_iota(jnp.int32, (tm, 1), 0)
        t_glob = ti * tm + rows
        has_prev = (ti > 0).astype(F32)
        dmixo = dmixo_ref[...]

        s_p = _sigmoid(proj_ref[:, DP + 2 * DR:DP + 2 * DR + D])
        s_r = _sigmoid(proj_ref[:, DP + 2 * DR + D:DIN])
        dproj_ref[:, DP + 2 * DR:DP + 2 * DR + D] = (dmixo * yp_ref[...] * s_p * (1.0 - s_p)).astype(BF16)
        dproj_ref[:, DP + 2 * DR + D:DIN] = (dmixo * yr_ref[...] * s_r * (1.0 - s_r)).astype(BF16)
        dyp = (dmixo * s_p).astype(BF16)
        dyr = (dmixo * s_r).astype(BF16)
        dypb_ref[...] = dyp
        dyrb_ref[...] = dyr

        dz = _dot_nt(dyr, wro_ref[...])
        u_gate = proj_ref[:, DP + DR:DP + 2 * DR]
        gg, dgelu = _gelu(u_gate, with_grad=True)
        hr_t = hr_ref[...]
        dproj_ref[:, DP + DR:DP + 2 * DR] = (dz * hr_t * dgelu).astype(BF16)
        dhr = dz * gg

        sp = _softplus_neg(lam_ref[...])
        v, r, gi, a, mult = (kept_ref[k] for k in range(KEPT))
        inv_mult = 1.0 / mult

        C = jnp.where(rows == tm - 1, a_carry[0:1, :], pltpu.roll(a, tm - 1, axis=0))
        g_carry[0:1, :] = _linear_scan(g_scr, C, dhr, g_carry[0:1, :], reverse=True)
        a_carry[0:1, :] = a[0:1, :]
        g = g_scr[...]

        h_prev = jnp.where(rows == 0, hrh_ref[7:8, :] * has_prev, pltpu.roll(hr_t, 1, axis=0))
        da = g * h_prev
        gm = g * mult
        dmult = g * gi * v
        di = gm * v
        dv = gm * gi
        dlog_a = da * a - dmult * (a * a * inv_mult)
        dvec_ref[4:5, :] += jnp.sum(dlog_a * r, axis=0, keepdims=True)
        dra = (dlog_a * ((-LRU_C) * sp) * r * (1.0 - r))
        drx = di * gi * (1.0 - gi)
        dvec_ref[2:3, :] += jnp.sum(dra, axis=0, keepdims=True)
        dvec_ref[3:4, :] += jnp.sum(drx, axis=0, keepdims=True)
        drab = dra.astype(BF16)
        drxb = drx.astype(BF16)
        vb = v.astype(BF16)
        dvg = []
        for h in range(NH):
            sl = slice(h * HD, (h + 1) * HD)
            dvg.append(_dot_nt(drab[:, sl], wa_ref[h]) + _dot_nt(drxb[:, sl], wx_ref[h]))
            dmat_ref[MAT_WA + h * HD:MAT_WA + (h + 1) * HD, :] += _dot_tn(vb[:, sl], drab[:, sl])
            dmat_ref[MAT_WX + h * HD:MAT_WX + (h + 1) * HD, :] += _dot_tn(vb[:, sl], drxb[:, sl])
        dv = dv + jnp.concatenate(dvg, axis=1)
        dvec_ref[1:2, :] += jnp.sum(dv, axis=0, keepdims=True)
        dvext = jnp.concatenate([dv, dv_carry[...]], axis=0)
        dv_carry[...] = dv[0:CONV_HALO, :]
        n = tm + CONV_HALO
        u_rnn = proj_ref[:, DP:DP + DR]
        du_rnn = dv * cw_ref[3:4, :]
        dvec_ref[8:9, :] += jnp.sum(dv * u_rnn, axis=0, keepdims=True)
        for k in range(3):
            dv_k = pltpu.roll(dvext, n - (3 - k), axis=0)[0:tm, :]
            du_rnn = du_rnn + dv_k * cw_ref[k:k + 1, :]
            dvec_ref[5 + k:6 + k, :] += jnp.sum(dv_k * u_rnn, axis=0, keepdims=True)
        dproj_ref[:, DP:DP + DR] = du_rnn.astype(BF16)

        dpm = _dot_nt(dyp, wpo_ref[...])
        u_pool = proj_ref[:, 0:DP]
        ext = jnp.concatenate([projh_ref[:, 0:DP] * has_prev, u_pool], axis=0)
        sums = _pool_windows(ext, +1)
        scale_v = scale_ref[...]
        qs = []
        dpooled = []
        dscale = []
        for gi_, w in enumerate(WINDOWS):
            sl = slice(gi_ * PG, (gi_ + 1) * PG)
            inv_cnt = 1.0 / jnp.minimum(t_glob + 1, w).astype(F32)
            pooled_b = (sums[gi_][POOL_HALO:, :] * inv_cnt - u_pool[:, sl]).astype(BF16)
            mixed_g = _dot(pooled_b, wg_ref[gi_])
            dscale.append(jnp.sum(dpm[:, sl] * mixed_g, axis=0, keepdims=True))
            dmixed_b = (dpm[:, sl] * scale_v[:, sl]).astype(BF16)
            dmat_ref[gi_ * PG:(gi_ + 1) * PG, :] += _dot_tn(pooled_b, dmixed_b)
            dp_g = _dot_nt(dmixed_b, wg_ref[gi_])
            dpooled.append(dp_g)
            qs.append(dp_g * inv_cnt)
        dvec_ref[0:1, 0:DP] += jnp.concatenate(dscale, axis=1)
        q = jnp.concatenate(qs, axis=1)
        qext = jnp.concatenate([q, q_carry[...]], axis=0)
        q_carry[...] = q[0:POOL_HALO, :]
        tsum = _pool_windows(qext, -1)
        for gi_ in range(4):
            dproj_ref[:, gi_ * PG:(gi_ + 1) * PG] = (tsum[gi_][0:tm, :] - dpooled[gi_]).astype(BF16)

        @pl.when(i == nt - 1)
        def _():
            dvec_ref[4:5, :] = dvec_ref[4:5, :] * (LRU_C * _sigmoid(-lam_ref[...]))

    return _call(
        body, "mixer_bwd", (nt,),
        in_specs=[rev(DIN), halo(POOL_HALO, DIN), rev(D), rev(D), rev(D), rev(DR), halo(8, DR),
                  pl.BlockSpec((KEPT, tm, DR), lambda i: (0, nt - 1 - i, 0)), _resident((4, PG, PG)), _resident((1, DP)), _resident((DP, D)), _resident((4, DR)), _resident((1, DR)),
                  _resident((NH, HD, HD)), _resident((1, DR)), _resident((NH, HD, HD)), _resident((1, DR)),
                  _resident((1, DR)), _resident((DR, D))],
        out_specs=[rev(DIN), rev(D), rev(D), _resident((MAT_ROWS, HD)), _resident((VEC_ROWS, DR))],
        out_shape=[jax.ShapeDtypeStruct((S, DIN), BF16), jax.ShapeDtypeStruct((S, D), BF16),
                   jax.ShapeDtypeStruct((S, D), BF16), jax.ShapeDtypeStruct((MAT_ROWS, HD), F32),
                   jax.ShapeDtypeStruct((VEC_ROWS, DR), F32)],
        scratch_shapes=[pltpu.VMEM((POOL_HALO, DP), F32), pltpu.VMEM((CONV_HALO, DR), F32), pltpu.VMEM((8, DR), F32),
                        pltpu.VMEM((8, DR), F32), pltpu.VMEM((tm, DR), F32)],
        operands=(proj, proj, dmixo, y_pool, y_rnn, hr, hr, kept, wg, scale, w_pool_out, conv_w, conv_b, wa, ba, wx, bx, lam,
                  w_rnn_out),
        exchange=exchange, exchange_operands=exchange_operands)


def _in_bwd(dproj, x, dx2, norm_mix, w_in, exchange=None, exchange_operands=(), tm=512):
    S = x.shape[0]

    def body(dp_ref, x_ref, dx2_ref, g_ref, w_ref, dx_ref, dg_ref):
        i = pl.program_id(0)

        @pl.when(i == 0)
        def _():
            dg_ref[...] = jnp.zeros_like(dg_ref)

        dh = _dot(dp_ref[:, 0:1536], w_ref[0:1536, :])
        dh = dh + _dot(dp_ref[:, 1536:3072], w_ref[1536:3072, :])
        dh = dh + _dot(dp_ref[:, 3072:DIN], w_ref[3072:DIN, :])
        xv = x_ref[...]
        r = lax.rsqrt(jnp.mean(xv * xv, axis=-1, keepdims=True) + EPS)
        xh = xv * r
        dg_ref[0:1, :] += jnp.sum(dh * xh, axis=0, keepdims=True)
        dxh = dh * g_ref[...]
        dx_ref[...] = dx2_ref[...] + r * (dxh - xh * jnp.mean(dxh * xh, axis=-1, keepdims=True))

    return _call(
        body, "in_bwd", (S // tm,),
        in_specs=[_rows(DIN, tm), _rows(D, tm), _rows(D, tm), _resident((1, D)), _resident((DIN, D))],
        out_specs=[_rows(D, tm), _resident((8, D))],
        out_shape=[jax.ShapeDtypeStruct((S, D), F32), jax.ShapeDtypeStruct((8, D), F32)],
        operands=(dproj, x, dx2, norm_mix, w_in), exchange=exchange, exchange_operands=exchange_operands)


def _wgrad(a, b, name, tk, tn, exchange=None, exchange_operands=()):
    S, K = a.shape
    N = b.shape[1]

    def body(a_ref, b_ref, o_ref):
        o_ref[...] = _dot_tn(a_ref[...], b_ref[...]).astype(BF16)

    (out,), exchanged = _call(
        body, name, (K // tk, N // tn),
        in_specs=[pl.BlockSpec((S, tk), lambda k, n: (0, k)), pl.BlockSpec((S, tn), lambda k, n: (0, n))],
        out_specs=[pl.BlockSpec((tk, tn), lambda k, n: (k, n))],
        out_shape=[jax.ShapeDtypeStruct((K, N), BF16)],
        operands=(a, b), exchange=exchange, exchange_operands=exchange_operands)
    return (out, exchanged) if exchange is not None else out


VEC_SCALE, VEC_CONV_B, VEC_BA, VEC_BX, VEC_LAM, VEC_CONV_W, VEC_NORM_FINAL, VEC_NORM_FFN = 0, 1, 2, 3, 4, 5, 9, 10
VEC_LOSS = 11


class _Big:
    def __init__(self, name, rows, cols, axis, n, dtype=BF16, transposed=False, src_cols=None):
        self.name, self.rows, self.cols, self.axis, self.n, self.dtype = name, rows, cols, axis, n, dtype
        self.transposed = transposed
        self.src_cols = src_cols
        self.block_shape = (rows, n) if axis == 1 else (n, cols)

    def block(self, ref, p):
        if self.axis == 1:
            return ref.at[:, pl.ds(pl.multiple_of(p * self.n, 128), self.n)]
        return ref.at[pl.ds(pl.multiple_of(p * self.n, 16 if self.dtype == BF16 else 8), self.n), :]

    def block_index(self, p):
        return (0, p) if self.axis == 1 else (p, 0)


BIG = (_Big("w_in", DIN, D, 0, DIN // 8, transposed=True), _Big("w_pool_out", DP, D, 1, D // 8),
       _Big("w_rnn_out", DR, D, 0, DR // 8), _Big("w_o", D, D, 0, D // 8),
       _Big("w_ffn_in", 2 * DFF, D, 0, 2 * DFF // 8, transposed=True), _Big("w_ffn_out", DFF, D, 0, DFF // 8))
CONV_W = _Big("conv_w", 8, DR, 1, DR // 8, F32)
W_FFN_IN_HALVES = (_Big("w_ffn_in_lo", 2 * DFF, D // 2, 0, 2 * DFF // 8, src_cols=(0, D // 2)),
                   _Big("w_ffn_in_hi", 2 * DFF, D // 2, 0, 2 * DFF // 8, src_cols=(D // 2, D)))
GATHERED = BIG + (CONV_W,) + W_FFN_IN_HALVES

HBM_SPEC = pl.BlockSpec(memory_space=pl.ANY)
VMEM_SPEC = pl.BlockSpec(memory_space=pltpu.VMEM)


def _place():
    x, y, c = (lax.axis_index(a) for a in MESH_AXES)
    other_chips = [(1 - x, y), (x, 1 - y), (1 - x, 1 - y)]
    return x, y, c, other_chips


def _remote(src, dst, send_sems, recv_sems, idx, to):
    return pltpu.make_async_remote_copy(src_ref=src, dst_ref=dst, send_sem=send_sems.at[idx], recv_sem=recv_sems.at[idx],
                                        device_id=to, device_id_type=MESH)


def _device_index(chip, core):
    return 4 * chip[0] + 2 * chip[1] + core


class _Gather:
    def __init__(self, tensors):
        self.tensors = tuple(tensors)
        n = len(self.tensors)
        self.in_specs = [HBM_SPEC] * n
        self.out_specs = [HBM_SPEC] * n
        self.out_shape = [jax.ShapeDtypeStruct((T.rows, T.cols), T.dtype) for T in self.tensors]
        self.scratch_shapes = [pltpu.VMEM(T.block_shape, T.dtype) for T in self.tensors] + [
            pltpu.VMEM(T.block_shape, F32) for T in self.tensors] + [
            pltpu.SemaphoreType.DMA((n, 7)), pltpu.SemaphoreType.DMA((n, 7)), pltpu.SemaphoreType.DMA((n, 2))]

    def middles(self, steps):
        return [(steps - 1, self.middle)]

    def _copies(self, ins, outs, scratch):
        n = len(self.tensors)
        mine, raw, (send_sems, recv_sems, loc_sems) = scratch[:n], scratch[n:2 * n], scratch[2 * n:]
        x, y, c, chips = _place()
        sibling = (x, y, 1 - c)
        me = _device_index((x, y), c)
        loads, stores, first, passed, arrivals, late = [], [], [], [], [], []
        for t, T in enumerate(self.tensors):
            place = T.block(outs[t], me)
            src = ins[t] if T.src_cols is None else ins[t].at[:, T.src_cols[0]:T.src_cols[1]]
            loads.append(pltpu.make_async_copy(src, raw[t], loc_sems.at[t, 0]))
            stores.append(pltpu.make_async_copy(mine[t], place, loc_sems.at[t, 1]))
            first.append(_remote(mine[t], place, send_sems, recv_sems, (t, 0), sibling))
            theirs = T.block(outs[t], _device_index((x, y), 1 - c))
            late.append(_remote(theirs, theirs, send_sems, recv_sems, (t, 0), sibling))
            for k, chip in enumerate(chips):
                first.append(_remote(mine[t], place, send_sems, recv_sems, (t, 1 + k), (*chip, c)))
                land = T.block(outs[t], _device_index(chip, c))
                arrivals.append(_remote(land, land, send_sems, recv_sems, (t, 1 + k), sibling))
                passed.append(_remote(land, land, send_sems, recv_sems, (t, 4 + k), sibling))
                theirs = T.block(outs[t], _device_index(chip, 1 - c))
                late.append(_remote(theirs, theirs, send_sems, recv_sems, (t, 4 + k), sibling))
        return loads, stores, first, passed, arrivals, late

    def start(self, ins, outs, scratch):
        loads, stores, first, _, _, _ = self._copies(ins, outs, scratch)
        n = len(self.tensors)
        for cp in loads:
            cp.start()
        for t, cp in enumerate(loads):
            cp.wait()
            scratch[t][...] = scratch[n + t][...].astype(self.tensors[t].dtype)
        for cp in stores + first:
            cp.start()

    def middle(self, ins, outs, scratch, skip=0):
        _, _, _, passed, arrivals, _ = self._copies(ins, outs, scratch)
        for arrived, cp in zip(arrivals[3 * skip:], passed[3 * skip:]):
            arrived.wait_recv()
            cp.start()

    def finish(self, ins, outs, scratch, skip=0):
        _, stores, first, passed, _, late = self._copies(ins, outs, scratch)
        for cp in late[4 * skip:]:
            cp.wait_recv()
        for cp in first + passed:
            cp.wait_send()
        for cp in stores[skip:]:
            cp.wait()


def _in_proj_gather(x, norm_mix, blocks, tensors, order, tm=512):
    S = x.shape[0]
    nt = S // tm
    n = len(tensors)
    gather = _Gather(tensors)
    CB = 2 * tensors[0].n

    def body(order_ref, x_ref, g_ref, *refs):
        ins, (proj_ref, h_ref), outs = refs[:n], refs[n:n + 2], refs[n + 2:2 * n + 2]
        (h_all, w_chip, w_sem), scratch = refs[2 * n + 2:2 * n + 5], refs[2 * n + 5:]
        q, i = pl.program_id(0), pl.program_id(1)
        _, stores, first, passed, arrivals, late = gather._copies(ins, outs, scratch)

        def fetch(turn):
            rows = outs[0].at[pl.ds(pl.multiple_of(order_ref[turn] * CB, 16), CB), :]
            return pltpu.make_async_copy(rows, w_chip.at[turn % 2], w_sem.at[turn % 2])

        @pl.when((q == 0) & (i == 0))
        def _():
            gather.start(ins, outs, scratch)
            late[0].wait_recv()
            stores[0].wait()
            fetch(0).start()
            fetch(0).wait()

        for k in range(3):
            @pl.when((q == k) & (i == nt // 2))
            def _(k=k):
                arrivals[k].wait_recv()
                passed[k].start()
                late[1 + k].wait_recv()
                fetch(k + 1).start()

            @pl.when((q == k + 1) & (i == 0))
            def _(k=k):
                fetch(k + 1).wait()

        rows = pl.ds(pl.multiple_of(i * tm, tm), tm)

        @pl.when(q == 0)
        def _():
            xv = x_ref[...]
            r = lax.rsqrt(jnp.mean(xv * xv, axis=-1, keepdims=True) + EPS)
            h = (xv * r * g_ref[...]).astype(BF16)
            h_all[rows, :] = h
            h_ref[...] = h

        proj_ref[...] = _dot_nt(h_all[rows, :], w_chip[q % 2])

        @pl.when((q == 3) & (i == nt - 1))
        def _():
            gather.middle(ins, outs, scratch, skip=1)
            gather.finish(ins, outs, scratch, skip=1)

    row_tile = lambda q, i, order: (jnp.where(q == 0, i, nt - 1), 0)
    whole = lambda shape: pl.BlockSpec(shape, lambda q, i, order: (0,) * len(shape), pipeline_mode=pl.Buffered(1))
    outs = pl.pallas_call(
        body, name="in_proj_gather",
        grid_spec=pltpu.PrefetchScalarGridSpec(
            num_scalar_prefetch=1, grid=(4, nt),
            in_specs=[pl.BlockSpec((tm, D), row_tile), whole((1, D))] + gather.in_specs,
            out_specs=[pl.BlockSpec((tm, CB), lambda q, i, order: (i, order[q])), pl.BlockSpec((tm, D), row_tile)]
            + gather.out_specs,
            scratch_shapes=[pltpu.VMEM((S, D), BF16), pltpu.VMEM((2, CB, D), BF16), pltpu.SemaphoreType.DMA((2,))]
            + gather.scratch_shapes),
        out_shape=[jax.ShapeDtypeStruct((S, DIN), F32), jax.ShapeDtypeStruct((S, D), BF16)] + gather.out_shape,
        compiler_params=_params("arbitrary", "arbitrary"),
    )(order, x, norm_mix, *blocks)
    return outs[:2], outs[2:]


PAIR_ROWS = 32


def _pair_reduce(grads, tensors, name):
    nt = len(tensors)

    def body(*refs):
        ins, own_out, sums_out, landed, mine = (refs[k * nt:(k + 1) * nt] for k in range(5))
        send_sems, recv_sems, loc_sems = refs[5 * nt:]
        x, y, c, chips = _place()
        chip_of = [2 * chip[0] + chip[1] for chip in chips]
        swaps, loads = [], []
        for t, T in enumerate(tensors):
            for j in range(4):
                swaps.append(_remote(T.block(ins[t], 2 * j + 1 - c), landed[t].at[j], send_sems, recv_sems, (t, j),
                                     (x, y, 1 - c)))
            for k in range(3):
                loads.append(pltpu.make_async_copy(T.block(ins[t], 2 * chip_of[k] + c), mine[t].at[k], loc_sems.at[t, k]))
        for cp in swaps + loads:
            cp.start()
        for cp in loads:
            cp.wait()
        for cp in swaps:
            cp.wait_recv()
        stores = []
        for t, T in enumerate(tensors):
            for k in range(3):
                acc, got = mine[t].at[k], landed[t].at[chip_of[k]]

                def add(i, carry, acc=acc, got=got):
                    rows = pl.ds(pl.multiple_of(i * PAIR_ROWS, PAIR_ROWS), PAIR_ROWS)
                    acc[rows, :] = (acc[rows, :].astype(F32) + got[rows, :].astype(F32)).astype(BF16)
                    return carry

                lax.fori_loop(0, T.block_shape[0] // PAIR_ROWS, add, 0)
            stores.append(pltpu.make_async_copy(mine[t], sums_out[t], loc_sems.at[t, 3]))
            stores.append(pltpu.make_async_copy(landed[t].at[2 * x + y], own_out[t], loc_sems.at[t, 4]))
        for cp in stores:
            cp.start()
        for cp in swaps:
            cp.wait_send()
        for cp in stores:
            cp.wait()

    blocks = [T.block_shape for T in tensors]
    return pl.pallas_call(
        body, name=name,
        in_specs=[HBM_SPEC] * nt, out_specs=[HBM_SPEC] * (2 * nt),
        out_shape=[jax.ShapeDtypeStruct(b, BF16) for b in blocks] + [jax.ShapeDtypeStruct((3,) + b, BF16) for b in blocks],
        scratch_shapes=[pltpu.VMEM((4,) + b, BF16) for b in blocks] + [pltpu.VMEM((3,) + b, BF16) for b in blocks]
        + [pltpu.SemaphoreType.DMA((nt, 4)), pltpu.SemaphoreType.DMA((nt, 4)), pltpu.SemaphoreType.DMA((nt, 5))],
        compiler_params=pltpu.CompilerParams(vmem_limit_bytes=VMEM_LIMIT),
    )(*grads)


class _Scatter:
    def middles(self, steps):
        return []

    def __init__(self, tensors):
        n = len(tensors)
        self.in_specs = [HBM_SPEC] * n
        self.out_specs = [HBM_SPEC] * n
        self.out_shape = [jax.ShapeDtypeStruct((3,) + T.block_shape, BF16) for T in tensors]
        self.scratch_shapes = [pltpu.SemaphoreType.DMA((n, 3)), pltpu.SemaphoreType.DMA((n, 3))]

    def _copies(self, ins, outs, scratch):
        send_sems, recv_sems = scratch
        x, y, c, chips = _place()
        return [_remote(ins[t].at[k], outs[t].at[k], send_sems, recv_sems, (t, k), (*chip, c))
                for t in range(len(ins)) for k, chip in enumerate(chips)]

    def start(self, ins, outs, scratch):
        for cp in self._copies(ins, outs, scratch):
            cp.start()

    def finish(self, ins, outs, scratch):
        for cp in self._copies(ins, outs, scratch):
            cp.wait()


def _chip_scatter(sums, tensors, name):
    scatter = _Scatter(tensors)
    n = len(tensors)

    def body(*refs):
        ins, outs, scratch = refs[:n], refs[n:2 * n], refs[2 * n:]
        scatter.start(ins, outs, scratch)
        scatter.finish(ins, outs, scratch)

    return pl.pallas_call(
        body, name=name, in_specs=scatter.in_specs, out_specs=scatter.out_specs, out_shape=scatter.out_shape,
        scratch_shapes=scatter.scratch_shapes,
    )(*sums)


def _adamw(w, g, m, v):
    m = ADAM_B1 * m + (1.0 - ADAM_B1) * g
    v = ADAM_B2 * v + (1.0 - ADAM_B2) * (g * g)
    m_hat = m / (1.0 - ADAM_B1 ** ADAM_STEP)
    v_hat = v / (1.0 - ADAM_B2 ** ADAM_STEP)
    delta = -ADAM_LR * (m_hat / (jnp.sqrt(v_hat) + ADAM_EPS) + ADAM_WD * w)
    return delta, m, v


def _final_sum(T, g, lz1, lz2, where, w, m, v):
    rows, cols = T.block_shape
    sub = 4 if T.axis == 0 and rows % 64 == 0 and rows > 256 else 1
    blk = (rows // sub, cols)

    def body(where_ref, g_ref, l1_ref, l2_ref, w_ref, m_ref, v_ref, g_out, d_out, m_out, v_out):
        tot = g_ref[...].astype(F32) + l1_ref[...].astype(F32)
        for k in range(3):
            tot = tot + l2_ref[k].astype(F32)
        g_out[...] = tot
        d_out[...], m_out[...], v_out[...] = _adamw(w_ref[...], tot, m_ref[...], v_ref[...])

    def in_whole(r, wh):
        p = wh[0]
        return (0, p) if T.axis == 1 else (p * sub + r, 0)

    own = pl.BlockSpec(blk, lambda r, wh: (r, 0))
    return pl.pallas_call(
        body, name="grad_final_" + T.name,
        grid_spec=pltpu.PrefetchScalarGridSpec(
            num_scalar_prefetch=1, grid=(sub,),
            in_specs=[pl.BlockSpec(blk, in_whole),
                      own,
                      pl.BlockSpec((3,) + blk, lambda r, wh: (0, r, 0)), own, own, own],
            out_specs=[own] * 4),
        out_shape=[jax.ShapeDtypeStruct(T.block_shape, F32)] * 4,
        compiler_params=_params("arbitrary"),
    )(where, g, lz1, lz2, w, m, v)


MAT_PIECE = MAT_ROWS // 8
VEC_PIECE = DR // 8


class _AllReduce:
    def __init__(self, items):
        self.items = tuple(items)
        n = len(self.items)
        self.in_specs = [HBM_SPEC] * n
        self.out_specs = [HBM_SPEC] * n
        self.out_shape = [jax.ShapeDtypeStruct(shape, F32) for shape, _ in self.items]
        pieces = [(shape[0] // 8, shape[1]) if axis == 0 else (shape[0], shape[1] // 8) for shape, axis in self.items]
        self.scratch_shapes = ([pltpu.VMEM((8,) + p, F32) for p in pieces] + [pltpu.VMEM(p, F32) for p in pieces] + [
            pltpu.SemaphoreType.DMA((2 * n, 8)), pltpu.SemaphoreType.DMA((2 * n, 8)), pltpu.SemaphoreType.DMA((2 * n,))])

    def middles(self, steps):
        return [(steps // 2, self.middle)]

    def _copies(self, ins, outs, scratch):
        n = len(self.items)
        landed, sums, (send_sems, recv_sems, loc_sems) = scratch[:n], scratch[n:2 * n], scratch[2 * n:]
        x, y, c, _ = _place()
        me = _device_index((x, y), c)

        def peer(r):
            return (1 - x if r & 4 else x, 1 - y if r & 2 else y, 1 - c if r & 1 else c)

        def piece(i, ref, p):
            shape, axis = self.items[i]
            if axis == 0:
                rows = shape[0] // 8
                return ref.at[pl.ds(pl.multiple_of(p * rows, 8), rows), :]
            cols = shape[1] // 8
            return ref.at[:, pl.ds(pl.multiple_of(p * cols, 128), cols)]

        own, scatter, arrivals, keep, spread, late = [], [], [], [], [], []
        for i in range(n):
            own.append(pltpu.make_async_copy(piece(i, ins[i], me), landed[i].at[0], loc_sems.at[2 * i]))
            keep.append(pltpu.make_async_copy(sums[i], piece(i, outs[i], me), loc_sems.at[2 * i + 1]))
            for r in range(1, 8):
                to = peer(r)
                p = _device_index(to[:2], to[2])
                scatter.append(_remote(piece(i, ins[i], p), landed[i].at[r], send_sems, recv_sems, (2 * i, r), to))
                spread.append(_remote(sums[i], piece(i, outs[i], me), send_sems, recv_sems, (2 * i + 1, r), to))
                late.append(_remote(sums[i], piece(i, outs[i], p), send_sems, recv_sems, (2 * i + 1, r), to))
        return own, scatter, keep, spread, late, landed, sums

    def start(self, ins, outs, scratch):
        own, scatter, _, _, _, _, _ = self._copies(ins, outs, scratch)
        for cp in own + scatter:
            cp.start()

    def middle(self, ins, outs, scratch):
        own, scatter, keep, spread, _, landed, sums = self._copies(ins, outs, scratch)
        for cp in own:
            cp.wait()
        for cp in scatter:
            cp.wait_recv()
        for i in range(len(self.items)):
            total = landed[i][0]
            for r in range(1, 8):
                total = total + landed[i][r]
            sums[i][...] = total
        for cp in keep + spread:
            cp.start()

    def finish(self, ins, outs, scratch):
        _, scatter, keep, spread, late, _, _ = self._copies(ins, outs, scratch)
        for cp in late:
            cp.wait_recv()
        for cp in scatter + spread:
            cp.wait_send()
        for cp in keep:
            cp.wait()


class _Both:
    def __init__(self, a, b):
        self.a, self.b = a, b
        self.in_specs, self.out_specs = a.in_specs + b.in_specs, a.out_specs + b.out_specs
        self.out_shape, self.scratch_shapes = a.out_shape + b.out_shape, a.scratch_shapes + b.scratch_shapes

    def _each(self, ins, outs, scratch):
        a = self.a
        i, o, s = len(a.in_specs), len(a.out_specs), len(a.scratch_shapes)
        return (a, ins[:i], outs[:o], scratch[:s]), (self.b, ins[i:], outs[o:], scratch[s:])

    def middles(self, steps):
        def of(which, middle):
            return lambda ins, outs, scratch: middle(*self._each(ins, outs, scratch)[which][1:])
        return [(at, of(which, middle)) for which, e in enumerate((self.a, self.b)) for at, middle in e.middles(steps)]

    def start(self, ins, outs, scratch):
        for e, i, o, s in self._each(ins, outs, scratch):
            e.start(i, o, s)

    def finish(self, ins, outs, scratch):
        for e, i, o, s in self._each(ins, outs, scratch):
            e.finish(i, o, s)


def _all_reduce(arrays, items, name):
    reduce = _AllReduce(items)
    n = len(items)

    def body(*refs):
        ins, outs, scratch = refs[:n], refs[n:2 * n], refs[2 * n:]
        reduce.start(ins, outs, scratch)
        reduce.middle(ins, outs, scratch)
        reduce.finish(ins, outs, scratch)

    return pl.pallas_call(
        body, name=name, in_specs=reduce.in_specs, out_specs=reduce.out_specs, out_shape=reduce.out_shape,
        scratch_shapes=reduce.scratch_shapes,
    )(*arrays)


def _adam_small(grads, wmv):
    n = len(grads)

    def body(*refs):
        g_refs, rest = refs[:n], refs[n:]
        ins, outs = rest[:3 * n], rest[3 * n:]
        for i in range(n):
            d, m, v = _adamw(ins[3 * i][...], g_refs[i][...], ins[3 * i + 1][...], ins[3 * i + 2][...])
            outs[3 * i][...], outs[3 * i + 1][...], outs[3 * i + 2][...] = d, m, v

    flat = [a for t in wmv for a in t]
    return pl.pallas_call(
        body, name="adam_small",
        in_specs=[VMEM_SPEC] * (4 * n), out_specs=[VMEM_SPEC] * (3 * n),
        out_shape=[jax.ShapeDtypeStruct(a.shape, F32) for a in flat],
    )(*grads, *flat)


WEIGHT_NAMES = ("norm_mix", "w_in", "w_pool_grp", "pool_scale", "w_pool_out", "conv_w", "conv_b", "w_rg_a", "b_rg_a", "w_rg_x",
                "b_rg_x", "lru_lambda", "w_rnn_out", "w_o", "norm_ffn", "w_ffn_in", "w_ffn_out", "norm_final")


def kernel(x, norm_mix, w_in, w_pool_grp, pool_scale, w_pool_out, conv_w, conv_b, w_rg_a, b_rg_a, w_rg_x, b_rg_x, lru_lambda, w_rnn_out, w_o, norm_ffn, w_ffn_in, w_ffn_out, norm_final, loss_target, m_norm_mix, m_w_in, m_w_pool_grp, m_pool_scale, m_w_pool_out, m_conv_w, m_conv_b, m_w_rg_a, m_b_rg_a, m_w_rg_x, m_b_rg_x, m_lru_lambda, m_w_rnn_out, m_w_o, m_norm_ffn, m_w_ffn_in, m_w_ffn_out, m_norm_final, v_norm_mix, v_w_in, v_w_pool_grp, v_pool_scale, v_w_pool_out, v_conv_w, v_conv_b, v_w_rg_a, v_b_rg_a, v_w_rg_x, v_b_rg_x, v_lru_lambda, v_w_rnn_out, v_w_o, v_norm_ffn, v_w_ffn_in, v_w_ffn_out, v_norm_final):
    w = dict(norm_mix=norm_mix, w_in=w_in, w_pool_grp=w_pool_grp, pool_scale=pool_scale, w_pool_out=w_pool_out, conv_w=conv_w,
             conv_b=conv_b, w_rg_a=w_rg_a, b_rg_a=b_rg_a, w_rg_x=w_rg_x, b_rg_x=b_rg_x, lru_lambda=lru_lambda,
             w_rnn_out=w_rnn_out, w_o=w_o, norm_ffn=norm_ffn, w_ffn_in=w_ffn_in, w_ffn_out=w_ffn_out, norm_final=norm_final)
    m = dict(norm_mix=m_norm_mix, w_in=m_w_in, w_pool_grp=m_w_pool_grp, pool_scale=m_pool_scale, w_pool_out=m_w_pool_out,
             conv_w=m_conv_w, conv_b=m_conv_b, w_rg_a=m_w_rg_a, b_rg_a=m_b_rg_a, w_rg_x=m_w_rg_x, b_rg_x=m_b_rg_x,
             lru_lambda=m_lru_lambda, w_rnn_out=m_w_rnn_out, w_o=m_w_o, norm_ffn=m_norm_ffn, w_ffn_in=m_w_ffn_in,
             w_ffn_out=m_w_ffn_out, norm_final=m_norm_final)
    v = dict(norm_mix=v_norm_mix, w_in=v_w_in, w_pool_grp=v_w_pool_grp, pool_scale=v_pool_scale, w_pool_out=v_w_pool_out,
             conv_w=v_conv_w, conv_b=v_conv_b, w_rg_a=v_w_rg_a, b_rg_a=v_b_rg_a, w_rg_x=v_w_rg_x, b_rg_x=v_b_rg_x,
             lru_lambda=v_lru_lambda, w_rnn_out=v_w_rnn_out, w_o=v_w_o, norm_ffn=v_norm_ffn, w_ffn_in=v_w_ffn_in,
             w_ffn_out=v_w_ffn_out, norm_final=v_norm_final)
    xi, yi, ci = (lax.axis_index(a) for a in MESH_AXES)
    chip = 2 * xi + yi

    def held(T, a):
        return jnp.swapaxes(a, 0, 1) if T.transposed else a

    where = jnp.stack([2 * chip + ci]).astype(jnp.int32)
    by_name = {T.name: T for T in GATHERED}
    block = {T.name: held(T, w[T.name][0]) for T in BIG}
    block["conv_w"] = jnp.pad(conv_w[0], ((0, CONV_W.rows - 4), (0, 0)))
    block["w_ffn_in_lo"] = block["w_ffn_in_hi"] = block["w_ffn_in"]

    def gather_of(*names):
        return dict(exchange=_Gather([by_name[n] for n in names]), exchange_operands=[block[n] for n in names])

    def pair_sums(names, partials, tag):
        out = _pair_reduce(partials, [by_name[n] for n in names], "grad_pair_reduce_" + tag)
        return list(out[:len(names)]), list(out[len(names):])

    xs, target = x[0], loss_target[0]
    wg_b, wa_b, wx_b = (a[0].astype(BF16) for a in (w_pool_grp, w_rg_a, w_rg_x))
    ba2, bx2 = b_rg_a.reshape(1, DR), b_rg_x.reshape(1, DR)
    first = ("w_in", "w_pool_out", "w_rnn_out", "conv_w")
    order = jnp.stack([chip, 2 * (1 - xi) + yi, 2 * xi + (1 - yi), 2 * (1 - xi) + (1 - yi)]).astype(jnp.int32)
    (proj, h1), (w_in_g, w_pool_out_g, w_rnn_out_g, conv_g) = _in_proj_gather(
        xs, norm_mix, [block[n] for n in first], [by_name[n] for n in first], order)
    mixer_weights = (wg_b, pool_scale, w_pool_out_g, conv_g[0:4], conv_b, wa_b, ba2, wx_b, bx2, lru_lambda, w_rnn_out_g)
    (_, pm, y_pool, hr, z, y_rnn, kept), (w_o_g, w_ffn_hi_g) = _mixer_fwd(
        proj, *mixer_weights, **gather_of("w_o", "w_ffn_in_hi"))
    (mix, x2, h2), (w_ffn_lo_g,) = _merge_out(xs, proj, y_pool, y_rnn, w_o_g, norm_ffn, **gather_of("w_ffn_in_lo"))
    (gu, act), (w_ffn_out_g,) = _ffn_up(h2, w_ffn_lo_g, w_ffn_hi_g, **gather_of("w_ffn_out"))
    dx3, dx3b, loss_part, dvec_fin = _ffn_down_loss(act, x2, target, w_ffn_out_g, norm_final.reshape(1, D))

    dgu = _ffn_bwd_down(dx3b, gu, w_ffn_out_g)
    dx2, dx2b, dmixo, dvec_ffn = _ffn_bwd_up(dgu, x2, dx3, w_ffn_lo_g, w_ffn_hi_g, norm_ffn, w_o_g)
    names_a = ("w_ffn_in", "w_ffn_out", "w_o")
    part_a = [_wgrad(dgu, h2, "wgrad_ffn_in", 1408, 512), _wgrad(act, dx3b, "wgrad_ffn_out", 1408, 512),
              _wgrad(mix, dx2b, "wgrad_o", 1024, 1024)]
    lz1_a, sums_a = pair_sums(names_a, part_a, "ffn")
    (dproj, dypb, dyrb, dmat, dvec_mix), lz2_a = _mixer_bwd(
        proj, dmixo, y_pool, y_rnn, hr, kept, *mixer_weights,
        exchange=_Scatter([by_name[n] for n in names_a]), exchange_operands=sums_a)
    names_b = ("w_pool_out", "w_rnn_out")
    part_b = [_wgrad(pm, dypb, "wgrad_pool_out", 512, 1024), _wgrad(z, dyrb, "wgrad_rnn_out", 1024, 1024)]
    lz1_b, sums_b = pair_sums(names_b, part_b, "mix")
    dvec = jnp.concatenate([dvec_mix[0:9], dvec_fin[0:1], dvec_ffn[0:1], jnp.pad(loss_part, ((0, 0), (0, DR - 1))),
                            jnp.zeros((VEC_ROWS - 12, DR), F32)], axis=0)
    g_in, exchanged = _wgrad(
        dproj, h1, "wgrad_in", 1152, 1024,
        exchange=_Both(_Scatter([by_name[n] for n in names_b]), _AllReduce([((MAT_ROWS, HD), 0), ((VEC_ROWS, DR), 1)])),
        exchange_operands=sums_b + [dmat, dvec])
    lz2_b, (mat, vec) = exchanged[:2], exchanged[2:]
    loss = vec[VEC_LOSS, 0]
    lz1_c, sums_c = pair_sums(("w_in",), [g_in], "in")
    (grad_x, dvec_in), lz2_c = _in_bwd(dproj, xs, dx2, norm_mix, w_in_g,
                                       exchange=_Scatter([by_name["w_in"]]), exchange_operands=sums_c)
    (vec_in,) = _all_reduce([dvec_in], [((8, D), 1)], "all_reduce_norm_mix")

    grads, delta, new_m, new_v = {}, {}, {}, {}
    for n, g, l1, l2 in zip(names_a + names_b + ("w_in",), part_a + part_b + [g_in], lz1_a + lz1_b + lz1_c,
                            lz2_a + lz2_b + lz2_c):
        T = by_name[n]
        out = _final_sum(T, g, l1, l2, where, held(T, w[n][0]), held(T, m[n][0]), held(T, v[n][0]))
        grads[n], delta[n], new_m[n], new_v[n] = (held(T, a) for a in out)
    me = 4 * xi + 2 * yi + ci
    small_grads = dict(
        w_pool_grp=mat[0:MAT_WA], w_rg_a=mat[MAT_WA:MAT_WX], w_rg_x=mat[MAT_WX:MAT_ROWS],
        pool_scale=vec[VEC_SCALE:VEC_SCALE + 1, 0:DP], conv_b=vec[VEC_CONV_B:VEC_CONV_B + 1],
        b_rg_a=vec[VEC_BA:VEC_BA + 1], b_rg_x=vec[VEC_BX:VEC_BX + 1], lru_lambda=vec[VEC_LAM:VEC_LAM + 1],
        conv_w=lax.dynamic_slice(vec, (VEC_CONV_W, VEC_PIECE * me), (4, VEC_PIECE)),
        norm_final=vec[VEC_NORM_FINAL:VEC_NORM_FINAL + 1], norm_ffn=vec[VEC_NORM_FFN:VEC_NORM_FFN + 1],
        norm_mix=vec_in[0:1])
    names = list(small_grads)
    as2d = lambda a, g: a.reshape(g.shape)
    upd = _adam_small([small_grads[n] for n in names],
                      [(as2d(w[n], small_grads[n]), as2d(m[n], small_grads[n]), as2d(v[n], small_grads[n])) for n in names])
    for i, n in enumerate(names):
        grads[n] = small_grads[n]
        delta[n], new_m[n], new_v[n] = upd[3 * i:3 * i + 3]

    shaped = lambda d: [d[n].reshape(w[n].shape) for n in WEIGHT_NAMES]
    return (loss, grad_x[None], *shaped(grads), *shaped(delta), *shaped(new_m), *shaped(new_v))
```

```python
import functools
import math

import jax
import jax.numpy as jnp
from jax import lax
from jax.experimental import pallas as pl
from jax.experimental.pallas import tpu as pltpu

F32 = jnp.float32
BF16 = jnp.bfloat16

D = 1024
DP = 512
PG = 128
WINDOWS = (2, 4, 8, 16)
DR = 1024
NH = 8
HD = 128
DIN = 4608
DFF = 2816
EPS = 1e-6
LRU_C = 8.0
POOL_HALO = 16
CONV_HALO = 8
KEPT = 5

ADAM_LR = 0.001
ADAM_B1 = 0.9
ADAM_B2 = 0.999
ADAM_EPS = 1e-08
ADAM_WD = 0.01
ADAM_STEP = 10

VMEM_LIMIT = 56 * 1024 * 1024
MESH_AXES = ("x", "y", "c")
MESH = pl.DeviceIdType.MESH


def _dot(a, b):
    return jnp.dot(a, b, preferred_element_type=F32)


def _dot_nt(a, b):
    return lax.dot_general(a, b, (((1,), (1,)), ((), ())), preferred_element_type=F32)


def _dot_tn(a, b):
    return lax.dot_general(a, b, (((0,), (0,)), ((), ())), preferred_element_type=F32)


def _params(*sem):
    return pltpu.CompilerParams(dimension_semantics=sem, vmem_limit_bytes=VMEM_LIMIT)


def _resident(shape):
    nd = len(shape)
    return pl.BlockSpec(shape, lambda i: (0,) * nd, pipeline_mode=pl.Buffered(1))


def _rows(shape_cols, tm):
    return pl.BlockSpec((tm, shape_cols), lambda i: (i, 0))


def _call(body, name, grid, in_specs, out_specs, out_shape, operands, scratch_shapes=(), exchange=None, exchange_operands=()):
    n_in, n_out, n_scr = len(in_specs), len(out_specs), len(scratch_shapes)
    steps = math.prod(grid)
    if exchange is None:
        outs = pl.pallas_call(body, name=name, grid=grid, in_specs=in_specs, out_specs=out_specs, out_shape=out_shape,
                              scratch_shapes=list(scratch_shapes), compiler_params=_params(*["arbitrary"] * len(grid)))(*operands)
        return outs, []
    e_in, e_out = len(exchange.in_specs), len(exchange.out_specs)

    def hosted(*refs):
        ins, refs = refs[:n_in], refs[n_in:]
        e_ins, refs = refs[:e_in], refs[e_in:]
        outs, refs = refs[:n_out], refs[n_out:]
        e_outs, refs = refs[:e_out], refs[e_out:]
        scr, e_scr = refs[:n_scr], refs[n_scr:]
        step = pl.program_id(0)
        for axis in range(1, len(grid)):
            step = step * grid[axis] + pl.program_id(axis)
        pl.when(step == 0)(lambda: exchange.start(e_ins, e_outs, e_scr))
        for at, middle in exchange.middles(steps):
            pl.when(step == at)(lambda middle=middle: middle(e_ins, e_outs, e_scr))
        body(*ins, *outs, *scr)
        pl.when(step == steps - 1)(lambda: exchange.finish(e_ins, e_outs, e_scr))

    outs = pl.pallas_call(
        hosted, name=name, grid=grid, in_specs=list(in_specs) + exchange.in_specs,
        out_specs=list(out_specs) + exchange.out_specs, out_shape=list(out_shape) + exchange.out_shape,
        scratch_shapes=list(scratch_shapes) + exchange.scratch_shapes,
        compiler_params=_params(*["arbitrary"] * len(grid)))(*operands, *exchange_operands)
    return outs[:n_out], outs[n_out:]


GELU_C = math.sqrt(2.0 / math.pi)
GELU_K = 0.044715 * GELU_C


def _gelu(x, with_grad=False):
    x2 = x * x
    t = jnp.tanh(x * (GELU_C + GELU_K * x2))
    hx = 0.5 * x
    y = hx + hx * t
    if not with_grad:
        return y
    return y, 0.5 + 0.5 * t + hx * (1.0 - t * t) * (GELU_C + (3.0 * GELU_K) * x2)


def _softplus_neg(lam):
    z = jnp.exp(-jnp.abs(lam))
    u = 1.0 + z
    dlt = u - 1.0
    log1p = jnp.where(dlt == 0.0, z, jnp.log(u) * (z / jnp.where(dlt == 0.0, 1.0, dlt)))
    return jnp.maximum(-lam, 0.0) + log1p


def _sigmoid(x):
    return 0.5 * jnp.tanh(0.5 * x) + 0.5


def _linear_scan(out_ref, A, B, h0, reverse):
    n = A.shape[0]
    sub = lax.broadcasted_iota(jnp.int32, (8, 1), 0)
    tiles = range(n // 8 - 1, -1, -1) if reverse else range(n // 8)
    carry = h0
    for j in tiles:
        a, b = A[8 * j:8 * j + 8, :], B[8 * j:8 * j + 8, :]
        for d in (1, 2, 4):
            keep = (sub < 8 - d) if reverse else (sub >= d)
            shift = 8 - d if reverse else d
            b = jnp.where(keep, a * pltpu.roll(b, shift, axis=0) + b, b)
            a = jnp.where(keep, a * pltpu.roll(a, shift, axis=0), a)
        h = a * carry + b
        out_ref[8 * j:8 * j + 8, :] = h
        carry = h[0:1, :] if reverse else h[7:8, :]
    return carry


def _pool_windows(ext, shift_sign):
    n = ext.shape[0]
    s = ext
    outs = []
    for w in WINDOWS:
        d = w // 2
        s = s + pltpu.roll(s, d if shift_sign > 0 else n - d, axis=0)
        outs.append(s[:, :PG])
        s = s[:, PG:]
    return outs


def _conv_taps(uext):
    taps = []
    for k in range(4):
        sh = 3 - k
        v = uext if sh == 0 else pltpu.roll(uext, sh, axis=0)
        taps.append(v[CONV_HALO:, :])
    return taps


def _gates(v, wa_ref, ba_ref, wx_ref, bx_ref, sp):
    vb = v.astype(BF16)
    ra, rx = [], []
    for h in range(NH):
        vh = vb[:, h * HD:(h + 1) * HD]
        ra.append(_dot(vh, wa_ref[h]))
        rx.append(_dot(vh, wx_ref[h]))
    r = _sigmoid(jnp.concatenate(ra, axis=1) + ba_ref[...])
    i = _sigmoid(jnp.concatenate(rx, axis=1) + bx_ref[...])
    log_a = r * ((-LRU_C) * sp)
    a = jnp.exp(log_a)
    one_minus = -jnp.tanh(log_a) * (1.0 + a * a)
    return r, i, a, jnp.sqrt(one_minus), lax.rsqrt(one_minus)


def _mixer_fwd(proj, wg, scale, w_pool_out, conv_w, conv_b, wa, ba, wx, bx, lam, w_rnn_out, exchange=None,
               exchange_operands=(), tm=256):
    S = proj.shape[0]
    UW = DP + 2 * DR

    def body(proj_ref, wg_ref, scale_ref, wpo_ref, cw_ref, cb_ref, wa_ref, ba_ref, wx_ref, bx_ref, lam_ref, wro_ref,
             pooled_ref, pm_ref, ypool_ref, hr_ref, z_ref, yrnn_ref, kept_ref, pool_carry, conv_carry, h_carry):
        i = pl.program_id(0)

        @pl.when(i == 0)
        def _():
            pool_carry[...] = jnp.zeros_like(pool_carry)
            conv_carry[...] = jnp.zeros_like(conv_carry)
            h_carry[...] = jnp.zeros_like(h_carry)

        rows = lax.broadcasted_iota(jnp.int32, (tm, 1), 0)
        t_glob = i * tm + rows

        u_pool = proj_ref[:, 0:DP]
        ext = jnp.concatenate([pool_carry[...], u_pool], axis=0)
        pool_carry[...] = u_pool[tm - POOL_HALO:, :]
        sums = _pool_windows(ext, +1)
        mixed = []
        for g, w in enumerate(WINDOWS):
            inv_cnt = 1.0 / jnp.minimum(t_glob + 1, w).astype(F32)
            pooled_g = sums[g][POOL_HALO:, :] * inv_cnt - u_pool[:, g * PG:(g + 1) * PG]
            pooled_b = pooled_g.astype(BF16)
            pooled_ref[:, g * PG:(g + 1) * PG] = pooled_b
            mixed.append(_dot(pooled_b, wg_ref[g]))
        pm = (jnp.concatenate(mixed, axis=1) * scale_ref[...]).astype(BF16)
        pm_ref[...] = pm
        ypool_ref[...] = _dot(pm, wpo_ref[...])

        u_rnn = proj_ref[:, DP:DP + DR]
        uext = jnp.concatenate([conv_carry[...], u_rnn], axis=0)
        conv_carry[...] = u_rnn[tm - CONV_HALO:, :]
        taps = _conv_taps(uext)
        v = cb_ref[...]
        for k in range(4):
            v = v + taps[k] * cw_ref[k:k + 1, :]
        sp = _softplus_neg(lam_ref[...])
        r, gi, a, mult, _ = _gates(v, wa_ref, ba_ref, wx_ref, bx_ref, sp)
        for k, kept in enumerate((v, r, gi, a, mult)):
            kept_ref[k] = kept
        h_carry[0:1, :] = _linear_scan(hr_ref, a, mult * gi * v, h_carry[0:1, :], reverse=False)
        z = (hr_ref[...] * _gelu(proj_ref[:, DP + DR:UW])).astype(BF16)
        z_ref[...] = z
        yrnn_ref[...] = _dot(z, wro_ref[...])

    return _call(
        body, "mixer_fwd", (S // tm,),
        in_specs=[_rows(UW, tm), _resident((4, PG, PG)), _resident((1, DP)), _resident((DP, D)), _resident((4, DR)),
                  _resident((1, DR)), _resident((NH, HD, HD)), _resident((1, DR)), _resident((NH, HD, HD)),
                  _resident((1, DR)), _resident((1, DR)), _resident((DR, D))],
        out_specs=[_rows(DP, tm), _rows(DP, tm), _rows(D, tm), _rows(DR, tm), _rows(DR, tm), _rows(D, tm),
                   pl.BlockSpec((KEPT, tm, DR), lambda i: (0, i, 0))],
        out_shape=[jax.ShapeDtypeStruct((S, DP), BF16), jax.ShapeDtypeStruct((S, DP), BF16),
                   jax.ShapeDtypeStruct((S, D), F32), jax.ShapeDtypeStruct((S, DR), F32),
                   jax.ShapeDtypeStruct((S, DR), BF16), jax.ShapeDtypeStruct((S, D), F32),
                   jax.ShapeDtypeStruct((KEPT, S, DR), F32)],
        scratch_shapes=[pltpu.VMEM((POOL_HALO, DP), F32), pltpu.VMEM((CONV_HALO, DR), F32), pltpu.VMEM((8, DR), F32)],
        operands=(proj, wg, scale, w_pool_out, conv_w, conv_b, wa, ba, wx, bx, lam, w_rnn_out),
        exchange=exchange, exchange_operands=exchange_operands)


FF_CHUNKS = ((0, 768), (768, 1536), (1536, 2304), (2304, DFF))


def _rms(x):
    r = lax.rsqrt(jnp.mean(x * x, axis=-1, keepdims=True) + EPS)
    return r, x * r


def _rms_bwd(dh, g, r, xh):
    dxh = dh * g
    return r * (dxh - xh * jnp.mean(dxh * xh, axis=-1, keepdims=True))


def _merge_out(x, proj, y_pool, y_rnn, w_o, norm_ffn, exchange=None, exchange_operands=(), tm=512):
    S = x.shape[0]
    GL0 = (DP + 2 * DR) // 512

    def gl_spec(k):
        return pl.BlockSpec((tm, 512), lambda i: (i, GL0 + k))

    def body(x_ref, gl0, gl1, gl2, gl3, yp_ref, yr_ref, wo_ref, gf_ref, mix_ref, x2_ref, h2_ref):
        s_p = _sigmoid(jnp.concatenate([gl0[...], gl1[...]], axis=1))
        s_r = _sigmoid(jnp.concatenate([gl2[...], gl3[...]], axis=1))
        mix = (s_p * yp_ref[...] + s_r * yr_ref[...]).astype(BF16)
        mix_ref[...] = mix
        x2 = x_ref[...] + _dot(mix, wo_ref[...])
        x2_ref[...] = x2
        _, xh2 = _rms(x2)
        h2_ref[...] = (xh2 * gf_ref[...]).astype(BF16)

    return _call(
        body, "merge_out", (S // tm,),
        in_specs=[_rows(D, tm), gl_spec(0), gl_spec(1), gl_spec(2), gl_spec(3), _rows(D, tm), _rows(D, tm),
                  _resident((D, D)), _resident((1, D))],
        out_specs=[_rows(D, tm), _rows(D, tm), _rows(D, tm)],
        out_shape=[jax.ShapeDtypeStruct((S, D), BF16), jax.ShapeDtypeStruct((S, D), F32), jax.ShapeDtypeStruct((S, D), BF16)],
        operands=(x, proj, proj, proj, proj, y_pool, y_rnn, w_o, norm_ffn),
        exchange=exchange, exchange_operands=exchange_operands)


def _ffn_up(h2, w_lo, w_hi, exchange=None, exchange_operands=(), tm=512):
    S = h2.shape[0]
    HALF = D // 2

    def body(h_ref, lo_ref, hi_ref, gu_ref, act_ref):
        h_lo, h_hi = h_ref[:, 0:HALF], h_ref[:, HALF:D]
        for c0, c1 in FF_CHUNKS:
            gate = _dot_nt(h_lo, lo_ref[c0:c1, :]) + _dot_nt(h_hi, hi_ref[c0:c1, :])
            up = _dot_nt(h_lo, lo_ref[DFF + c0:DFF + c1, :]) + _dot_nt(h_hi, hi_ref[DFF + c0:DFF + c1, :])
            gu_ref[:, c0:c1] = gate.astype(BF16)
            gu_ref[:, DFF + c0:DFF + c1] = up.astype(BF16)
            act_ref[:, c0:c1] = (gate * _sigmoid(gate) * up).astype(BF16)

    return _call(
        body, "ffn_up", (S // tm,),
        in_specs=[_rows(D, tm), _resident((2 * DFF, HALF)), _resident((2 * DFF, HALF))],
        out_specs=[_rows(2 * DFF, tm), _rows(DFF, tm)],
        out_shape=[jax.ShapeDtypeStruct((S, 2 * DFF), BF16), jax.ShapeDtypeStruct((S, DFF), BF16)],
        operands=(h2, w_lo, w_hi), exchange=exchange, exchange_operands=exchange_operands)


def _ffn_down_loss(act, x2, target, w_ffn_out, norm_final, tm=512):
    S = act.shape[0]

    def body(act_ref, x2_ref, t_ref, w_ref, gn_ref, dx3_ref, dx3b_ref, loss_ref, dvec_ref):
        i = pl.program_id(0)

        @pl.when(i == 0)
        def _():
            loss_ref[...] = jnp.zeros_like(loss_ref)
            dvec_ref[...] = jnp.zeros_like(dvec_ref)

        x3 = x2_ref[...] + _dot(act_ref[...], w_ref[...])
        r3, xh3 = _rms(x3)
        g_fin = gn_ref[...]
        e = xh3 * g_fin - t_ref[...]
        loss_ref[...] += jnp.sum(e * e, axis=(0, 1), keepdims=True) * (0.5 / D)
        dy = e * (1.0 / D)
        dvec_ref[0:1, :] += jnp.sum(dy * xh3, axis=0, keepdims=True)
        dx3 = _rms_bwd(dy, g_fin, r3, xh3)
        dx3_ref[...] = dx3
        dx3b_ref[...] = dx3.astype(BF16)

    return pl.pallas_call(
        body, name="ffn_down_loss", grid=(S // tm,),
        in_specs=[_rows(DFF, tm), _rows(D, tm), _rows(D, tm), _resident((DFF, D)), _resident((1, D))],
        out_specs=[_rows(D, tm), _rows(D, tm), _resident((1, 1)), _resident((8, D))],
        out_shape=[jax.ShapeDtypeStruct((S, D), F32), jax.ShapeDtypeStruct((S, D), BF16),
                   jax.ShapeDtypeStruct((1, 1), F32), jax.ShapeDtypeStruct((8, D), F32)],
        compiler_params=_params("arbitrary"),
    )(act, x2, target, w_ffn_out, norm_final)


def _ffn_bwd_down(dx3b, gu, w_ffn_out, tm=512):
    S = dx3b.shape[0]

    def body(d_ref, gu_ref, w_ref, dgu_ref):
        d = d_ref[...]
        for c0, c1 in FF_CHUNKS:
            dact = _dot_nt(d, w_ref[c0:c1, :])
            gate = gu_ref[:, c0:c1].astype(F32)
            up = gu_ref[:, DFF + c0:DFF + c1].astype(F32)
            sg = _sigmoid(gate)
            dgu_ref[:, c0:c1] = (dact * up * (sg * (1.0 + gate * (1.0 - sg)))).astype(BF16)
            dgu_ref[:, DFF + c0:DFF + c1] = (dact * (gate * sg)).astype(BF16)

    return pl.pallas_call(
        body, name="ffn_bwd_down", grid=(S // tm,),
        in_specs=[_rows(D, tm), _rows(2 * DFF, tm), _resident((DFF, D))],
        out_specs=_rows(2 * DFF, tm),
        out_shape=jax.ShapeDtypeStruct((S, 2 * DFF), BF16),
        compiler_params=_params("parallel"),
    )(dx3b, gu, w_ffn_out)


def _ffn_bwd_up(dgu, x2, dx3, w_lo, w_hi, norm_ffn, w_o, tm=512):
    S = dgu.shape[0]
    HALF = D // 2

    def body(dgu_ref, x2_ref, dx3_ref, lo_ref, hi_ref, gf_ref, wo_ref, dx2_ref, dx2b_ref, dmixo_ref, dvec_ref):
        i = pl.program_id(0)

        @pl.when(i == 0)
        def _():
            dvec_ref[...] = jnp.zeros_like(dvec_ref)

        dgate, dup = dgu_ref[:, 0:DFF], dgu_ref[:, DFF:2 * DFF]
        dh2 = jnp.concatenate([_dot(dgate, w[0:DFF, :]) + _dot(dup, w[DFF:2 * DFF, :]) for w in (lo_ref, hi_ref)], axis=1)
        r2, xh2 = _rms(x2_ref[...])
        dvec_ref[0:1, :] += jnp.sum(dh2 * xh2, axis=0, keepdims=True)
        dx2 = dx3_ref[...] + _rms_bwd(dh2, gf_ref[...], r2, xh2)
        dx2_ref[...] = dx2
        dx2b = dx2.astype(BF16)
        dx2b_ref[...] = dx2b
        dmixo_ref[...] = _dot_nt(dx2b, wo_ref[...])

    return pl.pallas_call(
        body, name="ffn_bwd_up", grid=(S // tm,),
        in_specs=[_rows(2 * DFF, tm), _rows(D, tm), _rows(D, tm), _resident((2 * DFF, HALF)), _resident((2 * DFF, HALF)),
                  _resident((1, D)), _resident((D, D))],
        out_specs=[_rows(D, tm), _rows(D, tm), _rows(D, tm), _resident((8, D))],
        out_shape=[jax.ShapeDtypeStruct((S, D), F32), jax.ShapeDtypeStruct((S, D), BF16), jax.ShapeDtypeStruct((S, D), F32),
                   jax.ShapeDtypeStruct((8, D), F32)],
        compiler_params=_params("arbitrary"),
    )(dgu, x2, dx3, w_lo, w_hi, norm_ffn, w_o)


VEC_ROWS = 16
MAT_WA = 4 * PG
MAT_WX = MAT_WA + NH * HD
MAT_ROWS = MAT_WX + NH * HD


def _mixer_bwd(proj, dmixo, y_pool, y_rnn, hr, kept, wg, scale, w_pool_out, conv_w, conv_b, wa, ba, wx, bx, lam, w_rnn_out,
               exchange=None, exchange_operands=(), tm=256):
    S = proj.shape[0]
    nt = S // tm

    def rev(cols):
        return pl.BlockSpec((tm, cols), lambda i: (nt - 1 - i, 0))

    def halo(rows_, cols):
        per = tm // rows_
        return pl.BlockSpec((rows_, cols), lambda i: (jnp.maximum((nt - 1 - i) * per - 1, 0), 0))

    def body(proj_ref, projh_ref, dmixo_ref, yp_ref, yr_ref, hr_ref, hrh_ref, kept_ref, wg_ref, scale_ref, wpo_ref, cw_ref, cb_ref,
             wa_ref, ba_ref, wx_ref, bx_ref, lam_ref, wro_ref,
             dproj_ref, dypb_ref, dyrb_ref, dmat_ref, dvec_ref,
             q_carry, dv_carry, a_carry, g_carry, g_scr):
        i = pl.program_id(0)
        ti = nt - 1 - i

        @pl.when(i == 0)
        def _():
            q_carry[...] = jnp.zeros_like(q_carry)
            dv_carry[...] = jnp.zeros_like(dv_carry)
            a_carry[...] = jnp.zeros_like(a_carry)
            g_carry[...] = jnp.zeros_like(g_carry)
            dmat_ref[...] = jnp.zeros_like(dmat_ref)
            dvec_ref[...] = jnp.zeros_like(dvec_ref)

        rows = lax.broadcasted_iota(jnp.int32, (tm, 1), 0)
        t_glob = ti * tm + rows
        has_prev = (ti > 0).astype(F32)
        dmixo = dmixo_ref[...]

        s_p = _sigmoid(proj_ref[:, DP + 2 * DR:DP + 2 * DR + D])
        s_r = _sigmoid(proj_ref[:, DP + 2 * DR + D:DIN])
        dproj_ref[:, DP + 2 * DR:DP + 2 * DR + D] = (dmixo * yp_ref[...] * s_p * (1.0 - s_p)).astype(BF16)
        dproj_ref[:, DP + 2 * DR + D:DIN] = (dmixo * yr_ref[...] * s_r * (1.0 - s_r)).astype(BF16)
        dyp = (dmixo * s_p).astype(BF16)
        dyr = (dmixo * s_r).astype(BF16)
        dypb_ref[...] = dyp
        dyrb_ref[...] = dyr

        dz = _dot_nt(dyr, wro_ref[...])
        u_gate = proj_ref[:, DP + DR:DP + 2 * DR]
        gg, dgelu = _gelu(u_gate, with_grad=True)
        hr_t = hr_ref[...]
        dproj_ref[:, DP + DR:DP + 2 * DR] = (dz * hr_t * dgelu).astype(BF16)
        dhr = dz * gg

        sp = _softplus_neg(lam_ref[...])
        v, r, gi, a, mult = (kept_ref[k] for k in range(KEPT))
        inv_mult = 1.0 / mult

        C = jnp.where(rows == tm - 1, a_carry[0:1, :], pltpu.roll(a, tm - 1, axis=0))
        g_carry[0:1, :] = _linear_scan(g_scr, C, dhr, g_carry[0:1, :], reverse=True)
        a_carry[0:1, :] = a[0:1, :]
        g = g_scr[...]

        h_prev = jnp.where(rows == 0, hrh_ref[7:8, :] * has_prev, pltpu.roll(hr_t, 1, axis=0))
        da = g * h_prev
        gm = g * mult
        dmult = g * gi * v
        di = gm * v
        dv = gm * gi
        dlog_a = da * a - dmult * (a * a * inv_mult)
        dvec_ref[4:5, :] += jnp.sum(dlog_a * r, axis=0, keepdims=True)
        dra = (dlog_a * ((-LRU_C) * sp) * r * (1.0 - r))
        drx = di * gi * (1.0 - gi)
        dvec_ref[2:3, :] += jnp.sum(dra, axis=0, keepdims=True)
        dvec_ref[3:4, :] += jnp.sum(drx, axis=0, keepdims=True)
        drab = dra.astype(BF16)
        drxb = drx.astype(BF16)
        vb = v.astype(BF16)
        dvg = []
        for h in range(NH):
            sl = slice(h * HD, (h + 1) * HD)
            dvg.append(_dot_nt(drab[:, sl], wa_ref[h]) + _dot_nt(drxb[:, sl], wx_ref[h]))
            dmat_ref[MAT_WA + h * HD:MAT_WA + (h + 1) * HD, :] += _dot_tn(vb[:, sl], drab[:, sl])
            dmat_ref[MAT_WX + h * HD:MAT_WX + (h + 1) * HD, :] += _dot_tn(vb[:, sl], drxb[:, sl])
        dv = dv + jnp.concatenate(dvg, axis=1)
        dvec_ref[1:2, :] += jnp.sum(dv, axis=0, keepdims=True)
        dvext = jnp.concatenate([dv, dv_carry[...]], axis=0)
        dv_carry[...] = dv[0:CONV_HALO, :]
        n = tm + CONV_HALO
        u_rnn = proj_ref[:, DP:DP + DR]
        du_rnn = dv * cw_ref[3:4, :]
        dvec_ref[8:9, :] += jnp.sum(dv * u_rnn, axis=0, keepdims=True)
        for k in range(3):
            dv_k = pltpu.roll(dvext, n - (3 - k), axis=0)[0:tm, :]
            du_rnn = du_rnn + dv_k * cw_ref[k:k + 1, :]
            dvec_ref[5 + k:6 + k, :] += jnp.sum(dv_k * u_rnn, axis=0, keepdims=True)
        dproj_ref[:, DP:DP + DR] = du_rnn.astype(BF16)

        dpm = _dot_nt(dyp, wpo_ref[...])
        u_pool = proj_ref[:, 0:DP]
        ext = jnp.concatenate([projh_ref[:, 0:DP] * has_prev, u_pool], axis=0)
        sums = _pool_windows(ext, +1)
        scale_v = scale_ref[...]
        qs = []
        dpooled = []
        dscale = []
        for gi_, w in enumerate(WINDOWS):
            sl = slice(gi_ * PG, (gi_ + 1) * PG)
            inv_cnt = 1.0 / jnp.minimum(t_glob + 1, w).astype(F32)
            pooled_b = (sums[gi_][POOL_HALO:, :] * inv_cnt - u_pool[:, sl]).astype(BF16)
            mixed_g = _dot(pooled_b, wg_ref[gi_])
            dscale.append(jnp.sum(dpm[:, sl] * mixed_g, axis=0, keepdims=True))
            dmixed_b = (dpm[:, sl] * scale_v[:, sl]).astype(BF16)
            dmat_ref[gi_ * PG:(gi_ + 1) * PG, :] += _dot_tn(pooled_b, dmixed_b)
            dp_g = _dot_nt(dmixed_b, wg_ref[gi_])
            dpooled.append(dp_g)
            qs.append(dp_g * inv_cnt)
        dvec_ref[0:1, 0:DP] += jnp.concatenate(dscale, axis=1)
        q = jnp.concatenate(qs, axis=1)
        qext = jnp.concatenate([q, q_carry[...]], axis=0)
        q_carry[...] = q[0:POOL_HALO, :]
        tsum = _pool_windows(qext, -1)
        for gi_ in range(4):
            dproj_ref[:, gi_ * PG:(gi_ + 1) * PG] = (tsum[gi_][0:tm, :] - dpooled[gi_]).astype(BF16)

        @pl.when(i == nt - 1)
        def _():
            dvec_ref[4:5, :] = dvec_ref[4:5, :] * (LRU_C * _sigmoid(-lam_ref[...]))

    return _call(
        body, "mixer_bwd", (nt,),
        in_specs=[rev(DIN), halo(POOL_HALO, DIN), rev(D), rev(D), rev(D), rev(DR), halo(8, DR),
                  pl.BlockSpec((KEPT, tm, DR), lambda i: (0, nt - 1 - i, 0)), _resident((4, PG, PG)), _resident((1, DP)), _resident((DP, D)), _resident((4, DR)), _resident((1, DR)),
                  _resident((NH, HD, HD)), _resident((1, DR)), _resident((NH, HD, HD)), _resident((1, DR)),
                  _resident((1, DR)), _resident((DR, D))],
        out_specs=[rev(DIN), rev(D), rev(D), _resident((MAT_ROWS, HD)), _resident((VEC_ROWS, DR))],
        out_shape=[jax.ShapeDtypeStruct((S, DIN), BF16), jax.ShapeDtypeStruct((S, D), BF16),
                   jax.ShapeDtypeStruct((S, D), BF16), jax.ShapeDtypeStruct((MAT_ROWS, HD), F32),
                   jax.ShapeDtypeStruct((VEC_ROWS, DR), F32)],
        scratch_shapes=[pltpu.VMEM((POOL_HALO, DP), F32), pltpu.VMEM((CONV_HALO, DR), F32), pltpu.VMEM((8, DR), F32),
                        pltpu.VMEM((8, DR), F32), pltpu.VMEM((tm, DR), F32)],
        operands=(proj, proj, dmixo, y_pool, y_rnn, hr, hr, kept, wg, scale, w_pool_out, conv_w, conv_b, wa, ba, wx, bx, lam,
                  w_rnn_out),
        exchange=exchange, exchange_operands=exchange_operands)


def _in_bwd(dproj, x, dx2, norm_mix, w_in, exchange=None, exchange_operands=(), tm=512):
    S = x.shape[0]

    def body(dp_ref, x_ref, dx2_ref, g_ref, w_ref, dx_ref, dg_ref):
        i = pl.program_id(0)

        @pl.when(i == 0)
        def _():
            dg_ref[...] = jnp.zeros_like(dg_ref)

        dh = _dot(dp_ref[:, 0:1536], w_ref[0:1536, :])
        dh = dh + _dot(dp_ref[:, 1536:3072], w_ref[1536:3072, :])
        dh = dh + _dot(dp_ref[:, 3072:DIN], w_ref[3072:DIN, :])
        xv = x_ref[...]
        r = lax.rsqrt(jnp.mean(xv * xv, axis=-1, keepdims=True) + EPS)
        xh = xv * r
        dg_ref[0:1, :] += jnp.sum(dh * xh, axis=0, keepdims=True)
        dxh = dh * g_ref[...]
        dx_ref[...] = dx2_ref[...] + r * (dxh - xh * jnp.mean(dxh * xh, axis=-1, keepdims=True))

    return _call(
        body, "in_bwd", (S // tm,),
        in_specs=[_rows(DIN, tm), _rows(D, tm), _rows(D, tm), _resident((1, D)), _resident((DIN, D))],
        out_specs=[_rows(D, tm), _resident((8, D))],
        out_shape=[jax.ShapeDtypeStruct((S, D), F32), jax.ShapeDtypeStruct((8, D), F32)],
        operands=(dproj, x, dx2, norm_mix, w_in), exchange=exchange, exchange_operands=exchange_operands)


def _wgrad(a, b, name, tk, tn, exchange=None, exchange_operands=()):
    S, K = a.shape
    N = b.shape[1]

    def body(a_ref, b_ref, o_ref):
        o_ref[...] = _dot_tn(a_ref[...], b_ref[...]).astype(BF16)

    (out,), exchanged = _call(
        body, name, (K // tk, N // tn),
        in_specs=[pl.BlockSpec((S, tk), lambda k, n: (0, k)), pl.BlockSpec((S, tn), lambda k, n: (0, n))],
        out_specs=[pl.BlockSpec((tk, tn), lambda k, n: (k, n))],
        out_shape=[jax.ShapeDtypeStruct((K, N), BF16)],
        operands=(a, b), exchange=exchange, exchange_operands=exchange_operands)
    return (out, exchanged) if exchange is not None else out


VEC_SCALE, VEC_CONV_B, VEC_BA, VEC_BX, VEC_LAM, VEC_CONV_W, VEC_NORM_FINAL, VEC_NORM_FFN = 0, 1, 2, 3, 4, 5, 9, 10
VEC_LOSS = 11


class _Big:
    def __init__(self, name, rows, cols, axis, n, dtype=BF16, transposed=False, src_cols=None):
        self.name, self.rows, self.cols, self.axis, self.n, self.dtype = name, rows, cols, axis, n, dtype
        self.transposed = transposed
        self.src_cols = src_cols
        self.block_shape = (rows, n) if axis == 1 else (n, cols)

    def block(self, ref, p):
        if self.axis == 1:
            return ref.at[:, pl.ds(pl.multiple_of(p * self.n, 128), self.n)]
        return ref.at[pl.ds(pl.multiple_of(p * self.n, 16 if self.dtype == BF16 else 8), self.n), :]

    def block_index(self, p):
        return (0, p) if self.axis == 1 else (p, 0)


BIG = (_Big("w_in", DIN, D, 0, DIN // 8, transposed=True), _Big("w_pool_out", DP, D, 1, D // 8),
       _Big("w_rnn_out", DR, D, 0, DR // 8), _Big("w_o", D, D, 0, D // 8),
       _Big("w_ffn_in", 2 * DFF, D, 0, 2 * DFF // 8, transposed=True), _Big("w_ffn_out", DFF, D, 0, DFF // 8))
CONV_W = _Big("conv_w", 8, DR, 1, DR // 8, F32)
W_FFN_IN_HALVES = (_Big("w_ffn_in_lo", 2 * DFF, D // 2, 0, 2 * DFF // 8, src_cols=(0, D // 2)),
                   _Big("w_ffn_in_hi", 2 * DFF, D // 2, 0, 2 * DFF // 8, src_cols=(D // 2, D)))
GATHERED = BIG + (CONV_W,) + W_FFN_IN_HALVES

HBM_SPEC = pl.BlockSpec(memory_space=pl.ANY)
VMEM_SPEC = pl.BlockSpec(memory_space=pltpu.VMEM)


def _place():
    x, y, c = (lax.axis_index(a) for a in MESH_AXES)
    other_chips = [(1 - x, y), (x, 1 - y), (1 - x, 1 - y)]
    return x, y, c, other_chips


def _remote(src, dst, send_sems, recv_sems, idx, to):
    return pltpu.make_async_remote_copy(src_ref=src, dst_ref=dst, send_sem=send_sems.at[idx], recv_sem=recv_sems.at[idx],
                                        device_id=to, device_id_type=MESH)


def _device_index(chip, core):
    return 4 * chip[0] + 2 * chip[1] + core


class _Gather:
    def __init__(self, tensors):
        self.tensors = tuple(tensors)
        n = len(self.tensors)
        self.in_specs = [HBM_SPEC] * n
        self.out_specs = [HBM_SPEC] * n
        self.out_shape = [jax.ShapeDtypeStruct((T.rows, T.cols), T.dtype) for T in self.tensors]
        self.scratch_shapes = [pltpu.VMEM(T.block_shape, T.dtype) for T in self.tensors] + [
            pltpu.VMEM(T.block_shape, F32) for T in self.tensors] + [
            pltpu.SemaphoreType.DMA((n, 7)), pltpu.SemaphoreType.DMA((n, 7)), pltpu.SemaphoreType.DMA((n, 2))]

    def middles(self, steps):
        return [(steps // 2, self.relay), (steps - 1, self.middle)]

    def _copies(self, ins, outs, scratch):
        n = len(self.tensors)
        mine, raw, (send_sems, recv_sems, loc_sems) = scratch[:n], scratch[n:2 * n], scratch[2 * n:]
        x, y, c, chips = _place()
        sibling = (x, y, 1 - c)
        me = _device_index((x, y), c)
        relay_from = (jnp.where(c == 0, 1 - x, x), jnp.where(c == 0, y, 1 - y))
        relay_to = (jnp.where(c == 0, x, 1 - x), jnp.where(c == 0, 1 - y, y))
        loads, stores, first, relays, passed, arrivals, late = [], [], [], [], [], [], []
        for t, T in enumerate(self.tensors):
            place = T.block(outs[t], me)
            src = ins[t] if T.src_cols is None else ins[t].at[:, T.src_cols[0]:T.src_cols[1]]
            loads.append(pltpu.make_async_copy(src, raw[t], loc_sems.at[t, 0]))
            stores.append(pltpu.make_async_copy(mine[t], place, loc_sems.at[t, 1]))
            first.append(_remote(mine[t], place, send_sems, recv_sems, (t, 0), sibling))
            theirs = T.block(outs[t], _device_index((x, y), 1 - c))
            late.append(_remote(theirs, theirs, send_sems, recv_sems, (t, 0), sibling))
            relayed = T.block(outs[t], _device_index(relay_from, c))
            relays.append(_remote(relayed, relayed, send_sems, recv_sems, (t, 3), (*relay_to, c)))
            for k, chip in enumerate(chips):
                if k < 2:
                    first.append(_remote(mine[t], place, send_sems, recv_sems, (t, 1 + k), (*chip, c)))
                land = T.block(outs[t], _device_index(chip, c))
                arrivals.append(_remote(land, land, send_sems, recv_sems, (t, 1 + k), sibling))
                passed.append(_remote(land, land, send_sems, recv_sems, (t, 4 + k), sibling))
                theirs = T.block(outs[t], _device_index(chip, 1 - c))
                late.append(_remote(theirs, theirs, send_sems, recv_sems, (t, 4 + k), sibling))
        return loads, stores, first, relays, passed, arrivals, late

    def start(self, ins, outs, scratch):
        loads, stores, first, _, _, _, _ = self._copies(ins, outs, scratch)
        n = len(self.tensors)
        for cp in loads:
            cp.start()
        for t, cp in enumerate(loads):
            cp.wait()
            scratch[t][...] = scratch[n + t][...].astype(self.tensors[t].dtype)
        for cp in stores + first:
            cp.start()

    def relay(self, ins, outs, scratch, skip=0):
        _, _, _, relays, passed, arrivals, _ = self._copies(ins, outs, scratch)
        for t in range(skip, len(self.tensors)):
            arrivals[3 * t].wait_recv()
            arrivals[3 * t + 1].wait_recv()
            for cp in (relays[t], passed[3 * t], passed[3 * t + 1]):
                cp.start()

    def middle(self, ins, outs, scratch, skip=0):
        _, _, _, _, passed, arrivals, _ = self._copies(ins, outs, scratch)
        for t in range(skip, len(self.tensors)):
            arrivals[3 * t + 2].wait_recv()
            passed[3 * t + 2].start()

    def finish(self, ins, outs, scratch, skip=0):
        _, stores, first, relays, passed, _, late = self._copies(ins, outs, scratch)
        for cp in late[4 * skip:]:
            cp.wait_recv()
        for cp in first + relays + passed:
            cp.wait_send()
        for cp in stores[skip:]:
            cp.wait()


def _in_proj_gather(x, norm_mix, blocks, tensors, order, tm=512):
    S = x.shape[0]
    nt = S // tm
    n = len(tensors)
    gather = _Gather(tensors)
    CB = 2 * tensors[0].n

    def body(order_ref, x_ref, g_ref, *refs):
        ins, (proj_ref, h_ref), outs = refs[:n], refs[n:n + 2], refs[n + 2:2 * n + 2]
        (h_all, w_chip, w_sem), scratch = refs[2 * n + 2:2 * n + 5], refs[2 * n + 5:]
        q, i = pl.program_id(0), pl.program_id(1)
        _, stores, _, relays, passed, arrivals, late = gather._copies(ins, outs, scratch)

        def fetch(turn):
            rows = outs[0].at[pl.ds(pl.multiple_of(order_ref[turn] * CB, 16), CB), :]
            cp = pltpu.make_async_copy(rows, w_chip, w_sem)
            cp.start()
            cp.wait()

        @pl.when((q == 0) & (i == 0))
        def _():
            gather.start(ins, outs, scratch)
            late[0].wait_recv()
            stores[0].wait()
            fetch(0)

        @pl.when((q == 1) & (i == 0))
        def _():
            arrivals[0].wait_recv()
            arrivals[1].wait_recv()
            for cp in (relays[0], passed[0], passed[1]):
                cp.start()
            late[1].wait_recv()
            fetch(1)

        @pl.when((q == 2) & (i == 0))
        def _():
            late[2].wait_recv()
            fetch(2)
            gather.relay(ins, outs, scratch, skip=1)

        @pl.when((q == 3) & (i == 0))
        def _():
            arrivals[2].wait_recv()
            passed[2].start()
            late[3].wait_recv()
            fetch(3)

        rows = pl.ds(pl.multiple_of(i * tm, tm), tm)

        @pl.when(q == 0)
        def _():
            xv = x_ref[...]
            r = lax.rsqrt(jnp.mean(xv * xv, axis=-1, keepdims=True) + EPS)
            h = (xv * r * g_ref[...]).astype(BF16)
            h_all[rows, :] = h
            h_ref[...] = h

        proj_ref[...] = _dot_nt(h_all[rows, :], w_chip[...])

        @pl.when((q == 3) & (i == nt - 1))
        def _():
            gather.middle(ins, outs, scratch, skip=1)
            gather.finish(ins, outs, scratch, skip=1)

    row_tile = lambda q, i, order: (jnp.where(q == 0, i, nt - 1), 0)
    whole = lambda shape: pl.BlockSpec(shape, lambda q, i, order: (0,) * len(shape), pipeline_mode=pl.Buffered(1))
    outs = pl.pallas_call(
        body, name="in_proj_gather",
        grid_spec=pltpu.PrefetchScalarGridSpec(
            num_scalar_prefetch=1, grid=(4, nt),
            in_specs=[pl.BlockSpec((tm, D), row_tile), whole((1, D))] + gather.in_specs,
            out_specs=[pl.BlockSpec((tm, CB), lambda q, i, order: (i, order[q])), pl.BlockSpec((tm, D), row_tile)]
            + gather.out_specs,
            scratch_shapes=[pltpu.VMEM((S, D), BF16), pltpu.VMEM((CB, D), BF16), pltpu.SemaphoreType.DMA]
            + gather.scratch_shapes),
        out_shape=[jax.ShapeDtypeStruct((S, DIN), F32), jax.ShapeDtypeStruct((S, D), BF16)] + gather.out_shape,
        compiler_params=_params("arbitrary", "arbitrary"),
    )(order, x, norm_mix, *blocks)
    return outs[:2], outs[2:]


PAIR_ROWS = 32


def _pair_reduce(grads, tensors, name):
    nt = len(tensors)

    def body(*refs):
        ins, own_out, sums_out, landed, mine = (refs[k * nt:(k + 1) * nt] for k in range(5))
        send_sems, recv_sems, loc_sems = refs[5 * nt:]
        x, y, c, chips = _place()
        chip_of = [2 * chip[0] + chip[1] for chip in chips]
        swaps, loads = [], []
        for t, T in enumerate(tensors):
            for j in range(4):
                swaps.append(_remote(T.block(ins[t], 2 * j + 1 - c), landed[t].at[j], send_sems, recv_sems, (t, j),
                                     (x, y, 1 - c)))
            for k in range(3):
                loads.append(pltpu.make_async_copy(T.block(ins[t], 2 * chip_of[k] + c), mine[t].at[k], loc_sems.at[t, k]))
        for cp in swaps + loads:
            cp.start()
        for cp in loads:
            cp.wait()
        for cp in swaps:
            cp.wait_recv()
        stores = []
        for t, T in enumerate(tensors):
            for k in range(3):
                acc, got = mine[t].at[k], landed[t].at[chip_of[k]]

                def add(i, carry, acc=acc, got=got):
                    rows = pl.ds(pl.multiple_of(i * PAIR_ROWS, PAIR_ROWS), PAIR_ROWS)
                    acc[rows, :] = (acc[rows, :].astype(F32) + got[rows, :].astype(F32)).astype(BF16)
                    return carry

                lax.fori_loop(0, T.block_shape[0] // PAIR_ROWS, add, 0)
            stores.append(pltpu.make_async_copy(mine[t], sums_out[t], loc_sems.at[t, 3]))
            stores.append(pltpu.make_async_copy(landed[t].at[2 * x + y], own_out[t], loc_sems.at[t, 4]))
        for cp in stores:
            cp.start()
        for cp in swaps:
            cp.wait_send()
        for cp in stores:
            cp.wait()

    blocks = [T.block_shape for T in tensors]
    return pl.pallas_call(
        body, name=name,
        in_specs=[HBM_SPEC] * nt, out_specs=[HBM_SPEC] * (2 * nt),
        out_shape=[jax.ShapeDtypeStruct(b, BF16) for b in blocks] + [jax.ShapeDtypeStruct((3,) + b, BF16) for b in blocks],
        scratch_shapes=[pltpu.VMEM((4,) + b, BF16) for b in blocks] + [pltpu.VMEM((3,) + b, BF16) for b in blocks]
        + [pltpu.SemaphoreType.DMA((nt, 4)), pltpu.SemaphoreType.DMA((nt, 4)), pltpu.SemaphoreType.DMA((nt, 5))],
        compiler_params=pltpu.CompilerParams(vmem_limit_bytes=VMEM_LIMIT),
    )(*grads)


class _Scatter:
    def middles(self, steps):
        return []

    def __init__(self, tensors):
        n = len(tensors)
        self.in_specs = [HBM_SPEC] * n
        self.out_specs = [HBM_SPEC] * n
        self.out_shape = [jax.ShapeDtypeStruct((3,) + T.block_shape, BF16) for T in tensors]
        self.scratch_shapes = [pltpu.SemaphoreType.DMA((n, 3)), pltpu.SemaphoreType.DMA((n, 3))]

    def _copies(self, ins, outs, scratch):
        send_sems, recv_sems = scratch
        x, y, c, chips = _place()
        return [_remote(ins[t].at[k], outs[t].at[k], send_sems, recv_sems, (t, k), (*chip, c))
                for t in range(len(ins)) for k, chip in enumerate(chips)]

    def start(self, ins, outs, scratch):
        for cp in self._copies(ins, outs, scratch):
            cp.start()

    def finish(self, ins, outs, scratch):
        for cp in self._copies(ins, outs, scratch):
            cp.wait()


def _chip_scatter(sums, tensors, name):
    scatter = _Scatter(tensors)
    n = len(tensors)

    def body(*refs):
        ins, outs, scratch = refs[:n], refs[n:2 * n], refs[2 * n:]
        scatter.start(ins, outs, scratch)
        scatter.finish(ins, outs, scratch)

    return pl.pallas_call(
        body, name=name, in_specs=scatter.in_specs, out_specs=scatter.out_specs, out_shape=scatter.out_shape,
        scratch_shapes=scatter.scratch_shapes,
    )(*sums)


def _adamw(w, g, m, v):
    m = ADAM_B1 * m + (1.0 - ADAM_B1) * g
    v = ADAM_B2 * v + (1.0 - ADAM_B2) * (g * g)
    m_hat = m / (1.0 - ADAM_B1 ** ADAM_STEP)
    v_hat = v / (1.0 - ADAM_B2 ** ADAM_STEP)
    delta = -ADAM_LR * (m_hat / (jnp.sqrt(v_hat) + ADAM_EPS) + ADAM_WD * w)
    return delta, m, v


def _final_sum(T, g, lz1, lz2, where, w, m, v):
    rows, cols = T.block_shape
    sub = 4 if T.axis == 0 and rows % 64 == 0 and rows > 256 else 1
    blk = (rows // sub, cols)

    def body(where_ref, g_ref, l1_ref, l2_ref, w_ref, m_ref, v_ref, g_out, d_out, m_out, v_out):
        tot = g_ref[...].astype(F32) + l1_ref[...].astype(F32)
        for k in range(3):
            tot = tot + l2_ref[k].astype(F32)
        g_out[...] = tot
        d_out[...], m_out[...], v_out[...] = _adamw(w_ref[...], tot, m_ref[...], v_ref[...])

    def in_whole(r, wh):
        p = wh[0]
        return (0, p) if T.axis == 1 else (p * sub + r, 0)

    own = pl.BlockSpec(blk, lambda r, wh: (r, 0))
    return pl.pallas_call(
        body, name="grad_final_" + T.name,
        grid_spec=pltpu.PrefetchScalarGridSpec(
            num_scalar_prefetch=1, grid=(sub,),
            in_specs=[pl.BlockSpec(blk, in_whole),
                      own,
                      pl.BlockSpec((3,) + blk, lambda r, wh: (0, r, 0)), own, own, own],
            out_specs=[own] * 4),
        out_shape=[jax.ShapeDtypeStruct(T.block_shape, F32)] * 4,
        compiler_params=_params("arbitrary"),
    )(where, g, lz1, lz2, w, m, v)


MAT_PIECE = MAT_ROWS // 8
VEC_PIECE = DR // 8


class _AllReduce:
    def __init__(self, items):
        self.items = tuple(items)
        n = len(self.items)
        self.in_specs = [HBM_SPEC] * n
        self.out_specs = [HBM_SPEC] * n
        self.out_shape = [jax.ShapeDtypeStruct(shape, F32) for shape, _ in self.items]
        pieces = [(shape[0] // 8, shape[1]) if axis == 0 else (shape[0], shape[1] // 8) for shape, axis in self.items]
        self.scratch_shapes = ([pltpu.VMEM((8,) + p, F32) for p in pieces] + [pltpu.VMEM(p, F32) for p in pieces] + [
            pltpu.SemaphoreType.DMA((2 * n, 8)), pltpu.SemaphoreType.DMA((2 * n, 8)), pltpu.SemaphoreType.DMA((2 * n,))])

    def middles(self, steps):
        return [(steps // 2, self.middle)]

    def _copies(self, ins, outs, scratch):
        n = len(self.items)
        landed, sums, (send_sems, recv_sems, loc_sems) = scratch[:n], scratch[n:2 * n], scratch[2 * n:]
        x, y, c, _ = _place()
        me = _device_index((x, y), c)

        def peer(r):
            return (1 - x if r & 4 else x, 1 - y if r & 2 else y, 1 - c if r & 1 else c)

        def piece(i, ref, p):
            shape, axis = self.items[i]
            if axis == 0:
                rows = shape[0] // 8
                return ref.at[pl.ds(pl.multiple_of(p * rows, 8), rows), :]
            cols = shape[1] // 8
            return ref.at[:, pl.ds(pl.multiple_of(p * cols, 128), cols)]

        own, scatter, arrivals, keep, spread, late = [], [], [], [], [], []
        for i in range(n):
            own.append(pltpu.make_async_copy(piece(i, ins[i], me), landed[i].at[0], loc_sems.at[2 * i]))
            keep.append(pltpu.make_async_copy(sums[i], piece(i, outs[i], me), loc_sems.at[2 * i + 1]))
            for r in range(1, 8):
                to = peer(r)
                p = _device_index(to[:2], to[2])
                scatter.append(_remote(piece(i, ins[i], p), landed[i].at[r], send_sems, recv_sems, (2 * i, r), to))
                spread.append(_remote(sums[i], piece(i, outs[i], me), send_sems, recv_sems, (2 * i + 1, r), to))
                late.append(_remote(sums[i], piece(i, outs[i], p), send_sems, recv_sems, (2 * i + 1, r), to))
        return own, scatter, keep, spread, late, landed, sums

    def start(self, ins, outs, scratch):
        own, scatter, _, _, _, _, _ = self._copies(ins, outs, scratch)
        for cp in own + scatter:
            cp.start()

    def middle(self, ins, outs, scratch):
        own, scatter, keep, spread, _, landed, sums = self._copies(ins, outs, scratch)
        for cp in own:
            cp.wait()
        for cp in scatter:
            cp.wait_recv()
        for i in range(len(self.items)):
            total = landed[i][0]
            for r in range(1, 8):
                total = total + landed[i][r]
            sums[i][...] = total
        for cp in keep + spread:
            cp.start()

    def finish(self, ins, outs, scratch):
        _, scatter, keep, spread, late, _, _ = self._copies(ins, outs, scratch)
        for cp in late:
            cp.wait_recv()
        for cp in scatter + spread:
            cp.wait_send()
        for cp in keep:
            cp.wait()


class _Both:
    def __init__(self, a, b):
        self.a, self.b = a, b
        self.in_specs, self.out_specs = a.in_specs + b.in_specs, a.out_specs + b.out_specs
        self.out_shape, self.scratch_shapes = a.out_shape + b.out_shape, a.scratch_shapes + b.scratch_shapes

    def _each(self, ins, outs, scratch):
        a = self.a
        i, o, s = len(a.in_specs), len(a.out_specs), len(a.scratch_shapes)
        return (a, ins[:i], outs[:o], scratch[:s]), (self.b, ins[i:], outs[o:], scratch[s:])

    def middles(self, steps):
        def of(which, middle):
            return lambda ins, outs, scratch: middle(*self._each(ins, outs, scratch)[which][1:])
        return [(at, of(which, middle)) for which, e in enumerate((self.a, self.b)) for at, middle in e.middles(steps)]

    def start(self, ins, outs, scratch):
        for e, i, o, s in self._each(ins, outs, scratch):
            e.start(i, o, s)

    def finish(self, ins, outs, scratch):
        for e, i, o, s in self._each(ins, outs, scratch):
            e.finish(i, o, s)


def _all_reduce(arrays, items, name):
    reduce = _AllReduce(items)
    n = len(items)

    def body(*refs):
        ins, outs, scratch = refs[:n], refs[n:2 * n], refs[2 * n:]
        reduce.start(ins, outs, scratch)
        reduce.middle(ins, outs, scratch)
        reduce.finish(ins, outs, scratch)

    return pl.pallas_call(
        body, name=name, in_specs=reduce.in_specs, out_specs=reduce.out_specs, out_shape=reduce.out_shape,
        scratch_shapes=reduce.scratch_shapes,
    )(*arrays)


def _adam_small(grads, wmv):
    n = len(grads)

    def body(*refs):
        g_refs, rest = refs[:n], refs[n:]
        ins, outs = rest[:3 * n], rest[3 * n:]
        for i in range(n):
            d, m, v = _adamw(ins[3 * i][...], g_refs[i][...], ins[3 * i + 1][...], ins[3 * i + 2][...])
            outs[3 * i][...], outs[3 * i + 1][...], outs[3 * i + 2][...] = d, m, v

    flat = [a for t in wmv for a in t]
    return pl.pallas_call(
        body, name="adam_small",
        in_specs=[VMEM_SPEC] * (4 * n), out_specs=[VMEM_SPEC] * (3 * n),
        out_shape=[jax.ShapeDtypeStruct(a.shape, F32) for a in flat],
    )(*grads, *flat)


WEIGHT_NAMES = ("norm_mix", "w_in", "w_pool_grp", "pool_scale", "w_pool_out", "conv_w", "conv_b", "w_rg_a", "b_rg_a", "w_rg_x",
                "b_rg_x", "lru_lambda", "w_rnn_out", "w_o", "norm_ffn", "w_ffn_in", "w_ffn_out", "norm_final")


def kernel(x, norm_mix, w_in, w_pool_grp, pool_scale, w_pool_out, conv_w, conv_b, w_rg_a, b_rg_a, w_rg_x, b_rg_x, lru_lambda, w_rnn_out, w_o, norm_ffn, w_ffn_in, w_ffn_out, norm_final, loss_target, m_norm_mix, m_w_in, m_w_pool_grp, m_pool_scale, m_w_pool_out, m_conv_w, m_conv_b, m_w_rg_a, m_b_rg_a, m_w_rg_x, m_b_rg_x, m_lru_lambda, m_w_rnn_out, m_w_o, m_norm_ffn, m_w_ffn_in, m_w_ffn_out, m_norm_final, v_norm_mix, v_w_in, v_w_pool_grp, v_pool_scale, v_w_pool_out, v_conv_w, v_conv_b, v_w_rg_a, v_b_rg_a, v_w_rg_x, v_b_rg_x, v_lru_lambda, v_w_rnn_out, v_w_o, v_norm_ffn, v_w_ffn_in, v_w_ffn_out, v_norm_final):
    w = dict(norm_mix=norm_mix, w_in=w_in, w_pool_grp=w_pool_grp, pool_scale=pool_scale, w_pool_out=w_pool_out, conv_w=conv_w,
             conv_b=conv_b, w_rg_a=w_rg_a, b_rg_a=b_rg_a, w_rg_x=w_rg_x, b_rg_x=b_rg_x, lru_lambda=lru_lambda,
             w_rnn_out=w_rnn_out, w_o=w_o, norm_ffn=norm_ffn, w_ffn_in=w_ffn_in, w_ffn_out=w_ffn_out, norm_final=norm_final)
    m = dict(norm_mix=m_norm_mix, w_in=m_w_in, w_pool_grp=m_w_pool_grp, pool_scale=m_pool_scale, w_pool_out=m_w_pool_out,
             conv_w=m_conv_w, conv_b=m_conv_b, w_rg_a=m_w_rg_a, b_rg_a=m_b_rg_a, w_rg_x=m_w_rg_x, b_rg_x=m_b_rg_x,
             lru_lambda=m_lru_lambda, w_rnn_out=m_w_rnn_out, w_o=m_w_o, norm_ffn=m_norm_ffn, w_ffn_in=m_w_ffn_in,
             w_ffn_out=m_w_ffn_out, norm_final=m_norm_final)
    v = dict(norm_mix=v_norm_mix, w_in=v_w_in, w_pool_grp=v_w_pool_grp, pool_scale=v_pool_scale, w_pool_out=v_w_pool_out,
             conv_w=v_conv_w, conv_b=v_conv_b, w_rg_a=v_w_rg_a, b_rg_a=v_b_rg_a, w_rg_x=v_w_rg_x, b_rg_x=v_b_rg_x,
             lru_lambda=v_lru_lambda, w_rnn_out=v_w_rnn_out, w_o=v_w_o, norm_ffn=v_norm_ffn, w_ffn_in=v_w_ffn_in,
             w_ffn_out=v_w_ffn_out, norm_final=v_norm_final)
    xi, yi, ci = (lax.axis_index(a) for a in MESH_AXES)
    chip = 2 * xi + yi

    def held(T, a):
        return jnp.swapaxes(a, 0, 1) if T.transposed else a

    where = jnp.stack([2 * chip + ci]).astype(jnp.int32)
    by_name = {T.name: T for T in GATHERED}
    block = {T.name: held(T, w[T.name][0]) for T in BIG}
    block["conv_w"] = jnp.pad(conv_w[0], ((0, CONV_W.rows - 4), (0, 0)))
    block["w_ffn_in_lo"] = block["w_ffn_in_hi"] = block["w_ffn_in"]

    def gather_of(*names):
        return dict(exchange=_Gather([by_name[n] for n in names]), exchange_operands=[block[n] for n in names])

    def pair_sums(names, partials, tag):
        out = _pair_reduce(partials, [by_name[n] for n in names], "grad_pair_reduce_" + tag)
        return list(out[:len(names)]), list(out[len(names):])

    xs, target = x[0], loss_target[0]
    wg_b, wa_b, wx_b = (a[0].astype(BF16) for a in (w_pool_grp, w_rg_a, w_rg_x))
    ba2, bx2 = b_rg_a.reshape(1, DR), b_rg_x.reshape(1, DR)
    first = ("w_in", "w_pool_out", "w_rnn_out", "conv_w")
    order = jnp.stack([chip, 2 * (1 - xi) + yi, 2 * xi + (1 - yi), 2 * (1 - xi) + (1 - yi)]).astype(jnp.int32)
    (proj, h1), (w_in_g, w_pool_out_g, w_rnn_out_g, conv_g) = _in_proj_gather(
        xs, norm_mix, [block[n] for n in first], [by_name[n] for n in first], order)
    mixer_weights = (wg_b, pool_scale, w_pool_out_g, conv_g[0:4], conv_b, wa_b, ba2, wx_b, bx2, lru_lambda, w_rnn_out_g)
    (_, pm, y_pool, hr, z, y_rnn, kept), (w_o_g, w_ffn_hi_g) = _mixer_fwd(
        proj, *mixer_weights, **gather_of("w_o", "w_ffn_in_hi"))
    (mix, x2, h2), (w_ffn_lo_g,) = _merge_out(xs, proj, y_pool, y_rnn, w_o_g, norm_ffn, **gather_of("w_ffn_in_lo"))
    (gu, act), (w_ffn_out_g,) = _ffn_up(h2, w_ffn_lo_g, w_ffn_hi_g, **gather_of("w_ffn_out"))
    dx3, dx3b, loss_part, dvec_fin = _ffn_down_loss(act, x2, target, w_ffn_out_g, norm_final.reshape(1, D))

    dgu = _ffn_bwd_down(dx3b, gu, w_ffn_out_g)
    dx2, dx2b, dmixo, dvec_ffn = _ffn_bwd_up(dgu, x2, dx3, w_ffn_lo_g, w_ffn_hi_g, norm_ffn, w_o_g)
    names_a = ("w_ffn_in", "w_ffn_out", "w_o")
    part_a = [_wgrad(dgu, h2, "wgrad_ffn_in", 1408, 512), _wgrad(act, dx3b, "wgrad_ffn_out", 1408, 512),
              _wgrad(mix, dx2b, "wgrad_o", 1024, 1024)]
    lz1_a, sums_a = pair_sums(names_a, part_a, "ffn")
    (dproj, dypb, dyrb, dmat, dvec_mix), lz2_a = _mixer_bwd(
        proj, dmixo, y_pool, y_rnn, hr, kept, *mixer_weights,
        exchange=_Scatter([by_name[n] for n in names_a]), exchange_operands=sums_a)
    names_b = ("w_pool_out", "w_rnn_out")
    part_b = [_wgrad(pm, dypb, "wgrad_pool_out", 512, 1024), _wgrad(z, dyrb, "wgrad_rnn_out", 1024, 1024)]
    lz1_b, sums_b = pair_sums(names_b, part_b, "mix")
    dvec = jnp.concatenate([dvec_mix[0:9], dvec_fin[0:1], dvec_ffn[0:1], jnp.pad(loss_part, ((0, 0), (0, DR - 1))),
                            jnp.zeros((VEC_ROWS - 12, DR), F32)], axis=0)
    g_in, exchanged = _wgrad(
        dproj, h1, "wgrad_in", 1152, 1024,
        exchange=_Both(_Scatter([by_name[n] for n in names_b]), _AllReduce([((MAT_ROWS, HD), 0), ((VEC_ROWS, DR), 1)])),
        exchange_operands=sums_b + [dmat, dvec])
    lz2_b, (mat, vec) = exchanged[:2], exchanged[2:]
    loss = vec[VEC_LOSS, 0]
    lz1_c, sums_c = pair_sums(("w_in",), [g_in], "in")
    (grad_x, dvec_in), lz2_c = _in_bwd(dproj, xs, dx2, norm_mix, w_in_g,
                                       exchange=_Scatter([by_name["w_in"]]), exchange_operands=sums_c)
    (vec_in,) = _all_reduce([dvec_in], [((8, D), 1)], "all_reduce_norm_mix")

    grads, delta, new_m, new_v = {}, {}, {}, {}
    for n, g, l1, l2 in zip(names_a + names_b + ("w_in",), part_a + part_b + [g_in], lz1_a + lz1_b + lz1_c,
                            lz2_a + lz2_b + lz2_c):
        T = by_name[n]
        out = _final_sum(T, g, l1, l2, where, held(T, w[n][0]), held(T, m[n][0]), held(T, v[n][0]))
        grads[n], delta[n], new_m[n], new_v[n] = (held(T, a) for a in out)
    me = 4 * xi + 2 * yi + ci
    small_grads = dict(
        w_pool_grp=mat[0:MAT_WA], w_rg_a=mat[MAT_WA:MAT_WX], w_rg_x=mat[MAT_WX:MAT_ROWS],
        pool_scale=vec[VEC_SCALE:VEC_SCALE + 1, 0:DP], conv_b=vec[VEC_CONV_B:VEC_CONV_B + 1],
        b_rg_a=vec[VEC_BA:VEC_BA + 1], b_rg_x=vec[VEC_BX:VEC_BX + 1], lru_lambda=vec[VEC_LAM:VEC_LAM + 1],
        conv_w=lax.dynamic_slice(vec, (VEC_CONV_W, VEC_PIECE * me), (4, VEC_PIECE)),
        norm_final=vec[VEC_NORM_FINAL:VEC_NORM_FINAL + 1], norm_ffn=vec[VEC_NORM_FFN:VEC_NORM_FFN + 1],
        norm_mix=vec_in[0:1])
    names = list(small_grads)
    as2d = lambda a, g: a.reshape(g.shape)
    upd = _adam_small([small_grads[n] for n in names],
                      [(as2d(w[n], small_grads[n]), as2d(m[n], small_grads[n]), as2d(v[n], small_grads[n])) for n in names])
    for i, n in enumerate(names):
        grads[n] = small_grads[n]
        delta[n], new_m[n], new_v[n] = upd[3 * i:3 * i + 3]

    shaped = lambda d: [d[n].reshape(w[n].shape) for n in WEIGHT_NAMES]
    return (loss, grad_x[None], *shaped(grads), *shaped(delta), *shaped(new_m), *shaped(new_v))
```

```python
import math

import jax
import jax.numpy as jnp
from jax import lax
from jax.experimental import pallas as pl
from jax.experimental.pallas import tpu as pltpu

F32 = jnp.float32
BF16 = jnp.bfloat16

D = 1024
DP = 512
PG = 128
WINDOWS = (2, 4, 8, 16)
DR = 1024
NH = 8
HD = 128
DIN = 4608
DFF = 2816
EPS = 1e-6
LRU_C = 8.0
POOL_HALO = 16
CONV_HALO = 8
KEPT = 5

ADAM_LR = 0.001
ADAM_B1 = 0.9
ADAM_B2 = 0.999
ADAM_EPS = 1e-08
ADAM_WD = 0.01
ADAM_STEP = 10

VMEM_LIMIT = 56 * 1024 * 1024
MESH_AXES = ("x", "y", "c")
MESH = pl.DeviceIdType.MESH


def _dot(a, b):
    return jnp.dot(a, b, preferred_element_type=F32)


def _dot_nt(a, b):
    return lax.dot_general(a, b, (((1,), (1,)), ((), ())), preferred_element_type=F32)


def _dot_tn(a, b):
    return lax.dot_general(a, b, (((0,), (0,)), ((), ())), preferred_element_type=F32)


def _params(*sem):
    return pltpu.CompilerParams(dimension_semantics=sem, vmem_limit_bytes=VMEM_LIMIT)


def _resident(shape):
    nd = len(shape)
    return pl.BlockSpec(shape, lambda i: (0,) * nd, pipeline_mode=pl.Buffered(1))


def _rows(shape_cols, tm):
    return pl.BlockSpec((tm, shape_cols), lambda i: (i, 0))


def _call(body, name, grid, in_specs, out_specs, out_shape, operands, scratch_shapes=(), exchange=None, exchange_operands=()):
    n_in, n_out, n_scr = len(in_specs), len(out_specs), len(scratch_shapes)
    steps = math.prod(grid)
    if exchange is None:
        outs = pl.pallas_call(body, name=name, grid=grid, in_specs=in_specs, out_specs=out_specs, out_shape=out_shape,
                              scratch_shapes=list(scratch_shapes), compiler_params=_params(*["arbitrary"] * len(grid)))(*operands)
        return outs, []
    e_in, e_out = len(exchange.in_specs), len(exchange.out_specs)

    def hosted(*refs):
        ins, refs = refs[:n_in], refs[n_in:]
        e_ins, refs = refs[:e_in], refs[e_in:]
        outs, refs = refs[:n_out], refs[n_out:]
        e_outs, refs = refs[:e_out], refs[e_out:]
        scr, e_scr = refs[:n_scr], refs[n_scr:]
        step = pl.program_id(0)
        for axis in range(1, len(grid)):
            step = step * grid[axis] + pl.program_id(axis)
        pl.when(step == 0)(lambda: exchange.start(e_ins, e_outs, e_scr))
        for at, middle in exchange.middles(steps):
            pl.when(step == at)(lambda middle=middle: middle(e_ins, e_outs, e_scr))
        body(*ins, *outs, *scr)
        pl.when(step == steps - 1)(lambda: exchange.finish(e_ins, e_outs, e_scr))

    outs = pl.pallas_call(
        hosted, name=name, grid=grid, in_specs=list(in_specs) + exchange.in_specs,
        out_specs=list(out_specs) + exchange.out_specs, out_shape=list(out_shape) + exchange.out_shape,
        scratch_shapes=list(scratch_shapes) + exchange.scratch_shapes,
        compiler_params=_params(*["arbitrary"] * len(grid)))(*operands, *exchange_operands)
    return outs[:n_out], outs[n_out:]


GELU_C = math.sqrt(2.0 / math.pi)
GELU_K = 0.044715 * GELU_C


def _gelu(x, with_grad=False):
    x2 = x * x
    t = jnp.tanh(x * (GELU_C + GELU_K * x2))
    hx = 0.5 * x
    y = hx + hx * t
    if not with_grad:
        return y
    return y, 0.5 + 0.5 * t + hx * (1.0 - t * t) * (GELU_C + (3.0 * GELU_K) * x2)


def _softplus_neg(lam):
    z = jnp.exp(-jnp.abs(lam))
    u = 1.0 + z
    dlt = u - 1.0
    log1p = jnp.where(dlt == 0.0, z, jnp.log(u) * (z / jnp.where(dlt == 0.0, 1.0, dlt)))
    return jnp.maximum(-lam, 0.0) + log1p


def _sigmoid(x):
    return 0.5 * jnp.tanh(0.5 * x) + 0.5


def _linear_scan(out_ref, A, B, h0, reverse):
    n = A.shape[0]
    sub = lax.broadcasted_iota(jnp.int32, (8, 1), 0)
    tiles = range(n // 8 - 1, -1, -1) if reverse else range(n // 8)
    carry = h0
    for j in tiles:
        a, b = A[8 * j:8 * j + 8, :], B[8 * j:8 * j + 8, :]
        for d in (1, 2, 4):
            keep = (sub < 8 - d) if reverse else (sub >= d)
            shift = 8 - d if reverse else d
            b = jnp.where(keep, a * pltpu.roll(b, shift, axis=0) + b, b)
            a = jnp.where(keep, a * pltpu.roll(a, shift, axis=0), a)
        h = a * carry + b
        out_ref[8 * j:8 * j + 8, :] = h
        carry = h[0:1, :] if reverse else h[7:8, :]
    return carry


def _pool_windows(ext, shift_sign):
    n = ext.shape[0]
    s = ext
    outs = []
    for w in WINDOWS:
        d = w // 2
        s = s + pltpu.roll(s, d if shift_sign > 0 else n - d, axis=0)
        outs.append(s[:, :PG])
        s = s[:, PG:]
    return outs


def _conv_taps(uext):
    taps = []
    for k in range(4):
        sh = 3 - k
        v = uext if sh == 0 else pltpu.roll(uext, sh, axis=0)
        taps.append(v[CONV_HALO:, :])
    return taps


def _gates(v, wa_ref, ba_ref, wx_ref, bx_ref, sp):
    vb = v.astype(BF16)
    ra, rx = [], []
    for h in range(NH):
        vh = vb[:, h * HD:(h + 1) * HD]
        ra.append(_dot(vh, wa_ref[h]))
        rx.append(_dot(vh, wx_ref[h]))
    r = _sigmoid(jnp.concatenate(ra, axis=1) + ba_ref[...])
    i = _sigmoid(jnp.concatenate(rx, axis=1) + bx_ref[...])
    log_a = r * ((-LRU_C) * sp)
    a = jnp.exp(log_a)
    one_minus = -jnp.tanh(log_a) * (1.0 + a * a)
    return r, i, a, jnp.sqrt(one_minus), lax.rsqrt(one_minus)


def _mixer_fwd(proj, wg, scale, w_pool_out, conv_w, conv_b, wa, ba, wx, bx, lam, w_rnn_out, exchange=None,
               exchange_operands=(), tm=256):
    S = proj.shape[0]
    UW = DP + 2 * DR

    def body(proj_ref, wg_ref, scale_ref, wpo_ref, cw_ref, cb_ref, wa_ref, ba_ref, wx_ref, bx_ref, lam_ref, wro_ref,
             pm_ref, ypool_ref, hr_ref, z_ref, yrnn_ref, kept_ref, pool_carry, conv_carry, h_carry):
        i = pl.program_id(0)

        @pl.when(i == 0)
        def _():
            pool_carry[...] = jnp.zeros_like(pool_carry)
            conv_carry[...] = jnp.zeros_like(conv_carry)
            h_carry[...] = jnp.zeros_like(h_carry)

        rows = lax.broadcasted_iota(jnp.int32, (tm, 1), 0)
        t_glob = i * tm + rows

        u_pool = proj_ref[:, 0:DP]
        ext = jnp.concatenate([pool_carry[...], u_pool], axis=0)
        pool_carry[...] = u_pool[tm - POOL_HALO:, :]
        sums = _pool_windows(ext, +1)
        mixed = []
        for g, w in enumerate(WINDOWS):
            inv_cnt = 1.0 / jnp.minimum(t_glob + 1, w).astype(F32)
            pooled_g = sums[g][POOL_HALO:, :] * inv_cnt - u_pool[:, g * PG:(g + 1) * PG]
            mixed.append(_dot(pooled_g.astype(BF16), wg_ref[g]))
        pm = (jnp.concatenate(mixed, axis=1) * scale_ref[...]).astype(BF16)
        pm_ref[...] = pm
        ypool_ref[...] = _dot(pm, wpo_ref[...])

        u_rnn = proj_ref[:, DP:DP + DR]
        uext = jnp.concatenate([conv_carry[...], u_rnn], axis=0)
        conv_carry[...] = u_rnn[tm - CONV_HALO:, :]
        taps = _conv_taps(uext)
        v = cb_ref[...]
        for k in range(4):
            v = v + taps[k] * cw_ref[k:k + 1, :]
        sp = _softplus_neg(lam_ref[...])
        r, gi, a, mult, _ = _gates(v, wa_ref, ba_ref, wx_ref, bx_ref, sp)
        for k, kept in enumerate((v, r, gi, a, mult)):
            kept_ref[k] = kept
        h_carry[0:1, :] = _linear_scan(hr_ref, a, mult * gi * v, h_carry[0:1, :], reverse=False)
        z = (hr_ref[...] * _gelu(proj_ref[:, DP + DR:UW])).astype(BF16)
        z_ref[...] = z
        yrnn_ref[...] = _dot(z, wro_ref[...])

    return _call(
        body, "mixer_fwd", (S // tm,),
        in_specs=[_rows(UW, tm), _resident((4, PG, PG)), _resident((1, DP)), _resident((DP, D)), _resident((4, DR)),
                  _resident((1, DR)), _resident((NH, HD, HD)), _resident((1, DR)), _resident((NH, HD, HD)),
                  _resident((1, DR)), _resident((1, DR)), _resident((DR, D))],
        out_specs=[_rows(DP, tm), _rows(D, tm), _rows(DR, tm), _rows(DR, tm), _rows(D, tm),
                   pl.BlockSpec((KEPT, tm, DR), lambda i: (0, i, 0))],
        out_shape=[jax.ShapeDtypeStruct((S, DP), BF16),
                   jax.ShapeDtypeStruct((S, D), F32), jax.ShapeDtypeStruct((S, DR), F32),
                   jax.ShapeDtypeStruct((S, DR), BF16), jax.ShapeDtypeStruct((S, D), F32),
                   jax.ShapeDtypeStruct((KEPT, S, DR), F32)],
        scratch_shapes=[pltpu.VMEM((POOL_HALO, DP), F32), pltpu.VMEM((CONV_HALO, DR), F32), pltpu.VMEM((8, DR), F32)],
        operands=(proj, wg, scale, w_pool_out, conv_w, conv_b, wa, ba, wx, bx, lam, w_rnn_out),
        exchange=exchange, exchange_operands=exchange_operands)


FF_CHUNKS = ((0, 768), (768, 1536), (1536, 2304), (2304, DFF))


def _rms(x):
    r = lax.rsqrt(jnp.mean(x * x, axis=-1, keepdims=True) + EPS)
    return r, x * r


def _rms_bwd(dh, g, r, xh):
    dxh = dh * g
    return r * (dxh - xh * jnp.mean(dxh * xh, axis=-1, keepdims=True))


def _merge_out(x, proj, y_pool, y_rnn, w_o, norm_ffn, exchange=None, exchange_operands=(), tm=512):
    S = x.shape[0]
    GL0 = (DP + 2 * DR) // 512

    def gl_spec(k):
        return pl.BlockSpec((tm, 512), lambda i: (i, GL0 + k))

    def body(x_ref, gl0, gl1, gl2, gl3, yp_ref, yr_ref, wo_ref, gf_ref, mix_ref, x2_ref, h2_ref):
        s_p = _sigmoid(jnp.concatenate([gl0[...], gl1[...]], axis=1))
        s_r = _sigmoid(jnp.concatenate([gl2[...], gl3[...]], axis=1))
        mix = (s_p * yp_ref[...] + s_r * yr_ref[...]).astype(BF16)
        mix_ref[...] = mix
        x2 = x_ref[...] + _dot(mix, wo_ref[...])
        x2_ref[...] = x2
        _, xh2 = _rms(x2)
        h2_ref[...] = (xh2 * gf_ref[...]).astype(BF16)

    return _call(
        body, "merge_out", (S // tm,),
        in_specs=[_rows(D, tm), gl_spec(0), gl_spec(1), gl_spec(2), gl_spec(3), _rows(D, tm), _rows(D, tm),
                  _resident((D, D)), _resident((1, D))],
        out_specs=[_rows(D, tm), _rows(D, tm), _rows(D, tm)],
        out_shape=[jax.ShapeDtypeStruct((S, D), BF16), jax.ShapeDtypeStruct((S, D), F32), jax.ShapeDtypeStruct((S, D), BF16)],
        operands=(x, proj, proj, proj, proj, y_pool, y_rnn, w_o, norm_ffn),
        exchange=exchange, exchange_operands=exchange_operands)


def _ffn_up(h2, w_lo, w_hi, exchange=None, exchange_operands=(), tm=512):
    S = h2.shape[0]
    HALF = D // 2

    def body(h_ref, lo_ref, hi_ref, gu_ref, act_ref):
        h_lo, h_hi = h_ref[:, 0:HALF], h_ref[:, HALF:D]
        for c0, c1 in FF_CHUNKS:
            gate = _dot_nt(h_lo, lo_ref[c0:c1, :]) + _dot_nt(h_hi, hi_ref[c0:c1, :])
            up = _dot_nt(h_lo, lo_ref[DFF + c0:DFF + c1, :]) + _dot_nt(h_hi, hi_ref[DFF + c0:DFF + c1, :])
            gu_ref[:, c0:c1] = gate.astype(BF16)
            gu_ref[:, DFF + c0:DFF + c1] = up.astype(BF16)
            act_ref[:, c0:c1] = (gate * _sigmoid(gate) * up).astype(BF16)

    return _call(
        body, "ffn_up", (S // tm,),
        in_specs=[_rows(D, tm), _resident((2 * DFF, HALF)), _resident((2 * DFF, HALF))],
        out_specs=[_rows(2 * DFF, tm), _rows(DFF, tm)],
        out_shape=[jax.ShapeDtypeStruct((S, 2 * DFF), BF16), jax.ShapeDtypeStruct((S, DFF), BF16)],
        operands=(h2, w_lo, w_hi), exchange=exchange, exchange_operands=exchange_operands)


def _ffn_down_loss(act, x2, target, w_ffn_out, norm_final, tm=512):
    S = act.shape[0]

    def body(act_ref, x2_ref, t_ref, w_ref, gn_ref, dx3_ref, dx3b_ref, loss_ref, dvec_ref):
        i = pl.program_id(0)

        @pl.when(i == 0)
        def _():
            loss_ref[...] = jnp.zeros_like(loss_ref)
            dvec_ref[...] = jnp.zeros_like(dvec_ref)

        x3 = x2_ref[...] + _dot(act_ref[...], w_ref[...])
        r3, xh3 = _rms(x3)
        g_fin = gn_ref[...]
        e = xh3 * g_fin - t_ref[...]
        loss_ref[...] += jnp.sum(e * e, axis=(0, 1), keepdims=True) * (0.5 / D)
        dy = e * (1.0 / D)
        dvec_ref[0:1, :] += jnp.sum(dy * xh3, axis=0, keepdims=True)
        dx3 = _rms_bwd(dy, g_fin, r3, xh3)
        dx3_ref[...] = dx3
        dx3b_ref[...] = dx3.astype(BF16)

    return pl.pallas_call(
        body, name="ffn_down_loss", grid=(S // tm,),
        in_specs=[_rows(DFF, tm), _rows(D, tm), _rows(D, tm), _resident((DFF, D)), _resident((1, D))],
        out_specs=[_rows(D, tm), _rows(D, tm), _resident((1, 1)), _resident((8, D))],
        out_shape=[jax.ShapeDtypeStruct((S, D), F32), jax.ShapeDtypeStruct((S, D), BF16),
                   jax.ShapeDtypeStruct((1, 1), F32), jax.ShapeDtypeStruct((8, D), F32)],
        compiler_params=_params("arbitrary"),
    )(act, x2, target, w_ffn_out, norm_final)


def _ffn_bwd_down(dx3b, gu, w_ffn_out, tm=512):
    S = dx3b.shape[0]

    def body(d_ref, gu_ref, w_ref, dgu_ref):
        d = d_ref[...]
        for c0, c1 in FF_CHUNKS:
            dact = _dot_nt(d, w_ref[c0:c1, :])
            gate = gu_ref[:, c0:c1].astype(F32)
            up = gu_ref[:, DFF + c0:DFF + c1].astype(F32)
            sg = _sigmoid(gate)
            dgu_ref[:, c0:c1] = (dact * up * (sg * (1.0 + gate * (1.0 - sg)))).astype(BF16)
            dgu_ref[:, DFF + c0:DFF + c1] = (dact * (gate * sg)).astype(BF16)

    return pl.pallas_call(
        body, name="ffn_bwd_down", grid=(S // tm,),
        in_specs=[_rows(D, tm), _rows(2 * DFF, tm), _resident((DFF, D))],
        out_specs=_rows(2 * DFF, tm),
        out_shape=jax.ShapeDtypeStruct((S, 2 * DFF), BF16),
        compiler_params=_params("parallel"),
    )(dx3b, gu, w_ffn_out)


def _ffn_bwd_up(dgu, x2, dx3, w_lo, w_hi, norm_ffn, w_o, tm=512):
    S = dgu.shape[0]
    HALF = D // 2

    def body(dgu_ref, x2_ref, dx3_ref, lo_ref, hi_ref, gf_ref, wo_ref, dx2_ref, dx2b_ref, dmixo_ref, dvec_ref):
        i = pl.program_id(0)

        @pl.when(i == 0)
        def _():
            dvec_ref[...] = jnp.zeros_like(dvec_ref)

        dgate, dup = dgu_ref[:, 0:DFF], dgu_ref[:, DFF:2 * DFF]
        dh2 = jnp.concatenate([_dot(dgate, w[0:DFF, :]) + _dot(dup, w[DFF:2 * DFF, :]) for w in (lo_ref, hi_ref)], axis=1)
        r2, xh2 = _rms(x2_ref[...])
        dvec_ref[0:1, :] += jnp.sum(dh2 * xh2, axis=0, keepdims=True)
        dx2 = dx3_ref[...] + _rms_bwd(dh2, gf_ref[...], r2, xh2)
        dx2_ref[...] = dx2
        dx2b = dx2.astype(BF16)
        dx2b_ref[...] = dx2b
        dmixo_ref[...] = _dot_nt(dx2b, wo_ref[...])

    return pl.pallas_call(
        body, name="ffn_bwd_up", grid=(S // tm,),
        in_specs=[_rows(2 * DFF, tm), _rows(D, tm), _rows(D, tm), _resident((2 * DFF, HALF)), _resident((2 * DFF, HALF)),
                  _resident((1, D)), _resident((D, D))],
        out_specs=[_rows(D, tm), _rows(D, tm), _rows(D, tm), _resident((8, D))],
        out_shape=[jax.ShapeDtypeStruct((S, D), F32), jax.ShapeDtypeStruct((S, D), BF16), jax.ShapeDtypeStruct((S, D), F32),
                   jax.ShapeDtypeStruct((8, D), F32)],
        compiler_params=_params("arbitrary"),
    )(dgu, x2, dx3, w_lo, w_hi, norm_ffn, w_o)


VEC_ROWS = 16
MAT_WA = 4 * PG
MAT_WX = MAT_WA + NH * HD
MAT_ROWS = MAT_WX + NH * HD


def _mixer_bwd(proj, dmixo, y_pool, y_rnn, hr, kept, wg, scale, w_pool_out, conv_w, conv_b, wa, ba, wx, bx, lam, w_rnn_out,
               exchange=None, exchange_operands=(), tm=256):
    S = proj.shape[0]
    nt = S // tm

    def rev(cols):
        return pl.BlockSpec((tm, cols), lambda i: (nt - 1 - i, 0))

    def halo(rows_, cols):
        per = tm // rows_
        return pl.BlockSpec((rows_, cols), lambda i: (jnp.maximum((nt - 1 - i) * per - 1, 0), 0))

    def body(proj_ref, projh_ref, dmixo_ref, yp_ref, yr_ref, hr_ref, hrh_ref, kept_ref, wg_ref, scale_ref, wpo_ref, cw_ref, cb_ref,
             wa_ref, ba_ref, wx_ref, bx_ref, lam_ref, wro_ref,
             dproj_ref, dypb_ref, dyrb_ref, dmat_ref, dvec_ref,
             q_carry, dv_carry, a_carry, g_carry, g_scr):
        i = pl.program_id(0)
        ti = nt - 1 - i

        @pl.when(i == 0)
        def _():
            q_carry[...] = jnp.zeros_like(q_carry)
            dv_carry[...] = jnp.zeros_like(dv_carry)
            a_carry[...] = jnp.zeros_like(a_carry)
            g_carry[...] = jnp.zeros_like(g_carry)
            dmat_ref[...] = jnp.zeros_like(dmat_ref)
            dvec_ref[...] = jnp.zeros_like(dvec_ref)

        rows = lax.broadcasted_iota(jnp.int32, (tm, 1), 0)
        t_glob = ti * tm + rows
        has_prev = (ti > 0).astype(F32)
        dmixo = dmixo_ref[...]

        s_p = _sigmoid(proj_ref[:, DP + 2 * DR:DP + 2 * DR + D])
        s_r = _sigmoid(proj_ref[:, DP + 2 * DR + D:DIN])
        dproj_ref[:, DP + 2 * DR:DP + 2 * DR + D] = (dmixo * yp_ref[...] * s_p * (1.0 - s_p)).astype(BF16)
        dproj_ref[:, DP + 2 * DR + D:DIN] = (dmixo * yr_ref[...] * s_r * (1.0 - s_r)).astype(BF16)
        dyp = (dmixo * s_p).astype(BF16)
        dyr = (dmixo * s_r).astype(BF16)
        dypb_ref[...] = dyp
        dyrb_ref[...] = dyr

        dz = _dot_nt(dyr, wro_ref[...])
        u_gate = proj_ref[:, DP + DR:DP + 2 * DR]
        gg, dgelu = _gelu(u_gate, with_grad=True)
        hr_t = hr_ref[...]
        dproj_ref[:, DP + DR:DP + 2 * DR] = (dz * hr_t * dgelu).astype(BF16)
        dhr = dz * gg

        sp = _softplus_neg(lam_ref[...])
        v, r, gi, a, mult = (kept_ref[k] for k in range(KEPT))
        inv_mult = 1.0 / mult

        C = jnp.where(rows == tm - 1, a_carry[0:1, :], pltpu.roll(a, tm - 1, axis=0))
        g_carry[0:1, :] = _linear_scan(g_scr, C, dhr, g_carry[0:1, :], reverse=True)
        a_carry[0:1, :] = a[0:1, :]
        g = g_scr[...]

        h_prev = jnp.where(rows == 0, hrh_ref[7:8, :] * has_prev, pltpu.roll(hr_t, 1, axis=0))
        da = g * h_prev
        gm = g * mult
        dmult = g * gi * v
        di = gm * v
        dv = gm * gi
        dlog_a = da * a - dmult * (a * a * inv_mult)
        dvec_ref[4:5, :] += jnp.sum(dlog_a * r, axis=0, keepdims=True)
        dra = (dlog_a * ((-LRU_C) * sp) * r * (1.0 - r))
        drx = di * gi * (1.0 - gi)
        dvec_ref[2:3, :] += jnp.sum(dra, axis=0, keepdims=True)
        dvec_ref[3:4, :] += jnp.sum(drx, axis=0, keepdims=True)
        drab = dra.astype(BF16)
        drxb = drx.astype(BF16)
        vb = v.astype(BF16)
        dvg = []
        for h in range(NH):
            sl = slice(h * HD, (h + 1) * HD)
            dvg.append(_dot_nt(drab[:, sl], wa_ref[h]) + _dot_nt(drxb[:, sl], wx_ref[h]))
            dmat_ref[MAT_WA + h * HD:MAT_WA + (h + 1) * HD, :] += _dot_tn(vb[:, sl], drab[:, sl])
            dmat_ref[MAT_WX + h * HD:MAT_WX + (h + 1) * HD, :] += _dot_tn(vb[:, sl], drxb[:, sl])
        dv = dv + jnp.concatenate(dvg, axis=1)
        dvec_ref[1:2, :] += jnp.sum(dv, axis=0, keepdims=True)
        dvext = jnp.concatenate([dv, dv_carry[...]], axis=0)
        dv_carry[...] = dv[0:CONV_HALO, :]
        n = tm + CONV_HALO
        u_rnn = proj_ref[:, DP:DP + DR]
        du_rnn = dv * cw_ref[3:4, :]
        dvec_ref[8:9, :] += jnp.sum(dv * u_rnn, axis=0, keepdims=True)
        for k in range(3):
            dv_k = pltpu.roll(dvext, n - (3 - k), axis=0)[0:tm, :]
            du_rnn = du_rnn + dv_k * cw_ref[k:k + 1, :]
            dvec_ref[5 + k:6 + k, :] += jnp.sum(dv_k * u_rnn, axis=0, keepdims=True)
        dproj_ref[:, DP:DP + DR] = du_rnn.astype(BF16)

        dpm = _dot_nt(dyp, wpo_ref[...])
        u_pool = proj_ref[:, 0:DP]
        ext = jnp.concatenate([projh_ref[:, 0:DP] * has_prev, u_pool], axis=0)
        sums = _pool_windows(ext, +1)
        scale_v = scale_ref[...]
        qs = []
        dpooled = []
        dscale = []
        for gi_, w in enumerate(WINDOWS):
            sl = slice(gi_ * PG, (gi_ + 1) * PG)
            inv_cnt = 1.0 / jnp.minimum(t_glob + 1, w).astype(F32)
            pooled_b = (sums[gi_][POOL_HALO:, :] * inv_cnt - u_pool[:, sl]).astype(BF16)
            mixed_g = _dot(pooled_b, wg_ref[gi_])
            dscale.append(jnp.sum(dpm[:, sl] * mixed_g, axis=0, keepdims=True))
            dmixed_b = (dpm[:, sl] * scale_v[:, sl]).astype(BF16)
            dmat_ref[gi_ * PG:(gi_ + 1) * PG, :] += _dot_tn(pooled_b, dmixed_b)
            dp_g = _dot_nt(dmixed_b, wg_ref[gi_])
            dpooled.append(dp_g)
            qs.append(dp_g * inv_cnt)
        dvec_ref[0:1, 0:DP] += jnp.concatenate(dscale, axis=1)
        q = jnp.concatenate(qs, axis=1)
        qext = jnp.concatenate([q, q_carry[...]], axis=0)
        q_carry[...] = q[0:POOL_HALO, :]
        tsum = _pool_windows(qext, -1)
        for gi_ in range(4):
            dproj_ref[:, gi_ * PG:(gi_ + 1) * PG] = (tsum[gi_][0:tm, :] - dpooled[gi_]).astype(BF16)

        @pl.when(i == nt - 1)
        def _():
            dvec_ref[4:5, :] = dvec_ref[4:5, :] * (LRU_C * _sigmoid(-lam_ref[...]))

    return _call(
        body, "mixer_bwd", (nt,),
        in_specs=[rev(DIN), halo(POOL_HALO, DIN), rev(D), rev(D), rev(D), rev(DR), halo(8, DR),
                  pl.BlockSpec((KEPT, tm, DR), lambda i: (0, nt - 1 - i, 0)), _resident((4, PG, PG)), _resident((1, DP)), _resident((DP, D)), _resident((4, DR)), _resident((1, DR)),
                  _resident((NH, HD, HD)), _resident((1, DR)), _resident((NH, HD, HD)), _resident((1, DR)),
                  _resident((1, DR)), _resident((DR, D))],
        out_specs=[rev(DIN), rev(D), rev(D), _resident((MAT_ROWS, HD)), _resident((VEC_ROWS, DR))],
        out_shape=[jax.ShapeDtypeStruct((S, DIN), BF16), jax.ShapeDtypeStruct((S, D), BF16),
                   jax.ShapeDtypeStruct((S, D), BF16), jax.ShapeDtypeStruct((MAT_ROWS, HD), F32),
                   jax.ShapeDtypeStruct((VEC_ROWS, DR), F32)],
        scratch_shapes=[pltpu.VMEM((POOL_HALO, DP), F32), pltpu.VMEM((CONV_HALO, DR), F32), pltpu.VMEM((8, DR), F32),
                        pltpu.VMEM((8, DR), F32), pltpu.VMEM((tm, DR), F32)],
        operands=(proj, proj, dmixo, y_pool, y_rnn, hr, hr, kept, wg, scale, w_pool_out, conv_w, conv_b, wa, ba, wx, bx, lam,
                  w_rnn_out),
        exchange=exchange, exchange_operands=exchange_operands)


def _in_bwd(dproj, x, dx2, norm_mix, w_in, exchange=None, exchange_operands=(), tm=512):
    S = x.shape[0]

    def body(dp_ref, x_ref, dx2_ref, g_ref, w_ref, dx_ref, dg_ref):
        i = pl.program_id(0)

        @pl.when(i == 0)
        def _():
            dg_ref[...] = jnp.zeros_like(dg_ref)

        dh = _dot(dp_ref[:, 0:1536], w_ref[0:1536, :])
        dh = dh + _dot(dp_ref[:, 1536:3072], w_ref[1536:3072, :])
        dh = dh + _dot(dp_ref[:, 3072:DIN], w_ref[3072:DIN, :])
        xv = x_ref[...]
        r = lax.rsqrt(jnp.mean(xv * xv, axis=-1, keepdims=True) + EPS)
        xh = xv * r
        dg_ref[0:1, :] += jnp.sum(dh * xh, axis=0, keepdims=True)
        dxh = dh * g_ref[...]
        dx_ref[...] = dx2_ref[...] + r * (dxh - xh * jnp.mean(dxh * xh, axis=-1, keepdims=True))

    return _call(
        body, "in_bwd", (S // tm,),
        in_specs=[_rows(DIN, tm), _rows(D, tm), _rows(D, tm), _resident((1, D)), _resident((DIN, D))],
        out_specs=[_rows(D, tm), _resident((8, D))],
        out_shape=[jax.ShapeDtypeStruct((S, D), F32), jax.ShapeDtypeStruct((8, D), F32)],
        operands=(dproj, x, dx2, norm_mix, w_in), exchange=exchange, exchange_operands=exchange_operands)


def _wgrad(a, b, name, tk, tn, exchange=None, exchange_operands=()):
    S, K = a.shape
    N = b.shape[1]

    def body(a_ref, b_ref, o_ref):
        o_ref[...] = _dot_tn(a_ref[...], b_ref[...]).astype(BF16)

    (out,), exchanged = _call(
        body, name, (K // tk, N // tn),
        in_specs=[pl.BlockSpec((S, tk), lambda k, n: (0, k)), pl.BlockSpec((S, tn), lambda k, n: (0, n))],
        out_specs=[pl.BlockSpec((tk, tn), lambda k, n: (k, n))],
        out_shape=[jax.ShapeDtypeStruct((K, N), BF16)],
        operands=(a, b), exchange=exchange, exchange_operands=exchange_operands)
    return (out, exchanged) if exchange is not None else out


VEC_SCALE, VEC_CONV_B, VEC_BA, VEC_BX, VEC_LAM, VEC_CONV_W, VEC_NORM_FINAL, VEC_NORM_FFN = 0, 1, 2, 3, 4, 5, 9, 10
VEC_LOSS = 11


class _Big:
    def __init__(self, name, rows, cols, axis, n, dtype=BF16, transposed=False, src_cols=None):
        self.name, self.rows, self.cols, self.axis, self.n, self.dtype = name, rows, cols, axis, n, dtype
        self.transposed = transposed
        self.src_cols = src_cols
        self.block_shape = (rows, n) if axis == 1 else (n, cols)

    def block(self, ref, p):
        if self.axis == 1:
            return ref.at[:, pl.ds(pl.multiple_of(p * self.n, 128), self.n)]
        return ref.at[pl.ds(pl.multiple_of(p * self.n, 16 if self.dtype == BF16 else 8), self.n), :]


BIG = (_Big("w_in", DIN, D, 0, DIN // 8, transposed=True), _Big("w_pool_out", DP, D, 1, D // 8),
       _Big("w_rnn_out", DR, D, 0, DR // 8), _Big("w_o", D, D, 0, D // 8),
       _Big("w_ffn_in", 2 * DFF, D, 0, 2 * DFF // 8, transposed=True), _Big("w_ffn_out", DFF, D, 0, DFF // 8))
CONV_W = _Big("conv_w", 8, DR, 1, DR // 8, F32)
W_FFN_IN_HALVES = (_Big("w_ffn_in_lo", 2 * DFF, D // 2, 0, 2 * DFF // 8, src_cols=(0, D // 2)),
                   _Big("w_ffn_in_hi", 2 * DFF, D // 2, 0, 2 * DFF // 8, src_cols=(D // 2, D)))
GATHERED = BIG + (CONV_W,) + W_FFN_IN_HALVES

HBM_SPEC = pl.BlockSpec(memory_space=pl.ANY)
VMEM_SPEC = pl.BlockSpec(memory_space=pltpu.VMEM)


def _place():
    x, y, c = (lax.axis_index(a) for a in MESH_AXES)
    other_chips = [(1 - x, y), (x, 1 - y), (1 - x, 1 - y)]
    return x, y, c, other_chips


def _remote(src, dst, send_sems, recv_sems, idx, to):
    return pltpu.make_async_remote_copy(src_ref=src, dst_ref=dst, send_sem=send_sems.at[idx], recv_sem=recv_sems.at[idx],
                                        device_id=to, device_id_type=MESH)


def _device_index(chip, core):
    return 4 * chip[0] + 2 * chip[1] + core


class _Gather:
    def __init__(self, tensors):
        self.tensors = tuple(tensors)
        n = len(self.tensors)
        self.in_specs = [HBM_SPEC] * n
        self.out_specs = [HBM_SPEC] * n
        self.out_shape = [jax.ShapeDtypeStruct((T.rows, T.cols), T.dtype) for T in self.tensors]
        self.scratch_shapes = [pltpu.VMEM(T.block_shape, T.dtype) for T in self.tensors] + [
            pltpu.VMEM(T.block_shape, F32) for T in self.tensors] + [
            pltpu.SemaphoreType.DMA((n, 7)), pltpu.SemaphoreType.DMA((n, 7)), pltpu.SemaphoreType.DMA((n, 2))]

    def middles(self, steps):
        return [(steps // 2, self.relay), (steps - 1, self.middle)]

    def _copies(self, ins, outs, scratch):
        n = len(self.tensors)
        mine, raw, (send_sems, recv_sems, loc_sems) = scratch[:n], scratch[n:2 * n], scratch[2 * n:]
        x, y, c, chips = _place()
        sibling = (x, y, 1 - c)
        me = _device_index((x, y), c)
        relay_from = (jnp.where(c == 0, 1 - x, x), jnp.where(c == 0, y, 1 - y))
        relay_to = (jnp.where(c == 0, x, 1 - x), jnp.where(c == 0, 1 - y, y))
        loads, stores, first, relays, passed, arrivals, late = [], [], [], [], [], [], []
        for t, T in enumerate(self.tensors):
            place = T.block(outs[t], me)
            src = ins[t] if T.src_cols is None else ins[t].at[:, T.src_cols[0]:T.src_cols[1]]
            loads.append(pltpu.make_async_copy(src, raw[t], loc_sems.at[t, 0]))
            stores.append(pltpu.make_async_copy(mine[t], place, loc_sems.at[t, 1]))
            first.append(_remote(mine[t], place, send_sems, recv_sems, (t, 0), sibling))
            theirs = T.block(outs[t], _device_index((x, y), 1 - c))
            late.append(_remote(theirs, theirs, send_sems, recv_sems, (t, 0), sibling))
            relayed = T.block(outs[t], _device_index(relay_from, c))
            relays.append(_remote(relayed, relayed, send_sems, recv_sems, (t, 3), (*relay_to, c)))
            for k, chip in enumerate(chips):
                if k < 2:
                    first.append(_remote(mine[t], place, send_sems, recv_sems, (t, 1 + k), (*chip, c)))
                land = T.block(outs[t], _device_index(chip, c))
                arrivals.append(_remote(land, land, send_sems, recv_sems, (t, 1 + k), sibling))
                passed.append(_remote(land, land, send_sems, recv_sems, (t, 4 + k), sibling))
                theirs = T.block(outs[t], _device_index(chip, 1 - c))
                late.append(_remote(theirs, theirs, send_sems, recv_sems, (t, 4 + k), sibling))
        return loads, stores, first, relays, passed, arrivals, late

    def start(self, ins, outs, scratch):
        loads, stores, first, _, _, _, _ = self._copies(ins, outs, scratch)
        n = len(self.tensors)
        for cp in loads:
            cp.start()
        for t, cp in enumerate(loads):
            cp.wait()
            scratch[t][...] = scratch[n + t][...].astype(self.tensors[t].dtype)
        for cp in stores + first:
            cp.start()

    def relay(self, ins, outs, scratch, skip=0):
        _, _, _, relays, passed, arrivals, _ = self._copies(ins, outs, scratch)
        for t in range(skip, len(self.tensors)):
            arrivals[3 * t].wait_recv()
            arrivals[3 * t + 1].wait_recv()
            for cp in (relays[t], passed[3 * t], passed[3 * t + 1]):
                cp.start()

    def middle(self, ins, outs, scratch, skip=0):
        _, _, _, _, passed, arrivals, _ = self._copies(ins, outs, scratch)
        for t in range(skip, len(self.tensors)):
            arrivals[3 * t + 2].wait_recv()
            passed[3 * t + 2].start()

    def finish(self, ins, outs, scratch, skip=0):
        _, stores, first, relays, passed, _, late = self._copies(ins, outs, scratch)
        for cp in late[4 * skip:]:
            cp.wait_recv()
        for cp in first + relays + passed:
            cp.wait_send()
        for cp in stores[skip:]:
            cp.wait()


def _in_proj_gather(x, norm_mix, blocks, tensors, order, tm=512):
    S = x.shape[0]
    nt = S // tm
    n = len(tensors)
    gather = _Gather(tensors)
    CB = 2 * tensors[0].n

    def body(order_ref, x_ref, g_ref, *refs):
        ins, (proj_ref, h_ref), outs = refs[:n], refs[n:n + 2], refs[n + 2:2 * n + 2]
        (h_all, w_chip, w_sem), scratch = refs[2 * n + 2:2 * n + 5], refs[2 * n + 5:]
        q, i = pl.program_id(0), pl.program_id(1)
        _, stores, _, relays, passed, arrivals, late = gather._copies(ins, outs, scratch)

        def fetch(turn):
            rows = outs[0].at[pl.ds(pl.multiple_of(order_ref[turn] * CB, 16), CB), :]
            cp = pltpu.make_async_copy(rows, w_chip, w_sem)
            cp.start()
            cp.wait()

        @pl.when((q == 0) & (i == 0))
        def _():
            gather.start(ins, outs, scratch)
            late[0].wait_recv()
            stores[0].wait()
            fetch(0)

        @pl.when((q == 1) & (i == 0))
        def _():
            arrivals[0].wait_recv()
            arrivals[1].wait_recv()
            for cp in (relays[0], passed[0], passed[1]):
                cp.start()
            late[1].wait_recv()
            fetch(1)

        @pl.when((q == 2) & (i == 0))
        def _():
            late[2].wait_recv()
            fetch(2)
            gather.relay(ins, outs, scratch, skip=1)

        @pl.when((q == 3) & (i == 0))
        def _():
            arrivals[2].wait_recv()
            passed[2].start()
            late[3].wait_recv()
            fetch(3)

        rows = pl.ds(pl.multiple_of(i * tm, tm), tm)

        @pl.when(q == 0)
        def _():
            xv = x_ref[...]
            r = lax.rsqrt(jnp.mean(xv * xv, axis=-1, keepdims=True) + EPS)
            h = (xv * r * g_ref[...]).astype(BF16)
            h_all[rows, :] = h
            h_ref[...] = h

        proj_ref[...] = _dot_nt(h_all[rows, :], w_chip[...])

        @pl.when((q == 3) & (i == nt - 1))
        def _():
            gather.middle(ins, outs, scratch, skip=1)
            gather.finish(ins, outs, scratch, skip=1)

    row_tile = lambda q, i, order: (jnp.where(q == 0, i, nt - 1), 0)
    whole = lambda shape: pl.BlockSpec(shape, lambda q, i, order: (0,) * len(shape), pipeline_mode=pl.Buffered(1))
    outs = pl.pallas_call(
        body, name="in_proj_gather",
        grid_spec=pltpu.PrefetchScalarGridSpec(
            num_scalar_prefetch=1, grid=(4, nt),
            in_specs=[pl.BlockSpec((tm, D), row_tile), whole((1, D))] + gather.in_specs,
            out_specs=[pl.BlockSpec((tm, CB), lambda q, i, order: (i, order[q])), pl.BlockSpec((tm, D), row_tile)]
            + gather.out_specs,
            scratch_shapes=[pltpu.VMEM((S, D), BF16), pltpu.VMEM((CB, D), BF16), pltpu.SemaphoreType.DMA]
            + gather.scratch_shapes),
        out_shape=[jax.ShapeDtypeStruct((S, DIN), F32), jax.ShapeDtypeStruct((S, D), BF16)] + gather.out_shape,
        compiler_params=_params("arbitrary", "arbitrary"),
    )(order, x, norm_mix, *blocks)
    return outs[:2], outs[2:]


PAIR_ROWS = 32


def _pair_reduce(grads, tensors, name):
    nt = len(tensors)

    def body(*refs):
        ins, own_out, sums_out, landed, mine = (refs[k * nt:(k + 1) * nt] for k in range(5))
        send_sems, recv_sems, loc_sems = refs[5 * nt:]
        x, y, c, chips = _place()
        chip_of = [2 * chip[0] + chip[1] for chip in chips]
        swaps, loads = [], []
        for t, T in enumerate(tensors):
            for j in range(4):
                swaps.append(_remote(T.block(ins[t], 2 * j + 1 - c), landed[t].at[j], send_sems, recv_sems, (t, j),
                                     (x, y, 1 - c)))
            for k in range(3):
                loads.append(pltpu.make_async_copy(T.block(ins[t], 2 * chip_of[k] + c), mine[t].at[k], loc_sems.at[t, k]))
        for cp in swaps + loads:
            cp.start()
        for cp in loads:
            cp.wait()
        for cp in swaps:
            cp.wait_recv()
        stores = []
        for t, T in enumerate(tensors):
            for k in range(3):
                acc, got = mine[t].at[k], landed[t].at[chip_of[k]]

                def add(i, carry, acc=acc, got=got):
                    rows = pl.ds(pl.multiple_of(i * PAIR_ROWS, PAIR_ROWS), PAIR_ROWS)
                    acc[rows, :] = (acc[rows, :].astype(F32) + got[rows, :].astype(F32)).astype(BF16)
                    return carry

                lax.fori_loop(0, T.block_shape[0] // PAIR_ROWS, add, 0)
            stores.append(pltpu.make_async_copy(mine[t], sums_out[t], loc_sems.at[t, 3]))
            stores.append(pltpu.make_async_copy(landed[t].at[2 * x + y], own_out[t], loc_sems.at[t, 4]))
        for cp in stores:
            cp.start()
        for cp in swaps:
            cp.wait_send()
        for cp in stores:
            cp.wait()

    blocks = [T.block_shape for T in tensors]
    return pl.pallas_call(
        body, name=name,
        in_specs=[HBM_SPEC] * nt, out_specs=[HBM_SPEC] * (2 * nt),
        out_shape=[jax.ShapeDtypeStruct(b, BF16) for b in blocks] + [jax.ShapeDtypeStruct((3,) + b, BF16) for b in blocks],
        scratch_shapes=[pltpu.VMEM((4,) + b, BF16) for b in blocks] + [pltpu.VMEM((3,) + b, BF16) for b in blocks]
        + [pltpu.SemaphoreType.DMA((nt, 4)), pltpu.SemaphoreType.DMA((nt, 4)), pltpu.SemaphoreType.DMA((nt, 5))],
        compiler_params=pltpu.CompilerParams(vmem_limit_bytes=VMEM_LIMIT),
    )(*grads)


class _Scatter:
    def middles(self, steps):
        return []

    def __init__(self, tensors):
        n = len(tensors)
        self.in_specs = [HBM_SPEC] * n
        self.out_specs = [HBM_SPEC] * n
        self.out_shape = [jax.ShapeDtypeStruct((3,) + T.block_shape, BF16) for T in tensors]
        self.scratch_shapes = [pltpu.SemaphoreType.DMA((n, 3)), pltpu.SemaphoreType.DMA((n, 3))]

    def _copies(self, ins, outs, scratch):
        send_sems, recv_sems = scratch
        x, y, c, chips = _place()
        return [_remote(ins[t].at[k], outs[t].at[k], send_sems, recv_sems, (t, k), (*chip, c))
                for t in range(len(ins)) for k, chip in enumerate(chips)]

    def start(self, ins, outs, scratch):
        for cp in self._copies(ins, outs, scratch):
            cp.start()

    def finish(self, ins, outs, scratch):
        for cp in self._copies(ins, outs, scratch):
            cp.wait()


def _adamw(w, g, m, v):
    m = ADAM_B1 * m + (1.0 - ADAM_B1) * g
    v = ADAM_B2 * v + (1.0 - ADAM_B2) * (g * g)
    m_hat = m / (1.0 - ADAM_B1 ** ADAM_STEP)
    v_hat = v / (1.0 - ADAM_B2 ** ADAM_STEP)
    delta = -ADAM_LR * (m_hat / (jnp.sqrt(v_hat) + ADAM_EPS) + ADAM_WD * w)
    return delta, m, v


def _final_sum(T, g, lz1, lz2, where, w, m, v):
    rows, cols = T.block_shape
    sub = 4 if T.axis == 0 and rows % 64 == 0 and rows > 256 else 1
    blk = (rows // sub, cols)

    def body(where_ref, g_ref, l1_ref, l2_ref, w_ref, m_ref, v_ref, g_out, d_out, m_out, v_out):
        tot = g_ref[...].astype(F32) + l1_ref[...].astype(F32)
        for k in range(3):
            tot = tot + l2_ref[k].astype(F32)
        g_out[...] = tot
        d_out[...], m_out[...], v_out[...] = _adamw(w_ref[...], tot, m_ref[...], v_ref[...])

    def in_whole(r, wh):
        p = wh[0]
        return (0, p) if T.axis == 1 else (p * sub + r, 0)

    own = pl.BlockSpec(blk, lambda r, wh: (r, 0))
    return pl.pallas_call(
        body, name="grad_final_" + T.name,
        grid_spec=pltpu.PrefetchScalarGridSpec(
            num_scalar_prefetch=1, grid=(sub,),
            in_specs=[pl.BlockSpec(blk, in_whole),
                      own,
                      pl.BlockSpec((3,) + blk, lambda r, wh: (0, r, 0)), own, own, own],
            out_specs=[own] * 4),
        out_shape=[jax.ShapeDtypeStruct(T.block_shape, F32)] * 4,
        compiler_params=_params("arbitrary"),
    )(where, g, lz1, lz2, w, m, v)


VEC_PIECE = DR // 8


class _AllReduce:
    def __init__(self, items):
        self.items = tuple(items)
        n = len(self.items)
        self.in_specs = [HBM_SPEC] * n
        self.out_specs = [HBM_SPEC] * n
        self.out_shape = [jax.ShapeDtypeStruct(shape, F32) for shape, _ in self.items]
        pieces = [(shape[0] // 8, shape[1]) if axis == 0 else (shape[0], shape[1] // 8) for shape, axis in self.items]
        self.scratch_shapes = ([pltpu.VMEM((8,) + p, F32) for p in pieces] + [pltpu.VMEM(p, F32) for p in pieces] + [
            pltpu.SemaphoreType.DMA((2 * n, 8)), pltpu.SemaphoreType.DMA((2 * n, 8)), pltpu.SemaphoreType.DMA((2 * n,))])

    def middles(self, steps):
        return [(steps // 2, self.middle)]

    def _copies(self, ins, outs, scratch):
        n = len(self.items)
        landed, sums, (send_sems, recv_sems, loc_sems) = scratch[:n], scratch[n:2 * n], scratch[2 * n:]
        x, y, c, _ = _place()
        me = _device_index((x, y), c)

        def peer(r):
            return (1 - x if r & 4 else x, 1 - y if r & 2 else y, 1 - c if r & 1 else c)

        def piece(i, ref, p):
            shape, axis = self.items[i]
            if axis == 0:
                rows = shape[0] // 8
                return ref.at[pl.ds(pl.multiple_of(p * rows, 8), rows), :]
            cols = shape[1] // 8
            return ref.at[:, pl.ds(pl.multiple_of(p * cols, 128), cols)]

        own, scatter, arrivals, keep, spread, late = [], [], [], [], [], []
        for i in range(n):
            own.append(pltpu.make_async_copy(piece(i, ins[i], me), landed[i].at[0], loc_sems.at[2 * i]))
            keep.append(pltpu.make_async_copy(sums[i], piece(i, outs[i], me), loc_sems.at[2 * i + 1]))
            for r in range(1, 8):
                to = peer(r)
                p = _device_index(to[:2], to[2])
                scatter.append(_remote(piece(i, ins[i], p), landed[i].at[r], send_sems, recv_sems, (2 * i, r), to))
                spread.append(_remote(sums[i], piece(i, outs[i], me), send_sems, recv_sems, (2 * i + 1, r), to))
                late.append(_remote(sums[i], piece(i, outs[i], p), send_sems, recv_sems, (2 * i + 1, r), to))
        return own, scatter, keep, spread, late, landed, sums

    def start(self, ins, outs, scratch):
        own, scatter, _, _, _, _, _ = self._copies(ins, outs, scratch)
        for cp in own + scatter:
            cp.start()

    def middle(self, ins, outs, scratch):
        own, scatter, keep, spread, _, landed, sums = self._copies(ins, outs, scratch)
        for cp in own:
            cp.wait()
        for cp in scatter:
            cp.wait_recv()
        for i in range(len(self.items)):
            total = landed[i][0]
            for r in range(1, 8):
                total = total + landed[i][r]
            sums[i][...] = total
        for cp in keep + spread:
            cp.start()

    def finish(self, ins, outs, scratch):
        _, scatter, keep, spread, late, _, _ = self._copies(ins, outs, scratch)
        for cp in late:
            cp.wait_recv()
        for cp in scatter + spread:
            cp.wait_send()
        for cp in keep:
            cp.wait()


class _Both:
    def __init__(self, a, b):
        self.a, self.b = a, b
        self.in_specs, self.out_specs = a.in_specs + b.in_specs, a.out_specs + b.out_specs
        self.out_shape, self.scratch_shapes = a.out_shape + b.out_shape, a.scratch_shapes + b.scratch_shapes

    def _each(self, ins, outs, scratch):
        a = self.a
        i, o, s = len(a.in_specs), len(a.out_specs), len(a.scratch_shapes)
        return (a, ins[:i], outs[:o], scratch[:s]), (self.b, ins[i:], outs[o:], scratch[s:])

    def middles(self, steps):
        def of(which, middle):
            return lambda ins, outs, scratch: middle(*self._each(ins, outs, scratch)[which][1:])
        return [(at, of(which, middle)) for which, e in enumerate((self.a, self.b)) for at, middle in e.middles(steps)]

    def start(self, ins, outs, scratch):
        for e, i, o, s in self._each(ins, outs, scratch):
            e.start(i, o, s)

    def finish(self, ins, outs, scratch):
        for e, i, o, s in self._each(ins, outs, scratch):
            e.finish(i, o, s)


def _all_reduce(arrays, items, name):
    reduce = _AllReduce(items)
    n = len(items)

    def body(*refs):
        ins, outs, scratch = refs[:n], refs[n:2 * n], refs[2 * n:]
        reduce.start(ins, outs, scratch)
        reduce.middle(ins, outs, scratch)
        reduce.finish(ins, outs, scratch)

    return pl.pallas_call(
        body, name=name, in_specs=reduce.in_specs, out_specs=reduce.out_specs, out_shape=reduce.out_shape,
        scratch_shapes=reduce.scratch_shapes,
    )(*arrays)


def _adam_small(grads, wmv):
    n = len(grads)

    def body(*refs):
        g_refs, rest = refs[:n], refs[n:]
        ins, outs = rest[:3 * n], rest[3 * n:]
        for i in range(n):
            d, m, v = _adamw(ins[3 * i][...], g_refs[i][...], ins[3 * i + 1][...], ins[3 * i + 2][...])
            outs[3 * i][...], outs[3 * i + 1][...], outs[3 * i + 2][...] = d, m, v

    flat = [a for t in wmv for a in t]
    return pl.pallas_call(
        body, name="adam_small",
        in_specs=[VMEM_SPEC] * (4 * n), out_specs=[VMEM_SPEC] * (3 * n),
        out_shape=[jax.ShapeDtypeStruct(a.shape, F32) for a in flat],
    )(*grads, *flat)


WEIGHT_NAMES = ("norm_mix", "w_in", "w_pool_grp", "pool_scale", "w_pool_out", "conv_w", "conv_b", "w_rg_a", "b_rg_a", "w_rg_x",
                "b_rg_x", "lru_lambda", "w_rnn_out", "w_o", "norm_ffn", "w_ffn_in", "w_ffn_out", "norm_final")


def kernel(x, norm_mix, w_in, w_pool_grp, pool_scale, w_pool_out, conv_w, conv_b, w_rg_a, b_rg_a, w_rg_x, b_rg_x, lru_lambda, w_rnn_out, w_o, norm_ffn, w_ffn_in, w_ffn_out, norm_final, loss_target, m_norm_mix, m_w_in, m_w_pool_grp, m_pool_scale, m_w_pool_out, m_conv_w, m_conv_b, m_w_rg_a, m_b_rg_a, m_w_rg_x, m_b_rg_x, m_lru_lambda, m_w_rnn_out, m_w_o, m_norm_ffn, m_w_ffn_in, m_w_ffn_out, m_norm_final, v_norm_mix, v_w_in, v_w_pool_grp, v_pool_scale, v_w_pool_out, v_conv_w, v_conv_b, v_w_rg_a, v_b_rg_a, v_w_rg_x, v_b_rg_x, v_lru_lambda, v_w_rnn_out, v_w_o, v_norm_ffn, v_w_ffn_in, v_w_ffn_out, v_norm_final):
    w = dict(norm_mix=norm_mix, w_in=w_in, w_pool_grp=w_pool_grp, pool_scale=pool_scale, w_pool_out=w_pool_out, conv_w=conv_w,
             conv_b=conv_b, w_rg_a=w_rg_a, b_rg_a=b_rg_a, w_rg_x=w_rg_x, b_rg_x=b_rg_x, lru_lambda=lru_lambda,
             w_rnn_out=w_rnn_out, w_o=w_o, norm_ffn=norm_ffn, w_ffn_in=w_ffn_in, w_ffn_out=w_ffn_out, norm_final=norm_final)
    m = dict(norm_mix=m_norm_mix, w_in=m_w_in, w_pool_grp=m_w_pool_grp, pool_scale=m_pool_scale, w_pool_out=m_w_pool_out,
             conv_w=m_conv_w, conv_b=m_conv_b, w_rg_a=m_w_rg_a, b_rg_a=m_b_rg_a, w_rg_x=m_w_rg_x, b_rg_x=m_b_rg_x,
             lru_lambda=m_lru_lambda, w_rnn_out=m_w_rnn_out, w_o=m_w_o, norm_ffn=m_norm_ffn, w_ffn_in=m_w_ffn_in,
             w_ffn_out=m_w_ffn_out, norm_final=m_norm_final)
    v = dict(norm_mix=v_norm_mix, w_in=v_w_in, w_pool_grp=v_w_pool_grp, pool_scale=v_pool_scale, w_pool_out=v_w_pool_out,
             conv_w=v_conv_w, conv_b=v_conv_b, w_rg_a=v_w_rg_a, b_rg_a=v_b_rg_a, w_rg_x=v_w_rg_x, b_rg_x=v_b_rg_x,
             lru_lambda=v_lru_lambda, w_rnn_out=v_w_rnn_out, w_o=v_w_o, norm_ffn=v_norm_ffn, w_ffn_in=v_w_ffn_in,
             w_ffn_out=v_w_ffn_out, norm_final=v_norm_final)
    xi, yi, ci = (lax.axis_index(a) for a in MESH_AXES)
    chip = 2 * xi + yi

    def held(T, a):
        return jnp.swapaxes(a, 0, 1) if T.transposed else a

    where = jnp.stack([2 * chip + ci]).astype(jnp.int32)
    by_name = {T.name: T for T in GATHERED}
    block = {T.name: held(T, w[T.name][0]) for T in BIG}
    block["conv_w"] = jnp.pad(conv_w[0], ((0, CONV_W.rows - 4), (0, 0)))
    block["w_ffn_in_lo"] = block["w_ffn_in_hi"] = block["w_ffn_in"]

    def gather_of(*names):
        return dict(exchange=_Gather([by_name[n] for n in names]), exchange_operands=[block[n] for n in names])

    def pair_sums(names, partials, tag):
        out = _pair_reduce(partials, [by_name[n] for n in names], "grad_pair_reduce_" + tag)
        return list(out[:len(names)]), list(out[len(names):])

    xs, target = x[0], loss_target[0]
    wg_b, wa_b, wx_b = (a[0].astype(BF16) for a in (w_pool_grp, w_rg_a, w_rg_x))
    ba2, bx2 = b_rg_a.reshape(1, DR), b_rg_x.reshape(1, DR)
    first = ("w_in", "w_pool_out", "w_rnn_out", "conv_w")
    order = jnp.stack([chip, 2 * (1 - xi) + yi, 2 * xi + (1 - yi), 2 * (1 - xi) + (1 - yi)]).astype(jnp.int32)
    (proj, h1), (w_in_g, w_pool_out_g, w_rnn_out_g, conv_g) = _in_proj_gather(
        xs, norm_mix, [block[n] for n in first], [by_name[n] for n in first], order)
    mixer_weights = (wg_b, pool_scale, w_pool_out_g, conv_g[0:4], conv_b, wa_b, ba2, wx_b, bx2, lru_lambda, w_rnn_out_g)
    (pm, y_pool, hr, z, y_rnn, kept), (w_o_g, w_ffn_lo_g, w_ffn_hi_g) = _mixer_fwd(
        proj, *mixer_weights, **gather_of("w_o", "w_ffn_in_lo", "w_ffn_in_hi"))
    (mix, x2, h2), _ = _merge_out(xs, proj, y_pool, y_rnn, w_o_g, norm_ffn)
    (gu, act), (w_ffn_out_g,) = _ffn_up(h2, w_ffn_lo_g, w_ffn_hi_g, **gather_of("w_ffn_out"))
    dx3, dx3b, loss_part, dvec_fin = _ffn_down_loss(act, x2, target, w_ffn_out_g, norm_final.reshape(1, D))

    dgu = _ffn_bwd_down(dx3b, gu, w_ffn_out_g)
    dx2, dx2b, dmixo, dvec_ffn = _ffn_bwd_up(dgu, x2, dx3, w_ffn_lo_g, w_ffn_hi_g, norm_ffn, w_o_g)
    names_a = ("w_ffn_in", "w_ffn_out", "w_o")
    part_a = [_wgrad(dgu, h2, "wgrad_ffn_in", 1408, 512), _wgrad(act, dx3b, "wgrad_ffn_out", 1408, 512),
              _wgrad(mix, dx2b, "wgrad_o", 1024, 1024)]
    lz1_a, sums_a = pair_sums(names_a, part_a, "ffn")
    (dproj, dypb, dyrb, dmat, dvec_mix), lz2_a = _mixer_bwd(
        proj, dmixo, y_pool, y_rnn, hr, kept, *mixer_weights,
        exchange=_Scatter([by_name[n] for n in names_a]), exchange_operands=sums_a)
    names_b = ("w_pool_out", "w_rnn_out")
    part_b = [_wgrad(pm, dypb, "wgrad_pool_out", 512, 1024), _wgrad(z, dyrb, "wgrad_rnn_out", 1024, 1024)]
    lz1_b, sums_b = pair_sums(names_b, part_b, "mix")
    dvec = jnp.concatenate([dvec_mix[0:9], dvec_fin[0:1], dvec_ffn[0:1], jnp.pad(loss_part, ((0, 0), (0, DR - 1))),
                            jnp.zeros((VEC_ROWS - 12, DR), F32)], axis=0)
    g_in, exchanged = _wgrad(
        dproj, h1, "wgrad_in", 1152, 1024,
        exchange=_Both(_Scatter([by_name[n] for n in names_b]), _AllReduce([((MAT_ROWS, HD), 0), ((VEC_ROWS, DR), 1)])),
        exchange_operands=sums_b + [dmat, dvec])
    lz2_b, (mat, vec) = exchanged[:2], exchanged[2:]
    loss = vec[VEC_LOSS, 0]
    lz1_c, sums_c = pair_sums(("w_in",), [g_in], "in")
    (grad_x, dvec_in), lz2_c = _in_bwd(dproj, xs, dx2, norm_mix, w_in_g,
                                       exchange=_Scatter([by_name["w_in"]]), exchange_operands=sums_c)
    (vec_in,) = _all_reduce([dvec_in], [((8, D), 1)], "all_reduce_norm_mix")

    grads, delta, new_m, new_v = {}, {}, {}, {}
    for n, g, l1, l2 in zip(names_a + names_b + ("w_in",), part_a + part_b + [g_in], lz1_a + lz1_b + lz1_c,
                            lz2_a + lz2_b + lz2_c):
        T = by_name[n]
        out = _final_sum(T, g, l1, l2, where, held(T, w[n][0]), held(T, m[n][0]), held(T, v[n][0]))
        grads[n], delta[n], new_m[n], new_v[n] = (held(T, a) for a in out)
    me = 4 * xi + 2 * yi + ci
    small_grads = dict(
        w_pool_grp=mat[0:MAT_WA], w_rg_a=mat[MAT_WA:MAT_WX], w_rg_x=mat[MAT_WX:MAT_ROWS],
        pool_scale=vec[VEC_SCALE:VEC_SCALE + 1, 0:DP], conv_b=vec[VEC_CONV_B:VEC_CONV_B + 1],
        b_rg_a=vec[VEC_BA:VEC_BA + 1], b_rg_x=vec[VEC_BX:VEC_BX + 1], lru_lambda=vec[VEC_LAM:VEC_LAM + 1],
        conv_w=lax.dynamic_slice(vec, (VEC_CONV_W, VEC_PIECE * me), (4, VEC_PIECE)),
        norm_final=vec[VEC_NORM_FINAL:VEC_NORM_FINAL + 1], norm_ffn=vec[VEC_NORM_FFN:VEC_NORM_FFN + 1],
        norm_mix=vec_in[0:1])
    names = list(small_grads)
    as2d = lambda a, g: a.reshape(g.shape)
    upd = _adam_small([small_grads[n] for n in names],
                      [(as2d(w[n], small_grads[n]), as2d(m[n], small_grads[n]), as2d(v[n], small_grads[n])) for n in names])
    for i, n in enumerate(names):
        grads[n] = small_grads[n]
        delta[n], new_m[n], new_v[n] = upd[3 * i:3 * i + 3]

    shaped = lambda d: [d[n].reshape(w[n].shape) for n in WEIGHT_NAMES]
    return (loss, grad_x[None], *shaped(grads), *shaped(delta), *shaped(new_m), *shaped(new_v))
```

```python
import math

import jax
import jax.numpy as jnp
from jax import lax
from jax.experimental import pallas as pl
from jax.experimental.pallas import tpu as pltpu

F32 = jnp.float32
BF16 = jnp.bfloat16

D = 1024
DP = 512
PG = 128
WINDOWS = (2, 4, 8, 16)
DR = 1024
NH = 8
HD = 128
DIN = 4608
DFF = 2816
EPS = 1e-6
LRU_C = 8.0
POOL_HALO = 16
CONV_HALO = 8
KEPT = 5

ADAM_LR = 0.001
ADAM_B1 = 0.9
ADAM_B2 = 0.999
ADAM_EPS = 1e-08
ADAM_WD = 0.01
ADAM_STEP = 10

VMEM_LIMIT = 56 * 1024 * 1024
MESH_AXES = ("x", "y", "c")
MESH = pl.DeviceIdType.MESH


def _dot(a, b):
    return jnp.dot(a, b, preferred_element_type=F32)


def _dot_nt(a, b):
    return lax.dot_general(a, b, (((1,), (1,)), ((), ())), preferred_element_type=F32)


def _dot_tn(a, b):
    return lax.dot_general(a, b, (((0,), (0,)), ((), ())), preferred_element_type=F32)


def _params(*sem):
    return pltpu.CompilerParams(dimension_semantics=sem, vmem_limit_bytes=VMEM_LIMIT)


def _resident(shape):
    nd = len(shape)
    return pl.BlockSpec(shape, lambda i: (0,) * nd, pipeline_mode=pl.Buffered(1))


def _rows(shape_cols, tm):
    return pl.BlockSpec((tm, shape_cols), lambda i: (i, 0))


def _call(body, name, grid, in_specs, out_specs, out_shape, operands, scratch_shapes=(), exchange=None, exchange_operands=()):
    n_in, n_out, n_scr = len(in_specs), len(out_specs), len(scratch_shapes)
    steps = math.prod(grid)
    if exchange is None:
        outs = pl.pallas_call(body, name=name, grid=grid, in_specs=in_specs, out_specs=out_specs, out_shape=out_shape,
                              scratch_shapes=list(scratch_shapes), compiler_params=_params(*["arbitrary"] * len(grid)))(*operands)
        return outs, []
    e_in, e_out = len(exchange.in_specs), len(exchange.out_specs)

    def hosted(*refs):
        ins, refs = refs[:n_in], refs[n_in:]
        e_ins, refs = refs[:e_in], refs[e_in:]
        outs, refs = refs[:n_out], refs[n_out:]
        e_outs, refs = refs[:e_out], refs[e_out:]
        scr, e_scr = refs[:n_scr], refs[n_scr:]
        step = pl.program_id(0)
        for axis in range(1, len(grid)):
            step = step * grid[axis] + pl.program_id(axis)
        pl.when(step == 0)(lambda: exchange.start(e_ins, e_outs, e_scr))
        for at, middle in exchange.middles(steps):
            pl.when(step == at)(lambda middle=middle: middle(e_ins, e_outs, e_scr))
        body(*ins, *outs, *scr)
        pl.when(step == steps - 1)(lambda: exchange.finish(e_ins, e_outs, e_scr))

    outs = pl.pallas_call(
        hosted, name=name, grid=grid, in_specs=list(in_specs) + exchange.in_specs,
        out_specs=list(out_specs) + exchange.out_specs, out_shape=list(out_shape) + exchange.out_shape,
        scratch_shapes=list(scratch_shapes) + exchange.scratch_shapes,
        compiler_params=_params(*["arbitrary"] * len(grid)))(*operands, *exchange_operands)
    return outs[:n_out], outs[n_out:]


GELU_C = math.sqrt(2.0 / math.pi)
GELU_K = 0.044715 * GELU_C


def _gelu(x, with_grad=False):
    x2 = x * x
    t = jnp.tanh(x * (GELU_C + GELU_K * x2))
    hx = 0.5 * x
    y = hx + hx * t
    if not with_grad:
        return y
    return y, 0.5 + 0.5 * t + hx * (1.0 - t * t) * (GELU_C + (3.0 * GELU_K) * x2)


def _softplus_neg(lam):
    z = jnp.exp(-jnp.abs(lam))
    u = 1.0 + z
    dlt = u - 1.0
    log1p = jnp.where(dlt == 0.0, z, jnp.log(u) * (z / jnp.where(dlt == 0.0, 1.0, dlt)))
    return jnp.maximum(-lam, 0.0) + log1p


def _sigmoid(x):
    return 0.5 * jnp.tanh(0.5 * x) + 0.5


def _linear_scan(out_ref, A, B, h0, reverse):
    n = A.shape[0]
    sub = lax.broadcasted_iota(jnp.int32, (8, 1), 0)
    tiles = range(n // 8 - 1, -1, -1) if reverse else range(n // 8)
    carry = h0
    for j in tiles:
        a, b = A[8 * j:8 * j + 8, :], B[8 * j:8 * j + 8, :]
        for d in (1, 2, 4):
            keep = (sub < 8 - d) if reverse else (sub >= d)
            shift = 8 - d if reverse else d
            b = jnp.where(keep, a * pltpu.roll(b, shift, axis=0) + b, b)
            a = jnp.where(keep, a * pltpu.roll(a, shift, axis=0), a)
        h = a * carry + b
        out_ref[8 * j:8 * j + 8, :] = h
        carry = h[0:1, :] if reverse else h[7:8, :]
    return carry


def _pool_windows(ext, shift_sign):
    n = ext.shape[0]
    s = ext
    outs = []
    for w in WINDOWS:
        d = w // 2
        s = s + pltpu.roll(s, d if shift_sign > 0 else n - d, axis=0)
        outs.append(s[:, :PG])
        s = s[:, PG:]
    return outs


def _conv_taps(uext):
    taps = []
    for k in range(4):
        sh = 3 - k
        v = uext if sh == 0 else pltpu.roll(uext, sh, axis=0)
        taps.append(v[CONV_HALO:, :])
    return taps


def _gates(v, wa_ref, ba_ref, wx_ref, bx_ref, sp):
    vb = v.astype(BF16)
    ra, rx = [], []
    for h in range(NH):
        vh = vb[:, h * HD:(h + 1) * HD]
        ra.append(_dot(vh, wa_ref[h]))
        rx.append(_dot(vh, wx_ref[h]))
    r = _sigmoid(jnp.concatenate(ra, axis=1) + ba_ref[...])
    i = _sigmoid(jnp.concatenate(rx, axis=1) + bx_ref[...])
    log_a = r * ((-LRU_C) * sp)
    a = jnp.exp(log_a)
    one_minus = -jnp.tanh(log_a) * (1.0 + a * a)
    return r, i, a, jnp.sqrt(one_minus), lax.rsqrt(one_minus)


def _mixer_fwd(proj, wg, scale, w_pool_out, conv_w, conv_b, wa, ba, wx, bx, lam, w_rnn_out, exchange=None,
               exchange_operands=(), tm=256):
    S = proj.shape[0]
    UW = DP + 2 * DR

    def body(proj_ref, wg_ref, scale_ref, wpo_ref, cw_ref, cb_ref, wa_ref, ba_ref, wx_ref, bx_ref, lam_ref, wro_ref,
             pm_ref, ypool_ref, hr_ref, z_ref, yrnn_ref, kept_ref, pool_carry, conv_carry, h_carry):
        i = pl.program_id(0)

        @pl.when(i == 0)
        def _():
            pool_carry[...] = jnp.zeros_like(pool_carry)
            conv_carry[...] = jnp.zeros_like(conv_carry)
            h_carry[...] = jnp.zeros_like(h_carry)

        rows = lax.broadcasted_iota(jnp.int32, (tm, 1), 0)
        t_glob = i * tm + rows

        u_pool = proj_ref[:, 0:DP]
        ext = jnp.concatenate([pool_carry[...], u_pool], axis=0)
        pool_carry[...] = u_pool[tm - POOL_HALO:, :]
        sums = _pool_windows(ext, +1)
        mixed = []
        for g, w in enumerate(WINDOWS):
            inv_cnt = 1.0 / jnp.minimum(t_glob + 1, w).astype(F32)
            pooled_g = sums[g][POOL_HALO:, :] * inv_cnt - u_pool[:, g * PG:(g + 1) * PG]
            mixed.append(_dot(pooled_g.astype(BF16), wg_ref[g]))
        pm = (jnp.concatenate(mixed, axis=1) * scale_ref[...]).astype(BF16)
        pm_ref[...] = pm
        ypool_ref[...] = _dot(pm, wpo_ref[...])

        u_rnn = proj_ref[:, DP:DP + DR]
        uext = jnp.concatenate([conv_carry[...], u_rnn], axis=0)
        conv_carry[...] = u_rnn[tm - CONV_HALO:, :]
        taps = _conv_taps(uext)
        v = cb_ref[...]
        for k in range(4):
            v = v + taps[k] * cw_ref[k:k + 1, :]
        sp = _softplus_neg(lam_ref[...])
        r, gi, a, mult, _ = _gates(v, wa_ref, ba_ref, wx_ref, bx_ref, sp)
        for k, kept in enumerate((v, r, gi, a, mult)):
            kept_ref[k] = kept
        h_carry[0:1, :] = _linear_scan(hr_ref, a, mult * gi * v, h_carry[0:1, :], reverse=False)
        z = (hr_ref[...] * _gelu(proj_ref[:, DP + DR:UW])).astype(BF16)
        z_ref[...] = z
        yrnn_ref[...] = _dot(z, wro_ref[...])

    return _call(
        body, "mixer_fwd", (S // tm,),
        in_specs=[_rows(UW, tm), _resident((4, PG, PG)), _resident((1, DP)), _resident((DP, D)), _resident((4, DR)),
                  _resident((1, DR)), _resident((NH, HD, HD)), _resident((1, DR)), _resident((NH, HD, HD)),
                  _resident((1, DR)), _resident((1, DR)), _resident((DR, D))],
        out_specs=[_rows(DP, tm), _rows(D, tm), _rows(DR, tm), _rows(DR, tm), _rows(D, tm),
                   pl.BlockSpec((KEPT, tm, DR), lambda i: (0, i, 0))],
        out_shape=[jax.ShapeDtypeStruct((S, DP), BF16),
                   jax.ShapeDtypeStruct((S, D), F32), jax.ShapeDtypeStruct((S, DR), F32),
                   jax.ShapeDtypeStruct((S, DR), BF16), jax.ShapeDtypeStruct((S, D), F32),
                   jax.ShapeDtypeStruct((KEPT, S, DR), F32)],
        scratch_shapes=[pltpu.VMEM((POOL_HALO, DP), F32), pltpu.VMEM((CONV_HALO, DR), F32), pltpu.VMEM((8, DR), F32)],
        operands=(proj, wg, scale, w_pool_out, conv_w, conv_b, wa, ba, wx, bx, lam, w_rnn_out),
        exchange=exchange, exchange_operands=exchange_operands)


FF_CHUNKS = ((0, 768), (768, 1536), (1536, 2304), (2304, DFF))


def _rms(x):
    r = lax.rsqrt(jnp.mean(x * x, axis=-1, keepdims=True) + EPS)
    return r, x * r


def _rms_bwd(dh, g, r, xh):
    dxh = dh * g
    return r * (dxh - xh * jnp.mean(dxh * xh, axis=-1, keepdims=True))


def _merge_out(x, proj, y_pool, y_rnn, w_o, norm_ffn, exchange=None, exchange_operands=(), tm=512):
    S = x.shape[0]
    GL0 = (DP + 2 * DR) // 512

    def gl_spec(k):
        return pl.BlockSpec((tm, 512), lambda i: (i, GL0 + k))

    def body(x_ref, gl0, gl1, gl2, gl3, yp_ref, yr_ref, wo_ref, gf_ref, mix_ref, x2_ref, h2_ref):
        s_p = _sigmoid(jnp.concatenate([gl0[...], gl1[...]], axis=1))
        s_r = _sigmoid(jnp.concatenate([gl2[...], gl3[...]], axis=1))
        mix = (s_p * yp_ref[...] + s_r * yr_ref[...]).astype(BF16)
        mix_ref[...] = mix
        x2 = x_ref[...] + _dot(mix, wo_ref[...])
        x2_ref[...] = x2
        _, xh2 = _rms(x2)
        h2_ref[...] = (xh2 * gf_ref[...]).astype(BF16)

    return _call(
        body, "merge_out", (S // tm,),
        in_specs=[_rows(D, tm), gl_spec(0), gl_spec(1), gl_spec(2), gl_spec(3), _rows(D, tm), _rows(D, tm),
                  _resident((D, D)), _resident((1, D))],
        out_specs=[_rows(D, tm), _rows(D, tm), _rows(D, tm)],
        out_shape=[jax.ShapeDtypeStruct((S, D), BF16), jax.ShapeDtypeStruct((S, D), F32), jax.ShapeDtypeStruct((S, D), BF16)],
        operands=(x, proj, proj, proj, proj, y_pool, y_rnn, w_o, norm_ffn),
        exchange=exchange, exchange_operands=exchange_operands)


def _ffn_up(h2, w_lo, w_hi, exchange=None, exchange_operands=(), tm=512):
    S = h2.shape[0]
    HALF = D // 2

    def body(h_ref, lo_ref, hi_ref, gu_ref, act_ref):
        h_lo, h_hi = h_ref[:, 0:HALF], h_ref[:, HALF:D]
        for c0, c1 in FF_CHUNKS:
            gate = _dot_nt(h_lo, lo_ref[c0:c1, :]) + _dot_nt(h_hi, hi_ref[c0:c1, :])
            up = _dot_nt(h_lo, lo_ref[DFF + c0:DFF + c1, :]) + _dot_nt(h_hi, hi_ref[DFF + c0:DFF + c1, :])
            gu_ref[:, c0:c1] = gate.astype(BF16)
            gu_ref[:, DFF + c0:DFF + c1] = up.astype(BF16)
            act_ref[:, c0:c1] = (gate * _sigmoid(gate) * up).astype(BF16)

    return _call(
        body, "ffn_up", (S // tm,),
        in_specs=[_rows(D, tm), _resident((2 * DFF, HALF)), _resident((2 * DFF, HALF))],
        out_specs=[_rows(2 * DFF, tm), _rows(DFF, tm)],
        out_shape=[jax.ShapeDtypeStruct((S, 2 * DFF), BF16), jax.ShapeDtypeStruct((S, DFF), BF16)],
        operands=(h2, w_lo, w_hi), exchange=exchange, exchange_operands=exchange_operands)


def _ffn_down_loss(act, x2, target, w_ffn_out, norm_final, tm=512):
    S = act.shape[0]

    def body(act_ref, x2_ref, t_ref, w_ref, gn_ref, dx3_ref, dx3b_ref, loss_ref, dvec_ref):
        i = pl.program_id(0)

        @pl.when(i == 0)
        def _():
            loss_ref[...] = jnp.zeros_like(loss_ref)
            dvec_ref[...] = jnp.zeros_like(dvec_ref)

        x3 = x2_ref[...] + _dot(act_ref[...], w_ref[...])
        r3, xh3 = _rms(x3)
        g_fin = gn_ref[...]
        e = xh3 * g_fin - t_ref[...]
        loss_ref[...] += jnp.sum(e * e, axis=(0, 1), keepdims=True) * (0.5 / D)
        dy = e * (1.0 / D)
        dvec_ref[0:1, :] += jnp.sum(dy * xh3, axis=0, keepdims=True)
        dx3 = _rms_bwd(dy, g_fin, r3, xh3)
        dx3_ref[...] = dx3
        dx3b_ref[...] = dx3.astype(BF16)

    return pl.pallas_call(
        body, name="ffn_down_loss", grid=(S // tm,),
        in_specs=[_rows(DFF, tm), _rows(D, tm), _rows(D, tm), _resident((DFF, D)), _resident((1, D))],
        out_specs=[_rows(D, tm), _rows(D, tm), _resident((1, 1)), _resident((8, D))],
        out_shape=[jax.ShapeDtypeStruct((S, D), F32), jax.ShapeDtypeStruct((S, D), BF16),
                   jax.ShapeDtypeStruct((1, 1), F32), jax.ShapeDtypeStruct((8, D), F32)],
        compiler_params=_params("arbitrary"),
    )(act, x2, target, w_ffn_out, norm_final)


def _ffn_bwd_down(dx3b, gu, w_ffn_out, tm=512):
    S = dx3b.shape[0]

    def body(d_ref, gu_ref, w_ref, dgu_ref):
        d = d_ref[...]
        for c0, c1 in FF_CHUNKS:
            dact = _dot_nt(d, w_ref[c0:c1, :])
            gate = gu_ref[:, c0:c1].astype(F32)
            up = gu_ref[:, DFF + c0:DFF + c1].astype(F32)
            sg = _sigmoid(gate)
            dgu_ref[:, c0:c1] = (dact * up * (sg * (1.0 + gate * (1.0 - sg)))).astype(BF16)
            dgu_ref[:, DFF + c0:DFF + c1] = (dact * (gate * sg)).astype(BF16)

    return pl.pallas_call(
        body, name="ffn_bwd_down", grid=(S // tm,),
        in_specs=[_rows(D, tm), _rows(2 * DFF, tm), _resident((DFF, D))],
        out_specs=_rows(2 * DFF, tm),
        out_shape=jax.ShapeDtypeStruct((S, 2 * DFF), BF16),
        compiler_params=_params("parallel"),
    )(dx3b, gu, w_ffn_out)


def _ffn_bwd_up(dgu, x2, dx3, w_lo, w_hi, norm_ffn, w_o, tm=512):
    S = dgu.shape[0]
    HALF = D // 2

    def body(dgu_ref, x2_ref, dx3_ref, lo_ref, hi_ref, gf_ref, wo_ref, dx2_ref, dx2b_ref, dmixo_ref, dvec_ref):
        i = pl.program_id(0)

        @pl.when(i == 0)
        def _():
            dvec_ref[...] = jnp.zeros_like(dvec_ref)

        dgate, dup = dgu_ref[:, 0:DFF], dgu_ref[:, DFF:2 * DFF]
        dh2 = jnp.concatenate([_dot(dgate, w[0:DFF, :]) + _dot(dup, w[DFF:2 * DFF, :]) for w in (lo_ref, hi_ref)], axis=1)
        r2, xh2 = _rms(x2_ref[...])
        dvec_ref[0:1, :] += jnp.sum(dh2 * xh2, axis=0, keepdims=True)
        dx2 = dx3_ref[...] + _rms_bwd(dh2, gf_ref[...], r2, xh2)
        dx2_ref[...] = dx2
        dx2b = dx2.astype(BF16)
        dx2b_ref[...] = dx2b
        dmixo_ref[...] = _dot_nt(dx2b, wo_ref[...])

    return pl.pallas_call(
        body, name="ffn_bwd_up", grid=(S // tm,),
        in_specs=[_rows(2 * DFF, tm), _rows(D, tm), _rows(D, tm), _resident((2 * DFF, HALF)), _resident((2 * DFF, HALF)),
                  _resident((1, D)), _resident((D, D))],
        out_specs=[_rows(D, tm), _rows(D, tm), _rows(D, tm), _resident((8, D))],
        out_shape=[jax.ShapeDtypeStruct((S, D), F32), jax.ShapeDtypeStruct((S, D), BF16), jax.ShapeDtypeStruct((S, D), F32),
                   jax.ShapeDtypeStruct((8, D), F32)],
        compiler_params=_params("arbitrary"),
    )(dgu, x2, dx3, w_lo, w_hi, norm_ffn, w_o)


VEC_ROWS = 16
MAT_WA = 4 * PG
MAT_WX = MAT_WA + NH * HD
MAT_ROWS = MAT_WX + NH * HD


def _mixer_bwd(proj, dmixo, y_pool, y_rnn, hr, kept, wg, scale, w_pool_out, conv_w, conv_b, wa, ba, wx, bx, lam, w_rnn_out,
               exchange=None, exchange_operands=(), tm=256):
    S = proj.shape[0]
    nt = S // tm

    def rev(cols):
        return pl.BlockSpec((tm, cols), lambda i: (nt - 1 - i, 0))

    def halo(rows_, cols):
        per = tm // rows_
        return pl.BlockSpec((rows_, cols), lambda i: (jnp.maximum((nt - 1 - i) * per - 1, 0), 0))

    def body(proj_ref, projh_ref, dmixo_ref, yp_ref, yr_ref, hr_ref, hrh_ref, kept_ref, wg_ref, scale_ref, wpo_ref, cw_ref, cb_ref,
             wa_ref, ba_ref, wx_ref, bx_ref, lam_ref, wro_ref,
             dproj_ref, dypb_ref, dyrb_ref, dmat_ref, dvec_ref,
             q_carry, dv_carry, a_carry, g_carry, g_scr):
        i = pl.program_id(0)
        ti = nt - 1 - i

        @pl.when(i == 0)
        def _():
            q_carry[...] = jnp.zeros_like(q_carry)
            dv_carry[...] = jnp.zeros_like(dv_carry)
            a_carry[...] = jnp.zeros_like(a_carry)
            g_carry[...] = jnp.zeros_like(g_carry)
            dmat_ref[...] = jnp.zeros_like(dmat_ref)
            dvec_ref[...] = jnp.zeros_like(dvec_ref)

        rows = lax.broadcasted_iota(jnp.int32, (tm, 1), 0)
        t_glob = ti * tm + rows
        has_prev = (ti > 0).astype(F32)
        dmixo = dmixo_ref[...]

        s_p = _sigmoid(proj_ref[:, DP + 2 * DR:DP + 2 * DR + D])
        s_r = _sigmoid(proj_ref[:, DP + 2 * DR + D:DIN])
        dproj_ref[:, DP + 2 * DR:DP + 2 * DR + D] = (dmixo * yp_ref[...] * s_p * (1.0 - s_p)).astype(BF16)
        dproj_ref[:, DP + 2 * DR + D:DIN] = (dmixo * yr_ref[...] * s_r * (1.0 - s_r)).astype(BF16)
        dyp = (dmixo * s_p).astype(BF16)
        dyr = (dmixo * s_r).astype(BF16)
        dypb_ref[...] = dyp
        dyrb_ref[...] = dyr

        dz = _dot_nt(dyr, wro_ref[...])
        u_gate = proj_ref[:, DP + DR:DP + 2 * DR]
        gg, dgelu = _gelu(u_gate, with_grad=True)
        hr_t = hr_ref[...]
        dproj_ref[:, DP + DR:DP + 2 * DR] = (dz * hr_t * dgelu).astype(BF16)
        dhr = dz * gg

        sp = _softplus_neg(lam_ref[...])
        v, r, gi, a, mult = (kept_ref[k] for k in range(KEPT))
        inv_mult = 1.0 / mult

        C = jnp.where(rows == tm - 1, a_carry[0:1, :], pltpu.roll(a, tm - 1, axis=0))
        g_carry[0:1, :] = _linear_scan(g_scr, C, dhr, g_carry[0:1, :], reverse=True)
        a_carry[0:1, :] = a[0:1, :]
        g = g_scr[...]

        h_prev = jnp.where(rows == 0, hrh_ref[7:8, :] * has_prev, pltpu.roll(hr_t, 1, axis=0))
        da = g * h_prev
        gm = g * mult
        dmult = g * gi * v
        di = gm * v
        dv = gm * gi
        dlog_a = da * a - dmult * (a * a * inv_mult)
        dvec_ref[4:5, :] += jnp.sum(dlog_a * r, axis=0, keepdims=True)
        dra = (dlog_a * ((-LRU_C) * sp) * r * (1.0 - r))
        drx = di * gi * (1.0 - gi)
        dvec_ref[2:3, :] += jnp.sum(dra, axis=0, keepdims=True)
        dvec_ref[3:4, :] += jnp.sum(drx, axis=0, keepdims=True)
        drab = dra.astype(BF16)
        drxb = drx.astype(BF16)
        vb = v.astype(BF16)
        dvg = []
        for h in range(NH):
            sl = slice(h * HD, (h + 1) * HD)
            dvg.append(_dot_nt(drab[:, sl], wa_ref[h]) + _dot_nt(drxb[:, sl], wx_ref[h]))
            dmat_ref[MAT_WA + h * HD:MAT_WA + (h + 1) * HD, :] += _dot_tn(vb[:, sl], drab[:, sl])
            dmat_ref[MAT_WX + h * HD:MAT_WX + (h + 1) * HD, :] += _dot_tn(vb[:, sl], drxb[:, sl])
        dv = dv + jnp.concatenate(dvg, axis=1)
        dvec_ref[1:2, :] += jnp.sum(dv, axis=0, keepdims=True)
        dvext = jnp.concatenate([dv, dv_carry[...]], axis=0)
        dv_carry[...] = dv[0:CONV_HALO, :]
        n = tm + CONV_HALO
        u_rnn = proj_ref[:, DP:DP + DR]
        du_rnn = dv * cw_ref[3:4, :]
        dvec_ref[8:9, :] += jnp.sum(dv * u_rnn, axis=0, keepdims=True)
        for k in range(3):
            dv_k = pltpu.roll(dvext, n - (3 - k), axis=0)[0:tm, :]
            du_rnn = du_rnn + dv_k * cw_ref[k:k + 1, :]
            dvec_ref[5 + k:6 + k, :] += jnp.sum(dv_k * u_rnn, axis=0, keepdims=True)
        dproj_ref[:, DP:DP + DR] = du_rnn.astype(BF16)

        dpm = _dot_nt(dyp, wpo_ref[...])
        u_pool = proj_ref[:, 0:DP]
        ext = jnp.concatenate([projh_ref[:, 0:DP] * has_prev, u_pool], axis=0)
        sums = _pool_windows(ext, +1)
        scale_v = scale_ref[...]
        qs = []
        dpooled = []
        dscale = []
        for gi_, w in enumerate(WINDOWS):
            sl = slice(gi_ * PG, (gi_ + 1) * PG)
            inv_cnt = 1.0 / jnp.minimum(t_glob + 1, w).astype(F32)
            pooled_b = (sums[gi_][POOL_HALO:, :] * inv_cnt - u_pool[:, sl]).astype(BF16)
            mixed_g = _dot(pooled_b, wg_ref[gi_])
            dscale.append(jnp.sum(dpm[:, sl] * mixed_g, axis=0, keepdims=True))
            dmixed_b = (dpm[:, sl] * scale_v[:, sl]).astype(BF16)
            dmat_ref[gi_ * PG:(gi_ + 1) * PG, :] += _dot_tn(pooled_b, dmixed_b)
            dp_g = _dot_nt(dmixed_b, wg_ref[gi_])
            dpooled.append(dp_g)
            qs.append(dp_g * inv_cnt)
        dvec_ref[0:1, 0:DP] += jnp.concatenate(dscale, axis=1)
        q = jnp.concatenate(qs, axis=1)
        qext = jnp.concatenate([q, q_carry[...]], axis=0)
        q_carry[...] = q[0:POOL_HALO, :]
        tsum = _pool_windows(qext, -1)
        for gi_ in range(4):
            dproj_ref[:, gi_ * PG:(gi_ + 1) * PG] = (tsum[gi_][0:tm, :] - dpooled[gi_]).astype(BF16)

        @pl.when(i == nt - 1)
        def _():
            dvec_ref[4:5, :] = dvec_ref[4:5, :] * (LRU_C * _sigmoid(-lam_ref[...]))

    return _call(
        body, "mixer_bwd", (nt,),
        in_specs=[rev(DIN), halo(POOL_HALO, DIN), rev(D), rev(D), rev(D), rev(DR), halo(8, DR),
                  pl.BlockSpec((KEPT, tm, DR), lambda i: (0, nt - 1 - i, 0)), _resident((4, PG, PG)), _resident((1, DP)), _resident((DP, D)), _resident((4, DR)), _resident((1, DR)),
                  _resident((NH, HD, HD)), _resident((1, DR)), _resident((NH, HD, HD)), _resident((1, DR)),
                  _resident((1, DR)), _resident((DR, D))],
        out_specs=[rev(DIN), rev(D), rev(D), _resident((MAT_ROWS, HD)), _resident((VEC_ROWS, DR))],
        out_shape=[jax.ShapeDtypeStruct((S, DIN), BF16), jax.ShapeDtypeStruct((S, D), BF16),
                   jax.ShapeDtypeStruct((S, D), BF16), jax.ShapeDtypeStruct((MAT_ROWS, HD), F32),
                   jax.ShapeDtypeStruct((VEC_ROWS, DR), F32)],
        scratch_shapes=[pltpu.VMEM((POOL_HALO, DP), F32), pltpu.VMEM((CONV_HALO, DR), F32), pltpu.VMEM((8, DR), F32),
                        pltpu.VMEM((8, DR), F32), pltpu.VMEM((tm, DR), F32)],
        operands=(proj, proj, dmixo, y_pool, y_rnn, hr, hr, kept, wg, scale, w_pool_out, conv_w, conv_b, wa, ba, wx, bx, lam,
                  w_rnn_out),
        exchange=exchange, exchange_operands=exchange_operands)


def _in_bwd(dproj, x, dx2, norm_mix, w_in, exchange=None, exchange_operands=(), tm=512):
    S = x.shape[0]

    def body(dp_ref, x_ref, dx2_ref, g_ref, w_ref, dx_ref, dg_ref):
        i = pl.program_id(0)

        @pl.when(i == 0)
        def _():
            dg_ref[...] = jnp.zeros_like(dg_ref)

        dh = _dot(dp_ref[:, 0:1536], w_ref[0:1536, :])
        dh = dh + _dot(dp_ref[:, 1536:3072], w_ref[1536:3072, :])
        dh = dh + _dot(dp_ref[:, 3072:DIN], w_ref[3072:DIN, :])
        xv = x_ref[...]
        r = lax.rsqrt(jnp.mean(xv * xv, axis=-1, keepdims=True) + EPS)
        xh = xv * r
        dg_ref[0:1, :] += jnp.sum(dh * xh, axis=0, keepdims=True)
        dxh = dh * g_ref[...]
        dx_ref[...] = dx2_ref[...] + r * (dxh - xh * jnp.mean(dxh * xh, axis=-1, keepdims=True))

    return _call(
        body, "in_bwd", (S // tm,),
        in_specs=[_rows(DIN, tm), _rows(D, tm), _rows(D, tm), _resident((1, D)), _resident((DIN, D))],
        out_specs=[_rows(D, tm), _resident((8, D))],
        out_shape=[jax.ShapeDtypeStruct((S, D), F32), jax.ShapeDtypeStruct((8, D), F32)],
        operands=(dproj, x, dx2, norm_mix, w_in), exchange=exchange, exchange_operands=exchange_operands)


def _wgrad(a, b, name, tk, tn, exchange=None, exchange_operands=()):
    S, K = a.shape
    N = b.shape[1]

    def body(a_ref, b_ref, o_ref):
        o_ref[...] = _dot_tn(a_ref[...], b_ref[...]).astype(BF16)

    (out,), exchanged = _call(
        body, name, (K // tk, N // tn),
        in_specs=[pl.BlockSpec((S, tk), lambda k, n: (0, k)), pl.BlockSpec((S, tn), lambda k, n: (0, n))],
        out_specs=[pl.BlockSpec((tk, tn), lambda k, n: (k, n))],
        out_shape=[jax.ShapeDtypeStruct((K, N), BF16)],
        operands=(a, b), exchange=exchange, exchange_operands=exchange_operands)
    return (out, exchanged) if exchange is not None else out


VEC_SCALE, VEC_CONV_B, VEC_BA, VEC_BX, VEC_LAM, VEC_CONV_W, VEC_NORM_FINAL, VEC_NORM_FFN = 0, 1, 2, 3, 4, 5, 9, 10
VEC_LOSS = 11


class _Big:
    def __init__(self, name, rows, cols, axis, n, dtype=BF16, transposed=False, src_cols=None):
        self.name, self.rows, self.cols, self.axis, self.n, self.dtype = name, rows, cols, axis, n, dtype
        self.transposed = transposed
        self.src_cols = src_cols
        self.block_shape = (rows, n) if axis == 1 else (n, cols)

    def block(self, ref, p):
        if self.axis == 1:
            return ref.at[:, pl.ds(pl.multiple_of(p * self.n, 128), self.n)]
        return ref.at[pl.ds(pl.multiple_of(p * self.n, 16 if self.dtype == BF16 else 8), self.n), :]


BIG = (_Big("w_in", DIN, D, 0, DIN // 8, transposed=True), _Big("w_pool_out", DP, D, 1, D // 8),
       _Big("w_rnn_out", DR, D, 0, DR // 8), _Big("w_o", D, D, 0, D // 8),
       _Big("w_ffn_in", 2 * DFF, D, 0, 2 * DFF // 8, transposed=True), _Big("w_ffn_out", DFF, D, 0, DFF // 8))
CONV_W = _Big("conv_w", 8, DR, 1, DR // 8, F32)
W_FFN_IN_HALVES = (_Big("w_ffn_in_lo", 2 * DFF, D // 2, 0, 2 * DFF // 8, src_cols=(0, D // 2)),
                   _Big("w_ffn_in_hi", 2 * DFF, D // 2, 0, 2 * DFF // 8, src_cols=(D // 2, D)))
GATHERED = BIG + (CONV_W,) + W_FFN_IN_HALVES

HBM_SPEC = pl.BlockSpec(memory_space=pl.ANY)
VMEM_SPEC = pl.BlockSpec(memory_space=pltpu.VMEM)


def _place():
    x, y, c = (lax.axis_index(a) for a in MESH_AXES)
    other_chips = [(1 - x, y), (x, 1 - y), (1 - x, 1 - y)]
    return x, y, c, other_chips


def _remote(src, dst, send_sems, recv_sems, idx, to):
    return pltpu.make_async_remote_copy(src_ref=src, dst_ref=dst, send_sem=send_sems.at[idx], recv_sem=recv_sems.at[idx],
                                        device_id=to, device_id_type=MESH)


def _device_index(chip, core):
    return 4 * chip[0] + 2 * chip[1] + core


class _Gather:
    def __init__(self, tensors):
        self.tensors = tuple(tensors)
        n = len(self.tensors)
        self.in_specs = [HBM_SPEC] * n
        self.out_specs = [HBM_SPEC] * n
        self.out_shape = [jax.ShapeDtypeStruct((T.rows, T.cols), T.dtype) for T in self.tensors]
        self.scratch_shapes = [pltpu.VMEM(T.block_shape, T.dtype) for T in self.tensors] + [
            pltpu.VMEM(T.block_shape, F32) for T in self.tensors] + [
            pltpu.SemaphoreType.DMA((n, 7)), pltpu.SemaphoreType.DMA((n, 7)), pltpu.SemaphoreType.DMA((n, 2))]

    def middles(self, steps):
        return [(steps // 2, self.relay), (steps - 1, self.middle)]

    def _copies(self, ins, outs, scratch):
        n = len(self.tensors)
        mine, raw, (send_sems, recv_sems, loc_sems) = scratch[:n], scratch[n:2 * n], scratch[2 * n:]
        x, y, c, chips = _place()
        sibling = (x, y, 1 - c)
        me = _device_index((x, y), c)
        relay_from = (jnp.where(c == 0, 1 - x, x), jnp.where(c == 0, y, 1 - y))
        relay_to = (jnp.where(c == 0, x, 1 - x), jnp.where(c == 0, 1 - y, y))
        loads, stores, first, relays, passed, arrivals, late = [], [], [], [], [], [], []
        for t, T in enumerate(self.tensors):
            place = T.block(outs[t], me)
            src = ins[t] if T.src_cols is None else ins[t].at[:, T.src_cols[0]:T.src_cols[1]]
            loads.append(pltpu.make_async_copy(src, raw[t], loc_sems.at[t, 0]))
            stores.append(pltpu.make_async_copy(mine[t], place, loc_sems.at[t, 1]))
            first.append(_remote(mine[t], place, send_sems, recv_sems, (t, 0), sibling))
            theirs = T.block(outs[t], _device_index((x, y), 1 - c))
            late.append(_remote(theirs, theirs, send_sems, recv_sems, (t, 0), sibling))
            relayed = T.block(outs[t], _device_index(relay_from, c))
            relays.append(_remote(relayed, relayed, send_sems, recv_sems, (t, 3), (*relay_to, c)))
            for k, chip in enumerate(chips):
                if k < 2:
                    first.append(_remote(mine[t], place, send_sems, recv_sems, (t, 1 + k), (*chip, c)))
                land = T.block(outs[t], _device_index(chip, c))
                arrivals.append(_remote(land, land, send_sems, recv_sems, (t, 1 + k), sibling))
                passed.append(_remote(land, land, send_sems, recv_sems, (t, 4 + k), sibling))
                theirs = T.block(outs[t], _device_index(chip, 1 - c))
                late.append(_remote(theirs, theirs, send_sems, recv_sems, (t, 4 + k), sibling))
        return loads, stores, first, relays, passed, arrivals, late

    def start(self, ins, outs, scratch):
        loads, stores, first, _, _, _, _ = self._copies(ins, outs, scratch)
        n = len(self.tensors)
        for cp in loads:
            cp.start()
        for t, cp in enumerate(loads):
            cp.wait()
            scratch[t][...] = scratch[n + t][...].astype(self.tensors[t].dtype)
        for cp in stores + first:
            cp.start()

    def relay(self, ins, outs, scratch, skip=0):
        _, _, _, relays, passed, arrivals, _ = self._copies(ins, outs, scratch)
        for t in range(skip, len(self.tensors)):
            arrivals[3 * t].wait_recv()
            arrivals[3 * t + 1].wait_recv()
            for cp in (relays[t], passed[3 * t], passed[3 * t + 1]):
                cp.start()

    def middle(self, ins, outs, scratch, skip=0):
        _, _, _, _, passed, arrivals, _ = self._copies(ins, outs, scratch)
        for t in range(skip, len(self.tensors)):
            arrivals[3 * t + 2].wait_recv()
            passed[3 * t + 2].start()

    def finish(self, ins, outs, scratch, skip=0):
        _, stores, first, relays, passed, _, late = self._copies(ins, outs, scratch)
        for cp in late[4 * skip:]:
            cp.wait_recv()
        for cp in first + relays + passed:
            cp.wait_send()
        for cp in stores[skip:]:
            cp.wait()


def _in_proj_gather(x, norm_mix, blocks, tensors, order, tm=512):
    S = x.shape[0]
    nt = S // tm
    n = len(tensors)
    gather = _Gather(tensors)
    CB = 2 * tensors[0].n

    def body(order_ref, x_ref, g_ref, *refs):
        ins, (proj_ref, h_ref), outs = refs[:n], refs[n:n + 2], refs[n + 2:2 * n + 2]
        (h_all, w_chip, w_sem), scratch = refs[2 * n + 2:2 * n + 5], refs[2 * n + 5:]
        q, i = pl.program_id(0), pl.program_id(1)
        _, stores, _, relays, passed, arrivals, late = gather._copies(ins, outs, scratch)

        def fetch(turn):
            rows = outs[0].at[pl.ds(pl.multiple_of(order_ref[turn] * CB, 16), CB), :]
            cp = pltpu.make_async_copy(rows, w_chip, w_sem)
            cp.start()
            cp.wait()

        @pl.when((q == 0) & (i == 0))
        def _():
            gather.start(ins, outs, scratch)
            late[0].wait_recv()
            stores[0].wait()
            fetch(0)

        @pl.when((q == 1) & (i == 0))
        def _():
            arrivals[0].wait_recv()
            arrivals[1].wait_recv()
            for cp in (relays[0], passed[0], passed[1]):
                cp.start()
            late[1].wait_recv()
            fetch(1)

        @pl.when((q == 2) & (i == 0))
        def _():
            late[2].wait_recv()
            fetch(2)
            gather.relay(ins, outs, scratch, skip=1)

        @pl.when((q == 3) & (i == 0))
        def _():
            arrivals[2].wait_recv()
            passed[2].start()
            late[3].wait_recv()
            fetch(3)

        rows = pl.ds(pl.multiple_of(i * tm, tm), tm)

        @pl.when(q == 0)
        def _():
            xv = x_ref[...]
            r = lax.rsqrt(jnp.mean(xv * xv, axis=-1, keepdims=True) + EPS)
            h = (xv * r * g_ref[...]).astype(BF16)
            h_all[rows, :] = h
            h_ref[...] = h

        proj_ref[...] = _dot_nt(h_all[rows, :], w_chip[...])

        @pl.when((q == 3) & (i == nt - 1))
        def _():
            gather.middle(ins, outs, scratch, skip=1)
            gather.finish(ins, outs, scratch, skip=1)

    row_tile = lambda q, i, order: (jnp.where(q == 0, i, nt - 1), 0)
    whole = lambda shape: pl.BlockSpec(shape, lambda q, i, order: (0,) * len(shape), pipeline_mode=pl.Buffered(1))
    outs = pl.pallas_call(
        body, name="in_proj_gather",
        grid_spec=pltpu.PrefetchScalarGridSpec(
            num_scalar_prefetch=1, grid=(4, nt),
            in_specs=[pl.BlockSpec((tm, D), row_tile), whole((1, D))] + gather.in_specs,
            out_specs=[pl.BlockSpec((tm, CB), lambda q, i, order: (i, order[q])), pl.BlockSpec((tm, D), row_tile)]
            + gather.out_specs,
            scratch_shapes=[pltpu.VMEM((S, D), BF16), pltpu.VMEM((CB, D), BF16), pltpu.SemaphoreType.DMA]
            + gather.scratch_shapes),
        out_shape=[jax.ShapeDtypeStruct((S, DIN), F32), jax.ShapeDtypeStruct((S, D), BF16)] + gather.out_shape,
        compiler_params=_params("arbitrary", "arbitrary"),
    )(order, x, norm_mix, *blocks)
    return outs[:2], outs[2:]


PAIR_ROWS = 32


def _pair_reduce(grads, tensors, name):
    nt = len(tensors)

    def body(*refs):
        ins, own_out, sums_out, landed, mine = (refs[k * nt:(k + 1) * nt] for k in range(5))
        send_sems, recv_sems, loc_sems = refs[5 * nt:]
        x, y, c, chips = _place()
        chip_of = [2 * chip[0] + chip[1] for chip in chips]
        swaps, loads = [], []
        for t, T in enumerate(tensors):
            for j in range(4):
                swaps.append(_remote(T.block(ins[t], 2 * j + 1 - c), landed[t].at[j], send_sems, recv_sems, (t, j),
                                     (x, y, 1 - c)))
            for k in range(3):
                loads.append(pltpu.make_async_copy(T.block(ins[t], 2 * chip_of[k] + c), mine[t].at[k], loc_sems.at[t, k]))
        for cp in swaps + loads:
            cp.start()
        for cp in loads:
            cp.wait()
        for cp in swaps:
            cp.wait_recv()
        stores = []
        for t, T in enumerate(tensors):
            for k in range(3):
                acc, got = mine[t].at[k], landed[t].at[chip_of[k]]

                def add(i, carry, acc=acc, got=got):
                    rows = pl.ds(pl.multiple_of(i * PAIR_ROWS, PAIR_ROWS), PAIR_ROWS)
                    acc[rows, :] = (acc[rows, :].astype(F32) + got[rows, :].astype(F32)).astype(BF16)
                    return carry

                lax.fori_loop(0, T.block_shape[0] // PAIR_ROWS, add, 0)
            stores.append(pltpu.make_async_copy(mine[t], sums_out[t], loc_sems.at[t, 3]))
            stores.append(pltpu.make_async_copy(landed[t].at[2 * x + y], own_out[t], loc_sems.at[t, 4]))
        for cp in stores:
            cp.start()
        for cp in swaps:
            cp.wait_send()
        for cp in stores:
            cp.wait()

    blocks = [T.block_shape for T in tensors]
    return pl.pallas_call(
        body, name=name,
        in_specs=[HBM_SPEC] * nt, out_specs=[HBM_SPEC] * (2 * nt),
        out_shape=[jax.ShapeDtypeStruct(b, BF16) for b in blocks] + [jax.ShapeDtypeStruct((3,) + b, BF16) for b in blocks],
        scratch_shapes=[pltpu.VMEM((4,) + b, BF16) for b in blocks] + [pltpu.VMEM((3,) + b, BF16) for b in blocks]
        + [pltpu.SemaphoreType.DMA((nt, 4)), pltpu.SemaphoreType.DMA((nt, 4)), pltpu.SemaphoreType.DMA((nt, 5))],
        compiler_params=pltpu.CompilerParams(vmem_limit_bytes=VMEM_LIMIT),
    )(*grads)


class _Scatter:
    def middles(self, steps):
        return []

    def __init__(self, tensors):
        n = len(tensors)
        self.in_specs = [HBM_SPEC] * n
        self.out_specs = [HBM_SPEC] * n
        self.out_shape = [jax.ShapeDtypeStruct((3,) + T.block_shape, BF16) for T in tensors]
        self.scratch_shapes = [pltpu.SemaphoreType.DMA((n, 3)), pltpu.SemaphoreType.DMA((n, 3))]

    def _copies(self, ins, outs, scratch):
        send_sems, recv_sems = scratch
        x, y, c, chips = _place()
        return [_remote(ins[t].at[k], outs[t].at[k], send_sems, recv_sems, (t, k), (*chip, c))
                for t in range(len(ins)) for k, chip in enumerate(chips)]

    def start(self, ins, outs, scratch):
        for cp in self._copies(ins, outs, scratch):
            cp.start()

    def finish(self, ins, outs, scratch):
        for cp in self._copies(ins, outs, scratch):
            cp.wait()


def _adamw(w, g, m, v):
    m = ADAM_B1 * m + (1.0 - ADAM_B1) * g
    v = ADAM_B2 * v + (1.0 - ADAM_B2) * (g * g)
    m_hat = m / (1.0 - ADAM_B1 ** ADAM_STEP)
    v_hat = v / (1.0 - ADAM_B2 ** ADAM_STEP)
    delta = -ADAM_LR * (m_hat / (jnp.sqrt(v_hat) + ADAM_EPS) + ADAM_WD * w)
    return delta, m, v


def _final_sum(T, g, lz1, lz2, where, w, m, v):
    rows, cols = T.block_shape
    sub = 4 if T.axis == 0 and rows % 64 == 0 and rows > 256 else 1
    blk = (rows // sub, cols)

    def body(where_ref, g_ref, l1_ref, l2_ref, w_ref, m_ref, v_ref, g_out, d_out, m_out, v_out):
        tot = g_ref[...].astype(F32) + l1_ref[...].astype(F32)
        for k in range(3):
            tot = tot + l2_ref[k].astype(F32)
        g_out[...] = tot
        d_out[...], m_out[...], v_out[...] = _adamw(w_ref[...], tot, m_ref[...], v_ref[...])

    def in_whole(r, wh):
        p = wh[0]
        return (0, p) if T.axis == 1 else (p * sub + r, 0)

    own = pl.BlockSpec(blk, lambda r, wh: (r, 0))
    return pl.pallas_call(
        body, name="grad_final_" + T.name,
        grid_spec=pltpu.PrefetchScalarGridSpec(
            num_scalar_prefetch=1, grid=(sub,),
            in_specs=[pl.BlockSpec(blk, in_whole),
                      own,
                      pl.BlockSpec((3,) + blk, lambda r, wh: (0, r, 0)), own, own, own],
            out_specs=[own] * 4),
        out_shape=[jax.ShapeDtypeStruct(T.block_shape, F32)] * 4,
        compiler_params=_params("arbitrary"),
    )(where, g, lz1, lz2, w, m, v)


VEC_PIECE = DR // 8


class _AllReduce:
    def __init__(self, items):
        self.items = tuple(items)
        n = len(self.items)
        self.in_specs = [HBM_SPEC] * n
        self.out_specs = [HBM_SPEC] * n
        self.out_shape = [jax.ShapeDtypeStruct(shape, F32) for shape, _ in self.items]
        pieces = [(shape[0] // 8, shape[1]) if axis == 0 else (shape[0], shape[1] // 8) for shape, axis in self.items]
        self.scratch_shapes = ([pltpu.VMEM((8,) + p, F32) for p in pieces] + [pltpu.VMEM(p, F32) for p in pieces] + [
            pltpu.SemaphoreType.DMA((2 * n, 8)), pltpu.SemaphoreType.DMA((2 * n, 8)), pltpu.SemaphoreType.DMA((2 * n,))])

    def middles(self, steps):
        return [(steps // 2, self.middle)]

    def _copies(self, ins, outs, scratch):
        n = len(self.items)
        landed, sums, (send_sems, recv_sems, loc_sems) = scratch[:n], scratch[n:2 * n], scratch[2 * n:]
        x, y, c, _ = _place()
        me = _device_index((x, y), c)

        def peer(r):
            return (1 - x if r & 4 else x, 1 - y if r & 2 else y, 1 - c if r & 1 else c)

        def piece(i, ref, p):
            shape, axis = self.items[i]
            if axis == 0:
                rows = shape[0] // 8
                return ref.at[pl.ds(pl.multiple_of(p * rows, 8), rows), :]
            cols = shape[1] // 8
            return ref.at[:, pl.ds(pl.multiple_of(p * cols, 128), cols)]

        own, scatter, arrivals, keep, spread, late = [], [], [], [], [], []
        for i in range(n):
            own.append(pltpu.make_async_copy(piece(i, ins[i], me), landed[i].at[0], loc_sems.at[2 * i]))
            keep.append(pltpu.make_async_copy(sums[i], piece(i, outs[i], me), loc_sems.at[2 * i + 1]))
            for r in range(1, 8):
                to = peer(r)
                p = _device_index(to[:2], to[2])
                scatter.append(_remote(piece(i, ins[i], p), landed[i].at[r], send_sems, recv_sems, (2 * i, r), to))
                spread.append(_remote(sums[i], piece(i, outs[i], me), send_sems, recv_sems, (2 * i + 1, r), to))
                late.append(_remote(sums[i], piece(i, outs[i], p), send_sems, recv_sems, (2 * i + 1, r), to))
        return own, scatter, keep, spread, late, landed, sums

    def start(self, ins, outs, scratch):
        own, scatter, _, _, _, _, _ = self._copies(ins, outs, scratch)
        for cp in own + scatter:
            cp.start()

    def middle(self, ins, outs, scratch):
        own, scatter, keep, spread, _, landed, sums = self._copies(ins, outs, scratch)
        for cp in own:
            cp.wait()
        for cp in scatter:
            cp.wait_recv()
        for i in range(len(self.items)):
            total = landed[i][0]
            for r in range(1, 8):
                total = total + landed[i][r]
            sums[i][...] = total
        for cp in keep + spread:
            cp.start()

    def finish(self, ins, outs, scratch):
        _, scatter, keep, spread, late, _, _ = self._copies(ins, outs, scratch)
        for cp in late:
            cp.wait_recv()
        for cp in scatter + spread:
            cp.wait_send()
        for cp in keep:
            cp.wait()


class _Both:
    def __init__(self, a, b):
        self.a, self.b = a, b
        self.in_specs, self.out_specs = a.in_specs + b.in_specs, a.out_specs + b.out_specs
        self.out_shape, self.scratch_shapes = a.out_shape + b.out_shape, a.scratch_shapes + b.scratch_shapes

    def _each(self, ins, outs, scratch):
        a = self.a
        i, o, s = len(a.in_specs), len(a.out_specs), len(a.scratch_shapes)
        return (a, ins[:i], outs[:o], scratch[:s]), (self.b, ins[i:], outs[o:], scratch[s:])

    def middles(self, steps):
        def of(which, middle):
            return lambda ins, outs, scratch: middle(*self._each(ins, outs, scratch)[which][1:])
        return [(at, of(which, middle)) for which, e in enumerate((self.a, self.b)) for at, middle in e.middles(steps)]

    def start(self, ins, outs, scratch):
        for e, i, o, s in self._each(ins, outs, scratch):
            e.start(i, o, s)

    def finish(self, ins, outs, scratch):
        for e, i, o, s in self._each(ins, outs, scratch):
            e.finish(i, o, s)


def _all_reduce(arrays, items, name):
    reduce = _AllReduce(items)
    n = len(items)

    def body(*refs):
        ins, outs, scratch = refs[:n], refs[n:2 * n], refs[2 * n:]
        reduce.start(ins, outs, scratch)
        reduce.middle(ins, outs, scratch)
        reduce.finish(ins, outs, scratch)

    return pl.pallas_call(
        body, name=name, in_specs=reduce.in_specs, out_specs=reduce.out_specs, out_shape=reduce.out_shape,
        scratch_shapes=reduce.scratch_shapes,
    )(*arrays)


def _adam_small(grads, wmv):
    n = len(grads)

    def body(*refs):
        g_refs, rest = refs[:n], refs[n:]
        ins, outs = rest[:3 * n], rest[3 * n:]
        for i in range(n):
            d, m, v = _adamw(ins[3 * i][...], g_refs[i][...], ins[3 * i + 1][...], ins[3 * i + 2][...])
            outs[3 * i][...], outs[3 * i + 1][...], outs[3 * i + 2][...] = d, m, v

    flat = [a for t in wmv for a in t]
    return pl.pallas_call(
        body, name="adam_small",
        in_specs=[VMEM_SPEC] * (4 * n), out_specs=[VMEM_SPEC] * (3 * n),
        out_shape=[jax.ShapeDtypeStruct(a.shape, F32) for a in flat],
    )(*grads, *flat)


WEIGHT_NAMES = ("norm_mix", "w_in", "w_pool_grp", "pool_scale", "w_pool_out", "conv_w", "conv_b", "w_rg_a", "b_rg_a", "w_rg_x",
                "b_rg_x", "lru_lambda", "w_rnn_out", "w_o", "norm_ffn", "w_ffn_in", "w_ffn_out", "norm_final")


def kernel(x, norm_mix, w_in, w_pool_grp, pool_scale, w_pool_out, conv_w, conv_b, w_rg_a, b_rg_a, w_rg_x, b_rg_x, lru_lambda, w_rnn_out, w_o, norm_ffn, w_ffn_in, w_ffn_out, norm_final, loss_target, m_norm_mix, m_w_in, m_w_pool_grp, m_pool_scale, m_w_pool_out, m_conv_w, m_conv_b, m_w_rg_a, m_b_rg_a, m_w_rg_x, m_b_rg_x, m_lru_lambda, m_w_rnn_out, m_w_o, m_norm_ffn, m_w_ffn_in, m_w_ffn_out, m_norm_final, v_norm_mix, v_w_in, v_w_pool_grp, v_pool_scale, v_w_pool_out, v_conv_w, v_conv_b, v_w_rg_a, v_b_rg_a, v_w_rg_x, v_b_rg_x, v_lru_lambda, v_w_rnn_out, v_w_o, v_norm_ffn, v_w_ffn_in, v_w_ffn_out, v_norm_final):
    w = dict(norm_mix=norm_mix, w_in=w_in, w_pool_grp=w_pool_grp, pool_scale=pool_scale, w_pool_out=w_pool_out, conv_w=conv_w,
             conv_b=conv_b, w_rg_a=w_rg_a, b_rg_a=b_rg_a, w_rg_x=w_rg_x, b_rg_x=b_rg_x, lru_lambda=lru_lambda,
             w_rnn_out=w_rnn_out, w_o=w_o, norm_ffn=norm_ffn, w_ffn_in=w_ffn_in, w_ffn_out=w_ffn_out, norm_final=norm_final)
    m = dict(norm_mix=m_norm_mix, w_in=m_w_in, w_pool_grp=m_w_pool_grp, pool_scale=m_pool_scale, w_pool_out=m_w_pool_out,
             conv_w=m_conv_w, conv_b=m_conv_b, w_rg_a=m_w_rg_a, b_rg_a=m_b_rg_a, w_rg_x=m_w_rg_x, b_rg_x=m_b_rg_x,
             lru_lambda=m_lru_lambda, w_rnn_out=m_w_rnn_out, w_o=m_w_o, norm_ffn=m_norm_ffn, w_ffn_in=m_w_ffn_in,
             w_ffn_out=m_w_ffn_out, norm_final=m_norm_final)
    v = dict(norm_mix=v_norm_mix, w_in=v_w_in, w_pool_grp=v_w_pool_grp, pool_scale=v_pool_scale, w_pool_out=v_w_pool_out,
             conv_w=v_conv_w, conv_b=v_conv_b, w_rg_a=v_w_rg_a, b_rg_a=v_b_rg_a, w_rg_x=v_w_rg_x, b_rg_x=v_b_rg_x,
             lru_lambda=v_lru_lambda, w_rnn_out=v_w_rnn_out, w_o=v_w_o, norm_ffn=v_norm_ffn, w_ffn_in=v_w_ffn_in,
             w_ffn_out=v_w_ffn_out, norm_final=v_norm_final)
    xi, yi, ci = (lax.axis_index(a) for a in MESH_AXES)
    chip = 2 * xi + yi

    def held(T, a):
        return jnp.swapaxes(a, 0, 1) if T.transposed else a

    where = jnp.stack([2 * chip + ci]).astype(jnp.int32)
    by_name = {T.name: T for T in GATHERED}
    block = {T.name: held(T, w[T.name][0]) for T in BIG}
    block["conv_w"] = jnp.pad(conv_w[0], ((0, CONV_W.rows - 4), (0, 0)))
    block["w_ffn_in_lo"] = block["w_ffn_in_hi"] = block["w_ffn_in"]

    def gather_of(*names):
        return dict(exchange=_Gather([by_name[n] for n in names]), exchange_operands=[block[n] for n in names])

    def pair_sums(names, partials, tag):
        out = _pair_reduce(partials, [by_name[n] for n in names], "grad_pair_reduce_" + tag)
        return list(out[:len(names)]), list(out[len(names):])

    xs, target = x[0], loss_target[0]
    wg_b, wa_b, wx_b = (a[0].astype(BF16) for a in (w_pool_grp, w_rg_a, w_rg_x))
    ba2, bx2 = b_rg_a.reshape(1, DR), b_rg_x.reshape(1, DR)
    first = ("w_in", "w_pool_out", "w_rnn_out", "conv_w", "w_o")
    order = jnp.stack([chip, 2 * (1 - xi) + yi, 2 * xi + (1 - yi), 2 * (1 - xi) + (1 - yi)]).astype(jnp.int32)
    (proj, h1), (w_in_g, w_pool_out_g, w_rnn_out_g, conv_g, w_o_g) = _in_proj_gather(
        xs, norm_mix, [block[n] for n in first], [by_name[n] for n in first], order)
    mixer_weights = (wg_b, pool_scale, w_pool_out_g, conv_g[0:4], conv_b, wa_b, ba2, wx_b, bx2, lru_lambda, w_rnn_out_g)
    (pm, y_pool, hr, z, y_rnn, kept), (w_ffn_lo_g, w_ffn_hi_g) = _mixer_fwd(
        proj, *mixer_weights, **gather_of("w_ffn_in_lo", "w_ffn_in_hi"))
    (mix, x2, h2), _ = _merge_out(xs, proj, y_pool, y_rnn, w_o_g, norm_ffn)
    (gu, act), (w_ffn_out_g,) = _ffn_up(h2, w_ffn_lo_g, w_ffn_hi_g, **gather_of("w_ffn_out"))
    dx3, dx3b, loss_part, dvec_fin = _ffn_down_loss(act, x2, target, w_ffn_out_g, norm_final.reshape(1, D))

    dgu = _ffn_bwd_down(dx3b, gu, w_ffn_out_g)
    dx2, dx2b, dmixo, dvec_ffn = _ffn_bwd_up(dgu, x2, dx3, w_ffn_lo_g, w_ffn_hi_g, norm_ffn, w_o_g)
    names_a = ("w_ffn_in", "w_ffn_out", "w_o")
    part_a = [_wgrad(dgu, h2, "wgrad_ffn_in", 1408, 512), _wgrad(act, dx3b, "wgrad_ffn_out", 1408, 512),
              _wgrad(mix, dx2b, "wgrad_o", 1024, 256)]
    lz1_a, sums_a = pair_sums(names_a, part_a, "ffn")
    (dproj, dypb, dyrb, dmat, dvec_mix), lz2_a = _mixer_bwd(
        proj, dmixo, y_pool, y_rnn, hr, kept, *mixer_weights,
        exchange=_Scatter([by_name[n] for n in names_a]), exchange_operands=sums_a)
    names_b = ("w_pool_out", "w_rnn_out")
    part_b = [_wgrad(pm, dypb, "wgrad_pool_out", 512, 256), _wgrad(z, dyrb, "wgrad_rnn_out", 1024, 256)]
    lz1_b, sums_b = pair_sums(names_b, part_b, "mix")
    dvec = jnp.concatenate([dvec_mix[0:9], dvec_fin[0:1], dvec_ffn[0:1], jnp.pad(loss_part, ((0, 0), (0, DR - 1))),
                            jnp.zeros((VEC_ROWS - 12, DR), F32)], axis=0)
    g_in, exchanged = _wgrad(
        dproj, h1, "wgrad_in", 1152, 1024,
        exchange=_Both(_Scatter([by_name[n] for n in names_b]), _AllReduce([((MAT_ROWS, HD), 0), ((VEC_ROWS, DR), 1)])),
        exchange_operands=sums_b + [dmat, dvec])
    lz2_b, (mat, vec) = exchanged[:2], exchanged[2:]
    loss = vec[VEC_LOSS, 0]
    lz1_c, sums_c = pair_sums(("w_in",), [g_in], "in")
    (grad_x, dvec_in), lz2_c = _in_bwd(dproj, xs, dx2, norm_mix, w_in_g,
                                       exchange=_Scatter([by_name["w_in"]]), exchange_operands=sums_c)
    (vec_in,) = _all_reduce([dvec_in], [((8, D), 1)], "all_reduce_norm_mix")

    grads, delta, new_m, new_v = {}, {}, {}, {}
    for n, g, l1, l2 in zip(names_a + names_b + ("w_in",), part_a + part_b + [g_in], lz1_a + lz1_b + lz1_c,
                            lz2_a + lz2_b + lz2_c):
        T = by_name[n]
        out = _final_sum(T, g, l1, l2, where, held(T, w[n][0]), held(T, m[n][0]), held(T, v[n][0]))
        grads[n], delta[n], new_m[n], new_v[n] = (held(T, a) for a in out)
    me = 4 * xi + 2 * yi + ci
    small_grads = dict(
        w_pool_grp=mat[0:MAT_WA], w_rg_a=mat[MAT_WA:MAT_WX], w_rg_x=mat[MAT_WX:MAT_ROWS],
        pool_scale=vec[VEC_SCALE:VEC_SCALE + 1, 0:DP], conv_b=vec[VEC_CONV_B:VEC_CONV_B + 1],
        b_rg_a=vec[VEC_BA:VEC_BA + 1], b_rg_x=vec[VEC_BX:VEC_BX + 1], lru_lambda=vec[VEC_LAM:VEC_LAM + 1],
        conv_w=lax.dynamic_slice(vec, (VEC_CONV_W, VEC_PIECE * me), (4, VEC_PIECE)),
        norm_final=vec[VEC_NORM_FINAL:VEC_NORM_FINAL + 1], norm_ffn=vec[VEC_NORM_FFN:VEC_NORM_FFN + 1],
        norm_mix=vec_in[0:1])
    names = list(small_grads)
    as2d = lambda a, g: a.reshape(g.shape)
    upd = _adam_small([small_grads[n] for n in names],
                      [(as2d(w[n], small_grads[n]), as2d(m[n], small_grads[n]), as2d(v[n], small_grads[n])) for n in names])
    for i, n in enumerate(names):
        grads[n] = small_grads[n]
        delta[n], new_m[n], new_v[n] = upd[3 * i:3 * i + 3]

    shaped = lambda d: [d[n].reshape(w[n].shape) for n in WEIGHT_NAMES]
    return (loss, grad_x[None], *shaped(grads), *shaped(delta), *shaped(new_m), *shaped(new_v))
```

```python
import math

import jax
import jax.numpy as jnp
from jax import lax
from jax.experimental import pallas as pl
from jax.experimental.pallas import tpu as pltpu

F32 = jnp.float32
BF16 = jnp.bfloat16

D = 1024
DP = 512
PG = 128
WINDOWS = (2, 4, 8, 16)
DR = 1024
NH = 8
HD = 128
DIN = 4608
DFF = 2816
EPS = 1e-6
LRU_C = 8.0
POOL_HALO = 16
CONV_HALO = 8
KEPT = 5

ADAM_LR = 0.001
ADAM_B1 = 0.9
ADAM_B2 = 0.999
ADAM_EPS = 1e-08
ADAM_WD = 0.01
ADAM_STEP = 10

VMEM_LIMIT = 56 * 1024 * 1024
MESH_AXES = ("x", "y", "c")
MESH = pl.DeviceIdType.MESH


def _dot(a, b):
    return jnp.dot(a, b, preferred_element_type=F32)


def _dot_nt(a, b):
    return lax.dot_general(a, b, (((1,), (1,)), ((), ())), preferred_element_type=F32)


def _dot_tn(a, b):
    return lax.dot_general(a, b, (((0,), (0,)), ((), ())), preferred_element_type=F32)


def _params(*sem):
    return pltpu.CompilerParams(dimension_semantics=sem, vmem_limit_bytes=VMEM_LIMIT)


def _resident(shape):
    nd = len(shape)
    return pl.BlockSpec(shape, lambda i: (0,) * nd, pipeline_mode=pl.Buffered(1))


def _rows(shape_cols, tm):
    return pl.BlockSpec((tm, shape_cols), lambda i: (i, 0))


def _call(body, name, grid, in_specs, out_specs, out_shape, operands, scratch_shapes=(), exchange=None, exchange_operands=()):
    n_in, n_out, n_scr = len(in_specs), len(out_specs), len(scratch_shapes)
    steps = math.prod(grid)
    if exchange is None:
        outs = pl.pallas_call(body, name=name, grid=grid, in_specs=in_specs, out_specs=out_specs, out_shape=out_shape,
                              scratch_shapes=list(scratch_shapes), compiler_params=_params(*["arbitrary"] * len(grid)))(*operands)
        return outs, []
    e_in, e_out = len(exchange.in_specs), len(exchange.out_specs)

    def hosted(*refs):
        ins, refs = refs[:n_in], refs[n_in:]
        e_ins, refs = refs[:e_in], refs[e_in:]
        outs, refs = refs[:n_out], refs[n_out:]
        e_outs, refs = refs[:e_out], refs[e_out:]
        scr, e_scr = refs[:n_scr], refs[n_scr:]
        step = pl.program_id(0)
        for axis in range(1, len(grid)):
            step = step * grid[axis] + pl.program_id(axis)
        pl.when(step == 0)(lambda: exchange.start(e_ins, e_outs, e_scr))
        for at, middle in exchange.middles(steps):
            pl.when(step == at)(lambda middle=middle: middle(e_ins, e_outs, e_scr))
        body(*ins, *outs, *scr)
        pl.when(step == steps - 1)(lambda: exchange.finish(e_ins, e_outs, e_scr))

    outs = pl.pallas_call(
        hosted, name=name, grid=grid, in_specs=list(in_specs) + exchange.in_specs,
        out_specs=list(out_specs) + exchange.out_specs, out_shape=list(out_shape) + exchange.out_shape,
        scratch_shapes=list(scratch_shapes) + exchange.scratch_shapes,
        compiler_params=_params(*["arbitrary"] * len(grid)))(*operands, *exchange_operands)
    return outs[:n_out], outs[n_out:]


GELU_C = math.sqrt(2.0 / math.pi)
GELU_K = 0.044715 * GELU_C


def _gelu(x, with_grad=False):
    x2 = x * x
    t = jnp.tanh(x * (GELU_C + GELU_K * x2))
    hx = 0.5 * x
    y = hx + hx * t
    if not with_grad:
        return y
    return y, 0.5 + 0.5 * t + hx * (1.0 - t * t) * (GELU_C + (3.0 * GELU_K) * x2)


def _softplus_neg(lam):
    z = jnp.exp(-jnp.abs(lam))
    u = 1.0 + z
    dlt = u - 1.0
    log1p = jnp.where(dlt == 0.0, z, jnp.log(u) * (z / jnp.where(dlt == 0.0, 1.0, dlt)))
    return jnp.maximum(-lam, 0.0) + log1p


def _sigmoid(x):
    return 0.5 * jnp.tanh(0.5 * x) + 0.5


def _linear_scan(out_ref, A, B, h0, reverse):
    n = A.shape[0]
    sub = lax.broadcasted_iota(jnp.int32, (8, 1), 0)
    tiles = range(n // 8 - 1, -1, -1) if reverse else range(n // 8)
    carry = h0
    for j in tiles:
        a, b = A[8 * j:8 * j + 8, :], B[8 * j:8 * j + 8, :]
        for d in (1, 2, 4):
            keep = (sub < 8 - d) if reverse else (sub >= d)
            shift = 8 - d if reverse else d
            b = jnp.where(keep, a * pltpu.roll(b, shift, axis=0) + b, b)
            a = jnp.where(keep, a * pltpu.roll(a, shift, axis=0), a)
        h = a * carry + b
        out_ref[8 * j:8 * j + 8, :] = h
        carry = h[0:1, :] if reverse else h[7:8, :]
    return carry


def _pool_windows(ext, shift_sign):
    n = ext.shape[0]
    s = ext
    outs = []
    for w in WINDOWS:
        d = w // 2
        s = s + pltpu.roll(s, d if shift_sign > 0 else n - d, axis=0)
        outs.append(s[:, :PG])
        s = s[:, PG:]
    return outs


def _conv_taps(uext):
    taps = []
    for k in range(4):
        sh = 3 - k
        v = uext if sh == 0 else pltpu.roll(uext, sh, axis=0)
        taps.append(v[CONV_HALO:, :])
    return taps


def _gates(v, wa_ref, ba_ref, wx_ref, bx_ref, sp):
    vb = v.astype(BF16)
    ra, rx = [], []
    for h in range(NH):
        vh = vb[:, h * HD:(h + 1) * HD]
        ra.append(_dot(vh, wa_ref[h]))
        rx.append(_dot(vh, wx_ref[h]))
    r = _sigmoid(jnp.concatenate(ra, axis=1) + ba_ref[...])
    i = _sigmoid(jnp.concatenate(rx, axis=1) + bx_ref[...])
    log_a = r * ((-LRU_C) * sp)
    a = jnp.exp(log_a)
    one_minus = -jnp.tanh(log_a) * (1.0 + a * a)
    return r, i, a, jnp.sqrt(one_minus), lax.rsqrt(one_minus)


def _mixer_fwd(proj, wg, scale, w_pool_out, conv_w, conv_b, wa, ba, wx, bx, lam, w_rnn_out, exchange=None,
               exchange_operands=(), tm=256):
    S = proj.shape[0]
    UW = DP + 2 * DR

    def body(proj_ref, wg_ref, scale_ref, wpo_ref, cw_ref, cb_ref, wa_ref, ba_ref, wx_ref, bx_ref, lam_ref, wro_ref,
             pm_ref, ypool_ref, hr_ref, z_ref, yrnn_ref, kept_ref, pool_carry, conv_carry, h_carry):
        i = pl.program_id(0)

        @pl.when(i == 0)
        def _():
            pool_carry[...] = jnp.zeros_like(pool_carry)
            conv_carry[...] = jnp.zeros_like(conv_carry)
            h_carry[...] = jnp.zeros_like(h_carry)

        rows = lax.broadcasted_iota(jnp.int32, (tm, 1), 0)
        t_glob = i * tm + rows

        u_pool = proj_ref[:, 0:DP]
        ext = jnp.concatenate([pool_carry[...], u_pool], axis=0)
        pool_carry[...] = u_pool[tm - POOL_HALO:, :]
        sums = _pool_windows(ext, +1)
        mixed = []
        for g, w in enumerate(WINDOWS):
            inv_cnt = 1.0 / jnp.minimum(t_glob + 1, w).astype(F32)
            pooled_g = sums[g][POOL_HALO:, :] * inv_cnt - u_pool[:, g * PG:(g + 1) * PG]
            mixed.append(_dot(pooled_g.astype(BF16), wg_ref[g]))
        pm = (jnp.concatenate(mixed, axis=1) * scale_ref[...]).astype(BF16)
        pm_ref[...] = pm
        ypool_ref[...] = _dot(pm, wpo_ref[...])

        u_rnn = proj_ref[:, DP:DP + DR]
        uext = jnp.concatenate([conv_carry[...], u_rnn], axis=0)
        conv_carry[...] = u_rnn[tm - CONV_HALO:, :]
        taps = _conv_taps(uext)
        v = cb_ref[...]
        for k in range(4):
            v = v + taps[k] * cw_ref[k:k + 1, :]
        sp = _softplus_neg(lam_ref[...])
        r, gi, a, mult, _ = _gates(v, wa_ref, ba_ref, wx_ref, bx_ref, sp)
        for k, kept in enumerate((v, r, gi, a, mult)):
            kept_ref[k] = kept
        h_carry[0:1, :] = _linear_scan(hr_ref, a, mult * gi * v, h_carry[0:1, :], reverse=False)
        z = (hr_ref[...] * _gelu(proj_ref[:, DP + DR:UW])).astype(BF16)
        z_ref[...] = z
        yrnn_ref[...] = _dot(z, wro_ref[...])

    return _call(
        body, "mixer_fwd", (S // tm,),
        in_specs=[_rows(UW, tm), _resident((4, PG, PG)), _resident((1, DP)), _resident((DP, D)), _resident((4, DR)),
                  _resident((1, DR)), _resident((NH, HD, HD)), _resident((1, DR)), _resident((NH, HD, HD)),
                  _resident((1, DR)), _resident((1, DR)), _resident((DR, D))],
        out_specs=[_rows(DP, tm), _rows(D, tm), _rows(DR, tm), _rows(DR, tm), _rows(D, tm),
                   pl.BlockSpec((KEPT, tm, DR), lambda i: (0, i, 0))],
        out_shape=[jax.ShapeDtypeStruct((S, DP), BF16),
                   jax.ShapeDtypeStruct((S, D), F32), jax.ShapeDtypeStruct((S, DR), F32),
                   jax.ShapeDtypeStruct((S, DR), BF16), jax.ShapeDtypeStruct((S, D), F32),
                   jax.ShapeDtypeStruct((KEPT, S, DR), F32)],
        scratch_shapes=[pltpu.VMEM((POOL_HALO, DP), F32), pltpu.VMEM((CONV_HALO, DR), F32), pltpu.VMEM((8, DR), F32)],
        operands=(proj, wg, scale, w_pool_out, conv_w, conv_b, wa, ba, wx, bx, lam, w_rnn_out),
        exchange=exchange, exchange_operands=exchange_operands)


FF_CHUNKS = ((0, 768), (768, 1536), (1536, 2304), (2304, DFF))


def _rms(x):
    r = lax.rsqrt(jnp.mean(x * x, axis=-1, keepdims=True) + EPS)
    return r, x * r


def _rms_bwd(dh, g, r, xh):
    dxh = dh * g
    return r * (dxh - xh * jnp.mean(dxh * xh, axis=-1, keepdims=True))


def _merge_out(x, proj, y_pool, y_rnn, w_o, norm_ffn, exchange=None, exchange_operands=(), tm=512):
    S = x.shape[0]
    GL0 = (DP + 2 * DR) // 512

    def gl_spec(k):
        return pl.BlockSpec((tm, 512), lambda i: (i, GL0 + k))

    def body(x_ref, gl0, gl1, gl2, gl3, yp_ref, yr_ref, wo_ref, gf_ref, mix_ref, x2_ref, h2_ref):
        s_p = _sigmoid(jnp.concatenate([gl0[...], gl1[...]], axis=1))
        s_r = _sigmoid(jnp.concatenate([gl2[...], gl3[...]], axis=1))
        mix = (s_p * yp_ref[...] + s_r * yr_ref[...]).astype(BF16)
        mix_ref[...] = mix
        x2 = x_ref[...] + _dot(mix, wo_ref[...])
        x2_ref[...] = x2
        _, xh2 = _rms(x2)
        h2_ref[...] = (xh2 * gf_ref[...]).astype(BF16)

    return _call(
        body, "merge_out", (S // tm,),
        in_specs=[_rows(D, tm), gl_spec(0), gl_spec(1), gl_spec(2), gl_spec(3), _rows(D, tm), _rows(D, tm),
                  _resident((D, D)), _resident((1, D))],
        out_specs=[_rows(D, tm), _rows(D, tm), _rows(D, tm)],
        out_shape=[jax.ShapeDtypeStruct((S, D), BF16), jax.ShapeDtypeStruct((S, D), F32), jax.ShapeDtypeStruct((S, D), BF16)],
        operands=(x, proj, proj, proj, proj, y_pool, y_rnn, w_o, norm_ffn),
        exchange=exchange, exchange_operands=exchange_operands)


def _ffn_up(h2, w_lo, w_hi, exchange=None, exchange_operands=(), tm=512):
    S = h2.shape[0]
    HALF = D // 2

    def body(h_ref, lo_ref, hi_ref, back_ref, act_ref):
        h_lo, h_hi = h_ref[:, 0:HALF], h_ref[:, HALF:D]
        for c0, c1 in FF_CHUNKS:
            gate = _dot_nt(h_lo, lo_ref[c0:c1, :]) + _dot_nt(h_hi, hi_ref[c0:c1, :])
            up = _dot_nt(h_lo, lo_ref[DFF + c0:DFF + c1, :]) + _dot_nt(h_hi, hi_ref[DFF + c0:DFF + c1, :])
            sg = _sigmoid(gate)
            silu = gate * sg
            back_ref[:, c0:c1] = (up * (sg * (1.0 + gate * (1.0 - sg)))).astype(BF16)
            back_ref[:, DFF + c0:DFF + c1] = silu.astype(BF16)
            act_ref[:, c0:c1] = (silu * up).astype(BF16)

    return _call(
        body, "ffn_up", (S // tm,),
        in_specs=[_rows(D, tm), _resident((2 * DFF, HALF)), _resident((2 * DFF, HALF))],
        out_specs=[_rows(2 * DFF, tm), _rows(DFF, tm)],
        out_shape=[jax.ShapeDtypeStruct((S, 2 * DFF), BF16), jax.ShapeDtypeStruct((S, DFF), BF16)],
        operands=(h2, w_lo, w_hi), exchange=exchange, exchange_operands=exchange_operands)


def _ffn_down_loss(act, x2, target, w_ffn_out, norm_final, tm=512):
    S = act.shape[0]

    def body(act_ref, x2_ref, t_ref, w_ref, gn_ref, dx3_ref, dx3b_ref, loss_ref, dvec_ref):
        i = pl.program_id(0)

        @pl.when(i == 0)
        def _():
            loss_ref[...] = jnp.zeros_like(loss_ref)
            dvec_ref[...] = jnp.zeros_like(dvec_ref)

        x3 = x2_ref[...] + _dot(act_ref[...], w_ref[...])
        r3, xh3 = _rms(x3)
        g_fin = gn_ref[...]
        e = xh3 * g_fin - t_ref[...]
        loss_ref[...] += jnp.sum(e * e, axis=(0, 1), keepdims=True) * (0.5 / D)
        dy = e * (1.0 / D)
        dvec_ref[0:1, :] += jnp.sum(dy * xh3, axis=0, keepdims=True)
        dx3 = _rms_bwd(dy, g_fin, r3, xh3)
        dx3_ref[...] = dx3
        dx3b_ref[...] = dx3.astype(BF16)

    return pl.pallas_call(
        body, name="ffn_down_loss", grid=(S // tm,),
        in_specs=[_rows(DFF, tm), _rows(D, tm), _rows(D, tm), _resident((DFF, D)), _resident((1, D))],
        out_specs=[_rows(D, tm), _rows(D, tm), _resident((1, 1)), _resident((8, D))],
        out_shape=[jax.ShapeDtypeStruct((S, D), F32), jax.ShapeDtypeStruct((S, D), BF16),
                   jax.ShapeDtypeStruct((1, 1), F32), jax.ShapeDtypeStruct((8, D), F32)],
        compiler_params=_params("arbitrary"),
    )(act, x2, target, w_ffn_out, norm_final)


def _ffn_bwd_down(dx3b, gu, w_ffn_out, tm=512):
    S = dx3b.shape[0]

    def body(d_ref, back_ref, w_ref, dgu_ref):
        d = d_ref[...]
        for c0, c1 in FF_CHUNKS:
            dact = _dot_nt(d, w_ref[c0:c1, :])
            dgu_ref[:, c0:c1] = (dact * back_ref[:, c0:c1].astype(F32)).astype(BF16)
            dgu_ref[:, DFF + c0:DFF + c1] = (dact * back_ref[:, DFF + c0:DFF + c1].astype(F32)).astype(BF16)

    return pl.pallas_call(
        body, name="ffn_bwd_down", grid=(S // tm,),
        in_specs=[_rows(D, tm), _rows(2 * DFF, tm), _resident((DFF, D))],
        out_specs=_rows(2 * DFF, tm),
        out_shape=jax.ShapeDtypeStruct((S, 2 * DFF), BF16),
        compiler_params=_params("parallel"),
    )(dx3b, gu, w_ffn_out)


def _ffn_bwd_up(dgu, x2, dx3, w_lo, w_hi, norm_ffn, w_o, tm=512):
    S = dgu.shape[0]
    HALF = D // 2

    def body(dgu_ref, x2_ref, dx3_ref, lo_ref, hi_ref, gf_ref, wo_ref, dx2_ref, dx2b_ref, dmixo_ref, dvec_ref):
        i = pl.program_id(0)

        @pl.when(i == 0)
        def _():
            dvec_ref[...] = jnp.zeros_like(dvec_ref)

        dgate, dup = dgu_ref[:, 0:DFF], dgu_ref[:, DFF:2 * DFF]
        dh2 = jnp.concatenate([_dot(dgate, w[0:DFF, :]) + _dot(dup, w[DFF:2 * DFF, :]) for w in (lo_ref, hi_ref)], axis=1)
        r2, xh2 = _rms(x2_ref[...])
        dvec_ref[0:1, :] += jnp.sum(dh2 * xh2, axis=0, keepdims=True)
        dx2 = dx3_ref[...] + _rms_bwd(dh2, gf_ref[...], r2, xh2)
        dx2_ref[...] = dx2
        dx2b = dx2.astype(BF16)
        dx2b_ref[...] = dx2b
        dmixo_ref[...] = _dot_nt(dx2b, wo_ref[...])

    return pl.pallas_call(
        body, name="ffn_bwd_up", grid=(S // tm,),
        in_specs=[_rows(2 * DFF, tm), _rows(D, tm), _rows(D, tm), _resident((2 * DFF, HALF)), _resident((2 * DFF, HALF)),
                  _resident((1, D)), _resident((D, D))],
        out_specs=[_rows(D, tm), _rows(D, tm), _rows(D, tm), _resident((8, D))],
        out_shape=[jax.ShapeDtypeStruct((S, D), F32), jax.ShapeDtypeStruct((S, D), BF16), jax.ShapeDtypeStruct((S, D), F32),
                   jax.ShapeDtypeStruct((8, D), F32)],
        compiler_params=_params("arbitrary"),
    )(dgu, x2, dx3, w_lo, w_hi, norm_ffn, w_o)


VEC_ROWS = 16
MAT_WA = 4 * PG
MAT_WX = MAT_WA + NH * HD
MAT_ROWS = MAT_WX + NH * HD


def _mixer_bwd(proj, dmixo, y_pool, y_rnn, hr, kept, wg, scale, w_pool_out, conv_w, conv_b, wa, ba, wx, bx, lam, w_rnn_out,
               exchange=None, exchange_operands=(), tm=256):
    S = proj.shape[0]
    nt = S // tm

    def rev(cols):
        return pl.BlockSpec((tm, cols), lambda i: (nt - 1 - i, 0))

    def halo(rows_, cols):
        per = tm // rows_
        return pl.BlockSpec((rows_, cols), lambda i: (jnp.maximum((nt - 1 - i) * per - 1, 0), 0))

    def body(proj_ref, projh_ref, dmixo_ref, yp_ref, yr_ref, hr_ref, hrh_ref, kept_ref, wg_ref, scale_ref, wpo_ref, cw_ref, cb_ref,
             wa_ref, ba_ref, wx_ref, bx_ref, lam_ref, wro_ref,
             dproj_ref, dypb_ref, dyrb_ref, dmat_ref, dvec_ref,
             q_carry, dv_carry, a_carry, g_carry, g_scr):
        i = pl.program_id(0)
        ti = nt - 1 - i

        @pl.when(i == 0)
        def _():
            q_carry[...] = jnp.zeros_like(q_carry)
            dv_carry[...] = jnp.zeros_like(dv_carry)
            a_carry[...] = jnp.zeros_like(a_carry)
            g_carry[...] = jnp.zeros_like(g_carry)
            dmat_ref[...] = jnp.zeros_like(dmat_ref)
            dvec_ref[...] = jnp.zeros_like(dvec_ref)

        rows = lax.broadcasted_iota(jnp.int32, (tm, 1), 0)
        t_glob = ti * tm + rows
        has_prev = (ti > 0).astype(F32)
        dmixo = dmixo_ref[...]

        s_p = _sigmoid(proj_ref[:, DP + 2 * DR:DP + 2 * DR + D])
        s_r = _sigmoid(proj_ref[:, DP + 2 * DR + D:DIN])
        dproj_ref[:, DP + 2 * DR:DP + 2 * DR + D] = (dmixo * yp_ref[...] * s_p * (1.0 - s_p)).astype(BF16)
        dproj_ref[:, DP + 2 * DR + D:DIN] = (dmixo * yr_ref[...] * s_r * (1.0 - s_r)).astype(BF16)
        dyp = (dmixo * s_p).astype(BF16)
        dyr = (dmixo * s_r).astype(BF16)
        dypb_ref[...] = dyp
        dyrb_ref[...] = dyr

        dz = _dot_nt(dyr, wro_ref[...])
        u_gate = proj_ref[:, DP + DR:DP + 2 * DR]
        gg, dgelu = _gelu(u_gate, with_grad=True)
        hr_t = hr_ref[...]
        dproj_ref[:, DP + DR:DP + 2 * DR] = (dz * hr_t * dgelu).astype(BF16)
        dhr = dz * gg

        sp = _softplus_neg(lam_ref[...])
        v, r, gi, a, mult = (kept_ref[k] for k in range(KEPT))
        inv_mult = 1.0 / mult

        C = jnp.where(rows == tm - 1, a_carry[0:1, :], pltpu.roll(a, tm - 1, axis=0))
        g_carry[0:1, :] = _linear_scan(g_scr, C, dhr, g_carry[0:1, :], reverse=True)
        a_carry[0:1, :] = a[0:1, :]
        g = g_scr[...]

        h_prev = jnp.where(rows == 0, hrh_ref[7:8, :] * has_prev, pltpu.roll(hr_t, 1, axis=0))
        da = g * h_prev
        gm = g * mult
        dmult = g * gi * v
        di = gm * v
        dv = gm * gi
        dlog_a = da * a - dmult * (a * a * inv_mult)
        dvec_ref[4:5, :] += jnp.sum(dlog_a * r, axis=0, keepdims=True)
        dra = (dlog_a * ((-LRU_C) * sp) * r * (1.0 - r))
        drx = di * gi * (1.0 - gi)
        dvec_ref[2:3, :] += jnp.sum(dra, axis=0, keepdims=True)
        dvec_ref[3:4, :] += jnp.sum(drx, axis=0, keepdims=True)
        drab = dra.astype(BF16)
        drxb = drx.astype(BF16)
        vb = v.astype(BF16)
        dvg = []
        for h in range(NH):
            sl = slice(h * HD, (h + 1) * HD)
            dvg.append(_dot_nt(drab[:, sl], wa_ref[h]) + _dot_nt(drxb[:, sl], wx_ref[h]))
            dmat_ref[MAT_WA + h * HD:MAT_WA + (h + 1) * HD, :] += _dot_tn(vb[:, sl], drab[:, sl])
            dmat_ref[MAT_WX + h * HD:MAT_WX + (h + 1) * HD, :] += _dot_tn(vb[:, sl], drxb[:, sl])
        dv = dv + jnp.concatenate(dvg, axis=1)
        dvec_ref[1:2, :] += jnp.sum(dv, axis=0, keepdims=True)
        dvext = jnp.concatenate([dv, dv_carry[...]], axis=0)
        dv_carry[...] = dv[0:CONV_HALO, :]
        n = tm + CONV_HALO
        u_rnn = proj_ref[:, DP:DP + DR]
        du_rnn = dv * cw_ref[3:4, :]
        dvec_ref[8:9, :] += jnp.sum(dv * u_rnn, axis=0, keepdims=True)
        for k in range(3):
            dv_k = pltpu.roll(dvext, n - (3 - k), axis=0)[0:tm, :]
            du_rnn = du_rnn + dv_k * cw_ref[k:k + 1, :]
            dvec_ref[5 + k:6 + k, :] += jnp.sum(dv_k * u_rnn, axis=0, keepdims=True)
        dproj_ref[:, DP:DP + DR] = du_rnn.astype(BF16)

        dpm = _dot_nt(dyp, wpo_ref[...])
        u_pool = proj_ref[:, 0:DP]
        ext = jnp.concatenate([projh_ref[:, 0:DP] * has_prev, u_pool], axis=0)
        sums = _pool_windows(ext, +1)
        scale_v = scale_ref[...]
        qs = []
        dpooled = []
        dscale = []
        for gi_, w in enumerate(WINDOWS):
            sl = slice(gi_ * PG, (gi_ + 1) * PG)
            inv_cnt = 1.0 / jnp.minimum(t_glob + 1, w).astype(F32)
            pooled_b = (sums[gi_][POOL_HALO:, :] * inv_cnt - u_pool[:, sl]).astype(BF16)
            mixed_g = _dot(pooled_b, wg_ref[gi_])
            dscale.append(jnp.sum(dpm[:, sl] * mixed_g, axis=0, keepdims=True))
            dmixed_b = (dpm[:, sl] * scale_v[:, sl]).astype(BF16)
            dmat_ref[gi_ * PG:(gi_ + 1) * PG, :] += _dot_tn(pooled_b, dmixed_b)
            dp_g = _dot_nt(dmixed_b, wg_ref[gi_])
            dpooled.append(dp_g)
            qs.append(dp_g * inv_cnt)
        dvec_ref[0:1, 0:DP] += jnp.concatenate(dscale, axis=1)
        q = jnp.concatenate(qs, axis=1)
        qext = jnp.concatenate([q, q_carry[...]], axis=0)
        q_carry[...] = q[0:POOL_HALO, :]
        tsum = _pool_windows(qext, -1)
        for gi_ in range(4):
            dproj_ref[:, gi_ * PG:(gi_ + 1) * PG] = (tsum[gi_][0:tm, :] - dpooled[gi_]).astype(BF16)

        @pl.when(i == nt - 1)
        def _():
            dvec_ref[4:5, :] = dvec_ref[4:5, :] * (LRU_C * _sigmoid(-lam_ref[...]))

    return _call(
        body, "mixer_bwd", (nt,),
        in_specs=[rev(DIN), halo(POOL_HALO, DIN), rev(D), rev(D), rev(D), rev(DR), halo(8, DR),
                  pl.BlockSpec((KEPT, tm, DR), lambda i: (0, nt - 1 - i, 0)), _resident((4, PG, PG)), _resident((1, DP)), _resident((DP, D)), _resident((4, DR)), _resident((1, DR)),
                  _resident((NH, HD, HD)), _resident((1, DR)), _resident((NH, HD, HD)), _resident((1, DR)),
                  _resident((1, DR)), _resident((DR, D))],
        out_specs=[rev(DIN), rev(D), rev(D), _resident((MAT_ROWS, HD)), _resident((VEC_ROWS, DR))],
        out_shape=[jax.ShapeDtypeStruct((S, DIN), BF16), jax.ShapeDtypeStruct((S, D), BF16),
                   jax.ShapeDtypeStruct((S, D), BF16), jax.ShapeDtypeStruct((MAT_ROWS, HD), F32),
                   jax.ShapeDtypeStruct((VEC_ROWS, DR), F32)],
        scratch_shapes=[pltpu.VMEM((POOL_HALO, DP), F32), pltpu.VMEM((CONV_HALO, DR), F32), pltpu.VMEM((8, DR), F32),
                        pltpu.VMEM((8, DR), F32), pltpu.VMEM((tm, DR), F32)],
        operands=(proj, proj, dmixo, y_pool, y_rnn, hr, hr, kept, wg, scale, w_pool_out, conv_w, conv_b, wa, ba, wx, bx, lam,
                  w_rnn_out),
        exchange=exchange, exchange_operands=exchange_operands)


def _in_bwd(dproj, x, dx2, norm_mix, w_in, exchange=None, exchange_operands=(), tm=512):
    S = x.shape[0]

    def body(dp_ref, x_ref, dx2_ref, g_ref, w_ref, dx_ref, dg_ref):
        i = pl.program_id(0)

        @pl.when(i == 0)
        def _():
            dg_ref[...] = jnp.zeros_like(dg_ref)

        dh = _dot(dp_ref[:, 0:1536], w_ref[0:1536, :])
        dh = dh + _dot(dp_ref[:, 1536:3072], w_ref[1536:3072, :])
        dh = dh + _dot(dp_ref[:, 3072:DIN], w_ref[3072:DIN, :])
        xv = x_ref[...]
        r = lax.rsqrt(jnp.mean(xv * xv, axis=-1, keepdims=True) + EPS)
        xh = xv * r
        dg_ref[0:1, :] += jnp.sum(dh * xh, axis=0, keepdims=True)
        dxh = dh * g_ref[...]
        dx_ref[...] = dx2_ref[...] + r * (dxh - xh * jnp.mean(dxh * xh, axis=-1, keepdims=True))

    return _call(
        body, "in_bwd", (S // tm,),
        in_specs=[_rows(DIN, tm), _rows(D, tm), _rows(D, tm), _resident((1, D)), _resident((DIN, D))],
        out_specs=[_rows(D, tm), _resident((8, D))],
        out_shape=[jax.ShapeDtypeStruct((S, D), F32), jax.ShapeDtypeStruct((8, D), F32)],
        operands=(dproj, x, dx2, norm_mix, w_in), exchange=exchange, exchange_operands=exchange_operands)


def _wgrad(a, b, name, tk, tn, exchange=None, exchange_operands=()):
    S, K = a.shape
    N = b.shape[1]

    def body(a_ref, b_ref, o_ref):
        o_ref[...] = _dot_tn(a_ref[...], b_ref[...]).astype(BF16)

    (out,), exchanged = _call(
        body, name, (K // tk, N // tn),
        in_specs=[pl.BlockSpec((S, tk), lambda k, n: (0, k)), pl.BlockSpec((S, tn), lambda k, n: (0, n))],
        out_specs=[pl.BlockSpec((tk, tn), lambda k, n: (k, n))],
        out_shape=[jax.ShapeDtypeStruct((K, N), BF16)],
        operands=(a, b), exchange=exchange, exchange_operands=exchange_operands)
    return (out, exchanged) if exchange is not None else out


VEC_SCALE, VEC_CONV_B, VEC_BA, VEC_BX, VEC_LAM, VEC_CONV_W, VEC_NORM_FINAL, VEC_NORM_FFN = 0, 1, 2, 3, 4, 5, 9, 10
VEC_LOSS = 11


class _Big:
    def __init__(self, name, rows, cols, axis, n, dtype=BF16, transposed=False, src_cols=None):
        self.name, self.rows, self.cols, self.axis, self.n, self.dtype = name, rows, cols, axis, n, dtype
        self.transposed = transposed
        self.src_cols = src_cols
        self.block_shape = (rows, n) if axis == 1 else (n, cols)

    def block(self, ref, p):
        if self.axis == 1:
            return ref.at[:, pl.ds(pl.multiple_of(p * self.n, 128), self.n)]
        return ref.at[pl.ds(pl.multiple_of(p * self.n, 16 if self.dtype == BF16 else 8), self.n), :]


BIG = (_Big("w_in", DIN, D, 0, DIN // 8, transposed=True), _Big("w_pool_out", DP, D, 1, D // 8),
       _Big("w_rnn_out", DR, D, 0, DR // 8), _Big("w_o", D, D, 0, D // 8),
       _Big("w_ffn_in", 2 * DFF, D, 0, 2 * DFF // 8, transposed=True), _Big("w_ffn_out", DFF, D, 0, DFF // 8))
CONV_W = _Big("conv_w", 8, DR, 1, DR // 8, F32)
W_FFN_IN_HALVES = (_Big("w_ffn_in_lo", 2 * DFF, D // 2, 0, 2 * DFF // 8, src_cols=(0, D // 2)),
                   _Big("w_ffn_in_hi", 2 * DFF, D // 2, 0, 2 * DFF // 8, src_cols=(D // 2, D)))
GATHERED = BIG + (CONV_W,) + W_FFN_IN_HALVES

HBM_SPEC = pl.BlockSpec(memory_space=pl.ANY)
VMEM_SPEC = pl.BlockSpec(memory_space=pltpu.VMEM)


def _place():
    x, y, c = (lax.axis_index(a) for a in MESH_AXES)
    other_chips = [(1 - x, y), (x, 1 - y), (1 - x, 1 - y)]
    return x, y, c, other_chips


def _remote(src, dst, send_sems, recv_sems, idx, to):
    return pltpu.make_async_remote_copy(src_ref=src, dst_ref=dst, send_sem=send_sems.at[idx], recv_sem=recv_sems.at[idx],
                                        device_id=to, device_id_type=MESH)


def _device_index(chip, core):
    return 4 * chip[0] + 2 * chip[1] + core


class _Gather:
    def __init__(self, tensors):
        self.tensors = tuple(tensors)
        n = len(self.tensors)
        self.in_specs = [HBM_SPEC] * n
        self.out_specs = [HBM_SPEC] * n
        self.out_shape = [jax.ShapeDtypeStruct((T.rows, T.cols), T.dtype) for T in self.tensors]
        self.scratch_shapes = [pltpu.VMEM(T.block_shape, T.dtype) for T in self.tensors] + [
            pltpu.VMEM(T.block_shape, F32) for T in self.tensors] + [
            pltpu.SemaphoreType.DMA((n, 7)), pltpu.SemaphoreType.DMA((n, 7)), pltpu.SemaphoreType.DMA((n, 2))]

    def middles(self, steps):
        return [(steps // 2, self.relay), (steps - 1, self.middle)]

    def _copies(self, ins, outs, scratch):
        n = len(self.tensors)
        mine, raw, (send_sems, recv_sems, loc_sems) = scratch[:n], scratch[n:2 * n], scratch[2 * n:]
        x, y, c, chips = _place()
        sibling = (x, y, 1 - c)
        me = _device_index((x, y), c)
        relay_from = (jnp.where(c == 0, 1 - x, x), jnp.where(c == 0, y, 1 - y))
        relay_to = (jnp.where(c == 0, x, 1 - x), jnp.where(c == 0, 1 - y, y))
        loads, stores, first, relays, passed, arrivals, late = [], [], [], [], [], [], []
        for t, T in enumerate(self.tensors):
            place = T.block(outs[t], me)
            src = ins[t] if T.src_cols is None else ins[t].at[:, T.src_cols[0]:T.src_cols[1]]
            loads.append(pltpu.make_async_copy(src, raw[t], loc_sems.at[t, 0]))
            stores.append(pltpu.make_async_copy(mine[t], place, loc_sems.at[t, 1]))
            first.append(_remote(mine[t], place, send_sems, recv_sems, (t, 0), sibling))
            theirs = T.block(outs[t], _device_index((x, y), 1 - c))
            late.append(_remote(theirs, theirs, send_sems, recv_sems, (t, 0), sibling))
            relayed = T.block(outs[t], _device_index(relay_from, c))
            relays.append(_remote(relayed, relayed, send_sems, recv_sems, (t, 3), (*relay_to, c)))
            for k, chip in enumerate(chips):
                if k < 2:
                    first.append(_remote(mine[t], place, send_sems, recv_sems, (t, 1 + k), (*chip, c)))
                land = T.block(outs[t], _device_index(chip, c))
                arrivals.append(_remote(land, land, send_sems, recv_sems, (t, 1 + k), sibling))
                passed.append(_remote(land, land, send_sems, recv_sems, (t, 4 + k), sibling))
                theirs = T.block(outs[t], _device_index(chip, 1 - c))
                late.append(_remote(theirs, theirs, send_sems, recv_sems, (t, 4 + k), sibling))
        return loads, stores, first, relays, passed, arrivals, late

    def start(self, ins, outs, scratch):
        loads, stores, first, _, _, _, _ = self._copies(ins, outs, scratch)
        n = len(self.tensors)
        for cp in loads:
            cp.start()
        for t, cp in enumerate(loads):
            cp.wait()
            scratch[t][...] = scratch[n + t][...].astype(self.tensors[t].dtype)
        for cp in stores + first:
            cp.start()

    def relay(self, ins, outs, scratch, skip=0):
        _, _, _, relays, passed, arrivals, _ = self._copies(ins, outs, scratch)
        for t in range(skip, len(self.tensors)):
            arrivals[3 * t].wait_recv()
            arrivals[3 * t + 1].wait_recv()
            for cp in (relays[t], passed[3 * t], passed[3 * t + 1]):
                cp.start()

    def middle(self, ins, outs, scratch, skip=0):
        _, _, _, _, passed, arrivals, _ = self._copies(ins, outs, scratch)
        for t in range(skip, len(self.tensors)):
            arrivals[3 * t + 2].wait_recv()
            passed[3 * t + 2].start()

    def finish(self, ins, outs, scratch, skip=0):
        _, stores, first, relays, passed, _, late = self._copies(ins, outs, scratch)
        for cp in late[4 * skip:]:
            cp.wait_recv()
        for cp in first + relays + passed:
            cp.wait_send()
        for cp in stores[skip:]:
            cp.wait()


def _in_proj_gather(x, norm_mix, blocks, tensors, order, tm=512):
    S = x.shape[0]
    nt = S // tm
    n = len(tensors)
    gather = _Gather(tensors)
    CB = 2 * tensors[0].n

    def body(order_ref, x_ref, g_ref, *refs):
        ins, (proj_ref, h_ref), outs = refs[:n], refs[n:n + 2], refs[n + 2:2 * n + 2]
        (h_all, w_chip, w_sem), scratch = refs[2 * n + 2:2 * n + 5], refs[2 * n + 5:]
        q, i = pl.program_id(0), pl.program_id(1)
        _, stores, _, relays, passed, arrivals, late = gather._copies(ins, outs, scratch)

        def fetch(turn):
            rows = outs[0].at[pl.ds(pl.multiple_of(order_ref[turn] * CB, 16), CB), :]
            cp = pltpu.make_async_copy(rows, w_chip, w_sem)
            cp.start()
            cp.wait()

        @pl.when((q == 0) & (i == 0))
        def _():
            gather.start(ins, outs, scratch)
            late[0].wait_recv()
            stores[0].wait()
            fetch(0)

        @pl.when((q == 1) & (i == 0))
        def _():
            arrivals[0].wait_recv()
            arrivals[1].wait_recv()
            for cp in (relays[0], passed[0], passed[1]):
                cp.start()
            late[1].wait_recv()
            fetch(1)

        @pl.when((q == 2) & (i == 0))
        def _():
            late[2].wait_recv()
            fetch(2)
            gather.relay(ins, outs, scratch, skip=1)

        @pl.when((q == 3) & (i == 0))
        def _():
            arrivals[2].wait_recv()
            passed[2].start()
            late[3].wait_recv()
            fetch(3)

        rows = pl.ds(pl.multiple_of(i * tm, tm), tm)

        @pl.when(q == 0)
        def _():
            xv = x_ref[...]
            r = lax.rsqrt(jnp.mean(xv * xv, axis=-1, keepdims=True) + EPS)
            h = (xv * r * g_ref[...]).astype(BF16)
            h_all[rows, :] = h
            h_ref[...] = h

        proj_ref[...] = _dot_nt(h_all[rows, :], w_chip[...])

        @pl.when((q == 3) & (i == nt - 1))
        def _():
            gather.middle(ins, outs, scratch, skip=1)
            gather.finish(ins, outs, scratch, skip=1)

    row_tile = lambda q, i, order: (jnp.where(q == 0, i, nt - 1), 0)
    whole = lambda shape: pl.BlockSpec(shape, lambda q, i, order: (0,) * len(shape), pipeline_mode=pl.Buffered(1))
    outs = pl.pallas_call(
        body, name="in_proj_gather",
        grid_spec=pltpu.PrefetchScalarGridSpec(
            num_scalar_prefetch=1, grid=(4, nt),
            in_specs=[pl.BlockSpec((tm, D), row_tile), whole((1, D))] + gather.in_specs,
            out_specs=[pl.BlockSpec((tm, CB), lambda q, i, order: (i, order[q])), pl.BlockSpec((tm, D), row_tile)]
            + gather.out_specs,
            scratch_shapes=[pltpu.VMEM((S, D), BF16), pltpu.VMEM((CB, D), BF16), pltpu.SemaphoreType.DMA]
            + gather.scratch_shapes),
        out_shape=[jax.ShapeDtypeStruct((S, DIN), F32), jax.ShapeDtypeStruct((S, D), BF16)] + gather.out_shape,
        compiler_params=_params("arbitrary", "arbitrary"),
    )(order, x, norm_mix, *blocks)
    return outs[:2], outs[2:]


PAIR_ROWS = 32


def _pair_reduce(grads, tensors, name):
    nt = len(tensors)

    def body(*refs):
        ins, own_out, sums_out, landed, mine = (refs[k * nt:(k + 1) * nt] for k in range(5))
        send_sems, recv_sems, loc_sems = refs[5 * nt:]
        x, y, c, chips = _place()
        chip_of = [2 * chip[0] + chip[1] for chip in chips]
        swaps, loads = [], []
        for t, T in enumerate(tensors):
            for j in range(4):
                swaps.append(_remote(T.block(ins[t], 2 * j + 1 - c), landed[t].at[j], send_sems, recv_sems, (t, j),
                                     (x, y, 1 - c)))
            for k in range(3):
                loads.append(pltpu.make_async_copy(T.block(ins[t], 2 * chip_of[k] + c), mine[t].at[k], loc_sems.at[t, k]))
        for cp in swaps + loads:
            cp.start()
        for cp in loads:
            cp.wait()
        for cp in swaps:
            cp.wait_recv()
        stores = []
        for t, T in enumerate(tensors):
            for k in range(3):
                acc, got = mine[t].at[k], landed[t].at[chip_of[k]]

                def add(i, carry, acc=acc, got=got):
                    rows = pl.ds(pl.multiple_of(i * PAIR_ROWS, PAIR_ROWS), PAIR_ROWS)
                    acc[rows, :] = (acc[rows, :].astype(F32) + got[rows, :].astype(F32)).astype(BF16)
                    return carry

                lax.fori_loop(0, T.block_shape[0] // PAIR_ROWS, add, 0)
            stores.append(pltpu.make_async_copy(mine[t], sums_out[t], loc_sems.at[t, 3]))
            stores.append(pltpu.make_async_copy(landed[t].at[2 * x + y], own_out[t], loc_sems.at[t, 4]))
        for cp in stores:
            cp.start()
        for cp in swaps:
            cp.wait_send()
        for cp in stores:
            cp.wait()

    blocks = [T.block_shape for T in tensors]
    return pl.pallas_call(
        body, name=name,
        in_specs=[HBM_SPEC] * nt, out_specs=[HBM_SPEC] * (2 * nt),
        out_shape=[jax.ShapeDtypeStruct(b, BF16) for b in blocks] + [jax.ShapeDtypeStruct((3,) + b, BF16) for b in blocks],
        scratch_shapes=[pltpu.VMEM((4,) + b, BF16) for b in blocks] + [pltpu.VMEM((3,) + b, BF16) for b in blocks]
        + [pltpu.SemaphoreType.DMA((nt, 4)), pltpu.SemaphoreType.DMA((nt, 4)), pltpu.SemaphoreType.DMA((nt, 5))],
        compiler_params=pltpu.CompilerParams(vmem_limit_bytes=VMEM_LIMIT),
    )(*grads)


class _Scatter:
    def middles(self, steps):
        return []

    def __init__(self, tensors):
        n = len(tensors)
        self.in_specs = [HBM_SPEC] * n
        self.out_specs = [HBM_SPEC] * n
        self.out_shape = [jax.ShapeDtypeStruct((3,) + T.block_shape, BF16) for T in tensors]
        self.scratch_shapes = [pltpu.SemaphoreType.DMA((n, 3)), pltpu.SemaphoreType.DMA((n, 3))]

    def _copies(self, ins, outs, scratch):
        send_sems, recv_sems = scratch
        x, y, c, chips = _place()
        return [_remote(ins[t].at[k], outs[t].at[k], send_sems, recv_sems, (t, k), (*chip, c))
                for t in range(len(ins)) for k, chip in enumerate(chips)]

    def start(self, ins, outs, scratch):
        for cp in self._copies(ins, outs, scratch):
            cp.start()

    def finish(self, ins, outs, scratch):
        for cp in self._copies(ins, outs, scratch):
            cp.wait()


def _adamw(w, g, m, v):
    m = ADAM_B1 * m + (1.0 - ADAM_B1) * g
    v = ADAM_B2 * v + (1.0 - ADAM_B2) * (g * g)
    m_hat = m / (1.0 - ADAM_B1 ** ADAM_STEP)
    v_hat = v / (1.0 - ADAM_B2 ** ADAM_STEP)
    delta = -ADAM_LR * (m_hat / (jnp.sqrt(v_hat) + ADAM_EPS) + ADAM_WD * w)
    return delta, m, v


def _final_sum(T, g, lz1, lz2, where, w, m, v):
    rows, cols = T.block_shape
    sub = 4 if T.axis == 0 and rows % 64 == 0 and rows > 256 else 1
    blk = (rows // sub, cols)

    def body(where_ref, g_ref, l1_ref, l2_ref, w_ref, m_ref, v_ref, g_out, d_out, m_out, v_out):
        tot = g_ref[...].astype(F32) + l1_ref[...].astype(F32)
        for k in range(3):
            tot = tot + l2_ref[k].astype(F32)
        g_out[...] = tot
        d_out[...], m_out[...], v_out[...] = _adamw(w_ref[...], tot, m_ref[...], v_ref[...])

    def in_whole(r, wh):
        p = wh[0]
        return (0, p) if T.axis == 1 else (p * sub + r, 0)

    own = pl.BlockSpec(blk, lambda r, wh: (r, 0))
    return pl.pallas_call(
        body, name="grad_final_" + T.name,
        grid_spec=pltpu.PrefetchScalarGridSpec(
            num_scalar_prefetch=1, grid=(sub,),
            in_specs=[pl.BlockSpec(blk, in_whole),
                      own,
                      pl.BlockSpec((3,) + blk, lambda r, wh: (0, r, 0)), own, own, own],
            out_specs=[own] * 4),
        out_shape=[jax.ShapeDtypeStruct(T.block_shape, F32)] * 4,
        compiler_params=_params("arbitrary"),
    )(where, g, lz1, lz2, w, m, v)


VEC_PIECE = DR // 8


class _AllReduce:
    def __init__(self, items):
        self.items = tuple(items)
        n = len(self.items)
        self.in_specs = [HBM_SPEC] * n
        self.out_specs = [HBM_SPEC] * n
        self.out_shape = [jax.ShapeDtypeStruct(shape, F32) for shape, _ in self.items]
        pieces = [(shape[0] // 8, shape[1]) if axis == 0 else (shape[0], shape[1] // 8) for shape, axis in self.items]
        self.scratch_shapes = ([pltpu.VMEM((8,) + p, F32) for p in pieces] + [pltpu.VMEM(p, F32) for p in pieces] + [
            pltpu.SemaphoreType.DMA((2 * n, 8)), pltpu.SemaphoreType.DMA((2 * n, 8)), pltpu.SemaphoreType.DMA((2 * n,))])

    def middles(self, steps):
        return [(steps // 2, self.middle)]

    def _copies(self, ins, outs, scratch):
        n = len(self.items)
        landed, sums, (send_sems, recv_sems, loc_sems) = scratch[:n], scratch[n:2 * n], scratch[2 * n:]
        x, y, c, _ = _place()
        me = _device_index((x, y), c)

        def peer(r):
            return (1 - x if r & 4 else x, 1 - y if r & 2 else y, 1 - c if r & 1 else c)

        def piece(i, ref, p):
            shape, axis = self.items[i]
            if axis == 0:
                rows = shape[0] // 8
                return ref.at[pl.ds(pl.multiple_of(p * rows, 8), rows), :]
            cols = shape[1] // 8
            return ref.at[:, pl.ds(pl.multiple_of(p * cols, 128), cols)]

        own, scatter, arrivals, keep, spread, late = [], [], [], [], [], []
        for i in range(n):
            own.append(pltpu.make_async_copy(piece(i, ins[i], me), landed[i].at[0], loc_sems.at[2 * i]))
            keep.append(pltpu.make_async_copy(sums[i], piece(i, outs[i], me), loc_sems.at[2 * i + 1]))
            for r in range(1, 8):
                to = peer(r)
                p = _device_index(to[:2], to[2])
                scatter.append(_remote(piece(i, ins[i], p), landed[i].at[r], send_sems, recv_sems, (2 * i, r), to))
                spread.append(_remote(sums[i], piece(i, outs[i], me), send_sems, recv_sems, (2 * i + 1, r), to))
                late.append(_remote(sums[i], piece(i, outs[i], p), send_sems, recv_sems, (2 * i + 1, r), to))
        return own, scatter, keep, spread, late, landed, sums

    def start(self, ins, outs, scratch):
        own, scatter, _, _, _, _, _ = self._copies(ins, outs, scratch)
        for cp in own + scatter:
            cp.start()

    def middle(self, ins, outs, scratch):
        own, scatter, keep, spread, _, landed, sums = self._copies(ins, outs, scratch)
        for cp in own:
            cp.wait()
        for cp in scatter:
            cp.wait_recv()
        for i in range(len(self.items)):
            total = landed[i][0]
            for r in range(1, 8):
                total = total + landed[i][r]
            sums[i][...] = total
        for cp in keep + spread:
            cp.start()

    def finish(self, ins, outs, scratch):
        _, scatter, keep, spread, late, _, _ = self._copies(ins, outs, scratch)
        for cp in late:
            cp.wait_recv()
        for cp in scatter + spread:
            cp.wait_send()
        for cp in keep:
            cp.wait()


class _Both:
    def __init__(self, a, b):
        self.a, self.b = a, b
        self.in_specs, self.out_specs = a.in_specs + b.in_specs, a.out_specs + b.out_specs
        self.out_shape, self.scratch_shapes = a.out_shape + b.out_shape, a.scratch_shapes + b.scratch_shapes

    def _each(self, ins, outs, scratch):
        a = self.a
        i, o, s = len(a.in_specs), len(a.out_specs), len(a.scratch_shapes)
        return (a, ins[:i], outs[:o], scratch[:s]), (self.b, ins[i:], outs[o:], scratch[s:])

    def middles(self, steps):
        def of(which, middle):
            return lambda ins, outs, scratch: middle(*self._each(ins, outs, scratch)[which][1:])
        return [(at, of(which, middle)) for which, e in enumerate((self.a, self.b)) for at, middle in e.middles(steps)]

    def start(self, ins, outs, scratch):
        for e, i, o, s in self._each(ins, outs, scratch):
            e.start(i, o, s)

    def finish(self, ins, outs, scratch):
        for e, i, o, s in self._each(ins, outs, scratch):
            e.finish(i, o, s)


def _all_reduce(arrays, items, name):
    reduce = _AllReduce(items)
    n = len(items)

    def body(*refs):
        ins, outs, scratch = refs[:n], refs[n:2 * n], refs[2 * n:]
        reduce.start(ins, outs, scratch)
        reduce.middle(ins, outs, scratch)
        reduce.finish(ins, outs, scratch)

    return pl.pallas_call(
        body, name=name, in_specs=reduce.in_specs, out_specs=reduce.out_specs, out_shape=reduce.out_shape,
        scratch_shapes=reduce.scratch_shapes,
    )(*arrays)


def _adam_small(grads, wmv):
    n = len(grads)

    def body(*refs):
        g_refs, rest = refs[:n], refs[n:]
        ins, outs = rest[:3 * n], rest[3 * n:]
        for i in range(n):
            d, m, v = _adamw(ins[3 * i][...], g_refs[i][...], ins[3 * i + 1][...], ins[3 * i + 2][...])
            outs[3 * i][...], outs[3 * i + 1][...], outs[3 * i + 2][...] = d, m, v

    flat = [a for t in wmv for a in t]
    return pl.pallas_call(
        body, name="adam_small",
        in_specs=[VMEM_SPEC] * (4 * n), out_specs=[VMEM_SPEC] * (3 * n),
        out_shape=[jax.ShapeDtypeStruct(a.shape, F32) for a in flat],
    )(*grads, *flat)


WEIGHT_NAMES = ("norm_mix", "w_in", "w_pool_grp", "pool_scale", "w_pool_out", "conv_w", "conv_b", "w_rg_a", "b_rg_a", "w_rg_x",
                "b_rg_x", "lru_lambda", "w_rnn_out", "w_o", "norm_ffn", "w_ffn_in", "w_ffn_out", "norm_final")


def kernel(x, norm_mix, w_in, w_pool_grp, pool_scale, w_pool_out, conv_w, conv_b, w_rg_a, b_rg_a, w_rg_x, b_rg_x, lru_lambda, w_rnn_out, w_o, norm_ffn, w_ffn_in, w_ffn_out, norm_final, loss_target, m_norm_mix, m_w_in, m_w_pool_grp, m_pool_scale, m_w_pool_out, m_conv_w, m_conv_b, m_w_rg_a, m_b_rg_a, m_w_rg_x, m_b_rg_x, m_lru_lambda, m_w_rnn_out, m_w_o, m_norm_ffn, m_w_ffn_in, m_w_ffn_out, m_norm_final, v_norm_mix, v_w_in, v_w_pool_grp, v_pool_scale, v_w_pool_out, v_conv_w, v_conv_b, v_w_rg_a, v_b_rg_a, v_w_rg_x, v_b_rg_x, v_lru_lambda, v_w_rnn_out, v_w_o, v_norm_ffn, v_w_ffn_in, v_w_ffn_out, v_norm_final):
    w = dict(norm_mix=norm_mix, w_in=w_in, w_pool_grp=w_pool_grp, pool_scale=pool_scale, w_pool_out=w_pool_out, conv_w=conv_w,
             conv_b=conv_b, w_rg_a=w_rg_a, b_rg_a=b_rg_a, w_rg_x=w_rg_x, b_rg_x=b_rg_x, lru_lambda=lru_lambda,
             w_rnn_out=w_rnn_out, w_o=w_o, norm_ffn=norm_ffn, w_ffn_in=w_ffn_in, w_ffn_out=w_ffn_out, norm_final=norm_final)
    m = dict(norm_mix=m_norm_mix, w_in=m_w_in, w_pool_grp=m_w_pool_grp, pool_scale=m_pool_scale, w_pool_out=m_w_pool_out,
             conv_w=m_conv_w, conv_b=m_conv_b, w_rg_a=m_w_rg_a, b_rg_a=m_b_rg_a, w_rg_x=m_w_rg_x, b_rg_x=m_b_rg_x,
             lru_lambda=m_lru_lambda, w_rnn_out=m_w_rnn_out, w_o=m_w_o, norm_ffn=m_norm_ffn, w_ffn_in=m_w_ffn_in,
             w_ffn_out=m_w_ffn_out, norm_final=m_norm_final)
    v = dict(norm_mix=v_norm_mix, w_in=v_w_in, w_pool_grp=v_w_pool_grp, pool_scale=v_pool_scale, w_pool_out=v_w_pool_out,
             conv_w=v_conv_w, conv_b=v_conv_b, w_rg_a=v_w_rg_a, b_rg_a=v_b_rg_a, w_rg_x=v_w_rg_x, b_rg_x=v_b_rg_x,
             lru_lambda=v_lru_lambda, w_rnn_out=v_w_rnn_out, w_o=v_w_o, norm_ffn=v_norm_ffn, w_ffn_in=v_w_ffn_in,
             w_ffn_out=v_w_ffn_out, norm_final=v_norm_final)
    xi, yi, ci = (lax.axis_index(a) for a in MESH_AXES)
    chip = 2 * xi + yi

    def held(T, a):
        return jnp.swapaxes(a, 0, 1) if T.transposed else a

    where = jnp.stack([2 * chip + ci]).astype(jnp.int32)
    by_name = {T.name: T for T in GATHERED}
    block = {T.name: held(T, w[T.name][0]) for T in BIG}
    block["conv_w"] = jnp.pad(conv_w[0], ((0, CONV_W.rows - 4), (0, 0)))
    block["w_ffn_in_lo"] = block["w_ffn_in_hi"] = block["w_ffn_in"]

    def gather_of(*names):
        return dict(exchange=_Gather([by_name[n] for n in names]), exchange_operands=[block[n] for n in names])

    def pair_sums(names, partials, tag):
        out = _pair_reduce(partials, [by_name[n] for n in names], "grad_pair_reduce_" + tag)
        return list(out[:len(names)]), list(out[len(names):])

    xs, target = x[0], loss_target[0]
    wg_b, wa_b, wx_b = (a[0].astype(BF16) for a in (w_pool_grp, w_rg_a, w_rg_x))
    ba2, bx2 = b_rg_a.reshape(1, DR), b_rg_x.reshape(1, DR)
    first = ("w_in", "w_pool_out", "w_rnn_out", "conv_w", "w_o")
    order = jnp.stack([chip, 2 * (1 - xi) + yi, 2 * xi + (1 - yi), 2 * (1 - xi) + (1 - yi)]).astype(jnp.int32)
    (proj, h1), (w_in_g, w_pool_out_g, w_rnn_out_g, conv_g, w_o_g) = _in_proj_gather(
        xs, norm_mix, [block[n] for n in first], [by_name[n] for n in first], order)
    mixer_weights = (wg_b, pool_scale, w_pool_out_g, conv_g[0:4], conv_b, wa_b, ba2, wx_b, bx2, lru_lambda, w_rnn_out_g)
    (pm, y_pool, hr, z, y_rnn, kept), (w_ffn_lo_g, w_ffn_hi_g) = _mixer_fwd(
        proj, *mixer_weights, **gather_of("w_ffn_in_lo", "w_ffn_in_hi"))
    (mix, x2, h2), _ = _merge_out(xs, proj, y_pool, y_rnn, w_o_g, norm_ffn)
    (gu, act), (w_ffn_out_g,) = _ffn_up(h2, w_ffn_lo_g, w_ffn_hi_g, **gather_of("w_ffn_out"))
    dx3, dx3b, loss_part, dvec_fin = _ffn_down_loss(act, x2, target, w_ffn_out_g, norm_final.reshape(1, D))

    dgu = _ffn_bwd_down(dx3b, gu, w_ffn_out_g)
    dx2, dx2b, dmixo, dvec_ffn = _ffn_bwd_up(dgu, x2, dx3, w_ffn_lo_g, w_ffn_hi_g, norm_ffn, w_o_g)
    names_a = ("w_ffn_in", "w_ffn_out", "w_o")
    part_a = [_wgrad(dgu, h2, "wgrad_ffn_in", 1408, 512), _wgrad(act, dx3b, "wgrad_ffn_out", 1408, 512),
              _wgrad(mix, dx2b, "wgrad_o", 1024, 256)]
    lz1_a, sums_a = pair_sums(names_a, part_a, "ffn")
    (dproj, dypb, dyrb, dmat, dvec_mix), lz2_a = _mixer_bwd(
        proj, dmixo, y_pool, y_rnn, hr, kept, *mixer_weights,
        exchange=_Scatter([by_name[n] for n in names_a]), exchange_operands=sums_a)
    names_b = ("w_pool_out", "w_rnn_out")
    part_b = [_wgrad(pm, dypb, "wgrad_pool_out", 512, 256), _wgrad(z, dyrb, "wgrad_rnn_out", 1024, 256)]
    lz1_b, sums_b = pair_sums(names_b, part_b, "mix")
    dvec = jnp.concatenate([dvec_mix[0:9], dvec_fin[0:1], dvec_ffn[0:1], jnp.pad(loss_part, ((0, 0), (0, DR - 1))),
                            jnp.zeros((VEC_ROWS - 12, DR), F32)], axis=0)
    g_in, exchanged = _wgrad(
        dproj, h1, "wgrad_in", 1152, 1024,
        exchange=_Both(_Scatter([by_name[n] for n in names_b]), _AllReduce([((MAT_ROWS, HD), 0), ((VEC_ROWS, DR), 1)])),
        exchange_operands=sums_b + [dmat, dvec])
    lz2_b, (mat, vec) = exchanged[:2], exchanged[2:]
    loss = vec[VEC_LOSS, 0]
    lz1_c, sums_c = pair_sums(("w_in",), [g_in], "in")
    (grad_x, dvec_in), lz2_c = _in_bwd(dproj, xs, dx2, norm_mix, w_in_g,
                                       exchange=_Scatter([by_name["w_in"]]), exchange_operands=sums_c)
    (vec_in,) = _all_reduce([dvec_in], [((8, D), 1)], "all_reduce_norm_mix")

    grads, delta, new_m, new_v = {}, {}, {}, {}
    for n, g, l1, l2 in zip(names_a + names_b + ("w_in",), part_a + part_b + [g_in], lz1_a + lz1_b + lz1_c,
                            lz2_a + lz2_b + lz2_c):
        T = by_name[n]
        out = _final_sum(T, g, l1, l2, where, held(T, w[n][0]), held(T, m[n][0]), held(T, v[n][0]))
        grads[n], delta[n], new_m[n], new_v[n] = (held(T, a) for a in out)
    me = 4 * xi + 2 * yi + ci
    small_grads = dict(
        w_pool_grp=mat[0:MAT_WA], w_rg_a=mat[MAT_WA:MAT_WX], w_rg_x=mat[MAT_WX:MAT_ROWS],
        pool_scale=vec[VEC_SCALE:VEC_SCALE + 1, 0:DP], conv_b=vec[VEC_CONV_B:VEC_CONV_B + 1],
        b_rg_a=vec[VEC_BA:VEC_BA + 1], b_rg_x=vec[VEC_BX:VEC_BX + 1], lru_lambda=vec[VEC_LAM:VEC_LAM + 1],
        conv_w=lax.dynamic_slice(vec, (VEC_CONV_W, VEC_PIECE * me), (4, VEC_PIECE)),
        norm_final=vec[VEC_NORM_FINAL:VEC_NORM_FINAL + 1], norm_ffn=vec[VEC_NORM_FFN:VEC_NORM_FFN + 1],
        norm_mix=vec_in[0:1])
    names = list(small_grads)
    as2d = lambda a, g: a.reshape(g.shape)
    upd = _adam_small([small_grads[n] for n in names],
                      [(as2d(w[n], small_grads[n]), as2d(m[n], small_grads[n]), as2d(v[n], small_grads[n])) for n in names])
    for i, n in enumerate(names):
        grads[n] = small_grads[n]
        delta[n], new_m[n], new_v[n] = upd[3 * i:3 * i + 3]

    shaped = lambda d: [d[n].reshape(w[n].shape) for n in WEIGHT_NAMES]
    return (loss, grad_x[None], *shaped(grads), *shaped(delta), *shaped(new_m), *shaped(new_v))
```

```python
import math

import jax
import jax.numpy as jnp
from jax import lax
from jax.experimental import pallas as pl
from jax.experimental.pallas import tpu as pltpu

F32 = jnp.float32
BF16 = jnp.bfloat16

D = 1024
DP = 512
PG = 128
WINDOWS = (2, 4, 8, 16)
DR = 1024
NH = 8
HD = 128
DIN = 4608
DFF = 2816
EPS = 1e-6
LRU_C = 8.0
POOL_HALO = 16
CONV_HALO = 8
KEPT = 5

ADAM_LR = 0.001
ADAM_B1 = 0.9
ADAM_B2 = 0.999
ADAM_EPS = 1e-08
ADAM_WD = 0.01
ADAM_STEP = 10

VMEM_LIMIT = 56 * 1024 * 1024
MESH_AXES = ("x", "y", "c")
MESH = pl.DeviceIdType.MESH


def _dot(a, b):
    return jnp.dot(a, b, preferred_element_type=F32)


def _dot_nt(a, b):
    return lax.dot_general(a, b, (((1,), (1,)), ((), ())), preferred_element_type=F32)


def _dot_tn(a, b):
    return lax.dot_general(a, b, (((0,), (0,)), ((), ())), preferred_element_type=F32)


def _params(*sem):
    return pltpu.CompilerParams(dimension_semantics=sem, vmem_limit_bytes=VMEM_LIMIT)


def _resident(shape):
    nd = len(shape)
    return pl.BlockSpec(shape, lambda i: (0,) * nd, pipeline_mode=pl.Buffered(1))


def _rows(shape_cols, tm):
    return pl.BlockSpec((tm, shape_cols), lambda i: (i, 0))


def _call(body, name, grid, in_specs, out_specs, out_shape, operands, scratch_shapes=(), exchange=None, exchange_operands=()):
    n_in, n_out, n_scr = len(in_specs), len(out_specs), len(scratch_shapes)
    steps = math.prod(grid)
    if exchange is None:
        outs = pl.pallas_call(body, name=name, grid=grid, in_specs=in_specs, out_specs=out_specs, out_shape=out_shape,
                              scratch_shapes=list(scratch_shapes), compiler_params=_params(*["arbitrary"] * len(grid)))(*operands)
        return outs, []
    e_in, e_out = len(exchange.in_specs), len(exchange.out_specs)

    def hosted(*refs):
        ins, refs = refs[:n_in], refs[n_in:]
        e_ins, refs = refs[:e_in], refs[e_in:]
        outs, refs = refs[:n_out], refs[n_out:]
        e_outs, refs = refs[:e_out], refs[e_out:]
        scr, e_scr = refs[:n_scr], refs[n_scr:]
        step = pl.program_id(0)
        for axis in range(1, len(grid)):
            step = step * grid[axis] + pl.program_id(axis)
        pl.when(step == 0)(lambda: exchange.start(e_ins, e_outs, e_scr))
        for at, middle in exchange.middles(steps):
            pl.when(step == at)(lambda middle=middle: middle(e_ins, e_outs, e_scr))
        body(*ins, *outs, *scr)
        pl.when(step == steps - 1)(lambda: exchange.finish(e_ins, e_outs, e_scr))

    outs = pl.pallas_call(
        hosted, name=name, grid=grid, in_specs=list(in_specs) + exchange.in_specs,
        out_specs=list(out_specs) + exchange.out_specs, out_shape=list(out_shape) + exchange.out_shape,
        scratch_shapes=list(scratch_shapes) + exchange.scratch_shapes,
        compiler_params=_params(*["arbitrary"] * len(grid)))(*operands, *exchange_operands)
    return outs[:n_out], outs[n_out:]


GELU_C = math.sqrt(2.0 / math.pi)
GELU_K = 0.044715 * GELU_C


def _gelu(x, with_grad=False):
    x2 = x * x
    t = jnp.tanh(x * (GELU_C + GELU_K * x2))
    hx = 0.5 * x
    y = hx + hx * t
    if not with_grad:
        return y
    return y, 0.5 + 0.5 * t + hx * (1.0 - t * t) * (GELU_C + (3.0 * GELU_K) * x2)


def _softplus_neg(lam):
    z = jnp.exp(-jnp.abs(lam))
    u = 1.0 + z
    dlt = u - 1.0
    log1p = jnp.where(dlt == 0.0, z, jnp.log(u) * (z / jnp.where(dlt == 0.0, 1.0, dlt)))
    return jnp.maximum(-lam, 0.0) + log1p


def _sigmoid(x):
    return 0.5 * jnp.tanh(0.5 * x) + 0.5


def _linear_scan(out_ref, A, B, h0, reverse):
    n = A.shape[0]
    sub = lax.broadcasted_iota(jnp.int32, (8, 1), 0)
    tiles = range(n // 8 - 1, -1, -1) if reverse else range(n // 8)
    carry = h0
    for j in tiles:
        a, b = A[8 * j:8 * j + 8, :], B[8 * j:8 * j + 8, :]
        for d in (1, 2, 4):
            keep = (sub < 8 - d) if reverse else (sub >= d)
            shift = 8 - d if reverse else d
            b = jnp.where(keep, a * pltpu.roll(b, shift, axis=0) + b, b)
            a = jnp.where(keep, a * pltpu.roll(a, shift, axis=0), a)
        h = a * carry + b
        out_ref[8 * j:8 * j + 8, :] = h
        carry = h[0:1, :] if reverse else h[7:8, :]
    return carry


def _pool_windows(ext, shift_sign):
    n = ext.shape[0]
    s = ext
    outs = []
    for w in WINDOWS:
        d = w // 2
        s = s + pltpu.roll(s, d if shift_sign > 0 else n - d, axis=0)
        outs.append(s[:, :PG])
        s = s[:, PG:]
    return outs


def _conv_taps(uext):
    taps = []
    for k in range(4):
        sh = 3 - k
        v = uext if sh == 0 else pltpu.roll(uext, sh, axis=0)
        taps.append(v[CONV_HALO:, :])
    return taps


def _gates(v, wa_ref, ba_ref, wx_ref, bx_ref, sp):
    vb = v.astype(BF16)
    ra, rx = [], []
    for h in range(NH):
        vh = vb[:, h * HD:(h + 1) * HD]
        ra.append(_dot(vh, wa_ref[h]))
        rx.append(_dot(vh, wx_ref[h]))
    r = _sigmoid(jnp.concatenate(ra, axis=1) + ba_ref[...])
    i = _sigmoid(jnp.concatenate(rx, axis=1) + bx_ref[...])
    log_a = r * ((-LRU_C) * sp)
    a = jnp.exp(log_a)
    one_minus = -jnp.tanh(log_a) * (1.0 + a * a)
    return r, i, a, jnp.sqrt(one_minus), lax.rsqrt(one_minus)


def _mixer_fwd(proj, wg, scale, w_pool_out, conv_w, conv_b, wa, ba, wx, bx, lam, w_rnn_out, exchange=None,
               exchange_operands=(), tm=256):
    S = proj.shape[0]
    UW = DP + 2 * DR

    def body(proj_ref, wg_ref, scale_ref, wpo_ref, cw_ref, cb_ref, wa_ref, ba_ref, wx_ref, bx_ref, lam_ref, wro_ref,
             pm_ref, ypool_ref, hr_ref, z_ref, yrnn_ref, kept_ref, pool_carry, conv_carry, h_carry):
        i = pl.program_id(0)

        @pl.when(i == 0)
        def _():
            pool_carry[...] = jnp.zeros_like(pool_carry)
            conv_carry[...] = jnp.zeros_like(conv_carry)
            h_carry[...] = jnp.zeros_like(h_carry)

        rows = lax.broadcasted_iota(jnp.int32, (tm, 1), 0)
        t_glob = i * tm + rows

        u_pool = proj_ref[:, 0:DP]
        ext = jnp.concatenate([pool_carry[...], u_pool], axis=0)
        pool_carry[...] = u_pool[tm - POOL_HALO:, :]
        sums = _pool_windows(ext, +1)
        mixed = []
        for g, w in enumerate(WINDOWS):
            inv_cnt = 1.0 / jnp.minimum(t_glob + 1, w).astype(F32)
            pooled_g = sums[g][POOL_HALO:, :] * inv_cnt - u_pool[:, g * PG:(g + 1) * PG]
            mixed.append(_dot(pooled_g.astype(BF16), wg_ref[g]))
        pm = (jnp.concatenate(mixed, axis=1) * scale_ref[...]).astype(BF16)
        pm_ref[...] = pm
        ypool_ref[...] = _dot(pm, wpo_ref[...])

        u_rnn = proj_ref[:, DP:DP + DR]
        uext = jnp.concatenate([conv_carry[...], u_rnn], axis=0)
        conv_carry[...] = u_rnn[tm - CONV_HALO:, :]
        taps = _conv_taps(uext)
        v = cb_ref[...]
        for k in range(4):
            v = v + taps[k] * cw_ref[k:k + 1, :]
        sp = _softplus_neg(lam_ref[...])
        r, gi, a, mult, _ = _gates(v, wa_ref, ba_ref, wx_ref, bx_ref, sp)
        for k, kept in enumerate((v, r, gi, a, mult)):
            kept_ref[k] = kept
        h_carry[0:1, :] = _linear_scan(hr_ref, a, mult * gi * v, h_carry[0:1, :], reverse=False)
        z = (hr_ref[...] * _gelu(proj_ref[:, DP + DR:UW])).astype(BF16)
        z_ref[...] = z
        yrnn_ref[...] = _dot(z, wro_ref[...])

    return _call(
        body, "mixer_fwd", (S // tm,),
        in_specs=[_rows(UW, tm), _resident((4, PG, PG)), _resident((1, DP)), _resident((DP, D)), _resident((4, DR)),
                  _resident((1, DR)), _resident((NH, HD, HD)), _resident((1, DR)), _resident((NH, HD, HD)),
                  _resident((1, DR)), _resident((1, DR)), _resident((DR, D))],
        out_specs=[_rows(DP, tm), _rows(D, tm), _rows(DR, tm), _rows(DR, tm), _rows(D, tm),
                   pl.BlockSpec((KEPT, tm, DR), lambda i: (0, i, 0))],
        out_shape=[jax.ShapeDtypeStruct((S, DP), BF16),
                   jax.ShapeDtypeStruct((S, D), F32), jax.ShapeDtypeStruct((S, DR), F32),
                   jax.ShapeDtypeStruct((S, DR), BF16), jax.ShapeDtypeStruct((S, D), F32),
                   jax.ShapeDtypeStruct((KEPT, S, DR), F32)],
        scratch_shapes=[pltpu.VMEM((POOL_HALO, DP), F32), pltpu.VMEM((CONV_HALO, DR), F32), pltpu.VMEM((8, DR), F32)],
        operands=(proj, wg, scale, w_pool_out, conv_w, conv_b, wa, ba, wx, bx, lam, w_rnn_out),
        exchange=exchange, exchange_operands=exchange_operands)


FF_CHUNKS = ((0, 768), (768, 1536), (1536, 2304), (2304, DFF))


def _rms(x):
    r = lax.rsqrt(jnp.mean(x * x, axis=-1, keepdims=True) + EPS)
    return r, x * r


def _rms_bwd(dh, g, r, xh):
    dxh = dh * g
    return r * (dxh - xh * jnp.mean(dxh * xh, axis=-1, keepdims=True))


def _merge_out(x, proj, y_pool, y_rnn, w_o, norm_ffn, exchange=None, exchange_operands=(), tm=512):
    S = x.shape[0]
    GL0 = (DP + 2 * DR) // 512

    def gl_spec(k):
        return pl.BlockSpec((tm, 512), lambda i: (i, GL0 + k))

    def body(x_ref, gl0, gl1, gl2, gl3, yp_ref, yr_ref, wo_ref, gf_ref, mix_ref, x2_ref, h2_ref):
        s_p = _sigmoid(jnp.concatenate([gl0[...], gl1[...]], axis=1))
        s_r = _sigmoid(jnp.concatenate([gl2[...], gl3[...]], axis=1))
        mix = (s_p * yp_ref[...] + s_r * yr_ref[...]).astype(BF16)
        mix_ref[...] = mix
        x2 = x_ref[...] + _dot(mix, wo_ref[...])
        x2_ref[...] = x2
        _, xh2 = _rms(x2)
        h2_ref[...] = (xh2 * gf_ref[...]).astype(BF16)

    return _call(
        body, "merge_out", (S // tm,),
        in_specs=[_rows(D, tm), gl_spec(0), gl_spec(1), gl_spec(2), gl_spec(3), _rows(D, tm), _rows(D, tm),
                  _resident((D, D)), _resident((1, D))],
        out_specs=[_rows(D, tm), _rows(D, tm), _rows(D, tm)],
        out_shape=[jax.ShapeDtypeStruct((S, D), BF16), jax.ShapeDtypeStruct((S, D), F32), jax.ShapeDtypeStruct((S, D), BF16)],
        operands=(x, proj, proj, proj, proj, y_pool, y_rnn, w_o, norm_ffn),
        exchange=exchange, exchange_operands=exchange_operands)


def _ffn_up(h2, w_lo, w_hi, proj, exchange=None, exchange_operands=(), tm=512):
    S = h2.shape[0]
    HALF = D // 2
    UG0 = (DP + DR) // 512

    def body(h_ref, lo_ref, hi_ref, ug0_ref, ug1_ref, back_ref, act_ref, gg_ref, dgelu_ref):
        gg, dgelu = _gelu(jnp.concatenate([ug0_ref[...], ug1_ref[...]], axis=1), with_grad=True)
        gg_ref[...] = gg
        dgelu_ref[...] = dgelu.astype(BF16)
        h_lo, h_hi = h_ref[:, 0:HALF], h_ref[:, HALF:D]
        for c0, c1 in FF_CHUNKS:
            gate = _dot_nt(h_lo, lo_ref[c0:c1, :]) + _dot_nt(h_hi, hi_ref[c0:c1, :])
            up = _dot_nt(h_lo, lo_ref[DFF + c0:DFF + c1, :]) + _dot_nt(h_hi, hi_ref[DFF + c0:DFF + c1, :])
            sg = _sigmoid(gate)
            silu = gate * sg
            back_ref[:, c0:c1] = (up * (sg * (1.0 + gate * (1.0 - sg)))).astype(BF16)
            back_ref[:, DFF + c0:DFF + c1] = silu.astype(BF16)
            act_ref[:, c0:c1] = (silu * up).astype(BF16)

    return _call(
        body, "ffn_up", (S // tm,),
        in_specs=[_rows(D, tm), _resident((2 * DFF, HALF)), _resident((2 * DFF, HALF)),
                  pl.BlockSpec((tm, 512), lambda i: (i, UG0)), pl.BlockSpec((tm, 512), lambda i: (i, UG0 + 1))],
        out_specs=[_rows(2 * DFF, tm), _rows(DFF, tm), _rows(DR, tm), _rows(DR, tm)],
        out_shape=[jax.ShapeDtypeStruct((S, 2 * DFF), BF16), jax.ShapeDtypeStruct((S, DFF), BF16),
                   jax.ShapeDtypeStruct((S, DR), F32), jax.ShapeDtypeStruct((S, DR), BF16)],
        operands=(h2, w_lo, w_hi, proj, proj), exchange=exchange, exchange_operands=exchange_operands)


def _ffn_down_loss(act, x2, target, w_ffn_out, norm_final, tm=512):
    S = act.shape[0]

    def body(act_ref, x2_ref, t_ref, w_ref, gn_ref, dx3_ref, dx3b_ref, loss_ref, dvec_ref):
        i = pl.program_id(0)

        @pl.when(i == 0)
        def _():
            loss_ref[...] = jnp.zeros_like(loss_ref)
            dvec_ref[...] = jnp.zeros_like(dvec_ref)

        x3 = x2_ref[...] + _dot(act_ref[...], w_ref[...])
        r3, xh3 = _rms(x3)
        g_fin = gn_ref[...]
        e = xh3 * g_fin - t_ref[...]
        loss_ref[...] += jnp.sum(e * e, axis=(0, 1), keepdims=True) * (0.5 / D)
        dy = e * (1.0 / D)
        dvec_ref[0:1, :] += jnp.sum(dy * xh3, axis=0, keepdims=True)
        dx3 = _rms_bwd(dy, g_fin, r3, xh3)
        dx3_ref[...] = dx3
        dx3b_ref[...] = dx3.astype(BF16)

    return pl.pallas_call(
        body, name="ffn_down_loss", grid=(S // tm,),
        in_specs=[_rows(DFF, tm), _rows(D, tm), _rows(D, tm), _resident((DFF, D)), _resident((1, D))],
        out_specs=[_rows(D, tm), _rows(D, tm), _resident((1, 1)), _resident((8, D))],
        out_shape=[jax.ShapeDtypeStruct((S, D), F32), jax.ShapeDtypeStruct((S, D), BF16),
                   jax.ShapeDtypeStruct((1, 1), F32), jax.ShapeDtypeStruct((8, D), F32)],
        compiler_params=_params("arbitrary"),
    )(act, x2, target, w_ffn_out, norm_final)


def _ffn_bwd_down(dx3b, gu, w_ffn_out, tm=512):
    S = dx3b.shape[0]

    def body(d_ref, back_ref, w_ref, dgu_ref):
        d = d_ref[...]
        for c0, c1 in FF_CHUNKS:
            dact = _dot_nt(d, w_ref[c0:c1, :])
            dgu_ref[:, c0:c1] = (dact * back_ref[:, c0:c1].astype(F32)).astype(BF16)
            dgu_ref[:, DFF + c0:DFF + c1] = (dact * back_ref[:, DFF + c0:DFF + c1].astype(F32)).astype(BF16)

    return pl.pallas_call(
        body, name="ffn_bwd_down", grid=(S // tm,),
        in_specs=[_rows(D, tm), _rows(2 * DFF, tm), _resident((DFF, D))],
        out_specs=_rows(2 * DFF, tm),
        out_shape=jax.ShapeDtypeStruct((S, 2 * DFF), BF16),
        compiler_params=_params("parallel"),
    )(dx3b, gu, w_ffn_out)


def _ffn_bwd_up(dgu, x2, dx3, w_lo, w_hi, norm_ffn, w_o, tm=512):
    S = dgu.shape[0]
    HALF = D // 2

    def body(dgu_ref, x2_ref, dx3_ref, lo_ref, hi_ref, gf_ref, wo_ref, dx2_ref, dx2b_ref, dmixo_ref, dvec_ref):
        i = pl.program_id(0)

        @pl.when(i == 0)
        def _():
            dvec_ref[...] = jnp.zeros_like(dvec_ref)

        dgate, dup = dgu_ref[:, 0:DFF], dgu_ref[:, DFF:2 * DFF]
        dh2 = jnp.concatenate([_dot(dgate, w[0:DFF, :]) + _dot(dup, w[DFF:2 * DFF, :]) for w in (lo_ref, hi_ref)], axis=1)
        r2, xh2 = _rms(x2_ref[...])
        dvec_ref[0:1, :] += jnp.sum(dh2 * xh2, axis=0, keepdims=True)
        dx2 = dx3_ref[...] + _rms_bwd(dh2, gf_ref[...], r2, xh2)
        dx2_ref[...] = dx2
        dx2b = dx2.astype(BF16)
        dx2b_ref[...] = dx2b
        dmixo_ref[...] = _dot_nt(dx2b, wo_ref[...])

    return pl.pallas_call(
        body, name="ffn_bwd_up", grid=(S // tm,),
        in_specs=[_rows(2 * DFF, tm), _rows(D, tm), _rows(D, tm), _resident((2 * DFF, HALF)), _resident((2 * DFF, HALF)),
                  _resident((1, D)), _resident((D, D))],
        out_specs=[_rows(D, tm), _rows(D, tm), _rows(D, tm), _resident((8, D))],
        out_shape=[jax.ShapeDtypeStruct((S, D), F32), jax.ShapeDtypeStruct((S, D), BF16), jax.ShapeDtypeStruct((S, D), F32),
                   jax.ShapeDtypeStruct((8, D), F32)],
        compiler_params=_params("arbitrary"),
    )(dgu, x2, dx3, w_lo, w_hi, norm_ffn, w_o)


VEC_ROWS = 16
MAT_WA = 4 * PG
MAT_WX = MAT_WA + NH * HD
MAT_ROWS = MAT_WX + NH * HD


def _mixer_bwd(proj, dmixo, y_pool, y_rnn, hr, kept, gg, dgelu, wg, scale, w_pool_out, conv_w, conv_b, wa, ba, wx, bx, lam, w_rnn_out,
               exchange=None, exchange_operands=(), tm=256):
    S = proj.shape[0]
    nt = S // tm

    def rev(cols):
        return pl.BlockSpec((tm, cols), lambda i: (nt - 1 - i, 0))

    def halo(rows_, cols):
        per = tm // rows_
        return pl.BlockSpec((rows_, cols), lambda i: (jnp.maximum((nt - 1 - i) * per - 1, 0), 0))

    def body(proj_ref, projh_ref, dmixo_ref, yp_ref, yr_ref, hr_ref, hrh_ref, kept_ref, gg_ref, dgelu_ref, wg_ref, scale_ref, wpo_ref, cw_ref, cb_ref,
             wa_ref, ba_ref, wx_ref, bx_ref, lam_ref, wro_ref,
             dproj_ref, dypb_ref, dyrb_ref, dmat_ref, dvec_ref,
             q_carry, dv_carry, a_carry, g_carry, g_scr):
        i = pl.program_id(0)
        ti = nt - 1 - i

        @pl.when(i == 0)
        def _():
            q_carry[...] = jnp.zeros_like(q_carry)
            dv_carry[...] = jnp.zeros_like(dv_carry)
            a_carry[...] = jnp.zeros_like(a_carry)
            g_carry[...] = jnp.zeros_like(g_carry)
            dmat_ref[...] = jnp.zeros_like(dmat_ref)
            dvec_ref[...] = jnp.zeros_like(dvec_ref)

        rows = lax.broadcasted_iota(jnp.int32, (tm, 1), 0)
        t_glob = ti * tm + rows
        has_prev = (ti > 0).astype(F32)
        dmixo = dmixo_ref[...]

        s_p = _sigmoid(proj_ref[:, DP + 2 * DR:DP + 2 * DR + D])
        s_r = _sigmoid(proj_ref[:, DP + 2 * DR + D:DIN])
        dproj_ref[:, DP + 2 * DR:DP + 2 * DR + D] = (dmixo * yp_ref[...] * s_p * (1.0 - s_p)).astype(BF16)
        dproj_ref[:, DP + 2 * DR + D:DIN] = (dmixo * yr_ref[...] * s_r * (1.0 - s_r)).astype(BF16)
        dyp = (dmixo * s_p).astype(BF16)
        dyr = (dmixo * s_r).astype(BF16)
        dypb_ref[...] = dyp
        dyrb_ref[...] = dyr

        dz = _dot_nt(dyr, wro_ref[...])
        hr_t = hr_ref[...]
        dproj_ref[:, DP + DR:DP + 2 * DR] = (dz * hr_t * dgelu_ref[...].astype(F32)).astype(BF16)
        dhr = dz * gg_ref[...]

        sp = _softplus_neg(lam_ref[...])
        v, r, gi, a, mult = (kept_ref[k] for k in range(KEPT))
        inv_mult = 1.0 / mult

        C = jnp.where(rows == tm - 1, a_carry[0:1, :], pltpu.roll(a, tm - 1, axis=0))
        g_carry[0:1, :] = _linear_scan(g_scr, C, dhr, g_carry[0:1, :], reverse=True)
        a_carry[0:1, :] = a[0:1, :]
        g = g_scr[...]

        h_prev = jnp.where(rows == 0, hrh_ref[7:8, :] * has_prev, pltpu.roll(hr_t, 1, axis=0))
        da = g * h_prev
        gm = g * mult
        dmult = g * gi * v
        di = gm * v
        dv = gm * gi
        dlog_a = da * a - dmult * (a * a * inv_mult)
        dvec_ref[4:5, :] += jnp.sum(dlog_a * r, axis=0, keepdims=True)
        dra = (dlog_a * ((-LRU_C) * sp) * r * (1.0 - r))
        drx = di * gi * (1.0 - gi)
        dvec_ref[2:3, :] += jnp.sum(dra, axis=0, keepdims=True)
        dvec_ref[3:4, :] += jnp.sum(drx, axis=0, keepdims=True)
        drab = dra.astype(BF16)
        drxb = drx.astype(BF16)
        vb = v.astype(BF16)
        dvg = []
        for h in range(NH):
            sl = slice(h * HD, (h + 1) * HD)
            dvg.append(_dot_nt(drab[:, sl], wa_ref[h]) + _dot_nt(drxb[:, sl], wx_ref[h]))
            dmat_ref[MAT_WA + h * HD:MAT_WA + (h + 1) * HD, :] += _dot_tn(vb[:, sl], drab[:, sl])
            dmat_ref[MAT_WX + h * HD:MAT_WX + (h + 1) * HD, :] += _dot_tn(vb[:, sl], drxb[:, sl])
        dv = dv + jnp.concatenate(dvg, axis=1)
        dvec_ref[1:2, :] += jnp.sum(dv, axis=0, keepdims=True)
        dvext = jnp.concatenate([dv, dv_carry[...]], axis=0)
        dv_carry[...] = dv[0:CONV_HALO, :]
        n = tm + CONV_HALO
        u_rnn = proj_ref[:, DP:DP + DR]
        du_rnn = dv * cw_ref[3:4, :]
        dvec_ref[8:9, :] += jnp.sum(dv * u_rnn, axis=0, keepdims=True)
        for k in range(3):
            dv_k = pltpu.roll(dvext, n - (3 - k), axis=0)[0:tm, :]
            du_rnn = du_rnn + dv_k * cw_ref[k:k + 1, :]
            dvec_ref[5 + k:6 + k, :] += jnp.sum(dv_k * u_rnn, axis=0, keepdims=True)
        dproj_ref[:, DP:DP + DR] = du_rnn.astype(BF16)

        dpm = _dot_nt(dyp, wpo_ref[...])
        u_pool = proj_ref[:, 0:DP]
        ext = jnp.concatenate([projh_ref[:, 0:DP] * has_prev, u_pool], axis=0)
        sums = _pool_windows(ext, +1)
        scale_v = scale_ref[...]
        qs = []
        dpooled = []
        dscale = []
        for gi_, w in enumerate(WINDOWS):
            sl = slice(gi_ * PG, (gi_ + 1) * PG)
            inv_cnt = 1.0 / jnp.minimum(t_glob + 1, w).astype(F32)
            pooled_b = (sums[gi_][POOL_HALO:, :] * inv_cnt - u_pool[:, sl]).astype(BF16)
            mixed_g = _dot(pooled_b, wg_ref[gi_])
            dscale.append(jnp.sum(dpm[:, sl] * mixed_g, axis=0, keepdims=True))
            dmixed_b = (dpm[:, sl] * scale_v[:, sl]).astype(BF16)
            dmat_ref[gi_ * PG:(gi_ + 1) * PG, :] += _dot_tn(pooled_b, dmixed_b)
            dp_g = _dot_nt(dmixed_b, wg_ref[gi_])
            dpooled.append(dp_g)
            qs.append(dp_g * inv_cnt)
        dvec_ref[0:1, 0:DP] += jnp.concatenate(dscale, axis=1)
        q = jnp.concatenate(qs, axis=1)
        qext = jnp.concatenate([q, q_carry[...]], axis=0)
        q_carry[...] = q[0:POOL_HALO, :]
        tsum = _pool_windows(qext, -1)
        for gi_ in range(4):
            dproj_ref[:, gi_ * PG:(gi_ + 1) * PG] = (tsum[gi_][0:tm, :] - dpooled[gi_]).astype(BF16)

        @pl.when(i == nt - 1)
        def _():
            dvec_ref[4:5, :] = dvec_ref[4:5, :] * (LRU_C * _sigmoid(-lam_ref[...]))

    return _call(
        body, "mixer_bwd", (nt,),
        in_specs=[rev(DIN), halo(POOL_HALO, DIN), rev(D), rev(D), rev(D), rev(DR), halo(8, DR),
                  pl.BlockSpec((KEPT, tm, DR), lambda i: (0, nt - 1 - i, 0)), rev(DR), rev(DR), _resident((4, PG, PG)), _resident((1, DP)), _resident((DP, D)), _resident((4, DR)), _resident((1, DR)),
                  _resident((NH, HD, HD)), _resident((1, DR)), _resident((NH, HD, HD)), _resident((1, DR)),
                  _resident((1, DR)), _resident((DR, D))],
        out_specs=[rev(DIN), rev(D), rev(D), _resident((MAT_ROWS, HD)), _resident((VEC_ROWS, DR))],
        out_shape=[jax.ShapeDtypeStruct((S, DIN), BF16), jax.ShapeDtypeStruct((S, D), BF16),
                   jax.ShapeDtypeStruct((S, D), BF16), jax.ShapeDtypeStruct((MAT_ROWS, HD), F32),
                   jax.ShapeDtypeStruct((VEC_ROWS, DR), F32)],
        scratch_shapes=[pltpu.VMEM((POOL_HALO, DP), F32), pltpu.VMEM((CONV_HALO, DR), F32), pltpu.VMEM((8, DR), F32),
                        pltpu.VMEM((8, DR), F32), pltpu.VMEM((tm, DR), F32)],
        operands=(proj, proj, dmixo, y_pool, y_rnn, hr, hr, kept, gg, dgelu, wg, scale, w_pool_out, conv_w, conv_b, wa, ba, wx, bx, lam,
                  w_rnn_out),
        exchange=exchange, exchange_operands=exchange_operands)


def _in_bwd(dproj, x, dx2, norm_mix, w_in, exchange=None, exchange_operands=(), tm=512):
    S = x.shape[0]

    def body(dp_ref, x_ref, dx2_ref, g_ref, w_ref, dx_ref, dg_ref):
        i = pl.program_id(0)

        @pl.when(i == 0)
        def _():
            dg_ref[...] = jnp.zeros_like(dg_ref)

        dh = _dot(dp_ref[:, 0:1536], w_ref[0:1536, :])
        dh = dh + _dot(dp_ref[:, 1536:3072], w_ref[1536:3072, :])
        dh = dh + _dot(dp_ref[:, 3072:DIN], w_ref[3072:DIN, :])
        xv = x_ref[...]
        r = lax.rsqrt(jnp.mean(xv * xv, axis=-1, keepdims=True) + EPS)
        xh = xv * r
        dg_ref[0:1, :] += jnp.sum(dh * xh, axis=0, keepdims=True)
        dxh = dh * g_ref[...]
        dx_ref[...] = dx2_ref[...] + r * (dxh - xh * jnp.mean(dxh * xh, axis=-1, keepdims=True))

    return _call(
        body, "in_bwd", (S // tm,),
        in_specs=[_rows(DIN, tm), _rows(D, tm), _rows(D, tm), _resident((1, D)), _resident((DIN, D))],
        out_specs=[_rows(D, tm), _resident((8, D))],
        out_shape=[jax.ShapeDtypeStruct((S, D), F32), jax.ShapeDtypeStruct((8, D), F32)],
        operands=(dproj, x, dx2, norm_mix, w_in), exchange=exchange, exchange_operands=exchange_operands)


def _wgrad(a, b, name, tk, tn, exchange=None, exchange_operands=()):
    S, K = a.shape
    N = b.shape[1]

    def body(a_ref, b_ref, o_ref):
        o_ref[...] = _dot_tn(a_ref[...], b_ref[...]).astype(BF16)

    (out,), exchanged = _call(
        body, name, (K // tk, N // tn),
        in_specs=[pl.BlockSpec((S, tk), lambda k, n: (0, k)), pl.BlockSpec((S, tn), lambda k, n: (0, n))],
        out_specs=[pl.BlockSpec((tk, tn), lambda k, n: (k, n))],
        out_shape=[jax.ShapeDtypeStruct((K, N), BF16)],
        operands=(a, b), exchange=exchange, exchange_operands=exchange_operands)
    return (out, exchanged) if exchange is not None else out


VEC_SCALE, VEC_CONV_B, VEC_BA, VEC_BX, VEC_LAM, VEC_CONV_W, VEC_NORM_FINAL, VEC_NORM_FFN = 0, 1, 2, 3, 4, 5, 9, 10
VEC_LOSS = 11


class _Big:
    def __init__(self, name, rows, cols, axis, n, dtype=BF16, transposed=False, src_cols=None):
        self.name, self.rows, self.cols, self.axis, self.n, self.dtype = name, rows, cols, axis, n, dtype
        self.transposed = transposed
        self.src_cols = src_cols
        self.block_shape = (rows, n) if axis == 1 else (n, cols)

    def block(self, ref, p):
        if self.axis == 1:
            return ref.at[:, pl.ds(pl.multiple_of(p * self.n, 128), self.n)]
        return ref.at[pl.ds(pl.multiple_of(p * self.n, 16 if self.dtype == BF16 else 8), self.n), :]


BIG = (_Big("w_in", DIN, D, 0, DIN // 8, transposed=True), _Big("w_pool_out", DP, D, 1, D // 8),
       _Big("w_rnn_out", DR, D, 0, DR // 8), _Big("w_o", D, D, 0, D // 8),
       _Big("w_ffn_in", 2 * DFF, D, 0, 2 * DFF // 8, transposed=True), _Big("w_ffn_out", DFF, D, 0, DFF // 8))
CONV_W = _Big("conv_w", 8, DR, 1, DR // 8, F32)
W_FFN_IN_HALVES = (_Big("w_ffn_in_lo", 2 * DFF, D // 2, 0, 2 * DFF // 8, src_cols=(0, D // 2)),
                   _Big("w_ffn_in_hi", 2 * DFF, D // 2, 0, 2 * DFF // 8, src_cols=(D // 2, D)))
GATHERED = BIG + (CONV_W,) + W_FFN_IN_HALVES

HBM_SPEC = pl.BlockSpec(memory_space=pl.ANY)
VMEM_SPEC = pl.BlockSpec(memory_space=pltpu.VMEM)


def _place():
    x, y, c = (lax.axis_index(a) for a in MESH_AXES)
    other_chips = [(1 - x, y), (x, 1 - y), (1 - x, 1 - y)]
    return x, y, c, other_chips


def _remote(src, dst, send_sems, recv_sems, idx, to):
    return pltpu.make_async_remote_copy(src_ref=src, dst_ref=dst, send_sem=send_sems.at[idx], recv_sem=recv_sems.at[idx],
                                        device_id=to, device_id_type=MESH)


def _device_index(chip, core):
    return 4 * chip[0] + 2 * chip[1] + core


class _Gather:
    def __init__(self, tensors):
        self.tensors = tuple(tensors)
        n = len(self.tensors)
        self.in_specs = [HBM_SPEC] * n
        self.out_specs = [HBM_SPEC] * n
        self.out_shape = [jax.ShapeDtypeStruct((T.rows, T.cols), T.dtype) for T in self.tensors]
        self.scratch_shapes = [pltpu.VMEM(T.block_shape, T.dtype) for T in self.tensors] + [
            pltpu.VMEM(T.block_shape, F32) for T in self.tensors] + [
            pltpu.SemaphoreType.DMA((n, 7)), pltpu.SemaphoreType.DMA((n, 7)), pltpu.SemaphoreType.DMA((n, 2))]

    def middles(self, steps):
        return [(steps // 2, self.relay), (steps - 1, self.middle)]

    def _copies(self, ins, outs, scratch):
        n = len(self.tensors)
        mine, raw, (send_sems, recv_sems, loc_sems) = scratch[:n], scratch[n:2 * n], scratch[2 * n:]
        x, y, c, chips = _place()
        sibling = (x, y, 1 - c)
        me = _device_index((x, y), c)
        relay_from = (jnp.where(c == 0, 1 - x, x), jnp.where(c == 0, y, 1 - y))
        relay_to = (jnp.where(c == 0, x, 1 - x), jnp.where(c == 0, 1 - y, y))
        loads, stores, first, relays, passed, arrivals, late = [], [], [], [], [], [], []
        for t, T in enumerate(self.tensors):
            place = T.block(outs[t], me)
            src = ins[t] if T.src_cols is None else ins[t].at[:, T.src_cols[0]:T.src_cols[1]]
            loads.append(pltpu.make_async_copy(src, raw[t], loc_sems.at[t, 0]))
            stores.append(pltpu.make_async_copy(mine[t], place, loc_sems.at[t, 1]))
            first.append(_remote(mine[t], place, send_sems, recv_sems, (t, 0), sibling))
            theirs = T.block(outs[t], _device_index((x, y), 1 - c))
            late.append(_remote(theirs, theirs, send_sems, recv_sems, (t, 0), sibling))
            relayed = T.block(outs[t], _device_index(relay_from, c))
            relays.append(_remote(relayed, relayed, send_sems, recv_sems, (t, 3), (*relay_to, c)))
            for k, chip in enumerate(chips):
                if k < 2:
                    first.append(_remote(mine[t], place, send_sems, recv_sems, (t, 1 + k), (*chip, c)))
                land = T.block(outs[t], _device_index(chip, c))
                arrivals.append(_remote(land, land, send_sems, recv_sems, (t, 1 + k), sibling))
                passed.append(_remote(land, land, send_sems, recv_sems, (t, 4 + k), sibling))
                theirs = T.block(outs[t], _device_index(chip, 1 - c))
                late.append(_remote(theirs, theirs, send_sems, recv_sems, (t, 4 + k), sibling))
        return loads, stores, first, relays, passed, arrivals, late

    def start(self, ins, outs, scratch):
        loads, stores, first, _, _, _, _ = self._copies(ins, outs, scratch)
        n = len(self.tensors)
        for cp in loads:
            cp.start()
        for t, cp in enumerate(loads):
            cp.wait()
            scratch[t][...] = scratch[n + t][...].astype(self.tensors[t].dtype)
        for cp in stores + first:
            cp.start()

    def relay(self, ins, outs, scratch, skip=0):
        _, _, _, relays, passed, arrivals, _ = self._copies(ins, outs, scratch)
        for t in range(skip, len(self.tensors)):
            arrivals[3 * t].wait_recv()
            arrivals[3 * t + 1].wait_recv()
            for cp in (relays[t], passed[3 * t], passed[3 * t + 1]):
                cp.start()

    def middle(self, ins, outs, scratch, skip=0):
        _, _, _, _, passed, arrivals, _ = self._copies(ins, outs, scratch)
        for t in range(skip, len(self.tensors)):
            arrivals[3 * t + 2].wait_recv()
            passed[3 * t + 2].start()

    def finish(self, ins, outs, scratch, skip=0):
        _, stores, first, relays, passed, _, late = self._copies(ins, outs, scratch)
        for cp in late[4 * skip:]:
            cp.wait_recv()
        for cp in first + relays + passed:
            cp.wait_send()
        for cp in stores[skip:]:
            cp.wait()


def _in_proj_gather(x, norm_mix, blocks, tensors, order, tm=512):
    S = x.shape[0]
    nt = S // tm
    n = len(tensors)
    gather = _Gather(tensors)
    CB = 2 * tensors[0].n

    def body(order_ref, x_ref, g_ref, *refs):
        ins, (proj_ref, h_ref), outs = refs[:n], refs[n:n + 2], refs[n + 2:2 * n + 2]
        (h_all, w_chip, w_sem), scratch = refs[2 * n + 2:2 * n + 5], refs[2 * n + 5:]
        q, i = pl.program_id(0), pl.program_id(1)
        _, stores, _, relays, passed, arrivals, late = gather._copies(ins, outs, scratch)

        def fetch(turn):
            rows = outs[0].at[pl.ds(pl.multiple_of(order_ref[turn] * CB, 16), CB), :]
            cp = pltpu.make_async_copy(rows, w_chip, w_sem)
            cp.start()
            cp.wait()

        @pl.when((q == 0) & (i == 0))
        def _():
            gather.start(ins, outs, scratch)
            late[0].wait_recv()
            stores[0].wait()
            fetch(0)

        @pl.when((q == 1) & (i == 0))
        def _():
            arrivals[0].wait_recv()
            arrivals[1].wait_recv()
            for cp in (relays[0], passed[0], passed[1]):
                cp.start()
            late[1].wait_recv()
            fetch(1)

        @pl.when((q == 2) & (i == 0))
        def _():
            late[2].wait_recv()
            fetch(2)
            gather.relay(ins, outs, scratch, skip=1)

        @pl.when((q == 3) & (i == 0))
        def _():
            arrivals[2].wait_recv()
            passed[2].start()
            late[3].wait_recv()
            fetch(3)

        rows = pl.ds(pl.multiple_of(i * tm, tm), tm)

        @pl.when(q == 0)
        def _():
            xv = x_ref[...]
            r = lax.rsqrt(jnp.mean(xv * xv, axis=-1, keepdims=True) + EPS)
            h = (xv * r * g_ref[...]).astype(BF16)
            h_all[rows, :] = h
            h_ref[...] = h

        proj_ref[...] = _dot_nt(h_all[rows, :], w_chip[...])

        @pl.when((q == 3) & (i == nt - 1))
        def _():
            gather.middle(ins, outs, scratch, skip=1)
            gather.finish(ins, outs, scratch, skip=1)

    row_tile = lambda q, i, order: (jnp.where(q == 0, i, nt - 1), 0)
    whole = lambda shape: pl.BlockSpec(shape, lambda q, i, order: (0,) * len(shape), pipeline_mode=pl.Buffered(1))
    outs = pl.pallas_call(
        body, name="in_proj_gather",
        grid_spec=pltpu.PrefetchScalarGridSpec(
            num_scalar_prefetch=1, grid=(4, nt),
            in_specs=[pl.BlockSpec((tm, D), row_tile), whole((1, D))] + gather.in_specs,
            out_specs=[pl.BlockSpec((tm, CB), lambda q, i, order: (i, order[q])), pl.BlockSpec((tm, D), row_tile)]
            + gather.out_specs,
            scratch_shapes=[pltpu.VMEM((S, D), BF16), pltpu.VMEM((CB, D), BF16), pltpu.SemaphoreType.DMA]
            + gather.scratch_shapes),
        out_shape=[jax.ShapeDtypeStruct((S, DIN), F32), jax.ShapeDtypeStruct((S, D), BF16)] + gather.out_shape,
        compiler_params=_params("arbitrary", "arbitrary"),
    )(order, x, norm_mix, *blocks)
    return outs[:2], outs[2:]


PAIR_ROWS = 32


def _pair_reduce(grads, tensors, name):
    nt = len(tensors)

    def body(*refs):
        ins, own_out, sums_out, landed, mine = (refs[k * nt:(k + 1) * nt] for k in range(5))
        send_sems, recv_sems, loc_sems = refs[5 * nt:]
        x, y, c, chips = _place()
        chip_of = [2 * chip[0] + chip[1] for chip in chips]
        swaps, loads = [], []
        for t, T in enumerate(tensors):
            for j in range(4):
                swaps.append(_remote(T.block(ins[t], 2 * j + 1 - c), landed[t].at[j], send_sems, recv_sems, (t, j),
                                     (x, y, 1 - c)))
            for k in range(3):
                loads.append(pltpu.make_async_copy(T.block(ins[t], 2 * chip_of[k] + c), mine[t].at[k], loc_sems.at[t, k]))
        for cp in swaps + loads:
            cp.start()
        for cp in loads:
            cp.wait()
        for cp in swaps:
            cp.wait_recv()
        stores = []
        for t, T in enumerate(tensors):
            for k in range(3):
                acc, got = mine[t].at[k], landed[t].at[chip_of[k]]

                def add(i, carry, acc=acc, got=got):
                    rows = pl.ds(pl.multiple_of(i * PAIR_ROWS, PAIR_ROWS), PAIR_ROWS)
                    acc[rows, :] = (acc[rows, :].astype(F32) + got[rows, :].astype(F32)).astype(BF16)
                    return carry

                lax.fori_loop(0, T.block_shape[0] // PAIR_ROWS, add, 0)
            stores.append(pltpu.make_async_copy(mine[t], sums_out[t], loc_sems.at[t, 3]))
            stores.append(pltpu.make_async_copy(landed[t].at[2 * x + y], own_out[t], loc_sems.at[t, 4]))
        for cp in stores:
            cp.start()
        for cp in swaps:
            cp.wait_send()
        for cp in stores:
            cp.wait()

    blocks = [T.block_shape for T in tensors]
    return pl.pallas_call(
        body, name=name,
        in_specs=[HBM_SPEC] * nt, out_specs=[HBM_SPEC] * (2 * nt),
        out_shape=[jax.ShapeDtypeStruct(b, BF16) for b in blocks] + [jax.ShapeDtypeStruct((3,) + b, BF16) for b in blocks],
        scratch_shapes=[pltpu.VMEM((4,) + b, BF16) for b in blocks] + [pltpu.VMEM((3,) + b, BF16) for b in blocks]
        + [pltpu.SemaphoreType.DMA((nt, 4)), pltpu.SemaphoreType.DMA((nt, 4)), pltpu.SemaphoreType.DMA((nt, 5))],
        compiler_params=pltpu.CompilerParams(vmem_limit_bytes=VMEM_LIMIT),
    )(*grads)


class _Scatter:
    def middles(self, steps):
        return []

    def __init__(self, tensors):
        n = len(tensors)
        self.in_specs = [HBM_SPEC] * n
        self.out_specs = [HBM_SPEC] * n
        self.out_shape = [jax.ShapeDtypeStruct((3,) + T.block_shape, BF16) for T in tensors]
        self.scratch_shapes = [pltpu.SemaphoreType.DMA((n, 3)), pltpu.SemaphoreType.DMA((n, 3))]

    def _copies(self, ins, outs, scratch):
        send_sems, recv_sems = scratch
        x, y, c, chips = _place()
        return [_remote(ins[t].at[k], outs[t].at[k], send_sems, recv_sems, (t, k), (*chip, c))
                for t in range(len(ins)) for k, chip in enumerate(chips)]

    def start(self, ins, outs, scratch):
        for cp in self._copies(ins, outs, scratch):
            cp.start()

    def finish(self, ins, outs, scratch):
        for cp in self._copies(ins, outs, scratch):
            cp.wait()


def _adamw(w, g, m, v):
    m = ADAM_B1 * m + (1.0 - ADAM_B1) * g
    v = ADAM_B2 * v + (1.0 - ADAM_B2) * (g * g)
    m_hat = m / (1.0 - ADAM_B1 ** ADAM_STEP)
    v_hat = v / (1.0 - ADAM_B2 ** ADAM_STEP)
    delta = -ADAM_LR * (m_hat / (jnp.sqrt(v_hat) + ADAM_EPS) + ADAM_WD * w)
    return delta, m, v


def _final_sum(T, g, lz1, lz2, where, w, m, v):
    rows, cols = T.block_shape
    sub = 4 if T.axis == 0 and rows % 64 == 0 and rows > 256 else 1
    blk = (rows // sub, cols)

    def body(where_ref, g_ref, l1_ref, l2_ref, w_ref, m_ref, v_ref, g_out, d_out, m_out, v_out):
        tot = g_ref[...].astype(F32) + l1_ref[...].astype(F32)
        for k in range(3):
            tot = tot + l2_ref[k].astype(F32)
        g_out[...] = tot
        d_out[...], m_out[...], v_out[...] = _adamw(w_ref[...], tot, m_ref[...], v_ref[...])

    def in_whole(r, wh):
        p = wh[0]
        return (0, p) if T.axis == 1 else (p * sub + r, 0)

    own = pl.BlockSpec(blk, lambda r, wh: (r, 0))
    return pl.pallas_call(
        body, name="grad_final_" + T.name,
        grid_spec=pltpu.PrefetchScalarGridSpec(
            num_scalar_prefetch=1, grid=(sub,),
            in_specs=[pl.BlockSpec(blk, in_whole),
                      own,
                      pl.BlockSpec((3,) + blk, lambda r, wh: (0, r, 0)), own, own, own],
            out_specs=[own] * 4),
        out_shape=[jax.ShapeDtypeStruct(T.block_shape, F32)] * 4,
        compiler_params=_params("arbitrary"),
    )(where, g, lz1, lz2, w, m, v)


VEC_PIECE = DR // 8


class _AllReduce:
    def __init__(self, items):
        self.items = tuple(items)
        n = len(self.items)
        self.in_specs = [HBM_SPEC] * n
        self.out_specs = [HBM_SPEC] * n
        self.out_shape = [jax.ShapeDtypeStruct(shape, F32) for shape, _ in self.items]
        pieces = [(shape[0] // 8, shape[1]) if axis == 0 else (shape[0], shape[1] // 8) for shape, axis in self.items]
        self.scratch_shapes = ([pltpu.VMEM((8,) + p, F32) for p in pieces] + [pltpu.VMEM(p, F32) for p in pieces] + [
            pltpu.SemaphoreType.DMA((2 * n, 8)), pltpu.SemaphoreType.DMA((2 * n, 8)), pltpu.SemaphoreType.DMA((2 * n,))])

    def middles(self, steps):
        return [(steps // 2, self.middle)]

    def _copies(self, ins, outs, scratch):
        n = len(self.items)
        landed, sums, (send_sems, recv_sems, loc_sems) = scratch[:n], scratch[n:2 * n], scratch[2 * n:]
        x, y, c, _ = _place()
        me = _device_index((x, y), c)

        def peer(r):
            return (1 - x if r & 4 else x, 1 - y if r & 2 else y, 1 - c if r & 1 else c)

        def piece(i, ref, p):
            shape, axis = self.items[i]
            if axis == 0:
                rows = shape[0] // 8
                return ref.at[pl.ds(pl.multiple_of(p * rows, 8), rows), :]
            cols = shape[1] // 8
            return ref.at[:, pl.ds(pl.multiple_of(p * cols, 128), cols)]

        own, scatter, arrivals, keep, spread, late = [], [], [], [], [], []
        for i in range(n):
            own.append(pltpu.make_async_copy(piece(i, ins[i], me), landed[i].at[0], loc_sems.at[2 * i]))
            keep.append(pltpu.make_async_copy(sums[i], piece(i, outs[i], me), loc_sems.at[2 * i + 1]))
            for r in range(1, 8):
                to = peer(r)
                p = _device_index(to[:2], to[2])
                scatter.append(_remote(piece(i, ins[i], p), landed[i].at[r], send_sems, recv_sems, (2 * i, r), to))
                spread.append(_remote(sums[i], piece(i, outs[i], me), send_sems, recv_sems, (2 * i + 1, r), to))
                late.append(_remote(sums[i], piece(i, outs[i], p), send_sems, recv_sems, (2 * i + 1, r), to))
        return own, scatter, keep, spread, late, landed, sums

    def start(self, ins, outs, scratch):
        own, scatter, _, _, _, _, _ = self._copies(ins, outs, scratch)
        for cp in own + scatter:
            cp.start()

    def middle(self, ins, outs, scratch):
        own, scatter, keep, spread, _, landed, sums = self._copies(ins, outs, scratch)
        for cp in own:
            cp.wait()
        for cp in scatter:
            cp.wait_recv()
        for i in range(len(self.items)):
            total = landed[i][0]
            for r in range(1, 8):
                total = total + landed[i][r]
            sums[i][...] = total
        for cp in keep + spread:
            cp.start()

    def finish(self, ins, outs, scratch):
        _, scatter, keep, spread, late, _, _ = self._copies(ins, outs, scratch)
        for cp in late:
            cp.wait_recv()
        for cp in scatter + spread:
            cp.wait_send()
        for cp in keep:
            cp.wait()


class _Both:
    def __init__(self, a, b):
        self.a, self.b = a, b
        self.in_specs, self.out_specs = a.in_specs + b.in_specs, a.out_specs + b.out_specs
        self.out_shape, self.scratch_shapes = a.out_shape + b.out_shape, a.scratch_shapes + b.scratch_shapes

    def _each(self, ins, outs, scratch):
        a = self.a
        i, o, s = len(a.in_specs), len(a.out_specs), len(a.scratch_shapes)
        return (a, ins[:i], outs[:o], scratch[:s]), (self.b, ins[i:], outs[o:], scratch[s:])

    def middles(self, steps):
        def of(which, middle):
            return lambda ins, outs, scratch: middle(*self._each(ins, outs, scratch)[which][1:])
        return [(at, of(which, middle)) for which, e in enumerate((self.a, self.b)) for at, middle in e.middles(steps)]

    def start(self, ins, outs, scratch):
        for e, i, o, s in self._each(ins, outs, scratch):
            e.start(i, o, s)

    def finish(self, ins, outs, scratch):
        for e, i, o, s in self._each(ins, outs, scratch):
            e.finish(i, o, s)


def _all_reduce(arrays, items, name):
    reduce = _AllReduce(items)
    n = len(items)

    def body(*refs):
        ins, outs, scratch = refs[:n], refs[n:2 * n], refs[2 * n:]
        reduce.start(ins, outs, scratch)
        reduce.middle(ins, outs, scratch)
        reduce.finish(ins, outs, scratch)

    return pl.pallas_call(
        body, name=name, in_specs=reduce.in_specs, out_specs=reduce.out_specs, out_shape=reduce.out_shape,
        scratch_shapes=reduce.scratch_shapes,
    )(*arrays)


def _adam_small(grads, wmv):
    n = len(grads)

    def body(*refs):
        g_refs, rest = refs[:n], refs[n:]
        ins, outs = rest[:3 * n], rest[3 * n:]
        for i in range(n):
            d, m, v = _adamw(ins[3 * i][...], g_refs[i][...], ins[3 * i + 1][...], ins[3 * i + 2][...])
            outs[3 * i][...], outs[3 * i + 1][...], outs[3 * i + 2][...] = d, m, v

    flat = [a for t in wmv for a in t]
    return pl.pallas_call(
        body, name="adam_small",
        in_specs=[VMEM_SPEC] * (4 * n), out_specs=[VMEM_SPEC] * (3 * n),
        out_shape=[jax.ShapeDtypeStruct(a.shape, F32) for a in flat],
    )(*grads, *flat)


WEIGHT_NAMES = ("norm_mix", "w_in", "w_pool_grp", "pool_scale", "w_pool_out", "conv_w", "conv_b", "w_rg_a", "b_rg_a", "w_rg_x",
                "b_rg_x", "lru_lambda", "w_rnn_out", "w_o", "norm_ffn", "w_ffn_in", "w_ffn_out", "norm_final")


def kernel(x, norm_mix, w_in, w_pool_grp, pool_scale, w_pool_out, conv_w, conv_b, w_rg_a, b_rg_a, w_rg_x, b_rg_x, lru_lambda, w_rnn_out, w_o, norm_ffn, w_ffn_in, w_ffn_out, norm_final, loss_target, m_norm_mix, m_w_in, m_w_pool_grp, m_pool_scale, m_w_pool_out, m_conv_w, m_conv_b, m_w_rg_a, m_b_rg_a, m_w_rg_x, m_b_rg_x, m_lru_lambda, m_w_rnn_out, m_w_o, m_norm_ffn, m_w_ffn_in, m_w_ffn_out, m_norm_final, v_norm_mix, v_w_in, v_w_pool_grp, v_pool_scale, v_w_pool_out, v_conv_w, v_conv_b, v_w_rg_a, v_b_rg_a, v_w_rg_x, v_b_rg_x, v_lru_lambda, v_w_rnn_out, v_w_o, v_norm_ffn, v_w_ffn_in, v_w_ffn_out, v_norm_final):
    w = dict(norm_mix=norm_mix, w_in=w_in, w_pool_grp=w_pool_grp, pool_scale=pool_scale, w_pool_out=w_pool_out, conv_w=conv_w,
             conv_b=conv_b, w_rg_a=w_rg_a, b_rg_a=b_rg_a, w_rg_x=w_rg_x, b_rg_x=b_rg_x, lru_lambda=lru_lambda,
             w_rnn_out=w_rnn_out, w_o=w_o, norm_ffn=norm_ffn, w_ffn_in=w_ffn_in, w_ffn_out=w_ffn_out, norm_final=norm_final)
    m = dict(norm_mix=m_norm_mix, w_in=m_w_in, w_pool_grp=m_w_pool_grp, pool_scale=m_pool_scale, w_pool_out=m_w_pool_out,
             conv_w=m_conv_w, conv_b=m_conv_b, w_rg_a=m_w_rg_a, b_rg_a=m_b_rg_a, w_rg_x=m_w_rg_x, b_rg_x=m_b_rg_x,
             lru_lambda=m_lru_lambda, w_rnn_out=m_w_rnn_out, w_o=m_w_o, norm_ffn=m_norm_ffn, w_ffn_in=m_w_ffn_in,
             w_ffn_out=m_w_ffn_out, norm_final=m_norm_final)
    v = dict(norm_mix=v_norm_mix, w_in=v_w_in, w_pool_grp=v_w_pool_grp, pool_scale=v_pool_scale, w_pool_out=v_w_pool_out,
             conv_w=v_conv_w, conv_b=v_conv_b, w_rg_a=v_w_rg_a, b_rg_a=v_b_rg_a, w_rg_x=v_w_rg_x, b_rg_x=v_b_rg_x,
             lru_lambda=v_lru_lambda, w_rnn_out=v_w_rnn_out, w_o=v_w_o, norm_ffn=v_norm_ffn, w_ffn_in=v_w_ffn_in,
             w_ffn_out=v_w_ffn_out, norm_final=v_norm_final)
    xi, yi, ci = (lax.axis_index(a) for a in MESH_AXES)
    chip = 2 * xi + yi

    def held(T, a):
        return jnp.swapaxes(a, 0, 1) if T.transposed else a

    where = jnp.stack([2 * chip + ci]).astype(jnp.int32)
    by_name = {T.name: T for T in GATHERED}
    block = {T.name: held(T, w[T.name][0]) for T in BIG}
    block["conv_w"] = jnp.pad(conv_w[0], ((0, CONV_W.rows - 4), (0, 0)))
    block["w_ffn_in_lo"] = block["w_ffn_in_hi"] = block["w_ffn_in"]

    def gather_of(*names):
        return dict(exchange=_Gather([by_name[n] for n in names]), exchange_operands=[block[n] for n in names])

    def pair_sums(names, partials, tag):
        out = _pair_reduce(partials, [by_name[n] for n in names], "grad_pair_reduce_" + tag)
        return list(out[:len(names)]), list(out[len(names):])

    xs, target = x[0], loss_target[0]
    wg_b, wa_b, wx_b = (a[0].astype(BF16) for a in (w_pool_grp, w_rg_a, w_rg_x))
    ba2, bx2 = b_rg_a.reshape(1, DR), b_rg_x.reshape(1, DR)
    first = ("w_in", "w_pool_out", "w_rnn_out", "conv_w", "w_o")
    order = jnp.stack([chip, 2 * (1 - xi) + yi, 2 * xi + (1 - yi), 2 * (1 - xi) + (1 - yi)]).astype(jnp.int32)
    (proj, h1), (w_in_g, w_pool_out_g, w_rnn_out_g, conv_g, w_o_g) = _in_proj_gather(
        xs, norm_mix, [block[n] for n in first], [by_name[n] for n in first], order)
    mixer_weights = (wg_b, pool_scale, w_pool_out_g, conv_g[0:4], conv_b, wa_b, ba2, wx_b, bx2, lru_lambda, w_rnn_out_g)
    (pm, y_pool, hr, z, y_rnn, kept), (w_ffn_lo_g, w_ffn_hi_g) = _mixer_fwd(
        proj, *mixer_weights, **gather_of("w_ffn_in_lo", "w_ffn_in_hi"))
    (mix, x2, h2), _ = _merge_out(xs, proj, y_pool, y_rnn, w_o_g, norm_ffn)
    (gu, act, gelu_gate, dgelu_gate), (w_ffn_out_g,) = _ffn_up(h2, w_ffn_lo_g, w_ffn_hi_g, proj, **gather_of("w_ffn_out"))
    dx3, dx3b, loss_part, dvec_fin = _ffn_down_loss(act, x2, target, w_ffn_out_g, norm_final.reshape(1, D))

    dgu = _ffn_bwd_down(dx3b, gu, w_ffn_out_g)
    dx2, dx2b, dmixo, dvec_ffn = _ffn_bwd_up(dgu, x2, dx3, w_ffn_lo_g, w_ffn_hi_g, norm_ffn, w_o_g)
    names_a = ("w_ffn_in", "w_ffn_out", "w_o")
    part_a = [_wgrad(dgu, h2, "wgrad_ffn_in", 1408, 512), _wgrad(act, dx3b, "wgrad_ffn_out", 1408, 512),
              _wgrad(mix, dx2b, "wgrad_o", 1024, 256)]
    lz1_a, sums_a = pair_sums(names_a, part_a, "ffn")
    (dproj, dypb, dyrb, dmat, dvec_mix), lz2_a = _mixer_bwd(
        proj, dmixo, y_pool, y_rnn, hr, kept, gelu_gate, dgelu_gate, *mixer_weights,
        exchange=_Scatter([by_name[n] for n in names_a]), exchange_operands=sums_a)
    names_b = ("w_pool_out", "w_rnn_out")
    part_b = [_wgrad(pm, dypb, "wgrad_pool_out", 512, 256), _wgrad(z, dyrb, "wgrad_rnn_out", 1024, 256)]
    lz1_b, sums_b = pair_sums(names_b, part_b, "mix")
    dvec = jnp.concatenate([dvec_mix[0:9], dvec_fin[0:1], dvec_ffn[0:1], jnp.pad(loss_part, ((0, 0), (0, DR - 1))),
                            jnp.zeros((VEC_ROWS - 12, DR), F32)], axis=0)
    g_in, exchanged = _wgrad(
        dproj, h1, "wgrad_in", 1152, 1024,
        exchange=_Both(_Scatter([by_name[n] for n in names_b]), _AllReduce([((MAT_ROWS, HD), 0), ((VEC_ROWS, DR), 1)])),
        exchange_operands=sums_b + [dmat, dvec])
    lz2_b, (mat, vec) = exchanged[:2], exchanged[2:]
    loss = vec[VEC_LOSS, 0]
    lz1_c, sums_c = pair_sums(("w_in",), [g_in], "in")
    (grad_x, dvec_in), lz2_c = _in_bwd(dproj, xs, dx2, norm_mix, w_in_g,
                                       exchange=_Scatter([by_name["w_in"]]), exchange_operands=sums_c)
    (vec_in,) = _all_reduce([dvec_in], [((8, D), 1)], "all_reduce_norm_mix")

    grads, delta, new_m, new_v = {}, {}, {}, {}
    for n, g, l1, l2 in zip(names_a + names_b + ("w_in",), part_a + part_b + [g_in], lz1_a + lz1_b + lz1_c,
                            lz2_a + lz2_b + lz2_c):
        T = by_name[n]
        out = _final_sum(T, g, l1, l2, where, held(T, w[n][0]), held(T, m[n][0]), held(T, v[n][0]))
        grads[n], delta[n], new_m[n], new_v[n] = (held(T, a) for a in out)
    me = 4 * xi + 2 * yi + ci
    small_grads = dict(
        w_pool_grp=mat[0:MAT_WA], w_rg_a=mat[MAT_WA:MAT_WX], w_rg_x=mat[MAT_WX:MAT_ROWS],
        pool_scale=vec[VEC_SCALE:VEC_SCALE + 1, 0:DP], conv_b=vec[VEC_CONV_B:VEC_CONV_B + 1],
        b_rg_a=vec[VEC_BA:VEC_BA + 1], b_rg_x=vec[VEC_BX:VEC_BX + 1], lru_lambda=vec[VEC_LAM:VEC_LAM + 1],
        conv_w=lax.dynamic_slice(vec, (VEC_CONV_W, VEC_PIECE * me), (4, VEC_PIECE)),
        norm_final=vec[VEC_NORM_FINAL:VEC_NORM_FINAL + 1], norm_ffn=vec[VEC_NORM_FFN:VEC_NORM_FFN + 1],
        norm_mix=vec_in[0:1])
    names = list(small_grads)
    as2d = lambda a, g: a.reshape(g.shape)
    upd = _adam_small([small_grads[n] for n in names],
                      [(as2d(w[n], small_grads[n]), as2d(m[n], small_grads[n]), as2d(v[n], small_grads[n])) for n in names])
    for i, n in enumerate(names):
        grads[n] = small_grads[n]
        delta[n], new_m[n], new_v[n] = upd[3 * i:3 * i + 3]

    shaped = lambda d: [d[n].reshape(w[n].shape) for n in WEIGHT_NAMES]
    return (loss, grad_x[None], *shaped(grads), *shaped(delta), *shaped(new_m), *shaped(new_v))
```

```python
import math

import jax
import jax.numpy as jnp
from jax import lax
from jax.experimental import pallas as pl
from jax.experimental.pallas import tpu as pltpu

F32 = jnp.float32
BF16 = jnp.bfloat16

D = 1024
DP = 512
PG = 128
WINDOWS = (2, 4, 8, 16)
DR = 1024
NH = 8
HD = 128
DIN = 4608
DFF = 2816
EPS = 1e-6
LRU_C = 8.0
POOL_HALO = 16
CONV_HALO = 8
KEPT = 3

ADAM_LR = 0.001
ADAM_B1 = 0.9
ADAM_B2 = 0.999
ADAM_EPS = 1e-08
ADAM_WD = 0.01
ADAM_STEP = 10

VMEM_LIMIT = 56 * 1024 * 1024
MESH_AXES = ("x", "y", "c")
MESH = pl.DeviceIdType.MESH


def _dot(a, b):
    return jnp.dot(a, b, preferred_element_type=F32)


def _dot_nt(a, b):
    return lax.dot_general(a, b, (((1,), (1,)), ((), ())), preferred_element_type=F32)


def _dot_tn(a, b):
    return lax.dot_general(a, b, (((0,), (0,)), ((), ())), preferred_element_type=F32)


def _params(*sem):
    return pltpu.CompilerParams(dimension_semantics=sem, vmem_limit_bytes=VMEM_LIMIT)


def _resident(shape):
    nd = len(shape)
    return pl.BlockSpec(shape, lambda i: (0,) * nd, pipeline_mode=pl.Buffered(1))


def _rows(shape_cols, tm):
    return pl.BlockSpec((tm, shape_cols), lambda i: (i, 0))


def _call(body, name, grid, in_specs, out_specs, out_shape, operands, scratch_shapes=(), exchange=None, exchange_operands=()):
    n_in, n_out, n_scr = len(in_specs), len(out_specs), len(scratch_shapes)
    steps = math.prod(grid)
    if exchange is None:
        outs = pl.pallas_call(body, name=name, grid=grid, in_specs=in_specs, out_specs=out_specs, out_shape=out_shape,
                              scratch_shapes=list(scratch_shapes), compiler_params=_params(*["arbitrary"] * len(grid)))(*operands)
        return outs, []
    e_in, e_out = len(exchange.in_specs), len(exchange.out_specs)

    def hosted(*refs):
        ins, refs = refs[:n_in], refs[n_in:]
        e_ins, refs = refs[:e_in], refs[e_in:]
        outs, refs = refs[:n_out], refs[n_out:]
        e_outs, refs = refs[:e_out], refs[e_out:]
        scr, e_scr = refs[:n_scr], refs[n_scr:]
        step = pl.program_id(0)
        for axis in range(1, len(grid)):
            step = step * grid[axis] + pl.program_id(axis)
        pl.when(step == 0)(lambda: exchange.start(e_ins, e_outs, e_scr))
        for at, middle in exchange.middles(steps):
            pl.when(step == at)(lambda middle=middle: middle(e_ins, e_outs, e_scr))
        body(*ins, *outs, *scr)
        pl.when(step == steps - 1)(lambda: exchange.finish(e_ins, e_outs, e_scr))

    outs = pl.pallas_call(
        hosted, name=name, grid=grid, in_specs=list(in_specs) + exchange.in_specs,
        out_specs=list(out_specs) + exchange.out_specs, out_shape=list(out_shape) + exchange.out_shape,
        scratch_shapes=list(scratch_shapes) + exchange.scratch_shapes,
        compiler_params=_params(*["arbitrary"] * len(grid)))(*operands, *exchange_operands)
    return outs[:n_out], outs[n_out:]


GELU_C = math.sqrt(2.0 / math.pi)
GELU_K = 0.044715 * GELU_C


def _gelu(x, with_grad=False):
    x2 = x * x
    t = jnp.tanh(x * (GELU_C + GELU_K * x2))
    hx = 0.5 * x
    y = hx + hx * t
    if not with_grad:
        return y
    return y, 0.5 + 0.5 * t + hx * (1.0 - t * t) * (GELU_C + (3.0 * GELU_K) * x2)


def _softplus_neg(lam):
    z = jnp.exp(-jnp.abs(lam))
    u = 1.0 + z
    dlt = u - 1.0
    log1p = jnp.where(dlt == 0.0, z, jnp.log(u) * (z / jnp.where(dlt == 0.0, 1.0, dlt)))
    return jnp.maximum(-lam, 0.0) + log1p


def _sigmoid(x):
    return 0.5 * jnp.tanh(0.5 * x) + 0.5


def _linear_scan(out_ref, A, B, h0, reverse):
    n = A.shape[0]
    sub = lax.broadcasted_iota(jnp.int32, (8, 1), 0)
    tiles = range(n // 8 - 1, -1, -1) if reverse else range(n // 8)
    carry = h0
    for j in tiles:
        a, b = A[8 * j:8 * j + 8, :], B[8 * j:8 * j + 8, :]
        for d in (1, 2, 4):
            keep = (sub < 8 - d) if reverse else (sub >= d)
            shift = 8 - d if reverse else d
            b = jnp.where(keep, a * pltpu.roll(b, shift, axis=0) + b, b)
            a = jnp.where(keep, a * pltpu.roll(a, shift, axis=0), a)
        h = a * carry + b
        out_ref[8 * j:8 * j + 8, :] = h
        carry = h[0:1, :] if reverse else h[7:8, :]
    return carry


def _pool_windows(ext, shift_sign):
    n = ext.shape[0]
    s = ext
    outs = []
    for w in WINDOWS:
        d = w // 2
        s = s + pltpu.roll(s, d if shift_sign > 0 else n - d, axis=0)
        outs.append(s[:, :PG])
        s = s[:, PG:]
    return outs


def _conv_taps(uext):
    taps = []
    for k in range(4):
        sh = 3 - k
        v = uext if sh == 0 else pltpu.roll(uext, sh, axis=0)
        taps.append(v[CONV_HALO:, :])
    return taps


def _gates(v, wa_ref, ba_ref, wx_ref, bx_ref, sp):
    vb = v.astype(BF16)
    ra, rx = [], []
    for h in range(NH):
        vh = vb[:, h * HD:(h + 1) * HD]
        ra.append(_dot(vh, wa_ref[h]))
        rx.append(_dot(vh, wx_ref[h]))
    r = _sigmoid(jnp.concatenate(ra, axis=1) + ba_ref[...])
    i = _sigmoid(jnp.concatenate(rx, axis=1) + bx_ref[...])
    log_a = r * ((-LRU_C) * sp)
    a = jnp.exp(log_a)
    one_minus = -jnp.tanh(log_a) * (1.0 + a * a)
    return r, i, a, jnp.sqrt(one_minus), lax.rsqrt(one_minus)


def _mixer_fwd(proj, wg, scale, w_pool_out, conv_w, conv_b, wa, ba, wx, bx, lam, w_rnn_out, exchange=None,
               exchange_operands=(), tm=256):
    S = proj.shape[0]
    UW = DP + 2 * DR

    def body(proj_ref, wg_ref, scale_ref, wpo_ref, cw_ref, cb_ref, wa_ref, ba_ref, wx_ref, bx_ref, lam_ref, wro_ref,
             pm_ref, ypool_ref, hr_ref, z_ref, yrnn_ref, kept_ref, gates_ref, pool_carry, conv_carry, h_carry):
        i = pl.program_id(0)

        @pl.when(i == 0)
        def _():
            pool_carry[...] = jnp.zeros_like(pool_carry)
            conv_carry[...] = jnp.zeros_like(conv_carry)
            h_carry[...] = jnp.zeros_like(h_carry)

        rows = lax.broadcasted_iota(jnp.int32, (tm, 1), 0)
        t_glob = i * tm + rows

        u_pool = proj_ref[:, 0:DP]
        ext = jnp.concatenate([pool_carry[...], u_pool], axis=0)
        pool_carry[...] = u_pool[tm - POOL_HALO:, :]
        sums = _pool_windows(ext, +1)
        mixed = []
        for g, w in enumerate(WINDOWS):
            inv_cnt = 1.0 / jnp.minimum(t_glob + 1, w).astype(F32)
            pooled_g = sums[g][POOL_HALO:, :] * inv_cnt - u_pool[:, g * PG:(g + 1) * PG]
            mixed.append(_dot(pooled_g.astype(BF16), wg_ref[g]))
        pm = (jnp.concatenate(mixed, axis=1) * scale_ref[...]).astype(BF16)
        pm_ref[...] = pm
        ypool_ref[...] = _dot(pm, wpo_ref[...]).astype(BF16)

        u_rnn = proj_ref[:, DP:DP + DR]
        uext = jnp.concatenate([conv_carry[...], u_rnn], axis=0)
        conv_carry[...] = u_rnn[tm - CONV_HALO:, :]
        taps = _conv_taps(uext)
        v = cb_ref[...]
        for k in range(4):
            v = v + taps[k] * cw_ref[k:k + 1, :]
        sp = _softplus_neg(lam_ref[...])
        r, gi, a, mult, _ = _gates(v, wa_ref, ba_ref, wx_ref, bx_ref, sp)
        for k, kept in enumerate((v, a, mult)):
            kept_ref[k] = kept
        for k, kept in enumerate((r, gi)):
            gates_ref[k] = kept.astype(BF16)
        h_carry[0:1, :] = _linear_scan(hr_ref, a, mult * gi * v, h_carry[0:1, :], reverse=False)
        z = (hr_ref[...] * _gelu(proj_ref[:, DP + DR:UW])).astype(BF16)
        z_ref[...] = z
        yrnn_ref[...] = _dot(z, wro_ref[...]).astype(BF16)

    return _call(
        body, "mixer_fwd", (S // tm,),
        in_specs=[_rows(UW, tm), _resident((4, PG, PG)), _resident((1, DP)), _resident((DP, D)), _resident((4, DR)),
                  _resident((1, DR)), _resident((NH, HD, HD)), _resident((1, DR)), _resident((NH, HD, HD)),
                  _resident((1, DR)), _resident((1, DR)), _resident((DR, D))],
        out_specs=[_rows(DP, tm), _rows(D, tm), _rows(DR, tm), _rows(DR, tm), _rows(D, tm),
                   pl.BlockSpec((KEPT, tm, DR), lambda i: (0, i, 0)), pl.BlockSpec((2, tm, DR), lambda i: (0, i, 0))],
        out_shape=[jax.ShapeDtypeStruct((S, DP), BF16),
                   jax.ShapeDtypeStruct((S, D), BF16), jax.ShapeDtypeStruct((S, DR), F32),
                   jax.ShapeDtypeStruct((S, DR), BF16), jax.ShapeDtypeStruct((S, D), BF16),
                   jax.ShapeDtypeStruct((KEPT, S, DR), F32), jax.ShapeDtypeStruct((2, S, DR), BF16)],
        scratch_shapes=[pltpu.VMEM((POOL_HALO, DP), F32), pltpu.VMEM((CONV_HALO, DR), F32), pltpu.VMEM((8, DR), F32)],
        operands=(proj, wg, scale, w_pool_out, conv_w, conv_b, wa, ba, wx, bx, lam, w_rnn_out),
        exchange=exchange, exchange_operands=exchange_operands)


FF_CHUNKS = ((0, 768), (768, 1536), (1536, 2304), (2304, DFF))


def _rms(x):
    r = lax.rsqrt(jnp.mean(x * x, axis=-1, keepdims=True) + EPS)
    return r, x * r


def _rms_bwd(dh, g, r, xh):
    dxh = dh * g
    return r * (dxh - xh * jnp.mean(dxh * xh, axis=-1, keepdims=True))


def _merge_out(x, proj, y_pool, y_rnn, w_o, norm_ffn, exchange=None, exchange_operands=(), tm=512):
    S = x.shape[0]
    GL0 = (DP + 2 * DR) // 512

    def gl_spec(k):
        return pl.BlockSpec((tm, 512), lambda i: (i, GL0 + k))

    def body(x_ref, gl0, gl1, gl2, gl3, yp_ref, yr_ref, wo_ref, gf_ref, mix_ref, x2_ref, h2_ref):
        s_p = _sigmoid(jnp.concatenate([gl0[...], gl1[...]], axis=1))
        s_r = _sigmoid(jnp.concatenate([gl2[...], gl3[...]], axis=1))
        mix = (s_p * yp_ref[...].astype(F32) + s_r * yr_ref[...].astype(F32)).astype(BF16)
        mix_ref[...] = mix
        x2 = x_ref[...] + _dot(mix, wo_ref[...])
        x2_ref[...] = x2
        _, xh2 = _rms(x2)
        h2_ref[...] = (xh2 * gf_ref[...]).astype(BF16)

    return _call(
        body, "merge_out", (S // tm,),
        in_specs=[_rows(D, tm), gl_spec(0), gl_spec(1), gl_spec(2), gl_spec(3), _rows(D, tm), _rows(D, tm),
                  _resident((D, D)), _resident((1, D))],
        out_specs=[_rows(D, tm), _rows(D, tm), _rows(D, tm)],
        out_shape=[jax.ShapeDtypeStruct((S, D), BF16), jax.ShapeDtypeStruct((S, D), F32), jax.ShapeDtypeStruct((S, D), BF16)],
        operands=(x, proj, proj, proj, proj, y_pool, y_rnn, w_o, norm_ffn),
        exchange=exchange, exchange_operands=exchange_operands)


def _ffn_up(h2, w_lo, w_hi, exchange=None, exchange_operands=(), tm=512):
    S = h2.shape[0]
    HALF = D // 2

    def body(h_ref, lo_ref, hi_ref, back_ref, act_ref):
        h_lo, h_hi = h_ref[:, 0:HALF], h_ref[:, HALF:D]
        for c0, c1 in FF_CHUNKS:
            gate = _dot_nt(h_lo, lo_ref[c0:c1, :]) + _dot_nt(h_hi, hi_ref[c0:c1, :])
            up = _dot_nt(h_lo, lo_ref[DFF + c0:DFF + c1, :]) + _dot_nt(h_hi, hi_ref[DFF + c0:DFF + c1, :])
            sg = _sigmoid(gate)
            silu = gate * sg
            back_ref[:, c0:c1] = (up * (sg * (1.0 + gate * (1.0 - sg)))).astype(BF16)
            back_ref[:, DFF + c0:DFF + c1] = silu.astype(BF16)
            act_ref[:, c0:c1] = (silu * up).astype(BF16)

    return _call(
        body, "ffn_up", (S // tm,),
        in_specs=[_rows(D, tm), _resident((2 * DFF, HALF)), _resident((2 * DFF, HALF))],
        out_specs=[_rows(2 * DFF, tm), _rows(DFF, tm)],
        out_shape=[jax.ShapeDtypeStruct((S, 2 * DFF), BF16), jax.ShapeDtypeStruct((S, DFF), BF16)],
        operands=(h2, w_lo, w_hi), exchange=exchange, exchange_operands=exchange_operands)


def _ffn_down_loss(act, x2, target, w_ffn_out, norm_final, tm=512):
    S = act.shape[0]

    def body(act_ref, x2_ref, t_ref, w_ref, gn_ref, dx3_ref, dx3b_ref, loss_ref, dvec_ref):
        i = pl.program_id(0)

        @pl.when(i == 0)
        def _():
            loss_ref[...] = jnp.zeros_like(loss_ref)
            dvec_ref[...] = jnp.zeros_like(dvec_ref)

        x3 = x2_ref[...] + _dot(act_ref[...], w_ref[...])
        r3, xh3 = _rms(x3)
        g_fin = gn_ref[...]
        e = xh3 * g_fin - t_ref[...]
        loss_ref[...] += jnp.sum(e * e, axis=(0, 1), keepdims=True) * (0.5 / D)
        dy = e * (1.0 / D)
        dvec_ref[0:1, :] += jnp.sum(dy * xh3, axis=0, keepdims=True)
        dx3 = _rms_bwd(dy, g_fin, r3, xh3)
        dx3_ref[...] = dx3
        dx3b_ref[...] = dx3.astype(BF16)

    return pl.pallas_call(
        body, name="ffn_down_loss", grid=(S // tm,),
        in_specs=[_rows(DFF, tm), _rows(D, tm), _rows(D, tm), _resident((DFF, D)), _resident((1, D))],
        out_specs=[_rows(D, tm), _rows(D, tm), _resident((1, 1)), _resident((8, D))],
        out_shape=[jax.ShapeDtypeStruct((S, D), F32), jax.ShapeDtypeStruct((S, D), BF16),
                   jax.ShapeDtypeStruct((1, 1), F32), jax.ShapeDtypeStruct((8, D), F32)],
        compiler_params=_params("arbitrary"),
    )(act, x2, target, w_ffn_out, norm_final)


def _ffn_bwd_down(dx3b, gu, w_ffn_out, tm=512):
    S = dx3b.shape[0]

    def body(d_ref, back_ref, w_ref, dgu_ref):
        d = d_ref[...]
        for c0, c1 in FF_CHUNKS:
            dact = _dot_nt(d, w_ref[c0:c1, :])
            dgu_ref[:, c0:c1] = (dact * back_ref[:, c0:c1].astype(F32)).astype(BF16)
            dgu_ref[:, DFF + c0:DFF + c1] = (dact * back_ref[:, DFF + c0:DFF + c1].astype(F32)).astype(BF16)

    return pl.pallas_call(
        body, name="ffn_bwd_down", grid=(S // tm,),
        in_specs=[_rows(D, tm), _rows(2 * DFF, tm), _resident((DFF, D))],
        out_specs=_rows(2 * DFF, tm),
        out_shape=jax.ShapeDtypeStruct((S, 2 * DFF), BF16),
        compiler_params=_params("parallel"),
    )(dx3b, gu, w_ffn_out)


def _ffn_bwd_up(dgu, x2, dx3, w_lo, w_hi, norm_ffn, w_o, tm=512):
    S = dgu.shape[0]
    HALF = D // 2

    def body(dgu_ref, x2_ref, dx3_ref, lo_ref, hi_ref, gf_ref, wo_ref, dx2_ref, dx2b_ref, dmixo_ref, dvec_ref):
        i = pl.program_id(0)

        @pl.when(i == 0)
        def _():
            dvec_ref[...] = jnp.zeros_like(dvec_ref)

        dgate, dup = dgu_ref[:, 0:DFF], dgu_ref[:, DFF:2 * DFF]
        dh2 = jnp.concatenate([_dot(dgate, w[0:DFF, :]) + _dot(dup, w[DFF:2 * DFF, :]) for w in (lo_ref, hi_ref)], axis=1)
        r2, xh2 = _rms(x2_ref[...])
        dvec_ref[0:1, :] += jnp.sum(dh2 * xh2, axis=0, keepdims=True)
        dx2 = dx3_ref[...] + _rms_bwd(dh2, gf_ref[...], r2, xh2)
        dx2_ref[...] = dx2
        dx2b = dx2.astype(BF16)
        dx2b_ref[...] = dx2b
        dmixo_ref[...] = _dot_nt(dx2b, wo_ref[...]).astype(BF16)

    return pl.pallas_call(
        body, name="ffn_bwd_up", grid=(S // tm,),
        in_specs=[_rows(2 * DFF, tm), _rows(D, tm), _rows(D, tm), _resident((2 * DFF, HALF)), _resident((2 * DFF, HALF)),
                  _resident((1, D)), _resident((D, D))],
        out_specs=[_rows(D, tm), _rows(D, tm), _rows(D, tm), _resident((8, D))],
        out_shape=[jax.ShapeDtypeStruct((S, D), F32), jax.ShapeDtypeStruct((S, D), BF16), jax.ShapeDtypeStruct((S, D), BF16),
                   jax.ShapeDtypeStruct((8, D), F32)],
        compiler_params=_params("arbitrary"),
    )(dgu, x2, dx3, w_lo, w_hi, norm_ffn, w_o)


VEC_ROWS = 16
MAT_WA = 4 * PG
MAT_WX = MAT_WA + NH * HD
MAT_ROWS = MAT_WX + NH * HD


def _mixer_bwd(proj, dmixo, y_pool, y_rnn, hr, kept, gates, wg, scale, w_pool_out, conv_w, conv_b, wa, ba, wx, bx, lam, w_rnn_out,
               exchange=None, exchange_operands=(), tm=256):
    S = proj.shape[0]
    nt = S // tm

    def rev(cols):
        return pl.BlockSpec((tm, cols), lambda i: (nt - 1 - i, 0))

    def halo(rows_, cols):
        per = tm // rows_
        return pl.BlockSpec((rows_, cols), lambda i: (jnp.maximum((nt - 1 - i) * per - 1, 0), 0))

    def body(proj_ref, projh_ref, dmixo_ref, yp_ref, yr_ref, hr_ref, hrh_ref, kept_ref, gates_ref, wg_ref, scale_ref, wpo_ref, cw_ref, cb_ref,
             wa_ref, ba_ref, wx_ref, bx_ref, lam_ref, wro_ref,
             dproj_ref, dypb_ref, dyrb_ref, dmat_ref, dvec_ref,
             q_carry, dv_carry, a_carry, g_carry, g_scr):
        i = pl.program_id(0)
        ti = nt - 1 - i

        @pl.when(i == 0)
        def _():
            q_carry[...] = jnp.zeros_like(q_carry)
            dv_carry[...] = jnp.zeros_like(dv_carry)
            a_carry[...] = jnp.zeros_like(a_carry)
            g_carry[...] = jnp.zeros_like(g_carry)
            dmat_ref[...] = jnp.zeros_like(dmat_ref)
            dvec_ref[...] = jnp.zeros_like(dvec_ref)

        rows = lax.broadcasted_iota(jnp.int32, (tm, 1), 0)
        t_glob = ti * tm + rows
        has_prev = (ti > 0).astype(F32)
        dmixo = dmixo_ref[...].astype(F32)

        s_p = _sigmoid(proj_ref[:, DP + 2 * DR:DP + 2 * DR + D])
        s_r = _sigmoid(proj_ref[:, DP + 2 * DR + D:DIN])
        dproj_ref[:, DP + 2 * DR:DP + 2 * DR + D] = (dmixo * yp_ref[...].astype(F32) * s_p * (1.0 - s_p)).astype(BF16)
        dproj_ref[:, DP + 2 * DR + D:DIN] = (dmixo * yr_ref[...].astype(F32) * s_r * (1.0 - s_r)).astype(BF16)
        dyp = (dmixo * s_p).astype(BF16)
        dyr = (dmixo * s_r).astype(BF16)
        dypb_ref[...] = dyp
        dyrb_ref[...] = dyr

        dz = _dot_nt(dyr, wro_ref[...])
        u_gate = proj_ref[:, DP + DR:DP + 2 * DR]
        gg, dgelu = _gelu(u_gate, with_grad=True)
        hr_t = hr_ref[...]
        dproj_ref[:, DP + DR:DP + 2 * DR] = (dz * hr_t * dgelu).astype(BF16)
        dhr = dz * gg

        sp = _softplus_neg(lam_ref[...])
        v, a, mult = (kept_ref[k] for k in range(KEPT))
        r, gi = (gates_ref[k].astype(F32) for k in range(2))
        inv_mult = 1.0 / mult

        C = jnp.where(rows == tm - 1, a_carry[0:1, :], pltpu.roll(a, tm - 1, axis=0))
        g_carry[0:1, :] = _linear_scan(g_scr, C, dhr, g_carry[0:1, :], reverse=True)
        a_carry[0:1, :] = a[0:1, :]
        g = g_scr[...]

        h_prev = jnp.where(rows == 0, hrh_ref[7:8, :] * has_prev, pltpu.roll(hr_t, 1, axis=0))
        da = g * h_prev
        gm = g * mult
        dmult = g * gi * v
        di = gm * v
        dv = gm * gi
        dlog_a = da * a - dmult * (a * a * inv_mult)
        dvec_ref[4:5, :] += jnp.sum(dlog_a * r, axis=0, keepdims=True)
        dra = (dlog_a * ((-LRU_C) * sp) * r * (1.0 - r))
        drx = di * gi * (1.0 - gi)
        dvec_ref[2:3, :] += jnp.sum(dra, axis=0, keepdims=True)
        dvec_ref[3:4, :] += jnp.sum(drx, axis=0, keepdims=True)
        drab = dra.astype(BF16)
        drxb = drx.astype(BF16)
        vb = v.astype(BF16)
        dvg = []
        for h in range(NH):
            sl = slice(h * HD, (h + 1) * HD)
            dvg.append(_dot_nt(drab[:, sl], wa_ref[h]) + _dot_nt(drxb[:, sl], wx_ref[h]))
            dmat_ref[MAT_WA + h * HD:MAT_WA + (h + 1) * HD, :] += _dot_tn(vb[:, sl], drab[:, sl])
            dmat_ref[MAT_WX + h * HD:MAT_WX + (h + 1) * HD, :] += _dot_tn(vb[:, sl], drxb[:, sl])
        dv = dv + jnp.concatenate(dvg, axis=1)
        dvec_ref[1:2, :] += jnp.sum(dv, axis=0, keepdims=True)
        dvext = jnp.concatenate([dv, dv_carry[...]], axis=0)
        dv_carry[...] = dv[0:CONV_HALO, :]
        n = tm + CONV_HALO
        u_rnn = proj_ref[:, DP:DP + DR]
        du_rnn = dv * cw_ref[3:4, :]
        dvec_ref[8:9, :] += jnp.sum(dv * u_rnn, axis=0, keepdims=True)
        for k in range(3):
            dv_k = pltpu.roll(dvext, n - (3 - k), axis=0)[0:tm, :]
            du_rnn = du_rnn + dv_k * cw_ref[k:k + 1, :]
            dvec_ref[5 + k:6 + k, :] += jnp.sum(dv_k * u_rnn, axis=0, keepdims=True)
        dproj_ref[:, DP:DP + DR] = du_rnn.astype(BF16)

        dpm = _dot_nt(dyp, wpo_ref[...])
        u_pool = proj_ref[:, 0:DP]
        ext = jnp.concatenate([projh_ref[:, 0:DP] * has_prev, u_pool], axis=0)
        sums = _pool_windows(ext, +1)
        scale_v = scale_ref[...]
        qs = []
        dpooled = []
        dscale = []
        for gi_, w in enumerate(WINDOWS):
            sl = slice(gi_ * PG, (gi_ + 1) * PG)
            inv_cnt = 1.0 / jnp.minimum(t_glob + 1, w).astype(F32)
            pooled_b = (sums[gi_][POOL_HALO:, :] * inv_cnt - u_pool[:, sl]).astype(BF16)
            mixed_g = _dot(pooled_b, wg_ref[gi_])
            dscale.append(jnp.sum(dpm[:, sl] * mixed_g, axis=0, keepdims=True))
            dmixed_b = (dpm[:, sl] * scale_v[:, sl]).astype(BF16)
            dmat_ref[gi_ * PG:(gi_ + 1) * PG, :] += _dot_tn(pooled_b, dmixed_b)
            dp_g = _dot_nt(dmixed_b, wg_ref[gi_])
            dpooled.append(dp_g)
            qs.append(dp_g * inv_cnt)
        dvec_ref[0:1, 0:DP] += jnp.concatenate(dscale, axis=1)
        q = jnp.concatenate(qs, axis=1)
        qext = jnp.concatenate([q, q_carry[...]], axis=0)
        q_carry[...] = q[0:POOL_HALO, :]
        tsum = _pool_windows(qext, -1)
        for gi_ in range(4):
            dproj_ref[:, gi_ * PG:(gi_ + 1) * PG] = (tsum[gi_][0:tm, :] - dpooled[gi_]).astype(BF16)

        @pl.when(i == nt - 1)
        def _():
            dvec_ref[4:5, :] = dvec_ref[4:5, :] * (LRU_C * _sigmoid(-lam_ref[...]))

    return _call(
        body, "mixer_bwd", (nt,),
        in_specs=[rev(DIN), halo(POOL_HALO, DIN), rev(D), rev(D), rev(D), rev(DR), halo(8, DR),
                  pl.BlockSpec((KEPT, tm, DR), lambda i: (0, nt - 1 - i, 0)),
                  pl.BlockSpec((2, tm, DR), lambda i: (0, nt - 1 - i, 0)), _resident((4, PG, PG)), _resident((1, DP)), _resident((DP, D)), _resident((4, DR)), _resident((1, DR)),
                  _resident((NH, HD, HD)), _resident((1, DR)), _resident((NH, HD, HD)), _resident((1, DR)),
                  _resident((1, DR)), _resident((DR, D))],
        out_specs=[rev(DIN), rev(D), rev(D), _resident((MAT_ROWS, HD)), _resident((VEC_ROWS, DR))],
        out_shape=[jax.ShapeDtypeStruct((S, DIN), BF16), jax.ShapeDtypeStruct((S, D), BF16),
                   jax.ShapeDtypeStruct((S, D), BF16), jax.ShapeDtypeStruct((MAT_ROWS, HD), F32),
                   jax.ShapeDtypeStruct((VEC_ROWS, DR), F32)],
        scratch_shapes=[pltpu.VMEM((POOL_HALO, DP), F32), pltpu.VMEM((CONV_HALO, DR), F32), pltpu.VMEM((8, DR), F32),
                        pltpu.VMEM((8, DR), F32), pltpu.VMEM((tm, DR), F32)],
        operands=(proj, proj, dmixo, y_pool, y_rnn, hr, hr, kept, gates, wg, scale, w_pool_out, conv_w, conv_b, wa, ba, wx, bx, lam,
                  w_rnn_out),
        exchange=exchange, exchange_operands=exchange_operands)


def _in_bwd(dproj, x, dx2, norm_mix, w_in, exchange=None, exchange_operands=(), tm=512):
    S = x.shape[0]

    def body(dp_ref, x_ref, dx2_ref, g_ref, w_ref, dx_ref, dg_ref):
        i = pl.program_id(0)

        @pl.when(i == 0)
        def _():
            dg_ref[...] = jnp.zeros_like(dg_ref)

        dh = _dot(dp_ref[:, 0:1536], w_ref[0:1536, :])
        dh = dh + _dot(dp_ref[:, 1536:3072], w_ref[1536:3072, :])
        dh = dh + _dot(dp_ref[:, 3072:DIN], w_ref[3072:DIN, :])
        xv = x_ref[...]
        r = lax.rsqrt(jnp.mean(xv * xv, axis=-1, keepdims=True) + EPS)
        xh = xv * r
        dg_ref[0:1, :] += jnp.sum(dh * xh, axis=0, keepdims=True)
        dxh = dh * g_ref[...]
        dx_ref[...] = dx2_ref[...] + r * (dxh - xh * jnp.mean(dxh * xh, axis=-1, keepdims=True))

    return _call(
        body, "in_bwd", (S // tm,),
        in_specs=[_rows(DIN, tm), _rows(D, tm), _rows(D, tm), _resident((1, D)), _resident((DIN, D))],
        out_specs=[_rows(D, tm), _resident((8, D))],
        out_shape=[jax.ShapeDtypeStruct((S, D), F32), jax.ShapeDtypeStruct((8, D), F32)],
        operands=(dproj, x, dx2, norm_mix, w_in), exchange=exchange, exchange_operands=exchange_operands)


def _wgrad(a, b, name, tk, tn, exchange=None, exchange_operands=()):
    S, K = a.shape
    N = b.shape[1]

    def body(a_ref, b_ref, o_ref):
        o_ref[...] = _dot_tn(a_ref[...], b_ref[...]).astype(BF16)

    (out,), exchanged = _call(
        body, name, (K // tk, N // tn),
        in_specs=[pl.BlockSpec((S, tk), lambda k, n: (0, k)), pl.BlockSpec((S, tn), lambda k, n: (0, n))],
        out_specs=[pl.BlockSpec((tk, tn), lambda k, n: (k, n))],
        out_shape=[jax.ShapeDtypeStruct((K, N), BF16)],
        operands=(a, b), exchange=exchange, exchange_operands=exchange_operands)
    return (out, exchanged) if exchange is not None else out


VEC_SCALE, VEC_CONV_B, VEC_BA, VEC_BX, VEC_LAM, VEC_CONV_W, VEC_NORM_FINAL, VEC_NORM_FFN = 0, 1, 2, 3, 4, 5, 9, 10
VEC_LOSS = 11


class _Big:
    def __init__(self, name, rows, cols, axis, n, dtype=BF16, transposed=False, src_cols=None):
        self.name, self.rows, self.cols, self.axis, self.n, self.dtype = name, rows, cols, axis, n, dtype
        self.transposed = transposed
        self.src_cols = src_cols
        self.block_shape = (rows, n) if axis == 1 else (n, cols)

    def block(self, ref, p):
        if self.axis == 1:
            return ref.at[:, pl.ds(pl.multiple_of(p * self.n, 128), self.n)]
        return ref.at[pl.ds(pl.multiple_of(p * self.n, 16 if self.dtype == BF16 else 8), self.n), :]


BIG = (_Big("w_in", DIN, D, 0, DIN // 8, transposed=True), _Big("w_pool_out", DP, D, 1, D // 8),
       _Big("w_rnn_out", DR, D, 0, DR // 8), _Big("w_o", D, D, 0, D // 8),
       _Big("w_ffn_in", 2 * DFF, D, 0, 2 * DFF // 8, transposed=True), _Big("w_ffn_out", DFF, D, 0, DFF // 8))
CONV_W = _Big("conv_w", 8, DR, 1, DR // 8, F32)
W_FFN_IN_HALVES = (_Big("w_ffn_in_lo", 2 * DFF, D // 2, 0, 2 * DFF // 8, src_cols=(0, D // 2)),
                   _Big("w_ffn_in_hi", 2 * DFF, D // 2, 0, 2 * DFF // 8, src_cols=(D // 2, D)))
GATHERED = BIG + (CONV_W,) + W_FFN_IN_HALVES

HBM_SPEC = pl.BlockSpec(memory_space=pl.ANY)
VMEM_SPEC = pl.BlockSpec(memory_space=pltpu.VMEM)


def _place():
    x, y, c = (lax.axis_index(a) for a in MESH_AXES)
    other_chips = [(1 - x, y), (x, 1 - y), (1 - x, 1 - y)]
    return x, y, c, other_chips


def _remote(src, dst, send_sems, recv_sems, idx, to):
    return pltpu.make_async_remote_copy(src_ref=src, dst_ref=dst, send_sem=send_sems.at[idx], recv_sem=recv_sems.at[idx],
                                        device_id=to, device_id_type=MESH)


def _device_index(chip, core):
    return 4 * chip[0] + 2 * chip[1] + core


class _Gather:
    def __init__(self, tensors):
        self.tensors = tuple(tensors)
        n = len(self.tensors)
        self.in_specs = [HBM_SPEC] * n
        self.out_specs = [HBM_SPEC] * n
        self.out_shape = [jax.ShapeDtypeStruct((T.rows, T.cols), T.dtype) for T in self.tensors]
        self.scratch_shapes = [pltpu.VMEM(T.block_shape, T.dtype) for T in self.tensors] + [
            pltpu.VMEM(T.block_shape, F32) for T in self.tensors] + [
            pltpu.SemaphoreType.DMA((n, 7)), pltpu.SemaphoreType.DMA((n, 7)), pltpu.SemaphoreType.DMA((n, 2))]

    def middles(self, steps):
        return [(steps // 2, self.relay), (steps - 1, self.middle)]

    def _copies(self, ins, outs, scratch):
        n = len(self.tensors)
        mine, raw, (send_sems, recv_sems, loc_sems) = scratch[:n], scratch[n:2 * n], scratch[2 * n:]
        x, y, c, chips = _place()
        sibling = (x, y, 1 - c)
        me = _device_index((x, y), c)
        relay_from = (jnp.where(c == 0, 1 - x, x), jnp.where(c == 0, y, 1 - y))
        relay_to = (jnp.where(c == 0, x, 1 - x), jnp.where(c == 0, 1 - y, y))
        loads, stores, first, relays, passed, arrivals, late = [], [], [], [], [], [], []
        for t, T in enumerate(self.tensors):
            place = T.block(outs[t], me)
            src = ins[t] if T.src_cols is None else ins[t].at[:, T.src_cols[0]:T.src_cols[1]]
            loads.append(pltpu.make_async_copy(src, raw[t], loc_sems.at[t, 0]))
            stores.append(pltpu.make_async_copy(mine[t], place, loc_sems.at[t, 1]))
            first.append(_remote(mine[t], place, send_sems, recv_sems, (t, 0), sibling))
            theirs = T.block(outs[t], _device_index((x, y), 1 - c))
            late.append(_remote(theirs, theirs, send_sems, recv_sems, (t, 0), sibling))
            relayed = T.block(outs[t], _device_index(relay_from, c))
            relays.append(_remote(relayed, relayed, send_sems, recv_sems, (t, 3), (*relay_to, c)))
            for k, chip in enumerate(chips):
                if k < 2:
                    first.append(_remote(mine[t], place, send_sems, recv_sems, (t, 1 + k), (*chip, c)))
                land = T.block(outs[t], _device_index(chip, c))
                arrivals.append(_remote(land, land, send_sems, recv_sems, (t, 1 + k), sibling))
                passed.append(_remote(land, land, send_sems, recv_sems, (t, 4 + k), sibling))
                theirs = T.block(outs[t], _device_index(chip, 1 - c))
                late.append(_remote(theirs, theirs, send_sems, recv_sems, (t, 4 + k), sibling))
        return loads, stores, first, relays, passed, arrivals, late

    def start(self, ins, outs, scratch):
        loads, stores, first, _, _, _, _ = self._copies(ins, outs, scratch)
        n = len(self.tensors)
        for cp in loads:
            cp.start()
        for t, cp in enumerate(loads):
            cp.wait()
            scratch[t][...] = scratch[n + t][...].astype(self.tensors[t].dtype)
        for cp in stores + first:
            cp.start()

    def relay(self, ins, outs, scratch, skip=0):
        _, _, _, relays, passed, arrivals, _ = self._copies(ins, outs, scratch)
        for t in range(skip, len(self.tensors)):
            arrivals[3 * t].wait_recv()
            arrivals[3 * t + 1].wait_recv()
            for cp in (relays[t], passed[3 * t], passed[3 * t + 1]):
                cp.start()

    def middle(self, ins, outs, scratch, skip=0):
        _, _, _, _, passed, arrivals, _ = self._copies(ins, outs, scratch)
        for t in range(skip, len(self.tensors)):
            arrivals[3 * t + 2].wait_recv()
            passed[3 * t + 2].start()

    def finish(self, ins, outs, scratch, skip=0):
        _, stores, first, relays, passed, _, late = self._copies(ins, outs, scratch)
        for cp in late[4 * skip:]:
            cp.wait_recv()
        for cp in first + relays + passed:
            cp.wait_send()
        for cp in stores[skip:]:
            cp.wait()


def _in_proj_gather(x, norm_mix, blocks, tensors, order, tm=512):
    S = x.shape[0]
    nt = S // tm
    n = len(tensors)
    gather = _Gather(tensors)
    CB = 2 * tensors[0].n

    def body(order_ref, x_ref, g_ref, *refs):
        ins, (proj_ref, h_ref), outs = refs[:n], refs[n:n + 2], refs[n + 2:2 * n + 2]
        (h_all, w_chip, w_sem), scratch = refs[2 * n + 2:2 * n + 5], refs[2 * n + 5:]
        q, i = pl.program_id(0), pl.program_id(1)
        _, stores, _, relays, passed, arrivals, late = gather._copies(ins, outs, scratch)

        def fetch(turn):
            rows = outs[0].at[pl.ds(pl.multiple_of(order_ref[turn] * CB, 16), CB), :]
            cp = pltpu.make_async_copy(rows, w_chip, w_sem)
            cp.start()
            cp.wait()

        @pl.when((q == 0) & (i == 0))
        def _():
            gather.start(ins, outs, scratch)
            late[0].wait_recv()
            stores[0].wait()
            fetch(0)

        @pl.when((q == 1) & (i == 0))
        def _():
            arrivals[0].wait_recv()
            arrivals[1].wait_recv()
            for cp in (relays[0], passed[0], passed[1]):
                cp.start()
            late[1].wait_recv()
            fetch(1)

        @pl.when((q == 2) & (i == 0))
        def _():
            late[2].wait_recv()
            fetch(2)
            gather.relay(ins, outs, scratch, skip=1)

        @pl.when((q == 3) & (i == 0))
        def _():
            arrivals[2].wait_recv()
            passed[2].start()
            late[3].wait_recv()
            fetch(3)

        rows = pl.ds(pl.multiple_of(i * tm, tm), tm)

        @pl.when(q == 0)
        def _():
            xv = x_ref[...]
            r = lax.rsqrt(jnp.mean(xv * xv, axis=-1, keepdims=True) + EPS)
            h = (xv * r * g_ref[...]).astype(BF16)
            h_all[rows, :] = h
            h_ref[...] = h

        proj_ref[...] = _dot_nt(h_all[rows, :], w_chip[...])

        @pl.when((q == 3) & (i == nt - 1))
        def _():
            gather.middle(ins, outs, scratch, skip=1)
            gather.finish(ins, outs, scratch, skip=1)

    row_tile = lambda q, i, order: (jnp.where(q == 0, i, nt - 1), 0)
    whole = lambda shape: pl.BlockSpec(shape, lambda q, i, order: (0,) * len(shape), pipeline_mode=pl.Buffered(1))
    outs = pl.pallas_call(
        body, name="in_proj_gather",
        grid_spec=pltpu.PrefetchScalarGridSpec(
            num_scalar_prefetch=1, grid=(4, nt),
            in_specs=[pl.BlockSpec((tm, D), row_tile), whole((1, D))] + gather.in_specs,
            out_specs=[pl.BlockSpec((tm, CB), lambda q, i, order: (i, order[q])), pl.BlockSpec((tm, D), row_tile)]
            + gather.out_specs,
            scratch_shapes=[pltpu.VMEM((S, D), BF16), pltpu.VMEM((CB, D), BF16), pltpu.SemaphoreType.DMA]
            + gather.scratch_shapes),
        out_shape=[jax.ShapeDtypeStruct((S, DIN), F32), jax.ShapeDtypeStruct((S, D), BF16)] + gather.out_shape,
        compiler_params=_params("arbitrary", "arbitrary"),
    )(order, x, norm_mix, *blocks)
    return outs[:2], outs[2:]


PAIR_ROWS = 32


def _pair_reduce(grads, tensors, name):
    nt = len(tensors)

    def body(*refs):
        ins, own_out, sums_out, landed, mine = (refs[k * nt:(k + 1) * nt] for k in range(5))
        send_sems, recv_sems, loc_sems = refs[5 * nt:]
        x, y, c, chips = _place()
        chip_of = [2 * chip[0] + chip[1] for chip in chips]
        swaps, loads = [], []
        for t, T in enumerate(tensors):
            for j in range(4):
                swaps.append(_remote(T.block(ins[t], 2 * j + 1 - c), landed[t].at[j], send_sems, recv_sems, (t, j),
                                     (x, y, 1 - c)))
            for k in range(3):
                loads.append(pltpu.make_async_copy(T.block(ins[t], 2 * chip_of[k] + c), mine[t].at[k], loc_sems.at[t, k]))
        for cp in swaps + loads:
            cp.start()
        for cp in loads:
            cp.wait()
        for cp in swaps:
            cp.wait_recv()
        stores = []
        for t, T in enumerate(tensors):
            for k in range(3):
                acc, got = mine[t].at[k], landed[t].at[chip_of[k]]

                def add(i, carry, acc=acc, got=got):
                    rows = pl.ds(pl.multiple_of(i * PAIR_ROWS, PAIR_ROWS), PAIR_ROWS)
                    acc[rows, :] = (acc[rows, :].astype(F32) + got[rows, :].astype(F32)).astype(BF16)
                    return carry

                lax.fori_loop(0, T.block_shape[0] // PAIR_ROWS, add, 0)
            stores.append(pltpu.make_async_copy(mine[t], sums_out[t], loc_sems.at[t, 3]))
            stores.append(pltpu.make_async_copy(landed[t].at[2 * x + y], own_out[t], loc_sems.at[t, 4]))
        for cp in stores:
            cp.start()
        for cp in swaps:
            cp.wait_send()
        for cp in stores:
            cp.wait()

    blocks = [T.block_shape for T in tensors]
    return pl.pallas_call(
        body, name=name,
        in_specs=[HBM_SPEC] * nt, out_specs=[HBM_SPEC] * (2 * nt),
        out_shape=[jax.ShapeDtypeStruct(b, BF16) for b in blocks] + [jax.ShapeDtypeStruct((3,) + b, BF16) for b in blocks],
        scratch_shapes=[pltpu.VMEM((4,) + b, BF16) for b in blocks] + [pltpu.VMEM((3,) + b, BF16) for b in blocks]
        + [pltpu.SemaphoreType.DMA((nt, 4)), pltpu.SemaphoreType.DMA((nt, 4)), pltpu.SemaphoreType.DMA((nt, 5))],
        compiler_params=pltpu.CompilerParams(vmem_limit_bytes=VMEM_LIMIT),
    )(*grads)


class _Scatter:
    def middles(self, steps):
        return []

    def __init__(self, tensors):
        n = len(tensors)
        self.in_specs = [HBM_SPEC] * n
        self.out_specs = [HBM_SPEC] * n
        self.out_shape = [jax.ShapeDtypeStruct((3,) + T.block_shape, BF16) for T in tensors]
        self.scratch_shapes = [pltpu.SemaphoreType.DMA((n, 3)), pltpu.SemaphoreType.DMA((n, 3))]

    def _copies(self, ins, outs, scratch):
        send_sems, recv_sems = scratch
        x, y, c, chips = _place()
        return [_remote(ins[t].at[k], outs[t].at[k], send_sems, recv_sems, (t, k), (*chip, c))
                for t in range(len(ins)) for k, chip in enumerate(chips)]

    def start(self, ins, outs, scratch):
        for cp in self._copies(ins, outs, scratch):
            cp.start()

    def finish(self, ins, outs, scratch):
        for cp in self._copies(ins, outs, scratch):
            cp.wait()


def _adamw(w, g, m, v):
    m = ADAM_B1 * m + (1.0 - ADAM_B1) * g
    v = ADAM_B2 * v + (1.0 - ADAM_B2) * (g * g)
    m_hat = m / (1.0 - ADAM_B1 ** ADAM_STEP)
    v_hat = v / (1.0 - ADAM_B2 ** ADAM_STEP)
    delta = -ADAM_LR * (m_hat / (jnp.sqrt(v_hat) + ADAM_EPS) + ADAM_WD * w)
    return delta, m, v


def _final_sum(T, g, lz1, lz2, where, w, m, v):
    rows, cols = T.block_shape
    sub = 4 if T.axis == 0 and rows % 64 == 0 and rows > 256 else 1
    blk = (rows // sub, cols)

    def body(where_ref, g_ref, l1_ref, l2_ref, w_ref, m_ref, v_ref, g_out, d_out, m_out, v_out):
        tot = g_ref[...].astype(F32) + l1_ref[...].astype(F32)
        for k in range(3):
            tot = tot + l2_ref[k].astype(F32)
        g_out[...] = tot
        d_out[...], m_out[...], v_out[...] = _adamw(w_ref[...], tot, m_ref[...], v_ref[...])

    def in_whole(r, wh):
        p = wh[0]
        return (0, p) if T.axis == 1 else (p * sub + r, 0)

    own = pl.BlockSpec(blk, lambda r, wh: (r, 0))
    return pl.pallas_call(
        body, name="grad_final_" + T.name,
        grid_spec=pltpu.PrefetchScalarGridSpec(
            num_scalar_prefetch=1, grid=(sub,),
            in_specs=[pl.BlockSpec(blk, in_whole),
                      own,
                      pl.BlockSpec((3,) + blk, lambda r, wh: (0, r, 0)), own, own, own],
            out_specs=[own] * 4),
        out_shape=[jax.ShapeDtypeStruct(T.block_shape, F32)] * 4,
        compiler_params=_params("arbitrary"),
    )(where, g, lz1, lz2, w, m, v)


VEC_PIECE = DR // 8


class _AllReduce:
    def __init__(self, items):
        self.items = tuple(items)
        n = len(self.items)
        self.in_specs = [HBM_SPEC] * n
        self.out_specs = [HBM_SPEC] * n
        self.out_shape = [jax.ShapeDtypeStruct(shape, F32) for shape, _ in self.items]
        pieces = [(shape[0] // 8, shape[1]) if axis == 0 else (shape[0], shape[1] // 8) for shape, axis in self.items]
        self.scratch_shapes = ([pltpu.VMEM((8,) + p, F32) for p in pieces] + [pltpu.VMEM(p, F32) for p in pieces] + [
            pltpu.SemaphoreType.DMA((2 * n, 8)), pltpu.SemaphoreType.DMA((2 * n, 8)), pltpu.SemaphoreType.DMA((2 * n,))])

    def middles(self, steps):
        return [(steps // 2, self.middle)]

    def _copies(self, ins, outs, scratch):
        n = len(self.items)
        landed, sums, (send_sems, recv_sems, loc_sems) = scratch[:n], scratch[n:2 * n], scratch[2 * n:]
        x, y, c, _ = _place()
        me = _device_index((x, y), c)

        def peer(r):
            return (1 - x if r & 4 else x, 1 - y if r & 2 else y, 1 - c if r & 1 else c)

        def piece(i, ref, p):
            shape, axis = self.items[i]
            if axis == 0:
                rows = shape[0] // 8
                return ref.at[pl.ds(pl.multiple_of(p * rows, 8), rows), :]
            cols = shape[1] // 8
            return ref.at[:, pl.ds(pl.multiple_of(p * cols, 128), cols)]

        own, scatter, arrivals, keep, spread, late = [], [], [], [], [], []
        for i in range(n):
            own.append(pltpu.make_async_copy(piece(i, ins[i], me), landed[i].at[0], loc_sems.at[2 * i]))
            keep.append(pltpu.make_async_copy(sums[i], piece(i, outs[i], me), loc_sems.at[2 * i + 1]))
            for r in range(1, 8):
                to = peer(r)
                p = _device_index(to[:2], to[2])
                scatter.append(_remote(piece(i, ins[i], p), landed[i].at[r], send_sems, recv_sems, (2 * i, r), to))
                spread.append(_remote(sums[i], piece(i, outs[i], me), send_sems, recv_sems, (2 * i + 1, r), to))
                late.append(_remote(sums[i], piece(i, outs[i], p), send_sems, recv_sems, (2 * i + 1, r), to))
        return own, scatter, keep, spread, late, landed, sums

    def start(self, ins, outs, scratch):
        own, scatter, _, _, _, _, _ = self._copies(ins, outs, scratch)
        for cp in own + scatter:
            cp.start()

    def middle(self, ins, outs, scratch):
        own, scatter, keep, spread, _, landed, sums = self._copies(ins, outs, scratch)
        for cp in own:
            cp.wait()
        for cp in scatter:
            cp.wait_recv()
        for i in range(len(self.items)):
            total = landed[i][0]
            for r in range(1, 8):
                total = total + landed[i][r]
            sums[i][...] = total
        for cp in keep + spread:
            cp.start()

    def finish(self, ins, outs, scratch):
        _, scatter, keep, spread, late, _, _ = self._copies(ins, outs, scratch)
        for cp in late:
            cp.wait_recv()
        for cp in scatter + spread:
            cp.wait_send()
        for cp in keep:
            cp.wait()


class _Both:
    def __init__(self, a, b):
        self.a, self.b = a, b
        self.in_specs, self.out_specs = a.in_specs + b.in_specs, a.out_specs + b.out_specs
        self.out_shape, self.scratch_shapes = a.out_shape + b.out_shape, a.scratch_shapes + b.scratch_shapes

    def _each(self, ins, outs, scratch):
        a = self.a
        i, o, s = len(a.in_specs), len(a.out_specs), len(a.scratch_shapes)
        return (a, ins[:i], outs[:o], scratch[:s]), (self.b, ins[i:], outs[o:], scratch[s:])

    def middles(self, steps):
        def of(which, middle):
            return lambda ins, outs, scratch: middle(*self._each(ins, outs, scratch)[which][1:])
        return [(at, of(which, middle)) for which, e in enumerate((self.a, self.b)) for at, middle in e.middles(steps)]

    def start(self, ins, outs, scratch):
        for e, i, o, s in self._each(ins, outs, scratch):
            e.start(i, o, s)

    def finish(self, ins, outs, scratch):
        for e, i, o, s in self._each(ins, outs, scratch):
            e.finish(i, o, s)


def _all_reduce(arrays, items, name):
    reduce = _AllReduce(items)
    n = len(items)

    def body(*refs):
        ins, outs, scratch = refs[:n], refs[n:2 * n], refs[2 * n:]
        reduce.start(ins, outs, scratch)
        reduce.middle(ins, outs, scratch)
        reduce.finish(ins, outs, scratch)

    return pl.pallas_call(
        body, name=name, in_specs=reduce.in_specs, out_specs=reduce.out_specs, out_shape=reduce.out_shape,
        scratch_shapes=reduce.scratch_shapes,
    )(*arrays)


def _adam_small(grads, wmv):
    n = len(grads)

    def body(*refs):
        g_refs, rest = refs[:n], refs[n:]
        ins, outs = rest[:3 * n], rest[3 * n:]
        for i in range(n):
            d, m, v = _adamw(ins[3 * i][...], g_refs[i][...], ins[3 * i + 1][...], ins[3 * i + 2][...])
            outs[3 * i][...], outs[3 * i + 1][...], outs[3 * i + 2][...] = d, m, v

    flat = [a for t in wmv for a in t]
    return pl.pallas_call(
        body, name="adam_small",
        in_specs=[VMEM_SPEC] * (4 * n), out_specs=[VMEM_SPEC] * (3 * n),
        out_shape=[jax.ShapeDtypeStruct(a.shape, F32) for a in flat],
    )(*grads, *flat)


WEIGHT_NAMES = ("norm_mix", "w_in", "w_pool_grp", "pool_scale", "w_pool_out", "conv_w", "conv_b", "w_rg_a", "b_rg_a", "w_rg_x",
                "b_rg_x", "lru_lambda", "w_rnn_out", "w_o", "norm_ffn", "w_ffn_in", "w_ffn_out", "norm_final")


def kernel(x, norm_mix, w_in, w_pool_grp, pool_scale, w_pool_out, conv_w, conv_b, w_rg_a, b_rg_a, w_rg_x, b_rg_x, lru_lambda, w_rnn_out, w_o, norm_ffn, w_ffn_in, w_ffn_out, norm_final, loss_target, m_norm_mix, m_w_in, m_w_pool_grp, m_pool_scale, m_w_pool_out, m_conv_w, m_conv_b, m_w_rg_a, m_b_rg_a, m_w_rg_x, m_b_rg_x, m_lru_lambda, m_w_rnn_out, m_w_o, m_norm_ffn, m_w_ffn_in, m_w_ffn_out, m_norm_final, v_norm_mix, v_w_in, v_w_pool_grp, v_pool_scale, v_w_pool_out, v_conv_w, v_conv_b, v_w_rg_a, v_b_rg_a, v_w_rg_x, v_b_rg_x, v_lru_lambda, v_w_rnn_out, v_w_o, v_norm_ffn, v_w_ffn_in, v_w_ffn_out, v_norm_final):
    w = dict(norm_mix=norm_mix, w_in=w_in, w_pool_grp=w_pool_grp, pool_scale=pool_scale, w_pool_out=w_pool_out, conv_w=conv_w,
             conv_b=conv_b, w_rg_a=w_rg_a, b_rg_a=b_rg_a, w_rg_x=w_rg_x, b_rg_x=b_rg_x, lru_lambda=lru_lambda,
             w_rnn_out=w_rnn_out, w_o=w_o, norm_ffn=norm_ffn, w_ffn_in=w_ffn_in, w_ffn_out=w_ffn_out, norm_final=norm_final)
    m = dict(norm_mix=m_norm_mix, w_in=m_w_in, w_pool_grp=m_w_pool_grp, pool_scale=m_pool_scale, w_pool_out=m_w_pool_out,
             conv_w=m_conv_w, conv_b=m_conv_b, w_rg_a=m_w_rg_a, b_rg_a=m_b_rg_a, w_rg_x=m_w_rg_x, b_rg_x=m_b_rg_x,
             lru_lambda=m_lru_lambda, w_rnn_out=m_w_rnn_out, w_o=m_w_o, norm_ffn=m_norm_ffn, w_ffn_in=m_w_ffn_in,
             w_ffn_out=m_w_ffn_out, norm_final=m_norm_final)
    v = dict(norm_mix=v_norm_mix, w_in=v_w_in, w_pool_grp=v_w_pool_grp, pool_scale=v_pool_scale, w_pool_out=v_w_pool_out,
             conv_w=v_conv_w, conv_b=v_conv_b, w_rg_a=v_w_rg_a, b_rg_a=v_b_rg_a, w_rg_x=v_w_rg_x, b_rg_x=v_b_rg_x,
             lru_lambda=v_lru_lambda, w_rnn_out=v_w_rnn_out, w_o=v_w_o, norm_ffn=v_norm_ffn, w_ffn_in=v_w_ffn_in,
             w_ffn_out=v_w_ffn_out, norm_final=v_norm_final)
    xi, yi, ci = (lax.axis_index(a) for a in MESH_AXES)
    chip = 2 * xi + yi

    def held(T, a):
        return jnp.swapaxes(a, 0, 1) if T.transposed else a

    where = jnp.stack([2 * chip + ci]).astype(jnp.int32)
    by_name = {T.name: T for T in GATHERED}
    block = {T.name: held(T, w[T.name][0]) for T in BIG}
    block["conv_w"] = jnp.pad(conv_w[0], ((0, CONV_W.rows - 4), (0, 0)))
    block["w_ffn_in_lo"] = block["w_ffn_in_hi"] = block["w_ffn_in"]

    def gather_of(*names):
        return dict(exchange=_Gather([by_name[n] for n in names]), exchange_operands=[block[n] for n in names])

    def pair_sums(names, partials, tag):
        out = _pair_reduce(partials, [by_name[n] for n in names], "grad_pair_reduce_" + tag)
        return list(out[:len(names)]), list(out[len(names):])

    xs, target = x[0], loss_target[0]
    wg_b, wa_b, wx_b = (a[0].astype(BF16) for a in (w_pool_grp, w_rg_a, w_rg_x))
    ba2, bx2 = b_rg_a.reshape(1, DR), b_rg_x.reshape(1, DR)
    first = ("w_in", "w_pool_out", "w_rnn_out", "conv_w", "w_o")
    order = jnp.stack([chip, 2 * (1 - xi) + yi, 2 * xi + (1 - yi), 2 * (1 - xi) + (1 - yi)]).astype(jnp.int32)
    (proj, h1), (w_in_g, w_pool_out_g, w_rnn_out_g, conv_g, w_o_g) = _in_proj_gather(
        xs, norm_mix, [block[n] for n in first], [by_name[n] for n in first], order)
    mixer_weights = (wg_b, pool_scale, w_pool_out_g, conv_g[0:4], conv_b, wa_b, ba2, wx_b, bx2, lru_lambda, w_rnn_out_g)
    (pm, y_pool, hr, z, y_rnn, kept, gates), (w_ffn_lo_g, w_ffn_hi_g) = _mixer_fwd(
        proj, *mixer_weights, **gather_of("w_ffn_in_lo", "w_ffn_in_hi"))
    (mix, x2, h2), _ = _merge_out(xs, proj, y_pool, y_rnn, w_o_g, norm_ffn)
    (gu, act), (w_ffn_out_g,) = _ffn_up(h2, w_ffn_lo_g, w_ffn_hi_g, **gather_of("w_ffn_out"))
    dx3, dx3b, loss_part, dvec_fin = _ffn_down_loss(act, x2, target, w_ffn_out_g, norm_final.reshape(1, D))

    dgu = _ffn_bwd_down(dx3b, gu, w_ffn_out_g)
    dx2, dx2b, dmixo, dvec_ffn = _ffn_bwd_up(dgu, x2, dx3, w_ffn_lo_g, w_ffn_hi_g, norm_ffn, w_o_g)
    names_a = ("w_ffn_in", "w_ffn_out", "w_o")
    part_a = [_wgrad(dgu, h2, "wgrad_ffn_in", 1408, 512), _wgrad(act, dx3b, "wgrad_ffn_out", 1408, 512),
              _wgrad(mix, dx2b, "wgrad_o", 1024, 256)]
    lz1_a, sums_a = pair_sums(names_a, part_a, "ffn")
    (dproj, dypb, dyrb, dmat, dvec_mix), lz2_a = _mixer_bwd(
        proj, dmixo, y_pool, y_rnn, hr, kept, gates, *mixer_weights,
        exchange=_Scatter([by_name[n] for n in names_a]), exchange_operands=sums_a)
    names_b = ("w_pool_out", "w_rnn_out")
    part_b = [_wgrad(pm, dypb, "wgrad_pool_out", 512, 256), _wgrad(z, dyrb, "wgrad_rnn_out", 1024, 256)]
    lz1_b, sums_b = pair_sums(names_b, part_b, "mix")
    dvec = jnp.concatenate([dvec_mix[0:9], dvec_fin[0:1], dvec_ffn[0:1], jnp.pad(loss_part, ((0, 0), (0, DR - 1))),
                            jnp.zeros((VEC_ROWS - 12, DR), F32)], axis=0)
    g_in, exchanged = _wgrad(
        dproj, h1, "wgrad_in", 1152, 1024,
        exchange=_Both(_Scatter([by_name[n] for n in names_b]), _AllReduce([((MAT_ROWS, HD), 0), ((VEC_ROWS, DR), 1)])),
        exchange_operands=sums_b + [dmat, dvec])
    lz2_b, (mat, vec) = exchanged[:2], exchanged[2:]
    loss = vec[VEC_LOSS, 0]
    lz1_c, sums_c = pair_sums(("w_in",), [g_in], "in")
    (grad_x, dvec_in), lz2_c = _in_bwd(dproj, xs, dx2, norm_mix, w_in_g,
                                       exchange=_Scatter([by_name["w_in"]]), exchange_operands=sums_c)
    (vec_in,) = _all_reduce([dvec_in], [((8, D), 1)], "all_reduce_norm_mix")

    grads, delta, new_m, new_v = {}, {}, {}, {}
    for n, g, l1, l2 in zip(names_a + names_b + ("w_in",), part_a + part_b + [g_in], lz1_a + lz1_b + lz1_c,
                            lz2_a + lz2_b + lz2_c):
        T = by_name[n]
        out = _final_sum(T, g, l1, l2, where, held(T, w[n][0]), held(T, m[n][0]), held(T, v[n][0]))
        grads[n], delta[n], new_m[n], new_v[n] = (held(T, a) for a in out)
    me = 4 * xi + 2 * yi + ci
    small_grads = dict(
        w_pool_grp=mat[0:MAT_WA], w_rg_a=mat[MAT_WA:MAT_WX], w_rg_x=mat[MAT_WX:MAT_ROWS],
        pool_scale=vec[VEC_SCALE:VEC_SCALE + 1, 0:DP], conv_b=vec[VEC_CONV_B:VEC_CONV_B + 1],
        b_rg_a=vec[VEC_BA:VEC_BA + 1], b_rg_x=vec[VEC_BX:VEC_BX + 1], lru_lambda=vec[VEC_LAM:VEC_LAM + 1],
        conv_w=lax.dynamic_slice(vec, (VEC_CONV_W, VEC_PIECE * me), (4, VEC_PIECE)),
        norm_final=vec[VEC_NORM_FINAL:VEC_NORM_FINAL + 1], norm_ffn=vec[VEC_NORM_FFN:VEC_NORM_FFN + 1],
        norm_mix=vec_in[0:1])
    names = list(small_grads)
    as2d = lambda a, g: a.reshape(g.shape)
    upd = _adam_small([small_grads[n] for n in names],
                      [(as2d(w[n], small_grads[n]), as2d(m[n], small_grads[n]), as2d(v[n], small_grads[n])) for n in names])
    for i, n in enumerate(names):
        grads[n] = small_grads[n]
        delta[n], new_m[n], new_v[n] = upd[3 * i:3 * i + 3]

    shaped = lambda d: [d[n].reshape(w[n].shape) for n in WEIGHT_NAMES]
    return (loss, grad_x[None], *shaped(grads), *shaped(delta), *shaped(new_m), *shaped(new_v))
```

```python
import math

import jax
import jax.numpy as jnp
from jax import lax
from jax.experimental import pallas as pl
from jax.experimental.pallas import tpu as pltpu

F32 = jnp.float32
BF16 = jnp.bfloat16

D = 1024
DP = 512
PG = 128
WINDOWS = (2, 4, 8, 16)
DR = 1024
NH = 8
HD = 128
DIN = 4608
DFF = 2816
EPS = 1e-6
LRU_C = 8.0
POOL_HALO = 16
CONV_HALO = 8
KEPT = 3

ADAM_LR = 0.001
ADAM_B1 = 0.9
ADAM_B2 = 0.999
ADAM_EPS = 1e-08
ADAM_WD = 0.01
ADAM_STEP = 10

VMEM_LIMIT = 56 * 1024 * 1024
MESH_AXES = ("x", "y", "c")
MESH = pl.DeviceIdType.MESH


def _dot(a, b):
    return jnp.dot(a, b, preferred_element_type=F32)


def _dot_nt(a, b):
    return lax.dot_general(a, b, (((1,), (1,)), ((), ())), preferred_element_type=F32)


def _dot_tn(a, b):
    return lax.dot_general(a, b, (((0,), (0,)), ((), ())), preferred_element_type=F32)


def _params(*sem):
    return pltpu.CompilerParams(dimension_semantics=sem, vmem_limit_bytes=VMEM_LIMIT)


def _resident(shape):
    nd = len(shape)
    return pl.BlockSpec(shape, lambda i: (0,) * nd, pipeline_mode=pl.Buffered(1))


def _rows(shape_cols, tm):
    return pl.BlockSpec((tm, shape_cols), lambda i: (i, 0))


def _call(body, name, grid, in_specs, out_specs, out_shape, operands, scratch_shapes=(), exchange=None, exchange_operands=()):
    n_in, n_out, n_scr = len(in_specs), len(out_specs), len(scratch_shapes)
    steps = math.prod(grid)
    if exchange is None:
        outs = pl.pallas_call(body, name=name, grid=grid, in_specs=in_specs, out_specs=out_specs, out_shape=out_shape,
                              scratch_shapes=list(scratch_shapes), compiler_params=_params(*["arbitrary"] * len(grid)))(*operands)
        return outs, []
    e_in, e_out = len(exchange.in_specs), len(exchange.out_specs)

    def hosted(*refs):
        ins, refs = refs[:n_in], refs[n_in:]
        e_ins, refs = refs[:e_in], refs[e_in:]
        outs, refs = refs[:n_out], refs[n_out:]
        e_outs, refs = refs[:e_out], refs[e_out:]
        scr, e_scr = refs[:n_scr], refs[n_scr:]
        step = pl.program_id(0)
        for axis in range(1, len(grid)):
            step = step * grid[axis] + pl.program_id(axis)
        pl.when(step == 0)(lambda: exchange.start(e_ins, e_outs, e_scr))
        for at, middle in exchange.middles(steps):
            pl.when(step == at)(lambda middle=middle: middle(e_ins, e_outs, e_scr))
        body(*ins, *outs, *scr)
        pl.when(step == steps - 1)(lambda: exchange.finish(e_ins, e_outs, e_scr))

    outs = pl.pallas_call(
        hosted, name=name, grid=grid, in_specs=list(in_specs) + exchange.in_specs,
        out_specs=list(out_specs) + exchange.out_specs, out_shape=list(out_shape) + exchange.out_shape,
        scratch_shapes=list(scratch_shapes) + exchange.scratch_shapes,
        compiler_params=_params(*["arbitrary"] * len(grid)))(*operands, *exchange_operands)
    return outs[:n_out], outs[n_out:]


GELU_C = math.sqrt(2.0 / math.pi)
GELU_K = 0.044715 * GELU_C


def _gelu(x, with_grad=False):
    x2 = x * x
    t = jnp.tanh(x * (GELU_C + GELU_K * x2))
    hx = 0.5 * x
    y = hx + hx * t
    if not with_grad:
        return y
    return y, 0.5 + 0.5 * t + hx * (1.0 - t * t) * (GELU_C + (3.0 * GELU_K) * x2)


def _softplus_neg(lam):
    z = jnp.exp(-jnp.abs(lam))
    u = 1.0 + z
    dlt = u - 1.0
    log1p = jnp.where(dlt == 0.0, z, jnp.log(u) * (z / jnp.where(dlt == 0.0, 1.0, dlt)))
    return jnp.maximum(-lam, 0.0) + log1p


def _sigmoid(x):
    return 0.5 * jnp.tanh(0.5 * x) + 0.5


def _linear_scan(out_ref, A, B, h0, reverse):
    n = A.shape[0]
    sub = lax.broadcasted_iota(jnp.int32, (8, 1), 0)
    tiles = range(n // 8 - 1, -1, -1) if reverse else range(n // 8)
    carry = h0
    for j in tiles:
        a, b = A[8 * j:8 * j + 8, :], B[8 * j:8 * j + 8, :]
        for d in (1, 2, 4):
            keep = (sub < 8 - d) if reverse else (sub >= d)
            shift = 8 - d if reverse else d
            b = jnp.where(keep, a * pltpu.roll(b, shift, axis=0) + b, b)
            a = jnp.where(keep, a * pltpu.roll(a, shift, axis=0), a)
        h = a * carry + b
        out_ref[8 * j:8 * j + 8, :] = h
        carry = h[0:1, :] if reverse else h[7:8, :]
    return carry


def _pool_windows(ext, shift_sign):
    n = ext.shape[0]
    s = ext
    outs = []
    for w in WINDOWS:
        d = w // 2
        s = s + pltpu.roll(s, d if shift_sign > 0 else n - d, axis=0)
        outs.append(s[:, :PG])
        s = s[:, PG:]
    return outs


def _conv_taps(uext):
    taps = []
    for k in range(4):
        sh = 3 - k
        v = uext if sh == 0 else pltpu.roll(uext, sh, axis=0)
        taps.append(v[CONV_HALO:, :])
    return taps


def _gates(v, wa_ref, ba_ref, wx_ref, bx_ref, sp):
    vb = v.astype(BF16)
    ra, rx = [], []
    for h in range(NH):
        vh = vb[:, h * HD:(h + 1) * HD]
        ra.append(_dot(vh, wa_ref[h]))
        rx.append(_dot(vh, wx_ref[h]))
    r = _sigmoid(jnp.concatenate(ra, axis=1) + ba_ref[...])
    i = _sigmoid(jnp.concatenate(rx, axis=1) + bx_ref[...])
    log_a = r * ((-LRU_C) * sp)
    a = jnp.exp(log_a)
    one_minus = -jnp.tanh(log_a) * (1.0 + a * a)
    return r, i, a, jnp.sqrt(one_minus), lax.rsqrt(one_minus)


def _mixer_fwd(proj, wg, scale, w_pool_out, conv_w, conv_b, wa, ba, wx, bx, lam, w_rnn_out, exchange=None,
               exchange_operands=(), tm=256):
    S = proj.shape[0]
    UW = DP + 2 * DR

    def body(proj_ref, wg_ref, scale_ref, wpo_ref, cw_ref, cb_ref, wa_ref, ba_ref, wx_ref, bx_ref, lam_ref, wro_ref,
             pm_ref, ypool_ref, hr_ref, z_ref, yrnn_ref, kept_ref, gates_ref, pool_carry, conv_carry, h_carry):
        i = pl.program_id(0)

        @pl.when(i == 0)
        def _():
            pool_carry[...] = jnp.zeros_like(pool_carry)
            conv_carry[...] = jnp.zeros_like(conv_carry)
            h_carry[...] = jnp.zeros_like(h_carry)

        rows = lax.broadcasted_iota(jnp.int32, (tm, 1), 0)
        t_glob = i * tm + rows

        u_pool = proj_ref[:, 0:DP]
        ext = jnp.concatenate([pool_carry[...], u_pool], axis=0)
        pool_carry[...] = u_pool[tm - POOL_HALO:, :]
        sums = _pool_windows(ext, +1)
        mixed = []
        for g, w in enumerate(WINDOWS):
            inv_cnt = 1.0 / jnp.minimum(t_glob + 1, w).astype(F32)
            pooled_g = sums[g][POOL_HALO:, :] * inv_cnt - u_pool[:, g * PG:(g + 1) * PG]
            mixed.append(_dot(pooled_g.astype(BF16), wg_ref[g]))
        pm = (jnp.concatenate(mixed, axis=1) * scale_ref[...]).astype(BF16)
        pm_ref[...] = pm
        ypool_ref[...] = _dot(pm, wpo_ref[...]).astype(BF16)

        u_rnn = proj_ref[:, DP:DP + DR]
        uext = jnp.concatenate([conv_carry[...], u_rnn], axis=0)
        conv_carry[...] = u_rnn[tm - CONV_HALO:, :]
        taps = _conv_taps(uext)
        v = cb_ref[...]
        for k in range(4):
            v = v + taps[k] * cw_ref[k:k + 1, :]
        sp = _softplus_neg(lam_ref[...])
        r, gi, a, mult, _ = _gates(v, wa_ref, ba_ref, wx_ref, bx_ref, sp)
        for k, kept in enumerate((v, a, mult)):
            kept_ref[k] = kept
        for k, kept in enumerate((r, gi)):
            gates_ref[k] = kept.astype(BF16)
        h_carry[0:1, :] = _linear_scan(hr_ref, a, mult * gi * v, h_carry[0:1, :], reverse=False)
        z = (hr_ref[...] * _gelu(proj_ref[:, DP + DR:UW])).astype(BF16)
        z_ref[...] = z
        yrnn_ref[...] = _dot(z, wro_ref[...]).astype(BF16)

    return _call(
        body, "mixer_fwd", (S // tm,),
        in_specs=[_rows(UW, tm), _resident((4, PG, PG)), _resident((1, DP)), _resident((DP, D)), _resident((4, DR)),
                  _resident((1, DR)), _resident((NH, HD, HD)), _resident((1, DR)), _resident((NH, HD, HD)),
                  _resident((1, DR)), _resident((1, DR)), _resident((DR, D))],
        out_specs=[_rows(DP, tm), _rows(D, tm), _rows(DR, tm), _rows(DR, tm), _rows(D, tm),
                   pl.BlockSpec((KEPT, tm, DR), lambda i: (0, i, 0)), pl.BlockSpec((2, tm, DR), lambda i: (0, i, 0))],
        out_shape=[jax.ShapeDtypeStruct((S, DP), BF16),
                   jax.ShapeDtypeStruct((S, D), BF16), jax.ShapeDtypeStruct((S, DR), F32),
                   jax.ShapeDtypeStruct((S, DR), BF16), jax.ShapeDtypeStruct((S, D), BF16),
                   jax.ShapeDtypeStruct((KEPT, S, DR), F32), jax.ShapeDtypeStruct((2, S, DR), BF16)],
        scratch_shapes=[pltpu.VMEM((POOL_HALO, DP), F32), pltpu.VMEM((CONV_HALO, DR), F32), pltpu.VMEM((8, DR), F32)],
        operands=(proj, wg, scale, w_pool_out, conv_w, conv_b, wa, ba, wx, bx, lam, w_rnn_out),
        exchange=exchange, exchange_operands=exchange_operands)


FF_CHUNKS = ((0, 768), (768, 1536), (1536, 2304), (2304, DFF))


def _rms(x):
    r = lax.rsqrt(jnp.mean(x * x, axis=-1, keepdims=True) + EPS)
    return r, x * r


def _rms_bwd(dh, g, r, xh):
    dxh = dh * g
    return r * (dxh - xh * jnp.mean(dxh * xh, axis=-1, keepdims=True))


def _merge_out(x, proj, y_pool, y_rnn, w_o, norm_ffn, exchange=None, exchange_operands=(), tm=512):
    S = x.shape[0]
    GL0 = (DP + 2 * DR) // 512

    def gl_spec(k):
        return pl.BlockSpec((tm, 512), lambda i: (i, GL0 + k))

    def body(x_ref, gl0, gl1, gl2, gl3, yp_ref, yr_ref, wo_ref, gf_ref, mix_ref, x2_ref, h2_ref):
        s_p = _sigmoid(jnp.concatenate([gl0[...], gl1[...]], axis=1))
        s_r = _sigmoid(jnp.concatenate([gl2[...], gl3[...]], axis=1))
        mix = (s_p * yp_ref[...].astype(F32) + s_r * yr_ref[...].astype(F32)).astype(BF16)
        mix_ref[...] = mix
        x2 = x_ref[...] + _dot(mix, wo_ref[...])
        x2_ref[...] = x2
        _, xh2 = _rms(x2)
        h2_ref[...] = (xh2 * gf_ref[...]).astype(BF16)

    return _call(
        body, "merge_out", (S // tm,),
        in_specs=[_rows(D, tm), gl_spec(0), gl_spec(1), gl_spec(2), gl_spec(3), _rows(D, tm), _rows(D, tm),
                  _resident((D, D)), _resident((1, D))],
        out_specs=[_rows(D, tm), _rows(D, tm), _rows(D, tm)],
        out_shape=[jax.ShapeDtypeStruct((S, D), BF16), jax.ShapeDtypeStruct((S, D), F32), jax.ShapeDtypeStruct((S, D), BF16)],
        operands=(x, proj, proj, proj, proj, y_pool, y_rnn, w_o, norm_ffn),
        exchange=exchange, exchange_operands=exchange_operands)


def _ffn_up(h2, w_lo, w_hi, exchange=None, exchange_operands=(), tm=512):
    S = h2.shape[0]
    HALF = D // 2

    def body(h_ref, lo_ref, hi_ref, back_ref, act_ref):
        h_lo, h_hi = h_ref[:, 0:HALF], h_ref[:, HALF:D]
        for c0, c1 in FF_CHUNKS:
            gate = _dot_nt(h_lo, lo_ref[c0:c1, :]) + _dot_nt(h_hi, hi_ref[c0:c1, :])
            up = _dot_nt(h_lo, lo_ref[DFF + c0:DFF + c1, :]) + _dot_nt(h_hi, hi_ref[DFF + c0:DFF + c1, :])
            sg = _sigmoid(gate)
            silu = gate * sg
            back_ref[:, c0:c1] = (up * (sg * (1.0 + gate * (1.0 - sg)))).astype(BF16)
            back_ref[:, DFF + c0:DFF + c1] = silu.astype(BF16)
            act_ref[:, c0:c1] = (silu * up).astype(BF16)

    return _call(
        body, "ffn_up", (S // tm,),
        in_specs=[_rows(D, tm), _resident((2 * DFF, HALF)), _resident((2 * DFF, HALF))],
        out_specs=[_rows(2 * DFF, tm), _rows(DFF, tm)],
        out_shape=[jax.ShapeDtypeStruct((S, 2 * DFF), BF16), jax.ShapeDtypeStruct((S, DFF), BF16)],
        operands=(h2, w_lo, w_hi), exchange=exchange, exchange_operands=exchange_operands)


def _ffn_down_loss(act, x2, target, w_ffn_out, norm_final, tm=512):
    S = act.shape[0]

    def body(act_ref, x2_ref, t_ref, w_ref, gn_ref, dx3_ref, dx3b_ref, loss_ref, dvec_ref):
        i = pl.program_id(0)

        @pl.when(i == 0)
        def _():
            loss_ref[...] = jnp.zeros_like(loss_ref)
            dvec_ref[...] = jnp.zeros_like(dvec_ref)

        x3 = x2_ref[...] + _dot(act_ref[...], w_ref[...])
        r3, xh3 = _rms(x3)
        g_fin = gn_ref[...]
        e = xh3 * g_fin - t_ref[...]
        loss_ref[...] += jnp.sum(e * e, axis=(0, 1), keepdims=True) * (0.5 / D)
        dy = e * (1.0 / D)
        dvec_ref[0:1, :] += jnp.sum(dy * xh3, axis=0, keepdims=True)
        dx3 = _rms_bwd(dy, g_fin, r3, xh3)
        dx3_ref[...] = dx3
        dx3b_ref[...] = dx3.astype(BF16)

    return pl.pallas_call(
        body, name="ffn_down_loss", grid=(S // tm,),
        in_specs=[_rows(DFF, tm), _rows(D, tm), _rows(D, tm), _resident((DFF, D)), _resident((1, D))],
        out_specs=[_rows(D, tm), _rows(D, tm), _resident((1, 1)), _resident((8, D))],
        out_shape=[jax.ShapeDtypeStruct((S, D), F32), jax.ShapeDtypeStruct((S, D), BF16),
                   jax.ShapeDtypeStruct((1, 1), F32), jax.ShapeDtypeStruct((8, D), F32)],
        compiler_params=_params("arbitrary"),
    )(act, x2, target, w_ffn_out, norm_final)


def _ffn_bwd_down(dx3b, gu, w_ffn_out, tm=512):
    S = dx3b.shape[0]

    def body(d_ref, back_ref, w_ref, dgu_ref):
        d = d_ref[...]
        for c0, c1 in FF_CHUNKS:
            dact = _dot_nt(d, w_ref[c0:c1, :])
            dgu_ref[:, c0:c1] = (dact * back_ref[:, c0:c1].astype(F32)).astype(BF16)
            dgu_ref[:, DFF + c0:DFF + c1] = (dact * back_ref[:, DFF + c0:DFF + c1].astype(F32)).astype(BF16)

    return pl.pallas_call(
        body, name="ffn_bwd_down", grid=(S // tm,),
        in_specs=[_rows(D, tm), _rows(2 * DFF, tm), _resident((DFF, D))],
        out_specs=_rows(2 * DFF, tm),
        out_shape=jax.ShapeDtypeStruct((S, 2 * DFF), BF16),
        compiler_params=_params("parallel"),
    )(dx3b, gu, w_ffn_out)


def _ffn_bwd_up(dgu, x2, dx3, w_lo, w_hi, norm_ffn, w_o, tm=512):
    S = dgu.shape[0]
    HALF = D // 2

    def body(dgu_ref, x2_ref, dx3_ref, lo_ref, hi_ref, gf_ref, wo_ref, dx2_ref, dx2b_ref, dmixo_ref, dvec_ref):
        i = pl.program_id(0)

        @pl.when(i == 0)
        def _():
            dvec_ref[...] = jnp.zeros_like(dvec_ref)

        dgate, dup = dgu_ref[:, 0:DFF], dgu_ref[:, DFF:2 * DFF]
        dh2 = jnp.concatenate([_dot(dgate, w[0:DFF, :]) + _dot(dup, w[DFF:2 * DFF, :]) for w in (lo_ref, hi_ref)], axis=1)
        r2, xh2 = _rms(x2_ref[...])
        dvec_ref[0:1, :] += jnp.sum(dh2 * xh2, axis=0, keepdims=True)
        dx2 = dx3_ref[...] + _rms_bwd(dh2, gf_ref[...], r2, xh2)
        dx2_ref[...] = dx2
        dx2b = dx2.astype(BF16)
        dx2b_ref[...] = dx2b
        dmixo_ref[...] = _dot_nt(dx2b, wo_ref[...]).astype(BF16)

    return pl.pallas_call(
        body, name="ffn_bwd_up", grid=(S // tm,),
        in_specs=[_rows(2 * DFF, tm), _rows(D, tm), _rows(D, tm), _resident((2 * DFF, HALF)), _resident((2 * DFF, HALF)),
                  _resident((1, D)), _resident((D, D))],
        out_specs=[_rows(D, tm), _rows(D, tm), _rows(D, tm), _resident((8, D))],
        out_shape=[jax.ShapeDtypeStruct((S, D), F32), jax.ShapeDtypeStruct((S, D), BF16), jax.ShapeDtypeStruct((S, D), BF16),
                   jax.ShapeDtypeStruct((8, D), F32)],
        compiler_params=_params("arbitrary"),
    )(dgu, x2, dx3, w_lo, w_hi, norm_ffn, w_o)


VEC_ROWS = 16
MAT_WA = 4 * PG
MAT_WX = MAT_WA + NH * HD
MAT_ROWS = MAT_WX + NH * HD


def _mixer_bwd(proj, dmixo, y_pool, y_rnn, hr, kept, gates, wg, scale, w_pool_out, conv_w, conv_b, wa, ba, wx, bx, lam, w_rnn_out,
               exchange=None, exchange_operands=(), tm=256):
    S = proj.shape[0]
    nt = S // tm

    def rev(cols):
        return pl.BlockSpec((tm, cols), lambda i: (nt - 1 - i, 0))

    def halo(rows_, cols):
        per = tm // rows_
        return pl.BlockSpec((rows_, cols), lambda i: (jnp.maximum((nt - 1 - i) * per - 1, 0), 0))

    def body(proj_ref, projh_ref, dmixo_ref, yp_ref, yr_ref, hr_ref, hrh_ref, kept_ref, gates_ref, wg_ref, scale_ref, wpo_ref, cw_ref, cb_ref,
             wa_ref, ba_ref, wx_ref, bx_ref, lam_ref, wro_ref,
             dproj_ref, dypb_ref, dyrb_ref, dmat_ref, dvec_ref,
             q_carry, dv_carry, a_carry, g_carry, g_scr):
        i = pl.program_id(0)
        ti = nt - 1 - i

        @pl.when(i == 0)
        def _():
            q_carry[...] = jnp.zeros_like(q_carry)
            dv_carry[...] = jnp.zeros_like(dv_carry)
            a_carry[...] = jnp.zeros_like(a_carry)
            g_carry[...] = jnp.zeros_like(g_carry)
            dmat_ref[...] = jnp.zeros_like(dmat_ref)
            dvec_ref[...] = jnp.zeros_like(dvec_ref)

        rows = lax.broadcasted_iota(jnp.int32, (tm, 1), 0)
        t_glob = ti * tm + rows
        has_prev = (ti > 0).astype(F32)
        dmixo = dmixo_ref[...].astype(F32)

        s_p = _sigmoid(proj_ref[:, DP + 2 * DR:DP + 2 * DR + D])
        s_r = _sigmoid(proj_ref[:, DP + 2 * DR + D:DIN])
        dproj_ref[:, DP + 2 * DR:DP + 2 * DR + D] = (dmixo * yp_ref[...].astype(F32) * s_p * (1.0 - s_p)).astype(BF16)
        dproj_ref[:, DP + 2 * DR + D:DIN] = (dmixo * yr_ref[...].astype(F32) * s_r * (1.0 - s_r)).astype(BF16)
        dyp = (dmixo * s_p).astype(BF16)
        dyr = (dmixo * s_r).astype(BF16)
        dypb_ref[...] = dyp
        dyrb_ref[...] = dyr

        dz = _dot_nt(dyr, wro_ref[...])
        u_gate = proj_ref[:, DP + DR:DP + 2 * DR]
        gg, dgelu = _gelu(u_gate, with_grad=True)
        hr_t = hr_ref[...]
        dproj_ref[:, DP + DR:DP + 2 * DR] = (dz * hr_t * dgelu).astype(BF16)
        dhr = dz * gg

        sp = _softplus_neg(lam_ref[...])
        v, a, mult = (kept_ref[k] for k in range(KEPT))
        r, gi = (gates_ref[k].astype(F32) for k in range(2))
        inv_mult = 1.0 / mult

        C = jnp.where(rows == tm - 1, a_carry[0:1, :], pltpu.roll(a, tm - 1, axis=0))
        g_carry[0:1, :] = _linear_scan(g_scr, C, dhr, g_carry[0:1, :], reverse=True)
        a_carry[0:1, :] = a[0:1, :]
        g = g_scr[...]

        h_prev = jnp.where(rows == 0, hrh_ref[7:8, :] * has_prev, pltpu.roll(hr_t, 1, axis=0))
        da = g * h_prev
        gm = g * mult
        dmult = g * gi * v
        di = gm * v
        dv = gm * gi
        dlog_a = da * a - dmult * (a * a * inv_mult)
        dvec_ref[4:5, :] += jnp.sum(dlog_a * r, axis=0, keepdims=True)
        dra = (dlog_a * ((-LRU_C) * sp) * r * (1.0 - r))
        drx = di * gi * (1.0 - gi)
        dvec_ref[2:3, :] += jnp.sum(dra, axis=0, keepdims=True)
        dvec_ref[3:4, :] += jnp.sum(drx, axis=0, keepdims=True)
        drab = dra.astype(BF16)
        drxb = drx.astype(BF16)
        vb = v.astype(BF16)
        dvg = []
        for h in range(NH):
            sl = slice(h * HD, (h + 1) * HD)
            dvg.append(_dot_nt(drab[:, sl], wa_ref[h]) + _dot_nt(drxb[:, sl], wx_ref[h]))
            dmat_ref[MAT_WA + h * HD:MAT_WA + (h + 1) * HD, :] += _dot_tn(vb[:, sl], drab[:, sl])
            dmat_ref[MAT_WX + h * HD:MAT_WX + (h + 1) * HD, :] += _dot_tn(vb[:, sl], drxb[:, sl])
        dv = dv + jnp.concatenate(dvg, axis=1)
        dvec_ref[1:2, :] += jnp.sum(dv, axis=0, keepdims=True)
        dvext = jnp.concatenate([dv, dv_carry[...]], axis=0)
        dv_carry[...] = dv[0:CONV_HALO, :]
        n = tm + CONV_HALO
        u_rnn = proj_ref[:, DP:DP + DR]
        du_rnn = dv * cw_ref[3:4, :]
        dvec_ref[8:9, :] += jnp.sum(dv * u_rnn, axis=0, keepdims=True)
        for k in range(3):
            dv_k = pltpu.roll(dvext, n - (3 - k), axis=0)[0:tm, :]
            du_rnn = du_rnn + dv_k * cw_ref[k:k + 1, :]
            dvec_ref[5 + k:6 + k, :] += jnp.sum(dv_k * u_rnn, axis=0, keepdims=True)
        dproj_ref[:, DP:DP + DR] = du_rnn.astype(BF16)

        dpm = _dot_nt(dyp, wpo_ref[...])
        u_pool = proj_ref[:, 0:DP]
        ext = jnp.concatenate([projh_ref[:, 0:DP] * has_prev, u_pool], axis=0)
        sums = _pool_windows(ext, +1)
        scale_v = scale_ref[...]
        qs = []
        dpooled = []
        dscale = []
        for gi_, w in enumerate(WINDOWS):
            sl = slice(gi_ * PG, (gi_ + 1) * PG)
            inv_cnt = 1.0 / jnp.minimum(t_glob + 1, w).astype(F32)
            pooled_b = (sums[gi_][POOL_HALO:, :] * inv_cnt - u_pool[:, sl]).astype(BF16)
            mixed_g = _dot(pooled_b, wg_ref[gi_])
            dscale.append(jnp.sum(dpm[:, sl] * mixed_g, axis=0, keepdims=True))
            dmixed_b = (dpm[:, sl] * scale_v[:, sl]).astype(BF16)
            dmat_ref[gi_ * PG:(gi_ + 1) * PG, :] += _dot_tn(pooled_b, dmixed_b)
            dp_g = _dot_nt(dmixed_b, wg_ref[gi_])
            dpooled.append(dp_g)
            qs.append(dp_g * inv_cnt)
        dvec_ref[0:1, 0:DP] += jnp.concatenate(dscale, axis=1)
        q = jnp.concatenate(qs, axis=1)
        qext = jnp.concatenate([q, q_carry[...]], axis=0)
        q_carry[...] = q[0:POOL_HALO, :]
        tsum = _pool_windows(qext, -1)
        for gi_ in range(4):
            dproj_ref[:, gi_ * PG:(gi_ + 1) * PG] = (tsum[gi_][0:tm, :] - dpooled[gi_]).astype(BF16)

        @pl.when(i == nt - 1)
        def _():
            dvec_ref[4:5, :] = dvec_ref[4:5, :] * (LRU_C * _sigmoid(-lam_ref[...]))

    return _call(
        body, "mixer_bwd", (nt,),
        in_specs=[rev(DIN), halo(POOL_HALO, DIN), rev(D), rev(D), rev(D), rev(DR), halo(8, DR),
                  pl.BlockSpec((KEPT, tm, DR), lambda i: (0, nt - 1 - i, 0)),
                  pl.BlockSpec((2, tm, DR), lambda i: (0, nt - 1 - i, 0)), _resident((4, PG, PG)), _resident((1, DP)), _resident((DP, D)), _resident((4, DR)), _resident((1, DR)),
                  _resident((NH, HD, HD)), _resident((1, DR)), _resident((NH, HD, HD)), _resident((1, DR)),
                  _resident((1, DR)), _resident((DR, D))],
        out_specs=[rev(DIN), rev(D), rev(D), _resident((MAT_ROWS, HD)), _resident((VEC_ROWS, DR))],
        out_shape=[jax.ShapeDtypeStruct((S, DIN), BF16), jax.ShapeDtypeStruct((S, D), BF16),
                   jax.ShapeDtypeStruct((S, D), BF16), jax.ShapeDtypeStruct((MAT_ROWS, HD), F32),
                   jax.ShapeDtypeStruct((VEC_ROWS, DR), F32)],
        scratch_shapes=[pltpu.VMEM((POOL_HALO, DP), F32), pltpu.VMEM((CONV_HALO, DR), F32), pltpu.VMEM((8, DR), F32),
                        pltpu.VMEM((8, DR), F32), pltpu.VMEM((tm, DR), F32)],
        operands=(proj, proj, dmixo, y_pool, y_rnn, hr, hr, kept, gates, wg, scale, w_pool_out, conv_w, conv_b, wa, ba, wx, bx, lam,
                  w_rnn_out),
        exchange=exchange, exchange_operands=exchange_operands)


def _in_bwd(dproj, x, dx2, norm_mix, w_in, exchange=None, exchange_operands=(), tm=512):
    S = x.shape[0]

    def body(dp_ref, x_ref, dx2_ref, g_ref, w_ref, dx_ref, dg_ref):
        i = pl.program_id(0)

        @pl.when(i == 0)
        def _():
            dg_ref[...] = jnp.zeros_like(dg_ref)

        dh = _dot(dp_ref[:, 0:1536], w_ref[0:1536, :])
        dh = dh + _dot(dp_ref[:, 1536:3072], w_ref[1536:3072, :])
        dh = dh + _dot(dp_ref[:, 3072:DIN], w_ref[3072:DIN, :])
        xv = x_ref[...]
        r = lax.rsqrt(jnp.mean(xv * xv, axis=-1, keepdims=True) + EPS)
        xh = xv * r
        dg_ref[0:1, :] += jnp.sum(dh * xh, axis=0, keepdims=True)
        dxh = dh * g_ref[...]
        dx_ref[...] = dx2_ref[...] + r * (dxh - xh * jnp.mean(dxh * xh, axis=-1, keepdims=True))

    return _call(
        body, "in_bwd", (S // tm,),
        in_specs=[_rows(DIN, tm), _rows(D, tm), _rows(D, tm), _resident((1, D)), _resident((DIN, D))],
        out_specs=[_rows(D, tm), _resident((8, D))],
        out_shape=[jax.ShapeDtypeStruct((S, D), F32), jax.ShapeDtypeStruct((8, D), F32)],
        operands=(dproj, x, dx2, norm_mix, w_in), exchange=exchange, exchange_operands=exchange_operands)


def _wgrad(a, b, name, tk, tn, exchange=None, exchange_operands=()):
    S, K = a.shape
    N = b.shape[1]

    def body(a_ref, b_ref, o_ref):
        o_ref[...] = _dot_tn(a_ref[...], b_ref[...]).astype(BF16)

    (out,), exchanged = _call(
        body, name, (K // tk, N // tn),
        in_specs=[pl.BlockSpec((S, tk), lambda k, n: (0, k)), pl.BlockSpec((S, tn), lambda k, n: (0, n))],
        out_specs=[pl.BlockSpec((tk, tn), lambda k, n: (k, n))],
        out_shape=[jax.ShapeDtypeStruct((K, N), BF16)],
        operands=(a, b), exchange=exchange, exchange_operands=exchange_operands)
    return (out, exchanged) if exchange is not None else out


VEC_SCALE, VEC_CONV_B, VEC_BA, VEC_BX, VEC_LAM, VEC_CONV_W, VEC_NORM_FINAL, VEC_NORM_FFN = 0, 1, 2, 3, 4, 5, 9, 10
VEC_LOSS = 11


class _Big:
    def __init__(self, name, rows, cols, axis, n, dtype=BF16, transposed=False, src_cols=None):
        self.name, self.rows, self.cols, self.axis, self.n, self.dtype = name, rows, cols, axis, n, dtype
        self.transposed = transposed
        self.src_cols = src_cols
        self.block_shape = (rows, n) if axis == 1 else (n, cols)

    def block(self, ref, p):
        if self.axis == 1:
            return ref.at[:, pl.ds(pl.multiple_of(p * self.n, 128), self.n)]
        return ref.at[pl.ds(pl.multiple_of(p * self.n, 16 if self.dtype == BF16 else 8), self.n), :]


BIG = (_Big("w_in", DIN, D, 0, DIN // 8, transposed=True), _Big("w_pool_out", DP, D, 1, D // 8),
       _Big("w_rnn_out", DR, D, 0, DR // 8), _Big("w_o", D, D, 0, D // 8),
       _Big("w_ffn_in", 2 * DFF, D, 0, 2 * DFF // 8, transposed=True), _Big("w_ffn_out", DFF, D, 0, DFF // 8))
CONV_W = _Big("conv_w", 8, DR, 1, DR // 8, F32)
W_FFN_IN_HALVES = (_Big("w_ffn_in_lo", 2 * DFF, D // 2, 0, 2 * DFF // 8, src_cols=(0, D // 2)),
                   _Big("w_ffn_in_hi", 2 * DFF, D // 2, 0, 2 * DFF // 8, src_cols=(D // 2, D)))
GATHERED = BIG + (CONV_W,) + W_FFN_IN_HALVES

HBM_SPEC = pl.BlockSpec(memory_space=pl.ANY)
VMEM_SPEC = pl.BlockSpec(memory_space=pltpu.VMEM)


def _place():
    x, y, c = (lax.axis_index(a) for a in MESH_AXES)
    other_chips = [(1 - x, y), (x, 1 - y), (1 - x, 1 - y)]
    return x, y, c, other_chips


def _remote(src, dst, send_sems, recv_sems, idx, to):
    return pltpu.make_async_remote_copy(src_ref=src, dst_ref=dst, send_sem=send_sems.at[idx], recv_sem=recv_sems.at[idx],
                                        device_id=to, device_id_type=MESH)


def _device_index(chip, core):
    return 4 * chip[0] + 2 * chip[1] + core


class _Gather:
    def __init__(self, tensors):
        self.tensors = tuple(tensors)
        n = len(self.tensors)
        self.in_specs = [HBM_SPEC] * n
        self.out_specs = [HBM_SPEC] * n
        self.out_shape = [jax.ShapeDtypeStruct((T.rows, T.cols), T.dtype) for T in self.tensors]
        self.scratch_shapes = [pltpu.VMEM(T.block_shape, T.dtype) for T in self.tensors] + [
            pltpu.VMEM(T.block_shape, F32) for T in self.tensors] + [
            pltpu.SemaphoreType.DMA((n, 7)), pltpu.SemaphoreType.DMA((n, 7)), pltpu.SemaphoreType.DMA((n, 2))]

    def middles(self, steps):
        return [(steps // 2, self.relay), (steps - 1, self.middle)]

    def _copies(self, ins, outs, scratch):
        n = len(self.tensors)
        mine, raw, (send_sems, recv_sems, loc_sems) = scratch[:n], scratch[n:2 * n], scratch[2 * n:]
        x, y, c, chips = _place()
        sibling = (x, y, 1 - c)
        me = _device_index((x, y), c)
        relay_from = (jnp.where(c == 0, 1 - x, x), jnp.where(c == 0, y, 1 - y))
        relay_to = (jnp.where(c == 0, x, 1 - x), jnp.where(c == 0, 1 - y, y))
        loads, stores, first, relays, passed, arrivals, late = [], [], [], [], [], [], []
        for t, T in enumerate(self.tensors):
            place = T.block(outs[t], me)
            src = ins[t] if T.src_cols is None else ins[t].at[:, T.src_cols[0]:T.src_cols[1]]
            loads.append(pltpu.make_async_copy(src, raw[t], loc_sems.at[t, 0]))
            stores.append(pltpu.make_async_copy(mine[t], place, loc_sems.at[t, 1]))
            first.append(_remote(mine[t], place, send_sems, recv_sems, (t, 0), sibling))
            theirs = T.block(outs[t], _device_index((x, y), 1 - c))
            late.append(_remote(theirs, theirs, send_sems, recv_sems, (t, 0), sibling))
            relayed = T.block(outs[t], _device_index(relay_from, c))
            relays.append(_remote(relayed, relayed, send_sems, recv_sems, (t, 3), (*relay_to, c)))
            for k, chip in enumerate(chips):
                if k < 2:
                    first.append(_remote(mine[t], place, send_sems, recv_sems, (t, 1 + k), (*chip, c)))
                land = T.block(outs[t], _device_index(chip, c))
                arrivals.append(_remote(land, land, send_sems, recv_sems, (t, 1 + k), sibling))
                passed.append(_remote(land, land, send_sems, recv_sems, (t, 4 + k), sibling))
                theirs = T.block(outs[t], _device_index(chip, 1 - c))
                late.append(_remote(theirs, theirs, send_sems, recv_sems, (t, 4 + k), sibling))
        return loads, stores, first, relays, passed, arrivals, late

    def start(self, ins, outs, scratch):
        loads, stores, first, _, _, _, _ = self._copies(ins, outs, scratch)
        n = len(self.tensors)
        for cp in loads:
            cp.start()
        for t, cp in enumerate(loads):
            cp.wait()
            scratch[t][...] = scratch[n + t][...].astype(self.tensors[t].dtype)
        for cp in stores + first:
            cp.start()

    def relay(self, ins, outs, scratch, skip=0):
        _, _, _, relays, passed, arrivals, _ = self._copies(ins, outs, scratch)
        for t in range(skip, len(self.tensors)):
            arrivals[3 * t].wait_recv()
            arrivals[3 * t + 1].wait_recv()
            for cp in (relays[t], passed[3 * t], passed[3 * t + 1]):
                cp.start()

    def middle(self, ins, outs, scratch, skip=0):
        _, _, _, _, passed, arrivals, _ = self._copies(ins, outs, scratch)
        for t in range(skip, len(self.tensors)):
            arrivals[3 * t + 2].wait_recv()
            passed[3 * t + 2].start()

    def finish(self, ins, outs, scratch, skip=0):
        _, stores, first, relays, passed, _, late = self._copies(ins, outs, scratch)
        for cp in late[4 * skip:]:
            cp.wait_recv()
        for cp in first + relays + passed:
            cp.wait_send()
        for cp in stores[skip:]:
            cp.wait()


def _in_proj_gather(x, norm_mix, blocks, tensors, order, tm=1024):
    S = x.shape[0]
    nt = S // tm
    n = len(tensors)
    gather = _Gather(tensors)
    CB = 2 * tensors[0].n

    def body(order_ref, x_ref, g_ref, *refs):
        ins, (proj_ref, h_ref), outs = refs[:n], refs[n:n + 2], refs[n + 2:2 * n + 2]
        (h_all, w_chip, w_sem), scratch = refs[2 * n + 2:2 * n + 5], refs[2 * n + 5:]
        q, i = pl.program_id(0), pl.program_id(1)
        _, stores, _, relays, passed, arrivals, late = gather._copies(ins, outs, scratch)

        def fetch(turn):
            rows = outs[0].at[pl.ds(pl.multiple_of(order_ref[turn] * CB, 16), CB), :]
            cp = pltpu.make_async_copy(rows, w_chip, w_sem)
            cp.start()
            cp.wait()

        @pl.when((q == 0) & (i == 0))
        def _():
            gather.start(ins, outs, scratch)
            late[0].wait_recv()
            stores[0].wait()
            fetch(0)

        @pl.when((q == 1) & (i == 0))
        def _():
            arrivals[0].wait_recv()
            arrivals[1].wait_recv()
            for cp in (relays[0], passed[0], passed[1]):
                cp.start()
            late[1].wait_recv()
            fetch(1)

        @pl.when((q == 2) & (i == 0))
        def _():
            late[2].wait_recv()
            fetch(2)
            gather.relay(ins, outs, scratch, skip=1)

        @pl.when((q == 3) & (i == 0))
        def _():
            arrivals[2].wait_recv()
            passed[2].start()
            late[3].wait_recv()
            fetch(3)

        rows = pl.ds(pl.multiple_of(i * tm, tm), tm)

        @pl.when(q == 0)
        def _():
            xv = x_ref[...]
            r = lax.rsqrt(jnp.mean(xv * xv, axis=-1, keepdims=True) + EPS)
            h = (xv * r * g_ref[...]).astype(BF16)
            h_all[rows, :] = h
            h_ref[...] = h

        proj_ref[...] = _dot_nt(h_all[rows, :], w_chip[...])

        @pl.when((q == 3) & (i == nt - 1))
        def _():
            gather.middle(ins, outs, scratch, skip=1)
            gather.finish(ins, outs, scratch, skip=1)

    row_tile = lambda q, i, order: (jnp.where(q == 0, i, nt - 1), 0)
    whole = lambda shape: pl.BlockSpec(shape, lambda q, i, order: (0,) * len(shape), pipeline_mode=pl.Buffered(1))
    outs = pl.pallas_call(
        body, name="in_proj_gather",
        grid_spec=pltpu.PrefetchScalarGridSpec(
            num_scalar_prefetch=1, grid=(4, nt),
            in_specs=[pl.BlockSpec((tm, D), row_tile), whole((1, D))] + gather.in_specs,
            out_specs=[pl.BlockSpec((tm, CB), lambda q, i, order: (i, order[q])), pl.BlockSpec((tm, D), row_tile)]
            + gather.out_specs,
            scratch_shapes=[pltpu.VMEM((S, D), BF16), pltpu.VMEM((CB, D), BF16), pltpu.SemaphoreType.DMA]
            + gather.scratch_shapes),
        out_shape=[jax.ShapeDtypeStruct((S, DIN), F32), jax.ShapeDtypeStruct((S, D), BF16)] + gather.out_shape,
        compiler_params=_params("arbitrary", "arbitrary"),
    )(order, x, norm_mix, *blocks)
    return outs[:2], outs[2:]


PAIR_ROWS = 32


def _pair_reduce(grads, tensors, name):
    nt = len(tensors)

    def body(*refs):
        ins, own_out, sums_out, landed, mine = (refs[k * nt:(k + 1) * nt] for k in range(5))
        send_sems, recv_sems, loc_sems = refs[5 * nt:]
        x, y, c, chips = _place()
        chip_of = [2 * chip[0] + chip[1] for chip in chips]
        swaps, loads = [], []
        for t, T in enumerate(tensors):
            for j in range(4):
                swaps.append(_remote(T.block(ins[t], 2 * j + 1 - c), landed[t].at[j], send_sems, recv_sems, (t, j),
                                     (x, y, 1 - c)))
            for k in range(3):
                loads.append(pltpu.make_async_copy(T.block(ins[t], 2 * chip_of[k] + c), mine[t].at[k], loc_sems.at[t, k]))
        for cp in swaps + loads:
            cp.start()
        for cp in loads:
            cp.wait()
        for cp in swaps:
            cp.wait_recv()
        stores = []
        for t, T in enumerate(tensors):
            for k in range(3):
                acc, got = mine[t].at[k], landed[t].at[chip_of[k]]

                def add(i, carry, acc=acc, got=got):
                    rows = pl.ds(pl.multiple_of(i * PAIR_ROWS, PAIR_ROWS), PAIR_ROWS)
                    acc[rows, :] = (acc[rows, :].astype(F32) + got[rows, :].astype(F32)).astype(BF16)
                    return carry

                lax.fori_loop(0, T.block_shape[0] // PAIR_ROWS, add, 0)
            stores.append(pltpu.make_async_copy(mine[t], sums_out[t], loc_sems.at[t, 3]))
            stores.append(pltpu.make_async_copy(landed[t].at[2 * x + y], own_out[t], loc_sems.at[t, 4]))
        for cp in stores:
            cp.start()
        for cp in swaps:
            cp.wait_send()
        for cp in stores:
            cp.wait()

    blocks = [T.block_shape for T in tensors]
    return pl.pallas_call(
        body, name=name,
        in_specs=[HBM_SPEC] * nt, out_specs=[HBM_SPEC] * (2 * nt),
        out_shape=[jax.ShapeDtypeStruct(b, BF16) for b in blocks] + [jax.ShapeDtypeStruct((3,) + b, BF16) for b in blocks],
        scratch_shapes=[pltpu.VMEM((4,) + b, BF16) for b in blocks] + [pltpu.VMEM((3,) + b, BF16) for b in blocks]
        + [pltpu.SemaphoreType.DMA((nt, 4)), pltpu.SemaphoreType.DMA((nt, 4)), pltpu.SemaphoreType.DMA((nt, 5))],
        compiler_params=pltpu.CompilerParams(vmem_limit_bytes=VMEM_LIMIT),
    )(*grads)


class _Scatter:
    def middles(self, steps):
        return []

    def __init__(self, tensors):
        n = len(tensors)
        self.in_specs = [HBM_SPEC] * n
        self.out_specs = [HBM_SPEC] * n
        self.out_shape = [jax.ShapeDtypeStruct((3,) + T.block_shape, BF16) for T in tensors]
        self.scratch_shapes = [pltpu.SemaphoreType.DMA((n, 3)), pltpu.SemaphoreType.DMA((n, 3))]

    def _copies(self, ins, outs, scratch):
        send_sems, recv_sems = scratch
        x, y, c, chips = _place()
        return [_remote(ins[t].at[k], outs[t].at[k], send_sems, recv_sems, (t, k), (*chip, c))
                for t in range(len(ins)) for k, chip in enumerate(chips)]

    def start(self, ins, outs, scratch):
        for cp in self._copies(ins, outs, scratch):
            cp.start()

    def finish(self, ins, outs, scratch):
        for cp in self._copies(ins, outs, scratch):
            cp.wait()


def _adamw(w, g, m, v):
    m = ADAM_B1 * m + (1.0 - ADAM_B1) * g
    v = ADAM_B2 * v + (1.0 - ADAM_B2) * (g * g)
    m_hat = m / (1.0 - ADAM_B1 ** ADAM_STEP)
    v_hat = v / (1.0 - ADAM_B2 ** ADAM_STEP)
    delta = -ADAM_LR * (m_hat / (jnp.sqrt(v_hat) + ADAM_EPS) + ADAM_WD * w)
    return delta, m, v


def _final_sum(T, g, lz1, lz2, where, w, m, v):
    rows, cols = T.block_shape
    sub = 4 if T.axis == 0 and rows % 64 == 0 and rows > 256 else 1
    blk = (rows // sub, cols)

    def body(where_ref, g_ref, l1_ref, l2_ref, w_ref, m_ref, v_ref, g_out, d_out, m_out, v_out):
        tot = g_ref[...].astype(F32) + l1_ref[...].astype(F32)
        for k in range(3):
            tot = tot + l2_ref[k].astype(F32)
        g_out[...] = tot
        d_out[...], m_out[...], v_out[...] = _adamw(w_ref[...], tot, m_ref[...], v_ref[...])

    def in_whole(r, wh):
        p = wh[0]
        return (0, p) if T.axis == 1 else (p * sub + r, 0)

    own = pl.BlockSpec(blk, lambda r, wh: (r, 0))
    return pl.pallas_call(
        body, name="grad_final_" + T.name,
        grid_spec=pltpu.PrefetchScalarGridSpec(
            num_scalar_prefetch=1, grid=(sub,),
            in_specs=[pl.BlockSpec(blk, in_whole),
                      own,
                      pl.BlockSpec((3,) + blk, lambda r, wh: (0, r, 0)), own, own, own],
            out_specs=[own] * 4),
        out_shape=[jax.ShapeDtypeStruct(T.block_shape, F32)] * 4,
        compiler_params=_params("arbitrary"),
    )(where, g, lz1, lz2, w, m, v)


VEC_PIECE = DR // 8


class _AllReduce:
    def __init__(self, items):
        self.items = tuple(items)
        n = len(self.items)
        self.in_specs = [HBM_SPEC] * n
        self.out_specs = [HBM_SPEC] * n
        self.out_shape = [jax.ShapeDtypeStruct(shape, F32) for shape, _ in self.items]
        pieces = [(shape[0] // 8, shape[1]) if axis == 0 else (shape[0], shape[1] // 8) for shape, axis in self.items]
        self.scratch_shapes = ([pltpu.VMEM((8,) + p, F32) for p in pieces] + [pltpu.VMEM(p, F32) for p in pieces] + [
            pltpu.SemaphoreType.DMA((2 * n, 8)), pltpu.SemaphoreType.DMA((2 * n, 8)), pltpu.SemaphoreType.DMA((2 * n,))])

    def middles(self, steps):
        return [(steps // 2, self.middle)]

    def _copies(self, ins, outs, scratch):
        n = len(self.items)
        landed, sums, (send_sems, recv_sems, loc_sems) = scratch[:n], scratch[n:2 * n], scratch[2 * n:]
        x, y, c, _ = _place()
        me = _device_index((x, y), c)

        def peer(r):
            return (1 - x if r & 4 else x, 1 - y if r & 2 else y, 1 - c if r & 1 else c)

        def piece(i, ref, p):
            shape, axis = self.items[i]
            if axis == 0:
                rows = shape[0] // 8
                return ref.at[pl.ds(pl.multiple_of(p * rows, 8), rows), :]
            cols = shape[1] // 8
            return ref.at[:, pl.ds(pl.multiple_of(p * cols, 128), cols)]

        own, scatter, arrivals, keep, spread, late = [], [], [], [], [], []
        for i in range(n):
            own.append(pltpu.make_async_copy(piece(i, ins[i], me), landed[i].at[0], loc_sems.at[2 * i]))
            keep.append(pltpu.make_async_copy(sums[i], piece(i, outs[i], me), loc_sems.at[2 * i + 1]))
            for r in range(1, 8):
                to = peer(r)
                p = _device_index(to[:2], to[2])
                scatter.append(_remote(piece(i, ins[i], p), landed[i].at[r], send_sems, recv_sems, (2 * i, r), to))
                spread.append(_remote(sums[i], piece(i, outs[i], me), send_sems, recv_sems, (2 * i + 1, r), to))
                late.append(_remote(sums[i], piece(i, outs[i], p), send_sems, recv_sems, (2 * i + 1, r), to))
        return own, scatter, keep, spread, late, landed, sums

    def start(self, ins, outs, scratch):
        own, scatter, _, _, _, _, _ = self._copies(ins, outs, scratch)
        for cp in own + scatter:
            cp.start()

    def middle(self, ins, outs, scratch):
        own, scatter, keep, spread, _, landed, sums = self._copies(ins, outs, scratch)
        for cp in own:
            cp.wait()
        for cp in scatter:
            cp.wait_recv()
        for i in range(len(self.items)):
            total = landed[i][0]
            for r in range(1, 8):
                total = total + landed[i][r]
            sums[i][...] = total
        for cp in keep + spread:
            cp.start()

    def finish(self, ins, outs, scratch):
        _, scatter, keep, spread, late, _, _ = self._copies(ins, outs, scratch)
        for cp in late:
            cp.wait_recv()
        for cp in scatter + spread:
            cp.wait_send()
        for cp in keep:
            cp.wait()


class _Both:
    def __init__(self, a, b):
        self.a, self.b = a, b
        self.in_specs, self.out_specs = a.in_specs + b.in_specs, a.out_specs + b.out_specs
        self.out_shape, self.scratch_shapes = a.out_shape + b.out_shape, a.scratch_shapes + b.scratch_shapes

    def _each(self, ins, outs, scratch):
        a = self.a
        i, o, s = len(a.in_specs), len(a.out_specs), len(a.scratch_shapes)
        return (a, ins[:i], outs[:o], scratch[:s]), (self.b, ins[i:], outs[o:], scratch[s:])

    def middles(self, steps):
        def of(which, middle):
            return lambda ins, outs, scratch: middle(*self._each(ins, outs, scratch)[which][1:])
        return [(at, of(which, middle)) for which, e in enumerate((self.a, self.b)) for at, middle in e.middles(steps)]

    def start(self, ins, outs, scratch):
        for e, i, o, s in self._each(ins, outs, scratch):
            e.start(i, o, s)

    def finish(self, ins, outs, scratch):
        for e, i, o, s in self._each(ins, outs, scratch):
            e.finish(i, o, s)


def _all_reduce(arrays, items, name):
    reduce = _AllReduce(items)
    n = len(items)

    def body(*refs):
        ins, outs, scratch = refs[:n], refs[n:2 * n], refs[2 * n:]
        reduce.start(ins, outs, scratch)
        reduce.middle(ins, outs, scratch)
        reduce.finish(ins, outs, scratch)

    return pl.pallas_call(
        body, name=name, in_specs=reduce.in_specs, out_specs=reduce.out_specs, out_shape=reduce.out_shape,
        scratch_shapes=reduce.scratch_shapes,
    )(*arrays)


def _adam_small(grads, wmv):
    n = len(grads)

    def body(*refs):
        g_refs, rest = refs[:n], refs[n:]
        ins, outs = rest[:3 * n], rest[3 * n:]
        for i in range(n):
            d, m, v = _adamw(ins[3 * i][...], g_refs[i][...], ins[3 * i + 1][...], ins[3 * i + 2][...])
            outs[3 * i][...], outs[3 * i + 1][...], outs[3 * i + 2][...] = d, m, v

    flat = [a for t in wmv for a in t]
    return pl.pallas_call(
        body, name="adam_small",
        in_specs=[VMEM_SPEC] * (4 * n), out_specs=[VMEM_SPEC] * (3 * n),
        out_shape=[jax.ShapeDtypeStruct(a.shape, F32) for a in flat],
    )(*grads, *flat)


WEIGHT_NAMES = ("norm_mix", "w_in", "w_pool_grp", "pool_scale", "w_pool_out", "conv_w", "conv_b", "w_rg_a", "b_rg_a", "w_rg_x",
                "b_rg_x", "lru_lambda", "w_rnn_out", "w_o", "norm_ffn", "w_ffn_in", "w_ffn_out", "norm_final")


def kernel(x, norm_mix, w_in, w_pool_grp, pool_scale, w_pool_out, conv_w, conv_b, w_rg_a, b_rg_a, w_rg_x, b_rg_x, lru_lambda, w_rnn_out, w_o, norm_ffn, w_ffn_in, w_ffn_out, norm_final, loss_target, m_norm_mix, m_w_in, m_w_pool_grp, m_pool_scale, m_w_pool_out, m_conv_w, m_conv_b, m_w_rg_a, m_b_rg_a, m_w_rg_x, m_b_rg_x, m_lru_lambda, m_w_rnn_out, m_w_o, m_norm_ffn, m_w_ffn_in, m_w_ffn_out, m_norm_final, v_norm_mix, v_w_in, v_w_pool_grp, v_pool_scale, v_w_pool_out, v_conv_w, v_conv_b, v_w_rg_a, v_b_rg_a, v_w_rg_x, v_b_rg_x, v_lru_lambda, v_w_rnn_out, v_w_o, v_norm_ffn, v_w_ffn_in, v_w_ffn_out, v_norm_final):
    w = dict(norm_mix=norm_mix, w_in=w_in, w_pool_grp=w_pool_grp, pool_scale=pool_scale, w_pool_out=w_pool_out, conv_w=conv_w,
             conv_b=conv_b, w_rg_a=w_rg_a, b_rg_a=b_rg_a, w_rg_x=w_rg_x, b_rg_x=b_rg_x, lru_lambda=lru_lambda,
             w_rnn_out=w_rnn_out, w_o=w_o, norm_ffn=norm_ffn, w_ffn_in=w_ffn_in, w_ffn_out=w_ffn_out, norm_final=norm_final)
    m = dict(norm_mix=m_norm_mix, w_in=m_w_in, w_pool_grp=m_w_pool_grp, pool_scale=m_pool_scale, w_pool_out=m_w_pool_out,
             conv_w=m_conv_w, conv_b=m_conv_b, w_rg_a=m_w_rg_a, b_rg_a=m_b_rg_a, w_rg_x=m_w_rg_x, b_rg_x=m_b_rg_x,
             lru_lambda=m_lru_lambda, w_rnn_out=m_w_rnn_out, w_o=m_w_o, norm_ffn=m_norm_ffn, w_ffn_in=m_w_ffn_in,
             w_ffn_out=m_w_ffn_out, norm_final=m_norm_final)
    v = dict(norm_mix=v_norm_mix, w_in=v_w_in, w_pool_grp=v_w_pool_grp, pool_scale=v_pool_scale, w_pool_out=v_w_pool_out,
             conv_w=v_conv_w, conv_b=v_conv_b, w_rg_a=v_w_rg_a, b_rg_a=v_b_rg_a, w_rg_x=v_w_rg_x, b_rg_x=v_b_rg_x,
             lru_lambda=v_lru_lambda, w_rnn_out=v_w_rnn_out, w_o=v_w_o, norm_ffn=v_norm_ffn, w_ffn_in=v_w_ffn_in,
             w_ffn_out=v_w_ffn_out, norm_final=v_norm_final)
    xi, yi, ci = (lax.axis_index(a) for a in MESH_AXES)
    chip = 2 * xi + yi

    def held(T, a):
        return jnp.swapaxes(a, 0, 1) if T.transposed else a

    where = jnp.stack([2 * chip + ci]).astype(jnp.int32)
    by_name = {T.name: T for T in GATHERED}
    block = {T.name: held(T, w[T.name][0]) for T in BIG}
    block["conv_w"] = jnp.pad(conv_w[0], ((0, CONV_W.rows - 4), (0, 0)))
    block["w_ffn_in_lo"] = block["w_ffn_in_hi"] = block["w_ffn_in"]

    def gather_of(*names):
        return dict(exchange=_Gather([by_name[n] for n in names]), exchange_operands=[block[n] for n in names])

    def pair_sums(names, partials, tag):
        out = _pair_reduce(partials, [by_name[n] for n in names], "grad_pair_reduce_" + tag)
        return list(out[:len(names)]), list(out[len(names):])

    xs, target = x[0], loss_target[0]
    wg_b, wa_b, wx_b = (a[0].astype(BF16) for a in (w_pool_grp, w_rg_a, w_rg_x))
    ba2, bx2 = b_rg_a.reshape(1, DR), b_rg_x.reshape(1, DR)
    first = ("w_in", "w_pool_out", "w_rnn_out", "conv_w", "w_o")
    order = jnp.stack([chip, 2 * (1 - xi) + yi, 2 * xi + (1 - yi), 2 * (1 - xi) + (1 - yi)]).astype(jnp.int32)
    (proj, h1), (w_in_g, w_pool_out_g, w_rnn_out_g, conv_g, w_o_g) = _in_proj_gather(
        xs, norm_mix, [block[n] for n in first], [by_name[n] for n in first], order)
    mixer_weights = (wg_b, pool_scale, w_pool_out_g, conv_g[0:4], conv_b, wa_b, ba2, wx_b, bx2, lru_lambda, w_rnn_out_g)
    (pm, y_pool, hr, z, y_rnn, kept, gates), (w_ffn_lo_g, w_ffn_hi_g) = _mixer_fwd(
        proj, *mixer_weights, **gather_of("w_ffn_in_lo", "w_ffn_in_hi"))
    (mix, x2, h2), _ = _merge_out(xs, proj, y_pool, y_rnn, w_o_g, norm_ffn)
    (gu, act), (w_ffn_out_g,) = _ffn_up(h2, w_ffn_lo_g, w_ffn_hi_g, **gather_of("w_ffn_out"))
    dx3, dx3b, loss_part, dvec_fin = _ffn_down_loss(act, x2, target, w_ffn_out_g, norm_final.reshape(1, D))

    dgu = _ffn_bwd_down(dx3b, gu, w_ffn_out_g)
    dx2, dx2b, dmixo, dvec_ffn = _ffn_bwd_up(dgu, x2, dx3, w_ffn_lo_g, w_ffn_hi_g, norm_ffn, w_o_g)
    names_a = ("w_ffn_in", "w_ffn_out", "w_o")
    part_a = [_wgrad(dgu, h2, "wgrad_ffn_in", 1408, 512), _wgrad(act, dx3b, "wgrad_ffn_out", 1408, 512),
              _wgrad(mix, dx2b, "wgrad_o", 1024, 256)]
    lz1_a, sums_a = pair_sums(names_a, part_a, "ffn")
    (dproj, dypb, dyrb, dmat, dvec_mix), lz2_a = _mixer_bwd(
        proj, dmixo, y_pool, y_rnn, hr, kept, gates, *mixer_weights,
        exchange=_Scatter([by_name[n] for n in names_a]), exchange_operands=sums_a)
    names_b = ("w_pool_out", "w_rnn_out")
    part_b = [_wgrad(pm, dypb, "wgrad_pool_out", 512, 256), _wgrad(z, dyrb, "wgrad_rnn_out", 1024, 256)]
    lz1_b, sums_b = pair_sums(names_b, part_b, "mix")
    dvec = jnp.concatenate([dvec_mix[0:9], dvec_fin[0:1], dvec_ffn[0:1], jnp.pad(loss_part, ((0, 0), (0, DR - 1))),
                            jnp.zeros((VEC_ROWS - 12, DR), F32)], axis=0)
    g_in, exchanged = _wgrad(
        dproj, h1, "wgrad_in", 1152, 1024,
        exchange=_Both(_Scatter([by_name[n] for n in names_b]), _AllReduce([((MAT_ROWS, HD), 0), ((VEC_ROWS, DR), 1)])),
        exchange_operands=sums_b + [dmat, dvec])
    lz2_b, (mat, vec) = exchanged[:2], exchanged[2:]
    loss = vec[VEC_LOSS, 0]
    lz1_c, sums_c = pair_sums(("w_in",), [g_in], "in")
    (grad_x, dvec_in), lz2_c = _in_bwd(dproj, xs, dx2, norm_mix, w_in_g,
                                       exchange=_Scatter([by_name["w_in"]]), exchange_operands=sums_c)
    (vec_in,) = _all_reduce([dvec_in], [((8, D), 1)], "all_reduce_norm_mix")

    grads, delta, new_m, new_v = {}, {}, {}, {}
    for n, g, l1, l2 in zip(names_a + names_b + ("w_in",), part_a + part_b + [g_in], lz1_a + lz1_b + lz1_c,
                            lz2_a + lz2_b + lz2_c):
        T = by_name[n]
        out = _final_sum(T, g, l1, l2, where, held(T, w[n][0]), held(T, m[n][0]), held(T, v[n][0]))
        grads[n], delta[n], new_m[n], new_v[n] = (held(T, a) for a in out)
    me = 4 * xi + 2 * yi + ci
    small_grads = dict(
        w_pool_grp=mat[0:MAT_WA], w_rg_a=mat[MAT_WA:MAT_WX], w_rg_x=mat[MAT_WX:MAT_ROWS],
        pool_scale=vec[VEC_SCALE:VEC_SCALE + 1, 0:DP], conv_b=vec[VEC_CONV_B:VEC_CONV_B + 1],
        b_rg_a=vec[VEC_BA:VEC_BA + 1], b_rg_x=vec[VEC_BX:VEC_BX + 1], lru_lambda=vec[VEC_LAM:VEC_LAM + 1],
        conv_w=lax.dynamic_slice(vec, (VEC_CONV_W, VEC_PIECE * me), (4, VEC_PIECE)),
        norm_final=vec[VEC_NORM_FINAL:VEC_NORM_FINAL + 1], norm_ffn=vec[VEC_NORM_FFN:VEC_NORM_FFN + 1],
        norm_mix=vec_in[0:1])
    names = list(small_grads)
    as2d = lambda a, g: a.reshape(g.shape)
    upd = _adam_small([small_grads[n] for n in names],
                      [(as2d(w[n], small_grads[n]), as2d(m[n], small_grads[n]), as2d(v[n], small_grads[n])) for n in names])
    for i, n in enumerate(names):
        grads[n] = small_grads[n]
        delta[n], new_m[n], new_v[n] = upd[3 * i:3 * i + 3]

    shaped = lambda d: [d[n].reshape(w[n].shape) for n in WEIGHT_NAMES]
    return (loss, grad_x[None], *shaped(grads), *shaped(delta), *shaped(new_m), *shaped(new_v))
```

```python
import math

import jax
import jax.numpy as jnp
from jax import lax
from jax.experimental import pallas as pl
from jax.experimental.pallas import tpu as pltpu

F32 = jnp.float32
BF16 = jnp.bfloat16

D = 1024
DP = 512
PG = 128
WINDOWS = (2, 4, 8, 16)
DR = 1024
NH = 8
HD = 128
DIN = 4608
DFF = 2816
EPS = 1e-6
LRU_C = 8.0
POOL_HALO = 16
CONV_HALO = 8
KEPT = 3

ADAM_LR = 0.001
ADAM_B1 = 0.9
ADAM_B2 = 0.999
ADAM_EPS = 1e-08
ADAM_WD = 0.01
ADAM_STEP = 10

VMEM_LIMIT = 56 * 1024 * 1024
MESH_AXES = ("x", "y", "c")
MESH = pl.DeviceIdType.MESH


def _dot(a, b):
    return jnp.dot(a, b, preferred_element_type=F32)


def _dot_nt(a, b):
    return lax.dot_general(a, b, (((1,), (1,)), ((), ())), preferred_element_type=F32)


def _dot_tn(a, b):
    return lax.dot_general(a, b, (((0,), (0,)), ((), ())), preferred_element_type=F32)


def _params(*sem):
    return pltpu.CompilerParams(dimension_semantics=sem, vmem_limit_bytes=VMEM_LIMIT)


def _resident(shape):
    nd = len(shape)
    return pl.BlockSpec(shape, lambda i: (0,) * nd, pipeline_mode=pl.Buffered(1))


def _rows(shape_cols, tm):
    return pl.BlockSpec((tm, shape_cols), lambda i: (i, 0))


def _call(body, name, grid, in_specs, out_specs, out_shape, operands, scratch_shapes=(), exchange=None, exchange_operands=()):
    n_in, n_out, n_scr = len(in_specs), len(out_specs), len(scratch_shapes)
    steps = math.prod(grid)
    if exchange is None:
        outs = pl.pallas_call(body, name=name, grid=grid, in_specs=in_specs, out_specs=out_specs, out_shape=out_shape,
                              scratch_shapes=list(scratch_shapes), compiler_params=_params(*["arbitrary"] * len(grid)))(*operands)
        return outs, []
    e_in, e_out = len(exchange.in_specs), len(exchange.out_specs)

    def hosted(*refs):
        ins, refs = refs[:n_in], refs[n_in:]
        e_ins, refs = refs[:e_in], refs[e_in:]
        outs, refs = refs[:n_out], refs[n_out:]
        e_outs, refs = refs[:e_out], refs[e_out:]
        scr, e_scr = refs[:n_scr], refs[n_scr:]
        step = pl.program_id(0)
        for axis in range(1, len(grid)):
            step = step * grid[axis] + pl.program_id(axis)
        pl.when(step == 0)(lambda: exchange.start(e_ins, e_outs, e_scr))
        for at, middle in exchange.middles(steps):
            pl.when(step == at)(lambda middle=middle: middle(e_ins, e_outs, e_scr))
        body(*ins, *outs, *scr)
        pl.when(step == steps - 1)(lambda: exchange.finish(e_ins, e_outs, e_scr))

    outs = pl.pallas_call(
        hosted, name=name, grid=grid, in_specs=list(in_specs) + exchange.in_specs,
        out_specs=list(out_specs) + exchange.out_specs, out_shape=list(out_shape) + exchange.out_shape,
        scratch_shapes=list(scratch_shapes) + exchange.scratch_shapes,
        compiler_params=_params(*["arbitrary"] * len(grid)))(*operands, *exchange_operands)
    return outs[:n_out], outs[n_out:]


GELU_C = math.sqrt(2.0 / math.pi)
GELU_K = 0.044715 * GELU_C


def _gelu(x, with_grad=False):
    x2 = x * x
    t = jnp.tanh(x * (GELU_C + GELU_K * x2))
    hx = 0.5 * x
    y = hx + hx * t
    if not with_grad:
        return y
    return y, 0.5 + 0.5 * t + hx * (1.0 - t * t) * (GELU_C + (3.0 * GELU_K) * x2)


def _softplus_neg(lam):
    z = jnp.exp(-jnp.abs(lam))
    u = 1.0 + z
    dlt = u - 1.0
    log1p = jnp.where(dlt == 0.0, z, jnp.log(u) * (z / jnp.where(dlt == 0.0, 1.0, dlt)))
    return jnp.maximum(-lam, 0.0) + log1p


def _sigmoid(x):
    return 0.5 * jnp.tanh(0.5 * x) + 0.5


def _linear_scan(out_ref, A, B, h0, reverse):
    n = A.shape[0]
    sub = lax.broadcasted_iota(jnp.int32, (8, 1), 0)
    tiles = range(n // 8 - 1, -1, -1) if reverse else range(n // 8)
    carry = h0
    for j in tiles:
        a, b = A[8 * j:8 * j + 8, :], B[8 * j:8 * j + 8, :]
        for d in (1, 2, 4):
            keep = (sub < 8 - d) if reverse else (sub >= d)
            shift = 8 - d if reverse else d
            b = jnp.where(keep, a * pltpu.roll(b, shift, axis=0) + b, b)
            a = jnp.where(keep, a * pltpu.roll(a, shift, axis=0), a)
        h = a * carry + b
        out_ref[8 * j:8 * j + 8, :] = h
        carry = h[0:1, :] if reverse else h[7:8, :]
    return carry


def _pool_windows(ext, shift_sign):
    n = ext.shape[0]
    s = ext
    outs = []
    for w in WINDOWS:
        d = w // 2
        s = s + pltpu.roll(s, d if shift_sign > 0 else n - d, axis=0)
        outs.append(s[:, :PG])
        s = s[:, PG:]
    return outs


def _conv_taps(uext):
    taps = []
    for k in range(4):
        sh = 3 - k
        v = uext if sh == 0 else pltpu.roll(uext, sh, axis=0)
        taps.append(v[CONV_HALO:, :])
    return taps


def _gates(v, wa_ref, ba_ref, wx_ref, bx_ref, sp):
    vb = v.astype(BF16)
    ra, rx = [], []
    for h in range(NH):
        vh = vb[:, h * HD:(h + 1) * HD]
        ra.append(_dot(vh, wa_ref[h]))
        rx.append(_dot(vh, wx_ref[h]))
    r = _sigmoid(jnp.concatenate(ra, axis=1) + ba_ref[...])
    i = _sigmoid(jnp.concatenate(rx, axis=1) + bx_ref[...])
    log_a = r * ((-LRU_C) * sp)
    a = jnp.exp(log_a)
    one_minus = -jnp.tanh(log_a) * (1.0 + a * a)
    return r, i, a, jnp.sqrt(one_minus), lax.rsqrt(one_minus)


def _mixer_fwd(proj, wg, scale, w_pool_out, conv_w, conv_b, wa, ba, wx, bx, lam, w_rnn_out, exchange=None,
               exchange_operands=(), tm=256):
    S = proj.shape[0]
    UW = DP + 2 * DR

    def body(proj_ref, wg_ref, scale_ref, wpo_ref, cw_ref, cb_ref, wa_ref, ba_ref, wx_ref, bx_ref, lam_ref, wro_ref,
             pm_ref, ypool_ref, hr_ref, z_ref, yrnn_ref, kept_ref, gates_ref, pool_carry, conv_carry, h_carry):
        i = pl.program_id(0)

        @pl.when(i == 0)
        def _():
            pool_carry[...] = jnp.zeros_like(pool_carry)
            conv_carry[...] = jnp.zeros_like(conv_carry)
            h_carry[...] = jnp.zeros_like(h_carry)

        rows = lax.broadcasted_iota(jnp.int32, (tm, 1), 0)
        t_glob = i * tm + rows

        u_pool = proj_ref[:, 0:DP]
        ext = jnp.concatenate([pool_carry[...], u_pool], axis=0)
        pool_carry[...] = u_pool[tm - POOL_HALO:, :]
        sums = _pool_windows(ext, +1)
        mixed = []
        for g, w in enumerate(WINDOWS):
            inv_cnt = 1.0 / jnp.minimum(t_glob + 1, w).astype(F32)
            pooled_g = sums[g][POOL_HALO:, :] * inv_cnt - u_pool[:, g * PG:(g + 1) * PG]
            mixed.append(_dot(pooled_g.astype(BF16), wg_ref[g]))
        pm = (jnp.concatenate(mixed, axis=1) * scale_ref[...]).astype(BF16)
        pm_ref[...] = pm
        ypool_ref[...] = _dot(pm, wpo_ref[...]).astype(BF16)

        u_rnn = proj_ref[:, DP:DP + DR]
        uext = jnp.concatenate([conv_carry[...], u_rnn], axis=0)
        conv_carry[...] = u_rnn[tm - CONV_HALO:, :]
        taps = _conv_taps(uext)
        v = cb_ref[...]
        for k in range(4):
            v = v + taps[k] * cw_ref[k:k + 1, :]
        sp = _softplus_neg(lam_ref[...])
        r, gi, a, mult, _ = _gates(v, wa_ref, ba_ref, wx_ref, bx_ref, sp)
        for k, kept in enumerate((v, a, mult)):
            kept_ref[k] = kept
        for k, kept in enumerate((r, gi)):
            gates_ref[k] = kept.astype(BF16)
        h_carry[0:1, :] = _linear_scan(hr_ref, a, mult * gi * v, h_carry[0:1, :], reverse=False)
        z = (hr_ref[...] * _gelu(proj_ref[:, DP + DR:UW])).astype(BF16)
        z_ref[...] = z
        yrnn_ref[...] = _dot(z, wro_ref[...]).astype(BF16)

    return _call(
        body, "mixer_fwd", (S // tm,),
        in_specs=[_rows(UW, tm), _resident((4, PG, PG)), _resident((1, DP)), _resident((DP, D)), _resident((4, DR)),
                  _resident((1, DR)), _resident((NH, HD, HD)), _resident((1, DR)), _resident((NH, HD, HD)),
                  _resident((1, DR)), _resident((1, DR)), _resident((DR, D))],
        out_specs=[_rows(DP, tm), _rows(D, tm), _rows(DR, tm), _rows(DR, tm), _rows(D, tm),
                   pl.BlockSpec((KEPT, tm, DR), lambda i: (0, i, 0)), pl.BlockSpec((2, tm, DR), lambda i: (0, i, 0))],
        out_shape=[jax.ShapeDtypeStruct((S, DP), BF16),
                   jax.ShapeDtypeStruct((S, D), BF16), jax.ShapeDtypeStruct((S, DR), F32),
                   jax.ShapeDtypeStruct((S, DR), BF16), jax.ShapeDtypeStruct((S, D), BF16),
                   jax.ShapeDtypeStruct((KEPT, S, DR), F32), jax.ShapeDtypeStruct((2, S, DR), BF16)],
        scratch_shapes=[pltpu.VMEM((POOL_HALO, DP), F32), pltpu.VMEM((CONV_HALO, DR), F32), pltpu.VMEM((8, DR), F32)],
        operands=(proj, wg, scale, w_pool_out, conv_w, conv_b, wa, ba, wx, bx, lam, w_rnn_out),
        exchange=exchange, exchange_operands=exchange_operands)


FF_CHUNKS = ((0, 768), (768, 1536), (1536, 2304), (2304, DFF))


def _rms(x):
    r = lax.rsqrt(jnp.mean(x * x, axis=-1, keepdims=True) + EPS)
    return r, x * r


def _rms_bwd(dh, g, r, xh):
    dxh = dh * g
    return r * (dxh - xh * jnp.mean(dxh * xh, axis=-1, keepdims=True))


def _merge_out(x, proj, y_pool, y_rnn, w_o, norm_ffn, exchange=None, exchange_operands=(), tm=512):
    S = x.shape[0]
    GL0 = (DP + 2 * DR) // 512

    def gl_spec(k):
        return pl.BlockSpec((tm, 512), lambda i: (i, GL0 + k))

    def body(x_ref, gl0, gl1, gl2, gl3, yp_ref, yr_ref, wo_ref, gf_ref, mix_ref, x2_ref, h2_ref):
        s_p = _sigmoid(jnp.concatenate([gl0[...], gl1[...]], axis=1))
        s_r = _sigmoid(jnp.concatenate([gl2[...], gl3[...]], axis=1))
        mix = (s_p * yp_ref[...].astype(F32) + s_r * yr_ref[...].astype(F32)).astype(BF16)
        mix_ref[...] = mix
        x2 = x_ref[...] + _dot(mix, wo_ref[...])
        x2_ref[...] = x2
        _, xh2 = _rms(x2)
        h2_ref[...] = (xh2 * gf_ref[...]).astype(BF16)

    return _call(
        body, "merge_out", (S // tm,),
        in_specs=[_rows(D, tm), gl_spec(0), gl_spec(1), gl_spec(2), gl_spec(3), _rows(D, tm), _rows(D, tm),
                  _resident((D, D)), _resident((1, D))],
        out_specs=[_rows(D, tm), _rows(D, tm), _rows(D, tm)],
        out_shape=[jax.ShapeDtypeStruct((S, D), BF16), jax.ShapeDtypeStruct((S, D), F32), jax.ShapeDtypeStruct((S, D), BF16)],
        operands=(x, proj, proj, proj, proj, y_pool, y_rnn, w_o, norm_ffn),
        exchange=exchange, exchange_operands=exchange_operands)


def _ffn_up(h2, w_lo, w_hi, exchange=None, exchange_operands=(), tm=512):
    S = h2.shape[0]
    HALF = D // 2

    def body(h_ref, lo_ref, hi_ref, back_ref, act_ref):
        h_lo, h_hi = h_ref[:, 0:HALF], h_ref[:, HALF:D]
        for c0, c1 in FF_CHUNKS:
            gate = _dot_nt(h_lo, lo_ref[c0:c1, :]) + _dot_nt(h_hi, hi_ref[c0:c1, :])
            up = _dot_nt(h_lo, lo_ref[DFF + c0:DFF + c1, :]) + _dot_nt(h_hi, hi_ref[DFF + c0:DFF + c1, :])
            sg = _sigmoid(gate)
            silu = gate * sg
            back_ref[:, c0:c1] = (up * (sg * (1.0 + gate * (1.0 - sg)))).astype(BF16)
            back_ref[:, DFF + c0:DFF + c1] = silu.astype(BF16)
            act_ref[:, c0:c1] = (silu * up).astype(BF16)

    return _call(
        body, "ffn_up", (S // tm,),
        in_specs=[_rows(D, tm), _resident((2 * DFF, HALF)), _resident((2 * DFF, HALF))],
        out_specs=[_rows(2 * DFF, tm), _rows(DFF, tm)],
        out_shape=[jax.ShapeDtypeStruct((S, 2 * DFF), BF16), jax.ShapeDtypeStruct((S, DFF), BF16)],
        operands=(h2, w_lo, w_hi), exchange=exchange, exchange_operands=exchange_operands)


def _ffn_down_loss(act, x2, target, w_ffn_out, norm_final, tm=512):
    S = act.shape[0]

    def body(act_ref, x2_ref, t_ref, w_ref, gn_ref, dx3_ref, dx3b_ref, loss_ref, dvec_ref):
        i = pl.program_id(0)

        @pl.when(i == 0)
        def _():
            loss_ref[...] = jnp.zeros_like(loss_ref)
            dvec_ref[...] = jnp.zeros_like(dvec_ref)

        x3 = x2_ref[...] + _dot(act_ref[...], w_ref[...])
        r3, xh3 = _rms(x3)
        g_fin = gn_ref[...]
        e = xh3 * g_fin - t_ref[...]
        loss_ref[...] += jnp.sum(e * e, axis=(0, 1), keepdims=True) * (0.5 / D)
        dy = e * (1.0 / D)
        dvec_ref[0:1, :] += jnp.sum(dy * xh3, axis=0, keepdims=True)
        dx3 = _rms_bwd(dy, g_fin, r3, xh3)
        dx3_ref[...] = dx3
        dx3b_ref[...] = dx3.astype(BF16)

    return pl.pallas_call(
        body, name="ffn_down_loss", grid=(S // tm,),
        in_specs=[_rows(DFF, tm), _rows(D, tm), _rows(D, tm), _resident((DFF, D)), _resident((1, D))],
        out_specs=[_rows(D, tm), _rows(D, tm), _resident((1, 1)), _resident((8, D))],
        out_shape=[jax.ShapeDtypeStruct((S, D), F32), jax.ShapeDtypeStruct((S, D), BF16),
                   jax.ShapeDtypeStruct((1, 1), F32), jax.ShapeDtypeStruct((8, D), F32)],
        compiler_params=_params("arbitrary"),
    )(act, x2, target, w_ffn_out, norm_final)


def _ffn_bwd_down(dx3b, gu, w_ffn_out, tm=512):
    S = dx3b.shape[0]

    def body(d_ref, back_ref, w_ref, dgu_ref):
        d = d_ref[...]
        for c0, c1 in FF_CHUNKS:
            dact = _dot_nt(d, w_ref[c0:c1, :])
            dgu_ref[:, c0:c1] = (dact * back_ref[:, c0:c1].astype(F32)).astype(BF16)
            dgu_ref[:, DFF + c0:DFF + c1] = (dact * back_ref[:, DFF + c0:DFF + c1].astype(F32)).astype(BF16)

    return pl.pallas_call(
        body, name="ffn_bwd_down", grid=(S // tm,),
        in_specs=[_rows(D, tm), _rows(2 * DFF, tm), _resident((DFF, D))],
        out_specs=_rows(2 * DFF, tm),
        out_shape=jax.ShapeDtypeStruct((S, 2 * DFF), BF16),
        compiler_params=_params("parallel"),
    )(dx3b, gu, w_ffn_out)


def _ffn_bwd_up(dgu, x2, dx3, w_lo, w_hi, norm_ffn, w_o, tm=512):
    S = dgu.shape[0]
    HALF = D // 2

    def body(dgu_ref, x2_ref, dx3_ref, lo_ref, hi_ref, gf_ref, wo_ref, dx2_ref, dx2b_ref, dmixo_ref, dvec_ref):
        i = pl.program_id(0)

        @pl.when(i == 0)
        def _():
            dvec_ref[...] = jnp.zeros_like(dvec_ref)

        dgate, dup = dgu_ref[:, 0:DFF], dgu_ref[:, DFF:2 * DFF]
        dh2 = jnp.concatenate([_dot(dgate, w[0:DFF, :]) + _dot(dup, w[DFF:2 * DFF, :]) for w in (lo_ref, hi_ref)], axis=1)
        r2, xh2 = _rms(x2_ref[...])
        dvec_ref[0:1, :] += jnp.sum(dh2 * xh2, axis=0, keepdims=True)
        dx2 = dx3_ref[...] + _rms_bwd(dh2, gf_ref[...], r2, xh2)
        dx2_ref[...] = dx2
        dx2b = dx2.astype(BF16)
        dx2b_ref[...] = dx2b
        dmixo_ref[...] = _dot_nt(dx2b, wo_ref[...]).astype(BF16)

    return pl.pallas_call(
        body, name="ffn_bwd_up", grid=(S // tm,),
        in_specs=[_rows(2 * DFF, tm), _rows(D, tm), _rows(D, tm), _resident((2 * DFF, HALF)), _resident((2 * DFF, HALF)),
                  _resident((1, D)), _resident((D, D))],
        out_specs=[_rows(D, tm), _rows(D, tm), _rows(D, tm), _resident((8, D))],
        out_shape=[jax.ShapeDtypeStruct((S, D), F32), jax.ShapeDtypeStruct((S, D), BF16), jax.ShapeDtypeStruct((S, D), BF16),
                   jax.ShapeDtypeStruct((8, D), F32)],
        compiler_params=_params("arbitrary"),
    )(dgu, x2, dx3, w_lo, w_hi, norm_ffn, w_o)


VEC_ROWS = 16
MAT_WA = 4 * PG
MAT_WX = MAT_WA + NH * HD
MAT_ROWS = MAT_WX + NH * HD


def _mixer_bwd(proj, dmixo, y_pool, y_rnn, hr, kept, gates, wg, scale, w_pool_out, conv_w, conv_b, wa, ba, wx, bx, lam, w_rnn_out,
               exchange=None, exchange_operands=(), tm=256):
    S = proj.shape[0]
    nt = S // tm

    def rev(cols):
        return pl.BlockSpec((tm, cols), lambda i: (nt - 1 - i, 0))

    def halo(rows_, cols):
        per = tm // rows_
        return pl.BlockSpec((rows_, cols), lambda i: (jnp.maximum((nt - 1 - i) * per - 1, 0), 0))

    def body(proj_ref, projh_ref, dmixo_ref, yp_ref, yr_ref, hr_ref, hrh_ref, kept_ref, gates_ref, wg_ref, scale_ref, wpo_ref, cw_ref, cb_ref,
             wa_ref, ba_ref, wx_ref, bx_ref, lam_ref, wro_ref,
             dproj_ref, dypb_ref, dyrb_ref, dmat_ref, dvec_ref,
             q_carry, dv_carry, a_carry, g_carry, g_scr):
        i = pl.program_id(0)
        ti = nt - 1 - i

        @pl.when(i == 0)
        def _():
            q_carry[...] = jnp.zeros_like(q_carry)
            dv_carry[...] = jnp.zeros_like(dv_carry)
            a_carry[...] = jnp.zeros_like(a_carry)
            g_carry[...] = jnp.zeros_like(g_carry)
            dmat_ref[...] = jnp.zeros_like(dmat_ref)
            dvec_ref[...] = jnp.zeros_like(dvec_ref)

        rows = lax.broadcasted_iota(jnp.int32, (tm, 1), 0)
        t_glob = ti * tm + rows
        has_prev = (ti > 0).astype(F32)
        dmixo = dmixo_ref[...].astype(F32)

        s_p = _sigmoid(proj_ref[:, DP + 2 * DR:DP + 2 * DR + D])
        s_r = _sigmoid(proj_ref[:, DP + 2 * DR + D:DIN])
        dproj_ref[:, DP + 2 * DR:DP + 2 * DR + D] = (dmixo * yp_ref[...].astype(F32) * s_p * (1.0 - s_p)).astype(BF16)
        dproj_ref[:, DP + 2 * DR + D:DIN] = (dmixo * yr_ref[...].astype(F32) * s_r * (1.0 - s_r)).astype(BF16)
        dyp = (dmixo * s_p).astype(BF16)
        dyr = (dmixo * s_r).astype(BF16)
        dypb_ref[...] = dyp
        dyrb_ref[...] = dyr

        dz = _dot_nt(dyr, wro_ref[...])
        u_gate = proj_ref[:, DP + DR:DP + 2 * DR]
        gg, dgelu = _gelu(u_gate, with_grad=True)
        hr_t = hr_ref[...]
        dproj_ref[:, DP + DR:DP + 2 * DR] = (dz * hr_t * dgelu).astype(BF16)
        dhr = dz * gg

        sp = _softplus_neg(lam_ref[...])
        v, a, mult = (kept_ref[k] for k in range(KEPT))
        r, gi = (gates_ref[k].astype(F32) for k in range(2))
        inv_mult = 1.0 / mult

        C = jnp.where(rows == tm - 1, a_carry[0:1, :], pltpu.roll(a, tm - 1, axis=0))
        g_carry[0:1, :] = _linear_scan(g_scr, C, dhr, g_carry[0:1, :], reverse=True)
        a_carry[0:1, :] = a[0:1, :]
        g = g_scr[...]

        h_prev = jnp.where(rows == 0, hrh_ref[7:8, :] * has_prev, pltpu.roll(hr_t, 1, axis=0))
        da = g * h_prev
        gm = g * mult
        dmult = g * gi * v
        di = gm * v
        dv = gm * gi
        dlog_a = da * a - dmult * (a * a * inv_mult)
        dvec_ref[4:5, :] += jnp.sum(dlog_a * r, axis=0, keepdims=True)
        dra = (dlog_a * ((-LRU_C) * sp) * r * (1.0 - r))
        drx = di * gi * (1.0 - gi)
        dvec_ref[2:3, :] += jnp.sum(dra, axis=0, keepdims=True)
        dvec_ref[3:4, :] += jnp.sum(drx, axis=0, keepdims=True)
        drab = dra.astype(BF16)
        drxb = drx.astype(BF16)
        vb = v.astype(BF16)
        dvg = []
        for h in range(NH):
            sl = slice(h * HD, (h + 1) * HD)
            dvg.append(_dot_nt(drab[:, sl], wa_ref[h]) + _dot_nt(drxb[:, sl], wx_ref[h]))
            dmat_ref[MAT_WA + h * HD:MAT_WA + (h + 1) * HD, :] += _dot_tn(vb[:, sl], drab[:, sl])
            dmat_ref[MAT_WX + h * HD:MAT_WX + (h + 1) * HD, :] += _dot_tn(vb[:, sl], drxb[:, sl])
        dv = dv + jnp.concatenate(dvg, axis=1)
        dvec_ref[1:2, :] += jnp.sum(dv, axis=0, keepdims=True)
        dvext = jnp.concatenate([dv, dv_carry[...]], axis=0)
        dv_carry[...] = dv[0:CONV_HALO, :]
        n = tm + CONV_HALO
        u_rnn = proj_ref[:, DP:DP + DR]
        du_rnn = dv * cw_ref[3:4, :]
        dvec_ref[8:9, :] += jnp.sum(dv * u_rnn, axis=0, keepdims=True)
        for k in range(3):
            dv_k = pltpu.roll(dvext, n - (3 - k), axis=0)[0:tm, :]
            du_rnn = du_rnn + dv_k * cw_ref[k:k + 1, :]
            dvec_ref[5 + k:6 + k, :] += jnp.sum(dv_k * u_rnn, axis=0, keepdims=True)
        dproj_ref[:, DP:DP + DR] = du_rnn.astype(BF16)

        dpm = _dot_nt(dyp, wpo_ref[...])
        u_pool = proj_ref[:, 0:DP]
        ext = jnp.concatenate([projh_ref[:, 0:DP] * has_prev, u_pool], axis=0)
        sums = _pool_windows(ext, +1)
        scale_v = scale_ref[...]
        qs = []
        dpooled = []
        dscale = []
        for gi_, w in enumerate(WINDOWS):
            sl = slice(gi_ * PG, (gi_ + 1) * PG)
            inv_cnt = 1.0 / jnp.minimum(t_glob + 1, w).astype(F32)
            pooled_b = (sums[gi_][POOL_HALO:, :] * inv_cnt - u_pool[:, sl]).astype(BF16)
            mixed_g = _dot(pooled_b, wg_ref[gi_])
            dscale.append(jnp.sum(dpm[:, sl] * mixed_g, axis=0, keepdims=True))
            dmixed_b = (dpm[:, sl] * scale_v[:, sl]).astype(BF16)
            dmat_ref[gi_ * PG:(gi_ + 1) * PG, :] += _dot_tn(pooled_b, dmixed_b)
            dp_g = _dot_nt(dmixed_b, wg_ref[gi_])
            dpooled.append(dp_g)
            qs.append(dp_g * inv_cnt)
        dvec_ref[0:1, 0:DP] += jnp.concatenate(dscale, axis=1)
        q = jnp.concatenate(qs, axis=1)
        qext = jnp.concatenate([q, q_carry[...]], axis=0)
        q_carry[...] = q[0:POOL_HALO, :]
        tsum = _pool_windows(qext, -1)
        for gi_ in range(4):
            dproj_ref[:, gi_ * PG:(gi_ + 1) * PG] = (tsum[gi_][0:tm, :] - dpooled[gi_]).astype(BF16)

        @pl.when(i == nt - 1)
        def _():
            dvec_ref[4:5, :] = dvec_ref[4:5, :] * (LRU_C * _sigmoid(-lam_ref[...]))

    return _call(
        body, "mixer_bwd", (nt,),
        in_specs=[rev(DIN), halo(POOL_HALO, DIN), rev(D), rev(D), rev(D), rev(DR), halo(8, DR),
                  pl.BlockSpec((KEPT, tm, DR), lambda i: (0, nt - 1 - i, 0)),
                  pl.BlockSpec((2, tm, DR), lambda i: (0, nt - 1 - i, 0)), _resident((4, PG, PG)), _resident((1, DP)), _resident((DP, D)), _resident((4, DR)), _resident((1, DR)),
                  _resident((NH, HD, HD)), _resident((1, DR)), _resident((NH, HD, HD)), _resident((1, DR)),
                  _resident((1, DR)), _resident((DR, D))],
        out_specs=[rev(DIN), rev(D), rev(D), _resident((MAT_ROWS, HD)), _resident((VEC_ROWS, DR))],
        out_shape=[jax.ShapeDtypeStruct((S, DIN), BF16), jax.ShapeDtypeStruct((S, D), BF16),
                   jax.ShapeDtypeStruct((S, D), BF16), jax.ShapeDtypeStruct((MAT_ROWS, HD), F32),
                   jax.ShapeDtypeStruct((VEC_ROWS, DR), F32)],
        scratch_shapes=[pltpu.VMEM((POOL_HALO, DP), F32), pltpu.VMEM((CONV_HALO, DR), F32), pltpu.VMEM((8, DR), F32),
                        pltpu.VMEM((8, DR), F32), pltpu.VMEM((tm, DR), F32)],
        operands=(proj, proj, dmixo, y_pool, y_rnn, hr, hr, kept, gates, wg, scale, w_pool_out, conv_w, conv_b, wa, ba, wx, bx, lam,
                  w_rnn_out),
        exchange=exchange, exchange_operands=exchange_operands)


def _in_bwd(dproj, x, dx2, norm_mix, w_in, exchange=None, exchange_operands=(), tm=512):
    S = x.shape[0]

    def body(dp_ref, x_ref, dx2_ref, g_ref, w_ref, dx_ref, dg_ref):
        i = pl.program_id(0)

        @pl.when(i == 0)
        def _():
            dg_ref[...] = jnp.zeros_like(dg_ref)

        dh = _dot(dp_ref[:, 0:1536], w_ref[0:1536, :])
        dh = dh + _dot(dp_ref[:, 1536:3072], w_ref[1536:3072, :])
        dh = dh + _dot(dp_ref[:, 3072:DIN], w_ref[3072:DIN, :])
        xv = x_ref[...]
        r = lax.rsqrt(jnp.mean(xv * xv, axis=-1, keepdims=True) + EPS)
        xh = xv * r
        dg_ref[0:1, :] += jnp.sum(dh * xh, axis=0, keepdims=True)
        dxh = dh * g_ref[...]
        dx_ref[...] = dx2_ref[...] + r * (dxh - xh * jnp.mean(dxh * xh, axis=-1, keepdims=True))

    return _call(
        body, "in_bwd", (S // tm,),
        in_specs=[_rows(DIN, tm), _rows(D, tm), _rows(D, tm), _resident((1, D)), _resident((DIN, D))],
        out_specs=[_rows(D, tm), _resident((8, D))],
        out_shape=[jax.ShapeDtypeStruct((S, D), F32), jax.ShapeDtypeStruct((8, D), F32)],
        operands=(dproj, x, dx2, norm_mix, w_in), exchange=exchange, exchange_operands=exchange_operands)


def _wgrad(a, b, name, tk, tn, exchange=None, exchange_operands=()):
    S, K = a.shape
    N = b.shape[1]

    def body(a_ref, b_ref, o_ref):
        o_ref[...] = _dot_tn(a_ref[...], b_ref[...]).astype(BF16)

    (out,), exchanged = _call(
        body, name, (K // tk, N // tn),
        in_specs=[pl.BlockSpec((S, tk), lambda k, n: (0, k)), pl.BlockSpec((S, tn), lambda k, n: (0, n))],
        out_specs=[pl.BlockSpec((tk, tn), lambda k, n: (k, n))],
        out_shape=[jax.ShapeDtypeStruct((K, N), BF16)],
        operands=(a, b), exchange=exchange, exchange_operands=exchange_operands)
    return (out, exchanged) if exchange is not None else out


VEC_SCALE, VEC_CONV_B, VEC_BA, VEC_BX, VEC_LAM, VEC_CONV_W, VEC_NORM_FINAL, VEC_NORM_FFN = 0, 1, 2, 3, 4, 5, 9, 10
VEC_LOSS = 11


class _Big:
    def __init__(self, name, rows, cols, axis, n, dtype=BF16, transposed=False, src_cols=None):
        self.name, self.rows, self.cols, self.axis, self.n, self.dtype = name, rows, cols, axis, n, dtype
        self.transposed = transposed
        self.src_cols = src_cols
        self.block_shape = (rows, n) if axis == 1 else (n, cols)

    def block(self, ref, p):
        if self.axis == 1:
            return ref.at[:, pl.ds(pl.multiple_of(p * self.n, 128), self.n)]
        return ref.at[pl.ds(pl.multiple_of(p * self.n, 16 if self.dtype == BF16 else 8), self.n), :]


BIG = (_Big("w_in", DIN, D, 0, DIN // 8, transposed=True), _Big("w_pool_out", DP, D, 1, D // 8),
       _Big("w_rnn_out", DR, D, 0, DR // 8), _Big("w_o", D, D, 0, D // 8),
       _Big("w_ffn_in", 2 * DFF, D, 0, 2 * DFF // 8, transposed=True), _Big("w_ffn_out", DFF, D, 0, DFF // 8))
CONV_W = _Big("conv_w", 8, DR, 1, DR // 8, F32)
W_FFN_IN_HALVES = (_Big("w_ffn_in_lo", 2 * DFF, D // 2, 0, 2 * DFF // 8, src_cols=(0, D // 2)),
                   _Big("w_ffn_in_hi", 2 * DFF, D // 2, 0, 2 * DFF // 8, src_cols=(D // 2, D)))
GATHERED = BIG + (CONV_W,) + W_FFN_IN_HALVES

HBM_SPEC = pl.BlockSpec(memory_space=pl.ANY)
VMEM_SPEC = pl.BlockSpec(memory_space=pltpu.VMEM)


def _place():
    x, y, c = (lax.axis_index(a) for a in MESH_AXES)
    other_chips = [(1 - x, y), (x, 1 - y), (1 - x, 1 - y)]
    return x, y, c, other_chips


def _remote(src, dst, send_sems, recv_sems, idx, to):
    return pltpu.make_async_remote_copy(src_ref=src, dst_ref=dst, send_sem=send_sems.at[idx], recv_sem=recv_sems.at[idx],
                                        device_id=to, device_id_type=MESH)


def _device_index(chip, core):
    return 4 * chip[0] + 2 * chip[1] + core


class _Gather:
    def __init__(self, tensors):
        self.tensors = tuple(tensors)
        n = len(self.tensors)
        self.in_specs = [HBM_SPEC] * n
        self.out_specs = [HBM_SPEC] * n
        self.out_shape = [jax.ShapeDtypeStruct((T.rows, T.cols), T.dtype) for T in self.tensors]
        self.scratch_shapes = [pltpu.VMEM(T.block_shape, T.dtype) for T in self.tensors] + [
            pltpu.VMEM(T.block_shape, F32) for T in self.tensors] + [
            pltpu.SemaphoreType.DMA((n, 7)), pltpu.SemaphoreType.DMA((n, 7)), pltpu.SemaphoreType.DMA((n, 2))]

    def middles(self, steps):
        return [(steps // 2, self.relay), (steps - 1, self.middle)]

    def _copies(self, ins, outs, scratch):
        n = len(self.tensors)
        mine, raw, (send_sems, recv_sems, loc_sems) = scratch[:n], scratch[n:2 * n], scratch[2 * n:]
        x, y, c, chips = _place()
        sibling = (x, y, 1 - c)
        me = _device_index((x, y), c)
        relay_from = (jnp.where(c == 0, 1 - x, x), jnp.where(c == 0, y, 1 - y))
        relay_to = (jnp.where(c == 0, x, 1 - x), jnp.where(c == 0, 1 - y, y))
        loads, stores, first, relays, passed, arrivals, late = [], [], [], [], [], [], []
        for t, T in enumerate(self.tensors):
            place = T.block(outs[t], me)
            src = ins[t] if T.src_cols is None else ins[t].at[:, T.src_cols[0]:T.src_cols[1]]
            loads.append(pltpu.make_async_copy(src, raw[t], loc_sems.at[t, 0]))
            stores.append(pltpu.make_async_copy(mine[t], place, loc_sems.at[t, 1]))
            first.append(_remote(mine[t], place, send_sems, recv_sems, (t, 0), sibling))
            theirs = T.block(outs[t], _device_index((x, y), 1 - c))
            late.append(_remote(theirs, theirs, send_sems, recv_sems, (t, 0), sibling))
            relayed = T.block(outs[t], _device_index(relay_from, c))
            relays.append(_remote(relayed, relayed, send_sems, recv_sems, (t, 3), (*relay_to, c)))
            for k, chip in enumerate(chips):
                if k < 2:
                    first.append(_remote(mine[t], place, send_sems, recv_sems, (t, 1 + k), (*chip, c)))
                land = T.block(outs[t], _device_index(chip, c))
                arrivals.append(_remote(land, land, send_sems, recv_sems, (t, 1 + k), sibling))
                passed.append(_remote(land, land, send_sems, recv_sems, (t, 4 + k), sibling))
                theirs = T.block(outs[t], _device_index(chip, 1 - c))
                late.append(_remote(theirs, theirs, send_sems, recv_sems, (t, 4 + k), sibling))
        return loads, stores, first, relays, passed, arrivals, late

    def start(self, ins, outs, scratch):
        loads, stores, first, _, _, _, _ = self._copies(ins, outs, scratch)
        n = len(self.tensors)
        for cp in loads:
            cp.start()
        for t, cp in enumerate(loads):
            cp.wait()
            scratch[t][...] = scratch[n + t][...].astype(self.tensors[t].dtype)
        for cp in stores + first:
            cp.start()

    def relay(self, ins, outs, scratch, skip=0):
        _, _, _, relays, passed, arrivals, _ = self._copies(ins, outs, scratch)
        for t in range(skip, len(self.tensors)):
            arrivals[3 * t].wait_recv()
            arrivals[3 * t + 1].wait_recv()
            for cp in (relays[t], passed[3 * t], passed[3 * t + 1]):
                cp.start()

    def middle(self, ins, outs, scratch, skip=0):
        _, _, _, _, passed, arrivals, _ = self._copies(ins, outs, scratch)
        for t in range(skip, len(self.tensors)):
            arrivals[3 * t + 2].wait_recv()
            passed[3 * t + 2].start()

    def finish(self, ins, outs, scratch, skip=0):
        _, stores, first, relays, passed, _, late = self._copies(ins, outs, scratch)
        for cp in late[4 * skip:]:
            cp.wait_recv()
        for cp in first + relays + passed:
            cp.wait_send()
        for cp in stores[skip:]:
            cp.wait()


def _in_proj_gather(x, norm_mix, blocks, tensors, order, tm=1024):
    S = x.shape[0]
    nt = S // tm
    n = len(tensors)
    gather = _Gather(tensors)
    CB = 2 * tensors[0].n

    def body(order_ref, x_ref, g_ref, *refs):
        ins, (proj_ref, h_ref), outs = refs[:n], refs[n:n + 2], refs[n + 2:2 * n + 2]
        (h_all, w_chip, w_sem), scratch = refs[2 * n + 2:2 * n + 5], refs[2 * n + 5:]
        q, i = pl.program_id(0), pl.program_id(1)
        _, stores, _, relays, passed, arrivals, late = gather._copies(ins, outs, scratch)

        def fetch(turn):
            rows = outs[0].at[pl.ds(pl.multiple_of(order_ref[turn] * CB, 16), CB), :]
            cp = pltpu.make_async_copy(rows, w_chip, w_sem)
            cp.start()
            cp.wait()

        @pl.when((q == 0) & (i == 0))
        def _():
            gather.start(ins, outs, scratch)
            late[0].wait_recv()
            stores[0].wait()
            fetch(0)

        @pl.when((q == 1) & (i == 0))
        def _():
            arrivals[0].wait_recv()
            arrivals[1].wait_recv()
            for cp in (relays[0], passed[0], passed[1]):
                cp.start()
            late[1].wait_recv()
            fetch(1)

        @pl.when((q == 2) & (i == 0))
        def _():
            late[2].wait_recv()
            fetch(2)
            gather.relay(ins, outs, scratch, skip=1)

        @pl.when((q == 3) & (i == 0))
        def _():
            arrivals[2].wait_recv()
            passed[2].start()
            late[3].wait_recv()
            fetch(3)

        rows = pl.ds(pl.multiple_of(i * tm, tm), tm)

        @pl.when(q == 0)
        def _():
            xv = x_ref[...]
            r = lax.rsqrt(jnp.mean(xv * xv, axis=-1, keepdims=True) + EPS)
            h = (xv * r * g_ref[...]).astype(BF16)
            h_all[rows, :] = h
            h_ref[...] = h

        proj_ref[...] = _dot_nt(h_all[rows, :], w_chip[...])

        @pl.when((q == 3) & (i == nt - 1))
        def _():
            gather.middle(ins, outs, scratch, skip=1)
            gather.finish(ins, outs, scratch, skip=1)

    row_tile = lambda q, i, order: (jnp.where(q == 0, i, nt - 1), 0)
    whole = lambda shape: pl.BlockSpec(shape, lambda q, i, order: (0,) * len(shape), pipeline_mode=pl.Buffered(1))
    outs = pl.pallas_call(
        body, name="in_proj_gather",
        grid_spec=pltpu.PrefetchScalarGridSpec(
            num_scalar_prefetch=1, grid=(4, nt),
            in_specs=[pl.BlockSpec((tm, D), row_tile), whole((1, D))] + gather.in_specs,
            out_specs=[pl.BlockSpec((tm, CB), lambda q, i, order: (i, order[q])), pl.BlockSpec((tm, D), row_tile)]
            + gather.out_specs,
            scratch_shapes=[pltpu.VMEM((S, D), BF16), pltpu.VMEM((CB, D), BF16), pltpu.SemaphoreType.DMA]
            + gather.scratch_shapes),
        out_shape=[jax.ShapeDtypeStruct((S, DIN), F32), jax.ShapeDtypeStruct((S, D), BF16)] + gather.out_shape,
        compiler_params=_params("arbitrary", "arbitrary"),
    )(order, x, norm_mix, *blocks)
    return outs[:2], outs[2:]


PAIR_ROWS = 32


def _pair_reduce(grads, tensors, name):
    nt = len(tensors)

    def body(*refs):
        ins, own_out, sums_out, landed, mine = (refs[k * nt:(k + 1) * nt] for k in range(5))
        send_sems, recv_sems, loc_sems = refs[5 * nt:]
        x, y, c, chips = _place()
        chip_of = [2 * chip[0] + chip[1] for chip in chips]
        swaps, loads = [], []
        for t, T in enumerate(tensors):
            for j in range(4):
                swaps.append(_remote(T.block(ins[t], 2 * j + 1 - c), landed[t].at[j], send_sems, recv_sems, (t, j),
                                     (x, y, 1 - c)))
            for k in range(3):
                loads.append(pltpu.make_async_copy(T.block(ins[t], 2 * chip_of[k] + c), mine[t].at[k], loc_sems.at[t, k]))
        for cp in swaps + loads:
            cp.start()
        for cp in loads:
            cp.wait()
        for cp in swaps:
            cp.wait_recv()
        stores = []
        for t, T in enumerate(tensors):
            for k in range(3):
                acc, got = mine[t].at[k], landed[t].at[chip_of[k]]

                def add(i, carry, acc=acc, got=got):
                    rows = pl.ds(pl.multiple_of(i * PAIR_ROWS, PAIR_ROWS), PAIR_ROWS)
                    acc[rows, :] = (acc[rows, :].astype(F32) + got[rows, :].astype(F32)).astype(BF16)
                    return carry

                lax.fori_loop(0, T.block_shape[0] // PAIR_ROWS, add, 0)
            stores.append(pltpu.make_async_copy(mine[t], sums_out[t], loc_sems.at[t, 3]))
            stores.append(pltpu.make_async_copy(landed[t].at[2 * x + y], own_out[t], loc_sems.at[t, 4]))
        for cp in stores:
            cp.start()
        for cp in swaps:
            cp.wait_send()
        for cp in stores:
            cp.wait()

    blocks = [T.block_shape for T in tensors]
    return pl.pallas_call(
        body, name=name,
        in_specs=[HBM_SPEC] * nt, out_specs=[HBM_SPEC] * (2 * nt),
        out_shape=[jax.ShapeDtypeStruct(b, BF16) for b in blocks] + [jax.ShapeDtypeStruct((3,) + b, BF16) for b in blocks],
        scratch_shapes=[pltpu.VMEM((4,) + b, BF16) for b in blocks] + [pltpu.VMEM((3,) + b, BF16) for b in blocks]
        + [pltpu.SemaphoreType.DMA((nt, 4)), pltpu.SemaphoreType.DMA((nt, 4)), pltpu.SemaphoreType.DMA((nt, 5))],
        compiler_params=pltpu.CompilerParams(vmem_limit_bytes=VMEM_LIMIT),
    )(*grads)


class _Scatter:
    def __init__(self, tensors):
        self.tensors = tuple(tensors)
        n = len(self.tensors)
        self.in_specs = [HBM_SPEC] * n
        self.out_specs = [HBM_SPEC] * n
        self.out_shape = [jax.ShapeDtypeStruct((2,) + T.block_shape, BF16) for T in self.tensors]
        self.scratch_shapes = [pltpu.VMEM(T.block_shape, BF16) for T in self.tensors] * 2 + [
            pltpu.SemaphoreType.DMA((n, 3)), pltpu.SemaphoreType.DMA((n, 3)), pltpu.SemaphoreType.DMA((n,))]

    def middles(self, steps):
        return [(steps // 2, self.middle)]

    def _copies(self, ins, outs, scratch):
        n = len(self.tensors)
        landed, mine, (send_sems, recv_sems, loc_sems) = scratch[:n], scratch[n:2 * n], scratch[2 * n:]
        x, y, c, _ = _place()
        direct = (jnp.where(c == 0, 1 - x, x), jnp.where(c == 0, y, 1 - y), c)
        other = (jnp.where(c == 0, x, 1 - x), jnp.where(c == 0, 1 - y, y), c)
        k_direct = jnp.where(c == 0, 0, 1)
        to_direct, legs, loads, combined, arrivals = [], [], [], [], []
        for t in range(n):
            to_direct.append(_remote(ins[t].at[k_direct], outs[t].at[0], send_sems, recv_sems, (t, 0), direct))
            legs.append(_remote(ins[t].at[2], landed[t], send_sems, recv_sems, (t, 2), direct))
            loads.append(pltpu.make_async_copy(ins[t].at[1 - k_direct], mine[t], loc_sems.at[t]))
            combined.append(_remote(mine[t], outs[t].at[1], send_sems, recv_sems, (t, 1), other))
            arrivals.append(_remote(landed[t], landed[t], send_sems, recv_sems, (t, 2), direct))
        return to_direct, legs, loads, combined, arrivals, landed, mine

    def start(self, ins, outs, scratch):
        to_direct, legs, loads, _, _, _, _ = self._copies(ins, outs, scratch)
        for cp in to_direct + legs + loads:
            cp.start()

    def middle(self, ins, outs, scratch):
        _, _, loads, combined, arrivals, landed, mine = self._copies(ins, outs, scratch)
        for t, T in enumerate(self.tensors):
            loads[t].wait()
            arrivals[t].wait_recv()
            acc, got = mine[t], landed[t]

            def add(i, carry, acc=acc, got=got):
                rows = pl.ds(pl.multiple_of(i * PAIR_ROWS, PAIR_ROWS), PAIR_ROWS)
                acc[rows, :] = (acc[rows, :].astype(F32) + got[rows, :].astype(F32)).astype(BF16)
                return carry

            lax.fori_loop(0, T.block_shape[0] // PAIR_ROWS, add, 0)
            combined[t].start()

    def finish(self, ins, outs, scratch):
        to_direct, legs, _, combined, _, _, _ = self._copies(ins, outs, scratch)
        for cp in to_direct + combined:
            cp.wait()
        for cp in legs:
            cp.wait_send()


def _adamw(w, g, m, v):
    m = ADAM_B1 * m + (1.0 - ADAM_B1) * g
    v = ADAM_B2 * v + (1.0 - ADAM_B2) * (g * g)
    m_hat = m / (1.0 - ADAM_B1 ** ADAM_STEP)
    v_hat = v / (1.0 - ADAM_B2 ** ADAM_STEP)
    delta = -ADAM_LR * (m_hat / (jnp.sqrt(v_hat) + ADAM_EPS) + ADAM_WD * w)
    return delta, m, v


def _final_sum(T, g, lz1, lz2, where, w, m, v):
    rows, cols = T.block_shape
    sub = 4 if T.axis == 0 and rows % 64 == 0 and rows > 256 else 1
    blk = (rows // sub, cols)

    def body(where_ref, g_ref, l1_ref, l2_ref, w_ref, m_ref, v_ref, g_out, d_out, m_out, v_out):
        tot = g_ref[...].astype(F32) + l1_ref[...].astype(F32)
        for k in range(2):
            tot = tot + l2_ref[k].astype(F32)
        g_out[...] = tot
        d_out[...], m_out[...], v_out[...] = _adamw(w_ref[...], tot, m_ref[...], v_ref[...])

    def in_whole(r, wh):
        p = wh[0]
        return (0, p) if T.axis == 1 else (p * sub + r, 0)

    own = pl.BlockSpec(blk, lambda r, wh: (r, 0))
    return pl.pallas_call(
        body, name="grad_final_" + T.name,
        grid_spec=pltpu.PrefetchScalarGridSpec(
            num_scalar_prefetch=1, grid=(sub,),
            in_specs=[pl.BlockSpec(blk, in_whole),
                      own,
                      pl.BlockSpec((2,) + blk, lambda r, wh: (0, r, 0)), own, own, own],
            out_specs=[own] * 4),
        out_shape=[jax.ShapeDtypeStruct(T.block_shape, F32)] * 4,
        compiler_params=_params("arbitrary"),
    )(where, g, lz1, lz2, w, m, v)


VEC_PIECE = DR // 8


class _AllReduce:
    def __init__(self, items):
        self.items = tuple(items)
        n = len(self.items)
        self.in_specs = [HBM_SPEC] * n
        self.out_specs = [HBM_SPEC] * n
        self.out_shape = [jax.ShapeDtypeStruct(shape, F32) for shape, _ in self.items]
        pieces = [(shape[0] // 8, shape[1]) if axis == 0 else (shape[0], shape[1] // 8) for shape, axis in self.items]
        self.scratch_shapes = ([pltpu.VMEM((8,) + p, F32) for p in pieces] + [pltpu.VMEM(p, F32) for p in pieces] + [
            pltpu.SemaphoreType.DMA((2 * n, 8)), pltpu.SemaphoreType.DMA((2 * n, 8)), pltpu.SemaphoreType.DMA((2 * n,))])

    def middles(self, steps):
        return [(steps // 2, self.middle)]

    def _copies(self, ins, outs, scratch):
        n = len(self.items)
        landed, sums, (send_sems, recv_sems, loc_sems) = scratch[:n], scratch[n:2 * n], scratch[2 * n:]
        x, y, c, _ = _place()
        me = _device_index((x, y), c)

        def peer(r):
            return (1 - x if r & 4 else x, 1 - y if r & 2 else y, 1 - c if r & 1 else c)

        def piece(i, ref, p):
            shape, axis = self.items[i]
            if axis == 0:
                rows = shape[0] // 8
                return ref.at[pl.ds(pl.multiple_of(p * rows, 8), rows), :]
            cols = shape[1] // 8
            return ref.at[:, pl.ds(pl.multiple_of(p * cols, 128), cols)]

        own, scatter, arrivals, keep, spread, late = [], [], [], [], [], []
        for i in range(n):
            own.append(pltpu.make_async_copy(piece(i, ins[i], me), landed[i].at[0], loc_sems.at[2 * i]))
            keep.append(pltpu.make_async_copy(sums[i], piece(i, outs[i], me), loc_sems.at[2 * i + 1]))
            for r in range(1, 8):
                to = peer(r)
                p = _device_index(to[:2], to[2])
                scatter.append(_remote(piece(i, ins[i], p), landed[i].at[r], send_sems, recv_sems, (2 * i, r), to))
                spread.append(_remote(sums[i], piece(i, outs[i], me), send_sems, recv_sems, (2 * i + 1, r), to))
                late.append(_remote(sums[i], piece(i, outs[i], p), send_sems, recv_sems, (2 * i + 1, r), to))
        return own, scatter, keep, spread, late, landed, sums

    def start(self, ins, outs, scratch):
        own, scatter, _, _, _, _, _ = self._copies(ins, outs, scratch)
        for cp in own + scatter:
            cp.start()

    def middle(self, ins, outs, scratch):
        own, scatter, keep, spread, _, landed, sums = self._copies(ins, outs, scratch)
        for cp in own:
            cp.wait()
        for cp in scatter:
            cp.wait_recv()
        for i in range(len(self.items)):
            total = landed[i][0]
            for r in range(1, 8):
                total = total + landed[i][r]
            sums[i][...] = total
        for cp in keep + spread:
            cp.start()

    def finish(self, ins, outs, scratch):
        _, scatter, keep, spread, late, _, _ = self._copies(ins, outs, scratch)
        for cp in late:
            cp.wait_recv()
        for cp in scatter + spread:
            cp.wait_send()
        for cp in keep:
            cp.wait()


class _Both:
    def __init__(self, a, b):
        self.a, self.b = a, b
        self.in_specs, self.out_specs = a.in_specs + b.in_specs, a.out_specs + b.out_specs
        self.out_shape, self.scratch_shapes = a.out_shape + b.out_shape, a.scratch_shapes + b.scratch_shapes

    def _each(self, ins, outs, scratch):
        a = self.a
        i, o, s = len(a.in_specs), len(a.out_specs), len(a.scratch_shapes)
        return (a, ins[:i], outs[:o], scratch[:s]), (self.b, ins[i:], outs[o:], scratch[s:])

    def middles(self, steps):
        def of(which, middle):
            return lambda ins, outs, scratch: middle(*self._each(ins, outs, scratch)[which][1:])
        return [(at, of(which, middle)) for which, e in enumerate((self.a, self.b)) for at, middle in e.middles(steps)]

    def start(self, ins, outs, scratch):
        for e, i, o, s in self._each(ins, outs, scratch):
            e.start(i, o, s)

    def finish(self, ins, outs, scratch):
        for e, i, o, s in self._each(ins, outs, scratch):
            e.finish(i, o, s)


def _all_reduce(arrays, items, name):
    reduce = _AllReduce(items)
    n = len(items)

    def body(*refs):
        ins, outs, scratch = refs[:n], refs[n:2 * n], refs[2 * n:]
        reduce.start(ins, outs, scratch)
        reduce.middle(ins, outs, scratch)
        reduce.finish(ins, outs, scratch)

    return pl.pallas_call(
        body, name=name, in_specs=reduce.in_specs, out_specs=reduce.out_specs, out_shape=reduce.out_shape,
        scratch_shapes=reduce.scratch_shapes,
    )(*arrays)


def _adam_small(grads, wmv):
    n = len(grads)

    def body(*refs):
        g_refs, rest = refs[:n], refs[n:]
        ins, outs = rest[:3 * n], rest[3 * n:]
        for i in range(n):
            d, m, v = _adamw(ins[3 * i][...], g_refs[i][...], ins[3 * i + 1][...], ins[3 * i + 2][...])
            outs[3 * i][...], outs[3 * i + 1][...], outs[3 * i + 2][...] = d, m, v

    flat = [a for t in wmv for a in t]
    return pl.pallas_call(
        body, name="adam_small",
        in_specs=[VMEM_SPEC] * (4 * n), out_specs=[VMEM_SPEC] * (3 * n),
        out_shape=[jax.ShapeDtypeStruct(a.shape, F32) for a in flat],
    )(*grads, *flat)


WEIGHT_NAMES = ("norm_mix", "w_in", "w_pool_grp", "pool_scale", "w_pool_out", "conv_w", "conv_b", "w_rg_a", "b_rg_a", "w_rg_x",
                "b_rg_x", "lru_lambda", "w_rnn_out", "w_o", "norm_ffn", "w_ffn_in", "w_ffn_out", "norm_final")


def kernel(x, norm_mix, w_in, w_pool_grp, pool_scale, w_pool_out, conv_w, conv_b, w_rg_a, b_rg_a, w_rg_x, b_rg_x, lru_lambda, w_rnn_out, w_o, norm_ffn, w_ffn_in, w_ffn_out, norm_final, loss_target, m_norm_mix, m_w_in, m_w_pool_grp, m_pool_scale, m_w_pool_out, m_conv_w, m_conv_b, m_w_rg_a, m_b_rg_a, m_w_rg_x, m_b_rg_x, m_lru_lambda, m_w_rnn_out, m_w_o, m_norm_ffn, m_w_ffn_in, m_w_ffn_out, m_norm_final, v_norm_mix, v_w_in, v_w_pool_grp, v_pool_scale, v_w_pool_out, v_conv_w, v_conv_b, v_w_rg_a, v_b_rg_a, v_w_rg_x, v_b_rg_x, v_lru_lambda, v_w_rnn_out, v_w_o, v_norm_ffn, v_w_ffn_in, v_w_ffn_out, v_norm_final):
    w = dict(norm_mix=norm_mix, w_in=w_in, w_pool_grp=w_pool_grp, pool_scale=pool_scale, w_pool_out=w_pool_out, conv_w=conv_w,
             conv_b=conv_b, w_rg_a=w_rg_a, b_rg_a=b_rg_a, w_rg_x=w_rg_x, b_rg_x=b_rg_x, lru_lambda=lru_lambda,
             w_rnn_out=w_rnn_out, w_o=w_o, norm_ffn=norm_ffn, w_ffn_in=w_ffn_in, w_ffn_out=w_ffn_out, norm_final=norm_final)
    m = dict(norm_mix=m_norm_mix, w_in=m_w_in, w_pool_grp=m_w_pool_grp, pool_scale=m_pool_scale, w_pool_out=m_w_pool_out,
             conv_w=m_conv_w, conv_b=m_conv_b, w_rg_a=m_w_rg_a, b_rg_a=m_b_rg_a, w_rg_x=m_w_rg_x, b_rg_x=m_b_rg_x,
             lru_lambda=m_lru_lambda, w_rnn_out=m_w_rnn_out, w_o=m_w_o, norm_ffn=m_norm_ffn, w_ffn_in=m_w_ffn_in,
             w_ffn_out=m_w_ffn_out, norm_final=m_norm_final)
    v = dict(norm_mix=v_norm_mix, w_in=v_w_in, w_pool_grp=v_w_pool_grp, pool_scale=v_pool_scale, w_pool_out=v_w_pool_out,
             conv_w=v_conv_w, conv_b=v_conv_b, w_rg_a=v_w_rg_a, b_rg_a=v_b_rg_a, w_rg_x=v_w_rg_x, b_rg_x=v_b_rg_x,
             lru_lambda=v_lru_lambda, w_rnn_out=v_w_rnn_out, w_o=v_w_o, norm_ffn=v_norm_ffn, w_ffn_in=v_w_ffn_in,
             w_ffn_out=v_w_ffn_out, norm_final=v_norm_final)
    xi, yi, ci = (lax.axis_index(a) for a in MESH_AXES)
    chip = 2 * xi + yi

    def held(T, a):
        return jnp.swapaxes(a, 0, 1) if T.transposed else a

    where = jnp.stack([2 * chip + ci]).astype(jnp.int32)
    by_name = {T.name: T for T in GATHERED}
    block = {T.name: held(T, w[T.name][0]) for T in BIG}
    block["conv_w"] = jnp.pad(conv_w[0], ((0, CONV_W.rows - 4), (0, 0)))
    block["w_ffn_in_lo"] = block["w_ffn_in_hi"] = block["w_ffn_in"]

    def gather_of(*names):
        return dict(exchange=_Gather([by_name[n] for n in names]), exchange_operands=[block[n] for n in names])

    def pair_sums(names, partials, tag):
        out = _pair_reduce(partials, [by_name[n] for n in names], "grad_pair_reduce_" + tag)
        return list(out[:len(names)]), list(out[len(names):])

    xs, target = x[0], loss_target[0]
    wg_b, wa_b, wx_b = (a[0].astype(BF16) for a in (w_pool_grp, w_rg_a, w_rg_x))
    ba2, bx2 = b_rg_a.reshape(1, DR), b_rg_x.reshape(1, DR)
    first = ("w_in", "w_pool_out", "w_rnn_out", "conv_w", "w_o")
    order = jnp.stack([chip, 2 * (1 - xi) + yi, 2 * xi + (1 - yi), 2 * (1 - xi) + (1 - yi)]).astype(jnp.int32)
    (proj, h1), (w_in_g, w_pool_out_g, w_rnn_out_g, conv_g, w_o_g) = _in_proj_gather(
        xs, norm_mix, [block[n] for n in first], [by_name[n] for n in first], order)
    mixer_weights = (wg_b, pool_scale, w_pool_out_g, conv_g[0:4], conv_b, wa_b, ba2, wx_b, bx2, lru_lambda, w_rnn_out_g)
    (pm, y_pool, hr, z, y_rnn, kept, gates), (w_ffn_lo_g, w_ffn_hi_g) = _mixer_fwd(
        proj, *mixer_weights, **gather_of("w_ffn_in_lo", "w_ffn_in_hi"))
    (mix, x2, h2), _ = _merge_out(xs, proj, y_pool, y_rnn, w_o_g, norm_ffn)
    (gu, act), (w_ffn_out_g,) = _ffn_up(h2, w_ffn_lo_g, w_ffn_hi_g, **gather_of("w_ffn_out"))
    dx3, dx3b, loss_part, dvec_fin = _ffn_down_loss(act, x2, target, w_ffn_out_g, norm_final.reshape(1, D))

    dgu = _ffn_bwd_down(dx3b, gu, w_ffn_out_g)
    dx2, dx2b, dmixo, dvec_ffn = _ffn_bwd_up(dgu, x2, dx3, w_ffn_lo_g, w_ffn_hi_g, norm_ffn, w_o_g)
    names_a = ("w_ffn_in", "w_ffn_out", "w_o")
    part_a = [_wgrad(dgu, h2, "wgrad_ffn_in", 1408, 512), _wgrad(act, dx3b, "wgrad_ffn_out", 1408, 512),
              _wgrad(mix, dx2b, "wgrad_o", 1024, 256)]
    lz1_a, sums_a = pair_sums(names_a, part_a, "ffn")
    (dproj, dypb, dyrb, dmat, dvec_mix), lz2_a = _mixer_bwd(
        proj, dmixo, y_pool, y_rnn, hr, kept, gates, *mixer_weights,
        exchange=_Scatter([by_name[n] for n in names_a]), exchange_operands=sums_a)
    names_b = ("w_pool_out", "w_rnn_out")
    part_b = [_wgrad(pm, dypb, "wgrad_pool_out", 512, 256), _wgrad(z, dyrb, "wgrad_rnn_out", 1024, 256)]
    lz1_b, sums_b = pair_sums(names_b, part_b, "mix")
    dvec = jnp.concatenate([dvec_mix[0:9], dvec_fin[0:1], dvec_ffn[0:1], jnp.pad(loss_part, ((0, 0), (0, DR - 1))),
                            jnp.zeros((VEC_ROWS - 12, DR), F32)], axis=0)
    g_in, exchanged = _wgrad(
        dproj, h1, "wgrad_in", 1152, 1024,
        exchange=_Both(_Scatter([by_name[n] for n in names_b]), _AllReduce([((MAT_ROWS, HD), 0), ((VEC_ROWS, DR), 1)])),
        exchange_operands=sums_b + [dmat, dvec])
    lz2_b, (mat, vec) = exchanged[:2], exchanged[2:]
    loss = vec[VEC_LOSS, 0]
    lz1_c, sums_c = pair_sums(("w_in",), [g_in], "in")
    (grad_x, dvec_in), lz2_c = _in_bwd(dproj, xs, dx2, norm_mix, w_in_g,
                                       exchange=_Scatter([by_name["w_in"]]), exchange_operands=sums_c)
    (vec_in,) = _all_reduce([dvec_in], [((8, D), 1)], "all_reduce_norm_mix")

    grads, delta, new_m, new_v = {}, {}, {}, {}
    for n, g, l1, l2 in zip(names_a + names_b + ("w_in",), part_a + part_b + [g_in], lz1_a + lz1_b + lz1_c,
                            lz2_a + lz2_b + lz2_c):
        T = by_name[n]
        out = _final_sum(T, g, l1, l2, where, held(T, w[n][0]), held(T, m[n][0]), held(T, v[n][0]))
        grads[n], delta[n], new_m[n], new_v[n] = (held(T, a) for a in out)
    me = 4 * xi + 2 * yi + ci
    small_grads = dict(
        w_pool_grp=mat[0:MAT_WA], w_rg_a=mat[MAT_WA:MAT_WX], w_rg_x=mat[MAT_WX:MAT_ROWS],
        pool_scale=vec[VEC_SCALE:VEC_SCALE + 1, 0:DP], conv_b=vec[VEC_CONV_B:VEC_CONV_B + 1],
        b_rg_a=vec[VEC_BA:VEC_BA + 1], b_rg_x=vec[VEC_BX:VEC_BX + 1], lru_lambda=vec[VEC_LAM:VEC_LAM + 1],
        conv_w=lax.dynamic_slice(vec, (VEC_CONV_W, VEC_PIECE * me), (4, VEC_PIECE)),
        norm_final=vec[VEC_NORM_FINAL:VEC_NORM_FINAL + 1], norm_ffn=vec[VEC_NORM_FFN:VEC_NORM_FFN + 1],
        norm_mix=vec_in[0:1])
    names = list(small_grads)
    as2d = lambda a, g: a.reshape(g.shape)
    upd = _adam_small([small_grads[n] for n in names],
                      [(as2d(w[n], small_grads[n]), as2d(m[n], small_grads[n]), as2d(v[n], small_grads[n])) for n in names])
    for i, n in enumerate(names):
        grads[n] = small_grads[n]
        delta[n], new_m[n], new_v[n] = upd[3 * i:3 * i + 3]

    shaped = lambda d: [d[n].reshape(w[n].shape) for n in WEIGHT_NAMES]
    return (loss, grad_x[None], *shaped(grads), *shaped(delta), *shaped(new_m), *shaped(new_v))
```

```python
import math

import jax
import jax.numpy as jnp
from jax import lax
from jax.experimental import pallas as pl
from jax.experimental.pallas import tpu as pltpu

F32 = jnp.float32
BF16 = jnp.bfloat16

D = 1024
DP = 512
PG = 128
WINDOWS = (2, 4, 8, 16)
DR = 1024
NH = 8
HD = 128
DIN = 4608
DFF = 2816
EPS = 1e-6
LRU_C = 8.0
POOL_HALO = 16
CONV_HALO = 8
KEPT = 3

ADAM_LR = 0.001
ADAM_B1 = 0.9
ADAM_B2 = 0.999
ADAM_EPS = 1e-08
ADAM_WD = 0.01
ADAM_STEP = 10

VMEM_LIMIT = 56 * 1024 * 1024
MESH_AXES = ("x", "y", "c")
MESH = pl.DeviceIdType.MESH


def _dot(a, b):
    return jnp.dot(a, b, preferred_element_type=F32)


def _dot_nt(a, b):
    return lax.dot_general(a, b, (((1,), (1,)), ((), ())), preferred_element_type=F32)


def _dot_tn(a, b):
    return lax.dot_general(a, b, (((0,), (0,)), ((), ())), preferred_element_type=F32)


def _params(*sem):
    return pltpu.CompilerParams(dimension_semantics=sem, vmem_limit_bytes=VMEM_LIMIT)


def _resident(shape):
    nd = len(shape)
    return pl.BlockSpec(shape, lambda i: (0,) * nd, pipeline_mode=pl.Buffered(1))


def _rows(shape_cols, tm):
    return pl.BlockSpec((tm, shape_cols), lambda i: (i, 0))


def _call(body, name, grid, in_specs, out_specs, out_shape, operands, scratch_shapes=(), exchange=None, exchange_operands=()):
    n_in, n_out, n_scr = len(in_specs), len(out_specs), len(scratch_shapes)
    steps = math.prod(grid)
    if exchange is None:
        outs = pl.pallas_call(body, name=name, grid=grid, in_specs=in_specs, out_specs=out_specs, out_shape=out_shape,
                              scratch_shapes=list(scratch_shapes), compiler_params=_params(*["arbitrary"] * len(grid)))(*operands)
        return outs, []
    e_in, e_out = len(exchange.in_specs), len(exchange.out_specs)

    def hosted(*refs):
        ins, refs = refs[:n_in], refs[n_in:]
        e_ins, refs = refs[:e_in], refs[e_in:]
        outs, refs = refs[:n_out], refs[n_out:]
        e_outs, refs = refs[:e_out], refs[e_out:]
        scr, e_scr = refs[:n_scr], refs[n_scr:]
        step = pl.program_id(0)
        for axis in range(1, len(grid)):
            step = step * grid[axis] + pl.program_id(axis)
        @pl.when(step == 0)
        def _():
            _enter(exchange)
            exchange.start(e_ins, e_outs, e_scr)

        for at, middle in exchange.middles(steps):
            pl.when(step == at)(lambda middle=middle: middle(e_ins, e_outs, e_scr))
        body(*ins, *outs, *scr)
        pl.when(step == steps - 1)(lambda: exchange.finish(e_ins, e_outs, e_scr))

    outs = pl.pallas_call(
        hosted, name=name, grid=grid, in_specs=list(in_specs) + exchange.in_specs,
        out_specs=list(out_specs) + exchange.out_specs, out_shape=list(out_shape) + exchange.out_shape,
        scratch_shapes=list(scratch_shapes) + exchange.scratch_shapes,
        compiler_params=pltpu.CompilerParams(dimension_semantics=("arbitrary",) * len(grid), vmem_limit_bytes=VMEM_LIMIT,
                                             collective_id=exchange.collective_id))(*operands, *exchange_operands)
    return outs[:n_out], outs[n_out:]


def _enter(exchange):
    peers = exchange.peers()
    if peers:
        barrier = pltpu.get_barrier_semaphore()
        for peer in peers:
            pl.semaphore_signal(barrier, inc=1, device_id=peer, device_id_type=MESH)
        pl.semaphore_wait(barrier, len(peers))


GELU_C = math.sqrt(2.0 / math.pi)
GELU_K = 0.044715 * GELU_C


def _gelu(x, with_grad=False):
    x2 = x * x
    t = jnp.tanh(x * (GELU_C + GELU_K * x2))
    hx = 0.5 * x
    y = hx + hx * t
    if not with_grad:
        return y
    return y, 0.5 + 0.5 * t + hx * (1.0 - t * t) * (GELU_C + (3.0 * GELU_K) * x2)


def _softplus_neg(lam):
    z = jnp.exp(-jnp.abs(lam))
    u = 1.0 + z
    dlt = u - 1.0
    log1p = jnp.where(dlt == 0.0, z, jnp.log(u) * (z / jnp.where(dlt == 0.0, 1.0, dlt)))
    return jnp.maximum(-lam, 0.0) + log1p


def _sigmoid(x):
    return 0.5 * jnp.tanh(0.5 * x) + 0.5


def _linear_scan(out_ref, A, B, h0, reverse):
    n = A.shape[0]
    sub = lax.broadcasted_iota(jnp.int32, (8, 1), 0)
    tiles = range(n // 8 - 1, -1, -1) if reverse else range(n // 8)
    carry = h0
    for j in tiles:
        a, b = A[8 * j:8 * j + 8, :], B[8 * j:8 * j + 8, :]
        for d in (1, 2, 4):
            keep = (sub < 8 - d) if reverse else (sub >= d)
            shift = 8 - d if reverse else d
            b = jnp.where(keep, a * pltpu.roll(b, shift, axis=0) + b, b)
            a = jnp.where(keep, a * pltpu.roll(a, shift, axis=0), a)
        h = a * carry + b
        out_ref[8 * j:8 * j + 8, :] = h
        carry = h[0:1, :] if reverse else h[7:8, :]
    return carry


def _pool_windows(ext, shift_sign):
    n = ext.shape[0]
    s = ext
    outs = []
    for w in WINDOWS:
        d = w // 2
        s = s + pltpu.roll(s, d if shift_sign > 0 else n - d, axis=0)
        outs.append(s[:, :PG])
        s = s[:, PG:]
    return outs


def _conv_taps(uext):
    taps = []
    for k in range(4):
        sh = 3 - k
        v = uext if sh == 0 else pltpu.roll(uext, sh, axis=0)
        taps.append(v[CONV_HALO:, :])
    return taps


def _gates(v, wa_ref, ba_ref, wx_ref, bx_ref, sp):
    vb = v.astype(BF16)
    ra, rx = [], []
    for h in range(NH):
        vh = vb[:, h * HD:(h + 1) * HD]
        ra.append(_dot(vh, wa_ref[h]))
        rx.append(_dot(vh, wx_ref[h]))
    r = _sigmoid(jnp.concatenate(ra, axis=1) + ba_ref[...])
    i = _sigmoid(jnp.concatenate(rx, axis=1) + bx_ref[...])
    log_a = r * ((-LRU_C) * sp)
    a = jnp.exp(log_a)
    one_minus = -jnp.tanh(log_a) * (1.0 + a * a)
    return r, i, a, jnp.sqrt(one_minus), lax.rsqrt(one_minus)


def _mixer_fwd(proj, wg, scale, w_pool_out, conv_w, conv_b, wa, ba, wx, bx, lam, w_rnn_out, exchange=None,
               exchange_operands=(), tm=256):
    S = proj.shape[0]
    UW = DP + 2 * DR

    def body(proj_ref, wg_ref, scale_ref, wpo_ref, cw_ref, cb_ref, wa_ref, ba_ref, wx_ref, bx_ref, lam_ref, wro_ref,
             pm_ref, ypool_ref, hr_ref, z_ref, yrnn_ref, kept_ref, gates_ref, pool_carry, conv_carry, h_carry):
        i = pl.program_id(0)

        @pl.when(i == 0)
        def _():
            pool_carry[...] = jnp.zeros_like(pool_carry)
            conv_carry[...] = jnp.zeros_like(conv_carry)
            h_carry[...] = jnp.zeros_like(h_carry)

        rows = lax.broadcasted_iota(jnp.int32, (tm, 1), 0)
        t_glob = i * tm + rows

        u_pool = proj_ref[:, 0:DP]
        ext = jnp.concatenate([pool_carry[...], u_pool], axis=0)
        pool_carry[...] = u_pool[tm - POOL_HALO:, :]
        sums = _pool_windows(ext, +1)
        mixed = []
        for g, w in enumerate(WINDOWS):
            inv_cnt = 1.0 / jnp.minimum(t_glob + 1, w).astype(F32)
            pooled_g = sums[g][POOL_HALO:, :] * inv_cnt - u_pool[:, g * PG:(g + 1) * PG]
            mixed.append(_dot(pooled_g.astype(BF16), wg_ref[g]))
        pm = (jnp.concatenate(mixed, axis=1) * scale_ref[...]).astype(BF16)
        pm_ref[...] = pm
        ypool_ref[...] = _dot(pm, wpo_ref[...]).astype(BF16)

        u_rnn = proj_ref[:, DP:DP + DR]
        uext = jnp.concatenate([conv_carry[...], u_rnn], axis=0)
        conv_carry[...] = u_rnn[tm - CONV_HALO:, :]
        taps = _conv_taps(uext)
        v = cb_ref[...]
        for k in range(4):
            v = v + taps[k] * cw_ref[k:k + 1, :]
        sp = _softplus_neg(lam_ref[...])
        r, gi, a, mult, _ = _gates(v, wa_ref, ba_ref, wx_ref, bx_ref, sp)
        for k, kept in enumerate((v, a, mult)):
            kept_ref[k] = kept
        for k, kept in enumerate((r, gi)):
            gates_ref[k] = kept.astype(BF16)
        h_carry[0:1, :] = _linear_scan(hr_ref, a, mult * gi * v, h_carry[0:1, :], reverse=False)
        z = (hr_ref[...] * _gelu(proj_ref[:, DP + DR:UW])).astype(BF16)
        z_ref[...] = z
        yrnn_ref[...] = _dot(z, wro_ref[...]).astype(BF16)

    return _call(
        body, "mixer_fwd", (S // tm,),
        in_specs=[_rows(UW, tm), _resident((4, PG, PG)), _resident((1, DP)), _resident((DP, D)), _resident((4, DR)),
                  _resident((1, DR)), _resident((NH, HD, HD)), _resident((1, DR)), _resident((NH, HD, HD)),
                  _resident((1, DR)), _resident((1, DR)), _resident((DR, D))],
        out_specs=[_rows(DP, tm), _rows(D, tm), _rows(DR, tm), _rows(DR, tm), _rows(D, tm),
                   pl.BlockSpec((KEPT, tm, DR), lambda i: (0, i, 0)), pl.BlockSpec((2, tm, DR), lambda i: (0, i, 0))],
        out_shape=[jax.ShapeDtypeStruct((S, DP), BF16),
                   jax.ShapeDtypeStruct((S, D), BF16), jax.ShapeDtypeStruct((S, DR), F32),
                   jax.ShapeDtypeStruct((S, DR), BF16), jax.ShapeDtypeStruct((S, D), BF16),
                   jax.ShapeDtypeStruct((KEPT, S, DR), F32), jax.ShapeDtypeStruct((2, S, DR), BF16)],
        scratch_shapes=[pltpu.VMEM((POOL_HALO, DP), F32), pltpu.VMEM((CONV_HALO, DR), F32), pltpu.VMEM((8, DR), F32)],
        operands=(proj, wg, scale, w_pool_out, conv_w, conv_b, wa, ba, wx, bx, lam, w_rnn_out),
        exchange=exchange, exchange_operands=exchange_operands)


FF_CHUNKS = ((0, 768), (768, 1536), (1536, 2304), (2304, DFF))


def _rms(x):
    r = lax.rsqrt(jnp.mean(x * x, axis=-1, keepdims=True) + EPS)
    return r, x * r


def _rms_bwd(dh, g, r, xh):
    dxh = dh * g
    return r * (dxh - xh * jnp.mean(dxh * xh, axis=-1, keepdims=True))


def _merge_out(x, proj, y_pool, y_rnn, w_o, norm_ffn, exchange=None, exchange_operands=(), tm=512):
    S = x.shape[0]
    GL0 = (DP + 2 * DR) // 512

    def gl_spec(k):
        return pl.BlockSpec((tm, 512), lambda i: (i, GL0 + k))

    def body(x_ref, gl0, gl1, gl2, gl3, yp_ref, yr_ref, wo_ref, gf_ref, mix_ref, x2_ref, h2_ref):
        s_p = _sigmoid(jnp.concatenate([gl0[...], gl1[...]], axis=1))
        s_r = _sigmoid(jnp.concatenate([gl2[...], gl3[...]], axis=1))
        mix = (s_p * yp_ref[...].astype(F32) + s_r * yr_ref[...].astype(F32)).astype(BF16)
        mix_ref[...] = mix
        x2 = x_ref[...] + _dot(mix, wo_ref[...])
        x2_ref[...] = x2
        _, xh2 = _rms(x2)
        h2_ref[...] = (xh2 * gf_ref[...]).astype(BF16)

    return _call(
        body, "merge_out", (S // tm,),
        in_specs=[_rows(D, tm), gl_spec(0), gl_spec(1), gl_spec(2), gl_spec(3), _rows(D, tm), _rows(D, tm),
                  _resident((D, D)), _resident((1, D))],
        out_specs=[_rows(D, tm), _rows(D, tm), _rows(D, tm)],
        out_shape=[jax.ShapeDtypeStruct((S, D), BF16), jax.ShapeDtypeStruct((S, D), F32), jax.ShapeDtypeStruct((S, D), BF16)],
        operands=(x, proj, proj, proj, proj, y_pool, y_rnn, w_o, norm_ffn),
        exchange=exchange, exchange_operands=exchange_operands)


def _ffn_up(h2, w_lo, w_hi, exchange=None, exchange_operands=(), tm=512):
    S = h2.shape[0]
    HALF = D // 2

    def body(h_ref, lo_ref, hi_ref, back_ref, act_ref):
        h_lo, h_hi = h_ref[:, 0:HALF], h_ref[:, HALF:D]
        for c0, c1 in FF_CHUNKS:
            gate = _dot_nt(h_lo, lo_ref[c0:c1, :]) + _dot_nt(h_hi, hi_ref[c0:c1, :])
            up = _dot_nt(h_lo, lo_ref[DFF + c0:DFF + c1, :]) + _dot_nt(h_hi, hi_ref[DFF + c0:DFF + c1, :])
            sg = _sigmoid(gate)
            silu = gate * sg
            back_ref[:, c0:c1] = (up * (sg * (1.0 + gate * (1.0 - sg)))).astype(BF16)
            back_ref[:, DFF + c0:DFF + c1] = silu.astype(BF16)
            act_ref[:, c0:c1] = (silu * up).astype(BF16)

    return _call(
        body, "ffn_up", (S // tm,),
        in_specs=[_rows(D, tm), _resident((2 * DFF, HALF)), _resident((2 * DFF, HALF))],
        out_specs=[_rows(2 * DFF, tm), _rows(DFF, tm)],
        out_shape=[jax.ShapeDtypeStruct((S, 2 * DFF), BF16), jax.ShapeDtypeStruct((S, DFF), BF16)],
        operands=(h2, w_lo, w_hi), exchange=exchange, exchange_operands=exchange_operands)


def _ffn_down_loss(act, x2, target, w_ffn_out, norm_final, tm=512):
    S = act.shape[0]

    def body(act_ref, x2_ref, t_ref, w_ref, gn_ref, dx3_ref, dx3b_ref, loss_ref, dvec_ref):
        i = pl.program_id(0)

        @pl.when(i == 0)
        def _():
            loss_ref[...] = jnp.zeros_like(loss_ref)
            dvec_ref[...] = jnp.zeros_like(dvec_ref)

        x3 = x2_ref[...] + _dot(act_ref[...], w_ref[...])
        r3, xh3 = _rms(x3)
        g_fin = gn_ref[...]
        e = xh3 * g_fin - t_ref[...]
        loss_ref[...] += jnp.sum(e * e, axis=(0, 1), keepdims=True) * (0.5 / D)
        dy = e * (1.0 / D)
        dvec_ref[0:1, :] += jnp.sum(dy * xh3, axis=0, keepdims=True)
        dx3 = _rms_bwd(dy, g_fin, r3, xh3)
        dx3_ref[...] = dx3
        dx3b_ref[...] = dx3.astype(BF16)

    return pl.pallas_call(
        body, name="ffn_down_loss", grid=(S // tm,),
        in_specs=[_rows(DFF, tm), _rows(D, tm), _rows(D, tm), _resident((DFF, D)), _resident((1, D))],
        out_specs=[_rows(D, tm), _rows(D, tm), _resident((1, 1)), _resident((8, D))],
        out_shape=[jax.ShapeDtypeStruct((S, D), F32), jax.ShapeDtypeStruct((S, D), BF16),
                   jax.ShapeDtypeStruct((1, 1), F32), jax.ShapeDtypeStruct((8, D), F32)],
        compiler_params=_params("arbitrary"),
    )(act, x2, target, w_ffn_out, norm_final)


def _ffn_bwd_down(dx3b, gu, w_ffn_out, tm=512):
    S = dx3b.shape[0]

    def body(d_ref, back_ref, w_ref, dgu_ref):
        d = d_ref[...]
        for c0, c1 in FF_CHUNKS:
            dact = _dot_nt(d, w_ref[c0:c1, :])
            dgu_ref[:, c0:c1] = (dact * back_ref[:, c0:c1].astype(F32)).astype(BF16)
            dgu_ref[:, DFF + c0:DFF + c1] = (dact * back_ref[:, DFF + c0:DFF + c1].astype(F32)).astype(BF16)

    return pl.pallas_call(
        body, name="ffn_bwd_down", grid=(S // tm,),
        in_specs=[_rows(D, tm), _rows(2 * DFF, tm), _resident((DFF, D))],
        out_specs=_rows(2 * DFF, tm),
        out_shape=jax.ShapeDtypeStruct((S, 2 * DFF), BF16),
        compiler_params=_params("parallel"),
    )(dx3b, gu, w_ffn_out)


def _ffn_bwd_up(dgu, x2, dx3, w_lo, w_hi, norm_ffn, w_o, tm=512):
    S = dgu.shape[0]
    HALF = D // 2

    def body(dgu_ref, x2_ref, dx3_ref, lo_ref, hi_ref, gf_ref, wo_ref, dx2_ref, dx2b_ref, dmixo_ref, dvec_ref):
        i = pl.program_id(0)

        @pl.when(i == 0)
        def _():
            dvec_ref[...] = jnp.zeros_like(dvec_ref)

        dgate, dup = dgu_ref[:, 0:DFF], dgu_ref[:, DFF:2 * DFF]
        dh2 = jnp.concatenate([_dot(dgate, w[0:DFF, :]) + _dot(dup, w[DFF:2 * DFF, :]) for w in (lo_ref, hi_ref)], axis=1)
        r2, xh2 = _rms(x2_ref[...])
        dvec_ref[0:1, :] += jnp.sum(dh2 * xh2, axis=0, keepdims=True)
        dx2 = dx3_ref[...] + _rms_bwd(dh2, gf_ref[...], r2, xh2)
        dx2_ref[...] = dx2
        dx2b = dx2.astype(BF16)
        dx2b_ref[...] = dx2b
        dmixo_ref[...] = _dot_nt(dx2b, wo_ref[...]).astype(BF16)

    return pl.pallas_call(
        body, name="ffn_bwd_up", grid=(S // tm,),
        in_specs=[_rows(2 * DFF, tm), _rows(D, tm), _rows(D, tm), _resident((2 * DFF, HALF)), _resident((2 * DFF, HALF)),
                  _resident((1, D)), _resident((D, D))],
        out_specs=[_rows(D, tm), _rows(D, tm), _rows(D, tm), _resident((8, D))],
        out_shape=[jax.ShapeDtypeStruct((S, D), F32), jax.ShapeDtypeStruct((S, D), BF16), jax.ShapeDtypeStruct((S, D), BF16),
                   jax.ShapeDtypeStruct((8, D), F32)],
        compiler_params=_params("arbitrary"),
    )(dgu, x2, dx3, w_lo, w_hi, norm_ffn, w_o)


VEC_ROWS = 16
MAT_WA = 4 * PG
MAT_WX = MAT_WA + NH * HD
MAT_ROWS = MAT_WX + NH * HD


def _mixer_bwd(proj, dmixo, y_pool, y_rnn, hr, kept, gates, wg, scale, w_pool_out, conv_w, conv_b, wa, ba, wx, bx, lam, w_rnn_out,
               exchange=None, exchange_operands=(), tm=256):
    S = proj.shape[0]
    nt = S // tm

    def rev(cols):
        return pl.BlockSpec((tm, cols), lambda i: (nt - 1 - i, 0))

    def halo(rows_, cols):
        per = tm // rows_
        return pl.BlockSpec((rows_, cols), lambda i: (jnp.maximum((nt - 1 - i) * per - 1, 0), 0))

    def body(proj_ref, projh_ref, dmixo_ref, yp_ref, yr_ref, hr_ref, hrh_ref, kept_ref, gates_ref, wg_ref, scale_ref, wpo_ref, cw_ref, cb_ref,
             wa_ref, ba_ref, wx_ref, bx_ref, lam_ref, wro_ref,
             dproj_ref, dypb_ref, dyrb_ref, dmat_ref, dvec_ref,
             q_carry, dv_carry, a_carry, g_carry, g_scr):
        i = pl.program_id(0)
        ti = nt - 1 - i

        @pl.when(i == 0)
        def _():
            q_carry[...] = jnp.zeros_like(q_carry)
            dv_carry[...] = jnp.zeros_like(dv_carry)
            a_carry[...] = jnp.zeros_like(a_carry)
            g_carry[...] = jnp.zeros_like(g_carry)
            dmat_ref[...] = jnp.zeros_like(dmat_ref)
            dvec_ref[...] = jnp.zeros_like(dvec_ref)

        rows = lax.broadcasted_iota(jnp.int32, (tm, 1), 0)
        t_glob = ti * tm + rows
        has_prev = (ti > 0).astype(F32)
        dmixo = dmixo_ref[...].astype(F32)

        s_p = _sigmoid(proj_ref[:, DP + 2 * DR:DP + 2 * DR + D])
        s_r = _sigmoid(proj_ref[:, DP + 2 * DR + D:DIN])
        dproj_ref[:, DP + 2 * DR:DP + 2 * DR + D] = (dmixo * yp_ref[...].astype(F32) * s_p * (1.0 - s_p)).astype(BF16)
        dproj_ref[:, DP + 2 * DR + D:DIN] = (dmixo * yr_ref[...].astype(F32) * s_r * (1.0 - s_r)).astype(BF16)
        dyp = (dmixo * s_p).astype(BF16)
        dyr = (dmixo * s_r).astype(BF16)
        dypb_ref[...] = dyp
        dyrb_ref[...] = dyr

        dz = _dot_nt(dyr, wro_ref[...])
        u_gate = proj_ref[:, DP + DR:DP + 2 * DR]
        gg, dgelu = _gelu(u_gate, with_grad=True)
        hr_t = hr_ref[...]
        dproj_ref[:, DP + DR:DP + 2 * DR] = (dz * hr_t * dgelu).astype(BF16)
        dhr = dz * gg

        sp = _softplus_neg(lam_ref[...])
        v, a, mult = (kept_ref[k] for k in range(KEPT))
        r, gi = (gates_ref[k].astype(F32) for k in range(2))
        inv_mult = 1.0 / mult

        C = jnp.where(rows == tm - 1, a_carry[0:1, :], pltpu.roll(a, tm - 1, axis=0))
        g_carry[0:1, :] = _linear_scan(g_scr, C, dhr, g_carry[0:1, :], reverse=True)
        a_carry[0:1, :] = a[0:1, :]
        g = g_scr[...]

        h_prev = jnp.where(rows == 0, hrh_ref[7:8, :] * has_prev, pltpu.roll(hr_t, 1, axis=0))
        da = g * h_prev
        gm = g * mult
        dmult = g * gi * v
        di = gm * v
        dv = gm * gi
        dlog_a = da * a - dmult * (a * a * inv_mult)
        dvec_ref[4:5, :] += jnp.sum(dlog_a * r, axis=0, keepdims=True)
        dra = (dlog_a * ((-LRU_C) * sp) * r * (1.0 - r))
        drx = di * gi * (1.0 - gi)
        dvec_ref[2:3, :] += jnp.sum(dra, axis=0, keepdims=True)
        dvec_ref[3:4, :] += jnp.sum(drx, axis=0, keepdims=True)
        drab = dra.astype(BF16)
        drxb = drx.astype(BF16)
        vb = v.astype(BF16)
        dvg = []
        for h in range(NH):
            sl = slice(h * HD, (h + 1) * HD)
            dvg.append(_dot_nt(drab[:, sl], wa_ref[h]) + _dot_nt(drxb[:, sl], wx_ref[h]))
            dmat_ref[MAT_WA + h * HD:MAT_WA + (h + 1) * HD, :] += _dot_tn(vb[:, sl], drab[:, sl])
            dmat_ref[MAT_WX + h * HD:MAT_WX + (h + 1) * HD, :] += _dot_tn(vb[:, sl], drxb[:, sl])
        dv = dv + jnp.concatenate(dvg, axis=1)
        dvec_ref[1:2, :] += jnp.sum(dv, axis=0, keepdims=True)
        dvext = jnp.concatenate([dv, dv_carry[...]], axis=0)
        dv_carry[...] = dv[0:CONV_HALO, :]
        n = tm + CONV_HALO
        u_rnn = proj_ref[:, DP:DP + DR]
        du_rnn = dv * cw_ref[3:4, :]
        dvec_ref[8:9, :] += jnp.sum(dv * u_rnn, axis=0, keepdims=True)
        for k in range(3):
            dv_k = pltpu.roll(dvext, n - (3 - k), axis=0)[0:tm, :]
            du_rnn = du_rnn + dv_k * cw_ref[k:k + 1, :]
            dvec_ref[5 + k:6 + k, :] += jnp.sum(dv_k * u_rnn, axis=0, keepdims=True)
        dproj_ref[:, DP:DP + DR] = du_rnn.astype(BF16)

        dpm = _dot_nt(dyp, wpo_ref[...])
        u_pool = proj_ref[:, 0:DP]
        ext = jnp.concatenate([projh_ref[:, 0:DP] * has_prev, u_pool], axis=0)
        sums = _pool_windows(ext, +1)
        scale_v = scale_ref[...]
        qs = []
        dpooled = []
        dscale = []
        for gi_, w in enumerate(WINDOWS):
            sl = slice(gi_ * PG, (gi_ + 1) * PG)
            inv_cnt = 1.0 / jnp.minimum(t_glob + 1, w).astype(F32)
            pooled_b = (sums[gi_][POOL_HALO:, :] * inv_cnt - u_pool[:, sl]).astype(BF16)
            mixed_g = _dot(pooled_b, wg_ref[gi_])
            dscale.append(jnp.sum(dpm[:, sl] * mixed_g, axis=0, keepdims=True))
            dmixed_b = (dpm[:, sl] * scale_v[:, sl]).astype(BF16)
            dmat_ref[gi_ * PG:(gi_ + 1) * PG, :] += _dot_tn(pooled_b, dmixed_b)
            dp_g = _dot_nt(dmixed_b, wg_ref[gi_])
            dpooled.append(dp_g)
            qs.append(dp_g * inv_cnt)
        dvec_ref[0:1, 0:DP] += jnp.concatenate(dscale, axis=1)
        q = jnp.concatenate(qs, axis=1)
        qext = jnp.concatenate([q, q_carry[...]], axis=0)
        q_carry[...] = q[0:POOL_HALO, :]
        tsum = _pool_windows(qext, -1)
        for gi_ in range(4):
            dproj_ref[:, gi_ * PG:(gi_ + 1) * PG] = (tsum[gi_][0:tm, :] - dpooled[gi_]).astype(BF16)

        @pl.when(i == nt - 1)
        def _():
            dvec_ref[4:5, :] = dvec_ref[4:5, :] * (LRU_C * _sigmoid(-lam_ref[...]))

    return _call(
        body, "mixer_bwd", (nt,),
        in_specs=[rev(DIN), halo(POOL_HALO, DIN), rev(D), rev(D), rev(D), rev(DR), halo(8, DR),
                  pl.BlockSpec((KEPT, tm, DR), lambda i: (0, nt - 1 - i, 0)),
                  pl.BlockSpec((2, tm, DR), lambda i: (0, nt - 1 - i, 0)), _resident((4, PG, PG)), _resident((1, DP)), _resident((DP, D)), _resident((4, DR)), _resident((1, DR)),
                  _resident((NH, HD, HD)), _resident((1, DR)), _resident((NH, HD, HD)), _resident((1, DR)),
                  _resident((1, DR)), _resident((DR, D))],
        out_specs=[rev(DIN), rev(D), rev(D), _resident((MAT_ROWS, HD)), _resident((VEC_ROWS, DR))],
        out_shape=[jax.ShapeDtypeStruct((S, DIN), BF16), jax.ShapeDtypeStruct((S, D), BF16),
                   jax.ShapeDtypeStruct((S, D), BF16), jax.ShapeDtypeStruct((MAT_ROWS, HD), F32),
                   jax.ShapeDtypeStruct((VEC_ROWS, DR), F32)],
        scratch_shapes=[pltpu.VMEM((POOL_HALO, DP), F32), pltpu.VMEM((CONV_HALO, DR), F32), pltpu.VMEM((8, DR), F32),
                        pltpu.VMEM((8, DR), F32), pltpu.VMEM((tm, DR), F32)],
        operands=(proj, proj, dmixo, y_pool, y_rnn, hr, hr, kept, gates, wg, scale, w_pool_out, conv_w, conv_b, wa, ba, wx, bx, lam,
                  w_rnn_out),
        exchange=exchange, exchange_operands=exchange_operands)


def _in_bwd(dproj, x, dx2, norm_mix, w_in, exchange=None, exchange_operands=(), tm=512):
    S = x.shape[0]

    def body(dp_ref, x_ref, dx2_ref, g_ref, w_ref, dx_ref, dg_ref):
        i = pl.program_id(0)

        @pl.when(i == 0)
        def _():
            dg_ref[...] = jnp.zeros_like(dg_ref)

        dh = _dot(dp_ref[:, 0:1536], w_ref[0:1536, :])
        dh = dh + _dot(dp_ref[:, 1536:3072], w_ref[1536:3072, :])
        dh = dh + _dot(dp_ref[:, 3072:DIN], w_ref[3072:DIN, :])
        xv = x_ref[...]
        r = lax.rsqrt(jnp.mean(xv * xv, axis=-1, keepdims=True) + EPS)
        xh = xv * r
        dg_ref[0:1, :] += jnp.sum(dh * xh, axis=0, keepdims=True)
        dxh = dh * g_ref[...]
        dx_ref[...] = dx2_ref[...] + r * (dxh - xh * jnp.mean(dxh * xh, axis=-1, keepdims=True))

    return _call(
        body, "in_bwd", (S // tm,),
        in_specs=[_rows(DIN, tm), _rows(D, tm), _rows(D, tm), _resident((1, D)), _resident((DIN, D))],
        out_specs=[_rows(D, tm), _resident((8, D))],
        out_shape=[jax.ShapeDtypeStruct((S, D), F32), jax.ShapeDtypeStruct((8, D), F32)],
        operands=(dproj, x, dx2, norm_mix, w_in), exchange=exchange, exchange_operands=exchange_operands)


def _wgrad(a, b, name, tk, tn, exchange=None, exchange_operands=()):
    S, K = a.shape
    N = b.shape[1]

    def body(a_ref, b_ref, o_ref):
        o_ref[...] = _dot_tn(a_ref[...], b_ref[...]).astype(BF16)

    (out,), exchanged = _call(
        body, name, (K // tk, N // tn),
        in_specs=[pl.BlockSpec((S, tk), lambda k, n: (0, k)), pl.BlockSpec((S, tn), lambda k, n: (0, n))],
        out_specs=[pl.BlockSpec((tk, tn), lambda k, n: (k, n))],
        out_shape=[jax.ShapeDtypeStruct((K, N), BF16)],
        operands=(a, b), exchange=exchange, exchange_operands=exchange_operands)
    return (out, exchanged) if exchange is not None else out


VEC_SCALE, VEC_CONV_B, VEC_BA, VEC_BX, VEC_LAM, VEC_CONV_W, VEC_NORM_FINAL, VEC_NORM_FFN = 0, 1, 2, 3, 4, 5, 9, 10
VEC_LOSS = 11


class _Big:
    def __init__(self, name, rows, cols, axis, n, dtype=BF16, transposed=False, src_cols=None):
        self.name, self.rows, self.cols, self.axis, self.n, self.dtype = name, rows, cols, axis, n, dtype
        self.transposed = transposed
        self.src_cols = src_cols
        self.block_shape = (rows, n) if axis == 1 else (n, cols)

    def block(self, ref, p):
        if self.axis == 1:
            return ref.at[:, pl.ds(pl.multiple_of(p * self.n, 128), self.n)]
        return ref.at[pl.ds(pl.multiple_of(p * self.n, 16 if self.dtype == BF16 else 8), self.n), :]


BIG = (_Big("w_in", DIN, D, 0, DIN // 8, transposed=True), _Big("w_pool_out", DP, D, 1, D // 8),
       _Big("w_rnn_out", DR, D, 0, DR // 8), _Big("w_o", D, D, 0, D // 8),
       _Big("w_ffn_in", 2 * DFF, D, 0, 2 * DFF // 8, transposed=True), _Big("w_ffn_out", DFF, D, 0, DFF // 8))
CONV_W = _Big("conv_w", 8, DR, 1, DR // 8, F32)
W_FFN_IN_HALVES = (_Big("w_ffn_in_lo", 2 * DFF, D // 2, 0, 2 * DFF // 8, src_cols=(0, D // 2)),
                   _Big("w_ffn_in_hi", 2 * DFF, D // 2, 0, 2 * DFF // 8, src_cols=(D // 2, D)))
GATHERED = BIG + (CONV_W,) + W_FFN_IN_HALVES

HBM_SPEC = pl.BlockSpec(memory_space=pl.ANY)
VMEM_SPEC = pl.BlockSpec(memory_space=pltpu.VMEM)


def _place():
    x, y, c = (lax.axis_index(a) for a in MESH_AXES)
    other_chips = [(1 - x, y), (x, 1 - y), (1 - x, 1 - y)]
    return x, y, c, other_chips


def _remote(src, dst, send_sems, recv_sems, idx, to):
    return pltpu.make_async_remote_copy(src_ref=src, dst_ref=dst, send_sem=send_sems.at[idx], recv_sem=recv_sems.at[idx],
                                        device_id=to, device_id_type=MESH)


def _device_index(chip, core):
    return 4 * chip[0] + 2 * chip[1] + core


class _Gather:
    def __init__(self, tensors):
        self.tensors = tuple(tensors)
        n = len(self.tensors)
        self.in_specs = [HBM_SPEC] * n
        self.out_specs = [HBM_SPEC] * n
        self.out_shape = [jax.ShapeDtypeStruct((T.rows, T.cols), T.dtype) for T in self.tensors]
        self.scratch_shapes = [pltpu.VMEM(T.block_shape, T.dtype) for T in self.tensors] + [
            pltpu.VMEM(T.block_shape, F32) for T in self.tensors] + [
            pltpu.SemaphoreType.DMA((n, 7)), pltpu.SemaphoreType.DMA((n, 7)), pltpu.SemaphoreType.DMA((n, 2))]

    collective_id = 1

    def peers(self):
        x, y, c, _ = _place()
        return [(x, y, 1 - c), (1 - x, y, c), (x, 1 - y, c)]

    def middles(self, steps):
        return [(steps // 2, self.relay), (steps - 1, self.middle)]

    def _copies(self, ins, outs, scratch):
        n = len(self.tensors)
        mine, raw, (send_sems, recv_sems, loc_sems) = scratch[:n], scratch[n:2 * n], scratch[2 * n:]
        x, y, c, chips = _place()
        sibling = (x, y, 1 - c)
        me = _device_index((x, y), c)
        relay_from = (jnp.where(c == 0, 1 - x, x), jnp.where(c == 0, y, 1 - y))
        relay_to = (jnp.where(c == 0, x, 1 - x), jnp.where(c == 0, 1 - y, y))
        loads, stores, first, relays, passed, arrivals, late = [], [], [], [], [], [], []
        for t, T in enumerate(self.tensors):
            place = T.block(outs[t], me)
            src = ins[t] if T.src_cols is None else ins[t].at[:, T.src_cols[0]:T.src_cols[1]]
            loads.append(pltpu.make_async_copy(src, raw[t], loc_sems.at[t, 0]))
            stores.append(pltpu.make_async_copy(mine[t], place, loc_sems.at[t, 1]))
            first.append(_remote(mine[t], place, send_sems, recv_sems, (t, 0), sibling))
            theirs = T.block(outs[t], _device_index((x, y), 1 - c))
            late.append(_remote(theirs, theirs, send_sems, recv_sems, (t, 0), sibling))
            relayed = T.block(outs[t], _device_index(relay_from, c))
            relays.append(_remote(relayed, relayed, send_sems, recv_sems, (t, 3), (*relay_to, c)))
            for k, chip in enumerate(chips):
                if k < 2:
                    first.append(_remote(mine[t], place, send_sems, recv_sems, (t, 1 + k), (*chip, c)))
                land = T.block(outs[t], _device_index(chip, c))
                arrivals.append(_remote(land, land, send_sems, recv_sems, (t, 1 + k), sibling))
                passed.append(_remote(land, land, send_sems, recv_sems, (t, 4 + k), sibling))
                theirs = T.block(outs[t], _device_index(chip, 1 - c))
                late.append(_remote(theirs, theirs, send_sems, recv_sems, (t, 4 + k), sibling))
        return loads, stores, first, relays, passed, arrivals, late

    def start(self, ins, outs, scratch):
        loads, stores, first, _, _, _, _ = self._copies(ins, outs, scratch)
        n = len(self.tensors)
        for cp in loads:
            cp.start()
        for t, cp in enumerate(loads):
            cp.wait()
            scratch[t][...] = scratch[n + t][...].astype(self.tensors[t].dtype)
        for cp in stores + first:
            cp.start()

    def relay(self, ins, outs, scratch, skip=0):
        _, _, _, relays, passed, arrivals, _ = self._copies(ins, outs, scratch)
        for t in range(skip, len(self.tensors)):
            arrivals[3 * t].wait_recv()
            arrivals[3 * t + 1].wait_recv()
            for cp in (relays[t], passed[3 * t], passed[3 * t + 1]):
                cp.start()

    def middle(self, ins, outs, scratch, skip=0):
        _, _, _, _, passed, arrivals, _ = self._copies(ins, outs, scratch)
        for t in range(skip, len(self.tensors)):
            arrivals[3 * t + 2].wait_recv()
            passed[3 * t + 2].start()

    def finish(self, ins, outs, scratch, skip=0):
        _, stores, first, relays, passed, _, late = self._copies(ins, outs, scratch)
        for cp in late[4 * skip:]:
            cp.wait_recv()
        for cp in first + relays + passed:
            cp.wait_send()
        for cp in stores[skip:]:
            cp.wait()


def _in_proj_gather(x, norm_mix, blocks, tensors, order, tm=1024):
    S = x.shape[0]
    nt = S // tm
    n = len(tensors)
    gather = _Gather(tensors)
    CB = 2 * tensors[0].n

    def body(order_ref, x_ref, g_ref, *refs):
        ins, (proj_ref, h_ref), outs = refs[:n], refs[n:n + 2], refs[n + 2:2 * n + 2]
        (h_all, w_chip, w_sem), scratch = refs[2 * n + 2:2 * n + 5], refs[2 * n + 5:]
        q, i = pl.program_id(0), pl.program_id(1)
        _, stores, _, relays, passed, arrivals, late = gather._copies(ins, outs, scratch)

        def fetch(turn):
            rows = outs[0].at[pl.ds(pl.multiple_of(order_ref[turn] * CB, 16), CB), :]
            cp = pltpu.make_async_copy(rows, w_chip, w_sem)
            cp.start()
            cp.wait()

        @pl.when((q == 0) & (i == 0))
        def _():
            _enter(gather)
            gather.start(ins, outs, scratch)
            late[0].wait_recv()
            stores[0].wait()
            fetch(0)

        @pl.when((q == 1) & (i == 0))
        def _():
            arrivals[0].wait_recv()
            arrivals[1].wait_recv()
            for cp in (relays[0], passed[0], passed[1]):
                cp.start()
            late[1].wait_recv()
            fetch(1)

        @pl.when((q == 2) & (i == 0))
        def _():
            late[2].wait_recv()
            fetch(2)
            gather.relay(ins, outs, scratch, skip=1)

        @pl.when((q == 3) & (i == 0))
        def _():
            arrivals[2].wait_recv()
            passed[2].start()
            late[3].wait_recv()
            fetch(3)

        rows = pl.ds(pl.multiple_of(i * tm, tm), tm)

        @pl.when(q == 0)
        def _():
            xv = x_ref[...]
            r = lax.rsqrt(jnp.mean(xv * xv, axis=-1, keepdims=True) + EPS)
            h = (xv * r * g_ref[...]).astype(BF16)
            h_all[rows, :] = h
            h_ref[...] = h

        proj_ref[...] = _dot_nt(h_all[rows, :], w_chip[...])

        @pl.when((q == 3) & (i == nt - 1))
        def _():
            gather.middle(ins, outs, scratch, skip=1)
            gather.finish(ins, outs, scratch, skip=1)

    row_tile = lambda q, i, order: (jnp.where(q == 0, i, nt - 1), 0)
    whole = lambda shape: pl.BlockSpec(shape, lambda q, i, order: (0,) * len(shape), pipeline_mode=pl.Buffered(1))
    outs = pl.pallas_call(
        body, name="in_proj_gather",
        grid_spec=pltpu.PrefetchScalarGridSpec(
            num_scalar_prefetch=1, grid=(4, nt),
            in_specs=[pl.BlockSpec((tm, D), row_tile), whole((1, D))] + gather.in_specs,
            out_specs=[pl.BlockSpec((tm, CB), lambda q, i, order: (i, order[q])), pl.BlockSpec((tm, D), row_tile)]
            + gather.out_specs,
            scratch_shapes=[pltpu.VMEM((S, D), BF16), pltpu.VMEM((CB, D), BF16), pltpu.SemaphoreType.DMA]
            + gather.scratch_shapes),
        out_shape=[jax.ShapeDtypeStruct((S, DIN), F32), jax.ShapeDtypeStruct((S, D), BF16)] + gather.out_shape,
        compiler_params=pltpu.CompilerParams(dimension_semantics=("arbitrary", "arbitrary"), vmem_limit_bytes=VMEM_LIMIT,
                                             collective_id=gather.collective_id),
    )(order, x, norm_mix, *blocks)
    return outs[:2], outs[2:]


PAIR_ROWS = 32


def _pair_reduce(grads, tensors, name):
    nt = len(tensors)

    def body(*refs):
        ins, own_out, sums_out, landed, mine = (refs[k * nt:(k + 1) * nt] for k in range(5))
        send_sems, recv_sems, loc_sems = refs[5 * nt:]
        x, y, c, chips = _place()
        chip_of = [2 * chip[0] + chip[1] for chip in chips]
        barrier = pltpu.get_barrier_semaphore()
        pl.semaphore_signal(barrier, inc=1, device_id=(x, y, 1 - c), device_id_type=MESH)
        pl.semaphore_wait(barrier, 1)
        swaps, loads = [], []
        for t, T in enumerate(tensors):
            for j in range(4):
                swaps.append(_remote(T.block(ins[t], 2 * j + 1 - c), landed[t].at[j], send_sems, recv_sems, (t, j),
                                     (x, y, 1 - c)))
            for k in range(3):
                loads.append(pltpu.make_async_copy(T.block(ins[t], 2 * chip_of[k] + c), mine[t].at[k], loc_sems.at[t, k]))
        for cp in swaps + loads:
            cp.start()
        for cp in loads:
            cp.wait()
        for cp in swaps:
            cp.wait_recv()
        stores = []
        for t, T in enumerate(tensors):
            for k in range(3):
                acc, got = mine[t].at[k], landed[t].at[chip_of[k]]

                def add(i, carry, acc=acc, got=got):
                    rows = pl.ds(pl.multiple_of(i * PAIR_ROWS, PAIR_ROWS), PAIR_ROWS)
                    acc[rows, :] = (acc[rows, :].astype(F32) + got[rows, :].astype(F32)).astype(BF16)
                    return carry

                lax.fori_loop(0, T.block_shape[0] // PAIR_ROWS, add, 0)
            stores.append(pltpu.make_async_copy(mine[t], sums_out[t], loc_sems.at[t, 3]))
            stores.append(pltpu.make_async_copy(landed[t].at[2 * x + y], own_out[t], loc_sems.at[t, 4]))
        for cp in stores:
            cp.start()
        for cp in swaps:
            cp.wait_send()
        for cp in stores:
            cp.wait()

    blocks = [T.block_shape for T in tensors]
    return pl.pallas_call(
        body, name=name,
        in_specs=[HBM_SPEC] * nt, out_specs=[HBM_SPEC] * (2 * nt),
        out_shape=[jax.ShapeDtypeStruct(b, BF16) for b in blocks] + [jax.ShapeDtypeStruct((3,) + b, BF16) for b in blocks],
        scratch_shapes=[pltpu.VMEM((4,) + b, BF16) for b in blocks] + [pltpu.VMEM((3,) + b, BF16) for b in blocks]
        + [pltpu.SemaphoreType.DMA((nt, 4)), pltpu.SemaphoreType.DMA((nt, 4)), pltpu.SemaphoreType.DMA((nt, 5))],
        compiler_params=pltpu.CompilerParams(vmem_limit_bytes=VMEM_LIMIT, collective_id=3),
    )(*grads)


class _Scatter:
    def __init__(self, tensors):
        self.tensors = tuple(tensors)
        n = len(self.tensors)
        self.in_specs = [HBM_SPEC] * n
        self.out_specs = [HBM_SPEC] * n
        self.out_shape = [jax.ShapeDtypeStruct((2,) + T.block_shape, BF16) for T in self.tensors]
        self.scratch_shapes = [pltpu.VMEM(T.block_shape, BF16) for T in self.tensors] * 2 + [
            pltpu.SemaphoreType.DMA((n, 3)), pltpu.SemaphoreType.DMA((n, 3)), pltpu.SemaphoreType.DMA((n,))]

    collective_id = 2

    def peers(self):
        x, y, c, _ = _place()
        return [(1 - x, y, c), (x, 1 - y, c)]

    def middles(self, steps):
        return [(steps // 2, self.middle)]

    def _copies(self, ins, outs, scratch):
        n = len(self.tensors)
        landed, mine, (send_sems, recv_sems, loc_sems) = scratch[:n], scratch[n:2 * n], scratch[2 * n:]
        x, y, c, _ = _place()
        direct = (jnp.where(c == 0, 1 - x, x), jnp.where(c == 0, y, 1 - y), c)
        other = (jnp.where(c == 0, x, 1 - x), jnp.where(c == 0, 1 - y, y), c)
        k_direct = jnp.where(c == 0, 0, 1)
        to_direct, legs, loads, combined, arrivals = [], [], [], [], []
        for t in range(n):
            to_direct.append(_remote(ins[t].at[k_direct], outs[t].at[0], send_sems, recv_sems, (t, 0), direct))
            legs.append(_remote(ins[t].at[2], landed[t], send_sems, recv_sems, (t, 2), direct))
            loads.append(pltpu.make_async_copy(ins[t].at[1 - k_direct], mine[t], loc_sems.at[t]))
            combined.append(_remote(mine[t], outs[t].at[1], send_sems, recv_sems, (t, 1), other))
            arrivals.append(_remote(landed[t], landed[t], send_sems, recv_sems, (t, 2), direct))
        return to_direct, legs, loads, combined, arrivals, landed, mine

    def start(self, ins, outs, scratch):
        to_direct, legs, loads, _, _, _, _ = self._copies(ins, outs, scratch)
        for cp in to_direct + legs + loads:
            cp.start()

    def middle(self, ins, outs, scratch):
        _, _, loads, combined, arrivals, landed, mine = self._copies(ins, outs, scratch)
        for t, T in enumerate(self.tensors):
            loads[t].wait()
            arrivals[t].wait_recv()
            acc, got = mine[t], landed[t]

            def add(i, carry, acc=acc, got=got):
                rows = pl.ds(pl.multiple_of(i * PAIR_ROWS, PAIR_ROWS), PAIR_ROWS)
                acc[rows, :] = (acc[rows, :].astype(F32) + got[rows, :].astype(F32)).astype(BF16)
                return carry

            lax.fori_loop(0, T.block_shape[0] // PAIR_ROWS, add, 0)
            combined[t].start()

    def finish(self, ins, outs, scratch):
        to_direct, legs, _, combined, _, _, _ = self._copies(ins, outs, scratch)
        for cp in to_direct + combined:
            cp.wait()
        for cp in legs:
            cp.wait_send()


def _adamw(w, g, m, v):
    m = ADAM_B1 * m + (1.0 - ADAM_B1) * g
    v = ADAM_B2 * v + (1.0 - ADAM_B2) * (g * g)
    m_hat = m / (1.0 - ADAM_B1 ** ADAM_STEP)
    v_hat = v / (1.0 - ADAM_B2 ** ADAM_STEP)
    delta = -ADAM_LR * (m_hat / (jnp.sqrt(v_hat) + ADAM_EPS) + ADAM_WD * w)
    return delta, m, v


def _final_sum(T, g, lz1, lz2, where, w, m, v):
    rows, cols = T.block_shape
    sub = 4 if T.axis == 0 and rows % 64 == 0 and rows > 256 else 1
    blk = (rows // sub, cols)

    def body(where_ref, g_ref, l1_ref, l2_ref, w_ref, m_ref, v_ref, g_out, d_out, m_out, v_out):
        tot = g_ref[...].astype(F32) + l1_ref[...].astype(F32)
        for k in range(2):
            tot = tot + l2_ref[k].astype(F32)
        g_out[...] = tot
        d_out[...], m_out[...], v_out[...] = _adamw(w_ref[...], tot, m_ref[...], v_ref[...])

    def in_whole(r, wh):
        p = wh[0]
        return (0, p) if T.axis == 1 else (p * sub + r, 0)

    own = pl.BlockSpec(blk, lambda r, wh: (r, 0))
    return pl.pallas_call(
        body, name="grad_final_" + T.name,
        grid_spec=pltpu.PrefetchScalarGridSpec(
            num_scalar_prefetch=1, grid=(sub,),
            in_specs=[pl.BlockSpec(blk, in_whole),
                      own,
                      pl.BlockSpec((2,) + blk, lambda r, wh: (0, r, 0)), own, own, own],
            out_specs=[own] * 4),
        out_shape=[jax.ShapeDtypeStruct(T.block_shape, F32)] * 4,
        compiler_params=_params("arbitrary"),
    )(where, g, lz1, lz2, w, m, v)


VEC_PIECE = DR // 8


class _AllReduce:
    def __init__(self, items):
        self.items = tuple(items)
        n = len(self.items)
        self.in_specs = [HBM_SPEC] * n
        self.out_specs = [HBM_SPEC] * n
        self.out_shape = [jax.ShapeDtypeStruct(shape, F32) for shape, _ in self.items]
        pieces = [(shape[0] // 8, shape[1]) if axis == 0 else (shape[0], shape[1] // 8) for shape, axis in self.items]
        self.scratch_shapes = ([pltpu.VMEM((8,) + p, F32) for p in pieces] + [pltpu.VMEM(p, F32) for p in pieces] + [
            pltpu.SemaphoreType.DMA((2 * n, 8)), pltpu.SemaphoreType.DMA((2 * n, 8)), pltpu.SemaphoreType.DMA((2 * n,))])

    collective_id = None

    def peers(self):
        return []

    def middles(self, steps):
        return [(steps // 2, self.middle)]

    def _copies(self, ins, outs, scratch):
        n = len(self.items)
        landed, sums, (send_sems, recv_sems, loc_sems) = scratch[:n], scratch[n:2 * n], scratch[2 * n:]
        x, y, c, _ = _place()
        me = _device_index((x, y), c)

        def peer(r):
            return (1 - x if r & 4 else x, 1 - y if r & 2 else y, 1 - c if r & 1 else c)

        def piece(i, ref, p):
            shape, axis = self.items[i]
            if axis == 0:
                rows = shape[0] // 8
                return ref.at[pl.ds(pl.multiple_of(p * rows, 8), rows), :]
            cols = shape[1] // 8
            return ref.at[:, pl.ds(pl.multiple_of(p * cols, 128), cols)]

        own, scatter, arrivals, keep, spread, late = [], [], [], [], [], []
        for i in range(n):
            own.append(pltpu.make_async_copy(piece(i, ins[i], me), landed[i].at[0], loc_sems.at[2 * i]))
            keep.append(pltpu.make_async_copy(sums[i], piece(i, outs[i], me), loc_sems.at[2 * i + 1]))
            for r in range(1, 8):
                to = peer(r)
                p = _device_index(to[:2], to[2])
                scatter.append(_remote(piece(i, ins[i], p), landed[i].at[r], send_sems, recv_sems, (2 * i, r), to))
                spread.append(_remote(sums[i], piece(i, outs[i], me), send_sems, recv_sems, (2 * i + 1, r), to))
                late.append(_remote(sums[i], piece(i, outs[i], p), send_sems, recv_sems, (2 * i + 1, r), to))
        return own, scatter, keep, spread, late, landed, sums

    def start(self, ins, outs, scratch):
        own, scatter, _, _, _, _, _ = self._copies(ins, outs, scratch)
        for cp in own + scatter:
            cp.start()

    def middle(self, ins, outs, scratch):
        own, scatter, keep, spread, _, landed, sums = self._copies(ins, outs, scratch)
        for cp in own:
            cp.wait()
        for cp in scatter:
            cp.wait_recv()
        for i in range(len(self.items)):
            total = landed[i][0]
            for r in range(1, 8):
                total = total + landed[i][r]
            sums[i][...] = total
        for cp in keep + spread:
            cp.start()

    def finish(self, ins, outs, scratch):
        _, scatter, keep, spread, late, _, _ = self._copies(ins, outs, scratch)
        for cp in late:
            cp.wait_recv()
        for cp in scatter + spread:
            cp.wait_send()
        for cp in keep:
            cp.wait()


class _Both:
    def __init__(self, a, b):
        self.a, self.b = a, b
        self.in_specs, self.out_specs = a.in_specs + b.in_specs, a.out_specs + b.out_specs
        self.out_shape, self.scratch_shapes = a.out_shape + b.out_shape, a.scratch_shapes + b.scratch_shapes

    collective_id = None

    def peers(self):
        return []

    def _each(self, ins, outs, scratch):
        a = self.a
        i, o, s = len(a.in_specs), len(a.out_specs), len(a.scratch_shapes)
        return (a, ins[:i], outs[:o], scratch[:s]), (self.b, ins[i:], outs[o:], scratch[s:])

    def middles(self, steps):
        def of(which, middle):
            return lambda ins, outs, scratch: middle(*self._each(ins, outs, scratch)[which][1:])
        return [(at, of(which, middle)) for which, e in enumerate((self.a, self.b)) for at, middle in e.middles(steps)]

    def start(self, ins, outs, scratch):
        for e, i, o, s in self._each(ins, outs, scratch):
            e.start(i, o, s)

    def finish(self, ins, outs, scratch):
        for e, i, o, s in self._each(ins, outs, scratch):
            e.finish(i, o, s)


def _all_reduce(arrays, items, name):
    reduce = _AllReduce(items)
    n = len(items)

    def body(*refs):
        ins, outs, scratch = refs[:n], refs[n:2 * n], refs[2 * n:]
        reduce.start(ins, outs, scratch)
        reduce.middle(ins, outs, scratch)
        reduce.finish(ins, outs, scratch)

    return pl.pallas_call(
        body, name=name, in_specs=reduce.in_specs, out_specs=reduce.out_specs, out_shape=reduce.out_shape,
        scratch_shapes=reduce.scratch_shapes,
    )(*arrays)


def _adam_small(grads, wmv):
    n = len(grads)

    def body(*refs):
        g_refs, rest = refs[:n], refs[n:]
        ins, outs = rest[:3 * n], rest[3 * n:]
        for i in range(n):
            d, m, v = _adamw(ins[3 * i][...], g_refs[i][...], ins[3 * i + 1][...], ins[3 * i + 2][...])
            outs[3 * i][...], outs[3 * i + 1][...], outs[3 * i + 2][...] = d, m, v

    flat = [a for t in wmv for a in t]
    return pl.pallas_call(
        body, name="adam_small",
        in_specs=[VMEM_SPEC] * (4 * n), out_specs=[VMEM_SPEC] * (3 * n),
        out_shape=[jax.ShapeDtypeStruct(a.shape, F32) for a in flat],
    )(*grads, *flat)


WEIGHT_NAMES = ("norm_mix", "w_in", "w_pool_grp", "pool_scale", "w_pool_out", "conv_w", "conv_b", "w_rg_a", "b_rg_a", "w_rg_x",
                "b_rg_x", "lru_lambda", "w_rnn_out", "w_o", "norm_ffn", "w_ffn_in", "w_ffn_out", "norm_final")


def kernel(x, norm_mix, w_in, w_pool_grp, pool_scale, w_pool_out, conv_w, conv_b, w_rg_a, b_rg_a, w_rg_x, b_rg_x, lru_lambda, w_rnn_out, w_o, norm_ffn, w_ffn_in, w_ffn_out, norm_final, loss_target, m_norm_mix, m_w_in, m_w_pool_grp, m_pool_scale, m_w_pool_out, m_conv_w, m_conv_b, m_w_rg_a, m_b_rg_a, m_w_rg_x, m_b_rg_x, m_lru_lambda, m_w_rnn_out, m_w_o, m_norm_ffn, m_w_ffn_in, m_w_ffn_out, m_norm_final, v_norm_mix, v_w_in, v_w_pool_grp, v_pool_scale, v_w_pool_out, v_conv_w, v_conv_b, v_w_rg_a, v_b_rg_a, v_w_rg_x, v_b_rg_x, v_lru_lambda, v_w_rnn_out, v_w_o, v_norm_ffn, v_w_ffn_in, v_w_ffn_out, v_norm_final):
    w = dict(norm_mix=norm_mix, w_in=w_in, w_pool_grp=w_pool_grp, pool_scale=pool_scale, w_pool_out=w_pool_out, conv_w=conv_w,
             conv_b=conv_b, w_rg_a=w_rg_a, b_rg_a=b_rg_a, w_rg_x=w_rg_x, b_rg_x=b_rg_x, lru_lambda=lru_lambda,
             w_rnn_out=w_rnn_out, w_o=w_o, norm_ffn=norm_ffn, w_ffn_in=w_ffn_in, w_ffn_out=w_ffn_out, norm_final=norm_final)
    m = dict(norm_mix=m_norm_mix, w_in=m_w_in, w_pool_grp=m_w_pool_grp, pool_scale=m_pool_scale, w_pool_out=m_w_pool_out,
             conv_w=m_conv_w, conv_b=m_conv_b, w_rg_a=m_w_rg_a, b_rg_a=m_b_rg_a, w_rg_x=m_w_rg_x, b_rg_x=m_b_rg_x,
             lru_lambda=m_lru_lambda, w_rnn_out=m_w_rnn_out, w_o=m_w_o, norm_ffn=m_norm_ffn, w_ffn_in=m_w_ffn_in,
             w_ffn_out=m_w_ffn_out, norm_final=m_norm_final)
    v = dict(norm_mix=v_norm_mix, w_in=v_w_in, w_pool_grp=v_w_pool_grp, pool_scale=v_pool_scale, w_pool_out=v_w_pool_out,
             conv_w=v_conv_w, conv_b=v_conv_b, w_rg_a=v_w_rg_a, b_rg_a=v_b_rg_a, w_rg_x=v_w_rg_x, b_rg_x=v_b_rg_x,
             lru_lambda=v_lru_lambda, w_rnn_out=v_w_rnn_out, w_o=v_w_o, norm_ffn=v_norm_ffn, w_ffn_in=v_w_ffn_in,
             w_ffn_out=v_w_ffn_out, norm_final=v_norm_final)
    xi, yi, ci = (lax.axis_index(a) for a in MESH_AXES)
    chip = 2 * xi + yi

    def held(T, a):
        return jnp.swapaxes(a, 0, 1) if T.transposed else a

    where = jnp.stack([2 * chip + ci]).astype(jnp.int32)
    by_name = {T.name: T for T in GATHERED}
    block = {T.name: held(T, w[T.name][0]) for T in BIG}
    block["conv_w"] = jnp.pad(conv_w[0], ((0, CONV_W.rows - 4), (0, 0)))
    block["w_ffn_in_lo"] = block["w_ffn_in_hi"] = block["w_ffn_in"]

    def gather_of(*names):
        return dict(exchange=_Gather([by_name[n] for n in names]), exchange_operands=[block[n] for n in names])

    def pair_sums(names, partials, tag):
        out = _pair_reduce(partials, [by_name[n] for n in names], "grad_pair_reduce_" + tag)
        return list(out[:len(names)]), list(out[len(names):])

    xs, target = x[0], loss_target[0]
    wg_b, wa_b, wx_b = (a[0].astype(BF16) for a in (w_pool_grp, w_rg_a, w_rg_x))
    ba2, bx2 = b_rg_a.reshape(1, DR), b_rg_x.reshape(1, DR)
    first = ("w_in", "w_pool_out", "w_rnn_out", "conv_w", "w_o")
    order = jnp.stack([chip, 2 * (1 - xi) + yi, 2 * xi + (1 - yi), 2 * (1 - xi) + (1 - yi)]).astype(jnp.int32)
    (proj, h1), (w_in_g, w_pool_out_g, w_rnn_out_g, conv_g, w_o_g) = _in_proj_gather(
        xs, norm_mix, [block[n] for n in first], [by_name[n] for n in first], order)
    mixer_weights = (wg_b, pool_scale, w_pool_out_g, conv_g[0:4], conv_b, wa_b, ba2, wx_b, bx2, lru_lambda, w_rnn_out_g)
    (pm, y_pool, hr, z, y_rnn, kept, gates), (w_ffn_lo_g, w_ffn_hi_g) = _mixer_fwd(
        proj, *mixer_weights, **gather_of("w_ffn_in_lo", "w_ffn_in_hi"))
    (mix, x2, h2), _ = _merge_out(xs, proj, y_pool, y_rnn, w_o_g, norm_ffn)
    (gu, act), (w_ffn_out_g,) = _ffn_up(h2, w_ffn_lo_g, w_ffn_hi_g, **gather_of("w_ffn_out"))
    dx3, dx3b, loss_part, dvec_fin = _ffn_down_loss(act, x2, target, w_ffn_out_g, norm_final.reshape(1, D))

    dgu = _ffn_bwd_down(dx3b, gu, w_ffn_out_g)
    dx2, dx2b, dmixo, dvec_ffn = _ffn_bwd_up(dgu, x2, dx3, w_ffn_lo_g, w_ffn_hi_g, norm_ffn, w_o_g)
    names_a = ("w_ffn_in", "w_ffn_out", "w_o")
    part_a = [_wgrad(dgu, h2, "wgrad_ffn_in", 1408, 512), _wgrad(act, dx3b, "wgrad_ffn_out", 1408, 512),
              _wgrad(mix, dx2b, "wgrad_o", 1024, 256)]
    lz1_a, sums_a = pair_sums(names_a, part_a, "ffn")
    (dproj, dypb, dyrb, dmat, dvec_mix), lz2_a = _mixer_bwd(
        proj, dmixo, y_pool, y_rnn, hr, kept, gates, *mixer_weights,
        exchange=_Scatter([by_name[n] for n in names_a]), exchange_operands=sums_a)
    names_b = ("w_pool_out", "w_rnn_out")
    part_b = [_wgrad(pm, dypb, "wgrad_pool_out", 512, 256), _wgrad(z, dyrb, "wgrad_rnn_out", 1024, 256)]
    lz1_b, sums_b = pair_sums(names_b, part_b, "mix")
    dvec = jnp.concatenate([dvec_mix[0:9], dvec_fin[0:1], dvec_ffn[0:1], jnp.pad(loss_part, ((0, 0), (0, DR - 1))),
                            jnp.zeros((VEC_ROWS - 12, DR), F32)], axis=0)
    g_in, exchanged = _wgrad(
        dproj, h1, "wgrad_in", 1152, 1024,
        exchange=_Both(_Scatter([by_name[n] for n in names_b]), _AllReduce([((MAT_ROWS, HD), 0), ((VEC_ROWS, DR), 1)])),
        exchange_operands=sums_b + [dmat, dvec])
    lz2_b, (mat, vec) = exchanged[:2], exchanged[2:]
    loss = vec[VEC_LOSS, 0]
    lz1_c, sums_c = pair_sums(("w_in",), [g_in], "in")
    (grad_x, dvec_in), lz2_c = _in_bwd(dproj, xs, dx2, norm_mix, w_in_g,
                                       exchange=_Scatter([by_name["w_in"]]), exchange_operands=sums_c)
    (vec_in,) = _all_reduce([dvec_in], [((8, D), 1)], "all_reduce_norm_mix")

    grads, delta, new_m, new_v = {}, {}, {}, {}
    for n, g, l1, l2 in zip(names_a + names_b + ("w_in",), part_a + part_b + [g_in], lz1_a + lz1_b + lz1_c,
                            lz2_a + lz2_b + lz2_c):
        T = by_name[n]
        out = _final_sum(T, g, l1, l2, where, held(T, w[n][0]), held(T, m[n][0]), held(T, v[n][0]))
        grads[n], delta[n], new_m[n], new_v[n] = (held(T, a) for a in out)
    me = 4 * xi + 2 * yi + ci
    small_grads = dict(
        w_pool_grp=mat[0:MAT_WA], w_rg_a=mat[MAT_WA:MAT_WX], w_rg_x=mat[MAT_WX:MAT_ROWS],
        pool_scale=vec[VEC_SCALE:VEC_SCALE + 1, 0:DP], conv_b=vec[VEC_CONV_B:VEC_CONV_B + 1],
        b_rg_a=vec[VEC_BA:VEC_BA + 1], b_rg_x=vec[VEC_BX:VEC_BX + 1], lru_lambda=vec[VEC_LAM:VEC_LAM + 1],
        conv_w=lax.dynamic_slice(vec, (VEC_CONV_W, VEC_PIECE * me), (4, VEC_PIECE)),
        norm_final=vec[VEC_NORM_FINAL:VEC_NORM_FINAL + 1], norm_ffn=vec[VEC_NORM_FFN:VEC_NORM_FFN + 1],
        norm_mix=vec_in[0:1])
    names = list(small_grads)
    as2d = lambda a, g: a.reshape(g.shape)
    upd = _adam_small([small_grads[n] for n in names],
                      [(as2d(w[n], small_grads[n]), as2d(m[n], small_grads[n]), as2d(v[n], small_grads[n])) for n in names])
    for i, n in enumerate(names):
        grads[n] = small_grads[n]
        delta[n], new_m[n], new_v[n] = upd[3 * i:3 * i + 3]

    shaped = lambda d: [d[n].reshape(w[n].shape) for n in WEIGHT_NAMES]
    return (loss, grad_x[None], *shaped(grads), *shaped(delta), *shaped(new_m), *shaped(new_v))
```

```python
import math

import jax
import jax.numpy as jnp
from jax import lax
from jax.experimental import pallas as pl
from jax.experimental.pallas import tpu as pltpu

F32 = jnp.float32
BF16 = jnp.bfloat16

D = 1024
DP = 512
PG = 128
WINDOWS = (2, 4, 8, 16)
DR = 1024
NH = 8
HD = 128
DIN = 4608
DFF = 2816
EPS = 1e-6
LRU_C = 8.0
POOL_HALO = 16
CONV_HALO = 8
KEPT = 3

ADAM_LR = 0.001
ADAM_B1 = 0.9
ADAM_B2 = 0.999
ADAM_EPS = 1e-08
ADAM_WD = 0.01
ADAM_STEP = 10

VMEM_LIMIT = 56 * 1024 * 1024
MESH_AXES = ("x", "y", "c")
MESH = pl.DeviceIdType.MESH


def _dot(a, b):
    return jnp.dot(a, b, preferred_element_type=F32)


def _dot_nt(a, b):
    return lax.dot_general(a, b, (((1,), (1,)), ((), ())), preferred_element_type=F32)


def _dot_tn(a, b):
    return lax.dot_general(a, b, (((0,), (0,)), ((), ())), preferred_element_type=F32)


def _params(*sem):
    return pltpu.CompilerParams(dimension_semantics=sem, vmem_limit_bytes=VMEM_LIMIT)


def _resident(shape):
    nd = len(shape)
    return pl.BlockSpec(shape, lambda i: (0,) * nd, pipeline_mode=pl.Buffered(1))


def _rows(shape_cols, tm):
    return pl.BlockSpec((tm, shape_cols), lambda i: (i, 0))


def _call(body, name, grid, in_specs, out_specs, out_shape, operands, scratch_shapes=(), exchange=None, exchange_operands=()):
    n_in, n_out, n_scr = len(in_specs), len(out_specs), len(scratch_shapes)
    steps = math.prod(grid)
    if exchange is None:
        outs = pl.pallas_call(body, name=name, grid=grid, in_specs=in_specs, out_specs=out_specs, out_shape=out_shape,
                              scratch_shapes=list(scratch_shapes), compiler_params=_params(*["arbitrary"] * len(grid)))(*operands)
        return outs, []
    e_in, e_out = len(exchange.in_specs), len(exchange.out_specs)

    def hosted(*refs):
        ins, refs = refs[:n_in], refs[n_in:]
        e_ins, refs = refs[:e_in], refs[e_in:]
        outs, refs = refs[:n_out], refs[n_out:]
        e_outs, refs = refs[:e_out], refs[e_out:]
        scr, e_scr = refs[:n_scr], refs[n_scr:]
        step = pl.program_id(0)
        for axis in range(1, len(grid)):
            step = step * grid[axis] + pl.program_id(axis)
        @pl.when(step == 0)
        def _():
            _enter(exchange)
            exchange.start(e_ins, e_outs, e_scr)

        for at, middle in exchange.middles(steps):
            pl.when(step == at)(lambda middle=middle: middle(e_ins, e_outs, e_scr))
        body(*ins, *outs, *scr)
        pl.when(step == steps - 1)(lambda: exchange.finish(e_ins, e_outs, e_scr))

    outs = pl.pallas_call(
        hosted, name=name, grid=grid, in_specs=list(in_specs) + exchange.in_specs,
        out_specs=list(out_specs) + exchange.out_specs, out_shape=list(out_shape) + exchange.out_shape,
        scratch_shapes=list(scratch_shapes) + exchange.scratch_shapes,
        compiler_params=pltpu.CompilerParams(dimension_semantics=("arbitrary",) * len(grid), vmem_limit_bytes=VMEM_LIMIT,
                                             collective_id=exchange.collective_id))(*operands, *exchange_operands)
    return outs[:n_out], outs[n_out:]


def _enter(exchange):
    peers = exchange.peers()
    if peers:
        barrier = pltpu.get_barrier_semaphore()
        for peer in peers:
            pl.semaphore_signal(barrier, inc=1, device_id=peer, device_id_type=MESH)
        pl.semaphore_wait(barrier, len(peers))


GELU_C = math.sqrt(2.0 / math.pi)
GELU_K = 0.044715 * GELU_C


def _gelu(x, with_grad=False):
    x2 = x * x
    t = jnp.tanh(x * (GELU_C + GELU_K * x2))
    hx = 0.5 * x
    y = hx + hx * t
    if not with_grad:
        return y
    return y, 0.5 + 0.5 * t + hx * (1.0 - t * t) * (GELU_C + (3.0 * GELU_K) * x2)


def _softplus_neg(lam):
    z = jnp.exp(-jnp.abs(lam))
    u = 1.0 + z
    dlt = u - 1.0
    log1p = jnp.where(dlt == 0.0, z, jnp.log(u) * (z / jnp.where(dlt == 0.0, 1.0, dlt)))
    return jnp.maximum(-lam, 0.0) + log1p


def _sigmoid(x):
    return 0.5 * jnp.tanh(0.5 * x) + 0.5


def _linear_scan(out_ref, A, B, h0, reverse):
    n = A.shape[0]
    sub = lax.broadcasted_iota(jnp.int32, (8, 1), 0)
    tiles = range(n // 8 - 1, -1, -1) if reverse else range(n // 8)
    carry = h0
    for j in tiles:
        a, b = A[8 * j:8 * j + 8, :], B[8 * j:8 * j + 8, :]
        for d in (1, 2, 4):
            keep = (sub < 8 - d) if reverse else (sub >= d)
            shift = 8 - d if reverse else d
            b = jnp.where(keep, a * pltpu.roll(b, shift, axis=0) + b, b)
            a = jnp.where(keep, a * pltpu.roll(a, shift, axis=0), a)
        h = a * carry + b
        out_ref[8 * j:8 * j + 8, :] = h
        carry = h[0:1, :] if reverse else h[7:8, :]
    return carry


def _pool_windows(ext, shift_sign):
    n = ext.shape[0]
    s = ext
    outs = []
    for w in WINDOWS:
        d = w // 2
        s = s + pltpu.roll(s, d if shift_sign > 0 else n - d, axis=0)
        outs.append(s[:, :PG])
        s = s[:, PG:]
    return outs


def _conv_taps(uext):
    taps = []
    for k in range(4):
        sh = 3 - k
        v = uext if sh == 0 else pltpu.roll(uext, sh, axis=0)
        taps.append(v[CONV_HALO:, :])
    return taps


def _gates(v, wa_ref, ba_ref, wx_ref, bx_ref, sp):
    vb = v.astype(BF16)
    ra, rx = [], []
    for h in range(NH):
        vh = vb[:, h * HD:(h + 1) * HD]
        ra.append(_dot(vh, wa_ref[h]))
        rx.append(_dot(vh, wx_ref[h]))
    r = _sigmoid(jnp.concatenate(ra, axis=1) + ba_ref[...])
    i = _sigmoid(jnp.concatenate(rx, axis=1) + bx_ref[...])
    log_a = r * ((-LRU_C) * sp)
    a = jnp.exp(log_a)
    one_minus = -jnp.tanh(log_a) * (1.0 + a * a)
    return r, i, a, jnp.sqrt(one_minus), lax.rsqrt(one_minus)


def _mixer_fwd(proj, wg, scale, w_pool_out, conv_w, conv_b, wa, ba, wx, bx, lam, w_rnn_out, exchange=None,
               exchange_operands=(), tm=256):
    S = proj.shape[0]
    UW = DP + 2 * DR

    def body(proj_ref, wg_ref, scale_ref, wpo_ref, cw_ref, cb_ref, wa_ref, ba_ref, wx_ref, bx_ref, lam_ref, wro_ref,
             pm_ref, ypool_ref, hr_ref, z_ref, yrnn_ref, kept_ref, gates_ref, pool_carry, conv_carry, h_carry):
        i = pl.program_id(0)

        @pl.when(i == 0)
        def _():
            pool_carry[...] = jnp.zeros_like(pool_carry)
            conv_carry[...] = jnp.zeros_like(conv_carry)
            h_carry[...] = jnp.zeros_like(h_carry)

        rows = lax.broadcasted_iota(jnp.int32, (tm, 1), 0)
        t_glob = i * tm + rows

        u_pool = proj_ref[:, 0:DP]
        ext = jnp.concatenate([pool_carry[...], u_pool], axis=0)
        pool_carry[...] = u_pool[tm - POOL_HALO:, :]
        sums = _pool_windows(ext, +1)
        mixed = []
        for g, w in enumerate(WINDOWS):
            inv_cnt = 1.0 / jnp.minimum(t_glob + 1, w).astype(F32)
            pooled_g = sums[g][POOL_HALO:, :] * inv_cnt - u_pool[:, g * PG:(g + 1) * PG]
            mixed.append(_dot(pooled_g.astype(BF16), wg_ref[g]))
        pm = (jnp.concatenate(mixed, axis=1) * scale_ref[...]).astype(BF16)
        pm_ref[...] = pm
        ypool_ref[...] = _dot(pm, wpo_ref[...]).astype(BF16)

        u_rnn = proj_ref[:, DP:DP + DR]
        uext = jnp.concatenate([conv_carry[...], u_rnn], axis=0)
        conv_carry[...] = u_rnn[tm - CONV_HALO:, :]
        taps = _conv_taps(uext)
        v = cb_ref[...]
        for k in range(4):
            v = v + taps[k] * cw_ref[k:k + 1, :]
        sp = _softplus_neg(lam_ref[...])
        r, gi, a, mult, _ = _gates(v, wa_ref, ba_ref, wx_ref, bx_ref, sp)
        for k, kept in enumerate((v, a, mult)):
            kept_ref[k] = kept
        for k, kept in enumerate((r, gi)):
            gates_ref[k] = kept.astype(BF16)
        h_carry[0:1, :] = _linear_scan(hr_ref, a, mult * gi * v, h_carry[0:1, :], reverse=False)
        z = (hr_ref[...] * _gelu(proj_ref[:, DP + DR:UW])).astype(BF16)
        z_ref[...] = z
        yrnn_ref[...] = _dot(z, wro_ref[...]).astype(BF16)

    return _call(
        body, "mixer_fwd", (S // tm,),
        in_specs=[_rows(UW, tm), _resident((4, PG, PG)), _resident((1, DP)), _resident((DP, D)), _resident((4, DR)),
                  _resident((1, DR)), _resident((NH, HD, HD)), _resident((1, DR)), _resident((NH, HD, HD)),
                  _resident((1, DR)), _resident((1, DR)), _resident((DR, D))],
        out_specs=[_rows(DP, tm), _rows(D, tm), _rows(DR, tm), _rows(DR, tm), _rows(D, tm),
                   pl.BlockSpec((KEPT, tm, DR), lambda i: (0, i, 0)), pl.BlockSpec((2, tm, DR), lambda i: (0, i, 0))],
        out_shape=[jax.ShapeDtypeStruct((S, DP), BF16),
                   jax.ShapeDtypeStruct((S, D), BF16), jax.ShapeDtypeStruct((S, DR), F32),
                   jax.ShapeDtypeStruct((S, DR), BF16), jax.ShapeDtypeStruct((S, D), BF16),
                   jax.ShapeDtypeStruct((KEPT, S, DR), F32), jax.ShapeDtypeStruct((2, S, DR), BF16)],
        scratch_shapes=[pltpu.VMEM((POOL_HALO, DP), F32), pltpu.VMEM((CONV_HALO, DR), F32), pltpu.VMEM((8, DR), F32)],
        operands=(proj, wg, scale, w_pool_out, conv_w, conv_b, wa, ba, wx, bx, lam, w_rnn_out),
        exchange=exchange, exchange_operands=exchange_operands)


FF_CHUNKS = ((0, 768), (768, 1536), (1536, 2304), (2304, DFF))


def _rms(x):
    r = lax.rsqrt(jnp.mean(x * x, axis=-1, keepdims=True) + EPS)
    return r, x * r


def _rms_bwd(dh, g, r, xh):
    dxh = dh * g
    return r * (dxh - xh * jnp.mean(dxh * xh, axis=-1, keepdims=True))


def _merge_out(x, proj, y_pool, y_rnn, w_o, norm_ffn, exchange=None, exchange_operands=(), tm=512):
    S = x.shape[0]
    GL0 = (DP + 2 * DR) // 512

    def gl_spec(k):
        return pl.BlockSpec((tm, 512), lambda i: (i, GL0 + k))

    def body(x_ref, gl0, gl1, gl2, gl3, yp_ref, yr_ref, wo_ref, gf_ref, mix_ref, x2_ref, h2_ref):
        s_p = _sigmoid(jnp.concatenate([gl0[...], gl1[...]], axis=1))
        s_r = _sigmoid(jnp.concatenate([gl2[...], gl3[...]], axis=1))
        mix = (s_p * yp_ref[...].astype(F32) + s_r * yr_ref[...].astype(F32)).astype(BF16)
        mix_ref[...] = mix
        x2 = x_ref[...] + _dot(mix, wo_ref[...])
        x2_ref[...] = x2
        _, xh2 = _rms(x2)
        h2_ref[...] = (xh2 * gf_ref[...]).astype(BF16)

    return _call(
        body, "merge_out", (S // tm,),
        in_specs=[_rows(D, tm), gl_spec(0), gl_spec(1), gl_spec(2), gl_spec(3), _rows(D, tm), _rows(D, tm),
                  _resident((D, D)), _resident((1, D))],
        out_specs=[_rows(D, tm), _rows(D, tm), _rows(D, tm)],
        out_shape=[jax.ShapeDtypeStruct((S, D), BF16), jax.ShapeDtypeStruct((S, D), F32), jax.ShapeDtypeStruct((S, D), BF16)],
        operands=(x, proj, proj, proj, proj, y_pool, y_rnn, w_o, norm_ffn),
        exchange=exchange, exchange_operands=exchange_operands)


def _ffn_up(h2, w_lo, w_hi, exchange=None, exchange_operands=(), tm=512):
    S = h2.shape[0]
    HALF = D // 2

    def body(h_ref, lo_ref, hi_ref, back_ref, act_ref):
        h_lo, h_hi = h_ref[:, 0:HALF], h_ref[:, HALF:D]
        for c0, c1 in FF_CHUNKS:
            gate = _dot_nt(h_lo, lo_ref[c0:c1, :]) + _dot_nt(h_hi, hi_ref[c0:c1, :])
            up = _dot_nt(h_lo, lo_ref[DFF + c0:DFF + c1, :]) + _dot_nt(h_hi, hi_ref[DFF + c0:DFF + c1, :])
            sg = _sigmoid(gate)
            silu = gate * sg
            back_ref[:, c0:c1] = (up * (sg * (1.0 + gate * (1.0 - sg)))).astype(BF16)
            back_ref[:, DFF + c0:DFF + c1] = silu.astype(BF16)
            act_ref[:, c0:c1] = (silu * up).astype(BF16)

    return _call(
        body, "ffn_up", (S // tm,),
        in_specs=[_rows(D, tm), _resident((2 * DFF, HALF)), _resident((2 * DFF, HALF))],
        out_specs=[_rows(2 * DFF, tm), _rows(DFF, tm)],
        out_shape=[jax.ShapeDtypeStruct((S, 2 * DFF), BF16), jax.ShapeDtypeStruct((S, DFF), BF16)],
        operands=(h2, w_lo, w_hi), exchange=exchange, exchange_operands=exchange_operands)


def _ffn_down_loss(act, x2, target, w_ffn_out, norm_final, tm=512):
    S = act.shape[0]

    def body(act_ref, x2_ref, t_ref, w_ref, gn_ref, dx3_ref, dx3b_ref, loss_ref, dvec_ref):
        i = pl.program_id(0)

        @pl.when(i == 0)
        def _():
            loss_ref[...] = jnp.zeros_like(loss_ref)
            dvec_ref[...] = jnp.zeros_like(dvec_ref)

        x3 = x2_ref[...] + _dot(act_ref[...], w_ref[...])
        r3, xh3 = _rms(x3)
        g_fin = gn_ref[...]
        e = xh3 * g_fin - t_ref[...]
        loss_ref[...] += jnp.sum(e * e, axis=(0, 1), keepdims=True) * (0.5 / D)
        dy = e * (1.0 / D)
        dvec_ref[0:1, :] += jnp.sum(dy * xh3, axis=0, keepdims=True)
        dx3 = _rms_bwd(dy, g_fin, r3, xh3)
        dx3_ref[...] = dx3
        dx3b_ref[...] = dx3.astype(BF16)

    return pl.pallas_call(
        body, name="ffn_down_loss", grid=(S // tm,),
        in_specs=[_rows(DFF, tm), _rows(D, tm), _rows(D, tm), _resident((DFF, D)), _resident((1, D))],
        out_specs=[_rows(D, tm), _rows(D, tm), _resident((1, 1)), _resident((8, D))],
        out_shape=[jax.ShapeDtypeStruct((S, D), F32), jax.ShapeDtypeStruct((S, D), BF16),
                   jax.ShapeDtypeStruct((1, 1), F32), jax.ShapeDtypeStruct((8, D), F32)],
        compiler_params=_params("arbitrary"),
    )(act, x2, target, w_ffn_out, norm_final)


def _ffn_bwd_down(dx3b, gu, w_ffn_out, tm=512):
    S = dx3b.shape[0]

    def body(d_ref, back_ref, w_ref, dgu_ref):
        d = d_ref[...]
        for c0, c1 in FF_CHUNKS:
            dact = _dot_nt(d, w_ref[c0:c1, :])
            dgu_ref[:, c0:c1] = (dact * back_ref[:, c0:c1].astype(F32)).astype(BF16)
            dgu_ref[:, DFF + c0:DFF + c1] = (dact * back_ref[:, DFF + c0:DFF + c1].astype(F32)).astype(BF16)

    return pl.pallas_call(
        body, name="ffn_bwd_down", grid=(S // tm,),
        in_specs=[_rows(D, tm), _rows(2 * DFF, tm), _resident((DFF, D))],
        out_specs=_rows(2 * DFF, tm),
        out_shape=jax.ShapeDtypeStruct((S, 2 * DFF), BF16),
        compiler_params=_params("parallel"),
    )(dx3b, gu, w_ffn_out)


def _ffn_bwd_up(dgu, x2, dx3, w_lo, w_hi, norm_ffn, w_o, tm=512):
    S = dgu.shape[0]
    HALF = D // 2

    def body(dgu_ref, x2_ref, dx3_ref, lo_ref, hi_ref, gf_ref, wo_ref, dx2_ref, dx2b_ref, dmixo_ref, dvec_ref):
        i = pl.program_id(0)

        @pl.when(i == 0)
        def _():
            dvec_ref[...] = jnp.zeros_like(dvec_ref)

        dgate, dup = dgu_ref[:, 0:DFF], dgu_ref[:, DFF:2 * DFF]
        dh2 = jnp.concatenate([_dot(dgate, w[0:DFF, :]) + _dot(dup, w[DFF:2 * DFF, :]) for w in (lo_ref, hi_ref)], axis=1)
        r2, xh2 = _rms(x2_ref[...])
        dvec_ref[0:1, :] += jnp.sum(dh2 * xh2, axis=0, keepdims=True)
        dx2 = dx3_ref[...] + _rms_bwd(dh2, gf_ref[...], r2, xh2)
        dx2_ref[...] = dx2
        dx2b = dx2.astype(BF16)
        dx2b_ref[...] = dx2b
        dmixo_ref[...] = _dot_nt(dx2b, wo_ref[...]).astype(BF16)

    return pl.pallas_call(
        body, name="ffn_bwd_up", grid=(S // tm,),
        in_specs=[_rows(2 * DFF, tm), _rows(D, tm), _rows(D, tm), _resident((2 * DFF, HALF)), _resident((2 * DFF, HALF)),
                  _resident((1, D)), _resident((D, D))],
        out_specs=[_rows(D, tm), _rows(D, tm), _rows(D, tm), _resident((8, D))],
        out_shape=[jax.ShapeDtypeStruct((S, D), F32), jax.ShapeDtypeStruct((S, D), BF16), jax.ShapeDtypeStruct((S, D), BF16),
                   jax.ShapeDtypeStruct((8, D), F32)],
        compiler_params=_params("arbitrary"),
    )(dgu, x2, dx3, w_lo, w_hi, norm_ffn, w_o)


VEC_ROWS = 16
MAT_WA = 4 * PG
MAT_WX = MAT_WA + NH * HD
MAT_ROWS = MAT_WX + NH * HD


def _mixer_bwd(proj, dmixo, y_pool, y_rnn, hr, kept, gates, wg, scale, w_pool_out, conv_w, conv_b, wa, ba, wx, bx, lam, w_rnn_out,
               exchange=None, exchange_operands=(), tm=256):
    S = proj.shape[0]
    nt = S // tm

    def rev(cols):
        return pl.BlockSpec((tm, cols), lambda i: (nt - 1 - i, 0))

    def halo(rows_, cols):
        per = tm // rows_
        return pl.BlockSpec((rows_, cols), lambda i: (jnp.maximum((nt - 1 - i) * per - 1, 0), 0))

    def body(proj_ref, projh_ref, dmixo_ref, yp_ref, yr_ref, hr_ref, hrh_ref, kept_ref, gates_ref, wg_ref, scale_ref, wpo_ref, cw_ref, cb_ref,
             wa_ref, ba_ref, wx_ref, bx_ref, lam_ref, wro_ref,
             dproj_ref, dypb_ref, dyrb_ref, dmat_ref, dvec_ref,
             q_carry, dv_carry, a_carry, g_carry, g_scr):
        i = pl.program_id(0)
        ti = nt - 1 - i

        @pl.when(i == 0)
        def _():
            q_carry[...] = jnp.zeros_like(q_carry)
            dv_carry[...] = jnp.zeros_like(dv_carry)
            a_carry[...] = jnp.zeros_like(a_carry)
            g_carry[...] = jnp.zeros_like(g_carry)
            dmat_ref[...] = jnp.zeros_like(dmat_ref)
            dvec_ref[...] = jnp.zeros_like(dvec_ref)

        rows = lax.broadcasted_iota(jnp.int32, (tm, 1), 0)
        t_glob = ti * tm + rows
        has_prev = (ti > 0).astype(F32)
        dmixo = dmixo_ref[...].astype(F32)

        s_p = _sigmoid(proj_ref[:, DP + 2 * DR:DP + 2 * DR + D])
        s_r = _sigmoid(proj_ref[:, DP + 2 * DR + D:DIN])
        dproj_ref[:, DP + 2 * DR:DP + 2 * DR + D] = (dmixo * yp_ref[...].astype(F32) * s_p * (1.0 - s_p)).astype(BF16)
        dproj_ref[:, DP + 2 * DR + D:DIN] = (dmixo * yr_ref[...].astype(F32) * s_r * (1.0 - s_r)).astype(BF16)
        dyp = (dmixo * s_p).astype(BF16)
        dyr = (dmixo * s_r).astype(BF16)
        dypb_ref[...] = dyp
        dyrb_ref[...] = dyr

        dz = _dot_nt(dyr, wro_ref[...])
        u_gate = proj_ref[:, DP + DR:DP + 2 * DR]
        gg, dgelu = _gelu(u_gate, with_grad=True)
        hr_t = hr_ref[...]
        dproj_ref[:, DP + DR:DP + 2 * DR] = (dz * hr_t * dgelu).astype(BF16)
        dhr = dz * gg

        sp = _softplus_neg(lam_ref[...])
        v, a, mult = (kept_ref[k] for k in range(KEPT))
        r, gi = (gates_ref[k].astype(F32) for k in range(2))
        inv_mult = 1.0 / mult

        C = jnp.where(rows == tm - 1, a_carry[0:1, :], pltpu.roll(a, tm - 1, axis=0))
        g_carry[0:1, :] = _linear_scan(g_scr, C, dhr, g_carry[0:1, :], reverse=True)
        a_carry[0:1, :] = a[0:1, :]
        g = g_scr[...]

        h_prev = jnp.where(rows == 0, hrh_ref[7:8, :] * has_prev, pltpu.roll(hr_t, 1, axis=0))
        da = g * h_prev
        gm = g * mult
        dmult = g * gi * v
        di = gm * v
        dv = gm * gi
        dlog_a = da * a - dmult * (a * a * inv_mult)
        dvec_ref[4:5, :] += jnp.sum(dlog_a * r, axis=0, keepdims=True)
        dra = (dlog_a * ((-LRU_C) * sp) * r * (1.0 - r))
        drx = di * gi * (1.0 - gi)
        dvec_ref[2:3, :] += jnp.sum(dra, axis=0, keepdims=True)
        dvec_ref[3:4, :] += jnp.sum(drx, axis=0, keepdims=True)
        drab = dra.astype(BF16)
        drxb = drx.astype(BF16)
        vb = v.astype(BF16)
        dvg = []
        for h in range(NH):
            sl = slice(h * HD, (h + 1) * HD)
            dvg.append(_dot_nt(drab[:, sl], wa_ref[h]) + _dot_nt(drxb[:, sl], wx_ref[h]))
            dmat_ref[MAT_WA + h * HD:MAT_WA + (h + 1) * HD, :] += _dot_tn(vb[:, sl], drab[:, sl])
            dmat_ref[MAT_WX + h * HD:MAT_WX + (h + 1) * HD, :] += _dot_tn(vb[:, sl], drxb[:, sl])
        dv = dv + jnp.concatenate(dvg, axis=1)
        dvec_ref[1:2, :] += jnp.sum(dv, axis=0, keepdims=True)
        dvext = jnp.concatenate([dv, dv_carry[...]], axis=0)
        dv_carry[...] = dv[0:CONV_HALO, :]
        n = tm + CONV_HALO
        u_rnn = proj_ref[:, DP:DP + DR]
        du_rnn = dv * cw_ref[3:4, :]
        dvec_ref[8:9, :] += jnp.sum(dv * u_rnn, axis=0, keepdims=True)
        for k in range(3):
            dv_k = pltpu.roll(dvext, n - (3 - k), axis=0)[0:tm, :]
            du_rnn = du_rnn + dv_k * cw_ref[k:k + 1, :]
            dvec_ref[5 + k:6 + k, :] += jnp.sum(dv_k * u_rnn, axis=0, keepdims=True)
        dproj_ref[:, DP:DP + DR] = du_rnn.astype(BF16)

        dpm = _dot_nt(dyp, wpo_ref[...])
        u_pool = proj_ref[:, 0:DP]
        ext = jnp.concatenate([projh_ref[:, 0:DP] * has_prev, u_pool], axis=0)
        sums = _pool_windows(ext, +1)
        scale_v = scale_ref[...]
        qs = []
        dpooled = []
        dscale = []
        for gi_, w in enumerate(WINDOWS):
            sl = slice(gi_ * PG, (gi_ + 1) * PG)
            inv_cnt = 1.0 / jnp.minimum(t_glob + 1, w).astype(F32)
            pooled_b = (sums[gi_][POOL_HALO:, :] * inv_cnt - u_pool[:, sl]).astype(BF16)
            mixed_g = _dot(pooled_b, wg_ref[gi_])
            dscale.append(jnp.sum(dpm[:, sl] * mixed_g, axis=0, keepdims=True))
            dmixed_b = (dpm[:, sl] * scale_v[:, sl]).astype(BF16)
            dmat_ref[gi_ * PG:(gi_ + 1) * PG, :] += _dot_tn(pooled_b, dmixed_b)
            dp_g = _dot_nt(dmixed_b, wg_ref[gi_])
            dpooled.append(dp_g)
            qs.append(dp_g * inv_cnt)
        dvec_ref[0:1, 0:DP] += jnp.concatenate(dscale, axis=1)
        q = jnp.concatenate(qs, axis=1)
        qext = jnp.concatenate([q, q_carry[...]], axis=0)
        q_carry[...] = q[0:POOL_HALO, :]
        tsum = _pool_windows(qext, -1)
        for gi_ in range(4):
            dproj_ref[:, gi_ * PG:(gi_ + 1) * PG] = (tsum[gi_][0:tm, :] - dpooled[gi_]).astype(BF16)

        @pl.when(i == nt - 1)
        def _():
            dvec_ref[4:5, :] = dvec_ref[4:5, :] * (LRU_C * _sigmoid(-lam_ref[...]))

    return _call(
        body, "mixer_bwd", (nt,),
        in_specs=[rev(DIN), halo(POOL_HALO, DIN), rev(D), rev(D), rev(D), rev(DR), halo(8, DR),
                  pl.BlockSpec((KEPT, tm, DR), lambda i: (0, nt - 1 - i, 0)),
                  pl.BlockSpec((2, tm, DR), lambda i: (0, nt - 1 - i, 0)), _resident((4, PG, PG)), _resident((1, DP)), _resident((DP, D)), _resident((4, DR)), _resident((1, DR)),
                  _resident((NH, HD, HD)), _resident((1, DR)), _resident((NH, HD, HD)), _resident((1, DR)),
                  _resident((1, DR)), _resident((DR, D))],
        out_specs=[rev(DIN), rev(D), rev(D), _resident((MAT_ROWS, HD)), _resident((VEC_ROWS, DR))],
        out_shape=[jax.ShapeDtypeStruct((S, DIN), BF16), jax.ShapeDtypeStruct((S, D), BF16),
                   jax.ShapeDtypeStruct((S, D), BF16), jax.ShapeDtypeStruct((MAT_ROWS, HD), F32),
                   jax.ShapeDtypeStruct((VEC_ROWS, DR), F32)],
        scratch_shapes=[pltpu.VMEM((POOL_HALO, DP), F32), pltpu.VMEM((CONV_HALO, DR), F32), pltpu.VMEM((8, DR), F32),
                        pltpu.VMEM((8, DR), F32), pltpu.VMEM((tm, DR), F32)],
        operands=(proj, proj, dmixo, y_pool, y_rnn, hr, hr, kept, gates, wg, scale, w_pool_out, conv_w, conv_b, wa, ba, wx, bx, lam,
                  w_rnn_out),
        exchange=exchange, exchange_operands=exchange_operands)


def _in_bwd(dproj, x, dx2, norm_mix, w_in, exchange=None, exchange_operands=(), tm=512):
    S = x.shape[0]

    def body(dp_ref, x_ref, dx2_ref, g_ref, w_ref, dx_ref, dg_ref):
        i = pl.program_id(0)

        @pl.when(i == 0)
        def _():
            dg_ref[...] = jnp.zeros_like(dg_ref)

        dh = _dot(dp_ref[:, 0:1536], w_ref[0:1536, :])
        dh = dh + _dot(dp_ref[:, 1536:3072], w_ref[1536:3072, :])
        dh = dh + _dot(dp_ref[:, 3072:DIN], w_ref[3072:DIN, :])
        xv = x_ref[...]
        r = lax.rsqrt(jnp.mean(xv * xv, axis=-1, keepdims=True) + EPS)
        xh = xv * r
        dg_ref[0:1, :] += jnp.sum(dh * xh, axis=0, keepdims=True)
        dxh = dh * g_ref[...]
        dx_ref[...] = dx2_ref[...] + r * (dxh - xh * jnp.mean(dxh * xh, axis=-1, keepdims=True))

    return _call(
        body, "in_bwd", (S // tm,),
        in_specs=[_rows(DIN, tm), _rows(D, tm), _rows(D, tm), _resident((1, D)), _resident((DIN, D))],
        out_specs=[_rows(D, tm), _resident((8, D))],
        out_shape=[jax.ShapeDtypeStruct((S, D), F32), jax.ShapeDtypeStruct((8, D), F32)],
        operands=(dproj, x, dx2, norm_mix, w_in), exchange=exchange, exchange_operands=exchange_operands)


def _wgrad(a, b, name, tk, tn, exchange=None, exchange_operands=()):
    S, K = a.shape
    N = b.shape[1]

    def body(a_ref, b_ref, o_ref):
        o_ref[...] = _dot_tn(a_ref[...], b_ref[...]).astype(BF16)

    (out,), exchanged = _call(
        body, name, (K // tk, N // tn),
        in_specs=[pl.BlockSpec((S, tk), lambda k, n: (0, k)), pl.BlockSpec((S, tn), lambda k, n: (0, n))],
        out_specs=[pl.BlockSpec((tk, tn), lambda k, n: (k, n))],
        out_shape=[jax.ShapeDtypeStruct((K, N), BF16)],
        operands=(a, b), exchange=exchange, exchange_operands=exchange_operands)
    return (out, exchanged) if exchange is not None else out


VEC_SCALE, VEC_CONV_B, VEC_BA, VEC_BX, VEC_LAM, VEC_CONV_W, VEC_NORM_FINAL, VEC_NORM_FFN = 0, 1, 2, 3, 4, 5, 9, 10
VEC_LOSS = 11


class _Big:
    def __init__(self, name, rows, cols, axis, n, dtype=BF16, transposed=False, src_cols=None):
        self.name, self.rows, self.cols, self.axis, self.n, self.dtype = name, rows, cols, axis, n, dtype
        self.transposed = transposed
        self.src_cols = src_cols
        self.block_shape = (rows, n) if axis == 1 else (n, cols)

    def block(self, ref, p):
        if self.axis == 1:
            return ref.at[:, pl.ds(pl.multiple_of(p * self.n, 128), self.n)]
        return ref.at[pl.ds(pl.multiple_of(p * self.n, 16 if self.dtype == BF16 else 8), self.n), :]


BIG = (_Big("w_in", DIN, D, 0, DIN // 8, transposed=True), _Big("w_pool_out", DP, D, 1, D // 8),
       _Big("w_rnn_out", DR, D, 0, DR // 8), _Big("w_o", D, D, 0, D // 8),
       _Big("w_ffn_in", 2 * DFF, D, 0, 2 * DFF // 8, transposed=True), _Big("w_ffn_out", DFF, D, 0, DFF // 8))
CONV_W = _Big("conv_w", 8, DR, 1, DR // 8, F32)
W_FFN_IN_HALVES = (_Big("w_ffn_in_lo", 2 * DFF, D // 2, 0, 2 * DFF // 8, src_cols=(0, D // 2)),
                   _Big("w_ffn_in_hi", 2 * DFF, D // 2, 0, 2 * DFF // 8, src_cols=(D // 2, D)))
GATHERED = BIG + (CONV_W,) + W_FFN_IN_HALVES

HBM_SPEC = pl.BlockSpec(memory_space=pl.ANY)
VMEM_SPEC = pl.BlockSpec(memory_space=pltpu.VMEM)


def _place():
    x, y, c = (lax.axis_index(a) for a in MESH_AXES)
    other_chips = [(1 - x, y), (x, 1 - y), (1 - x, 1 - y)]
    return x, y, c, other_chips


def _remote(src, dst, send_sems, recv_sems, idx, to):
    return pltpu.make_async_remote_copy(src_ref=src, dst_ref=dst, send_sem=send_sems.at[idx], recv_sem=recv_sems.at[idx],
                                        device_id=to, device_id_type=MESH)


def _device_index(chip, core):
    return 4 * chip[0] + 2 * chip[1] + core


class _Gather:
    def __init__(self, tensors):
        self.tensors = tuple(tensors)
        n = len(self.tensors)
        self.in_specs = [HBM_SPEC] * n
        self.out_specs = [HBM_SPEC] * n
        self.out_shape = [jax.ShapeDtypeStruct((T.rows, T.cols), T.dtype) for T in self.tensors]
        self.scratch_shapes = [pltpu.VMEM(T.block_shape, T.dtype) for T in self.tensors] + [
            pltpu.VMEM(T.block_shape, F32) for T in self.tensors] + [
            pltpu.SemaphoreType.DMA((n, 7)), pltpu.SemaphoreType.DMA((n, 7)), pltpu.SemaphoreType.DMA((n, 2))]

    collective_id = 1

    def peers(self):
        x, y, c, _ = _place()
        return [(x, y, 1 - c), (1 - x, y, c), (x, 1 - y, c)]

    def middles(self, steps):
        return [(steps // 2, self.relay), (steps - 1, self.middle)]

    def _copies(self, ins, outs, scratch):
        n = len(self.tensors)
        mine, raw, (send_sems, recv_sems, loc_sems) = scratch[:n], scratch[n:2 * n], scratch[2 * n:]
        x, y, c, chips = _place()
        sibling = (x, y, 1 - c)
        me = _device_index((x, y), c)
        relay_from = (jnp.where(c == 0, 1 - x, x), jnp.where(c == 0, y, 1 - y))
        relay_to = (jnp.where(c == 0, x, 1 - x), jnp.where(c == 0, 1 - y, y))
        loads, stores, first, relays, passed, arrivals, late = [], [], [], [], [], [], []
        for t, T in enumerate(self.tensors):
            place = T.block(outs[t], me)
            src = ins[t] if T.src_cols is None else ins[t].at[:, T.src_cols[0]:T.src_cols[1]]
            loads.append(pltpu.make_async_copy(src, raw[t], loc_sems.at[t, 0]))
            stores.append(pltpu.make_async_copy(mine[t], place, loc_sems.at[t, 1]))
            first.append(_remote(mine[t], place, send_sems, recv_sems, (t, 0), sibling))
            theirs = T.block(outs[t], _device_index((x, y), 1 - c))
            late.append(_remote(theirs, theirs, send_sems, recv_sems, (t, 0), sibling))
            relayed = T.block(outs[t], _device_index(relay_from, c))
            relays.append(_remote(relayed, relayed, send_sems, recv_sems, (t, 3), (*relay_to, c)))
            for k, chip in enumerate(chips):
                if k < 2:
                    first.append(_remote(mine[t], place, send_sems, recv_sems, (t, 1 + k), (*chip, c)))
                land = T.block(outs[t], _device_index(chip, c))
                arrivals.append(_remote(land, land, send_sems, recv_sems, (t, 1 + k), sibling))
                passed.append(_remote(land, land, send_sems, recv_sems, (t, 4 + k), sibling))
                theirs = T.block(outs[t], _device_index(chip, 1 - c))
                late.append(_remote(theirs, theirs, send_sems, recv_sems, (t, 4 + k), sibling))
        return loads, stores, first, relays, passed, arrivals, late

    def start(self, ins, outs, scratch):
        loads, stores, first, _, _, _, _ = self._copies(ins, outs, scratch)
        n = len(self.tensors)
        for cp in loads:
            cp.start()
        for t, cp in enumerate(loads):
            cp.wait()
            scratch[t][...] = scratch[n + t][...].astype(self.tensors[t].dtype)
        for cp in stores + first:
            cp.start()

    def relay(self, ins, outs, scratch, skip=0):
        _, _, _, relays, passed, arrivals, _ = self._copies(ins, outs, scratch)
        for t in range(skip, len(self.tensors)):
            arrivals[3 * t].wait_recv()
            arrivals[3 * t + 1].wait_recv()
            for cp in (relays[t], passed[3 * t], passed[3 * t + 1]):
                cp.start()

    def middle(self, ins, outs, scratch, skip=0):
        _, _, _, _, passed, arrivals, _ = self._copies(ins, outs, scratch)
        for t in range(skip, len(self.tensors)):
            arrivals[3 * t + 2].wait_recv()
            passed[3 * t + 2].start()

    def finish(self, ins, outs, scratch, skip=0):
        _, stores, first, relays, passed, _, late = self._copies(ins, outs, scratch)
        for cp in late[4 * skip:]:
            cp.wait_recv()
        for cp in first + relays + passed:
            cp.wait_send()
        for cp in stores[skip:]:
            cp.wait()


def _in_proj_gather(x, norm_mix, blocks, tensors, order, tm=1024):
    S = x.shape[0]
    nt = S // tm
    n = len(tensors)
    gather = _Gather(tensors)
    CB = 2 * tensors[0].n

    def body(order_ref, x_ref, g_ref, *refs):
        ins, (proj_ref, h_ref), outs = refs[:n], refs[n:n + 2], refs[n + 2:2 * n + 2]
        (h_all, w_chip, w_sem), scratch = refs[2 * n + 2:2 * n + 5], refs[2 * n + 5:]
        q, i = pl.program_id(0), pl.program_id(1)
        _, stores, _, relays, passed, arrivals, late = gather._copies(ins, outs, scratch)

        def fetch(turn):
            rows = outs[0].at[pl.ds(pl.multiple_of(order_ref[turn] * CB, 16), CB), :]
            cp = pltpu.make_async_copy(rows, w_chip, w_sem)
            cp.start()
            cp.wait()

        @pl.when((q == 0) & (i == 0))
        def _():
            _enter(gather)
            gather.start(ins, outs, scratch)
            late[0].wait_recv()
            stores[0].wait()
            fetch(0)

        @pl.when((q == 1) & (i == 0))
        def _():
            arrivals[0].wait_recv()
            arrivals[1].wait_recv()
            for cp in (relays[0], passed[0], passed[1]):
                cp.start()
            late[1].wait_recv()
            fetch(1)

        @pl.when((q == 2) & (i == 0))
        def _():
            late[2].wait_recv()
            fetch(2)
            gather.relay(ins, outs, scratch, skip=1)

        @pl.when((q == 3) & (i == 0))
        def _():
            arrivals[2].wait_recv()
            passed[2].start()
            late[3].wait_recv()
            fetch(3)

        rows = pl.ds(pl.multiple_of(i * tm, tm), tm)

        @pl.when(q == 0)
        def _():
            xv = x_ref[...]
            r = lax.rsqrt(jnp.mean(xv * xv, axis=-1, keepdims=True) + EPS)
            h = (xv * r * g_ref[...]).astype(BF16)
            h_all[rows, :] = h
            h_ref[...] = h

        proj_ref[...] = _dot_nt(h_all[rows, :], w_chip[...])

        @pl.when((q == 3) & (i == nt - 1))
        def _():
            gather.middle(ins, outs, scratch, skip=1)
            gather.finish(ins, outs, scratch, skip=1)

    row_tile = lambda q, i, order: (jnp.where(q == 0, i, nt - 1), 0)
    whole = lambda shape: pl.BlockSpec(shape, lambda q, i, order: (0,) * len(shape), pipeline_mode=pl.Buffered(1))
    outs = pl.pallas_call(
        body, name="in_proj_gather",
        grid_spec=pltpu.PrefetchScalarGridSpec(
            num_scalar_prefetch=1, grid=(4, nt),
            in_specs=[pl.BlockSpec((tm, D), row_tile), whole((1, D))] + gather.in_specs,
            out_specs=[pl.BlockSpec((tm, CB), lambda q, i, order: (i, order[q])), pl.BlockSpec((tm, D), row_tile)]
            + gather.out_specs,
            scratch_shapes=[pltpu.VMEM((S, D), BF16), pltpu.VMEM((CB, D), BF16), pltpu.SemaphoreType.DMA]
            + gather.scratch_shapes),
        out_shape=[jax.ShapeDtypeStruct((S, DIN), F32), jax.ShapeDtypeStruct((S, D), BF16)] + gather.out_shape,
        compiler_params=pltpu.CompilerParams(dimension_semantics=("arbitrary", "arbitrary"), vmem_limit_bytes=VMEM_LIMIT,
                                             collective_id=gather.collective_id),
    )(order, x, norm_mix, *blocks)
    return outs[:2], outs[2:]


PAIR_ROWS = 32


def _pair_reduce(grads, tensors, name):
    nt = len(tensors)

    def body(*refs):
        ins, own_out, sums_out, landed, mine = (refs[k * nt:(k + 1) * nt] for k in range(5))
        send_sems, recv_sems, loc_sems = refs[5 * nt:]
        x, y, c, chips = _place()
        chip_of = [2 * chip[0] + chip[1] for chip in chips]
        barrier = pltpu.get_barrier_semaphore()
        pl.semaphore_signal(barrier, inc=1, device_id=(x, y, 1 - c), device_id_type=MESH)
        pl.semaphore_wait(barrier, 1)
        swaps, loads = [], []
        for t, T in enumerate(tensors):
            for j in range(4):
                swaps.append(_remote(T.block(ins[t], 2 * j + 1 - c), landed[t].at[j], send_sems, recv_sems, (t, j),
                                     (x, y, 1 - c)))
            for k in range(3):
                loads.append(pltpu.make_async_copy(T.block(ins[t], 2 * chip_of[k] + c), mine[t].at[k], loc_sems.at[t, k]))
        for cp in swaps + loads:
            cp.start()
        for cp in loads:
            cp.wait()
        for cp in swaps:
            cp.wait_recv()
        stores = []
        for t, T in enumerate(tensors):
            for k in range(3):
                acc, got = mine[t].at[k], landed[t].at[chip_of[k]]

                def add(i, carry, acc=acc, got=got):
                    rows = pl.ds(pl.multiple_of(i * PAIR_ROWS, PAIR_ROWS), PAIR_ROWS)
                    acc[rows, :] = (acc[rows, :].astype(F32) + got[rows, :].astype(F32)).astype(BF16)
                    return carry

                lax.fori_loop(0, T.block_shape[0] // PAIR_ROWS, add, 0)
            stores.append(pltpu.make_async_copy(mine[t], sums_out[t], loc_sems.at[t, 3]))
            stores.append(pltpu.make_async_copy(landed[t].at[2 * x + y], own_out[t], loc_sems.at[t, 4]))
        for cp in stores:
            cp.start()
        for cp in swaps:
            cp.wait_send()
        for cp in stores:
            cp.wait()

    blocks = [T.block_shape for T in tensors]
    return pl.pallas_call(
        body, name=name,
        in_specs=[HBM_SPEC] * nt, out_specs=[HBM_SPEC] * (2 * nt),
        out_shape=[jax.ShapeDtypeStruct(b, BF16) for b in blocks] + [jax.ShapeDtypeStruct((3,) + b, BF16) for b in blocks],
        scratch_shapes=[pltpu.VMEM((4,) + b, BF16) for b in blocks] + [pltpu.VMEM((3,) + b, BF16) for b in blocks]
        + [pltpu.SemaphoreType.DMA((nt, 4)), pltpu.SemaphoreType.DMA((nt, 4)), pltpu.SemaphoreType.DMA((nt, 5))],
        compiler_params=pltpu.CompilerParams(vmem_limit_bytes=VMEM_LIMIT, collective_id=3),
    )(*grads)


class _Scatter:
    def __init__(self, tensors):
        self.tensors = tuple(tensors)
        n = len(self.tensors)
        self.in_specs = [HBM_SPEC] * n
        self.out_specs = [HBM_SPEC] * n
        self.out_shape = [jax.ShapeDtypeStruct((2,) + T.block_shape, BF16) for T in self.tensors]
        self.scratch_shapes = [pltpu.VMEM(T.block_shape, BF16) for T in self.tensors] * 2 + [
            pltpu.SemaphoreType.DMA((n, 3)), pltpu.SemaphoreType.DMA((n, 3)), pltpu.SemaphoreType.DMA((n,))]

    collective_id = 2

    def peers(self):
        x, y, c, _ = _place()
        return [(1 - x, y, c), (x, 1 - y, c)]

    def middles(self, steps):
        return [(steps // 2, self.middle)]

    def _copies(self, ins, outs, scratch):
        n = len(self.tensors)
        landed, mine, (send_sems, recv_sems, loc_sems) = scratch[:n], scratch[n:2 * n], scratch[2 * n:]
        x, y, c, _ = _place()
        direct = (jnp.where(c == 0, 1 - x, x), jnp.where(c == 0, y, 1 - y), c)
        other = (jnp.where(c == 0, x, 1 - x), jnp.where(c == 0, 1 - y, y), c)
        k_direct = jnp.where(c == 0, 0, 1)
        to_direct, legs, loads, combined, arrivals = [], [], [], [], []
        for t in range(n):
            to_direct.append(_remote(ins[t].at[k_direct], outs[t].at[0], send_sems, recv_sems, (t, 0), direct))
            legs.append(_remote(ins[t].at[2], landed[t], send_sems, recv_sems, (t, 2), direct))
            loads.append(pltpu.make_async_copy(ins[t].at[1 - k_direct], mine[t], loc_sems.at[t]))
            combined.append(_remote(mine[t], outs[t].at[1], send_sems, recv_sems, (t, 1), other))
            arrivals.append(_remote(landed[t], landed[t], send_sems, recv_sems, (t, 2), direct))
        return to_direct, legs, loads, combined, arrivals, landed, mine

    def start(self, ins, outs, scratch):
        to_direct, legs, loads, _, _, _, _ = self._copies(ins, outs, scratch)
        for cp in to_direct + legs + loads:
            cp.start()

    def middle(self, ins, outs, scratch):
        _, _, loads, combined, arrivals, landed, mine = self._copies(ins, outs, scratch)
        for t, T in enumerate(self.tensors):
            loads[t].wait()
            arrivals[t].wait_recv()
            acc, got = mine[t], landed[t]

            def add(i, carry, acc=acc, got=got):
                rows = pl.ds(pl.multiple_of(i * PAIR_ROWS, PAIR_ROWS), PAIR_ROWS)
                acc[rows, :] = (acc[rows, :].astype(F32) + got[rows, :].astype(F32)).astype(BF16)
                return carry

            lax.fori_loop(0, T.block_shape[0] // PAIR_ROWS, add, 0)
            combined[t].start()

    def finish(self, ins, outs, scratch):
        to_direct, legs, _, combined, _, _, _ = self._copies(ins, outs, scratch)
        for cp in to_direct + combined:
            cp.wait()
        for cp in legs:
            cp.wait_send()


def _adamw(w, g, m, v):
    m = ADAM_B1 * m + (1.0 - ADAM_B1) * g
    v = ADAM_B2 * v + (1.0 - ADAM_B2) * (g * g)
    m_hat = m / (1.0 - ADAM_B1 ** ADAM_STEP)
    v_hat = v / (1.0 - ADAM_B2 ** ADAM_STEP)
    delta = -ADAM_LR * (m_hat / (jnp.sqrt(v_hat) + ADAM_EPS) + ADAM_WD * w)
    return delta, m, v


def _final_sum(T, g, lz1, lz2, where, w, m, v):
    rows, cols = T.block_shape
    sub = 4 if T.axis == 0 and rows % 64 == 0 and rows > 256 else 1
    blk = (rows // sub, cols)

    def body(where_ref, g_ref, l1_ref, l2_ref, w_ref, m_ref, v_ref, g_out, d_out, m_out, v_out):
        tot = g_ref[...].astype(F32) + l1_ref[...].astype(F32)
        for k in range(2):
            tot = tot + l2_ref[k].astype(F32)
        g_out[...] = tot
        d_out[...], m_out[...], v_out[...] = _adamw(w_ref[...], tot, m_ref[...], v_ref[...])

    def in_whole(r, wh):
        p = wh[0]
        return (0, p) if T.axis == 1 else (p * sub + r, 0)

    own = pl.BlockSpec(blk, lambda r, wh: (r, 0))
    return pl.pallas_call(
        body, name="grad_final_" + T.name,
        grid_spec=pltpu.PrefetchScalarGridSpec(
            num_scalar_prefetch=1, grid=(sub,),
            in_specs=[pl.BlockSpec(blk, in_whole),
                      own,
                      pl.BlockSpec((2,) + blk, lambda r, wh: (0, r, 0)), own, own, own],
            out_specs=[own] * 4),
        out_shape=[jax.ShapeDtypeStruct(T.block_shape, F32)] * 4,
        compiler_params=_params("arbitrary"),
    )(where, g, lz1, lz2, w, m, v)


VEC_PIECE = DR // 8


class _AllReduce:
    def __init__(self, items):
        self.items = tuple(items)
        n = len(self.items)
        self.in_specs = [HBM_SPEC] * n
        self.out_specs = [HBM_SPEC] * n
        self.out_shape = [jax.ShapeDtypeStruct(shape, F32) for shape, _ in self.items]
        pieces = [(shape[0] // 8, shape[1]) if axis == 0 else (shape[0], shape[1] // 8) for shape, axis in self.items]
        self.scratch_shapes = ([pltpu.VMEM((8,) + p, F32) for p in pieces] + [pltpu.VMEM(p, F32) for p in pieces] + [
            pltpu.SemaphoreType.DMA((2 * n, 8)), pltpu.SemaphoreType.DMA((2 * n, 8)), pltpu.SemaphoreType.DMA((2 * n,))])

    collective_id = 4

    def peers(self):
        x, y, c, _ = _place()
        return [(1 - x if r & 4 else x, 1 - y if r & 2 else y, 1 - c if r & 1 else c) for r in range(1, 8)]

    def middles(self, steps):
        return [(steps // 2, self.middle)]

    def _copies(self, ins, outs, scratch):
        n = len(self.items)
        landed, sums, (send_sems, recv_sems, loc_sems) = scratch[:n], scratch[n:2 * n], scratch[2 * n:]
        x, y, c, _ = _place()
        me = _device_index((x, y), c)

        def peer(r):
            return (1 - x if r & 4 else x, 1 - y if r & 2 else y, 1 - c if r & 1 else c)

        def piece(i, ref, p):
            shape, axis = self.items[i]
            if axis == 0:
                rows = shape[0] // 8
                return ref.at[pl.ds(pl.multiple_of(p * rows, 8), rows), :]
            cols = shape[1] // 8
            return ref.at[:, pl.ds(pl.multiple_of(p * cols, 128), cols)]

        own, scatter, arrivals, keep, spread, late = [], [], [], [], [], []
        for i in range(n):
            own.append(pltpu.make_async_copy(piece(i, ins[i], me), landed[i].at[0], loc_sems.at[2 * i]))
            keep.append(pltpu.make_async_copy(sums[i], piece(i, outs[i], me), loc_sems.at[2 * i + 1]))
            for r in range(1, 8):
                to = peer(r)
                p = _device_index(to[:2], to[2])
                scatter.append(_remote(piece(i, ins[i], p), landed[i].at[r], send_sems, recv_sems, (2 * i, r), to))
                spread.append(_remote(sums[i], piece(i, outs[i], me), send_sems, recv_sems, (2 * i + 1, r), to))
                late.append(_remote(sums[i], piece(i, outs[i], p), send_sems, recv_sems, (2 * i + 1, r), to))
        return own, scatter, keep, spread, late, landed, sums

    def start(self, ins, outs, scratch):
        own, scatter, _, _, _, _, _ = self._copies(ins, outs, scratch)
        for cp in own + scatter:
            cp.start()

    def middle(self, ins, outs, scratch):
        own, scatter, keep, spread, _, landed, sums = self._copies(ins, outs, scratch)
        for cp in own:
            cp.wait()
        for cp in scatter:
            cp.wait_recv()
        for i in range(len(self.items)):
            total = landed[i][0]
            for r in range(1, 8):
                total = total + landed[i][r]
            sums[i][...] = total
        for cp in keep + spread:
            cp.start()

    def finish(self, ins, outs, scratch):
        _, scatter, keep, spread, late, _, _ = self._copies(ins, outs, scratch)
        for cp in late:
            cp.wait_recv()
        for cp in scatter + spread:
            cp.wait_send()
        for cp in keep:
            cp.wait()


class _Both:
    def __init__(self, a, b):
        self.a, self.b = a, b
        self.in_specs, self.out_specs = a.in_specs + b.in_specs, a.out_specs + b.out_specs
        self.out_shape, self.scratch_shapes = a.out_shape + b.out_shape, a.scratch_shapes + b.scratch_shapes

    @property
    def collective_id(self):
        return self.b.collective_id

    def peers(self):
        return self.b.peers()

    def _each(self, ins, outs, scratch):
        a = self.a
        i, o, s = len(a.in_specs), len(a.out_specs), len(a.scratch_shapes)
        return (a, ins[:i], outs[:o], scratch[:s]), (self.b, ins[i:], outs[o:], scratch[s:])

    def middles(self, steps):
        def of(which, middle):
            return lambda ins, outs, scratch: middle(*self._each(ins, outs, scratch)[which][1:])
        return [(at, of(which, middle)) for which, e in enumerate((self.a, self.b)) for at, middle in e.middles(steps)]

    def start(self, ins, outs, scratch):
        for e, i, o, s in self._each(ins, outs, scratch):
            e.start(i, o, s)

    def finish(self, ins, outs, scratch):
        for e, i, o, s in self._each(ins, outs, scratch):
            e.finish(i, o, s)


def _all_reduce(arrays, items, name):
    reduce = _AllReduce(items)
    n = len(items)

    def body(*refs):
        ins, outs, scratch = refs[:n], refs[n:2 * n], refs[2 * n:]
        _enter(reduce)
        reduce.start(ins, outs, scratch)
        reduce.middle(ins, outs, scratch)
        reduce.finish(ins, outs, scratch)

    return pl.pallas_call(
        body, name=name, in_specs=reduce.in_specs, out_specs=reduce.out_specs, out_shape=reduce.out_shape,
        scratch_shapes=reduce.scratch_shapes, compiler_params=pltpu.CompilerParams(collective_id=reduce.collective_id),
    )(*arrays)


def _adam_small(grads, wmv):
    n = len(grads)

    def body(*refs):
        g_refs, rest = refs[:n], refs[n:]
        ins, outs = rest[:3 * n], rest[3 * n:]
        for i in range(n):
            d, m, v = _adamw(ins[3 * i][...], g_refs[i][...], ins[3 * i + 1][...], ins[3 * i + 2][...])
            outs[3 * i][...], outs[3 * i + 1][...], outs[3 * i + 2][...] = d, m, v

    flat = [a for t in wmv for a in t]
    return pl.pallas_call(
        body, name="adam_small",
        in_specs=[VMEM_SPEC] * (4 * n), out_specs=[VMEM_SPEC] * (3 * n),
        out_shape=[jax.ShapeDtypeStruct(a.shape, F32) for a in flat],
    )(*grads, *flat)


WEIGHT_NAMES = ("norm_mix", "w_in", "w_pool_grp", "pool_scale", "w_pool_out", "conv_w", "conv_b", "w_rg_a", "b_rg_a", "w_rg_x",
                "b_rg_x", "lru_lambda", "w_rnn_out", "w_o", "norm_ffn", "w_ffn_in", "w_ffn_out", "norm_final")


def kernel(x, norm_mix, w_in, w_pool_grp, pool_scale, w_pool_out, conv_w, conv_b, w_rg_a, b_rg_a, w_rg_x, b_rg_x, lru_lambda, w_rnn_out, w_o, norm_ffn, w_ffn_in, w_ffn_out, norm_final, loss_target, m_norm_mix, m_w_in, m_w_pool_grp, m_pool_scale, m_w_pool_out, m_conv_w, m_conv_b, m_w_rg_a, m_b_rg_a, m_w_rg_x, m_b_rg_x, m_lru_lambda, m_w_rnn_out, m_w_o, m_norm_ffn, m_w_ffn_in, m_w_ffn_out, m_norm_final, v_norm_mix, v_w_in, v_w_pool_grp, v_pool_scale, v_w_pool_out, v_conv_w, v_conv_b, v_w_rg_a, v_b_rg_a, v_w_rg_x, v_b_rg_x, v_lru_lambda, v_w_rnn_out, v_w_o, v_norm_ffn, v_w_ffn_in, v_w_ffn_out, v_norm_final):
    w = dict(norm_mix=norm_mix, w_in=w_in, w_pool_grp=w_pool_grp, pool_scale=pool_scale, w_pool_out=w_pool_out, conv_w=conv_w,
             conv_b=conv_b, w_rg_a=w_rg_a, b_rg_a=b_rg_a, w_rg_x=w_rg_x, b_rg_x=b_rg_x, lru_lambda=lru_lambda,
             w_rnn_out=w_rnn_out, w_o=w_o, norm_ffn=norm_ffn, w_ffn_in=w_ffn_in, w_ffn_out=w_ffn_out, norm_final=norm_final)
    m = dict(norm_mix=m_norm_mix, w_in=m_w_in, w_pool_grp=m_w_pool_grp, pool_scale=m_pool_scale, w_pool_out=m_w_pool_out,
             conv_w=m_conv_w, conv_b=m_conv_b, w_rg_a=m_w_rg_a, b_rg_a=m_b_rg_a, w_rg_x=m_w_rg_x, b_rg_x=m_b_rg_x,
             lru_lambda=m_lru_lambda, w_rnn_out=m_w_rnn_out, w_o=m_w_o, norm_ffn=m_norm_ffn, w_ffn_in=m_w_ffn_in,
             w_ffn_out=m_w_ffn_out, norm_final=m_norm_final)
    v = dict(norm_mix=v_norm_mix, w_in=v_w_in, w_pool_grp=v_w_pool_grp, pool_scale=v_pool_scale, w_pool_out=v_w_pool_out,
             conv_w=v_conv_w, conv_b=v_conv_b, w_rg_a=v_w_rg_a, b_rg_a=v_b_rg_a, w_rg_x=v_w_rg_x, b_rg_x=v_b_rg_x,
             lru_lambda=v_lru_lambda, w_rnn_out=v_w_rnn_out, w_o=v_w_o, norm_ffn=v_norm_ffn, w_ffn_in=v_w_ffn_in,
             w_ffn_out=v_w_ffn_out, norm_final=v_norm_final)
    xi, yi, ci = (lax.axis_index(a) for a in MESH_AXES)
    chip = 2 * xi + yi

    def held(T, a):
        return jnp.swapaxes(a, 0, 1) if T.transposed else a

    where = jnp.stack([2 * chip + ci]).astype(jnp.int32)
    by_name = {T.name: T for T in GATHERED}
    block = {T.name: held(T, w[T.name][0]) for T in BIG}
    block["conv_w"] = jnp.pad(conv_w[0], ((0, CONV_W.rows - 4), (0, 0)))
    block["w_ffn_in_lo"] = block["w_ffn_in_hi"] = block["w_ffn_in"]

    def gather_of(*names):
        return dict(exchange=_Gather([by_name[n] for n in names]), exchange_operands=[block[n] for n in names])

    def pair_sums(names, partials, tag):
        out = _pair_reduce(partials, [by_name[n] for n in names], "grad_pair_reduce_" + tag)
        return list(out[:len(names)]), list(out[len(names):])

    xs, target = x[0], loss_target[0]
    wg_b, wa_b, wx_b = (a[0].astype(BF16) for a in (w_pool_grp, w_rg_a, w_rg_x))
    ba2, bx2 = b_rg_a.reshape(1, DR), b_rg_x.reshape(1, DR)
    first = ("w_in", "w_pool_out", "w_rnn_out", "conv_w", "w_o")
    order = jnp.stack([chip, 2 * (1 - xi) + yi, 2 * xi + (1 - yi), 2 * (1 - xi) + (1 - yi)]).astype(jnp.int32)
    (proj, h1), (w_in_g, w_pool_out_g, w_rnn_out_g, conv_g, w_o_g) = _in_proj_gather(
        xs, norm_mix, [block[n] for n in first], [by_name[n] for n in first], order)
    mixer_weights = (wg_b, pool_scale, w_pool_out_g, conv_g[0:4], conv_b, wa_b, ba2, wx_b, bx2, lru_lambda, w_rnn_out_g)
    (pm, y_pool, hr, z, y_rnn, kept, gates), (w_ffn_lo_g, w_ffn_hi_g) = _mixer_fwd(
        proj, *mixer_weights, **gather_of("w_ffn_in_lo", "w_ffn_in_hi"))
    (mix, x2, h2), _ = _merge_out(xs, proj, y_pool, y_rnn, w_o_g, norm_ffn)
    (gu, act), (w_ffn_out_g,) = _ffn_up(h2, w_ffn_lo_g, w_ffn_hi_g, **gather_of("w_ffn_out"))
    dx3, dx3b, loss_part, dvec_fin = _ffn_down_loss(act, x2, target, w_ffn_out_g, norm_final.reshape(1, D))

    dgu = _ffn_bwd_down(dx3b, gu, w_ffn_out_g)
    dx2, dx2b, dmixo, dvec_ffn = _ffn_bwd_up(dgu, x2, dx3, w_ffn_lo_g, w_ffn_hi_g, norm_ffn, w_o_g)
    names_a = ("w_ffn_in", "w_ffn_out", "w_o")
    part_a = [_wgrad(dgu, h2, "wgrad_ffn_in", 1408, 512), _wgrad(act, dx3b, "wgrad_ffn_out", 1408, 512),
              _wgrad(mix, dx2b, "wgrad_o", 1024, 256)]
    lz1_a, sums_a = pair_sums(names_a, part_a, "ffn")
    (dproj, dypb, dyrb, dmat, dvec_mix), lz2_a = _mixer_bwd(
        proj, dmixo, y_pool, y_rnn, hr, kept, gates, *mixer_weights,
        exchange=_Scatter([by_name[n] for n in names_a]), exchange_operands=sums_a)
    names_b = ("w_pool_out", "w_rnn_out")
    part_b = [_wgrad(pm, dypb, "wgrad_pool_out", 512, 256), _wgrad(z, dyrb, "wgrad_rnn_out", 1024, 256)]
    lz1_b, sums_b = pair_sums(names_b, part_b, "mix")
    dvec = jnp.concatenate([dvec_mix[0:9], dvec_fin[0:1], dvec_ffn[0:1], jnp.pad(loss_part, ((0, 0), (0, DR - 1))),
                            jnp.zeros((VEC_ROWS - 12, DR), F32)], axis=0)
    g_in, exchanged = _wgrad(
        dproj, h1, "wgrad_in", 1152, 1024,
        exchange=_Both(_Scatter([by_name[n] for n in names_b]), _AllReduce([((MAT_ROWS, HD), 0), ((VEC_ROWS, DR), 1)])),
        exchange_operands=sums_b + [dmat, dvec])
    lz2_b, (mat, vec) = exchanged[:2], exchanged[2:]
    loss = vec[VEC_LOSS, 0]
    lz1_c, sums_c = pair_sums(("w_in",), [g_in], "in")
    (grad_x, dvec_in), lz2_c = _in_bwd(dproj, xs, dx2, norm_mix, w_in_g,
                                       exchange=_Scatter([by_name["w_in"]]), exchange_operands=sums_c)
    (vec_in,) = _all_reduce([dvec_in], [((8, D), 1)], "all_reduce_norm_mix")

    grads, delta, new_m, new_v = {}, {}, {}, {}
    for n, g, l1, l2 in zip(names_a + names_b + ("w_in",), part_a + part_b + [g_in], lz1_a + lz1_b + lz1_c,
                            lz2_a + lz2_b + lz2_c):
        T = by_name[n]
        out = _final_sum(T, g, l1, l2, where, held(T, w[n][0]), held(T, m[n][0]), held(T, v[n][0]))
        grads[n], delta[n], new_m[n], new_v[n] = (held(T, a) for a in out)
    me = 4 * xi + 2 * yi + ci
    small_grads = dict(
        w_pool_grp=mat[0:MAT_WA], w_rg_a=mat[MAT_WA:MAT_WX], w_rg_x=mat[MAT_WX:MAT_ROWS],
        pool_scale=vec[VEC_SCALE:VEC_SCALE + 1, 0:DP], conv_b=vec[VEC_CONV_B:VEC_CONV_B + 1],
        b_rg_a=vec[VEC_BA:VEC_BA + 1], b_rg_x=vec[VEC_BX:VEC_BX + 1], lru_lambda=vec[VEC_LAM:VEC_LAM + 1],
        conv_w=lax.dynamic_slice(vec, (VEC_CONV_W, VEC_PIECE * me), (4, VEC_PIECE)),
        norm_final=vec[VEC_NORM_FINAL:VEC_NORM_FINAL + 1], norm_ffn=vec[VEC_NORM_FFN:VEC_NORM_FFN + 1],
        norm_mix=vec_in[0:1])
    names = list(small_grads)
    as2d = lambda a, g: a.reshape(g.shape)
    upd = _adam_small([small_grads[n] for n in names],
                      [(as2d(w[n], small_grads[n]), as2d(m[n], small_grads[n]), as2d(v[n], small_grads[n])) for n in names])
    for i, n in enumerate(names):
        grads[n] = small_grads[n]
        delta[n], new_m[n], new_v[n] = upd[3 * i:3 * i + 3]

    shaped = lambda d: [d[n].reshape(w[n].shape) for n in WEIGHT_NAMES]
    return (loss, grad_x[None], *shaped(grads), *shaped(delta), *shaped(new_m), *shaped(new_v))
```

```python
import math

import jax
import jax.numpy as jnp
from jax import lax
from jax.experimental import pallas as pl
from jax.experimental.pallas import tpu as pltpu

F32 = jnp.float32
BF16 = jnp.bfloat16

D = 1024
DP = 512
PG = 128
WINDOWS = (2, 4, 8, 16)
DR = 1024
NH = 8
HD = 128
DIN = 4608
DFF = 2816
EPS = 1e-6
LRU_C = 8.0
POOL_HALO = 16
CONV_HALO = 8
KEPT = 3

ADAM_LR = 0.001
ADAM_B1 = 0.9
ADAM_B2 = 0.999
ADAM_EPS = 1e-08
ADAM_WD = 0.01
ADAM_STEP = 10

VMEM_LIMIT = 56 * 1024 * 1024
MESH_AXES = ("x", "y", "c")
MESH = pl.DeviceIdType.MESH


def _dot(a, b):
    return jnp.dot(a, b, preferred_element_type=F32)


def _dot_nt(a, b):
    return lax.dot_general(a, b, (((1,), (1,)), ((), ())), preferred_element_type=F32)


def _dot_tn(a, b):
    return lax.dot_general(a, b, (((0,), (0,)), ((), ())), preferred_element_type=F32)


def _params(*sem):
    return pltpu.CompilerParams(dimension_semantics=sem, vmem_limit_bytes=VMEM_LIMIT)


def _resident(shape):
    nd = len(shape)
    return pl.BlockSpec(shape, lambda i: (0,) * nd, pipeline_mode=pl.Buffered(1))


def _rows(shape_cols, tm):
    return pl.BlockSpec((tm, shape_cols), lambda i: (i, 0))


def _call(body, name, grid, in_specs, out_specs, out_shape, operands, scratch_shapes=(), exchange=None, exchange_operands=()):
    n_in, n_out, n_scr = len(in_specs), len(out_specs), len(scratch_shapes)
    steps = math.prod(grid)
    if exchange is None:
        outs = pl.pallas_call(body, name=name, grid=grid, in_specs=in_specs, out_specs=out_specs, out_shape=out_shape,
                              scratch_shapes=list(scratch_shapes), compiler_params=_params(*["arbitrary"] * len(grid)))(*operands)
        return outs, []
    e_in, e_out = len(exchange.in_specs), len(exchange.out_specs)

    def hosted(*refs):
        ins, refs = refs[:n_in], refs[n_in:]
        e_ins, refs = refs[:e_in], refs[e_in:]
        outs, refs = refs[:n_out], refs[n_out:]
        e_outs, refs = refs[:e_out], refs[e_out:]
        scr, e_scr = refs[:n_scr], refs[n_scr:]
        step = pl.program_id(0)
        for axis in range(1, len(grid)):
            step = step * grid[axis] + pl.program_id(axis)
        @pl.when(step == 0)
        def _():
            _enter(exchange)
            exchange.start(e_ins, e_outs, e_scr)

        for at, middle in exchange.middles(steps):
            pl.when(step == at)(lambda middle=middle: middle(e_ins, e_outs, e_scr))
        body(*ins, *outs, *scr)
        pl.when(step == steps - 1)(lambda: exchange.finish(e_ins, e_outs, e_scr))

    outs = pl.pallas_call(
        hosted, name=name, grid=grid, in_specs=list(in_specs) + exchange.in_specs,
        out_specs=list(out_specs) + exchange.out_specs, out_shape=list(out_shape) + exchange.out_shape,
        scratch_shapes=list(scratch_shapes) + exchange.scratch_shapes,
        compiler_params=pltpu.CompilerParams(dimension_semantics=("arbitrary",) * len(grid), vmem_limit_bytes=VMEM_LIMIT,
                                             collective_id=exchange.collective_id))(*operands, *exchange_operands)
    return outs[:n_out], outs[n_out:]


def _enter(exchange):
    peers = exchange.peers()
    if peers:
        barrier = pltpu.get_barrier_semaphore()
        for peer in peers:
            pl.semaphore_signal(barrier, inc=1, device_id=peer, device_id_type=MESH)
        pl.semaphore_wait(barrier, len(peers))


GELU_C = math.sqrt(2.0 / math.pi)
GELU_K = 0.044715 * GELU_C


def _gelu(x, with_grad=False):
    x2 = x * x
    t = jnp.tanh(x * (GELU_C + GELU_K * x2))
    hx = 0.5 * x
    y = hx + hx * t
    if not with_grad:
        return y
    return y, 0.5 + 0.5 * t + hx * (1.0 - t * t) * (GELU_C + (3.0 * GELU_K) * x2)


def _softplus_neg(lam):
    z = jnp.exp(-jnp.abs(lam))
    u = 1.0 + z
    dlt = u - 1.0
    log1p = jnp.where(dlt == 0.0, z, jnp.log(u) * (z / jnp.where(dlt == 0.0, 1.0, dlt)))
    return jnp.maximum(-lam, 0.0) + log1p


def _sigmoid(x):
    return 0.5 * jnp.tanh(0.5 * x) + 0.5


def _linear_scan(out_ref, A, B, h0, reverse):
    n = A.shape[0]
    sub = lax.broadcasted_iota(jnp.int32, (8, 1), 0)
    tiles = range(n // 8 - 1, -1, -1) if reverse else range(n // 8)
    carry = h0
    for j in tiles:
        a, b = A[8 * j:8 * j + 8, :], B[8 * j:8 * j + 8, :]
        for d in (1, 2, 4):
            keep = (sub < 8 - d) if reverse else (sub >= d)
            shift = 8 - d if reverse else d
            b = jnp.where(keep, a * pltpu.roll(b, shift, axis=0) + b, b)
            a = jnp.where(keep, a * pltpu.roll(a, shift, axis=0), a)
        h = a * carry + b
        out_ref[8 * j:8 * j + 8, :] = h
        carry = h[0:1, :] if reverse else h[7:8, :]
    return carry


def _pool_windows(ext, shift_sign):
    n = ext.shape[0]
    s = ext
    outs = []
    for w in WINDOWS:
        d = w // 2
        s = s + pltpu.roll(s, d if shift_sign > 0 else n - d, axis=0)
        outs.append(s[:, :PG])
        s = s[:, PG:]
    return outs


def _conv_taps(uext):
    taps = []
    for k in range(4):
        sh = 3 - k
        v = uext if sh == 0 else pltpu.roll(uext, sh, axis=0)
        taps.append(v[CONV_HALO:, :])
    return taps


def _gates(v, wa_ref, ba_ref, wx_ref, bx_ref, sp):
    vb = v.astype(BF16)
    ra, rx = [], []
    for h in range(NH):
        vh = vb[:, h * HD:(h + 1) * HD]
        ra.append(_dot(vh, wa_ref[h]))
        rx.append(_dot(vh, wx_ref[h]))
    r = _sigmoid(jnp.concatenate(ra, axis=1) + ba_ref[...])
    i = _sigmoid(jnp.concatenate(rx, axis=1) + bx_ref[...])
    log_a = r * ((-LRU_C) * sp)
    a = jnp.exp(log_a)
    one_minus = -jnp.tanh(log_a) * (1.0 + a * a)
    return r, i, a, jnp.sqrt(one_minus), lax.rsqrt(one_minus)


def _mixer_fwd(proj, wg, scale, w_pool_out, conv_w, conv_b, wa, ba, wx, bx, lam, w_rnn_out, exchange=None,
               exchange_operands=(), tm=256):
    S = proj.shape[0]
    UW = DP + 2 * DR

    def body(proj_ref, wg_ref, scale_ref, wpo_ref, cw_ref, cb_ref, wa_ref, ba_ref, wx_ref, bx_ref, lam_ref, wro_ref,
             pm_ref, ypool_ref, hr_ref, z_ref, yrnn_ref, kept_ref, gates_ref, pool_carry, conv_carry, h_carry):
        i = pl.program_id(0)

        @pl.when(i == 0)
        def _():
            pool_carry[...] = jnp.zeros_like(pool_carry)
            conv_carry[...] = jnp.zeros_like(conv_carry)
            h_carry[...] = jnp.zeros_like(h_carry)

        rows = lax.broadcasted_iota(jnp.int32, (tm, 1), 0)
        t_glob = i * tm + rows

        u_pool = proj_ref[:, 0:DP]
        ext = jnp.concatenate([pool_carry[...], u_pool], axis=0)
        pool_carry[...] = u_pool[tm - POOL_HALO:, :]
        sums = _pool_windows(ext, +1)
        mixed = []
        for g, w in enumerate(WINDOWS):
            inv_cnt = 1.0 / jnp.minimum(t_glob + 1, w).astype(F32)
            pooled_g = sums[g][POOL_HALO:, :] * inv_cnt - u_pool[:, g * PG:(g + 1) * PG]
            mixed.append(_dot(pooled_g.astype(BF16), wg_ref[g]))
        pm = (jnp.concatenate(mixed, axis=1) * scale_ref[...]).astype(BF16)
        pm_ref[...] = pm
        ypool_ref[...] = _dot(pm, wpo_ref[...]).astype(BF16)

        u_rnn = proj_ref[:, DP:DP + DR]
        uext = jnp.concatenate([conv_carry[...], u_rnn], axis=0)
        conv_carry[...] = u_rnn[tm - CONV_HALO:, :]
        taps = _conv_taps(uext)
        v = cb_ref[...]
        for k in range(4):
            v = v + taps[k] * cw_ref[k:k + 1, :]
        sp = _softplus_neg(lam_ref[...])
        r, gi, a, mult, _ = _gates(v, wa_ref, ba_ref, wx_ref, bx_ref, sp)
        for k, kept in enumerate((v, a, mult)):
            kept_ref[k] = kept
        for k, kept in enumerate((r, gi)):
            gates_ref[k] = kept.astype(BF16)
        h_carry[0:1, :] = _linear_scan(hr_ref, a, mult * gi * v, h_carry[0:1, :], reverse=False)
        z = (hr_ref[...] * _gelu(proj_ref[:, DP + DR:UW])).astype(BF16)
        z_ref[...] = z
        yrnn_ref[...] = _dot(z, wro_ref[...]).astype(BF16)

    return _call(
        body, "mixer_fwd", (S // tm,),
        in_specs=[_rows(UW, tm), _resident((4, PG, PG)), _resident((1, DP)), _resident((DP, D)), _resident((4, DR)),
                  _resident((1, DR)), _resident((NH, HD, HD)), _resident((1, DR)), _resident((NH, HD, HD)),
                  _resident((1, DR)), _resident((1, DR)), _resident((DR, D))],
        out_specs=[_rows(DP, tm), _rows(D, tm), _rows(DR, tm), _rows(DR, tm), _rows(D, tm),
                   pl.BlockSpec((KEPT, tm, DR), lambda i: (0, i, 0)), pl.BlockSpec((2, tm, DR), lambda i: (0, i, 0))],
        out_shape=[jax.ShapeDtypeStruct((S, DP), BF16),
                   jax.ShapeDtypeStruct((S, D), BF16), jax.ShapeDtypeStruct((S, DR), F32),
                   jax.ShapeDtypeStruct((S, DR), BF16), jax.ShapeDtypeStruct((S, D), BF16),
                   jax.ShapeDtypeStruct((KEPT, S, DR), F32), jax.ShapeDtypeStruct((2, S, DR), BF16)],
        scratch_shapes=[pltpu.VMEM((POOL_HALO, DP), F32), pltpu.VMEM((CONV_HALO, DR), F32), pltpu.VMEM((8, DR), F32)],
        operands=(proj, wg, scale, w_pool_out, conv_w, conv_b, wa, ba, wx, bx, lam, w_rnn_out),
        exchange=exchange, exchange_operands=exchange_operands)


FF_CHUNKS = ((0, 768), (768, 1536), (1536, 2304), (2304, DFF))


def _rms(x):
    r = lax.rsqrt(jnp.mean(x * x, axis=-1, keepdims=True) + EPS)
    return r, x * r


def _rms_bwd(dh, g, r, xh):
    dxh = dh * g
    return r * (dxh - xh * jnp.mean(dxh * xh, axis=-1, keepdims=True))


def _merge_out(x, proj, y_pool, y_rnn, w_o, norm_ffn, exchange=None, exchange_operands=(), tm=512):
    S = x.shape[0]
    GL0 = (DP + 2 * DR) // 512

    def gl_spec(k):
        return pl.BlockSpec((tm, 512), lambda i: (i, GL0 + k))

    def body(x_ref, gl0, gl1, gl2, gl3, yp_ref, yr_ref, wo_ref, gf_ref, mix_ref, x2_ref, h2_ref):
        s_p = _sigmoid(jnp.concatenate([gl0[...], gl1[...]], axis=1))
        s_r = _sigmoid(jnp.concatenate([gl2[...], gl3[...]], axis=1))
        mix = (s_p * yp_ref[...].astype(F32) + s_r * yr_ref[...].astype(F32)).astype(BF16)
        mix_ref[...] = mix
        x2 = x_ref[...] + _dot(mix, wo_ref[...])
        x2_ref[...] = x2
        _, xh2 = _rms(x2)
        h2_ref[...] = (xh2 * gf_ref[...]).astype(BF16)

    return _call(
        body, "merge_out", (S // tm,),
        in_specs=[_rows(D, tm), gl_spec(0), gl_spec(1), gl_spec(2), gl_spec(3), _rows(D, tm), _rows(D, tm),
                  _resident((D, D)), _resident((1, D))],
        out_specs=[_rows(D, tm), _rows(D, tm), _rows(D, tm)],
        out_shape=[jax.ShapeDtypeStruct((S, D), BF16), jax.ShapeDtypeStruct((S, D), F32), jax.ShapeDtypeStruct((S, D), BF16)],
        operands=(x, proj, proj, proj, proj, y_pool, y_rnn, w_o, norm_ffn),
        exchange=exchange, exchange_operands=exchange_operands)


def _ffn_up(h2, w_lo, w_hi, exchange=None, exchange_operands=(), tm=512):
    S = h2.shape[0]
    HALF = D // 2

    def body(h_ref, lo_ref, hi_ref, back_ref, act_ref):
        h_lo, h_hi = h_ref[:, 0:HALF], h_ref[:, HALF:D]
        for c0, c1 in FF_CHUNKS:
            gate = _dot_nt(h_lo, lo_ref[c0:c1, :]) + _dot_nt(h_hi, hi_ref[c0:c1, :])
            up = _dot_nt(h_lo, lo_ref[DFF + c0:DFF + c1, :]) + _dot_nt(h_hi, hi_ref[DFF + c0:DFF + c1, :])
            sg = _sigmoid(gate)
            silu = gate * sg
            back_ref[:, c0:c1] = (up * (sg * (1.0 + gate * (1.0 - sg)))).astype(BF16)
            back_ref[:, DFF + c0:DFF + c1] = silu.astype(BF16)
            act_ref[:, c0:c1] = (silu * up).astype(BF16)

    return _call(
        body, "ffn_up", (S // tm,),
        in_specs=[_rows(D, tm), _resident((2 * DFF, HALF)), _resident((2 * DFF, HALF))],
        out_specs=[_rows(2 * DFF, tm), _rows(DFF, tm)],
        out_shape=[jax.ShapeDtypeStruct((S, 2 * DFF), BF16), jax.ShapeDtypeStruct((S, DFF), BF16)],
        operands=(h2, w_lo, w_hi), exchange=exchange, exchange_operands=exchange_operands)


def _ffn_down_loss(act, x2, target, w_ffn_out, norm_final, tm=512):
    S = act.shape[0]

    def body(act_ref, x2_ref, t_ref, w_ref, gn_ref, dx3_ref, dx3b_ref, loss_ref, dvec_ref):
        i = pl.program_id(0)

        @pl.when(i == 0)
        def _():
            loss_ref[...] = jnp.zeros_like(loss_ref)
            dvec_ref[...] = jnp.zeros_like(dvec_ref)

        x3 = x2_ref[...] + _dot(act_ref[...], w_ref[...])
        r3, xh3 = _rms(x3)
        g_fin = gn_ref[...]
        e = xh3 * g_fin - t_ref[...]
        loss_ref[...] += jnp.sum(e * e, axis=(0, 1), keepdims=True) * (0.5 / D)
        dy = e * (1.0 / D)
        dvec_ref[0:1, :] += jnp.sum(dy * xh3, axis=0, keepdims=True)
        dx3 = _rms_bwd(dy, g_fin, r3, xh3)
        dx3_ref[...] = dx3
        dx3b_ref[...] = dx3.astype(BF16)

    return pl.pallas_call(
        body, name="ffn_down_loss", grid=(S // tm,),
        in_specs=[_rows(DFF, tm), _rows(D, tm), _rows(D, tm), _resident((DFF, D)), _resident((1, D))],
        out_specs=[_rows(D, tm), _rows(D, tm), _resident((1, 1)), _resident((8, D))],
        out_shape=[jax.ShapeDtypeStruct((S, D), F32), jax.ShapeDtypeStruct((S, D), BF16),
                   jax.ShapeDtypeStruct((1, 1), F32), jax.ShapeDtypeStruct((8, D), F32)],
        compiler_params=_params("arbitrary"),
    )(act, x2, target, w_ffn_out, norm_final)


def _ffn_bwd_down(dx3b, gu, w_ffn_out, tm=512):
    S = dx3b.shape[0]

    def body(d_ref, back_ref, w_ref, dgu_ref):
        d = d_ref[...]
        for c0, c1 in FF_CHUNKS:
            dact = _dot_nt(d, w_ref[c0:c1, :])
            dgu_ref[:, c0:c1] = (dact * back_ref[:, c0:c1].astype(F32)).astype(BF16)
            dgu_ref[:, DFF + c0:DFF + c1] = (dact * back_ref[:, DFF + c0:DFF + c1].astype(F32)).astype(BF16)

    return pl.pallas_call(
        body, name="ffn_bwd_down", grid=(S // tm,),
        in_specs=[_rows(D, tm), _rows(2 * DFF, tm), _resident((DFF, D))],
        out_specs=_rows(2 * DFF, tm),
        out_shape=jax.ShapeDtypeStruct((S, 2 * DFF), BF16),
        compiler_params=_params("parallel"),
    )(dx3b, gu, w_ffn_out)


def _ffn_bwd_up(dgu, x2, dx3, w_lo, w_hi, norm_ffn, w_o, tm=512):
    S = dgu.shape[0]
    HALF = D // 2

    def body(dgu_ref, x2_ref, dx3_ref, lo_ref, hi_ref, gf_ref, wo_ref, dx2_ref, dx2b_ref, dmixo_ref, dvec_ref):
        i = pl.program_id(0)

        @pl.when(i == 0)
        def _():
            dvec_ref[...] = jnp.zeros_like(dvec_ref)

        dgate, dup = dgu_ref[:, 0:DFF], dgu_ref[:, DFF:2 * DFF]
        dh2 = jnp.concatenate([_dot(dgate, w[0:DFF, :]) + _dot(dup, w[DFF:2 * DFF, :]) for w in (lo_ref, hi_ref)], axis=1)
        r2, xh2 = _rms(x2_ref[...])
        dvec_ref[0:1, :] += jnp.sum(dh2 * xh2, axis=0, keepdims=True)
        dx2 = dx3_ref[...] + _rms_bwd(dh2, gf_ref[...], r2, xh2)
        dx2_ref[...] = dx2
        dx2b = dx2.astype(BF16)
        dx2b_ref[...] = dx2b
        dmixo_ref[...] = _dot_nt(dx2b, wo_ref[...]).astype(BF16)

    return pl.pallas_call(
        body, name="ffn_bwd_up", grid=(S // tm,),
        in_specs=[_rows(2 * DFF, tm), _rows(D, tm), _rows(D, tm), _resident((2 * DFF, HALF)), _resident((2 * DFF, HALF)),
                  _resident((1, D)), _resident((D, D))],
        out_specs=[_rows(D, tm), _rows(D, tm), _rows(D, tm), _resident((8, D))],
        out_shape=[jax.ShapeDtypeStruct((S, D), F32), jax.ShapeDtypeStruct((S, D), BF16), jax.ShapeDtypeStruct((S, D), BF16),
                   jax.ShapeDtypeStruct((8, D), F32)],
        compiler_params=_params("arbitrary"),
    )(dgu, x2, dx3, w_lo, w_hi, norm_ffn, w_o)


VEC_ROWS = 16
MAT_WA = 4 * PG
MAT_WX = MAT_WA + NH * HD
MAT_ROWS = MAT_WX + NH * HD


def _mixer_bwd(proj, dmixo, y_pool, y_rnn, hr, kept, gates, wg, scale, w_pool_out, conv_w, conv_b, wa, ba, wx, bx, lam, w_rnn_out,
               exchange=None, exchange_operands=(), tm=256):
    S = proj.shape[0]
    nt = S // tm

    def rev(cols):
        return pl.BlockSpec((tm, cols), lambda i: (nt - 1 - i, 0))

    def halo(rows_, cols):
        per = tm // rows_
        return pl.BlockSpec((rows_, cols), lambda i: (jnp.maximum((nt - 1 - i) * per - 1, 0), 0))

    def body(proj_ref, projh_ref, dmixo_ref, yp_ref, yr_ref, hr_ref, hrh_ref, kept_ref, gates_ref, wg_ref, scale_ref, wpo_ref, cw_ref, cb_ref,
             wa_ref, ba_ref, wx_ref, bx_ref, lam_ref, wro_ref,
             dproj_ref, dypb_ref, dyrb_ref, dmat_ref, dvec_ref,
             q_carry, dv_carry, a_carry, g_carry, g_scr):
        i = pl.program_id(0)
        ti = nt - 1 - i

        @pl.when(i == 0)
        def _():
            q_carry[...] = jnp.zeros_like(q_carry)
            dv_carry[...] = jnp.zeros_like(dv_carry)
            a_carry[...] = jnp.zeros_like(a_carry)
            g_carry[...] = jnp.zeros_like(g_carry)
            dmat_ref[...] = jnp.zeros_like(dmat_ref)
            dvec_ref[...] = jnp.zeros_like(dvec_ref)

        rows = lax.broadcasted_iota(jnp.int32, (tm, 1), 0)
        t_glob = ti * tm + rows
        has_prev = (ti > 0).astype(F32)
        dmixo = dmixo_ref[...].astype(F32)

        s_p = _sigmoid(proj_ref[:, DP + 2 * DR:DP + 2 * DR + D])
        s_r = _sigmoid(proj_ref[:, DP + 2 * DR + D:DIN])
        dproj_ref[:, DP + 2 * DR:DP + 2 * DR + D] = (dmixo * yp_ref[...].astype(F32) * s_p * (1.0 - s_p)).astype(BF16)
        dproj_ref[:, DP + 2 * DR + D:DIN] = (dmixo * yr_ref[...].astype(F32) * s_r * (1.0 - s_r)).astype(BF16)
        dyp = (dmixo * s_p).astype(BF16)
        dyr = (dmixo * s_r).astype(BF16)
        dypb_ref[...] = dyp
        dyrb_ref[...] = dyr

        dz = _dot_nt(dyr, wro_ref[...])
        u_gate = proj_ref[:, DP + DR:DP + 2 * DR]
        gg, dgelu = _gelu(u_gate, with_grad=True)
        hr_t = hr_ref[...]
        dproj_ref[:, DP + DR:DP + 2 * DR] = (dz * hr_t * dgelu).astype(BF16)
        dhr = dz * gg

        sp = _softplus_neg(lam_ref[...])
        v, a, mult = (kept_ref[k] for k in range(KEPT))
        r, gi = (gates_ref[k].astype(F32) for k in range(2))
        inv_mult = 1.0 / mult

        C = jnp.where(rows == tm - 1, a_carry[0:1, :], pltpu.roll(a, tm - 1, axis=0))
        g_carry[0:1, :] = _linear_scan(g_scr, C, dhr, g_carry[0:1, :], reverse=True)
        a_carry[0:1, :] = a[0:1, :]
        g = g_scr[...]

        h_prev = jnp.where(rows == 0, hrh_ref[7:8, :] * has_prev, pltpu.roll(hr_t, 1, axis=0))
        da = g * h_prev
        gm = g * mult
        dmult = g * gi * v
        di = gm * v
        dv = gm * gi
        dlog_a = da * a - dmult * (a * a * inv_mult)
        dvec_ref[4:5, :] += jnp.sum(dlog_a * r, axis=0, keepdims=True)
        dra = (dlog_a * ((-LRU_C) * sp) * r * (1.0 - r))
        drx = di * gi * (1.0 - gi)
        dvec_ref[2:3, :] += jnp.sum(dra, axis=0, keepdims=True)
        dvec_ref[3:4, :] += jnp.sum(drx, axis=0, keepdims=True)
        drab = dra.astype(BF16)
        drxb = drx.astype(BF16)
        vb = v.astype(BF16)
        dvg = []
        for h in range(NH):
            sl = slice(h * HD, (h + 1) * HD)
            dvg.append(_dot_nt(drab[:, sl], wa_ref[h]) + _dot_nt(drxb[:, sl], wx_ref[h]))
            dmat_ref[MAT_WA + h * HD:MAT_WA + (h + 1) * HD, :] += _dot_tn(vb[:, sl], drab[:, sl])
            dmat_ref[MAT_WX + h * HD:MAT_WX + (h + 1) * HD, :] += _dot_tn(vb[:, sl], drxb[:, sl])
        dv = dv + jnp.concatenate(dvg, axis=1)
        dvec_ref[1:2, :] += jnp.sum(dv, axis=0, keepdims=True)
        dvext = jnp.concatenate([dv, dv_carry[...]], axis=0)
        dv_carry[...] = dv[0:CONV_HALO, :]
        n = tm + CONV_HALO
        u_rnn = proj_ref[:, DP:DP + DR]
        du_rnn = dv * cw_ref[3:4, :]
        dvec_ref[8:9, :] += jnp.sum(dv * u_rnn, axis=0, keepdims=True)
        for k in range(3):
            dv_k = pltpu.roll(dvext, n - (3 - k), axis=0)[0:tm, :]
            du_rnn = du_rnn + dv_k * cw_ref[k:k + 1, :]
            dvec_ref[5 + k:6 + k, :] += jnp.sum(dv_k * u_rnn, axis=0, keepdims=True)
        dproj_ref[:, DP:DP + DR] = du_rnn.astype(BF16)

        dpm = _dot_nt(dyp, wpo_ref[...])
        u_pool = proj_ref[:, 0:DP]
        ext = jnp.concatenate([projh_ref[:, 0:DP] * has_prev, u_pool], axis=0)
        sums = _pool_windows(ext, +1)
        scale_v = scale_ref[...]
        qs = []
        dpooled = []
        dscale = []
        for gi_, w in enumerate(WINDOWS):
            sl = slice(gi_ * PG, (gi_ + 1) * PG)
            inv_cnt = 1.0 / jnp.minimum(t_glob + 1, w).astype(F32)
            pooled_b = (sums[gi_][POOL_HALO:, :] * inv_cnt - u_pool[:, sl]).astype(BF16)
            mixed_g = _dot(pooled_b, wg_ref[gi_])
            dscale.append(jnp.sum(dpm[:, sl] * mixed_g, axis=0, keepdims=True))
            dmixed_b = (dpm[:, sl] * scale_v[:, sl]).astype(BF16)
            dmat_ref[gi_ * PG:(gi_ + 1) * PG, :] += _dot_tn(pooled_b, dmixed_b)
            dp_g = _dot_nt(dmixed_b, wg_ref[gi_])
            dpooled.append(dp_g)
            qs.append(dp_g * inv_cnt)
        dvec_ref[0:1, 0:DP] += jnp.concatenate(dscale, axis=1)
        q = jnp.concatenate(qs, axis=1)
        qext = jnp.concatenate([q, q_carry[...]], axis=0)
        q_carry[...] = q[0:POOL_HALO, :]
        tsum = _pool_windows(qext, -1)
        for gi_ in range(4):
            dproj_ref[:, gi_ * PG:(gi_ + 1) * PG] = (tsum[gi_][0:tm, :] - dpooled[gi_]).astype(BF16)

        @pl.when(i == nt - 1)
        def _():
            dvec_ref[4:5, :] = dvec_ref[4:5, :] * (LRU_C * _sigmoid(-lam_ref[...]))

    return _call(
        body, "mixer_bwd", (nt,),
        in_specs=[rev(DIN), halo(POOL_HALO, DIN), rev(D), rev(D), rev(D), rev(DR), halo(8, DR),
                  pl.BlockSpec((KEPT, tm, DR), lambda i: (0, nt - 1 - i, 0)),
                  pl.BlockSpec((2, tm, DR), lambda i: (0, nt - 1 - i, 0)), _resident((4, PG, PG)), _resident((1, DP)), _resident((DP, D)), _resident((4, DR)), _resident((1, DR)),
                  _resident((NH, HD, HD)), _resident((1, DR)), _resident((NH, HD, HD)), _resident((1, DR)),
                  _resident((1, DR)), _resident((DR, D))],
        out_specs=[rev(DIN), rev(D), rev(D), _resident((MAT_ROWS, HD)), _resident((VEC_ROWS, DR))],
        out_shape=[jax.ShapeDtypeStruct((S, DIN), BF16), jax.ShapeDtypeStruct((S, D), BF16),
                   jax.ShapeDtypeStruct((S, D), BF16), jax.ShapeDtypeStruct((MAT_ROWS, HD), F32),
                   jax.ShapeDtypeStruct((VEC_ROWS, DR), F32)],
        scratch_shapes=[pltpu.VMEM((POOL_HALO, DP), F32), pltpu.VMEM((CONV_HALO, DR), F32), pltpu.VMEM((8, DR), F32),
                        pltpu.VMEM((8, DR), F32), pltpu.VMEM((tm, DR), F32)],
        operands=(proj, proj, dmixo, y_pool, y_rnn, hr, hr, kept, gates, wg, scale, w_pool_out, conv_w, conv_b, wa, ba, wx, bx, lam,
                  w_rnn_out),
        exchange=exchange, exchange_operands=exchange_operands)


def _in_bwd(dproj, x, dx2, norm_mix, w_in, exchange=None, exchange_operands=(), tm=512):
    S = x.shape[0]

    def body(dp_ref, x_ref, dx2_ref, g_ref, w_ref, dx_ref, dg_ref):
        i = pl.program_id(0)

        @pl.when(i == 0)
        def _():
            dg_ref[...] = jnp.zeros_like(dg_ref)

        dh = _dot(dp_ref[:, 0:1536], w_ref[0:1536, :])
        dh = dh + _dot(dp_ref[:, 1536:3072], w_ref[1536:3072, :])
        dh = dh + _dot(dp_ref[:, 3072:DIN], w_ref[3072:DIN, :])
        xv = x_ref[...]
        r = lax.rsqrt(jnp.mean(xv * xv, axis=-1, keepdims=True) + EPS)
        xh = xv * r
        dg_ref[0:1, :] += jnp.sum(dh * xh, axis=0, keepdims=True)
        dxh = dh * g_ref[...]
        dx_ref[...] = dx2_ref[...] + r * (dxh - xh * jnp.mean(dxh * xh, axis=-1, keepdims=True))

    return _call(
        body, "in_bwd", (S // tm,),
        in_specs=[_rows(DIN, tm), _rows(D, tm), _rows(D, tm), _resident((1, D)), _resident((DIN, D))],
        out_specs=[_rows(D, tm), _resident((8, D))],
        out_shape=[jax.ShapeDtypeStruct((S, D), F32), jax.ShapeDtypeStruct((8, D), F32)],
        operands=(dproj, x, dx2, norm_mix, w_in), exchange=exchange, exchange_operands=exchange_operands)


def _wgrad(a, b, name, tk, tn, exchange=None, exchange_operands=()):
    S, K = a.shape
    N = b.shape[1]

    def body(a_ref, b_ref, o_ref):
        o_ref[...] = _dot_tn(a_ref[...], b_ref[...]).astype(BF16)

    (out,), exchanged = _call(
        body, name, (K // tk, N // tn),
        in_specs=[pl.BlockSpec((S, tk), lambda k, n: (0, k)), pl.BlockSpec((S, tn), lambda k, n: (0, n))],
        out_specs=[pl.BlockSpec((tk, tn), lambda k, n: (k, n))],
        out_shape=[jax.ShapeDtypeStruct((K, N), BF16)],
        operands=(a, b), exchange=exchange, exchange_operands=exchange_operands)
    return (out, exchanged) if exchange is not None else out


VEC_SCALE, VEC_CONV_B, VEC_BA, VEC_BX, VEC_LAM, VEC_CONV_W, VEC_NORM_FINAL, VEC_NORM_FFN = 0, 1, 2, 3, 4, 5, 9, 10
VEC_LOSS = 11


class _Big:
    def __init__(self, name, rows, cols, axis, n, dtype=BF16, transposed=False, src_cols=None):
        self.name, self.rows, self.cols, self.axis, self.n, self.dtype = name, rows, cols, axis, n, dtype
        self.transposed = transposed
        self.src_cols = src_cols
        self.block_shape = (rows, n) if axis == 1 else (n, cols)

    def block(self, ref, p):
        if self.axis == 1:
            return ref.at[:, pl.ds(pl.multiple_of(p * self.n, 128), self.n)]
        return ref.at[pl.ds(pl.multiple_of(p * self.n, 16 if self.dtype == BF16 else 8), self.n), :]


BIG = (_Big("w_in", DIN, D, 0, DIN // 8, transposed=True), _Big("w_pool_out", DP, D, 1, D // 8),
       _Big("w_rnn_out", DR, D, 0, DR // 8), _Big("w_o", D, D, 0, D // 8),
       _Big("w_ffn_in", 2 * DFF, D, 0, 2 * DFF // 8, transposed=True), _Big("w_ffn_out", DFF, D, 0, DFF // 8))
CONV_W = _Big("conv_w", 8, DR, 1, DR // 8, F32)
W_FFN_IN_HALVES = (_Big("w_ffn_in_lo", 2 * DFF, D // 2, 0, 2 * DFF // 8, src_cols=(0, D // 2)),
                   _Big("w_ffn_in_hi", 2 * DFF, D // 2, 0, 2 * DFF // 8, src_cols=(D // 2, D)))
GATHERED = BIG + (CONV_W,) + W_FFN_IN_HALVES

HBM_SPEC = pl.BlockSpec(memory_space=pl.ANY)
VMEM_SPEC = pl.BlockSpec(memory_space=pltpu.VMEM)


def _place():
    x, y, c = (lax.axis_index(a) for a in MESH_AXES)
    other_chips = [(1 - x, y), (x, 1 - y), (1 - x, 1 - y)]
    return x, y, c, other_chips


def _remote(src, dst, send_sems, recv_sems, idx, to):
    return pltpu.make_async_remote_copy(src_ref=src, dst_ref=dst, send_sem=send_sems.at[idx], recv_sem=recv_sems.at[idx],
                                        device_id=to, device_id_type=MESH)


def _device_index(chip, core):
    return 4 * chip[0] + 2 * chip[1] + core


class _Gather:
    def __init__(self, tensors):
        self.tensors = tuple(tensors)
        n = len(self.tensors)
        self.in_specs = [HBM_SPEC] * n
        self.out_specs = [HBM_SPEC] * n
        self.out_shape = [jax.ShapeDtypeStruct((T.rows, T.cols), T.dtype) for T in self.tensors]
        self.scratch_shapes = [pltpu.VMEM(T.block_shape, T.dtype) for T in self.tensors] + [
            pltpu.VMEM(T.block_shape, F32) for T in self.tensors] + [
            pltpu.SemaphoreType.DMA((n, 7)), pltpu.SemaphoreType.DMA((n, 7)), pltpu.SemaphoreType.DMA((n, 2))]

    collective_id = 1

    def peers(self):
        x, y, c, _ = _place()
        return [(x, y, 1 - c), (1 - x, y, c), (x, 1 - y, c)]

    def middles(self, steps):
        return [(steps // 2, self.relay), (steps - 1, self.middle)]

    def _copies(self, ins, outs, scratch):
        n = len(self.tensors)
        mine, raw, (send_sems, recv_sems, loc_sems) = scratch[:n], scratch[n:2 * n], scratch[2 * n:]
        x, y, c, chips = _place()
        sibling = (x, y, 1 - c)
        me = _device_index((x, y), c)
        relay_from = (jnp.where(c == 0, 1 - x, x), jnp.where(c == 0, y, 1 - y))
        relay_to = (jnp.where(c == 0, x, 1 - x), jnp.where(c == 0, 1 - y, y))
        loads, stores, first, relays, passed, arrivals, late = [], [], [], [], [], [], []
        for t, T in enumerate(self.tensors):
            place = T.block(outs[t], me)
            src = ins[t] if T.src_cols is None else ins[t].at[:, T.src_cols[0]:T.src_cols[1]]
            loads.append(pltpu.make_async_copy(src, raw[t], loc_sems.at[t, 0]))
            stores.append(pltpu.make_async_copy(mine[t], place, loc_sems.at[t, 1]))
            first.append(_remote(mine[t], place, send_sems, recv_sems, (t, 0), sibling))
            theirs = T.block(outs[t], _device_index((x, y), 1 - c))
            late.append(_remote(theirs, theirs, send_sems, recv_sems, (t, 0), sibling))
            relayed = T.block(outs[t], _device_index(relay_from, c))
            relays.append(_remote(relayed, relayed, send_sems, recv_sems, (t, 3), (*relay_to, c)))
            for k, chip in enumerate(chips):
                if k < 2:
                    first.append(_remote(mine[t], place, send_sems, recv_sems, (t, 1 + k), (*chip, c)))
                land = T.block(outs[t], _device_index(chip, c))
                arrivals.append(_remote(land, land, send_sems, recv_sems, (t, 1 + k), sibling))
                passed.append(_remote(land, land, send_sems, recv_sems, (t, 4 + k), sibling))
                theirs = T.block(outs[t], _device_index(chip, 1 - c))
                late.append(_remote(theirs, theirs, send_sems, recv_sems, (t, 4 + k), sibling))
        return loads, stores, first, relays, passed, arrivals, late

    def start(self, ins, outs, scratch):
        loads, stores, first, _, _, _, _ = self._copies(ins, outs, scratch)
        n = len(self.tensors)
        for cp in loads:
            cp.start()
        for t, cp in enumerate(loads):
            cp.wait()
            scratch[t][...] = scratch[n + t][...].astype(self.tensors[t].dtype)
        for cp in stores + first:
            cp.start()

    def relay(self, ins, outs, scratch, skip=0):
        _, _, _, relays, passed, arrivals, _ = self._copies(ins, outs, scratch)
        for t in range(skip, len(self.tensors)):
            arrivals[3 * t].wait_recv()
            arrivals[3 * t + 1].wait_recv()
            for cp in (relays[t], passed[3 * t], passed[3 * t + 1]):
                cp.start()

    def middle(self, ins, outs, scratch, skip=0):
        _, _, _, _, passed, arrivals, _ = self._copies(ins, outs, scratch)
        for t in range(skip, len(self.tensors)):
            arrivals[3 * t + 2].wait_recv()
            passed[3 * t + 2].start()

    def finish(self, ins, outs, scratch, skip=0):
        _, stores, first, relays, passed, _, late = self._copies(ins, outs, scratch)
        for cp in late[4 * skip:]:
            cp.wait_recv()
        for cp in first + relays + passed:
            cp.wait_send()
        for cp in stores[skip:]:
            cp.wait()


def _in_proj_gather(x, norm_mix, blocks, tensors, order, tm=1024):
    S = x.shape[0]
    nt = S // tm
    n = len(tensors)
    gather = _Gather(tensors)
    CB = 2 * tensors[0].n

    def body(order_ref, x_ref, g_ref, *refs):
        ins, (proj_ref, h_ref), outs = refs[:n], refs[n:n + 2], refs[n + 2:2 * n + 2]
        (h_all, w_chip, w_sem), scratch = refs[2 * n + 2:2 * n + 5], refs[2 * n + 5:]
        q, i = pl.program_id(0), pl.program_id(1)
        _, stores, _, relays, passed, arrivals, late = gather._copies(ins, outs, scratch)

        def fetch(turn):
            rows = outs[0].at[pl.ds(pl.multiple_of(order_ref[turn] * CB, 16), CB), :]
            cp = pltpu.make_async_copy(rows, w_chip, w_sem)
            cp.start()
            cp.wait()

        @pl.when((q == 0) & (i == 0))
        def _():
            _enter(gather)
            gather.start(ins, outs, scratch)
            late[0].wait_recv()
            stores[0].wait()
            fetch(0)

        @pl.when((q == 1) & (i == 0))
        def _():
            arrivals[0].wait_recv()
            arrivals[1].wait_recv()
            for cp in (relays[0], passed[0], passed[1]):
                cp.start()
            late[1].wait_recv()
            fetch(1)

        @pl.when((q == 2) & (i == 0))
        def _():
            late[2].wait_recv()
            fetch(2)
            gather.relay(ins, outs, scratch, skip=1)

        @pl.when((q == 3) & (i == 0))
        def _():
            arrivals[2].wait_recv()
            passed[2].start()
            late[3].wait_recv()
            fetch(3)

        rows = pl.ds(pl.multiple_of(i * tm, tm), tm)

        @pl.when(q == 0)
        def _():
            xv = x_ref[...]
            r = lax.rsqrt(jnp.mean(xv * xv, axis=-1, keepdims=True) + EPS)
            h = (xv * r * g_ref[...]).astype(BF16)
            h_all[rows, :] = h
            h_ref[...] = h

        proj_ref[...] = _dot_nt(h_all[rows, :], w_chip[...])

        @pl.when((q == 3) & (i == nt - 1))
        def _():
            gather.middle(ins, outs, scratch, skip=1)
            gather.finish(ins, outs, scratch, skip=1)

    row_tile = lambda q, i, order: (jnp.where(q == 0, i, nt - 1), 0)
    whole = lambda shape: pl.BlockSpec(shape, lambda q, i, order: (0,) * len(shape), pipeline_mode=pl.Buffered(1))
    outs = pl.pallas_call(
        body, name="in_proj_gather",
        grid_spec=pltpu.PrefetchScalarGridSpec(
            num_scalar_prefetch=1, grid=(4, nt),
            in_specs=[pl.BlockSpec((tm, D), row_tile), whole((1, D))] + gather.in_specs,
            out_specs=[pl.BlockSpec((tm, CB), lambda q, i, order: (i, order[q])), pl.BlockSpec((tm, D), row_tile)]
            + gather.out_specs,
            scratch_shapes=[pltpu.VMEM((S, D), BF16), pltpu.VMEM((CB, D), BF16), pltpu.SemaphoreType.DMA]
            + gather.scratch_shapes),
        out_shape=[jax.ShapeDtypeStruct((S, DIN), F32), jax.ShapeDtypeStruct((S, D), BF16)] + gather.out_shape,
        compiler_params=pltpu.CompilerParams(dimension_semantics=("arbitrary", "arbitrary"), vmem_limit_bytes=VMEM_LIMIT,
                                             collective_id=gather.collective_id),
    )(order, x, norm_mix, *blocks)
    return outs[:2], outs[2:]


PAIR_ROWS = 32


class _PairReduce:
    collective_id = 3

    def __init__(self, tensors):
        self.tensors = tuple(tensors)
        nt = len(self.tensors)
        blocks = [T.block_shape for T in self.tensors]
        self.in_specs = [HBM_SPEC] * nt
        self.out_specs = [HBM_SPEC] * (2 * nt)
        self.out_shape = ([jax.ShapeDtypeStruct(b, BF16) for b in blocks]
                          + [jax.ShapeDtypeStruct((3,) + b, BF16) for b in blocks])
        self.scratch_shapes = ([pltpu.VMEM((4,) + b, BF16) for b in blocks] + [pltpu.VMEM((3,) + b, BF16) for b in blocks]
                               + [pltpu.SemaphoreType.DMA((nt, 4)), pltpu.SemaphoreType.DMA((nt, 4)),
                                  pltpu.SemaphoreType.DMA((nt, 5))])

    def peers(self):
        x, y, c, _ = _place()
        return [(x, y, 1 - c)]

    def middles(self, steps):
        return []

    def _copies(self, ins, outs, scratch):
        nt = len(self.tensors)
        own_out, sums_out, landed, mine = outs[:nt], outs[nt:], scratch[:nt], scratch[nt:2 * nt]
        send_sems, recv_sems, loc_sems = scratch[2 * nt:]
        x, y, c, chips = _place()
        chip_of = [2 * chip[0] + chip[1] for chip in chips]
        swaps, loads, stores = [], [], []
        for t, T in enumerate(self.tensors):
            for j in range(4):
                swaps.append(_remote(T.block(ins[t], 2 * j + 1 - c), landed[t].at[j], send_sems, recv_sems, (t, j),
                                     (x, y, 1 - c)))
            for k in range(3):
                loads.append(pltpu.make_async_copy(T.block(ins[t], 2 * chip_of[k] + c), mine[t].at[k], loc_sems.at[t, k]))
            stores.append(pltpu.make_async_copy(mine[t], sums_out[t], loc_sems.at[t, 3]))
            stores.append(pltpu.make_async_copy(landed[t].at[2 * x + y], own_out[t], loc_sems.at[t, 4]))
        return swaps, loads, stores, landed, mine, chip_of

    def start(self, ins, outs, scratch):
        swaps, loads, _, _, _, _ = self._copies(ins, outs, scratch)
        for cp in swaps + loads:
            cp.start()

    def finish(self, ins, outs, scratch):
        swaps, loads, stores, landed, mine, chip_of = self._copies(ins, outs, scratch)
        for cp in loads:
            cp.wait()
        for cp in swaps:
            cp.wait_recv()
        for t, T in enumerate(self.tensors):
            for k in range(3):
                acc, got = mine[t].at[k], landed[t].at[chip_of[k]]

                def add(i, carry, acc=acc, got=got):
                    rows = pl.ds(pl.multiple_of(i * PAIR_ROWS, PAIR_ROWS), PAIR_ROWS)
                    acc[rows, :] = (acc[rows, :].astype(F32) + got[rows, :].astype(F32)).astype(BF16)
                    return carry

                lax.fori_loop(0, T.block_shape[0] // PAIR_ROWS, add, 0)
        for cp in stores:
            cp.start()
        for cp in swaps:
            cp.wait_send()
        for cp in stores:
            cp.wait()


def _pair_reduce(grads, tensors, name):
    reduce = _PairReduce(tensors)
    nt = len(reduce.tensors)

    def body(*refs):
        ins, outs, scratch = refs[:nt], refs[nt:3 * nt], refs[3 * nt:]
        _enter(reduce)
        reduce.start(ins, outs, scratch)
        reduce.finish(ins, outs, scratch)

    return pl.pallas_call(
        body, name=name, in_specs=reduce.in_specs, out_specs=reduce.out_specs, out_shape=reduce.out_shape,
        scratch_shapes=reduce.scratch_shapes,
        compiler_params=pltpu.CompilerParams(vmem_limit_bytes=VMEM_LIMIT, collective_id=reduce.collective_id),
    )(*grads)


class _Scatter:
    def __init__(self, tensors):
        self.tensors = tuple(tensors)
        n = len(self.tensors)
        self.in_specs = [HBM_SPEC] * n
        self.out_specs = [HBM_SPEC] * n
        self.out_shape = [jax.ShapeDtypeStruct((2,) + T.block_shape, BF16) for T in self.tensors]
        self.scratch_shapes = [pltpu.VMEM(T.block_shape, BF16) for T in self.tensors] * 2 + [
            pltpu.SemaphoreType.DMA((n, 3)), pltpu.SemaphoreType.DMA((n, 3)), pltpu.SemaphoreType.DMA((n,))]

    collective_id = 2

    def peers(self):
        x, y, c, _ = _place()
        return [(1 - x, y, c), (x, 1 - y, c)]

    def middles(self, steps):
        return [(steps // 2, self.middle)]

    def _copies(self, ins, outs, scratch):
        n = len(self.tensors)
        landed, mine, (send_sems, recv_sems, loc_sems) = scratch[:n], scratch[n:2 * n], scratch[2 * n:]
        x, y, c, _ = _place()
        direct = (jnp.where(c == 0, 1 - x, x), jnp.where(c == 0, y, 1 - y), c)
        other = (jnp.where(c == 0, x, 1 - x), jnp.where(c == 0, 1 - y, y), c)
        k_direct = jnp.where(c == 0, 0, 1)
        to_direct, legs, loads, combined, arrivals = [], [], [], [], []
        for t in range(n):
            to_direct.append(_remote(ins[t].at[k_direct], outs[t].at[0], send_sems, recv_sems, (t, 0), direct))
            legs.append(_remote(ins[t].at[2], landed[t], send_sems, recv_sems, (t, 2), direct))
            loads.append(pltpu.make_async_copy(ins[t].at[1 - k_direct], mine[t], loc_sems.at[t]))
            combined.append(_remote(mine[t], outs[t].at[1], send_sems, recv_sems, (t, 1), other))
            arrivals.append(_remote(landed[t], landed[t], send_sems, recv_sems, (t, 2), direct))
        return to_direct, legs, loads, combined, arrivals, landed, mine

    def start(self, ins, outs, scratch):
        to_direct, legs, loads, _, _, _, _ = self._copies(ins, outs, scratch)
        for cp in to_direct + legs + loads:
            cp.start()

    def middle(self, ins, outs, scratch):
        _, _, loads, combined, arrivals, landed, mine = self._copies(ins, outs, scratch)
        for t, T in enumerate(self.tensors):
            loads[t].wait()
            arrivals[t].wait_recv()
            acc, got = mine[t], landed[t]

            def add(i, carry, acc=acc, got=got):
                rows = pl.ds(pl.multiple_of(i * PAIR_ROWS, PAIR_ROWS), PAIR_ROWS)
                acc[rows, :] = (acc[rows, :].astype(F32) + got[rows, :].astype(F32)).astype(BF16)
                return carry

            lax.fori_loop(0, T.block_shape[0] // PAIR_ROWS, add, 0)
            combined[t].start()

    def finish(self, ins, outs, scratch):
        to_direct, legs, _, combined, _, _, _ = self._copies(ins, outs, scratch)
        for cp in to_direct + combined:
            cp.wait()
        for cp in legs:
            cp.wait_send()


def _adamw(w, g, m, v):
    m = ADAM_B1 * m + (1.0 - ADAM_B1) * g
    v = ADAM_B2 * v + (1.0 - ADAM_B2) * (g * g)
    m_hat = m / (1.0 - ADAM_B1 ** ADAM_STEP)
    v_hat = v / (1.0 - ADAM_B2 ** ADAM_STEP)
    delta = -ADAM_LR * (m_hat / (jnp.sqrt(v_hat) + ADAM_EPS) + ADAM_WD * w)
    return delta, m, v


def _final_sum(T, g, lz1, lz2, where, w, m, v):
    rows, cols = T.block_shape
    sub = 4 if T.axis == 0 and rows % 64 == 0 and rows > 256 else 1
    blk = (rows // sub, cols)

    def body(where_ref, g_ref, l1_ref, l2_ref, w_ref, m_ref, v_ref, g_out, d_out, m_out, v_out):
        tot = g_ref[...].astype(F32) + l1_ref[...].astype(F32)
        for k in range(2):
            tot = tot + l2_ref[k].astype(F32)
        g_out[...] = tot
        d_out[...], m_out[...], v_out[...] = _adamw(w_ref[...], tot, m_ref[...], v_ref[...])

    def in_whole(r, wh):
        p = wh[0]
        return (0, p) if T.axis == 1 else (p * sub + r, 0)

    own = pl.BlockSpec(blk, lambda r, wh: (r, 0))
    return pl.pallas_call(
        body, name="grad_final_" + T.name,
        grid_spec=pltpu.PrefetchScalarGridSpec(
            num_scalar_prefetch=1, grid=(sub,),
            in_specs=[pl.BlockSpec(blk, in_whole),
                      own,
                      pl.BlockSpec((2,) + blk, lambda r, wh: (0, r, 0)), own, own, own],
            out_specs=[own] * 4),
        out_shape=[jax.ShapeDtypeStruct(T.block_shape, F32)] * 4,
        compiler_params=_params("arbitrary"),
    )(where, g, lz1, lz2, w, m, v)


VEC_PIECE = DR // 8


class _AllReduce:
    def __init__(self, items):
        self.items = tuple(items)
        n = len(self.items)
        self.in_specs = [HBM_SPEC] * n
        self.out_specs = [HBM_SPEC] * n
        self.out_shape = [jax.ShapeDtypeStruct(shape, F32) for shape, _ in self.items]
        pieces = [(shape[0] // 8, shape[1]) if axis == 0 else (shape[0], shape[1] // 8) for shape, axis in self.items]
        self.scratch_shapes = ([pltpu.VMEM((8,) + p, F32) for p in pieces] + [pltpu.VMEM(p, F32) for p in pieces] + [
            pltpu.SemaphoreType.DMA((2 * n, 8)), pltpu.SemaphoreType.DMA((2 * n, 8)), pltpu.SemaphoreType.DMA((2 * n,))])

    collective_id = None

    def peers(self):
        return []

    def middles(self, steps):
        return [(steps // 2, self.middle)]

    def _copies(self, ins, outs, scratch):
        n = len(self.items)
        landed, sums, (send_sems, recv_sems, loc_sems) = scratch[:n], scratch[n:2 * n], scratch[2 * n:]
        x, y, c, _ = _place()
        me = _device_index((x, y), c)

        def peer(r):
            return (1 - x if r & 4 else x, 1 - y if r & 2 else y, 1 - c if r & 1 else c)

        def piece(i, ref, p):
            shape, axis = self.items[i]
            if axis == 0:
                rows = shape[0] // 8
                return ref.at[pl.ds(pl.multiple_of(p * rows, 8), rows), :]
            cols = shape[1] // 8
            return ref.at[:, pl.ds(pl.multiple_of(p * cols, 128), cols)]

        own, scatter, arrivals, keep, spread, late = [], [], [], [], [], []
        for i in range(n):
            own.append(pltpu.make_async_copy(piece(i, ins[i], me), landed[i].at[0], loc_sems.at[2 * i]))
            keep.append(pltpu.make_async_copy(sums[i], piece(i, outs[i], me), loc_sems.at[2 * i + 1]))
            for r in range(1, 8):
                to = peer(r)
                p = _device_index(to[:2], to[2])
                scatter.append(_remote(piece(i, ins[i], p), landed[i].at[r], send_sems, recv_sems, (2 * i, r), to))
                spread.append(_remote(sums[i], piece(i, outs[i], me), send_sems, recv_sems, (2 * i + 1, r), to))
                late.append(_remote(sums[i], piece(i, outs[i], p), send_sems, recv_sems, (2 * i + 1, r), to))
        return own, scatter, keep, spread, late, landed, sums

    def start(self, ins, outs, scratch):
        own, scatter, _, _, _, _, _ = self._copies(ins, outs, scratch)
        for cp in own + scatter:
            cp.start()

    def middle(self, ins, outs, scratch):
        own, scatter, keep, spread, _, landed, sums = self._copies(ins, outs, scratch)
        for cp in own:
            cp.wait()
        for cp in scatter:
            cp.wait_recv()
        for i in range(len(self.items)):
            total = landed[i][0]
            for r in range(1, 8):
                total = total + landed[i][r]
            sums[i][...] = total
        for cp in keep + spread:
            cp.start()

    def finish(self, ins, outs, scratch):
        _, scatter, keep, spread, late, _, _ = self._copies(ins, outs, scratch)
        for cp in late:
            cp.wait_recv()
        for cp in scatter + spread:
            cp.wait_send()
        for cp in keep:
            cp.wait()


class _Both:
    def __init__(self, a, b):
        self.a, self.b = a, b
        self.in_specs, self.out_specs = a.in_specs + b.in_specs, a.out_specs + b.out_specs
        self.out_shape, self.scratch_shapes = a.out_shape + b.out_shape, a.scratch_shapes + b.scratch_shapes

    collective_id = None

    def peers(self):
        return []

    def _each(self, ins, outs, scratch):
        a = self.a
        i, o, s = len(a.in_specs), len(a.out_specs), len(a.scratch_shapes)
        return (a, ins[:i], outs[:o], scratch[:s]), (self.b, ins[i:], outs[o:], scratch[s:])

    def middles(self, steps):
        def of(which, middle):
            return lambda ins, outs, scratch: middle(*self._each(ins, outs, scratch)[which][1:])
        return [(at, of(which, middle)) for which, e in enumerate((self.a, self.b)) for at, middle in e.middles(steps)]

    def start(self, ins, outs, scratch):
        for e, i, o, s in self._each(ins, outs, scratch):
            e.start(i, o, s)

    def finish(self, ins, outs, scratch):
        for e, i, o, s in self._each(ins, outs, scratch):
            e.finish(i, o, s)


def _all_reduce(arrays, items, name):
    reduce = _AllReduce(items)
    n = len(items)

    def body(*refs):
        ins, outs, scratch = refs[:n], refs[n:2 * n], refs[2 * n:]
        reduce.start(ins, outs, scratch)
        reduce.middle(ins, outs, scratch)
        reduce.finish(ins, outs, scratch)

    return pl.pallas_call(
        body, name=name, in_specs=reduce.in_specs, out_specs=reduce.out_specs, out_shape=reduce.out_shape,
        scratch_shapes=reduce.scratch_shapes,
    )(*arrays)


def _adam_small(grads, wmv):
    n = len(grads)

    def body(*refs):
        g_refs, rest = refs[:n], refs[n:]
        ins, outs = rest[:3 * n], rest[3 * n:]
        for i in range(n):
            d, m, v = _adamw(ins[3 * i][...], g_refs[i][...], ins[3 * i + 1][...], ins[3 * i + 2][...])
            outs[3 * i][...], outs[3 * i + 1][...], outs[3 * i + 2][...] = d, m, v

    flat = [a for t in wmv for a in t]
    return pl.pallas_call(
        body, name="adam_small",
        in_specs=[VMEM_SPEC] * (4 * n), out_specs=[VMEM_SPEC] * (3 * n),
        out_shape=[jax.ShapeDtypeStruct(a.shape, F32) for a in flat],
    )(*grads, *flat)


WEIGHT_NAMES = ("norm_mix", "w_in", "w_pool_grp", "pool_scale", "w_pool_out", "conv_w", "conv_b", "w_rg_a", "b_rg_a", "w_rg_x",
                "b_rg_x", "lru_lambda", "w_rnn_out", "w_o", "norm_ffn", "w_ffn_in", "w_ffn_out", "norm_final")


def kernel(x, norm_mix, w_in, w_pool_grp, pool_scale, w_pool_out, conv_w, conv_b, w_rg_a, b_rg_a, w_rg_x, b_rg_x, lru_lambda, w_rnn_out, w_o, norm_ffn, w_ffn_in, w_ffn_out, norm_final, loss_target, m_norm_mix, m_w_in, m_w_pool_grp, m_pool_scale, m_w_pool_out, m_conv_w, m_conv_b, m_w_rg_a, m_b_rg_a, m_w_rg_x, m_b_rg_x, m_lru_lambda, m_w_rnn_out, m_w_o, m_norm_ffn, m_w_ffn_in, m_w_ffn_out, m_norm_final, v_norm_mix, v_w_in, v_w_pool_grp, v_pool_scale, v_w_pool_out, v_conv_w, v_conv_b, v_w_rg_a, v_b_rg_a, v_w_rg_x, v_b_rg_x, v_lru_lambda, v_w_rnn_out, v_w_o, v_norm_ffn, v_w_ffn_in, v_w_ffn_out, v_norm_final):
    w = dict(norm_mix=norm_mix, w_in=w_in, w_pool_grp=w_pool_grp, pool_scale=pool_scale, w_pool_out=w_pool_out, conv_w=conv_w,
             conv_b=conv_b, w_rg_a=w_rg_a, b_rg_a=b_rg_a, w_rg_x=w_rg_x, b_rg_x=b_rg_x, lru_lambda=lru_lambda,
             w_rnn_out=w_rnn_out, w_o=w_o, norm_ffn=norm_ffn, w_ffn_in=w_ffn_in, w_ffn_out=w_ffn_out, norm_final=norm_final)
    m = dict(norm_mix=m_norm_mix, w_in=m_w_in, w_pool_grp=m_w_pool_grp, pool_scale=m_pool_scale, w_pool_out=m_w_pool_out,
             conv_w=m_conv_w, conv_b=m_conv_b, w_rg_a=m_w_rg_a, b_rg_a=m_b_rg_a, w_rg_x=m_w_rg_x, b_rg_x=m_b_rg_x,
             lru_lambda=m_lru_lambda, w_rnn_out=m_w_rnn_out, w_o=m_w_o, norm_ffn=m_norm_ffn, w_ffn_in=m_w_ffn_in,
             w_ffn_out=m_w_ffn_out, norm_final=m_norm_final)
    v = dict(norm_mix=v_norm_mix, w_in=v_w_in, w_pool_grp=v_w_pool_grp, pool_scale=v_pool_scale, w_pool_out=v_w_pool_out,
             conv_w=v_conv_w, conv_b=v_conv_b, w_rg_a=v_w_rg_a, b_rg_a=v_b_rg_a, w_rg_x=v_w_rg_x, b_rg_x=v_b_rg_x,
             lru_lambda=v_lru_lambda, w_rnn_out=v_w_rnn_out, w_o=v_w_o, norm_ffn=v_norm_ffn, w_ffn_in=v_w_ffn_in,
             w_ffn_out=v_w_ffn_out, norm_final=v_norm_final)
    xi, yi, ci = (lax.axis_index(a) for a in MESH_AXES)
    chip = 2 * xi + yi

    def held(T, a):
        return jnp.swapaxes(a, 0, 1) if T.transposed else a

    where = jnp.stack([2 * chip + ci]).astype(jnp.int32)
    by_name = {T.name: T for T in GATHERED}
    block = {T.name: held(T, w[T.name][0]) for T in BIG}
    block["conv_w"] = jnp.pad(conv_w[0], ((0, CONV_W.rows - 4), (0, 0)))
    block["w_ffn_in_lo"] = block["w_ffn_in_hi"] = block["w_ffn_in"]

    def gather_of(*names):
        return dict(exchange=_Gather([by_name[n] for n in names]), exchange_operands=[block[n] for n in names])

    def pair_sums(names, partials, tag):
        out = _pair_reduce(partials, [by_name[n] for n in names], "grad_pair_reduce_" + tag)
        return list(out[:len(names)]), list(out[len(names):])

    xs, target = x[0], loss_target[0]
    wg_b, wa_b, wx_b = (a[0].astype(BF16) for a in (w_pool_grp, w_rg_a, w_rg_x))
    ba2, bx2 = b_rg_a.reshape(1, DR), b_rg_x.reshape(1, DR)
    first = ("w_in", "w_pool_out", "w_rnn_out", "conv_w", "w_o")
    order = jnp.stack([chip, 2 * (1 - xi) + yi, 2 * xi + (1 - yi), 2 * (1 - xi) + (1 - yi)]).astype(jnp.int32)
    (proj, h1), (w_in_g, w_pool_out_g, w_rnn_out_g, conv_g, w_o_g) = _in_proj_gather(
        xs, norm_mix, [block[n] for n in first], [by_name[n] for n in first], order)
    mixer_weights = (wg_b, pool_scale, w_pool_out_g, conv_g[0:4], conv_b, wa_b, ba2, wx_b, bx2, lru_lambda, w_rnn_out_g)
    (pm, y_pool, hr, z, y_rnn, kept, gates), (w_ffn_lo_g, w_ffn_hi_g) = _mixer_fwd(
        proj, *mixer_weights, **gather_of("w_ffn_in_lo", "w_ffn_in_hi"))
    (mix, x2, h2), _ = _merge_out(xs, proj, y_pool, y_rnn, w_o_g, norm_ffn)
    (gu, act), (w_ffn_out_g,) = _ffn_up(h2, w_ffn_lo_g, w_ffn_hi_g, **gather_of("w_ffn_out"))
    dx3, dx3b, loss_part, dvec_fin = _ffn_down_loss(act, x2, target, w_ffn_out_g, norm_final.reshape(1, D))

    dgu = _ffn_bwd_down(dx3b, gu, w_ffn_out_g)
    dx2, dx2b, dmixo, dvec_ffn = _ffn_bwd_up(dgu, x2, dx3, w_ffn_lo_g, w_ffn_hi_g, norm_ffn, w_o_g)
    names_a = ("w_ffn_in", "w_ffn_out", "w_o")
    g_ffn_out = _wgrad(act, dx3b, "wgrad_ffn_out", 1408, 512)
    g_ffn_in, (own_ffn_out, sums_ffn_out) = _wgrad(
        dgu, h2, "wgrad_ffn_in", 1408, 512, exchange=_PairReduce([by_name["w_ffn_out"]]), exchange_operands=[g_ffn_out])
    g_o, (own_ffn_in, sums_ffn_in) = _wgrad(
        mix, dx2b, "wgrad_o", 1024, 256, exchange=_PairReduce([by_name["w_ffn_in"]]), exchange_operands=[g_ffn_in])
    (own_o,), (sums_o,) = pair_sums(("w_o",), [g_o], "o")
    part_a = [g_ffn_in, g_ffn_out, g_o]
    lz1_a, sums_a = [own_ffn_in, own_ffn_out, own_o], [sums_ffn_in, sums_ffn_out, sums_o]
    (dproj, dypb, dyrb, dmat, dvec_mix), lz2_a = _mixer_bwd(
        proj, dmixo, y_pool, y_rnn, hr, kept, gates, *mixer_weights,
        exchange=_Scatter([by_name[n] for n in names_a]), exchange_operands=sums_a)
    names_b = ("w_pool_out", "w_rnn_out")
    part_b = [_wgrad(pm, dypb, "wgrad_pool_out", 512, 256), _wgrad(z, dyrb, "wgrad_rnn_out", 1024, 256)]
    lz1_b, sums_b = pair_sums(names_b, part_b, "mix")
    dvec = jnp.concatenate([dvec_mix[0:9], dvec_fin[0:1], dvec_ffn[0:1], jnp.pad(loss_part, ((0, 0), (0, DR - 1))),
                            jnp.zeros((VEC_ROWS - 12, DR), F32)], axis=0)
    g_in, exchanged = _wgrad(
        dproj, h1, "wgrad_in", 1152, 1024,
        exchange=_Both(_Scatter([by_name[n] for n in names_b]), _AllReduce([((MAT_ROWS, HD), 0), ((VEC_ROWS, DR), 1)])),
        exchange_operands=sums_b + [dmat, dvec])
    lz2_b, (mat, vec) = exchanged[:2], exchanged[2:]
    loss = vec[VEC_LOSS, 0]
    lz1_c, sums_c = pair_sums(("w_in",), [g_in], "in")
    (grad_x, dvec_in), lz2_c = _in_bwd(dproj, xs, dx2, norm_mix, w_in_g,
                                       exchange=_Scatter([by_name["w_in"]]), exchange_operands=sums_c)
    (vec_in,) = _all_reduce([dvec_in], [((8, D), 1)], "all_reduce_norm_mix")

    grads, delta, new_m, new_v = {}, {}, {}, {}
    for n, g, l1, l2 in zip(names_a + names_b + ("w_in",), part_a + part_b + [g_in], lz1_a + lz1_b + lz1_c,
                            lz2_a + lz2_b + lz2_c):
        T = by_name[n]
        out = _final_sum(T, g, l1, l2, where, held(T, w[n][0]), held(T, m[n][0]), held(T, v[n][0]))
        grads[n], delta[n], new_m[n], new_v[n] = (held(T, a) for a in out)
    me = 4 * xi + 2 * yi + ci
    small_grads = dict(
        w_pool_grp=mat[0:MAT_WA], w_rg_a=mat[MAT_WA:MAT_WX], w_rg_x=mat[MAT_WX:MAT_ROWS],
        pool_scale=vec[VEC_SCALE:VEC_SCALE + 1, 0:DP], conv_b=vec[VEC_CONV_B:VEC_CONV_B + 1],
        b_rg_a=vec[VEC_BA:VEC_BA + 1], b_rg_x=vec[VEC_BX:VEC_BX + 1], lru_lambda=vec[VEC_LAM:VEC_LAM + 1],
        conv_w=lax.dynamic_slice(vec, (VEC_CONV_W, VEC_PIECE * me), (4, VEC_PIECE)),
        norm_final=vec[VEC_NORM_FINAL:VEC_NORM_FINAL + 1], norm_ffn=vec[VEC_NORM_FFN:VEC_NORM_FFN + 1],
        norm_mix=vec_in[0:1])
    names = list(small_grads)
    as2d = lambda a, g: a.reshape(g.shape)
    upd = _adam_small([small_grads[n] for n in names],
                      [(as2d(w[n], small_grads[n]), as2d(m[n], small_grads[n]), as2d(v[n], small_grads[n])) for n in names])
    for i, n in enumerate(names):
        grads[n] = small_grads[n]
        delta[n], new_m[n], new_v[n] = upd[3 * i:3 * i + 3]

    shaped = lambda d: [d[n].reshape(w[n].shape) for n in WEIGHT_NAMES]
    return (loss, grad_x[None], *shaped(grads), *shaped(delta), *shaped(new_m), *shaped(new_v))
```

```python
import math

import jax
import jax.numpy as jnp
from jax import lax
from jax.experimental import pallas as pl
from jax.experimental.pallas import tpu as pltpu

F32 = jnp.float32
BF16 = jnp.bfloat16

D = 1024
DP = 512
PG = 128
WINDOWS = (2, 4, 8, 16)
DR = 1024
NH = 8
HD = 128
DIN = 4608
DFF = 2816
EPS = 1e-6
LRU_C = 8.0
POOL_HALO = 16
CONV_HALO = 8
KEPT = 3

ADAM_LR = 0.001
ADAM_B1 = 0.9
ADAM_B2 = 0.999
ADAM_EPS = 1e-08
ADAM_WD = 0.01
ADAM_STEP = 10

VMEM_LIMIT = 56 * 1024 * 1024
MESH_AXES = ("x", "y", "c")
MESH = pl.DeviceIdType.MESH


def _dot(a, b):
    return jnp.dot(a, b, preferred_element_type=F32)


def _dot_nt(a, b):
    return lax.dot_general(a, b, (((1,), (1,)), ((), ())), preferred_element_type=F32)


def _dot_tn(a, b):
    return lax.dot_general(a, b, (((0,), (0,)), ((), ())), preferred_element_type=F32)


def _params(*sem):
    return pltpu.CompilerParams(dimension_semantics=sem, vmem_limit_bytes=VMEM_LIMIT)


def _resident(shape):
    nd = len(shape)
    return pl.BlockSpec(shape, lambda i: (0,) * nd, pipeline_mode=pl.Buffered(1))


def _rows(shape_cols, tm):
    return pl.BlockSpec((tm, shape_cols), lambda i: (i, 0))


def _call(body, name, grid, in_specs, out_specs, out_shape, operands, scratch_shapes=(), exchange=None, exchange_operands=()):
    n_in, n_out, n_scr = len(in_specs), len(out_specs), len(scratch_shapes)
    steps = math.prod(grid)
    if exchange is None:
        outs = pl.pallas_call(body, name=name, grid=grid, in_specs=in_specs, out_specs=out_specs, out_shape=out_shape,
                              scratch_shapes=list(scratch_shapes), compiler_params=_params(*["arbitrary"] * len(grid)))(*operands)
        return outs, []
    e_in, e_out = len(exchange.in_specs), len(exchange.out_specs)

    def hosted(*refs):
        ins, refs = refs[:n_in], refs[n_in:]
        e_ins, refs = refs[:e_in], refs[e_in:]
        outs, refs = refs[:n_out], refs[n_out:]
        e_outs, refs = refs[:e_out], refs[e_out:]
        scr, e_scr = refs[:n_scr], refs[n_scr:]
        step = pl.program_id(0)
        for axis in range(1, len(grid)):
            step = step * grid[axis] + pl.program_id(axis)
        @pl.when(step == 0)
        def _():
            _enter(exchange)
            exchange.start(e_ins, e_outs, e_scr)

        for at, middle in exchange.middles(steps):
            pl.when(step == at)(lambda middle=middle: middle(e_ins, e_outs, e_scr))
        body(*ins, *outs, *scr)
        pl.when(step == steps - 1)(lambda: exchange.finish(e_ins, e_outs, e_scr))

    outs = pl.pallas_call(
        hosted, name=name, grid=grid, in_specs=list(in_specs) + exchange.in_specs,
        out_specs=list(out_specs) + exchange.out_specs, out_shape=list(out_shape) + exchange.out_shape,
        scratch_shapes=list(scratch_shapes) + exchange.scratch_shapes,
        compiler_params=pltpu.CompilerParams(dimension_semantics=("arbitrary",) * len(grid), vmem_limit_bytes=VMEM_LIMIT,
                                             collective_id=exchange.collective_id))(*operands, *exchange_operands)
    return outs[:n_out], outs[n_out:]


def _enter(exchange):
    peers = exchange.peers()
    if peers:
        barrier = pltpu.get_barrier_semaphore()
        for peer in peers:
            pl.semaphore_signal(barrier, inc=1, device_id=peer, device_id_type=MESH)
        pl.semaphore_wait(barrier, len(peers))


GELU_C = math.sqrt(2.0 / math.pi)
GELU_K = 0.044715 * GELU_C


def _gelu(x, with_grad=False):
    x2 = x * x
    t = jnp.tanh(x * (GELU_C + GELU_K * x2))
    hx = 0.5 * x
    y = hx + hx * t
    if not with_grad:
        return y
    return y, 0.5 + 0.5 * t + hx * (1.0 - t * t) * (GELU_C + (3.0 * GELU_K) * x2)


def _softplus_neg(lam):
    z = jnp.exp(-jnp.abs(lam))
    u = 1.0 + z
    dlt = u - 1.0
    log1p = jnp.where(dlt == 0.0, z, jnp.log(u) * (z / jnp.where(dlt == 0.0, 1.0, dlt)))
    return jnp.maximum(-lam, 0.0) + log1p


def _sigmoid(x):
    return 0.5 * jnp.tanh(0.5 * x) + 0.5


def _linear_scan(out_ref, A, B, h0, reverse):
    n = A.shape[0]
    sub = lax.broadcasted_iota(jnp.int32, (8, 1), 0)
    tiles = range(n // 8 - 1, -1, -1) if reverse else range(n // 8)
    carry = h0
    for j in tiles:
        a, b = A[8 * j:8 * j + 8, :], B[8 * j:8 * j + 8, :]
        for d in (1, 2, 4):
            keep = (sub < 8 - d) if reverse else (sub >= d)
            shift = 8 - d if reverse else d
            b = jnp.where(keep, a * pltpu.roll(b, shift, axis=0) + b, b)
            a = jnp.where(keep, a * pltpu.roll(a, shift, axis=0), a)
        h = a * carry + b
        out_ref[8 * j:8 * j + 8, :] = h
        carry = h[0:1, :] if reverse else h[7:8, :]
    return carry


def _pool_windows(ext, shift_sign):
    n = ext.shape[0]
    s = ext
    outs = []
    for w in WINDOWS:
        d = w // 2
        s = s + pltpu.roll(s, d if shift_sign > 0 else n - d, axis=0)
        outs.append(s[:, :PG])
        s = s[:, PG:]
    return outs


def _conv_taps(uext):
    taps = []
    for k in range(4):
        sh = 3 - k
        v = uext if sh == 0 else pltpu.roll(uext, sh, axis=0)
        taps.append(v[CONV_HALO:, :])
    return taps


def _gates(v, wa_ref, ba_ref, wx_ref, bx_ref, sp):
    vb = v.astype(BF16)
    ra, rx = [], []
    for h in range(NH):
        vh = vb[:, h * HD:(h + 1) * HD]
        ra.append(_dot(vh, wa_ref[h]))
        rx.append(_dot(vh, wx_ref[h]))
    r = _sigmoid(jnp.concatenate(ra, axis=1) + ba_ref[...])
    i = _sigmoid(jnp.concatenate(rx, axis=1) + bx_ref[...])
    log_a = r * ((-LRU_C) * sp)
    a = jnp.exp(log_a)
    one_minus = -jnp.tanh(log_a) * (1.0 + a * a)
    return r, i, a, jnp.sqrt(one_minus), lax.rsqrt(one_minus)


def _mixer_fwd(proj, wg, scale, w_pool_out, conv_w, conv_b, wa, ba, wx, bx, lam, w_rnn_out, exchange=None,
               exchange_operands=(), tm=256):
    S = proj.shape[0]
    UW = DP + 2 * DR

    def body(proj_ref, wg_ref, scale_ref, wpo_ref, cw_ref, cb_ref, wa_ref, ba_ref, wx_ref, bx_ref, lam_ref, wro_ref,
             pm_ref, ypool_ref, hr_ref, z_ref, yrnn_ref, kept_ref, gates_ref, pool_carry, conv_carry, h_carry):
        i = pl.program_id(0)

        @pl.when(i == 0)
        def _():
            pool_carry[...] = jnp.zeros_like(pool_carry)
            conv_carry[...] = jnp.zeros_like(conv_carry)
            h_carry[...] = jnp.zeros_like(h_carry)

        rows = lax.broadcasted_iota(jnp.int32, (tm, 1), 0)
        t_glob = i * tm + rows

        u_pool = proj_ref[:, 0:DP]
        ext = jnp.concatenate([pool_carry[...], u_pool], axis=0)
        pool_carry[...] = u_pool[tm - POOL_HALO:, :]
        sums = _pool_windows(ext, +1)
        mixed = []
        for g, w in enumerate(WINDOWS):
            inv_cnt = 1.0 / jnp.minimum(t_glob + 1, w).astype(F32)
            pooled_g = sums[g][POOL_HALO:, :] * inv_cnt - u_pool[:, g * PG:(g + 1) * PG]
            mixed.append(_dot(pooled_g.astype(BF16), wg_ref[g]))
        pm = (jnp.concatenate(mixed, axis=1) * scale_ref[...]).astype(BF16)
        pm_ref[...] = pm
        ypool_ref[...] = _dot(pm, wpo_ref[...]).astype(BF16)

        u_rnn = proj_ref[:, DP:DP + DR]
        uext = jnp.concatenate([conv_carry[...], u_rnn], axis=0)
        conv_carry[...] = u_rnn[tm - CONV_HALO:, :]
        taps = _conv_taps(uext)
        v = cb_ref[...]
        for k in range(4):
            v = v + taps[k] * cw_ref[k:k + 1, :]
        sp = _softplus_neg(lam_ref[...])
        r, gi, a, mult, _ = _gates(v, wa_ref, ba_ref, wx_ref, bx_ref, sp)
        for k, kept in enumerate((v, a, mult)):
            kept_ref[k] = kept
        for k, kept in enumerate((r, gi)):
            gates_ref[k] = kept.astype(BF16)
        h_carry[0:1, :] = _linear_scan(hr_ref, a, mult * gi * v, h_carry[0:1, :], reverse=False)
        z = (hr_ref[...] * _gelu(proj_ref[:, DP + DR:UW])).astype(BF16)
        z_ref[...] = z
        yrnn_ref[...] = _dot(z, wro_ref[...]).astype(BF16)

    return _call(
        body, "mixer_fwd", (S // tm,),
        in_specs=[_rows(UW, tm), _resident((4, PG, PG)), _resident((1, DP)), _resident((DP, D)), _resident((4, DR)),
                  _resident((1, DR)), _resident((NH, HD, HD)), _resident((1, DR)), _resident((NH, HD, HD)),
                  _resident((1, DR)), _resident((1, DR)), _resident((DR, D))],
        out_specs=[_rows(DP, tm), _rows(D, tm), _rows(DR, tm), _rows(DR, tm), _rows(D, tm),
                   pl.BlockSpec((KEPT, tm, DR), lambda i: (0, i, 0)), pl.BlockSpec((2, tm, DR), lambda i: (0, i, 0))],
        out_shape=[jax.ShapeDtypeStruct((S, DP), BF16),
                   jax.ShapeDtypeStruct((S, D), BF16), jax.ShapeDtypeStruct((S, DR), F32),
                   jax.ShapeDtypeStruct((S, DR), BF16), jax.ShapeDtypeStruct((S, D), BF16),
                   jax.ShapeDtypeStruct((KEPT, S, DR), F32), jax.ShapeDtypeStruct((2, S, DR), BF16)],
        scratch_shapes=[pltpu.VMEM((POOL_HALO, DP), F32), pltpu.VMEM((CONV_HALO, DR), F32), pltpu.VMEM((8, DR), F32)],
        operands=(proj, wg, scale, w_pool_out, conv_w, conv_b, wa, ba, wx, bx, lam, w_rnn_out),
        exchange=exchange, exchange_operands=exchange_operands)


FF_CHUNKS = ((0, 768), (768, 1536), (1536, 2304), (2304, DFF))


def _rms(x):
    r = lax.rsqrt(jnp.mean(x * x, axis=-1, keepdims=True) + EPS)
    return r, x * r


def _rms_bwd(dh, g, r, xh):
    dxh = dh * g
    return r * (dxh - xh * jnp.mean(dxh * xh, axis=-1, keepdims=True))


def _merge_out(x, proj, y_pool, y_rnn, w_o, norm_ffn, exchange=None, exchange_operands=(), tm=512):
    S = x.shape[0]
    GL0 = (DP + 2 * DR) // 512

    def gl_spec(k):
        return pl.BlockSpec((tm, 512), lambda i: (i, GL0 + k))

    def body(x_ref, gl0, gl1, gl2, gl3, yp_ref, yr_ref, wo_ref, gf_ref, mix_ref, x2_ref, h2_ref):
        s_p = _sigmoid(jnp.concatenate([gl0[...], gl1[...]], axis=1))
        s_r = _sigmoid(jnp.concatenate([gl2[...], gl3[...]], axis=1))
        mix = (s_p * yp_ref[...].astype(F32) + s_r * yr_ref[...].astype(F32)).astype(BF16)
        mix_ref[...] = mix
        x2 = x_ref[...] + _dot(mix, wo_ref[...])
        x2_ref[...] = x2
        _, xh2 = _rms(x2)
        h2_ref[...] = (xh2 * gf_ref[...]).astype(BF16)

    return _call(
        body, "merge_out", (S // tm,),
        in_specs=[_rows(D, tm), gl_spec(0), gl_spec(1), gl_spec(2), gl_spec(3), _rows(D, tm), _rows(D, tm),
                  _resident((D, D)), _resident((1, D))],
        out_specs=[_rows(D, tm), _rows(D, tm), _rows(D, tm)],
        out_shape=[jax.ShapeDtypeStruct((S, D), BF16), jax.ShapeDtypeStruct((S, D), F32), jax.ShapeDtypeStruct((S, D), BF16)],
        operands=(x, proj, proj, proj, proj, y_pool, y_rnn, w_o, norm_ffn),
        exchange=exchange, exchange_operands=exchange_operands)


def _ffn_up(h2, w_lo, w_hi, exchange=None, exchange_operands=(), tm=512):
    S = h2.shape[0]
    HALF = D // 2

    def body(h_ref, lo_ref, hi_ref, back_ref, act_ref):
        h_lo, h_hi = h_ref[:, 0:HALF], h_ref[:, HALF:D]
        for c0, c1 in FF_CHUNKS:
            gate = _dot_nt(h_lo, lo_ref[c0:c1, :]) + _dot_nt(h_hi, hi_ref[c0:c1, :])
            up = _dot_nt(h_lo, lo_ref[DFF + c0:DFF + c1, :]) + _dot_nt(h_hi, hi_ref[DFF + c0:DFF + c1, :])
            sg = _sigmoid(gate)
            silu = gate * sg
            back_ref[:, c0:c1] = (up * (sg * (1.0 + gate * (1.0 - sg)))).astype(BF16)
            back_ref[:, DFF + c0:DFF + c1] = silu.astype(BF16)
            act_ref[:, c0:c1] = (silu * up).astype(BF16)

    return _call(
        body, "ffn_up", (S // tm,),
        in_specs=[_rows(D, tm), _resident((2 * DFF, HALF)), _resident((2 * DFF, HALF))],
        out_specs=[_rows(2 * DFF, tm), _rows(DFF, tm)],
        out_shape=[jax.ShapeDtypeStruct((S, 2 * DFF), BF16), jax.ShapeDtypeStruct((S, DFF), BF16)],
        operands=(h2, w_lo, w_hi), exchange=exchange, exchange_operands=exchange_operands)


def _ffn_down_loss(act, x2, target, w_ffn_out, norm_final, tm=512):
    S = act.shape[0]

    def body(act_ref, x2_ref, t_ref, w_ref, gn_ref, dx3_ref, dx3b_ref, loss_ref, dvec_ref):
        i = pl.program_id(0)

        @pl.when(i == 0)
        def _():
            loss_ref[...] = jnp.zeros_like(loss_ref)
            dvec_ref[...] = jnp.zeros_like(dvec_ref)

        x3 = x2_ref[...] + _dot(act_ref[...], w_ref[...])
        r3, xh3 = _rms(x3)
        g_fin = gn_ref[...]
        e = xh3 * g_fin - t_ref[...]
        loss_ref[...] += jnp.sum(e * e, axis=(0, 1), keepdims=True) * (0.5 / D)
        dy = e * (1.0 / D)
        dvec_ref[0:1, :] += jnp.sum(dy * xh3, axis=0, keepdims=True)
        dx3 = _rms_bwd(dy, g_fin, r3, xh3)
        dx3_ref[...] = dx3
        dx3b_ref[...] = dx3.astype(BF16)

    return pl.pallas_call(
        body, name="ffn_down_loss", grid=(S // tm,),
        in_specs=[_rows(DFF, tm), _rows(D, tm), _rows(D, tm), _resident((DFF, D)), _resident((1, D))],
        out_specs=[_rows(D, tm), _rows(D, tm), _resident((1, 1)), _resident((8, D))],
        out_shape=[jax.ShapeDtypeStruct((S, D), F32), jax.ShapeDtypeStruct((S, D), BF16),
                   jax.ShapeDtypeStruct((1, 1), F32), jax.ShapeDtypeStruct((8, D), F32)],
        compiler_params=_params("arbitrary"),
    )(act, x2, target, w_ffn_out, norm_final)


def _ffn_bwd_down(dx3b, gu, w_ffn_out, tm=512):
    S = dx3b.shape[0]

    def body(d_ref, back_ref, w_ref, dgu_ref):
        d = d_ref[...]
        for c0, c1 in FF_CHUNKS:
            dact = _dot_nt(d, w_ref[c0:c1, :])
            dgu_ref[:, c0:c1] = (dact * back_ref[:, c0:c1].astype(F32)).astype(BF16)
            dgu_ref[:, DFF + c0:DFF + c1] = (dact * back_ref[:, DFF + c0:DFF + c1].astype(F32)).astype(BF16)

    return pl.pallas_call(
        body, name="ffn_bwd_down", grid=(S // tm,),
        in_specs=[_rows(D, tm), _rows(2 * DFF, tm), _resident((DFF, D))],
        out_specs=_rows(2 * DFF, tm),
        out_shape=jax.ShapeDtypeStruct((S, 2 * DFF), BF16),
        compiler_params=_params("parallel"),
    )(dx3b, gu, w_ffn_out)


def _ffn_bwd_up(dgu, x2, dx3, w_lo, w_hi, norm_ffn, w_o, tm=512):
    S = dgu.shape[0]
    HALF = D // 2

    def body(dgu_ref, x2_ref, dx3_ref, lo_ref, hi_ref, gf_ref, wo_ref, dx2_ref, dx2b_ref, dmixo_ref, dvec_ref):
        i = pl.program_id(0)

        @pl.when(i == 0)
        def _():
            dvec_ref[...] = jnp.zeros_like(dvec_ref)

        dgate, dup = dgu_ref[:, 0:DFF], dgu_ref[:, DFF:2 * DFF]
        dh2 = jnp.concatenate([_dot(dgate, w[0:DFF, :]) + _dot(dup, w[DFF:2 * DFF, :]) for w in (lo_ref, hi_ref)], axis=1)
        r2, xh2 = _rms(x2_ref[...])
        dvec_ref[0:1, :] += jnp.sum(dh2 * xh2, axis=0, keepdims=True)
        dx2 = dx3_ref[...] + _rms_bwd(dh2, gf_ref[...], r2, xh2)
        dx2_ref[...] = dx2
        dx2b = dx2.astype(BF16)
        dx2b_ref[...] = dx2b
        dmixo_ref[...] = _dot_nt(dx2b, wo_ref[...]).astype(BF16)

    return pl.pallas_call(
        body, name="ffn_bwd_up", grid=(S // tm,),
        in_specs=[_rows(2 * DFF, tm), _rows(D, tm), _rows(D, tm), _resident((2 * DFF, HALF)), _resident((2 * DFF, HALF)),
                  _resident((1, D)), _resident((D, D))],
        out_specs=[_rows(D, tm), _rows(D, tm), _rows(D, tm), _resident((8, D))],
        out_shape=[jax.ShapeDtypeStruct((S, D), F32), jax.ShapeDtypeStruct((S, D), BF16), jax.ShapeDtypeStruct((S, D), BF16),
                   jax.ShapeDtypeStruct((8, D), F32)],
        compiler_params=_params("arbitrary"),
    )(dgu, x2, dx3, w_lo, w_hi, norm_ffn, w_o)


VEC_ROWS = 16
MAT_WA = 4 * PG
MAT_WX = MAT_WA + NH * HD
MAT_ROWS = MAT_WX + NH * HD


def _mixer_bwd(proj, dmixo, y_pool, y_rnn, hr, kept, gates, wg, scale, w_pool_out, conv_w, conv_b, wa, ba, wx, bx, lam, w_rnn_out,
               exchange=None, exchange_operands=(), tm=256):
    S = proj.shape[0]
    nt = S // tm

    def rev(cols):
        return pl.BlockSpec((tm, cols), lambda i: (nt - 1 - i, 0))

    def halo(rows_, cols):
        per = tm // rows_
        return pl.BlockSpec((rows_, cols), lambda i: (jnp.maximum((nt - 1 - i) * per - 1, 0), 0))

    def body(proj_ref, projh_ref, dmixo_ref, yp_ref, yr_ref, hr_ref, hrh_ref, kept_ref, gates_ref, wg_ref, scale_ref, wpo_ref, cw_ref, cb_ref,
             wa_ref, ba_ref, wx_ref, bx_ref, lam_ref, wro_ref,
             dproj_ref, dypb_ref, dyrb_ref, dmat_ref, dvec_ref,
             q_carry, dv_carry, a_carry, g_carry, g_scr):
        i = pl.program_id(0)
        ti = nt - 1 - i

        @pl.when(i == 0)
        def _():
            q_carry[...] = jnp.zeros_like(q_carry)
            dv_carry[...] = jnp.zeros_like(dv_carry)
            a_carry[...] = jnp.zeros_like(a_carry)
            g_carry[...] = jnp.zeros_like(g_carry)
            dmat_ref[...] = jnp.zeros_like(dmat_ref)
            dvec_ref[...] = jnp.zeros_like(dvec_ref)

        rows = lax.broadcasted_iota(jnp.int32, (tm, 1), 0)
        t_glob = ti * tm + rows
        has_prev = (ti > 0).astype(F32)
        dmixo = dmixo_ref[...].astype(F32)

        s_p = _sigmoid(proj_ref[:, DP + 2 * DR:DP + 2 * DR + D])
        s_r = _sigmoid(proj_ref[:, DP + 2 * DR + D:DIN])
        dproj_ref[:, DP + 2 * DR:DP + 2 * DR + D] = (dmixo * yp_ref[...].astype(F32) * s_p * (1.0 - s_p)).astype(BF16)
        dproj_ref[:, DP + 2 * DR + D:DIN] = (dmixo * yr_ref[...].astype(F32) * s_r * (1.0 - s_r)).astype(BF16)
        dyp = (dmixo * s_p).astype(BF16)
        dyr = (dmixo * s_r).astype(BF16)
        dypb_ref[...] = dyp
        dyrb_ref[...] = dyr

        dz = _dot_nt(dyr, wro_ref[...])
        u_gate = proj_ref[:, DP + DR:DP + 2 * DR]
        gg, dgelu = _gelu(u_gate, with_grad=True)
        hr_t = hr_ref[...]
        dproj_ref[:, DP + DR:DP + 2 * DR] = (dz * hr_t * dgelu).astype(BF16)
        dhr = dz * gg

        sp = _softplus_neg(lam_ref[...])
        v, a, mult = (kept_ref[k] for k in range(KEPT))
        r, gi = (gates_ref[k].astype(F32) for k in range(2))
        inv_mult = 1.0 / mult

        C = jnp.where(rows == tm - 1, a_carry[0:1, :], pltpu.roll(a, tm - 1, axis=0))
        g_carry[0:1, :] = _linear_scan(g_scr, C, dhr, g_carry[0:1, :], reverse=True)
        a_carry[0:1, :] = a[0:1, :]
        g = g_scr[...]

        h_prev = jnp.where(rows == 0, hrh_ref[7:8, :] * has_prev, pltpu.roll(hr_t, 1, axis=0))
        da = g * h_prev
        gm = g * mult
        dmult = g * gi * v
        di = gm * v
        dv = gm * gi
        dlog_a = da * a - dmult * (a * a * inv_mult)
        dvec_ref[4:5, :] += jnp.sum(dlog_a * r, axis=0, keepdims=True)
        dra = (dlog_a * ((-LRU_C) * sp) * r * (1.0 - r))
        drx = di * gi * (1.0 - gi)
        dvec_ref[2:3, :] += jnp.sum(dra, axis=0, keepdims=True)
        dvec_ref[3:4, :] += jnp.sum(drx, axis=0, keepdims=True)
        drab = dra.astype(BF16)
        drxb = drx.astype(BF16)
        vb = v.astype(BF16)
        dvg = []
        for h in range(NH):
            sl = slice(h * HD, (h + 1) * HD)
            dvg.append(_dot_nt(drab[:, sl], wa_ref[h]) + _dot_nt(drxb[:, sl], wx_ref[h]))
            dmat_ref[MAT_WA + h * HD:MAT_WA + (h + 1) * HD, :] += _dot_tn(vb[:, sl], drab[:, sl])
            dmat_ref[MAT_WX + h * HD:MAT_WX + (h + 1) * HD, :] += _dot_tn(vb[:, sl], drxb[:, sl])
        dv = dv + jnp.concatenate(dvg, axis=1)
        dvec_ref[1:2, :] += jnp.sum(dv, axis=0, keepdims=True)
        dvext = jnp.concatenate([dv, dv_carry[...]], axis=0)
        dv_carry[...] = dv[0:CONV_HALO, :]
        n = tm + CONV_HALO
        u_rnn = proj_ref[:, DP:DP + DR]
        du_rnn = dv * cw_ref[3:4, :]
        dvec_ref[8:9, :] += jnp.sum(dv * u_rnn, axis=0, keepdims=True)
        for k in range(3):
            dv_k = pltpu.roll(dvext, n - (3 - k), axis=0)[0:tm, :]
            du_rnn = du_rnn + dv_k * cw_ref[k:k + 1, :]
            dvec_ref[5 + k:6 + k, :] += jnp.sum(dv_k * u_rnn, axis=0, keepdims=True)
        dproj_ref[:, DP:DP + DR] = du_rnn.astype(BF16)

        dpm = _dot_nt(dyp, wpo_ref[...])
        u_pool = proj_ref[:, 0:DP]
        ext = jnp.concatenate([projh_ref[:, 0:DP] * has_prev, u_pool], axis=0)
        sums = _pool_windows(ext, +1)
        scale_v = scale_ref[...]
        qs = []
        dpooled = []
        dscale = []
        for gi_, w in enumerate(WINDOWS):
            sl = slice(gi_ * PG, (gi_ + 1) * PG)
            inv_cnt = 1.0 / jnp.minimum(t_glob + 1, w).astype(F32)
            pooled_b = (sums[gi_][POOL_HALO:, :] * inv_cnt - u_pool[:, sl]).astype(BF16)
            mixed_g = _dot(pooled_b, wg_ref[gi_])
            dscale.append(jnp.sum(dpm[:, sl] * mixed_g, axis=0, keepdims=True))
            dmixed_b = (dpm[:, sl] * scale_v[:, sl]).astype(BF16)
            dmat_ref[gi_ * PG:(gi_ + 1) * PG, :] += _dot_tn(pooled_b, dmixed_b)
            dp_g = _dot_nt(dmixed_b, wg_ref[gi_])
            dpooled.append(dp_g)
            qs.append(dp_g * inv_cnt)
        dvec_ref[0:1, 0:DP] += jnp.concatenate(dscale, axis=1)
        q = jnp.concatenate(qs, axis=1)
        qext = jnp.concatenate([q, q_carry[...]], axis=0)
        q_carry[...] = q[0:POOL_HALO, :]
        tsum = _pool_windows(qext, -1)
        for gi_ in range(4):
            dproj_ref[:, gi_ * PG:(gi_ + 1) * PG] = (tsum[gi_][0:tm, :] - dpooled[gi_]).astype(BF16)

        @pl.when(i == nt - 1)
        def _():
            dvec_ref[4:5, :] = dvec_ref[4:5, :] * (LRU_C * _sigmoid(-lam_ref[...]))

    return _call(
        body, "mixer_bwd", (nt,),
        in_specs=[rev(DIN), halo(POOL_HALO, DIN), rev(D), rev(D), rev(D), rev(DR), halo(8, DR),
                  pl.BlockSpec((KEPT, tm, DR), lambda i: (0, nt - 1 - i, 0)),
                  pl.BlockSpec((2, tm, DR), lambda i: (0, nt - 1 - i, 0)), _resident((4, PG, PG)), _resident((1, DP)), _resident((DP, D)), _resident((4, DR)), _resident((1, DR)),
                  _resident((NH, HD, HD)), _resident((1, DR)), _resident((NH, HD, HD)), _resident((1, DR)),
                  _resident((1, DR)), _resident((DR, D))],
        out_specs=[rev(DIN), rev(D), rev(D), _resident((MAT_ROWS, HD)), _resident((VEC_ROWS, DR))],
        out_shape=[jax.ShapeDtypeStruct((S, DIN), BF16), jax.ShapeDtypeStruct((S, D), BF16),
                   jax.ShapeDtypeStruct((S, D), BF16), jax.ShapeDtypeStruct((MAT_ROWS, HD), F32),
                   jax.ShapeDtypeStruct((VEC_ROWS, DR), F32)],
        scratch_shapes=[pltpu.VMEM((POOL_HALO, DP), F32), pltpu.VMEM((CONV_HALO, DR), F32), pltpu.VMEM((8, DR), F32),
                        pltpu.VMEM((8, DR), F32), pltpu.VMEM((tm, DR), F32)],
        operands=(proj, proj, dmixo, y_pool, y_rnn, hr, hr, kept, gates, wg, scale, w_pool_out, conv_w, conv_b, wa, ba, wx, bx, lam,
                  w_rnn_out),
        exchange=exchange, exchange_operands=exchange_operands)


def _in_bwd(dproj, x, dx2, norm_mix, w_in, exchange=None, exchange_operands=(), tm=512):
    S = x.shape[0]

    def body(dp_ref, x_ref, dx2_ref, g_ref, w_ref, dx_ref, dg_ref):
        i = pl.program_id(0)

        @pl.when(i == 0)
        def _():
            dg_ref[...] = jnp.zeros_like(dg_ref)

        dh = _dot(dp_ref[:, 0:1536], w_ref[0:1536, :])
        dh = dh + _dot(dp_ref[:, 1536:3072], w_ref[1536:3072, :])
        dh = dh + _dot(dp_ref[:, 3072:DIN], w_ref[3072:DIN, :])
        xv = x_ref[...]
        r = lax.rsqrt(jnp.mean(xv * xv, axis=-1, keepdims=True) + EPS)
        xh = xv * r
        dg_ref[0:1, :] += jnp.sum(dh * xh, axis=0, keepdims=True)
        dxh = dh * g_ref[...]
        dx_ref[...] = dx2_ref[...] + r * (dxh - xh * jnp.mean(dxh * xh, axis=-1, keepdims=True))

    return _call(
        body, "in_bwd", (S // tm,),
        in_specs=[_rows(DIN, tm), _rows(D, tm), _rows(D, tm), _resident((1, D)), _resident((DIN, D))],
        out_specs=[_rows(D, tm), _resident((8, D))],
        out_shape=[jax.ShapeDtypeStruct((S, D), F32), jax.ShapeDtypeStruct((8, D), F32)],
        operands=(dproj, x, dx2, norm_mix, w_in), exchange=exchange, exchange_operands=exchange_operands)


def _wgrad(a, b, name, tk, tn, exchange=None, exchange_operands=()):
    S, K = a.shape
    N = b.shape[1]

    def body(a_ref, b_ref, o_ref):
        o_ref[...] = _dot_tn(a_ref[...], b_ref[...]).astype(BF16)

    (out,), exchanged = _call(
        body, name, (K // tk, N // tn),
        in_specs=[pl.BlockSpec((S, tk), lambda k, n: (0, k)), pl.BlockSpec((S, tn), lambda k, n: (0, n))],
        out_specs=[pl.BlockSpec((tk, tn), lambda k, n: (k, n))],
        out_shape=[jax.ShapeDtypeStruct((K, N), BF16)],
        operands=(a, b), exchange=exchange, exchange_operands=exchange_operands)
    return (out, exchanged) if exchange is not None else out


VEC_SCALE, VEC_CONV_B, VEC_BA, VEC_BX, VEC_LAM, VEC_CONV_W, VEC_NORM_FINAL, VEC_NORM_FFN = 0, 1, 2, 3, 4, 5, 9, 10
VEC_LOSS = 11


class _Big:
    def __init__(self, name, rows, cols, axis, n, dtype=BF16, transposed=False, src_cols=None):
        self.name, self.rows, self.cols, self.axis, self.n, self.dtype = name, rows, cols, axis, n, dtype
        self.transposed = transposed
        self.src_cols = src_cols
        self.block_shape = (rows, n) if axis == 1 else (n, cols)

    def block(self, ref, p):
        if self.axis == 1:
            return ref.at[:, pl.ds(pl.multiple_of(p * self.n, 128), self.n)]
        return ref.at[pl.ds(pl.multiple_of(p * self.n, 16 if self.dtype == BF16 else 8), self.n), :]


BIG = (_Big("w_in", DIN, D, 0, DIN // 8, transposed=True), _Big("w_pool_out", DP, D, 1, D // 8),
       _Big("w_rnn_out", DR, D, 0, DR // 8), _Big("w_o", D, D, 0, D // 8),
       _Big("w_ffn_in", 2 * DFF, D, 0, 2 * DFF // 8, transposed=True), _Big("w_ffn_out", DFF, D, 0, DFF // 8))
CONV_W = _Big("conv_w", 8, DR, 1, DR // 8, F32)
W_FFN_IN_HALVES = (_Big("w_ffn_in_lo", 2 * DFF, D // 2, 0, 2 * DFF // 8, src_cols=(0, D // 2)),
                   _Big("w_ffn_in_hi", 2 * DFF, D // 2, 0, 2 * DFF // 8, src_cols=(D // 2, D)))
GATHERED = BIG + (CONV_W,) + W_FFN_IN_HALVES

HBM_SPEC = pl.BlockSpec(memory_space=pl.ANY)
VMEM_SPEC = pl.BlockSpec(memory_space=pltpu.VMEM)


def _place():
    x, y, c = (lax.axis_index(a) for a in MESH_AXES)
    other_chips = [(1 - x, y), (x, 1 - y), (1 - x, 1 - y)]
    return x, y, c, other_chips


def _remote(src, dst, send_sems, recv_sems, idx, to):
    return pltpu.make_async_remote_copy(src_ref=src, dst_ref=dst, send_sem=send_sems.at[idx], recv_sem=recv_sems.at[idx],
                                        device_id=to, device_id_type=MESH)


def _device_index(chip, core):
    return 4 * chip[0] + 2 * chip[1] + core


class _Gather:
    def __init__(self, tensors):
        self.tensors = tuple(tensors)
        n = len(self.tensors)
        self.in_specs = [HBM_SPEC] * n
        self.out_specs = [HBM_SPEC] * n
        self.out_shape = [jax.ShapeDtypeStruct((T.rows, T.cols), T.dtype) for T in self.tensors]
        self.scratch_shapes = [pltpu.VMEM(T.block_shape, T.dtype) for T in self.tensors] + [
            pltpu.VMEM(T.block_shape, F32) for T in self.tensors] + [
            pltpu.SemaphoreType.DMA((n, 7)), pltpu.SemaphoreType.DMA((n, 7)), pltpu.SemaphoreType.DMA((n, 2))]

    collective_id = 1

    def peers(self):
        x, y, c, _ = _place()
        return [(x, y, 1 - c), (1 - x, y, c), (x, 1 - y, c)]

    def middles(self, steps):
        return [(steps // 2, self.relay), (steps - 1, self.middle)]

    def _copies(self, ins, outs, scratch):
        n = len(self.tensors)
        mine, raw, (send_sems, recv_sems, loc_sems) = scratch[:n], scratch[n:2 * n], scratch[2 * n:]
        x, y, c, chips = _place()
        sibling = (x, y, 1 - c)
        me = _device_index((x, y), c)
        relay_from = (jnp.where(c == 0, 1 - x, x), jnp.where(c == 0, y, 1 - y))
        relay_to = (jnp.where(c == 0, x, 1 - x), jnp.where(c == 0, 1 - y, y))
        loads, stores, first, relays, passed, arrivals, late = [], [], [], [], [], [], []
        for t, T in enumerate(self.tensors):
            place = T.block(outs[t], me)
            src = ins[t] if T.src_cols is None else ins[t].at[:, T.src_cols[0]:T.src_cols[1]]
            loads.append(pltpu.make_async_copy(src, raw[t], loc_sems.at[t, 0]))
            stores.append(pltpu.make_async_copy(mine[t], place, loc_sems.at[t, 1]))
            first.append(_remote(mine[t], place, send_sems, recv_sems, (t, 0), sibling))
            theirs = T.block(outs[t], _device_index((x, y), 1 - c))
            late.append(_remote(theirs, theirs, send_sems, recv_sems, (t, 0), sibling))
            relayed = T.block(outs[t], _device_index(relay_from, c))
            relays.append(_remote(relayed, relayed, send_sems, recv_sems, (t, 3), (*relay_to, c)))
            for k, chip in enumerate(chips):
                if k < 2:
                    first.append(_remote(mine[t], place, send_sems, recv_sems, (t, 1 + k), (*chip, c)))
                land = T.block(outs[t], _device_index(chip, c))
                arrivals.append(_remote(land, land, send_sems, recv_sems, (t, 1 + k), sibling))
                passed.append(_remote(land, land, send_sems, recv_sems, (t, 4 + k), sibling))
                theirs = T.block(outs[t], _device_index(chip, 1 - c))
                late.append(_remote(theirs, theirs, send_sems, recv_sems, (t, 4 + k), sibling))
        return loads, stores, first, relays, passed, arrivals, late

    def start(self, ins, outs, scratch):
        loads, stores, first, _, _, _, _ = self._copies(ins, outs, scratch)
        n = len(self.tensors)
        for cp in loads:
            cp.start()
        for t, cp in enumerate(loads):
            cp.wait()
            scratch[t][...] = scratch[n + t][...].astype(self.tensors[t].dtype)
        for cp in stores + first:
            cp.start()

    def relay(self, ins, outs, scratch, skip=0):
        _, _, _, relays, passed, arrivals, _ = self._copies(ins, outs, scratch)
        for t in range(skip, len(self.tensors)):
            arrivals[3 * t].wait_recv()
            arrivals[3 * t + 1].wait_recv()
            for cp in (relays[t], passed[3 * t], passed[3 * t + 1]):
                cp.start()

    def middle(self, ins, outs, scratch, skip=0):
        _, _, _, _, passed, arrivals, _ = self._copies(ins, outs, scratch)
        for t in range(skip, len(self.tensors)):
            arrivals[3 * t + 2].wait_recv()
            passed[3 * t + 2].start()

    def finish(self, ins, outs, scratch, skip=0):
        _, stores, first, relays, passed, _, late = self._copies(ins, outs, scratch)
        for cp in late[4 * skip:]:
            cp.wait_recv()
        for cp in first + relays + passed:
            cp.wait_send()
        for cp in stores[skip:]:
            cp.wait()


def _in_proj_gather(x, norm_mix, blocks, tensors, order, tm=1024):
    S = x.shape[0]
    nt = S // tm
    n = len(tensors)
    gather = _Gather(tensors)
    CB = 2 * tensors[0].n

    def body(order_ref, x_ref, g_ref, *refs):
        ins, (proj_ref, h_ref), outs = refs[:n], refs[n:n + 2], refs[n + 2:2 * n + 2]
        (h_all, w_chip, w_sem), scratch = refs[2 * n + 2:2 * n + 5], refs[2 * n + 5:]
        q, i = pl.program_id(0), pl.program_id(1)
        _, stores, _, relays, passed, arrivals, late = gather._copies(ins, outs, scratch)

        def fetch(turn):
            rows = outs[0].at[pl.ds(pl.multiple_of(order_ref[turn] * CB, 16), CB), :]
            cp = pltpu.make_async_copy(rows, w_chip, w_sem)
            cp.start()
            cp.wait()

        @pl.when((q == 0) & (i == 0))
        def _():
            _enter(gather)
            gather.start(ins, outs, scratch)
            late[0].wait_recv()
            stores[0].wait()
            fetch(0)

        @pl.when((q == 1) & (i == 0))
        def _():
            arrivals[0].wait_recv()
            arrivals[1].wait_recv()
            for cp in (relays[0], passed[0], passed[1]):
                cp.start()
            late[1].wait_recv()
            fetch(1)

        @pl.when((q == 2) & (i == 0))
        def _():
            late[2].wait_recv()
            fetch(2)
            gather.relay(ins, outs, scratch, skip=1)

        @pl.when((q == 3) & (i == 0))
        def _():
            arrivals[2].wait_recv()
            passed[2].start()
            late[3].wait_recv()
            fetch(3)

        rows = pl.ds(pl.multiple_of(i * tm, tm), tm)

        @pl.when(q == 0)
        def _():
            xv = x_ref[...]
            r = lax.rsqrt(jnp.mean(xv * xv, axis=-1, keepdims=True) + EPS)
            h = (xv * r * g_ref[...]).astype(BF16)
            h_all[rows, :] = h
            h_ref[...] = h

        proj_ref[...] = _dot_nt(h_all[rows, :], w_chip[...])

        @pl.when((q == 3) & (i == nt - 1))
        def _():
            gather.middle(ins, outs, scratch, skip=1)
            gather.finish(ins, outs, scratch, skip=1)

    row_tile = lambda q, i, order: (jnp.where(q == 0, i, nt - 1), 0)
    whole = lambda shape: pl.BlockSpec(shape, lambda q, i, order: (0,) * len(shape), pipeline_mode=pl.Buffered(1))
    outs = pl.pallas_call(
        body, name="in_proj_gather",
        grid_spec=pltpu.PrefetchScalarGridSpec(
            num_scalar_prefetch=1, grid=(4, nt),
            in_specs=[pl.BlockSpec((tm, D), row_tile), whole((1, D))] + gather.in_specs,
            out_specs=[pl.BlockSpec((tm, CB), lambda q, i, order: (i, order[q])), pl.BlockSpec((tm, D), row_tile)]
            + gather.out_specs,
            scratch_shapes=[pltpu.VMEM((S, D), BF16), pltpu.VMEM((CB, D), BF16), pltpu.SemaphoreType.DMA]
            + gather.scratch_shapes),
        out_shape=[jax.ShapeDtypeStruct((S, DIN), F32), jax.ShapeDtypeStruct((S, D), BF16)] + gather.out_shape,
        compiler_params=pltpu.CompilerParams(dimension_semantics=("arbitrary", "arbitrary"), vmem_limit_bytes=VMEM_LIMIT,
                                             collective_id=gather.collective_id),
    )(order, x, norm_mix, *blocks)
    return outs[:2], outs[2:]


PAIR_ROWS = 32


class _PairReduce:
    collective_id = 3

    def __init__(self, tensors):
        self.tensors = tuple(tensors)
        nt = len(self.tensors)
        blocks = [T.block_shape for T in self.tensors]
        self.in_specs = [HBM_SPEC] * nt
        self.out_specs = [HBM_SPEC] * (2 * nt)
        self.out_shape = ([jax.ShapeDtypeStruct(b, BF16) for b in blocks]
                          + [jax.ShapeDtypeStruct((3,) + b, BF16) for b in blocks])
        self.scratch_shapes = ([pltpu.VMEM((4,) + b, BF16) for b in blocks] + [pltpu.VMEM((3,) + b, BF16) for b in blocks]
                               + [pltpu.SemaphoreType.DMA((nt, 4)), pltpu.SemaphoreType.DMA((nt, 4)),
                                  pltpu.SemaphoreType.DMA((nt, 5))])

    def peers(self):
        x, y, c, _ = _place()
        return [(x, y, 1 - c)]

    def middles(self, steps):
        return []

    def _copies(self, ins, outs, scratch):
        nt = len(self.tensors)
        own_out, sums_out, landed, mine = outs[:nt], outs[nt:], scratch[:nt], scratch[nt:2 * nt]
        send_sems, recv_sems, loc_sems = scratch[2 * nt:]
        x, y, c, chips = _place()
        chip_of = [2 * chip[0] + chip[1] for chip in chips]
        swaps, loads, stores = [], [], []
        for t, T in enumerate(self.tensors):
            for j in range(4):
                swaps.append(_remote(T.block(ins[t], 2 * j + 1 - c), landed[t].at[j], send_sems, recv_sems, (t, j),
                                     (x, y, 1 - c)))
            for k in range(3):
                loads.append(pltpu.make_async_copy(T.block(ins[t], 2 * chip_of[k] + c), mine[t].at[k], loc_sems.at[t, k]))
            stores.append(pltpu.make_async_copy(mine[t], sums_out[t], loc_sems.at[t, 3]))
            stores.append(pltpu.make_async_copy(landed[t].at[2 * x + y], own_out[t], loc_sems.at[t, 4]))
        return swaps, loads, stores, landed, mine, chip_of

    def start(self, ins, outs, scratch):
        swaps, loads, _, _, _, _ = self._copies(ins, outs, scratch)
        for cp in swaps + loads:
            cp.start()

    def finish(self, ins, outs, scratch):
        swaps, loads, stores, landed, mine, chip_of = self._copies(ins, outs, scratch)
        for cp in loads:
            cp.wait()
        for cp in swaps:
            cp.wait_recv()
        for t, T in enumerate(self.tensors):
            for k in range(3):
                acc, got = mine[t].at[k], landed[t].at[chip_of[k]]

                def add(i, carry, acc=acc, got=got):
                    rows = pl.ds(pl.multiple_of(i * PAIR_ROWS, PAIR_ROWS), PAIR_ROWS)
                    acc[rows, :] = (acc[rows, :].astype(F32) + got[rows, :].astype(F32)).astype(BF16)
                    return carry

                lax.fori_loop(0, T.block_shape[0] // PAIR_ROWS, add, 0)
        for cp in stores:
            cp.start()
        for cp in swaps:
            cp.wait_send()
        for cp in stores:
            cp.wait()


def _pair_reduce(grads, tensors, name):
    reduce = _PairReduce(tensors)
    nt = len(reduce.tensors)

    def body(*refs):
        ins, outs, scratch = refs[:nt], refs[nt:3 * nt], refs[3 * nt:]
        _enter(reduce)
        reduce.start(ins, outs, scratch)
        reduce.finish(ins, outs, scratch)

    return pl.pallas_call(
        body, name=name, in_specs=reduce.in_specs, out_specs=reduce.out_specs, out_shape=reduce.out_shape,
        scratch_shapes=reduce.scratch_shapes,
        compiler_params=pltpu.CompilerParams(vmem_limit_bytes=VMEM_LIMIT, collective_id=reduce.collective_id),
    )(*grads)


class _Scatter:
    def __init__(self, tensors):
        self.tensors = tuple(tensors)
        n = len(self.tensors)
        self.in_specs = [HBM_SPEC] * n
        self.out_specs = [HBM_SPEC] * n
        self.out_shape = [jax.ShapeDtypeStruct((2,) + T.block_shape, BF16) for T in self.tensors]
        self.scratch_shapes = [pltpu.VMEM(T.block_shape, BF16) for T in self.tensors] * 2 + [
            pltpu.SemaphoreType.DMA((n, 3)), pltpu.SemaphoreType.DMA((n, 3)), pltpu.SemaphoreType.DMA((n,))]

    collective_id = 2

    def peers(self):
        x, y, c, _ = _place()
        return [(1 - x, y, c), (x, 1 - y, c)]

    def middles(self, steps):
        return [(steps // 2, self.middle)]

    def _copies(self, ins, outs, scratch):
        n = len(self.tensors)
        landed, mine, (send_sems, recv_sems, loc_sems) = scratch[:n], scratch[n:2 * n], scratch[2 * n:]
        x, y, c, _ = _place()
        direct = (jnp.where(c == 0, 1 - x, x), jnp.where(c == 0, y, 1 - y), c)
        other = (jnp.where(c == 0, x, 1 - x), jnp.where(c == 0, 1 - y, y), c)
        k_direct = jnp.where(c == 0, 0, 1)
        to_direct, legs, loads, combined, arrivals = [], [], [], [], []
        for t in range(n):
            to_direct.append(_remote(ins[t].at[k_direct], outs[t].at[0], send_sems, recv_sems, (t, 0), direct))
            legs.append(_remote(ins[t].at[2], landed[t], send_sems, recv_sems, (t, 2), direct))
            loads.append(pltpu.make_async_copy(ins[t].at[1 - k_direct], mine[t], loc_sems.at[t]))
            combined.append(_remote(mine[t], outs[t].at[1], send_sems, recv_sems, (t, 1), other))
            arrivals.append(_remote(landed[t], landed[t], send_sems, recv_sems, (t, 2), direct))
        return to_direct, legs, loads, combined, arrivals, landed, mine

    def start(self, ins, outs, scratch):
        to_direct, legs, loads, _, _, _, _ = self._copies(ins, outs, scratch)
        for cp in to_direct + legs + loads:
            cp.start()

    def middle(self, ins, outs, scratch):
        _, _, loads, combined, arrivals, landed, mine = self._copies(ins, outs, scratch)
        for t, T in enumerate(self.tensors):
            loads[t].wait()
            arrivals[t].wait_recv()
            acc, got = mine[t], landed[t]

            def add(i, carry, acc=acc, got=got):
                rows = pl.ds(pl.multiple_of(i * PAIR_ROWS, PAIR_ROWS), PAIR_ROWS)
                acc[rows, :] = (acc[rows, :].astype(F32) + got[rows, :].astype(F32)).astype(BF16)
                return carry

            lax.fori_loop(0, T.block_shape[0] // PAIR_ROWS, add, 0)
            combined[t].start()

    def finish(self, ins, outs, scratch):
        to_direct, legs, _, combined, _, _, _ = self._copies(ins, outs, scratch)
        for cp in to_direct + combined:
            cp.wait()
        for cp in legs:
            cp.wait_send()


class _ReduceScatter:
    collective_id = 1

    def __init__(self, tensors):
        self.first, self.second = _PairReduce(tensors), _Scatter(tensors)
        self.in_specs = self.first.in_specs
        self.out_specs = self.first.out_specs + self.second.out_specs
        self.out_shape = self.first.out_shape + self.second.out_shape
        self.scratch_shapes = self.first.scratch_shapes + self.second.scratch_shapes

    def peers(self):
        x, y, c, _ = _place()
        return [(x, y, 1 - c), (1 - x, y, c), (x, 1 - y, c)]

    def _parts(self, ins, outs, scratch):
        n_out, n_scr = len(self.first.out_specs), len(self.first.scratch_shapes)
        pair_sums = list(outs[n_out // 2:n_out])
        return (ins, outs[:n_out], scratch[:n_scr]), (pair_sums, outs[n_out:], scratch[n_scr:])

    def start(self, ins, outs, scratch):
        self.first.start(*self._parts(ins, outs, scratch)[0])

    def middles(self, steps):
        def hand_over(ins, outs, scratch):
            first, second = self._parts(ins, outs, scratch)
            self.first.finish(*first)
            self.second.start(*second)

        def relay(ins, outs, scratch):
            self.second.middle(*self._parts(ins, outs, scratch)[1])

        return [(steps // 4, hand_over), ((5 * steps) // 8, relay)]

    def finish(self, ins, outs, scratch):
        self.second.finish(*self._parts(ins, outs, scratch)[1])


def _adamw(w, g, m, v):
    m = ADAM_B1 * m + (1.0 - ADAM_B1) * g
    v = ADAM_B2 * v + (1.0 - ADAM_B2) * (g * g)
    m_hat = m / (1.0 - ADAM_B1 ** ADAM_STEP)
    v_hat = v / (1.0 - ADAM_B2 ** ADAM_STEP)
    delta = -ADAM_LR * (m_hat / (jnp.sqrt(v_hat) + ADAM_EPS) + ADAM_WD * w)
    return delta, m, v


def _final_sum(T, g, lz1, lz2, where, w, m, v):
    rows, cols = T.block_shape
    sub = 4 if T.axis == 0 and rows % 64 == 0 and rows > 256 else 1
    blk = (rows // sub, cols)

    def body(where_ref, g_ref, l1_ref, l2_ref, w_ref, m_ref, v_ref, g_out, d_out, m_out, v_out):
        tot = g_ref[...].astype(F32) + l1_ref[...].astype(F32)
        for k in range(2):
            tot = tot + l2_ref[k].astype(F32)
        g_out[...] = tot
        d_out[...], m_out[...], v_out[...] = _adamw(w_ref[...], tot, m_ref[...], v_ref[...])

    def in_whole(r, wh):
        p = wh[0]
        return (0, p) if T.axis == 1 else (p * sub + r, 0)

    own = pl.BlockSpec(blk, lambda r, wh: (r, 0))
    return pl.pallas_call(
        body, name="grad_final_" + T.name,
        grid_spec=pltpu.PrefetchScalarGridSpec(
            num_scalar_prefetch=1, grid=(sub,),
            in_specs=[pl.BlockSpec(blk, in_whole),
                      own,
                      pl.BlockSpec((2,) + blk, lambda r, wh: (0, r, 0)), own, own, own],
            out_specs=[own] * 4),
        out_shape=[jax.ShapeDtypeStruct(T.block_shape, F32)] * 4,
        compiler_params=_params("arbitrary"),
    )(where, g, lz1, lz2, w, m, v)


VEC_PIECE = DR // 8


class _AllReduce:
    def __init__(self, items):
        self.items = tuple(items)
        n = len(self.items)
        self.in_specs = [HBM_SPEC] * n
        self.out_specs = [HBM_SPEC] * n
        self.out_shape = [jax.ShapeDtypeStruct(shape, F32) for shape, _ in self.items]
        pieces = [(shape[0] // 8, shape[1]) if axis == 0 else (shape[0], shape[1] // 8) for shape, axis in self.items]
        self.scratch_shapes = ([pltpu.VMEM((8,) + p, F32) for p in pieces] + [pltpu.VMEM(p, F32) for p in pieces] + [
            pltpu.SemaphoreType.DMA((2 * n, 8)), pltpu.SemaphoreType.DMA((2 * n, 8)), pltpu.SemaphoreType.DMA((2 * n,))])

    collective_id = None

    def peers(self):
        return []

    def middles(self, steps):
        return [(steps // 2, self.middle)]

    def _copies(self, ins, outs, scratch):
        n = len(self.items)
        landed, sums, (send_sems, recv_sems, loc_sems) = scratch[:n], scratch[n:2 * n], scratch[2 * n:]
        x, y, c, _ = _place()
        me = _device_index((x, y), c)

        def peer(r):
            return (1 - x if r & 4 else x, 1 - y if r & 2 else y, 1 - c if r & 1 else c)

        def piece(i, ref, p):
            shape, axis = self.items[i]
            if axis == 0:
                rows = shape[0] // 8
                return ref.at[pl.ds(pl.multiple_of(p * rows, 8), rows), :]
            cols = shape[1] // 8
            return ref.at[:, pl.ds(pl.multiple_of(p * cols, 128), cols)]

        own, scatter, arrivals, keep, spread, late = [], [], [], [], [], []
        for i in range(n):
            own.append(pltpu.make_async_copy(piece(i, ins[i], me), landed[i].at[0], loc_sems.at[2 * i]))
            keep.append(pltpu.make_async_copy(sums[i], piece(i, outs[i], me), loc_sems.at[2 * i + 1]))
            for r in range(1, 8):
                to = peer(r)
                p = _device_index(to[:2], to[2])
                scatter.append(_remote(piece(i, ins[i], p), landed[i].at[r], send_sems, recv_sems, (2 * i, r), to))
                spread.append(_remote(sums[i], piece(i, outs[i], me), send_sems, recv_sems, (2 * i + 1, r), to))
                late.append(_remote(sums[i], piece(i, outs[i], p), send_sems, recv_sems, (2 * i + 1, r), to))
        return own, scatter, keep, spread, late, landed, sums

    def start(self, ins, outs, scratch):
        own, scatter, _, _, _, _, _ = self._copies(ins, outs, scratch)
        for cp in own + scatter:
            cp.start()

    def middle(self, ins, outs, scratch):
        own, scatter, keep, spread, _, landed, sums = self._copies(ins, outs, scratch)
        for cp in own:
            cp.wait()
        for cp in scatter:
            cp.wait_recv()
        for i in range(len(self.items)):
            total = landed[i][0]
            for r in range(1, 8):
                total = total + landed[i][r]
            sums[i][...] = total
        for cp in keep + spread:
            cp.start()

    def finish(self, ins, outs, scratch):
        _, scatter, keep, spread, late, _, _ = self._copies(ins, outs, scratch)
        for cp in late:
            cp.wait_recv()
        for cp in scatter + spread:
            cp.wait_send()
        for cp in keep:
            cp.wait()


class _Both:
    def __init__(self, a, b):
        self.a, self.b = a, b
        self.in_specs, self.out_specs = a.in_specs + b.in_specs, a.out_specs + b.out_specs
        self.out_shape, self.scratch_shapes = a.out_shape + b.out_shape, a.scratch_shapes + b.scratch_shapes

    collective_id = None

    def peers(self):
        return []

    def _each(self, ins, outs, scratch):
        a = self.a
        i, o, s = len(a.in_specs), len(a.out_specs), len(a.scratch_shapes)
        return (a, ins[:i], outs[:o], scratch[:s]), (self.b, ins[i:], outs[o:], scratch[s:])

    def middles(self, steps):
        def of(which, middle):
            return lambda ins, outs, scratch: middle(*self._each(ins, outs, scratch)[which][1:])
        return [(at, of(which, middle)) for which, e in enumerate((self.a, self.b)) for at, middle in e.middles(steps)]

    def start(self, ins, outs, scratch):
        for e, i, o, s in self._each(ins, outs, scratch):
            e.start(i, o, s)

    def finish(self, ins, outs, scratch):
        for e, i, o, s in self._each(ins, outs, scratch):
            e.finish(i, o, s)


def _all_reduce(arrays, items, name):
    reduce = _AllReduce(items)
    n = len(items)

    def body(*refs):
        ins, outs, scratch = refs[:n], refs[n:2 * n], refs[2 * n:]
        reduce.start(ins, outs, scratch)
        reduce.middle(ins, outs, scratch)
        reduce.finish(ins, outs, scratch)

    return pl.pallas_call(
        body, name=name, in_specs=reduce.in_specs, out_specs=reduce.out_specs, out_shape=reduce.out_shape,
        scratch_shapes=reduce.scratch_shapes,
    )(*arrays)


def _adam_small(grads, wmv):
    n = len(grads)

    def body(*refs):
        g_refs, rest = refs[:n], refs[n:]
        ins, outs = rest[:3 * n], rest[3 * n:]
        for i in range(n):
            d, m, v = _adamw(ins[3 * i][...], g_refs[i][...], ins[3 * i + 1][...], ins[3 * i + 2][...])
            outs[3 * i][...], outs[3 * i + 1][...], outs[3 * i + 2][...] = d, m, v

    flat = [a for t in wmv for a in t]
    return pl.pallas_call(
        body, name="adam_small",
        in_specs=[VMEM_SPEC] * (4 * n), out_specs=[VMEM_SPEC] * (3 * n),
        out_shape=[jax.ShapeDtypeStruct(a.shape, F32) for a in flat],
    )(*grads, *flat)


WEIGHT_NAMES = ("norm_mix", "w_in", "w_pool_grp", "pool_scale", "w_pool_out", "conv_w", "conv_b", "w_rg_a", "b_rg_a", "w_rg_x",
                "b_rg_x", "lru_lambda", "w_rnn_out", "w_o", "norm_ffn", "w_ffn_in", "w_ffn_out", "norm_final")


def kernel(x, norm_mix, w_in, w_pool_grp, pool_scale, w_pool_out, conv_w, conv_b, w_rg_a, b_rg_a, w_rg_x, b_rg_x, lru_lambda, w_rnn_out, w_o, norm_ffn, w_ffn_in, w_ffn_out, norm_final, loss_target, m_norm_mix, m_w_in, m_w_pool_grp, m_pool_scale, m_w_pool_out, m_conv_w, m_conv_b, m_w_rg_a, m_b_rg_a, m_w_rg_x, m_b_rg_x, m_lru_lambda, m_w_rnn_out, m_w_o, m_norm_ffn, m_w_ffn_in, m_w_ffn_out, m_norm_final, v_norm_mix, v_w_in, v_w_pool_grp, v_pool_scale, v_w_pool_out, v_conv_w, v_conv_b, v_w_rg_a, v_b_rg_a, v_w_rg_x, v_b_rg_x, v_lru_lambda, v_w_rnn_out, v_w_o, v_norm_ffn, v_w_ffn_in, v_w_ffn_out, v_norm_final):
    w = dict(norm_mix=norm_mix, w_in=w_in, w_pool_grp=w_pool_grp, pool_scale=pool_scale, w_pool_out=w_pool_out, conv_w=conv_w,
             conv_b=conv_b, w_rg_a=w_rg_a, b_rg_a=b_rg_a, w_rg_x=w_rg_x, b_rg_x=b_rg_x, lru_lambda=lru_lambda,
             w_rnn_out=w_rnn_out, w_o=w_o, norm_ffn=norm_ffn, w_ffn_in=w_ffn_in, w_ffn_out=w_ffn_out, norm_final=norm_final)
    m = dict(norm_mix=m_norm_mix, w_in=m_w_in, w_pool_grp=m_w_pool_grp, pool_scale=m_pool_scale, w_pool_out=m_w_pool_out,
             conv_w=m_conv_w, conv_b=m_conv_b, w_rg_a=m_w_rg_a, b_rg_a=m_b_rg_a, w_rg_x=m_w_rg_x, b_rg_x=m_b_rg_x,
             lru_lambda=m_lru_lambda, w_rnn_out=m_w_rnn_out, w_o=m_w_o, norm_ffn=m_norm_ffn, w_ffn_in=m_w_ffn_in,
             w_ffn_out=m_w_ffn_out, norm_final=m_norm_final)
    v = dict(norm_mix=v_norm_mix, w_in=v_w_in, w_pool_grp=v_w_pool_grp, pool_scale=v_pool_scale, w_pool_out=v_w_pool_out,
             conv_w=v_conv_w, conv_b=v_conv_b, w_rg_a=v_w_rg_a, b_rg_a=v_b_rg_a, w_rg_x=v_w_rg_x, b_rg_x=v_b_rg_x,
             lru_lambda=v_lru_lambda, w_rnn_out=v_w_rnn_out, w_o=v_w_o, norm_ffn=v_norm_ffn, w_ffn_in=v_w_ffn_in,
             w_ffn_out=v_w_ffn_out, norm_final=v_norm_final)
    xi, yi, ci = (lax.axis_index(a) for a in MESH_AXES)
    chip = 2 * xi + yi

    def held(T, a):
        return jnp.swapaxes(a, 0, 1) if T.transposed else a

    where = jnp.stack([2 * chip + ci]).astype(jnp.int32)
    by_name = {T.name: T for T in GATHERED}
    block = {T.name: held(T, w[T.name][0]) for T in BIG}
    block["conv_w"] = jnp.pad(conv_w[0], ((0, CONV_W.rows - 4), (0, 0)))
    block["w_ffn_in_lo"] = block["w_ffn_in_hi"] = block["w_ffn_in"]

    def gather_of(*names):
        return dict(exchange=_Gather([by_name[n] for n in names]), exchange_operands=[block[n] for n in names])

    def pair_sums(names, partials, tag):
        out = _pair_reduce(partials, [by_name[n] for n in names], "grad_pair_reduce_" + tag)
        return list(out[:len(names)]), list(out[len(names):])

    xs, target = x[0], loss_target[0]
    wg_b, wa_b, wx_b = (a[0].astype(BF16) for a in (w_pool_grp, w_rg_a, w_rg_x))
    ba2, bx2 = b_rg_a.reshape(1, DR), b_rg_x.reshape(1, DR)
    first = ("w_in", "w_pool_out", "w_rnn_out", "conv_w", "w_o")
    order = jnp.stack([chip, 2 * (1 - xi) + yi, 2 * xi + (1 - yi), 2 * (1 - xi) + (1 - yi)]).astype(jnp.int32)
    (proj, h1), (w_in_g, w_pool_out_g, w_rnn_out_g, conv_g, w_o_g) = _in_proj_gather(
        xs, norm_mix, [block[n] for n in first], [by_name[n] for n in first], order)
    mixer_weights = (wg_b, pool_scale, w_pool_out_g, conv_g[0:4], conv_b, wa_b, ba2, wx_b, bx2, lru_lambda, w_rnn_out_g)
    (pm, y_pool, hr, z, y_rnn, kept, gates), (w_ffn_lo_g, w_ffn_hi_g) = _mixer_fwd(
        proj, *mixer_weights, **gather_of("w_ffn_in_lo", "w_ffn_in_hi"))
    (mix, x2, h2), _ = _merge_out(xs, proj, y_pool, y_rnn, w_o_g, norm_ffn)
    (gu, act), (w_ffn_out_g,) = _ffn_up(h2, w_ffn_lo_g, w_ffn_hi_g, **gather_of("w_ffn_out"))
    dx3, dx3b, loss_part, dvec_fin = _ffn_down_loss(act, x2, target, w_ffn_out_g, norm_final.reshape(1, D))

    dgu = _ffn_bwd_down(dx3b, gu, w_ffn_out_g)
    dx2, dx2b, dmixo, dvec_ffn = _ffn_bwd_up(dgu, x2, dx3, w_ffn_lo_g, w_ffn_hi_g, norm_ffn, w_o_g)
    names_a = ("w_ffn_in", "w_ffn_out", "w_o")
    g_ffn_out = _wgrad(act, dx3b, "wgrad_ffn_out", 1408, 512)
    g_ffn_in, (own_ffn_out, sums_ffn_out) = _wgrad(
        dgu, h2, "wgrad_ffn_in", 1408, 512, exchange=_PairReduce([by_name["w_ffn_out"]]), exchange_operands=[g_ffn_out])
    g_o, (own_ffn_in, sums_ffn_in) = _wgrad(
        mix, dx2b, "wgrad_o", 1024, 256, exchange=_PairReduce([by_name["w_ffn_in"]]), exchange_operands=[g_ffn_in])
    (own_o,), (sums_o,) = pair_sums(("w_o",), [g_o], "o")
    part_a = [g_ffn_in, g_ffn_out, g_o]
    lz1_a, sums_a = [own_ffn_in, own_ffn_out, own_o], [sums_ffn_in, sums_ffn_out, sums_o]
    (dproj, dypb, dyrb, dmat, dvec_mix), lz2_a = _mixer_bwd(
        proj, dmixo, y_pool, y_rnn, hr, kept, gates, *mixer_weights,
        exchange=_Scatter([by_name[n] for n in names_a]), exchange_operands=sums_a)
    names_b = ("w_pool_out", "w_rnn_out")
    part_b = [_wgrad(pm, dypb, "wgrad_pool_out", 512, 256), _wgrad(z, dyrb, "wgrad_rnn_out", 1024, 256)]
    lz1_b, sums_b = pair_sums(names_b, part_b, "mix")
    dvec = jnp.concatenate([dvec_mix[0:9], dvec_fin[0:1], dvec_ffn[0:1], jnp.pad(loss_part, ((0, 0), (0, DR - 1))),
                            jnp.zeros((VEC_ROWS - 12, DR), F32)], axis=0)
    g_in, exchanged = _wgrad(
        dproj, h1, "wgrad_in", 1152, 1024,
        exchange=_Both(_Scatter([by_name[n] for n in names_b]), _AllReduce([((MAT_ROWS, HD), 0), ((VEC_ROWS, DR), 1)])),
        exchange_operands=sums_b + [dmat, dvec])
    lz2_b, (mat, vec) = exchanged[:2], exchanged[2:]
    loss = vec[VEC_LOSS, 0]
    (grad_x, dvec_in), (own_in, _, scattered_in) = _in_bwd(
        dproj, xs, dx2, norm_mix, w_in_g, exchange=_ReduceScatter([by_name["w_in"]]), exchange_operands=[g_in])
    lz1_c, lz2_c = [own_in], [scattered_in]
    (vec_in,) = _all_reduce([dvec_in], [((8, D), 1)], "all_reduce_norm_mix")

    grads, delta, new_m, new_v = {}, {}, {}, {}
    for n, g, l1, l2 in zip(names_a + names_b + ("w_in",), part_a + part_b + [g_in], lz1_a + lz1_b + lz1_c,
                            list(lz2_a) + list(lz2_b) + lz2_c):
        T = by_name[n]
        out = _final_sum(T, g, l1, l2, where, held(T, w[n][0]), held(T, m[n][0]), held(T, v[n][0]))
        grads[n], delta[n], new_m[n], new_v[n] = (held(T, a) for a in out)
    me = 4 * xi + 2 * yi + ci
    small_grads = dict(
        w_pool_grp=mat[0:MAT_WA], w_rg_a=mat[MAT_WA:MAT_WX], w_rg_x=mat[MAT_WX:MAT_ROWS],
        pool_scale=vec[VEC_SCALE:VEC_SCALE + 1, 0:DP], conv_b=vec[VEC_CONV_B:VEC_CONV_B + 1],
        b_rg_a=vec[VEC_BA:VEC_BA + 1], b_rg_x=vec[VEC_BX:VEC_BX + 1], lru_lambda=vec[VEC_LAM:VEC_LAM + 1],
        conv_w=lax.dynamic_slice(vec, (VEC_CONV_W, VEC_PIECE * me), (4, VEC_PIECE)),
        norm_final=vec[VEC_NORM_FINAL:VEC_NORM_FINAL + 1], norm_ffn=vec[VEC_NORM_FFN:VEC_NORM_FFN + 1],
        norm_mix=vec_in[0:1])
    names = list(small_grads)
    as2d = lambda a, g: a.reshape(g.shape)
    upd = _adam_small([small_grads[n] for n in names],
                      [(as2d(w[n], small_grads[n]), as2d(m[n], small_grads[n]), as2d(v[n], small_grads[n])) for n in names])
    for i, n in enumerate(names):
        grads[n] = small_grads[n]
        delta[n], new_m[n], new_v[n] = upd[3 * i:3 * i + 3]

    shaped = lambda d: [d[n].reshape(w[n].shape) for n in WEIGHT_NAMES]
    return (loss, grad_x[None], *shaped(grads), *shaped(delta), *shaped(new_m), *shaped(new_v))
```

```python
import math

import jax
import jax.numpy as jnp
from jax import lax
from jax.experimental import pallas as pl
from jax.experimental.pallas import tpu as pltpu

F32 = jnp.float32
BF16 = jnp.bfloat16

D = 1024
DP = 512
PG = 128
WINDOWS = (2, 4, 8, 16)
DR = 1024
NH = 8
HD = 128
DIN = 4608
DFF = 2816
EPS = 1e-6
LRU_C = 8.0
POOL_HALO = 16
CONV_HALO = 8
KEPT = 3

ADAM_LR = 0.001
ADAM_B1 = 0.9
ADAM_B2 = 0.999
ADAM_EPS = 1e-08
ADAM_WD = 0.01
ADAM_STEP = 10

VMEM_LIMIT = 56 * 1024 * 1024
MESH_AXES = ("x", "y", "c")
MESH = pl.DeviceIdType.MESH


def _dot(a, b):
    return jnp.dot(a, b, preferred_element_type=F32)


def _dot_nt(a, b):
    return lax.dot_general(a, b, (((1,), (1,)), ((), ())), preferred_element_type=F32)


def _dot_tn(a, b):
    return lax.dot_general(a, b, (((0,), (0,)), ((), ())), preferred_element_type=F32)


def _params(*sem):
    return pltpu.CompilerParams(dimension_semantics=sem, vmem_limit_bytes=VMEM_LIMIT)


def _resident(shape):
    nd = len(shape)
    return pl.BlockSpec(shape, lambda i: (0,) * nd, pipeline_mode=pl.Buffered(1))


def _rows(shape_cols, tm):
    return pl.BlockSpec((tm, shape_cols), lambda i: (i, 0))


def _call(body, name, grid, in_specs, out_specs, out_shape, operands, scratch_shapes=(), exchange=None, exchange_operands=()):
    n_in, n_out, n_scr = len(in_specs), len(out_specs), len(scratch_shapes)
    steps = math.prod(grid)
    if exchange is None:
        outs = pl.pallas_call(body, name=name, grid=grid, in_specs=in_specs, out_specs=out_specs, out_shape=out_shape,
                              scratch_shapes=list(scratch_shapes), compiler_params=_params(*["arbitrary"] * len(grid)))(*operands)
        return outs, []
    e_in, e_out = len(exchange.in_specs), len(exchange.out_specs)

    def hosted(*refs):
        ins, refs = refs[:n_in], refs[n_in:]
        e_ins, refs = refs[:e_in], refs[e_in:]
        outs, refs = refs[:n_out], refs[n_out:]
        e_outs, refs = refs[:e_out], refs[e_out:]
        scr, e_scr = refs[:n_scr], refs[n_scr:]
        step = pl.program_id(0)
        for axis in range(1, len(grid)):
            step = step * grid[axis] + pl.program_id(axis)
        @pl.when(step == 0)
        def _():
            _enter(exchange)
            exchange.start(e_ins, e_outs, e_scr)

        for at, middle in exchange.middles(steps):
            pl.when(step == at)(lambda middle=middle: middle(e_ins, e_outs, e_scr))
        body(*ins, *outs, *scr)
        pl.when(step == steps - 1)(lambda: exchange.finish(e_ins, e_outs, e_scr))

    outs = pl.pallas_call(
        hosted, name=name, grid=grid, in_specs=list(in_specs) + exchange.in_specs,
        out_specs=list(out_specs) + exchange.out_specs, out_shape=list(out_shape) + exchange.out_shape,
        scratch_shapes=list(scratch_shapes) + exchange.scratch_shapes,
        compiler_params=pltpu.CompilerParams(dimension_semantics=("arbitrary",) * len(grid), vmem_limit_bytes=VMEM_LIMIT,
                                             collective_id=exchange.collective_id))(*operands, *exchange_operands)
    return outs[:n_out], outs[n_out:]


def _enter(exchange):
    peers = exchange.peers()
    if peers:
        barrier = pltpu.get_barrier_semaphore()
        for peer in peers:
            pl.semaphore_signal(barrier, inc=1, device_id=peer, device_id_type=MESH)
        pl.semaphore_wait(barrier, len(peers))


GELU_C = math.sqrt(2.0 / math.pi)
GELU_K = 0.044715 * GELU_C


def _gelu(x, with_grad=False):
    x2 = x * x
    t = jnp.tanh(x * (GELU_C + GELU_K * x2))
    hx = 0.5 * x
    y = hx + hx * t
    if not with_grad:
        return y
    return y, 0.5 + 0.5 * t + hx * (1.0 - t * t) * (GELU_C + (3.0 * GELU_K) * x2)


def _softplus_neg(lam):
    z = jnp.exp(-jnp.abs(lam))
    u = 1.0 + z
    dlt = u - 1.0
    log1p = jnp.where(dlt == 0.0, z, jnp.log(u) * (z / jnp.where(dlt == 0.0, 1.0, dlt)))
    return jnp.maximum(-lam, 0.0) + log1p


def _sigmoid(x):
    return 0.5 * jnp.tanh(0.5 * x) + 0.5


def _linear_scan(out_ref, A, B, h0, reverse):
    n = A.shape[0]
    sub = lax.broadcasted_iota(jnp.int32, (8, 1), 0)
    tiles = range(n // 8 - 1, -1, -1) if reverse else range(n // 8)
    carry = h0
    for j in tiles:
        a, b = A[8 * j:8 * j + 8, :], B[8 * j:8 * j + 8, :]
        for d in (1, 2, 4):
            keep = (sub < 8 - d) if reverse else (sub >= d)
            shift = 8 - d if reverse else d
            b = jnp.where(keep, a * pltpu.roll(b, shift, axis=0) + b, b)
            a = jnp.where(keep, a * pltpu.roll(a, shift, axis=0), a)
        h = a * carry + b
        out_ref[8 * j:8 * j + 8, :] = h
        carry = h[0:1, :] if reverse else h[7:8, :]
    return carry


def _pool_windows(ext, shift_sign):
    n = ext.shape[0]
    s = ext
    outs = []
    for w in WINDOWS:
        d = w // 2
        s = s + pltpu.roll(s, d if shift_sign > 0 else n - d, axis=0)
        outs.append(s[:, :PG])
        s = s[:, PG:]
    return outs


def _conv_taps(uext):
    taps = []
    for k in range(4):
        sh = 3 - k
        v = uext if sh == 0 else pltpu.roll(uext, sh, axis=0)
        taps.append(v[CONV_HALO:, :])
    return taps


def _gates(v, wa_ref, ba_ref, wx_ref, bx_ref, sp):
    vb = v.astype(BF16)
    ra, rx = [], []
    for h in range(NH):
        vh = vb[:, h * HD:(h + 1) * HD]
        ra.append(_dot(vh, wa_ref[h]))
        rx.append(_dot(vh, wx_ref[h]))
    r = _sigmoid(jnp.concatenate(ra, axis=1) + ba_ref[...])
    i = _sigmoid(jnp.concatenate(rx, axis=1) + bx_ref[...])
    log_a = r * ((-LRU_C) * sp)
    a = jnp.exp(log_a)
    one_minus = -jnp.tanh(log_a) * (1.0 + a * a)
    return r, i, a, jnp.sqrt(one_minus), lax.rsqrt(one_minus)


def _mixer_fwd(proj, wg, scale, w_pool_out, conv_w, conv_b, wa, ba, wx, bx, lam, w_rnn_out, exchange=None,
               exchange_operands=(), tm=256):
    S = proj.shape[0]
    UW = DP + 2 * DR

    def body(proj_ref, wg_ref, scale_ref, wpo_ref, cw_ref, cb_ref, wa_ref, ba_ref, wx_ref, bx_ref, lam_ref, wro_ref,
             pm_ref, ypool_ref, hr_ref, z_ref, yrnn_ref, kept_ref, gates_ref, pool_carry, conv_carry, h_carry):
        i = pl.program_id(0)

        @pl.when(i == 0)
        def _():
            pool_carry[...] = jnp.zeros_like(pool_carry)
            conv_carry[...] = jnp.zeros_like(conv_carry)
            h_carry[...] = jnp.zeros_like(h_carry)

        rows = lax.broadcasted_iota(jnp.int32, (tm, 1), 0)
        t_glob = i * tm + rows

        u_pool = proj_ref[:, 0:DP]
        ext = jnp.concatenate([pool_carry[...], u_pool], axis=0)
        pool_carry[...] = u_pool[tm - POOL_HALO:, :]
        sums = _pool_windows(ext, +1)
        mixed = []
        for g, w in enumerate(WINDOWS):
            inv_cnt = 1.0 / jnp.minimum(t_glob + 1, w).astype(F32)
            pooled_g = sums[g][POOL_HALO:, :] * inv_cnt - u_pool[:, g * PG:(g + 1) * PG]
            mixed.append(_dot(pooled_g.astype(BF16), wg_ref[g]))
        pm = (jnp.concatenate(mixed, axis=1) * scale_ref[...]).astype(BF16)
        pm_ref[...] = pm
        ypool_ref[...] = _dot(pm, wpo_ref[...]).astype(BF16)

        u_rnn = proj_ref[:, DP:DP + DR]
        uext = jnp.concatenate([conv_carry[...], u_rnn], axis=0)
        conv_carry[...] = u_rnn[tm - CONV_HALO:, :]
        taps = _conv_taps(uext)
        v = cb_ref[...]
        for k in range(4):
            v = v + taps[k] * cw_ref[k:k + 1, :]
        sp = _softplus_neg(lam_ref[...])
        r, gi, a, mult, _ = _gates(v, wa_ref, ba_ref, wx_ref, bx_ref, sp)
        for k, kept in enumerate((v, a, mult)):
            kept_ref[k] = kept
        for k, kept in enumerate((r, gi)):
            gates_ref[k] = kept.astype(BF16)
        h_carry[0:1, :] = _linear_scan(hr_ref, a, mult * gi * v, h_carry[0:1, :], reverse=False)
        z = (hr_ref[...] * _gelu(proj_ref[:, DP + DR:UW])).astype(BF16)
        z_ref[...] = z
        yrnn_ref[...] = _dot(z, wro_ref[...]).astype(BF16)

    return _call(
        body, "mixer_fwd", (S // tm,),
        in_specs=[_rows(UW, tm), _resident((4, PG, PG)), _resident((1, DP)), _resident((DP, D)), _resident((4, DR)),
                  _resident((1, DR)), _resident((NH, HD, HD)), _resident((1, DR)), _resident((NH, HD, HD)),
                  _resident((1, DR)), _resident((1, DR)), _resident((DR, D))],
        out_specs=[_rows(DP, tm), _rows(D, tm), _rows(DR, tm), _rows(DR, tm), _rows(D, tm),
                   pl.BlockSpec((KEPT, tm, DR), lambda i: (0, i, 0)), pl.BlockSpec((2, tm, DR), lambda i: (0, i, 0))],
        out_shape=[jax.ShapeDtypeStruct((S, DP), BF16),
                   jax.ShapeDtypeStruct((S, D), BF16), jax.ShapeDtypeStruct((S, DR), F32),
                   jax.ShapeDtypeStruct((S, DR), BF16), jax.ShapeDtypeStruct((S, D), BF16),
                   jax.ShapeDtypeStruct((KEPT, S, DR), F32), jax.ShapeDtypeStruct((2, S, DR), BF16)],
        scratch_shapes=[pltpu.VMEM((POOL_HALO, DP), F32), pltpu.VMEM((CONV_HALO, DR), F32), pltpu.VMEM((8, DR), F32)],
        operands=(proj, wg, scale, w_pool_out, conv_w, conv_b, wa, ba, wx, bx, lam, w_rnn_out),
        exchange=exchange, exchange_operands=exchange_operands)


FF_CHUNKS = ((0, 768), (768, 1536), (1536, 2304), (2304, DFF))


def _rms(x):
    r = lax.rsqrt(jnp.mean(x * x, axis=-1, keepdims=True) + EPS)
    return r, x * r


def _rms_bwd(dh, g, r, xh):
    dxh = dh * g
    return r * (dxh - xh * jnp.mean(dxh * xh, axis=-1, keepdims=True))


def _merge_out(x, proj, y_pool, y_rnn, w_o, norm_ffn, exchange=None, exchange_operands=(), tm=512):
    S = x.shape[0]
    GL0 = (DP + 2 * DR) // 512

    def gl_spec(k):
        return pl.BlockSpec((tm, 512), lambda i: (i, GL0 + k))

    def body(x_ref, gl0, gl1, gl2, gl3, yp_ref, yr_ref, wo_ref, gf_ref, mix_ref, x2_ref, h2_ref):
        s_p = _sigmoid(jnp.concatenate([gl0[...], gl1[...]], axis=1))
        s_r = _sigmoid(jnp.concatenate([gl2[...], gl3[...]], axis=1))
        mix = (s_p * yp_ref[...].astype(F32) + s_r * yr_ref[...].astype(F32)).astype(BF16)
        mix_ref[...] = mix
        x2 = x_ref[...] + _dot(mix, wo_ref[...])
        x2_ref[...] = x2
        _, xh2 = _rms(x2)
        h2_ref[...] = (xh2 * gf_ref[...]).astype(BF16)

    return _call(
        body, "merge_out", (S // tm,),
        in_specs=[_rows(D, tm), gl_spec(0), gl_spec(1), gl_spec(2), gl_spec(3), _rows(D, tm), _rows(D, tm),
                  _resident((D, D)), _resident((1, D))],
        out_specs=[_rows(D, tm), _rows(D, tm), _rows(D, tm)],
        out_shape=[jax.ShapeDtypeStruct((S, D), BF16), jax.ShapeDtypeStruct((S, D), F32), jax.ShapeDtypeStruct((S, D), BF16)],
        operands=(x, proj, proj, proj, proj, y_pool, y_rnn, w_o, norm_ffn),
        exchange=exchange, exchange_operands=exchange_operands)


def _ffn_up(h2, w_lo, w_hi, exchange=None, exchange_operands=(), tm=512):
    S = h2.shape[0]
    HALF = D // 2

    def body(h_ref, lo_ref, hi_ref, back_ref, act_ref):
        h_lo, h_hi = h_ref[:, 0:HALF], h_ref[:, HALF:D]
        for c0, c1 in FF_CHUNKS:
            gate = _dot_nt(h_lo, lo_ref[c0:c1, :]) + _dot_nt(h_hi, hi_ref[c0:c1, :])
            up = _dot_nt(h_lo, lo_ref[DFF + c0:DFF + c1, :]) + _dot_nt(h_hi, hi_ref[DFF + c0:DFF + c1, :])
            sg = _sigmoid(gate)
            silu = gate * sg
            back_ref[:, c0:c1] = (up * (sg * (1.0 + gate * (1.0 - sg)))).astype(BF16)
            back_ref[:, DFF + c0:DFF + c1] = silu.astype(BF16)
            act_ref[:, c0:c1] = (silu * up).astype(BF16)

    return _call(
        body, "ffn_up", (S // tm,),
        in_specs=[_rows(D, tm), _resident((2 * DFF, HALF)), _resident((2 * DFF, HALF))],
        out_specs=[_rows(2 * DFF, tm), _rows(DFF, tm)],
        out_shape=[jax.ShapeDtypeStruct((S, 2 * DFF), BF16), jax.ShapeDtypeStruct((S, DFF), BF16)],
        operands=(h2, w_lo, w_hi), exchange=exchange, exchange_operands=exchange_operands)


def _ffn_down_loss(act, x2, target, w_ffn_out, norm_final, tm=512):
    S = act.shape[0]

    def body(act_ref, x2_ref, t_ref, w_ref, gn_ref, dx3_ref, dx3b_ref, loss_ref, dvec_ref):
        i = pl.program_id(0)

        @pl.when(i == 0)
        def _():
            loss_ref[...] = jnp.zeros_like(loss_ref)
            dvec_ref[...] = jnp.zeros_like(dvec_ref)

        x3 = x2_ref[...] + _dot(act_ref[...], w_ref[...])
        r3, xh3 = _rms(x3)
        g_fin = gn_ref[...]
        e = xh3 * g_fin - t_ref[...]
        loss_ref[...] += jnp.sum(e * e, axis=(0, 1), keepdims=True) * (0.5 / D)
        dy = e * (1.0 / D)
        dvec_ref[0:1, :] += jnp.sum(dy * xh3, axis=0, keepdims=True)
        dx3 = _rms_bwd(dy, g_fin, r3, xh3)
        dx3_ref[...] = dx3
        dx3b_ref[...] = dx3.astype(BF16)

    return pl.pallas_call(
        body, name="ffn_down_loss", grid=(S // tm,),
        in_specs=[_rows(DFF, tm), _rows(D, tm), _rows(D, tm), _resident((DFF, D)), _resident((1, D))],
        out_specs=[_rows(D, tm), _rows(D, tm), _resident((1, 1)), _resident((8, D))],
        out_shape=[jax.ShapeDtypeStruct((S, D), F32), jax.ShapeDtypeStruct((S, D), BF16),
                   jax.ShapeDtypeStruct((1, 1), F32), jax.ShapeDtypeStruct((8, D), F32)],
        compiler_params=_params("arbitrary"),
    )(act, x2, target, w_ffn_out, norm_final)


def _ffn_bwd_down(dx3b, gu, w_ffn_out, tm=512):
    S = dx3b.shape[0]

    def body(d_ref, back_ref, w_ref, dgu_ref):
        d = d_ref[...]
        for c0, c1 in FF_CHUNKS:
            dact = _dot_nt(d, w_ref[c0:c1, :])
            dgu_ref[:, c0:c1] = (dact * back_ref[:, c0:c1].astype(F32)).astype(BF16)
            dgu_ref[:, DFF + c0:DFF + c1] = (dact * back_ref[:, DFF + c0:DFF + c1].astype(F32)).astype(BF16)

    return pl.pallas_call(
        body, name="ffn_bwd_down", grid=(S // tm,),
        in_specs=[_rows(D, tm), _rows(2 * DFF, tm), _resident((DFF, D))],
        out_specs=_rows(2 * DFF, tm),
        out_shape=jax.ShapeDtypeStruct((S, 2 * DFF), BF16),
        compiler_params=_params("parallel"),
    )(dx3b, gu, w_ffn_out)


def _ffn_bwd_up(dgu, x2, dx3, w_lo, w_hi, norm_ffn, w_o, tm=512):
    S = dgu.shape[0]
    HALF = D // 2

    def body(dgu_ref, x2_ref, dx3_ref, lo_ref, hi_ref, gf_ref, wo_ref, dx2_ref, dx2b_ref, dmixo_ref, dvec_ref):
        i = pl.program_id(0)

        @pl.when(i == 0)
        def _():
            dvec_ref[...] = jnp.zeros_like(dvec_ref)

        dgate, dup = dgu_ref[:, 0:DFF], dgu_ref[:, DFF:2 * DFF]
        dh2 = jnp.concatenate([_dot(dgate, w[0:DFF, :]) + _dot(dup, w[DFF:2 * DFF, :]) for w in (lo_ref, hi_ref)], axis=1)
        r2, xh2 = _rms(x2_ref[...])
        dvec_ref[0:1, :] += jnp.sum(dh2 * xh2, axis=0, keepdims=True)
        dx2 = dx3_ref[...] + _rms_bwd(dh2, gf_ref[...], r2, xh2)
        dx2_ref[...] = dx2
        dx2b = dx2.astype(BF16)
        dx2b_ref[...] = dx2b
        dmixo_ref[...] = _dot_nt(dx2b, wo_ref[...]).astype(BF16)

    return pl.pallas_call(
        body, name="ffn_bwd_up", grid=(S // tm,),
        in_specs=[_rows(2 * DFF, tm), _rows(D, tm), _rows(D, tm), _resident((2 * DFF, HALF)), _resident((2 * DFF, HALF)),
                  _resident((1, D)), _resident((D, D))],
        out_specs=[_rows(D, tm), _rows(D, tm), _rows(D, tm), _resident((8, D))],
        out_shape=[jax.ShapeDtypeStruct((S, D), F32), jax.ShapeDtypeStruct((S, D), BF16), jax.ShapeDtypeStruct((S, D), BF16),
                   jax.ShapeDtypeStruct((8, D), F32)],
        compiler_params=_params("arbitrary"),
    )(dgu, x2, dx3, w_lo, w_hi, norm_ffn, w_o)


VEC_ROWS = 16
MAT_WA = 4 * PG
MAT_WX = MAT_WA + NH * HD
MAT_ROWS = MAT_WX + NH * HD


def _mixer_bwd(proj, dmixo, y_pool, y_rnn, hr, kept, gates, wg, scale, w_pool_out, conv_w, conv_b, wa, ba, wx, bx, lam, w_rnn_out,
               exchange=None, exchange_operands=(), tm=256):
    S = proj.shape[0]
    nt = S // tm

    def rev(cols):
        return pl.BlockSpec((tm, cols), lambda i: (nt - 1 - i, 0))

    def halo(rows_, cols):
        per = tm // rows_
        return pl.BlockSpec((rows_, cols), lambda i: (jnp.maximum((nt - 1 - i) * per - 1, 0), 0))

    def body(proj_ref, projh_ref, dmixo_ref, yp_ref, yr_ref, hr_ref, hrh_ref, kept_ref, gates_ref, wg_ref, scale_ref, wpo_ref, cw_ref, cb_ref,
             wa_ref, ba_ref, wx_ref, bx_ref, lam_ref, wro_ref,
             dproj_ref, dypb_ref, dyrb_ref, dmat_ref, dvec_ref,
             q_carry, dv_carry, a_carry, g_carry, g_scr):
        i = pl.program_id(0)
        ti = nt - 1 - i

        @pl.when(i == 0)
        def _():
            q_carry[...] = jnp.zeros_like(q_carry)
            dv_carry[...] = jnp.zeros_like(dv_carry)
            a_carry[...] = jnp.zeros_like(a_carry)
            g_carry[...] = jnp.zeros_like(g_carry)
            dmat_ref[...] = jnp.zeros_like(dmat_ref)
            dvec_ref[...] = jnp.zeros_like(dvec_ref)

        rows = lax.broadcasted_iota(jnp.int32, (tm, 1), 0)
        t_glob = ti * tm + rows
        has_prev = (ti > 0).astype(F32)
        dmixo = dmixo_ref[...].astype(F32)

        s_p = _sigmoid(proj_ref[:, DP + 2 * DR:DP + 2 * DR + D])
        s_r = _sigmoid(proj_ref[:, DP + 2 * DR + D:DIN])
        dproj_ref[:, DP + 2 * DR:DP + 2 * DR + D] = (dmixo * yp_ref[...].astype(F32) * s_p * (1.0 - s_p)).astype(BF16)
        dproj_ref[:, DP + 2 * DR + D:DIN] = (dmixo * yr_ref[...].astype(F32) * s_r * (1.0 - s_r)).astype(BF16)
        dyp = (dmixo * s_p).astype(BF16)
        dyr = (dmixo * s_r).astype(BF16)
        dypb_ref[...] = dyp
        dyrb_ref[...] = dyr

        dz = _dot_nt(dyr, wro_ref[...])
        u_gate = proj_ref[:, DP + DR:DP + 2 * DR]
        gg, dgelu = _gelu(u_gate, with_grad=True)
        hr_t = hr_ref[...]
        dproj_ref[:, DP + DR:DP + 2 * DR] = (dz * hr_t * dgelu).astype(BF16)
        dhr = dz * gg

        sp = _softplus_neg(lam_ref[...])
        v, a, mult = (kept_ref[k] for k in range(KEPT))
        r, gi = (gates_ref[k].astype(F32) for k in range(2))
        inv_mult = 1.0 / mult

        C = jnp.where(rows == tm - 1, a_carry[0:1, :], pltpu.roll(a, tm - 1, axis=0))
        g_carry[0:1, :] = _linear_scan(g_scr, C, dhr, g_carry[0:1, :], reverse=True)
        a_carry[0:1, :] = a[0:1, :]
        g = g_scr[...]

        h_prev = jnp.where(rows == 0, hrh_ref[7:8, :] * has_prev, pltpu.roll(hr_t, 1, axis=0))
        da = g * h_prev
        gm = g * mult
        dmult = g * gi * v
        di = gm * v
        dv = gm * gi
        dlog_a = da * a - dmult * (a * a * inv_mult)
        dvec_ref[4:5, :] += jnp.sum(dlog_a * r, axis=0, keepdims=True)
        dra = (dlog_a * ((-LRU_C) * sp) * r * (1.0 - r))
        drx = di * gi * (1.0 - gi)
        dvec_ref[2:3, :] += jnp.sum(dra, axis=0, keepdims=True)
        dvec_ref[3:4, :] += jnp.sum(drx, axis=0, keepdims=True)
        drab = dra.astype(BF16)
        drxb = drx.astype(BF16)
        vb = v.astype(BF16)
        dvg = []
        for h in range(NH):
            sl = slice(h * HD, (h + 1) * HD)
            dvg.append(_dot_nt(drab[:, sl], wa_ref[h]) + _dot_nt(drxb[:, sl], wx_ref[h]))
            dmat_ref[MAT_WA + h * HD:MAT_WA + (h + 1) * HD, :] += _dot_tn(vb[:, sl], drab[:, sl])
            dmat_ref[MAT_WX + h * HD:MAT_WX + (h + 1) * HD, :] += _dot_tn(vb[:, sl], drxb[:, sl])
        dv = dv + jnp.concatenate(dvg, axis=1)
        dvec_ref[1:2, :] += jnp.sum(dv, axis=0, keepdims=True)
        dvext = jnp.concatenate([dv, dv_carry[...]], axis=0)
        dv_carry[...] = dv[0:CONV_HALO, :]
        n = tm + CONV_HALO
        u_rnn = proj_ref[:, DP:DP + DR]
        du_rnn = dv * cw_ref[3:4, :]
        dvec_ref[8:9, :] += jnp.sum(dv * u_rnn, axis=0, keepdims=True)
        for k in range(3):
            dv_k = pltpu.roll(dvext, n - (3 - k), axis=0)[0:tm, :]
            du_rnn = du_rnn + dv_k * cw_ref[k:k + 1, :]
            dvec_ref[5 + k:6 + k, :] += jnp.sum(dv_k * u_rnn, axis=0, keepdims=True)
        dproj_ref[:, DP:DP + DR] = du_rnn.astype(BF16)

        dpm = _dot_nt(dyp, wpo_ref[...])
        u_pool = proj_ref[:, 0:DP]
        ext = jnp.concatenate([projh_ref[:, 0:DP] * has_prev, u_pool], axis=0)
        sums = _pool_windows(ext, +1)
        scale_v = scale_ref[...]
        qs = []
        dpooled = []
        dscale = []
        for gi_, w in enumerate(WINDOWS):
            sl = slice(gi_ * PG, (gi_ + 1) * PG)
            inv_cnt = 1.0 / jnp.minimum(t_glob + 1, w).astype(F32)
            pooled_b = (sums[gi_][POOL_HALO:, :] * inv_cnt - u_pool[:, sl]).astype(BF16)
            mixed_g = _dot(pooled_b, wg_ref[gi_])
            dscale.append(jnp.sum(dpm[:, sl] * mixed_g, axis=0, keepdims=True))
            dmixed_b = (dpm[:, sl] * scale_v[:, sl]).astype(BF16)
            dmat_ref[gi_ * PG:(gi_ + 1) * PG, :] += _dot_tn(pooled_b, dmixed_b)
            dp_g = _dot_nt(dmixed_b, wg_ref[gi_])
            dpooled.append(dp_g)
            qs.append(dp_g * inv_cnt)
        dvec_ref[0:1, 0:DP] += jnp.concatenate(dscale, axis=1)
        q = jnp.concatenate(qs, axis=1)
        qext = jnp.concatenate([q, q_carry[...]], axis=0)
        q_carry[...] = q[0:POOL_HALO, :]
        tsum = _pool_windows(qext, -1)
        for gi_ in range(4):
            dproj_ref[:, gi_ * PG:(gi_ + 1) * PG] = (tsum[gi_][0:tm, :] - dpooled[gi_]).astype(BF16)

        @pl.when(i == nt - 1)
        def _():
            dvec_ref[4:5, :] = dvec_ref[4:5, :] * (LRU_C * _sigmoid(-lam_ref[...]))

    return _call(
        body, "mixer_bwd", (nt,),
        in_specs=[rev(DIN), halo(POOL_HALO, DIN), rev(D), rev(D), rev(D), rev(DR), halo(8, DR),
                  pl.BlockSpec((KEPT, tm, DR), lambda i: (0, nt - 1 - i, 0)),
                  pl.BlockSpec((2, tm, DR), lambda i: (0, nt - 1 - i, 0)), _resident((4, PG, PG)), _resident((1, DP)), _resident((DP, D)), _resident((4, DR)), _resident((1, DR)),
                  _resident((NH, HD, HD)), _resident((1, DR)), _resident((NH, HD, HD)), _resident((1, DR)),
                  _resident((1, DR)), _resident((DR, D))],
        out_specs=[rev(DIN), rev(D), rev(D), _resident((MAT_ROWS, HD)), _resident((VEC_ROWS, DR))],
        out_shape=[jax.ShapeDtypeStruct((S, DIN), BF16), jax.ShapeDtypeStruct((S, D), BF16),
                   jax.ShapeDtypeStruct((S, D), BF16), jax.ShapeDtypeStruct((MAT_ROWS, HD), F32),
                   jax.ShapeDtypeStruct((VEC_ROWS, DR), F32)],
        scratch_shapes=[pltpu.VMEM((POOL_HALO, DP), F32), pltpu.VMEM((CONV_HALO, DR), F32), pltpu.VMEM((8, DR), F32),
                        pltpu.VMEM((8, DR), F32), pltpu.VMEM((tm, DR), F32)],
        operands=(proj, proj, dmixo, y_pool, y_rnn, hr, hr, kept, gates, wg, scale, w_pool_out, conv_w, conv_b, wa, ba, wx, bx, lam,
                  w_rnn_out),
        exchange=exchange, exchange_operands=exchange_operands)


def _in_bwd(dproj, x, dx2, norm_mix, w_in, exchange=None, exchange_operands=(), tm=512):
    S = x.shape[0]

    def body(dp_ref, x_ref, dx2_ref, g_ref, w_ref, dx_ref, dg_ref):
        i = pl.program_id(0)

        @pl.when(i == 0)
        def _():
            dg_ref[...] = jnp.zeros_like(dg_ref)

        dh = _dot(dp_ref[:, 0:1536], w_ref[0:1536, :])
        dh = dh + _dot(dp_ref[:, 1536:3072], w_ref[1536:3072, :])
        dh = dh + _dot(dp_ref[:, 3072:DIN], w_ref[3072:DIN, :])
        xv = x_ref[...]
        r = lax.rsqrt(jnp.mean(xv * xv, axis=-1, keepdims=True) + EPS)
        xh = xv * r
        dg_ref[0:1, :] += jnp.sum(dh * xh, axis=0, keepdims=True)
        dxh = dh * g_ref[...]
        dx_ref[...] = dx2_ref[...] + r * (dxh - xh * jnp.mean(dxh * xh, axis=-1, keepdims=True))

    return _call(
        body, "in_bwd", (S // tm,),
        in_specs=[_rows(DIN, tm), _rows(D, tm), _rows(D, tm), _resident((1, D)), _resident((DIN, D))],
        out_specs=[_rows(D, tm), _resident((8, D))],
        out_shape=[jax.ShapeDtypeStruct((S, D), F32), jax.ShapeDtypeStruct((8, D), F32)],
        operands=(dproj, x, dx2, norm_mix, w_in), exchange=exchange, exchange_operands=exchange_operands)


def _wgrad(a, b, name, tk, tn, exchange=None, exchange_operands=()):
    S, K = a.shape
    N = b.shape[1]

    def body(a_ref, b_ref, o_ref):
        o_ref[...] = _dot_tn(a_ref[...], b_ref[...]).astype(BF16)

    (out,), exchanged = _call(
        body, name, (K // tk, N // tn),
        in_specs=[pl.BlockSpec((S, tk), lambda k, n: (0, k)), pl.BlockSpec((S, tn), lambda k, n: (0, n))],
        out_specs=[pl.BlockSpec((tk, tn), lambda k, n: (k, n))],
        out_shape=[jax.ShapeDtypeStruct((K, N), BF16)],
        operands=(a, b), exchange=exchange, exchange_operands=exchange_operands)
    return (out, exchanged) if exchange is not None else out


VEC_SCALE, VEC_CONV_B, VEC_BA, VEC_BX, VEC_LAM, VEC_CONV_W, VEC_NORM_FINAL, VEC_NORM_FFN = 0, 1, 2, 3, 4, 5, 9, 10
VEC_LOSS = 11


class _Big:
    def __init__(self, name, rows, cols, axis, n, dtype=BF16, transposed=False, src_cols=None):
        self.name, self.rows, self.cols, self.axis, self.n, self.dtype = name, rows, cols, axis, n, dtype
        self.transposed = transposed
        self.src_cols = src_cols
        self.block_shape = (rows, n) if axis == 1 else (n, cols)

    def block(self, ref, p):
        if self.axis == 1:
            return ref.at[:, pl.ds(pl.multiple_of(p * self.n, 128), self.n)]
        return ref.at[pl.ds(pl.multiple_of(p * self.n, 16 if self.dtype == BF16 else 8), self.n), :]


BIG = (_Big("w_in", DIN, D, 0, DIN // 8, transposed=True), _Big("w_pool_out", DP, D, 1, D // 8),
       _Big("w_rnn_out", DR, D, 0, DR // 8), _Big("w_o", D, D, 0, D // 8),
       _Big("w_ffn_in", 2 * DFF, D, 0, 2 * DFF // 8, transposed=True), _Big("w_ffn_out", DFF, D, 0, DFF // 8))
CONV_W = _Big("conv_w", 8, DR, 1, DR // 8, F32)
W_FFN_IN_HALVES = (_Big("w_ffn_in_lo", 2 * DFF, D // 2, 0, 2 * DFF // 8, src_cols=(0, D // 2)),
                   _Big("w_ffn_in_hi", 2 * DFF, D // 2, 0, 2 * DFF // 8, src_cols=(D // 2, D)))
GATHERED = BIG + (CONV_W,) + W_FFN_IN_HALVES

HBM_SPEC = pl.BlockSpec(memory_space=pl.ANY)
VMEM_SPEC = pl.BlockSpec(memory_space=pltpu.VMEM)


def _place():
    x, y, c = (lax.axis_index(a) for a in MESH_AXES)
    other_chips = [(1 - x, y), (x, 1 - y), (1 - x, 1 - y)]
    return x, y, c, other_chips


def _remote(src, dst, send_sems, recv_sems, idx, to):
    return pltpu.make_async_remote_copy(src_ref=src, dst_ref=dst, send_sem=send_sems.at[idx], recv_sem=recv_sems.at[idx],
                                        device_id=to, device_id_type=MESH)


def _device_index(chip, core):
    return 4 * chip[0] + 2 * chip[1] + core


class _Gather:
    def __init__(self, tensors):
        self.tensors = tuple(tensors)
        n = len(self.tensors)
        self.in_specs = [HBM_SPEC] * n
        self.out_specs = [HBM_SPEC] * n
        self.out_shape = [jax.ShapeDtypeStruct((T.rows, T.cols), T.dtype) for T in self.tensors]
        self.scratch_shapes = [pltpu.VMEM(T.block_shape, T.dtype) for T in self.tensors] + [
            pltpu.VMEM(T.block_shape, F32) for T in self.tensors] + [
            pltpu.SemaphoreType.DMA((n, 7)), pltpu.SemaphoreType.DMA((n, 7)), pltpu.SemaphoreType.DMA((n, 2))]

    collective_id = 1

    def peers(self):
        x, y, c, _ = _place()
        return [(x, y, 1 - c), (1 - x, y, c), (x, 1 - y, c)]

    def middles(self, steps):
        return [(steps // 2, self.relay), (steps - 1, self.middle)]

    def _copies(self, ins, outs, scratch):
        n = len(self.tensors)
        mine, raw, (send_sems, recv_sems, loc_sems) = scratch[:n], scratch[n:2 * n], scratch[2 * n:]
        x, y, c, chips = _place()
        sibling = (x, y, 1 - c)
        me = _device_index((x, y), c)
        relay_from = (jnp.where(c == 0, 1 - x, x), jnp.where(c == 0, y, 1 - y))
        relay_to = (jnp.where(c == 0, x, 1 - x), jnp.where(c == 0, 1 - y, y))
        loads, stores, first, relays, passed, arrivals, late = [], [], [], [], [], [], []
        for t, T in enumerate(self.tensors):
            place = T.block(outs[t], me)
            src = ins[t] if T.src_cols is None else ins[t].at[:, T.src_cols[0]:T.src_cols[1]]
            loads.append(pltpu.make_async_copy(src, raw[t], loc_sems.at[t, 0]))
            stores.append(pltpu.make_async_copy(mine[t], place, loc_sems.at[t, 1]))
            first.append(_remote(mine[t], place, send_sems, recv_sems, (t, 0), sibling))
            theirs = T.block(outs[t], _device_index((x, y), 1 - c))
            late.append(_remote(theirs, theirs, send_sems, recv_sems, (t, 0), sibling))
            relayed = T.block(outs[t], _device_index(relay_from, c))
            relays.append(_remote(relayed, relayed, send_sems, recv_sems, (t, 3), (*relay_to, c)))
            for k, chip in enumerate(chips):
                if k < 2:
                    first.append(_remote(mine[t], place, send_sems, recv_sems, (t, 1 + k), (*chip, c)))
                land = T.block(outs[t], _device_index(chip, c))
                arrivals.append(_remote(land, land, send_sems, recv_sems, (t, 1 + k), sibling))
                passed.append(_remote(land, land, send_sems, recv_sems, (t, 4 + k), sibling))
                theirs = T.block(outs[t], _device_index(chip, 1 - c))
                late.append(_remote(theirs, theirs, send_sems, recv_sems, (t, 4 + k), sibling))
        return loads, stores, first, relays, passed, arrivals, late

    def start(self, ins, outs, scratch):
        loads, stores, first, _, _, _, _ = self._copies(ins, outs, scratch)
        n = len(self.tensors)
        for cp in loads:
            cp.start()
        for t, cp in enumerate(loads):
            cp.wait()
            scratch[t][...] = scratch[n + t][...].astype(self.tensors[t].dtype)
        for cp in stores + first:
            cp.start()

    def relay(self, ins, outs, scratch, skip=0):
        _, _, _, relays, passed, arrivals, _ = self._copies(ins, outs, scratch)
        for t in range(skip, len(self.tensors)):
            arrivals[3 * t].wait_recv()
            arrivals[3 * t + 1].wait_recv()
            for cp in (relays[t], passed[3 * t], passed[3 * t + 1]):
                cp.start()

    def middle(self, ins, outs, scratch, skip=0):
        _, _, _, _, passed, arrivals, _ = self._copies(ins, outs, scratch)
        for t in range(skip, len(self.tensors)):
            arrivals[3 * t + 2].wait_recv()
            passed[3 * t + 2].start()

    def finish(self, ins, outs, scratch, skip=0):
        _, stores, first, relays, passed, _, late = self._copies(ins, outs, scratch)
        for cp in late[4 * skip:]:
            cp.wait_recv()
        for cp in first + relays + passed:
            cp.wait_send()
        for cp in stores[skip:]:
            cp.wait()


def _in_proj_gather(x, norm_mix, blocks, tensors, order, tm=1024):
    S = x.shape[0]
    nt = S // tm
    n = len(tensors)
    gather = _Gather(tensors)
    CB = 2 * tensors[0].n

    def body(order_ref, x_ref, g_ref, *refs):
        ins, (proj_ref, h_ref), outs = refs[:n], refs[n:n + 2], refs[n + 2:2 * n + 2]
        (h_all, w_chip, w_sem), scratch = refs[2 * n + 2:2 * n + 5], refs[2 * n + 5:]
        q, i = pl.program_id(0), pl.program_id(1)
        _, stores, _, relays, passed, arrivals, late = gather._copies(ins, outs, scratch)

        def fetch(turn):
            rows = outs[0].at[pl.ds(pl.multiple_of(order_ref[turn] * CB, 16), CB), :]
            cp = pltpu.make_async_copy(rows, w_chip, w_sem)
            cp.start()
            cp.wait()

        @pl.when((q == 0) & (i == 0))
        def _():
            _enter(gather)
            gather.start(ins, outs, scratch)
            late[0].wait_recv()
            stores[0].wait()
            fetch(0)

        @pl.when((q == 1) & (i == 0))
        def _():
            arrivals[0].wait_recv()
            arrivals[1].wait_recv()
            for cp in (relays[0], passed[0], passed[1]):
                cp.start()
            late[1].wait_recv()
            fetch(1)

        @pl.when((q == 2) & (i == 0))
        def _():
            late[2].wait_recv()
            fetch(2)
            gather.relay(ins, outs, scratch, skip=1)

        @pl.when((q == 3) & (i == 0))
        def _():
            arrivals[2].wait_recv()
            passed[2].start()
            late[3].wait_recv()
            fetch(3)

        rows = pl.ds(pl.multiple_of(i * tm, tm), tm)

        @pl.when(q == 0)
        def _():
            xv = x_ref[...]
            r = lax.rsqrt(jnp.mean(xv * xv, axis=-1, keepdims=True) + EPS)
            h = (xv * r * g_ref[...]).astype(BF16)
            h_all[rows, :] = h
            h_ref[...] = h

        proj_ref[...] = _dot_nt(h_all[rows, :], w_chip[...])

        @pl.when((q == 3) & (i == nt - 1))
        def _():
            gather.middle(ins, outs, scratch, skip=1)
            gather.finish(ins, outs, scratch, skip=1)

    row_tile = lambda q, i, order: (jnp.where(q == 0, i, nt - 1), 0)
    whole = lambda shape: pl.BlockSpec(shape, lambda q, i, order: (0,) * len(shape), pipeline_mode=pl.Buffered(1))
    outs = pl.pallas_call(
        body, name="in_proj_gather",
        grid_spec=pltpu.PrefetchScalarGridSpec(
            num_scalar_prefetch=1, grid=(4, nt),
            in_specs=[pl.BlockSpec((tm, D), row_tile), whole((1, D))] + gather.in_specs,
            out_specs=[pl.BlockSpec((tm, CB), lambda q, i, order: (i, order[q])), pl.BlockSpec((tm, D), row_tile)]
            + gather.out_specs,
            scratch_shapes=[pltpu.VMEM((S, D), BF16), pltpu.VMEM((CB, D), BF16), pltpu.SemaphoreType.DMA]
            + gather.scratch_shapes),
        out_shape=[jax.ShapeDtypeStruct((S, DIN), F32), jax.ShapeDtypeStruct((S, D), BF16)] + gather.out_shape,
        compiler_params=pltpu.CompilerParams(dimension_semantics=("arbitrary", "arbitrary"), vmem_limit_bytes=VMEM_LIMIT,
                                             collective_id=gather.collective_id),
    )(order, x, norm_mix, *blocks)
    return outs[:2], outs[2:]


PAIR_ROWS = 32


class _PairReduce:
    collective_id = 3

    def __init__(self, tensors):
        self.tensors = tuple(tensors)
        nt = len(self.tensors)
        blocks = [T.block_shape for T in self.tensors]
        self.in_specs = [HBM_SPEC] * nt
        self.out_specs = [HBM_SPEC] * (2 * nt)
        self.out_shape = ([jax.ShapeDtypeStruct(b, BF16) for b in blocks]
                          + [jax.ShapeDtypeStruct((3,) + b, BF16) for b in blocks])
        self.scratch_shapes = ([pltpu.VMEM((4,) + b, BF16) for b in blocks] + [pltpu.VMEM((3,) + b, BF16) for b in blocks]
                               + [pltpu.SemaphoreType.DMA((nt, 4)), pltpu.SemaphoreType.DMA((nt, 4)),
                                  pltpu.SemaphoreType.DMA((nt, 5))])

    def peers(self):
        x, y, c, _ = _place()
        return [(x, y, 1 - c)]

    def middles(self, steps):
        return []

    def _copies(self, ins, outs, scratch):
        nt = len(self.tensors)
        own_out, sums_out, landed, mine = outs[:nt], outs[nt:], scratch[:nt], scratch[nt:2 * nt]
        send_sems, recv_sems, loc_sems = scratch[2 * nt:]
        x, y, c, chips = _place()
        chip_of = [2 * chip[0] + chip[1] for chip in chips]
        swaps, loads, stores = [], [], []
        for t, T in enumerate(self.tensors):
            for j in range(4):
                swaps.append(_remote(T.block(ins[t], 2 * j + 1 - c), landed[t].at[j], send_sems, recv_sems, (t, j),
                                     (x, y, 1 - c)))
            for k in range(3):
                loads.append(pltpu.make_async_copy(T.block(ins[t], 2 * chip_of[k] + c), mine[t].at[k], loc_sems.at[t, k]))
            stores.append(pltpu.make_async_copy(mine[t], sums_out[t], loc_sems.at[t, 3]))
            stores.append(pltpu.make_async_copy(landed[t].at[2 * x + y], own_out[t], loc_sems.at[t, 4]))
        return swaps, loads, stores, landed, mine, chip_of

    def start(self, ins, outs, scratch):
        swaps, loads, _, _, _, _ = self._copies(ins, outs, scratch)
        for cp in swaps + loads:
            cp.start()

    def finish(self, ins, outs, scratch):
        swaps, loads, stores, landed, mine, chip_of = self._copies(ins, outs, scratch)
        for cp in loads:
            cp.wait()
        for cp in swaps:
            cp.wait_recv()
        for t, T in enumerate(self.tensors):
            for k in range(3):
                acc, got = mine[t].at[k], landed[t].at[chip_of[k]]

                def add(i, carry, acc=acc, got=got):
                    rows = pl.ds(pl.multiple_of(i * PAIR_ROWS, PAIR_ROWS), PAIR_ROWS)
                    acc[rows, :] = (acc[rows, :].astype(F32) + got[rows, :].astype(F32)).astype(BF16)
                    return carry

                lax.fori_loop(0, T.block_shape[0] // PAIR_ROWS, add, 0)
        for cp in stores:
            cp.start()
        for cp in swaps:
            cp.wait_send()
        for cp in stores:
            cp.wait()


def _pair_reduce(grads, tensors, name):
    reduce = _PairReduce(tensors)
    nt = len(reduce.tensors)

    def body(*refs):
        ins, outs, scratch = refs[:nt], refs[nt:3 * nt], refs[3 * nt:]
        _enter(reduce)
        reduce.start(ins, outs, scratch)
        reduce.finish(ins, outs, scratch)

    return pl.pallas_call(
        body, name=name, in_specs=reduce.in_specs, out_specs=reduce.out_specs, out_shape=reduce.out_shape,
        scratch_shapes=reduce.scratch_shapes,
        compiler_params=pltpu.CompilerParams(vmem_limit_bytes=VMEM_LIMIT, collective_id=reduce.collective_id),
    )(*grads)


class _Scatter:
    def __init__(self, tensors):
        self.tensors = tuple(tensors)
        n = len(self.tensors)
        self.in_specs = [HBM_SPEC] * n
        self.out_specs = [HBM_SPEC] * n
        self.out_shape = [jax.ShapeDtypeStruct((2,) + T.block_shape, BF16) for T in self.tensors]
        self.scratch_shapes = [pltpu.VMEM(T.block_shape, BF16) for T in self.tensors] * 2 + [
            pltpu.SemaphoreType.DMA((n, 3)), pltpu.SemaphoreType.DMA((n, 3)), pltpu.SemaphoreType.DMA((n,))]

    collective_id = 2

    def peers(self):
        x, y, c, _ = _place()
        return [(1 - x, y, c), (x, 1 - y, c)]

    def middles(self, steps):
        return [(steps // 2, self.middle)]

    def _copies(self, ins, outs, scratch):
        n = len(self.tensors)
        landed, mine, (send_sems, recv_sems, loc_sems) = scratch[:n], scratch[n:2 * n], scratch[2 * n:]
        x, y, c, _ = _place()
        direct = (jnp.where(c == 0, 1 - x, x), jnp.where(c == 0, y, 1 - y), c)
        other = (jnp.where(c == 0, x, 1 - x), jnp.where(c == 0, 1 - y, y), c)
        k_direct = jnp.where(c == 0, 0, 1)
        to_direct, legs, loads, combined, arrivals = [], [], [], [], []
        for t in range(n):
            to_direct.append(_remote(ins[t].at[k_direct], outs[t].at[0], send_sems, recv_sems, (t, 0), direct))
            legs.append(_remote(ins[t].at[2], landed[t], send_sems, recv_sems, (t, 2), direct))
            loads.append(pltpu.make_async_copy(ins[t].at[1 - k_direct], mine[t], loc_sems.at[t]))
            combined.append(_remote(mine[t], outs[t].at[1], send_sems, recv_sems, (t, 1), other))
            arrivals.append(_remote(landed[t], landed[t], send_sems, recv_sems, (t, 2), direct))
        return to_direct, legs, loads, combined, arrivals, landed, mine

    def start(self, ins, outs, scratch):
        to_direct, legs, loads, _, _, _, _ = self._copies(ins, outs, scratch)
        for cp in to_direct + legs + loads:
            cp.start()

    def middle(self, ins, outs, scratch):
        _, _, loads, combined, arrivals, landed, mine = self._copies(ins, outs, scratch)
        for t, T in enumerate(self.tensors):
            loads[t].wait()
            arrivals[t].wait_recv()
            acc, got = mine[t], landed[t]

            def add(i, carry, acc=acc, got=got):
                rows = pl.ds(pl.multiple_of(i * PAIR_ROWS, PAIR_ROWS), PAIR_ROWS)
                acc[rows, :] = (acc[rows, :].astype(F32) + got[rows, :].astype(F32)).astype(BF16)
                return carry

            lax.fori_loop(0, T.block_shape[0] // PAIR_ROWS, add, 0)
            combined[t].start()

    def finish(self, ins, outs, scratch):
        to_direct, legs, _, combined, _, _, _ = self._copies(ins, outs, scratch)
        for cp in to_direct + combined:
            cp.wait()
        for cp in legs:
            cp.wait_send()


class _ReduceScatter:
    collective_id = 1

    def __init__(self, tensors):
        self.first, self.second = _PairReduce(tensors), _Scatter(tensors)
        self.in_specs = self.first.in_specs
        self.out_specs = self.first.out_specs + self.second.out_specs
        self.out_shape = self.first.out_shape + self.second.out_shape
        self.scratch_shapes = self.first.scratch_shapes + self.second.scratch_shapes

    def peers(self):
        x, y, c, _ = _place()
        return [(x, y, 1 - c), (1 - x, y, c), (x, 1 - y, c)]

    def _parts(self, ins, outs, scratch):
        n_out, n_scr = len(self.first.out_specs), len(self.first.scratch_shapes)
        pair_sums = list(outs[n_out // 2:n_out])
        return (ins, outs[:n_out], scratch[:n_scr]), (pair_sums, outs[n_out:], scratch[n_scr:])

    def start(self, ins, outs, scratch):
        self.first.start(*self._parts(ins, outs, scratch)[0])

    def middles(self, steps):
        def hand_over(ins, outs, scratch):
            first, second = self._parts(ins, outs, scratch)
            self.first.finish(*first)
            self.second.start(*second)

        def relay(ins, outs, scratch):
            self.second.middle(*self._parts(ins, outs, scratch)[1])

        return [(steps // 8, hand_over), (steps // 2, relay)]

    def finish(self, ins, outs, scratch):
        self.second.finish(*self._parts(ins, outs, scratch)[1])


def _adamw(w, g, m, v):
    m = ADAM_B1 * m + (1.0 - ADAM_B1) * g
    v = ADAM_B2 * v + (1.0 - ADAM_B2) * (g * g)
    m_hat = m / (1.0 - ADAM_B1 ** ADAM_STEP)
    v_hat = v / (1.0 - ADAM_B2 ** ADAM_STEP)
    delta = -ADAM_LR * (m_hat / (jnp.sqrt(v_hat) + ADAM_EPS) + ADAM_WD * w)
    return delta, m, v


def _final_sum(T, g, lz1, lz2, where, w, m, v):
    rows, cols = T.block_shape
    sub = 4 if T.axis == 0 and rows % 64 == 0 and rows > 256 else 1
    blk = (rows // sub, cols)

    def body(where_ref, g_ref, l1_ref, l2_ref, w_ref, m_ref, v_ref, g_out, d_out, m_out, v_out):
        tot = g_ref[...].astype(F32) + l1_ref[...].astype(F32)
        for k in range(2):
            tot = tot + l2_ref[k].astype(F32)
        g_out[...] = tot
        d_out[...], m_out[...], v_out[...] = _adamw(w_ref[...], tot, m_ref[...], v_ref[...])

    def in_whole(r, wh):
        p = wh[0]
        return (0, p) if T.axis == 1 else (p * sub + r, 0)

    own = pl.BlockSpec(blk, lambda r, wh: (r, 0))
    return pl.pallas_call(
        body, name="grad_final_" + T.name,
        grid_spec=pltpu.PrefetchScalarGridSpec(
            num_scalar_prefetch=1, grid=(sub,),
            in_specs=[pl.BlockSpec(blk, in_whole),
                      own,
                      pl.BlockSpec((2,) + blk, lambda r, wh: (0, r, 0)), own, own, own],
            out_specs=[own] * 4),
        out_shape=[jax.ShapeDtypeStruct(T.block_shape, F32)] * 4,
        compiler_params=_params("arbitrary"),
    )(where, g, lz1, lz2, w, m, v)


VEC_PIECE = DR // 8


class _AllReduce:
    def __init__(self, items):
        self.items = tuple(items)
        n = len(self.items)
        self.in_specs = [HBM_SPEC] * n
        self.out_specs = [HBM_SPEC] * n
        self.out_shape = [jax.ShapeDtypeStruct(shape, F32) for shape, _ in self.items]
        pieces = [(shape[0] // 8, shape[1]) if axis == 0 else (shape[0], shape[1] // 8) for shape, axis in self.items]
        self.scratch_shapes = ([pltpu.VMEM((8,) + p, F32) for p in pieces] + [pltpu.VMEM(p, F32) for p in pieces] + [
            pltpu.SemaphoreType.DMA((2 * n, 8)), pltpu.SemaphoreType.DMA((2 * n, 8)), pltpu.SemaphoreType.DMA((2 * n,))])

    collective_id = None

    def peers(self):
        return []

    def middles(self, steps):
        return [(steps // 2, self.middle)]

    def _copies(self, ins, outs, scratch):
        n = len(self.items)
        landed, sums, (send_sems, recv_sems, loc_sems) = scratch[:n], scratch[n:2 * n], scratch[2 * n:]
        x, y, c, _ = _place()
        me = _device_index((x, y), c)

        def peer(r):
            return (1 - x if r & 4 else x, 1 - y if r & 2 else y, 1 - c if r & 1 else c)

        def piece(i, ref, p):
            shape, axis = self.items[i]
            if axis == 0:
                rows = shape[0] // 8
                return ref.at[pl.ds(pl.multiple_of(p * rows, 8), rows), :]
            cols = shape[1] // 8
            return ref.at[:, pl.ds(pl.multiple_of(p * cols, 128), cols)]

        own, scatter, arrivals, keep, spread, late = [], [], [], [], [], []
        for i in range(n):
            own.append(pltpu.make_async_copy(piece(i, ins[i], me), landed[i].at[0], loc_sems.at[2 * i]))
            keep.append(pltpu.make_async_copy(sums[i], piece(i, outs[i], me), loc_sems.at[2 * i + 1]))
            for r in range(1, 8):
                to = peer(r)
                p = _device_index(to[:2], to[2])
                scatter.append(_remote(piece(i, ins[i], p), landed[i].at[r], send_sems, recv_sems, (2 * i, r), to))
                spread.append(_remote(sums[i], piece(i, outs[i], me), send_sems, recv_sems, (2 * i + 1, r), to))
                late.append(_remote(sums[i], piece(i, outs[i], p), send_sems, recv_sems, (2 * i + 1, r), to))
        return own, scatter, keep, spread, late, landed, sums

    def start(self, ins, outs, scratch):
        own, scatter, _, _, _, _, _ = self._copies(ins, outs, scratch)
        for cp in own + scatter:
            cp.start()

    def middle(self, ins, outs, scratch):
        own, scatter, keep, spread, _, landed, sums = self._copies(ins, outs, scratch)
        for cp in own:
            cp.wait()
        for cp in scatter:
            cp.wait_recv()
        for i in range(len(self.items)):
            total = landed[i][0]
            for r in range(1, 8):
                total = total + landed[i][r]
            sums[i][...] = total
        for cp in keep + spread:
            cp.start()

    def finish(self, ins, outs, scratch):
        _, scatter, keep, spread, late, _, _ = self._copies(ins, outs, scratch)
        for cp in late:
            cp.wait_recv()
        for cp in scatter + spread:
            cp.wait_send()
        for cp in keep:
            cp.wait()


class _Both:
    def __init__(self, a, b):
        self.a, self.b = a, b
        self.in_specs, self.out_specs = a.in_specs + b.in_specs, a.out_specs + b.out_specs
        self.out_shape, self.scratch_shapes = a.out_shape + b.out_shape, a.scratch_shapes + b.scratch_shapes

    collective_id = None

    def peers(self):
        return []

    def _each(self, ins, outs, scratch):
        a = self.a
        i, o, s = len(a.in_specs), len(a.out_specs), len(a.scratch_shapes)
        return (a, ins[:i], outs[:o], scratch[:s]), (self.b, ins[i:], outs[o:], scratch[s:])

    def middles(self, steps):
        def of(which, middle):
            return lambda ins, outs, scratch: middle(*self._each(ins, outs, scratch)[which][1:])
        return [(at, of(which, middle)) for which, e in enumerate((self.a, self.b)) for at, middle in e.middles(steps)]

    def start(self, ins, outs, scratch):
        for e, i, o, s in self._each(ins, outs, scratch):
            e.start(i, o, s)

    def finish(self, ins, outs, scratch):
        for e, i, o, s in self._each(ins, outs, scratch):
            e.finish(i, o, s)


def _all_reduce(arrays, items, name):
    reduce = _AllReduce(items)
    n = len(items)

    def body(*refs):
        ins, outs, scratch = refs[:n], refs[n:2 * n], refs[2 * n:]
        reduce.start(ins, outs, scratch)
        reduce.middle(ins, outs, scratch)
        reduce.finish(ins, outs, scratch)

    return pl.pallas_call(
        body, name=name, in_specs=reduce.in_specs, out_specs=reduce.out_specs, out_shape=reduce.out_shape,
        scratch_shapes=reduce.scratch_shapes,
    )(*arrays)


def _adam_small(grads, wmv):
    n = len(grads)

    def body(*refs):
        g_refs, rest = refs[:n], refs[n:]
        ins, outs = rest[:3 * n], rest[3 * n:]
        for i in range(n):
            d, m, v = _adamw(ins[3 * i][...], g_refs[i][...], ins[3 * i + 1][...], ins[3 * i + 2][...])
            outs[3 * i][...], outs[3 * i + 1][...], outs[3 * i + 2][...] = d, m, v

    flat = [a for t in wmv for a in t]
    return pl.pallas_call(
        body, name="adam_small",
        in_specs=[VMEM_SPEC] * (4 * n), out_specs=[VMEM_SPEC] * (3 * n),
        out_shape=[jax.ShapeDtypeStruct(a.shape, F32) for a in flat],
    )(*grads, *flat)


WEIGHT_NAMES = ("norm_mix", "w_in", "w_pool_grp", "pool_scale", "w_pool_out", "conv_w", "conv_b", "w_rg_a", "b_rg_a", "w_rg_x",
                "b_rg_x", "lru_lambda", "w_rnn_out", "w_o", "norm_ffn", "w_ffn_in", "w_ffn_out", "norm_final")


def kernel(x, norm_mix, w_in, w_pool_grp, pool_scale, w_pool_out, conv_w, conv_b, w_rg_a, b_rg_a, w_rg_x, b_rg_x, lru_lambda, w_rnn_out, w_o, norm_ffn, w_ffn_in, w_ffn_out, norm_final, loss_target, m_norm_mix, m_w_in, m_w_pool_grp, m_pool_scale, m_w_pool_out, m_conv_w, m_conv_b, m_w_rg_a, m_b_rg_a, m_w_rg_x, m_b_rg_x, m_lru_lambda, m_w_rnn_out, m_w_o, m_norm_ffn, m_w_ffn_in, m_w_ffn_out, m_norm_final, v_norm_mix, v_w_in, v_w_pool_grp, v_pool_scale, v_w_pool_out, v_conv_w, v_conv_b, v_w_rg_a, v_b_rg_a, v_w_rg_x, v_b_rg_x, v_lru_lambda, v_w_rnn_out, v_w_o, v_norm_ffn, v_w_ffn_in, v_w_ffn_out, v_norm_final):
    w = dict(norm_mix=norm_mix, w_in=w_in, w_pool_grp=w_pool_grp, pool_scale=pool_scale, w_pool_out=w_pool_out, conv_w=conv_w,
             conv_b=conv_b, w_rg_a=w_rg_a, b_rg_a=b_rg_a, w_rg_x=w_rg_x, b_rg_x=b_rg_x, lru_lambda=lru_lambda,
             w_rnn_out=w_rnn_out, w_o=w_o, norm_ffn=norm_ffn, w_ffn_in=w_ffn_in, w_ffn_out=w_ffn_out, norm_final=norm_final)
    m = dict(norm_mix=m_norm_mix, w_in=m_w_in, w_pool_grp=m_w_pool_grp, pool_scale=m_pool_scale, w_pool_out=m_w_pool_out,
             conv_w=m_conv_w, conv_b=m_conv_b, w_rg_a=m_w_rg_a, b_rg_a=m_b_rg_a, w_rg_x=m_w_rg_x, b_rg_x=m_b_rg_x,
             lru_lambda=m_lru_lambda, w_rnn_out=m_w_rnn_out, w_o=m_w_o, norm_ffn=m_norm_ffn, w_ffn_in=m_w_ffn_in,
             w_ffn_out=m_w_ffn_out, norm_final=m_norm_final)
    v = dict(norm_mix=v_norm_mix, w_in=v_w_in, w_pool_grp=v_w_pool_grp, pool_scale=v_pool_scale, w_pool_out=v_w_pool_out,
             conv_w=v_conv_w, conv_b=v_conv_b, w_rg_a=v_w_rg_a, b_rg_a=v_b_rg_a, w_rg_x=v_w_rg_x, b_rg_x=v_b_rg_x,
             lru_lambda=v_lru_lambda, w_rnn_out=v_w_rnn_out, w_o=v_w_o, norm_ffn=v_norm_ffn, w_ffn_in=v_w_ffn_in,
             w_ffn_out=v_w_ffn_out, norm_final=v_norm_final)
    xi, yi, ci = (lax.axis_index(a) for a in MESH_AXES)
    chip = 2 * xi + yi

    def held(T, a):
        return jnp.swapaxes(a, 0, 1) if T.transposed else a

    where = jnp.stack([2 * chip + ci]).astype(jnp.int32)
    by_name = {T.name: T for T in GATHERED}
    block = {T.name: held(T, w[T.name][0]) for T in BIG}
    block["conv_w"] = jnp.pad(conv_w[0], ((0, CONV_W.rows - 4), (0, 0)))
    block["w_ffn_in_lo"] = block["w_ffn_in_hi"] = block["w_ffn_in"]

    def gather_of(*names):
        return dict(exchange=_Gather([by_name[n] for n in names]), exchange_operands=[block[n] for n in names])

    def pair_sums(names, partials, tag):
        out = _pair_reduce(partials, [by_name[n] for n in names], "grad_pair_reduce_" + tag)
        return list(out[:len(names)]), list(out[len(names):])

    xs, target = x[0], loss_target[0]
    wg_b, wa_b, wx_b = (a[0].astype(BF16) for a in (w_pool_grp, w_rg_a, w_rg_x))
    ba2, bx2 = b_rg_a.reshape(1, DR), b_rg_x.reshape(1, DR)
    first = ("w_in", "w_pool_out", "w_rnn_out", "conv_w", "w_o")
    order = jnp.stack([chip, 2 * (1 - xi) + yi, 2 * xi + (1 - yi), 2 * (1 - xi) + (1 - yi)]).astype(jnp.int32)
    (proj, h1), (w_in_g, w_pool_out_g, w_rnn_out_g, conv_g, w_o_g) = _in_proj_gather(
        xs, norm_mix, [block[n] for n in first], [by_name[n] for n in first], order)
    mixer_weights = (wg_b, pool_scale, w_pool_out_g, conv_g[0:4], conv_b, wa_b, ba2, wx_b, bx2, lru_lambda, w_rnn_out_g)
    (pm, y_pool, hr, z, y_rnn, kept, gates), (w_ffn_lo_g, w_ffn_hi_g) = _mixer_fwd(
        proj, *mixer_weights, **gather_of("w_ffn_in_lo", "w_ffn_in_hi"))
    (mix, x2, h2), _ = _merge_out(xs, proj, y_pool, y_rnn, w_o_g, norm_ffn)
    (gu, act), (w_ffn_out_g,) = _ffn_up(h2, w_ffn_lo_g, w_ffn_hi_g, **gather_of("w_ffn_out"))
    dx3, dx3b, loss_part, dvec_fin = _ffn_down_loss(act, x2, target, w_ffn_out_g, norm_final.reshape(1, D))

    dgu = _ffn_bwd_down(dx3b, gu, w_ffn_out_g)
    dx2, dx2b, dmixo, dvec_ffn = _ffn_bwd_up(dgu, x2, dx3, w_ffn_lo_g, w_ffn_hi_g, norm_ffn, w_o_g)
    names_a = ("w_ffn_in", "w_ffn_out", "w_o")
    g_ffn_out = _wgrad(act, dx3b, "wgrad_ffn_out", 1408, 512)
    g_ffn_in, (own_ffn_out, sums_ffn_out) = _wgrad(
        dgu, h2, "wgrad_ffn_in", 1408, 512, exchange=_PairReduce([by_name["w_ffn_out"]]), exchange_operands=[g_ffn_out])
    g_o, (own_ffn_in, sums_ffn_in) = _wgrad(
        mix, dx2b, "wgrad_o", 1024, 256, exchange=_PairReduce([by_name["w_ffn_in"]]), exchange_operands=[g_ffn_in])
    (own_o,), (sums_o,) = pair_sums(("w_o",), [g_o], "o")
    part_a = [g_ffn_in, g_ffn_out, g_o]
    lz1_a, sums_a = [own_ffn_in, own_ffn_out, own_o], [sums_ffn_in, sums_ffn_out, sums_o]
    (dproj, dypb, dyrb, dmat, dvec_mix), lz2_a = _mixer_bwd(
        proj, dmixo, y_pool, y_rnn, hr, kept, gates, *mixer_weights,
        exchange=_Scatter([by_name[n] for n in names_a]), exchange_operands=sums_a)
    names_b = ("w_pool_out", "w_rnn_out")
    part_b = [_wgrad(pm, dypb, "wgrad_pool_out", 512, 256), _wgrad(z, dyrb, "wgrad_rnn_out", 1024, 256)]
    lz1_b, sums_b = pair_sums(names_b, part_b, "mix")
    dvec = jnp.concatenate([dvec_mix[0:9], dvec_fin[0:1], dvec_ffn[0:1], jnp.pad(loss_part, ((0, 0), (0, DR - 1))),
                            jnp.zeros((VEC_ROWS - 12, DR), F32)], axis=0)
    g_in, exchanged = _wgrad(
        dproj, h1, "wgrad_in", 1152, 1024,
        exchange=_Both(_Scatter([by_name[n] for n in names_b]), _AllReduce([((MAT_ROWS, HD), 0), ((VEC_ROWS, DR), 1)])),
        exchange_operands=sums_b + [dmat, dvec])
    lz2_b, (mat, vec) = exchanged[:2], exchanged[2:]
    loss = vec[VEC_LOSS, 0]
    (grad_x, dvec_in), (own_in, _, scattered_in) = _in_bwd(
        dproj, xs, dx2, norm_mix, w_in_g, exchange=_ReduceScatter([by_name["w_in"]]), exchange_operands=[g_in])
    lz1_c, lz2_c = [own_in], [scattered_in]
    (vec_in,) = _all_reduce([dvec_in], [((8, D), 1)], "all_reduce_norm_mix")

    grads, delta, new_m, new_v = {}, {}, {}, {}
    for n, g, l1, l2 in zip(names_a + names_b + ("w_in",), part_a + part_b + [g_in], lz1_a + lz1_b + lz1_c,
                            list(lz2_a) + list(lz2_b) + lz2_c):
        T = by_name[n]
        out = _final_sum(T, g, l1, l2, where, held(T, w[n][0]), held(T, m[n][0]), held(T, v[n][0]))
        grads[n], delta[n], new_m[n], new_v[n] = (held(T, a) for a in out)
    me = 4 * xi + 2 * yi + ci
    small_grads = dict(
        w_pool_grp=mat[0:MAT_WA], w_rg_a=mat[MAT_WA:MAT_WX], w_rg_x=mat[MAT_WX:MAT_ROWS],
        pool_scale=vec[VEC_SCALE:VEC_SCALE + 1, 0:DP], conv_b=vec[VEC_CONV_B:VEC_CONV_B + 1],
        b_rg_a=vec[VEC_BA:VEC_BA + 1], b_rg_x=vec[VEC_BX:VEC_BX + 1], lru_lambda=vec[VEC_LAM:VEC_LAM + 1],
        conv_w=lax.dynamic_slice(vec, (VEC_CONV_W, VEC_PIECE * me), (4, VEC_PIECE)),
        norm_final=vec[VEC_NORM_FINAL:VEC_NORM_FINAL + 1], norm_ffn=vec[VEC_NORM_FFN:VEC_NORM_FFN + 1],
        norm_mix=vec_in[0:1])
    names = list(small_grads)
    as2d = lambda a, g: a.reshape(g.shape)
    upd = _adam_small([small_grads[n] for n in names],
                      [(as2d(w[n], small_grads[n]), as2d(m[n], small_grads[n]), as2d(v[n], small_grads[n])) for n in names])
    for i, n in enumerate(names):
        grads[n] = small_grads[n]
        delta[n], new_m[n], new_v[n] = upd[3 * i:3 * i + 3]

    shaped = lambda d: [d[n].reshape(w[n].shape) for n in WEIGHT_NAMES]
    return (loss, grad_x[None], *shaped(grads), *shaped(delta), *shaped(new_m), *shaped(new_v))
```

```python
import math

import jax
import jax.numpy as jnp
from jax import lax
from jax.experimental import pallas as pl
from jax.experimental.pallas import tpu as pltpu

F32 = jnp.float32
BF16 = jnp.bfloat16

D = 1024
DP = 512
PG = 128
WINDOWS = (2, 4, 8, 16)
DR = 1024
NH = 8
HD = 128
DIN = 4608
DFF = 2816
EPS = 1e-6
LRU_C = 8.0
POOL_HALO = 16
CONV_HALO = 8
KEPT = 3

ADAM_LR = 0.001
ADAM_B1 = 0.9
ADAM_B2 = 0.999
ADAM_EPS = 1e-08
ADAM_WD = 0.01
ADAM_STEP = 10

VMEM_LIMIT = 56 * 1024 * 1024
MESH_AXES = ("x", "y", "c")
MESH = pl.DeviceIdType.MESH


def _dot(a, b):
    return jnp.dot(a, b, preferred_element_type=F32)


def _dot_nt(a, b):
    return lax.dot_general(a, b, (((1,), (1,)), ((), ())), preferred_element_type=F32)


def _dot_tn(a, b):
    return lax.dot_general(a, b, (((0,), (0,)), ((), ())), preferred_element_type=F32)


def _params(*sem):
    return pltpu.CompilerParams(dimension_semantics=sem, vmem_limit_bytes=VMEM_LIMIT)


def _resident(shape):
    nd = len(shape)
    return pl.BlockSpec(shape, lambda i: (0,) * nd, pipeline_mode=pl.Buffered(1))


def _rows(shape_cols, tm):
    return pl.BlockSpec((tm, shape_cols), lambda i: (i, 0))


def _call(body, name, grid, in_specs, out_specs, out_shape, operands, scratch_shapes=(), exchange=None, exchange_operands=()):
    n_in, n_out, n_scr = len(in_specs), len(out_specs), len(scratch_shapes)
    steps = math.prod(grid)
    if exchange is None:
        outs = pl.pallas_call(body, name=name, grid=grid, in_specs=in_specs, out_specs=out_specs, out_shape=out_shape,
                              scratch_shapes=list(scratch_shapes), compiler_params=_params(*["arbitrary"] * len(grid)))(*operands)
        return outs, []
    e_in, e_out = len(exchange.in_specs), len(exchange.out_specs)

    def hosted(*refs):
        ins, refs = refs[:n_in], refs[n_in:]
        e_ins, refs = refs[:e_in], refs[e_in:]
        outs, refs = refs[:n_out], refs[n_out:]
        e_outs, refs = refs[:e_out], refs[e_out:]
        scr, e_scr = refs[:n_scr], refs[n_scr:]
        step = pl.program_id(0)
        for axis in range(1, len(grid)):
            step = step * grid[axis] + pl.program_id(axis)
        @pl.when(step == 0)
        def _():
            _enter(exchange)
            exchange.start(e_ins, e_outs, e_scr)

        for at, middle in exchange.middles(steps):
            pl.when(step == at)(lambda middle=middle: middle(e_ins, e_outs, e_scr))
        body(*ins, *outs, *scr)
        pl.when(step == steps - 1)(lambda: exchange.finish(e_ins, e_outs, e_scr))

    outs = pl.pallas_call(
        hosted, name=name, grid=grid, in_specs=list(in_specs) + exchange.in_specs,
        out_specs=list(out_specs) + exchange.out_specs, out_shape=list(out_shape) + exchange.out_shape,
        scratch_shapes=list(scratch_shapes) + exchange.scratch_shapes,
        compiler_params=pltpu.CompilerParams(dimension_semantics=("arbitrary",) * len(grid), vmem_limit_bytes=VMEM_LIMIT,
                                             collective_id=exchange.collective_id))(*operands, *exchange_operands)
    return outs[:n_out], outs[n_out:]


def _enter(exchange):
    peers = exchange.peers()
    if peers:
        barrier = pltpu.get_barrier_semaphore()
        for peer in peers:
            pl.semaphore_signal(barrier, inc=1, device_id=peer, device_id_type=MESH)
        pl.semaphore_wait(barrier, len(peers))


GELU_C = math.sqrt(2.0 / math.pi)
GELU_K = 0.044715 * GELU_C


def _gelu(x, with_grad=False):
    x2 = x * x
    t = jnp.tanh(x * (GELU_C + GELU_K * x2))
    hx = 0.5 * x
    y = hx + hx * t
    if not with_grad:
        return y
    return y, 0.5 + 0.5 * t + hx * (1.0 - t * t) * (GELU_C + (3.0 * GELU_K) * x2)


def _softplus_neg(lam):
    z = jnp.exp(-jnp.abs(lam))
    u = 1.0 + z
    dlt = u - 1.0
    log1p = jnp.where(dlt == 0.0, z, jnp.log(u) * (z / jnp.where(dlt == 0.0, 1.0, dlt)))
    return jnp.maximum(-lam, 0.0) + log1p


def _sigmoid(x):
    return 0.5 * jnp.tanh(0.5 * x) + 0.5


def _linear_scan(out_ref, A, B, h0, reverse):
    n = A.shape[0]
    sub = lax.broadcasted_iota(jnp.int32, (8, 1), 0)
    tiles = range(n // 8 - 1, -1, -1) if reverse else range(n // 8)
    carry = h0
    for j in tiles:
        a, b = A[8 * j:8 * j + 8, :], B[8 * j:8 * j + 8, :]
        for d in (1, 2, 4):
            keep = (sub < 8 - d) if reverse else (sub >= d)
            shift = 8 - d if reverse else d
            b = jnp.where(keep, a * pltpu.roll(b, shift, axis=0) + b, b)
            a = jnp.where(keep, a * pltpu.roll(a, shift, axis=0), a)
        h = a * carry + b
        out_ref[8 * j:8 * j + 8, :] = h
        carry = h[0:1, :] if reverse else h[7:8, :]
    return carry


def _pool_windows(ext, shift_sign):
    n = ext.shape[0]
    s = ext
    outs = []
    for w in WINDOWS:
        d = w // 2
        s = s + pltpu.roll(s, d if shift_sign > 0 else n - d, axis=0)
        outs.append(s[:, :PG])
        s = s[:, PG:]
    return outs


def _conv_taps(uext):
    taps = []
    for k in range(4):
        sh = 3 - k
        v = uext if sh == 0 else pltpu.roll(uext, sh, axis=0)
        taps.append(v[CONV_HALO:, :])
    return taps


def _gates(v, wa_ref, ba_ref, wx_ref, bx_ref, sp):
    vb = v.astype(BF16)
    ra, rx = [], []
    for h in range(NH):
        vh = vb[:, h * HD:(h + 1) * HD]
        ra.append(_dot(vh, wa_ref[h]))
        rx.append(_dot(vh, wx_ref[h]))
    r = _sigmoid(jnp.concatenate(ra, axis=1) + ba_ref[...])
    i = _sigmoid(jnp.concatenate(rx, axis=1) + bx_ref[...])
    log_a = r * ((-LRU_C) * sp)
    a = jnp.exp(log_a)
    one_minus = -jnp.tanh(log_a) * (1.0 + a * a)
    return r, i, a, jnp.sqrt(one_minus), lax.rsqrt(one_minus)


def _mixer_fwd(proj, wg, scale, w_pool_out, conv_w, conv_b, wa, ba, wx, bx, lam, w_rnn_out, exchange=None,
               exchange_operands=(), tm=256):
    S = proj.shape[0]
    UW = DP + 2 * DR

    def body(proj_ref, wg_ref, scale_ref, wpo_ref, cw_ref, cb_ref, wa_ref, ba_ref, wx_ref, bx_ref, lam_ref, wro_ref,
             pm_ref, ypool_ref, hr_ref, z_ref, yrnn_ref, kept_ref, gates_ref, pool_carry, conv_carry, h_carry):
        i = pl.program_id(0)

        @pl.when(i == 0)
        def _():
            pool_carry[...] = jnp.zeros_like(pool_carry)
            conv_carry[...] = jnp.zeros_like(conv_carry)
            h_carry[...] = jnp.zeros_like(h_carry)

        rows = lax.broadcasted_iota(jnp.int32, (tm, 1), 0)
        t_glob = i * tm + rows

        u_pool = proj_ref[:, 0:DP]
        ext = jnp.concatenate([pool_carry[...], u_pool], axis=0)
        pool_carry[...] = u_pool[tm - POOL_HALO:, :]
        sums = _pool_windows(ext, +1)
        mixed = []
        for g, w in enumerate(WINDOWS):
            inv_cnt = 1.0 / jnp.minimum(t_glob + 1, w).astype(F32)
            pooled_g = sums[g][POOL_HALO:, :] * inv_cnt - u_pool[:, g * PG:(g + 1) * PG]
            mixed.append(_dot(pooled_g.astype(BF16), wg_ref[g]))
        pm = (jnp.concatenate(mixed, axis=1) * scale_ref[...]).astype(BF16)
        pm_ref[...] = pm
        ypool_ref[...] = _dot(pm, wpo_ref[...]).astype(BF16)

        u_rnn = proj_ref[:, DP:DP + DR]
        uext = jnp.concatenate([conv_carry[...], u_rnn], axis=0)
        conv_carry[...] = u_rnn[tm - CONV_HALO:, :]
        taps = _conv_taps(uext)
        v = cb_ref[...]
        for k in range(4):
            v = v + taps[k] * cw_ref[k:k + 1, :]
        sp = _softplus_neg(lam_ref[...])
        r, gi, a, mult, _ = _gates(v, wa_ref, ba_ref, wx_ref, bx_ref, sp)
        for k, kept in enumerate((v, a, mult)):
            kept_ref[k] = kept
        for k, kept in enumerate((r, gi)):
            gates_ref[k] = kept.astype(BF16)
        h_carry[0:1, :] = _linear_scan(hr_ref, a, mult * gi * v, h_carry[0:1, :], reverse=False)
        z = (hr_ref[...] * _gelu(proj_ref[:, DP + DR:UW])).astype(BF16)
        z_ref[...] = z
        yrnn_ref[...] = _dot(z, wro_ref[...]).astype(BF16)

    return _call(
        body, "mixer_fwd", (S // tm,),
        in_specs=[_rows(UW, tm), _resident((4, PG, PG)), _resident((1, DP)), _resident((DP, D)), _resident((4, DR)),
                  _resident((1, DR)), _resident((NH, HD, HD)), _resident((1, DR)), _resident((NH, HD, HD)),
                  _resident((1, DR)), _resident((1, DR)), _resident((DR, D))],
        out_specs=[_rows(DP, tm), _rows(D, tm), _rows(DR, tm), _rows(DR, tm), _rows(D, tm),
                   pl.BlockSpec((KEPT, tm, DR), lambda i: (0, i, 0)), pl.BlockSpec((2, tm, DR), lambda i: (0, i, 0))],
        out_shape=[jax.ShapeDtypeStruct((S, DP), BF16),
                   jax.ShapeDtypeStruct((S, D), BF16), jax.ShapeDtypeStruct((S, DR), F32),
                   jax.ShapeDtypeStruct((S, DR), BF16), jax.ShapeDtypeStruct((S, D), BF16),
                   jax.ShapeDtypeStruct((KEPT, S, DR), F32), jax.ShapeDtypeStruct((2, S, DR), BF16)],
        scratch_shapes=[pltpu.VMEM((POOL_HALO, DP), F32), pltpu.VMEM((CONV_HALO, DR), F32), pltpu.VMEM((8, DR), F32)],
        operands=(proj, wg, scale, w_pool_out, conv_w, conv_b, wa, ba, wx, bx, lam, w_rnn_out),
        exchange=exchange, exchange_operands=exchange_operands)


FF_CHUNKS = ((0, 768), (768, 1536), (1536, 2304), (2304, DFF))


def _rms(x):
    r = lax.rsqrt(jnp.mean(x * x, axis=-1, keepdims=True) + EPS)
    return r, x * r


def _rms_bwd(dh, g, r, xh):
    dxh = dh * g
    return r * (dxh - xh * jnp.mean(dxh * xh, axis=-1, keepdims=True))


def _merge_out(x, proj, y_pool, y_rnn, w_o, norm_ffn, exchange=None, exchange_operands=(), tm=512):
    S = x.shape[0]
    GL0 = (DP + 2 * DR) // 512

    def gl_spec(k):
        return pl.BlockSpec((tm, 512), lambda i: (i, GL0 + k))

    def body(x_ref, gl0, gl1, gl2, gl3, yp_ref, yr_ref, wo_ref, gf_ref, mix_ref, x2_ref, h2_ref):
        s_p = _sigmoid(jnp.concatenate([gl0[...], gl1[...]], axis=1))
        s_r = _sigmoid(jnp.concatenate([gl2[...], gl3[...]], axis=1))
        mix = (s_p * yp_ref[...].astype(F32) + s_r * yr_ref[...].astype(F32)).astype(BF16)
        mix_ref[...] = mix
        x2 = x_ref[...] + _dot(mix, wo_ref[...])
        x2_ref[...] = x2
        _, xh2 = _rms(x2)
        h2_ref[...] = (xh2 * gf_ref[...]).astype(BF16)

    return _call(
        body, "merge_out", (S // tm,),
        in_specs=[_rows(D, tm), gl_spec(0), gl_spec(1), gl_spec(2), gl_spec(3), _rows(D, tm), _rows(D, tm),
                  _resident((D, D)), _resident((1, D))],
        out_specs=[_rows(D, tm), _rows(D, tm), _rows(D, tm)],
        out_shape=[jax.ShapeDtypeStruct((S, D), BF16), jax.ShapeDtypeStruct((S, D), F32), jax.ShapeDtypeStruct((S, D), BF16)],
        operands=(x, proj, proj, proj, proj, y_pool, y_rnn, w_o, norm_ffn),
        exchange=exchange, exchange_operands=exchange_operands)


def _ffn_up(h2, w_lo, w_hi, exchange=None, exchange_operands=(), tm=512):
    S = h2.shape[0]
    HALF = D // 2

    def body(h_ref, lo_ref, hi_ref, back_ref, act_ref):
        h_lo, h_hi = h_ref[:, 0:HALF], h_ref[:, HALF:D]
        for c0, c1 in FF_CHUNKS:
            gate = _dot_nt(h_lo, lo_ref[c0:c1, :]) + _dot_nt(h_hi, hi_ref[c0:c1, :])
            up = _dot_nt(h_lo, lo_ref[DFF + c0:DFF + c1, :]) + _dot_nt(h_hi, hi_ref[DFF + c0:DFF + c1, :])
            sg = _sigmoid(gate)
            silu = gate * sg
            back_ref[:, c0:c1] = (up * (sg * (1.0 + gate * (1.0 - sg)))).astype(BF16)
            back_ref[:, DFF + c0:DFF + c1] = silu.astype(BF16)
            act_ref[:, c0:c1] = (silu * up).astype(BF16)

    return _call(
        body, "ffn_up", (S // tm,),
        in_specs=[_rows(D, tm), _resident((2 * DFF, HALF)), _resident((2 * DFF, HALF))],
        out_specs=[_rows(2 * DFF, tm), _rows(DFF, tm)],
        out_shape=[jax.ShapeDtypeStruct((S, 2 * DFF), BF16), jax.ShapeDtypeStruct((S, DFF), BF16)],
        operands=(h2, w_lo, w_hi), exchange=exchange, exchange_operands=exchange_operands)


def _ffn_down_loss(act, x2, target, w_ffn_out, norm_final, tm=512):
    S = act.shape[0]

    def body(act_ref, x2_ref, t_ref, w_ref, gn_ref, dx3_ref, dx3b_ref, loss_ref, dvec_ref):
        i = pl.program_id(0)

        @pl.when(i == 0)
        def _():
            loss_ref[...] = jnp.zeros_like(loss_ref)
            dvec_ref[...] = jnp.zeros_like(dvec_ref)

        x3 = x2_ref[...] + _dot(act_ref[...], w_ref[...])
        r3, xh3 = _rms(x3)
        g_fin = gn_ref[...]
        e = xh3 * g_fin - t_ref[...]
        loss_ref[...] += jnp.sum(e * e, axis=(0, 1), keepdims=True) * (0.5 / D)
        dy = e * (1.0 / D)
        dvec_ref[0:1, :] += jnp.sum(dy * xh3, axis=0, keepdims=True)
        dx3 = _rms_bwd(dy, g_fin, r3, xh3)
        dx3_ref[...] = dx3
        dx3b_ref[...] = dx3.astype(BF16)

    return pl.pallas_call(
        body, name="ffn_down_loss", grid=(S // tm,),
        in_specs=[_rows(DFF, tm), _rows(D, tm), _rows(D, tm), _resident((DFF, D)), _resident((1, D))],
        out_specs=[_rows(D, tm), _rows(D, tm), _resident((1, 1)), _resident((8, D))],
        out_shape=[jax.ShapeDtypeStruct((S, D), F32), jax.ShapeDtypeStruct((S, D), BF16),
                   jax.ShapeDtypeStruct((1, 1), F32), jax.ShapeDtypeStruct((8, D), F32)],
        compiler_params=_params("arbitrary"),
    )(act, x2, target, w_ffn_out, norm_final)


def _ffn_bwd_down(dx3b, gu, w_ffn_out, tm=512):
    S = dx3b.shape[0]

    def body(d_ref, back_ref, w_ref, dgu_ref):
        d = d_ref[...]
        for c0, c1 in FF_CHUNKS:
            dact = _dot_nt(d, w_ref[c0:c1, :])
            dgu_ref[:, c0:c1] = (dact * back_ref[:, c0:c1].astype(F32)).astype(BF16)
            dgu_ref[:, DFF + c0:DFF + c1] = (dact * back_ref[:, DFF + c0:DFF + c1].astype(F32)).astype(BF16)

    return pl.pallas_call(
        body, name="ffn_bwd_down", grid=(S // tm,),
        in_specs=[_rows(D, tm), _rows(2 * DFF, tm), _resident((DFF, D))],
        out_specs=_rows(2 * DFF, tm),
        out_shape=jax.ShapeDtypeStruct((S, 2 * DFF), BF16),
        compiler_params=_params("parallel"),
    )(dx3b, gu, w_ffn_out)


def _ffn_bwd_up(dgu, x2, dx3, w_lo, w_hi, norm_ffn, w_o, tm=512):
    S = dgu.shape[0]
    HALF = D // 2

    def body(dgu_ref, x2_ref, dx3_ref, lo_ref, hi_ref, gf_ref, wo_ref, dx2_ref, dx2b_ref, dmixo_ref, dvec_ref):
        i = pl.program_id(0)

        @pl.when(i == 0)
        def _():
            dvec_ref[...] = jnp.zeros_like(dvec_ref)

        dgate, dup = dgu_ref[:, 0:DFF], dgu_ref[:, DFF:2 * DFF]
        dh2 = jnp.concatenate([_dot(dgate, w[0:DFF, :]) + _dot(dup, w[DFF:2 * DFF, :]) for w in (lo_ref, hi_ref)], axis=1)
        r2, xh2 = _rms(x2_ref[...])
        dvec_ref[0:1, :] += jnp.sum(dh2 * xh2, axis=0, keepdims=True)
        dx2 = dx3_ref[...] + _rms_bwd(dh2, gf_ref[...], r2, xh2)
        dx2_ref[...] = dx2
        dx2b = dx2.astype(BF16)
        dx2b_ref[...] = dx2b
        dmixo_ref[...] = _dot_nt(dx2b, wo_ref[...]).astype(BF16)

    return pl.pallas_call(
        body, name="ffn_bwd_up", grid=(S // tm,),
        in_specs=[_rows(2 * DFF, tm), _rows(D, tm), _rows(D, tm), _resident((2 * DFF, HALF)), _resident((2 * DFF, HALF)),
                  _resident((1, D)), _resident((D, D))],
        out_specs=[_rows(D, tm), _rows(D, tm), _rows(D, tm), _resident((8, D))],
        out_shape=[jax.ShapeDtypeStruct((S, D), F32), jax.ShapeDtypeStruct((S, D), BF16), jax.ShapeDtypeStruct((S, D), BF16),
                   jax.ShapeDtypeStruct((8, D), F32)],
        compiler_params=_params("arbitrary"),
    )(dgu, x2, dx3, w_lo, w_hi, norm_ffn, w_o)


VEC_ROWS = 16
MAT_WA = 4 * PG
MAT_WX = MAT_WA + NH * HD
MAT_ROWS = MAT_WX + NH * HD


def _mixer_bwd(proj, dmixo, y_pool, y_rnn, hr, kept, gates, wg, scale, w_pool_out, conv_w, conv_b, wa, ba, wx, bx, lam, w_rnn_out,
               dvec_fin, dvec_ffn, loss_part, exchange=None, exchange_operands=(), tm=256):
    S = proj.shape[0]
    nt = S // tm

    def rev(cols):
        return pl.BlockSpec((tm, cols), lambda i: (nt - 1 - i, 0))

    def halo(rows_, cols):
        per = tm // rows_
        return pl.BlockSpec((rows_, cols), lambda i: (jnp.maximum((nt - 1 - i) * per - 1, 0), 0))

    def body(proj_ref, projh_ref, dmixo_ref, yp_ref, yr_ref, hr_ref, hrh_ref, kept_ref, gates_ref, wg_ref, scale_ref, wpo_ref, cw_ref, cb_ref,
             wa_ref, ba_ref, wx_ref, bx_ref, lam_ref, wro_ref, fin_ref, ffn_ref, loss_ref,
             dproj_ref, dypb_ref, dyrb_ref, dmat_ref, dvec_ref,
             q_carry, dv_carry, a_carry, g_carry, g_scr):
        i = pl.program_id(0)
        ti = nt - 1 - i

        @pl.when(i == 0)
        def _():
            q_carry[...] = jnp.zeros_like(q_carry)
            dv_carry[...] = jnp.zeros_like(dv_carry)
            a_carry[...] = jnp.zeros_like(a_carry)
            g_carry[...] = jnp.zeros_like(g_carry)
            dmat_ref[...] = jnp.zeros_like(dmat_ref)
            dvec_ref[...] = jnp.zeros_like(dvec_ref)

        rows = lax.broadcasted_iota(jnp.int32, (tm, 1), 0)
        t_glob = ti * tm + rows
        has_prev = (ti > 0).astype(F32)
        dmixo = dmixo_ref[...].astype(F32)

        s_p = _sigmoid(proj_ref[:, DP + 2 * DR:DP + 2 * DR + D])
        s_r = _sigmoid(proj_ref[:, DP + 2 * DR + D:DIN])
        dproj_ref[:, DP + 2 * DR:DP + 2 * DR + D] = (dmixo * yp_ref[...].astype(F32) * s_p * (1.0 - s_p)).astype(BF16)
        dproj_ref[:, DP + 2 * DR + D:DIN] = (dmixo * yr_ref[...].astype(F32) * s_r * (1.0 - s_r)).astype(BF16)
        dyp = (dmixo * s_p).astype(BF16)
        dyr = (dmixo * s_r).astype(BF16)
        dypb_ref[...] = dyp
        dyrb_ref[...] = dyr

        dz = _dot_nt(dyr, wro_ref[...])
        u_gate = proj_ref[:, DP + DR:DP + 2 * DR]
        gg, dgelu = _gelu(u_gate, with_grad=True)
        hr_t = hr_ref[...]
        dproj_ref[:, DP + DR:DP + 2 * DR] = (dz * hr_t * dgelu).astype(BF16)
        dhr = dz * gg

        sp = _softplus_neg(lam_ref[...])
        v, a, mult = (kept_ref[k] for k in range(KEPT))
        r, gi = (gates_ref[k].astype(F32) for k in range(2))
        inv_mult = 1.0 / mult

        C = jnp.where(rows == tm - 1, a_carry[0:1, :], pltpu.roll(a, tm - 1, axis=0))
        g_carry[0:1, :] = _linear_scan(g_scr, C, dhr, g_carry[0:1, :], reverse=True)
        a_carry[0:1, :] = a[0:1, :]
        g = g_scr[...]

        h_prev = jnp.where(rows == 0, hrh_ref[7:8, :] * has_prev, pltpu.roll(hr_t, 1, axis=0))
        da = g * h_prev
        gm = g * mult
        dmult = g * gi * v
        di = gm * v
        dv = gm * gi
        dlog_a = da * a - dmult * (a * a * inv_mult)
        dvec_ref[4:5, :] += jnp.sum(dlog_a * r, axis=0, keepdims=True)
        dra = (dlog_a * ((-LRU_C) * sp) * r * (1.0 - r))
        drx = di * gi * (1.0 - gi)
        dvec_ref[2:3, :] += jnp.sum(dra, axis=0, keepdims=True)
        dvec_ref[3:4, :] += jnp.sum(drx, axis=0, keepdims=True)
        drab = dra.astype(BF16)
        drxb = drx.astype(BF16)
        vb = v.astype(BF16)
        dvg = []
        for h in range(NH):
            sl = slice(h * HD, (h + 1) * HD)
            dvg.append(_dot_nt(drab[:, sl], wa_ref[h]) + _dot_nt(drxb[:, sl], wx_ref[h]))
            dmat_ref[MAT_WA + h * HD:MAT_WA + (h + 1) * HD, :] += _dot_tn(vb[:, sl], drab[:, sl])
            dmat_ref[MAT_WX + h * HD:MAT_WX + (h + 1) * HD, :] += _dot_tn(vb[:, sl], drxb[:, sl])
        dv = dv + jnp.concatenate(dvg, axis=1)
        dvec_ref[1:2, :] += jnp.sum(dv, axis=0, keepdims=True)
        dvext = jnp.concatenate([dv, dv_carry[...]], axis=0)
        dv_carry[...] = dv[0:CONV_HALO, :]
        n = tm + CONV_HALO
        u_rnn = proj_ref[:, DP:DP + DR]
        du_rnn = dv * cw_ref[3:4, :]
        dvec_ref[8:9, :] += jnp.sum(dv * u_rnn, axis=0, keepdims=True)
        for k in range(3):
            dv_k = pltpu.roll(dvext, n - (3 - k), axis=0)[0:tm, :]
            du_rnn = du_rnn + dv_k * cw_ref[k:k + 1, :]
            dvec_ref[5 + k:6 + k, :] += jnp.sum(dv_k * u_rnn, axis=0, keepdims=True)
        dproj_ref[:, DP:DP + DR] = du_rnn.astype(BF16)

        dpm = _dot_nt(dyp, wpo_ref[...])
        u_pool = proj_ref[:, 0:DP]
        ext = jnp.concatenate([projh_ref[:, 0:DP] * has_prev, u_pool], axis=0)
        sums = _pool_windows(ext, +1)
        scale_v = scale_ref[...]
        qs = []
        dpooled = []
        dscale = []
        for gi_, w in enumerate(WINDOWS):
            sl = slice(gi_ * PG, (gi_ + 1) * PG)
            inv_cnt = 1.0 / jnp.minimum(t_glob + 1, w).astype(F32)
            pooled_b = (sums[gi_][POOL_HALO:, :] * inv_cnt - u_pool[:, sl]).astype(BF16)
            mixed_g = _dot(pooled_b, wg_ref[gi_])
            dscale.append(jnp.sum(dpm[:, sl] * mixed_g, axis=0, keepdims=True))
            dmixed_b = (dpm[:, sl] * scale_v[:, sl]).astype(BF16)
            dmat_ref[gi_ * PG:(gi_ + 1) * PG, :] += _dot_tn(pooled_b, dmixed_b)
            dp_g = _dot_nt(dmixed_b, wg_ref[gi_])
            dpooled.append(dp_g)
            qs.append(dp_g * inv_cnt)
        dvec_ref[0:1, 0:DP] += jnp.concatenate(dscale, axis=1)
        q = jnp.concatenate(qs, axis=1)
        qext = jnp.concatenate([q, q_carry[...]], axis=0)
        q_carry[...] = q[0:POOL_HALO, :]
        tsum = _pool_windows(qext, -1)
        for gi_ in range(4):
            dproj_ref[:, gi_ * PG:(gi_ + 1) * PG] = (tsum[gi_][0:tm, :] - dpooled[gi_]).astype(BF16)

        @pl.when(i == nt - 1)
        def _():
            dvec_ref[4:5, :] = dvec_ref[4:5, :] * (LRU_C * _sigmoid(-lam_ref[...]))
            dvec_ref[VEC_NORM_FINAL:VEC_NORM_FINAL + 1, :] = fin_ref[0:1, :]
            dvec_ref[VEC_NORM_FFN:VEC_NORM_FFN + 1, :] = ffn_ref[0:1, :]
            lane = lax.broadcasted_iota(jnp.int32, (1, DR), 1)
            dvec_ref[VEC_LOSS:VEC_LOSS + 1, :] = jnp.where(lane == 0, jnp.broadcast_to(loss_ref[...], (1, DR)), 0.0)

    return _call(
        body, "mixer_bwd", (nt,),
        in_specs=[rev(DIN), halo(POOL_HALO, DIN), rev(D), rev(D), rev(D), rev(DR), halo(8, DR),
                  pl.BlockSpec((KEPT, tm, DR), lambda i: (0, nt - 1 - i, 0)),
                  pl.BlockSpec((2, tm, DR), lambda i: (0, nt - 1 - i, 0)), _resident((4, PG, PG)), _resident((1, DP)), _resident((DP, D)), _resident((4, DR)), _resident((1, DR)),
                  _resident((NH, HD, HD)), _resident((1, DR)), _resident((NH, HD, HD)), _resident((1, DR)),
                  _resident((1, DR)), _resident((DR, D)), _resident((8, D)), _resident((8, D)), _resident((1, 1))],
        out_specs=[rev(DIN), rev(D), rev(D), _resident((MAT_ROWS, HD)), _resident((VEC_ROWS, DR))],
        out_shape=[jax.ShapeDtypeStruct((S, DIN), BF16), jax.ShapeDtypeStruct((S, D), BF16),
                   jax.ShapeDtypeStruct((S, D), BF16), jax.ShapeDtypeStruct((MAT_ROWS, HD), F32),
                   jax.ShapeDtypeStruct((VEC_ROWS, DR), F32)],
        scratch_shapes=[pltpu.VMEM((POOL_HALO, DP), F32), pltpu.VMEM((CONV_HALO, DR), F32), pltpu.VMEM((8, DR), F32),
                        pltpu.VMEM((8, DR), F32), pltpu.VMEM((tm, DR), F32)],
        operands=(proj, proj, dmixo, y_pool, y_rnn, hr, hr, kept, gates, wg, scale, w_pool_out, conv_w, conv_b, wa, ba, wx, bx, lam,
                  w_rnn_out, dvec_fin, dvec_ffn, loss_part),
        exchange=exchange, exchange_operands=exchange_operands)


def _in_bwd(dproj, x, dx2, norm_mix, w_in, exchange=None, exchange_operands=(), tm=512):
    S = x.shape[0]

    def body(dp_ref, x_ref, dx2_ref, g_ref, w_ref, dx_ref, dg_ref):
        i = pl.program_id(0)

        @pl.when(i == 0)
        def _():
            dg_ref[...] = jnp.zeros_like(dg_ref)

        dh = _dot(dp_ref[:, 0:1536], w_ref[0:1536, :])
        dh = dh + _dot(dp_ref[:, 1536:3072], w_ref[1536:3072, :])
        dh = dh + _dot(dp_ref[:, 3072:DIN], w_ref[3072:DIN, :])
        xv = x_ref[...]
        r = lax.rsqrt(jnp.mean(xv * xv, axis=-1, keepdims=True) + EPS)
        xh = xv * r
        dg_ref[0:1, :] += jnp.sum(dh * xh, axis=0, keepdims=True)
        dxh = dh * g_ref[...]
        dx_ref[...] = dx2_ref[...] + r * (dxh - xh * jnp.mean(dxh * xh, axis=-1, keepdims=True))

    return _call(
        body, "in_bwd", (S // tm,),
        in_specs=[_rows(DIN, tm), _rows(D, tm), _rows(D, tm), _resident((1, D)), _resident((DIN, D))],
        out_specs=[_rows(D, tm), _resident((8, D))],
        out_shape=[jax.ShapeDtypeStruct((S, D), F32), jax.ShapeDtypeStruct((8, D), F32)],
        operands=(dproj, x, dx2, norm_mix, w_in), exchange=exchange, exchange_operands=exchange_operands)


def _wgrad(a, b, name, tk, tn, exchange=None, exchange_operands=()):
    S, K = a.shape
    N = b.shape[1]

    def body(a_ref, b_ref, o_ref):
        o_ref[...] = _dot_tn(a_ref[...], b_ref[...]).astype(BF16)

    (out,), exchanged = _call(
        body, name, (K // tk, N // tn),
        in_specs=[pl.BlockSpec((S, tk), lambda k, n: (0, k)), pl.BlockSpec((S, tn), lambda k, n: (0, n))],
        out_specs=[pl.BlockSpec((tk, tn), lambda k, n: (k, n))],
        out_shape=[jax.ShapeDtypeStruct((K, N), BF16)],
        operands=(a, b), exchange=exchange, exchange_operands=exchange_operands)
    return (out, exchanged) if exchange is not None else out


VEC_SCALE, VEC_CONV_B, VEC_BA, VEC_BX, VEC_LAM, VEC_CONV_W, VEC_NORM_FINAL, VEC_NORM_FFN = 0, 1, 2, 3, 4, 5, 9, 10
VEC_LOSS = 11


class _Big:
    def __init__(self, name, rows, cols, axis, n, dtype=BF16, transposed=False, src_cols=None):
        self.name, self.rows, self.cols, self.axis, self.n, self.dtype = name, rows, cols, axis, n, dtype
        self.transposed = transposed
        self.src_cols = src_cols
        self.block_shape = (rows, n) if axis == 1 else (n, cols)

    def block(self, ref, p):
        if self.axis == 1:
            return ref.at[:, pl.ds(pl.multiple_of(p * self.n, 128), self.n)]
        return ref.at[pl.ds(pl.multiple_of(p * self.n, 16 if self.dtype == BF16 else 8), self.n), :]


BIG = (_Big("w_in", DIN, D, 0, DIN // 8, transposed=True), _Big("w_pool_out", DP, D, 1, D // 8),
       _Big("w_rnn_out", DR, D, 0, DR // 8), _Big("w_o", D, D, 0, D // 8),
       _Big("w_ffn_in", 2 * DFF, D, 0, 2 * DFF // 8, transposed=True), _Big("w_ffn_out", DFF, D, 0, DFF // 8))
CONV_W = _Big("conv_w", 8, DR, 1, DR // 8, F32)
W_FFN_IN_HALVES = (_Big("w_ffn_in_lo", 2 * DFF, D // 2, 0, 2 * DFF // 8, src_cols=(0, D // 2)),
                   _Big("w_ffn_in_hi", 2 * DFF, D // 2, 0, 2 * DFF // 8, src_cols=(D // 2, D)))
GATHERED = BIG + (CONV_W,) + W_FFN_IN_HALVES

HBM_SPEC = pl.BlockSpec(memory_space=pl.ANY)
VMEM_SPEC = pl.BlockSpec(memory_space=pltpu.VMEM)


def _place():
    x, y, c = (lax.axis_index(a) for a in MESH_AXES)
    other_chips = [(1 - x, y), (x, 1 - y), (1 - x, 1 - y)]
    return x, y, c, other_chips


def _remote(src, dst, send_sems, recv_sems, idx, to):
    return pltpu.make_async_remote_copy(src_ref=src, dst_ref=dst, send_sem=send_sems.at[idx], recv_sem=recv_sems.at[idx],
                                        device_id=to, device_id_type=MESH)


def _device_index(chip, core):
    return 4 * chip[0] + 2 * chip[1] + core


class _Gather:
    def __init__(self, tensors):
        self.tensors = tuple(tensors)
        n = len(self.tensors)
        self.in_specs = [HBM_SPEC] * n
        self.out_specs = [HBM_SPEC] * n
        self.out_shape = [jax.ShapeDtypeStruct((T.rows, T.cols), T.dtype) for T in self.tensors]
        self.scratch_shapes = [pltpu.VMEM(T.block_shape, T.dtype) for T in self.tensors] + [
            pltpu.VMEM(T.block_shape, F32) for T in self.tensors] + [
            pltpu.SemaphoreType.DMA((n, 7)), pltpu.SemaphoreType.DMA((n, 7)), pltpu.SemaphoreType.DMA((n, 2))]

    collective_id = 1

    def peers(self):
        x, y, c, _ = _place()
        return [(x, y, 1 - c), (1 - x, y, c), (x, 1 - y, c)]

    def middles(self, steps):
        return [(steps // 2, self.relay), (steps - 1, self.middle)]

    def _copies(self, ins, outs, scratch):
        n = len(self.tensors)
        mine, raw, (send_sems, recv_sems, loc_sems) = scratch[:n], scratch[n:2 * n], scratch[2 * n:]
        x, y, c, chips = _place()
        sibling = (x, y, 1 - c)
        me = _device_index((x, y), c)
        relay_from = (jnp.where(c == 0, 1 - x, x), jnp.where(c == 0, y, 1 - y))
        relay_to = (jnp.where(c == 0, x, 1 - x), jnp.where(c == 0, 1 - y, y))
        loads, stores, first, relays, passed, arrivals, late = [], [], [], [], [], [], []
        for t, T in enumerate(self.tensors):
            place = T.block(outs[t], me)
            src = ins[t] if T.src_cols is None else ins[t].at[:, T.src_cols[0]:T.src_cols[1]]
            loads.append(pltpu.make_async_copy(src, raw[t], loc_sems.at[t, 0]))
            stores.append(pltpu.make_async_copy(mine[t], place, loc_sems.at[t, 1]))
            first.append(_remote(mine[t], place, send_sems, recv_sems, (t, 0), sibling))
            theirs = T.block(outs[t], _device_index((x, y), 1 - c))
            late.append(_remote(theirs, theirs, send_sems, recv_sems, (t, 0), sibling))
            relayed = T.block(outs[t], _device_index(relay_from, c))
            relays.append(_remote(relayed, relayed, send_sems, recv_sems, (t, 3), (*relay_to, c)))
            for k, chip in enumerate(chips):
                if k < 2:
                    first.append(_remote(mine[t], place, send_sems, recv_sems, (t, 1 + k), (*chip, c)))
                land = T.block(outs[t], _device_index(chip, c))
                arrivals.append(_remote(land, land, send_sems, recv_sems, (t, 1 + k), sibling))
                passed.append(_remote(land, land, send_sems, recv_sems, (t, 4 + k), sibling))
                theirs = T.block(outs[t], _device_index(chip, 1 - c))
                late.append(_remote(theirs, theirs, send_sems, recv_sems, (t, 4 + k), sibling))
        return loads, stores, first, relays, passed, arrivals, late

    def start(self, ins, outs, scratch):
        loads, stores, first, _, _, _, _ = self._copies(ins, outs, scratch)
        n = len(self.tensors)
        for cp in loads:
            cp.start()
        for t, cp in enumerate(loads):
            cp.wait()
            scratch[t][...] = scratch[n + t][...].astype(self.tensors[t].dtype)
        for cp in stores + first:
            cp.start()

    def relay(self, ins, outs, scratch, skip=0):
        _, _, _, relays, passed, arrivals, _ = self._copies(ins, outs, scratch)
        for t in range(skip, len(self.tensors)):
            arrivals[3 * t].wait_recv()
            arrivals[3 * t + 1].wait_recv()
            for cp in (relays[t], passed[3 * t], passed[3 * t + 1]):
                cp.start()

    def middle(self, ins, outs, scratch, skip=0):
        _, _, _, _, passed, arrivals, _ = self._copies(ins, outs, scratch)
        for t in range(skip, len(self.tensors)):
            arrivals[3 * t + 2].wait_recv()
            passed[3 * t + 2].start()

    def finish(self, ins, outs, scratch, skip=0):
        _, stores, first, relays, passed, _, late = self._copies(ins, outs, scratch)
        for cp in late[4 * skip:]:
            cp.wait_recv()
        for cp in first + relays + passed:
            cp.wait_send()
        for cp in stores[skip:]:
            cp.wait()


def _in_proj_gather(x, norm_mix, blocks, tensors, order, tm=1024):
    S = x.shape[0]
    nt = S // tm
    n = len(tensors)
    gather = _Gather(tensors)
    CB = 2 * tensors[0].n

    def body(order_ref, x_ref, g_ref, *refs):
        ins, (proj_ref, h_ref), outs = refs[:n], refs[n:n + 2], refs[n + 2:2 * n + 2]
        (h_all, w_chip, w_sem), scratch = refs[2 * n + 2:2 * n + 5], refs[2 * n + 5:]
        q, i = pl.program_id(0), pl.program_id(1)
        _, stores, _, relays, passed, arrivals, late = gather._copies(ins, outs, scratch)

        def fetch(turn):
            rows = outs[0].at[pl.ds(pl.multiple_of(order_ref[turn] * CB, 16), CB), :]
            cp = pltpu.make_async_copy(rows, w_chip, w_sem)
            cp.start()
            cp.wait()

        @pl.when((q == 0) & (i == 0))
        def _():
            _enter(gather)
            gather.start(ins, outs, scratch)
            late[0].wait_recv()
            stores[0].wait()
            fetch(0)

        @pl.when((q == 1) & (i == 0))
        def _():
            arrivals[0].wait_recv()
            arrivals[1].wait_recv()
            for cp in (relays[0], passed[0], passed[1]):
                cp.start()
            late[1].wait_recv()
            fetch(1)

        @pl.when((q == 2) & (i == 0))
        def _():
            late[2].wait_recv()
            fetch(2)
            gather.relay(ins, outs, scratch, skip=1)

        @pl.when((q == 3) & (i == 0))
        def _():
            arrivals[2].wait_recv()
            passed[2].start()
            late[3].wait_recv()
            fetch(3)

        rows = pl.ds(pl.multiple_of(i * tm, tm), tm)

        @pl.when(q == 0)
        def _():
            xv = x_ref[...]
            r = lax.rsqrt(jnp.mean(xv * xv, axis=-1, keepdims=True) + EPS)
            h = (xv * r * g_ref[...]).astype(BF16)
            h_all[rows, :] = h
            h_ref[...] = h

        proj_ref[...] = _dot_nt(h_all[rows, :], w_chip[...])

        @pl.when((q == 3) & (i == nt - 1))
        def _():
            gather.middle(ins, outs, scratch, skip=1)
            gather.finish(ins, outs, scratch, skip=1)

    row_tile = lambda q, i, order: (jnp.where(q == 0, i, nt - 1), 0)
    whole = lambda shape: pl.BlockSpec(shape, lambda q, i, order: (0,) * len(shape), pipeline_mode=pl.Buffered(1))
    outs = pl.pallas_call(
        body, name="in_proj_gather",
        grid_spec=pltpu.PrefetchScalarGridSpec(
            num_scalar_prefetch=1, grid=(4, nt),
            in_specs=[pl.BlockSpec((tm, D), row_tile), whole((1, D))] + gather.in_specs,
            out_specs=[pl.BlockSpec((tm, CB), lambda q, i, order: (i, order[q])), pl.BlockSpec((tm, D), row_tile)]
            + gather.out_specs,
            scratch_shapes=[pltpu.VMEM((S, D), BF16), pltpu.VMEM((CB, D), BF16), pltpu.SemaphoreType.DMA]
            + gather.scratch_shapes),
        out_shape=[jax.ShapeDtypeStruct((S, DIN), F32), jax.ShapeDtypeStruct((S, D), BF16)] + gather.out_shape,
        compiler_params=pltpu.CompilerParams(dimension_semantics=("arbitrary", "arbitrary"), vmem_limit_bytes=VMEM_LIMIT,
                                             collective_id=gather.collective_id),
    )(order, x, norm_mix, *blocks)
    return outs[:2], outs[2:]


PAIR_ROWS = 32


class _PairReduce:
    collective_id = 3

    def __init__(self, tensors):
        self.tensors = tuple(tensors)
        nt = len(self.tensors)
        blocks = [T.block_shape for T in self.tensors]
        self.in_specs = [HBM_SPEC] * nt
        self.out_specs = [HBM_SPEC] * (2 * nt)
        self.out_shape = ([jax.ShapeDtypeStruct(b, BF16) for b in blocks]
                          + [jax.ShapeDtypeStruct((3,) + b, BF16) for b in blocks])
        self.scratch_shapes = ([pltpu.VMEM((4,) + b, BF16) for b in blocks] + [pltpu.VMEM((3,) + b, BF16) for b in blocks]
                               + [pltpu.SemaphoreType.DMA((nt, 4)), pltpu.SemaphoreType.DMA((nt, 4)),
                                  pltpu.SemaphoreType.DMA((nt, 5))])

    def peers(self):
        x, y, c, _ = _place()
        return [(x, y, 1 - c)]

    def middles(self, steps):
        return []

    def _copies(self, ins, outs, scratch):
        nt = len(self.tensors)
        own_out, sums_out, landed, mine = outs[:nt], outs[nt:], scratch[:nt], scratch[nt:2 * nt]
        send_sems, recv_sems, loc_sems = scratch[2 * nt:]
        x, y, c, chips = _place()
        chip_of = [2 * chip[0] + chip[1] for chip in chips]
        swaps, loads, stores = [], [], []
        for t, T in enumerate(self.tensors):
            for j in range(4):
                swaps.append(_remote(T.block(ins[t], 2 * j + 1 - c), landed[t].at[j], send_sems, recv_sems, (t, j),
                                     (x, y, 1 - c)))
            for k in range(3):
                loads.append(pltpu.make_async_copy(T.block(ins[t], 2 * chip_of[k] + c), mine[t].at[k], loc_sems.at[t, k]))
            stores.append(pltpu.make_async_copy(mine[t], sums_out[t], loc_sems.at[t, 3]))
            stores.append(pltpu.make_async_copy(landed[t].at[2 * x + y], own_out[t], loc_sems.at[t, 4]))
        return swaps, loads, stores, landed, mine, chip_of

    def start(self, ins, outs, scratch):
        swaps, loads, _, _, _, _ = self._copies(ins, outs, scratch)
        for cp in swaps + loads:
            cp.start()

    def finish(self, ins, outs, scratch):
        swaps, loads, stores, landed, mine, chip_of = self._copies(ins, outs, scratch)
        for cp in loads:
            cp.wait()
        for cp in swaps:
            cp.wait_recv()
        for t, T in enumerate(self.tensors):
            for k in range(3):
                acc, got = mine[t].at[k], landed[t].at[chip_of[k]]

                def add(i, carry, acc=acc, got=got):
                    rows = pl.ds(pl.multiple_of(i * PAIR_ROWS, PAIR_ROWS), PAIR_ROWS)
                    acc[rows, :] = (acc[rows, :].astype(F32) + got[rows, :].astype(F32)).astype(BF16)
                    return carry

                lax.fori_loop(0, T.block_shape[0] // PAIR_ROWS, add, 0)
        for cp in stores:
            cp.start()
        for cp in swaps:
            cp.wait_send()
        for cp in stores:
            cp.wait()


def _pair_reduce(grads, tensors, name):
    reduce = _PairReduce(tensors)
    nt = len(reduce.tensors)

    def body(*refs):
        ins, outs, scratch = refs[:nt], refs[nt:3 * nt], refs[3 * nt:]
        _enter(reduce)
        reduce.start(ins, outs, scratch)
        reduce.finish(ins, outs, scratch)

    return pl.pallas_call(
        body, name=name, in_specs=reduce.in_specs, out_specs=reduce.out_specs, out_shape=reduce.out_shape,
        scratch_shapes=reduce.scratch_shapes,
        compiler_params=pltpu.CompilerParams(vmem_limit_bytes=VMEM_LIMIT, collective_id=reduce.collective_id),
    )(*grads)


class _Scatter:
    def __init__(self, tensors):
        self.tensors = tuple(tensors)
        n = len(self.tensors)
        self.in_specs = [HBM_SPEC] * n
        self.out_specs = [HBM_SPEC] * n
        self.out_shape = [jax.ShapeDtypeStruct((2,) + T.block_shape, BF16) for T in self.tensors]
        self.scratch_shapes = [pltpu.VMEM(T.block_shape, BF16) for T in self.tensors] * 2 + [
            pltpu.SemaphoreType.DMA((n, 3)), pltpu.SemaphoreType.DMA((n, 3)), pltpu.SemaphoreType.DMA((n,))]

    collective_id = 2

    def peers(self):
        x, y, c, _ = _place()
        return [(1 - x, y, c), (x, 1 - y, c)]

    def middles(self, steps):
        return [(steps // 2, self.middle)]

    def _copies(self, ins, outs, scratch):
        n = len(self.tensors)
        landed, mine, (send_sems, recv_sems, loc_sems) = scratch[:n], scratch[n:2 * n], scratch[2 * n:]
        x, y, c, _ = _place()
        direct = (jnp.where(c == 0, 1 - x, x), jnp.where(c == 0, y, 1 - y), c)
        other = (jnp.where(c == 0, x, 1 - x), jnp.where(c == 0, 1 - y, y), c)
        k_direct = jnp.where(c == 0, 0, 1)
        to_direct, legs, loads, combined, arrivals = [], [], [], [], []
        for t in range(n):
            to_direct.append(_remote(ins[t].at[k_direct], outs[t].at[0], send_sems, recv_sems, (t, 0), direct))
            legs.append(_remote(ins[t].at[2], landed[t], send_sems, recv_sems, (t, 2), direct))
            loads.append(pltpu.make_async_copy(ins[t].at[1 - k_direct], mine[t], loc_sems.at[t]))
            combined.append(_remote(mine[t], outs[t].at[1], send_sems, recv_sems, (t, 1), other))
            arrivals.append(_remote(landed[t], landed[t], send_sems, recv_sems, (t, 2), direct))
        return to_direct, legs, loads, combined, arrivals, landed, mine

    def start(self, ins, outs, scratch):
        to_direct, legs, loads, _, _, _, _ = self._copies(ins, outs, scratch)
        for cp in to_direct + legs + loads:
            cp.start()

    def middle(self, ins, outs, scratch):
        _, _, loads, combined, arrivals, landed, mine = self._copies(ins, outs, scratch)
        for t, T in enumerate(self.tensors):
            loads[t].wait()
            arrivals[t].wait_recv()
            acc, got = mine[t], landed[t]

            def add(i, carry, acc=acc, got=got):
                rows = pl.ds(pl.multiple_of(i * PAIR_ROWS, PAIR_ROWS), PAIR_ROWS)
                acc[rows, :] = (acc[rows, :].astype(F32) + got[rows, :].astype(F32)).astype(BF16)
                return carry

            lax.fori_loop(0, T.block_shape[0] // PAIR_ROWS, add, 0)
            combined[t].start()

    def finish(self, ins, outs, scratch):
        to_direct, legs, _, combined, _, _, _ = self._copies(ins, outs, scratch)
        for cp in to_direct + combined:
            cp.wait()
        for cp in legs:
            cp.wait_send()


class _ReduceScatter:
    collective_id = 1

    def __init__(self, tensors):
        self.first, self.second = _PairReduce(tensors), _Scatter(tensors)
        self.in_specs = self.first.in_specs
        self.out_specs = self.first.out_specs + self.second.out_specs
        self.out_shape = self.first.out_shape + self.second.out_shape
        self.scratch_shapes = self.first.scratch_shapes + self.second.scratch_shapes

    def peers(self):
        x, y, c, _ = _place()
        return [(x, y, 1 - c), (1 - x, y, c), (x, 1 - y, c)]

    def _parts(self, ins, outs, scratch):
        n_out, n_scr = len(self.first.out_specs), len(self.first.scratch_shapes)
        pair_sums = list(outs[n_out // 2:n_out])
        return (ins, outs[:n_out], scratch[:n_scr]), (pair_sums, outs[n_out:], scratch[n_scr:])

    def start(self, ins, outs, scratch):
        self.first.start(*self._parts(ins, outs, scratch)[0])

    def middles(self, steps):
        def hand_over(ins, outs, scratch):
            first, second = self._parts(ins, outs, scratch)
            self.first.finish(*first)
            self.second.start(*second)

        def relay(ins, outs, scratch):
            self.second.middle(*self._parts(ins, outs, scratch)[1])

        return [(steps // 4, hand_over), ((5 * steps) // 8, relay)]

    def finish(self, ins, outs, scratch):
        self.second.finish(*self._parts(ins, outs, scratch)[1])


def _adamw(w, g, m, v):
    m = ADAM_B1 * m + (1.0 - ADAM_B1) * g
    v = ADAM_B2 * v + (1.0 - ADAM_B2) * (g * g)
    m_hat = m / (1.0 - ADAM_B1 ** ADAM_STEP)
    v_hat = v / (1.0 - ADAM_B2 ** ADAM_STEP)
    delta = -ADAM_LR * (m_hat / (jnp.sqrt(v_hat) + ADAM_EPS) + ADAM_WD * w)
    return delta, m, v


def _final_sum(T, g, lz1, lz2, where, w, m, v):
    rows, cols = T.block_shape
    sub = 4 if T.axis == 0 and rows % 64 == 0 and rows > 256 else 1
    blk = (rows // sub, cols)

    def body(where_ref, g_ref, l1_ref, l2_ref, w_ref, m_ref, v_ref, g_out, d_out, m_out, v_out):
        tot = g_ref[...].astype(F32) + l1_ref[...].astype(F32)
        for k in range(2):
            tot = tot + l2_ref[k].astype(F32)
        g_out[...] = tot
        d_out[...], m_out[...], v_out[...] = _adamw(w_ref[...], tot, m_ref[...], v_ref[...])

    def in_whole(r, wh):
        p = wh[0]
        return (0, p) if T.axis == 1 else (p * sub + r, 0)

    own = pl.BlockSpec(blk, lambda r, wh: (r, 0))
    return pl.pallas_call(
        body, name="grad_final_" + T.name,
        grid_spec=pltpu.PrefetchScalarGridSpec(
            num_scalar_prefetch=1, grid=(sub,),
            in_specs=[pl.BlockSpec(blk, in_whole),
                      own,
                      pl.BlockSpec((2,) + blk, lambda r, wh: (0, r, 0)), own, own, own],
            out_specs=[own] * 4),
        out_shape=[jax.ShapeDtypeStruct(T.block_shape, F32)] * 4,
        compiler_params=_params("arbitrary"),
    )(where, g, lz1, lz2, w, m, v)


VEC_PIECE = DR // 8


class _AllReduce:
    def __init__(self, items):
        self.items = tuple(items)
        n = len(self.items)
        self.in_specs = [HBM_SPEC] * n
        self.out_specs = [HBM_SPEC] * n
        self.out_shape = [jax.ShapeDtypeStruct(shape, F32) for shape, _ in self.items]
        pieces = [(shape[0] // 8, shape[1]) if axis == 0 else (shape[0], shape[1] // 8) for shape, axis in self.items]
        self.scratch_shapes = ([pltpu.VMEM((8,) + p, F32) for p in pieces] + [pltpu.VMEM(p, F32) for p in pieces] + [
            pltpu.SemaphoreType.DMA((2 * n, 8)), pltpu.SemaphoreType.DMA((2 * n, 8)), pltpu.SemaphoreType.DMA((2 * n,))])

    collective_id = None

    def peers(self):
        return []

    def middles(self, steps):
        return [(steps // 2, self.middle)]

    def _copies(self, ins, outs, scratch):
        n = len(self.items)
        landed, sums, (send_sems, recv_sems, loc_sems) = scratch[:n], scratch[n:2 * n], scratch[2 * n:]
        x, y, c, _ = _place()
        me = _device_index((x, y), c)

        def peer(r):
            return (1 - x if r & 4 else x, 1 - y if r & 2 else y, 1 - c if r & 1 else c)

        def piece(i, ref, p):
            shape, axis = self.items[i]
            if axis == 0:
                rows = shape[0] // 8
                return ref.at[pl.ds(pl.multiple_of(p * rows, 8), rows), :]
            cols = shape[1] // 8
            return ref.at[:, pl.ds(pl.multiple_of(p * cols, 128), cols)]

        own, scatter, arrivals, keep, spread, late = [], [], [], [], [], []
        for i in range(n):
            own.append(pltpu.make_async_copy(piece(i, ins[i], me), landed[i].at[0], loc_sems.at[2 * i]))
            keep.append(pltpu.make_async_copy(sums[i], piece(i, outs[i], me), loc_sems.at[2 * i + 1]))
            for r in range(1, 8):
                to = peer(r)
                p = _device_index(to[:2], to[2])
                scatter.append(_remote(piece(i, ins[i], p), landed[i].at[r], send_sems, recv_sems, (2 * i, r), to))
                spread.append(_remote(sums[i], piece(i, outs[i], me), send_sems, recv_sems, (2 * i + 1, r), to))
                late.append(_remote(sums[i], piece(i, outs[i], p), send_sems, recv_sems, (2 * i + 1, r), to))
        return own, scatter, keep, spread, late, landed, sums

    def start(self, ins, outs, scratch):
        own, scatter, _, _, _, _, _ = self._copies(ins, outs, scratch)
        for cp in own + scatter:
            cp.start()

    def middle(self, ins, outs, scratch):
        own, scatter, keep, spread, _, landed, sums = self._copies(ins, outs, scratch)
        for cp in own:
            cp.wait()
        for cp in scatter:
            cp.wait_recv()
        for i in range(len(self.items)):
            total = landed[i][0]
            for r in range(1, 8):
                total = total + landed[i][r]
            sums[i][...] = total
        for cp in keep + spread:
            cp.start()

    def finish(self, ins, outs, scratch):
        _, scatter, keep, spread, late, _, _ = self._copies(ins, outs, scratch)
        for cp in late:
            cp.wait_recv()
        for cp in scatter + spread:
            cp.wait_send()
        for cp in keep:
            cp.wait()


class _Both:
    def __init__(self, a, b):
        self.a, self.b = a, b
        self.in_specs, self.out_specs = a.in_specs + b.in_specs, a.out_specs + b.out_specs
        self.out_shape, self.scratch_shapes = a.out_shape + b.out_shape, a.scratch_shapes + b.scratch_shapes

    collective_id = None

    def peers(self):
        return []

    def _each(self, ins, outs, scratch):
        a = self.a
        i, o, s = len(a.in_specs), len(a.out_specs), len(a.scratch_shapes)
        return (a, ins[:i], outs[:o], scratch[:s]), (self.b, ins[i:], outs[o:], scratch[s:])

    def middles(self, steps):
        def of(which, middle):
            return lambda ins, outs, scratch: middle(*self._each(ins, outs, scratch)[which][1:])
        return [(at, of(which, middle)) for which, e in enumerate((self.a, self.b)) for at, middle in e.middles(steps)]

    def start(self, ins, outs, scratch):
        for e, i, o, s in self._each(ins, outs, scratch):
            e.start(i, o, s)

    def finish(self, ins, outs, scratch):
        for e, i, o, s in self._each(ins, outs, scratch):
            e.finish(i, o, s)


def _all_reduce(arrays, items, name):
    reduce = _AllReduce(items)
    n = len(items)

    def body(*refs):
        ins, outs, scratch = refs[:n], refs[n:2 * n], refs[2 * n:]
        reduce.start(ins, outs, scratch)
        reduce.middle(ins, outs, scratch)
        reduce.finish(ins, outs, scratch)

    return pl.pallas_call(
        body, name=name, in_specs=reduce.in_specs, out_specs=reduce.out_specs, out_shape=reduce.out_shape,
        scratch_shapes=reduce.scratch_shapes,
    )(*arrays)


def _adam_small(grads, wmv):
    n = len(grads)

    def body(*refs):
        g_refs, rest = refs[:n], refs[n:]
        ins, outs = rest[:3 * n], rest[3 * n:]
        for i in range(n):
            d, m, v = _adamw(ins[3 * i][...], g_refs[i][...], ins[3 * i + 1][...], ins[3 * i + 2][...])
            outs[3 * i][...], outs[3 * i + 1][...], outs[3 * i + 2][...] = d, m, v

    flat = [a for t in wmv for a in t]
    return pl.pallas_call(
        body, name="adam_small",
        in_specs=[VMEM_SPEC] * (4 * n), out_specs=[VMEM_SPEC] * (3 * n),
        out_shape=[jax.ShapeDtypeStruct(a.shape, F32) for a in flat],
    )(*grads, *flat)


WEIGHT_NAMES = ("norm_mix", "w_in", "w_pool_grp", "pool_scale", "w_pool_out", "conv_w", "conv_b", "w_rg_a", "b_rg_a", "w_rg_x",
                "b_rg_x", "lru_lambda", "w_rnn_out", "w_o", "norm_ffn", "w_ffn_in", "w_ffn_out", "norm_final")


def kernel(x, norm_mix, w_in, w_pool_grp, pool_scale, w_pool_out, conv_w, conv_b, w_rg_a, b_rg_a, w_rg_x, b_rg_x, lru_lambda, w_rnn_out, w_o, norm_ffn, w_ffn_in, w_ffn_out, norm_final, loss_target, m_norm_mix, m_w_in, m_w_pool_grp, m_pool_scale, m_w_pool_out, m_conv_w, m_conv_b, m_w_rg_a, m_b_rg_a, m_w_rg_x, m_b_rg_x, m_lru_lambda, m_w_rnn_out, m_w_o, m_norm_ffn, m_w_ffn_in, m_w_ffn_out, m_norm_final, v_norm_mix, v_w_in, v_w_pool_grp, v_pool_scale, v_w_pool_out, v_conv_w, v_conv_b, v_w_rg_a, v_b_rg_a, v_w_rg_x, v_b_rg_x, v_lru_lambda, v_w_rnn_out, v_w_o, v_norm_ffn, v_w_ffn_in, v_w_ffn_out, v_norm_final):
    w = dict(norm_mix=norm_mix, w_in=w_in, w_pool_grp=w_pool_grp, pool_scale=pool_scale, w_pool_out=w_pool_out, conv_w=conv_w,
             conv_b=conv_b, w_rg_a=w_rg_a, b_rg_a=b_rg_a, w_rg_x=w_rg_x, b_rg_x=b_rg_x, lru_lambda=lru_lambda,
             w_rnn_out=w_rnn_out, w_o=w_o, norm_ffn=norm_ffn, w_ffn_in=w_ffn_in, w_ffn_out=w_ffn_out, norm_final=norm_final)
    m = dict(norm_mix=m_norm_mix, w_in=m_w_in, w_pool_grp=m_w_pool_grp, pool_scale=m_pool_scale, w_pool_out=m_w_pool_out,
             conv_w=m_conv_w, conv_b=m_conv_b, w_rg_a=m_w_rg_a, b_rg_a=m_b_rg_a, w_rg_x=m_w_rg_x, b_rg_x=m_b_rg_x,
             lru_lambda=m_lru_lambda, w_rnn_out=m_w_rnn_out, w_o=m_w_o, norm_ffn=m_norm_ffn, w_ffn_in=m_w_ffn_in,
             w_ffn_out=m_w_ffn_out, norm_final=m_norm_final)
    v = dict(norm_mix=v_norm_mix, w_in=v_w_in, w_pool_grp=v_w_pool_grp, pool_scale=v_pool_scale, w_pool_out=v_w_pool_out,
             conv_w=v_conv_w, conv_b=v_conv_b, w_rg_a=v_w_rg_a, b_rg_a=v_b_rg_a, w_rg_x=v_w_rg_x, b_rg_x=v_b_rg_x,
             lru_lambda=v_lru_lambda, w_rnn_out=v_w_rnn_out, w_o=v_w_o, norm_ffn=v_norm_ffn, w_ffn_in=v_w_ffn_in,
             w_ffn_out=v_w_ffn_out, norm_final=v_norm_final)
    xi, yi, ci = (lax.axis_index(a) for a in MESH_AXES)
    chip = 2 * xi + yi

    def held(T, a):
        return jnp.swapaxes(a, 0, 1) if T.transposed else a

    where = jnp.stack([2 * chip + ci]).astype(jnp.int32)
    by_name = {T.name: T for T in GATHERED}
    block = {T.name: held(T, w[T.name][0]) for T in BIG}
    block["conv_w"] = jnp.pad(conv_w[0], ((0, CONV_W.rows - 4), (0, 0)))
    block["w_ffn_in_lo"] = block["w_ffn_in_hi"] = block["w_ffn_in"]

    def gather_of(*names):
        return dict(exchange=_Gather([by_name[n] for n in names]), exchange_operands=[block[n] for n in names])

    def pair_sums(names, partials, tag):
        out = _pair_reduce(partials, [by_name[n] for n in names], "grad_pair_reduce_" + tag)
        return list(out[:len(names)]), list(out[len(names):])

    xs, target = x[0], loss_target[0]
    wg_b, wa_b, wx_b = (a[0].astype(BF16) for a in (w_pool_grp, w_rg_a, w_rg_x))
    ba2, bx2 = b_rg_a.reshape(1, DR), b_rg_x.reshape(1, DR)
    first = ("w_in", "w_pool_out", "w_rnn_out", "conv_w", "w_o")
    order = jnp.stack([chip, 2 * (1 - xi) + yi, 2 * xi + (1 - yi), 2 * (1 - xi) + (1 - yi)]).astype(jnp.int32)
    (proj, h1), (w_in_g, w_pool_out_g, w_rnn_out_g, conv_g, w_o_g) = _in_proj_gather(
        xs, norm_mix, [block[n] for n in first], [by_name[n] for n in first], order)
    mixer_weights = (wg_b, pool_scale, w_pool_out_g, conv_g[0:4], conv_b, wa_b, ba2, wx_b, bx2, lru_lambda, w_rnn_out_g)
    (pm, y_pool, hr, z, y_rnn, kept, gates), (w_ffn_lo_g, w_ffn_hi_g) = _mixer_fwd(
        proj, *mixer_weights, **gather_of("w_ffn_in_lo", "w_ffn_in_hi"))
    (mix, x2, h2), _ = _merge_out(xs, proj, y_pool, y_rnn, w_o_g, norm_ffn)
    (gu, act), (w_ffn_out_g,) = _ffn_up(h2, w_ffn_lo_g, w_ffn_hi_g, **gather_of("w_ffn_out"))
    dx3, dx3b, loss_part, dvec_fin = _ffn_down_loss(act, x2, target, w_ffn_out_g, norm_final.reshape(1, D))

    dgu = _ffn_bwd_down(dx3b, gu, w_ffn_out_g)
    dx2, dx2b, dmixo, dvec_ffn = _ffn_bwd_up(dgu, x2, dx3, w_ffn_lo_g, w_ffn_hi_g, norm_ffn, w_o_g)
    names_a = ("w_ffn_in", "w_ffn_out", "w_o")
    g_ffn_out = _wgrad(act, dx3b, "wgrad_ffn_out", 1408, 512)
    g_ffn_in, (own_ffn_out, sums_ffn_out) = _wgrad(
        dgu, h2, "wgrad_ffn_in", 1408, 512, exchange=_PairReduce([by_name["w_ffn_out"]]), exchange_operands=[g_ffn_out])
    g_o, (own_ffn_in, sums_ffn_in) = _wgrad(
        mix, dx2b, "wgrad_o", 1024, 256, exchange=_PairReduce([by_name["w_ffn_in"]]), exchange_operands=[g_ffn_in])
    (own_o,), (sums_o,) = pair_sums(("w_o",), [g_o], "o")
    part_a = [g_ffn_in, g_ffn_out, g_o]
    lz1_a, sums_a = [own_ffn_in, own_ffn_out, own_o], [sums_ffn_in, sums_ffn_out, sums_o]
    (dproj, dypb, dyrb, dmat, dvec), lz2_a = _mixer_bwd(
        proj, dmixo, y_pool, y_rnn, hr, kept, gates, *mixer_weights, dvec_fin, dvec_ffn, loss_part,
        exchange=_Scatter([by_name[n] for n in names_a]), exchange_operands=sums_a)
    names_b = ("w_pool_out", "w_rnn_out")
    part_b = [_wgrad(pm, dypb, "wgrad_pool_out", 512, 256), _wgrad(z, dyrb, "wgrad_rnn_out", 1024, 256)]
    lz1_b, sums_b = pair_sums(names_b, part_b, "mix")
    g_in, exchanged = _wgrad(
        dproj, h1, "wgrad_in", 1152, 1024,
        exchange=_Both(_Scatter([by_name[n] for n in names_b]), _AllReduce([((MAT_ROWS, HD), 0), ((VEC_ROWS, DR), 1)])),
        exchange_operands=sums_b + [dmat, dvec])
    lz2_b, (mat, vec) = exchanged[:2], exchanged[2:]
    loss = vec[VEC_LOSS, 0]
    (grad_x, dvec_in), (own_in, _, scattered_in) = _in_bwd(
        dproj, xs, dx2, norm_mix, w_in_g, exchange=_ReduceScatter([by_name["w_in"]]), exchange_operands=[g_in])
    lz1_c, lz2_c = [own_in], [scattered_in]
    (vec_in,) = _all_reduce([dvec_in], [((8, D), 1)], "all_reduce_norm_mix")

    grads, delta, new_m, new_v = {}, {}, {}, {}
    for n, g, l1, l2 in zip(names_a + names_b + ("w_in",), part_a + part_b + [g_in], lz1_a + lz1_b + lz1_c,
                            list(lz2_a) + list(lz2_b) + lz2_c):
        T = by_name[n]
        out = _final_sum(T, g, l1, l2, where, held(T, w[n][0]), held(T, m[n][0]), held(T, v[n][0]))
        grads[n], delta[n], new_m[n], new_v[n] = (held(T, a) for a in out)
    me = 4 * xi + 2 * yi + ci
    small_grads = dict(
        w_pool_grp=mat[0:MAT_WA], w_rg_a=mat[MAT_WA:MAT_WX], w_rg_x=mat[MAT_WX:MAT_ROWS],
        pool_scale=vec[VEC_SCALE:VEC_SCALE + 1, 0:DP], conv_b=vec[VEC_CONV_B:VEC_CONV_B + 1],
        b_rg_a=vec[VEC_BA:VEC_BA + 1], b_rg_x=vec[VEC_BX:VEC_BX + 1], lru_lambda=vec[VEC_LAM:VEC_LAM + 1],
        conv_w=lax.dynamic_slice(vec, (VEC_CONV_W, VEC_PIECE * me), (4, VEC_PIECE)),
        norm_final=vec[VEC_NORM_FINAL:VEC_NORM_FINAL + 1], norm_ffn=vec[VEC_NORM_FFN:VEC_NORM_FFN + 1],
        norm_mix=vec_in[0:1])
    names = list(small_grads)
    as2d = lambda a, g: a.reshape(g.shape)
    upd = _adam_small([small_grads[n] for n in names],
                      [(as2d(w[n], small_grads[n]), as2d(m[n], small_grads[n]), as2d(v[n], small_grads[n])) for n in names])
    for i, n in enumerate(names):
        grads[n] = small_grads[n]
        delta[n], new_m[n], new_v[n] = upd[3 * i:3 * i + 3]

    shaped = lambda d: [d[n].reshape(w[n].shape) for n in WEIGHT_NAMES]
    return (loss, grad_x[None], *shaped(grads), *shaped(delta), *shaped(new_m), *shaped(new_v))
```

```python
import math

import jax
import jax.numpy as jnp
from jax import lax
from jax.experimental import pallas as pl
from jax.experimental.pallas import tpu as pltpu

F32 = jnp.float32
BF16 = jnp.bfloat16

D = 1024
DP = 512
PG = 128
WINDOWS = (2, 4, 8, 16)
DR = 1024
NH = 8
HD = 128
DIN = 4608
DFF = 2816
EPS = 1e-6
LRU_C = 8.0
POOL_HALO = 16
CONV_HALO = 8
KEPT = 3

ADAM_LR = 0.001
ADAM_B1 = 0.9
ADAM_B2 = 0.999
ADAM_EPS = 1e-08
ADAM_WD = 0.01
ADAM_STEP = 10

VMEM_LIMIT = 56 * 1024 * 1024
MESH_AXES = ("x", "y", "c")
MESH = pl.DeviceIdType.MESH


def _dot(a, b):
    return jnp.dot(a, b, preferred_element_type=F32)


def _dot_nt(a, b):
    return lax.dot_general(a, b, (((1,), (1,)), ((), ())), preferred_element_type=F32)


def _dot_tn(a, b):
    return lax.dot_general(a, b, (((0,), (0,)), ((), ())), preferred_element_type=F32)


def _params(*sem):
    return pltpu.CompilerParams(dimension_semantics=sem, vmem_limit_bytes=VMEM_LIMIT)


def _resident(shape):
    nd = len(shape)
    return pl.BlockSpec(shape, lambda i: (0,) * nd, pipeline_mode=pl.Buffered(1))


def _rows(shape_cols, tm):
    return pl.BlockSpec((tm, shape_cols), lambda i: (i, 0))


def _call(body, name, grid, in_specs, out_specs, out_shape, operands, scratch_shapes=(), exchange=None, exchange_operands=()):
    n_in, n_out, n_scr = len(in_specs), len(out_specs), len(scratch_shapes)
    steps = math.prod(grid)
    if exchange is None:
        outs = pl.pallas_call(body, name=name, grid=grid, in_specs=in_specs, out_specs=out_specs, out_shape=out_shape,
                              scratch_shapes=list(scratch_shapes), compiler_params=_params(*["arbitrary"] * len(grid)))(*operands)
        return outs, []
    e_in, e_out = len(exchange.in_specs), len(exchange.out_specs)

    def hosted(*refs):
        ins, refs = refs[:n_in], refs[n_in:]
        e_ins, refs = refs[:e_in], refs[e_in:]
        outs, refs = refs[:n_out], refs[n_out:]
        e_outs, refs = refs[:e_out], refs[e_out:]
        scr, e_scr = refs[:n_scr], refs[n_scr:]
        step = pl.program_id(0)
        for axis in range(1, len(grid)):
            step = step * grid[axis] + pl.program_id(axis)
        @pl.when(step == 0)
        def _():
            _enter(exchange)
            exchange.start(e_ins, e_outs, e_scr)

        for at, middle in exchange.middles(steps):
            pl.when(step == at)(lambda middle=middle: middle(e_ins, e_outs, e_scr))
        body(*ins, *outs, *scr)
        pl.when(step == steps - 1)(lambda: exchange.finish(e_ins, e_outs, e_scr))

    outs = pl.pallas_call(
        hosted, name=name, grid=grid, in_specs=list(in_specs) + exchange.in_specs,
        out_specs=list(out_specs) + exchange.out_specs, out_shape=list(out_shape) + exchange.out_shape,
        scratch_shapes=list(scratch_shapes) + exchange.scratch_shapes,
        compiler_params=pltpu.CompilerParams(dimension_semantics=("arbitrary",) * len(grid), vmem_limit_bytes=VMEM_LIMIT,
                                             collective_id=exchange.collective_id))(*operands, *exchange_operands)
    return outs[:n_out], outs[n_out:]


def _enter(exchange):
    peers = exchange.peers()
    if peers:
        barrier = pltpu.get_barrier_semaphore()
        for peer in peers:
            pl.semaphore_signal(barrier, inc=1, device_id=peer, device_id_type=MESH)
        pl.semaphore_wait(barrier, len(peers))


GELU_C = math.sqrt(2.0 / math.pi)
GELU_K = 0.044715 * GELU_C


def _gelu(x, with_grad=False):
    x2 = x * x
    t = jnp.tanh(x * (GELU_C + GELU_K * x2))
    hx = 0.5 * x
    y = hx + hx * t
    if not with_grad:
        return y
    return y, 0.5 + 0.5 * t + hx * (1.0 - t * t) * (GELU_C + (3.0 * GELU_K) * x2)


def _softplus_neg(lam):
    z = jnp.exp(-jnp.abs(lam))
    u = 1.0 + z
    dlt = u - 1.0
    log1p = jnp.where(dlt == 0.0, z, jnp.log(u) * (z / jnp.where(dlt == 0.0, 1.0, dlt)))
    return jnp.maximum(-lam, 0.0) + log1p


def _sigmoid(x):
    return 0.5 * jnp.tanh(0.5 * x) + 0.5


def _linear_scan(out_ref, A, B, h0, reverse):
    n = A.shape[0]
    sub = lax.broadcasted_iota(jnp.int32, (8, 1), 0)
    tiles = range(n // 8 - 1, -1, -1) if reverse else range(n // 8)
    carry = h0
    for j in tiles:
        a, b = A[8 * j:8 * j + 8, :], B[8 * j:8 * j + 8, :]
        for d in (1, 2, 4):
            keep = (sub < 8 - d) if reverse else (sub >= d)
            shift = 8 - d if reverse else d
            b = jnp.where(keep, a * pltpu.roll(b, shift, axis=0) + b, b)
            a = jnp.where(keep, a * pltpu.roll(a, shift, axis=0), a)
        h = a * carry + b
        out_ref[8 * j:8 * j + 8, :] = h
        carry = h[0:1, :] if reverse else h[7:8, :]
    return carry


def _pool_windows(ext, shift_sign):
    n = ext.shape[0]
    s = ext
    outs = []
    for w in WINDOWS:
        d = w // 2
        s = s + pltpu.roll(s, d if shift_sign > 0 else n - d, axis=0)
        outs.append(s[:, :PG])
        s = s[:, PG:]
    return outs


def _conv_taps(uext):
    taps = []
    for k in range(4):
        sh = 3 - k
        v = uext if sh == 0 else pltpu.roll(uext, sh, axis=0)
        taps.append(v[CONV_HALO:, :])
    return taps


def _gates(v, wa_ref, ba_ref, wx_ref, bx_ref, sp):
    vb = v.astype(BF16)
    ra, rx = [], []
    for h in range(NH):
        vh = vb[:, h * HD:(h + 1) * HD]
        ra.append(_dot(vh, wa_ref[h]))
        rx.append(_dot(vh, wx_ref[h]))
    r = _sigmoid(jnp.concatenate(ra, axis=1) + ba_ref[...])
    i = _sigmoid(jnp.concatenate(rx, axis=1) + bx_ref[...])
    log_a = r * ((-LRU_C) * sp)
    a = jnp.exp(log_a)
    one_minus = -jnp.tanh(log_a) * (1.0 + a * a)
    return r, i, a, jnp.sqrt(one_minus), lax.rsqrt(one_minus)


def _mixer_fwd(proj, wg, scale, w_pool_out, conv_w, conv_b, wa, ba, wx, bx, lam, w_rnn_out, exchange=None,
               exchange_operands=(), tm=256):
    S = proj.shape[0]
    UW = DP + 2 * DR

    def body(proj_ref, wg_ref, scale_ref, wpo_ref, cw_ref, cb_ref, wa_ref, ba_ref, wx_ref, bx_ref, lam_ref, wro_ref,
             pm_ref, ypool_ref, hr_ref, z_ref, yrnn_ref, kept_ref, gates_ref, pool_carry, conv_carry, h_carry):
        i = pl.program_id(0)

        @pl.when(i == 0)
        def _():
            pool_carry[...] = jnp.zeros_like(pool_carry)
            conv_carry[...] = jnp.zeros_like(conv_carry)
            h_carry[...] = jnp.zeros_like(h_carry)

        rows = lax.broadcasted_iota(jnp.int32, (tm, 1), 0)
        t_glob = i * tm + rows

        u_pool = proj_ref[:, 0:DP]
        ext = jnp.concatenate([pool_carry[...], u_pool], axis=0)
        pool_carry[...] = u_pool[tm - POOL_HALO:, :]
        sums = _pool_windows(ext, +1)
        mixed = []
        for g, w in enumerate(WINDOWS):
            inv_cnt = 1.0 / jnp.minimum(t_glob + 1, w).astype(F32)
            pooled_g = sums[g][POOL_HALO:, :] * inv_cnt - u_pool[:, g * PG:(g + 1) * PG]
            mixed.append(_dot(pooled_g.astype(BF16), wg_ref[g]))
        pm = (jnp.concatenate(mixed, axis=1) * scale_ref[...]).astype(BF16)
        pm_ref[...] = pm
        ypool_ref[...] = _dot(pm, wpo_ref[...]).astype(BF16)

        u_rnn = proj_ref[:, DP:DP + DR]
        uext = jnp.concatenate([conv_carry[...], u_rnn], axis=0)
        conv_carry[...] = u_rnn[tm - CONV_HALO:, :]
        taps = _conv_taps(uext)
        v = cb_ref[...]
        for k in range(4):
            v = v + taps[k] * cw_ref[k:k + 1, :]
        sp = _softplus_neg(lam_ref[...])
        r, gi, a, mult, _ = _gates(v, wa_ref, ba_ref, wx_ref, bx_ref, sp)
        for k, kept in enumerate((v, a, mult)):
            kept_ref[k] = kept
        for k, kept in enumerate((r, gi)):
            gates_ref[k] = kept.astype(BF16)
        h_carry[0:1, :] = _linear_scan(hr_ref, a, mult * gi * v, h_carry[0:1, :], reverse=False)
        z = (hr_ref[...] * _gelu(proj_ref[:, DP + DR:UW])).astype(BF16)
        z_ref[...] = z
        yrnn_ref[...] = _dot(z, wro_ref[...]).astype(BF16)

    return _call(
        body, "mixer_fwd", (S // tm,),
        in_specs=[_rows(UW, tm), _resident((4, PG, PG)), _resident((1, DP)), _resident((DP, D)), _resident(conv_w.shape),
                  _resident((1, DR)), _resident((NH, HD, HD)), _resident((1, DR)), _resident((NH, HD, HD)),
                  _resident((1, DR)), _resident((1, DR)), _resident((DR, D))],
        out_specs=[_rows(DP, tm), _rows(D, tm), _rows(DR, tm), _rows(DR, tm), _rows(D, tm),
                   pl.BlockSpec((KEPT, tm, DR), lambda i: (0, i, 0)), pl.BlockSpec((2, tm, DR), lambda i: (0, i, 0))],
        out_shape=[jax.ShapeDtypeStruct((S, DP), BF16),
                   jax.ShapeDtypeStruct((S, D), BF16), jax.ShapeDtypeStruct((S, DR), F32),
                   jax.ShapeDtypeStruct((S, DR), BF16), jax.ShapeDtypeStruct((S, D), BF16),
                   jax.ShapeDtypeStruct((KEPT, S, DR), F32), jax.ShapeDtypeStruct((2, S, DR), BF16)],
        scratch_shapes=[pltpu.VMEM((POOL_HALO, DP), F32), pltpu.VMEM((CONV_HALO, DR), F32), pltpu.VMEM((8, DR), F32)],
        operands=(proj, wg, scale, w_pool_out, conv_w, conv_b, wa, ba, wx, bx, lam, w_rnn_out),
        exchange=exchange, exchange_operands=exchange_operands)


FF_CHUNKS = ((0, 768), (768, 1536), (1536, 2304), (2304, DFF))


def _rms(x):
    r = lax.rsqrt(jnp.mean(x * x, axis=-1, keepdims=True) + EPS)
    return r, x * r


def _rms_bwd(dh, g, r, xh):
    dxh = dh * g
    return r * (dxh - xh * jnp.mean(dxh * xh, axis=-1, keepdims=True))


def _merge_out(x, proj, y_pool, y_rnn, w_o, norm_ffn, exchange=None, exchange_operands=(), tm=512):
    S = x.shape[0]
    GL0 = (DP + 2 * DR) // 512

    def gl_spec(k):
        return pl.BlockSpec((tm, 512), lambda i: (i, GL0 + k))

    def body(x_ref, gl0, gl1, gl2, gl3, yp_ref, yr_ref, wo_ref, gf_ref, mix_ref, x2_ref, h2_ref):
        s_p = _sigmoid(jnp.concatenate([gl0[...], gl1[...]], axis=1))
        s_r = _sigmoid(jnp.concatenate([gl2[...], gl3[...]], axis=1))
        mix = (s_p * yp_ref[...].astype(F32) + s_r * yr_ref[...].astype(F32)).astype(BF16)
        mix_ref[...] = mix
        x2 = x_ref[...] + _dot(mix, wo_ref[...])
        x2_ref[...] = x2
        _, xh2 = _rms(x2)
        h2_ref[...] = (xh2 * gf_ref[...]).astype(BF16)

    return _call(
        body, "merge_out", (S // tm,),
        in_specs=[_rows(D, tm), gl_spec(0), gl_spec(1), gl_spec(2), gl_spec(3), _rows(D, tm), _rows(D, tm),
                  _resident((D, D)), _resident((1, D))],
        out_specs=[_rows(D, tm), _rows(D, tm), _rows(D, tm)],
        out_shape=[jax.ShapeDtypeStruct((S, D), BF16), jax.ShapeDtypeStruct((S, D), F32), jax.ShapeDtypeStruct((S, D), BF16)],
        operands=(x, proj, proj, proj, proj, y_pool, y_rnn, w_o, norm_ffn),
        exchange=exchange, exchange_operands=exchange_operands)


def _ffn_up(h2, w_lo, w_hi, exchange=None, exchange_operands=(), tm=512):
    S = h2.shape[0]
    HALF = D // 2

    def body(h_ref, lo_ref, hi_ref, back_ref, act_ref):
        h_lo, h_hi = h_ref[:, 0:HALF], h_ref[:, HALF:D]
        for c0, c1 in FF_CHUNKS:
            gate = _dot_nt(h_lo, lo_ref[c0:c1, :]) + _dot_nt(h_hi, hi_ref[c0:c1, :])
            up = _dot_nt(h_lo, lo_ref[DFF + c0:DFF + c1, :]) + _dot_nt(h_hi, hi_ref[DFF + c0:DFF + c1, :])
            sg = _sigmoid(gate)
            silu = gate * sg
            back_ref[:, c0:c1] = (up * (sg * (1.0 + gate * (1.0 - sg)))).astype(BF16)
            back_ref[:, DFF + c0:DFF + c1] = silu.astype(BF16)
            act_ref[:, c0:c1] = (silu * up).astype(BF16)

    return _call(
        body, "ffn_up", (S // tm,),
        in_specs=[_rows(D, tm), _resident((2 * DFF, HALF)), _resident((2 * DFF, HALF))],
        out_specs=[_rows(2 * DFF, tm), _rows(DFF, tm)],
        out_shape=[jax.ShapeDtypeStruct((S, 2 * DFF), BF16), jax.ShapeDtypeStruct((S, DFF), BF16)],
        operands=(h2, w_lo, w_hi), exchange=exchange, exchange_operands=exchange_operands)


def _ffn_down_loss(act, x2, target, w_ffn_out, norm_final, tm=512):
    S = act.shape[0]

    def body(act_ref, x2_ref, t_ref, w_ref, gn_ref, dx3_ref, dx3b_ref, loss_ref, dvec_ref):
        i = pl.program_id(0)

        @pl.when(i == 0)
        def _():
            loss_ref[...] = jnp.zeros_like(loss_ref)
            dvec_ref[...] = jnp.zeros_like(dvec_ref)

        x3 = x2_ref[...] + _dot(act_ref[...], w_ref[...])
        r3, xh3 = _rms(x3)
        g_fin = gn_ref[...]
        e = xh3 * g_fin - t_ref[...]
        loss_ref[...] += jnp.sum(e * e, axis=(0, 1), keepdims=True) * (0.5 / D)
        dy = e * (1.0 / D)
        dvec_ref[0:1, :] += jnp.sum(dy * xh3, axis=0, keepdims=True)
        dx3 = _rms_bwd(dy, g_fin, r3, xh3)
        dx3_ref[...] = dx3
        dx3b_ref[...] = dx3.astype(BF16)

    return pl.pallas_call(
        body, name="ffn_down_loss", grid=(S // tm,),
        in_specs=[_rows(DFF, tm), _rows(D, tm), _rows(D, tm), _resident((DFF, D)), _resident((1, D))],
        out_specs=[_rows(D, tm), _rows(D, tm), _resident((1, 1)), _resident((8, D))],
        out_shape=[jax.ShapeDtypeStruct((S, D), F32), jax.ShapeDtypeStruct((S, D), BF16),
                   jax.ShapeDtypeStruct((1, 1), F32), jax.ShapeDtypeStruct((8, D), F32)],
        compiler_params=_params("arbitrary"),
    )(act, x2, target, w_ffn_out, norm_final)


def _ffn_bwd_down(dx3b, gu, w_ffn_out, tm=512):
    S = dx3b.shape[0]

    def body(d_ref, back_ref, w_ref, dgu_ref):
        d = d_ref[...]
        for c0, c1 in FF_CHUNKS:
            dact = _dot_nt(d, w_ref[c0:c1, :])
            dgu_ref[:, c0:c1] = (dact * back_ref[:, c0:c1].astype(F32)).astype(BF16)
            dgu_ref[:, DFF + c0:DFF + c1] = (dact * back_ref[:, DFF + c0:DFF + c1].astype(F32)).astype(BF16)

    return pl.pallas_call(
        body, name="ffn_bwd_down", grid=(S // tm,),
        in_specs=[_rows(D, tm), _rows(2 * DFF, tm), _resident((DFF, D))],
        out_specs=_rows(2 * DFF, tm),
        out_shape=jax.ShapeDtypeStruct((S, 2 * DFF), BF16),
        compiler_params=_params("parallel"),
    )(dx3b, gu, w_ffn_out)


def _ffn_bwd_up(dgu, x2, dx3, w_lo, w_hi, norm_ffn, w_o, tm=512):
    S = dgu.shape[0]
    HALF = D // 2

    def body(dgu_ref, x2_ref, dx3_ref, lo_ref, hi_ref, gf_ref, wo_ref, dx2_ref, dx2b_ref, dmixo_ref, dvec_ref):
        i = pl.program_id(0)

        @pl.when(i == 0)
        def _():
            dvec_ref[...] = jnp.zeros_like(dvec_ref)

        dgate, dup = dgu_ref[:, 0:DFF], dgu_ref[:, DFF:2 * DFF]
        dh2 = jnp.concatenate([_dot(dgate, w[0:DFF, :]) + _dot(dup, w[DFF:2 * DFF, :]) for w in (lo_ref, hi_ref)], axis=1)
        r2, xh2 = _rms(x2_ref[...])
        dvec_ref[0:1, :] += jnp.sum(dh2 * xh2, axis=0, keepdims=True)
        dx2 = dx3_ref[...] + _rms_bwd(dh2, gf_ref[...], r2, xh2)
        dx2_ref[...] = dx2
        dx2b = dx2.astype(BF16)
        dx2b_ref[...] = dx2b
        dmixo_ref[...] = _dot_nt(dx2b, wo_ref[...]).astype(BF16)

    return pl.pallas_call(
        body, name="ffn_bwd_up", grid=(S // tm,),
        in_specs=[_rows(2 * DFF, tm), _rows(D, tm), _rows(D, tm), _resident((2 * DFF, HALF)), _resident((2 * DFF, HALF)),
                  _resident((1, D)), _resident((D, D))],
        out_specs=[_rows(D, tm), _rows(D, tm), _rows(D, tm), _resident((8, D))],
        out_shape=[jax.ShapeDtypeStruct((S, D), F32), jax.ShapeDtypeStruct((S, D), BF16), jax.ShapeDtypeStruct((S, D), BF16),
                   jax.ShapeDtypeStruct((8, D), F32)],
        compiler_params=_params("arbitrary"),
    )(dgu, x2, dx3, w_lo, w_hi, norm_ffn, w_o)


VEC_ROWS = 16
MAT_WA = 4 * PG
MAT_WX = MAT_WA + NH * HD
MAT_ROWS = MAT_WX + NH * HD


def _mixer_bwd(proj, dmixo, y_pool, y_rnn, hr, kept, gates, wg, scale, w_pool_out, conv_w, conv_b, wa, ba, wx, bx, lam, w_rnn_out,
               dvec_fin, dvec_ffn, loss_part, exchange=None, exchange_operands=(), tm=256):
    S = proj.shape[0]
    nt = S // tm

    def rev(cols):
        return pl.BlockSpec((tm, cols), lambda i: (nt - 1 - i, 0))

    def halo(rows_, cols):
        per = tm // rows_
        return pl.BlockSpec((rows_, cols), lambda i: (jnp.maximum((nt - 1 - i) * per - 1, 0), 0))

    def body(proj_ref, projh_ref, dmixo_ref, yp_ref, yr_ref, hr_ref, hrh_ref, kept_ref, gates_ref, wg_ref, scale_ref, wpo_ref, cw_ref, cb_ref,
             wa_ref, ba_ref, wx_ref, bx_ref, lam_ref, wro_ref, fin_ref, ffn_ref, loss_ref,
             dproj_ref, dypb_ref, dyrb_ref, dmat_ref, dvec_ref,
             q_carry, dv_carry, a_carry, g_carry, g_scr):
        i = pl.program_id(0)
        ti = nt - 1 - i

        @pl.when(i == 0)
        def _():
            q_carry[...] = jnp.zeros_like(q_carry)
            dv_carry[...] = jnp.zeros_like(dv_carry)
            a_carry[...] = jnp.zeros_like(a_carry)
            g_carry[...] = jnp.zeros_like(g_carry)
            dmat_ref[...] = jnp.zeros_like(dmat_ref)
            dvec_ref[...] = jnp.zeros_like(dvec_ref)

        rows = lax.broadcasted_iota(jnp.int32, (tm, 1), 0)
        t_glob = ti * tm + rows
        has_prev = (ti > 0).astype(F32)
        dmixo = dmixo_ref[...].astype(F32)

        s_p = _sigmoid(proj_ref[:, DP + 2 * DR:DP + 2 * DR + D])
        s_r = _sigmoid(proj_ref[:, DP + 2 * DR + D:DIN])
        dproj_ref[:, DP + 2 * DR:DP + 2 * DR + D] = (dmixo * yp_ref[...].astype(F32) * s_p * (1.0 - s_p)).astype(BF16)
        dproj_ref[:, DP + 2 * DR + D:DIN] = (dmixo * yr_ref[...].astype(F32) * s_r * (1.0 - s_r)).astype(BF16)
        dyp = (dmixo * s_p).astype(BF16)
        dyr = (dmixo * s_r).astype(BF16)
        dypb_ref[...] = dyp
        dyrb_ref[...] = dyr

        dz = _dot_nt(dyr, wro_ref[...])
        u_gate = proj_ref[:, DP + DR:DP + 2 * DR]
        gg, dgelu = _gelu(u_gate, with_grad=True)
        hr_t = hr_ref[...]
        dproj_ref[:, DP + DR:DP + 2 * DR] = (dz * hr_t * dgelu).astype(BF16)
        dhr = dz * gg

        sp = _softplus_neg(lam_ref[...])
        v, a, mult = (kept_ref[k] for k in range(KEPT))
        r, gi = (gates_ref[k].astype(F32) for k in range(2))
        inv_mult = 1.0 / mult

        C = jnp.where(rows == tm - 1, a_carry[0:1, :], pltpu.roll(a, tm - 1, axis=0))
        g_carry[0:1, :] = _linear_scan(g_scr, C, dhr, g_carry[0:1, :], reverse=True)
        a_carry[0:1, :] = a[0:1, :]
        g = g_scr[...]

        h_prev = jnp.where(rows == 0, hrh_ref[7:8, :] * has_prev, pltpu.roll(hr_t, 1, axis=0))
        da = g * h_prev
        gm = g * mult
        dmult = g * gi * v
        di = gm * v
        dv = gm * gi
        dlog_a = da * a - dmult * (a * a * inv_mult)
        dvec_ref[4:5, :] += jnp.sum(dlog_a * r, axis=0, keepdims=True)
        dra = (dlog_a * ((-LRU_C) * sp) * r * (1.0 - r))
        drx = di * gi * (1.0 - gi)
        dvec_ref[2:3, :] += jnp.sum(dra, axis=0, keepdims=True)
        dvec_ref[3:4, :] += jnp.sum(drx, axis=0, keepdims=True)
        drab = dra.astype(BF16)
        drxb = drx.astype(BF16)
        vb = v.astype(BF16)
        dvg = []
        for h in range(NH):
            sl = slice(h * HD, (h + 1) * HD)
            dvg.append(_dot_nt(drab[:, sl], wa_ref[h]) + _dot_nt(drxb[:, sl], wx_ref[h]))
            dmat_ref[MAT_WA + h * HD:MAT_WA + (h + 1) * HD, :] += _dot_tn(vb[:, sl], drab[:, sl])
            dmat_ref[MAT_WX + h * HD:MAT_WX + (h + 1) * HD, :] += _dot_tn(vb[:, sl], drxb[:, sl])
        dv = dv + jnp.concatenate(dvg, axis=1)
        dvec_ref[1:2, :] += jnp.sum(dv, axis=0, keepdims=True)
        dvext = jnp.concatenate([dv, dv_carry[...]], axis=0)
        dv_carry[...] = dv[0:CONV_HALO, :]
        n = tm + CONV_HALO
        u_rnn = proj_ref[:, DP:DP + DR]
        du_rnn = dv * cw_ref[3:4, :]
        dvec_ref[8:9, :] += jnp.sum(dv * u_rnn, axis=0, keepdims=True)
        for k in range(3):
            dv_k = pltpu.roll(dvext, n - (3 - k), axis=0)[0:tm, :]
            du_rnn = du_rnn + dv_k * cw_ref[k:k + 1, :]
            dvec_ref[5 + k:6 + k, :] += jnp.sum(dv_k * u_rnn, axis=0, keepdims=True)
        dproj_ref[:, DP:DP + DR] = du_rnn.astype(BF16)

        dpm = _dot_nt(dyp, wpo_ref[...])
        u_pool = proj_ref[:, 0:DP]
        ext = jnp.concatenate([projh_ref[:, 0:DP] * has_prev, u_pool], axis=0)
        sums = _pool_windows(ext, +1)
        scale_v = scale_ref[...]
        qs = []
        dpooled = []
        dscale = []
        for gi_, w in enumerate(WINDOWS):
            sl = slice(gi_ * PG, (gi_ + 1) * PG)
            inv_cnt = 1.0 / jnp.minimum(t_glob + 1, w).astype(F32)
            pooled_b = (sums[gi_][POOL_HALO:, :] * inv_cnt - u_pool[:, sl]).astype(BF16)
            mixed_g = _dot(pooled_b, wg_ref[gi_])
            dscale.append(jnp.sum(dpm[:, sl] * mixed_g, axis=0, keepdims=True))
            dmixed_b = (dpm[:, sl] * scale_v[:, sl]).astype(BF16)
            dmat_ref[gi_ * PG:(gi_ + 1) * PG, :] += _dot_tn(pooled_b, dmixed_b)
            dp_g = _dot_nt(dmixed_b, wg_ref[gi_])
            dpooled.append(dp_g)
            qs.append(dp_g * inv_cnt)
        dvec_ref[0:1, 0:DP] += jnp.concatenate(dscale, axis=1)
        q = jnp.concatenate(qs, axis=1)
        qext = jnp.concatenate([q, q_carry[...]], axis=0)
        q_carry[...] = q[0:POOL_HALO, :]
        tsum = _pool_windows(qext, -1)
        for gi_ in range(4):
            dproj_ref[:, gi_ * PG:(gi_ + 1) * PG] = (tsum[gi_][0:tm, :] - dpooled[gi_]).astype(BF16)

        @pl.when(i == nt - 1)
        def _():
            dvec_ref[4:5, :] = dvec_ref[4:5, :] * (LRU_C * _sigmoid(-lam_ref[...]))
            dvec_ref[VEC_NORM_FINAL:VEC_NORM_FINAL + 1, :] = fin_ref[0:1, :]
            dvec_ref[VEC_NORM_FFN:VEC_NORM_FFN + 1, :] = ffn_ref[0:1, :]
            lane = lax.broadcasted_iota(jnp.int32, (1, DR), 1)
            dvec_ref[VEC_LOSS:VEC_LOSS + 1, :] = jnp.where(lane == 0, jnp.broadcast_to(loss_ref[...], (1, DR)), 0.0)

    return _call(
        body, "mixer_bwd", (nt,),
        in_specs=[rev(DIN), halo(POOL_HALO, DIN), rev(D), rev(D), rev(D), rev(DR), halo(8, DR),
                  pl.BlockSpec((KEPT, tm, DR), lambda i: (0, nt - 1 - i, 0)),
                  pl.BlockSpec((2, tm, DR), lambda i: (0, nt - 1 - i, 0)), _resident((4, PG, PG)), _resident((1, DP)), _resident((DP, D)), _resident(conv_w.shape), _resident((1, DR)),
                  _resident((NH, HD, HD)), _resident((1, DR)), _resident((NH, HD, HD)), _resident((1, DR)),
                  _resident((1, DR)), _resident((DR, D)), _resident((8, D)), _resident((8, D)), _resident((1, 1))],
        out_specs=[rev(DIN), rev(D), rev(D), _resident((MAT_ROWS, HD)), _resident((VEC_ROWS, DR))],
        out_shape=[jax.ShapeDtypeStruct((S, DIN), BF16), jax.ShapeDtypeStruct((S, D), BF16),
                   jax.ShapeDtypeStruct((S, D), BF16), jax.ShapeDtypeStruct((MAT_ROWS, HD), F32),
                   jax.ShapeDtypeStruct((VEC_ROWS, DR), F32)],
        scratch_shapes=[pltpu.VMEM((POOL_HALO, DP), F32), pltpu.VMEM((CONV_HALO, DR), F32), pltpu.VMEM((8, DR), F32),
                        pltpu.VMEM((8, DR), F32), pltpu.VMEM((tm, DR), F32)],
        operands=(proj, proj, dmixo, y_pool, y_rnn, hr, hr, kept, gates, wg, scale, w_pool_out, conv_w, conv_b, wa, ba, wx, bx, lam,
                  w_rnn_out, dvec_fin, dvec_ffn, loss_part),
        exchange=exchange, exchange_operands=exchange_operands)


def _in_bwd(dproj, x, dx2, norm_mix, w_in, exchange=None, exchange_operands=(), tm=512):
    S = x.shape[0]

    def body(dp_ref, x_ref, dx2_ref, g_ref, w_ref, dx_ref, dg_ref):
        i = pl.program_id(0)

        @pl.when(i == 0)
        def _():
            dg_ref[...] = jnp.zeros_like(dg_ref)

        dh = _dot(dp_ref[:, 0:1536], w_ref[0:1536, :])
        dh = dh + _dot(dp_ref[:, 1536:3072], w_ref[1536:3072, :])
        dh = dh + _dot(dp_ref[:, 3072:DIN], w_ref[3072:DIN, :])
        xv = x_ref[...]
        r = lax.rsqrt(jnp.mean(xv * xv, axis=-1, keepdims=True) + EPS)
        xh = xv * r
        dg_ref[0:1, :] += jnp.sum(dh * xh, axis=0, keepdims=True)
        dxh = dh * g_ref[...]
        dx_ref[...] = dx2_ref[...] + r * (dxh - xh * jnp.mean(dxh * xh, axis=-1, keepdims=True))

    return _call(
        body, "in_bwd", (S // tm,),
        in_specs=[_rows(DIN, tm), _rows(D, tm), _rows(D, tm), _resident((1, D)), _resident((DIN, D))],
        out_specs=[_rows(D, tm), _resident((8, D))],
        out_shape=[jax.ShapeDtypeStruct((S, D), F32), jax.ShapeDtypeStruct((8, D), F32)],
        operands=(dproj, x, dx2, norm_mix, w_in), exchange=exchange, exchange_operands=exchange_operands)


def _wgrad(a, b, name, tk, tn, exchange=None, exchange_operands=()):
    S, K = a.shape
    N = b.shape[1]

    def body(a_ref, b_ref, o_ref):
        o_ref[...] = _dot_tn(a_ref[...], b_ref[...]).astype(BF16)

    (out,), exchanged = _call(
        body, name, (K // tk, N // tn),
        in_specs=[pl.BlockSpec((S, tk), lambda k, n: (0, k)), pl.BlockSpec((S, tn), lambda k, n: (0, n))],
        out_specs=[pl.BlockSpec((tk, tn), lambda k, n: (k, n))],
        out_shape=[jax.ShapeDtypeStruct((K, N), BF16)],
        operands=(a, b), exchange=exchange, exchange_operands=exchange_operands)
    return (out, exchanged) if exchange is not None else out


VEC_SCALE, VEC_CONV_B, VEC_BA, VEC_BX, VEC_LAM, VEC_CONV_W, VEC_NORM_FINAL, VEC_NORM_FFN = 0, 1, 2, 3, 4, 5, 9, 10
VEC_LOSS = 11


class _Big:
    def __init__(self, name, rows, cols, axis, n, dtype=BF16, transposed=False, src_cols=None):
        self.name, self.rows, self.cols, self.axis, self.n, self.dtype = name, rows, cols, axis, n, dtype
        self.transposed = transposed
        self.src_cols = src_cols
        self.block_shape = (rows, n) if axis == 1 else (n, cols)

    def block(self, ref, p):
        if self.axis == 1:
            return ref.at[:, pl.ds(pl.multiple_of(p * self.n, 128), self.n)]
        return ref.at[pl.ds(pl.multiple_of(p * self.n, 16 if self.dtype == BF16 else 8), self.n), :]


BIG = (_Big("w_in", DIN, D, 0, DIN // 8, transposed=True), _Big("w_pool_out", DP, D, 1, D // 8),
       _Big("w_rnn_out", DR, D, 0, DR // 8), _Big("w_o", D, D, 0, D // 8),
       _Big("w_ffn_in", 2 * DFF, D, 0, 2 * DFF // 8, transposed=True), _Big("w_ffn_out", DFF, D, 0, DFF // 8))
CONV_W = _Big("conv_w", 8, DR, 1, DR // 8, F32)
W_FFN_IN_HALVES = (_Big("w_ffn_in_lo", 2 * DFF, D // 2, 0, 2 * DFF // 8, src_cols=(0, D // 2)),
                   _Big("w_ffn_in_hi", 2 * DFF, D // 2, 0, 2 * DFF // 8, src_cols=(D // 2, D)))
GATHERED = BIG + (CONV_W,) + W_FFN_IN_HALVES

HBM_SPEC = pl.BlockSpec(memory_space=pl.ANY)
VMEM_SPEC = pl.BlockSpec(memory_space=pltpu.VMEM)


def _place():
    x, y, c = (lax.axis_index(a) for a in MESH_AXES)
    other_chips = [(1 - x, y), (x, 1 - y), (1 - x, 1 - y)]
    return x, y, c, other_chips


def _remote(src, dst, send_sems, recv_sems, idx, to):
    return pltpu.make_async_remote_copy(src_ref=src, dst_ref=dst, send_sem=send_sems.at[idx], recv_sem=recv_sems.at[idx],
                                        device_id=to, device_id_type=MESH)


def _device_index(chip, core):
    return 4 * chip[0] + 2 * chip[1] + core


class _Gather:
    def __init__(self, tensors):
        self.tensors = tuple(tensors)
        n = len(self.tensors)
        self.in_specs = [HBM_SPEC] * n
        self.out_specs = [HBM_SPEC] * n
        self.out_shape = [jax.ShapeDtypeStruct((T.rows, T.cols), T.dtype) for T in self.tensors]
        self.scratch_shapes = [pltpu.VMEM(T.block_shape, T.dtype) for T in self.tensors] + [
            pltpu.VMEM(T.block_shape, F32) for T in self.tensors] + [
            pltpu.SemaphoreType.DMA((n, 7)), pltpu.SemaphoreType.DMA((n, 7)), pltpu.SemaphoreType.DMA((n, 2))]

    collective_id = 1

    def peers(self):
        x, y, c, _ = _place()
        return [(x, y, 1 - c), (1 - x, y, c), (x, 1 - y, c)]

    def middles(self, steps):
        return [(steps // 2, self.relay), (steps - 1, self.middle)]

    def _copies(self, ins, outs, scratch):
        n = len(self.tensors)
        mine, raw, (send_sems, recv_sems, loc_sems) = scratch[:n], scratch[n:2 * n], scratch[2 * n:]
        x, y, c, chips = _place()
        sibling = (x, y, 1 - c)
        me = _device_index((x, y), c)
        relay_from = (jnp.where(c == 0, 1 - x, x), jnp.where(c == 0, y, 1 - y))
        relay_to = (jnp.where(c == 0, x, 1 - x), jnp.where(c == 0, 1 - y, y))
        loads, stores, first, relays, passed, arrivals, late = [], [], [], [], [], [], []
        for t, T in enumerate(self.tensors):
            place = T.block(outs[t], me)
            src = ins[t] if T.src_cols is None else ins[t].at[:, T.src_cols[0]:T.src_cols[1]]
            loads.append(pltpu.make_async_copy(src, raw[t], loc_sems.at[t, 0]))
            stores.append(pltpu.make_async_copy(mine[t], place, loc_sems.at[t, 1]))
            first.append(_remote(mine[t], place, send_sems, recv_sems, (t, 0), sibling))
            theirs = T.block(outs[t], _device_index((x, y), 1 - c))
            late.append(_remote(theirs, theirs, send_sems, recv_sems, (t, 0), sibling))
            relayed = T.block(outs[t], _device_index(relay_from, c))
            relays.append(_remote(relayed, relayed, send_sems, recv_sems, (t, 3), (*relay_to, c)))
            for k, chip in enumerate(chips):
                if k < 2:
                    first.append(_remote(mine[t], place, send_sems, recv_sems, (t, 1 + k), (*chip, c)))
                land = T.block(outs[t], _device_index(chip, c))
                arrivals.append(_remote(land, land, send_sems, recv_sems, (t, 1 + k), sibling))
                passed.append(_remote(land, land, send_sems, recv_sems, (t, 4 + k), sibling))
                theirs = T.block(outs[t], _device_index(chip, 1 - c))
                late.append(_remote(theirs, theirs, send_sems, recv_sems, (t, 4 + k), sibling))
        return loads, stores, first, relays, passed, arrivals, late

    def start(self, ins, outs, scratch):
        loads, stores, first, _, _, _, _ = self._copies(ins, outs, scratch)
        n = len(self.tensors)
        for cp in loads:
            cp.start()
        for t, cp in enumerate(loads):
            cp.wait()
            scratch[t][...] = scratch[n + t][...].astype(self.tensors[t].dtype)
        for cp in stores + first:
            cp.start()

    def relay(self, ins, outs, scratch, skip=0):
        _, _, _, relays, passed, arrivals, _ = self._copies(ins, outs, scratch)
        for t in range(skip, len(self.tensors)):
            arrivals[3 * t].wait_recv()
            arrivals[3 * t + 1].wait_recv()
            for cp in (relays[t], passed[3 * t], passed[3 * t + 1]):
                cp.start()

    def middle(self, ins, outs, scratch, skip=0):
        _, _, _, _, passed, arrivals, _ = self._copies(ins, outs, scratch)
        for t in range(skip, len(self.tensors)):
            arrivals[3 * t + 2].wait_recv()
            passed[3 * t + 2].start()

    def finish(self, ins, outs, scratch, skip=0):
        _, stores, first, relays, passed, _, late = self._copies(ins, outs, scratch)
        for cp in late[4 * skip:]:
            cp.wait_recv()
        for cp in first + relays + passed:
            cp.wait_send()
        for cp in stores[skip:]:
            cp.wait()


def _in_proj_gather(x, norm_mix, blocks, tensors, order, tm=1024):
    S = x.shape[0]
    nt = S // tm
    n = len(tensors)
    gather = _Gather(tensors)
    CB = 2 * tensors[0].n

    def body(order_ref, x_ref, g_ref, *refs):
        ins, (proj_ref, h_ref), outs = refs[:n], refs[n:n + 2], refs[n + 2:2 * n + 2]
        (h_all, w_chip, w_sem), scratch = refs[2 * n + 2:2 * n + 5], refs[2 * n + 5:]
        q, i = pl.program_id(0), pl.program_id(1)
        _, stores, _, relays, passed, arrivals, late = gather._copies(ins, outs, scratch)

        def fetch(turn):
            rows = outs[0].at[pl.ds(pl.multiple_of(order_ref[turn] * CB, 16), CB), :]
            cp = pltpu.make_async_copy(rows, w_chip, w_sem)
            cp.start()
            cp.wait()

        @pl.when((q == 0) & (i == 0))
        def _():
            _enter(gather)
            gather.start(ins, outs, scratch)
            late[0].wait_recv()
            stores[0].wait()
            fetch(0)

        @pl.when((q == 1) & (i == 0))
        def _():
            arrivals[0].wait_recv()
            arrivals[1].wait_recv()
            for cp in (relays[0], passed[0], passed[1]):
                cp.start()
            late[1].wait_recv()
            fetch(1)

        @pl.when((q == 2) & (i == 0))
        def _():
            late[2].wait_recv()
            fetch(2)
            gather.relay(ins, outs, scratch, skip=1)

        @pl.when((q == 3) & (i == 0))
        def _():
            arrivals[2].wait_recv()
            passed[2].start()
            late[3].wait_recv()
            fetch(3)

        rows = pl.ds(pl.multiple_of(i * tm, tm), tm)

        @pl.when(q == 0)
        def _():
            xv = x_ref[...]
            r = lax.rsqrt(jnp.mean(xv * xv, axis=-1, keepdims=True) + EPS)
            h = (xv * r * g_ref[...]).astype(BF16)
            h_all[rows, :] = h
            h_ref[...] = h

        proj_ref[...] = _dot_nt(h_all[rows, :], w_chip[...])

        @pl.when((q == 3) & (i == nt - 1))
        def _():
            gather.middle(ins, outs, scratch, skip=1)
            gather.finish(ins, outs, scratch, skip=1)

    row_tile = lambda q, i, order: (jnp.where(q == 0, i, nt - 1), 0)
    whole = lambda shape: pl.BlockSpec(shape, lambda q, i, order: (0,) * len(shape), pipeline_mode=pl.Buffered(1))
    outs = pl.pallas_call(
        body, name="in_proj_gather",
        grid_spec=pltpu.PrefetchScalarGridSpec(
            num_scalar_prefetch=1, grid=(4, nt),
            in_specs=[pl.BlockSpec((tm, D), row_tile), whole((1, D))] + gather.in_specs,
            out_specs=[pl.BlockSpec((tm, CB), lambda q, i, order: (i, order[q])), pl.BlockSpec((tm, D), row_tile)]
            + gather.out_specs,
            scratch_shapes=[pltpu.VMEM((S, D), BF16), pltpu.VMEM((CB, D), BF16), pltpu.SemaphoreType.DMA]
            + gather.scratch_shapes),
        out_shape=[jax.ShapeDtypeStruct((S, DIN), F32), jax.ShapeDtypeStruct((S, D), BF16)] + gather.out_shape,
        compiler_params=pltpu.CompilerParams(dimension_semantics=("arbitrary", "arbitrary"), vmem_limit_bytes=VMEM_LIMIT,
                                             collective_id=gather.collective_id),
    )(order, x, norm_mix, *blocks)
    return outs[:2], outs[2:]


PAIR_ROWS = 32


class _PairReduce:
    collective_id = 3

    def __init__(self, tensors):
        self.tensors = tuple(tensors)
        nt = len(self.tensors)
        blocks = [T.block_shape for T in self.tensors]
        self.in_specs = [HBM_SPEC] * nt
        self.out_specs = [HBM_SPEC] * (2 * nt)
        self.out_shape = ([jax.ShapeDtypeStruct(b, BF16) for b in blocks]
                          + [jax.ShapeDtypeStruct((3,) + b, BF16) for b in blocks])
        self.scratch_shapes = ([pltpu.VMEM((4,) + b, BF16) for b in blocks] + [pltpu.VMEM((3,) + b, BF16) for b in blocks]
                               + [pltpu.SemaphoreType.DMA((nt, 4)), pltpu.SemaphoreType.DMA((nt, 4)),
                                  pltpu.SemaphoreType.DMA((nt, 5))])

    def peers(self):
        x, y, c, _ = _place()
        return [(x, y, 1 - c)]

    def middles(self, steps):
        return []

    def _copies(self, ins, outs, scratch):
        nt = len(self.tensors)
        own_out, sums_out, landed, mine = outs[:nt], outs[nt:], scratch[:nt], scratch[nt:2 * nt]
        send_sems, recv_sems, loc_sems = scratch[2 * nt:]
        x, y, c, chips = _place()
        chip_of = [2 * chip[0] + chip[1] for chip in chips]
        swaps, loads, stores = [], [], []
        for t, T in enumerate(self.tensors):
            for j in range(4):
                swaps.append(_remote(T.block(ins[t], 2 * j + 1 - c), landed[t].at[j], send_sems, recv_sems, (t, j),
                                     (x, y, 1 - c)))
            for k in range(3):
                loads.append(pltpu.make_async_copy(T.block(ins[t], 2 * chip_of[k] + c), mine[t].at[k], loc_sems.at[t, k]))
            stores.append(pltpu.make_async_copy(mine[t], sums_out[t], loc_sems.at[t, 3]))
            stores.append(pltpu.make_async_copy(landed[t].at[2 * x + y], own_out[t], loc_sems.at[t, 4]))
        return swaps, loads, stores, landed, mine, chip_of

    def start(self, ins, outs, scratch):
        swaps, loads, _, _, _, _ = self._copies(ins, outs, scratch)
        for cp in swaps + loads:
            cp.start()

    def finish(self, ins, outs, scratch):
        swaps, loads, stores, landed, mine, chip_of = self._copies(ins, outs, scratch)
        for cp in loads:
            cp.wait()
        for cp in swaps:
            cp.wait_recv()
        for t, T in enumerate(self.tensors):
            for k in range(3):
                acc, got = mine[t].at[k], landed[t].at[chip_of[k]]

                def add(i, carry, acc=acc, got=got):
                    rows = pl.ds(pl.multiple_of(i * PAIR_ROWS, PAIR_ROWS), PAIR_ROWS)
                    acc[rows, :] = (acc[rows, :].astype(F32) + got[rows, :].astype(F32)).astype(BF16)
                    return carry

                lax.fori_loop(0, T.block_shape[0] // PAIR_ROWS, add, 0)
        for cp in stores:
            cp.start()
        for cp in swaps:
            cp.wait_send()
        for cp in stores:
            cp.wait()


def _pair_reduce(grads, tensors, name):
    reduce = _PairReduce(tensors)
    nt = len(reduce.tensors)

    def body(*refs):
        ins, outs, scratch = refs[:nt], refs[nt:3 * nt], refs[3 * nt:]
        _enter(reduce)
        reduce.start(ins, outs, scratch)
        reduce.finish(ins, outs, scratch)

    return pl.pallas_call(
        body, name=name, in_specs=reduce.in_specs, out_specs=reduce.out_specs, out_shape=reduce.out_shape,
        scratch_shapes=reduce.scratch_shapes,
        compiler_params=pltpu.CompilerParams(vmem_limit_bytes=VMEM_LIMIT, collective_id=reduce.collective_id),
    )(*grads)


class _Scatter:
    def __init__(self, tensors):
        self.tensors = tuple(tensors)
        n = len(self.tensors)
        self.in_specs = [HBM_SPEC] * n
        self.out_specs = [HBM_SPEC] * n
        self.out_shape = [jax.ShapeDtypeStruct((2,) + T.block_shape, BF16) for T in self.tensors]
        self.scratch_shapes = [pltpu.VMEM(T.block_shape, BF16) for T in self.tensors] * 2 + [
            pltpu.SemaphoreType.DMA((n, 3)), pltpu.SemaphoreType.DMA((n, 3)), pltpu.SemaphoreType.DMA((n,))]

    collective_id = 2

    def peers(self):
        x, y, c, _ = _place()
        return [(1 - x, y, c), (x, 1 - y, c)]

    def middles(self, steps):
        return [(steps // 2, self.middle)]

    def _copies(self, ins, outs, scratch):
        n = len(self.tensors)
        landed, mine, (send_sems, recv_sems, loc_sems) = scratch[:n], scratch[n:2 * n], scratch[2 * n:]
        x, y, c, _ = _place()
        direct = (jnp.where(c == 0, 1 - x, x), jnp.where(c == 0, y, 1 - y), c)
        other = (jnp.where(c == 0, x, 1 - x), jnp.where(c == 0, 1 - y, y), c)
        k_direct = jnp.where(c == 0, 0, 1)
        to_direct, legs, loads, combined, arrivals = [], [], [], [], []
        for t in range(n):
            to_direct.append(_remote(ins[t].at[k_direct], outs[t].at[0], send_sems, recv_sems, (t, 0), direct))
            legs.append(_remote(ins[t].at[2], landed[t], send_sems, recv_sems, (t, 2), direct))
            loads.append(pltpu.make_async_copy(ins[t].at[1 - k_direct], mine[t], loc_sems.at[t]))
            combined.append(_remote(mine[t], outs[t].at[1], send_sems, recv_sems, (t, 1), other))
            arrivals.append(_remote(landed[t], landed[t], send_sems, recv_sems, (t, 2), direct))
        return to_direct, legs, loads, combined, arrivals, landed, mine

    def start(self, ins, outs, scratch):
        to_direct, legs, loads, _, _, _, _ = self._copies(ins, outs, scratch)
        for cp in to_direct + legs + loads:
            cp.start()

    def middle(self, ins, outs, scratch):
        _, _, loads, combined, arrivals, landed, mine = self._copies(ins, outs, scratch)
        for t, T in enumerate(self.tensors):
            loads[t].wait()
            arrivals[t].wait_recv()
            acc, got = mine[t], landed[t]

            def add(i, carry, acc=acc, got=got):
                rows = pl.ds(pl.multiple_of(i * PAIR_ROWS, PAIR_ROWS), PAIR_ROWS)
                acc[rows, :] = (acc[rows, :].astype(F32) + got[rows, :].astype(F32)).astype(BF16)
                return carry

            lax.fori_loop(0, T.block_shape[0] // PAIR_ROWS, add, 0)
            combined[t].start()

    def finish(self, ins, outs, scratch):
        to_direct, legs, _, combined, _, _, _ = self._copies(ins, outs, scratch)
        for cp in to_direct + combined:
            cp.wait()
        for cp in legs:
            cp.wait_send()


class _ReduceScatter:
    collective_id = 1

    def __init__(self, tensors):
        self.first, self.second = _PairReduce(tensors), _Scatter(tensors)
        self.in_specs = self.first.in_specs
        self.out_specs = self.first.out_specs + self.second.out_specs
        self.out_shape = self.first.out_shape + self.second.out_shape
        self.scratch_shapes = self.first.scratch_shapes + self.second.scratch_shapes

    def peers(self):
        x, y, c, _ = _place()
        return [(x, y, 1 - c), (1 - x, y, c), (x, 1 - y, c)]

    def _parts(self, ins, outs, scratch):
        n_out, n_scr = len(self.first.out_specs), len(self.first.scratch_shapes)
        pair_sums = list(outs[n_out // 2:n_out])
        return (ins, outs[:n_out], scratch[:n_scr]), (pair_sums, outs[n_out:], scratch[n_scr:])

    def start(self, ins, outs, scratch):
        self.first.start(*self._parts(ins, outs, scratch)[0])

    def middles(self, steps):
        def hand_over(ins, outs, scratch):
            first, second = self._parts(ins, outs, scratch)
            self.first.finish(*first)
            self.second.start(*second)

        def relay(ins, outs, scratch):
            self.second.middle(*self._parts(ins, outs, scratch)[1])

        return [(steps // 4, hand_over), ((5 * steps) // 8, relay)]

    def finish(self, ins, outs, scratch):
        self.second.finish(*self._parts(ins, outs, scratch)[1])


def _adamw(w, g, m, v):
    m = ADAM_B1 * m + (1.0 - ADAM_B1) * g
    v = ADAM_B2 * v + (1.0 - ADAM_B2) * (g * g)
    m_hat = m / (1.0 - ADAM_B1 ** ADAM_STEP)
    v_hat = v / (1.0 - ADAM_B2 ** ADAM_STEP)
    delta = -ADAM_LR * (m_hat / (jnp.sqrt(v_hat) + ADAM_EPS) + ADAM_WD * w)
    return delta, m, v


def _final_sum(T, g, lz1, lz2, where, w, m, v):
    rows, cols = T.block_shape
    sub = 4 if T.axis == 0 and rows % 64 == 0 and rows > 256 else 1
    blk = (rows // sub, cols)

    def body(where_ref, g_ref, l1_ref, l2_ref, w_ref, m_ref, v_ref, g_out, d_out, m_out, v_out):
        tot = g_ref[...].astype(F32) + l1_ref[...].astype(F32)
        for k in range(2):
            tot = tot + l2_ref[k].astype(F32)
        g_out[...] = tot
        d_out[...], m_out[...], v_out[...] = _adamw(w_ref[...], tot, m_ref[...], v_ref[...])

    def in_whole(r, wh):
        p = wh[0]
        return (0, p) if T.axis == 1 else (p * sub + r, 0)

    own = pl.BlockSpec(blk, lambda r, wh: (r, 0))
    return pl.pallas_call(
        body, name="grad_final_" + T.name,
        grid_spec=pltpu.PrefetchScalarGridSpec(
            num_scalar_prefetch=1, grid=(sub,),
            in_specs=[pl.BlockSpec(blk, in_whole),
                      own,
                      pl.BlockSpec((2,) + blk, lambda r, wh: (0, r, 0)), own, own, own],
            out_specs=[own] * 4),
        out_shape=[jax.ShapeDtypeStruct(T.block_shape, F32)] * 4,
        compiler_params=_params("arbitrary"),
    )(where, g, lz1, lz2, w, m, v)


VEC_PIECE = DR // 8


class _AllReduce:
    def __init__(self, items):
        self.items = tuple(items)
        n = len(self.items)
        self.in_specs = [HBM_SPEC] * n
        self.out_specs = [HBM_SPEC] * n
        self.out_shape = [jax.ShapeDtypeStruct(shape, F32) for shape, _ in self.items]
        pieces = [(shape[0] // 8, shape[1]) if axis == 0 else (shape[0], shape[1] // 8) for shape, axis in self.items]
        self.scratch_shapes = ([pltpu.VMEM((8,) + p, F32) for p in pieces] + [pltpu.VMEM(p, F32) for p in pieces] + [
            pltpu.SemaphoreType.DMA((2 * n, 8)), pltpu.SemaphoreType.DMA((2 * n, 8)), pltpu.SemaphoreType.DMA((2 * n,))])

    collective_id = None

    def peers(self):
        return []

    def middles(self, steps):
        return [(steps // 2, self.middle)]

    def _copies(self, ins, outs, scratch):
        n = len(self.items)
        landed, sums, (send_sems, recv_sems, loc_sems) = scratch[:n], scratch[n:2 * n], scratch[2 * n:]
        x, y, c, _ = _place()
        me = _device_index((x, y), c)

        def peer(r):
            return (1 - x if r & 4 else x, 1 - y if r & 2 else y, 1 - c if r & 1 else c)

        def piece(i, ref, p):
            shape, axis = self.items[i]
            if axis == 0:
                rows = shape[0] // 8
                return ref.at[pl.ds(pl.multiple_of(p * rows, 8), rows), :]
            cols = shape[1] // 8
            return ref.at[:, pl.ds(pl.multiple_of(p * cols, 128), cols)]

        own, scatter, arrivals, keep, spread, late = [], [], [], [], [], []
        for i in range(n):
            own.append(pltpu.make_async_copy(piece(i, ins[i], me), landed[i].at[0], loc_sems.at[2 * i]))
            keep.append(pltpu.make_async_copy(sums[i], piece(i, outs[i], me), loc_sems.at[2 * i + 1]))
            for r in range(1, 8):
                to = peer(r)
                p = _device_index(to[:2], to[2])
                scatter.append(_remote(piece(i, ins[i], p), landed[i].at[r], send_sems, recv_sems, (2 * i, r), to))
                spread.append(_remote(sums[i], piece(i, outs[i], me), send_sems, recv_sems, (2 * i + 1, r), to))
                late.append(_remote(sums[i], piece(i, outs[i], p), send_sems, recv_sems, (2 * i + 1, r), to))
        return own, scatter, keep, spread, late, landed, sums

    def start(self, ins, outs, scratch):
        own, scatter, _, _, _, _, _ = self._copies(ins, outs, scratch)
        for cp in own + scatter:
            cp.start()

    def middle(self, ins, outs, scratch):
        own, scatter, keep, spread, _, landed, sums = self._copies(ins, outs, scratch)
        for cp in own:
            cp.wait()
        for cp in scatter:
            cp.wait_recv()
        for i in range(len(self.items)):
            total = landed[i][0]
            for r in range(1, 8):
                total = total + landed[i][r]
            sums[i][...] = total
        for cp in keep + spread:
            cp.start()

    def finish(self, ins, outs, scratch):
        _, scatter, keep, spread, late, _, _ = self._copies(ins, outs, scratch)
        for cp in late:
            cp.wait_recv()
        for cp in scatter + spread:
            cp.wait_send()
        for cp in keep:
            cp.wait()


class _Both:
    def __init__(self, a, b):
        self.a, self.b = a, b
        self.in_specs, self.out_specs = a.in_specs + b.in_specs, a.out_specs + b.out_specs
        self.out_shape, self.scratch_shapes = a.out_shape + b.out_shape, a.scratch_shapes + b.scratch_shapes

    collective_id = None

    def peers(self):
        return []

    def _each(self, ins, outs, scratch):
        a = self.a
        i, o, s = len(a.in_specs), len(a.out_specs), len(a.scratch_shapes)
        return (a, ins[:i], outs[:o], scratch[:s]), (self.b, ins[i:], outs[o:], scratch[s:])

    def middles(self, steps):
        def of(which, middle):
            return lambda ins, outs, scratch: middle(*self._each(ins, outs, scratch)[which][1:])
        return [(at, of(which, middle)) for which, e in enumerate((self.a, self.b)) for at, middle in e.middles(steps)]

    def start(self, ins, outs, scratch):
        for e, i, o, s in self._each(ins, outs, scratch):
            e.start(i, o, s)

    def finish(self, ins, outs, scratch):
        for e, i, o, s in self._each(ins, outs, scratch):
            e.finish(i, o, s)


def _all_reduce(arrays, items, name):
    reduce = _AllReduce(items)
    n = len(items)

    def body(*refs):
        ins, outs, scratch = refs[:n], refs[n:2 * n], refs[2 * n:]
        reduce.start(ins, outs, scratch)
        reduce.middle(ins, outs, scratch)
        reduce.finish(ins, outs, scratch)

    return pl.pallas_call(
        body, name=name, in_specs=reduce.in_specs, out_specs=reduce.out_specs, out_shape=reduce.out_shape,
        scratch_shapes=reduce.scratch_shapes,
    )(*arrays)


def _adam_small(grads, wmv):
    n = len(grads)

    def body(*refs):
        g_refs, rest = refs[:n], refs[n:]
        ins, outs = rest[:3 * n], rest[3 * n:]
        for i in range(n):
            d, m, v = _adamw(ins[3 * i][...], g_refs[i][...], ins[3 * i + 1][...], ins[3 * i + 2][...])
            outs[3 * i][...], outs[3 * i + 1][...], outs[3 * i + 2][...] = d, m, v

    flat = [a for t in wmv for a in t]
    return pl.pallas_call(
        body, name="adam_small",
        in_specs=[VMEM_SPEC] * (4 * n), out_specs=[VMEM_SPEC] * (3 * n),
        out_shape=[jax.ShapeDtypeStruct(a.shape, F32) for a in flat],
    )(*grads, *flat)


WEIGHT_NAMES = ("norm_mix", "w_in", "w_pool_grp", "pool_scale", "w_pool_out", "conv_w", "conv_b", "w_rg_a", "b_rg_a", "w_rg_x",
                "b_rg_x", "lru_lambda", "w_rnn_out", "w_o", "norm_ffn", "w_ffn_in", "w_ffn_out", "norm_final")


def kernel(x, norm_mix, w_in, w_pool_grp, pool_scale, w_pool_out, conv_w, conv_b, w_rg_a, b_rg_a, w_rg_x, b_rg_x, lru_lambda, w_rnn_out, w_o, norm_ffn, w_ffn_in, w_ffn_out, norm_final, loss_target, m_norm_mix, m_w_in, m_w_pool_grp, m_pool_scale, m_w_pool_out, m_conv_w, m_conv_b, m_w_rg_a, m_b_rg_a, m_w_rg_x, m_b_rg_x, m_lru_lambda, m_w_rnn_out, m_w_o, m_norm_ffn, m_w_ffn_in, m_w_ffn_out, m_norm_final, v_norm_mix, v_w_in, v_w_pool_grp, v_pool_scale, v_w_pool_out, v_conv_w, v_conv_b, v_w_rg_a, v_b_rg_a, v_w_rg_x, v_b_rg_x, v_lru_lambda, v_w_rnn_out, v_w_o, v_norm_ffn, v_w_ffn_in, v_w_ffn_out, v_norm_final):
    w = dict(norm_mix=norm_mix, w_in=w_in, w_pool_grp=w_pool_grp, pool_scale=pool_scale, w_pool_out=w_pool_out, conv_w=conv_w,
             conv_b=conv_b, w_rg_a=w_rg_a, b_rg_a=b_rg_a, w_rg_x=w_rg_x, b_rg_x=b_rg_x, lru_lambda=lru_lambda,
             w_rnn_out=w_rnn_out, w_o=w_o, norm_ffn=norm_ffn, w_ffn_in=w_ffn_in, w_ffn_out=w_ffn_out, norm_final=norm_final)
    m = dict(norm_mix=m_norm_mix, w_in=m_w_in, w_pool_grp=m_w_pool_grp, pool_scale=m_pool_scale, w_pool_out=m_w_pool_out,
             conv_w=m_conv_w, conv_b=m_conv_b, w_rg_a=m_w_rg_a, b_rg_a=m_b_rg_a, w_rg_x=m_w_rg_x, b_rg_x=m_b_rg_x,
             lru_lambda=m_lru_lambda, w_rnn_out=m_w_rnn_out, w_o=m_w_o, norm_ffn=m_norm_ffn, w_ffn_in=m_w_ffn_in,
             w_ffn_out=m_w_ffn_out, norm_final=m_norm_final)
    v = dict(norm_mix=v_norm_mix, w_in=v_w_in, w_pool_grp=v_w_pool_grp, pool_scale=v_pool_scale, w_pool_out=v_w_pool_out,
             conv_w=v_conv_w, conv_b=v_conv_b, w_rg_a=v_w_rg_a, b_rg_a=v_b_rg_a, w_rg_x=v_w_rg_x, b_rg_x=v_b_rg_x,
             lru_lambda=v_lru_lambda, w_rnn_out=v_w_rnn_out, w_o=v_w_o, norm_ffn=v_norm_ffn, w_ffn_in=v_w_ffn_in,
             w_ffn_out=v_w_ffn_out, norm_final=v_norm_final)
    xi, yi, ci = (lax.axis_index(a) for a in MESH_AXES)
    chip = 2 * xi + yi

    def held(T, a):
        return jnp.swapaxes(a, 0, 1) if T.transposed else a

    where = jnp.stack([2 * chip + ci]).astype(jnp.int32)
    by_name = {T.name: T for T in GATHERED}
    block = {T.name: held(T, w[T.name][0]) for T in BIG}
    block["conv_w"] = jnp.pad(conv_w[0], ((0, CONV_W.rows - 4), (0, 0)))
    block["w_ffn_in_lo"] = block["w_ffn_in_hi"] = block["w_ffn_in"]

    def gather_of(*names):
        return dict(exchange=_Gather([by_name[n] for n in names]), exchange_operands=[block[n] for n in names])

    def pair_sums(names, partials, tag):
        out = _pair_reduce(partials, [by_name[n] for n in names], "grad_pair_reduce_" + tag)
        return list(out[:len(names)]), list(out[len(names):])

    xs, target = x[0], loss_target[0]
    wg_b, wa_b, wx_b = (a[0].astype(BF16) for a in (w_pool_grp, w_rg_a, w_rg_x))
    ba2, bx2 = b_rg_a.reshape(1, DR), b_rg_x.reshape(1, DR)
    first = ("w_in", "w_pool_out", "w_rnn_out", "conv_w", "w_o")
    order = jnp.stack([chip, 2 * (1 - xi) + yi, 2 * xi + (1 - yi), 2 * (1 - xi) + (1 - yi)]).astype(jnp.int32)
    (proj, h1), (w_in_g, w_pool_out_g, w_rnn_out_g, conv_g, w_o_g) = _in_proj_gather(
        xs, norm_mix, [block[n] for n in first], [by_name[n] for n in first], order)
    mixer_weights = (wg_b, pool_scale, w_pool_out_g, conv_g, conv_b, wa_b, ba2, wx_b, bx2, lru_lambda, w_rnn_out_g)
    (pm, y_pool, hr, z, y_rnn, kept, gates), (w_ffn_lo_g, w_ffn_hi_g) = _mixer_fwd(
        proj, *mixer_weights, **gather_of("w_ffn_in_lo", "w_ffn_in_hi"))
    (mix, x2, h2), _ = _merge_out(xs, proj, y_pool, y_rnn, w_o_g, norm_ffn)
    (gu, act), (w_ffn_out_g,) = _ffn_up(h2, w_ffn_lo_g, w_ffn_hi_g, **gather_of("w_ffn_out"))
    dx3, dx3b, loss_part, dvec_fin = _ffn_down_loss(act, x2, target, w_ffn_out_g, norm_final.reshape(1, D))

    dgu = _ffn_bwd_down(dx3b, gu, w_ffn_out_g)
    dx2, dx2b, dmixo, dvec_ffn = _ffn_bwd_up(dgu, x2, dx3, w_ffn_lo_g, w_ffn_hi_g, norm_ffn, w_o_g)
    names_a = ("w_ffn_in", "w_ffn_out", "w_o")
    g_ffn_out = _wgrad(act, dx3b, "wgrad_ffn_out", 1408, 512)
    g_ffn_in, (own_ffn_out, sums_ffn_out) = _wgrad(
        dgu, h2, "wgrad_ffn_in", 1408, 512, exchange=_PairReduce([by_name["w_ffn_out"]]), exchange_operands=[g_ffn_out])
    g_o, (own_ffn_in, sums_ffn_in) = _wgrad(
        mix, dx2b, "wgrad_o", 1024, 256, exchange=_PairReduce([by_name["w_ffn_in"]]), exchange_operands=[g_ffn_in])
    (own_o,), (sums_o,) = pair_sums(("w_o",), [g_o], "o")
    part_a = [g_ffn_in, g_ffn_out, g_o]
    lz1_a, sums_a = [own_ffn_in, own_ffn_out, own_o], [sums_ffn_in, sums_ffn_out, sums_o]
    (dproj, dypb, dyrb, dmat, dvec), lz2_a = _mixer_bwd(
        proj, dmixo, y_pool, y_rnn, hr, kept, gates, *mixer_weights, dvec_fin, dvec_ffn, loss_part,
        exchange=_Scatter([by_name[n] for n in names_a]), exchange_operands=sums_a)
    names_b = ("w_pool_out", "w_rnn_out")
    part_b = [_wgrad(pm, dypb, "wgrad_pool_out", 512, 256), _wgrad(z, dyrb, "wgrad_rnn_out", 1024, 256)]
    lz1_b, sums_b = pair_sums(names_b, part_b, "mix")
    g_in, exchanged = _wgrad(
        dproj, h1, "wgrad_in", 1152, 1024,
        exchange=_Both(_Scatter([by_name[n] for n in names_b]), _AllReduce([((MAT_ROWS, HD), 0), ((VEC_ROWS, DR), 1)])),
        exchange_operands=sums_b + [dmat, dvec])
    lz2_b, (mat, vec) = exchanged[:2], exchanged[2:]
    loss = vec[VEC_LOSS, 0]
    (grad_x, dvec_in), (own_in, _, scattered_in) = _in_bwd(
        dproj, xs, dx2, norm_mix, w_in_g, exchange=_ReduceScatter([by_name["w_in"]]), exchange_operands=[g_in])
    lz1_c, lz2_c = [own_in], [scattered_in]
    (vec_in,) = _all_reduce([dvec_in], [((8, D), 1)], "all_reduce_norm_mix")

    grads, delta, new_m, new_v = {}, {}, {}, {}
    for n, g, l1, l2 in zip(names_a + names_b + ("w_in",), part_a + part_b + [g_in], lz1_a + lz1_b + lz1_c,
                            list(lz2_a) + list(lz2_b) + lz2_c):
        T = by_name[n]
        out = _final_sum(T, g, l1, l2, where, held(T, w[n][0]), held(T, m[n][0]), held(T, v[n][0]))
        grads[n], delta[n], new_m[n], new_v[n] = (held(T, a) for a in out)
    me = 4 * xi + 2 * yi + ci
    small_grads = dict(
        w_pool_grp=mat[0:MAT_WA], w_rg_a=mat[MAT_WA:MAT_WX], w_rg_x=mat[MAT_WX:MAT_ROWS],
        pool_scale=vec[VEC_SCALE:VEC_SCALE + 1, 0:DP], conv_b=vec[VEC_CONV_B:VEC_CONV_B + 1],
        b_rg_a=vec[VEC_BA:VEC_BA + 1], b_rg_x=vec[VEC_BX:VEC_BX + 1], lru_lambda=vec[VEC_LAM:VEC_LAM + 1],
        conv_w=lax.dynamic_slice(vec, (VEC_CONV_W, VEC_PIECE * me), (4, VEC_PIECE)),
        norm_final=vec[VEC_NORM_FINAL:VEC_NORM_FINAL + 1], norm_ffn=vec[VEC_NORM_FFN:VEC_NORM_FFN + 1],
        norm_mix=vec_in[0:1])
    names = list(small_grads)
    as2d = lambda a, g: a.reshape(g.shape)
    upd = _adam_small([small_grads[n] for n in names],
                      [(as2d(w[n], small_grads[n]), as2d(m[n], small_grads[n]), as2d(v[n], small_grads[n])) for n in names])
    for i, n in enumerate(names):
        grads[n] = small_grads[n]
        delta[n], new_m[n], new_v[n] = upd[3 * i:3 * i + 3]

    shaped = lambda d: [d[n].reshape(w[n].shape) for n in WEIGHT_NAMES]
    return (loss, grad_x[None], *shaped(grads), *shaped(delta), *shaped(new_m), *shaped(new_v))
```

```python
import math

import jax
import jax.numpy as jnp
from jax import lax
from jax.experimental import pallas as pl
from jax.experimental.pallas import tpu as pltpu

F32 = jnp.float32
BF16 = jnp.bfloat16

D = 1024
DP = 512
PG = 128
WINDOWS = (2, 4, 8, 16)
DR = 1024
NH = 8
HD = 128
DIN = 4608
DFF = 2816
EPS = 1e-6
LRU_C = 8.0
POOL_HALO = 16
CONV_HALO = 8
KEPT = 3

ADAM_LR = 0.001
ADAM_B1 = 0.9
ADAM_B2 = 0.999
ADAM_EPS = 1e-08
ADAM_WD = 0.01
ADAM_STEP = 10

VMEM_LIMIT = 56 * 1024 * 1024
MESH_AXES = ("x", "y", "c")
MESH = pl.DeviceIdType.MESH


def _dot(a, b):
    return jnp.dot(a, b, preferred_element_type=F32)


def _dot_nt(a, b):
    return lax.dot_general(a, b, (((1,), (1,)), ((), ())), preferred_element_type=F32)


def _dot_tn(a, b):
    return lax.dot_general(a, b, (((0,), (0,)), ((), ())), preferred_element_type=F32)


def _params(*sem):
    return pltpu.CompilerParams(dimension_semantics=sem, vmem_limit_bytes=VMEM_LIMIT)


def _resident(shape):
    nd = len(shape)
    return pl.BlockSpec(shape, lambda i: (0,) * nd, pipeline_mode=pl.Buffered(1))


def _rows(shape_cols, tm):
    return pl.BlockSpec((tm, shape_cols), lambda i: (i, 0))


def _call(body, name, grid, in_specs, out_specs, out_shape, operands, scratch_shapes=(), exchange=None, exchange_operands=()):
    n_in, n_out, n_scr = len(in_specs), len(out_specs), len(scratch_shapes)
    steps = math.prod(grid)
    if exchange is None:
        outs = pl.pallas_call(body, name=name, grid=grid, in_specs=in_specs, out_specs=out_specs, out_shape=out_shape,
                              scratch_shapes=list(scratch_shapes), compiler_params=_params(*["arbitrary"] * len(grid)))(*operands)
        return outs, []
    e_in, e_out = len(exchange.in_specs), len(exchange.out_specs)

    def hosted(*refs):
        ins, refs = refs[:n_in], refs[n_in:]
        e_ins, refs = refs[:e_in], refs[e_in:]
        outs, refs = refs[:n_out], refs[n_out:]
        e_outs, refs = refs[:e_out], refs[e_out:]
        scr, e_scr = refs[:n_scr], refs[n_scr:]
        step = pl.program_id(0)
        for axis in range(1, len(grid)):
            step = step * grid[axis] + pl.program_id(axis)
        @pl.when(step == 0)
        def _():
            _enter(exchange)
            exchange.start(e_ins, e_outs, e_scr)

        for at, middle in exchange.middles(steps):
            pl.when(step == at)(lambda middle=middle: middle(e_ins, e_outs, e_scr))
        body(*ins, *outs, *scr)
        pl.when(step == steps - 1)(lambda: exchange.finish(e_ins, e_outs, e_scr))

    outs = pl.pallas_call(
        hosted, name=name, grid=grid, in_specs=list(in_specs) + exchange.in_specs,
        out_specs=list(out_specs) + exchange.out_specs, out_shape=list(out_shape) + exchange.out_shape,
        scratch_shapes=list(scratch_shapes) + exchange.scratch_shapes,
        compiler_params=pltpu.CompilerParams(dimension_semantics=("arbitrary",) * len(grid), vmem_limit_bytes=VMEM_LIMIT,
                                             collective_id=exchange.collective_id))(*operands, *exchange_operands)
    return outs[:n_out], outs[n_out:]


def _enter(exchange):
    peers = exchange.peers()
    if peers:
        barrier = pltpu.get_barrier_semaphore()
        for peer in peers:
            pl.semaphore_signal(barrier, inc=1, device_id=peer, device_id_type=MESH)
        pl.semaphore_wait(barrier, len(peers))


GELU_C = math.sqrt(2.0 / math.pi)
GELU_K = 0.044715 * GELU_C


def _gelu(x, with_grad=False):
    x2 = x * x
    t = jnp.tanh(x * (GELU_C + GELU_K * x2))
    hx = 0.5 * x
    y = hx + hx * t
    if not with_grad:
        return y
    return y, 0.5 + 0.5 * t + hx * (1.0 - t * t) * (GELU_C + (3.0 * GELU_K) * x2)


def _softplus_neg(lam):
    z = jnp.exp(-jnp.abs(lam))
    u = 1.0 + z
    dlt = u - 1.0
    log1p = jnp.where(dlt == 0.0, z, jnp.log(u) * (z / jnp.where(dlt == 0.0, 1.0, dlt)))
    return jnp.maximum(-lam, 0.0) + log1p


def _sigmoid(x):
    return 0.5 * jnp.tanh(0.5 * x) + 0.5


def _linear_scan(out_ref, A, B, h0, reverse):
    n = A.shape[0]
    sub = lax.broadcasted_iota(jnp.int32, (8, 1), 0)
    tiles = range(n // 8 - 1, -1, -1) if reverse else range(n // 8)
    carry = h0
    for j in tiles:
        a, b = A[8 * j:8 * j + 8, :], B[8 * j:8 * j + 8, :]
        for d in (1, 2, 4):
            keep = (sub < 8 - d) if reverse else (sub >= d)
            shift = 8 - d if reverse else d
            b = jnp.where(keep, a * pltpu.roll(b, shift, axis=0) + b, b)
            a = jnp.where(keep, a * pltpu.roll(a, shift, axis=0), a)
        h = a * carry + b
        out_ref[8 * j:8 * j + 8, :] = h
        carry = h[0:1, :] if reverse else h[7:8, :]
    return carry


def _pool_windows(ext, shift_sign):
    n = ext.shape[0]
    s = ext
    outs = []
    for w in WINDOWS:
        d = w // 2
        s = s + pltpu.roll(s, d if shift_sign > 0 else n - d, axis=0)
        outs.append(s[:, :PG])
        s = s[:, PG:]
    return outs


def _conv_taps(uext):
    taps = []
    for k in range(4):
        sh = 3 - k
        v = uext if sh == 0 else pltpu.roll(uext, sh, axis=0)
        taps.append(v[CONV_HALO:, :])
    return taps


def _gates(v, wa_ref, ba_ref, wx_ref, bx_ref, sp):
    vb = v.astype(BF16)
    ra, rx = [], []
    for h in range(NH):
        vh = vb[:, h * HD:(h + 1) * HD]
        ra.append(_dot(vh, wa_ref[h]))
        rx.append(_dot(vh, wx_ref[h]))
    r = _sigmoid(jnp.concatenate(ra, axis=1) + ba_ref[...])
    i = _sigmoid(jnp.concatenate(rx, axis=1) + bx_ref[...])
    log_a = r * ((-LRU_C) * sp)
    a = jnp.exp(log_a)
    one_minus = -jnp.tanh(log_a) * (1.0 + a * a)
    return r, i, a, jnp.sqrt(one_minus), lax.rsqrt(one_minus)


def _mixer_fwd(proj, wg, scale, w_pool_out, conv_w, conv_b, wa, ba, wx, bx, lam, w_rnn_out, exchange=None,
               exchange_operands=(), tm=256):
    S = proj.shape[0]
    UW = DP + 2 * DR

    def body(proj_ref, wg_ref, scale_ref, wpo_ref, cw_ref, cb_ref, wa_ref, ba_ref, wx_ref, bx_ref, lam_ref, wro_ref,
             pm_ref, ypool_ref, hr_ref, z_ref, yrnn_ref, kept_ref, gates_ref, pool_carry, conv_carry, h_carry):
        i = pl.program_id(0)

        @pl.when(i == 0)
        def _():
            pool_carry[...] = jnp.zeros_like(pool_carry)
            conv_carry[...] = jnp.zeros_like(conv_carry)
            h_carry[...] = jnp.zeros_like(h_carry)

        rows = lax.broadcasted_iota(jnp.int32, (tm, 1), 0)
        t_glob = i * tm + rows

        u_pool = proj_ref[:, 0:DP]
        ext = jnp.concatenate([pool_carry[...], u_pool], axis=0)
        pool_carry[...] = u_pool[tm - POOL_HALO:, :]
        sums = _pool_windows(ext, +1)
        mixed = []
        for g, w in enumerate(WINDOWS):
            inv_cnt = 1.0 / jnp.minimum(t_glob + 1, w).astype(F32)
            pooled_g = sums[g][POOL_HALO:, :] * inv_cnt - u_pool[:, g * PG:(g + 1) * PG]
            mixed.append(_dot(pooled_g.astype(BF16), wg_ref[g]))
        pm = (jnp.concatenate(mixed, axis=1) * scale_ref[...]).astype(BF16)
        pm_ref[...] = pm
        ypool_ref[...] = _dot(pm, wpo_ref[...]).astype(BF16)

        u_rnn = proj_ref[:, DP:DP + DR]
        uext = jnp.concatenate([conv_carry[...], u_rnn], axis=0)
        conv_carry[...] = u_rnn[tm - CONV_HALO:, :]
        taps = _conv_taps(uext)
        v = cb_ref[...]
        for k in range(4):
            v = v + taps[k] * cw_ref[k:k + 1, :]
        sp = _softplus_neg(lam_ref[...])
        r, gi, a, mult, _ = _gates(v, wa_ref, ba_ref, wx_ref, bx_ref, sp)
        for k, kept in enumerate((v, a, mult)):
            kept_ref[k] = kept
        for k, kept in enumerate((r, gi)):
            gates_ref[k] = kept.astype(BF16)
        h_carry[0:1, :] = _linear_scan(hr_ref, a, mult * gi * v, h_carry[0:1, :], reverse=False)
        z = (hr_ref[...] * _gelu(proj_ref[:, DP + DR:UW])).astype(BF16)
        z_ref[...] = z
        yrnn_ref[...] = _dot(z, wro_ref[...]).astype(BF16)

    return _call(
        body, "mixer_fwd", (S // tm,),
        in_specs=[_rows(UW, tm), _resident((4, PG, PG)), _resident((1, DP)), _resident((DP, D)), _resident(conv_w.shape),
                  _resident((1, DR)), _resident((NH, HD, HD)), _resident((1, DR)), _resident((NH, HD, HD)),
                  _resident((1, DR)), _resident((1, DR)), _resident((DR, D))],
        out_specs=[_rows(DP, tm), _rows(D, tm), _rows(DR, tm), _rows(DR, tm), _rows(D, tm),
                   pl.BlockSpec((KEPT, tm, DR), lambda i: (0, i, 0)), pl.BlockSpec((2, tm, DR), lambda i: (0, i, 0))],
        out_shape=[jax.ShapeDtypeStruct((S, DP), BF16),
                   jax.ShapeDtypeStruct((S, D), BF16), jax.ShapeDtypeStruct((S, DR), F32),
                   jax.ShapeDtypeStruct((S, DR), BF16), jax.ShapeDtypeStruct((S, D), BF16),
                   jax.ShapeDtypeStruct((KEPT, S, DR), F32), jax.ShapeDtypeStruct((2, S, DR), BF16)],
        scratch_shapes=[pltpu.VMEM((POOL_HALO, DP), F32), pltpu.VMEM((CONV_HALO, DR), F32), pltpu.VMEM((8, DR), F32)],
        operands=(proj, wg, scale, w_pool_out, conv_w, conv_b, wa, ba, wx, bx, lam, w_rnn_out),
        exchange=exchange, exchange_operands=exchange_operands)


FF_CHUNKS = ((0, 768), (768, 1536), (1536, 2304), (2304, DFF))


def _rms(x):
    r = lax.rsqrt(jnp.mean(x * x, axis=-1, keepdims=True) + EPS)
    return r, x * r


def _rms_bwd(dh, g, r, xh):
    dxh = dh * g
    return r * (dxh - xh * jnp.mean(dxh * xh, axis=-1, keepdims=True))


def _merge_out(x, proj, y_pool, y_rnn, w_o, norm_ffn, exchange=None, exchange_operands=(), tm=512):
    S = x.shape[0]
    GL0 = (DP + 2 * DR) // 512

    def gl_spec(k):
        return pl.BlockSpec((tm, 512), lambda i: (i, GL0 + k))

    def body(x_ref, gl0, gl1, gl2, gl3, yp_ref, yr_ref, wo_ref, gf_ref, mix_ref, x2_ref, h2_ref):
        s_p = _sigmoid(jnp.concatenate([gl0[...], gl1[...]], axis=1))
        s_r = _sigmoid(jnp.concatenate([gl2[...], gl3[...]], axis=1))
        mix = (s_p * yp_ref[...].astype(F32) + s_r * yr_ref[...].astype(F32)).astype(BF16)
        mix_ref[...] = mix
        x2 = x_ref[...] + _dot(mix, wo_ref[...])
        x2_ref[...] = x2
        _, xh2 = _rms(x2)
        h2_ref[...] = (xh2 * gf_ref[...]).astype(BF16)

    return _call(
        body, "merge_out", (S // tm,),
        in_specs=[_rows(D, tm), gl_spec(0), gl_spec(1), gl_spec(2), gl_spec(3), _rows(D, tm), _rows(D, tm),
                  _resident((D, D)), _resident((1, D))],
        out_specs=[_rows(D, tm), _rows(D, tm), _rows(D, tm)],
        out_shape=[jax.ShapeDtypeStruct((S, D), BF16), jax.ShapeDtypeStruct((S, D), F32), jax.ShapeDtypeStruct((S, D), BF16)],
        operands=(x, proj, proj, proj, proj, y_pool, y_rnn, w_o, norm_ffn),
        exchange=exchange, exchange_operands=exchange_operands)


def _ffn_up(h2, w_lo, w_hi, exchange=None, exchange_operands=(), tm=512):
    S = h2.shape[0]
    HALF = D // 2

    def body(h_ref, lo_ref, hi_ref, back_ref, act_ref):
        h_lo, h_hi = h_ref[:, 0:HALF], h_ref[:, HALF:D]
        for c0, c1 in FF_CHUNKS:
            gate = _dot_nt(h_lo, lo_ref[c0:c1, :]) + _dot_nt(h_hi, hi_ref[c0:c1, :])
            up = _dot_nt(h_lo, lo_ref[DFF + c0:DFF + c1, :]) + _dot_nt(h_hi, hi_ref[DFF + c0:DFF + c1, :])
            sg = _sigmoid(gate)
            silu = gate * sg
            back_ref[:, c0:c1] = (up * (sg * (1.0 + gate * (1.0 - sg)))).astype(BF16)
            back_ref[:, DFF + c0:DFF + c1] = silu.astype(BF16)
            act_ref[:, c0:c1] = (silu * up).astype(BF16)

    return _call(
        body, "ffn_up", (S // tm,),
        in_specs=[_rows(D, tm), _resident((2 * DFF, HALF)), _resident((2 * DFF, HALF))],
        out_specs=[_rows(2 * DFF, tm), _rows(DFF, tm)],
        out_shape=[jax.ShapeDtypeStruct((S, 2 * DFF), BF16), jax.ShapeDtypeStruct((S, DFF), BF16)],
        operands=(h2, w_lo, w_hi), exchange=exchange, exchange_operands=exchange_operands)


def _ffn_down_loss(act, x2, target, w_ffn_out, norm_final, tm=512):
    S = act.shape[0]

    def body(act_ref, x2_ref, t_ref, w_ref, gn_ref, dx3_ref, dx3b_ref, loss_ref, dvec_ref):
        i = pl.program_id(0)

        @pl.when(i == 0)
        def _():
            loss_ref[...] = jnp.zeros_like(loss_ref)
            dvec_ref[...] = jnp.zeros_like(dvec_ref)

        x3 = x2_ref[...] + _dot(act_ref[...], w_ref[...])
        r3, xh3 = _rms(x3)
        g_fin = gn_ref[...]
        e = xh3 * g_fin - t_ref[...]
        loss_ref[...] += jnp.sum(e * e, axis=(0, 1), keepdims=True) * (0.5 / D)
        dy = e * (1.0 / D)
        dvec_ref[0:1, :] += jnp.sum(dy * xh3, axis=0, keepdims=True)
        dx3 = _rms_bwd(dy, g_fin, r3, xh3)
        dx3_ref[...] = dx3
        dx3b_ref[...] = dx3.astype(BF16)

    return pl.pallas_call(
        body, name="ffn_down_loss", grid=(S // tm,),
        in_specs=[_rows(DFF, tm), _rows(D, tm), _rows(D, tm), _resident((DFF, D)), _resident((1, D))],
        out_specs=[_rows(D, tm), _rows(D, tm), _resident((1, 1)), _resident((8, D))],
        out_shape=[jax.ShapeDtypeStruct((S, D), F32), jax.ShapeDtypeStruct((S, D), BF16),
                   jax.ShapeDtypeStruct((1, 1), F32), jax.ShapeDtypeStruct((8, D), F32)],
        compiler_params=_params("arbitrary"),
    )(act, x2, target, w_ffn_out, norm_final)


def _ffn_bwd_down(dx3b, gu, w_ffn_out, tm=512):
    S = dx3b.shape[0]

    def body(d_ref, back_ref, w_ref, dgu_ref):
        d = d_ref[...]
        for c0, c1 in FF_CHUNKS:
            dact = _dot_nt(d, w_ref[c0:c1, :])
            dgu_ref[:, c0:c1] = (dact * back_ref[:, c0:c1].astype(F32)).astype(BF16)
            dgu_ref[:, DFF + c0:DFF + c1] = (dact * back_ref[:, DFF + c0:DFF + c1].astype(F32)).astype(BF16)

    return pl.pallas_call(
        body, name="ffn_bwd_down", grid=(S // tm,),
        in_specs=[_rows(D, tm), _rows(2 * DFF, tm), _resident((DFF, D))],
        out_specs=_rows(2 * DFF, tm),
        out_shape=jax.ShapeDtypeStruct((S, 2 * DFF), BF16),
        compiler_params=_params("parallel"),
    )(dx3b, gu, w_ffn_out)


def _ffn_bwd_up(dgu, x2, dx3, w_lo, w_hi, norm_ffn, w_o, tm=512):
    S = dgu.shape[0]
    HALF = D // 2

    def body(dgu_ref, x2_ref, dx3_ref, lo_ref, hi_ref, gf_ref, wo_ref, dx2_ref, dx2b_ref, dmixo_ref, dvec_ref):
        i = pl.program_id(0)

        @pl.when(i == 0)
        def _():
            dvec_ref[...] = jnp.zeros_like(dvec_ref)

        dgate, dup = dgu_ref[:, 0:DFF], dgu_ref[:, DFF:2 * DFF]
        dh2 = jnp.concatenate([_dot(dgate, w[0:DFF, :]) + _dot(dup, w[DFF:2 * DFF, :]) for w in (lo_ref, hi_ref)], axis=1)
        r2, xh2 = _rms(x2_ref[...])
        dvec_ref[0:1, :] += jnp.sum(dh2 * xh2, axis=0, keepdims=True)
        dx2 = dx3_ref[...] + _rms_bwd(dh2, gf_ref[...], r2, xh2)
        dx2_ref[...] = dx2
        dx2b = dx2.astype(BF16)
        dx2b_ref[...] = dx2b
        dmixo_ref[...] = _dot_nt(dx2b, wo_ref[...]).astype(BF16)

    return pl.pallas_call(
        body, name="ffn_bwd_up", grid=(S // tm,),
        in_specs=[_rows(2 * DFF, tm), _rows(D, tm), _rows(D, tm), _resident((2 * DFF, HALF)), _resident((2 * DFF, HALF)),
                  _resident((1, D)), _resident((D, D))],
        out_specs=[_rows(D, tm), _rows(D, tm), _rows(D, tm), _resident((8, D))],
        out_shape=[jax.ShapeDtypeStruct((S, D), F32), jax.ShapeDtypeStruct((S, D), BF16), jax.ShapeDtypeStruct((S, D), BF16),
                   jax.ShapeDtypeStruct((8, D), F32)],
        compiler_params=_params("arbitrary"),
    )(dgu, x2, dx3, w_lo, w_hi, norm_ffn, w_o)


VEC_ROWS = 16
MAT_WA = 4 * PG
MAT_WX = MAT_WA + NH * HD
MAT_ROWS = MAT_WX + NH * HD


def _mixer_bwd(proj, dmixo, y_pool, y_rnn, hr, kept, gates, wg, scale, w_pool_out, conv_w, conv_b, wa, ba, wx, bx, lam, w_rnn_out,
               dvec_fin, dvec_ffn, loss_part, exchange=None, exchange_operands=(), tm=256):
    S = proj.shape[0]
    nt = S // tm

    def rev(cols):
        return pl.BlockSpec((tm, cols), lambda i: (nt - 1 - i, 0))

    def halo(rows_, cols):
        per = tm // rows_
        return pl.BlockSpec((rows_, cols), lambda i: (jnp.maximum((nt - 1 - i) * per - 1, 0), 0))

    def body(proj_ref, projh_ref, dmixo_ref, yp_ref, yr_ref, hr_ref, hrh_ref, kept_ref, gates_ref, wg_ref, scale_ref, wpo_ref, cw_ref, cb_ref,
             wa_ref, ba_ref, wx_ref, bx_ref, lam_ref, wro_ref, fin_ref, ffn_ref, loss_ref,
             dproj_ref, dypb_ref, dyrb_ref, dmat_ref, dvec_ref,
             q_carry, dv_carry, a_carry, g_carry, g_scr):
        i = pl.program_id(0)
        ti = nt - 1 - i

        @pl.when(i == 0)
        def _():
            q_carry[...] = jnp.zeros_like(q_carry)
            dv_carry[...] = jnp.zeros_like(dv_carry)
            a_carry[...] = jnp.zeros_like(a_carry)
            g_carry[...] = jnp.zeros_like(g_carry)
            dmat_ref[...] = jnp.zeros_like(dmat_ref)
            dvec_ref[...] = jnp.zeros_like(dvec_ref)

        rows = lax.broadcasted_iota(jnp.int32, (tm, 1), 0)
        t_glob = ti * tm + rows
        has_prev = (ti > 0).astype(F32)
        dmixo = dmixo_ref[...].astype(F32)

        s_p = _sigmoid(proj_ref[:, DP + 2 * DR:DP + 2 * DR + D])
        s_r = _sigmoid(proj_ref[:, DP + 2 * DR + D:DIN])
        dproj_ref[:, DP + 2 * DR:DP + 2 * DR + D] = (dmixo * yp_ref[...].astype(F32) * s_p * (1.0 - s_p)).astype(BF16)
        dproj_ref[:, DP + 2 * DR + D:DIN] = (dmixo * yr_ref[...].astype(F32) * s_r * (1.0 - s_r)).astype(BF16)
        dyp = (dmixo * s_p).astype(BF16)
        dyr = (dmixo * s_r).astype(BF16)
        dypb_ref[...] = dyp
        dyrb_ref[...] = dyr

        dz = _dot_nt(dyr, wro_ref[...])
        u_gate = proj_ref[:, DP + DR:DP + 2 * DR]
        gg, dgelu = _gelu(u_gate, with_grad=True)
        hr_t = hr_ref[...]
        dproj_ref[:, DP + DR:DP + 2 * DR] = (dz * hr_t * dgelu).astype(BF16)
        dhr = dz * gg

        sp = _softplus_neg(lam_ref[...])
        v, a, mult = (kept_ref[k] for k in range(KEPT))
        r, gi = (gates_ref[k].astype(F32) for k in range(2))
        inv_mult = 1.0 / mult

        C = jnp.where(rows == tm - 1, a_carry[0:1, :], pltpu.roll(a, tm - 1, axis=0))
        g_carry[0:1, :] = _linear_scan(g_scr, C, dhr, g_carry[0:1, :], reverse=True)
        a_carry[0:1, :] = a[0:1, :]
        g = g_scr[...]

        h_prev = jnp.where(rows == 0, hrh_ref[7:8, :] * has_prev, pltpu.roll(hr_t, 1, axis=0))
        da = g * h_prev
        gm = g * mult
        dmult = g * gi * v
        di = gm * v
        dv = gm * gi
        dlog_a = da * a - dmult * (a * a * inv_mult)
        dvec_ref[4:5, :] += jnp.sum(dlog_a * r, axis=0, keepdims=True)
        dra = (dlog_a * ((-LRU_C) * sp) * r * (1.0 - r))
        drx = di * gi * (1.0 - gi)
        dvec_ref[2:3, :] += jnp.sum(dra, axis=0, keepdims=True)
        dvec_ref[3:4, :] += jnp.sum(drx, axis=0, keepdims=True)
        drab = dra.astype(BF16)
        drxb = drx.astype(BF16)
        vb = v.astype(BF16)
        dvg = []
        for h in range(NH):
            sl = slice(h * HD, (h + 1) * HD)
            dvg.append(_dot_nt(drab[:, sl], wa_ref[h]) + _dot_nt(drxb[:, sl], wx_ref[h]))
            dmat_ref[MAT_WA + h * HD:MAT_WA + (h + 1) * HD, :] += _dot_tn(vb[:, sl], drab[:, sl])
            dmat_ref[MAT_WX + h * HD:MAT_WX + (h + 1) * HD, :] += _dot_tn(vb[:, sl], drxb[:, sl])
        dv = dv + jnp.concatenate(dvg, axis=1)
        dvec_ref[1:2, :] += jnp.sum(dv, axis=0, keepdims=True)
        dvext = jnp.concatenate([dv, dv_carry[...]], axis=0)
        dv_carry[...] = dv[0:CONV_HALO, :]
        n = tm + CONV_HALO
        u_rnn = proj_ref[:, DP:DP + DR]
        du_rnn = dv * cw_ref[3:4, :]
        dvec_ref[8:9, :] += jnp.sum(dv * u_rnn, axis=0, keepdims=True)
        for k in range(3):
            dv_k = pltpu.roll(dvext, n - (3 - k), axis=0)[0:tm, :]
            du_rnn = du_rnn + dv_k * cw_ref[k:k + 1, :]
            dvec_ref[5 + k:6 + k, :] += jnp.sum(dv_k * u_rnn, axis=0, keepdims=True)
        dproj_ref[:, DP:DP + DR] = du_rnn.astype(BF16)

        dpm = _dot_nt(dyp, wpo_ref[...])
        u_pool = proj_ref[:, 0:DP]
        ext = jnp.concatenate([projh_ref[:, 0:DP] * has_prev, u_pool], axis=0)
        sums = _pool_windows(ext, +1)
        scale_v = scale_ref[...]
        qs = []
        dpooled = []
        dscale = []
        for gi_, w in enumerate(WINDOWS):
            sl = slice(gi_ * PG, (gi_ + 1) * PG)
            inv_cnt = 1.0 / jnp.minimum(t_glob + 1, w).astype(F32)
            pooled_b = (sums[gi_][POOL_HALO:, :] * inv_cnt - u_pool[:, sl]).astype(BF16)
            mixed_g = _dot(pooled_b, wg_ref[gi_])
            dscale.append(jnp.sum(dpm[:, sl] * mixed_g, axis=0, keepdims=True))
            dmixed_b = (dpm[:, sl] * scale_v[:, sl]).astype(BF16)
            dmat_ref[gi_ * PG:(gi_ + 1) * PG, :] += _dot_tn(pooled_b, dmixed_b)
            dp_g = _dot_nt(dmixed_b, wg_ref[gi_])
            dpooled.append(dp_g)
            qs.append(dp_g * inv_cnt)
        dvec_ref[0:1, 0:DP] += jnp.concatenate(dscale, axis=1)
        q = jnp.concatenate(qs, axis=1)
        qext = jnp.concatenate([q, q_carry[...]], axis=0)
        q_carry[...] = q[0:POOL_HALO, :]
        tsum = _pool_windows(qext, -1)
        for gi_ in range(4):
            dproj_ref[:, gi_ * PG:(gi_ + 1) * PG] = (tsum[gi_][0:tm, :] - dpooled[gi_]).astype(BF16)

        @pl.when(i == nt - 1)
        def _():
            dvec_ref[4:5, :] = dvec_ref[4:5, :] * (LRU_C * _sigmoid(-lam_ref[...]))
            dvec_ref[VEC_NORM_FINAL:VEC_NORM_FINAL + 1, :] = fin_ref[0:1, :]
            dvec_ref[VEC_NORM_FFN:VEC_NORM_FFN + 1, :] = ffn_ref[0:1, :]
            lane = lax.broadcasted_iota(jnp.int32, (1, DR), 1)
            dvec_ref[VEC_LOSS:VEC_LOSS + 1, :] = jnp.where(lane == 0, jnp.broadcast_to(loss_ref[...], (1, DR)), 0.0)

    return _call(
        body, "mixer_bwd", (nt,),
        in_specs=[rev(DIN), halo(POOL_HALO, DIN), rev(D), rev(D), rev(D), rev(DR), halo(8, DR),
                  pl.BlockSpec((KEPT, tm, DR), lambda i: (0, nt - 1 - i, 0)),
                  pl.BlockSpec((2, tm, DR), lambda i: (0, nt - 1 - i, 0)), _resident((4, PG, PG)), _resident((1, DP)), _resident((DP, D)), _resident(conv_w.shape), _resident((1, DR)),
                  _resident((NH, HD, HD)), _resident((1, DR)), _resident((NH, HD, HD)), _resident((1, DR)),
                  _resident((1, DR)), _resident((DR, D)), _resident((8, D)), _resident((8, D)), _resident((1, 1))],
        out_specs=[rev(DIN), rev(D), rev(D), _resident((MAT_ROWS, HD)), _resident((VEC_ROWS, DR))],
        out_shape=[jax.ShapeDtypeStruct((S, DIN), BF16), jax.ShapeDtypeStruct((S, D), BF16),
                   jax.ShapeDtypeStruct((S, D), BF16), jax.ShapeDtypeStruct((MAT_ROWS, HD), F32),
                   jax.ShapeDtypeStruct((VEC_ROWS, DR), F32)],
        scratch_shapes=[pltpu.VMEM((POOL_HALO, DP), F32), pltpu.VMEM((CONV_HALO, DR), F32), pltpu.VMEM((8, DR), F32),
                        pltpu.VMEM((8, DR), F32), pltpu.VMEM((tm, DR), F32)],
        operands=(proj, proj, dmixo, y_pool, y_rnn, hr, hr, kept, gates, wg, scale, w_pool_out, conv_w, conv_b, wa, ba, wx, bx, lam,
                  w_rnn_out, dvec_fin, dvec_ffn, loss_part),
        exchange=exchange, exchange_operands=exchange_operands)


def _in_bwd(dproj, x, dx2, norm_mix, w_in, exchange=None, exchange_operands=(), tm=512):
    S = x.shape[0]

    def body(dp_ref, x_ref, dx2_ref, g_ref, w_ref, dx_ref, dg_ref):
        i = pl.program_id(0)

        @pl.when(i == 0)
        def _():
            dg_ref[...] = jnp.zeros_like(dg_ref)

        dh = _dot(dp_ref[:, 0:1536], w_ref[0:1536, :])
        dh = dh + _dot(dp_ref[:, 1536:3072], w_ref[1536:3072, :])
        dh = dh + _dot(dp_ref[:, 3072:DIN], w_ref[3072:DIN, :])
        xv = x_ref[...]
        r = lax.rsqrt(jnp.mean(xv * xv, axis=-1, keepdims=True) + EPS)
        xh = xv * r
        dg_ref[0:1, :] += jnp.sum(dh * xh, axis=0, keepdims=True)
        dxh = dh * g_ref[...]
        dx_ref[...] = dx2_ref[...] + r * (dxh - xh * jnp.mean(dxh * xh, axis=-1, keepdims=True))

    return _call(
        body, "in_bwd", (S // tm,),
        in_specs=[_rows(DIN, tm), _rows(D, tm), _rows(D, tm), _resident((1, D)), _resident((DIN, D))],
        out_specs=[_rows(D, tm), _resident((8, D))],
        out_shape=[jax.ShapeDtypeStruct((S, D), F32), jax.ShapeDtypeStruct((8, D), F32)],
        operands=(dproj, x, dx2, norm_mix, w_in), exchange=exchange, exchange_operands=exchange_operands)


def _wgrad(a, b, name, tk, tn, exchange=None, exchange_operands=()):
    S, K = a.shape
    N = b.shape[1]

    def body(a_ref, b_ref, o_ref):
        o_ref[...] = _dot_tn(a_ref[...], b_ref[...]).astype(BF16)

    (out,), exchanged = _call(
        body, name, (K // tk, N // tn),
        in_specs=[pl.BlockSpec((S, tk), lambda k, n: (0, k)), pl.BlockSpec((S, tn), lambda k, n: (0, n))],
        out_specs=[pl.BlockSpec((tk, tn), lambda k, n: (k, n))],
        out_shape=[jax.ShapeDtypeStruct((K, N), BF16)],
        operands=(a, b), exchange=exchange, exchange_operands=exchange_operands)
    return (out, exchanged) if exchange is not None else out


VEC_SCALE, VEC_CONV_B, VEC_BA, VEC_BX, VEC_LAM, VEC_CONV_W, VEC_NORM_FINAL, VEC_NORM_FFN = 0, 1, 2, 3, 4, 5, 9, 10
VEC_LOSS = 11


class _Big:
    def __init__(self, name, rows, cols, axis, n, dtype=BF16, transposed=False, src_cols=None):
        self.name, self.rows, self.cols, self.axis, self.n, self.dtype = name, rows, cols, axis, n, dtype
        self.transposed = transposed
        self.src_cols = src_cols
        self.block_shape = (rows, n) if axis == 1 else (n, cols)

    def block(self, ref, p):
        if self.axis == 1:
            return ref.at[:, pl.ds(pl.multiple_of(p * self.n, 128), self.n)]
        return ref.at[pl.ds(pl.multiple_of(p * self.n, 16 if self.dtype == BF16 else 8), self.n), :]


BIG = (_Big("w_in", DIN, D, 0, DIN // 8, transposed=True), _Big("w_pool_out", DP, D, 1, D // 8),
       _Big("w_rnn_out", DR, D, 0, DR // 8), _Big("w_o", D, D, 0, D // 8),
       _Big("w_ffn_in", 2 * DFF, D, 0, 2 * DFF // 8, transposed=True), _Big("w_ffn_out", DFF, D, 0, DFF // 8))
CONV_W = _Big("conv_w", 8, DR, 1, DR // 8, F32)
W_FFN_IN_HALVES = (_Big("w_ffn_in_lo", 2 * DFF, D // 2, 0, 2 * DFF // 8, src_cols=(0, D // 2)),
                   _Big("w_ffn_in_hi", 2 * DFF, D // 2, 0, 2 * DFF // 8, src_cols=(D // 2, D)))
GATHERED = BIG + (CONV_W,) + W_FFN_IN_HALVES

HBM_SPEC = pl.BlockSpec(memory_space=pl.ANY)
VMEM_SPEC = pl.BlockSpec(memory_space=pltpu.VMEM)


def _place():
    x, y, c = (lax.axis_index(a) for a in MESH_AXES)
    other_chips = [(1 - x, y), (x, 1 - y), (1 - x, 1 - y)]
    return x, y, c, other_chips


def _remote(src, dst, send_sems, recv_sems, idx, to):
    return pltpu.make_async_remote_copy(src_ref=src, dst_ref=dst, send_sem=send_sems.at[idx], recv_sem=recv_sems.at[idx],
                                        device_id=to, device_id_type=MESH)


def _device_index(chip, core):
    return 4 * chip[0] + 2 * chip[1] + core


class _Gather:
    def __init__(self, tensors):
        self.tensors = tuple(tensors)
        n = len(self.tensors)
        self.in_specs = [HBM_SPEC] * n
        self.out_specs = [HBM_SPEC] * n
        self.out_shape = [jax.ShapeDtypeStruct((T.rows, T.cols), T.dtype) for T in self.tensors]
        self.scratch_shapes = [pltpu.VMEM(T.block_shape, T.dtype) for T in self.tensors] + [
            pltpu.VMEM(T.block_shape, F32) for T in self.tensors] + [
            pltpu.SemaphoreType.DMA((n, 7)), pltpu.SemaphoreType.DMA((n, 7)), pltpu.SemaphoreType.DMA((n, 2))]

    collective_id = 1

    def peers(self):
        x, y, c, _ = _place()
        return [(x, y, 1 - c), (1 - x, y, c), (x, 1 - y, c)]

    def middles(self, steps):
        return [(steps // 2, self.relay), (steps - 1, self.middle)]

    def _copies(self, ins, outs, scratch):
        n = len(self.tensors)
        mine, raw, (send_sems, recv_sems, loc_sems) = scratch[:n], scratch[n:2 * n], scratch[2 * n:]
        x, y, c, chips = _place()
        sibling = (x, y, 1 - c)
        me = _device_index((x, y), c)
        relay_from = (jnp.where(c == 0, 1 - x, x), jnp.where(c == 0, y, 1 - y))
        relay_to = (jnp.where(c == 0, x, 1 - x), jnp.where(c == 0, 1 - y, y))
        loads, stores, first, relays, passed, arrivals, late = [], [], [], [], [], [], []
        for t, T in enumerate(self.tensors):
            place = T.block(outs[t], me)
            src = ins[t] if T.src_cols is None else ins[t].at[:, T.src_cols[0]:T.src_cols[1]]
            loads.append(pltpu.make_async_copy(src, raw[t], loc_sems.at[t, 0]))
            stores.append(pltpu.make_async_copy(mine[t], place, loc_sems.at[t, 1]))
            first.append(_remote(mine[t], place, send_sems, recv_sems, (t, 0), sibling))
            theirs = T.block(outs[t], _device_index((x, y), 1 - c))
            late.append(_remote(theirs, theirs, send_sems, recv_sems, (t, 0), sibling))
            relayed = T.block(outs[t], _device_index(relay_from, c))
            relays.append(_remote(relayed, relayed, send_sems, recv_sems, (t, 3), (*relay_to, c)))
            for k, chip in enumerate(chips):
                if k < 2:
                    first.append(_remote(mine[t], place, send_sems, recv_sems, (t, 1 + k), (*chip, c)))
                land = T.block(outs[t], _device_index(chip, c))
                arrivals.append(_remote(land, land, send_sems, recv_sems, (t, 1 + k), sibling))
                passed.append(_remote(land, land, send_sems, recv_sems, (t, 4 + k), sibling))
                theirs = T.block(outs[t], _device_index(chip, 1 - c))
                late.append(_remote(theirs, theirs, send_sems, recv_sems, (t, 4 + k), sibling))
        return loads, stores, first, relays, passed, arrivals, late

    def start(self, ins, outs, scratch):
        loads, stores, first, _, _, _, _ = self._copies(ins, outs, scratch)
        n = len(self.tensors)
        for cp in loads:
            cp.start()
        for t, cp in enumerate(loads):
            cp.wait()
            scratch[t][...] = scratch[n + t][...].astype(self.tensors[t].dtype)
        for cp in stores + first:
            cp.start()

    def relay(self, ins, outs, scratch, skip=0):
        _, _, _, relays, passed, arrivals, _ = self._copies(ins, outs, scratch)
        for t in range(skip, len(self.tensors)):
            arrivals[3 * t].wait_recv()
            arrivals[3 * t + 1].wait_recv()
            for cp in (relays[t], passed[3 * t], passed[3 * t + 1]):
                cp.start()

    def middle(self, ins, outs, scratch, skip=0):
        _, _, _, _, passed, arrivals, _ = self._copies(ins, outs, scratch)
        for t in range(skip, len(self.tensors)):
            arrivals[3 * t + 2].wait_recv()
            passed[3 * t + 2].start()

    def finish(self, ins, outs, scratch, skip=0):
        _, stores, first, relays, passed, _, late = self._copies(ins, outs, scratch)
        for cp in late[4 * skip:]:
            cp.wait_recv()
        for cp in first + relays + passed:
            cp.wait_send()
        for cp in stores[skip:]:
            cp.wait()


def _in_proj_gather(x, norm_mix, blocks, tensors, order, tm=1024):
    S = x.shape[0]
    nt = S // tm
    n = len(tensors)
    gather = _Gather(tensors)
    CB = 2 * tensors[0].n

    def body(order_ref, x_ref, g_ref, *refs):
        ins, (proj_ref, h_ref), outs = refs[:n], refs[n:n + 2], refs[n + 2:2 * n + 2]
        (h_all, w_chip, w_sem), scratch = refs[2 * n + 2:2 * n + 5], refs[2 * n + 5:]
        q, i = pl.program_id(0), pl.program_id(1)
        _, stores, _, relays, passed, arrivals, late = gather._copies(ins, outs, scratch)

        def fetch(turn):
            rows = outs[0].at[pl.ds(pl.multiple_of(order_ref[turn] * CB, 16), CB), :]
            cp = pltpu.make_async_copy(rows, w_chip, w_sem)
            cp.start()
            cp.wait()

        @pl.when((q == 0) & (i == 0))
        def _():
            _enter(gather)
            gather.start(ins, outs, scratch)
            late[0].wait_recv()
            stores[0].wait()
            fetch(0)

        @pl.when((q == 1) & (i == 0))
        def _():
            arrivals[0].wait_recv()
            arrivals[1].wait_recv()
            for cp in (relays[0], passed[0], passed[1]):
                cp.start()
            late[1].wait_recv()
            fetch(1)

        @pl.when((q == 2) & (i == 0))
        def _():
            late[2].wait_recv()
            fetch(2)
            gather.relay(ins, outs, scratch, skip=1)

        @pl.when((q == 3) & (i == 0))
        def _():
            arrivals[2].wait_recv()
            passed[2].start()
            late[3].wait_recv()
            fetch(3)

        rows = pl.ds(pl.multiple_of(i * tm, tm), tm)

        @pl.when(q == 0)
        def _():
            xv = x_ref[...]
            r = lax.rsqrt(jnp.mean(xv * xv, axis=-1, keepdims=True) + EPS)
            h = (xv * r * g_ref[...]).astype(BF16)
            h_all[rows, :] = h
            h_ref[...] = h

        proj_ref[...] = _dot_nt(h_all[rows, :], w_chip[...])

        @pl.when((q == 3) & (i == nt - 1))
        def _():
            gather.middle(ins, outs, scratch, skip=1)
            gather.finish(ins, outs, scratch, skip=1)

    row_tile = lambda q, i, order: (jnp.where(q == 0, i, nt - 1), 0)
    whole = lambda shape: pl.BlockSpec(shape, lambda q, i, order: (0,) * len(shape), pipeline_mode=pl.Buffered(1))
    outs = pl.pallas_call(
        body, name="in_proj_gather",
        grid_spec=pltpu.PrefetchScalarGridSpec(
            num_scalar_prefetch=1, grid=(4, nt),
            in_specs=[pl.BlockSpec((tm, D), row_tile), whole((1, D))] + gather.in_specs,
            out_specs=[pl.BlockSpec((tm, CB), lambda q, i, order: (i, order[q])), pl.BlockSpec((tm, D), row_tile)]
            + gather.out_specs,
            scratch_shapes=[pltpu.VMEM((S, D), BF16), pltpu.VMEM((CB, D), BF16), pltpu.SemaphoreType.DMA]
            + gather.scratch_shapes),
        out_shape=[jax.ShapeDtypeStruct((S, DIN), F32), jax.ShapeDtypeStruct((S, D), BF16)] + gather.out_shape,
        compiler_params=pltpu.CompilerParams(dimension_semantics=("arbitrary", "arbitrary"), vmem_limit_bytes=VMEM_LIMIT,
                                             collective_id=gather.collective_id),
    )(order, x, norm_mix, *blocks)
    return outs[:2], outs[2:]


PAIR_ROWS = 32


class _PairReduce:
    collective_id = 3

    def __init__(self, tensors):
        self.tensors = tuple(tensors)
        nt = len(self.tensors)
        blocks = [T.block_shape for T in self.tensors]
        self.in_specs = [HBM_SPEC] * nt
        self.out_specs = [HBM_SPEC] * (2 * nt)
        self.out_shape = ([jax.ShapeDtypeStruct(b, BF16) for b in blocks]
                          + [jax.ShapeDtypeStruct((3,) + b, BF16) for b in blocks])
        self.scratch_shapes = ([pltpu.VMEM((4,) + b, BF16) for b in blocks] + [pltpu.VMEM((3,) + b, BF16) for b in blocks]
                               + [pltpu.SemaphoreType.DMA((nt, 4)), pltpu.SemaphoreType.DMA((nt, 4)),
                                  pltpu.SemaphoreType.DMA((nt, 5))])

    def peers(self):
        x, y, c, _ = _place()
        return [(x, y, 1 - c)]

    def middles(self, steps):
        return []

    def _copies(self, ins, outs, scratch):
        nt = len(self.tensors)
        own_out, sums_out, landed, mine = outs[:nt], outs[nt:], scratch[:nt], scratch[nt:2 * nt]
        send_sems, recv_sems, loc_sems = scratch[2 * nt:]
        x, y, c, chips = _place()
        chip_of = [2 * chip[0] + chip[1] for chip in chips]
        swaps, loads, stores = [], [], []
        for t, T in enumerate(self.tensors):
            for j in range(4):
                swaps.append(_remote(T.block(ins[t], 2 * j + 1 - c), landed[t].at[j], send_sems, recv_sems, (t, j),
                                     (x, y, 1 - c)))
            for k in range(3):
                loads.append(pltpu.make_async_copy(T.block(ins[t], 2 * chip_of[k] + c), mine[t].at[k], loc_sems.at[t, k]))
            stores.append(pltpu.make_async_copy(mine[t], sums_out[t], loc_sems.at[t, 3]))
            stores.append(pltpu.make_async_copy(landed[t].at[2 * x + y], own_out[t], loc_sems.at[t, 4]))
        return swaps, loads, stores, landed, mine, chip_of

    def start(self, ins, outs, scratch):
        swaps, loads, _, _, _, _ = self._copies(ins, outs, scratch)
        for cp in swaps + loads:
            cp.start()

    def finish(self, ins, outs, scratch):
        swaps, loads, stores, landed, mine, chip_of = self._copies(ins, outs, scratch)
        for cp in loads:
            cp.wait()
        for cp in swaps:
            cp.wait_recv()
        for t, T in enumerate(self.tensors):
            for k in range(3):
                acc, got = mine[t].at[k], landed[t].at[chip_of[k]]

                def add(i, carry, acc=acc, got=got):
                    rows = pl.ds(pl.multiple_of(i * PAIR_ROWS, PAIR_ROWS), PAIR_ROWS)
                    acc[rows, :] = (acc[rows, :].astype(F32) + got[rows, :].astype(F32)).astype(BF16)
                    return carry

                lax.fori_loop(0, T.block_shape[0] // PAIR_ROWS, add, 0)
        for cp in stores:
            cp.start()
        for cp in swaps:
            cp.wait_send()
        for cp in stores:
            cp.wait()


def _pair_reduce(grads, tensors, name):
    reduce = _PairReduce(tensors)
    nt = len(reduce.tensors)

    def body(*refs):
        ins, outs, scratch = refs[:nt], refs[nt:3 * nt], refs[3 * nt:]
        _enter(reduce)
        reduce.start(ins, outs, scratch)
        reduce.finish(ins, outs, scratch)

    return pl.pallas_call(
        body, name=name, in_specs=reduce.in_specs, out_specs=reduce.out_specs, out_shape=reduce.out_shape,
        scratch_shapes=reduce.scratch_shapes,
        compiler_params=pltpu.CompilerParams(vmem_limit_bytes=VMEM_LIMIT, collective_id=reduce.collective_id),
    )(*grads)


class _Scatter:
    def __init__(self, tensors):
        self.tensors = tuple(tensors)
        n = len(self.tensors)
        self.in_specs = [HBM_SPEC] * n
        self.out_specs = [HBM_SPEC] * n
        self.out_shape = [jax.ShapeDtypeStruct((2,) + T.block_shape, BF16) for T in self.tensors]
        self.scratch_shapes = [pltpu.VMEM(T.block_shape, BF16) for T in self.tensors] * 2 + [
            pltpu.SemaphoreType.DMA((n, 3)), pltpu.SemaphoreType.DMA((n, 3)), pltpu.SemaphoreType.DMA((n,))]

    collective_id = 2

    def peers(self):
        x, y, c, _ = _place()
        return [(1 - x, y, c), (x, 1 - y, c)]

    def middles(self, steps):
        return [(steps // 2, self.middle)]

    def _copies(self, ins, outs, scratch):
        n = len(self.tensors)
        landed, mine, (send_sems, recv_sems, loc_sems) = scratch[:n], scratch[n:2 * n], scratch[2 * n:]
        x, y, c, _ = _place()
        direct = (jnp.where(c == 0, 1 - x, x), jnp.where(c == 0, y, 1 - y), c)
        other = (jnp.where(c == 0, x, 1 - x), jnp.where(c == 0, 1 - y, y), c)
        k_direct = jnp.where(c == 0, 0, 1)
        to_direct, legs, loads, combined, arrivals = [], [], [], [], []
        for t in range(n):
            to_direct.append(_remote(ins[t].at[k_direct], outs[t].at[0], send_sems, recv_sems, (t, 0), direct))
            legs.append(_remote(ins[t].at[2], landed[t], send_sems, recv_sems, (t, 2), direct))
            loads.append(pltpu.make_async_copy(ins[t].at[1 - k_direct], mine[t], loc_sems.at[t]))
            combined.append(_remote(mine[t], outs[t].at[1], send_sems, recv_sems, (t, 1), other))
            arrivals.append(_remote(landed[t], landed[t], send_sems, recv_sems, (t, 2), direct))
        return to_direct, legs, loads, combined, arrivals, landed, mine

    def start(self, ins, outs, scratch):
        to_direct, legs, loads, _, _, _, _ = self._copies(ins, outs, scratch)
        for cp in to_direct + legs + loads:
            cp.start()

    def middle(self, ins, outs, scratch):
        _, _, loads, combined, arrivals, landed, mine = self._copies(ins, outs, scratch)
        for t, T in enumerate(self.tensors):
            loads[t].wait()
            arrivals[t].wait_recv()
            acc, got = mine[t], landed[t]

            def add(i, carry, acc=acc, got=got):
                rows = pl.ds(pl.multiple_of(i * PAIR_ROWS, PAIR_ROWS), PAIR_ROWS)
                acc[rows, :] = (acc[rows, :].astype(F32) + got[rows, :].astype(F32)).astype(BF16)
                return carry

            lax.fori_loop(0, T.block_shape[0] // PAIR_ROWS, add, 0)
            combined[t].start()

    def finish(self, ins, outs, scratch):
        to_direct, legs, _, combined, _, _, _ = self._copies(ins, outs, scratch)
        for cp in to_direct + combined:
            cp.wait()
        for cp in legs:
            cp.wait_send()


class _ReduceScatter:
    collective_id = 1

    def __init__(self, tensors):
        self.first, self.second = _PairReduce(tensors), _Scatter(tensors)
        self.in_specs = self.first.in_specs
        self.out_specs = self.first.out_specs + self.second.out_specs
        self.out_shape = self.first.out_shape + self.second.out_shape
        self.scratch_shapes = self.first.scratch_shapes + self.second.scratch_shapes

    def peers(self):
        x, y, c, _ = _place()
        return [(x, y, 1 - c), (1 - x, y, c), (x, 1 - y, c)]

    def _parts(self, ins, outs, scratch):
        n_out, n_scr = len(self.first.out_specs), len(self.first.scratch_shapes)
        pair_sums = list(outs[n_out // 2:n_out])
        return (ins, outs[:n_out], scratch[:n_scr]), (pair_sums, outs[n_out:], scratch[n_scr:])

    def start(self, ins, outs, scratch):
        self.first.start(*self._parts(ins, outs, scratch)[0])

    def middles(self, steps):
        def hand_over(ins, outs, scratch):
            first, second = self._parts(ins, outs, scratch)
            self.first.finish(*first)
            self.second.start(*second)

        def relay(ins, outs, scratch):
            self.second.middle(*self._parts(ins, outs, scratch)[1])

        return [(steps // 4, hand_over), ((5 * steps) // 8, relay)]

    def finish(self, ins, outs, scratch):
        self.second.finish(*self._parts(ins, outs, scratch)[1])


def _adamw(w, g, m, v):
    m = ADAM_B1 * m + (1.0 - ADAM_B1) * g
    v = ADAM_B2 * v + (1.0 - ADAM_B2) * (g * g)
    m_hat = m / (1.0 - ADAM_B1 ** ADAM_STEP)
    v_hat = v / (1.0 - ADAM_B2 ** ADAM_STEP)
    delta = -ADAM_LR * (m_hat / (jnp.sqrt(v_hat) + ADAM_EPS) + ADAM_WD * w)
    return delta, m, v


def _final_sum(T, g, lz1, lz2, where, w, m, v):
    rows, cols = T.block_shape
    sub = 4 if T.axis == 0 and rows % 64 == 0 and rows > 256 else 1
    blk = (rows // sub, cols)

    def body(where_ref, g_ref, l1_ref, l2_ref, w_ref, m_ref, v_ref, g_out, d_out, m_out, v_out):
        tot = g_ref[...].astype(F32) + l1_ref[...].astype(F32)
        for k in range(2):
            tot = tot + l2_ref[k].astype(F32)
        g_out[...] = tot
        d_out[...], m_out[...], v_out[...] = _adamw(w_ref[...], tot, m_ref[...], v_ref[...])

    def in_whole(r, wh):
        p = wh[0]
        return (0, p) if T.axis == 1 else (p * sub + r, 0)

    own = pl.BlockSpec(blk, lambda r, wh: (r, 0))
    return pl.pallas_call(
        body, name="grad_final_" + T.name,
        grid_spec=pltpu.PrefetchScalarGridSpec(
            num_scalar_prefetch=1, grid=(sub,),
            in_specs=[pl.BlockSpec(blk, in_whole),
                      own,
                      pl.BlockSpec((2,) + blk, lambda r, wh: (0, r, 0)), own, own, own],
            out_specs=[own] * 4),
        out_shape=[jax.ShapeDtypeStruct(T.block_shape, F32)] * 4,
        compiler_params=_params("arbitrary"),
    )(where, g, lz1, lz2, w, m, v)


VEC_PIECE = DR // 8


class _AllReduce:
    def __init__(self, items):
        self.items = tuple(items)
        n = len(self.items)
        self.in_specs = [HBM_SPEC] * n
        self.out_specs = [HBM_SPEC] * n
        self.out_shape = [jax.ShapeDtypeStruct(shape, F32) for shape, _ in self.items]
        pieces = [(shape[0] // 8, shape[1]) if axis == 0 else (shape[0], shape[1] // 8) for shape, axis in self.items]
        self.scratch_shapes = ([pltpu.VMEM((8,) + p, F32) for p in pieces] + [pltpu.VMEM(p, F32) for p in pieces] + [
            pltpu.SemaphoreType.DMA((2 * n, 8)), pltpu.SemaphoreType.DMA((2 * n, 8)), pltpu.SemaphoreType.DMA((2 * n,))])

    collective_id = None

    def peers(self):
        return []

    def middles(self, steps):
        return [(steps // 2, self.middle)]

    def _copies(self, ins, outs, scratch):
        n = len(self.items)
        landed, sums, (send_sems, recv_sems, loc_sems) = scratch[:n], scratch[n:2 * n], scratch[2 * n:]
        x, y, c, _ = _place()
        me = _device_index((x, y), c)

        def peer(r):
            return (1 - x if r & 4 else x, 1 - y if r & 2 else y, 1 - c if r & 1 else c)

        def piece(i, ref, p):
            shape, axis = self.items[i]
            if axis == 0:
                rows = shape[0] // 8
                return ref.at[pl.ds(pl.multiple_of(p * rows, 8), rows), :]
            cols = shape[1] // 8
            return ref.at[:, pl.ds(pl.multiple_of(p * cols, 128), cols)]

        own, scatter, arrivals, keep, spread, late = [], [], [], [], [], []
        for i in range(n):
            own.append(pltpu.make_async_copy(piece(i, ins[i], me), landed[i].at[0], loc_sems.at[2 * i]))
            keep.append(pltpu.make_async_copy(sums[i], piece(i, outs[i], me), loc_sems.at[2 * i + 1]))
            for r in range(1, 8):
                to = peer(r)
                p = _device_index(to[:2], to[2])
                scatter.append(_remote(piece(i, ins[i], p), landed[i].at[r], send_sems, recv_sems, (2 * i, r), to))
                spread.append(_remote(sums[i], piece(i, outs[i], me), send_sems, recv_sems, (2 * i + 1, r), to))
                late.append(_remote(sums[i], piece(i, outs[i], p), send_sems, recv_sems, (2 * i + 1, r), to))
        return own, scatter, keep, spread, late, landed, sums

    def start(self, ins, outs, scratch):
        own, scatter, _, _, _, _, _ = self._copies(ins, outs, scratch)
        for cp in own + scatter:
            cp.start()

    def middle(self, ins, outs, scratch):
        own, scatter, keep, spread, _, landed, sums = self._copies(ins, outs, scratch)
        for cp in own:
            cp.wait()
        for cp in scatter:
            cp.wait_recv()
        for i in range(len(self.items)):
            total = landed[i][0]
            for r in range(1, 8):
                total = total + landed[i][r]
            sums[i][...] = total
        for cp in keep + spread:
            cp.start()

    def finish(self, ins, outs, scratch):
        _, scatter, keep, spread, late, _, _ = self._copies(ins, outs, scratch)
        for cp in late:
            cp.wait_recv()
        for cp in scatter + spread:
            cp.wait_send()
        for cp in keep:
            cp.wait()


class _Both:
    def __init__(self, a, b):
        self.a, self.b = a, b
        self.in_specs, self.out_specs = a.in_specs + b.in_specs, a.out_specs + b.out_specs
        self.out_shape, self.scratch_shapes = a.out_shape + b.out_shape, a.scratch_shapes + b.scratch_shapes

    collective_id = None

    def peers(self):
        return []

    def _each(self, ins, outs, scratch):
        a = self.a
        i, o, s = len(a.in_specs), len(a.out_specs), len(a.scratch_shapes)
        return (a, ins[:i], outs[:o], scratch[:s]), (self.b, ins[i:], outs[o:], scratch[s:])

    def middles(self, steps):
        def of(which, middle):
            return lambda ins, outs, scratch: middle(*self._each(ins, outs, scratch)[which][1:])
        return [(at, of(which, middle)) for which, e in enumerate((self.a, self.b)) for at, middle in e.middles(steps)]

    def start(self, ins, outs, scratch):
        for e, i, o, s in self._each(ins, outs, scratch):
            e.start(i, o, s)

    def finish(self, ins, outs, scratch):
        for e, i, o, s in self._each(ins, outs, scratch):
            e.finish(i, o, s)


def _all_reduce(arrays, items, name):
    reduce = _AllReduce(items)
    n = len(items)

    def body(*refs):
        ins, outs, scratch = refs[:n], refs[n:2 * n], refs[2 * n:]
        reduce.start(ins, outs, scratch)
        reduce.middle(ins, outs, scratch)
        reduce.finish(ins, outs, scratch)

    return pl.pallas_call(
        body, name=name, in_specs=reduce.in_specs, out_specs=reduce.out_specs, out_shape=reduce.out_shape,
        scratch_shapes=reduce.scratch_shapes,
    )(*arrays)


def _adam_small(blocks, cuts, wmv):
    n, n_blocks = len(cuts), len(blocks)
    loose = [c for c in cuts if not isinstance(c, tuple)]
    shapes = [(c[1].stop - c[1].start, c[2].stop - c[2].start) if isinstance(c, tuple) else c.shape for c in cuts]

    def body(*refs):
        b_refs, refs = refs[:n_blocks], refs[n_blocks:]
        l_refs, refs = list(refs[:len(loose)]), refs[len(loose):]
        ins, outs = refs[:3 * n], refs[3 * n:]
        for i, c in enumerate(cuts):
            g = b_refs[c[0]][c[1], c[2]] if isinstance(c, tuple) else l_refs.pop(0)[...]
            outs[4 * i][...] = g
            outs[4 * i + 1][...], outs[4 * i + 2][...], outs[4 * i + 3][...] = _adamw(
                ins[3 * i][...], g, ins[3 * i + 1][...], ins[3 * i + 2][...])

    flat = [a.reshape(shape) for t, shape in zip(wmv, shapes) for a in t]
    return pl.pallas_call(
        body, name="adam_small",
        in_specs=[VMEM_SPEC] * (n_blocks + len(loose) + 3 * n), out_specs=[VMEM_SPEC] * (4 * n),
        out_shape=[jax.ShapeDtypeStruct(shape, F32) for shape in shapes for _ in range(4)],
    )(*blocks, *loose, *flat)


WEIGHT_NAMES = ("norm_mix", "w_in", "w_pool_grp", "pool_scale", "w_pool_out", "conv_w", "conv_b", "w_rg_a", "b_rg_a", "w_rg_x",
                "b_rg_x", "lru_lambda", "w_rnn_out", "w_o", "norm_ffn", "w_ffn_in", "w_ffn_out", "norm_final")


def kernel(x, norm_mix, w_in, w_pool_grp, pool_scale, w_pool_out, conv_w, conv_b, w_rg_a, b_rg_a, w_rg_x, b_rg_x, lru_lambda, w_rnn_out, w_o, norm_ffn, w_ffn_in, w_ffn_out, norm_final, loss_target, m_norm_mix, m_w_in, m_w_pool_grp, m_pool_scale, m_w_pool_out, m_conv_w, m_conv_b, m_w_rg_a, m_b_rg_a, m_w_rg_x, m_b_rg_x, m_lru_lambda, m_w_rnn_out, m_w_o, m_norm_ffn, m_w_ffn_in, m_w_ffn_out, m_norm_final, v_norm_mix, v_w_in, v_w_pool_grp, v_pool_scale, v_w_pool_out, v_conv_w, v_conv_b, v_w_rg_a, v_b_rg_a, v_w_rg_x, v_b_rg_x, v_lru_lambda, v_w_rnn_out, v_w_o, v_norm_ffn, v_w_ffn_in, v_w_ffn_out, v_norm_final):
    w = dict(norm_mix=norm_mix, w_in=w_in, w_pool_grp=w_pool_grp, pool_scale=pool_scale, w_pool_out=w_pool_out, conv_w=conv_w,
             conv_b=conv_b, w_rg_a=w_rg_a, b_rg_a=b_rg_a, w_rg_x=w_rg_x, b_rg_x=b_rg_x, lru_lambda=lru_lambda,
             w_rnn_out=w_rnn_out, w_o=w_o, norm_ffn=norm_ffn, w_ffn_in=w_ffn_in, w_ffn_out=w_ffn_out, norm_final=norm_final)
    m = dict(norm_mix=m_norm_mix, w_in=m_w_in, w_pool_grp=m_w_pool_grp, pool_scale=m_pool_scale, w_pool_out=m_w_pool_out,
             conv_w=m_conv_w, conv_b=m_conv_b, w_rg_a=m_w_rg_a, b_rg_a=m_b_rg_a, w_rg_x=m_w_rg_x, b_rg_x=m_b_rg_x,
             lru_lambda=m_lru_lambda, w_rnn_out=m_w_rnn_out, w_o=m_w_o, norm_ffn=m_norm_ffn, w_ffn_in=m_w_ffn_in,
             w_ffn_out=m_w_ffn_out, norm_final=m_norm_final)
    v = dict(norm_mix=v_norm_mix, w_in=v_w_in, w_pool_grp=v_w_pool_grp, pool_scale=v_pool_scale, w_pool_out=v_w_pool_out,
             conv_w=v_conv_w, conv_b=v_conv_b, w_rg_a=v_w_rg_a, b_rg_a=v_b_rg_a, w_rg_x=v_w_rg_x, b_rg_x=v_b_rg_x,
             lru_lambda=v_lru_lambda, w_rnn_out=v_w_rnn_out, w_o=v_w_o, norm_ffn=v_norm_ffn, w_ffn_in=v_w_ffn_in,
             w_ffn_out=v_w_ffn_out, norm_final=v_norm_final)
    xi, yi, ci = (lax.axis_index(a) for a in MESH_AXES)
    chip = 2 * xi + yi

    def held(T, a):
        return jnp.swapaxes(a, 0, 1) if T.transposed else a

    where = jnp.stack([2 * chip + ci]).astype(jnp.int32)
    by_name = {T.name: T for T in GATHERED}
    block = {T.name: held(T, w[T.name][0]) for T in BIG}
    block["conv_w"] = jnp.pad(conv_w[0], ((0, CONV_W.rows - 4), (0, 0)))
    block["w_ffn_in_lo"] = block["w_ffn_in_hi"] = block["w_ffn_in"]

    def gather_of(*names):
        return dict(exchange=_Gather([by_name[n] for n in names]), exchange_operands=[block[n] for n in names])

    def pair_sums(names, partials, tag):
        out = _pair_reduce(partials, [by_name[n] for n in names], "grad_pair_reduce_" + tag)
        return list(out[:len(names)]), list(out[len(names):])

    xs, target = x[0], loss_target[0]
    wg_b, wa_b, wx_b = (a[0].astype(BF16) for a in (w_pool_grp, w_rg_a, w_rg_x))
    ba2, bx2 = b_rg_a.reshape(1, DR), b_rg_x.reshape(1, DR)
    first = ("w_in", "w_pool_out", "w_rnn_out", "conv_w", "w_o")
    order = jnp.stack([chip, 2 * (1 - xi) + yi, 2 * xi + (1 - yi), 2 * (1 - xi) + (1 - yi)]).astype(jnp.int32)
    (proj, h1), (w_in_g, w_pool_out_g, w_rnn_out_g, conv_g, w_o_g) = _in_proj_gather(
        xs, norm_mix, [block[n] for n in first], [by_name[n] for n in first], order)
    mixer_weights = (wg_b, pool_scale, w_pool_out_g, conv_g, conv_b, wa_b, ba2, wx_b, bx2, lru_lambda, w_rnn_out_g)
    (pm, y_pool, hr, z, y_rnn, kept, gates), (w_ffn_lo_g, w_ffn_hi_g) = _mixer_fwd(
        proj, *mixer_weights, **gather_of("w_ffn_in_lo", "w_ffn_in_hi"))
    (mix, x2, h2), _ = _merge_out(xs, proj, y_pool, y_rnn, w_o_g, norm_ffn)
    (gu, act), (w_ffn_out_g,) = _ffn_up(h2, w_ffn_lo_g, w_ffn_hi_g, **gather_of("w_ffn_out"))
    dx3, dx3b, loss_part, dvec_fin = _ffn_down_loss(act, x2, target, w_ffn_out_g, norm_final.reshape(1, D))

    dgu = _ffn_bwd_down(dx3b, gu, w_ffn_out_g)
    dx2, dx2b, dmixo, dvec_ffn = _ffn_bwd_up(dgu, x2, dx3, w_ffn_lo_g, w_ffn_hi_g, norm_ffn, w_o_g)
    names_a = ("w_ffn_in", "w_ffn_out", "w_o")
    g_ffn_out = _wgrad(act, dx3b, "wgrad_ffn_out", 1408, 512)
    g_ffn_in, (own_ffn_out, sums_ffn_out) = _wgrad(
        dgu, h2, "wgrad_ffn_in", 1408, 512, exchange=_PairReduce([by_name["w_ffn_out"]]), exchange_operands=[g_ffn_out])
    g_o, (own_ffn_in, sums_ffn_in) = _wgrad(
        mix, dx2b, "wgrad_o", 1024, 256, exchange=_PairReduce([by_name["w_ffn_in"]]), exchange_operands=[g_ffn_in])
    (own_o,), (sums_o,) = pair_sums(("w_o",), [g_o], "o")
    part_a = [g_ffn_in, g_ffn_out, g_o]
    lz1_a, sums_a = [own_ffn_in, own_ffn_out, own_o], [sums_ffn_in, sums_ffn_out, sums_o]
    (dproj, dypb, dyrb, dmat, dvec), lz2_a = _mixer_bwd(
        proj, dmixo, y_pool, y_rnn, hr, kept, gates, *mixer_weights, dvec_fin, dvec_ffn, loss_part,
        exchange=_Scatter([by_name[n] for n in names_a]), exchange_operands=sums_a)
    names_b = ("w_pool_out", "w_rnn_out")
    part_b = [_wgrad(pm, dypb, "wgrad_pool_out", 512, 256), _wgrad(z, dyrb, "wgrad_rnn_out", 1024, 256)]
    lz1_b, sums_b = pair_sums(names_b, part_b, "mix")
    g_in, exchanged = _wgrad(
        dproj, h1, "wgrad_in", 1152, 1024,
        exchange=_Both(_Scatter([by_name[n] for n in names_b]), _AllReduce([((MAT_ROWS, HD), 0), ((VEC_ROWS, DR), 1)])),
        exchange_operands=sums_b + [dmat, dvec])
    lz2_b, (mat, vec) = exchanged[:2], exchanged[2:]
    loss = vec[VEC_LOSS, 0]
    (grad_x, dvec_in), (own_in, _, scattered_in) = _in_bwd(
        dproj, xs, dx2, norm_mix, w_in_g, exchange=_ReduceScatter([by_name["w_in"]]), exchange_operands=[g_in])
    lz1_c, lz2_c = [own_in], [scattered_in]
    (vec_in,) = _all_reduce([dvec_in], [((8, D), 1)], "all_reduce_norm_mix")

    grads, delta, new_m, new_v = {}, {}, {}, {}
    for n, g, l1, l2 in zip(names_a + names_b + ("w_in",), part_a + part_b + [g_in], lz1_a + lz1_b + lz1_c,
                            list(lz2_a) + list(lz2_b) + lz2_c):
        T = by_name[n]
        out = _final_sum(T, g, l1, l2, where, held(T, w[n][0]), held(T, m[n][0]), held(T, v[n][0]))
        grads[n], delta[n], new_m[n], new_v[n] = (held(T, a) for a in out)
    me = 4 * xi + 2 * yi + ci
    rows_of = lambda block, lo, hi, cols: (block, slice(lo, hi), slice(0, cols))
    row_of = lambda block, r, cols=DR: rows_of(block, r, r + 1, cols)
    cuts = dict(
        w_pool_grp=rows_of(0, 0, MAT_WA, HD), w_rg_a=rows_of(0, MAT_WA, MAT_WX, HD), w_rg_x=rows_of(0, MAT_WX, MAT_ROWS, HD),
        pool_scale=row_of(1, VEC_SCALE, DP), conv_b=row_of(1, VEC_CONV_B), b_rg_a=row_of(1, VEC_BA), b_rg_x=row_of(1, VEC_BX),
        lru_lambda=row_of(1, VEC_LAM), conv_w=lax.dynamic_slice(vec, (VEC_CONV_W, VEC_PIECE * me), (4, VEC_PIECE)),
        norm_final=row_of(1, VEC_NORM_FINAL), norm_ffn=row_of(1, VEC_NORM_FFN), norm_mix=row_of(2, 0))
    names = list(cuts)
    upd = _adam_small([mat, vec, vec_in], [cuts[n] for n in names], [(w[n], m[n], v[n]) for n in names])
    for i, n in enumerate(names):
        grads[n], delta[n], new_m[n], new_v[n] = upd[4 * i:4 * i + 4]

    shaped = lambda d: [d[n].reshape(w[n].shape) for n in WEIGHT_NAMES]
    return (loss, grad_x[None], *shaped(grads), *shaped(delta), *shaped(new_m), *shaped(new_v))
```

```python
import math

import jax
import jax.numpy as jnp
from jax import lax
from jax.experimental import pallas as pl
from jax.experimental.pallas import tpu as pltpu

F32 = jnp.float32
BF16 = jnp.bfloat16

D = 1024
DP = 512
PG = 128
WINDOWS = (2, 4, 8, 16)
DR = 1024
NH = 8
HD = 128
DIN = 4608
DFF = 2816
EPS = 1e-6
LRU_C = 8.0
POOL_HALO = 16
CONV_HALO = 8
KEPT = 3

ADAM_LR = 0.001
ADAM_B1 = 0.9
ADAM_B2 = 0.999
ADAM_EPS = 1e-08
ADAM_WD = 0.01
ADAM_STEP = 10

VMEM_LIMIT = 56 * 1024 * 1024
MESH_AXES = ("x", "y", "c")
MESH = pl.DeviceIdType.MESH


def _dot(a, b):
    return jnp.dot(a, b, preferred_element_type=F32)


def _dot_nt(a, b):
    return lax.dot_general(a, b, (((1,), (1,)), ((), ())), preferred_element_type=F32)


def _dot_tn(a, b):
    return lax.dot_general(a, b, (((0,), (0,)), ((), ())), preferred_element_type=F32)


def _params(*sem):
    return pltpu.CompilerParams(dimension_semantics=sem, vmem_limit_bytes=VMEM_LIMIT)


def _resident(shape):
    nd = len(shape)
    return pl.BlockSpec(shape, lambda i: (0,) * nd, pipeline_mode=pl.Buffered(1))


def _rows(shape_cols, tm):
    return pl.BlockSpec((tm, shape_cols), lambda i: (i, 0))


def _call(body, name, grid, in_specs, out_specs, out_shape, operands, scratch_shapes=(), exchange=None, exchange_operands=()):
    n_in, n_out, n_scr = len(in_specs), len(out_specs), len(scratch_shapes)
    steps = math.prod(grid)
    if exchange is None:
        outs = pl.pallas_call(body, name=name, grid=grid, in_specs=in_specs, out_specs=out_specs, out_shape=out_shape,
                              scratch_shapes=list(scratch_shapes), compiler_params=_params(*["arbitrary"] * len(grid)))(*operands)
        return outs, []
    e_in, e_out = len(exchange.in_specs), len(exchange.out_specs)

    def hosted(*refs):
        ins, refs = refs[:n_in], refs[n_in:]
        e_ins, refs = refs[:e_in], refs[e_in:]
        outs, refs = refs[:n_out], refs[n_out:]
        e_outs, refs = refs[:e_out], refs[e_out:]
        scr, e_scr = refs[:n_scr], refs[n_scr:]
        step = pl.program_id(0)
        for axis in range(1, len(grid)):
            step = step * grid[axis] + pl.program_id(axis)
        @pl.when(step == 0)
        def _():
            _enter(exchange)
            exchange.start(e_ins, e_outs, e_scr)

        for at, middle in exchange.middles(steps):
            pl.when(step == at)(lambda middle=middle: middle(e_ins, e_outs, e_scr))
        body(*ins, *outs, *scr)
        pl.when(step == steps - 1)(lambda: exchange.finish(e_ins, e_outs, e_scr))

    outs = pl.pallas_call(
        hosted, name=name, grid=grid, in_specs=list(in_specs) + exchange.in_specs,
        out_specs=list(out_specs) + exchange.out_specs, out_shape=list(out_shape) + exchange.out_shape,
        scratch_shapes=list(scratch_shapes) + exchange.scratch_shapes,
        compiler_params=pltpu.CompilerParams(dimension_semantics=("arbitrary",) * len(grid), vmem_limit_bytes=VMEM_LIMIT,
                                             collective_id=exchange.collective_id))(*operands, *exchange_operands)
    return outs[:n_out], outs[n_out:]


def _enter(exchange):
    peers = exchange.peers()
    if peers:
        barrier = pltpu.get_barrier_semaphore()
        for peer in peers:
            pl.semaphore_signal(barrier, inc=1, device_id=peer, device_id_type=MESH)
        pl.semaphore_wait(barrier, len(peers))


GELU_C = math.sqrt(2.0 / math.pi)
GELU_K = 0.044715 * GELU_C


def _gelu(x, with_grad=False):
    x2 = x * x
    t = jnp.tanh(x * (GELU_C + GELU_K * x2))
    hx = 0.5 * x
    y = hx + hx * t
    if not with_grad:
        return y
    return y, 0.5 + 0.5 * t + hx * (1.0 - t * t) * (GELU_C + (3.0 * GELU_K) * x2)


def _softplus_neg(lam):
    z = jnp.exp(-jnp.abs(lam))
    u = 1.0 + z
    dlt = u - 1.0
    log1p = jnp.where(dlt == 0.0, z, jnp.log(u) * (z / jnp.where(dlt == 0.0, 1.0, dlt)))
    return jnp.maximum(-lam, 0.0) + log1p


def _sigmoid(x):
    return 0.5 * jnp.tanh(0.5 * x) + 0.5


def _linear_scan(out_ref, A, B, h0, reverse):
    n = A.shape[0]
    sub = lax.broadcasted_iota(jnp.int32, (8, 1), 0)
    tiles = range(n // 8 - 1, -1, -1) if reverse else range(n // 8)
    carry = h0
    for j in tiles:
        a, b = A[8 * j:8 * j + 8, :], B[8 * j:8 * j + 8, :]
        for d in (1, 2, 4):
            keep = (sub < 8 - d) if reverse else (sub >= d)
            shift = 8 - d if reverse else d
            b = jnp.where(keep, a * pltpu.roll(b, shift, axis=0) + b, b)
            a = jnp.where(keep, a * pltpu.roll(a, shift, axis=0), a)
        h = a * carry + b
        out_ref[8 * j:8 * j + 8, :] = h
        carry = h[0:1, :] if reverse else h[7:8, :]
    return carry


def _pool_windows(ext, shift_sign):
    n = ext.shape[0]
    s = ext
    outs = []
    for w in WINDOWS:
        d = w // 2
        s = s + pltpu.roll(s, d if shift_sign > 0 else n - d, axis=0)
        outs.append(s[:, :PG])
        s = s[:, PG:]
    return outs


def _conv_taps(uext):
    taps = []
    for k in range(4):
        sh = 3 - k
        v = uext if sh == 0 else pltpu.roll(uext, sh, axis=0)
        taps.append(v[CONV_HALO:, :])
    return taps


def _gates(v, wa_ref, ba_ref, wx_ref, bx_ref, sp):
    vb = v.astype(BF16)
    ra, rx = [], []
    for h in range(NH):
        vh = vb[:, h * HD:(h + 1) * HD]
        ra.append(_dot(vh, wa_ref[h]))
        rx.append(_dot(vh, wx_ref[h]))
    r = _sigmoid(jnp.concatenate(ra, axis=1) + ba_ref[...])
    i = _sigmoid(jnp.concatenate(rx, axis=1) + bx_ref[...])
    log_a = r * ((-LRU_C) * sp)
    a = jnp.exp(log_a)
    one_minus = -jnp.tanh(log_a) * (1.0 + a * a)
    return r, i, a, jnp.sqrt(one_minus), lax.rsqrt(one_minus)


def _mixer_fwd(proj, wg, scale, w_pool_out, conv_w, conv_b, wa, ba, wx, bx, lam, w_rnn_out, exchange=None,
               exchange_operands=(), tm=256):
    S = proj.shape[0]
    UW = DP + 2 * DR

    def body(proj_ref, wg_ref, scale_ref, wpo_ref, cw_ref, cb_ref, wa_ref, ba_ref, wx_ref, bx_ref, lam_ref, wro_ref,
             pm_ref, ypool_ref, hr_ref, z_ref, yrnn_ref, kept_ref, gates_ref, pool_carry, conv_carry, h_carry):
        i = pl.program_id(0)

        @pl.when(i == 0)
        def _():
            pool_carry[...] = jnp.zeros_like(pool_carry)
            conv_carry[...] = jnp.zeros_like(conv_carry)
            h_carry[...] = jnp.zeros_like(h_carry)

        rows = lax.broadcasted_iota(jnp.int32, (tm, 1), 0)
        t_glob = i * tm + rows

        u_pool = proj_ref[:, 0:DP]
        ext = jnp.concatenate([pool_carry[...], u_pool], axis=0)
        pool_carry[...] = u_pool[tm - POOL_HALO:, :]
        sums = _pool_windows(ext, +1)
        mixed = []
        for g, w in enumerate(WINDOWS):
            inv_cnt = 1.0 / jnp.minimum(t_glob + 1, w).astype(F32)
            pooled_g = sums[g][POOL_HALO:, :] * inv_cnt - u_pool[:, g * PG:(g + 1) * PG]
            mixed.append(_dot(pooled_g.astype(BF16), wg_ref[g]))
        pm = (jnp.concatenate(mixed, axis=1) * scale_ref[...]).astype(BF16)
        pm_ref[...] = pm
        ypool_ref[...] = _dot(pm, wpo_ref[...]).astype(BF16)

        u_rnn = proj_ref[:, DP:DP + DR]
        uext = jnp.concatenate([conv_carry[...], u_rnn], axis=0)
        conv_carry[...] = u_rnn[tm - CONV_HALO:, :]
        taps = _conv_taps(uext)
        v = cb_ref[...]
        for k in range(4):
            v = v + taps[k] * cw_ref[k:k + 1, :]
        sp = _softplus_neg(lam_ref[...])
        r, gi, a, mult, _ = _gates(v, wa_ref, ba_ref, wx_ref, bx_ref, sp)
        for k, kept in enumerate((v, a, mult)):
            kept_ref[k] = kept
        for k, kept in enumerate((r, gi)):
            gates_ref[k] = kept.astype(BF16)
        h_carry[0:1, :] = _linear_scan(hr_ref, a, mult * gi * v, h_carry[0:1, :], reverse=False)
        z = (hr_ref[...] * _gelu(proj_ref[:, DP + DR:UW])).astype(BF16)
        z_ref[...] = z
        yrnn_ref[...] = _dot(z, wro_ref[...]).astype(BF16)

    return _call(
        body, "mixer_fwd", (S // tm,),
        in_specs=[_rows(UW, tm), _resident((4, PG, PG)), _resident((1, DP)), _resident((DP, D)), _resident(conv_w.shape),
                  _resident((1, DR)), _resident((NH, HD, HD)), _resident((1, DR)), _resident((NH, HD, HD)),
                  _resident((1, DR)), _resident((1, DR)), _resident((DR, D))],
        out_specs=[_rows(DP, tm), _rows(D, tm), _rows(DR, tm), _rows(DR, tm), _rows(D, tm),
                   pl.BlockSpec((KEPT, tm, DR), lambda i: (0, i, 0)), pl.BlockSpec((2, tm, DR), lambda i: (0, i, 0))],
        out_shape=[jax.ShapeDtypeStruct((S, DP), BF16),
                   jax.ShapeDtypeStruct((S, D), BF16), jax.ShapeDtypeStruct((S, DR), F32),
                   jax.ShapeDtypeStruct((S, DR), BF16), jax.ShapeDtypeStruct((S, D), BF16),
                   jax.ShapeDtypeStruct((KEPT, S, DR), F32), jax.ShapeDtypeStruct((2, S, DR), BF16)],
        scratch_shapes=[pltpu.VMEM((POOL_HALO, DP), F32), pltpu.VMEM((CONV_HALO, DR), F32), pltpu.VMEM((8, DR), F32)],
        operands=(proj, wg, scale, w_pool_out, conv_w, conv_b, wa, ba, wx, bx, lam, w_rnn_out),
        exchange=exchange, exchange_operands=exchange_operands)


FF_CHUNKS = ((0, 768), (768, 1536), (1536, 2304), (2304, DFF))


def _rms(x):
    r = lax.rsqrt(jnp.mean(x * x, axis=-1, keepdims=True) + EPS)
    return r, x * r


def _rms_bwd(dh, g, r, xh):
    dxh = dh * g
    return r * (dxh - xh * jnp.mean(dxh * xh, axis=-1, keepdims=True))


def _merge_out(x, proj, y_pool, y_rnn, w_o, norm_ffn, exchange=None, exchange_operands=(), tm=512):
    S = x.shape[0]
    GL0 = (DP + 2 * DR) // 512

    def gl_spec(k):
        return pl.BlockSpec((tm, 512), lambda i: (i, GL0 + k))

    def body(x_ref, gl0, gl1, gl2, gl3, yp_ref, yr_ref, wo_ref, gf_ref, mix_ref, x2_ref, h2_ref):
        s_p = _sigmoid(jnp.concatenate([gl0[...], gl1[...]], axis=1))
        s_r = _sigmoid(jnp.concatenate([gl2[...], gl3[...]], axis=1))
        mix = (s_p * yp_ref[...].astype(F32) + s_r * yr_ref[...].astype(F32)).astype(BF16)
        mix_ref[...] = mix
        x2 = x_ref[...] + _dot(mix, wo_ref[...])
        x2_ref[...] = x2
        _, xh2 = _rms(x2)
        h2_ref[...] = (xh2 * gf_ref[...]).astype(BF16)

    return _call(
        body, "merge_out", (S // tm,),
        in_specs=[_rows(D, tm), gl_spec(0), gl_spec(1), gl_spec(2), gl_spec(3), _rows(D, tm), _rows(D, tm),
                  _resident((D, D)), _resident((1, D))],
        out_specs=[_rows(D, tm), _rows(D, tm), _rows(D, tm)],
        out_shape=[jax.ShapeDtypeStruct((S, D), BF16), jax.ShapeDtypeStruct((S, D), F32), jax.ShapeDtypeStruct((S, D), BF16)],
        operands=(x, proj, proj, proj, proj, y_pool, y_rnn, w_o, norm_ffn),
        exchange=exchange, exchange_operands=exchange_operands)


def _ffn_up(h2, w_lo, w_hi, exchange=None, exchange_operands=(), tm=512):
    S = h2.shape[0]
    HALF = D // 2

    def body(h_ref, lo_ref, hi_ref, back_ref, act_ref):
        h_lo, h_hi = h_ref[:, 0:HALF], h_ref[:, HALF:D]
        for c0, c1 in FF_CHUNKS:
            gate = _dot_nt(h_lo, lo_ref[c0:c1, :]) + _dot_nt(h_hi, hi_ref[c0:c1, :])
            up = _dot_nt(h_lo, lo_ref[DFF + c0:DFF + c1, :]) + _dot_nt(h_hi, hi_ref[DFF + c0:DFF + c1, :])
            sg = _sigmoid(gate)
            silu = gate * sg
            back_ref[:, c0:c1] = (up * (sg * (1.0 + gate * (1.0 - sg)))).astype(BF16)
            back_ref[:, DFF + c0:DFF + c1] = silu.astype(BF16)
            act_ref[:, c0:c1] = (silu * up).astype(BF16)

    return _call(
        body, "ffn_up", (S // tm,),
        in_specs=[_rows(D, tm), _resident((2 * DFF, HALF)), _resident((2 * DFF, HALF))],
        out_specs=[_rows(2 * DFF, tm), _rows(DFF, tm)],
        out_shape=[jax.ShapeDtypeStruct((S, 2 * DFF), BF16), jax.ShapeDtypeStruct((S, DFF), BF16)],
        operands=(h2, w_lo, w_hi), exchange=exchange, exchange_operands=exchange_operands)


def _ffn_down_loss(act, x2, target, w_ffn_out, norm_final, tm=512):
    S = act.shape[0]

    def body(act_ref, x2_ref, t_ref, w_ref, gn_ref, dx3_ref, dx3b_ref, loss_ref, dvec_ref):
        i = pl.program_id(0)

        @pl.when(i == 0)
        def _():
            loss_ref[...] = jnp.zeros_like(loss_ref)
            dvec_ref[...] = jnp.zeros_like(dvec_ref)

        x3 = x2_ref[...] + _dot(act_ref[...], w_ref[...])
        r3, xh3 = _rms(x3)
        g_fin = gn_ref[...]
        e = xh3 * g_fin - t_ref[...]
        loss_ref[...] += jnp.sum(e * e, axis=(0, 1), keepdims=True) * (0.5 / D)
        dy = e * (1.0 / D)
        dvec_ref[0:1, :] += jnp.sum(dy * xh3, axis=0, keepdims=True)
        dx3 = _rms_bwd(dy, g_fin, r3, xh3)
        dx3_ref[...] = dx3
        dx3b_ref[...] = dx3.astype(BF16)

    return pl.pallas_call(
        body, name="ffn_down_loss", grid=(S // tm,),
        in_specs=[_rows(DFF, tm), _rows(D, tm), _rows(D, tm), _resident((DFF, D)), _resident((1, D))],
        out_specs=[_rows(D, tm), _rows(D, tm), _resident((1, 1)), _resident((8, D))],
        out_shape=[jax.ShapeDtypeStruct((S, D), F32), jax.ShapeDtypeStruct((S, D), BF16),
                   jax.ShapeDtypeStruct((1, 1), F32), jax.ShapeDtypeStruct((8, D), F32)],
        compiler_params=_params("arbitrary"),
    )(act, x2, target, w_ffn_out, norm_final)


def _ffn_bwd_down(dx3b, gu, w_ffn_out, tm=512):
    S = dx3b.shape[0]

    def body(d_ref, back_ref, w_ref, dgu_ref):
        d = d_ref[...]
        for c0, c1 in FF_CHUNKS:
            dact = _dot_nt(d, w_ref[c0:c1, :])
            dgu_ref[:, c0:c1] = (dact * back_ref[:, c0:c1].astype(F32)).astype(BF16)
            dgu_ref[:, DFF + c0:DFF + c1] = (dact * back_ref[:, DFF + c0:DFF + c1].astype(F32)).astype(BF16)

    return pl.pallas_call(
        body, name="ffn_bwd_down", grid=(S // tm,),
        in_specs=[_rows(D, tm), _rows(2 * DFF, tm), _resident((DFF, D))],
        out_specs=_rows(2 * DFF, tm),
        out_shape=jax.ShapeDtypeStruct((S, 2 * DFF), BF16),
        compiler_params=_params("parallel"),
    )(dx3b, gu, w_ffn_out)


def _ffn_bwd_up(dgu, x2, dx3, w_lo, w_hi, norm_ffn, w_o, tm=512):
    S = dgu.shape[0]
    HALF = D // 2

    def body(dgu_ref, x2_ref, dx3_ref, lo_ref, hi_ref, gf_ref, wo_ref, dx2_ref, dx2b_ref, dmixo_ref, dvec_ref):
        i = pl.program_id(0)

        @pl.when(i == 0)
        def _():
            dvec_ref[...] = jnp.zeros_like(dvec_ref)

        dgate, dup = dgu_ref[:, 0:DFF], dgu_ref[:, DFF:2 * DFF]
        dh2 = jnp.concatenate([_dot(dgate, w[0:DFF, :]) + _dot(dup, w[DFF:2 * DFF, :]) for w in (lo_ref, hi_ref)], axis=1)
        r2, xh2 = _rms(x2_ref[...])
        dvec_ref[0:1, :] += jnp.sum(dh2 * xh2, axis=0, keepdims=True)
        dx2 = dx3_ref[...] + _rms_bwd(dh2, gf_ref[...], r2, xh2)
        dx2_ref[...] = dx2
        dx2b = dx2.astype(BF16)
        dx2b_ref[...] = dx2b
        dmixo_ref[...] = _dot_nt(dx2b, wo_ref[...]).astype(BF16)

    return pl.pallas_call(
        body, name="ffn_bwd_up", grid=(S // tm,),
        in_specs=[_rows(2 * DFF, tm), _rows(D, tm), _rows(D, tm), _resident((2 * DFF, HALF)), _resident((2 * DFF, HALF)),
                  _resident((1, D)), _resident((D, D))],
        out_specs=[_rows(D, tm), _rows(D, tm), _rows(D, tm), _resident((8, D))],
        out_shape=[jax.ShapeDtypeStruct((S, D), F32), jax.ShapeDtypeStruct((S, D), BF16), jax.ShapeDtypeStruct((S, D), BF16),
                   jax.ShapeDtypeStruct((8, D), F32)],
        compiler_params=_params("arbitrary"),
    )(dgu, x2, dx3, w_lo, w_hi, norm_ffn, w_o)


VEC_ROWS = 16
MAT_WA = 4 * PG
MAT_WX = MAT_WA + NH * HD
MAT_ROWS = MAT_WX + NH * HD


def _mixer_bwd(proj, dmixo, y_pool, y_rnn, hr, kept, gates, wg, scale, w_pool_out, conv_w, conv_b, wa, ba, wx, bx, lam, w_rnn_out,
               dvec_fin, dvec_ffn, loss_part, exchange=None, exchange_operands=(), tm=256):
    S = proj.shape[0]
    nt = S // tm

    def rev(cols):
        return pl.BlockSpec((tm, cols), lambda i: (nt - 1 - i, 0))

    def halo(rows_, cols):
        per = tm // rows_
        return pl.BlockSpec((rows_, cols), lambda i: (jnp.maximum((nt - 1 - i) * per - 1, 0), 0))

    def body(proj_ref, projh_ref, dmixo_ref, yp_ref, yr_ref, hr_ref, hrh_ref, kept_ref, gates_ref, wg_ref, scale_ref, wpo_ref, cw_ref, cb_ref,
             wa_ref, ba_ref, wx_ref, bx_ref, lam_ref, wro_ref, fin_ref, ffn_ref, loss_ref,
             dproj_ref, dypb_ref, dyrb_ref, dmat_ref, dvec_ref,
             q_carry, dv_carry, a_carry, g_carry, g_scr):
        i = pl.program_id(0)
        ti = nt - 1 - i

        @pl.when(i == 0)
        def _():
            q_carry[...] = jnp.zeros_like(q_carry)
            dv_carry[...] = jnp.zeros_like(dv_carry)
            a_carry[...] = jnp.zeros_like(a_carry)
            g_carry[...] = jnp.zeros_like(g_carry)
            dmat_ref[...] = jnp.zeros_like(dmat_ref)
            dvec_ref[...] = jnp.zeros_like(dvec_ref)

        rows = lax.broadcasted_iota(jnp.int32, (tm, 1), 0)
        t_glob = ti * tm + rows
        has_prev = (ti > 0).astype(F32)
        dmixo = dmixo_ref[...].astype(F32)

        s_p = _sigmoid(proj_ref[:, DP + 2 * DR:DP + 2 * DR + D])
        s_r = _sigmoid(proj_ref[:, DP + 2 * DR + D:DIN])
        dproj_ref[:, DP + 2 * DR:DP + 2 * DR + D] = (dmixo * yp_ref[...].astype(F32) * s_p * (1.0 - s_p)).astype(BF16)
        dproj_ref[:, DP + 2 * DR + D:DIN] = (dmixo * yr_ref[...].astype(F32) * s_r * (1.0 - s_r)).astype(BF16)
        dyp = (dmixo * s_p).astype(BF16)
        dyr = (dmixo * s_r).astype(BF16)
        dypb_ref[...] = dyp
        dyrb_ref[...] = dyr

        dz = _dot_nt(dyr, wro_ref[...])
        u_gate = proj_ref[:, DP + DR:DP + 2 * DR]
        gg, dgelu = _gelu(u_gate, with_grad=True)
        hr_t = hr_ref[...]
        dproj_ref[:, DP + DR:DP + 2 * DR] = (dz * hr_t * dgelu).astype(BF16)
        dhr = dz * gg

        sp = _softplus_neg(lam_ref[...])
        v, a, mult = (kept_ref[k] for k in range(KEPT))
        r, gi = (gates_ref[k].astype(F32) for k in range(2))
        inv_mult = 1.0 / mult

        C = jnp.where(rows == tm - 1, a_carry[0:1, :], pltpu.roll(a, tm - 1, axis=0))
        g_carry[0:1, :] = _linear_scan(g_scr, C, dhr, g_carry[0:1, :], reverse=True)
        a_carry[0:1, :] = a[0:1, :]
        g = g_scr[...]

        h_prev = jnp.where(rows == 0, hrh_ref[7:8, :] * has_prev, pltpu.roll(hr_t, 1, axis=0))
        da = g * h_prev
        gm = g * mult
        dmult = g * gi * v
        di = gm * v
        dv = gm * gi
        dlog_a = da * a - dmult * (a * a * inv_mult)
        dvec_ref[4:5, :] += jnp.sum(dlog_a * r, axis=0, keepdims=True)
        dra = (dlog_a * ((-LRU_C) * sp) * r * (1.0 - r))
        drx = di * gi * (1.0 - gi)
        dvec_ref[2:3, :] += jnp.sum(dra, axis=0, keepdims=True)
        dvec_ref[3:4, :] += jnp.sum(drx, axis=0, keepdims=True)
        drab = dra.astype(BF16)
        drxb = drx.astype(BF16)
        vb = v.astype(BF16)
        dvg = []
        for h in range(NH):
            sl = slice(h * HD, (h + 1) * HD)
            dvg.append(_dot_nt(drab[:, sl], wa_ref[h]) + _dot_nt(drxb[:, sl], wx_ref[h]))
            dmat_ref[MAT_WA + h * HD:MAT_WA + (h + 1) * HD, :] += _dot_tn(vb[:, sl], drab[:, sl])
            dmat_ref[MAT_WX + h * HD:MAT_WX + (h + 1) * HD, :] += _dot_tn(vb[:, sl], drxb[:, sl])
        dv = dv + jnp.concatenate(dvg, axis=1)
        dvec_ref[1:2, :] += jnp.sum(dv, axis=0, keepdims=True)
        dvext = jnp.concatenate([dv, dv_carry[...]], axis=0)
        dv_carry[...] = dv[0:CONV_HALO, :]
        n = tm + CONV_HALO
        u_rnn = proj_ref[:, DP:DP + DR]
        du_rnn = dv * cw_ref[3:4, :]
        dvec_ref[8:9, :] += jnp.sum(dv * u_rnn, axis=0, keepdims=True)
        for k in range(3):
            dv_k = pltpu.roll(dvext, n - (3 - k), axis=0)[0:tm, :]
            du_rnn = du_rnn + dv_k * cw_ref[k:k + 1, :]
            dvec_ref[5 + k:6 + k, :] += jnp.sum(dv_k * u_rnn, axis=0, keepdims=True)
        dproj_ref[:, DP:DP + DR] = du_rnn.astype(BF16)

        dpm = _dot_nt(dyp, wpo_ref[...])
        u_pool = proj_ref[:, 0:DP]
        ext = jnp.concatenate([projh_ref[:, 0:DP] * has_prev, u_pool], axis=0)
        sums = _pool_windows(ext, +1)
        scale_v = scale_ref[...]
        qs = []
        dpooled = []
        dscale = []
        for gi_, w in enumerate(WINDOWS):
            sl = slice(gi_ * PG, (gi_ + 1) * PG)
            inv_cnt = 1.0 / jnp.minimum(t_glob + 1, w).astype(F32)
            pooled_b = (sums[gi_][POOL_HALO:, :] * inv_cnt - u_pool[:, sl]).astype(BF16)
            mixed_g = _dot(pooled_b, wg_ref[gi_])
            dscale.append(jnp.sum(dpm[:, sl] * mixed_g, axis=0, keepdims=True))
            dmixed_b = (dpm[:, sl] * scale_v[:, sl]).astype(BF16)
            dmat_ref[gi_ * PG:(gi_ + 1) * PG, :] += _dot_tn(pooled_b, dmixed_b)
            dp_g = _dot_nt(dmixed_b, wg_ref[gi_])
            dpooled.append(dp_g)
            qs.append(dp_g * inv_cnt)
        dvec_ref[0:1, 0:DP] += jnp.concatenate(dscale, axis=1)
        q = jnp.concatenate(qs, axis=1)
        qext = jnp.concatenate([q, q_carry[...]], axis=0)
        q_carry[...] = q[0:POOL_HALO, :]
        tsum = _pool_windows(qext, -1)
        for gi_ in range(4):
            dproj_ref[:, gi_ * PG:(gi_ + 1) * PG] = (tsum[gi_][0:tm, :] - dpooled[gi_]).astype(BF16)

        @pl.when(i == nt - 1)
        def _():
            dvec_ref[4:5, :] = dvec_ref[4:5, :] * (LRU_C * _sigmoid(-lam_ref[...]))
            dvec_ref[VEC_NORM_FINAL:VEC_NORM_FINAL + 1, :] = fin_ref[0:1, :]
            dvec_ref[VEC_NORM_FFN:VEC_NORM_FFN + 1, :] = ffn_ref[0:1, :]
            lane = lax.broadcasted_iota(jnp.int32, (1, DR), 1)
            dvec_ref[VEC_LOSS:VEC_LOSS + 1, :] = jnp.where(lane == 0, jnp.broadcast_to(loss_ref[...], (1, DR)), 0.0)

    return _call(
        body, "mixer_bwd", (nt,),
        in_specs=[rev(DIN), halo(POOL_HALO, DIN), rev(D), rev(D), rev(D), rev(DR), halo(8, DR),
                  pl.BlockSpec((KEPT, tm, DR), lambda i: (0, nt - 1 - i, 0)),
                  pl.BlockSpec((2, tm, DR), lambda i: (0, nt - 1 - i, 0)), _resident((4, PG, PG)), _resident((1, DP)), _resident((DP, D)), _resident(conv_w.shape), _resident((1, DR)),
                  _resident((NH, HD, HD)), _resident((1, DR)), _resident((NH, HD, HD)), _resident((1, DR)),
                  _resident((1, DR)), _resident((DR, D)), _resident((8, D)), _resident((8, D)), _resident((1, 1))],
        out_specs=[rev(DIN), rev(D), rev(D), _resident((MAT_ROWS, HD)), _resident((VEC_ROWS, DR))],
        out_shape=[jax.ShapeDtypeStruct((S, DIN), BF16), jax.ShapeDtypeStruct((S, D), BF16),
                   jax.ShapeDtypeStruct((S, D), BF16), jax.ShapeDtypeStruct((MAT_ROWS, HD), F32),
                   jax.ShapeDtypeStruct((VEC_ROWS, DR), F32)],
        scratch_shapes=[pltpu.VMEM((POOL_HALO, DP), F32), pltpu.VMEM((CONV_HALO, DR), F32), pltpu.VMEM((8, DR), F32),
                        pltpu.VMEM((8, DR), F32), pltpu.VMEM((tm, DR), F32)],
        operands=(proj, proj, dmixo, y_pool, y_rnn, hr, hr, kept, gates, wg, scale, w_pool_out, conv_w, conv_b, wa, ba, wx, bx, lam,
                  w_rnn_out, dvec_fin, dvec_ffn, loss_part),
        exchange=exchange, exchange_operands=exchange_operands)


def _in_bwd(dproj, x, dx2, norm_mix, w_in, exchange=None, exchange_operands=(), tm=512):
    S = x.shape[0]

    def body(dp_ref, x_ref, dx2_ref, g_ref, w_ref, dx_ref, dg_ref):
        i = pl.program_id(0)

        @pl.when(i == 0)
        def _():
            dg_ref[...] = jnp.zeros_like(dg_ref)

        dh = _dot(dp_ref[:, 0:1536], w_ref[0:1536, :])
        dh = dh + _dot(dp_ref[:, 1536:3072], w_ref[1536:3072, :])
        dh = dh + _dot(dp_ref[:, 3072:DIN], w_ref[3072:DIN, :])
        xv = x_ref[...]
        r = lax.rsqrt(jnp.mean(xv * xv, axis=-1, keepdims=True) + EPS)
        xh = xv * r
        dg_ref[0:1, :] += jnp.sum(dh * xh, axis=0, keepdims=True)
        dxh = dh * g_ref[...]
        dx_ref[...] = dx2_ref[...] + r * (dxh - xh * jnp.mean(dxh * xh, axis=-1, keepdims=True))

    return _call(
        body, "in_bwd", (S // tm,),
        in_specs=[_rows(DIN, tm), _rows(D, tm), _rows(D, tm), _resident((1, D)), _resident((DIN, D))],
        out_specs=[_rows(D, tm), _resident((8, D))],
        out_shape=[jax.ShapeDtypeStruct((S, D), F32), jax.ShapeDtypeStruct((8, D), F32)],
        operands=(dproj, x, dx2, norm_mix, w_in), exchange=exchange, exchange_operands=exchange_operands)


def _wgrad(a, b, name, tk, tn, exchange=None, exchange_operands=()):
    S, K = a.shape
    N = b.shape[1]

    def body(a_ref, b_ref, o_ref):
        o_ref[...] = _dot_tn(a_ref[...], b_ref[...]).astype(BF16)

    (out,), exchanged = _call(
        body, name, (K // tk, N // tn),
        in_specs=[pl.BlockSpec((S, tk), lambda k, n: (0, k)), pl.BlockSpec((S, tn), lambda k, n: (0, n))],
        out_specs=[pl.BlockSpec((tk, tn), lambda k, n: (k, n))],
        out_shape=[jax.ShapeDtypeStruct((K, N), BF16)],
        operands=(a, b), exchange=exchange, exchange_operands=exchange_operands)
    return (out, exchanged) if exchange is not None else out


VEC_SCALE, VEC_CONV_B, VEC_BA, VEC_BX, VEC_LAM, VEC_CONV_W, VEC_NORM_FINAL, VEC_NORM_FFN = 0, 1, 2, 3, 4, 5, 9, 10
VEC_LOSS = 11


class _Big:
    def __init__(self, name, rows, cols, axis, n, dtype=BF16, transposed=False, src_cols=None):
        self.name, self.rows, self.cols, self.axis, self.n, self.dtype = name, rows, cols, axis, n, dtype
        self.transposed = transposed
        self.src_cols = src_cols
        self.block_shape = (rows, n) if axis == 1 else (n, cols)

    def block(self, ref, p):
        if self.axis == 1:
            return ref.at[:, pl.ds(pl.multiple_of(p * self.n, 128), self.n)]
        return ref.at[pl.ds(pl.multiple_of(p * self.n, 16 if self.dtype == BF16 else 8), self.n), :]


BIG = (_Big("w_in", DIN, D, 0, DIN // 8, transposed=True), _Big("w_pool_out", DP, D, 1, D // 8),
       _Big("w_rnn_out", DR, D, 0, DR // 8), _Big("w_o", D, D, 0, D // 8),
       _Big("w_ffn_in", 2 * DFF, D, 0, 2 * DFF // 8, transposed=True), _Big("w_ffn_out", DFF, D, 0, DFF // 8))
CONV_W = _Big("conv_w", 8, DR, 1, DR // 8, F32)
W_FFN_IN_HALVES = (_Big("w_ffn_in_lo", 2 * DFF, D // 2, 0, 2 * DFF // 8, src_cols=(0, D // 2)),
                   _Big("w_ffn_in_hi", 2 * DFF, D // 2, 0, 2 * DFF // 8, src_cols=(D // 2, D)))
GATHERED = BIG + (CONV_W,) + W_FFN_IN_HALVES

HBM_SPEC = pl.BlockSpec(memory_space=pl.ANY)
VMEM_SPEC = pl.BlockSpec(memory_space=pltpu.VMEM)


def _place():
    x, y, c = (lax.axis_index(a) for a in MESH_AXES)
    other_chips = [(1 - x, y), (x, 1 - y), (1 - x, 1 - y)]
    return x, y, c, other_chips


def _remote(src, dst, send_sems, recv_sems, idx, to):
    return pltpu.make_async_remote_copy(src_ref=src, dst_ref=dst, send_sem=send_sems.at[idx], recv_sem=recv_sems.at[idx],
                                        device_id=to, device_id_type=MESH)


def _device_index(chip, core):
    return 4 * chip[0] + 2 * chip[1] + core


class _Gather:
    def __init__(self, tensors):
        self.tensors = tuple(tensors)
        n = len(self.tensors)
        self.in_specs = [HBM_SPEC] * n
        self.out_specs = [HBM_SPEC] * n
        self.out_shape = [jax.ShapeDtypeStruct((T.rows, T.cols), T.dtype) for T in self.tensors]
        self.scratch_shapes = [pltpu.VMEM(T.block_shape, T.dtype) for T in self.tensors] + [
            pltpu.VMEM(T.block_shape, F32) for T in self.tensors] + [
            pltpu.SemaphoreType.DMA((n, 7)), pltpu.SemaphoreType.DMA((n, 7)), pltpu.SemaphoreType.DMA((n, 2))]

    collective_id = 1

    def peers(self):
        x, y, c, _ = _place()
        return [(x, y, 1 - c), (1 - x, y, c), (x, 1 - y, c)]

    def middles(self, steps):
        return [(steps // 2, self.relay), (steps - 1, self.middle)]

    def _copies(self, ins, outs, scratch):
        n = len(self.tensors)
        mine, raw, (send_sems, recv_sems, loc_sems) = scratch[:n], scratch[n:2 * n], scratch[2 * n:]
        x, y, c, chips = _place()
        sibling = (x, y, 1 - c)
        me = _device_index((x, y), c)
        relay_from = (jnp.where(c == 0, 1 - x, x), jnp.where(c == 0, y, 1 - y))
        relay_to = (jnp.where(c == 0, x, 1 - x), jnp.where(c == 0, 1 - y, y))
        loads, stores, first, relays, passed, arrivals, late = [], [], [], [], [], [], []
        for t, T in enumerate(self.tensors):
            place = T.block(outs[t], me)
            src = ins[t] if T.src_cols is None else ins[t].at[:, T.src_cols[0]:T.src_cols[1]]
            loads.append(pltpu.make_async_copy(src, raw[t], loc_sems.at[t, 0]))
            stores.append(pltpu.make_async_copy(mine[t], place, loc_sems.at[t, 1]))
            first.append(_remote(mine[t], place, send_sems, recv_sems, (t, 0), sibling))
            theirs = T.block(outs[t], _device_index((x, y), 1 - c))
            late.append(_remote(theirs, theirs, send_sems, recv_sems, (t, 0), sibling))
            relayed = T.block(outs[t], _device_index(relay_from, c))
            relays.append(_remote(relayed, relayed, send_sems, recv_sems, (t, 3), (*relay_to, c)))
            for k, chip in enumerate(chips):
                if k < 2:
                    first.append(_remote(mine[t], place, send_sems, recv_sems, (t, 1 + k), (*chip, c)))
                land = T.block(outs[t], _device_index(chip, c))
                arrivals.append(_remote(land, land, send_sems, recv_sems, (t, 1 + k), sibling))
                passed.append(_remote(land, land, send_sems, recv_sems, (t, 4 + k), sibling))
                theirs = T.block(outs[t], _device_index(chip, 1 - c))
                late.append(_remote(theirs, theirs, send_sems, recv_sems, (t, 4 + k), sibling))
        return loads, stores, first, relays, passed, arrivals, late

    def start(self, ins, outs, scratch):
        loads, stores, first, _, _, _, _ = self._copies(ins, outs, scratch)
        n = len(self.tensors)
        for cp in loads:
            cp.start()
        for t, cp in enumerate(loads):
            cp.wait()
            scratch[t][...] = scratch[n + t][...].astype(self.tensors[t].dtype)
        for cp in stores + first:
            cp.start()

    def relay(self, ins, outs, scratch, skip=0):
        _, _, _, relays, passed, arrivals, _ = self._copies(ins, outs, scratch)
        for t in range(skip, len(self.tensors)):
            arrivals[3 * t].wait_recv()
            arrivals[3 * t + 1].wait_recv()
            for cp in (relays[t], passed[3 * t], passed[3 * t + 1]):
                cp.start()

    def middle(self, ins, outs, scratch, skip=0):
        _, _, _, _, passed, arrivals, _ = self._copies(ins, outs, scratch)
        for t in range(skip, len(self.tensors)):
            arrivals[3 * t + 2].wait_recv()
            passed[3 * t + 2].start()

    def finish(self, ins, outs, scratch, skip=0):
        _, stores, first, relays, passed, _, late = self._copies(ins, outs, scratch)
        for cp in late[4 * skip:]:
            cp.wait_recv()
        for cp in first + relays + passed:
            cp.wait_send()
        for cp in stores[skip:]:
            cp.wait()


def _in_proj_gather(x, norm_mix, blocks, tensors, order, tm=1024):
    S = x.shape[0]
    nt = S // tm
    n = len(tensors)
    gather = _Gather(tensors)
    CB = 2 * tensors[0].n

    def body(order_ref, x_ref, g_ref, *refs):
        ins, (proj_ref, h_ref), outs = refs[:n], refs[n:n + 2], refs[n + 2:2 * n + 2]
        (h_all, w_chip, w_sem), scratch = refs[2 * n + 2:2 * n + 5], refs[2 * n + 5:]
        q, i = pl.program_id(0), pl.program_id(1)
        _, stores, _, relays, passed, arrivals, late = gather._copies(ins, outs, scratch)

        def fetch(turn):
            rows = outs[0].at[pl.ds(pl.multiple_of(order_ref[turn] * CB, 16), CB), :]
            cp = pltpu.make_async_copy(rows, w_chip, w_sem)
            cp.start()
            cp.wait()

        @pl.when((q == 0) & (i == 0))
        def _():
            _enter(gather)
            gather.start(ins, outs, scratch)
            late[0].wait_recv()
            stores[0].wait()
            fetch(0)

        @pl.when((q == 1) & (i == 0))
        def _():
            arrivals[0].wait_recv()
            arrivals[1].wait_recv()
            for cp in (relays[0], passed[0], passed[1]):
                cp.start()
            late[1].wait_recv()
            fetch(1)

        @pl.when((q == 2) & (i == 0))
        def _():
            late[2].wait_recv()
            fetch(2)
            gather.relay(ins, outs, scratch, skip=1)

        @pl.when((q == 3) & (i == 0))
        def _():
            arrivals[2].wait_recv()
            passed[2].start()
            late[3].wait_recv()
            fetch(3)

        rows = pl.ds(pl.multiple_of(i * tm, tm), tm)

        @pl.when(q == 0)
        def _():
            xv = x_ref[...]
            r = lax.rsqrt(jnp.mean(xv * xv, axis=-1, keepdims=True) + EPS)
            h = (xv * r * g_ref[...]).astype(BF16)
            h_all[rows, :] = h
            h_ref[...] = h

        proj_ref[...] = _dot_nt(h_all[rows, :], w_chip[...])

        @pl.when((q == 3) & (i == nt - 1))
        def _():
            gather.middle(ins, outs, scratch, skip=1)
            gather.finish(ins, outs, scratch, skip=1)

    row_tile = lambda q, i, order: (jnp.where(q == 0, i, nt - 1), 0)
    whole = lambda shape: pl.BlockSpec(shape, lambda q, i, order: (0,) * len(shape), pipeline_mode=pl.Buffered(1))
    outs = pl.pallas_call(
        body, name="in_proj_gather",
        grid_spec=pltpu.PrefetchScalarGridSpec(
            num_scalar_prefetch=1, grid=(4, nt),
            in_specs=[pl.BlockSpec((tm, D), row_tile), whole((1, D))] + gather.in_specs,
            out_specs=[pl.BlockSpec((tm, CB), lambda q, i, order: (i, order[q])), pl.BlockSpec((tm, D), row_tile)]
            + gather.out_specs,
            scratch_shapes=[pltpu.VMEM((S, D), BF16), pltpu.VMEM((CB, D), BF16), pltpu.SemaphoreType.DMA]
            + gather.scratch_shapes),
        out_shape=[jax.ShapeDtypeStruct((S, DIN), F32), jax.ShapeDtypeStruct((S, D), BF16)] + gather.out_shape,
        compiler_params=pltpu.CompilerParams(dimension_semantics=("arbitrary", "arbitrary"), vmem_limit_bytes=VMEM_LIMIT,
                                             collective_id=gather.collective_id),
    )(order, x, norm_mix, *blocks)
    return outs[:2], outs[2:]


PAIR_ROWS = 32


class _PairReduce:
    collective_id = 3

    def __init__(self, tensors):
        self.tensors = tuple(tensors)
        nt = len(self.tensors)
        blocks = [T.block_shape for T in self.tensors]
        self.in_specs = [HBM_SPEC] * nt
        self.out_specs = [HBM_SPEC] * (2 * nt)
        self.out_shape = ([jax.ShapeDtypeStruct(b, BF16) for b in blocks]
                          + [jax.ShapeDtypeStruct((3,) + b, BF16) for b in blocks])
        self.scratch_shapes = ([pltpu.VMEM((4,) + b, BF16) for b in blocks] + [pltpu.VMEM((3,) + b, BF16) for b in blocks]
                               + [pltpu.SemaphoreType.DMA((nt, 4)), pltpu.SemaphoreType.DMA((nt, 4)),
                                  pltpu.SemaphoreType.DMA((nt, 5))])

    def peers(self):
        x, y, c, _ = _place()
        return [(x, y, 1 - c)]

    def middles(self, steps):
        return []

    def _copies(self, ins, outs, scratch):
        nt = len(self.tensors)
        own_out, sums_out, landed, mine = outs[:nt], outs[nt:], scratch[:nt], scratch[nt:2 * nt]
        send_sems, recv_sems, loc_sems = scratch[2 * nt:]
        x, y, c, chips = _place()
        chip_of = [2 * chip[0] + chip[1] for chip in chips]
        swaps, loads, stores = [], [], []
        for t, T in enumerate(self.tensors):
            for j in range(4):
                swaps.append(_remote(T.block(ins[t], 2 * j + 1 - c), landed[t].at[j], send_sems, recv_sems, (t, j),
                                     (x, y, 1 - c)))
            for k in range(3):
                loads.append(pltpu.make_async_copy(T.block(ins[t], 2 * chip_of[k] + c), mine[t].at[k], loc_sems.at[t, k]))
            stores.append(pltpu.make_async_copy(mine[t], sums_out[t], loc_sems.at[t, 3]))
            stores.append(pltpu.make_async_copy(landed[t].at[2 * x + y], own_out[t], loc_sems.at[t, 4]))
        return swaps, loads, stores, landed, mine, chip_of

    def start(self, ins, outs, scratch):
        swaps, loads, _, _, _, _ = self._copies(ins, outs, scratch)
        for cp in swaps + loads:
            cp.start()

    def finish(self, ins, outs, scratch):
        swaps, loads, stores, landed, mine, chip_of = self._copies(ins, outs, scratch)
        for cp in loads:
            cp.wait()
        for cp in swaps:
            cp.wait_recv()
        for t, T in enumerate(self.tensors):
            for k in range(3):
                acc, got = mine[t].at[k], landed[t].at[chip_of[k]]

                def add(i, carry, acc=acc, got=got):
                    rows = pl.ds(pl.multiple_of(i * PAIR_ROWS, PAIR_ROWS), PAIR_ROWS)
                    acc[rows, :] = (acc[rows, :].astype(F32) + got[rows, :].astype(F32)).astype(BF16)
                    return carry

                lax.fori_loop(0, T.block_shape[0] // PAIR_ROWS, add, 0)
        for cp in stores:
            cp.start()
        for cp in swaps:
            cp.wait_send()
        for cp in stores:
            cp.wait()


def _pair_reduce(grads, tensors, name):
    reduce = _PairReduce(tensors)
    nt = len(reduce.tensors)

    def body(*refs):
        ins, outs, scratch = refs[:nt], refs[nt:3 * nt], refs[3 * nt:]
        _enter(reduce)
        reduce.start(ins, outs, scratch)
        reduce.finish(ins, outs, scratch)

    return pl.pallas_call(
        body, name=name, in_specs=reduce.in_specs, out_specs=reduce.out_specs, out_shape=reduce.out_shape,
        scratch_shapes=reduce.scratch_shapes,
        compiler_params=pltpu.CompilerParams(vmem_limit_bytes=VMEM_LIMIT, collective_id=reduce.collective_id),
    )(*grads)


class _Scatter:
    def __init__(self, tensors):
        self.tensors = tuple(tensors)
        n = len(self.tensors)
        self.in_specs = [HBM_SPEC] * n
        self.out_specs = [HBM_SPEC] * n
        self.out_shape = [jax.ShapeDtypeStruct((2,) + T.block_shape, BF16) for T in self.tensors]
        self.scratch_shapes = [pltpu.VMEM(T.block_shape, BF16) for T in self.tensors] * 2 + [
            pltpu.SemaphoreType.DMA((n, 3)), pltpu.SemaphoreType.DMA((n, 3)), pltpu.SemaphoreType.DMA((n,))]

    collective_id = 2

    def peers(self):
        x, y, c, _ = _place()
        return [(1 - x, y, c), (x, 1 - y, c)]

    def middles(self, steps):
        return [(steps // 2, self.middle)]

    def _copies(self, ins, outs, scratch):
        n = len(self.tensors)
        landed, mine, (send_sems, recv_sems, loc_sems) = scratch[:n], scratch[n:2 * n], scratch[2 * n:]
        x, y, c, _ = _place()
        direct = (jnp.where(c == 0, 1 - x, x), jnp.where(c == 0, y, 1 - y), c)
        other = (jnp.where(c == 0, x, 1 - x), jnp.where(c == 0, 1 - y, y), c)
        k_direct = jnp.where(c == 0, 0, 1)
        to_direct, legs, loads, combined, arrivals = [], [], [], [], []
        for t in range(n):
            to_direct.append(_remote(ins[t].at[k_direct], outs[t].at[0], send_sems, recv_sems, (t, 0), direct))
            legs.append(_remote(ins[t].at[2], landed[t], send_sems, recv_sems, (t, 2), direct))
            loads.append(pltpu.make_async_copy(ins[t].at[1 - k_direct], mine[t], loc_sems.at[t]))
            combined.append(_remote(mine[t], outs[t].at[1], send_sems, recv_sems, (t, 1), other))
            arrivals.append(_remote(landed[t], landed[t], send_sems, recv_sems, (t, 2), direct))
        return to_direct, legs, loads, combined, arrivals, landed, mine

    def start(self, ins, outs, scratch):
        to_direct, legs, loads, _, _, _, _ = self._copies(ins, outs, scratch)
        for cp in to_direct + legs + loads:
            cp.start()

    def middle(self, ins, outs, scratch):
        _, _, loads, combined, arrivals, landed, mine = self._copies(ins, outs, scratch)
        for t, T in enumerate(self.tensors):
            loads[t].wait()
            arrivals[t].wait_recv()
            acc, got = mine[t], landed[t]

            def add(i, carry, acc=acc, got=got):
                rows = pl.ds(pl.multiple_of(i * PAIR_ROWS, PAIR_ROWS), PAIR_ROWS)
                acc[rows, :] = (acc[rows, :].astype(F32) + got[rows, :].astype(F32)).astype(BF16)
                return carry

            lax.fori_loop(0, T.block_shape[0] // PAIR_ROWS, add, 0)
            combined[t].start()

    def finish(self, ins, outs, scratch):
        to_direct, legs, _, combined, _, _, _ = self._copies(ins, outs, scratch)
        for cp in to_direct + combined:
            cp.wait()
        for cp in legs:
            cp.wait_send()


class _ReduceScatter:
    collective_id = 1

    def __init__(self, tensors):
        self.first, self.second = _PairReduce(tensors), _Scatter(tensors)
        self.in_specs = self.first.in_specs
        self.out_specs = self.first.out_specs + self.second.out_specs
        self.out_shape = self.first.out_shape + self.second.out_shape
        self.scratch_shapes = self.first.scratch_shapes + self.second.scratch_shapes

    def peers(self):
        x, y, c, _ = _place()
        return [(x, y, 1 - c), (1 - x, y, c), (x, 1 - y, c)]

    def _parts(self, ins, outs, scratch):
        n_out, n_scr = len(self.first.out_specs), len(self.first.scratch_shapes)
        pair_sums = list(outs[n_out // 2:n_out])
        return (ins, outs[:n_out], scratch[:n_scr]), (pair_sums, outs[n_out:], scratch[n_scr:])

    def start(self, ins, outs, scratch):
        self.first.start(*self._parts(ins, outs, scratch)[0])

    def middles(self, steps):
        def hand_over(ins, outs, scratch):
            first, second = self._parts(ins, outs, scratch)
            self.first.finish(*first)
            self.second.start(*second)

        def relay(ins, outs, scratch):
            self.second.middle(*self._parts(ins, outs, scratch)[1])

        return [(steps // 4, hand_over), ((5 * steps) // 8, relay)]

    def finish(self, ins, outs, scratch):
        self.second.finish(*self._parts(ins, outs, scratch)[1])


def _adamw(w, g, m, v):
    m = ADAM_B1 * m + (1.0 - ADAM_B1) * g
    v = ADAM_B2 * v + (1.0 - ADAM_B2) * (g * g)
    m_hat = m / (1.0 - ADAM_B1 ** ADAM_STEP)
    v_hat = v / (1.0 - ADAM_B2 ** ADAM_STEP)
    delta = -ADAM_LR * (m_hat / (jnp.sqrt(v_hat) + ADAM_EPS) + ADAM_WD * w)
    return delta, m, v


def _final_sum(T, g, lz1, lz2, where, w, m, v):
    rows, cols = T.block_shape
    sub = 4 if T.axis == 0 and rows % 64 == 0 and rows > 256 else 1
    blk = (rows // sub, cols)

    def body(where_ref, g_ref, l1_ref, l2_ref, w_ref, m_ref, v_ref, g_out, d_out, m_out, v_out):
        tot = g_ref[...].astype(F32) + l1_ref[...].astype(F32)
        for k in range(2):
            tot = tot + l2_ref[k].astype(F32)
        g_out[...] = tot
        d_out[...], m_out[...], v_out[...] = _adamw(w_ref[...], tot, m_ref[...], v_ref[...])

    def in_whole(r, wh):
        p = wh[0]
        return (0, p) if T.axis == 1 else (p * sub + r, 0)

    own = pl.BlockSpec(blk, lambda r, wh: (r, 0))
    return pl.pallas_call(
        body, name="grad_final_" + T.name,
        grid_spec=pltpu.PrefetchScalarGridSpec(
            num_scalar_prefetch=1, grid=(sub,),
            in_specs=[pl.BlockSpec(blk, in_whole),
                      own,
                      pl.BlockSpec((2,) + blk, lambda r, wh: (0, r, 0)), own, own, own],
            out_specs=[own] * 4),
        out_shape=[jax.ShapeDtypeStruct(T.block_shape, F32)] * 4,
        compiler_params=_params("arbitrary"),
    )(where, g, lz1, lz2, w, m, v)


VEC_PIECE = DR // 8


class _AllReduce:
    def __init__(self, items):
        self.items = tuple(items)
        n = len(self.items)
        self.in_specs = [HBM_SPEC] * n
        self.out_specs = [HBM_SPEC] * n
        self.out_shape = [jax.ShapeDtypeStruct(shape, F32) for shape, _ in self.items]
        pieces = [(shape[0] // 8, shape[1]) if axis == 0 else (shape[0], shape[1] // 8) for shape, axis in self.items]
        self.scratch_shapes = ([pltpu.VMEM((8,) + p, F32) for p in pieces] + [pltpu.VMEM(p, F32) for p in pieces] + [
            pltpu.SemaphoreType.DMA((2 * n, 8)), pltpu.SemaphoreType.DMA((2 * n, 8)), pltpu.SemaphoreType.DMA((2 * n,))])

    collective_id = None

    def peers(self):
        return []

    def middles(self, steps):
        return [(steps // 2, self.middle)]

    def _copies(self, ins, outs, scratch):
        n = len(self.items)
        landed, sums, (send_sems, recv_sems, loc_sems) = scratch[:n], scratch[n:2 * n], scratch[2 * n:]
        x, y, c, _ = _place()
        me = _device_index((x, y), c)

        def peer(r):
            return (1 - x if r & 4 else x, 1 - y if r & 2 else y, 1 - c if r & 1 else c)

        def piece(i, ref, p):
            shape, axis = self.items[i]
            if axis == 0:
                rows = shape[0] // 8
                return ref.at[pl.ds(pl.multiple_of(p * rows, 8), rows), :]
            cols = shape[1] // 8
            return ref.at[:, pl.ds(pl.multiple_of(p * cols, 128), cols)]

        own, scatter, arrivals, keep, spread, late = [], [], [], [], [], []
        for i in range(n):
            own.append(pltpu.make_async_copy(piece(i, ins[i], me), landed[i].at[0], loc_sems.at[2 * i]))
            keep.append(pltpu.make_async_copy(sums[i], piece(i, outs[i], me), loc_sems.at[2 * i + 1]))
            for r in range(1, 8):
                to = peer(r)
                p = _device_index(to[:2], to[2])
                scatter.append(_remote(piece(i, ins[i], p), landed[i].at[r], send_sems, recv_sems, (2 * i, r), to))
                spread.append(_remote(sums[i], piece(i, outs[i], me), send_sems, recv_sems, (2 * i + 1, r), to))
                late.append(_remote(sums[i], piece(i, outs[i], p), send_sems, recv_sems, (2 * i + 1, r), to))
        return own, scatter, keep, spread, late, landed, sums

    def start(self, ins, outs, scratch):
        own, scatter, _, _, _, _, _ = self._copies(ins, outs, scratch)
        for cp in own + scatter:
            cp.start()

    def middle(self, ins, outs, scratch):
        own, scatter, keep, spread, _, landed, sums = self._copies(ins, outs, scratch)
        for cp in own:
            cp.wait()
        for cp in scatter:
            cp.wait_recv()
        for i in range(len(self.items)):
            total = landed[i][0]
            for r in range(1, 8):
                total = total + landed[i][r]
            sums[i][...] = total
        for cp in keep + spread:
            cp.start()

    def finish(self, ins, outs, scratch):
        _, scatter, keep, spread, late, _, _ = self._copies(ins, outs, scratch)
        for cp in late:
            cp.wait_recv()
        for cp in scatter + spread:
            cp.wait_send()
        for cp in keep:
            cp.wait()


class _Both:
    def __init__(self, a, b):
        self.a, self.b = a, b
        self.in_specs, self.out_specs = a.in_specs + b.in_specs, a.out_specs + b.out_specs
        self.out_shape, self.scratch_shapes = a.out_shape + b.out_shape, a.scratch_shapes + b.scratch_shapes

    collective_id = None

    def peers(self):
        return []

    def _each(self, ins, outs, scratch):
        a = self.a
        i, o, s = len(a.in_specs), len(a.out_specs), len(a.scratch_shapes)
        return (a, ins[:i], outs[:o], scratch[:s]), (self.b, ins[i:], outs[o:], scratch[s:])

    def middles(self, steps):
        def of(which, middle):
            return lambda ins, outs, scratch: middle(*self._each(ins, outs, scratch)[which][1:])
        return [(at, of(which, middle)) for which, e in enumerate((self.a, self.b)) for at, middle in e.middles(steps)]

    def start(self, ins, outs, scratch):
        for e, i, o, s in self._each(ins, outs, scratch):
            e.start(i, o, s)

    def finish(self, ins, outs, scratch):
        for e, i, o, s in self._each(ins, outs, scratch):
            e.finish(i, o, s)


def _all_reduce(arrays, items, name):
    reduce = _AllReduce(items)
    n = len(items)

    def body(*refs):
        ins, outs, scratch = refs[:n], refs[n:2 * n], refs[2 * n:]
        reduce.start(ins, outs, scratch)
        reduce.middle(ins, outs, scratch)
        reduce.finish(ins, outs, scratch)

    return pl.pallas_call(
        body, name=name, in_specs=reduce.in_specs, out_specs=reduce.out_specs, out_shape=reduce.out_shape,
        scratch_shapes=reduce.scratch_shapes,
    )(*arrays)


def _adam_small(where, blocks, cuts, wmv, loss_at):
    n, n_blocks = len(cuts), len(blocks)
    width = lambda cols: cols.stop - cols.start if isinstance(cols, slice) else cols
    shapes = [(rows.stop - rows.start, width(cols)) for _, rows, cols in cuts]

    def body(where_ref, *refs):
        b_refs, refs = refs[:n_blocks], refs[n_blocks:]
        ins, outs = refs[:3 * n], refs[3 * n:]
        for i, (block, rows, cols) in enumerate(cuts):
            if not isinstance(cols, slice):
                cols = pl.ds(pl.multiple_of(where_ref[0] * cols, cols), cols)
            g = b_refs[block][rows, cols]
            outs[4 * i][...] = g
            outs[4 * i + 1][...], outs[4 * i + 2][...], outs[4 * i + 3][...] = _adamw(
                ins[3 * i][...], g, ins[3 * i + 1][...], ins[3 * i + 2][...])
        outs[4 * n][...] = b_refs[loss_at[0]][loss_at[1]:loss_at[1] + 1, 0:1]

    flat = [a.reshape(shape) for t, shape in zip(wmv, shapes) for a in t]
    return pl.pallas_call(
        body, name="adam_small",
        in_specs=[pl.BlockSpec(memory_space=pltpu.SMEM)] + [VMEM_SPEC] * (n_blocks + 3 * n), out_specs=[VMEM_SPEC] * (4 * n + 1),
        out_shape=[jax.ShapeDtypeStruct(shape, F32) for shape in shapes for _ in range(4)] + [jax.ShapeDtypeStruct((1, 1), F32)],
    )(where, *blocks, *flat)


WEIGHT_NAMES = ("norm_mix", "w_in", "w_pool_grp", "pool_scale", "w_pool_out", "conv_w", "conv_b", "w_rg_a", "b_rg_a", "w_rg_x",
                "b_rg_x", "lru_lambda", "w_rnn_out", "w_o", "norm_ffn", "w_ffn_in", "w_ffn_out", "norm_final")


def kernel(x, norm_mix, w_in, w_pool_grp, pool_scale, w_pool_out, conv_w, conv_b, w_rg_a, b_rg_a, w_rg_x, b_rg_x, lru_lambda, w_rnn_out, w_o, norm_ffn, w_ffn_in, w_ffn_out, norm_final, loss_target, m_norm_mix, m_w_in, m_w_pool_grp, m_pool_scale, m_w_pool_out, m_conv_w, m_conv_b, m_w_rg_a, m_b_rg_a, m_w_rg_x, m_b_rg_x, m_lru_lambda, m_w_rnn_out, m_w_o, m_norm_ffn, m_w_ffn_in, m_w_ffn_out, m_norm_final, v_norm_mix, v_w_in, v_w_pool_grp, v_pool_scale, v_w_pool_out, v_conv_w, v_conv_b, v_w_rg_a, v_b_rg_a, v_w_rg_x, v_b_rg_x, v_lru_lambda, v_w_rnn_out, v_w_o, v_norm_ffn, v_w_ffn_in, v_w_ffn_out, v_norm_final):
    w = dict(norm_mix=norm_mix, w_in=w_in, w_pool_grp=w_pool_grp, pool_scale=pool_scale, w_pool_out=w_pool_out, conv_w=conv_w,
             conv_b=conv_b, w_rg_a=w_rg_a, b_rg_a=b_rg_a, w_rg_x=w_rg_x, b_rg_x=b_rg_x, lru_lambda=lru_lambda,
             w_rnn_out=w_rnn_out, w_o=w_o, norm_ffn=norm_ffn, w_ffn_in=w_ffn_in, w_ffn_out=w_ffn_out, norm_final=norm_final)
    m = dict(norm_mix=m_norm_mix, w_in=m_w_in, w_pool_grp=m_w_pool_grp, pool_scale=m_pool_scale, w_pool_out=m_w_pool_out,
             conv_w=m_conv_w, conv_b=m_conv_b, w_rg_a=m_w_rg_a, b_rg_a=m_b_rg_a, w_rg_x=m_w_rg_x, b_rg_x=m_b_rg_x,
             lru_lambda=m_lru_lambda, w_rnn_out=m_w_rnn_out, w_o=m_w_o, norm_ffn=m_norm_ffn, w_ffn_in=m_w_ffn_in,
             w_ffn_out=m_w_ffn_out, norm_final=m_norm_final)
    v = dict(norm_mix=v_norm_mix, w_in=v_w_in, w_pool_grp=v_w_pool_grp, pool_scale=v_pool_scale, w_pool_out=v_w_pool_out,
             conv_w=v_conv_w, conv_b=v_conv_b, w_rg_a=v_w_rg_a, b_rg_a=v_b_rg_a, w_rg_x=v_w_rg_x, b_rg_x=v_b_rg_x,
             lru_lambda=v_lru_lambda, w_rnn_out=v_w_rnn_out, w_o=v_w_o, norm_ffn=v_norm_ffn, w_ffn_in=v_w_ffn_in,
             w_ffn_out=v_w_ffn_out, norm_final=v_norm_final)
    xi, yi, ci = (lax.axis_index(a) for a in MESH_AXES)
    chip = 2 * xi + yi

    def held(T, a):
        return jnp.swapaxes(a, 0, 1) if T.transposed else a

    where = jnp.stack([2 * chip + ci]).astype(jnp.int32)
    by_name = {T.name: T for T in GATHERED}
    block = {T.name: held(T, w[T.name][0]) for T in BIG}
    block["conv_w"] = jnp.pad(conv_w[0], ((0, CONV_W.rows - 4), (0, 0)))
    block["w_ffn_in_lo"] = block["w_ffn_in_hi"] = block["w_ffn_in"]

    def gather_of(*names):
        return dict(exchange=_Gather([by_name[n] for n in names]), exchange_operands=[block[n] for n in names])

    def pair_sums(names, partials, tag):
        out = _pair_reduce(partials, [by_name[n] for n in names], "grad_pair_reduce_" + tag)
        return list(out[:len(names)]), list(out[len(names):])

    xs, target = x[0], loss_target[0]
    wg_b, wa_b, wx_b = (a[0].astype(BF16) for a in (w_pool_grp, w_rg_a, w_rg_x))
    ba2, bx2 = b_rg_a.reshape(1, DR), b_rg_x.reshape(1, DR)
    first = ("w_in", "w_pool_out", "w_rnn_out", "conv_w", "w_o")
    order = jnp.stack([chip, 2 * (1 - xi) + yi, 2 * xi + (1 - yi), 2 * (1 - xi) + (1 - yi)]).astype(jnp.int32)
    (proj, h1), (w_in_g, w_pool_out_g, w_rnn_out_g, conv_g, w_o_g) = _in_proj_gather(
        xs, norm_mix, [block[n] for n in first], [by_name[n] for n in first], order)
    mixer_weights = (wg_b, pool_scale, w_pool_out_g, conv_g, conv_b, wa_b, ba2, wx_b, bx2, lru_lambda, w_rnn_out_g)
    (pm, y_pool, hr, z, y_rnn, kept, gates), (w_ffn_lo_g, w_ffn_hi_g) = _mixer_fwd(
        proj, *mixer_weights, **gather_of("w_ffn_in_lo", "w_ffn_in_hi"))
    (mix, x2, h2), _ = _merge_out(xs, proj, y_pool, y_rnn, w_o_g, norm_ffn)
    (gu, act), (w_ffn_out_g,) = _ffn_up(h2, w_ffn_lo_g, w_ffn_hi_g, **gather_of("w_ffn_out"))
    dx3, dx3b, loss_part, dvec_fin = _ffn_down_loss(act, x2, target, w_ffn_out_g, norm_final.reshape(1, D))

    dgu = _ffn_bwd_down(dx3b, gu, w_ffn_out_g)
    dx2, dx2b, dmixo, dvec_ffn = _ffn_bwd_up(dgu, x2, dx3, w_ffn_lo_g, w_ffn_hi_g, norm_ffn, w_o_g)
    names_a = ("w_ffn_in", "w_ffn_out", "w_o")
    g_ffn_out = _wgrad(act, dx3b, "wgrad_ffn_out", 1408, 512)
    g_ffn_in, (own_ffn_out, sums_ffn_out) = _wgrad(
        dgu, h2, "wgrad_ffn_in", 1408, 512, exchange=_PairReduce([by_name["w_ffn_out"]]), exchange_operands=[g_ffn_out])
    g_o, (own_ffn_in, sums_ffn_in) = _wgrad(
        mix, dx2b, "wgrad_o", 1024, 256, exchange=_PairReduce([by_name["w_ffn_in"]]), exchange_operands=[g_ffn_in])
    (own_o,), (sums_o,) = pair_sums(("w_o",), [g_o], "o")
    part_a = [g_ffn_in, g_ffn_out, g_o]
    lz1_a, sums_a = [own_ffn_in, own_ffn_out, own_o], [sums_ffn_in, sums_ffn_out, sums_o]
    (dproj, dypb, dyrb, dmat, dvec), lz2_a = _mixer_bwd(
        proj, dmixo, y_pool, y_rnn, hr, kept, gates, *mixer_weights, dvec_fin, dvec_ffn, loss_part,
        exchange=_Scatter([by_name[n] for n in names_a]), exchange_operands=sums_a)
    names_b = ("w_pool_out", "w_rnn_out")
    part_b = [_wgrad(pm, dypb, "wgrad_pool_out", 512, 256), _wgrad(z, dyrb, "wgrad_rnn_out", 1024, 256)]
    lz1_b, sums_b = pair_sums(names_b, part_b, "mix")
    g_in, exchanged = _wgrad(
        dproj, h1, "wgrad_in", 1152, 1024,
        exchange=_Both(_Scatter([by_name[n] for n in names_b]), _AllReduce([((MAT_ROWS, HD), 0), ((VEC_ROWS, DR), 1)])),
        exchange_operands=sums_b + [dmat, dvec])
    lz2_b, (mat, vec) = exchanged[:2], exchanged[2:]
    (grad_x, dvec_in), (own_in, _, scattered_in) = _in_bwd(
        dproj, xs, dx2, norm_mix, w_in_g, exchange=_ReduceScatter([by_name["w_in"]]), exchange_operands=[g_in])
    lz1_c, lz2_c = [own_in], [scattered_in]
    (vec_in,) = _all_reduce([dvec_in], [((8, D), 1)], "all_reduce_norm_mix")

    grads, delta, new_m, new_v = {}, {}, {}, {}
    for n, g, l1, l2 in zip(names_a + names_b + ("w_in",), part_a + part_b + [g_in], lz1_a + lz1_b + lz1_c,
                            list(lz2_a) + list(lz2_b) + lz2_c):
        T = by_name[n]
        out = _final_sum(T, g, l1, l2, where, held(T, w[n][0]), held(T, m[n][0]), held(T, v[n][0]))
        grads[n], delta[n], new_m[n], new_v[n] = (held(T, a) for a in out)
    rows_of = lambda block, lo, hi, cols: (block, slice(lo, hi), slice(0, cols))
    row_of = lambda block, r, cols=DR: rows_of(block, r, r + 1, cols)
    cuts = dict(
        w_pool_grp=rows_of(0, 0, MAT_WA, HD), w_rg_a=rows_of(0, MAT_WA, MAT_WX, HD), w_rg_x=rows_of(0, MAT_WX, MAT_ROWS, HD),
        pool_scale=row_of(1, VEC_SCALE, DP), conv_b=row_of(1, VEC_CONV_B), b_rg_a=row_of(1, VEC_BA), b_rg_x=row_of(1, VEC_BX),
        lru_lambda=row_of(1, VEC_LAM), conv_w=(1, slice(VEC_CONV_W, VEC_CONV_W + 4), VEC_PIECE),
        norm_final=row_of(1, VEC_NORM_FINAL), norm_ffn=row_of(1, VEC_NORM_FFN), norm_mix=row_of(2, 0))
    names = list(cuts)
    upd = _adam_small(where, [mat, vec, vec_in], [cuts[n] for n in names], [(w[n], m[n], v[n]) for n in names], (1, VEC_LOSS))
    for i, n in enumerate(names):
        grads[n], delta[n], new_m[n], new_v[n] = upd[4 * i:4 * i + 4]
    loss = upd[-1].reshape(())

    shaped = lambda d: [d[n].reshape(w[n].shape) for n in WEIGHT_NAMES]
    return (loss, grad_x[None], *shaped(grads), *shaped(delta), *shaped(new_m), *shaped(new_v))
```
